```python
import math
import jax, jax.numpy as jnp
from jax import lax
import numpy as np

D_MODEL = 1024
BATCH = 8
SEQ = 4096
DEPTH = 1

GRID_W = 64
DILATED_GROUPS = ((128, 1), (512, 4), (2048, 16))
N_GROUPS_A = len(DILATED_GROUPS)
HEADS_PER_GROUP_A = 8
HEAD_DIM_A = 64
N_HEADS_A = N_GROUPS_A * HEADS_PER_GROUP_A
GROUP_WIDTH_A = HEADS_PER_GROUP_A * HEAD_DIM_A
A_QKV_WIDTH = 3 * N_HEADS_A * HEAD_DIM_A
BAND_BLK = 64
N_HEADS_B = 8
N_KV_B = 2
GQA_GROUP_B = N_HEADS_B // N_KV_B
HEAD_DIM_B = 128
B_Q_WIDTH = N_HEADS_B * HEAD_DIM_B
B_KV_WIDTH = N_KV_B * HEAD_DIM_B
ROPE_THETA = 10000.0
Q_BLOCK = 128
N_BRANCHES = 2
GATE_WIDTH = N_BRANCHES * D_MODEL
IN_WIDTH = A_QKV_WIDTH + B_Q_WIDTH + 2 * B_KV_WIDTH + GATE_WIDTH
N_BUCKETS = 32
MAX_DISTANCE = 1024
D_FF = 2816
EPS = 1e-6
NEG_INF = -1e30

kernel_name = 'hybrid_dilated_axial_gqa_macaron'


def rmsnorm(x, g):
    xf = x.astype(jnp.float32)
    y = xf * lax.rsqrt(jnp.mean(xf * xf, axis=-1, keepdims=True) + EPS)
    return (y * g.astype(jnp.float32)).astype(x.dtype)


def swiglu(x, w1, w3, w2):
    return (jax.nn.silu(x @ w1) * (x @ w3)) @ w2


def t5_bucket(rel):
    n = N_BUCKETS // 2
    max_exact = n // 2
    ret = jnp.where(rel > 0, n, 0)
    a = jnp.abs(rel)
    af = jnp.maximum(a, 1).astype(jnp.float32)
    large = max_exact + (jnp.log(af / max_exact) / math.log(MAX_DISTANCE / max_exact)
                         * (n - max_exact)).astype(jnp.int32)
    large = jnp.minimum(large, n - 1)
    return ret + jnp.where(a < max_exact, a, large)


def dilated_group_attention(q, k, v, bias_tab, dilation, half):
    B, S, H, hd = q.shape
    L = S // dilation
    nblk = -(-L // BAND_BLK)
    Lp = nblk * BAND_BLK

    def to_sub(a):
        a = a.reshape(B, L, dilation, H, hd).transpose(0, 2, 1, 3, 4)
        return jnp.pad(a, ((0, 0), (0, 0), (0, Lp - L), (0, 0), (0, 0)))

    def band(a):
        a = jnp.pad(a, ((0, 0), (0, 0), (BAND_BLK, BAND_BLK), (0, 0), (0, 0)))
        blocks = a.reshape(B, dilation, nblk + 2, BAND_BLK, H, hd)
        return jnp.concatenate([blocks[:, :, :-2], blocks[:, :, 1:-1], blocks[:, :, 2:]], axis=3)

    qs = to_sub(q).reshape(B, dilation, nblk, BAND_BLK, H, hd)
    kb = band(to_sub(k))
    vb = band(to_sub(v))

    scores = jnp.einsum('brnqhd,brnkhd->brnhqk', qs, kb,
                        preferred_element_type=jnp.float32) * (hd ** -0.5)
    qi = jnp.arange(BAND_BLK, dtype=jnp.int32)
    ki = jnp.arange(3 * BAND_BLK, dtype=jnp.int32) - BAND_BLK
    rel_steps = ki[None, :] - qi[:, None]
    key_m = jnp.arange(nblk, dtype=jnp.int32)[:, None] * BAND_BLK + ki[None, :]
    valid = ((jnp.abs(rel_steps) <= half)[None]
             & ((key_m >= 0) & (key_m < L))[:, None, :])
    bias = bias_tab[t5_bucket(rel_steps * dilation)].transpose(2, 0, 1)
    scores = scores + bias.astype(jnp.float32)
    scores = jnp.where(valid[None, None, :, None], scores, NEG_INF)
    lse = jax.nn.logsumexp(scores, axis=-1)
    p = jnp.exp(scores - lse[..., None])
    out = jnp.einsum('brnhqk,brnkhd->brnqhd', p.astype(v.dtype), vb)
    out = out.reshape(B, dilation, Lp, H, hd)[:, :, :L]
    out = out.transpose(0, 2, 1, 3, 4).reshape(B, S, H, hd)
    lse = lse.transpose(0, 1, 2, 4, 3).reshape(B, dilation, Lp, H)[:, :, :L]
    lse = lse.transpose(0, 2, 1, 3).reshape(B, S, H)
    return out, lse


def axial_rope_tables(rows):
    row = jnp.repeat(jnp.arange(rows, dtype=jnp.float32), GRID_W)
    col = jnp.tile(jnp.arange(GRID_W, dtype=jnp.float32), rows)
    n_freq = HEAD_DIM_B // 4
    freq = ROPE_THETA ** (-jnp.arange(n_freq, dtype=jnp.float32) / n_freq)
    ang = jnp.concatenate([row[:, None] * freq, col[:, None] * freq], axis=-1)
    return jnp.cos(ang), jnp.sin(ang)


def apply_rope(x, cos, sin):
    xf = x.astype(jnp.float32).reshape(*x.shape[:-1], x.shape[-1] // 2, 2)
    x0, x1 = xf[..., 0], xf[..., 1]
    c = cos[None, :, None, :]
    s = sin[None, :, None, :]
    out = jnp.stack([x0 * c - x1 * s, x0 * s + x1 * c], axis=-1)
    return out.reshape(x.shape).astype(x.dtype)


def gqa_axial_attention(q, k, v, q_norm, k_norm, cos, sin):
    B, S = q.shape[0], q.shape[1]
    q = apply_rope(rmsnorm(q, q_norm), cos, sin)
    k = apply_rope(rmsnorm(k, k_norm), cos, sin)
    scale = HEAD_DIM_B ** -0.5
    qblocks = q.reshape(B, S // Q_BLOCK, Q_BLOCK, N_KV_B, GQA_GROUP_B, HEAD_DIM_B).swapaxes(0, 1)

    def attn_block(qb):
        s = jnp.einsum('bqkgd,bskd->bkgqs', qb, k, preferred_element_type=jnp.float32) * scale
        p = jax.nn.softmax(s, axis=-1)
        return jnp.einsum('bkgqs,bskd->bqkgd', p.astype(v.dtype), v)

    ob = lax.map(attn_block, qblocks)
    return ob.swapaxes(0, 1).reshape(B, S, B_Q_WIDTH)


def hybrid_mixer(h, w_in, b_gate, q_norm, k_norm, rel_bias, w_branch_a, w_branch_b, w_out, cos, sin):
    B, S, D = h.shape
    proj = h @ w_in
    o1 = A_QKV_WIDTH
    o2 = o1 + B_Q_WIDTH
    o3 = o2 + B_KV_WIDTH
    o4 = o3 + B_KV_WIDTH
    pa, pq, pk, pv, pg = proj[..., :o1], proj[..., o1:o2], proj[..., o2:o3], proj[..., o3:o4], proj[..., o4:]

    a = pa.reshape(B, S, 3, N_GROUPS_A, HEADS_PER_GROUP_A, HEAD_DIM_A)
    bias_groups = rel_bias.reshape(N_BUCKETS, N_GROUPS_A, HEADS_PER_GROUP_A)
    outs, lses = [], []
    for g, (window, dil) in enumerate(DILATED_GROUPS):
        o, lse = dilated_group_attention(a[:, :, 0, g], a[:, :, 1, g], a[:, :, 2, g],
                                         bias_groups[:, g], dil, window // (2 * dil))
        outs.append(o)
        lses.append(lse)
    wgt = jax.nn.softmax(jnp.stack(lses, axis=0), axis=0)
    o_a = jnp.sum(wgt[..., None] * jnp.stack(outs, axis=0).astype(jnp.float32), axis=0)
    o_a = o_a.astype(h.dtype).reshape(B, S, GROUP_WIDTH_A)

    o_b = gqa_axial_attention(pq.reshape(B, S, N_HEADS_B, HEAD_DIM_B),
                              pk.reshape(B, S, N_KV_B, HEAD_DIM_B),
                              pv.reshape(B, S, N_KV_B, HEAD_DIM_B),
                              q_norm, k_norm, cos, sin)

    gates = jax.nn.sigmoid((pg + b_gate).reshape(B, S, N_BRANCHES, D))
    merged = gates[:, :, 0] * (o_a @ w_branch_a) + gates[:, :, 1] * (o_b @ w_branch_b)
    return merged @ w_out


def _fwd_setup_inputs(seed: int = 0) -> dict:
    key = jax.random.key(seed)
    ks = jax.random.split(key, 24)
    f32 = jnp.float32

    def w(k, shape, fan_in):
        return jax.random.normal(k, shape, f32) * (fan_in ** -0.5)

    def gain(k, shape):
        return 1.0 + 0.05 * jax.random.normal(k, shape, f32)

    L, D = DEPTH, D_MODEL
    return {
        'x': jax.random.normal(ks[0], (BATCH, SEQ, D), f32),
        'ffn1_norm': gain(ks[1], (L, D)),
        'ffn1_w1': w(ks[2], (L, D, D_FF), D),
        'ffn1_w3': w(ks[3], (L, D, D_FF), D),
        'ffn1_w2': w(ks[4], (L, D_FF, D), D_FF),
        'mix_norm': gain(ks[5], (L, D)),
        'w_in': w(ks[6], (L, D, IN_WIDTH), D),
        'b_gate': 0.02 * jax.random.normal(ks[7], (L, GATE_WIDTH), f32),
        'q_norm': gain(ks[8], (L, HEAD_DIM_B)),
        'k_norm': gain(ks[9], (L, HEAD_DIM_B)),
        'rel_bias': 0.5 * jax.random.normal(ks[10], (N_BUCKETS, N_HEADS_A), f32),
        'w_branch_a': w(ks[11], (L, GROUP_WIDTH_A, D), GROUP_WIDTH_A),
        'w_branch_b': w(ks[12], (L, B_Q_WIDTH, D), B_Q_WIDTH),
        'w_out': w(ks[13], (L, D, D), D),
        'ffn2_norm': gain(ks[14], (L, D)),
        'ffn2_w1': w(ks[15], (L, D, D_FF), D),
        'ffn2_w3': w(ks[16], (L, D, D_FF), D),
        'ffn2_w2': w(ks[17], (L, D_FF, D), D_FF),
        'final_norm': gain(ks[18], (D,)),
    }


def _fwd_reference(x, ffn1_norm, ffn1_w1, ffn1_w3, ffn1_w2, mix_norm, w_in, b_gate, q_norm, k_norm,
              rel_bias, w_branch_a, w_branch_b, w_out, ffn2_norm, ffn2_w1, ffn2_w3, ffn2_w2,
              final_norm):
    S = x.shape[1]
    rows = S // GRID_W
    cos, sin = axial_rope_tables(rows)
    for l in range(DEPTH):
        x = x + 0.5 * swiglu(rmsnorm(x, ffn1_norm[l]), ffn1_w1[l], ffn1_w3[l], ffn1_w2[l])
        h = rmsnorm(x, mix_norm[l])
        x = x + hybrid_mixer(h, w_in[l], b_gate[l], q_norm[l], k_norm[l], rel_bias,
                             w_branch_a[l], w_branch_b[l], w_out[l], cos, sin)
        x = x + 0.5 * swiglu(rmsnorm(x, ffn2_norm[l]), ffn2_w1[l], ffn2_w3[l], ffn2_w2[l])
    return rmsnorm(x, final_norm)


import jax as _jax
import jax.numpy as _jnp

TWIN_FORMAT = 'train_step'
FWD_PARAMS = ['x', 'ffn1_norm', 'ffn1_w1', 'ffn1_w3', 'ffn1_w2', 'mix_norm', 'w_in', 'b_gate', 'q_norm', 'k_norm', 'rel_bias', 'w_branch_a', 'w_branch_b', 'w_out', 'ffn2_norm', 'ffn2_w1', 'ffn2_w3', 'ffn2_w2', 'final_norm']
TWIN_WEIGHTS = ['ffn1_norm', 'ffn1_w1', 'ffn1_w3', 'ffn1_w2', 'mix_norm', 'w_in', 'b_gate', 'q_norm', 'k_norm', 'rel_bias', 'w_branch_a', 'w_branch_b', 'w_out', 'ffn2_norm', 'ffn2_w1', 'ffn2_w3', 'ffn2_w2', 'final_norm']
TWIN_DIFF_INPUT = 'x'
TWIN_INPUTS = ['x', 'ffn1_norm', 'ffn1_w1', 'ffn1_w3', 'ffn1_w2', 'mix_norm', 'w_in', 'b_gate', 'q_norm', 'k_norm', 'rel_bias', 'w_branch_a', 'w_branch_b', 'w_out', 'ffn2_norm', 'ffn2_w1', 'ffn2_w3', 'ffn2_w2', 'final_norm', 'loss_target', 'm_ffn1_norm', 'm_ffn1_w1', 'm_ffn1_w3', 'm_ffn1_w2', 'm_mix_norm', 'm_w_in', 'm_b_gate', 'm_q_norm', 'm_k_norm', 'm_rel_bias', 'm_w_branch_a', 'm_w_branch_b', 'm_w_out', 'm_ffn2_norm', 'm_ffn2_w1', 'm_ffn2_w3', 'm_ffn2_w2', 'm_final_norm', 'v_ffn1_norm', 'v_ffn1_w1', 'v_ffn1_w3', 'v_ffn1_w2', 'v_mix_norm', 'v_w_in', 'v_b_gate', 'v_q_norm', 'v_k_norm', 'v_rel_bias', 'v_w_branch_a', 'v_w_branch_b', 'v_w_out', 'v_ffn2_norm', 'v_ffn2_w1', 'v_ffn2_w3', 'v_ffn2_w2', 'v_final_norm']
TWIN_OUTPUTS = ['loss', 'grad_x', 'grad_ffn1_norm', 'grad_ffn1_w1', 'grad_ffn1_w3', 'grad_ffn1_w2', 'grad_mix_norm', 'grad_w_in', 'grad_b_gate', 'grad_q_norm', 'grad_k_norm', 'grad_rel_bias', 'grad_w_branch_a', 'grad_w_branch_b', 'grad_w_out', 'grad_ffn2_norm', 'grad_ffn2_w1', 'grad_ffn2_w3', 'grad_ffn2_w2', 'grad_final_norm', 'delta_ffn1_norm', 'delta_ffn1_w1', 'delta_ffn1_w3', 'delta_ffn1_w2', 'delta_mix_norm', 'delta_w_in', 'delta_b_gate', 'delta_q_norm', 'delta_k_norm', 'delta_rel_bias', 'delta_w_branch_a', 'delta_w_branch_b', 'delta_w_out', 'delta_ffn2_norm', 'delta_ffn2_w1', 'delta_ffn2_w3', 'delta_ffn2_w2', 'delta_final_norm', 'new_m_ffn1_norm', 'new_m_ffn1_w1', 'new_m_ffn1_w3', 'new_m_ffn1_w2', 'new_m_mix_norm', 'new_m_w_in', 'new_m_b_gate', 'new_m_q_norm', 'new_m_k_norm', 'new_m_rel_bias', 'new_m_w_branch_a', 'new_m_w_branch_b', 'new_m_w_out', 'new_m_ffn2_norm', 'new_m_ffn2_w1', 'new_m_ffn2_w3', 'new_m_ffn2_w2', 'new_m_final_norm', 'new_v_ffn1_norm', 'new_v_ffn1_w1', 'new_v_ffn1_w3', 'new_v_ffn1_w2', 'new_v_mix_norm', 'new_v_w_in', 'new_v_b_gate', 'new_v_q_norm', 'new_v_k_norm', 'new_v_rel_bias', 'new_v_w_branch_a', 'new_v_w_branch_b', 'new_v_w_out', 'new_v_ffn2_norm', 'new_v_ffn2_w1', 'new_v_ffn2_w3', 'new_v_ffn2_w2', 'new_v_final_norm']
TWIN_LEAF_KINDS = {'loss': 'loss', 'grad_x': 'grad_x', 'grad_ffn1_norm': 'grad_w', 'grad_ffn1_w1': 'grad_w', 'grad_ffn1_w3': 'grad_w', 'grad_ffn1_w2': 'grad_w', 'grad_mix_norm': 'grad_w', 'grad_w_in': 'grad_w', 'grad_b_gate': 'grad_w', 'grad_q_norm': 'grad_w', 'grad_k_norm': 'grad_w', 'grad_rel_bias': 'grad_w', 'grad_w_branch_a': 'grad_w', 'grad_w_branch_b': 'grad_w', 'grad_w_out': 'grad_w', 'grad_ffn2_norm': 'grad_w', 'grad_ffn2_w1': 'grad_w', 'grad_ffn2_w3': 'grad_w', 'grad_ffn2_w2': 'grad_w', 'grad_final_norm': 'grad_w', 'delta_ffn1_norm': 'delta_w', 'delta_ffn1_w1': 'delta_w', 'delta_ffn1_w3': 'delta_w', 'delta_ffn1_w2': 'delta_w', 'delta_mix_norm': 'delta_w', 'delta_w_in': 'delta_w', 'delta_b_gate': 'delta_w', 'delta_q_norm': 'delta_w', 'delta_k_norm': 'delta_w', 'delta_rel_bias': 'delta_w', 'delta_w_branch_a': 'delta_w', 'delta_w_branch_b': 'delta_w', 'delta_w_out': 'delta_w', 'delta_ffn2_norm': 'delta_w', 'delta_ffn2_w1': 'delta_w', 'delta_ffn2_w3': 'delta_w', 'delta_ffn2_w2': 'delta_w', 'delta_final_norm': 'delta_w', 'new_m_ffn1_norm': 'new_m', 'new_m_ffn1_w1': 'new_m', 'new_m_ffn1_w3': 'new_m', 'new_m_ffn1_w2': 'new_m', 'new_m_mix_norm': 'new_m', 'new_m_w_in': 'new_m', 'new_m_b_gate': 'new_m', 'new_m_q_norm': 'new_m', 'new_m_k_norm': 'new_m', 'new_m_rel_bias': 'new_m', 'new_m_w_branch_a': 'new_m', 'new_m_w_branch_b': 'new_m', 'new_m_w_out': 'new_m', 'new_m_ffn2_norm': 'new_m', 'new_m_ffn2_w1': 'new_m', 'new_m_ffn2_w3': 'new_m', 'new_m_ffn2_w2': 'new_m', 'new_m_final_norm': 'new_m', 'new_v_ffn1_norm': 'new_v', 'new_v_ffn1_w1': 'new_v', 'new_v_ffn1_w3': 'new_v', 'new_v_ffn1_w2': 'new_v', 'new_v_mix_norm': 'new_v', 'new_v_w_in': 'new_v', 'new_v_b_gate': 'new_v', 'new_v_q_norm': 'new_v', 'new_v_k_norm': 'new_v', 'new_v_rel_bias': 'new_v', 'new_v_w_branch_a': 'new_v', 'new_v_w_branch_b': 'new_v', 'new_v_w_out': 'new_v', 'new_v_ffn2_norm': 'new_v', 'new_v_ffn2_w1': 'new_v', 'new_v_ffn2_w3': 'new_v', 'new_v_ffn2_w2': 'new_v', 'new_v_final_norm': 'new_v'}


def _forward(args):
    return _fwd_reference(*[args[k] for k in FWD_PARAMS])


def _output_shape():
    out = _jax.eval_shape(lambda: _forward(_fwd_setup_inputs(0)))
    return out.shape, out.dtype

N_MICROBATCH = 1
ADAM_LR = 0.001
ADAM_B1 = 0.9
ADAM_B2 = 0.999
ADAM_EPS = 1e-08
ADAM_WD = 0.01
ADAM_STEP = 10
PER_EXAMPLE_BATCH_AXIS = {'x': 0, 'loss_target': 0}
SHARED_INPUTS = []
_WEIGHT_DTYPES = {'ffn1_norm': _jnp.float32, 'ffn1_w1': _jnp.float32, 'ffn1_w3': _jnp.float32, 'ffn1_w2': _jnp.float32, 'mix_norm': _jnp.float32, 'w_in': _jnp.float32, 'b_gate': _jnp.float32, 'q_norm': _jnp.float32, 'k_norm': _jnp.float32, 'rel_bias': _jnp.float32, 'w_branch_a': _jnp.float32, 'w_branch_b': _jnp.float32, 'w_out': _jnp.float32, 'ffn2_norm': _jnp.float32, 'ffn2_w1': _jnp.float32, 'ffn2_w3': _jnp.float32, 'ffn2_w2': _jnp.float32, 'final_norm': _jnp.float32}
MOMENT_SCALE = {'ffn1_norm': 9.038968e-02, 'ffn1_w1': 3.326683e-02, 'ffn1_w3': 3.217465e-02, 'ffn1_w2': 5.339055e-02, 'mix_norm': 2.853455e-02, 'w_in': 1.005427e-02, 'b_gate': 4.356280e-03, 'q_norm': 2.867726e-02, 'k_norm': 3.034029e-02, 'rel_bias': 1.411976e-02, 'w_branch_a': 1.287859e-02, 'w_branch_b': 9.803314e-03, 'w_out': 1.624557e-02, 'ffn2_norm': 7.033584e-02, 'ffn2_w1': 3.058521e-02, 'ffn2_w3': 2.976044e-02, 'ffn2_w2': 4.936715e-02, 'final_norm': 3.201978e+01}


def _to_microbatches(a, axis):
    t = _jnp.moveaxis(a, axis, 0)
    t = t.reshape((N_MICROBATCH, t.shape[0] // N_MICROBATCH) + t.shape[1:])
    return _jnp.moveaxis(t, 1, axis + 1)


def setup_inputs(seed: int = 0) -> dict:
    inp = _fwd_setup_inputs(seed)
    key = _jax.random.fold_in(_jax.random.key(seed), 7919)
    shape, _ = _output_shape()
    out = dict(inp)
    out["loss_target"] = _jax.random.normal(_jax.random.fold_in(key, 0), shape, _jnp.float32)
    for i, name in enumerate(TWIN_WEIGHTS):
        w = inp[name].astype(_jnp.float32)
        if MOMENT_SCALE is None:
            s = _jnp.sqrt(_jnp.mean(_jnp.square(w)) + 1e-30)
        else:
            s = MOMENT_SCALE[name]
        km, kv = _jax.random.split(_jax.random.fold_in(key, i + 1))
        out[name] = w
        out["m_" + name] = s * _jax.random.normal(km, w.shape, _jnp.float32)
        out["v_" + name] = (s * s) * _jax.random.uniform(kv, w.shape, _jnp.float32, 0.5, 1.5)
    if N_MICROBATCH > 1:
        for name, axis in PER_EXAMPLE_BATCH_AXIS.items():
            out[name] = _to_microbatches(out[name], axis)
    return {'x': out['x'], 'ffn1_norm': out['ffn1_norm'], 'ffn1_w1': out['ffn1_w1'], 'ffn1_w3': out['ffn1_w3'], 'ffn1_w2': out['ffn1_w2'], 'mix_norm': out['mix_norm'], 'w_in': out['w_in'], 'b_gate': out['b_gate'], 'q_norm': out['q_norm'], 'k_norm': out['k_norm'], 'rel_bias': out['rel_bias'], 'w_branch_a': out['w_branch_a'], 'w_branch_b': out['w_branch_b'], 'w_out': out['w_out'], 'ffn2_norm': out['ffn2_norm'], 'ffn2_w1': out['ffn2_w1'], 'ffn2_w3': out['ffn2_w3'], 'ffn2_w2': out['ffn2_w2'], 'final_norm': out['final_norm'], 'loss_target': out['loss_target'], 'm_ffn1_norm': out['m_ffn1_norm'], 'm_ffn1_w1': out['m_ffn1_w1'], 'm_ffn1_w3': out['m_ffn1_w3'], 'm_ffn1_w2': out['m_ffn1_w2'], 'm_mix_norm': out['m_mix_norm'], 'm_w_in': out['m_w_in'], 'm_b_gate': out['m_b_gate'], 'm_q_norm': out['m_q_norm'], 'm_k_norm': out['m_k_norm'], 'm_rel_bias': out['m_rel_bias'], 'm_w_branch_a': out['m_w_branch_a'], 'm_w_branch_b': out['m_w_branch_b'], 'm_w_out': out['m_w_out'], 'm_ffn2_norm': out['m_ffn2_norm'], 'm_ffn2_w1': out['m_ffn2_w1'], 'm_ffn2_w3': out['m_ffn2_w3'], 'm_ffn2_w2': out['m_ffn2_w2'], 'm_final_norm': out['m_final_norm'], 'v_ffn1_norm': out['v_ffn1_norm'], 'v_ffn1_w1': out['v_ffn1_w1'], 'v_ffn1_w3': out['v_ffn1_w3'], 'v_ffn1_w2': out['v_ffn1_w2'], 'v_mix_norm': out['v_mix_norm'], 'v_w_in': out['v_w_in'], 'v_b_gate': out['v_b_gate'], 'v_q_norm': out['v_q_norm'], 'v_k_norm': out['v_k_norm'], 'v_rel_bias': out['v_rel_bias'], 'v_w_branch_a': out['v_w_branch_a'], 'v_w_branch_b': out['v_w_branch_b'], 'v_w_out': out['v_w_out'], 'v_ffn2_norm': out['v_ffn2_norm'], 'v_ffn2_w1': out['v_ffn2_w1'], 'v_ffn2_w3': out['v_ffn2_w3'], 'v_ffn2_w2': out['v_ffn2_w2'], 'v_final_norm': out['v_final_norm']}


def _loss(weights, diff, rest, loss_target):
    with _jax.named_scope("forward"):
        args = {**rest, TWIN_DIFF_INPUT: diff, **{k: w.astype(_WEIGHT_DTYPES[k]) for k, w in weights.items()}}
        y = _forward(args)
    with _jax.named_scope("loss_head"):
        err = _jnp.square(y.astype(_jnp.float32) - loss_target)
        return 0.5 * _jnp.sum(_jnp.mean(err, axis=-1)) if err.ndim else 0.5 * err


def _adamw(w, g, m, v):
    m = ADAM_B1 * m + (1.0 - ADAM_B1) * g
    v = ADAM_B2 * v + (1.0 - ADAM_B2) * _jnp.square(g)
    m_hat = m / (1.0 - ADAM_B1 ** ADAM_STEP)
    v_hat = v / (1.0 - ADAM_B2 ** ADAM_STEP)
    delta = -ADAM_LR * (m_hat / (_jnp.sqrt(v_hat) + ADAM_EPS) + ADAM_WD * w)
    return delta, m, v


def reference(x, ffn1_norm, ffn1_w1, ffn1_w3, ffn1_w2, mix_norm, w_in, b_gate, q_norm, k_norm, rel_bias, w_branch_a, w_branch_b, w_out, ffn2_norm, ffn2_w1, ffn2_w3, ffn2_w2, final_norm, loss_target, m_ffn1_norm, m_ffn1_w1, m_ffn1_w3, m_ffn1_w2, m_mix_norm, m_w_in, m_b_gate, m_q_norm, m_k_norm, m_rel_bias, m_w_branch_a, m_w_branch_b, m_w_out, m_ffn2_norm, m_ffn2_w1, m_ffn2_w3, m_ffn2_w2, m_final_norm, v_ffn1_norm, v_ffn1_w1, v_ffn1_w3, v_ffn1_w2, v_mix_norm, v_w_in, v_b_gate, v_q_norm, v_k_norm, v_rel_bias, v_w_branch_a, v_w_branch_b, v_w_out, v_ffn2_norm, v_ffn2_w1, v_ffn2_w3, v_ffn2_w2, v_final_norm):
    given = dict(x=x, ffn1_norm=ffn1_norm, ffn1_w1=ffn1_w1, ffn1_w3=ffn1_w3, ffn1_w2=ffn1_w2, mix_norm=mix_norm, w_in=w_in, b_gate=b_gate, q_norm=q_norm, k_norm=k_norm, rel_bias=rel_bias, w_branch_a=w_branch_a, w_branch_b=w_branch_b, w_out=w_out, ffn2_norm=ffn2_norm, ffn2_w1=ffn2_w1, ffn2_w3=ffn2_w3, ffn2_w2=ffn2_w2, final_norm=final_norm, loss_target=loss_target, m_ffn1_norm=m_ffn1_norm, m_ffn1_w1=m_ffn1_w1, m_ffn1_w3=m_ffn1_w3, m_ffn1_w2=m_ffn1_w2, m_mix_norm=m_mix_norm, m_w_in=m_w_in, m_b_gate=m_b_gate, m_q_norm=m_q_norm, m_k_norm=m_k_norm, m_rel_bias=m_rel_bias, m_w_branch_a=m_w_branch_a, m_w_branch_b=m_w_branch_b, m_w_out=m_w_out, m_ffn2_norm=m_ffn2_norm, m_ffn2_w1=m_ffn2_w1, m_ffn2_w3=m_ffn2_w3, m_ffn2_w2=m_ffn2_w2, m_final_norm=m_final_norm, v_ffn1_norm=v_ffn1_norm, v_ffn1_w1=v_ffn1_w1, v_ffn1_w3=v_ffn1_w3, v_ffn1_w2=v_ffn1_w2, v_mix_norm=v_mix_norm, v_w_in=v_w_in, v_b_gate=v_b_gate, v_q_norm=v_q_norm, v_k_norm=v_k_norm, v_rel_bias=v_rel_bias, v_w_branch_a=v_w_branch_a, v_w_branch_b=v_w_branch_b, v_w_out=v_w_out, v_ffn2_norm=v_ffn2_norm, v_ffn2_w1=v_ffn2_w1, v_ffn2_w3=v_ffn2_w3, v_ffn2_w2=v_ffn2_w2, v_final_norm=v_final_norm)
    weights = {n: given[n] for n in TWIN_WEIGHTS}
    shared = {n: given[n] for n in SHARED_INPUTS}
    per_example = {n: given[n] for n in ['x']}
    grad_fn = _jax.value_and_grad(_loss, argnums=(0, 1))

    def one_microbatch(ex, loss_target):
        ex = dict(ex)
        diff = ex.pop(TWIN_DIFF_INPUT)
        return grad_fn(weights, diff, {**shared, **ex}, loss_target)

    if N_MICROBATCH == 1:
        loss, (grad_w, grad_x) = one_microbatch(per_example, given["loss_target"])
    else:
        def body(carry, xs):
            loss_sum, grad_sum = carry
            l_k, (gw_k, gx_k) = one_microbatch(xs[0], xs[1])
            with _jax.named_scope("update"):
                return (loss_sum + l_k, _jax.tree.map(_jnp.add, grad_sum, gw_k)), gx_k

        init = (_jnp.zeros((), _jnp.float32), _jax.tree.map(_jnp.zeros_like, weights))
        (loss, grad_w), grad_x = _jax.lax.scan(body, init, (per_example, given["loss_target"]))
    with _jax.named_scope("update"):
        delta_w, new_m, new_v = {}, {}, {}
        for n in TWIN_WEIGHTS:
            delta_w[n], new_m[n], new_v[n] = _adamw(weights[n], grad_w[n], given["m_" + n], given["v_" + n])
    return (loss, grad_x, *[grad_w[n] for n in TWIN_WEIGHTS], *[delta_w[n] for n in TWIN_WEIGHTS],
            *[new_m[n] for n in TWIN_WEIGHTS], *[new_v[n] for n in TWIN_WEIGHTS])
```

```python
import functools
import math

import numpy as np
import jax
import jax.numpy as jnp
from jax import lax
from jax.experimental import pallas as pl
from jax.experimental.pallas import tpu as pltpu

F32 = jnp.float32
BF16 = jnp.bfloat16
MESH = pl.DeviceIdType.MESH

NEG_INF = -1e30
EPS = 1e-6
GRID_W = 64
ROPE_THETA = 10000.0
DILATIONS = (1, 4, 16)
BAND_HALF = 64
HEAD_A = 64
HEADS_A = 8
WIDTH_A = HEADS_A * HEAD_A
HEAD_B = 128
N_BUCKETS = 32
MAX_DISTANCE = 1024
ADAM_LR, ADAM_B1, ADAM_B2, ADAM_EPS, ADAM_WD, ADAM_STEP = 0.001, 0.9, 0.999, 1e-08, 0.01, 10

A_Q, A_K, A_V = 0, 1536, 3072
B_Q, B_K, B_V = 4608, 5632, 5888
G_A, G_B = 6144, 7168
IN_WIDTH = 8192

VMEM_LIMIT_BYTES = 56 * 1024 * 1024
QB_A = 128
QB_B = 256


def _params(*sem):
    return pltpu.CompilerParams(dimension_semantics=sem, vmem_limit_bytes=VMEM_LIMIT_BYTES)


def _bs(shape, fn):
    return pl.BlockSpec(shape, fn)


def _mm(name, grid, pairs, out_shape, out_spec, dims, *, reduce_axis=None, extras=(), epilogue=None):
    n_pairs, n_extra = len(pairs), len(extras)
    operands = [p[0] for p in pairs] + [p[2] for p in pairs] + [e[0] for e in extras]
    in_specs = [p[1] for p in pairs] + [p[3] for p in pairs] + [e[1] for e in extras]
    tile = tuple(s for s in out_spec.block_shape if s is not None)
    n_steps = grid[reduce_axis] if reduce_axis is not None else 1

    def body(*refs):
        a_refs, b_refs = refs[:n_pairs], refs[n_pairs:2 * n_pairs]
        e_refs = refs[2 * n_pairs:2 * n_pairs + n_extra]
        o_ref = refs[2 * n_pairs + n_extra]
        acc = None
        for a_ref, b_ref in zip(a_refs, b_refs):
            t = lax.dot_general(a_ref[...], b_ref[...], (dims, ((), ())), preferred_element_type=F32)
            acc = t if acc is None else acc + t

        def finish(v):
            if epilogue is not None:
                v = epilogue(v, *[e[...] for e in e_refs])
            o_ref[...] = v.astype(o_ref.dtype)

        if reduce_axis is None:
            finish(acc)
        else:
            acc_ref = refs[-1]
            k = pl.program_id(reduce_axis)

            @pl.when(k == 0)
            def _():
                acc_ref[...] = acc

            @pl.when(k > 0)
            def _():
                acc_ref[...] += acc

            @pl.when(k == n_steps - 1)
            def _():
                finish(acc_ref[...])

    sem = ["parallel"] * len(grid)
    if reduce_axis is not None:
        sem[reduce_axis] = "arbitrary"
    return pl.pallas_call(
        body, out_shape=out_shape, grid=grid, in_specs=in_specs, out_specs=out_spec,
        scratch_shapes=[pltpu.VMEM(tile, F32)] if reduce_axis is not None else [],
        compiler_params=_params(*sem), name=name)(*operands)


NN = ((1,), (0,))
NT = ((1,), (1,))
TN = ((0,), (0,))


def _mm_cols(name, a, w, *, tm, tn, out_dtype, cat, extras=(), epilogue=None):
    M, K = a.shape
    J, _, n = w.shape
    tn = min(tn, n)
    nb = n // tn
    if cat:
        shape, spec = (M, J * n), _bs((tm, tn), lambda j, i, k: (i, j * nb + k))
    else:
        shape, spec = (J, M, n), _bs((None, tm, tn), lambda j, i, k: (j, i, k))
    ex = [(e, _bs((tm, tn), lambda j, i, k: (i, j * nb + k))) for e in extras]
    return _mm(name, (J, M // tm, nb),
               [(a, _bs((tm, K), lambda j, i, k: (i, 0)), w, _bs((None, K, tn), lambda j, i, k: (j, 0, k)))],
               jax.ShapeDtypeStruct(shape, out_dtype), spec, NN, extras=ex, epilogue=epilogue)


def _mm_rows_t(name, a, w, *, tm, out_dtype):
    M, N = a.shape
    J, f, _ = w.shape
    return _mm(name, (J, M // tm),
               [(a, _bs((tm, N), lambda j, i: (i, 0)), w, _bs((None, f, N), lambda j, i: (j, 0, 0)))],
               jax.ShapeDtypeStruct((J, M, f), out_dtype), _bs((None, tm, f), lambda j, i: (j, i, 0)), NT)


def _mm_wgrad(name, a, b, *, a_cols, b_cols, tm, tn, J):
    def pick(arr, cols, t):
        if arr.ndim == 3:
            T, c = arr.shape[1], arr.shape[2]
            t = min(t, c)
            return T, c, t, (lambda sel: _bs((None, T, t), lambda j, i, k: (j, 0, sel(i, k))))
        T = arr.shape[0]
        c = arr.shape[1] if cols is None else cols
        t = min(t, c)
        per = c // t
        if cols is None:
            return T, c, t, (lambda sel: _bs((T, t), lambda j, i, k: (0, sel(i, k))))
        return T, c, t, (lambda sel: _bs((T, t), lambda j, i, k: (0, j * per + sel(i, k))))
    _, ca, tm, mk_a = pick(a, a_cols, tm)
    _, cb, tn, mk_b = pick(b, b_cols, tn)
    return _mm(name, (J, ca // tm, cb // tn),
               [(a, mk_a(lambda i, k: i), b, mk_b(lambda i, k: k))],
               jax.ShapeDtypeStruct((J, ca, cb), BF16), _bs((None, tm, tn), lambda j, i, k: (j, i, k)), TN)


def _tiled(arr, width=None, col=0, rowblk=0):
    return ("t", arr, arr.shape[1] if width is None else width, col, rowblk)


def _table(arr):
    return ("f", arr)


def _whole(arr):
    return ("w", arr)


def _ew(name, fn, ins, outs, *, n_rows, rows, reds=(), ncols=1):
    nrb = n_rows // rows
    operands, in_specs = [], []
    for spec in ins:
        if spec[0] == "t":
            _, arr, width, col, rowblk = spec
            step = 1 if ncols > 1 else 0
            in_specs.append(_bs((rows, width), lambda c, i, col=col, rowblk=rowblk, step=step: (rowblk + i, col + c * step)))
        elif spec[0] == "f":
            arr = spec[1]
            in_specs.append(_bs((rows, arr.shape[1]), lambda c, i: (i, 0)))
        else:
            arr = spec[1]
            nd = arr.ndim
            if nd == 3:
                in_specs.append(_bs((None,) + arr.shape[1:], lambda c, i: (c, 0, 0)))
            else:
                in_specs.append(_bs(arr.shape, lambda c, i, nd=nd: (0,) * nd))
        operands.append(arr)
    out_shapes = [jax.ShapeDtypeStruct((n_rows, ncols * w), dt) for dt, w in outs]
    out_specs = [_bs((rows, w), lambda c, i: (i, c)) for _, w in outs]
    out_shapes += [jax.ShapeDtypeStruct((ncols, 1, w), F32) for w in reds]
    out_specs += [_bs((None, 1, w), lambda c, i: (c, 0, 0)) for w in reds]
    n_in, n_out, n_red = len(ins), len(outs), len(reds)

    def body(*refs):
        vals = fn(*[r[...] for r in refs[:n_in]])
        if not isinstance(vals, (tuple, list)):
            vals = (vals,)
        for o_ref, v in zip(refs[n_in:n_in + n_out], vals[:n_out]):
            o_ref[...] = v.astype(o_ref.dtype)
        if n_red:
            i = pl.program_id(1)
            for r_ref, v in zip(refs[n_in + n_out:], vals[n_out:]):
                @pl.when(i == 0)
                def _(r_ref=r_ref):
                    r_ref[...] = jnp.zeros_like(r_ref)
                r_ref[...] += v

    res = pl.pallas_call(
        body, out_shape=out_shapes, grid=(ncols, nrb), in_specs=in_specs, out_specs=out_specs,
        compiler_params=_params("parallel", "arbitrary" if n_red else "parallel"), name=name)(*operands)
    return res


def _colsum(v):
    return jnp.sum(v, axis=0, keepdims=True)


def _rstd(x):
    return lax.rsqrt(jnp.mean(x * x, axis=-1, keepdims=True) + EPS)


def _sigmoid(x):
    return 1.0 / (1.0 + jnp.exp(-x))


def _norm_fwd(x, g):
    return x * _rstd(x) * g


def _norm_bwd(x, g, dy):
    r = _rstd(x)
    xh = x * r
    dxh = dy * g
    dx = r * (dxh - xh * jnp.mean(dxh * xh, axis=-1, keepdims=True))
    return dx, dy * xh


def _ffn_fwd(tag, x, gain, w1, w3, w2):
    T, D = x.shape
    J, _, f = w1.shape
    (h,) = _ew(f"{tag}_norm", lambda xv, g: _norm_fwd(xv, g), [_tiled(x), _whole(gain)], [(BF16, D)], n_rows=T, rows=512)
    u = _mm_cols(f"{tag}_up1", h, w1, tm=512, tn=f, out_dtype=BF16, cat=False)
    g = _mm_cols(f"{tag}_up3", h, w3, tm=512, tn=f, out_dtype=BF16, cat=False)

    def act(uv, gv):
        uv, gv = uv.astype(F32), gv.astype(F32)
        return uv * _sigmoid(uv) * gv

    (a,) = _ew(f"{tag}_act", act, [_tiled(u.reshape(J * T, f)), _tiled(g.reshape(J * T, f))], [(BF16, f)],
               n_rows=J * T, rows=512)
    a = a.reshape(J, T, f)
    y = _mm(f"{tag}_down", (T // 512, D // 512, J),
            [(a, _bs((None, 512, f), lambda i, k, j: (j, i, 0)), w2, _bs((None, f, 512), lambda i, k, j: (j, 0, k)))],
            jax.ShapeDtypeStruct((T, D), F32), _bs((512, 512), lambda i, k, j: (i, k)), NN, reduce_axis=2,
            extras=[(x, _bs((512, 512), lambda i, k, j: (i, k)))], epilogue=lambda acc, xv: xv + 0.5 * acc)
    return y, (h, u, g, a)


def _ffn_bwd(tag, x, gain, w1, w3, w2, saved, dy, dy_half):
    h, u, g, a = saved
    T, D = x.shape
    J, _, f = w1.shape
    da = _mm_rows_t(f"{tag}_bwd_da", dy_half, w2, tm=512, out_dtype=BF16)
    dw2 = _mm_wgrad(f"{tag}_bwd_dw2", a, dy_half, a_cols=None, b_cols=None, tm=f, tn=512, J=J)

    def act_bwd(uv, gv, dav):
        uv, gv, dav = uv.astype(F32), gv.astype(F32), dav.astype(F32)
        s = _sigmoid(uv)
        return dav * gv * (s * (1.0 + uv * (1.0 - s))), dav * (uv * s)

    du, dg = _ew(f"{tag}_bwd_act", act_bwd,
                 [_tiled(u.reshape(J * T, f)), _tiled(g.reshape(J * T, f)), _tiled(da.reshape(J * T, f))],
                 [(BF16, f), (BF16, f)], n_rows=J * T, rows=512)
    du, dg = du.reshape(J, T, f), dg.reshape(J, T, f)
    dw1 = _mm_wgrad(f"{tag}_bwd_dw1", h, du, a_cols=None, b_cols=None, tm=512, tn=f, J=J)
    dw3 = _mm_wgrad(f"{tag}_bwd_dw3", h, dg, a_cols=None, b_cols=None, tm=512, tn=f, J=J)
    a_spec = _bs((None, 512, f), lambda i, k, j: (j, i, 0))
    w_spec = _bs((None, 512, f), lambda i, k, j: (j, k, 0))
    dh = _mm(f"{tag}_bwd_dh", (T // 512, D // 512, J), [(du, a_spec, w1, w_spec), (dg, a_spec, w3, w_spec)],
             jax.ShapeDtypeStruct((T, D), F32), _bs((512, 512), lambda i, k, j: (i, k)), NT, reduce_axis=2)

    def nb(xv, gv, dhv, dres):
        dx, dgr = _norm_bwd(xv, gv, dhv)
        dx = dx + dres
        return dx, 0.5 * dx, _colsum(dgr)

    dx, dx_half, dgain = _ew(f"{tag}_bwd_norm", nb, [_tiled(x), _whole(gain), _tiled(dh), _tiled(dy)],
                             [(F32, D), (BF16, D)], n_rows=T, rows=256, reds=(D,))
    return dx, dx_half, dgain.reshape(1, D), dw1, dw3, dw2


def _t5_bucket(rel):
    n = N_BUCKETS // 2
    max_exact = n // 2
    ret = jnp.where(rel > 0, n, 0)
    a = jnp.abs(rel)
    af = jnp.maximum(a, 1).astype(F32)
    large = max_exact + (jnp.log(af / max_exact) / math.log(MAX_DISTANCE / max_exact) * (n - max_exact)).astype(jnp.int32)
    large = jnp.minimum(large, n - 1)
    return ret + jnp.where(a < max_exact, a, large)


def _band_steps():
    qi = jnp.arange(QB_A, dtype=jnp.int32)[:, None]
    kj = jnp.arange(3 * QB_A, dtype=jnp.int32)[None, :] - QB_A
    return kj - qi


def _bias_tiles(rel_bias):
    steps = _band_steps()
    buckets = jnp.stack([_t5_bucket(steps * d) for d in DILATIONS])
    inband = (jnp.abs(steps) <= BAND_HALF).astype(jnp.int32)
    n_heads = rel_bias.shape[1]

    def body(tab_ref, b_ref, m_ref, o_ref):
        hd = pl.program_id(0)
        bkt = b_ref[...]
        acc = jnp.zeros(bkt.shape, F32)
        for b in range(N_BUCKETS):
            acc = jnp.where(bkt == b, tab_ref[b, hd], acc)
        o_ref[...] = jnp.where(m_ref[...] > 0, acc, NEG_INF)

    return pl.pallas_call(
        body, out_shape=jax.ShapeDtypeStruct((n_heads, QB_A, 3 * QB_A), F32), grid=(n_heads,),
        in_specs=[pl.BlockSpec(memory_space=pltpu.SMEM),
                  _bs((None, QB_A, 3 * QB_A), lambda hd: (hd // HEADS_A, 0, 0)),
                  _bs((QB_A, 3 * QB_A), lambda hd: (0, 0))],
        out_specs=_bs((None, QB_A, 3 * QB_A), lambda hd: (hd, 0, 0)),
        compiler_params=_params("parallel"), name="a_bias_tiles")(rel_bias, buckets, inband)


def _bias_grad(dbias):
    steps = np.arange(3 * QB_A)[None, :] - QB_A - np.arange(QB_A)[:, None]
    inband = np.abs(steps) <= BAND_HALF
    present = []
    for d in DILATIONS:
        rel = steps * d
        a = np.abs(rel)
        large = 8 + (np.log(np.maximum(a, 1) / 8.0) / math.log(MAX_DISTANCE / 8.0) * 8).astype(np.int64)
        bk = np.where(rel > 0, 16, 0) + np.where(a < 8, a, np.minimum(large, 15))
        present.append(sorted(set(bk[inband].tolist())))
    buckets = jnp.stack([_t5_bucket(_band_steps() * d) for d in DILATIONS])
    n_heads = dbias.shape[0]

    def body(b_ref, d_ref, o_ref):
        row = lax.broadcasted_iota(jnp.int32, (N_BUCKETS, n_heads), 0)
        col = lax.broadcasted_iota(jnp.int32, (N_BUCKETS, n_heads), 1)
        out = jnp.zeros((N_BUCKETS, n_heads), F32)
        for grp in range(len(DILATIONS)):
            bkt = b_ref[grp]
            for hh in range(HEADS_A):
                hd = grp * HEADS_A + hh
                ds = d_ref[hd]
                for b in present[grp]:
                    tot = jnp.sum(jnp.where(bkt == b, ds, 0.0))
                    out = jnp.where((row == b) & (col == hd), tot, out)
        o_ref[...] = out

    return pl.pallas_call(
        body, out_shape=jax.ShapeDtypeStruct((N_BUCKETS, n_heads), F32),
        compiler_params=pltpu.CompilerParams(vmem_limit_bytes=VMEM_LIMIT_BYTES), name="a_bias_grad")(buckets, dbias)


def _lane_is_second_head(shape):
    return lax.broadcasted_iota(jnp.int32, shape, len(shape) - 1) >= HEAD_A


def _dil_fwd(proj, bias, grp, d):
    T = proj.shape[0]
    L = T // d
    nblk = L // QB_A
    cb = IN_WIDTH // WIDTH_A
    pv = proj.reshape(L, d * IN_WIDTH)
    scale = HEAD_A ** -0.5

    def body(q_ref, kp_ref, kc_ref, kn_ref, vp_ref, vc_ref, vn_ref, b_ref, o_ref, l_ref):
        n = pl.program_id(1)
        neg_prev = jnp.where(n > 0, 0.0, NEG_INF)
        neg_next = jnp.where(n < nblk - 1, 0.0, NEG_INF)
        second = _lane_is_second_head((QB_A, 2 * HEAD_A))
        for hp in range(HEADS_A // 2):
            cols = slice(hp * 2 * HEAD_A, (hp + 1) * 2 * HEAD_A)
            q2 = q_ref[:, cols]
            ks = [kp_ref[:, cols], kc_ref[:, cols], kn_ref[:, cols]]
            vs = [vp_ref[:, cols], vc_ref[:, cols], vn_ref[:, cols]]
            o2 = jnp.zeros((QB_A, 2 * HEAD_A), F32)
            lse2 = jnp.zeros((QB_A, 2 * HEAD_A), F32)
            for hh in range(2):
                mine = second if hh else jnp.logical_not(second)
                qm = jnp.where(mine, q2, jnp.zeros_like(q2))
                s = jnp.concatenate([lax.dot_general(qm, k, (NT, ((), ())), preferred_element_type=F32) for k in ks], axis=1)
                s = s * scale + b_ref[2 * hp + hh]
                edge = jnp.concatenate([jnp.full((1, QB_A), neg_prev, F32), jnp.zeros((1, QB_A), F32),
                                        jnp.full((1, QB_A), neg_next, F32)], axis=1)
                s = s + edge
                m = jnp.max(s, axis=-1, keepdims=True)
                p = jnp.exp(s - m)
                l = jnp.sum(p, axis=-1, keepdims=True)
                pb = p.astype(BF16)
                acc = jnp.zeros((QB_A, 2 * HEAD_A), F32)
                for b in range(3):
                    vm = jnp.where(mine, vs[b], jnp.zeros_like(vs[b]))
                    acc = acc + jnp.dot(pb[:, b * QB_A:(b + 1) * QB_A], vm, preferred_element_type=F32)
                o2 = o2 + acc / l
                lse2 = jnp.where(mine, m + jnp.log(l), lse2)
            o_ref[:, cols] = o2.astype(o_ref.dtype)
            l_ref[:, cols] = lse2

    def row(dn):
        return lambda r, n: jnp.clip(n + dn, 0, nblk - 1)

    def kv_spec(base, dn):
        rf = row(dn)
        return _bs((QB_A, WIDTH_A), lambda r, n: (rf(r, n), r * cb + (base // WIDTH_A) + grp))

    in_specs = [_bs((QB_A, WIDTH_A), lambda r, n: (n, r * cb + grp))]
    in_specs += [kv_spec(A_K, dn) for dn in (-1, 0, 1)] + [kv_spec(A_V, dn) for dn in (-1, 0, 1)]
    in_specs += [_bs((HEADS_A, QB_A, 3 * QB_A), lambda r, n: (0, 0, 0))]
    o, lse = pl.pallas_call(
        body, out_shape=[jax.ShapeDtypeStruct((L, d * WIDTH_A), BF16), jax.ShapeDtypeStruct((L, d * WIDTH_A), F32)],
        grid=(d, nblk), in_specs=in_specs,
        out_specs=[_bs((QB_A, WIDTH_A), lambda r, n: (n, r)), _bs((QB_A, WIDTH_A), lambda r, n: (n, r))],
        compiler_params=_params("parallel", "parallel"), name=f"a_fwd_d{d}")(pv, pv, pv, pv, pv, pv, pv, bias)
    return o.reshape(T, WIDTH_A), lse.reshape(T, WIDTH_A)


def _dil_bwd(proj, bias, do, lse, cterm, grp, d):
    T = proj.shape[0]
    L = T // d
    nblk = L // QB_A
    W2 = 2 * HEAD_A
    cb = IN_WIDTH // W2
    ob = WIDTH_A // W2
    pv = proj.reshape(L, d * IN_WIDTH)
    view = lambda a: a.reshape(L, d * WIDTH_A)
    scale = HEAD_A ** -0.5

    def body(q_ref, kp_ref, kc_ref, kn_ref, vp_ref, vc_ref, vn_ref, do_ref, l_ref, c_ref, b_ref,
             dq_ref, dk_ref, dv_ref, db_ref):
        r, n = pl.program_id(1), pl.program_id(2)

        @pl.when(n == 0)
        def _():
            dk_ref[...] = jnp.zeros_like(dk_ref)
            dv_ref[...] = jnp.zeros_like(dv_ref)

        @pl.when((n == 0) & (r == 0))
        def _():
            db_ref[...] = jnp.zeros_like(db_ref)

        neg_prev = jnp.where(n > 0, 0.0, NEG_INF)
        neg_next = jnp.where(n < nblk - 1, 0.0, NEG_INF)
        edge = jnp.concatenate([jnp.full((1, QB_A), neg_prev, F32), jnp.zeros((1, QB_A), F32),
                                jnp.full((1, QB_A), neg_next, F32)], axis=1)
        second = _lane_is_second_head((QB_A, W2))
        q2, do2 = q_ref[...], do_ref[...]
        ks = [kp_ref[...], kc_ref[...], kn_ref[...]]
        vs = [vp_ref[...], vc_ref[...], vn_ref[...]]
        lse2, c2 = l_ref[...], c_ref[...]
        dq2 = jnp.zeros((QB_A, W2), F32)
        dks = [jnp.zeros((QB_A, W2), F32) for _ in range(3)]
        dvs = [jnp.zeros((QB_A, W2), F32) for _ in range(3)]
        for hh in range(2):
            mine = second if hh else jnp.logical_not(second)
            zero = jnp.zeros_like(q2)
            qm = jnp.where(mine, q2, zero)
            dom = jnp.where(mine, do2, zero)
            lane = hh * HEAD_A
            lse_h, c_h = lse2[:, lane:lane + 1], c2[:, lane:lane + 1]
            s = jnp.concatenate([lax.dot_general(qm, k, (NT, ((), ())), preferred_element_type=F32) for k in ks], axis=1)
            p = jnp.exp(s * scale + b_ref[hh] + edge - lse_h)
            dp = jnp.concatenate([lax.dot_general(dom, v, (NT, ((), ())), preferred_element_type=F32) for v in vs], axis=1)
            ds = p * (dp + c_h)
            db_ref[hh] += ds
            pb, dsb = p.astype(BF16), (ds * scale).astype(BF16)
            for b in range(3):
                blk = slice(b * QB_A, (b + 1) * QB_A)
                km = jnp.where(mine, ks[b], zero)
                dq2 = dq2 + jnp.dot(dsb[:, blk], km, preferred_element_type=F32)
                dks[b] = dks[b] + lax.dot_general(dsb[:, blk], qm, (TN, ((), ())), preferred_element_type=F32)
                dvs[b] = dvs[b] + lax.dot_general(pb[:, blk], dom, (TN, ((), ())), preferred_element_type=F32)
        dq_ref[...] = dq2.astype(dq_ref.dtype)
        for b, dn in enumerate((-1, 0, 1)):
            start = pl.multiple_of(jnp.clip(n + dn, 0, nblk - 1) * QB_A, QB_A)
            dk_ref[pl.ds(start, QB_A), :] += dks[b]
            dv_ref[pl.ds(start, QB_A), :] += dvs[b]

    def kv_spec(base, dn):
        return _bs((QB_A, W2), lambda hp, r, n: (jnp.clip(n + dn, 0, nblk - 1), r * cb + (base // W2) + grp * ob + hp))

    in_specs = [_bs((QB_A, W2), lambda hp, r, n: (n, r * cb + grp * ob + hp))]
    in_specs += [kv_spec(A_K, dn) for dn in (-1, 0, 1)] + [kv_spec(A_V, dn) for dn in (-1, 0, 1)]
    in_specs += [_bs((QB_A, W2), lambda hp, r, n: (n, r * ob + hp))] * 3
    in_specs += [_bs((2, QB_A, 3 * QB_A), lambda hp, r, n: (hp, 0, 0))]
    out_shape = [jax.ShapeDtypeStruct((L, d * WIDTH_A), BF16), jax.ShapeDtypeStruct((L, d * WIDTH_A), F32),
                 jax.ShapeDtypeStruct((L, d * WIDTH_A), F32), jax.ShapeDtypeStruct((HEADS_A, QB_A, 3 * QB_A), F32)]
    out_specs = [_bs((QB_A, W2), lambda hp, r, n: (n, r * ob + hp)),
                 _bs((L, W2), lambda hp, r, n: (0, r * ob + hp)), _bs((L, W2), lambda hp, r, n: (0, r * ob + hp)),
                 _bs((2, QB_A, 3 * QB_A), lambda hp, r, n: (hp, 0, 0))]
    dq, dk, dv, db = pl.pallas_call(
        body, out_shape=out_shape, grid=(ob, d, nblk), in_specs=in_specs, out_specs=out_specs,
        compiler_params=_params("arbitrary", "arbitrary", "arbitrary"), name=f"a_bwd_d{d}")(
            pv, pv, pv, pv, pv, pv, pv, view(do), view(lse), view(cterm), bias)
    return dq.reshape(T, WIDTH_A), dk.reshape(T, WIDTH_A), dv.reshape(T, WIDTH_A), db


def _segment_ones():
    i = np.arange(WIDTH_A)
    return jnp.asarray((i[:, None] // HEAD_A == i[None, :] // HEAD_A).astype(np.float32), dtype=BF16)


def _group_weights(l0, l1, l2):
    m = jnp.maximum(jnp.maximum(l0, l1), l2)
    e = [jnp.exp(l - m) for l in (l0, l1, l2)]
    z = e[0] + e[1] + e[2]
    return [ei / z for ei in e]


def _combine_fwd(outs, lses):
    T = outs[0].shape[0]

    def fn(o0, o1, o2, l0, l1, l2):
        w = _group_weights(l0, l1, l2)
        return w[0] * o0.astype(F32) + w[1] * o1.astype(F32) + w[2] * o2.astype(F32)

    (oa,) = _ew("a_combine", fn, [_tiled(o) for o in outs] + [_tiled(l) for l in lses], [(BF16, WIDTH_A)], n_rows=T, rows=512)
    return oa


def _combine_bwd(doa, outs, lses):
    T = doa.shape[0]

    def fn(d, o0, o1, o2, l0, l1, l2, seg):
        d = d.astype(F32)
        w = _group_weights(l0, l1, l2)
        tot = jnp.zeros(d.shape, F32)
        for wg, og in zip(w, (o0, o1, o2)):
            prod = wg * d * og.astype(F32)
            hi = prod.astype(BF16)
            lo = (prod - hi.astype(F32)).astype(BF16)
            tot = tot + jnp.dot(hi, seg, preferred_element_type=F32) + jnp.dot(lo, seg, preferred_element_type=F32)
        return tuple(wg * d for wg in w) + tuple(-wg * tot for wg in w)

    res = _ew("a_combine_bwd", fn, [_tiled(doa)] + [_tiled(o) for o in outs] + [_tiled(l) for l in lses] + [_whole(_segment_ones())],
              [(BF16, WIDTH_A)] * 3 + [(F32, WIDTH_A)] * 3, n_rows=T, rows=256)
    return res[:3], res[3:]


def _rope_tables(T):
    rows = T // GRID_W
    row = jnp.repeat(jnp.arange(rows, dtype=F32), GRID_W)
    col = jnp.tile(jnp.arange(GRID_W, dtype=F32), rows)
    n_freq = HEAD_B // 4
    freq = ROPE_THETA ** (-jnp.arange(n_freq, dtype=F32) / n_freq)
    ang = jnp.concatenate([row[:, None] * freq, col[:, None] * freq], axis=-1)
    cos, sin = jnp.repeat(jnp.cos(ang), 2, axis=1), jnp.repeat(jnp.sin(ang), 2, axis=1)
    sign = jnp.where(jnp.arange(HEAD_B) % 2 == 0, -1.0, 1.0).astype(F32)
    return cos, sin * sign


def _swap_pairs(v):
    even = lax.broadcasted_iota(jnp.int32, v.shape, v.ndim - 1) % 2 == 0
    n = v.shape[-1]
    return jnp.where(even, pltpu.roll(v, n - 1, v.ndim - 1), pltpu.roll(v, 1, v.ndim - 1))


def _qk_fwd(name, proj, col0, n_heads, gain, cos, sin):
    T = proj.shape[0]

    def fn(xr, g, c, s):
        xn = _norm_fwd(xr.astype(F32), g)
        return xn * c + _swap_pairs(xn) * s

    (out,) = _ew(name, fn, [_tiled(proj, HEAD_B, col0 // HEAD_B), _whole(gain), _table(cos), _table(sin)],
                 [(BF16, HEAD_B)], n_rows=T, rows=512, ncols=n_heads)
    return out


def _qk_bwd(name, dout, proj, col0, n_heads, gain, cos, sin):
    T = proj.shape[0]

    def fn(dv, xr, g, c, s):
        dv = dv.astype(F32)
        dxn = c * dv + _swap_pairs(s * dv)
        dx, dgr = _norm_bwd(xr.astype(F32), g, dxn)
        return dx, _colsum(dgr)

    dx, dg = _ew(name, fn, [_tiled(dout, HEAD_B, 0), _tiled(proj, HEAD_B, col0 // HEAD_B), _whole(gain),
                            _table(cos), _table(sin)],
                 [(BF16, HEAD_B)], n_rows=T, rows=512, reds=(HEAD_B,), ncols=n_heads)
    return dx, jnp.sum(dg, axis=0)


def _gqa_fwd(qn, kn, proj):
    T = qn.shape[0]
    GW = 4 * HEAD_B
    scale = HEAD_B ** -0.5

    def body(q_ref, k_ref, v_ref, o_ref, l_ref):
        k, v = k_ref[...], v_ref[...]
        lane = lax.broadcasted_iota(jnp.int32, (QB_B, HEAD_B), 1)
        lse_all = jnp.zeros((QB_B, HEAD_B), F32)
        for g in range(4):
            cols = slice(g * HEAD_B, (g + 1) * HEAD_B)
            s = lax.dot_general(q_ref[:, cols], k, (NT, ((), ())), preferred_element_type=F32) * scale
            m = jnp.max(s, axis=-1, keepdims=True)
            p = jnp.exp(s - m)
            l = jnp.sum(p, axis=-1, keepdims=True)
            o = jnp.dot(p.astype(BF16), v, preferred_element_type=F32) / l
            o_ref[:, cols] = o.astype(o_ref.dtype)
            lse_all = jnp.where(lane == g, m + jnp.log(l), lse_all)
        l_ref[...] = lse_all

    return pl.pallas_call(
        body, out_shape=[jax.ShapeDtypeStruct((T, 2 * GW), BF16), jax.ShapeDtypeStruct((2, T, HEAD_B), F32)],
        grid=(2, T // QB_B),
        in_specs=[_bs((QB_B, GW), lambda kv, i: (i, kv)), _bs((T, HEAD_B), lambda kv, i: (0, kv)),
                  _bs((T, HEAD_B), lambda kv, i: (0, B_V // HEAD_B + kv))],
        out_specs=[_bs((QB_B, GW), lambda kv, i: (i, kv)), _bs((None, QB_B, HEAD_B), lambda kv, i: (kv, i, 0))],
        compiler_params=_params("parallel", "parallel"), name="b_fwd")(qn, kn, proj)


def _gqa_bwd(qn, kn, proj, o, lse, do):
    T = qn.shape[0]
    GW = 4 * HEAD_B
    scale = HEAD_B ** -0.5

    def body(q_ref, k_ref, v_ref, o_ref, l_ref, do_ref, dq_ref, dk_ref, dv_ref):
        i = pl.program_id(1)

        @pl.when(i == 0)
        def _():
            dk_ref[...] = jnp.zeros_like(dk_ref)
            dv_ref[...] = jnp.zeros_like(dv_ref)

        k, v = k_ref[...], v_ref[...]
        lse_all = l_ref[...]
        for g in range(4):
            cols = slice(g * HEAD_B, (g + 1) * HEAD_B)
            q, dob = q_ref[:, cols], do_ref[:, cols]
            delta = jnp.sum(dob.astype(F32) * o_ref[:, cols].astype(F32), axis=-1, keepdims=True)
            s = lax.dot_general(q, k, (NT, ((), ())), preferred_element_type=F32) * scale
            p = jnp.exp(s - lse_all[:, g:g + 1])
            dp = lax.dot_general(dob, v, (NT, ((), ())), preferred_element_type=F32)
            ds = (p * (dp - delta) * scale).astype(BF16)
            dq_ref[:, cols] = jnp.dot(ds, k, preferred_element_type=F32).astype(dq_ref.dtype)
            dk_ref[...] += lax.dot_general(ds, q, (TN, ((), ())), preferred_element_type=F32)
            dv_ref[...] += lax.dot_general(p.astype(BF16), dob, (TN, ((), ())), preferred_element_type=F32)

    return pl.pallas_call(
        body, out_shape=[jax.ShapeDtypeStruct((T, 2 * GW), BF16), jax.ShapeDtypeStruct((T, 2 * HEAD_B), F32),
                         jax.ShapeDtypeStruct((T, 2 * HEAD_B), F32)],
        grid=(2, T // QB_B),
        in_specs=[_bs((QB_B, GW), lambda kv, i: (i, kv)), _bs((T, HEAD_B), lambda kv, i: (0, kv)),
                  _bs((T, HEAD_B), lambda kv, i: (0, B_V // HEAD_B + kv)), _bs((QB_B, GW), lambda kv, i: (i, kv)),
                  _bs((None, QB_B, HEAD_B), lambda kv, i: (kv, i, 0)), _bs((QB_B, GW), lambda kv, i: (i, kv))],
        out_specs=[_bs((QB_B, GW), lambda kv, i: (i, kv)), _bs((T, HEAD_B), lambda kv, i: (0, kv)),
                   _bs((T, HEAD_B), lambda kv, i: (0, kv))],
        compiler_params=_params("parallel", "arbitrary"), name="b_bwd")(qn, kn, proj, o, lse, do)


def _local_step(x, target, small, big):
    T, D = x.shape
    gs, gb = {}, {}

    x1, ffn1_saved = _ffn_fwd("ffn1", x, small["ffn1_norm"], big["ffn1_w1"], big["ffn1_w3"], big["ffn1_w2"])
    (h2,) = _ew("mix_norm", lambda xv, g: _norm_fwd(xv, g), [_tiled(x1), _whole(small["mix_norm"])], [(BF16, D)], n_rows=T, rows=512)
    proj = _mm_cols("mix_in", h2, big["w_in"], tm=512, tn=512, out_dtype=BF16, cat=True)

    bias = _bias_tiles(small["rel_bias"])
    a_outs, a_lses = [], []
    for grp, d in enumerate(DILATIONS):
        o, l = _dil_fwd(proj, bias[grp * HEADS_A:(grp + 1) * HEADS_A], grp, d)
        a_outs.append(o)
        a_lses.append(l)
    o_a = _combine_fwd(a_outs, a_lses)

    cos, sin = _rope_tables(T)
    qn = _qk_fwd("b_qnorm", proj, B_Q, 8, small["q_norm"], cos, sin)
    kn = _qk_fwd("b_knorm", proj, B_K, 2, small["k_norm"], cos, sin)
    o_b, lse_b = _gqa_fwd(qn, kn, proj)

    t_a = _mm_cols("mix_branch_a", o_a, big["w_branch_a"], tm=512, tn=256, out_dtype=BF16, cat=True)
    t_b = _mm_cols("mix_branch_b", o_b, big["w_branch_b"].reshape(1, D, D), tm=512, tn=512, out_dtype=BF16, cat=True)
    bg_a, bg_b = small["b_gate"][:, :D], small["b_gate"][:, D:]

    def merge(ta, tb, ga, gb_, ba, bb):
        sa, sb = _sigmoid(ga.astype(F32) + ba), _sigmoid(gb_.astype(F32) + bb)
        return sa * ta.astype(F32) + sb * tb.astype(F32)

    gate_ins = [_tiled(proj, D, G_A // D), _tiled(proj, D, G_B // D), _whole(bg_a), _whole(bg_b)]
    (merged,) = _ew("mix_merge", merge, [_tiled(t_a), _tiled(t_b)] + gate_ins, [(BF16, D)], n_rows=T, rows=512)
    x2 = _mm_cols("mix_out", merged, big["w_out"].reshape(1, D, D), tm=512, tn=512, out_dtype=F32, cat=True,
                  extras=[x1], epilogue=lambda acc, xv: xv + acc)
    x3, ffn2_saved = _ffn_fwd("ffn2", x2, small["ffn2_norm"], big["ffn2_w1"], big["ffn2_w3"], big["ffn2_w2"])

    def head(xv, g, tv):
        r = _rstd(xv)
        xh = xv * r
        e = xh * g - tv
        dy = e * (1.0 / D)
        dxh = dy * g
        dx = r * (dxh - xh * jnp.mean(dxh * xh, axis=-1, keepdims=True))
        return dx, 0.5 * dx, _colsum(e * e) * (0.5 / D), _colsum(dy * xh)

    dx3, dx3_half, loss_cols, g_final = _ew("loss_head", head, [_tiled(x3), _whole(small["final_norm"].reshape(1, D)), _tiled(target)],
                                            [(F32, D), (BF16, D)], n_rows=T, rows=256, reds=(D, D))
    gs["final_norm"] = g_final.reshape(D)

    dx2, _, gs["ffn2_norm"], gb["ffn2_w1"], gb["ffn2_w3"], gb["ffn2_w2"] = _ffn_bwd(
        "ffn2", x2, small["ffn2_norm"], big["ffn2_w1"], big["ffn2_w3"], big["ffn2_w2"], ffn2_saved, dx3, dx3_half)

    (dmix,) = _ew("mix_bwd_cast", lambda v: v, [_tiled(dx2)], [(BF16, D)], n_rows=T, rows=512)
    w_out3 = big["w_out"].reshape(1, D, D)
    gb["w_out"] = _mm_wgrad("mix_bwd_dwout", merged, dmix, a_cols=D // 4, b_cols=None, tm=256, tn=512, J=4).reshape(D, D)
    dmerged = _mm_rows_t("mix_bwd_dmerged", dmix, w_out3, tm=512, out_dtype=BF16).reshape(T, D)

    def merge_bwd(dm, ta, tb, ga, gb_, ba, bb):
        dm, ta, tb = dm.astype(F32), ta.astype(F32), tb.astype(F32)
        sa, sb = _sigmoid(ga.astype(F32) + ba), _sigmoid(gb_.astype(F32) + bb)
        dga, dgb = dm * ta * sa * (1.0 - sa), dm * tb * sb * (1.0 - sb)
        return dm * sa, dm * sb, dga, dgb, _colsum(dga), _colsum(dgb)

    dta, dtb, dga, dgb, dba, dbb = _ew("mix_bwd_merge", merge_bwd, [_tiled(dmerged), _tiled(t_a), _tiled(t_b)] + gate_ins,
                                       [(BF16, D)] * 4, n_rows=T, rows=256, reds=(D, D))
    gs["b_gate"] = jnp.concatenate([dba.reshape(1, D), dbb.reshape(1, D)], axis=1)

    wa, wb3 = big["w_branch_a"], big["w_branch_b"].reshape(1, D, D)
    gb["w_branch_a"] = _mm_wgrad("mix_bwd_dwa", o_a, dta, a_cols=None, b_cols=D // 4, tm=WIDTH_A, tn=256, J=4)
    gb["w_branch_b"] = _mm_wgrad("mix_bwd_dwb", o_b, dtb, a_cols=D // 4, b_cols=None, tm=256, tn=512, J=4).reshape(D, D)
    do_a = _mm("mix_bwd_doa", (T // 512, 4),
               [(dta, _bs((512, D // 4), lambda i, j: (i, j)), wa, _bs((None, WIDTH_A, D // 4), lambda i, j: (j, 0, 0)))],
               jax.ShapeDtypeStruct((T, WIDTH_A), BF16), _bs((512, WIDTH_A), lambda i, j: (i, 0)), NT, reduce_axis=1)
    do_b = _mm_rows_t("mix_bwd_dob", dtb, wb3, tm=512, out_dtype=BF16).reshape(T, D)

    dqn, dkn, dv_b = _gqa_bwd(qn, kn, proj, o_b, lse_b, do_b)
    dq_b, gs["q_norm"] = _qk_bwd("b_bwd_qnorm", dqn, proj, B_Q, 8, small["q_norm"], cos, sin)
    dk_b, gs["k_norm"] = _qk_bwd("b_bwd_knorm", dkn, proj, B_K, 2, small["k_norm"], cos, sin)

    do_groups, c_groups = _combine_bwd(do_a, a_outs, a_lses)
    dqs, dks, dvs, dbs = [], [], [], []
    for grp, d in enumerate(DILATIONS):
        dq, dk, dv, db = _dil_bwd(proj, bias[grp * HEADS_A:(grp + 1) * HEADS_A], do_groups[grp], a_lses[grp], c_groups[grp], grp, d)
        dqs.append(dq), dks.append(dk), dvs.append(dv), dbs.append(db)
    gs["rel_bias"] = _bias_grad(jnp.concatenate(dbs, axis=0))

    dproj = jnp.concatenate([p.astype(BF16) for p in dqs + dks + dvs + [dq_b, dk_b, dv_b, dga, dgb]], axis=1)
    w_in = big["w_in"]
    nq = w_in.shape[2]
    gb["w_in"] = _mm_wgrad("mix_bwd_dwin", h2, dproj, a_cols=None, b_cols=nq, tm=512, tn=512, J=4)
    dh2 = _mm("mix_bwd_dh", (T // 512, D // 512, 4),
              [(dproj, _bs((512, nq), lambda i, k, j: (i, j)), w_in, _bs((None, 512, nq), lambda i, k, j: (j, k, 0)))],
              jax.ShapeDtypeStruct((T, D), F32), _bs((512, 512), lambda i, k, j: (i, k)), NT, reduce_axis=2)

    def nb(xv, gv, dhv, dres):
        dx, dgr = _norm_bwd(xv, gv, dhv)
        dx = dx + dres
        return dx, 0.5 * dx, _colsum(dgr)

    dx1, dx1_half, g_mix = _ew("mix_bwd_norm", nb, [_tiled(x1), _whole(small["mix_norm"]), _tiled(dh2), _tiled(dx2)],
                               [(F32, D), (BF16, D)], n_rows=T, rows=256, reds=(D,))
    gs["mix_norm"] = g_mix.reshape(1, D)

    dx0, _, gs["ffn1_norm"], gb["ffn1_w1"], gb["ffn1_w3"], gb["ffn1_w2"] = _ffn_bwd(
        "ffn1", x, small["ffn1_norm"], big["ffn1_w1"], big["ffn1_w3"], big["ffn1_w2"], ffn1_saved, dx1, dx1_half)
    return loss_cols.reshape(1, D), dx0, gs, gb


def _position():
    return lax.axis_index("x"), lax.axis_index("y"), lax.axis_index("c")


def _any_specs(n):
    return [pl.BlockSpec(memory_space=pl.ANY)] * n


def _gather_weights(shards):
    n = len(shards)

    def body(*refs):
        ins, outs = refs[:n], refs[n:2 * n]
        send_sems, recv_sems, local_sems = refs[2 * n:]
        x, y, c = _position()
        me = 2 * x + y
        peers = [(1 - x, y, c), (x, 1 - y, c), (1 - x, 1 - y, c)]
        copies = []
        for i in range(n):
            cp = pltpu.make_async_copy(ins[i], outs[i].at[me], local_sems.at[i])
            cp.start()
            copies.append(cp)
            for p, peer in enumerate(peers):
                cp = pltpu.make_async_remote_copy(ins[i], outs[i].at[me], send_sems.at[3 * i + p], recv_sems.at[3 * i + p],
                                                  device_id=peer, device_id_type=MESH)
                cp.start()
                copies.append(cp)
        for cp in copies:
            cp.wait()

    return pl.pallas_call(
        body, out_shape=[jax.ShapeDtypeStruct((4,) + s.shape, s.dtype) for s in shards],
        in_specs=_any_specs(n), out_specs=_any_specs(n),
        scratch_shapes=[pltpu.SemaphoreType.DMA((3 * n,)), pltpu.SemaphoreType.DMA((3 * n,)), pltpu.SemaphoreType.DMA((n,))],
        compiler_params=pltpu.CompilerParams(has_side_effects=True), name="gather_weights")(*shards)


def _scatter_grads(stacks):
    n = len(stacks)

    def body(*refs):
        ins, outs = refs[:n], refs[n:2 * n]
        send_sems, recv_sems, local_sems = refs[2 * n:]
        x, y, c = _position()
        me = 2 * x + y
        peers = [(1 - x, y, c), (x, 1 - y, c), (1 - x, 1 - y, c)]
        copies = []
        for i in range(n):
            cp = pltpu.make_async_copy(ins[i].at[me], outs[i].at[me], local_sems.at[i])
            cp.start()
            copies.append(cp)
            for p, (px, py, pc) in enumerate(peers):
                cp = pltpu.make_async_remote_copy(ins[i].at[2 * px + py], outs[i].at[me], send_sems.at[3 * i + p],
                                                  recv_sems.at[3 * i + p], device_id=(px, py, pc), device_id_type=MESH)
                cp.start()
                copies.append(cp)
        for cp in copies:
            cp.wait()

    return pl.pallas_call(
        body, out_shape=[jax.ShapeDtypeStruct(s.shape, s.dtype) for s in stacks],
        in_specs=_any_specs(n), out_specs=_any_specs(n),
        scratch_shapes=[pltpu.SemaphoreType.DMA((3 * n,)), pltpu.SemaphoreType.DMA((3 * n,)), pltpu.SemaphoreType.DMA((n,))],
        compiler_params=pltpu.CompilerParams(has_side_effects=True), name="scatter_grads")(*stacks)


def _swap_with_sibling(parts):
    n = len(parts)

    def body(*refs):
        ins, outs = refs[:n], refs[n:2 * n]
        send_sems, recv_sems = refs[2 * n:]
        x, y, c = _position()
        copies = []
        for i in range(n):
            cp = pltpu.make_async_remote_copy(ins[i], outs[i], send_sems.at[i], recv_sems.at[i],
                                              device_id=(x, y, 1 - c), device_id_type=MESH)
            cp.start()
            copies.append(cp)
        for cp in copies:
            cp.wait()

    return pl.pallas_call(
        body, out_shape=[jax.ShapeDtypeStruct(s.shape, s.dtype) for s in parts],
        in_specs=_any_specs(n), out_specs=_any_specs(n),
        scratch_shapes=[pltpu.SemaphoreType.DMA((n,)), pltpu.SemaphoreType.DMA((n,))],
        compiler_params=pltpu.CompilerParams(has_side_effects=True), name="swap_with_sibling")(*parts)


def _allreduce_small(buf):
    R, C = buf.shape
    flips = [(fx, fy, fc) for fx in (0, 1) for fy in (0, 1) for fc in (0, 1)][1:]

    def body(in_ref, out_ref, land_ref, send_sems, recv_sems):
        x, y, c = _position()
        me = 4 * x + 2 * y + c
        copies = []
        for k, (fx, fy, fc) in enumerate(flips):
            px, py, pc = (1 - x if fx else x), (1 - y if fy else y), (1 - c if fc else c)
            cp = pltpu.make_async_remote_copy(in_ref, land_ref.at[me], send_sems.at[k], recv_sems.at[k],
                                              device_id=(px, py, pc), device_id_type=MESH)
            cp.start()
            copies.append(cp)
        land_ref[me] = in_ref[...]
        for cp in copies:
            cp.wait()
        acc = land_ref[0]
        for k in range(1, 8):
            acc = acc + land_ref[k]
        out_ref[...] = acc

    return pl.pallas_call(
        body, out_shape=jax.ShapeDtypeStruct((R, C), F32),
        in_specs=[pl.BlockSpec(memory_space=pltpu.VMEM)], out_specs=pl.BlockSpec(memory_space=pltpu.VMEM),
        scratch_shapes=[pltpu.VMEM((8, R, C), F32), pltpu.SemaphoreType.DMA((7,)), pltpu.SemaphoreType.DMA((7,))],
        compiler_params=pltpu.CompilerParams(has_side_effects=True), name="allreduce_small")(buf)


def _adamw_math(w, g, m, v):
    m2 = ADAM_B1 * m + (1.0 - ADAM_B1) * g
    v2 = ADAM_B2 * v + (1.0 - ADAM_B2) * (g * g)
    m_hat = m2 / (1.0 - ADAM_B1 ** ADAM_STEP)
    v_hat = v2 / (1.0 - ADAM_B2 ** ADAM_STEP)
    delta = -ADAM_LR * (m_hat / (jnp.sqrt(v_hat) + ADAM_EPS) + ADAM_WD * w)
    return delta, m2, v2


def _adamw_big(name, w, m, v, part_mine, part_sibling):
    R, C = w.shape
    rows = 256 if R % 256 == 0 else R // 2 if (R // 2) % 8 == 0 else R

    def fn(wv, mv, vv, a, b):
        g = a + b
        return (g,) + _adamw_math(wv, g, mv, vv)

    return _ew(name, fn, [_tiled(w), _tiled(m), _tiled(v), _tiled(part_mine), _tiled(part_sibling)], [(F32, C)] * 4, n_rows=R, rows=rows)


def _sum_four(name, stack):
    _, R, C = stack.shape
    rows = 256 if R % 256 == 0 else R // 2 if (R // 2) % 8 == 0 else R
    flat = stack.reshape(4 * R, C)
    nrb = R // rows

    def fn(a, b, c, d):
        return ((a.astype(F32) + b.astype(F32)) + c.astype(F32)) + d.astype(F32)

    (out,) = _ew(name, fn, [_tiled(flat, None, 0, k * nrb) for k in range(4)], [(F32, C)], n_rows=R, rows=rows)
    return out


BIG = ("ffn1_w1", "ffn1_w3", "ffn1_w2", "w_in", "w_branch_a", "w_branch_b", "w_out", "ffn2_w1", "ffn2_w3", "ffn2_w2")
SMALL = ("ffn1_norm", "mix_norm", "b_gate", "q_norm", "k_norm", "rel_bias", "ffn2_norm", "final_norm")
ORDER = ("ffn1_norm", "ffn1_w1", "ffn1_w3", "ffn1_w2", "mix_norm", "w_in", "b_gate", "q_norm", "k_norm", "rel_bias",
         "w_branch_a", "w_branch_b", "w_out", "ffn2_norm", "ffn2_w1", "ffn2_w3", "ffn2_w2", "final_norm")
LANES = 128


def _pack_small(d):
    rows = []
    for n in SMALL:
        flat = d[n].reshape(-1)
        pad = (-flat.shape[0]) % LANES
        rows.append(jnp.pad(flat, (0, pad)).reshape(-1, LANES))
    buf = jnp.concatenate(rows, axis=0)
    return jnp.pad(buf, ((0, (-buf.shape[0]) % 8), (0, 0)))


def _unpack_small(buf, like):
    out, r = {}, 0
    for n in SMALL:
        size = like[n].size
        nr = -(-size // LANES)
        out[n] = buf[r:r + nr].reshape(-1)[:size].reshape(like[n].shape)
        r += nr
    return out


def kernel(x, ffn1_norm, ffn1_w1, ffn1_w3, ffn1_w2, mix_norm, w_in, b_gate, q_norm, k_norm, rel_bias, w_branch_a, w_branch_b, w_out, ffn2_norm, ffn2_w1, ffn2_w3, ffn2_w2, final_norm, loss_target, m_ffn1_norm, m_ffn1_w1, m_ffn1_w3, m_ffn1_w2, m_mix_norm, m_w_in, m_b_gate, m_q_norm, m_k_norm, m_rel_bias, m_w_branch_a, m_w_branch_b, m_w_out, m_ffn2_norm, m_ffn2_w1, m_ffn2_w3, m_ffn2_w2, m_final_norm, v_ffn1_norm, v_ffn1_w1, v_ffn1_w3, v_ffn1_w2, v_mix_norm, v_w_in, v_b_gate, v_q_norm, v_k_norm, v_rel_bias, v_w_branch_a, v_w_branch_b, v_w_out, v_ffn2_norm, v_ffn2_w1, v_ffn2_w3, v_ffn2_w2, v_final_norm):
    given = dict(locals())
    w = {n: given[n] for n in ORDER}
    m = {n: given["m_" + n] for n in ORDER}
    v = {n: given["v_" + n] for n in ORDER}
    T, D = x.shape[1], x.shape[2]

    quarter = {n: w[n].reshape(w[n].shape[1:]) for n in BIG}
    gathered = _gather_weights([quarter[n].astype(BF16) for n in BIG])
    big = dict(zip(BIG, gathered))
    big["w_branch_b"] = big["w_branch_b"].reshape(D, D)
    big["w_out"] = big["w_out"].reshape(D, D)
    small = {n: w[n] for n in SMALL}
    small["rel_bias"] = w["rel_bias"]

    loss_cols, grad_x, gs, gb = _local_step(x.reshape(T, D), loss_target.reshape(T, D), small, big)
    loss = lax.psum(jnp.sum(loss_cols), ("x", "y", "c"))

    stacks = [gb[n].reshape((4,) + quarter[n].shape) for n in BIG]
    landed = _scatter_grads(stacks)
    partial = [_sum_four(f"sum4_{n}", s) for n, s in zip(BIG, landed)]
    other = _swap_with_sibling(partial)
    grads, deltas, new_m, new_v = {}, {}, {}, {}
    for n, mine, theirs in zip(BIG, partial, other):
        shp = w[n].shape
        two_d = quarter[n].shape
        res = _adamw_big(f"adamw_{n}", quarter[n], m[n].reshape(two_d), v[n].reshape(two_d), mine, theirs)
        grads[n], deltas[n], new_m[n], new_v[n] = [r.reshape(shp) for r in res]

    gs = {n: gs[n].reshape(w[n].shape) for n in SMALL}
    g_small = _allreduce_small(_pack_small(gs))
    packed = [_pack_small({n: d[n] for n in SMALL}) for d in (w, m, v)]
    R = g_small.shape[0]
    res = _ew("adamw_small", lambda wv, mv, vv, g: (g,) + _adamw_math(wv, g, mv, vv),
              [_tiled(packed[0]), _tiled(packed[1]), _tiled(packed[2]), _tiled(g_small)], [(F32, LANES)] * 4, n_rows=R, rows=R)
    for d, buf in zip((grads, deltas, new_m, new_v), res):
        d.update(_unpack_small(buf, w))

    return (loss, grad_x.reshape(x.shape), *[grads[n] for n in ORDER], *[deltas[n] for n in ORDER],
            *[new_m[n] for n in ORDER], *[new_v[n] for n in ORDER])
```

```python
import functools
import math

import numpy as np
import jax
import jax.numpy as jnp
from jax import lax
from jax.experimental import pallas as pl
from jax.experimental.pallas import tpu as pltpu

F32 = jnp.float32
BF16 = jnp.bfloat16
MESH = pl.DeviceIdType.MESH

NEG_INF = -1e30
EPS = 1e-6
GRID_W = 64
ROPE_THETA = 10000.0
DILATIONS = (1, 4, 16)
BAND_HALF = 64
HEAD_A = 64
HEADS_A = 8
WIDTH_A = HEADS_A * HEAD_A
HEAD_B = 128
N_BUCKETS = 32
MAX_DISTANCE = 1024
ADAM_LR, ADAM_B1, ADAM_B2, ADAM_EPS, ADAM_WD, ADAM_STEP = 0.001, 0.9, 0.999, 1e-08, 0.01, 10

A_Q, A_K, A_V = 0, 1536, 3072
B_Q, B_K, B_V = 4608, 5632, 5888
G_A, G_B = 6144, 7168
IN_WIDTH = 8192

VMEM_LIMIT_BYTES = 56 * 1024 * 1024
QB_A = 128
QB_B = 256


def _params(*sem):
    return pltpu.CompilerParams(dimension_semantics=sem, vmem_limit_bytes=VMEM_LIMIT_BYTES)


def _bs(shape, fn):
    return pl.BlockSpec(shape, fn)


def _mm(name, grid, pairs, out_shape, out_spec, dims, *, reduce_axis=None, extras=(), epilogue=None, deps=()):
    n_pairs, n_extra, n_deps = len(pairs), len(extras), len(deps)
    operands = [p[0] for p in pairs] + [p[2] for p in pairs] + [e[0] for e in extras] + list(deps)
    in_specs = [p[1] for p in pairs] + [p[3] for p in pairs] + [e[1] for e in extras] + _any_specs(n_deps)
    tile = tuple(s for s in out_spec.block_shape if s is not None)
    n_steps = grid[reduce_axis] if reduce_axis is not None else 1

    def body(*refs):
        a_refs, b_refs = refs[:n_pairs], refs[n_pairs:2 * n_pairs]
        e_refs = refs[2 * n_pairs:2 * n_pairs + n_extra]
        o_ref = refs[2 * n_pairs + n_extra + n_deps]
        acc = None
        for a_ref, b_ref in zip(a_refs, b_refs):
            t = lax.dot_general(a_ref[...], b_ref[...], (dims, ((), ())), preferred_element_type=F32)
            acc = t if acc is None else acc + t

        def finish(v):
            if epilogue is not None:
                v = epilogue(v, *[e[...] for e in e_refs])
            o_ref[...] = v.astype(o_ref.dtype)

        if reduce_axis is None:
            finish(acc)
        else:
            acc_ref = refs[-1]
            k = pl.program_id(reduce_axis)

            @pl.when(k == 0)
            def _():
                acc_ref[...] = acc

            @pl.when(k > 0)
            def _():
                acc_ref[...] += acc

            @pl.when(k == n_steps - 1)
            def _():
                finish(acc_ref[...])

    sem = ["parallel"] * len(grid)
    if reduce_axis is not None:
        sem[reduce_axis] = "arbitrary"
    return pl.pallas_call(
        body, out_shape=out_shape, grid=grid, in_specs=in_specs, out_specs=out_spec,
        scratch_shapes=[pltpu.VMEM(tile, F32)] if reduce_axis is not None else [],
        compiler_params=_params(*sem), name=name)(*operands)


NN = ((1,), (0,))
NT = ((1,), (1,))
TN = ((0,), (0,))


def _mm_cols(name, a, w, *, tm, tn, out_dtype, cat, extras=(), epilogue=None):
    M, K = a.shape
    J, _, n = w.shape
    tn = min(tn, n)
    nb = n // tn
    if cat:
        shape, spec = (M, J * n), _bs((tm, tn), lambda j, i, k: (i, j * nb + k))
    else:
        shape, spec = (J, M, n), _bs((None, tm, tn), lambda j, i, k: (j, i, k))
    ex = [(e, _bs((tm, tn), lambda j, i, k: (i, j * nb + k))) for e in extras]
    return _mm(name, (J, M // tm, nb),
               [(a, _bs((tm, K), lambda j, i, k: (i, 0)), w, _bs((None, K, tn), lambda j, i, k: (j, 0, k)))],
               jax.ShapeDtypeStruct(shape, out_dtype), spec, NN, extras=ex, epilogue=epilogue)


def _mm_rows_t(name, a, w, *, tm, out_dtype):
    M, N = a.shape
    J, f, _ = w.shape
    return _mm(name, (J, M // tm),
               [(a, _bs((tm, N), lambda j, i: (i, 0)), w, _bs((None, f, N), lambda j, i: (j, 0, 0)))],
               jax.ShapeDtypeStruct((J, M, f), out_dtype), _bs((None, tm, f), lambda j, i: (j, i, 0)), NT)


def _mm_wgrad(name, a, b, *, a_cols, b_cols, tm, tn, J):
    def pick(arr, cols, t):
        if arr.ndim == 3:
            T, c = arr.shape[1], arr.shape[2]
            t = min(t, c)
            return T, c, t, (lambda sel: _bs((None, T, t), lambda j, i, k: (j, 0, sel(i, k))))
        T = arr.shape[0]
        c = arr.shape[1] if cols is None else cols
        t = min(t, c)
        per = c // t
        if cols is None:
            return T, c, t, (lambda sel: _bs((T, t), lambda j, i, k: (0, sel(i, k))))
        return T, c, t, (lambda sel: _bs((T, t), lambda j, i, k: (0, j * per + sel(i, k))))
    _, ca, tm, mk_a = pick(a, a_cols, tm)
    _, cb, tn, mk_b = pick(b, b_cols, tn)
    return _mm(name, (J, ca // tm, cb // tn),
               [(a, mk_a(lambda i, k: i), b, mk_b(lambda i, k: k))],
               jax.ShapeDtypeStruct((J, ca, cb), BF16), _bs((None, tm, tn), lambda j, i, k: (j, i, k)), TN)


def _tiled(arr, width=None, col=0, rowblk=0):
    return ("t", arr, arr.shape[1] if width is None else width, col, rowblk)


def _table(arr):
    return ("f", arr)


def _whole(arr):
    return ("w", arr)


def _ew(name, fn, ins, outs, *, n_rows, rows, reds=(), ncols=1, deps=()):
    nrb = n_rows // rows
    n_deps = len(deps)
    operands, in_specs = [], []
    for spec in ins:
        if spec[0] == "t":
            _, arr, width, col, rowblk = spec
            step = 1 if ncols > 1 else 0
            in_specs.append(_bs((rows, width), lambda c, i, col=col, rowblk=rowblk, step=step: (rowblk + i, col + c * step)))
        elif spec[0] == "f":
            arr = spec[1]
            in_specs.append(_bs((rows, arr.shape[1]), lambda c, i: (i, 0)))
        else:
            arr = spec[1]
            nd = arr.ndim
            if nd == 3:
                in_specs.append(_bs((None,) + arr.shape[1:], lambda c, i: (c, 0, 0)))
            else:
                in_specs.append(_bs(arr.shape, lambda c, i, nd=nd: (0,) * nd))
        operands.append(arr)
    out_shapes = [jax.ShapeDtypeStruct((n_rows, ncols * w), dt) for dt, w in outs]
    out_specs = [_bs((rows, w), lambda c, i: (i, c)) for _, w in outs]
    out_shapes += [jax.ShapeDtypeStruct((ncols, 1, w), F32) for w in reds]
    out_specs += [_bs((None, 1, w), lambda c, i: (c, 0, 0)) for w in reds]
    n_in, n_out, n_red = len(ins), len(outs), len(reds)
    operands += list(deps)
    in_specs += _any_specs(n_deps)

    def body(*refs):
        vals = fn(*[r[...] for r in refs[:n_in]])
        if not isinstance(vals, (tuple, list)):
            vals = (vals,)
        o_refs = refs[n_in + n_deps:]
        for o_ref, v in zip(o_refs[:n_out], vals[:n_out]):
            o_ref[...] = v.astype(o_ref.dtype)
        if n_red:
            i = pl.program_id(1)
            for r_ref, v in zip(o_refs[n_out:], vals[n_out:]):
                @pl.when(i == 0)
                def _(r_ref=r_ref):
                    r_ref[...] = jnp.zeros_like(r_ref)
                r_ref[...] += v

    res = pl.pallas_call(
        body, out_shape=out_shapes, grid=(ncols, nrb), in_specs=in_specs, out_specs=out_specs,
        compiler_params=_params("parallel", "arbitrary" if n_red else "parallel"), name=name)(*operands)
    return res


def _colsum(v):
    return jnp.sum(v, axis=0, keepdims=True)


def _rstd(x):
    return lax.rsqrt(jnp.mean(x * x, axis=-1, keepdims=True) + EPS)


def _sigmoid(x):
    return 1.0 / (1.0 + jnp.exp(-x))


def _norm_fwd(x, g):
    return x * _rstd(x) * g


def _norm_bwd(x, g, dy):
    r = _rstd(x)
    xh = x * r
    dxh = dy * g
    dx = r * (dxh - xh * jnp.mean(dxh * xh, axis=-1, keepdims=True))
    return dx, dy * xh


def _ffn_fwd(tag, x, gain, get_w, deps=()):
    T, D = x.shape
    (h,) = _ew(f"{tag}_norm", lambda xv, g: _norm_fwd(xv, g), [_tiled(x), _whole(gain)], [(BF16, D)], n_rows=T, rows=512,
               deps=deps)
    w1, w3 = get_w(f"{tag}_w1", h), get_w(f"{tag}_w3", h)
    J, _, f = w1.shape
    u = _mm_cols(f"{tag}_up1", h, w1, tm=512, tn=f, out_dtype=BF16, cat=False)
    g = _mm_cols(f"{tag}_up3", h, w3, tm=512, tn=f, out_dtype=BF16, cat=False)

    def act(uv, gv):
        uv, gv = uv.astype(F32), gv.astype(F32)
        return uv * _sigmoid(uv) * gv

    (a,) = _ew(f"{tag}_act", act, [_tiled(u.reshape(J * T, f)), _tiled(g.reshape(J * T, f))], [(BF16, f)],
               n_rows=J * T, rows=512)
    a = a.reshape(J, T, f)
    w2 = get_w(f"{tag}_w2", a)
    y = _mm(f"{tag}_down", (T // 512, D // 512, J),
            [(a, _bs((None, 512, f), lambda i, k, j: (j, i, 0)), w2, _bs((None, f, 512), lambda i, k, j: (j, 0, k)))],
            jax.ShapeDtypeStruct((T, D), F32), _bs((512, 512), lambda i, k, j: (i, k)), NN, reduce_axis=2,
            extras=[(x, _bs((512, 512), lambda i, k, j: (i, k)))], epilogue=lambda acc, xv: xv + 0.5 * acc)
    return y, (h, u, g, a)


def _ffn_bwd(tag, x, gain, get_w, put_g, saved, dy, dy_half):
    h, u, g, a = saved
    T, D = x.shape
    w1, w3, w2 = [get_w(f"{tag}_{n}", dy_half) for n in ("w1", "w3", "w2")]
    J, _, f = w1.shape
    da = _mm_rows_t(f"{tag}_bwd_da", dy_half, w2, tm=512, out_dtype=BF16)
    dw2 = _mm_wgrad(f"{tag}_bwd_dw2", a, dy_half, a_cols=None, b_cols=None, tm=f, tn=512, J=J)

    def act_bwd(uv, gv, dav):
        uv, gv, dav = uv.astype(F32), gv.astype(F32), dav.astype(F32)
        s = _sigmoid(uv)
        return dav * gv * (s * (1.0 + uv * (1.0 - s))), dav * (uv * s)

    du, dg = _ew(f"{tag}_bwd_act", act_bwd,
                 [_tiled(u.reshape(J * T, f)), _tiled(g.reshape(J * T, f)), _tiled(da.reshape(J * T, f))],
                 [(BF16, f), (BF16, f)], n_rows=J * T, rows=512)
    du, dg = du.reshape(J, T, f), dg.reshape(J, T, f)
    dw1 = _mm_wgrad(f"{tag}_bwd_dw1", h, du, a_cols=None, b_cols=None, tm=512, tn=f, J=J)
    dw3 = _mm_wgrad(f"{tag}_bwd_dw3", h, dg, a_cols=None, b_cols=None, tm=512, tn=f, J=J)
    deps = put_g({f"{tag}_w1": dw1, f"{tag}_w3": dw3, f"{tag}_w2": dw2})
    a_spec = _bs((None, 512, f), lambda i, k, j: (j, i, 0))
    w_spec = _bs((None, 512, f), lambda i, k, j: (j, k, 0))
    dh = _mm(f"{tag}_bwd_dh", (T // 512, D // 512, J), [(du, a_spec, w1, w_spec), (dg, a_spec, w3, w_spec)],
             jax.ShapeDtypeStruct((T, D), F32), _bs((512, 512), lambda i, k, j: (i, k)), NT, reduce_axis=2, deps=deps)

    def nb(xv, gv, dhv, dres):
        dx, dgr = _norm_bwd(xv, gv, dhv)
        dx = dx + dres
        return dx, 0.5 * dx, _colsum(dgr)

    dx, dx_half, dgain = _ew(f"{tag}_bwd_norm", nb, [_tiled(x), _whole(gain), _tiled(dh), _tiled(dy)],
                             [(F32, D), (BF16, D)], n_rows=T, rows=256, reds=(D,))
    return dx, dx_half, dgain.reshape(1, D)


def _t5_bucket(rel):
    n = N_BUCKETS // 2
    max_exact = n // 2
    ret = jnp.where(rel > 0, n, 0)
    a = jnp.abs(rel)
    af = jnp.maximum(a, 1).astype(F32)
    large = max_exact + (jnp.log(af / max_exact) / math.log(MAX_DISTANCE / max_exact) * (n - max_exact)).astype(jnp.int32)
    large = jnp.minimum(large, n - 1)
    return ret + jnp.where(a < max_exact, a, large)


def _band_steps():
    qi = jnp.arange(QB_A, dtype=jnp.int32)[:, None]
    kj = jnp.arange(3 * QB_A, dtype=jnp.int32)[None, :] - QB_A
    return kj - qi


def _bias_tiles(rel_bias):
    steps = _band_steps()
    buckets = jnp.stack([_t5_bucket(steps * d) for d in DILATIONS])
    inband = (jnp.abs(steps) <= BAND_HALF).astype(jnp.int32)
    n_heads = rel_bias.shape[1]

    def body(tab_ref, b_ref, m_ref, o_ref):
        hd = pl.program_id(0)
        bkt = b_ref[...]
        acc = jnp.zeros(bkt.shape, F32)
        for b in range(N_BUCKETS):
            acc = jnp.where(bkt == b, tab_ref[b, hd], acc)
        o_ref[...] = jnp.where(m_ref[...] > 0, acc, NEG_INF)

    return pl.pallas_call(
        body, out_shape=jax.ShapeDtypeStruct((n_heads, QB_A, 3 * QB_A), F32), grid=(n_heads,),
        in_specs=[pl.BlockSpec(memory_space=pltpu.SMEM),
                  _bs((None, QB_A, 3 * QB_A), lambda hd: (hd // HEADS_A, 0, 0)),
                  _bs((QB_A, 3 * QB_A), lambda hd: (0, 0))],
        out_specs=_bs((None, QB_A, 3 * QB_A), lambda hd: (hd, 0, 0)),
        compiler_params=_params("parallel"), name="a_bias_tiles")(rel_bias, buckets, inband)


def _bias_grad(dbias):
    steps = np.arange(3 * QB_A)[None, :] - QB_A - np.arange(QB_A)[:, None]
    inband = np.abs(steps) <= BAND_HALF
    present = []
    for d in DILATIONS:
        rel = steps * d
        a = np.abs(rel)
        large = 8 + (np.log(np.maximum(a, 1) / 8.0) / math.log(MAX_DISTANCE / 8.0) * 8).astype(np.int64)
        bk = np.where(rel > 0, 16, 0) + np.where(a < 8, a, np.minimum(large, 15))
        present.append(sorted(set(bk[inband].tolist())))
    buckets = jnp.stack([_t5_bucket(_band_steps() * d) for d in DILATIONS])
    n_heads = dbias.shape[0]

    def body(b_ref, d_ref, o_ref):
        row = lax.broadcasted_iota(jnp.int32, (N_BUCKETS, n_heads), 0)
        col = lax.broadcasted_iota(jnp.int32, (N_BUCKETS, n_heads), 1)
        out = jnp.zeros((N_BUCKETS, n_heads), F32)
        for grp in range(len(DILATIONS)):
            bkt = b_ref[grp]
            for hh in range(HEADS_A):
                hd = grp * HEADS_A + hh
                ds = d_ref[hd]
                for b in present[grp]:
                    tot = jnp.sum(jnp.where(bkt == b, ds, 0.0))
                    out = jnp.where((row == b) & (col == hd), tot, out)
        o_ref[...] = out

    return pl.pallas_call(
        body, out_shape=jax.ShapeDtypeStruct((N_BUCKETS, n_heads), F32),
        compiler_params=pltpu.CompilerParams(vmem_limit_bytes=VMEM_LIMIT_BYTES), name="a_bias_grad")(buckets, dbias)


def _lane_is_second_head(shape):
    return lax.broadcasted_iota(jnp.int32, shape, len(shape) - 1) >= HEAD_A


def _dil_fwd(proj, bias, grp, d):
    T = proj.shape[0]
    L = T // d
    nblk = L // QB_A
    cb = IN_WIDTH // WIDTH_A
    pv = proj.reshape(L, d * IN_WIDTH)
    scale = HEAD_A ** -0.5

    def body(q_ref, kp_ref, kc_ref, kn_ref, vp_ref, vc_ref, vn_ref, b_ref, o_ref, l_ref):
        n = pl.program_id(1)
        neg_prev = jnp.where(n > 0, 0.0, NEG_INF)
        neg_next = jnp.where(n < nblk - 1, 0.0, NEG_INF)
        second = _lane_is_second_head((QB_A, 2 * HEAD_A))
        for hp in range(HEADS_A // 2):
            cols = slice(hp * 2 * HEAD_A, (hp + 1) * 2 * HEAD_A)
            q2 = q_ref[:, cols]
            ks = [kp_ref[:, cols], kc_ref[:, cols], kn_ref[:, cols]]
            vs = [vp_ref[:, cols], vc_ref[:, cols], vn_ref[:, cols]]
            o2 = jnp.zeros((QB_A, 2 * HEAD_A), F32)
            lse2 = jnp.zeros((QB_A, 2 * HEAD_A), F32)
            for hh in range(2):
                mine = second if hh else jnp.logical_not(second)
                qm = jnp.where(mine, q2, jnp.zeros_like(q2))
                s = jnp.concatenate([lax.dot_general(qm, k, (NT, ((), ())), preferred_element_type=F32) for k in ks], axis=1)
                s = s * scale + b_ref[2 * hp + hh]
                edge = jnp.concatenate([jnp.full((1, QB_A), neg_prev, F32), jnp.zeros((1, QB_A), F32),
                                        jnp.full((1, QB_A), neg_next, F32)], axis=1)
                s = s + edge
                m = jnp.max(s, axis=-1, keepdims=True)
                p = jnp.exp(s - m)
                l = jnp.sum(p, axis=-1, keepdims=True)
                pb = p.astype(BF16)
                acc = jnp.zeros((QB_A, 2 * HEAD_A), F32)
                for b in range(3):
                    vm = jnp.where(mine, vs[b], jnp.zeros_like(vs[b]))
                    acc = acc + jnp.dot(pb[:, b * QB_A:(b + 1) * QB_A], vm, preferred_element_type=F32)
                o2 = o2 + acc / l
                lse2 = jnp.where(mine, m + jnp.log(l), lse2)
            o_ref[:, cols] = o2.astype(o_ref.dtype)
            l_ref[:, cols] = lse2

    def row(dn):
        return lambda r, n: jnp.clip(n + dn, 0, nblk - 1)

    def kv_spec(base, dn):
        rf = row(dn)
        return _bs((QB_A, WIDTH_A), lambda r, n: (rf(r, n), r * cb + (base // WIDTH_A) + grp))

    in_specs = [_bs((QB_A, WIDTH_A), lambda r, n: (n, r * cb + grp))]
    in_specs += [kv_spec(A_K, dn) for dn in (-1, 0, 1)] + [kv_spec(A_V, dn) for dn in (-1, 0, 1)]
    in_specs += [_bs((HEADS_A, QB_A, 3 * QB_A), lambda r, n: (0, 0, 0))]
    o, lse = pl.pallas_call(
        body, out_shape=[jax.ShapeDtypeStruct((L, d * WIDTH_A), BF16), jax.ShapeDtypeStruct((L, d * WIDTH_A), F32)],
        grid=(d, nblk), in_specs=in_specs,
        out_specs=[_bs((QB_A, WIDTH_A), lambda r, n: (n, r)), _bs((QB_A, WIDTH_A), lambda r, n: (n, r))],
        compiler_params=_params("parallel", "parallel"), name=f"a_fwd_d{d}")(pv, pv, pv, pv, pv, pv, pv, bias)
    return o.reshape(T, WIDTH_A), lse.reshape(T, WIDTH_A)


def _dil_bwd(proj, bias, do, lse, cterm, grp, d):
    T = proj.shape[0]
    L = T // d
    nblk = L // QB_A
    W2 = 2 * HEAD_A
    cb = IN_WIDTH // W2
    ob = WIDTH_A // W2
    pv = proj.reshape(L, d * IN_WIDTH)
    view = lambda a: a.reshape(L, d * WIDTH_A)
    scale = HEAD_A ** -0.5

    def body(q_ref, kp_ref, kc_ref, kn_ref, vp_ref, vc_ref, vn_ref, do_ref, l_ref, c_ref, b_ref,
             dq_ref, dk_ref, dv_ref, db_ref):
        r, n = pl.program_id(1), pl.program_id(2)

        @pl.when(n == 0)
        def _():
            dk_ref[...] = jnp.zeros_like(dk_ref)
            dv_ref[...] = jnp.zeros_like(dv_ref)

        @pl.when((n == 0) & (r == 0))
        def _():
            db_ref[...] = jnp.zeros_like(db_ref)

        neg_prev = jnp.where(n > 0, 0.0, NEG_INF)
        neg_next = jnp.where(n < nblk - 1, 0.0, NEG_INF)
        edge = jnp.concatenate([jnp.full((1, QB_A), neg_prev, F32), jnp.zeros((1, QB_A), F32),
                                jnp.full((1, QB_A), neg_next, F32)], axis=1)
        second = _lane_is_second_head((QB_A, W2))
        q2, do2 = q_ref[...], do_ref[...]
        ks = [kp_ref[...], kc_ref[...], kn_ref[...]]
        vs = [vp_ref[...], vc_ref[...], vn_ref[...]]
        lse2, c2 = l_ref[...], c_ref[...]
        dq2 = jnp.zeros((QB_A, W2), F32)
        dks = [jnp.zeros((QB_A, W2), F32) for _ in range(3)]
        dvs = [jnp.zeros((QB_A, W2), F32) for _ in range(3)]
        for hh in range(2):
            mine = second if hh else jnp.logical_not(second)
            zero = jnp.zeros_like(q2)
            qm = jnp.where(mine, q2, zero)
            dom = jnp.where(mine, do2, zero)
            lane = hh * HEAD_A
            lse_h, c_h = lse2[:, lane:lane + 1], c2[:, lane:lane + 1]
            s = jnp.concatenate([lax.dot_general(qm, k, (NT, ((), ())), preferred_element_type=F32) for k in ks], axis=1)
            p = jnp.exp(s * scale + b_ref[hh] + edge - lse_h)
            dp = jnp.concatenate([lax.dot_general(dom, v, (NT, ((), ())), preferred_element_type=F32) for v in vs], axis=1)
            ds = p * (dp + c_h)
            db_ref[hh] += ds
            pb, dsb = p.astype(BF16), (ds * scale).astype(BF16)
            for b in range(3):
                blk = slice(b * QB_A, (b + 1) * QB_A)
                km = jnp.where(mine, ks[b], zero)
                dq2 = dq2 + jnp.dot(dsb[:, blk], km, preferred_element_type=F32)
                dks[b] = dks[b] + lax.dot_general(dsb[:, blk], qm, (TN, ((), ())), preferred_element_type=F32)
                dvs[b] = dvs[b] + lax.dot_general(pb[:, blk], dom, (TN, ((), ())), preferred_element_type=F32)
        dq_ref[...] = dq2.astype(dq_ref.dtype)
        for b, dn in enumerate((-1, 0, 1)):
            start = pl.multiple_of(jnp.clip(n + dn, 0, nblk - 1) * QB_A, QB_A)
            dk_ref[pl.ds(start, QB_A), :] += dks[b]
            dv_ref[pl.ds(start, QB_A), :] += dvs[b]

    def kv_spec(base, dn):
        return _bs((QB_A, W2), lambda hp, r, n: (jnp.clip(n + dn, 0, nblk - 1), r * cb + (base // W2) + grp * ob + hp))

    in_specs = [_bs((QB_A, W2), lambda hp, r, n: (n, r * cb + grp * ob + hp))]
    in_specs += [kv_spec(A_K, dn) for dn in (-1, 0, 1)] + [kv_spec(A_V, dn) for dn in (-1, 0, 1)]
    in_specs += [_bs((QB_A, W2), lambda hp, r, n: (n, r * ob + hp))] * 3
    in_specs += [_bs((2, QB_A, 3 * QB_A), lambda hp, r, n: (hp, 0, 0))]
    out_shape = [jax.ShapeDtypeStruct((L, d * WIDTH_A), BF16), jax.ShapeDtypeStruct((L, d * WIDTH_A), F32),
                 jax.ShapeDtypeStruct((L, d * WIDTH_A), F32), jax.ShapeDtypeStruct((HEADS_A, QB_A, 3 * QB_A), F32)]
    out_specs = [_bs((QB_A, W2), lambda hp, r, n: (n, r * ob + hp)),
                 _bs((L, W2), lambda hp, r, n: (0, r * ob + hp)), _bs((L, W2), lambda hp, r, n: (0, r * ob + hp)),
                 _bs((2, QB_A, 3 * QB_A), lambda hp, r, n: (hp, 0, 0))]
    dq, dk, dv, db = pl.pallas_call(
        body, out_shape=out_shape, grid=(ob, d, nblk), in_specs=in_specs, out_specs=out_specs,
        compiler_params=_params("arbitrary", "arbitrary", "arbitrary"), name=f"a_bwd_d{d}")(
            pv, pv, pv, pv, pv, pv, pv, view(do), view(lse), view(cterm), bias)
    return dq.reshape(T, WIDTH_A), dk.reshape(T, WIDTH_A), dv.reshape(T, WIDTH_A), db


def _segment_ones():
    i = np.arange(WIDTH_A)
    return jnp.asarray((i[:, None] // HEAD_A == i[None, :] // HEAD_A).astype(np.float32), dtype=BF16)


def _group_weights(l0, l1, l2):
    m = jnp.maximum(jnp.maximum(l0, l1), l2)
    e = [jnp.exp(l - m) for l in (l0, l1, l2)]
    z = e[0] + e[1] + e[2]
    return [ei / z for ei in e]


def _combine_fwd(outs, lses):
    T = outs[0].shape[0]

    def fn(o0, o1, o2, l0, l1, l2):
        w = _group_weights(l0, l1, l2)
        return w[0] * o0.astype(F32) + w[1] * o1.astype(F32) + w[2] * o2.astype(F32)

    (oa,) = _ew("a_combine", fn, [_tiled(o) for o in outs] + [_tiled(l) for l in lses], [(BF16, WIDTH_A)], n_rows=T, rows=512)
    return oa


def _combine_bwd(doa, outs, lses):
    T = doa.shape[0]

    def fn(d, o0, o1, o2, l0, l1, l2, seg):
        d = d.astype(F32)
        w = _group_weights(l0, l1, l2)
        tot = jnp.zeros(d.shape, F32)
        for wg, og in zip(w, (o0, o1, o2)):
            prod = wg * d * og.astype(F32)
            hi = prod.astype(BF16)
            lo = (prod - hi.astype(F32)).astype(BF16)
            tot = tot + jnp.dot(hi, seg, preferred_element_type=F32) + jnp.dot(lo, seg, preferred_element_type=F32)
        return tuple(wg * d for wg in w) + tuple(-wg * tot for wg in w)

    res = _ew("a_combine_bwd", fn, [_tiled(doa)] + [_tiled(o) for o in outs] + [_tiled(l) for l in lses] + [_whole(_segment_ones())],
              [(BF16, WIDTH_A)] * 3 + [(F32, WIDTH_A)] * 3, n_rows=T, rows=256)
    return res[:3], res[3:]


def _rope_tables(T):
    rows = T // GRID_W
    row = jnp.repeat(jnp.arange(rows, dtype=F32), GRID_W)
    col = jnp.tile(jnp.arange(GRID_W, dtype=F32), rows)
    n_freq = HEAD_B // 4
    freq = ROPE_THETA ** (-jnp.arange(n_freq, dtype=F32) / n_freq)
    ang = jnp.concatenate([row[:, None] * freq, col[:, None] * freq], axis=-1)
    cos, sin = jnp.repeat(jnp.cos(ang), 2, axis=1), jnp.repeat(jnp.sin(ang), 2, axis=1)
    sign = jnp.where(jnp.arange(HEAD_B) % 2 == 0, -1.0, 1.0).astype(F32)
    return cos, sin * sign


def _swap_pairs(v):
    even = lax.broadcasted_iota(jnp.int32, v.shape, v.ndim - 1) % 2 == 0
    n = v.shape[-1]
    return jnp.where(even, pltpu.roll(v, n - 1, v.ndim - 1), pltpu.roll(v, 1, v.ndim - 1))


def _qk_fwd(name, proj, col0, n_heads, gain, cos, sin):
    T = proj.shape[0]

    def fn(xr, g, c, s):
        xn = _norm_fwd(xr.astype(F32), g)
        return xn * c + _swap_pairs(xn) * s

    (out,) = _ew(name, fn, [_tiled(proj, HEAD_B, col0 // HEAD_B), _whole(gain), _table(cos), _table(sin)],
                 [(BF16, HEAD_B)], n_rows=T, rows=512, ncols=n_heads)
    return out


def _qk_bwd(name, dout, proj, col0, n_heads, gain, cos, sin):
    T = proj.shape[0]

    def fn(dv, xr, g, c, s):
        dv = dv.astype(F32)
        dxn = c * dv + _swap_pairs(s * dv)
        dx, dgr = _norm_bwd(xr.astype(F32), g, dxn)
        return dx, _colsum(dgr)

    dx, dg = _ew(name, fn, [_tiled(dout, HEAD_B, 0), _tiled(proj, HEAD_B, col0 // HEAD_B), _whole(gain),
                            _table(cos), _table(sin)],
                 [(BF16, HEAD_B)], n_rows=T, rows=512, reds=(HEAD_B,), ncols=n_heads)
    return dx, jnp.sum(dg, axis=0)


def _gqa_fwd(qn, kn, proj):
    T = qn.shape[0]
    GW = 4 * HEAD_B
    scale = HEAD_B ** -0.5

    def body(q_ref, k_ref, v_ref, o_ref, l_ref):
        k, v = k_ref[...], v_ref[...]
        lane = lax.broadcasted_iota(jnp.int32, (QB_B, HEAD_B), 1)
        lse_all = jnp.zeros((QB_B, HEAD_B), F32)
        for g in range(4):
            cols = slice(g * HEAD_B, (g + 1) * HEAD_B)
            s = lax.dot_general(q_ref[:, cols], k, (NT, ((), ())), preferred_element_type=F32) * scale
            m = jnp.max(s, axis=-1, keepdims=True)
            p = jnp.exp(s - m)
            l = jnp.sum(p, axis=-1, keepdims=True)
            o = jnp.dot(p.astype(BF16), v, preferred_element_type=F32) / l
            o_ref[:, cols] = o.astype(o_ref.dtype)
            lse_all = jnp.where(lane == g, m + jnp.log(l), lse_all)
        l_ref[...] = lse_all

    return pl.pallas_call(
        body, out_shape=[jax.ShapeDtypeStruct((T, 2 * GW), BF16), jax.ShapeDtypeStruct((2, T, HEAD_B), F32)],
        grid=(2, T // QB_B),
        in_specs=[_bs((QB_B, GW), lambda kv, i: (i, kv)), _bs((T, HEAD_B), lambda kv, i: (0, kv)),
                  _bs((T, HEAD_B), lambda kv, i: (0, B_V // HEAD_B + kv))],
        out_specs=[_bs((QB_B, GW), lambda kv, i: (i, kv)), _bs((None, QB_B, HEAD_B), lambda kv, i: (kv, i, 0))],
        compiler_params=_params("parallel", "parallel"), name="b_fwd")(qn, kn, proj)


def _gqa_bwd(qn, kn, proj, o, lse, do):
    T = qn.shape[0]
    GW = 4 * HEAD_B
    scale = HEAD_B ** -0.5

    def body(q_ref, k_ref, v_ref, o_ref, l_ref, do_ref, dq_ref, dk_ref, dv_ref):
        i = pl.program_id(1)

        @pl.when(i == 0)
        def _():
            dk_ref[...] = jnp.zeros_like(dk_ref)
            dv_ref[...] = jnp.zeros_like(dv_ref)

        k, v = k_ref[...], v_ref[...]
        lse_all = l_ref[...]
        for g in range(4):
            cols = slice(g * HEAD_B, (g + 1) * HEAD_B)
            q, dob = q_ref[:, cols], do_ref[:, cols]
            delta = jnp.sum(dob.astype(F32) * o_ref[:, cols].astype(F32), axis=-1, keepdims=True)
            s = lax.dot_general(q, k, (NT, ((), ())), preferred_element_type=F32) * scale
            p = jnp.exp(s - lse_all[:, g:g + 1])
            dp = lax.dot_general(dob, v, (NT, ((), ())), preferred_element_type=F32)
            ds = (p * (dp - delta) * scale).astype(BF16)
            dq_ref[:, cols] = jnp.dot(ds, k, preferred_element_type=F32).astype(dq_ref.dtype)
            dk_ref[...] += lax.dot_general(ds, q, (TN, ((), ())), preferred_element_type=F32)
            dv_ref[...] += lax.dot_general(p.astype(BF16), dob, (TN, ((), ())), preferred_element_type=F32)

    return pl.pallas_call(
        body, out_shape=[jax.ShapeDtypeStruct((T, 2 * GW), BF16), jax.ShapeDtypeStruct((T, 2 * HEAD_B), F32),
                         jax.ShapeDtypeStruct((T, 2 * HEAD_B), F32)],
        grid=(2, T // QB_B),
        in_specs=[_bs((QB_B, GW), lambda kv, i: (i, kv)), _bs((T, HEAD_B), lambda kv, i: (0, kv)),
                  _bs((T, HEAD_B), lambda kv, i: (0, B_V // HEAD_B + kv)), _bs((QB_B, GW), lambda kv, i: (i, kv)),
                  _bs((None, QB_B, HEAD_B), lambda kv, i: (kv, i, 0)), _bs((QB_B, GW), lambda kv, i: (i, kv))],
        out_specs=[_bs((QB_B, GW), lambda kv, i: (i, kv)), _bs((T, HEAD_B), lambda kv, i: (0, kv)),
                   _bs((T, HEAD_B), lambda kv, i: (0, kv))],
        compiler_params=_params("parallel", "arbitrary"), name="b_bwd")(qn, kn, proj, o, lse, do)


def _local_step(x, target, small, get_w, put_g, deps=()):
    T, D = x.shape
    gs = {}

    x1, ffn1_saved = _ffn_fwd("ffn1", x, small["ffn1_norm"], get_w, deps)
    (h2,) = _ew("mix_norm", lambda xv, g: _norm_fwd(xv, g), [_tiled(x1), _whole(small["mix_norm"])], [(BF16, D)], n_rows=T, rows=512)
    w_in = get_w("w_in", h2)
    proj = _mm_cols("mix_in", h2, w_in, tm=512, tn=512, out_dtype=BF16, cat=True)

    bias = _bias_tiles(small["rel_bias"])
    a_outs, a_lses = [], []
    for grp, d in enumerate(DILATIONS):
        o, l = _dil_fwd(proj, bias[grp * HEADS_A:(grp + 1) * HEADS_A], grp, d)
        a_outs.append(o)
        a_lses.append(l)
    o_a = _combine_fwd(a_outs, a_lses)

    cos, sin = _rope_tables(T)
    qn = _qk_fwd("b_qnorm", proj, B_Q, 8, small["q_norm"], cos, sin)
    kn = _qk_fwd("b_knorm", proj, B_K, 2, small["k_norm"], cos, sin)
    o_b, lse_b = _gqa_fwd(qn, kn, proj)

    wa, wb3, w_out3 = get_w("w_branch_a", o_b), get_w("w_branch_b", o_b).reshape(1, D, D), get_w("w_out", o_b).reshape(1, D, D)
    t_a = _mm_cols("mix_branch_a", o_a, wa, tm=512, tn=256, out_dtype=BF16, cat=True)
    t_b = _mm_cols("mix_branch_b", o_b, wb3, tm=512, tn=512, out_dtype=BF16, cat=True)
    bg_a, bg_b = small["b_gate"][:, :D], small["b_gate"][:, D:]

    def merge(ta, tb, ga, gb_, ba, bb):
        sa, sb = _sigmoid(ga.astype(F32) + ba), _sigmoid(gb_.astype(F32) + bb)
        return sa * ta.astype(F32) + sb * tb.astype(F32)

    gate_ins = [_tiled(proj, D, G_A // D), _tiled(proj, D, G_B // D), _whole(bg_a), _whole(bg_b)]
    (merged,) = _ew("mix_merge", merge, [_tiled(t_a), _tiled(t_b)] + gate_ins, [(BF16, D)], n_rows=T, rows=512)
    x2 = _mm_cols("mix_out", merged, w_out3, tm=512, tn=512, out_dtype=F32, cat=True,
                  extras=[x1], epilogue=lambda acc, xv: xv + acc)
    x3, ffn2_saved = _ffn_fwd("ffn2", x2, small["ffn2_norm"], get_w)

    def head(xv, g, tv):
        r = _rstd(xv)
        xh = xv * r
        e = xh * g - tv
        dy = e * (1.0 / D)
        dxh = dy * g
        dx = r * (dxh - xh * jnp.mean(dxh * xh, axis=-1, keepdims=True))
        return dx, 0.5 * dx, _colsum(e * e) * (0.5 / D), _colsum(dy * xh)

    dx3, dx3_half, loss_cols, g_final = _ew("loss_head", head, [_tiled(x3), _whole(small["final_norm"].reshape(1, D)), _tiled(target)],
                                            [(F32, D), (BF16, D)], n_rows=T, rows=256, reds=(D, D))
    gs["final_norm"] = g_final.reshape(D)

    dx2, _, gs["ffn2_norm"] = _ffn_bwd("ffn2", x2, small["ffn2_norm"], get_w, put_g, ffn2_saved, dx3, dx3_half)

    (dmix,) = _ew("mix_bwd_cast", lambda v: v, [_tiled(dx2)], [(BF16, D)], n_rows=T, rows=512)
    g_out = _mm_wgrad("mix_bwd_dwout", merged, dmix, a_cols=D // 4, b_cols=None, tm=256, tn=512, J=4).reshape(D, D)
    dmerged = _mm_rows_t("mix_bwd_dmerged", dmix, w_out3, tm=512, out_dtype=BF16).reshape(T, D)

    def merge_bwd(dm, ta, tb, ga, gb_, ba, bb):
        dm, ta, tb = dm.astype(F32), ta.astype(F32), tb.astype(F32)
        sa, sb = _sigmoid(ga.astype(F32) + ba), _sigmoid(gb_.astype(F32) + bb)
        dga, dgb = dm * ta * sa * (1.0 - sa), dm * tb * sb * (1.0 - sb)
        return dm * sa, dm * sb, dga, dgb, _colsum(dga), _colsum(dgb)

    dta, dtb, dga, dgb, dba, dbb = _ew("mix_bwd_merge", merge_bwd, [_tiled(dmerged), _tiled(t_a), _tiled(t_b)] + gate_ins,
                                       [(BF16, D)] * 4, n_rows=T, rows=256, reds=(D, D))
    gs["b_gate"] = jnp.concatenate([dba.reshape(1, D), dbb.reshape(1, D)], axis=1)

    g_a = _mm_wgrad("mix_bwd_dwa", o_a, dta, a_cols=None, b_cols=D // 4, tm=WIDTH_A, tn=256, J=4)
    g_b = _mm_wgrad("mix_bwd_dwb", o_b, dtb, a_cols=D // 4, b_cols=None, tm=256, tn=512, J=4).reshape(D, D)
    deps = put_g({"w_out": g_out, "w_branch_a": g_a, "w_branch_b": g_b})
    do_a = _mm("mix_bwd_doa", (T // 512, 4),
               [(dta, _bs((512, D // 4), lambda i, j: (i, j)), wa, _bs((None, WIDTH_A, D // 4), lambda i, j: (j, 0, 0)))],
               jax.ShapeDtypeStruct((T, WIDTH_A), BF16), _bs((512, WIDTH_A), lambda i, j: (i, 0)), NT, reduce_axis=1, deps=deps)
    do_b = _mm_rows_t("mix_bwd_dob", dtb, wb3, tm=512, out_dtype=BF16).reshape(T, D)

    dqn, dkn, dv_b = _gqa_bwd(qn, kn, proj, o_b, lse_b, do_b)
    dq_b, gs["q_norm"] = _qk_bwd("b_bwd_qnorm", dqn, proj, B_Q, 8, small["q_norm"], cos, sin)
    dk_b, gs["k_norm"] = _qk_bwd("b_bwd_knorm", dkn, proj, B_K, 2, small["k_norm"], cos, sin)

    do_groups, c_groups = _combine_bwd(do_a, a_outs, a_lses)
    dqs, dks, dvs, dbs = [], [], [], []
    for grp, d in enumerate(DILATIONS):
        dq, dk, dv, db = _dil_bwd(proj, bias[grp * HEADS_A:(grp + 1) * HEADS_A], do_groups[grp], a_lses[grp], c_groups[grp], grp, d)
        dqs.append(dq), dks.append(dk), dvs.append(dv), dbs.append(db)
    gs["rel_bias"] = _bias_grad(jnp.concatenate(dbs, axis=0))

    dproj = jnp.concatenate([p.astype(BF16) for p in dqs + dks + dvs + [dq_b, dk_b, dv_b, dga, dgb]], axis=1)
    nq = w_in.shape[2]
    deps = put_g({"w_in": _mm_wgrad("mix_bwd_dwin", h2, dproj, a_cols=None, b_cols=nq, tm=512, tn=512, J=4)})
    dh2 = _mm("mix_bwd_dh", (T // 512, D // 512, 4),
              [(dproj, _bs((512, nq), lambda i, k, j: (i, j)), w_in, _bs((None, 512, nq), lambda i, k, j: (j, k, 0)))],
              jax.ShapeDtypeStruct((T, D), F32), _bs((512, 512), lambda i, k, j: (i, k)), NT, reduce_axis=2, deps=deps)

    def nb(xv, gv, dhv, dres):
        dx, dgr = _norm_bwd(xv, gv, dhv)
        dx = dx + dres
        return dx, 0.5 * dx, _colsum(dgr)

    dx1, dx1_half, g_mix = _ew("mix_bwd_norm", nb, [_tiled(x1), _whole(small["mix_norm"]), _tiled(dh2), _tiled(dx2)],
                               [(F32, D), (BF16, D)], n_rows=T, rows=256, reds=(D,))
    gs["mix_norm"] = g_mix.reshape(1, D)

    dx0, _, gs["ffn1_norm"] = _ffn_bwd("ffn1", x, small["ffn1_norm"], get_w, put_g, ffn1_saved, dx1, dx1_half)
    return loss_cols.reshape(1, D), dx0, gs


def _position():
    return lax.axis_index("x"), lax.axis_index("y"), lax.axis_index("c")


def _any_specs(n):
    return [pl.BlockSpec(memory_space=pl.ANY)] * n


HBM_SPEC = pl.BlockSpec(memory_space=pltpu.HBM)
SEM_SPEC = pl.BlockSpec(memory_space=pltpu.SEMAPHORE)
DATAFLOW_EFFECT = pltpu.SideEffectType.DATAFLOW_SIDE_EFFECTING
N_PEER_CHIPS = 3
LANES = 128


def _quarter_copies(srcs, lands, send_sems, recv_sems, scatter):
    x, y, c = _position()
    me = 2 * x + y
    peers = [(1 - x, y, c), (x, 1 - y, c), (1 - x, 1 - y, c)]
    copies = []
    for src, land, send, recv in zip(srcs, lands, send_sems, recv_sems):
        for p, (px, py, pc) in enumerate(peers):
            copies.append(pltpu.make_async_remote_copy(
                src_ref=src.at[2 * px + py] if scatter else src, dst_ref=land.at[me], send_sem=send.at[p], recv_sem=recv.at[p],
                device_id=(px, py, pc), device_id_type=MESH))
    return copies


def _exchange_start(name, srcs, lands, scatter):
    n = len(srcs)

    def body(*refs):
        src_refs, land_refs = refs[:n], refs[n:2 * n]
        send_sems, recv_sems = refs[2 * n:3 * n], refs[3 * n:4 * n]
        token = refs[-1]
        for cp in _quarter_copies(src_refs, land_refs, send_sems, recv_sems, scatter):
            cp.start()
        token[...] = jnp.zeros_like(token)

    sem = pltpu.SemaphoreType.DMA((N_PEER_CHIPS,))
    out_shape = [sem] * (2 * n) + [pltpu.HBM(a.shape, a.dtype) for a in list(srcs) + list(lands)]
    out_shape += [jax.ShapeDtypeStruct((8, LANES), F32)]
    res = pl.pallas_call(
        body, name=name, out_shape=out_shape, in_specs=[HBM_SPEC] * (2 * n),
        out_specs=[SEM_SPEC] * (2 * n) + [HBM_SPEC] * (2 * n) + [pl.BlockSpec(memory_space=pltpu.VMEM)],
        input_output_aliases={i: 2 * n + i for i in range(2 * n)},
        compiler_params=pltpu.CompilerParams(has_side_effects=DATAFLOW_EFFECT),
    )(*[pltpu.with_memory_space_constraint(a, pltpu.HBM) for a in list(srcs) + list(lands)])
    return res[:n], res[n:2 * n], res[2 * n:3 * n], res[3 * n:4 * n], res[4 * n]


def _exchange_wait(name, srcs, lands, send_sems, recv_sems, after, scatter):
    n = len(srcs)

    def body(*refs):
        src_refs, land_refs = refs[:n], refs[n:2 * n]
        sends, recvs = refs[2 * n:3 * n], refs[3 * n:4 * n]
        for cp in _quarter_copies(src_refs, land_refs, sends, recvs, scatter):
            cp.wait_send()
            cp.wait_recv()

    res = pl.pallas_call(
        body, name=name, out_shape=[pltpu.HBM(a.shape, a.dtype) for a in list(srcs) + list(lands)],
        in_specs=[HBM_SPEC] * (2 * n) + [SEM_SPEC] * (2 * n) + [pl.BlockSpec(memory_space=pl.ANY)],
        out_specs=[HBM_SPEC] * (2 * n), input_output_aliases={i: i for i in range(2 * n)},
        compiler_params=pltpu.CompilerParams(has_side_effects=DATAFLOW_EFFECT),
    )(*srcs, *lands, *send_sems, *recv_sems, after)
    return res[n:]


def _own_slot(stack_shape, own, dtype):
    me = 2 * lax.axis_index("x") + lax.axis_index("y")
    return lax.dynamic_update_slice(jnp.zeros(stack_shape, dtype), own[None], (me,) + (0,) * own.ndim)


def _swap_with_sibling(parts):
    n = len(parts)

    def body(*refs):
        ins, outs = refs[:n], refs[n:2 * n]
        send_sems, recv_sems = refs[2 * n:]
        x, y, c = _position()
        copies = []
        for i in range(n):
            cp = pltpu.make_async_remote_copy(ins[i], outs[i], send_sems.at[i], recv_sems.at[i],
                                              device_id=(x, y, 1 - c), device_id_type=MESH)
            cp.start()
            copies.append(cp)
        for cp in copies:
            cp.wait()

    return pl.pallas_call(
        body, out_shape=[jax.ShapeDtypeStruct(s.shape, s.dtype) for s in parts],
        in_specs=_any_specs(n), out_specs=_any_specs(n),
        scratch_shapes=[pltpu.SemaphoreType.DMA((n,)), pltpu.SemaphoreType.DMA((n,))],
        compiler_params=pltpu.CompilerParams(has_side_effects=True), name="swap_with_sibling")(*parts)


def _allreduce_small(buf):
    R, C = buf.shape
    flips = [(fx, fy, fc) for fx in (0, 1) for fy in (0, 1) for fc in (0, 1)][1:]

    def body(in_ref, out_ref, land_ref, send_sems, recv_sems):
        x, y, c = _position()
        me = 4 * x + 2 * y + c
        copies = []
        for k, (fx, fy, fc) in enumerate(flips):
            px, py, pc = (1 - x if fx else x), (1 - y if fy else y), (1 - c if fc else c)
            cp = pltpu.make_async_remote_copy(in_ref, land_ref.at[me], send_sems.at[k], recv_sems.at[k],
                                              device_id=(px, py, pc), device_id_type=MESH)
            cp.start()
            copies.append(cp)
        land_ref[me] = in_ref[...]
        for cp in copies:
            cp.wait()
        acc = land_ref[0]
        for k in range(1, 8):
            acc = acc + land_ref[k]
        out_ref[...] = acc

    return pl.pallas_call(
        body, out_shape=jax.ShapeDtypeStruct((R, C), F32),
        in_specs=[pl.BlockSpec(memory_space=pltpu.VMEM)], out_specs=pl.BlockSpec(memory_space=pltpu.VMEM),
        scratch_shapes=[pltpu.VMEM((8, R, C), F32), pltpu.SemaphoreType.DMA((7,)), pltpu.SemaphoreType.DMA((7,))],
        compiler_params=pltpu.CompilerParams(has_side_effects=True), name="allreduce_small")(buf)


def _adamw_math(w, g, m, v):
    m2 = ADAM_B1 * m + (1.0 - ADAM_B1) * g
    v2 = ADAM_B2 * v + (1.0 - ADAM_B2) * (g * g)
    m_hat = m2 / (1.0 - ADAM_B1 ** ADAM_STEP)
    v_hat = v2 / (1.0 - ADAM_B2 ** ADAM_STEP)
    delta = -ADAM_LR * (m_hat / (jnp.sqrt(v_hat) + ADAM_EPS) + ADAM_WD * w)
    return delta, m2, v2


def _adamw_big(name, w, m, v, part_mine, part_sibling):
    R, C = w.shape
    rows = 256 if R % 256 == 0 else R // 2 if (R // 2) % 8 == 0 else R

    def fn(wv, mv, vv, a, b):
        g = a + b
        return (g,) + _adamw_math(wv, g, mv, vv)

    return _ew(name, fn, [_tiled(w), _tiled(m), _tiled(v), _tiled(part_mine), _tiled(part_sibling)], [(F32, C)] * 4, n_rows=R, rows=rows)


def _sum_four(name, stack):
    _, R, C = stack.shape
    rows = 256 if R % 256 == 0 else R // 2 if (R // 2) % 8 == 0 else R
    flat = stack.reshape(4 * R, C)
    nrb = R // rows

    def fn(a, b, c, d):
        return ((a.astype(F32) + b.astype(F32)) + c.astype(F32)) + d.astype(F32)

    (out,) = _ew(name, fn, [_tiled(flat, None, 0, k * nrb) for k in range(4)], [(F32, C)], n_rows=R, rows=rows)
    return out


BIG = ("ffn1_w1", "ffn1_w3", "ffn1_w2", "w_in", "w_branch_a", "w_branch_b", "w_out", "ffn2_w1", "ffn2_w3", "ffn2_w2")
SMALL = ("ffn1_norm", "mix_norm", "b_gate", "q_norm", "k_norm", "rel_bias", "ffn2_norm", "final_norm")
ORDER = ("ffn1_norm", "ffn1_w1", "ffn1_w3", "ffn1_w2", "mix_norm", "w_in", "b_gate", "q_norm", "k_norm", "rel_bias",
         "w_branch_a", "w_branch_b", "w_out", "ffn2_norm", "ffn2_w1", "ffn2_w3", "ffn2_w2", "final_norm")
GATHER_GROUPS = (("ffn1_w1", "ffn1_w3"), ("ffn1_w2",), ("w_in",), ("w_branch_a", "w_branch_b", "w_out"),
                 ("ffn2_w1", "ffn2_w3", "ffn2_w2"))


def _pack_small(d):
    rows = []
    for n in SMALL:
        flat = d[n].reshape(-1)
        pad = (-flat.shape[0]) % LANES
        rows.append(jnp.pad(flat, (0, pad)).reshape(-1, LANES))
    buf = jnp.concatenate(rows, axis=0)
    return jnp.pad(buf, ((0, (-buf.shape[0]) % 8), (0, 0)))


def _unpack_small(buf, like):
    out, r = {}, 0
    for n in SMALL:
        size = like[n].size
        nr = -(-size // LANES)
        out[n] = buf[r:r + nr].reshape(-1)[:size].reshape(like[n].shape)
        r += nr
    return out


def kernel(x, ffn1_norm, ffn1_w1, ffn1_w3, ffn1_w2, mix_norm, w_in, b_gate, q_norm, k_norm, rel_bias, w_branch_a, w_branch_b, w_out, ffn2_norm, ffn2_w1, ffn2_w3, ffn2_w2, final_norm, loss_target, m_ffn1_norm, m_ffn1_w1, m_ffn1_w3, m_ffn1_w2, m_mix_norm, m_w_in, m_b_gate, m_q_norm, m_k_norm, m_rel_bias, m_w_branch_a, m_w_branch_b, m_w_out, m_ffn2_norm, m_ffn2_w1, m_ffn2_w3, m_ffn2_w2, m_final_norm, v_ffn1_norm, v_ffn1_w1, v_ffn1_w3, v_ffn1_w2, v_mix_norm, v_w_in, v_b_gate, v_q_norm, v_k_norm, v_rel_bias, v_w_branch_a, v_w_branch_b, v_w_out, v_ffn2_norm, v_ffn2_w1, v_ffn2_w3, v_ffn2_w2, v_final_norm):
    given = dict(locals())
    w = {n: given[n] for n in ORDER}
    m = {n: given["m_" + n] for n in ORDER}
    v = {n: given["v_" + n] for n in ORDER}
    T, D = x.shape[1], x.shape[2]

    quarter = {n: w[n].reshape(w[n].shape[1:]) for n in BIG}
    q16 = [quarter[n].astype(BF16) for n in BIG]
    send, recv, src_thru, land_thru, token = _exchange_start(
        "gather_start", q16, [_own_slot((4,) + q.shape, q, BF16) for q in q16], scatter=False)
    index = {n: i for i, n in enumerate(BIG)}
    ready = {}

    def get_w(name, after):
        if name not in ready:
            group = next(g for g in GATHER_GROUPS if name in g)
            ids = [index[n] for n in group]
            stacks = _exchange_wait("gather_wait_" + group[0], [src_thru[i] for i in ids], [land_thru[i] for i in ids],
                                    [send[i] for i in ids], [recv[i] for i in ids], after, scatter=False)
            for n, st in zip(group, stacks):
                ready[n] = st.reshape(D, D) if n in ("w_branch_b", "w_out") else st
        return ready[name]

    me = 2 * lax.axis_index("x") + lax.axis_index("y")
    in_flight = []

    def put_g(grads):
        names = list(grads)
        stacks = [grads[n].reshape((4,) + quarter[n].shape) for n in names]
        lands = [_own_slot(s.shape, lax.dynamic_index_in_dim(s, me, 0, keepdims=False), BF16) for s in stacks]
        started = _exchange_start("scatter_start_" + names[0], stacks, lands, scatter=True)
        in_flight.append((names,) + tuple(started[:4]))
        return [started[4]]

    small = {n: w[n] for n in SMALL}
    loss_cols, grad_x, gs = _local_step(x.reshape(T, D), loss_target.reshape(T, D), small, get_w, put_g, deps=[token])
    loss = lax.psum(jnp.sum(loss_cols), ("x", "y", "c"))

    landed = {}
    for names, s_sem, r_sem, srcs, lands in in_flight:
        got = _exchange_wait("scatter_wait_" + names[0], srcs, lands, s_sem, r_sem, grad_x, scatter=True)
        landed.update(zip(names, got))
    partial = [_sum_four(f"sum4_{n}", landed[n]) for n in BIG]
    other = _swap_with_sibling(partial)
    grads, deltas, new_m, new_v = {}, {}, {}, {}
    for n, mine, theirs in zip(BIG, partial, other):
        shp = w[n].shape
        two_d = quarter[n].shape
        res = _adamw_big(f"adamw_{n}", quarter[n], m[n].reshape(two_d), v[n].reshape(two_d), mine, theirs)
        grads[n], deltas[n], new_m[n], new_v[n] = [r.reshape(shp) for r in res]

    gs = {n: gs[n].reshape(w[n].shape) for n in SMALL}
    g_small = _allreduce_small(_pack_small(gs))
    packed = [_pack_small({n: d[n] for n in SMALL}) for d in (w, m, v)]
    R = g_small.shape[0]
    res = _ew("adamw_small", lambda wv, mv, vv, g: (g,) + _adamw_math(wv, g, mv, vv),
              [_tiled(packed[0]), _tiled(packed[1]), _tiled(packed[2]), _tiled(g_small)], [(F32, LANES)] * 4, n_rows=R, rows=R)
    for d, buf in zip((grads, deltas, new_m, new_v), res):
        d.update(_unpack_small(buf, w))

    return (loss, grad_x.reshape(x.shape), *[grads[n] for n in ORDER], *[deltas[n] for n in ORDER],
            *[new_m[n] for n in ORDER], *[new_v[n] for n in ORDER])
```

```python
import functools
import math

import numpy as np
import jax
import jax.numpy as jnp
from jax import lax
from jax.experimental import pallas as pl
from jax.experimental.pallas import tpu as pltpu

F32 = jnp.float32
BF16 = jnp.bfloat16
MESH = pl.DeviceIdType.MESH

NEG_INF = -1e30
EPS = 1e-6
GRID_W = 64
ROPE_THETA = 10000.0
DILATIONS = (1, 4, 16)
BAND_HALF = 64
HEAD_A = 64
HEADS_A = 8
WIDTH_A = HEADS_A * HEAD_A
HEAD_B = 128
N_BUCKETS = 32
MAX_DISTANCE = 1024
ADAM_LR, ADAM_B1, ADAM_B2, ADAM_EPS, ADAM_WD, ADAM_STEP = 0.001, 0.9, 0.999, 1e-08, 0.01, 10

A_Q, A_K, A_V = 0, 1536, 3072
B_Q, B_K, B_V = 4608, 5632, 5888
G_A, G_B = 6144, 7168
IN_WIDTH = 8192

VMEM_LIMIT_BYTES = 56 * 1024 * 1024
QB_A = 128
QB_B = 256


def _params(*sem):
    return pltpu.CompilerParams(dimension_semantics=sem, vmem_limit_bytes=VMEM_LIMIT_BYTES)


def _bs(shape, fn):
    return pl.BlockSpec(shape, fn)


def _mm(name, grid, pairs, out_shape, out_spec, dims, *, reduce_axis=None, extras=(), epilogue=None, deps=()):
    n_pairs, n_extra, n_deps = len(pairs), len(extras), len(deps)
    operands = [p[0] for p in pairs] + [p[2] for p in pairs] + [e[0] for e in extras] + list(deps)
    in_specs = [p[1] for p in pairs] + [p[3] for p in pairs] + [e[1] for e in extras] + _any_specs(n_deps)
    tile = tuple(s for s in out_spec.block_shape if s is not None)
    n_steps = grid[reduce_axis] if reduce_axis is not None else 1

    def body(*refs):
        a_refs, b_refs = refs[:n_pairs], refs[n_pairs:2 * n_pairs]
        e_refs = refs[2 * n_pairs:2 * n_pairs + n_extra]
        o_ref = refs[2 * n_pairs + n_extra + n_deps]
        acc = None
        for a_ref, b_ref in zip(a_refs, b_refs):
            t = lax.dot_general(a_ref[...], b_ref[...], (dims, ((), ())), preferred_element_type=F32)
            acc = t if acc is None else acc + t

        def finish(v):
            if epilogue is not None:
                v = epilogue(v, *[e[...] for e in e_refs])
            o_ref[...] = v.astype(o_ref.dtype)

        if reduce_axis is None:
            finish(acc)
        else:
            acc_ref = refs[-1]
            k = pl.program_id(reduce_axis)

            @pl.when(k == 0)
            def _():
                acc_ref[...] = acc

            @pl.when(k > 0)
            def _():
                acc_ref[...] += acc

            @pl.when(k == n_steps - 1)
            def _():
                finish(acc_ref[...])

    sem = ["parallel"] * len(grid)
    if reduce_axis is not None:
        sem[reduce_axis] = "arbitrary"
    return pl.pallas_call(
        body, out_shape=out_shape, grid=grid, in_specs=in_specs, out_specs=out_spec,
        scratch_shapes=[pltpu.VMEM(tile, F32)] if reduce_axis is not None else [],
        compiler_params=_params(*sem), name=name)(*operands)


NN = ((1,), (0,))
NT = ((1,), (1,))
TN = ((0,), (0,))


def _mm_cols(name, a, w, *, tm, tn, out_dtype, cat, extras=(), epilogue=None):
    M, K = a.shape
    J, _, n = w.shape
    tn = min(tn, n)
    nb = n // tn
    if cat:
        shape, spec = (M, J * n), _bs((tm, tn), lambda j, i, k: (i, j * nb + k))
    else:
        shape, spec = (J, M, n), _bs((None, tm, tn), lambda j, i, k: (j, i, k))
    ex = [(e, _bs((tm, tn), lambda j, i, k: (i, j * nb + k))) for e in extras]
    return _mm(name, (J, M // tm, nb),
               [(a, _bs((tm, K), lambda j, i, k: (i, 0)), w, _bs((None, K, tn), lambda j, i, k: (j, 0, k)))],
               jax.ShapeDtypeStruct(shape, out_dtype), spec, NN, extras=ex, epilogue=epilogue)


def _mm_rows_t(name, a, w, *, tm, out_dtype):
    M, N = a.shape
    J, f, _ = w.shape
    return _mm(name, (J, M // tm),
               [(a, _bs((tm, N), lambda j, i: (i, 0)), w, _bs((None, f, N), lambda j, i: (j, 0, 0)))],
               jax.ShapeDtypeStruct((J, M, f), out_dtype), _bs((None, tm, f), lambda j, i: (j, i, 0)), NT)


def _mm_wgrad(name, a, b, *, a_cols, b_cols, tm, tn, J):
    def pick(arr, cols, t):
        if arr.ndim == 3:
            T, c = arr.shape[1], arr.shape[2]
            t = min(t, c)
            return T, c, t, (lambda sel: _bs((None, T, t), lambda j, i, k: (j, 0, sel(i, k))))
        T = arr.shape[0]
        c = arr.shape[1] if cols is None else cols
        t = min(t, c)
        per = c // t
        if cols is None:
            return T, c, t, (lambda sel: _bs((T, t), lambda j, i, k: (0, sel(i, k))))
        return T, c, t, (lambda sel: _bs((T, t), lambda j, i, k: (0, j * per + sel(i, k))))
    _, ca, tm, mk_a = pick(a, a_cols, tm)
    _, cb, tn, mk_b = pick(b, b_cols, tn)
    return _mm(name, (J, ca // tm, cb // tn),
               [(a, mk_a(lambda i, k: i), b, mk_b(lambda i, k: k))],
               jax.ShapeDtypeStruct((J, ca, cb), BF16), _bs((None, tm, tn), lambda j, i, k: (j, i, k)), TN)


def _tiled(arr, width=None, col=0, rowblk=0):
    return ("t", arr, arr.shape[1] if width is None else width, col, rowblk)


def _table(arr):
    return ("f", arr)


def _whole(arr):
    return ("w", arr)


def _ew(name, fn, ins, outs, *, n_rows, rows, reds=(), ncols=1, deps=()):
    nrb = n_rows // rows
    n_deps = len(deps)
    operands, in_specs = [], []
    for spec in ins:
        if spec[0] == "t":
            _, arr, width, col, rowblk = spec
            step = 1 if ncols > 1 else 0
            in_specs.append(_bs((rows, width), lambda c, i, col=col, rowblk=rowblk, step=step: (rowblk + i, col + c * step)))
        elif spec[0] == "f":
            arr = spec[1]
            in_specs.append(_bs((rows, arr.shape[1]), lambda c, i: (i, 0)))
        else:
            arr = spec[1]
            nd = arr.ndim
            if nd == 3:
                in_specs.append(_bs((None,) + arr.shape[1:], lambda c, i: (c, 0, 0)))
            else:
                in_specs.append(_bs(arr.shape, lambda c, i, nd=nd: (0,) * nd))
        operands.append(arr)
    out_shapes = [jax.ShapeDtypeStruct((n_rows, ncols * w), dt) for dt, w in outs]
    out_specs = [_bs((rows, w), lambda c, i: (i, c)) for _, w in outs]
    out_shapes += [jax.ShapeDtypeStruct((ncols, 1, w), F32) for w in reds]
    out_specs += [_bs((None, 1, w), lambda c, i: (c, 0, 0)) for w in reds]
    n_in, n_out, n_red = len(ins), len(outs), len(reds)
    operands += list(deps)
    in_specs += _any_specs(n_deps)

    def body(*refs):
        vals = fn(*[r[...] for r in refs[:n_in]])
        if not isinstance(vals, (tuple, list)):
            vals = (vals,)
        o_refs = refs[n_in + n_deps:]
        for o_ref, v in zip(o_refs[:n_out], vals[:n_out]):
            o_ref[...] = v.astype(o_ref.dtype)
        if n_red:
            i = pl.program_id(1)
            for r_ref, v in zip(o_refs[n_out:], vals[n_out:]):
                @pl.when(i == 0)
                def _(r_ref=r_ref):
                    r_ref[...] = jnp.zeros_like(r_ref)
                r_ref[...] += v

    res = pl.pallas_call(
        body, out_shape=out_shapes, grid=(ncols, nrb), in_specs=in_specs, out_specs=out_specs,
        compiler_params=_params("parallel", "arbitrary" if n_red else "parallel"), name=name)(*operands)
    return res


def _colsum(v):
    return jnp.sum(v, axis=0, keepdims=True)


def _rstd(x):
    return lax.rsqrt(jnp.mean(x * x, axis=-1, keepdims=True) + EPS)


def _sigmoid(x):
    return 1.0 / (1.0 + jnp.exp(-x))


def _norm_fwd(x, g):
    return x * _rstd(x) * g


def _norm_bwd(x, g, dy):
    r = _rstd(x)
    xh = x * r
    dxh = dy * g
    dx = r * (dxh - xh * jnp.mean(dxh * xh, axis=-1, keepdims=True))
    return dx, dy * xh


def _ffn_fwd(tag, x, gain, get_w, deps=()):
    T, D = x.shape
    (h,) = _ew(f"{tag}_norm", lambda xv, g: _norm_fwd(xv, g), [_tiled(x), _whole(gain)], [(BF16, D)], n_rows=T, rows=512,
               deps=deps)
    w1, w3 = get_w(f"{tag}_w1", h), get_w(f"{tag}_w3", h)
    J, _, f = w1.shape
    u = _mm_cols(f"{tag}_up1", h, w1, tm=1024, tn=f, out_dtype=BF16, cat=False)
    g = _mm_cols(f"{tag}_up3", h, w3, tm=1024, tn=f, out_dtype=BF16, cat=False)

    def act(uv, gv):
        uv, gv = uv.astype(F32), gv.astype(F32)
        return uv * _sigmoid(uv) * gv

    (a,) = _ew(f"{tag}_act", act, [_tiled(u.reshape(J * T, f)), _tiled(g.reshape(J * T, f))], [(BF16, f)],
               n_rows=J * T, rows=512)
    a = a.reshape(J, T, f)
    w2 = get_w(f"{tag}_w2", a)
    y = _mm(f"{tag}_down", (T // 1024, D // 512),
            [(a, _bs((None, 1024, f), lambda i, k, j=j: (j, i, 0)), w2, _bs((None, f, 512), lambda i, k, j=j: (j, 0, k)))
             for j in range(J)],
            jax.ShapeDtypeStruct((T, D), F32), _bs((1024, 512), lambda i, k: (i, k)), NN,
            extras=[(x, _bs((1024, 512), lambda i, k: (i, k)))], epilogue=lambda acc, xv: xv + 0.5 * acc)
    return y, (h, u, g, a)


def _ffn_bwd(tag, x, gain, get_w, put_g, saved, dy, dy_half):
    h, u, g, a = saved
    T, D = x.shape
    w1, w3, w2 = [get_w(f"{tag}_{n}", dy_half) for n in ("w1", "w3", "w2")]
    J, _, f = w1.shape
    da = _mm_rows_t(f"{tag}_bwd_da", dy_half, w2, tm=1024, out_dtype=BF16)
    dw2 = _mm_wgrad(f"{tag}_bwd_dw2", a, dy_half, a_cols=None, b_cols=None, tm=f, tn=512, J=J)

    def act_bwd(uv, gv, dav):
        uv, gv, dav = uv.astype(F32), gv.astype(F32), dav.astype(F32)
        s = _sigmoid(uv)
        return dav * gv * (s * (1.0 + uv * (1.0 - s))), dav * (uv * s)

    du, dg = _ew(f"{tag}_bwd_act", act_bwd,
                 [_tiled(u.reshape(J * T, f)), _tiled(g.reshape(J * T, f)), _tiled(da.reshape(J * T, f))],
                 [(BF16, f), (BF16, f)], n_rows=J * T, rows=512)
    du, dg = du.reshape(J, T, f), dg.reshape(J, T, f)
    dw1 = _mm_wgrad(f"{tag}_bwd_dw1", h, du, a_cols=None, b_cols=None, tm=512, tn=f, J=J)
    dw3 = _mm_wgrad(f"{tag}_bwd_dw3", h, dg, a_cols=None, b_cols=None, tm=512, tn=f, J=J)
    deps = put_g({f"{tag}_w1": dw1, f"{tag}_w3": dw3, f"{tag}_w2": dw2})
    pairs = []
    for j in range(J):
        a_spec = _bs((None, 512, f), lambda i, k, j=j: (j, i, 0))
        w_spec = _bs((None, 512, f), lambda i, k, j=j: (j, k, 0))
        pairs += [(du, a_spec, w1, w_spec), (dg, a_spec, w3, w_spec)]
    dh = _mm(f"{tag}_bwd_dh", (T // 512, D // 512), pairs,
             jax.ShapeDtypeStruct((T, D), F32), _bs((512, 512), lambda i, k: (i, k)), NT, deps=deps)

    def nb(xv, gv, dhv, dres):
        dx, dgr = _norm_bwd(xv, gv, dhv)
        dx = dx + dres
        return dx, 0.5 * dx, _colsum(dgr)

    dx, dx_half, dgain = _ew(f"{tag}_bwd_norm", nb, [_tiled(x), _whole(gain), _tiled(dh), _tiled(dy)],
                             [(F32, D), (BF16, D)], n_rows=T, rows=256, reds=(D,))
    return dx, dx_half, dgain.reshape(1, D)


def _t5_bucket(rel):
    n = N_BUCKETS // 2
    max_exact = n // 2
    ret = jnp.where(rel > 0, n, 0)
    a = jnp.abs(rel)
    af = jnp.maximum(a, 1).astype(F32)
    large = max_exact + (jnp.log(af / max_exact) / math.log(MAX_DISTANCE / max_exact) * (n - max_exact)).astype(jnp.int32)
    large = jnp.minimum(large, n - 1)
    return ret + jnp.where(a < max_exact, a, large)


def _band_steps():
    qi = jnp.arange(QB_A, dtype=jnp.int32)[:, None]
    kj = jnp.arange(3 * QB_A, dtype=jnp.int32)[None, :] - QB_A
    return kj - qi


def _bias_tiles(rel_bias):
    steps = _band_steps()
    buckets = jnp.stack([_t5_bucket(steps * d) for d in DILATIONS])
    inband = (jnp.abs(steps) <= BAND_HALF).astype(jnp.int32)
    n_heads = rel_bias.shape[1]

    def body(tab_ref, b_ref, m_ref, o_ref):
        hd = pl.program_id(0)
        bkt = b_ref[...]
        acc = jnp.zeros(bkt.shape, F32)
        for b in range(N_BUCKETS):
            acc = jnp.where(bkt == b, tab_ref[b, hd], acc)
        o_ref[...] = jnp.where(m_ref[...] > 0, acc, NEG_INF)

    return pl.pallas_call(
        body, out_shape=jax.ShapeDtypeStruct((n_heads, QB_A, 3 * QB_A), F32), grid=(n_heads,),
        in_specs=[pl.BlockSpec(memory_space=pltpu.SMEM),
                  _bs((None, QB_A, 3 * QB_A), lambda hd: (hd // HEADS_A, 0, 0)),
                  _bs((QB_A, 3 * QB_A), lambda hd: (0, 0))],
        out_specs=_bs((None, QB_A, 3 * QB_A), lambda hd: (hd, 0, 0)),
        compiler_params=_params("parallel"), name="a_bias_tiles")(rel_bias, buckets, inband)


def _bias_grad(dbias):
    steps = np.arange(3 * QB_A)[None, :] - QB_A - np.arange(QB_A)[:, None]
    inband = np.abs(steps) <= BAND_HALF
    present = []
    for d in DILATIONS:
        rel = steps * d
        a = np.abs(rel)
        large = 8 + (np.log(np.maximum(a, 1) / 8.0) / math.log(MAX_DISTANCE / 8.0) * 8).astype(np.int64)
        bk = np.where(rel > 0, 16, 0) + np.where(a < 8, a, np.minimum(large, 15))
        present.append(sorted(set(bk[inband].tolist())))
    buckets = jnp.stack([_t5_bucket(_band_steps() * d) for d in DILATIONS])
    n_heads = dbias.shape[0]

    def body(b_ref, d_ref, o_ref):
        row = lax.broadcasted_iota(jnp.int32, (N_BUCKETS, n_heads), 0)
        col = lax.broadcasted_iota(jnp.int32, (N_BUCKETS, n_heads), 1)
        out = jnp.zeros((N_BUCKETS, n_heads), F32)
        for grp in range(len(DILATIONS)):
            bkt = b_ref[grp]
            for hh in range(HEADS_A):
                hd = grp * HEADS_A + hh
                ds = d_ref[hd]
                for b in present[grp]:
                    tot = jnp.sum(jnp.where(bkt == b, ds, 0.0))
                    out = jnp.where((row == b) & (col == hd), tot, out)
        o_ref[...] = out

    return pl.pallas_call(
        body, out_shape=jax.ShapeDtypeStruct((N_BUCKETS, n_heads), F32),
        compiler_params=pltpu.CompilerParams(vmem_limit_bytes=VMEM_LIMIT_BYTES), name="a_bias_grad")(buckets, dbias)


def _lane_is_second_head(shape):
    return lax.broadcasted_iota(jnp.int32, shape, len(shape) - 1) >= HEAD_A


def _group_view(proj, grp, d):
    T = proj.shape[0]
    if d == 1:
        return proj, IN_WIDTH, grp * 3 * WIDTH_A
    part = proj[:, grp * 3 * WIDTH_A:(grp + 1) * 3 * WIDTH_A]
    return part.reshape(T // d, d * 3 * WIDTH_A), 3 * WIDTH_A, 0


def _stack_heads(v2, second):
    zero = jnp.zeros_like(v2)
    return jnp.concatenate([jnp.where(second, zero, v2), jnp.where(second, v2, zero)], axis=0)


def _unstack_heads(v, second):
    return jnp.where(second, v[QB_A:], v[:QB_A])


def _edge_mask(n, nblk):
    neg_prev = jnp.where(n > 0, 0.0, NEG_INF)
    neg_next = jnp.where(n < nblk - 1, 0.0, NEG_INF)
    return jnp.concatenate([jnp.full((1, QB_A), neg_prev, F32), jnp.zeros((1, QB_A), F32),
                            jnp.full((1, QB_A), neg_next, F32)], axis=1)


def _dil_fwd(proj, bias, grp, d):
    T = proj.shape[0]
    L = T // d
    nblk = L // QB_A
    pv, width, base = _group_view(proj, grp, d)
    cb, b0 = width // WIDTH_A, base // WIDTH_A
    W2 = 2 * HEAD_A
    scale = HEAD_A ** -0.5

    def body(q_ref, kp_ref, kc_ref, kn_ref, vp_ref, vc_ref, vn_ref, b_ref, o_ref, l_ref):
        edge = _edge_mask(pl.program_id(1), nblk)
        second = _lane_is_second_head((QB_A, W2))
        for hp in range(HEADS_A // 2):
            cols = slice(hp * W2, (hp + 1) * W2)
            kcat = jnp.concatenate([kp_ref[:, cols], kc_ref[:, cols], kn_ref[:, cols]], axis=0)
            vcat = jnp.concatenate([vp_ref[:, cols], vc_ref[:, cols], vn_ref[:, cols]], axis=0)
            qs = _stack_heads(q_ref[:, cols], second)
            s = lax.dot_general(qs, kcat, (NT, ((), ())), preferred_element_type=F32)
            s = s * scale + b_ref[2 * hp:2 * hp + 2].reshape(2 * QB_A, 3 * QB_A) + edge
            m = jnp.max(s, axis=-1, keepdims=True)
            p = jnp.exp(s - m)
            l = jnp.sum(p, axis=-1, keepdims=True)
            res = jnp.dot(p.astype(BF16), vcat, preferred_element_type=F32) / l
            o_ref[:, cols] = _unstack_heads(res, second).astype(o_ref.dtype)
            l_ref[:, cols] = _unstack_heads(jnp.broadcast_to(m + jnp.log(l), (2 * QB_A, W2)), second)

    def spec(part, dn):
        return _bs((QB_A, WIDTH_A), lambda r, n: (jnp.clip(n + dn, 0, nblk - 1), r * cb + b0 + part))

    in_specs = [spec(0, 0)] + [spec(1, dn) for dn in (-1, 0, 1)] + [spec(2, dn) for dn in (-1, 0, 1)]
    in_specs += [_bs((HEADS_A, QB_A, 3 * QB_A), lambda r, n: (0, 0, 0))]
    o, lse = pl.pallas_call(
        body, out_shape=[jax.ShapeDtypeStruct((L, d * WIDTH_A), BF16), jax.ShapeDtypeStruct((L, d * WIDTH_A), F32)],
        grid=(d, nblk), in_specs=in_specs,
        out_specs=[_bs((QB_A, WIDTH_A), lambda r, n: (n, r)), _bs((QB_A, WIDTH_A), lambda r, n: (n, r))],
        compiler_params=_params("parallel", "parallel"), name=f"a_fwd_d{d}")(pv, pv, pv, pv, pv, pv, pv, bias)
    return o.reshape(T, WIDTH_A), lse.reshape(T, WIDTH_A)


def _dil_bwd(proj, bias, do, lse, cterm, grp, d):
    T = proj.shape[0]
    L = T // d
    nblk = L // QB_A
    W2 = 2 * HEAD_A
    pv, width, base = _group_view(proj, grp, d)
    cb, b0 = width // W2, base // W2
    ob = WIDTH_A // W2
    view = lambda a: a.reshape(L, d * WIDTH_A)
    scale = HEAD_A ** -0.5

    def body(q_ref, kp_ref, kc_ref, kn_ref, vp_ref, vc_ref, vn_ref, do_ref, l_ref, c_ref, b_ref,
             dq_ref, dk_ref, dv_ref, db_ref):
        r, n = pl.program_id(1), pl.program_id(2)

        @pl.when(n == 0)
        def _():
            dk_ref[...] = jnp.zeros_like(dk_ref)
            dv_ref[...] = jnp.zeros_like(dv_ref)

        @pl.when((n == 0) & (r == 0))
        def _():
            db_ref[...] = jnp.zeros_like(db_ref)

        second = _lane_is_second_head((QB_A, W2))
        kcat = jnp.concatenate([kp_ref[...], kc_ref[...], kn_ref[...]], axis=0)
        vcat = jnp.concatenate([vp_ref[...], vc_ref[...], vn_ref[...]], axis=0)
        qs, dos = _stack_heads(q_ref[...], second), _stack_heads(do_ref[...], second)
        lse2, c2 = l_ref[...], c_ref[...]
        lse_rows = jnp.concatenate([lse2[:, 0:1], lse2[:, HEAD_A:HEAD_A + 1]], axis=0)
        c_rows = jnp.concatenate([c2[:, 0:1], c2[:, HEAD_A:HEAD_A + 1]], axis=0)
        s = lax.dot_general(qs, kcat, (NT, ((), ())), preferred_element_type=F32)
        p = jnp.exp(s * scale + b_ref[...].reshape(2 * QB_A, 3 * QB_A) + _edge_mask(n, nblk) - lse_rows)
        dp = lax.dot_general(dos, vcat, (NT, ((), ())), preferred_element_type=F32)
        ds = p * (dp + c_rows)
        db_ref[...] += ds.reshape(2, QB_A, 3 * QB_A)
        pb, dsb = p.astype(BF16), (ds * scale).astype(BF16)
        dq_ref[...] = _unstack_heads(jnp.dot(dsb, kcat, preferred_element_type=F32), second).astype(dq_ref.dtype)
        dkc = lax.dot_general(dsb, qs, (TN, ((), ())), preferred_element_type=F32)
        dvc = lax.dot_general(pb, dos, (TN, ((), ())), preferred_element_type=F32)
        for b, dn in enumerate((-1, 0, 1)):
            start = pl.multiple_of(jnp.clip(n + dn, 0, nblk - 1) * QB_A, QB_A)
            dk_ref[pl.ds(start, QB_A), :] += dkc[b * QB_A:(b + 1) * QB_A]
            dv_ref[pl.ds(start, QB_A), :] += dvc[b * QB_A:(b + 1) * QB_A]

    def spec(part, dn):
        return _bs((QB_A, W2), lambda hp, r, n: (jnp.clip(n + dn, 0, nblk - 1), r * cb + b0 + part * ob + hp))

    in_specs = [spec(0, 0)] + [spec(1, dn) for dn in (-1, 0, 1)] + [spec(2, dn) for dn in (-1, 0, 1)]
    in_specs += [_bs((QB_A, W2), lambda hp, r, n: (n, r * ob + hp))] * 3
    in_specs += [_bs((2, QB_A, 3 * QB_A), lambda hp, r, n: (hp, 0, 0))]
    out_shape = [jax.ShapeDtypeStruct((L, d * WIDTH_A), BF16), jax.ShapeDtypeStruct((L, d * WIDTH_A), F32),
                 jax.ShapeDtypeStruct((L, d * WIDTH_A), F32), jax.ShapeDtypeStruct((HEADS_A, QB_A, 3 * QB_A), F32)]
    out_specs = [_bs((QB_A, W2), lambda hp, r, n: (n, r * ob + hp)),
                 _bs((L, W2), lambda hp, r, n: (0, r * ob + hp)), _bs((L, W2), lambda hp, r, n: (0, r * ob + hp)),
                 _bs((2, QB_A, 3 * QB_A), lambda hp, r, n: (hp, 0, 0))]
    dq, dk, dv, db = pl.pallas_call(
        body, out_shape=out_shape, grid=(ob, d, nblk), in_specs=in_specs, out_specs=out_specs,
        compiler_params=_params("arbitrary", "arbitrary", "arbitrary"), name=f"a_bwd_d{d}")(
            pv, pv, pv, pv, pv, pv, pv, view(do), view(lse), view(cterm), bias)
    return dq.reshape(T, WIDTH_A), dk.reshape(T, WIDTH_A), dv.reshape(T, WIDTH_A), db


def _segment_ones():
    i = np.arange(WIDTH_A)
    return jnp.asarray((i[:, None] // HEAD_A == i[None, :] // HEAD_A).astype(np.float32), dtype=BF16)


def _group_weights(l0, l1, l2):
    m = jnp.maximum(jnp.maximum(l0, l1), l2)
    e = [jnp.exp(l - m) for l in (l0, l1, l2)]
    z = e[0] + e[1] + e[2]
    return [ei / z for ei in e]


def _combine_fwd(outs, lses):
    T = outs[0].shape[0]

    def fn(o0, o1, o2, l0, l1, l2):
        w = _group_weights(l0, l1, l2)
        return w[0] * o0.astype(F32) + w[1] * o1.astype(F32) + w[2] * o2.astype(F32)

    (oa,) = _ew("a_combine", fn, [_tiled(o) for o in outs] + [_tiled(l) for l in lses], [(BF16, WIDTH_A)], n_rows=T, rows=512)
    return oa


def _combine_bwd(doa, outs, lses):
    T = doa.shape[0]

    def fn(d, o0, o1, o2, l0, l1, l2, seg):
        d = d.astype(F32)
        w = _group_weights(l0, l1, l2)
        tot = jnp.zeros(d.shape, F32)
        for wg, og in zip(w, (o0, o1, o2)):
            prod = wg * d * og.astype(F32)
            hi = prod.astype(BF16)
            lo = (prod - hi.astype(F32)).astype(BF16)
            tot = tot + jnp.dot(hi, seg, preferred_element_type=F32) + jnp.dot(lo, seg, preferred_element_type=F32)
        return tuple(wg * d for wg in w) + tuple(-wg * tot for wg in w)

    res = _ew("a_combine_bwd", fn, [_tiled(doa)] + [_tiled(o) for o in outs] + [_tiled(l) for l in lses] + [_whole(_segment_ones())],
              [(BF16, WIDTH_A)] * 3 + [(F32, WIDTH_A)] * 3, n_rows=T, rows=256)
    return res[:3], res[3:]


def _rope_tables(T):
    rows = T // GRID_W
    row = jnp.repeat(jnp.arange(rows, dtype=F32), GRID_W)
    col = jnp.tile(jnp.arange(GRID_W, dtype=F32), rows)
    n_freq = HEAD_B // 4
    freq = ROPE_THETA ** (-jnp.arange(n_freq, dtype=F32) / n_freq)
    ang = jnp.concatenate([row[:, None] * freq, col[:, None] * freq], axis=-1)
    cos, sin = jnp.repeat(jnp.cos(ang), 2, axis=1), jnp.repeat(jnp.sin(ang), 2, axis=1)
    sign = jnp.where(jnp.arange(HEAD_B) % 2 == 0, -1.0, 1.0).astype(F32)
    return cos, sin * sign


def _swap_pairs(v):
    even = lax.broadcasted_iota(jnp.int32, v.shape, v.ndim - 1) % 2 == 0
    n = v.shape[-1]
    return jnp.where(even, pltpu.roll(v, n - 1, v.ndim - 1), pltpu.roll(v, 1, v.ndim - 1))


def _qk_fwd(name, proj, col0, n_heads, gain, cos, sin):
    T = proj.shape[0]

    def fn(xr, g, c, s):
        xn = _norm_fwd(xr.astype(F32), g)
        return xn * c + _swap_pairs(xn) * s

    (out,) = _ew(name, fn, [_tiled(proj, HEAD_B, col0 // HEAD_B), _whole(gain), _table(cos), _table(sin)],
                 [(BF16, HEAD_B)], n_rows=T, rows=512, ncols=n_heads)
    return out


def _qk_bwd(name, dout, proj, col0, n_heads, gain, cos, sin):
    T = proj.shape[0]

    def fn(dv, xr, g, c, s):
        dv = dv.astype(F32)
        dxn = c * dv + _swap_pairs(s * dv)
        dx, dgr = _norm_bwd(xr.astype(F32), g, dxn)
        return dx, _colsum(dgr)

    dx, dg = _ew(name, fn, [_tiled(dout, HEAD_B, 0), _tiled(proj, HEAD_B, col0 // HEAD_B), _whole(gain),
                            _table(cos), _table(sin)],
                 [(BF16, HEAD_B)], n_rows=T, rows=512, reds=(HEAD_B,), ncols=n_heads)
    return dx, jnp.sum(dg, axis=0)


def _gqa_fwd(qn, kn, proj):
    T = qn.shape[0]
    GW = 4 * HEAD_B
    scale = HEAD_B ** -0.5

    def body(q_ref, k_ref, v_ref, o_ref, l_ref):
        k, v = k_ref[...], v_ref[...]
        lane = lax.broadcasted_iota(jnp.int32, (QB_B, HEAD_B), 1)
        lse_all = jnp.zeros((QB_B, HEAD_B), F32)
        for g in range(4):
            cols = slice(g * HEAD_B, (g + 1) * HEAD_B)
            s = lax.dot_general(q_ref[:, cols], k, (NT, ((), ())), preferred_element_type=F32) * scale
            m = jnp.max(s, axis=-1, keepdims=True)
            p = jnp.exp(s - m)
            l = jnp.sum(p, axis=-1, keepdims=True)
            o = jnp.dot(p.astype(BF16), v, preferred_element_type=F32) / l
            o_ref[:, cols] = o.astype(o_ref.dtype)
            lse_all = jnp.where(lane == g, m + jnp.log(l), lse_all)
        l_ref[...] = lse_all

    return pl.pallas_call(
        body, out_shape=[jax.ShapeDtypeStruct((T, 2 * GW), BF16), jax.ShapeDtypeStruct((2, T, HEAD_B), F32)],
        grid=(2, T // QB_B),
        in_specs=[_bs((QB_B, GW), lambda kv, i: (i, kv)), _bs((T, HEAD_B), lambda kv, i: (0, kv)),
                  _bs((T, HEAD_B), lambda kv, i: (0, B_V // HEAD_B + kv))],
        out_specs=[_bs((QB_B, GW), lambda kv, i: (i, kv)), _bs((None, QB_B, HEAD_B), lambda kv, i: (kv, i, 0))],
        compiler_params=_params("parallel", "parallel"), name="b_fwd")(qn, kn, proj)


def _gqa_bwd(qn, kn, proj, o, lse, do):
    T = qn.shape[0]
    GW = 4 * HEAD_B
    scale = HEAD_B ** -0.5

    def body(q_ref, k_ref, v_ref, o_ref, l_ref, do_ref, dq_ref, dk_ref, dv_ref):
        i = pl.program_id(1)

        @pl.when(i == 0)
        def _():
            dk_ref[...] = jnp.zeros_like(dk_ref)
            dv_ref[...] = jnp.zeros_like(dv_ref)

        k, v = k_ref[...], v_ref[...]
        lse_all = l_ref[...]
        for g in range(4):
            cols = slice(g * HEAD_B, (g + 1) * HEAD_B)
            q, dob = q_ref[:, cols], do_ref[:, cols]
            delta = jnp.sum(dob.astype(F32) * o_ref[:, cols].astype(F32), axis=-1, keepdims=True)
            s = lax.dot_general(q, k, (NT, ((), ())), preferred_element_type=F32) * scale
            p = jnp.exp(s - lse_all[:, g:g + 1])
            dp = lax.dot_general(dob, v, (NT, ((), ())), preferred_element_type=F32)
            ds = (p * (dp - delta) * scale).astype(BF16)
            dq_ref[:, cols] = jnp.dot(ds, k, preferred_element_type=F32).astype(dq_ref.dtype)
            dk_ref[...] += lax.dot_general(ds, q, (TN, ((), ())), preferred_element_type=F32)
            dv_ref[...] += lax.dot_general(p.astype(BF16), dob, (TN, ((), ())), preferred_element_type=F32)

    return pl.pallas_call(
        body, out_shape=[jax.ShapeDtypeStruct((T, 2 * GW), BF16), jax.ShapeDtypeStruct((T, 2 * HEAD_B), F32),
                         jax.ShapeDtypeStruct((T, 2 * HEAD_B), F32)],
        grid=(2, T // QB_B),
        in_specs=[_bs((QB_B, GW), lambda kv, i: (i, kv)), _bs((T, HEAD_B), lambda kv, i: (0, kv)),
                  _bs((T, HEAD_B), lambda kv, i: (0, B_V // HEAD_B + kv)), _bs((QB_B, GW), lambda kv, i: (i, kv)),
                  _bs((None, QB_B, HEAD_B), lambda kv, i: (kv, i, 0)), _bs((QB_B, GW), lambda kv, i: (i, kv))],
        out_specs=[_bs((QB_B, GW), lambda kv, i: (i, kv)), _bs((T, HEAD_B), lambda kv, i: (0, kv)),
                   _bs((T, HEAD_B), lambda kv, i: (0, kv))],
        compiler_params=_params("parallel", "arbitrary"), name="b_bwd")(qn, kn, proj, o, lse, do)


def _local_step(x, target, small, get_w, put_g, deps=()):
    T, D = x.shape
    gs = {}

    x1, ffn1_saved = _ffn_fwd("ffn1", x, small["ffn1_norm"], get_w, deps)
    (h2,) = _ew("mix_norm", lambda xv, g: _norm_fwd(xv, g), [_tiled(x1), _whole(small["mix_norm"])], [(BF16, D)], n_rows=T, rows=512)
    w_in = get_w("w_in", h2)
    nq = w_in.shape[2]
    tpq = nq // WIDTH_A

    def proj_tile(j, k):
        c = j * tpq + k
        return jnp.where(c < 3 * len(DILATIONS), (c % 3) * 3 + c // 3, c)

    proj = _mm("mix_in", (4, T // 1024, tpq),
               [(h2, _bs((1024, D), lambda j, i, k: (i, 0)), w_in, _bs((None, D, WIDTH_A), lambda j, i, k: (j, 0, k)))],
               jax.ShapeDtypeStruct((T, IN_WIDTH), BF16), _bs((1024, WIDTH_A), lambda j, i, k: (i, proj_tile(j, k))), NN)

    bias = _bias_tiles(small["rel_bias"])
    a_outs, a_lses = [], []
    for grp, d in enumerate(DILATIONS):
        o, l = _dil_fwd(proj, bias[grp * HEADS_A:(grp + 1) * HEADS_A], grp, d)
        a_outs.append(o)
        a_lses.append(l)
    o_a = _combine_fwd(a_outs, a_lses)

    cos, sin = _rope_tables(T)
    qn = _qk_fwd("b_qnorm", proj, B_Q, 8, small["q_norm"], cos, sin)
    kn = _qk_fwd("b_knorm", proj, B_K, 2, small["k_norm"], cos, sin)
    o_b, lse_b = _gqa_fwd(qn, kn, proj)

    wa, wb3, w_out3 = get_w("w_branch_a", o_b), get_w("w_branch_b", o_b).reshape(1, D, D), get_w("w_out", o_b).reshape(1, D, D)
    t_a = _mm_cols("mix_branch_a", o_a, wa, tm=512, tn=256, out_dtype=BF16, cat=True)
    t_b = _mm_cols("mix_branch_b", o_b, wb3, tm=512, tn=512, out_dtype=BF16, cat=True)
    bg_a, bg_b = small["b_gate"][:, :D], small["b_gate"][:, D:]

    def merge(ta, tb, ga, gb_, ba, bb):
        sa, sb = _sigmoid(ga.astype(F32) + ba), _sigmoid(gb_.astype(F32) + bb)
        return sa * ta.astype(F32) + sb * tb.astype(F32)

    gate_ins = [_tiled(proj, D, G_A // D), _tiled(proj, D, G_B // D), _whole(bg_a), _whole(bg_b)]
    (merged,) = _ew("mix_merge", merge, [_tiled(t_a), _tiled(t_b)] + gate_ins, [(BF16, D)], n_rows=T, rows=512)
    x2 = _mm_cols("mix_out", merged, w_out3, tm=512, tn=512, out_dtype=F32, cat=True,
                  extras=[x1], epilogue=lambda acc, xv: xv + acc)
    x3, ffn2_saved = _ffn_fwd("ffn2", x2, small["ffn2_norm"], get_w)

    def head(xv, g, tv):
        r = _rstd(xv)
        xh = xv * r
        e = xh * g - tv
        dy = e * (1.0 / D)
        dxh = dy * g
        dx = r * (dxh - xh * jnp.mean(dxh * xh, axis=-1, keepdims=True))
        return dx, 0.5 * dx, _colsum(e * e) * (0.5 / D), _colsum(dy * xh)

    dx3, dx3_half, loss_cols, g_final = _ew("loss_head", head, [_tiled(x3), _whole(small["final_norm"].reshape(1, D)), _tiled(target)],
                                            [(F32, D), (BF16, D)], n_rows=T, rows=256, reds=(D, D))
    gs["final_norm"] = g_final.reshape(D)

    dx2, _, gs["ffn2_norm"] = _ffn_bwd("ffn2", x2, small["ffn2_norm"], get_w, put_g, ffn2_saved, dx3, dx3_half)

    (dmix,) = _ew("mix_bwd_cast", lambda v: v, [_tiled(dx2)], [(BF16, D)], n_rows=T, rows=512)
    g_out = _mm_wgrad("mix_bwd_dwout", merged, dmix, a_cols=D // 4, b_cols=None, tm=256, tn=512, J=4).reshape(D, D)
    dmerged = _mm_rows_t("mix_bwd_dmerged", dmix, w_out3, tm=512, out_dtype=BF16).reshape(T, D)

    def merge_bwd(dm, ta, tb, ga, gb_, ba, bb):
        dm, ta, tb = dm.astype(F32), ta.astype(F32), tb.astype(F32)
        sa, sb = _sigmoid(ga.astype(F32) + ba), _sigmoid(gb_.astype(F32) + bb)
        dga, dgb = dm * ta * sa * (1.0 - sa), dm * tb * sb * (1.0 - sb)
        return dm * sa, dm * sb, dga, dgb, _colsum(dga), _colsum(dgb)

    dta, dtb, dga, dgb, dba, dbb = _ew("mix_bwd_merge", merge_bwd, [_tiled(dmerged), _tiled(t_a), _tiled(t_b)] + gate_ins,
                                       [(BF16, D)] * 4, n_rows=T, rows=256, reds=(D, D))
    gs["b_gate"] = jnp.concatenate([dba.reshape(1, D), dbb.reshape(1, D)], axis=1)

    g_a = _mm_wgrad("mix_bwd_dwa", o_a, dta, a_cols=None, b_cols=D // 4, tm=WIDTH_A, tn=256, J=4)
    g_b = _mm_wgrad("mix_bwd_dwb", o_b, dtb, a_cols=D // 4, b_cols=None, tm=256, tn=512, J=4).reshape(D, D)
    deps = put_g({"w_out": g_out, "w_branch_a": g_a, "w_branch_b": g_b})
    do_a = _mm("mix_bwd_doa", (T // 1024,),
               [(dta, _bs((1024, D // 4), lambda i, j=j: (i, j)), wa, _bs((None, WIDTH_A, D // 4), lambda i, j=j: (j, 0, 0)))
                for j in range(4)],
               jax.ShapeDtypeStruct((T, WIDTH_A), BF16), _bs((1024, WIDTH_A), lambda i: (i, 0)), NT, deps=deps)
    do_b = _mm_rows_t("mix_bwd_dob", dtb, wb3, tm=512, out_dtype=BF16).reshape(T, D)

    dqn, dkn, dv_b = _gqa_bwd(qn, kn, proj, o_b, lse_b, do_b)
    dq_b, gs["q_norm"] = _qk_bwd("b_bwd_qnorm", dqn, proj, B_Q, 8, small["q_norm"], cos, sin)
    dk_b, gs["k_norm"] = _qk_bwd("b_bwd_knorm", dkn, proj, B_K, 2, small["k_norm"], cos, sin)

    do_groups, c_groups = _combine_bwd(do_a, a_outs, a_lses)
    dqs, dks, dvs, dbs = [], [], [], []
    for grp, d in enumerate(DILATIONS):
        dq, dk, dv, db = _dil_bwd(proj, bias[grp * HEADS_A:(grp + 1) * HEADS_A], do_groups[grp], a_lses[grp], c_groups[grp], grp, d)
        dqs.append(dq), dks.append(dk), dvs.append(dv), dbs.append(db)
    gs["rel_bias"] = _bias_grad(jnp.concatenate(dbs, axis=0))

    dproj = jnp.concatenate([p.astype(BF16) for p in dqs + dks + dvs + [dq_b, dk_b, dv_b, dga, dgb]], axis=1)
    nq = w_in.shape[2]
    deps = put_g({"w_in": _mm_wgrad("mix_bwd_dwin", h2, dproj, a_cols=None, b_cols=nq, tm=512, tn=512, J=4)})
    dh2 = _mm("mix_bwd_dh", (T // 512, D // 512),
              [(dproj, _bs((512, nq), lambda i, k, j=j: (i, j)), w_in, _bs((None, 512, nq), lambda i, k, j=j: (j, k, 0)))
               for j in range(4)],
              jax.ShapeDtypeStruct((T, D), F32), _bs((512, 512), lambda i, k: (i, k)), NT, deps=deps)

    def nb(xv, gv, dhv, dres):
        dx, dgr = _norm_bwd(xv, gv, dhv)
        dx = dx + dres
        return dx, 0.5 * dx, _colsum(dgr)

    dx1, dx1_half, g_mix = _ew("mix_bwd_norm", nb, [_tiled(x1), _whole(small["mix_norm"]), _tiled(dh2), _tiled(dx2)],
                               [(F32, D), (BF16, D)], n_rows=T, rows=256, reds=(D,))
    gs["mix_norm"] = g_mix.reshape(1, D)

    dx0, _, gs["ffn1_norm"] = _ffn_bwd("ffn1", x, small["ffn1_norm"], get_w, put_g, ffn1_saved, dx1, dx1_half)
    return loss_cols.reshape(1, D), dx0, gs


def _position():
    return lax.axis_index("x"), lax.axis_index("y"), lax.axis_index("c")


def _any_specs(n):
    return [pl.BlockSpec(memory_space=pl.ANY)] * n


HBM_SPEC = pl.BlockSpec(memory_space=pltpu.HBM)
SEM_SPEC = pl.BlockSpec(memory_space=pltpu.SEMAPHORE)
DATAFLOW_EFFECT = pltpu.SideEffectType.DATAFLOW_SIDE_EFFECTING
N_PEER_CHIPS = 3
LANES = 128


def _quarter_copies(srcs, lands, send_sems, recv_sems, scatter):
    x, y, c = _position()
    me = 2 * x + y
    peers = [(1 - x, y, c), (x, 1 - y, c), (1 - x, 1 - y, c)]
    copies = []
    for src, land, send, recv in zip(srcs, lands, send_sems, recv_sems):
        for p, (px, py, pc) in enumerate(peers):
            copies.append(pltpu.make_async_remote_copy(
                src_ref=src.at[2 * px + py] if scatter else src, dst_ref=land.at[me], send_sem=send.at[p], recv_sem=recv.at[p],
                device_id=(px, py, pc), device_id_type=MESH))
    return copies


def _exchange_start(name, srcs, lands, scatter):
    n = len(srcs)

    def body(*refs):
        src_refs, land_refs = refs[:n], refs[n:2 * n]
        send_sems, recv_sems = refs[2 * n:3 * n], refs[3 * n:4 * n]
        token = refs[-1]
        for cp in _quarter_copies(src_refs, land_refs, send_sems, recv_sems, scatter):
            cp.start()
        token[...] = jnp.zeros_like(token)

    sem = pltpu.SemaphoreType.DMA((N_PEER_CHIPS,))
    out_shape = [sem] * (2 * n) + [pltpu.HBM(a.shape, a.dtype) for a in list(srcs) + list(lands)]
    out_shape += [jax.ShapeDtypeStruct((8, LANES), F32)]
    res = pl.pallas_call(
        body, name=name, out_shape=out_shape, in_specs=[HBM_SPEC] * (2 * n),
        out_specs=[SEM_SPEC] * (2 * n) + [HBM_SPEC] * (2 * n) + [pl.BlockSpec(memory_space=pltpu.VMEM)],
        input_output_aliases={i: 2 * n + i for i in range(2 * n)},
        compiler_params=pltpu.CompilerParams(has_side_effects=DATAFLOW_EFFECT),
    )(*[pltpu.with_memory_space_constraint(a, pltpu.HBM) for a in list(srcs) + list(lands)])
    return res[:n], res[n:2 * n], res[2 * n:3 * n], res[3 * n:4 * n], res[4 * n]


def _exchange_wait(name, srcs, lands, send_sems, recv_sems, after, scatter):
    n = len(srcs)

    def body(*refs):
        src_refs, land_refs = refs[:n], refs[n:2 * n]
        sends, recvs = refs[2 * n:3 * n], refs[3 * n:4 * n]
        for cp in _quarter_copies(src_refs, land_refs, sends, recvs, scatter):
            cp.wait_send()
            cp.wait_recv()

    res = pl.pallas_call(
        body, name=name, out_shape=[pltpu.HBM(a.shape, a.dtype) for a in list(srcs) + list(lands)],
        in_specs=[HBM_SPEC] * (2 * n) + [SEM_SPEC] * (2 * n) + [pl.BlockSpec(memory_space=pl.ANY)],
        out_specs=[HBM_SPEC] * (2 * n), input_output_aliases={i: i for i in range(2 * n)},
        compiler_params=pltpu.CompilerParams(has_side_effects=DATAFLOW_EFFECT),
    )(*srcs, *lands, *send_sems, *recv_sems, after)
    return res[n:]


def _own_slot(stack_shape, own, dtype):
    me = 2 * lax.axis_index("x") + lax.axis_index("y")
    return lax.dynamic_update_slice(jnp.zeros(stack_shape, dtype), own[None], (me,) + (0,) * own.ndim)


def _swap_with_sibling(parts):
    n = len(parts)

    def body(*refs):
        ins, outs = refs[:n], refs[n:2 * n]
        send_sems, recv_sems = refs[2 * n:]
        x, y, c = _position()
        copies = []
        for i in range(n):
            cp = pltpu.make_async_remote_copy(ins[i], outs[i], send_sems.at[i], recv_sems.at[i],
                                              device_id=(x, y, 1 - c), device_id_type=MESH)
            cp.start()
            copies.append(cp)
        for cp in copies:
            cp.wait()

    return pl.pallas_call(
        body, out_shape=[jax.ShapeDtypeStruct(s.shape, s.dtype) for s in parts],
        in_specs=_any_specs(n), out_specs=_any_specs(n),
        scratch_shapes=[pltpu.SemaphoreType.DMA((n,)), pltpu.SemaphoreType.DMA((n,))],
        compiler_params=pltpu.CompilerParams(has_side_effects=True), name="swap_with_sibling")(*parts)


def _allreduce_small(buf):
    R, C = buf.shape
    flips = [(fx, fy, fc) for fx in (0, 1) for fy in (0, 1) for fc in (0, 1)][1:]

    def body(in_ref, out_ref, land_ref, send_sems, recv_sems):
        x, y, c = _position()
        me = 4 * x + 2 * y + c
        copies = []
        for k, (fx, fy, fc) in enumerate(flips):
            px, py, pc = (1 - x if fx else x), (1 - y if fy else y), (1 - c if fc else c)
            cp = pltpu.make_async_remote_copy(in_ref, land_ref.at[me], send_sems.at[k], recv_sems.at[k],
                                              device_id=(px, py, pc), device_id_type=MESH)
            cp.start()
            copies.append(cp)
        land_ref[me] = in_ref[...]
        for cp in copies:
            cp.wait()
        acc = land_ref[0]
        for k in range(1, 8):
            acc = acc + land_ref[k]
        out_ref[...] = acc

    return pl.pallas_call(
        body, out_shape=jax.ShapeDtypeStruct((R, C), F32),
        in_specs=[pl.BlockSpec(memory_space=pltpu.VMEM)], out_specs=pl.BlockSpec(memory_space=pltpu.VMEM),
        scratch_shapes=[pltpu.VMEM((8, R, C), F32), pltpu.SemaphoreType.DMA((7,)), pltpu.SemaphoreType.DMA((7,))],
        compiler_params=pltpu.CompilerParams(has_side_effects=True), name="allreduce_small")(buf)


def _adamw_math(w, g, m, v):
    m2 = ADAM_B1 * m + (1.0 - ADAM_B1) * g
    v2 = ADAM_B2 * v + (1.0 - ADAM_B2) * (g * g)
    m_hat = m2 / (1.0 - ADAM_B1 ** ADAM_STEP)
    v_hat = v2 / (1.0 - ADAM_B2 ** ADAM_STEP)
    delta = -ADAM_LR * (m_hat / (jnp.sqrt(v_hat) + ADAM_EPS) + ADAM_WD * w)
    return delta, m2, v2


def _adamw_big(name, w, m, v, part_mine, part_sibling):
    R, C = w.shape
    rows = 256 if R % 256 == 0 else R // 2 if (R // 2) % 8 == 0 else R

    def fn(wv, mv, vv, a, b):
        g = a + b
        return (g,) + _adamw_math(wv, g, mv, vv)

    return _ew(name, fn, [_tiled(w), _tiled(m), _tiled(v), _tiled(part_mine), _tiled(part_sibling)], [(F32, C)] * 4, n_rows=R, rows=rows)


def _sum_four(name, stack):
    _, R, C = stack.shape
    rows = 256 if R % 256 == 0 else R // 2 if (R // 2) % 8 == 0 else R
    flat = stack.reshape(4 * R, C)
    nrb = R // rows

    def fn(a, b, c, d):
        return ((a.astype(F32) + b.astype(F32)) + c.astype(F32)) + d.astype(F32)

    (out,) = _ew(name, fn, [_tiled(flat, None, 0, k * nrb) for k in range(4)], [(F32, C)], n_rows=R, rows=rows)
    return out


BIG = ("ffn1_w1", "ffn1_w3", "ffn1_w2", "w_in", "w_branch_a", "w_branch_b", "w_out", "ffn2_w1", "ffn2_w3", "ffn2_w2")
SMALL = ("ffn1_norm", "mix_norm", "b_gate", "q_norm", "k_norm", "rel_bias", "ffn2_norm", "final_norm")
ORDER = ("ffn1_norm", "ffn1_w1", "ffn1_w3", "ffn1_w2", "mix_norm", "w_in", "b_gate", "q_norm", "k_norm", "rel_bias",
         "w_branch_a", "w_branch_b", "w_out", "ffn2_norm", "ffn2_w1", "ffn2_w3", "ffn2_w2", "final_norm")
GATHER_GROUPS = (("ffn1_w1", "ffn1_w3"), ("ffn1_w2",), ("w_in",), ("w_branch_a", "w_branch_b", "w_out"),
                 ("ffn2_w1", "ffn2_w3", "ffn2_w2"))


def _pack_small(d):
    rows = []
    for n in SMALL:
        flat = d[n].reshape(-1)
        pad = (-flat.shape[0]) % LANES
        rows.append(jnp.pad(flat, (0, pad)).reshape(-1, LANES))
    buf = jnp.concatenate(rows, axis=0)
    return jnp.pad(buf, ((0, (-buf.shape[0]) % 8), (0, 0)))


def _unpack_small(buf, like):
    out, r = {}, 0
    for n in SMALL:
        size = like[n].size
        nr = -(-size // LANES)
        out[n] = buf[r:r + nr].reshape(-1)[:size].reshape(like[n].shape)
        r += nr
    return out


def kernel(x, ffn1_norm, ffn1_w1, ffn1_w3, ffn1_w2, mix_norm, w_in, b_gate, q_norm, k_norm, rel_bias, w_branch_a, w_branch_b, w_out, ffn2_norm, ffn2_w1, ffn2_w3, ffn2_w2, final_norm, loss_target, m_ffn1_norm, m_ffn1_w1, m_ffn1_w3, m_ffn1_w2, m_mix_norm, m_w_in, m_b_gate, m_q_norm, m_k_norm, m_rel_bias, m_w_branch_a, m_w_branch_b, m_w_out, m_ffn2_norm, m_ffn2_w1, m_ffn2_w3, m_ffn2_w2, m_final_norm, v_ffn1_norm, v_ffn1_w1, v_ffn1_w3, v_ffn1_w2, v_mix_norm, v_w_in, v_b_gate, v_q_norm, v_k_norm, v_rel_bias, v_w_branch_a, v_w_branch_b, v_w_out, v_ffn2_norm, v_ffn2_w1, v_ffn2_w3, v_ffn2_w2, v_final_norm):
    given = dict(locals())
    w = {n: given[n] for n in ORDER}
    m = {n: given["m_" + n] for n in ORDER}
    v = {n: given["v_" + n] for n in ORDER}
    T, D = x.shape[1], x.shape[2]

    quarter = {n: w[n].reshape(w[n].shape[1:]) for n in BIG}
    q16 = [quarter[n].astype(BF16) for n in BIG]
    send, recv, src_thru, land_thru, token = _exchange_start(
        "gather_start", q16, [_own_slot((4,) + q.shape, q, BF16) for q in q16], scatter=False)
    index = {n: i for i, n in enumerate(BIG)}
    ready = {}

    def get_w(name, after):
        if name not in ready:
            group = next(g for g in GATHER_GROUPS if name in g)
            ids = [index[n] for n in group]
            stacks = _exchange_wait("gather_wait_" + group[0], [src_thru[i] for i in ids], [land_thru[i] for i in ids],
                                    [send[i] for i in ids], [recv[i] for i in ids], after, scatter=False)
            for n, st in zip(group, stacks):
                ready[n] = st.reshape(D, D) if n in ("w_branch_b", "w_out") else st
        return ready[name]

    me = 2 * lax.axis_index("x") + lax.axis_index("y")
    in_flight = []

    def put_g(grads):
        names = list(grads)
        stacks = [grads[n].reshape((4,) + quarter[n].shape) for n in names]
        lands = [_own_slot(s.shape, lax.dynamic_index_in_dim(s, me, 0, keepdims=False), BF16) for s in stacks]
        started = _exchange_start("scatter_start_" + names[0], stacks, lands, scatter=True)
        in_flight.append((names,) + tuple(started[:4]))
        return [started[4]]

    small = {n: w[n] for n in SMALL}
    loss_cols, grad_x, gs = _local_step(x.reshape(T, D), loss_target.reshape(T, D), small, get_w, put_g, deps=[token])
    loss = lax.psum(jnp.sum(loss_cols), ("x", "y", "c"))

    landed = {}
    for names, s_sem, r_sem, srcs, lands in in_flight:
        got = _exchange_wait("scatter_wait_" + names[0], srcs, lands, s_sem, r_sem, grad_x, scatter=True)
        landed.update(zip(names, got))
    partial = [_sum_four(f"sum4_{n}", landed[n]) for n in BIG]
    other = _swap_with_sibling(partial)
    grads, deltas, new_m, new_v = {}, {}, {}, {}
    for n, mine, theirs in zip(BIG, partial, other):
        shp = w[n].shape
        two_d = quarter[n].shape
        res = _adamw_big(f"adamw_{n}", quarter[n], m[n].reshape(two_d), v[n].reshape(two_d), mine, theirs)
        grads[n], deltas[n], new_m[n], new_v[n] = [r.reshape(shp) for r in res]

    gs = {n: gs[n].reshape(w[n].shape) for n in SMALL}
    g_small = _allreduce_small(_pack_small(gs))
    packed = [_pack_small({n: d[n] for n in SMALL}) for d in (w, m, v)]
    R = g_small.shape[0]
    res = _ew("adamw_small", lambda wv, mv, vv, g: (g,) + _adamw_math(wv, g, mv, vv),
              [_tiled(packed[0]), _tiled(packed[1]), _tiled(packed[2]), _tiled(g_small)], [(F32, LANES)] * 4, n_rows=R, rows=R)
    for d, buf in zip((grads, deltas, new_m, new_v), res):
        d.update(_unpack_small(buf, w))

    return (loss, grad_x.reshape(x.shape), *[grads[n] for n in ORDER], *[deltas[n] for n in ORDER],
            *[new_m[n] for n in ORDER], *[new_v[n] for n in ORDER])
```

```python
import functools
import math

import numpy as np
import jax
import jax.numpy as jnp
from jax import lax
from jax.experimental import pallas as pl
from jax.experimental.pallas import tpu as pltpu

F32 = jnp.float32
BF16 = jnp.bfloat16
MESH = pl.DeviceIdType.MESH

NEG_INF = -1e30
EPS = 1e-6
GRID_W = 64
ROPE_THETA = 10000.0
DILATIONS = (1, 4, 16)
BAND_HALF = 64
HEAD_A = 64
HEADS_A = 8
WIDTH_A = HEADS_A * HEAD_A
HEAD_B = 128
N_BUCKETS = 32
MAX_DISTANCE = 1024
ADAM_LR, ADAM_B1, ADAM_B2, ADAM_EPS, ADAM_WD, ADAM_STEP = 0.001, 0.9, 0.999, 1e-08, 0.01, 10

A_Q, A_K, A_V = 0, 1536, 3072
B_Q, B_K, B_V = 4608, 5632, 5888
G_A, G_B = 6144, 7168
IN_WIDTH = 8192

VMEM_LIMIT_BYTES = 56 * 1024 * 1024
QB_A = 128
QB_B = 256


def _params(*sem):
    return pltpu.CompilerParams(dimension_semantics=sem, vmem_limit_bytes=VMEM_LIMIT_BYTES)


def _bs(shape, fn):
    return pl.BlockSpec(shape, fn)


def _mm(name, grid, pairs, out_shape, out_spec, dims, *, reduce_axis=None, extras=(), epilogue=None, deps=()):
    n_pairs, n_extra, n_deps = len(pairs), len(extras), len(deps)
    operands = [p[0] for p in pairs] + [p[2] for p in pairs] + [e[0] for e in extras] + list(deps)
    in_specs = [p[1] for p in pairs] + [p[3] for p in pairs] + [e[1] for e in extras] + _any_specs(n_deps)
    tile = tuple(s for s in out_spec.block_shape if s is not None)
    n_steps = grid[reduce_axis] if reduce_axis is not None else 1

    def body(*refs):
        a_refs, b_refs = refs[:n_pairs], refs[n_pairs:2 * n_pairs]
        e_refs = refs[2 * n_pairs:2 * n_pairs + n_extra]
        o_ref = refs[2 * n_pairs + n_extra + n_deps]
        acc = None
        for a_ref, b_ref in zip(a_refs, b_refs):
            t = lax.dot_general(a_ref[...], b_ref[...], (dims, ((), ())), preferred_element_type=F32)
            acc = t if acc is None else acc + t

        def finish(v):
            if epilogue is not None:
                v = epilogue(v, *[e[...] for e in e_refs])
            o_ref[...] = v.astype(o_ref.dtype)

        if reduce_axis is None:
            finish(acc)
        else:
            acc_ref = refs[-1]
            k = pl.program_id(reduce_axis)

            @pl.when(k == 0)
            def _():
                acc_ref[...] = acc

            @pl.when(k > 0)
            def _():
                acc_ref[...] += acc

            @pl.when(k == n_steps - 1)
            def _():
                finish(acc_ref[...])

    sem = ["parallel"] * len(grid)
    if reduce_axis is not None:
        sem[reduce_axis] = "arbitrary"
    return pl.pallas_call(
        body, out_shape=out_shape, grid=grid, in_specs=in_specs, out_specs=out_spec,
        scratch_shapes=[pltpu.VMEM(tile, F32)] if reduce_axis is not None else [],
        compiler_params=_params(*sem), name=name)(*operands)


NN = ((1,), (0,))
NT = ((1,), (1,))
TN = ((0,), (0,))


def _mm_cols(name, a, w, *, tm, tn, out_dtype, cat, extras=(), epilogue=None):
    M, K = a.shape
    J, _, n = w.shape
    tn = min(tn, n)
    nb = n // tn
    if cat:
        shape, spec = (M, J * n), _bs((tm, tn), lambda j, i, k: (i, j * nb + k))
    else:
        shape, spec = (J, M, n), _bs((None, tm, tn), lambda j, i, k: (j, i, k))
    ex = [(e, _bs((tm, tn), lambda j, i, k: (i, j * nb + k))) for e in extras]
    return _mm(name, (J, M // tm, nb),
               [(a, _bs((tm, K), lambda j, i, k: (i, 0)), w, _bs((None, K, tn), lambda j, i, k: (j, 0, k)))],
               jax.ShapeDtypeStruct(shape, out_dtype), spec, NN, extras=ex, epilogue=epilogue)


def _mm_rows_t(name, a, w, *, tm, out_dtype):
    M, N = a.shape
    J, f, _ = w.shape
    return _mm(name, (J, M // tm),
               [(a, _bs((tm, N), lambda j, i: (i, 0)), w, _bs((None, f, N), lambda j, i: (j, 0, 0)))],
               jax.ShapeDtypeStruct((J, M, f), out_dtype), _bs((None, tm, f), lambda j, i: (j, i, 0)), NT)


def _mm_wgrad(name, a, b, *, a_cols, b_cols, tm, tn, J):
    def pick(arr, cols, t):
        if arr.ndim == 3:
            T, c = arr.shape[1], arr.shape[2]
            t = min(t, c)
            return T, c, t, (lambda sel: _bs((None, T, t), lambda j, i, k: (j, 0, sel(i, k))))
        T = arr.shape[0]
        c = arr.shape[1] if cols is None else cols
        t = min(t, c)
        per = c // t
        if cols is None:
            return T, c, t, (lambda sel: _bs((T, t), lambda j, i, k: (0, sel(i, k))))
        return T, c, t, (lambda sel: _bs((T, t), lambda j, i, k: (0, j * per + sel(i, k))))
    _, ca, tm, mk_a = pick(a, a_cols, tm)
    _, cb, tn, mk_b = pick(b, b_cols, tn)
    return _mm(name, (J, ca // tm, cb // tn),
               [(a, mk_a(lambda i, k: i), b, mk_b(lambda i, k: k))],
               jax.ShapeDtypeStruct((J, ca, cb), BF16), _bs((None, tm, tn), lambda j, i, k: (j, i, k)), TN)


def _tiled(arr, width=None, col=0, rowblk=0):
    return ("t", arr, arr.shape[1] if width is None else width, col, rowblk)


def _table(arr):
    return ("f", arr)


def _whole(arr):
    return ("w", arr)


def _ew(name, fn, ins, outs, *, n_rows, rows, reds=(), ncols=1, deps=()):
    nrb = n_rows // rows
    n_deps = len(deps)
    operands, in_specs = [], []
    for spec in ins:
        if spec[0] == "t":
            _, arr, width, col, rowblk = spec
            step = 1 if ncols > 1 else 0
            in_specs.append(_bs((rows, width), lambda c, i, col=col, rowblk=rowblk, step=step: (rowblk + i, col + c * step)))
        elif spec[0] == "f":
            arr = spec[1]
            in_specs.append(_bs((rows, arr.shape[1]), lambda c, i: (i, 0)))
        else:
            arr = spec[1]
            nd = arr.ndim
            if nd == 3:
                in_specs.append(_bs((None,) + arr.shape[1:], lambda c, i: (c, 0, 0)))
            else:
                in_specs.append(_bs(arr.shape, lambda c, i, nd=nd: (0,) * nd))
        operands.append(arr)
    out_shapes = [jax.ShapeDtypeStruct((n_rows, ncols * w), dt) for dt, w in outs]
    out_specs = [_bs((rows, w), lambda c, i: (i, c)) for _, w in outs]
    out_shapes += [jax.ShapeDtypeStruct((ncols, 1, w), F32) for w in reds]
    out_specs += [_bs((None, 1, w), lambda c, i: (c, 0, 0)) for w in reds]
    n_in, n_out, n_red = len(ins), len(outs), len(reds)
    operands += list(deps)
    in_specs += _any_specs(n_deps)

    def body(*refs):
        vals = fn(*[r[...] for r in refs[:n_in]])
        if not isinstance(vals, (tuple, list)):
            vals = (vals,)
        o_refs = refs[n_in + n_deps:]
        for o_ref, v in zip(o_refs[:n_out], vals[:n_out]):
            o_ref[...] = v.astype(o_ref.dtype)
        if n_red:
            i = pl.program_id(1)
            for r_ref, v in zip(o_refs[n_out:], vals[n_out:]):
                @pl.when(i == 0)
                def _(r_ref=r_ref):
                    r_ref[...] = jnp.zeros_like(r_ref)
                r_ref[...] += v

    res = pl.pallas_call(
        body, out_shape=out_shapes, grid=(ncols, nrb), in_specs=in_specs, out_specs=out_specs,
        compiler_params=_params("parallel", "arbitrary" if n_red else "parallel"), name=name)(*operands)
    return res


def _colsum(v):
    return jnp.sum(v, axis=0, keepdims=True)


def _rstd(x):
    return lax.rsqrt(jnp.mean(x * x, axis=-1, keepdims=True) + EPS)


def _sigmoid(x):
    return 1.0 / (1.0 + jnp.exp(-x))


def _norm_fwd(x, g):
    return x * _rstd(x) * g


def _norm_bwd(x, g, dy):
    r = _rstd(x)
    xh = x * r
    dxh = dy * g
    dx = r * (dxh - xh * jnp.mean(dxh * xh, axis=-1, keepdims=True))
    return dx, dy * xh


def _ffn_fwd(tag, x, gain, get_w, deps=()):
    T, D = x.shape
    (h,) = _ew(f"{tag}_norm", lambda xv, g: _norm_fwd(xv, g), [_tiled(x), _whole(gain)], [(BF16, D)], n_rows=T, rows=512,
               deps=deps)
    w1, w3 = get_w(f"{tag}_w1", h), get_w(f"{tag}_w3", h)
    J, f, _ = w1.shape
    tm = 1024

    def up(h_ref, w1_ref, w3_ref, u_ref, g_ref, a_ref):
        hv = h_ref[...]
        u = lax.dot_general(hv, w1_ref[...], (NT, ((), ())), preferred_element_type=F32)
        g = lax.dot_general(hv, w3_ref[...], (NT, ((), ())), preferred_element_type=F32)
        u_ref[...] = u.astype(BF16)
        g_ref[...] = g.astype(BF16)
        a_ref[...] = (u * _sigmoid(u) * g).astype(BF16)

    slab = _bs((None, tm, f), lambda j, i: (j, i, 0))
    w_spec = _bs((None, f, D), lambda j, i: (j, 0, 0))
    u, g, a = pl.pallas_call(
        up, out_shape=[jax.ShapeDtypeStruct((J, T, f), BF16)] * 3, grid=(J, T // tm),
        in_specs=[_bs((tm, D), lambda j, i: (i, 0)), w_spec, w_spec], out_specs=[slab] * 3,
        compiler_params=_params("parallel", "parallel"), name=f"{tag}_up")(h, w1, w3)
    w2 = get_w(f"{tag}_w2", a)
    y = _mm(f"{tag}_down", (T // 1024, D // 512),
            [(a, _bs((None, 1024, f), lambda i, k, j=j: (j, i, 0)), w2, _bs((None, f, 512), lambda i, k, j=j: (j, 0, k)))
             for j in range(J)],
            jax.ShapeDtypeStruct((T, D), F32), _bs((1024, 512), lambda i, k: (i, k)), NN,
            extras=[(x, _bs((1024, 512), lambda i, k: (i, k)))], epilogue=lambda acc, xv: xv + 0.5 * acc)
    return y, (h, u, g, a)


def _ffn_bwd(tag, x, gain, get_w, put_g, saved, dy, dy_half):
    h, u, g, a = saved
    T, D = x.shape
    w1, w3, w2 = [get_w(f"{tag}_{n}", dy_half) for n in ("w1", "w3", "w2")]
    J, f, _ = w1.shape
    dw2 = _mm_wgrad(f"{tag}_bwd_dw2", a, dy_half, a_cols=None, b_cols=None, tm=f, tn=512, J=J)
    deps = put_g({f"{tag}_w2": dw2})
    tm = 1024

    def up_bwd(dy_ref, w2_ref, u_ref, g_ref, *rest):
        du_ref, dg_ref = rest[-2:]
        da = lax.dot_general(dy_ref[...], w2_ref[...], (NT, ((), ())), preferred_element_type=F32)
        uv, gv = u_ref[...].astype(F32), g_ref[...].astype(F32)
        s = _sigmoid(uv)
        du_ref[...] = (da * gv * (s * (1.0 + uv * (1.0 - s)))).astype(BF16)
        dg_ref[...] = (da * (uv * s)).astype(BF16)

    slab = _bs((None, tm, f), lambda j, i: (j, i, 0))
    du, dg = pl.pallas_call(
        up_bwd, out_shape=[jax.ShapeDtypeStruct((J, T, f), BF16)] * 2, grid=(J, T // tm),
        in_specs=[_bs((tm, D), lambda j, i: (i, 0)), _bs((None, f, D), lambda j, i: (j, 0, 0)), slab, slab] + _any_specs(len(deps)),
        out_specs=[slab] * 2, compiler_params=_params("parallel", "parallel"), name=f"{tag}_bwd_up")(dy_half, w2, u, g, *deps)
    dw1 = _mm_wgrad(f"{tag}_bwd_dw1", du, h, a_cols=None, b_cols=None, tm=f, tn=512, J=J)
    dw3 = _mm_wgrad(f"{tag}_bwd_dw3", dg, h, a_cols=None, b_cols=None, tm=f, tn=512, J=J)
    deps = deps + put_g({f"{tag}_w1": dw1, f"{tag}_w3": dw3})
    pairs = []
    for j in range(J):
        a_spec = _bs((None, 512, f), lambda i, k, j=j: (j, i, 0))
        w_spec = _bs((None, f, 512), lambda i, k, j=j: (j, 0, k))
        pairs += [(du, a_spec, w1, w_spec), (dg, a_spec, w3, w_spec)]
    dh = _mm(f"{tag}_bwd_dh", (T // 512, D // 512), pairs,
             jax.ShapeDtypeStruct((T, D), F32), _bs((512, 512), lambda i, k: (i, k)), NN, deps=deps)

    def nb(xv, gv, dhv, dres):
        dx, dgr = _norm_bwd(xv, gv, dhv)
        dx = dx + dres
        return dx, 0.5 * dx, _colsum(dgr)

    dx, dx_half, dgain = _ew(f"{tag}_bwd_norm", nb, [_tiled(x), _whole(gain), _tiled(dh), _tiled(dy)],
                             [(F32, D), (BF16, D)], n_rows=T, rows=256, reds=(D,))
    return dx, dx_half, dgain.reshape(1, D)


def _t5_bucket(rel):
    n = N_BUCKETS // 2
    max_exact = n // 2
    ret = jnp.where(rel > 0, n, 0)
    a = jnp.abs(rel)
    af = jnp.maximum(a, 1).astype(F32)
    large = max_exact + (jnp.log(af / max_exact) / math.log(MAX_DISTANCE / max_exact) * (n - max_exact)).astype(jnp.int32)
    large = jnp.minimum(large, n - 1)
    return ret + jnp.where(a < max_exact, a, large)


def _band_steps():
    qi = jnp.arange(QB_A, dtype=jnp.int32)[:, None]
    kj = jnp.arange(3 * QB_A, dtype=jnp.int32)[None, :] - QB_A
    return kj - qi


def _bias_tiles(rel_bias):
    steps = _band_steps()
    buckets = jnp.stack([_t5_bucket(steps * d) for d in DILATIONS])
    inband = (jnp.abs(steps) <= BAND_HALF).astype(jnp.int32)
    n_heads = rel_bias.shape[1]

    def body(tab_ref, b_ref, m_ref, o_ref):
        hd = pl.program_id(0)
        bkt = b_ref[...]
        acc = jnp.zeros(bkt.shape, F32)
        for b in range(N_BUCKETS):
            acc = jnp.where(bkt == b, tab_ref[b, hd], acc)
        o_ref[...] = jnp.where(m_ref[...] > 0, acc, NEG_INF)

    return pl.pallas_call(
        body, out_shape=jax.ShapeDtypeStruct((n_heads, QB_A, 3 * QB_A), F32), grid=(n_heads,),
        in_specs=[pl.BlockSpec(memory_space=pltpu.SMEM),
                  _bs((None, QB_A, 3 * QB_A), lambda hd: (hd // HEADS_A, 0, 0)),
                  _bs((QB_A, 3 * QB_A), lambda hd: (0, 0))],
        out_specs=_bs((None, QB_A, 3 * QB_A), lambda hd: (hd, 0, 0)),
        compiler_params=_params("parallel"), name="a_bias_tiles")(rel_bias, buckets, inband)


def _bias_grad(dbias):
    steps = np.arange(3 * QB_A)[None, :] - QB_A - np.arange(QB_A)[:, None]
    inband = np.abs(steps) <= BAND_HALF
    present = []
    for d in DILATIONS:
        rel = steps * d
        a = np.abs(rel)
        large = 8 + (np.log(np.maximum(a, 1) / 8.0) / math.log(MAX_DISTANCE / 8.0) * 8).astype(np.int64)
        bk = np.where(rel > 0, 16, 0) + np.where(a < 8, a, np.minimum(large, 15))
        present.append(sorted(set(bk[inband].tolist())))
    buckets = jnp.stack([_t5_bucket(_band_steps() * d) for d in DILATIONS])
    n_heads = dbias.shape[0]

    def body(b_ref, d_ref, o_ref):
        row = lax.broadcasted_iota(jnp.int32, (N_BUCKETS, n_heads), 0)
        col = lax.broadcasted_iota(jnp.int32, (N_BUCKETS, n_heads), 1)
        out = jnp.zeros((N_BUCKETS, n_heads), F32)
        for grp in range(len(DILATIONS)):
            bkt = b_ref[grp]
            for hh in range(HEADS_A):
                hd = grp * HEADS_A + hh
                ds = d_ref[hd]
                for b in present[grp]:
                    tot = jnp.sum(jnp.where(bkt == b, ds, 0.0))
                    out = jnp.where((row == b) & (col == hd), tot, out)
        o_ref[...] = out

    return pl.pallas_call(
        body, out_shape=jax.ShapeDtypeStruct((N_BUCKETS, n_heads), F32),
        compiler_params=pltpu.CompilerParams(vmem_limit_bytes=VMEM_LIMIT_BYTES), name="a_bias_grad")(buckets, dbias)


def _lane_is_second_head(shape):
    return lax.broadcasted_iota(jnp.int32, shape, len(shape) - 1) >= HEAD_A


def _group_view(proj, grp, d):
    T = proj.shape[0]
    if d == 1:
        return proj, IN_WIDTH, grp * 3 * WIDTH_A
    part = proj[:, grp * 3 * WIDTH_A:(grp + 1) * 3 * WIDTH_A]
    return part.reshape(T // d, d * 3 * WIDTH_A), 3 * WIDTH_A, 0


def _stack_heads(v2, second):
    zero = jnp.zeros_like(v2)
    return jnp.concatenate([jnp.where(second, zero, v2), jnp.where(second, v2, zero)], axis=0)


def _unstack_heads(v, second):
    return jnp.where(second, v[QB_A:], v[:QB_A])


def _edge_mask(n, nblk):
    neg_prev = jnp.where(n > 0, 0.0, NEG_INF)
    neg_next = jnp.where(n < nblk - 1, 0.0, NEG_INF)
    return jnp.concatenate([jnp.full((1, QB_A), neg_prev, F32), jnp.zeros((1, QB_A), F32),
                            jnp.full((1, QB_A), neg_next, F32)], axis=1)


def _dil_fwd(proj, bias, grp, d):
    T = proj.shape[0]
    L = T // d
    nblk = L // QB_A
    pv, width, base = _group_view(proj, grp, d)
    cb, b0 = width // WIDTH_A, base // WIDTH_A
    W2 = 2 * HEAD_A
    scale = HEAD_A ** -0.5

    def body(q_ref, kp_ref, kc_ref, kn_ref, vp_ref, vc_ref, vn_ref, b_ref, o_ref, l_ref):
        edge = _edge_mask(pl.program_id(1), nblk)
        second = _lane_is_second_head((QB_A, W2))
        for hp in range(HEADS_A // 2):
            cols = slice(hp * W2, (hp + 1) * W2)
            kcat = jnp.concatenate([kp_ref[:, cols], kc_ref[:, cols], kn_ref[:, cols]], axis=0)
            vcat = jnp.concatenate([vp_ref[:, cols], vc_ref[:, cols], vn_ref[:, cols]], axis=0)
            qs = _stack_heads(q_ref[:, cols], second)
            s = lax.dot_general(qs, kcat, (NT, ((), ())), preferred_element_type=F32)
            s = s * scale + b_ref[2 * hp:2 * hp + 2].reshape(2 * QB_A, 3 * QB_A) + edge
            m = jnp.max(s, axis=-1, keepdims=True)
            p = jnp.exp(s - m)
            l = jnp.sum(p, axis=-1, keepdims=True)
            res = jnp.dot(p.astype(BF16), vcat, preferred_element_type=F32) / l
            o_ref[:, cols] = _unstack_heads(res, second).astype(o_ref.dtype)
            l_ref[:, cols] = _unstack_heads(jnp.broadcast_to(m + jnp.log(l), (2 * QB_A, W2)), second)

    def spec(part, dn):
        return _bs((QB_A, WIDTH_A), lambda r, n: (jnp.clip(n + dn, 0, nblk - 1), r * cb + b0 + part))

    in_specs = [spec(0, 0)] + [spec(1, dn) for dn in (-1, 0, 1)] + [spec(2, dn) for dn in (-1, 0, 1)]
    in_specs += [_bs((HEADS_A, QB_A, 3 * QB_A), lambda r, n: (0, 0, 0))]
    o, lse = pl.pallas_call(
        body, out_shape=[jax.ShapeDtypeStruct((L, d * WIDTH_A), BF16), jax.ShapeDtypeStruct((L, d * WIDTH_A), F32)],
        grid=(d, nblk), in_specs=in_specs,
        out_specs=[_bs((QB_A, WIDTH_A), lambda r, n: (n, r)), _bs((QB_A, WIDTH_A), lambda r, n: (n, r))],
        compiler_params=_params("parallel", "parallel"), name=f"a_fwd_d{d}")(pv, pv, pv, pv, pv, pv, pv, bias)
    return o.reshape(T, WIDTH_A), lse.reshape(T, WIDTH_A)


def _dil_bwd(proj, bias, do, lse, cterm, grp, d):
    T = proj.shape[0]
    L = T // d
    nblk = L // QB_A
    W2 = 2 * HEAD_A
    pv, width, base = _group_view(proj, grp, d)
    cb, b0 = width // W2, base // W2
    ob = WIDTH_A // W2
    view = lambda a: a.reshape(L, d * WIDTH_A)
    scale = HEAD_A ** -0.5

    def body(q_ref, kp_ref, kc_ref, kn_ref, vp_ref, vc_ref, vn_ref, do_ref, l_ref, c_ref, b_ref,
             dq_ref, dk_ref, dv_ref, db_ref):
        r, n = pl.program_id(1), pl.program_id(2)

        @pl.when(n == 0)
        def _():
            dk_ref[...] = jnp.zeros_like(dk_ref)
            dv_ref[...] = jnp.zeros_like(dv_ref)

        @pl.when((n == 0) & (r == 0))
        def _():
            db_ref[...] = jnp.zeros_like(db_ref)

        second = _lane_is_second_head((QB_A, W2))
        kcat = jnp.concatenate([kp_ref[...], kc_ref[...], kn_ref[...]], axis=0)
        vcat = jnp.concatenate([vp_ref[...], vc_ref[...], vn_ref[...]], axis=0)
        qs, dos = _stack_heads(q_ref[...], second), _stack_heads(do_ref[...], second)
        lse2, c2 = l_ref[...], c_ref[...]
        lse_rows = jnp.concatenate([lse2[:, 0:1], lse2[:, HEAD_A:HEAD_A + 1]], axis=0)
        c_rows = jnp.concatenate([c2[:, 0:1], c2[:, HEAD_A:HEAD_A + 1]], axis=0)
        s = lax.dot_general(qs, kcat, (NT, ((), ())), preferred_element_type=F32)
        p = jnp.exp(s * scale + b_ref[...].reshape(2 * QB_A, 3 * QB_A) + _edge_mask(n, nblk) - lse_rows)
        dp = lax.dot_general(dos, vcat, (NT, ((), ())), preferred_element_type=F32)
        ds = p * (dp + c_rows)
        db_ref[...] += ds.reshape(2, QB_A, 3 * QB_A)
        pb, dsb = p.astype(BF16), (ds * scale).astype(BF16)
        dq_ref[...] = _unstack_heads(jnp.dot(dsb, kcat, preferred_element_type=F32), second).astype(dq_ref.dtype)
        dkc = lax.dot_general(dsb, qs, (TN, ((), ())), preferred_element_type=F32)
        dvc = lax.dot_general(pb, dos, (TN, ((), ())), preferred_element_type=F32)
        for b, dn in enumerate((-1, 0, 1)):
            start = pl.multiple_of(jnp.clip(n + dn, 0, nblk - 1) * QB_A, QB_A)
            dk_ref[pl.ds(start, QB_A), :] += dkc[b * QB_A:(b + 1) * QB_A]
            dv_ref[pl.ds(start, QB_A), :] += dvc[b * QB_A:(b + 1) * QB_A]

    def spec(part, dn):
        return _bs((QB_A, W2), lambda hp, r, n: (jnp.clip(n + dn, 0, nblk - 1), r * cb + b0 + part * ob + hp))

    in_specs = [spec(0, 0)] + [spec(1, dn) for dn in (-1, 0, 1)] + [spec(2, dn) for dn in (-1, 0, 1)]
    in_specs += [_bs((QB_A, W2), lambda hp, r, n: (n, r * ob + hp))] * 3
    in_specs += [_bs((2, QB_A, 3 * QB_A), lambda hp, r, n: (hp, 0, 0))]
    out_shape = [jax.ShapeDtypeStruct((L, d * WIDTH_A), BF16), jax.ShapeDtypeStruct((L, d * WIDTH_A), F32),
                 jax.ShapeDtypeStruct((L, d * WIDTH_A), F32), jax.ShapeDtypeStruct((HEADS_A, QB_A, 3 * QB_A), F32)]
    out_specs = [_bs((QB_A, W2), lambda hp, r, n: (n, r * ob + hp)),
                 _bs((L, W2), lambda hp, r, n: (0, r * ob + hp)), _bs((L, W2), lambda hp, r, n: (0, r * ob + hp)),
                 _bs((2, QB_A, 3 * QB_A), lambda hp, r, n: (hp, 0, 0))]
    dq, dk, dv, db = pl.pallas_call(
        body, out_shape=out_shape, grid=(ob, d, nblk), in_specs=in_specs, out_specs=out_specs,
        compiler_params=_params("arbitrary", "arbitrary", "arbitrary"), name=f"a_bwd_d{d}")(
            pv, pv, pv, pv, pv, pv, pv, view(do), view(lse), view(cterm), bias)
    return dq.reshape(T, WIDTH_A), dk.reshape(T, WIDTH_A), dv.reshape(T, WIDTH_A), db


def _segment_ones():
    i = np.arange(WIDTH_A)
    return jnp.asarray((i[:, None] // HEAD_A == i[None, :] // HEAD_A).astype(np.float32), dtype=BF16)


def _group_weights(l0, l1, l2):
    m = jnp.maximum(jnp.maximum(l0, l1), l2)
    e = [jnp.exp(l - m) for l in (l0, l1, l2)]
    z = e[0] + e[1] + e[2]
    return [ei / z for ei in e]


def _combine_fwd(outs, lses):
    T = outs[0].shape[0]

    def fn(o0, o1, o2, l0, l1, l2):
        w = _group_weights(l0, l1, l2)
        return w[0] * o0.astype(F32) + w[1] * o1.astype(F32) + w[2] * o2.astype(F32)

    (oa,) = _ew("a_combine", fn, [_tiled(o) for o in outs] + [_tiled(l) for l in lses], [(BF16, WIDTH_A)], n_rows=T, rows=512)
    return oa


def _combine_bwd(doa, outs, lses):
    T = doa.shape[0]

    def fn(d, o0, o1, o2, l0, l1, l2, seg):
        d = d.astype(F32)
        w = _group_weights(l0, l1, l2)
        tot = jnp.zeros(d.shape, F32)
        for wg, og in zip(w, (o0, o1, o2)):
            prod = wg * d * og.astype(F32)
            hi = prod.astype(BF16)
            lo = (prod - hi.astype(F32)).astype(BF16)
            tot = tot + jnp.dot(hi, seg, preferred_element_type=F32) + jnp.dot(lo, seg, preferred_element_type=F32)
        return tuple(wg * d for wg in w) + tuple(-wg * tot for wg in w)

    res = _ew("a_combine_bwd", fn, [_tiled(doa)] + [_tiled(o) for o in outs] + [_tiled(l) for l in lses] + [_whole(_segment_ones())],
              [(BF16, WIDTH_A)] * 3 + [(F32, WIDTH_A)] * 3, n_rows=T, rows=256)
    return res[:3], res[3:]


def _rope_tables(T):
    rows = T // GRID_W
    row = jnp.repeat(jnp.arange(rows, dtype=F32), GRID_W)
    col = jnp.tile(jnp.arange(GRID_W, dtype=F32), rows)
    n_freq = HEAD_B // 4
    freq = ROPE_THETA ** (-jnp.arange(n_freq, dtype=F32) / n_freq)
    ang = jnp.concatenate([row[:, None] * freq, col[:, None] * freq], axis=-1)
    cos, sin = jnp.repeat(jnp.cos(ang), 2, axis=1), jnp.repeat(jnp.sin(ang), 2, axis=1)
    sign = jnp.where(jnp.arange(HEAD_B) % 2 == 0, -1.0, 1.0).astype(F32)
    return cos, sin * sign


def _swap_pairs(v):
    even = lax.broadcasted_iota(jnp.int32, v.shape, v.ndim - 1) % 2 == 0
    n = v.shape[-1]
    return jnp.where(even, pltpu.roll(v, n - 1, v.ndim - 1), pltpu.roll(v, 1, v.ndim - 1))


def _qk_fwd(name, proj, col0, n_heads, gain, cos, sin):
    T = proj.shape[0]

    def fn(xr, g, c, s):
        xn = _norm_fwd(xr.astype(F32), g)
        return xn * c + _swap_pairs(xn) * s

    (out,) = _ew(name, fn, [_tiled(proj, HEAD_B, col0 // HEAD_B), _whole(gain), _table(cos), _table(sin)],
                 [(BF16, HEAD_B)], n_rows=T, rows=512, ncols=n_heads)
    return out


def _qk_bwd(name, dout, proj, col0, n_heads, gain, cos, sin):
    T = proj.shape[0]

    def fn(dv, xr, g, c, s):
        dv = dv.astype(F32)
        dxn = c * dv + _swap_pairs(s * dv)
        dx, dgr = _norm_bwd(xr.astype(F32), g, dxn)
        return dx, _colsum(dgr)

    dx, dg = _ew(name, fn, [_tiled(dout, HEAD_B, 0), _tiled(proj, HEAD_B, col0 // HEAD_B), _whole(gain),
                            _table(cos), _table(sin)],
                 [(BF16, HEAD_B)], n_rows=T, rows=512, reds=(HEAD_B,), ncols=n_heads)
    return dx, jnp.sum(dg, axis=0)


def _gqa_fwd(qn, kn, proj):
    T = qn.shape[0]
    GW = 4 * HEAD_B
    scale = HEAD_B ** -0.5

    def body(q_ref, k_ref, v_ref, o_ref, l_ref):
        k, v = k_ref[...], v_ref[...]
        lane = lax.broadcasted_iota(jnp.int32, (QB_B, HEAD_B), 1)
        lse_all = jnp.zeros((QB_B, HEAD_B), F32)
        for g in range(4):
            cols = slice(g * HEAD_B, (g + 1) * HEAD_B)
            s = lax.dot_general(q_ref[:, cols], k, (NT, ((), ())), preferred_element_type=F32) * scale
            m = jnp.max(s, axis=-1, keepdims=True)
            p = jnp.exp(s - m)
            l = jnp.sum(p, axis=-1, keepdims=True)
            o = jnp.dot(p.astype(BF16), v, preferred_element_type=F32) / l
            o_ref[:, cols] = o.astype(o_ref.dtype)
            lse_all = jnp.where(lane == g, m + jnp.log(l), lse_all)
        l_ref[...] = lse_all

    return pl.pallas_call(
        body, out_shape=[jax.ShapeDtypeStruct((T, 2 * GW), BF16), jax.ShapeDtypeStruct((2, T, HEAD_B), F32)],
        grid=(2, T // QB_B),
        in_specs=[_bs((QB_B, GW), lambda kv, i: (i, kv)), _bs((T, HEAD_B), lambda kv, i: (0, kv)),
                  _bs((T, HEAD_B), lambda kv, i: (0, B_V // HEAD_B + kv))],
        out_specs=[_bs((QB_B, GW), lambda kv, i: (i, kv)), _bs((None, QB_B, HEAD_B), lambda kv, i: (kv, i, 0))],
        compiler_params=_params("parallel", "parallel"), name="b_fwd")(qn, kn, proj)


def _gqa_bwd(qn, kn, proj, o, lse, do):
    T = qn.shape[0]
    GW = 4 * HEAD_B
    scale = HEAD_B ** -0.5

    def body(q_ref, k_ref, v_ref, o_ref, l_ref, do_ref, dq_ref, dk_ref, dv_ref):
        i = pl.program_id(1)

        @pl.when(i == 0)
        def _():
            dk_ref[...] = jnp.zeros_like(dk_ref)
            dv_ref[...] = jnp.zeros_like(dv_ref)

        k, v = k_ref[...], v_ref[...]
        lse_all = l_ref[...]
        for g in range(4):
            cols = slice(g * HEAD_B, (g + 1) * HEAD_B)
            q, dob = q_ref[:, cols], do_ref[:, cols]
            delta = jnp.sum(dob.astype(F32) * o_ref[:, cols].astype(F32), axis=-1, keepdims=True)
            s = lax.dot_general(q, k, (NT, ((), ())), preferred_element_type=F32) * scale
            p = jnp.exp(s - lse_all[:, g:g + 1])
            dp = lax.dot_general(dob, v, (NT, ((), ())), preferred_element_type=F32)
            ds = (p * (dp - delta) * scale).astype(BF16)
            dq_ref[:, cols] = jnp.dot(ds, k, preferred_element_type=F32).astype(dq_ref.dtype)
            dk_ref[...] += lax.dot_general(ds, q, (TN, ((), ())), preferred_element_type=F32)
            dv_ref[...] += lax.dot_general(p.astype(BF16), dob, (TN, ((), ())), preferred_element_type=F32)

    return pl.pallas_call(
        body, out_shape=[jax.ShapeDtypeStruct((T, 2 * GW), BF16), jax.ShapeDtypeStruct((T, 2 * HEAD_B), F32),
                         jax.ShapeDtypeStruct((T, 2 * HEAD_B), F32)],
        grid=(2, T // QB_B),
        in_specs=[_bs((QB_B, GW), lambda kv, i: (i, kv)), _bs((T, HEAD_B), lambda kv, i: (0, kv)),
                  _bs((T, HEAD_B), lambda kv, i: (0, B_V // HEAD_B + kv)), _bs((QB_B, GW), lambda kv, i: (i, kv)),
                  _bs((None, QB_B, HEAD_B), lambda kv, i: (kv, i, 0)), _bs((QB_B, GW), lambda kv, i: (i, kv))],
        out_specs=[_bs((QB_B, GW), lambda kv, i: (i, kv)), _bs((T, HEAD_B), lambda kv, i: (0, kv)),
                   _bs((T, HEAD_B), lambda kv, i: (0, kv))],
        compiler_params=_params("parallel", "arbitrary"), name="b_bwd")(qn, kn, proj, o, lse, do)


def _local_step(x, target, small, get_w, put_g, deps=()):
    T, D = x.shape
    gs = {}

    x1, ffn1_saved = _ffn_fwd("ffn1", x, small["ffn1_norm"], get_w, deps)
    (h2,) = _ew("mix_norm", lambda xv, g: _norm_fwd(xv, g), [_tiled(x1), _whole(small["mix_norm"])], [(BF16, D)], n_rows=T, rows=512)
    w_in = get_w("w_in", h2)
    nq = w_in.shape[2]
    tpq = nq // WIDTH_A

    def proj_tile(j, k):
        c = j * tpq + k
        return jnp.where(c < 3 * len(DILATIONS), (c % 3) * 3 + c // 3, c)

    proj = _mm("mix_in", (4, T // 1024, tpq),
               [(h2, _bs((1024, D), lambda j, i, k: (i, 0)), w_in, _bs((None, D, WIDTH_A), lambda j, i, k: (j, 0, k)))],
               jax.ShapeDtypeStruct((T, IN_WIDTH), BF16), _bs((1024, WIDTH_A), lambda j, i, k: (i, proj_tile(j, k))), NN)

    bias = _bias_tiles(small["rel_bias"])
    a_outs, a_lses = [], []
    for grp, d in enumerate(DILATIONS):
        o, l = _dil_fwd(proj, bias[grp * HEADS_A:(grp + 1) * HEADS_A], grp, d)
        a_outs.append(o)
        a_lses.append(l)
    o_a = _combine_fwd(a_outs, a_lses)

    cos, sin = _rope_tables(T)
    qn = _qk_fwd("b_qnorm", proj, B_Q, 8, small["q_norm"], cos, sin)
    kn = _qk_fwd("b_knorm", proj, B_K, 2, small["k_norm"], cos, sin)
    o_b, lse_b = _gqa_fwd(qn, kn, proj)

    wa, wb3, w_out3 = get_w("w_branch_a", o_b), get_w("w_branch_b", o_b).reshape(1, D, D), get_w("w_out", o_b).reshape(1, D, D)
    t_a = _mm_cols("mix_branch_a", o_a, wa, tm=512, tn=256, out_dtype=BF16, cat=True)
    t_b = _mm_cols("mix_branch_b", o_b, wb3, tm=512, tn=512, out_dtype=BF16, cat=True)
    bg_a, bg_b = small["b_gate"][:, :D], small["b_gate"][:, D:]

    def merge(ta, tb, ga, gb_, ba, bb):
        sa, sb = _sigmoid(ga.astype(F32) + ba), _sigmoid(gb_.astype(F32) + bb)
        return sa * ta.astype(F32) + sb * tb.astype(F32)

    gate_ins = [_tiled(proj, D, G_A // D), _tiled(proj, D, G_B // D), _whole(bg_a), _whole(bg_b)]
    (merged,) = _ew("mix_merge", merge, [_tiled(t_a), _tiled(t_b)] + gate_ins, [(BF16, D)], n_rows=T, rows=512)
    x2 = _mm_cols("mix_out", merged, w_out3, tm=512, tn=512, out_dtype=F32, cat=True,
                  extras=[x1], epilogue=lambda acc, xv: xv + acc)
    x3, ffn2_saved = _ffn_fwd("ffn2", x2, small["ffn2_norm"], get_w)

    def head(xv, g, tv):
        r = _rstd(xv)
        xh = xv * r
        e = xh * g - tv
        dy = e * (1.0 / D)
        dxh = dy * g
        dx = r * (dxh - xh * jnp.mean(dxh * xh, axis=-1, keepdims=True))
        return dx, 0.5 * dx, _colsum(e * e) * (0.5 / D), _colsum(dy * xh)

    dx3, dx3_half, loss_cols, g_final = _ew("loss_head", head, [_tiled(x3), _whole(small["final_norm"].reshape(1, D)), _tiled(target)],
                                            [(F32, D), (BF16, D)], n_rows=T, rows=256, reds=(D, D))
    gs["final_norm"] = g_final.reshape(D)

    dx2, _, gs["ffn2_norm"] = _ffn_bwd("ffn2", x2, small["ffn2_norm"], get_w, put_g, ffn2_saved, dx3, dx3_half)

    (dmix,) = _ew("mix_bwd_cast", lambda v: v, [_tiled(dx2)], [(BF16, D)], n_rows=T, rows=512)
    g_out = _mm_wgrad("mix_bwd_dwout", merged, dmix, a_cols=D // 4, b_cols=None, tm=256, tn=512, J=4).reshape(D, D)
    dmerged = _mm_rows_t("mix_bwd_dmerged", dmix, w_out3, tm=512, out_dtype=BF16).reshape(T, D)

    def merge_bwd(dm, ta, tb, ga, gb_, ba, bb):
        dm, ta, tb = dm.astype(F32), ta.astype(F32), tb.astype(F32)
        sa, sb = _sigmoid(ga.astype(F32) + ba), _sigmoid(gb_.astype(F32) + bb)
        dga, dgb = dm * ta * sa * (1.0 - sa), dm * tb * sb * (1.0 - sb)
        return dm * sa, dm * sb, dga, dgb, _colsum(dga), _colsum(dgb)

    dta, dtb, dga, dgb, dba, dbb = _ew("mix_bwd_merge", merge_bwd, [_tiled(dmerged), _tiled(t_a), _tiled(t_b)] + gate_ins,
                                       [(BF16, D)] * 4, n_rows=T, rows=256, reds=(D, D))
    gs["b_gate"] = jnp.concatenate([dba.reshape(1, D), dbb.reshape(1, D)], axis=1)

    g_a = _mm_wgrad("mix_bwd_dwa", o_a, dta, a_cols=None, b_cols=D // 4, tm=WIDTH_A, tn=256, J=4)
    g_b = _mm_wgrad("mix_bwd_dwb", o_b, dtb, a_cols=D // 4, b_cols=None, tm=256, tn=512, J=4).reshape(D, D)
    deps = put_g({"w_out": g_out, "w_branch_a": g_a, "w_branch_b": g_b})
    do_a = _mm("mix_bwd_doa", (T // 1024,),
               [(dta, _bs((1024, D // 4), lambda i, j=j: (i, j)), wa, _bs((None, WIDTH_A, D // 4), lambda i, j=j: (j, 0, 0)))
                for j in range(4)],
               jax.ShapeDtypeStruct((T, WIDTH_A), BF16), _bs((1024, WIDTH_A), lambda i: (i, 0)), NT, deps=deps)
    do_b = _mm_rows_t("mix_bwd_dob", dtb, wb3, tm=512, out_dtype=BF16).reshape(T, D)

    dqn, dkn, dv_b = _gqa_bwd(qn, kn, proj, o_b, lse_b, do_b)
    dq_b, gs["q_norm"] = _qk_bwd("b_bwd_qnorm", dqn, proj, B_Q, 8, small["q_norm"], cos, sin)
    dk_b, gs["k_norm"] = _qk_bwd("b_bwd_knorm", dkn, proj, B_K, 2, small["k_norm"], cos, sin)

    do_groups, c_groups = _combine_bwd(do_a, a_outs, a_lses)
    dqs, dks, dvs, dbs = [], [], [], []
    for grp, d in enumerate(DILATIONS):
        dq, dk, dv, db = _dil_bwd(proj, bias[grp * HEADS_A:(grp + 1) * HEADS_A], do_groups[grp], a_lses[grp], c_groups[grp], grp, d)
        dqs.append(dq), dks.append(dk), dvs.append(dv), dbs.append(db)
    gs["rel_bias"] = _bias_grad(jnp.concatenate(dbs, axis=0))

    dproj = jnp.concatenate([p.astype(BF16) for p in dqs + dks + dvs + [dq_b, dk_b, dv_b, dga, dgb]], axis=1)
    nq = w_in.shape[2]
    deps = put_g({"w_in": _mm_wgrad("mix_bwd_dwin", h2, dproj, a_cols=None, b_cols=nq, tm=512, tn=512, J=4)})
    dh2 = _mm("mix_bwd_dh", (T // 512, D // 512),
              [(dproj, _bs((512, nq), lambda i, k, j=j: (i, j)), w_in, _bs((None, 512, nq), lambda i, k, j=j: (j, k, 0)))
               for j in range(4)],
              jax.ShapeDtypeStruct((T, D), F32), _bs((512, 512), lambda i, k: (i, k)), NT, deps=deps)

    def nb(xv, gv, dhv, dres):
        dx, dgr = _norm_bwd(xv, gv, dhv)
        dx = dx + dres
        return dx, 0.5 * dx, _colsum(dgr)

    dx1, dx1_half, g_mix = _ew("mix_bwd_norm", nb, [_tiled(x1), _whole(small["mix_norm"]), _tiled(dh2), _tiled(dx2)],
                               [(F32, D), (BF16, D)], n_rows=T, rows=256, reds=(D,))
    gs["mix_norm"] = g_mix.reshape(1, D)

    dx0, _, gs["ffn1_norm"] = _ffn_bwd("ffn1", x, small["ffn1_norm"], get_w, put_g, ffn1_saved, dx1, dx1_half)
    return loss_cols.reshape(1, D), dx0, gs


def _position():
    return lax.axis_index("x"), lax.axis_index("y"), lax.axis_index("c")


def _any_specs(n):
    return [pl.BlockSpec(memory_space=pl.ANY)] * n


HBM_SPEC = pl.BlockSpec(memory_space=pltpu.HBM)
SEM_SPEC = pl.BlockSpec(memory_space=pltpu.SEMAPHORE)
DATAFLOW_EFFECT = pltpu.SideEffectType.DATAFLOW_SIDE_EFFECTING
N_PEER_CHIPS = 3
LANES = 128


def _quarter_copies(srcs, lands, send_sems, recv_sems, scatter):
    x, y, c = _position()
    me = 2 * x + y
    peers = [(1 - x, y, c), (x, 1 - y, c), (1 - x, 1 - y, c)]
    copies = []
    for src, land, send, recv in zip(srcs, lands, send_sems, recv_sems):
        for p, (px, py, pc) in enumerate(peers):
            copies.append(pltpu.make_async_remote_copy(
                src_ref=src.at[2 * px + py] if scatter else src, dst_ref=land.at[me], send_sem=send.at[p], recv_sem=recv.at[p],
                device_id=(px, py, pc), device_id_type=MESH))
    return copies


def _exchange_start(name, srcs, lands, scatter):
    n = len(srcs)

    def body(*refs):
        src_refs, land_refs = refs[:n], refs[n:2 * n]
        send_sems, recv_sems = refs[2 * n:3 * n], refs[3 * n:4 * n]
        token, local_sems = refs[6 * n], refs[6 * n + 1]
        me = 2 * lax.axis_index("x") + lax.axis_index("y")
        own = [pltpu.make_async_copy(src.at[me] if scatter else src, land.at[me], local_sems.at[i])
               for i, (src, land) in enumerate(zip(src_refs, land_refs))]
        for cp in own:
            cp.start()
        for cp in _quarter_copies(src_refs, land_refs, send_sems, recv_sems, scatter):
            cp.start()
        token[...] = jnp.zeros_like(token)
        for cp in own:
            cp.wait()

    sem = pltpu.SemaphoreType.DMA((N_PEER_CHIPS,))
    out_shape = [sem] * (2 * n) + [pltpu.HBM(a.shape, a.dtype) for a in list(srcs) + list(lands)]
    out_shape += [jax.ShapeDtypeStruct((8, LANES), F32)]
    res = pl.pallas_call(
        body, name=name, out_shape=out_shape, in_specs=[HBM_SPEC] * (2 * n),
        out_specs=[SEM_SPEC] * (2 * n) + [HBM_SPEC] * (2 * n) + [pl.BlockSpec(memory_space=pltpu.VMEM)],
        input_output_aliases={i: 2 * n + i for i in range(2 * n)},
        scratch_shapes=[pltpu.SemaphoreType.DMA((n,))],
        compiler_params=pltpu.CompilerParams(has_side_effects=DATAFLOW_EFFECT),
    )(*[pltpu.with_memory_space_constraint(a, pltpu.HBM) for a in list(srcs) + list(lands)])
    return res[:n], res[n:2 * n], res[2 * n:3 * n], res[3 * n:4 * n], res[4 * n]


def _exchange_wait(name, srcs, lands, send_sems, recv_sems, after, scatter):
    n = len(srcs)

    def body(*refs):
        src_refs, land_refs = refs[:n], refs[n:2 * n]
        sends, recvs = refs[2 * n:3 * n], refs[3 * n:4 * n]
        for cp in _quarter_copies(src_refs, land_refs, sends, recvs, scatter):
            cp.wait_send()
            cp.wait_recv()

    res = pl.pallas_call(
        body, name=name, out_shape=[pltpu.HBM(a.shape, a.dtype) for a in list(srcs) + list(lands)],
        in_specs=[HBM_SPEC] * (2 * n) + [SEM_SPEC] * (2 * n) + [pl.BlockSpec(memory_space=pl.ANY)],
        out_specs=[HBM_SPEC] * (2 * n), input_output_aliases={i: i for i in range(2 * n)},
        compiler_params=pltpu.CompilerParams(has_side_effects=DATAFLOW_EFFECT),
    )(*srcs, *lands, *send_sems, *recv_sems, after)
    return res[n:]


def _own_slot(stack_shape, own, dtype):
    me = 2 * lax.axis_index("x") + lax.axis_index("y")
    return lax.dynamic_update_slice(jnp.zeros(stack_shape, dtype), own[None], (me,) + (0,) * own.ndim)


def _swap_with_sibling(parts):
    n = len(parts)

    def body(*refs):
        ins, outs = refs[:n], refs[n:2 * n]
        send_sems, recv_sems = refs[2 * n:]
        x, y, c = _position()
        copies = []
        for i in range(n):
            cp = pltpu.make_async_remote_copy(ins[i], outs[i], send_sems.at[i], recv_sems.at[i],
                                              device_id=(x, y, 1 - c), device_id_type=MESH)
            cp.start()
            copies.append(cp)
        for cp in copies:
            cp.wait()

    return pl.pallas_call(
        body, out_shape=[jax.ShapeDtypeStruct(s.shape, s.dtype) for s in parts],
        in_specs=_any_specs(n), out_specs=_any_specs(n),
        scratch_shapes=[pltpu.SemaphoreType.DMA((n,)), pltpu.SemaphoreType.DMA((n,))],
        compiler_params=pltpu.CompilerParams(has_side_effects=True), name="swap_with_sibling")(*parts)


def _allreduce_small(buf):
    R, C = buf.shape
    flips = [(fx, fy, fc) for fx in (0, 1) for fy in (0, 1) for fc in (0, 1)][1:]

    def body(in_ref, out_ref, land_ref, send_sems, recv_sems):
        x, y, c = _position()
        me = 4 * x + 2 * y + c
        copies = []
        for k, (fx, fy, fc) in enumerate(flips):
            px, py, pc = (1 - x if fx else x), (1 - y if fy else y), (1 - c if fc else c)
            cp = pltpu.make_async_remote_copy(in_ref, land_ref.at[me], send_sems.at[k], recv_sems.at[k],
                                              device_id=(px, py, pc), device_id_type=MESH)
            cp.start()
            copies.append(cp)
        land_ref[me] = in_ref[...]
        for cp in copies:
            cp.wait()
        acc = land_ref[0]
        for k in range(1, 8):
            acc = acc + land_ref[k]
        out_ref[...] = acc

    return pl.pallas_call(
        body, out_shape=jax.ShapeDtypeStruct((R, C), F32),
        in_specs=[pl.BlockSpec(memory_space=pltpu.VMEM)], out_specs=pl.BlockSpec(memory_space=pltpu.VMEM),
        scratch_shapes=[pltpu.VMEM((8, R, C), F32), pltpu.SemaphoreType.DMA((7,)), pltpu.SemaphoreType.DMA((7,))],
        compiler_params=pltpu.CompilerParams(has_side_effects=True), name="allreduce_small")(buf)


def _adamw_math(w, g, m, v):
    m2 = ADAM_B1 * m + (1.0 - ADAM_B1) * g
    v2 = ADAM_B2 * v + (1.0 - ADAM_B2) * (g * g)
    m_hat = m2 / (1.0 - ADAM_B1 ** ADAM_STEP)
    v_hat = v2 / (1.0 - ADAM_B2 ** ADAM_STEP)
    delta = -ADAM_LR * (m_hat / (jnp.sqrt(v_hat) + ADAM_EPS) + ADAM_WD * w)
    return delta, m2, v2


def _adamw_big(name, w, m, v, part_mine, part_sibling):
    R, C = w.shape
    rows = 256 if R % 256 == 0 else R // 2 if (R // 2) % 8 == 0 else R

    def fn(wv, mv, vv, a, b):
        g = a + b
        return (g,) + _adamw_math(wv, g, mv, vv)

    return _ew(name, fn, [_tiled(w), _tiled(m), _tiled(v), _tiled(part_mine), _tiled(part_sibling)], [(F32, C)] * 4, n_rows=R, rows=rows)


def _sum_four(name, stack):
    _, R, C = stack.shape
    rows = 256 if R % 256 == 0 else R // 2 if (R // 2) % 8 == 0 else R
    flat = stack.reshape(4 * R, C)
    nrb = R // rows

    def fn(a, b, c, d):
        return ((a.astype(F32) + b.astype(F32)) + c.astype(F32)) + d.astype(F32)

    (out,) = _ew(name, fn, [_tiled(flat, None, 0, k * nrb) for k in range(4)], [(F32, C)], n_rows=R, rows=rows)
    return out


BIG = ("ffn1_w1", "ffn1_w3", "ffn1_w2", "w_in", "w_branch_a", "w_branch_b", "w_out", "ffn2_w1", "ffn2_w3", "ffn2_w2")
SMALL = ("ffn1_norm", "mix_norm", "b_gate", "q_norm", "k_norm", "rel_bias", "ffn2_norm", "final_norm")
ORDER = ("ffn1_norm", "ffn1_w1", "ffn1_w3", "ffn1_w2", "mix_norm", "w_in", "b_gate", "q_norm", "k_norm", "rel_bias",
         "w_branch_a", "w_branch_b", "w_out", "ffn2_norm", "ffn2_w1", "ffn2_w3", "ffn2_w2", "final_norm")
TRANSPOSED = ("ffn1_w1", "ffn1_w3", "ffn2_w1", "ffn2_w3")
GATHER_GROUPS = (("ffn1_w1", "ffn1_w3"), ("ffn1_w2",), ("w_in",), ("w_branch_a", "w_branch_b", "w_out"),
                 ("ffn2_w1", "ffn2_w3", "ffn2_w2"))


def _pack_small(d):
    rows = []
    for n in SMALL:
        flat = d[n].reshape(-1)
        pad = (-flat.shape[0]) % LANES
        rows.append(jnp.pad(flat, (0, pad)).reshape(-1, LANES))
    buf = jnp.concatenate(rows, axis=0)
    return jnp.pad(buf, ((0, (-buf.shape[0]) % 8), (0, 0)))


def _unpack_small(buf, like):
    out, r = {}, 0
    for n in SMALL:
        size = like[n].size
        nr = -(-size // LANES)
        out[n] = buf[r:r + nr].reshape(-1)[:size].reshape(like[n].shape)
        r += nr
    return out


def kernel(x, ffn1_norm, ffn1_w1, ffn1_w3, ffn1_w2, mix_norm, w_in, b_gate, q_norm, k_norm, rel_bias, w_branch_a, w_branch_b, w_out, ffn2_norm, ffn2_w1, ffn2_w3, ffn2_w2, final_norm, loss_target, m_ffn1_norm, m_ffn1_w1, m_ffn1_w3, m_ffn1_w2, m_mix_norm, m_w_in, m_b_gate, m_q_norm, m_k_norm, m_rel_bias, m_w_branch_a, m_w_branch_b, m_w_out, m_ffn2_norm, m_ffn2_w1, m_ffn2_w3, m_ffn2_w2, m_final_norm, v_ffn1_norm, v_ffn1_w1, v_ffn1_w3, v_ffn1_w2, v_mix_norm, v_w_in, v_b_gate, v_q_norm, v_k_norm, v_rel_bias, v_w_branch_a, v_w_branch_b, v_w_out, v_ffn2_norm, v_ffn2_w1, v_ffn2_w3, v_ffn2_w2, v_final_norm):
    given = dict(locals())
    w = {n: given[n] for n in ORDER}
    m = {n: given["m_" + n] for n in ORDER}
    v = {n: given["v_" + n] for n in ORDER}
    T, D = x.shape[1], x.shape[2]

    def stored(a, n):
        a = a.reshape(a.shape[1:])
        return a.T if n in TRANSPOSED else a

    def returned(a, n):
        return (a.T if n in TRANSPOSED else a).reshape(w[n].shape)

    quarter = {n: stored(w[n], n) for n in BIG}
    q16 = [quarter[n].astype(BF16) for n in BIG]
    send, recv, src_thru, land_thru, token = _exchange_start(
        "gather_start", q16, [lax.empty((4,) + q.shape, BF16) for q in q16], scatter=False)
    index = {n: i for i, n in enumerate(BIG)}
    ready = {}

    def get_w(name, after):
        if name not in ready:
            group = next(g for g in GATHER_GROUPS if name in g)
            ids = [index[n] for n in group]
            stacks = _exchange_wait("gather_wait_" + group[0], [src_thru[i] for i in ids], [land_thru[i] for i in ids],
                                    [send[i] for i in ids], [recv[i] for i in ids], after, scatter=False)
            for n, st in zip(group, stacks):
                ready[n] = st.reshape(D, D) if n in ("w_branch_b", "w_out") else st
        return ready[name]

    in_flight = []

    def put_g(grads):
        names = list(grads)
        stacks = [grads[n].reshape((4,) + quarter[n].shape) for n in names]
        lands = [lax.empty(s.shape, BF16) for s in stacks]
        started = _exchange_start("scatter_start_" + names[0], stacks, lands, scatter=True)
        in_flight.append((names,) + tuple(started[:4]))
        return [started[4]]

    small = {n: w[n] for n in SMALL}
    loss_cols, grad_x, gs = _local_step(x.reshape(T, D), loss_target.reshape(T, D), small, get_w, put_g, deps=[token])
    loss = lax.psum(jnp.sum(loss_cols), ("x", "y", "c"))

    landed = {}
    for names, s_sem, r_sem, srcs, lands in in_flight:
        got = _exchange_wait("scatter_wait_" + names[0], srcs, lands, s_sem, r_sem, grad_x, scatter=True)
        landed.update(zip(names, got))
    partial = [_sum_four(f"sum4_{n}", landed[n]) for n in BIG]
    other = _swap_with_sibling(partial)
    grads, deltas, new_m, new_v = {}, {}, {}, {}
    for n, mine, theirs in zip(BIG, partial, other):
        res = _adamw_big(f"adamw_{n}", quarter[n], stored(m[n], n), stored(v[n], n), mine, theirs)
        grads[n], deltas[n], new_m[n], new_v[n] = [returned(r, n) for r in res]

    gs = {n: gs[n].reshape(w[n].shape) for n in SMALL}
    g_small = _allreduce_small(_pack_small(gs))
    packed = [_pack_small({n: d[n] for n in SMALL}) for d in (w, m, v)]
    R = g_small.shape[0]
    res = _ew("adamw_small", lambda wv, mv, vv, g: (g,) + _adamw_math(wv, g, mv, vv),
              [_tiled(packed[0]), _tiled(packed[1]), _tiled(packed[2]), _tiled(g_small)], [(F32, LANES)] * 4, n_rows=R, rows=R)
    for d, buf in zip((grads, deltas, new_m, new_v), res):
        d.update(_unpack_small(buf, w))

    return (loss, grad_x.reshape(x.shape), *[grads[n] for n in ORDER], *[deltas[n] for n in ORDER],
            *[new_m[n] for n in ORDER], *[new_v[n] for n in ORDER])
```

```python
import functools
import math

import numpy as np
import jax
import jax.numpy as jnp
from jax import lax
from jax.experimental import pallas as pl
from jax.experimental.pallas import tpu as pltpu

F32 = jnp.float32
BF16 = jnp.bfloat16
MESH = pl.DeviceIdType.MESH

NEG_INF = -1e30
EPS = 1e-6
GRID_W = 64
ROPE_THETA = 10000.0
DILATIONS = (1, 4, 16)
BAND_HALF = 64
HEAD_A = 64
HEADS_A = 8
WIDTH_A = HEADS_A * HEAD_A
HEAD_B = 128
N_BUCKETS = 32
MAX_DISTANCE = 1024
ADAM_LR, ADAM_B1, ADAM_B2, ADAM_EPS, ADAM_WD, ADAM_STEP = 0.001, 0.9, 0.999, 1e-08, 0.01, 10

A_Q, A_K, A_V = 0, 1536, 3072
B_Q, B_K, B_V = 4608, 5632, 5888
G_A, G_B = 6144, 7168
IN_WIDTH = 8192

VMEM_LIMIT_BYTES = 56 * 1024 * 1024
QB_A = 128
QB_B = 256


def _params(*sem):
    return pltpu.CompilerParams(dimension_semantics=sem, vmem_limit_bytes=VMEM_LIMIT_BYTES)


def _bs(shape, fn):
    return pl.BlockSpec(shape, fn)


def _mm(name, grid, pairs, out_shape, out_spec, dims, *, reduce_axis=None, extras=(), epilogue=None, deps=()):
    n_pairs, n_extra, n_deps = len(pairs), len(extras), len(deps)
    operands = [p[0] for p in pairs] + [p[2] for p in pairs] + [e[0] for e in extras] + list(deps)
    in_specs = [p[1] for p in pairs] + [p[3] for p in pairs] + [e[1] for e in extras] + _any_specs(n_deps)
    tile = tuple(s for s in out_spec.block_shape if s is not None)
    n_steps = grid[reduce_axis] if reduce_axis is not None else 1

    def body(*refs):
        a_refs, b_refs = refs[:n_pairs], refs[n_pairs:2 * n_pairs]
        e_refs = refs[2 * n_pairs:2 * n_pairs + n_extra]
        o_ref = refs[2 * n_pairs + n_extra + n_deps]
        acc = None
        for a_ref, b_ref in zip(a_refs, b_refs):
            t = lax.dot_general(a_ref[...], b_ref[...], (dims, ((), ())), preferred_element_type=F32)
            acc = t if acc is None else acc + t

        def finish(v):
            if epilogue is not None:
                v = epilogue(v, *[e[...] for e in e_refs])
            o_ref[...] = v.astype(o_ref.dtype)

        if reduce_axis is None:
            finish(acc)
        else:
            acc_ref = refs[-1]
            k = pl.program_id(reduce_axis)

            @pl.when(k == 0)
            def _():
                acc_ref[...] = acc

            @pl.when(k > 0)
            def _():
                acc_ref[...] += acc

            @pl.when(k == n_steps - 1)
            def _():
                finish(acc_ref[...])

    sem = ["parallel"] * len(grid)
    if reduce_axis is not None:
        sem[reduce_axis] = "arbitrary"
    return pl.pallas_call(
        body, out_shape=out_shape, grid=grid, in_specs=in_specs, out_specs=out_spec,
        scratch_shapes=[pltpu.VMEM(tile, F32)] if reduce_axis is not None else [],
        compiler_params=_params(*sem), name=name)(*operands)


NN = ((1,), (0,))
NT = ((1,), (1,))
TN = ((0,), (0,))


def _mm_cols(name, a, w, *, tm, tn, out_dtype, cat, extras=(), epilogue=None):
    M, K = a.shape
    J, _, n = w.shape
    tn = min(tn, n)
    nb = n // tn
    if cat:
        shape, spec = (M, J * n), _bs((tm, tn), lambda j, i, k: (i, j * nb + k))
    else:
        shape, spec = (J, M, n), _bs((None, tm, tn), lambda j, i, k: (j, i, k))
    ex = [(e, _bs((tm, tn), lambda j, i, k: (i, j * nb + k))) for e in extras]
    return _mm(name, (J, M // tm, nb),
               [(a, _bs((tm, K), lambda j, i, k: (i, 0)), w, _bs((None, K, tn), lambda j, i, k: (j, 0, k)))],
               jax.ShapeDtypeStruct(shape, out_dtype), spec, NN, extras=ex, epilogue=epilogue)


def _mm_rows_t(name, a, w, *, tm, out_dtype):
    M, N = a.shape
    J, f, _ = w.shape
    return _mm(name, (J, M // tm),
               [(a, _bs((tm, N), lambda j, i: (i, 0)), w, _bs((None, f, N), lambda j, i: (j, 0, 0)))],
               jax.ShapeDtypeStruct((J, M, f), out_dtype), _bs((None, tm, f), lambda j, i: (j, i, 0)), NT)


def _mm_wgrad(name, a, b, *, a_cols, b_cols, tm, tn, J):
    def pick(arr, cols, t):
        if arr.ndim == 3:
            T, c = arr.shape[1], arr.shape[2]
            t = min(t, c)
            return T, c, t, (lambda sel: _bs((None, T, t), lambda j, i, k: (j, 0, sel(i, k))))
        T = arr.shape[0]
        c = arr.shape[1] if cols is None else cols
        t = min(t, c)
        per = c // t
        if cols is None:
            return T, c, t, (lambda sel: _bs((T, t), lambda j, i, k: (0, sel(i, k))))
        return T, c, t, (lambda sel: _bs((T, t), lambda j, i, k: (0, j * per + sel(i, k))))
    _, ca, tm, mk_a = pick(a, a_cols, tm)
    _, cb, tn, mk_b = pick(b, b_cols, tn)
    return _mm(name, (J, ca // tm, cb // tn),
               [(a, mk_a(lambda i, k: i), b, mk_b(lambda i, k: k))],
               jax.ShapeDtypeStruct((J, ca, cb), BF16), _bs((None, tm, tn), lambda j, i, k: (j, i, k)), TN)


def _tiled(arr, width=None, col=0, rowblk=0):
    return ("t", arr, arr.shape[1] if width is None else width, col, rowblk)


def _table(arr):
    return ("f", arr)


def _whole(arr):
    return ("w", arr)


def _ew(name, fn, ins, outs, *, n_rows, rows, reds=(), ncols=1, deps=()):
    nrb = n_rows // rows
    n_deps = len(deps)
    operands, in_specs = [], []
    for spec in ins:
        if spec[0] == "t":
            _, arr, width, col, rowblk = spec
            step = 1 if ncols > 1 else 0
            in_specs.append(_bs((rows, width), lambda c, i, col=col, rowblk=rowblk, step=step: (rowblk + i, col + c * step)))
        elif spec[0] == "f":
            arr = spec[1]
            in_specs.append(_bs((rows, arr.shape[1]), lambda c, i: (i, 0)))
        else:
            arr = spec[1]
            nd = arr.ndim
            if nd == 3:
                in_specs.append(_bs((None,) + arr.shape[1:], lambda c, i: (c, 0, 0)))
            else:
                in_specs.append(_bs(arr.shape, lambda c, i, nd=nd: (0,) * nd))
        operands.append(arr)
    out_shapes = [jax.ShapeDtypeStruct((n_rows, ncols * w), dt) for dt, w in outs]
    out_specs = [_bs((rows, w), lambda c, i: (i, c)) for _, w in outs]
    out_shapes += [jax.ShapeDtypeStruct((ncols, 1, w), F32) for w in reds]
    out_specs += [_bs((None, 1, w), lambda c, i: (c, 0, 0)) for w in reds]
    n_in, n_out, n_red = len(ins), len(outs), len(reds)
    operands += list(deps)
    in_specs += _any_specs(n_deps)

    def body(*refs):
        vals = fn(*[r[...] for r in refs[:n_in]])
        if not isinstance(vals, (tuple, list)):
            vals = (vals,)
        o_refs = refs[n_in + n_deps:]
        for o_ref, v in zip(o_refs[:n_out], vals[:n_out]):
            o_ref[...] = v.astype(o_ref.dtype)
        if n_red:
            i = pl.program_id(1)
            for r_ref, v in zip(o_refs[n_out:], vals[n_out:]):
                @pl.when(i == 0)
                def _(r_ref=r_ref):
                    r_ref[...] = jnp.zeros_like(r_ref)
                r_ref[...] += v

    res = pl.pallas_call(
        body, out_shape=out_shapes, grid=(ncols, nrb), in_specs=in_specs, out_specs=out_specs,
        compiler_params=_params("parallel", "arbitrary" if n_red else "parallel"), name=name)(*operands)
    return res


def _colsum(v):
    return jnp.sum(v, axis=0, keepdims=True)


def _rstd(x):
    return lax.rsqrt(jnp.mean(x * x, axis=-1, keepdims=True) + EPS)


def _sigmoid(x):
    return 1.0 / (1.0 + jnp.exp(-x))


def _norm_fwd(x, g):
    return x * _rstd(x) * g


def _norm_bwd(x, g, dy):
    r = _rstd(x)
    xh = x * r
    dxh = dy * g
    dx = r * (dxh - xh * jnp.mean(dxh * xh, axis=-1, keepdims=True))
    return dx, dy * xh


def _ffn_fwd(tag, x, gain, get_w, deps=()):
    T, D = x.shape
    (h,) = _ew(f"{tag}_norm", lambda xv, g: _norm_fwd(xv, g), [_tiled(x), _whole(gain)], [(BF16, D)], n_rows=T, rows=512,
               deps=deps)
    w1, w3 = get_w(f"{tag}_w1", h), get_w(f"{tag}_w3", h)
    J, f, _ = w1.shape
    tm = 1024

    def up(h_ref, w1_ref, w3_ref, u_ref, g_ref, a_ref):
        hv = h_ref[...]
        u = lax.dot_general(hv, w1_ref[...], (NT, ((), ())), preferred_element_type=F32)
        g = lax.dot_general(hv, w3_ref[...], (NT, ((), ())), preferred_element_type=F32)
        u_ref[...] = u.astype(BF16)
        g_ref[...] = g.astype(BF16)
        a_ref[...] = (u * _sigmoid(u) * g).astype(BF16)

    slab = _bs((None, tm, f), lambda j, i: (j, i, 0))
    w_spec = _bs((None, f, D), lambda j, i: (j, 0, 0))
    u, g, a = pl.pallas_call(
        up, out_shape=[jax.ShapeDtypeStruct((J, T, f), BF16)] * 3, grid=(J, T // tm),
        in_specs=[_bs((tm, D), lambda j, i: (i, 0)), w_spec, w_spec], out_specs=[slab] * 3,
        compiler_params=_params("parallel", "parallel"), name=f"{tag}_up")(h, w1, w3)
    w2 = get_w(f"{tag}_w2", a)
    y = _mm(f"{tag}_down", (T // 1024, D // 512),
            [(a, _bs((None, 1024, f), lambda i, k, j=j: (j, i, 0)), w2, _bs((None, f, 512), lambda i, k, j=j: (j, 0, k)))
             for j in range(J)],
            jax.ShapeDtypeStruct((T, D), F32), _bs((1024, 512), lambda i, k: (i, k)), NN,
            extras=[(x, _bs((1024, 512), lambda i, k: (i, k)))], epilogue=lambda acc, xv: xv + 0.5 * acc)
    return y, (h, u, g, a)


def _ffn_bwd(tag, x, gain, get_w, put_g, saved, dy, dy_half):
    h, u, g, a = saved
    T, D = x.shape
    w1, w3, w2 = [get_w(f"{tag}_{n}", dy_half) for n in ("w1", "w3", "w2")]
    J, f, _ = w1.shape
    dw2 = _mm_wgrad(f"{tag}_bwd_dw2", a, dy_half, a_cols=None, b_cols=None, tm=f, tn=512, J=J)
    deps = put_g({f"{tag}_w2": dw2})
    tm = 1024

    def up_bwd(dy_ref, w2_ref, u_ref, g_ref, *rest):
        du_ref, dg_ref = rest[-2:]
        da = lax.dot_general(dy_ref[...], w2_ref[...], (NT, ((), ())), preferred_element_type=F32)
        uv, gv = u_ref[...].astype(F32), g_ref[...].astype(F32)
        s = _sigmoid(uv)
        du_ref[...] = (da * gv * (s * (1.0 + uv * (1.0 - s)))).astype(BF16)
        dg_ref[...] = (da * (uv * s)).astype(BF16)

    slab = _bs((None, tm, f), lambda j, i: (j, i, 0))
    du, dg = pl.pallas_call(
        up_bwd, out_shape=[jax.ShapeDtypeStruct((J, T, f), BF16)] * 2, grid=(J, T // tm),
        in_specs=[_bs((tm, D), lambda j, i: (i, 0)), _bs((None, f, D), lambda j, i: (j, 0, 0)), slab, slab] + _any_specs(len(deps)),
        out_specs=[slab] * 2, compiler_params=_params("parallel", "parallel"), name=f"{tag}_bwd_up")(dy_half, w2, u, g, *deps)
    dw1 = _mm_wgrad(f"{tag}_bwd_dw1", du, h, a_cols=None, b_cols=None, tm=f, tn=512, J=J)
    dw3 = _mm_wgrad(f"{tag}_bwd_dw3", dg, h, a_cols=None, b_cols=None, tm=f, tn=512, J=J)
    deps = deps + put_g({f"{tag}_w1": dw1, f"{tag}_w3": dw3})
    pairs = []
    for j in range(J):
        a_spec = _bs((None, 512, f), lambda i, k, j=j: (j, i, 0))
        w_spec = _bs((None, f, 512), lambda i, k, j=j: (j, 0, k))
        pairs += [(du, a_spec, w1, w_spec), (dg, a_spec, w3, w_spec)]
    dh = _mm(f"{tag}_bwd_dh", (T // 512, D // 512), pairs,
             jax.ShapeDtypeStruct((T, D), F32), _bs((512, 512), lambda i, k: (i, k)), NN, deps=deps)

    def nb(xv, gv, dhv, dres):
        dx, dgr = _norm_bwd(xv, gv, dhv)
        dx = dx + dres
        return dx, 0.5 * dx, _colsum(dgr)

    dx, dx_half, dgain = _ew(f"{tag}_bwd_norm", nb, [_tiled(x), _whole(gain), _tiled(dh), _tiled(dy)],
                             [(F32, D), (BF16, D)], n_rows=T, rows=256, reds=(D,))
    return dx, dx_half, dgain.reshape(1, D)


def _t5_bucket(rel):
    n = N_BUCKETS // 2
    max_exact = n // 2
    ret = jnp.where(rel > 0, n, 0)
    a = jnp.abs(rel)
    af = jnp.maximum(a, 1).astype(F32)
    large = max_exact + (jnp.log(af / max_exact) / math.log(MAX_DISTANCE / max_exact) * (n - max_exact)).astype(jnp.int32)
    large = jnp.minimum(large, n - 1)
    return ret + jnp.where(a < max_exact, a, large)


def _band_steps():
    qi = jnp.arange(QB_A, dtype=jnp.int32)[:, None]
    kj = jnp.arange(3 * QB_A, dtype=jnp.int32)[None, :] - QB_A
    return kj - qi


def _bias_tiles(rel_bias):
    steps = _band_steps()
    buckets = jnp.stack([_t5_bucket(steps * d) for d in DILATIONS])
    inband = (jnp.abs(steps) <= BAND_HALF).astype(jnp.int32)
    n_heads = rel_bias.shape[1]

    def body(tab_ref, b_ref, m_ref, o_ref):
        hd = pl.program_id(0)
        bkt = b_ref[...]
        acc = jnp.zeros(bkt.shape, F32)
        for b in range(N_BUCKETS):
            acc = jnp.where(bkt == b, tab_ref[b, hd], acc)
        o_ref[...] = jnp.where(m_ref[...] > 0, acc, NEG_INF)

    return pl.pallas_call(
        body, out_shape=jax.ShapeDtypeStruct((n_heads, QB_A, 3 * QB_A), F32), grid=(n_heads,),
        in_specs=[pl.BlockSpec(memory_space=pltpu.SMEM),
                  _bs((None, QB_A, 3 * QB_A), lambda hd: (hd // HEADS_A, 0, 0)),
                  _bs((QB_A, 3 * QB_A), lambda hd: (0, 0))],
        out_specs=_bs((None, QB_A, 3 * QB_A), lambda hd: (hd, 0, 0)),
        compiler_params=_params("parallel"), name="a_bias_tiles")(rel_bias, buckets, inband)


def _bias_grad(dbias):
    steps = np.arange(3 * QB_A)[None, :] - QB_A - np.arange(QB_A)[:, None]
    inband = np.abs(steps) <= BAND_HALF
    present = []
    for d in DILATIONS:
        rel = steps * d
        a = np.abs(rel)
        large = 8 + (np.log(np.maximum(a, 1) / 8.0) / math.log(MAX_DISTANCE / 8.0) * 8).astype(np.int64)
        bk = np.where(rel > 0, 16, 0) + np.where(a < 8, a, np.minimum(large, 15))
        present.append(sorted(set(bk[inband].tolist())))
    buckets = jnp.stack([_t5_bucket(_band_steps() * d) for d in DILATIONS])
    n_heads = dbias.shape[0]

    def body(b_ref, d_ref, o_ref):
        row = lax.broadcasted_iota(jnp.int32, (N_BUCKETS, n_heads), 0)
        col = lax.broadcasted_iota(jnp.int32, (N_BUCKETS, n_heads), 1)
        out = jnp.zeros((N_BUCKETS, n_heads), F32)
        for grp in range(len(DILATIONS)):
            bkt = b_ref[grp]
            for hh in range(HEADS_A):
                hd = grp * HEADS_A + hh
                ds = d_ref[hd]
                for b in present[grp]:
                    tot = jnp.sum(jnp.where(bkt == b, ds, 0.0))
                    out = jnp.where((row == b) & (col == hd), tot, out)
        o_ref[...] = out

    return pl.pallas_call(
        body, out_shape=jax.ShapeDtypeStruct((N_BUCKETS, n_heads), F32),
        compiler_params=pltpu.CompilerParams(vmem_limit_bytes=VMEM_LIMIT_BYTES), name="a_bias_grad")(buckets, dbias)


def _lane_is_second_head(shape):
    return lax.broadcasted_iota(jnp.int32, shape, len(shape) - 1) >= HEAD_A


def _group_view(proj, grp, d):
    T = proj.shape[0]
    if d == 1:
        return proj, IN_WIDTH, grp * 3 * WIDTH_A
    part = proj[:, grp * 3 * WIDTH_A:(grp + 1) * 3 * WIDTH_A]
    return part.reshape(T // d, d * 3 * WIDTH_A), 3 * WIDTH_A, 0


def _stack_heads(v2, second):
    zero = jnp.zeros_like(v2)
    return jnp.concatenate([jnp.where(second, zero, v2), jnp.where(second, v2, zero)], axis=0)


def _unstack_heads(v, second):
    return jnp.where(second, v[QB_A:], v[:QB_A])


def _edge_mask(n, nblk):
    neg_prev = jnp.where(n > 0, 0.0, NEG_INF)
    neg_next = jnp.where(n < nblk - 1, 0.0, NEG_INF)
    return jnp.concatenate([jnp.full((1, QB_A), neg_prev, F32), jnp.zeros((1, QB_A), F32),
                            jnp.full((1, QB_A), neg_next, F32)], axis=1)


def _dil_fwd(proj, bias, grp, d):
    T = proj.shape[0]
    L = T // d
    nblk = L // QB_A
    pv, width, base = _group_view(proj, grp, d)
    cb, b0 = width // WIDTH_A, base // WIDTH_A
    W2 = 2 * HEAD_A
    scale = HEAD_A ** -0.5

    def body(q_ref, kp_ref, kc_ref, kn_ref, vp_ref, vc_ref, vn_ref, b_ref, o_ref, l_ref):
        edge = _edge_mask(pl.program_id(1), nblk)
        second = _lane_is_second_head((QB_A, W2))
        for hp in range(HEADS_A // 2):
            cols = slice(hp * W2, (hp + 1) * W2)
            kcat = jnp.concatenate([kp_ref[:, cols], kc_ref[:, cols], kn_ref[:, cols]], axis=0)
            vcat = jnp.concatenate([vp_ref[:, cols], vc_ref[:, cols], vn_ref[:, cols]], axis=0)
            qs = _stack_heads(q_ref[:, cols], second)
            s = lax.dot_general(qs, kcat, (NT, ((), ())), preferred_element_type=F32)
            s = s * scale + b_ref[2 * hp:2 * hp + 2].reshape(2 * QB_A, 3 * QB_A) + edge
            m = jnp.max(s, axis=-1, keepdims=True)
            p = jnp.exp(s - m)
            l = jnp.sum(p, axis=-1, keepdims=True)
            res = jnp.dot(p.astype(BF16), vcat, preferred_element_type=F32) / l
            o_ref[:, cols] = _unstack_heads(res, second).astype(o_ref.dtype)
            l_ref[:, cols] = _unstack_heads(jnp.broadcast_to(m + jnp.log(l), (2 * QB_A, W2)), second)

    def spec(part, dn):
        return _bs((QB_A, WIDTH_A), lambda r, n: (jnp.clip(n + dn, 0, nblk - 1), r * cb + b0 + part))

    in_specs = [spec(0, 0)] + [spec(1, dn) for dn in (-1, 0, 1)] + [spec(2, dn) for dn in (-1, 0, 1)]
    in_specs += [_bs((HEADS_A, QB_A, 3 * QB_A), lambda r, n: (0, 0, 0))]
    o, lse = pl.pallas_call(
        body, out_shape=[jax.ShapeDtypeStruct((L, d * WIDTH_A), BF16), jax.ShapeDtypeStruct((L, d * WIDTH_A), F32)],
        grid=(d, nblk), in_specs=in_specs,
        out_specs=[_bs((QB_A, WIDTH_A), lambda r, n: (n, r)), _bs((QB_A, WIDTH_A), lambda r, n: (n, r))],
        compiler_params=_params("parallel", "parallel"), name=f"a_fwd_d{d}")(pv, pv, pv, pv, pv, pv, pv, bias)
    return o.reshape(T, WIDTH_A), lse.reshape(T, WIDTH_A)


def _dil_bwd(proj, bias, do, lse, cterm, grp, d):
    T = proj.shape[0]
    L = T // d
    nblk = L // QB_A
    W2 = 2 * HEAD_A
    pv, width, base = _group_view(proj, grp, d)
    cb, b0 = width // W2, base // W2
    ob = WIDTH_A // W2
    view = lambda a: a.reshape(L, d * WIDTH_A)
    scale = HEAD_A ** -0.5

    def body(q_ref, kp_ref, kc_ref, kn_ref, vp_ref, vc_ref, vn_ref, do_ref, l_ref, c_ref, b_ref,
             dq_ref, dk_ref, dv_ref, db_ref):
        r, n = pl.program_id(1), pl.program_id(2)

        @pl.when(n == 0)
        def _():
            dk_ref[...] = jnp.zeros_like(dk_ref)
            dv_ref[...] = jnp.zeros_like(dv_ref)

        @pl.when((n == 0) & (r == 0))
        def _():
            db_ref[...] = jnp.zeros_like(db_ref)

        second = _lane_is_second_head((QB_A, W2))
        kcat = jnp.concatenate([kp_ref[...], kc_ref[...], kn_ref[...]], axis=0)
        vcat = jnp.concatenate([vp_ref[...], vc_ref[...], vn_ref[...]], axis=0)
        qs, dos = _stack_heads(q_ref[...], second), _stack_heads(do_ref[...], second)
        lse2, c2 = l_ref[...], c_ref[...]
        lse_rows = jnp.concatenate([lse2[:, 0:1], lse2[:, HEAD_A:HEAD_A + 1]], axis=0)
        c_rows = jnp.concatenate([c2[:, 0:1], c2[:, HEAD_A:HEAD_A + 1]], axis=0)
        s = lax.dot_general(qs, kcat, (NT, ((), ())), preferred_element_type=F32)
        p = jnp.exp(s * scale + b_ref[...].reshape(2 * QB_A, 3 * QB_A) + _edge_mask(n, nblk) - lse_rows)
        dp = lax.dot_general(dos, vcat, (NT, ((), ())), preferred_element_type=F32)
        ds = p * (dp + c_rows)
        db_ref[...] += ds.reshape(2, QB_A, 3 * QB_A)
        pb, dsb = p.astype(BF16), (ds * scale).astype(BF16)
        dq_ref[...] = _unstack_heads(jnp.dot(dsb, kcat, preferred_element_type=F32), second).astype(dq_ref.dtype)
        dkc = lax.dot_general(dsb, qs, (TN, ((), ())), preferred_element_type=F32)
        dvc = lax.dot_general(pb, dos, (TN, ((), ())), preferred_element_type=F32)
        for b, dn in enumerate((-1, 0, 1)):
            start = pl.multiple_of(jnp.clip(n + dn, 0, nblk - 1) * QB_A, QB_A)
            dk_ref[pl.ds(start, QB_A), :] += dkc[b * QB_A:(b + 1) * QB_A]
            dv_ref[pl.ds(start, QB_A), :] += dvc[b * QB_A:(b + 1) * QB_A]

    def spec(part, dn):
        return _bs((QB_A, W2), lambda hp, r, n: (jnp.clip(n + dn, 0, nblk - 1), r * cb + b0 + part * ob + hp))

    in_specs = [spec(0, 0)] + [spec(1, dn) for dn in (-1, 0, 1)] + [spec(2, dn) for dn in (-1, 0, 1)]
    in_specs += [_bs((QB_A, W2), lambda hp, r, n: (n, r * ob + hp))] * 3
    in_specs += [_bs((2, QB_A, 3 * QB_A), lambda hp, r, n: (hp, 0, 0))]
    out_shape = [jax.ShapeDtypeStruct((L, d * WIDTH_A), BF16), jax.ShapeDtypeStruct((L, d * WIDTH_A), F32),
                 jax.ShapeDtypeStruct((L, d * WIDTH_A), F32), jax.ShapeDtypeStruct((HEADS_A, QB_A, 3 * QB_A), F32)]
    out_specs = [_bs((QB_A, W2), lambda hp, r, n: (n, r * ob + hp)),
                 _bs((L, W2), lambda hp, r, n: (0, r * ob + hp)), _bs((L, W2), lambda hp, r, n: (0, r * ob + hp)),
                 _bs((2, QB_A, 3 * QB_A), lambda hp, r, n: (hp, 0, 0))]
    dq, dk, dv, db = pl.pallas_call(
        body, out_shape=out_shape, grid=(ob, d, nblk), in_specs=in_specs, out_specs=out_specs,
        compiler_params=_params("arbitrary", "arbitrary", "arbitrary"), name=f"a_bwd_d{d}")(
            pv, pv, pv, pv, pv, pv, pv, view(do), view(lse), view(cterm), bias)
    return dq.reshape(T, WIDTH_A), dk.reshape(T, WIDTH_A), dv.reshape(T, WIDTH_A), db


def _segment_ones():
    i = np.arange(WIDTH_A)
    return jnp.asarray((i[:, None] // HEAD_A == i[None, :] // HEAD_A).astype(np.float32), dtype=BF16)


def _group_weights(l0, l1, l2):
    m = jnp.maximum(jnp.maximum(l0, l1), l2)
    e = [jnp.exp(l - m) for l in (l0, l1, l2)]
    z = e[0] + e[1] + e[2]
    return [ei / z for ei in e]


def _combine_fwd(outs, lses):
    T = outs[0].shape[0]

    def fn(o0, o1, o2, l0, l1, l2):
        w = _group_weights(l0, l1, l2)
        return w[0] * o0.astype(F32) + w[1] * o1.astype(F32) + w[2] * o2.astype(F32)

    (oa,) = _ew("a_combine", fn, [_tiled(o) for o in outs] + [_tiled(l) for l in lses], [(BF16, WIDTH_A)], n_rows=T, rows=512)
    return oa


def _combine_bwd(doa, outs, lses):
    T = doa.shape[0]

    def fn(d, o0, o1, o2, l0, l1, l2, seg):
        d = d.astype(F32)
        w = _group_weights(l0, l1, l2)
        tot = jnp.zeros(d.shape, F32)
        for wg, og in zip(w, (o0, o1, o2)):
            prod = wg * d * og.astype(F32)
            hi = prod.astype(BF16)
            lo = (prod - hi.astype(F32)).astype(BF16)
            tot = tot + jnp.dot(hi, seg, preferred_element_type=F32) + jnp.dot(lo, seg, preferred_element_type=F32)
        return tuple(wg * d for wg in w) + tuple(-wg * tot for wg in w)

    res = _ew("a_combine_bwd", fn, [_tiled(doa)] + [_tiled(o) for o in outs] + [_tiled(l) for l in lses] + [_whole(_segment_ones())],
              [(BF16, WIDTH_A)] * 3 + [(F32, WIDTH_A)] * 3, n_rows=T, rows=256)
    return res[:3], res[3:]


def _rope_tables(T):
    rows = T // GRID_W
    row = jnp.repeat(jnp.arange(rows, dtype=F32), GRID_W)
    col = jnp.tile(jnp.arange(GRID_W, dtype=F32), rows)
    n_freq = HEAD_B // 4
    freq = ROPE_THETA ** (-jnp.arange(n_freq, dtype=F32) / n_freq)
    ang = jnp.concatenate([row[:, None] * freq, col[:, None] * freq], axis=-1)
    cos, sin = jnp.repeat(jnp.cos(ang), 2, axis=1), jnp.repeat(jnp.sin(ang), 2, axis=1)
    sign = jnp.where(jnp.arange(HEAD_B) % 2 == 0, -1.0, 1.0).astype(F32)
    return cos, sin * sign


def _swap_pairs(v):
    even = lax.broadcasted_iota(jnp.int32, v.shape, v.ndim - 1) % 2 == 0
    n = v.shape[-1]
    return jnp.where(even, pltpu.roll(v, n - 1, v.ndim - 1), pltpu.roll(v, 1, v.ndim - 1))


def _qk_fwd(name, proj, col0, n_heads, gain, cos, sin):
    T = proj.shape[0]

    def fn(xr, g, c, s):
        xn = _norm_fwd(xr.astype(F32), g)
        return xn * c + _swap_pairs(xn) * s

    (out,) = _ew(name, fn, [_tiled(proj, HEAD_B, col0 // HEAD_B), _whole(gain), _table(cos), _table(sin)],
                 [(BF16, HEAD_B)], n_rows=T, rows=2048, ncols=n_heads)
    return out


def _qk_bwd(name, dout, proj, col0, n_heads, gain, cos, sin):
    T = proj.shape[0]

    def fn(dv, xr, g, c, s):
        dv = dv.astype(F32)
        dxn = c * dv + _swap_pairs(s * dv)
        dx, dgr = _norm_bwd(xr.astype(F32), g, dxn)
        return dx, _colsum(dgr)

    dx, dg = _ew(name, fn, [_tiled(dout, HEAD_B, 0), _tiled(proj, HEAD_B, col0 // HEAD_B), _whole(gain),
                            _table(cos), _table(sin)],
                 [(BF16, HEAD_B)], n_rows=T, rows=2048, reds=(HEAD_B,), ncols=n_heads)
    return dx, jnp.sum(dg, axis=0)


def _gqa_fwd(qn, kn, proj):
    T = qn.shape[0]
    GW = 4 * HEAD_B
    scale = HEAD_B ** -0.5

    def body(q_ref, k_ref, v_ref, o_ref, l_ref):
        k, v = k_ref[...], v_ref[...]
        lane = lax.broadcasted_iota(jnp.int32, (QB_B, HEAD_B), 1)
        lse_all = jnp.zeros((QB_B, HEAD_B), F32)
        for g in range(4):
            cols = slice(g * HEAD_B, (g + 1) * HEAD_B)
            s = lax.dot_general(q_ref[:, cols], k, (NT, ((), ())), preferred_element_type=F32) * scale
            m = jnp.max(s, axis=-1, keepdims=True)
            p = jnp.exp(s - m)
            l = jnp.sum(p, axis=-1, keepdims=True)
            o = jnp.dot(p.astype(BF16), v, preferred_element_type=F32) / l
            o_ref[:, cols] = o.astype(o_ref.dtype)
            lse_all = jnp.where(lane == g, m + jnp.log(l), lse_all)
        l_ref[...] = lse_all

    return pl.pallas_call(
        body, out_shape=[jax.ShapeDtypeStruct((T, 2 * GW), BF16), jax.ShapeDtypeStruct((2, T, HEAD_B), F32)],
        grid=(2, T // QB_B),
        in_specs=[_bs((QB_B, GW), lambda kv, i: (i, kv)), _bs((T, HEAD_B), lambda kv, i: (0, kv)),
                  _bs((T, HEAD_B), lambda kv, i: (0, B_V // HEAD_B + kv))],
        out_specs=[_bs((QB_B, GW), lambda kv, i: (i, kv)), _bs((None, QB_B, HEAD_B), lambda kv, i: (kv, i, 0))],
        compiler_params=_params("parallel", "parallel"), name="b_fwd")(qn, kn, proj)


def _gqa_bwd(qn, kn, proj, o, lse, do):
    T = qn.shape[0]
    GW = 4 * HEAD_B
    scale = HEAD_B ** -0.5

    def body(q_ref, k_ref, v_ref, o_ref, l_ref, do_ref, dq_ref, dk_ref, dv_ref):
        i = pl.program_id(1)

        @pl.when(i == 0)
        def _():
            dk_ref[...] = jnp.zeros_like(dk_ref)
            dv_ref[...] = jnp.zeros_like(dv_ref)

        k, v = k_ref[...], v_ref[...]
        lse_all = l_ref[...]
        for g in range(4):
            cols = slice(g * HEAD_B, (g + 1) * HEAD_B)
            q, dob = q_ref[:, cols], do_ref[:, cols]
            delta = jnp.sum(dob.astype(F32) * o_ref[:, cols].astype(F32), axis=-1, keepdims=True)
            s = lax.dot_general(q, k, (NT, ((), ())), preferred_element_type=F32) * scale
            p = jnp.exp(s - lse_all[:, g:g + 1])
            dp = lax.dot_general(dob, v, (NT, ((), ())), preferred_element_type=F32)
            ds = (p * (dp - delta) * scale).astype(BF16)
            dq_ref[:, cols] = jnp.dot(ds, k, preferred_element_type=F32).astype(dq_ref.dtype)
            dk_ref[...] += lax.dot_general(ds, q, (TN, ((), ())), preferred_element_type=F32)
            dv_ref[...] += lax.dot_general(p.astype(BF16), dob, (TN, ((), ())), preferred_element_type=F32)

    return pl.pallas_call(
        body, out_shape=[jax.ShapeDtypeStruct((T, 2 * GW), BF16), jax.ShapeDtypeStruct((T, 2 * HEAD_B), F32),
                         jax.ShapeDtypeStruct((T, 2 * HEAD_B), F32)],
        grid=(2, T // QB_B),
        in_specs=[_bs((QB_B, GW), lambda kv, i: (i, kv)), _bs((T, HEAD_B), lambda kv, i: (0, kv)),
                  _bs((T, HEAD_B), lambda kv, i: (0, B_V // HEAD_B + kv)), _bs((QB_B, GW), lambda kv, i: (i, kv)),
                  _bs((None, QB_B, HEAD_B), lambda kv, i: (kv, i, 0)), _bs((QB_B, GW), lambda kv, i: (i, kv))],
        out_specs=[_bs((QB_B, GW), lambda kv, i: (i, kv)), _bs((T, HEAD_B), lambda kv, i: (0, kv)),
                   _bs((T, HEAD_B), lambda kv, i: (0, kv))],
        compiler_params=_params("parallel", "arbitrary"), name="b_bwd")(qn, kn, proj, o, lse, do)


def _local_step(x, target, small, get_w, put_g, deps=()):
    T, D = x.shape
    gs = {}

    x1, ffn1_saved = _ffn_fwd("ffn1", x, small["ffn1_norm"], get_w, deps)
    (h2,) = _ew("mix_norm", lambda xv, g: _norm_fwd(xv, g), [_tiled(x1), _whole(small["mix_norm"])], [(BF16, D)], n_rows=T, rows=512)
    w_in = get_w("w_in", h2)
    nq = w_in.shape[2]
    tpq = nq // WIDTH_A

    def proj_tile(j, k):
        c = j * tpq + k
        return jnp.where(c < 3 * len(DILATIONS), (c % 3) * 3 + c // 3, c)

    proj = _mm("mix_in", (4, T // 1024, tpq),
               [(h2, _bs((1024, D), lambda j, i, k: (i, 0)), w_in, _bs((None, D, WIDTH_A), lambda j, i, k: (j, 0, k)))],
               jax.ShapeDtypeStruct((T, IN_WIDTH), BF16), _bs((1024, WIDTH_A), lambda j, i, k: (i, proj_tile(j, k))), NN)

    bias = _bias_tiles(small["rel_bias"])
    a_outs, a_lses = [], []
    for grp, d in enumerate(DILATIONS):
        o, l = _dil_fwd(proj, bias[grp * HEADS_A:(grp + 1) * HEADS_A], grp, d)
        a_outs.append(o)
        a_lses.append(l)
    o_a = _combine_fwd(a_outs, a_lses)

    cos, sin = _rope_tables(T)
    qn = _qk_fwd("b_qnorm", proj, B_Q, 8, small["q_norm"], cos, sin)
    kn = _qk_fwd("b_knorm", proj, B_K, 2, small["k_norm"], cos, sin)
    o_b, lse_b = _gqa_fwd(qn, kn, proj)

    wa, wb3, w_out3 = get_w("w_branch_a", o_b), get_w("w_branch_b", o_b).reshape(1, D, D), get_w("w_out", o_b).reshape(1, D, D)
    t_a = _mm_cols("mix_branch_a", o_a, wa, tm=512, tn=256, out_dtype=BF16, cat=True)
    t_b = _mm_cols("mix_branch_b", o_b, wb3, tm=512, tn=512, out_dtype=BF16, cat=True)
    bg_a, bg_b = small["b_gate"][:, :D], small["b_gate"][:, D:]

    def merge(ta, tb, ga, gb_, ba, bb):
        sa, sb = _sigmoid(ga.astype(F32) + ba), _sigmoid(gb_.astype(F32) + bb)
        return sa * ta.astype(F32) + sb * tb.astype(F32)

    gate_ins = [_tiled(proj, D, G_A // D), _tiled(proj, D, G_B // D), _whole(bg_a), _whole(bg_b)]
    (merged,) = _ew("mix_merge", merge, [_tiled(t_a), _tiled(t_b)] + gate_ins, [(BF16, D)], n_rows=T, rows=512)
    x2 = _mm_cols("mix_out", merged, w_out3, tm=512, tn=512, out_dtype=F32, cat=True,
                  extras=[x1], epilogue=lambda acc, xv: xv + acc)
    x3, ffn2_saved = _ffn_fwd("ffn2", x2, small["ffn2_norm"], get_w)

    def head(xv, g, tv):
        r = _rstd(xv)
        xh = xv * r
        e = xh * g - tv
        dy = e * (1.0 / D)
        dxh = dy * g
        dx = r * (dxh - xh * jnp.mean(dxh * xh, axis=-1, keepdims=True))
        return dx, 0.5 * dx, _colsum(e * e) * (0.5 / D), _colsum(dy * xh)

    dx3, dx3_half, loss_cols, g_final = _ew("loss_head", head, [_tiled(x3), _whole(small["final_norm"].reshape(1, D)), _tiled(target)],
                                            [(F32, D), (BF16, D)], n_rows=T, rows=256, reds=(D, D))
    gs["final_norm"] = g_final.reshape(D)

    dx2, _, gs["ffn2_norm"] = _ffn_bwd("ffn2", x2, small["ffn2_norm"], get_w, put_g, ffn2_saved, dx3, dx3_half)

    (dmix,) = _ew("mix_bwd_cast", lambda v: v, [_tiled(dx2)], [(BF16, D)], n_rows=T, rows=512)
    g_out = _mm_wgrad("mix_bwd_dwout", merged, dmix, a_cols=D // 4, b_cols=None, tm=256, tn=512, J=4).reshape(D, D)
    dmerged = _mm_rows_t("mix_bwd_dmerged", dmix, w_out3, tm=512, out_dtype=BF16).reshape(T, D)

    def merge_bwd(dm, ta, tb, ga, gb_, ba, bb):
        dm, ta, tb = dm.astype(F32), ta.astype(F32), tb.astype(F32)
        sa, sb = _sigmoid(ga.astype(F32) + ba), _sigmoid(gb_.astype(F32) + bb)
        dga, dgb = dm * ta * sa * (1.0 - sa), dm * tb * sb * (1.0 - sb)
        return dm * sa, dm * sb, dga, dgb, _colsum(dga), _colsum(dgb)

    dta, dtb, dga, dgb, dba, dbb = _ew("mix_bwd_merge", merge_bwd, [_tiled(dmerged), _tiled(t_a), _tiled(t_b)] + gate_ins,
                                       [(BF16, D)] * 4, n_rows=T, rows=256, reds=(D, D))
    gs["b_gate"] = jnp.concatenate([dba.reshape(1, D), dbb.reshape(1, D)], axis=1)

    g_a = _mm_wgrad("mix_bwd_dwa", o_a, dta, a_cols=None, b_cols=D // 4, tm=WIDTH_A, tn=256, J=4)
    g_b = _mm_wgrad("mix_bwd_dwb", o_b, dtb, a_cols=D // 4, b_cols=None, tm=256, tn=512, J=4).reshape(D, D)
    deps = put_g({"w_out": g_out, "w_branch_a": g_a, "w_branch_b": g_b})
    do_a = _mm("mix_bwd_doa", (T // 1024,),
               [(dta, _bs((1024, D // 4), lambda i, j=j: (i, j)), wa, _bs((None, WIDTH_A, D // 4), lambda i, j=j: (j, 0, 0)))
                for j in range(4)],
               jax.ShapeDtypeStruct((T, WIDTH_A), BF16), _bs((1024, WIDTH_A), lambda i: (i, 0)), NT, deps=deps)
    do_b = _mm_rows_t("mix_bwd_dob", dtb, wb3, tm=512, out_dtype=BF16).reshape(T, D)

    dqn, dkn, dv_b = _gqa_bwd(qn, kn, proj, o_b, lse_b, do_b)
    dq_b, gs["q_norm"] = _qk_bwd("b_bwd_qnorm", dqn, proj, B_Q, 8, small["q_norm"], cos, sin)
    dk_b, gs["k_norm"] = _qk_bwd("b_bwd_knorm", dkn, proj, B_K, 2, small["k_norm"], cos, sin)

    do_groups, c_groups = _combine_bwd(do_a, a_outs, a_lses)
    dqs, dks, dvs, dbs = [], [], [], []
    for grp, d in enumerate(DILATIONS):
        dq, dk, dv, db = _dil_bwd(proj, bias[grp * HEADS_A:(grp + 1) * HEADS_A], do_groups[grp], a_lses[grp], c_groups[grp], grp, d)
        dqs.append(dq), dks.append(dk), dvs.append(dv), dbs.append(db)
    gs["rel_bias"] = _bias_grad(jnp.concatenate(dbs, axis=0))

    dproj = jnp.concatenate([p.astype(BF16) for p in dqs + dks + dvs + [dq_b, dk_b, dv_b, dga, dgb]], axis=1)
    nq = w_in.shape[2]
    deps = put_g({"w_in": _mm_wgrad("mix_bwd_dwin", h2, dproj, a_cols=None, b_cols=nq, tm=512, tn=512, J=4)})
    dh2 = _mm("mix_bwd_dh", (T // 512, D // 512),
              [(dproj, _bs((512, nq), lambda i, k, j=j: (i, j)), w_in, _bs((None, 512, nq), lambda i, k, j=j: (j, k, 0)))
               for j in range(4)],
              jax.ShapeDtypeStruct((T, D), F32), _bs((512, 512), lambda i, k: (i, k)), NT, deps=deps)

    def nb(xv, gv, dhv, dres):
        dx, dgr = _norm_bwd(xv, gv, dhv)
        dx = dx + dres
        return dx, 0.5 * dx, _colsum(dgr)

    dx1, dx1_half, g_mix = _ew("mix_bwd_norm", nb, [_tiled(x1), _whole(small["mix_norm"]), _tiled(dh2), _tiled(dx2)],
                               [(F32, D), (BF16, D)], n_rows=T, rows=256, reds=(D,))
    gs["mix_norm"] = g_mix.reshape(1, D)

    dx0, _, gs["ffn1_norm"] = _ffn_bwd("ffn1", x, small["ffn1_norm"], get_w, put_g, ffn1_saved, dx1, dx1_half)
    return loss_cols.reshape(1, D), dx0, gs


def _position():
    return lax.axis_index("x"), lax.axis_index("y"), lax.axis_index("c")


def _any_specs(n):
    return [pl.BlockSpec(memory_space=pl.ANY)] * n


HBM_SPEC = pl.BlockSpec(memory_space=pltpu.HBM)
SEM_SPEC = pl.BlockSpec(memory_space=pltpu.SEMAPHORE)
DATAFLOW_EFFECT = pltpu.SideEffectType.DATAFLOW_SIDE_EFFECTING
N_PEER_CHIPS = 3
LANES = 128


def _quarter_copies(srcs, lands, send_sems, recv_sems, scatter):
    x, y, c = _position()
    me = 2 * x + y
    peers = [(1 - x, y, c), (x, 1 - y, c), (1 - x, 1 - y, c)]
    copies = []
    for src, land, send, recv in zip(srcs, lands, send_sems, recv_sems):
        for p, (px, py, pc) in enumerate(peers):
            copies.append(pltpu.make_async_remote_copy(
                src_ref=src.at[2 * px + py] if scatter else src, dst_ref=land.at[me], send_sem=send.at[p], recv_sem=recv.at[p],
                device_id=(px, py, pc), device_id_type=MESH))
    return copies


def _exchange_start(name, srcs, lands, scatter):
    n = len(srcs)

    def body(*refs):
        src_refs, land_refs = refs[:n], refs[n:2 * n]
        send_sems, recv_sems = refs[2 * n:3 * n], refs[3 * n:4 * n]
        token, local_sems = refs[6 * n], refs[6 * n + 1]
        me = 2 * lax.axis_index("x") + lax.axis_index("y")
        own = [pltpu.make_async_copy(src.at[me] if scatter else src, land.at[me], local_sems.at[i])
               for i, (src, land) in enumerate(zip(src_refs, land_refs))]
        for cp in own:
            cp.start()
        for cp in own:
            cp.wait()
        for cp in _quarter_copies(src_refs, land_refs, send_sems, recv_sems, scatter):
            cp.start()
        token[...] = jnp.zeros_like(token)

    sem = pltpu.SemaphoreType.DMA((N_PEER_CHIPS,))
    out_shape = [sem] * (2 * n) + [pltpu.HBM(a.shape, a.dtype) for a in list(srcs) + list(lands)]
    out_shape += [jax.ShapeDtypeStruct((8, LANES), F32)]
    res = pl.pallas_call(
        body, name=name, out_shape=out_shape, in_specs=[HBM_SPEC] * (2 * n),
        out_specs=[SEM_SPEC] * (2 * n) + [HBM_SPEC] * (2 * n) + [pl.BlockSpec(memory_space=pltpu.VMEM)],
        input_output_aliases={i: 2 * n + i for i in range(2 * n)},
        scratch_shapes=[pltpu.SemaphoreType.DMA((n,))],
        compiler_params=pltpu.CompilerParams(has_side_effects=DATAFLOW_EFFECT),
    )(*[pltpu.with_memory_space_constraint(a, pltpu.HBM) for a in list(srcs) + list(lands)])
    return res[:n], res[n:2 * n], res[2 * n:3 * n], res[3 * n:4 * n], res[4 * n]


def _exchange_wait(name, srcs, lands, send_sems, recv_sems, after, scatter):
    n = len(srcs)

    def body(*refs):
        src_refs, land_refs = refs[:n], refs[n:2 * n]
        sends, recvs = refs[2 * n:3 * n], refs[3 * n:4 * n]
        for cp in _quarter_copies(src_refs, land_refs, sends, recvs, scatter):
            cp.wait_send()
            cp.wait_recv()

    res = pl.pallas_call(
        body, name=name, out_shape=[pltpu.HBM(a.shape, a.dtype) for a in list(srcs) + list(lands)],
        in_specs=[HBM_SPEC] * (2 * n) + [SEM_SPEC] * (2 * n) + [pl.BlockSpec(memory_space=pl.ANY)],
        out_specs=[HBM_SPEC] * (2 * n), input_output_aliases={i: i for i in range(2 * n)},
        compiler_params=pltpu.CompilerParams(has_side_effects=DATAFLOW_EFFECT),
    )(*srcs, *lands, *send_sems, *recv_sems, after)
    return res[n:]


def _own_slot(stack_shape, own, dtype):
    me = 2 * lax.axis_index("x") + lax.axis_index("y")
    return lax.dynamic_update_slice(jnp.zeros(stack_shape, dtype), own[None], (me,) + (0,) * own.ndim)


def _swap_with_sibling(parts):
    n = len(parts)

    def body(*refs):
        ins, outs = refs[:n], refs[n:2 * n]
        send_sems, recv_sems = refs[2 * n:]
        x, y, c = _position()
        copies = []
        for i in range(n):
            cp = pltpu.make_async_remote_copy(ins[i], outs[i], send_sems.at[i], recv_sems.at[i],
                                              device_id=(x, y, 1 - c), device_id_type=MESH)
            cp.start()
            copies.append(cp)
        for cp in copies:
            cp.wait()

    return pl.pallas_call(
        body, out_shape=[jax.ShapeDtypeStruct(s.shape, s.dtype) for s in parts],
        in_specs=_any_specs(n), out_specs=_any_specs(n),
        scratch_shapes=[pltpu.SemaphoreType.DMA((n,)), pltpu.SemaphoreType.DMA((n,))],
        compiler_params=pltpu.CompilerParams(has_side_effects=True), name="swap_with_sibling")(*parts)


def _allreduce_small(buf):
    R, C = buf.shape
    flips = [(fx, fy, fc) for fx in (0, 1) for fy in (0, 1) for fc in (0, 1)][1:]

    def body(in_ref, out_ref, land_ref, send_sems, recv_sems):
        x, y, c = _position()
        me = 4 * x + 2 * y + c
        copies = []
        for k, (fx, fy, fc) in enumerate(flips):
            px, py, pc = (1 - x if fx else x), (1 - y if fy else y), (1 - c if fc else c)
            cp = pltpu.make_async_remote_copy(in_ref, land_ref.at[me], send_sems.at[k], recv_sems.at[k],
                                              device_id=(px, py, pc), device_id_type=MESH)
            cp.start()
            copies.append(cp)
        land_ref[me] = in_ref[...]
        for cp in copies:
            cp.wait()
        acc = land_ref[0]
        for k in range(1, 8):
            acc = acc + land_ref[k]
        out_ref[...] = acc

    return pl.pallas_call(
        body, out_shape=jax.ShapeDtypeStruct((R, C), F32),
        in_specs=[pl.BlockSpec(memory_space=pltpu.VMEM)], out_specs=pl.BlockSpec(memory_space=pltpu.VMEM),
        scratch_shapes=[pltpu.VMEM((8, R, C), F32), pltpu.SemaphoreType.DMA((7,)), pltpu.SemaphoreType.DMA((7,))],
        compiler_params=pltpu.CompilerParams(has_side_effects=True), name="allreduce_small")(buf)


def _adamw_math(w, g, m, v):
    m2 = ADAM_B1 * m + (1.0 - ADAM_B1) * g
    v2 = ADAM_B2 * v + (1.0 - ADAM_B2) * (g * g)
    m_hat = m2 / (1.0 - ADAM_B1 ** ADAM_STEP)
    v_hat = v2 / (1.0 - ADAM_B2 ** ADAM_STEP)
    delta = -ADAM_LR * (m_hat / (jnp.sqrt(v_hat) + ADAM_EPS) + ADAM_WD * w)
    return delta, m2, v2


def _adamw_big(name, w, m, v, part_mine, part_sibling):
    R, C = w.shape
    rows = 256 if R % 256 == 0 else R // 2 if (R // 2) % 8 == 0 else R

    def fn(wv, mv, vv, a, b):
        g = a + b
        return (g,) + _adamw_math(wv, g, mv, vv)

    return _ew(name, fn, [_tiled(w), _tiled(m), _tiled(v), _tiled(part_mine), _tiled(part_sibling)], [(F32, C)] * 4, n_rows=R, rows=rows)


def _sum_four(name, stack):
    _, R, C = stack.shape
    rows = 256 if R % 256 == 0 else R // 2 if (R // 2) % 8 == 0 else R
    flat = stack.reshape(4 * R, C)
    nrb = R // rows

    def fn(a, b, c, d):
        return ((a.astype(F32) + b.astype(F32)) + c.astype(F32)) + d.astype(F32)

    (out,) = _ew(name, fn, [_tiled(flat, None, 0, k * nrb) for k in range(4)], [(F32, C)], n_rows=R, rows=rows)
    return out


BIG = ("ffn1_w1", "ffn1_w3", "ffn1_w2", "w_in", "w_branch_a", "w_branch_b", "w_out", "ffn2_w1", "ffn2_w3", "ffn2_w2")
SMALL = ("ffn1_norm", "mix_norm", "b_gate", "q_norm", "k_norm", "rel_bias", "ffn2_norm", "final_norm")
ORDER = ("ffn1_norm", "ffn1_w1", "ffn1_w3", "ffn1_w2", "mix_norm", "w_in", "b_gate", "q_norm", "k_norm", "rel_bias",
         "w_branch_a", "w_branch_b", "w_out", "ffn2_norm", "ffn2_w1", "ffn2_w3", "ffn2_w2", "final_norm")
TRANSPOSED = ("ffn1_w1", "ffn1_w3", "ffn2_w1", "ffn2_w3")
GATHER_GROUPS = (("ffn1_w1", "ffn1_w3"), ("ffn1_w2",), ("w_in",), ("w_branch_a", "w_branch_b", "w_out"),
                 ("ffn2_w1", "ffn2_w3", "ffn2_w2"))


def _pack_small(d):
    rows = []
    for n in SMALL:
        flat = d[n].reshape(-1)
        pad = (-flat.shape[0]) % LANES
        rows.append(jnp.pad(flat, (0, pad)).reshape(-1, LANES))
    buf = jnp.concatenate(rows, axis=0)
    return jnp.pad(buf, ((0, (-buf.shape[0]) % 8), (0, 0)))


def _unpack_small(buf, like):
    out, r = {}, 0
    for n in SMALL:
        size = like[n].size
        nr = -(-size // LANES)
        out[n] = buf[r:r + nr].reshape(-1)[:size].reshape(like[n].shape)
        r += nr
    return out


def kernel(x, ffn1_norm, ffn1_w1, ffn1_w3, ffn1_w2, mix_norm, w_in, b_gate, q_norm, k_norm, rel_bias, w_branch_a, w_branch_b, w_out, ffn2_norm, ffn2_w1, ffn2_w3, ffn2_w2, final_norm, loss_target, m_ffn1_norm, m_ffn1_w1, m_ffn1_w3, m_ffn1_w2, m_mix_norm, m_w_in, m_b_gate, m_q_norm, m_k_norm, m_rel_bias, m_w_branch_a, m_w_branch_b, m_w_out, m_ffn2_norm, m_ffn2_w1, m_ffn2_w3, m_ffn2_w2, m_final_norm, v_ffn1_norm, v_ffn1_w1, v_ffn1_w3, v_ffn1_w2, v_mix_norm, v_w_in, v_b_gate, v_q_norm, v_k_norm, v_rel_bias, v_w_branch_a, v_w_branch_b, v_w_out, v_ffn2_norm, v_ffn2_w1, v_ffn2_w3, v_ffn2_w2, v_final_norm):
    given = dict(locals())
    w = {n: given[n] for n in ORDER}
    m = {n: given["m_" + n] for n in ORDER}
    v = {n: given["v_" + n] for n in ORDER}
    T, D = x.shape[1], x.shape[2]

    def stored(a, n):
        a = a.reshape(a.shape[1:])
        return a.T if n in TRANSPOSED else a

    def returned(a, n):
        return (a.T if n in TRANSPOSED else a).reshape(w[n].shape)

    quarter = {n: stored(w[n], n) for n in BIG}
    q16 = [quarter[n].astype(BF16) for n in BIG]
    send, recv, src_thru, land_thru, token = _exchange_start(
        "gather_start", q16, [lax.empty((4,) + q.shape, BF16) for q in q16], scatter=False)
    index = {n: i for i, n in enumerate(BIG)}
    ready = {}

    def get_w(name, after):
        if name not in ready:
            group = next(g for g in GATHER_GROUPS if name in g)
            ids = [index[n] for n in group]
            stacks = _exchange_wait("gather_wait_" + group[0], [src_thru[i] for i in ids], [land_thru[i] for i in ids],
                                    [send[i] for i in ids], [recv[i] for i in ids], after, scatter=False)
            for n, st in zip(group, stacks):
                ready[n] = st.reshape(D, D) if n in ("w_branch_b", "w_out") else st
        return ready[name]

    in_flight = []

    def put_g(grads):
        names = list(grads)
        stacks = [grads[n].reshape((4,) + quarter[n].shape) for n in names]
        lands = [lax.empty(s.shape, BF16) for s in stacks]
        started = _exchange_start("scatter_start_" + names[0], stacks, lands, scatter=True)
        in_flight.append((names,) + tuple(started[:4]))
        return [started[4]]

    small = {n: w[n] for n in SMALL}
    loss_cols, grad_x, gs = _local_step(x.reshape(T, D), loss_target.reshape(T, D), small, get_w, put_g, deps=[token])
    loss = lax.psum(jnp.sum(loss_cols), ("x", "y", "c"))

    landed = {}
    for names, s_sem, r_sem, srcs, lands in in_flight:
        got = _exchange_wait("scatter_wait_" + names[0], srcs, lands, s_sem, r_sem, grad_x, scatter=True)
        landed.update(zip(names, got))
    partial = [_sum_four(f"sum4_{n}", landed[n]) for n in BIG]
    other = _swap_with_sibling(partial)
    grads, deltas, new_m, new_v = {}, {}, {}, {}
    for n, mine, theirs in zip(BIG, partial, other):
        res = _adamw_big(f"adamw_{n}", quarter[n], stored(m[n], n), stored(v[n], n), mine, theirs)
        grads[n], deltas[n], new_m[n], new_v[n] = [returned(r, n) for r in res]

    gs = {n: gs[n].reshape(w[n].shape) for n in SMALL}
    g_small = _allreduce_small(_pack_small(gs))
    packed = [_pack_small({n: d[n] for n in SMALL}) for d in (w, m, v)]
    R = g_small.shape[0]
    res = _ew("adamw_small", lambda wv, mv, vv, g: (g,) + _adamw_math(wv, g, mv, vv),
              [_tiled(packed[0]), _tiled(packed[1]), _tiled(packed[2]), _tiled(g_small)], [(F32, LANES)] * 4, n_rows=R, rows=R)
    for d, buf in zip((grads, deltas, new_m, new_v), res):
        d.update(_unpack_small(buf, w))

    return (loss, grad_x.reshape(x.shape), *[grads[n] for n in ORDER], *[deltas[n] for n in ORDER],
            *[new_m[n] for n in ORDER], *[new_v[n] for n in ORDER])
```

```python
import functools
import math

import numpy as np
import jax
import jax.numpy as jnp
from jax import lax
from jax.experimental import pallas as pl
from jax.experimental.pallas import tpu as pltpu

F32 = jnp.float32
BF16 = jnp.bfloat16
MESH = pl.DeviceIdType.MESH

NEG_INF = -1e30
EPS = 1e-6
GRID_W = 64
ROPE_THETA = 10000.0
DILATIONS = (1, 4, 16)
BAND_HALF = 64
HEAD_A = 64
HEADS_A = 8
WIDTH_A = HEADS_A * HEAD_A
HEAD_B = 128
N_BUCKETS = 32
MAX_DISTANCE = 1024
ADAM_LR, ADAM_B1, ADAM_B2, ADAM_EPS, ADAM_WD, ADAM_STEP = 0.001, 0.9, 0.999, 1e-08, 0.01, 10

A_Q, A_K, A_V = 0, 1536, 3072
B_Q, B_K, B_V = 4608, 5632, 5888
G_A, G_B = 6144, 7168
IN_WIDTH = 8192

VMEM_LIMIT_BYTES = 56 * 1024 * 1024
QB_A = 128
QB_B = 256


def _params(*sem):
    return pltpu.CompilerParams(dimension_semantics=sem, vmem_limit_bytes=VMEM_LIMIT_BYTES)


def _bs(shape, fn):
    return pl.BlockSpec(shape, fn)


def _mm(name, grid, pairs, out_shape, out_spec, dims, *, reduce_axis=None, extras=(), epilogue=None, deps=()):
    n_pairs, n_extra, n_deps = len(pairs), len(extras), len(deps)
    operands = [p[0] for p in pairs] + [p[2] for p in pairs] + [e[0] for e in extras] + list(deps)
    in_specs = [p[1] for p in pairs] + [p[3] for p in pairs] + [e[1] for e in extras] + _any_specs(n_deps)
    tile = tuple(s for s in out_spec.block_shape if s is not None)
    n_steps = grid[reduce_axis] if reduce_axis is not None else 1

    def body(*refs):
        a_refs, b_refs = refs[:n_pairs], refs[n_pairs:2 * n_pairs]
        e_refs = refs[2 * n_pairs:2 * n_pairs + n_extra]
        o_ref = refs[2 * n_pairs + n_extra + n_deps]
        acc = None
        for a_ref, b_ref in zip(a_refs, b_refs):
            t = lax.dot_general(a_ref[...], b_ref[...], (dims, ((), ())), preferred_element_type=F32)
            acc = t if acc is None else acc + t

        def finish(v):
            if epilogue is not None:
                v = epilogue(v, *[e[...] for e in e_refs])
            o_ref[...] = v.astype(o_ref.dtype)

        if reduce_axis is None:
            finish(acc)
        else:
            acc_ref = refs[-1]
            k = pl.program_id(reduce_axis)

            @pl.when(k == 0)
            def _():
                acc_ref[...] = acc

            @pl.when(k > 0)
            def _():
                acc_ref[...] += acc

            @pl.when(k == n_steps - 1)
            def _():
                finish(acc_ref[...])

    sem = ["parallel"] * len(grid)
    if reduce_axis is not None:
        sem[reduce_axis] = "arbitrary"
    return pl.pallas_call(
        body, out_shape=out_shape, grid=grid, in_specs=in_specs, out_specs=out_spec,
        scratch_shapes=[pltpu.VMEM(tile, F32)] if reduce_axis is not None else [],
        compiler_params=_params(*sem), name=name)(*operands)


NN = ((1,), (0,))
NT = ((1,), (1,))
TN = ((0,), (0,))


def _mm_cols(name, a, w, *, tm, tn, out_dtype, cat, extras=(), epilogue=None):
    M, K = a.shape
    J, _, n = w.shape
    tn = min(tn, n)
    nb = n // tn
    if cat:
        shape, spec = (M, J * n), _bs((tm, tn), lambda j, i, k: (i, j * nb + k))
    else:
        shape, spec = (J, M, n), _bs((None, tm, tn), lambda j, i, k: (j, i, k))
    ex = [(e, _bs((tm, tn), lambda j, i, k: (i, j * nb + k))) for e in extras]
    return _mm(name, (J, M // tm, nb),
               [(a, _bs((tm, K), lambda j, i, k: (i, 0)), w, _bs((None, K, tn), lambda j, i, k: (j, 0, k)))],
               jax.ShapeDtypeStruct(shape, out_dtype), spec, NN, extras=ex, epilogue=epilogue)


def _mm_rows_t(name, a, w, *, tm, out_dtype):
    M, N = a.shape
    J, f, _ = w.shape
    return _mm(name, (J, M // tm),
               [(a, _bs((tm, N), lambda j, i: (i, 0)), w, _bs((None, f, N), lambda j, i: (j, 0, 0)))],
               jax.ShapeDtypeStruct((J, M, f), out_dtype), _bs((None, tm, f), lambda j, i: (j, i, 0)), NT)


def _mm_wgrad(name, a, b, *, a_cols, b_cols, tm, tn, J):
    def pick(arr, cols, t):
        if arr.ndim == 3:
            T, c = arr.shape[1], arr.shape[2]
            t = min(t, c)
            return T, c, t, (lambda sel: _bs((None, T, t), lambda j, i, k: (j, 0, sel(i, k))))
        T = arr.shape[0]
        c = arr.shape[1] if cols is None else cols
        t = min(t, c)
        per = c // t
        if cols is None:
            return T, c, t, (lambda sel: _bs((T, t), lambda j, i, k: (0, sel(i, k))))
        return T, c, t, (lambda sel: _bs((T, t), lambda j, i, k: (0, j * per + sel(i, k))))
    _, ca, tm, mk_a = pick(a, a_cols, tm)
    _, cb, tn, mk_b = pick(b, b_cols, tn)
    return _mm(name, (J, ca // tm, cb // tn),
               [(a, mk_a(lambda i, k: i), b, mk_b(lambda i, k: k))],
               jax.ShapeDtypeStruct((J, ca, cb), BF16), _bs((None, tm, tn), lambda j, i, k: (j, i, k)), TN)


def _tiled(arr, width=None, col=0, rowblk=0):
    return ("t", arr, arr.shape[1] if width is None else width, col, rowblk)


def _table(arr):
    return ("f", arr)


def _whole(arr):
    return ("w", arr)


def _ew(name, fn, ins, outs, *, n_rows, rows, reds=(), ncols=1, deps=()):
    nrb = n_rows // rows
    n_deps = len(deps)
    operands, in_specs = [], []
    for spec in ins:
        if spec[0] == "t":
            _, arr, width, col, rowblk = spec
            step = 1 if ncols > 1 else 0
            in_specs.append(_bs((rows, width), lambda c, i, col=col, rowblk=rowblk, step=step: (rowblk + i, col + c * step)))
        elif spec[0] == "f":
            arr = spec[1]
            in_specs.append(_bs((rows, arr.shape[1]), lambda c, i: (i, 0)))
        else:
            arr = spec[1]
            nd = arr.ndim
            if nd == 3:
                in_specs.append(_bs((None,) + arr.shape[1:], lambda c, i: (c, 0, 0)))
            else:
                in_specs.append(_bs(arr.shape, lambda c, i, nd=nd: (0,) * nd))
        operands.append(arr)
    out_shapes = [jax.ShapeDtypeStruct((n_rows, ncols * w), dt) for dt, w in outs]
    out_specs = [_bs((rows, w), lambda c, i: (i, c)) for _, w in outs]
    out_shapes += [jax.ShapeDtypeStruct((ncols, 1, w), F32) for w in reds]
    out_specs += [_bs((None, 1, w), lambda c, i: (c, 0, 0)) for w in reds]
    n_in, n_out, n_red = len(ins), len(outs), len(reds)
    operands += list(deps)
    in_specs += _any_specs(n_deps)

    def body(*refs):
        vals = fn(*[r[...] for r in refs[:n_in]])
        if not isinstance(vals, (tuple, list)):
            vals = (vals,)
        o_refs = refs[n_in + n_deps:]
        for o_ref, v in zip(o_refs[:n_out], vals[:n_out]):
            o_ref[...] = v.astype(o_ref.dtype)
        if n_red:
            i = pl.program_id(1)
            for r_ref, v in zip(o_refs[n_out:], vals[n_out:]):
                @pl.when(i == 0)
                def _(r_ref=r_ref):
                    r_ref[...] = jnp.zeros_like(r_ref)
                r_ref[...] += v

    res = pl.pallas_call(
        body, out_shape=out_shapes, grid=(ncols, nrb), in_specs=in_specs, out_specs=out_specs,
        compiler_params=_params("parallel", "arbitrary" if n_red else "parallel"), name=name)(*operands)
    return res


def _colsum(v):
    return jnp.sum(v, axis=0, keepdims=True)


def _rstd(x):
    return lax.rsqrt(jnp.mean(x * x, axis=-1, keepdims=True) + EPS)


def _sigmoid(x):
    return 1.0 / (1.0 + jnp.exp(-x))


def _norm_fwd(x, g):
    return x * _rstd(x) * g


def _norm_bwd(x, g, dy):
    r = _rstd(x)
    xh = x * r
    dxh = dy * g
    dx = r * (dxh - xh * jnp.mean(dxh * xh, axis=-1, keepdims=True))
    return dx, dy * xh


def _ffn_fwd(tag, x, gain, get_w, deps=()):
    T, D = x.shape
    (h,) = _ew(f"{tag}_norm", lambda xv, g: _norm_fwd(xv, g), [_tiled(x), _whole(gain)], [(BF16, D)], n_rows=T, rows=512,
               deps=deps)
    w1, w3 = get_w(f"{tag}_w1", h), get_w(f"{tag}_w3", h)
    J, f, _ = w1.shape
    tm = 1024

    def up(h_ref, w1_ref, w3_ref, u_ref, g_ref, a_ref):
        hv = h_ref[...]
        u = lax.dot_general(hv, w1_ref[...], (NT, ((), ())), preferred_element_type=F32)
        g = lax.dot_general(hv, w3_ref[...], (NT, ((), ())), preferred_element_type=F32)
        u_ref[...] = u.astype(BF16)
        g_ref[...] = g.astype(BF16)
        a_ref[...] = (u * _sigmoid(u) * g).astype(BF16)

    slab = _bs((None, tm, f), lambda j, i: (j, i, 0))
    w_spec = _bs((None, f, D), lambda j, i: (j, 0, 0))
    u, g, a = pl.pallas_call(
        up, out_shape=[jax.ShapeDtypeStruct((J, T, f), BF16)] * 3, grid=(J, T // tm),
        in_specs=[_bs((tm, D), lambda j, i: (i, 0)), w_spec, w_spec], out_specs=[slab] * 3,
        compiler_params=_params("parallel", "parallel"), name=f"{tag}_up")(h, w1, w3)
    w2 = get_w(f"{tag}_w2", a)
    y = _mm(f"{tag}_down", (T // 1024, D // 512),
            [(a, _bs((None, 1024, f), lambda i, k, j=j: (j, i, 0)), w2, _bs((None, f, 512), lambda i, k, j=j: (j, 0, k)))
             for j in range(J)],
            jax.ShapeDtypeStruct((T, D), F32), _bs((1024, 512), lambda i, k: (i, k)), NN,
            extras=[(x, _bs((1024, 512), lambda i, k: (i, k)))], epilogue=lambda acc, xv: xv + 0.5 * acc)
    return y, (h, u, g, a)


def _ffn_bwd(tag, x, gain, get_w, put_g, saved, dy, dy_half):
    h, u, g, a = saved
    T, D = x.shape
    w1, w3, w2 = [get_w(f"{tag}_{n}", dy_half) for n in ("w1", "w3", "w2")]
    J, f, _ = w1.shape
    dw2 = _mm_wgrad(f"{tag}_bwd_dw2", a, dy_half, a_cols=None, b_cols=None, tm=f, tn=512, J=J)
    deps = put_g({f"{tag}_w2": dw2})
    tm = 1024

    def up_bwd(dy_ref, w2_ref, u_ref, g_ref, *rest):
        du_ref, dg_ref = rest[-2:]
        da = lax.dot_general(dy_ref[...], w2_ref[...], (NT, ((), ())), preferred_element_type=F32)
        uv, gv = u_ref[...].astype(F32), g_ref[...].astype(F32)
        s = _sigmoid(uv)
        du_ref[...] = (da * gv * (s * (1.0 + uv * (1.0 - s)))).astype(BF16)
        dg_ref[...] = (da * (uv * s)).astype(BF16)

    slab = _bs((None, tm, f), lambda j, i: (j, i, 0))
    du, dg = pl.pallas_call(
        up_bwd, out_shape=[jax.ShapeDtypeStruct((J, T, f), BF16)] * 2, grid=(J, T // tm),
        in_specs=[_bs((tm, D), lambda j, i: (i, 0)), _bs((None, f, D), lambda j, i: (j, 0, 0)), slab, slab] + _any_specs(len(deps)),
        out_specs=[slab] * 2, compiler_params=_params("parallel", "parallel"), name=f"{tag}_bwd_up")(dy_half, w2, u, g, *deps)
    dw1 = _mm_wgrad(f"{tag}_bwd_dw1", du, h, a_cols=None, b_cols=None, tm=f, tn=512, J=J)
    dw3 = _mm_wgrad(f"{tag}_bwd_dw3", dg, h, a_cols=None, b_cols=None, tm=f, tn=512, J=J)
    deps = deps + put_g({f"{tag}_w1": dw1, f"{tag}_w3": dw3})
    pairs = []
    for j in range(J):
        a_spec = _bs((None, 512, f), lambda i, k, j=j: (j, i, 0))
        w_spec = _bs((None, f, 512), lambda i, k, j=j: (j, 0, k))
        pairs += [(du, a_spec, w1, w_spec), (dg, a_spec, w3, w_spec)]
    dh = _mm(f"{tag}_bwd_dh", (T // 512, D // 512), pairs,
             jax.ShapeDtypeStruct((T, D), F32), _bs((512, 512), lambda i, k: (i, k)), NN, deps=deps)

    def nb(xv, gv, dhv, dres):
        dx, dgr = _norm_bwd(xv, gv, dhv)
        dx = dx + dres
        return dx, 0.5 * dx, _colsum(dgr)

    dx, dx_half, dgain = _ew(f"{tag}_bwd_norm", nb, [_tiled(x), _whole(gain), _tiled(dh), _tiled(dy)],
                             [(F32, D), (BF16, D)], n_rows=T, rows=256, reds=(D,))
    return dx, dx_half, dgain.reshape(1, D)


def _t5_bucket(rel):
    n = N_BUCKETS // 2
    max_exact = n // 2
    ret = jnp.where(rel > 0, n, 0)
    a = jnp.abs(rel)
    af = jnp.maximum(a, 1).astype(F32)
    large = max_exact + (jnp.log(af / max_exact) / math.log(MAX_DISTANCE / max_exact) * (n - max_exact)).astype(jnp.int32)
    large = jnp.minimum(large, n - 1)
    return ret + jnp.where(a < max_exact, a, large)


def _band_steps():
    qi = jnp.arange(QB_A, dtype=jnp.int32)[:, None]
    kj = jnp.arange(3 * QB_A, dtype=jnp.int32)[None, :] - QB_A
    return kj - qi


def _bias_tiles(rel_bias):
    steps = _band_steps()
    buckets = jnp.stack([_t5_bucket(steps * d) for d in DILATIONS])
    inband = (jnp.abs(steps) <= BAND_HALF).astype(jnp.int32)
    n_heads = rel_bias.shape[1]

    def body(tab_ref, b_ref, m_ref, o_ref):
        hd = pl.program_id(0)
        bkt = b_ref[...]
        acc = jnp.zeros(bkt.shape, F32)
        for b in range(N_BUCKETS):
            acc = jnp.where(bkt == b, tab_ref[b, hd], acc)
        o_ref[...] = jnp.where(m_ref[...] > 0, acc, NEG_INF)

    return pl.pallas_call(
        body, out_shape=jax.ShapeDtypeStruct((n_heads, QB_A, 3 * QB_A), F32), grid=(n_heads,),
        in_specs=[pl.BlockSpec(memory_space=pltpu.SMEM),
                  _bs((None, QB_A, 3 * QB_A), lambda hd: (hd // HEADS_A, 0, 0)),
                  _bs((QB_A, 3 * QB_A), lambda hd: (0, 0))],
        out_specs=_bs((None, QB_A, 3 * QB_A), lambda hd: (hd, 0, 0)),
        compiler_params=_params("parallel"), name="a_bias_tiles")(rel_bias, buckets, inband)


def _bias_grad(dbias):
    steps = np.arange(3 * QB_A)[None, :] - QB_A - np.arange(QB_A)[:, None]
    inband = np.abs(steps) <= BAND_HALF
    present = []
    for d in DILATIONS:
        rel = steps * d
        a = np.abs(rel)
        large = 8 + (np.log(np.maximum(a, 1) / 8.0) / math.log(MAX_DISTANCE / 8.0) * 8).astype(np.int64)
        bk = np.where(rel > 0, 16, 0) + np.where(a < 8, a, np.minimum(large, 15))
        present.append(sorted(set(bk[inband].tolist())))
    buckets = jnp.stack([_t5_bucket(_band_steps() * d) for d in DILATIONS])
    n_heads = dbias.shape[0]

    def body(b_ref, d_ref, o_ref):
        row = lax.broadcasted_iota(jnp.int32, (N_BUCKETS, n_heads), 0)
        col = lax.broadcasted_iota(jnp.int32, (N_BUCKETS, n_heads), 1)
        out = jnp.zeros((N_BUCKETS, n_heads), F32)
        for grp in range(len(DILATIONS)):
            bkt = b_ref[grp]
            for hh in range(HEADS_A):
                hd = grp * HEADS_A + hh
                ds = d_ref[hd]
                for b in present[grp]:
                    tot = jnp.sum(jnp.where(bkt == b, ds, 0.0))
                    out = jnp.where((row == b) & (col == hd), tot, out)
        o_ref[...] = out

    return pl.pallas_call(
        body, out_shape=jax.ShapeDtypeStruct((N_BUCKETS, n_heads), F32),
        compiler_params=pltpu.CompilerParams(vmem_limit_bytes=VMEM_LIMIT_BYTES), name="a_bias_grad")(buckets, dbias)


def _lane_is_second_head(shape):
    return lax.broadcasted_iota(jnp.int32, shape, len(shape) - 1) >= HEAD_A


def _group_view(proj, grp, d):
    T = proj.shape[0]
    if d == 1:
        return proj, IN_WIDTH, grp * 3 * WIDTH_A
    part = proj[:, grp * 3 * WIDTH_A:(grp + 1) * 3 * WIDTH_A]
    return part.reshape(T // d, d * 3 * WIDTH_A), 3 * WIDTH_A, 0


def _stack_heads(v2, second):
    zero = jnp.zeros_like(v2)
    return jnp.concatenate([jnp.where(second, zero, v2), jnp.where(second, v2, zero)], axis=0)


def _unstack_heads(v, second):
    return jnp.where(second, v[QB_A:], v[:QB_A])


def _edge_mask(n, nblk):
    neg_prev = jnp.where(n > 0, 0.0, NEG_INF)
    neg_next = jnp.where(n < nblk - 1, 0.0, NEG_INF)
    return jnp.concatenate([jnp.full((1, QB_A), neg_prev, F32), jnp.zeros((1, QB_A), F32),
                            jnp.full((1, QB_A), neg_next, F32)], axis=1)


def _dil_fwd(proj, bias, grp, d):
    T = proj.shape[0]
    L = T // d
    nblk = L // QB_A
    pv, width, base = _group_view(proj, grp, d)
    cb, b0 = width // WIDTH_A, base // WIDTH_A
    W2 = 2 * HEAD_A
    scale = HEAD_A ** -0.5

    def body(q_ref, kp_ref, kc_ref, kn_ref, vp_ref, vc_ref, vn_ref, b_ref, o_ref, l_ref):
        edge = _edge_mask(pl.program_id(1), nblk)
        second = _lane_is_second_head((QB_A, W2))
        for hp in range(HEADS_A // 2):
            cols = slice(hp * W2, (hp + 1) * W2)
            kcat = jnp.concatenate([kp_ref[:, cols], kc_ref[:, cols], kn_ref[:, cols]], axis=0)
            vcat = jnp.concatenate([vp_ref[:, cols], vc_ref[:, cols], vn_ref[:, cols]], axis=0)
            qs = _stack_heads(q_ref[:, cols], second)
            s = lax.dot_general(qs, kcat, (NT, ((), ())), preferred_element_type=F32)
            s = s * scale + b_ref[2 * hp:2 * hp + 2].reshape(2 * QB_A, 3 * QB_A) + edge
            m = jnp.max(s, axis=-1, keepdims=True)
            p = jnp.exp(s - m)
            l = jnp.sum(p, axis=-1, keepdims=True)
            res = jnp.dot(p.astype(BF16), vcat, preferred_element_type=F32) / l
            o_ref[:, cols] = _unstack_heads(res, second).astype(o_ref.dtype)
            l_ref[:, cols] = _unstack_heads(jnp.broadcast_to(m + jnp.log(l), (2 * QB_A, W2)), second)

    def spec(part, dn):
        return _bs((QB_A, WIDTH_A), lambda r, n: (jnp.clip(n + dn, 0, nblk - 1), r * cb + b0 + part))

    in_specs = [spec(0, 0)] + [spec(1, dn) for dn in (-1, 0, 1)] + [spec(2, dn) for dn in (-1, 0, 1)]
    in_specs += [_bs((HEADS_A, QB_A, 3 * QB_A), lambda r, n: (0, 0, 0))]
    o, lse = pl.pallas_call(
        body, out_shape=[jax.ShapeDtypeStruct((L, d * WIDTH_A), BF16), jax.ShapeDtypeStruct((L, d * WIDTH_A), F32)],
        grid=(d, nblk), in_specs=in_specs,
        out_specs=[_bs((QB_A, WIDTH_A), lambda r, n: (n, r)), _bs((QB_A, WIDTH_A), lambda r, n: (n, r))],
        compiler_params=_params("parallel", "parallel"), name=f"a_fwd_d{d}")(pv, pv, pv, pv, pv, pv, pv, bias)
    return o.reshape(T, WIDTH_A), lse.reshape(T, WIDTH_A)


def _dil_bwd(proj, bias, do, lse, cterm, grp, d):
    T = proj.shape[0]
    L = T // d
    nblk = L // QB_A
    W2 = 2 * HEAD_A
    pv, width, base = _group_view(proj, grp, d)
    cb, b0 = width // W2, base // W2
    ob = WIDTH_A // W2
    view = lambda a: a.reshape(L, d * WIDTH_A)
    scale = HEAD_A ** -0.5

    def body(q_ref, kp_ref, kc_ref, kn_ref, vp_ref, vc_ref, vn_ref, do_ref, l_ref, c_ref, b_ref,
             dq_ref, dk_ref, dv_ref, db_ref):
        r, n = pl.program_id(1), pl.program_id(2)

        @pl.when(n == 0)
        def _():
            dk_ref[...] = jnp.zeros_like(dk_ref)
            dv_ref[...] = jnp.zeros_like(dv_ref)

        @pl.when((n == 0) & (r == 0))
        def _():
            db_ref[...] = jnp.zeros_like(db_ref)

        second = _lane_is_second_head((QB_A, W2))
        kcat = jnp.concatenate([kp_ref[...], kc_ref[...], kn_ref[...]], axis=0)
        vcat = jnp.concatenate([vp_ref[...], vc_ref[...], vn_ref[...]], axis=0)
        qs, dos = _stack_heads(q_ref[...], second), _stack_heads(do_ref[...], second)
        lse2, c2 = l_ref[...], c_ref[...]
        lse_rows = jnp.concatenate([lse2[:, 0:1], lse2[:, HEAD_A:HEAD_A + 1]], axis=0)
        c_rows = jnp.concatenate([c2[:, 0:1], c2[:, HEAD_A:HEAD_A + 1]], axis=0)
        s = lax.dot_general(qs, kcat, (NT, ((), ())), preferred_element_type=F32)
        p = jnp.exp(s * scale + b_ref[...].reshape(2 * QB_A, 3 * QB_A) + _edge_mask(n, nblk) - lse_rows)
        dp = lax.dot_general(dos, vcat, (NT, ((), ())), preferred_element_type=F32)
        ds = p * (dp + c_rows)
        db_ref[...] += ds.reshape(2, QB_A, 3 * QB_A)
        pb, dsb = p.astype(BF16), (ds * scale).astype(BF16)
        dq_ref[...] = _unstack_heads(jnp.dot(dsb, kcat, preferred_element_type=F32), second).astype(dq_ref.dtype)
        dkc = lax.dot_general(dsb, qs, (TN, ((), ())), preferred_element_type=F32)
        dvc = lax.dot_general(pb, dos, (TN, ((), ())), preferred_element_type=F32)
        for b, dn in enumerate((-1, 0, 1)):
            start = pl.multiple_of(jnp.clip(n + dn, 0, nblk - 1) * QB_A, QB_A)
            dk_ref[pl.ds(start, QB_A), :] += dkc[b * QB_A:(b + 1) * QB_A]
            dv_ref[pl.ds(start, QB_A), :] += dvc[b * QB_A:(b + 1) * QB_A]

    def spec(part, dn):
        return _bs((QB_A, W2), lambda hp, r, n: (jnp.clip(n + dn, 0, nblk - 1), r * cb + b0 + part * ob + hp))

    in_specs = [spec(0, 0)] + [spec(1, dn) for dn in (-1, 0, 1)] + [spec(2, dn) for dn in (-1, 0, 1)]
    in_specs += [_bs((QB_A, W2), lambda hp, r, n: (n, r * ob + hp))] * 3
    in_specs += [_bs((2, QB_A, 3 * QB_A), lambda hp, r, n: (hp, 0, 0))]
    out_shape = [jax.ShapeDtypeStruct((L, d * WIDTH_A), BF16), jax.ShapeDtypeStruct((L, d * WIDTH_A), F32),
                 jax.ShapeDtypeStruct((L, d * WIDTH_A), F32), jax.ShapeDtypeStruct((HEADS_A, QB_A, 3 * QB_A), F32)]
    out_specs = [_bs((QB_A, W2), lambda hp, r, n: (n, r * ob + hp)),
                 _bs((L, W2), lambda hp, r, n: (0, r * ob + hp)), _bs((L, W2), lambda hp, r, n: (0, r * ob + hp)),
                 _bs((2, QB_A, 3 * QB_A), lambda hp, r, n: (hp, 0, 0))]
    dq, dk, dv, db = pl.pallas_call(
        body, out_shape=out_shape, grid=(ob, d, nblk), in_specs=in_specs, out_specs=out_specs,
        compiler_params=_params("arbitrary", "arbitrary", "arbitrary"), name=f"a_bwd_d{d}")(
            pv, pv, pv, pv, pv, pv, pv, view(do), view(lse), view(cterm), bias)
    return dq.reshape(T, WIDTH_A), dk.reshape(T, WIDTH_A), dv.reshape(T, WIDTH_A), db


def _segment_ones():
    i = np.arange(WIDTH_A)
    return jnp.asarray((i[:, None] // HEAD_A == i[None, :] // HEAD_A).astype(np.float32), dtype=BF16)


def _group_weights(l0, l1, l2):
    m = jnp.maximum(jnp.maximum(l0, l1), l2)
    e = [jnp.exp(l - m) for l in (l0, l1, l2)]
    z = e[0] + e[1] + e[2]
    return [ei / z for ei in e]


def _combine_fwd(outs, lses):
    T = outs[0].shape[0]

    def fn(o0, o1, o2, l0, l1, l2):
        w = _group_weights(l0, l1, l2)
        return w[0] * o0.astype(F32) + w[1] * o1.astype(F32) + w[2] * o2.astype(F32)

    (oa,) = _ew("a_combine", fn, [_tiled(o) for o in outs] + [_tiled(l) for l in lses], [(BF16, WIDTH_A)], n_rows=T, rows=512)
    return oa


def _combine_bwd(doa, outs, lses):
    T = doa.shape[0]

    def fn(d, o0, o1, o2, l0, l1, l2, seg):
        d = d.astype(F32)
        w = _group_weights(l0, l1, l2)
        tot = jnp.zeros(d.shape, F32)
        for wg, og in zip(w, (o0, o1, o2)):
            prod = wg * d * og.astype(F32)
            hi = prod.astype(BF16)
            lo = (prod - hi.astype(F32)).astype(BF16)
            tot = tot + jnp.dot(hi, seg, preferred_element_type=F32) + jnp.dot(lo, seg, preferred_element_type=F32)
        return tuple(wg * d for wg in w) + tuple(-wg * tot for wg in w)

    res = _ew("a_combine_bwd", fn, [_tiled(doa)] + [_tiled(o) for o in outs] + [_tiled(l) for l in lses] + [_whole(_segment_ones())],
              [(BF16, WIDTH_A)] * 3 + [(F32, WIDTH_A)] * 3, n_rows=T, rows=256)
    return res[:3], res[3:]


def _rope_tables(T):
    rows = T // GRID_W
    row = jnp.repeat(jnp.arange(rows, dtype=F32), GRID_W)
    col = jnp.tile(jnp.arange(GRID_W, dtype=F32), rows)
    n_freq = HEAD_B // 4
    freq = ROPE_THETA ** (-jnp.arange(n_freq, dtype=F32) / n_freq)
    ang = jnp.concatenate([row[:, None] * freq, col[:, None] * freq], axis=-1)
    cos, sin = jnp.repeat(jnp.cos(ang), 2, axis=1), jnp.repeat(jnp.sin(ang), 2, axis=1)
    sign = jnp.where(jnp.arange(HEAD_B) % 2 == 0, -1.0, 1.0).astype(F32)
    return cos, sin * sign


def _swap_pairs(v):
    even = lax.broadcasted_iota(jnp.int32, v.shape, v.ndim - 1) % 2 == 0
    n = v.shape[-1]
    return jnp.where(even, pltpu.roll(v, n - 1, v.ndim - 1), pltpu.roll(v, 1, v.ndim - 1))


def _qk_fwd(name, proj, col0, n_heads, gain, cos, sin):
    T = proj.shape[0]

    def fn(xr, g, c, s):
        xn = _norm_fwd(xr.astype(F32), g)
        return xn * c + _swap_pairs(xn) * s

    (out,) = _ew(name, fn, [_tiled(proj, HEAD_B, col0 // HEAD_B), _whole(gain), _table(cos), _table(sin)],
                 [(BF16, HEAD_B)], n_rows=T, rows=2048, ncols=n_heads)
    return out


def _qk_bwd(name, dout, proj, col0, n_heads, gain, cos, sin):
    T = proj.shape[0]

    def fn(dv, xr, g, c, s):
        dv = dv.astype(F32)
        dxn = c * dv + _swap_pairs(s * dv)
        dx, dgr = _norm_bwd(xr.astype(F32), g, dxn)
        return dx, _colsum(dgr)

    dx, dg = _ew(name, fn, [_tiled(dout, HEAD_B, 0), _tiled(proj, HEAD_B, col0 // HEAD_B), _whole(gain),
                            _table(cos), _table(sin)],
                 [(BF16, HEAD_B)], n_rows=T, rows=2048, reds=(HEAD_B,), ncols=n_heads)
    return dx, jnp.sum(dg, axis=0)


def _gqa_fwd(qn, kn, proj):
    T = qn.shape[0]
    GW = 4 * HEAD_B
    scale = HEAD_B ** -0.5

    def body(q_ref, k_ref, v_ref, o_ref, l_ref):
        k, v = k_ref[...], v_ref[...]
        lane = lax.broadcasted_iota(jnp.int32, (QB_B, HEAD_B), 1)
        lse_all = jnp.zeros((QB_B, HEAD_B), F32)
        for g in range(4):
            cols = slice(g * HEAD_B, (g + 1) * HEAD_B)
            s = lax.dot_general(q_ref[:, cols], k, (NT, ((), ())), preferred_element_type=F32) * scale
            m = jnp.max(s, axis=-1, keepdims=True)
            p = jnp.exp(s - m)
            l = jnp.sum(p, axis=-1, keepdims=True)
            o = jnp.dot(p.astype(BF16), v, preferred_element_type=F32) / l
            o_ref[:, cols] = o.astype(o_ref.dtype)
            lse_all = jnp.where(lane == g, m + jnp.log(l), lse_all)
        l_ref[...] = lse_all

    return pl.pallas_call(
        body, out_shape=[jax.ShapeDtypeStruct((T, 2 * GW), BF16), jax.ShapeDtypeStruct((2, T, HEAD_B), F32)],
        grid=(2, T // QB_B),
        in_specs=[_bs((QB_B, GW), lambda kv, i: (i, kv)), _bs((T, HEAD_B), lambda kv, i: (0, kv)),
                  _bs((T, HEAD_B), lambda kv, i: (0, B_V // HEAD_B + kv))],
        out_specs=[_bs((QB_B, GW), lambda kv, i: (i, kv)), _bs((None, QB_B, HEAD_B), lambda kv, i: (kv, i, 0))],
        compiler_params=_params("parallel", "parallel"), name="b_fwd")(qn, kn, proj)


def _gqa_bwd(qn, kn, proj, o, lse, do):
    T = qn.shape[0]
    GW = 4 * HEAD_B
    scale = HEAD_B ** -0.5

    def body(q_ref, k_ref, v_ref, o_ref, l_ref, do_ref, dq_ref, dk_ref, dv_ref):
        i = pl.program_id(1)

        @pl.when(i == 0)
        def _():
            dk_ref[...] = jnp.zeros_like(dk_ref)
            dv_ref[...] = jnp.zeros_like(dv_ref)

        k, v = k_ref[...], v_ref[...]
        lse_all = l_ref[...]
        for g in range(4):
            cols = slice(g * HEAD_B, (g + 1) * HEAD_B)
            q, dob = q_ref[:, cols], do_ref[:, cols]
            delta = jnp.sum(dob.astype(F32) * o_ref[:, cols].astype(F32), axis=-1, keepdims=True)
            s = lax.dot_general(q, k, (NT, ((), ())), preferred_element_type=F32) * scale
            p = jnp.exp(s - lse_all[:, g:g + 1])
            dp = lax.dot_general(dob, v, (NT, ((), ())), preferred_element_type=F32)
            ds = (p * (dp - delta) * scale).astype(BF16)
            dq_ref[:, cols] = jnp.dot(ds, k, preferred_element_type=F32).astype(dq_ref.dtype)
            dk_ref[...] += lax.dot_general(ds, q, (TN, ((), ())), preferred_element_type=F32)
            dv_ref[...] += lax.dot_general(p.astype(BF16), dob, (TN, ((), ())), preferred_element_type=F32)

    return pl.pallas_call(
        body, out_shape=[jax.ShapeDtypeStruct((T, 2 * GW), BF16), jax.ShapeDtypeStruct((T, 2 * HEAD_B), F32),
                         jax.ShapeDtypeStruct((T, 2 * HEAD_B), F32)],
        grid=(2, T // QB_B),
        in_specs=[_bs((QB_B, GW), lambda kv, i: (i, kv)), _bs((T, HEAD_B), lambda kv, i: (0, kv)),
                  _bs((T, HEAD_B), lambda kv, i: (0, B_V // HEAD_B + kv)), _bs((QB_B, GW), lambda kv, i: (i, kv)),
                  _bs((None, QB_B, HEAD_B), lambda kv, i: (kv, i, 0)), _bs((QB_B, GW), lambda kv, i: (i, kv))],
        out_specs=[_bs((QB_B, GW), lambda kv, i: (i, kv)), _bs((T, HEAD_B), lambda kv, i: (0, kv)),
                   _bs((T, HEAD_B), lambda kv, i: (0, kv))],
        compiler_params=_params("parallel", "arbitrary"), name="b_bwd")(qn, kn, proj, o, lse, do)


def _local_step(x, target, small, get_w, put_g, deps=()):
    T, D = x.shape
    gs = {}

    x1, ffn1_saved = _ffn_fwd("ffn1", x, small["ffn1_norm"], get_w, deps)
    (h2,) = _ew("mix_norm", lambda xv, g: _norm_fwd(xv, g), [_tiled(x1), _whole(small["mix_norm"])], [(BF16, D)], n_rows=T, rows=512)
    w_in = get_w("w_in", h2)
    nq = w_in.shape[2]
    tpq = nq // WIDTH_A

    def proj_tile(j, k):
        c = j * tpq + k
        return jnp.where(c < 3 * len(DILATIONS), (c % 3) * 3 + c // 3, c)

    proj = _mm("mix_in", (4, T // 1024, tpq),
               [(h2, _bs((1024, D), lambda j, i, k: (i, 0)), w_in, _bs((None, D, WIDTH_A), lambda j, i, k: (j, 0, k)))],
               jax.ShapeDtypeStruct((T, IN_WIDTH), BF16), _bs((1024, WIDTH_A), lambda j, i, k: (i, proj_tile(j, k))), NN)

    bias = _bias_tiles(small["rel_bias"])
    a_outs, a_lses = [], []
    for grp, d in enumerate(DILATIONS):
        o, l = _dil_fwd(proj, bias[grp * HEADS_A:(grp + 1) * HEADS_A], grp, d)
        a_outs.append(o)
        a_lses.append(l)
    o_a = _combine_fwd(a_outs, a_lses)

    cos, sin = _rope_tables(T)
    qn = _qk_fwd("b_qnorm", proj, B_Q, 8, small["q_norm"], cos, sin)
    kn = _qk_fwd("b_knorm", proj, B_K, 2, small["k_norm"], cos, sin)
    o_b, lse_b = _gqa_fwd(qn, kn, proj)

    wa, wb3, w_out3 = get_w("w_branch_a", o_b), get_w("w_branch_b", o_b).reshape(1, D, D), get_w("w_out", o_b).reshape(1, D, D)
    t_a = _mm_cols("mix_branch_a", o_a, wa, tm=512, tn=256, out_dtype=BF16, cat=True)
    t_b = _mm_cols("mix_branch_b", o_b, wb3, tm=512, tn=512, out_dtype=BF16, cat=True)
    bg_a, bg_b = small["b_gate"][:, :D], small["b_gate"][:, D:]

    def merge(ta, tb, ga, gb_, ba, bb):
        sa, sb = _sigmoid(ga.astype(F32) + ba), _sigmoid(gb_.astype(F32) + bb)
        return sa * ta.astype(F32) + sb * tb.astype(F32)

    gate_ins = [_tiled(proj, D, G_A // D), _tiled(proj, D, G_B // D), _whole(bg_a), _whole(bg_b)]
    (merged,) = _ew("mix_merge", merge, [_tiled(t_a), _tiled(t_b)] + gate_ins, [(BF16, D)], n_rows=T, rows=512)
    x2 = _mm_cols("mix_out", merged, w_out3, tm=512, tn=512, out_dtype=F32, cat=True,
                  extras=[x1], epilogue=lambda acc, xv: xv + acc)
    x3, ffn2_saved = _ffn_fwd("ffn2", x2, small["ffn2_norm"], get_w)

    def head(xv, g, tv):
        r = _rstd(xv)
        xh = xv * r
        e = xh * g - tv
        dy = e * (1.0 / D)
        dxh = dy * g
        dx = r * (dxh - xh * jnp.mean(dxh * xh, axis=-1, keepdims=True))
        return dx, 0.5 * dx, _colsum(e * e) * (0.5 / D), _colsum(dy * xh)

    dx3, dx3_half, loss_cols, g_final = _ew("loss_head", head, [_tiled(x3), _whole(small["final_norm"].reshape(1, D)), _tiled(target)],
                                            [(F32, D), (BF16, D)], n_rows=T, rows=256, reds=(D, D))
    gs["final_norm"] = g_final.reshape(D)

    dx2, _, gs["ffn2_norm"] = _ffn_bwd("ffn2", x2, small["ffn2_norm"], get_w, put_g, ffn2_saved, dx3, dx3_half)

    (dmix,) = _ew("mix_bwd_cast", lambda v: v, [_tiled(dx2)], [(BF16, D)], n_rows=T, rows=512)
    g_out = _mm_wgrad("mix_bwd_dwout", merged, dmix, a_cols=D // 4, b_cols=None, tm=256, tn=512, J=4).reshape(D, D)
    dmerged = _mm_rows_t("mix_bwd_dmerged", dmix, w_out3, tm=512, out_dtype=BF16).reshape(T, D)

    def merge_bwd(dm, ta, tb, ga, gb_, ba, bb):
        dm, ta, tb = dm.astype(F32), ta.astype(F32), tb.astype(F32)
        sa, sb = _sigmoid(ga.astype(F32) + ba), _sigmoid(gb_.astype(F32) + bb)
        dga, dgb = dm * ta * sa * (1.0 - sa), dm * tb * sb * (1.0 - sb)
        return dm * sa, dm * sb, dga, dgb, _colsum(dga), _colsum(dgb)

    dta, dtb, dga, dgb, dba, dbb = _ew("mix_bwd_merge", merge_bwd, [_tiled(dmerged), _tiled(t_a), _tiled(t_b)] + gate_ins,
                                       [(BF16, D)] * 4, n_rows=T, rows=256, reds=(D, D))
    gs["b_gate"] = jnp.concatenate([dba.reshape(1, D), dbb.reshape(1, D)], axis=1)

    g_a = _mm_wgrad("mix_bwd_dwa", o_a, dta, a_cols=None, b_cols=D // 4, tm=WIDTH_A, tn=256, J=4)
    g_b = _mm_wgrad("mix_bwd_dwb", o_b, dtb, a_cols=D // 4, b_cols=None, tm=256, tn=512, J=4).reshape(D, D)
    deps = put_g({"w_out": g_out, "w_branch_a": g_a, "w_branch_b": g_b})
    do_a = _mm("mix_bwd_doa", (T // 1024,),
               [(dta, _bs((1024, D // 4), lambda i, j=j: (i, j)), wa, _bs((None, WIDTH_A, D // 4), lambda i, j=j: (j, 0, 0)))
                for j in range(4)],
               jax.ShapeDtypeStruct((T, WIDTH_A), BF16), _bs((1024, WIDTH_A), lambda i: (i, 0)), NT, deps=deps)
    do_b = _mm_rows_t("mix_bwd_dob", dtb, wb3, tm=512, out_dtype=BF16).reshape(T, D)

    dqn, dkn, dv_b = _gqa_bwd(qn, kn, proj, o_b, lse_b, do_b)
    dq_b, gs["q_norm"] = _qk_bwd("b_bwd_qnorm", dqn, proj, B_Q, 8, small["q_norm"], cos, sin)
    dk_b, gs["k_norm"] = _qk_bwd("b_bwd_knorm", dkn, proj, B_K, 2, small["k_norm"], cos, sin)

    do_groups, c_groups = _combine_bwd(do_a, a_outs, a_lses)
    dqs, dks, dvs, dbs = [], [], [], []
    for grp, d in enumerate(DILATIONS):
        dq, dk, dv, db = _dil_bwd(proj, bias[grp * HEADS_A:(grp + 1) * HEADS_A], do_groups[grp], a_lses[grp], c_groups[grp], grp, d)
        dqs.append(dq), dks.append(dk), dvs.append(dv), dbs.append(db)
    gs["rel_bias"] = _bias_grad(jnp.concatenate(dbs, axis=0))

    dproj = jnp.concatenate([p.astype(BF16) for p in dqs + dks + dvs + [dq_b, dk_b, dv_b, dga, dgb]], axis=1)
    nq = w_in.shape[2]
    deps = put_g({"w_in": _mm_wgrad("mix_bwd_dwin", h2, dproj, a_cols=None, b_cols=nq, tm=512, tn=512, J=4)})
    dh2 = _mm("mix_bwd_dh", (T // 512, D // 512),
              [(dproj, _bs((512, nq), lambda i, k, j=j: (i, j)), w_in, _bs((None, 512, nq), lambda i, k, j=j: (j, k, 0)))
               for j in range(4)],
              jax.ShapeDtypeStruct((T, D), F32), _bs((512, 512), lambda i, k: (i, k)), NT, deps=deps)

    def nb(xv, gv, dhv, dres):
        dx, dgr = _norm_bwd(xv, gv, dhv)
        dx = dx + dres
        return dx, 0.5 * dx, _colsum(dgr)

    dx1, dx1_half, g_mix = _ew("mix_bwd_norm", nb, [_tiled(x1), _whole(small["mix_norm"]), _tiled(dh2), _tiled(dx2)],
                               [(F32, D), (BF16, D)], n_rows=T, rows=256, reds=(D,))
    gs["mix_norm"] = g_mix.reshape(1, D)

    dx0, _, gs["ffn1_norm"] = _ffn_bwd("ffn1", x, small["ffn1_norm"], get_w, put_g, ffn1_saved, dx1, dx1_half)
    return loss_cols.reshape(1, D), dx0, gs


def _position():
    return lax.axis_index("x"), lax.axis_index("y"), lax.axis_index("c")


def _any_specs(n):
    return [pl.BlockSpec(memory_space=pl.ANY)] * n


HBM_SPEC = pl.BlockSpec(memory_space=pltpu.HBM)
SEM_SPEC = pl.BlockSpec(memory_space=pltpu.SEMAPHORE)
DATAFLOW_EFFECT = pltpu.SideEffectType.DATAFLOW_SIDE_EFFECTING
N_PEER_CHIPS = 3
LANES = 128


def _quarter_copies(srcs, lands, send_sems, recv_sems, scatter):
    x, y, c = _position()
    me = 2 * x + y
    peers = [(1 - x, y, c), (x, 1 - y, c), (1 - x, 1 - y, c)]
    copies = []
    for src, land, send, recv in zip(srcs, lands, send_sems, recv_sems):
        half = land.shape[1] // 2
        rows = pl.ds(c * half, half)
        for p, (px, py, pc) in enumerate(peers):
            copies.append(pltpu.make_async_remote_copy(
                src_ref=src.at[2 * px + py] if scatter else src.at[rows], dst_ref=land.at[me] if scatter else land.at[me, rows],
                send_sem=send.at[p], recv_sem=recv.at[p], device_id=(px, py, pc), device_id_type=MESH))
    return copies


def _fill_from_sibling(name, stacks):
    n = len(stacks)

    def body(*refs):
        outs = refs[n:2 * n]
        send_sems, recv_sems = refs[2 * n:]
        x, y, c = _position()
        copies = []
        for i, ref in enumerate(outs):
            half = ref.shape[1] // 2
            rows = pl.ds(c * half, half)
            for p, k in enumerate((2 * (1 - x) + y, 2 * x + (1 - y), 2 * (1 - x) + (1 - y))):
                cp = pltpu.make_async_remote_copy(ref.at[k, rows], ref.at[k, rows], send_sems.at[3 * i + p], recv_sems.at[3 * i + p],
                                                  device_id=(x, y, 1 - c), device_id_type=MESH)
                cp.start()
                copies.append(cp)
        for cp in copies:
            cp.wait()

    return pl.pallas_call(
        body, out_shape=[jax.ShapeDtypeStruct(s.shape, s.dtype) for s in stacks],
        in_specs=_any_specs(n), out_specs=_any_specs(n), input_output_aliases={i: i for i in range(n)},
        scratch_shapes=[pltpu.SemaphoreType.DMA((N_PEER_CHIPS * n,)), pltpu.SemaphoreType.DMA((N_PEER_CHIPS * n,))],
        compiler_params=pltpu.CompilerParams(has_side_effects=True), name=name)(*stacks)


def _exchange_start(name, srcs, lands, scatter):
    n = len(srcs)

    def body(*refs):
        src_refs, land_refs = refs[:n], refs[n:2 * n]
        send_sems, recv_sems = refs[2 * n:3 * n], refs[3 * n:4 * n]
        token = refs[6 * n]
        for cp in _quarter_copies(src_refs, land_refs, send_sems, recv_sems, scatter):
            cp.start()
        token[...] = jnp.zeros_like(token)

    sem = pltpu.SemaphoreType.DMA((N_PEER_CHIPS,))
    out_shape = [sem] * (2 * n) + [pltpu.HBM(a.shape, a.dtype) for a in list(srcs) + list(lands)]
    out_shape += [jax.ShapeDtypeStruct((8, LANES), F32)]
    res = pl.pallas_call(
        body, name=name, out_shape=out_shape, in_specs=[HBM_SPEC] * (2 * n),
        out_specs=[SEM_SPEC] * (2 * n) + [HBM_SPEC] * (2 * n) + [pl.BlockSpec(memory_space=pltpu.VMEM)],
        input_output_aliases={i: 2 * n + i for i in range(2 * n)},
        compiler_params=pltpu.CompilerParams(has_side_effects=DATAFLOW_EFFECT),
    )(*[pltpu.with_memory_space_constraint(a, pltpu.HBM) for a in list(srcs) + list(lands)])
    return res[:n], res[n:2 * n], res[2 * n:3 * n], res[3 * n:4 * n], res[4 * n]


def _exchange_wait(name, srcs, lands, send_sems, recv_sems, after, scatter):
    n = len(srcs)

    def body(*refs):
        src_refs, land_refs = refs[:n], refs[n:2 * n]
        sends, recvs = refs[2 * n:3 * n], refs[3 * n:4 * n]
        for cp in _quarter_copies(src_refs, land_refs, sends, recvs, scatter):
            cp.wait_send()
            cp.wait_recv()

    res = pl.pallas_call(
        body, name=name, out_shape=[pltpu.HBM(a.shape, a.dtype) for a in list(srcs) + list(lands)],
        in_specs=[HBM_SPEC] * (2 * n) + [SEM_SPEC] * (2 * n) + [pl.BlockSpec(memory_space=pl.ANY)],
        out_specs=[HBM_SPEC] * (2 * n), input_output_aliases={i: i for i in range(2 * n)},
        compiler_params=pltpu.CompilerParams(has_side_effects=DATAFLOW_EFFECT),
    )(*srcs, *lands, *send_sems, *recv_sems, after)
    return res[n:]


def _own_slot(stack_shape, own, dtype):
    me = 2 * lax.axis_index("x") + lax.axis_index("y")
    return lax.dynamic_update_slice(lax.empty(stack_shape, dtype), own[None], (me,) + (0,) * own.ndim)


def _swap_with_sibling(parts):
    n = len(parts)

    def body(*refs):
        ins, outs = refs[:n], refs[n:2 * n]
        send_sems, recv_sems = refs[2 * n:]
        x, y, c = _position()
        copies = []
        for i in range(n):
            cp = pltpu.make_async_remote_copy(ins[i], outs[i], send_sems.at[i], recv_sems.at[i],
                                              device_id=(x, y, 1 - c), device_id_type=MESH)
            cp.start()
            copies.append(cp)
        for cp in copies:
            cp.wait()

    return pl.pallas_call(
        body, out_shape=[jax.ShapeDtypeStruct(s.shape, s.dtype) for s in parts],
        in_specs=_any_specs(n), out_specs=_any_specs(n),
        scratch_shapes=[pltpu.SemaphoreType.DMA((n,)), pltpu.SemaphoreType.DMA((n,))],
        compiler_params=pltpu.CompilerParams(has_side_effects=True), name="swap_with_sibling")(*parts)


def _allreduce_small(buf):
    R, C = buf.shape
    flips = [(fx, fy, fc) for fx in (0, 1) for fy in (0, 1) for fc in (0, 1)][1:]

    def body(in_ref, out_ref, land_ref, send_sems, recv_sems):
        x, y, c = _position()
        me = 4 * x + 2 * y + c
        copies = []
        for k, (fx, fy, fc) in enumerate(flips):
            px, py, pc = (1 - x if fx else x), (1 - y if fy else y), (1 - c if fc else c)
            cp = pltpu.make_async_remote_copy(in_ref, land_ref.at[me], send_sems.at[k], recv_sems.at[k],
                                              device_id=(px, py, pc), device_id_type=MESH)
            cp.start()
            copies.append(cp)
        land_ref[me] = in_ref[...]
        for cp in copies:
            cp.wait()
        acc = land_ref[0]
        for k in range(1, 8):
            acc = acc + land_ref[k]
        out_ref[...] = acc

    return pl.pallas_call(
        body, out_shape=jax.ShapeDtypeStruct((R, C), F32),
        in_specs=[pl.BlockSpec(memory_space=pltpu.VMEM)], out_specs=pl.BlockSpec(memory_space=pltpu.VMEM),
        scratch_shapes=[pltpu.VMEM((8, R, C), F32), pltpu.SemaphoreType.DMA((7,)), pltpu.SemaphoreType.DMA((7,))],
        compiler_params=pltpu.CompilerParams(has_side_effects=True), name="allreduce_small")(buf)


def _adamw_math(w, g, m, v):
    m2 = ADAM_B1 * m + (1.0 - ADAM_B1) * g
    v2 = ADAM_B2 * v + (1.0 - ADAM_B2) * (g * g)
    m_hat = m2 / (1.0 - ADAM_B1 ** ADAM_STEP)
    v_hat = v2 / (1.0 - ADAM_B2 ** ADAM_STEP)
    delta = -ADAM_LR * (m_hat / (jnp.sqrt(v_hat) + ADAM_EPS) + ADAM_WD * w)
    return delta, m2, v2


def _adamw_big(name, w, m, v, part_mine, part_sibling):
    R, C = w.shape
    rows = 256 if R % 256 == 0 else R // 2 if (R // 2) % 8 == 0 else R

    def fn(wv, mv, vv, a, b):
        g = a + b
        return (g,) + _adamw_math(wv, g, mv, vv)

    return _ew(name, fn, [_tiled(w), _tiled(m), _tiled(v), _tiled(part_mine), _tiled(part_sibling)], [(F32, C)] * 4, n_rows=R, rows=rows)


def _sum_four(name, stack):
    _, R, C = stack.shape
    rows = 256 if R % 256 == 0 else R // 2 if (R // 2) % 8 == 0 else R
    flat = stack.reshape(4 * R, C)
    nrb = R // rows

    def fn(a, b, c, d):
        return ((a.astype(F32) + b.astype(F32)) + c.astype(F32)) + d.astype(F32)

    (out,) = _ew(name, fn, [_tiled(flat, None, 0, k * nrb) for k in range(4)], [(F32, C)], n_rows=R, rows=rows)
    return out


BIG = ("ffn1_w1", "ffn1_w3", "ffn1_w2", "w_in", "w_branch_a", "w_branch_b", "w_out", "ffn2_w1", "ffn2_w3", "ffn2_w2")
SMALL = ("ffn1_norm", "mix_norm", "b_gate", "q_norm", "k_norm", "rel_bias", "ffn2_norm", "final_norm")
ORDER = ("ffn1_norm", "ffn1_w1", "ffn1_w3", "ffn1_w2", "mix_norm", "w_in", "b_gate", "q_norm", "k_norm", "rel_bias",
         "w_branch_a", "w_branch_b", "w_out", "ffn2_norm", "ffn2_w1", "ffn2_w3", "ffn2_w2", "final_norm")
TRANSPOSED = ("ffn1_w1", "ffn1_w3", "ffn2_w1", "ffn2_w3")
GATHER_GROUPS = (("ffn1_w1", "ffn1_w3"), ("ffn1_w2",), ("w_in",), ("w_branch_a", "w_branch_b", "w_out"),
                 ("ffn2_w1", "ffn2_w3", "ffn2_w2"))


def _pack_small(d):
    rows = []
    for n in SMALL:
        flat = d[n].reshape(-1)
        pad = (-flat.shape[0]) % LANES
        rows.append(jnp.pad(flat, (0, pad)).reshape(-1, LANES))
    buf = jnp.concatenate(rows, axis=0)
    return jnp.pad(buf, ((0, (-buf.shape[0]) % 8), (0, 0)))


def _unpack_small(buf, like):
    out, r = {}, 0
    for n in SMALL:
        size = like[n].size
        nr = -(-size // LANES)
        out[n] = buf[r:r + nr].reshape(-1)[:size].reshape(like[n].shape)
        r += nr
    return out


def kernel(x, ffn1_norm, ffn1_w1, ffn1_w3, ffn1_w2, mix_norm, w_in, b_gate, q_norm, k_norm, rel_bias, w_branch_a, w_branch_b, w_out, ffn2_norm, ffn2_w1, ffn2_w3, ffn2_w2, final_norm, loss_target, m_ffn1_norm, m_ffn1_w1, m_ffn1_w3, m_ffn1_w2, m_mix_norm, m_w_in, m_b_gate, m_q_norm, m_k_norm, m_rel_bias, m_w_branch_a, m_w_branch_b, m_w_out, m_ffn2_norm, m_ffn2_w1, m_ffn2_w3, m_ffn2_w2, m_final_norm, v_ffn1_norm, v_ffn1_w1, v_ffn1_w3, v_ffn1_w2, v_mix_norm, v_w_in, v_b_gate, v_q_norm, v_k_norm, v_rel_bias, v_w_branch_a, v_w_branch_b, v_w_out, v_ffn2_norm, v_ffn2_w1, v_ffn2_w3, v_ffn2_w2, v_final_norm):
    given = dict(locals())
    w = {n: given[n] for n in ORDER}
    m = {n: given["m_" + n] for n in ORDER}
    v = {n: given["v_" + n] for n in ORDER}
    T, D = x.shape[1], x.shape[2]

    def stored(a, n):
        a = a.reshape(a.shape[1:])
        return a.T if n in TRANSPOSED else a

    def returned(a, n):
        return (a.T if n in TRANSPOSED else a).reshape(w[n].shape)

    quarter = {n: stored(w[n], n) for n in BIG}
    q16 = [quarter[n].astype(BF16) for n in BIG]
    send, recv, src_thru, land_thru, token = _exchange_start(
        "gather_start", q16, [_own_slot((4,) + q.shape, q, BF16) for q in q16], scatter=False)
    index = {n: i for i, n in enumerate(BIG)}
    ready = {}

    def get_w(name, after):
        if name not in ready:
            group = next(g for g in GATHER_GROUPS if name in g)
            ids = [index[n] for n in group]
            stacks = _exchange_wait("gather_wait_" + group[0], [src_thru[i] for i in ids], [land_thru[i] for i in ids],
                                    [send[i] for i in ids], [recv[i] for i in ids], after, scatter=False)
            stacks = _fill_from_sibling("gather_fill_" + group[0], stacks)
            for n, st in zip(group, stacks):
                ready[n] = st.reshape(D, D) if n in ("w_branch_b", "w_out") else st
        return ready[name]

    me = 2 * lax.axis_index("x") + lax.axis_index("y")
    in_flight = []

    def put_g(grads):
        names = list(grads)
        stacks = [grads[n].reshape((4,) + quarter[n].shape) for n in names]
        lands = [_own_slot(s.shape, lax.dynamic_index_in_dim(s, me, 0, keepdims=False), BF16) for s in stacks]
        started = _exchange_start("scatter_start_" + names[0], stacks, lands, scatter=True)
        in_flight.append((names,) + tuple(started[:4]))
        return [started[4]]

    small = {n: w[n] for n in SMALL}
    loss_cols, grad_x, gs = _local_step(x.reshape(T, D), loss_target.reshape(T, D), small, get_w, put_g, deps=[token])
    loss = lax.psum(jnp.sum(loss_cols), ("x", "y", "c"))

    landed = {}
    for names, s_sem, r_sem, srcs, lands in in_flight:
        got = _exchange_wait("scatter_wait_" + names[0], srcs, lands, s_sem, r_sem, grad_x, scatter=True)
        landed.update(zip(names, got))
    partial = [_sum_four(f"sum4_{n}", landed[n]) for n in BIG]
    other = _swap_with_sibling(partial)
    grads, deltas, new_m, new_v = {}, {}, {}, {}
    for n, mine, theirs in zip(BIG, partial, other):
        res = _adamw_big(f"adamw_{n}", quarter[n], stored(m[n], n), stored(v[n], n), mine, theirs)
        grads[n], deltas[n], new_m[n], new_v[n] = [returned(r, n) for r in res]

    gs = {n: gs[n].reshape(w[n].shape) for n in SMALL}
    g_small = _allreduce_small(_pack_small(gs))
    packed = [_pack_small({n: d[n] for n in SMALL}) for d in (w, m, v)]
    R = g_small.shape[0]
    res = _ew("adamw_small", lambda wv, mv, vv, g: (g,) + _adamw_math(wv, g, mv, vv),
              [_tiled(packed[0]), _tiled(packed[1]), _tiled(packed[2]), _tiled(g_small)], [(F32, LANES)] * 4, n_rows=R, rows=R)
    for d, buf in zip((grads, deltas, new_m, new_v), res):
        d.update(_unpack_small(buf, w))

    return (loss, grad_x.reshape(x.shape), *[grads[n] for n in ORDER], *[deltas[n] for n in ORDER],
            *[new_m[n] for n in ORDER], *[new_v[n] for n in ORDER])
```

```python
import functools
import math

import numpy as np
import jax
import jax.numpy as jnp
from jax import lax
from jax.experimental import pallas as pl
from jax.experimental.pallas import tpu as pltpu

F32 = jnp.float32
BF16 = jnp.bfloat16
MESH = pl.DeviceIdType.MESH

NEG_INF = -1e30
EPS = 1e-6
GRID_W = 64
ROPE_THETA = 10000.0
DILATIONS = (1, 4, 16)
BAND_HALF = 64
HEAD_A = 64
HEADS_A = 8
WIDTH_A = HEADS_A * HEAD_A
HEAD_B = 128
LOG2_E = math.log2(math.e)
QK_SCALE_LOG2 = HEAD_B ** -0.5 * LOG2_E
N_BUCKETS = 32
MAX_DISTANCE = 1024
ADAM_LR, ADAM_B1, ADAM_B2, ADAM_EPS, ADAM_WD, ADAM_STEP = 0.001, 0.9, 0.999, 1e-08, 0.01, 10

A_Q, A_K, A_V = 0, 1536, 3072
B_Q, B_K, B_V = 4608, 5632, 5888
G_A, G_B = 6144, 7168
IN_WIDTH = 8192

VMEM_LIMIT_BYTES = 56 * 1024 * 1024
QB_A = 128
QB_B = 256


def _params(*sem):
    return pltpu.CompilerParams(dimension_semantics=sem, vmem_limit_bytes=VMEM_LIMIT_BYTES)


def _bs(shape, fn):
    return pl.BlockSpec(shape, fn)


def _mm(name, grid, pairs, out_shape, out_spec, dims, *, reduce_axis=None, extras=(), epilogue=None, deps=()):
    n_pairs, n_extra, n_deps = len(pairs), len(extras), len(deps)
    operands = [p[0] for p in pairs] + [p[2] for p in pairs] + [e[0] for e in extras] + list(deps)
    in_specs = [p[1] for p in pairs] + [p[3] for p in pairs] + [e[1] for e in extras] + _any_specs(n_deps)
    tile = tuple(s for s in out_spec.block_shape if s is not None)
    n_steps = grid[reduce_axis] if reduce_axis is not None else 1

    def body(*refs):
        a_refs, b_refs = refs[:n_pairs], refs[n_pairs:2 * n_pairs]
        e_refs = refs[2 * n_pairs:2 * n_pairs + n_extra]
        o_ref = refs[2 * n_pairs + n_extra + n_deps]
        acc = None
        for a_ref, b_ref in zip(a_refs, b_refs):
            t = lax.dot_general(a_ref[...], b_ref[...], (dims, ((), ())), preferred_element_type=F32)
            acc = t if acc is None else acc + t

        def finish(v):
            if epilogue is not None:
                v = epilogue(v, *[e[...] for e in e_refs])
            o_ref[...] = v.astype(o_ref.dtype)

        if reduce_axis is None:
            finish(acc)
        else:
            acc_ref = refs[-1]
            k = pl.program_id(reduce_axis)

            @pl.when(k == 0)
            def _():
                acc_ref[...] = acc

            @pl.when(k > 0)
            def _():
                acc_ref[...] += acc

            @pl.when(k == n_steps - 1)
            def _():
                finish(acc_ref[...])

    sem = ["parallel"] * len(grid)
    if reduce_axis is not None:
        sem[reduce_axis] = "arbitrary"
    return pl.pallas_call(
        body, out_shape=out_shape, grid=grid, in_specs=in_specs, out_specs=out_spec,
        scratch_shapes=[pltpu.VMEM(tile, F32)] if reduce_axis is not None else [],
        compiler_params=_params(*sem), name=name)(*operands)


NN = ((1,), (0,))
NT = ((1,), (1,))
TN = ((0,), (0,))


def _mm_cols(name, a, w, *, tm, tn, out_dtype, cat, extras=(), epilogue=None):
    M, K = a.shape
    J, _, n = w.shape
    tn = min(tn, n)
    nb = n // tn
    if cat:
        shape, spec = (M, J * n), _bs((tm, tn), lambda j, i, k: (i, j * nb + k))
    else:
        shape, spec = (J, M, n), _bs((None, tm, tn), lambda j, i, k: (j, i, k))
    ex = [(e, _bs((tm, tn), lambda j, i, k: (i, j * nb + k))) for e in extras]
    return _mm(name, (J, M // tm, nb),
               [(a, _bs((tm, K), lambda j, i, k: (i, 0)), w, _bs((None, K, tn), lambda j, i, k: (j, 0, k)))],
               jax.ShapeDtypeStruct(shape, out_dtype), spec, NN, extras=ex, epilogue=epilogue)


def _mm_rows_t(name, a, w, *, tm, out_dtype):
    M, N = a.shape
    J, f, _ = w.shape
    return _mm(name, (J, M // tm),
               [(a, _bs((tm, N), lambda j, i: (i, 0)), w, _bs((None, f, N), lambda j, i: (j, 0, 0)))],
               jax.ShapeDtypeStruct((J, M, f), out_dtype), _bs((None, tm, f), lambda j, i: (j, i, 0)), NT)


def _mm_wgrad(name, a, b, *, a_cols, b_cols, tm, tn, J):
    def pick(arr, cols, t):
        if arr.ndim == 3:
            T, c = arr.shape[1], arr.shape[2]
            t = min(t, c)
            return T, c, t, (lambda sel: _bs((None, T, t), lambda j, i, k: (j, 0, sel(i, k))))
        T = arr.shape[0]
        c = arr.shape[1] if cols is None else cols
        t = min(t, c)
        per = c // t
        if cols is None:
            return T, c, t, (lambda sel: _bs((T, t), lambda j, i, k: (0, sel(i, k))))
        return T, c, t, (lambda sel: _bs((T, t), lambda j, i, k: (0, j * per + sel(i, k))))
    _, ca, tm, mk_a = pick(a, a_cols, tm)
    _, cb, tn, mk_b = pick(b, b_cols, tn)
    return _mm(name, (J, ca // tm, cb // tn),
               [(a, mk_a(lambda i, k: i), b, mk_b(lambda i, k: k))],
               jax.ShapeDtypeStruct((J, ca, cb), BF16), _bs((None, tm, tn), lambda j, i, k: (j, i, k)), TN)


def _tiled(arr, width=None, col=0, rowblk=0):
    return ("t", arr, arr.shape[1] if width is None else width, col, rowblk)


def _table(arr):
    return ("f", arr)


def _whole(arr):
    return ("w", arr)


def _ew(name, fn, ins, outs, *, n_rows, rows, reds=(), ncols=1, deps=()):
    nrb = n_rows // rows
    n_deps = len(deps)
    operands, in_specs = [], []
    for spec in ins:
        if spec[0] == "t":
            _, arr, width, col, rowblk = spec
            step = 1 if ncols > 1 else 0
            in_specs.append(_bs((rows, width), lambda c, i, col=col, rowblk=rowblk, step=step: (rowblk + i, col + c * step)))
        elif spec[0] == "f":
            arr = spec[1]
            in_specs.append(_bs((rows, arr.shape[1]), lambda c, i: (i, 0)))
        else:
            arr = spec[1]
            nd = arr.ndim
            if nd == 3:
                in_specs.append(_bs((None,) + arr.shape[1:], lambda c, i: (c, 0, 0)))
            else:
                in_specs.append(_bs(arr.shape, lambda c, i, nd=nd: (0,) * nd))
        operands.append(arr)
    out_shapes = [jax.ShapeDtypeStruct((n_rows, ncols * w), dt) for dt, w in outs]
    out_specs = [_bs((rows, w), lambda c, i: (i, c)) for _, w in outs]
    out_shapes += [jax.ShapeDtypeStruct((ncols, 1, w), F32) for w in reds]
    out_specs += [_bs((None, 1, w), lambda c, i: (c, 0, 0)) for w in reds]
    n_in, n_out, n_red = len(ins), len(outs), len(reds)
    operands += list(deps)
    in_specs += _any_specs(n_deps)

    def body(*refs):
        vals = fn(*[r[...] for r in refs[:n_in]])
        if not isinstance(vals, (tuple, list)):
            vals = (vals,)
        o_refs = refs[n_in + n_deps:]
        for o_ref, v in zip(o_refs[:n_out], vals[:n_out]):
            o_ref[...] = v.astype(o_ref.dtype)
        if n_red:
            i = pl.program_id(1)
            for r_ref, v in zip(o_refs[n_out:], vals[n_out:]):
                @pl.when(i == 0)
                def _(r_ref=r_ref):
                    r_ref[...] = jnp.zeros_like(r_ref)
                r_ref[...] += v

    res = pl.pallas_call(
        body, out_shape=out_shapes, grid=(ncols, nrb), in_specs=in_specs, out_specs=out_specs,
        compiler_params=_params("parallel", "arbitrary" if n_red else "parallel"), name=name)(*operands)
    return res


def _colsum(v):
    return jnp.sum(v, axis=0, keepdims=True)


def _rstd(x):
    return lax.rsqrt(jnp.mean(x * x, axis=-1, keepdims=True) + EPS)


def _sigmoid(x):
    return 1.0 / (1.0 + jnp.exp(-x))


def _norm_fwd(x, g):
    return x * _rstd(x) * g


def _norm_bwd(x, g, dy):
    r = _rstd(x)
    xh = x * r
    dxh = dy * g
    dx = r * (dxh - xh * jnp.mean(dxh * xh, axis=-1, keepdims=True))
    return dx, dy * xh


def _ffn_fwd(tag, x, gain, get_w, deps=()):
    T, D = x.shape
    (h,) = _ew(f"{tag}_norm", lambda xv, g: _norm_fwd(xv, g), [_tiled(x), _whole(gain)], [(BF16, D)], n_rows=T, rows=512,
               deps=deps)
    w1, w3 = get_w(f"{tag}_w1", h), get_w(f"{tag}_w3", h)
    J, f, _ = w1.shape
    tm = 1024

    def up(h_ref, w1_ref, w3_ref, u_ref, g_ref, a_ref):
        hv = h_ref[...]
        u = lax.dot_general(hv, w1_ref[...], (NT, ((), ())), preferred_element_type=F32)
        g = lax.dot_general(hv, w3_ref[...], (NT, ((), ())), preferred_element_type=F32)
        u_ref[...] = u.astype(BF16)
        g_ref[...] = g.astype(BF16)
        a_ref[...] = (u * _sigmoid(u) * g).astype(BF16)

    slab = _bs((None, tm, f), lambda j, i: (j, i, 0))
    w_spec = _bs((None, f, D), lambda j, i: (j, 0, 0))
    u, g, a = pl.pallas_call(
        up, out_shape=[jax.ShapeDtypeStruct((J, T, f), BF16)] * 3, grid=(J, T // tm),
        in_specs=[_bs((tm, D), lambda j, i: (i, 0)), w_spec, w_spec], out_specs=[slab] * 3,
        compiler_params=_params("parallel", "parallel"), name=f"{tag}_up")(h, w1, w3)
    w2 = get_w(f"{tag}_w2", a)
    y = _mm(f"{tag}_down", (T // 1024, D // 512),
            [(a, _bs((None, 1024, f), lambda i, k, j=j: (j, i, 0)), w2, _bs((None, f, 512), lambda i, k, j=j: (j, 0, k)))
             for j in range(J)],
            jax.ShapeDtypeStruct((T, D), F32), _bs((1024, 512), lambda i, k: (i, k)), NN,
            extras=[(x, _bs((1024, 512), lambda i, k: (i, k)))], epilogue=lambda acc, xv: xv + 0.5 * acc)
    return y, (h, u, g, a)


def _ffn_bwd(tag, x, gain, get_w, put_g, saved, dy, dy_half):
    h, u, g, a = saved
    T, D = x.shape
    w1, w3, w2 = [get_w(f"{tag}_{n}", dy_half) for n in ("w1", "w3", "w2")]
    J, f, _ = w1.shape
    dw2 = _mm_wgrad(f"{tag}_bwd_dw2", a, dy_half, a_cols=None, b_cols=None, tm=f, tn=512, J=J)
    deps = put_g({f"{tag}_w2": dw2})
    tm = 1024

    def up_bwd(dy_ref, w2_ref, u_ref, g_ref, *rest):
        du_ref, dg_ref = rest[-2:]
        da = lax.dot_general(dy_ref[...], w2_ref[...], (NT, ((), ())), preferred_element_type=F32)
        uv, gv = u_ref[...].astype(F32), g_ref[...].astype(F32)
        s = _sigmoid(uv)
        du_ref[...] = (da * gv * (s * (1.0 + uv * (1.0 - s)))).astype(BF16)
        dg_ref[...] = (da * (uv * s)).astype(BF16)

    slab = _bs((None, tm, f), lambda j, i: (j, i, 0))
    du, dg = pl.pallas_call(
        up_bwd, out_shape=[jax.ShapeDtypeStruct((J, T, f), BF16)] * 2, grid=(J, T // tm),
        in_specs=[_bs((tm, D), lambda j, i: (i, 0)), _bs((None, f, D), lambda j, i: (j, 0, 0)), slab, slab] + _any_specs(len(deps)),
        out_specs=[slab] * 2, compiler_params=_params("parallel", "parallel"), name=f"{tag}_bwd_up")(dy_half, w2, u, g, *deps)
    dw1 = _mm_wgrad(f"{tag}_bwd_dw1", du, h, a_cols=None, b_cols=None, tm=f, tn=512, J=J)
    dw3 = _mm_wgrad(f"{tag}_bwd_dw3", dg, h, a_cols=None, b_cols=None, tm=f, tn=512, J=J)
    deps = deps + put_g({f"{tag}_w1": dw1, f"{tag}_w3": dw3})
    pairs = []
    for j in range(J):
        a_spec = _bs((None, 512, f), lambda i, k, j=j: (j, i, 0))
        w_spec = _bs((None, f, 512), lambda i, k, j=j: (j, 0, k))
        pairs += [(du, a_spec, w1, w_spec), (dg, a_spec, w3, w_spec)]
    dh = _mm(f"{tag}_bwd_dh", (T // 512, D // 512), pairs,
             jax.ShapeDtypeStruct((T, D), F32), _bs((512, 512), lambda i, k: (i, k)), NN, deps=deps)

    def nb(xv, gv, dhv, dres):
        dx, dgr = _norm_bwd(xv, gv, dhv)
        dx = dx + dres
        return dx, 0.5 * dx, _colsum(dgr)

    dx, dx_half, dgain = _ew(f"{tag}_bwd_norm", nb, [_tiled(x), _whole(gain), _tiled(dh), _tiled(dy)],
                             [(F32, D), (BF16, D)], n_rows=T, rows=256, reds=(D,))
    return dx, dx_half, dgain.reshape(1, D)


def _t5_bucket(rel):
    n = N_BUCKETS // 2
    max_exact = n // 2
    ret = jnp.where(rel > 0, n, 0)
    a = jnp.abs(rel)
    af = jnp.maximum(a, 1).astype(F32)
    large = max_exact + (jnp.log(af / max_exact) / math.log(MAX_DISTANCE / max_exact) * (n - max_exact)).astype(jnp.int32)
    large = jnp.minimum(large, n - 1)
    return ret + jnp.where(a < max_exact, a, large)


def _band_steps():
    qi = jnp.arange(QB_A, dtype=jnp.int32)[:, None]
    kj = jnp.arange(3 * QB_A, dtype=jnp.int32)[None, :] - QB_A
    return kj - qi


def _bias_tiles(rel_bias):
    steps = _band_steps()
    buckets = jnp.stack([_t5_bucket(steps * d) for d in DILATIONS])
    inband = (jnp.abs(steps) <= BAND_HALF).astype(jnp.int32)
    n_heads = rel_bias.shape[1]

    def body(tab_ref, b_ref, m_ref, o_ref):
        hd = pl.program_id(0)
        bkt = b_ref[...]
        acc = jnp.zeros(bkt.shape, F32)
        for b in range(N_BUCKETS):
            acc = jnp.where(bkt == b, tab_ref[b, hd], acc)
        o_ref[...] = jnp.where(m_ref[...] > 0, acc, NEG_INF)

    return pl.pallas_call(
        body, out_shape=jax.ShapeDtypeStruct((n_heads, QB_A, 3 * QB_A), F32), grid=(n_heads,),
        in_specs=[pl.BlockSpec(memory_space=pltpu.SMEM),
                  _bs((None, QB_A, 3 * QB_A), lambda hd: (hd // HEADS_A, 0, 0)),
                  _bs((QB_A, 3 * QB_A), lambda hd: (0, 0))],
        out_specs=_bs((None, QB_A, 3 * QB_A), lambda hd: (hd, 0, 0)),
        compiler_params=_params("parallel"), name="a_bias_tiles")(rel_bias, buckets, inband)


def _bias_grad(dbias):
    steps = np.arange(3 * QB_A)[None, :] - QB_A - np.arange(QB_A)[:, None]
    inband = np.abs(steps) <= BAND_HALF
    present = []
    for d in DILATIONS:
        rel = steps * d
        a = np.abs(rel)
        large = 8 + (np.log(np.maximum(a, 1) / 8.0) / math.log(MAX_DISTANCE / 8.0) * 8).astype(np.int64)
        bk = np.where(rel > 0, 16, 0) + np.where(a < 8, a, np.minimum(large, 15))
        present.append(sorted(set(bk[inband].tolist())))
    buckets = jnp.stack([_t5_bucket(_band_steps() * d) for d in DILATIONS])
    n_heads = dbias.shape[0]

    def body(b_ref, d_ref, o_ref):
        row = lax.broadcasted_iota(jnp.int32, (N_BUCKETS, n_heads), 0)
        col = lax.broadcasted_iota(jnp.int32, (N_BUCKETS, n_heads), 1)
        out = jnp.zeros((N_BUCKETS, n_heads), F32)
        for grp in range(len(DILATIONS)):
            bkt = b_ref[grp]
            for hh in range(HEADS_A):
                hd = grp * HEADS_A + hh
                ds = d_ref[hd]
                for b in present[grp]:
                    tot = jnp.sum(jnp.where(bkt == b, ds, 0.0))
                    out = jnp.where((row == b) & (col == hd), tot, out)
        o_ref[...] = out

    return pl.pallas_call(
        body, out_shape=jax.ShapeDtypeStruct((N_BUCKETS, n_heads), F32),
        compiler_params=pltpu.CompilerParams(vmem_limit_bytes=VMEM_LIMIT_BYTES), name="a_bias_grad")(buckets, dbias)


def _lane_is_second_head(shape):
    return lax.broadcasted_iota(jnp.int32, shape, len(shape) - 1) >= HEAD_A


def _group_view(proj, grp, d):
    T = proj.shape[0]
    if d == 1:
        return proj, IN_WIDTH, grp * 3 * WIDTH_A
    part = proj[:, grp * 3 * WIDTH_A:(grp + 1) * 3 * WIDTH_A]
    return part.reshape(T // d, d * 3 * WIDTH_A), 3 * WIDTH_A, 0


def _stack_heads(v2, second):
    zero = jnp.zeros_like(v2)
    return jnp.concatenate([jnp.where(second, zero, v2), jnp.where(second, v2, zero)], axis=0)


def _unstack_heads(v, second):
    return jnp.where(second, v[QB_A:], v[:QB_A])


def _edge_mask(n, nblk):
    neg_prev = jnp.where(n > 0, 0.0, NEG_INF)
    neg_next = jnp.where(n < nblk - 1, 0.0, NEG_INF)
    return jnp.concatenate([jnp.full((1, QB_A), neg_prev, F32), jnp.zeros((1, QB_A), F32),
                            jnp.full((1, QB_A), neg_next, F32)], axis=1)


def _dil_fwd(proj, bias, grp, d):
    T = proj.shape[0]
    L = T // d
    nblk = L // QB_A
    pv, width, base = _group_view(proj, grp, d)
    cb, b0 = width // WIDTH_A, base // WIDTH_A
    W2 = 2 * HEAD_A
    scale = HEAD_A ** -0.5

    def body(q_ref, kp_ref, kc_ref, kn_ref, vp_ref, vc_ref, vn_ref, b_ref, o_ref, l_ref):
        edge = _edge_mask(pl.program_id(1), nblk)
        second = _lane_is_second_head((QB_A, W2))
        for hp in range(HEADS_A // 2):
            cols = slice(hp * W2, (hp + 1) * W2)
            kcat = jnp.concatenate([kp_ref[:, cols], kc_ref[:, cols], kn_ref[:, cols]], axis=0)
            vcat = jnp.concatenate([vp_ref[:, cols], vc_ref[:, cols], vn_ref[:, cols]], axis=0)
            qs = _stack_heads(q_ref[:, cols], second)
            s = lax.dot_general(qs, kcat, (NT, ((), ())), preferred_element_type=F32)
            s = s * scale + b_ref[2 * hp:2 * hp + 2].reshape(2 * QB_A, 3 * QB_A) + edge
            m = jnp.max(s, axis=-1, keepdims=True)
            p = jnp.exp(s - m)
            l = jnp.sum(p, axis=-1, keepdims=True)
            res = jnp.dot(p.astype(BF16), vcat, preferred_element_type=F32) / l
            o_ref[:, cols] = _unstack_heads(res, second).astype(o_ref.dtype)
            l_ref[:, cols] = _unstack_heads(jnp.broadcast_to(m + jnp.log(l), (2 * QB_A, W2)), second)

    def spec(part, dn):
        return _bs((QB_A, WIDTH_A), lambda r, n: (jnp.clip(n + dn, 0, nblk - 1), r * cb + b0 + part))

    in_specs = [spec(0, 0)] + [spec(1, dn) for dn in (-1, 0, 1)] + [spec(2, dn) for dn in (-1, 0, 1)]
    in_specs += [_bs((HEADS_A, QB_A, 3 * QB_A), lambda r, n: (0, 0, 0))]
    o, lse = pl.pallas_call(
        body, out_shape=[jax.ShapeDtypeStruct((L, d * WIDTH_A), BF16), jax.ShapeDtypeStruct((L, d * WIDTH_A), F32)],
        grid=(d, nblk), in_specs=in_specs,
        out_specs=[_bs((QB_A, WIDTH_A), lambda r, n: (n, r)), _bs((QB_A, WIDTH_A), lambda r, n: (n, r))],
        compiler_params=_params("parallel", "parallel"), name=f"a_fwd_d{d}")(pv, pv, pv, pv, pv, pv, pv, bias)
    return o.reshape(T, WIDTH_A), lse.reshape(T, WIDTH_A)


def _dil_bwd(proj, bias, do, lse, cterm, grp, d):
    T = proj.shape[0]
    L = T // d
    nblk = L // QB_A
    W2 = 2 * HEAD_A
    PPS = 2
    WS = PPS * W2
    pv, width, base = _group_view(proj, grp, d)
    cb, b0 = width // WS, base // WS
    ob = WIDTH_A // WS
    view = lambda a: a.reshape(L, d * WIDTH_A)
    scale = HEAD_A ** -0.5

    def body(q_ref, kp_ref, kc_ref, kn_ref, vp_ref, vc_ref, vn_ref, do_ref, l_ref, c_ref, b_ref,
             dq_ref, dk_ref, dv_ref, db_ref):
        r, n = pl.program_id(1), pl.program_id(2)

        @pl.when(n == 0)
        def _():
            dk_ref[...] = jnp.zeros_like(dk_ref)
            dv_ref[...] = jnp.zeros_like(dv_ref)

        @pl.when((n == 0) & (r == 0))
        def _():
            db_ref[...] = jnp.zeros_like(db_ref)

        second = _lane_is_second_head((QB_A, W2))
        edge = _edge_mask(n, nblk)
        starts = [pl.multiple_of(jnp.clip(n + dn, 0, nblk - 1) * QB_A, QB_A) for dn in (-1, 0, 1)]
        for pp in range(PPS):
            cols = slice(pp * W2, (pp + 1) * W2)
            kcat = jnp.concatenate([kp_ref[:, cols], kc_ref[:, cols], kn_ref[:, cols]], axis=0)
            vcat = jnp.concatenate([vp_ref[:, cols], vc_ref[:, cols], vn_ref[:, cols]], axis=0)
            qs, dos = _stack_heads(q_ref[:, cols], second), _stack_heads(do_ref[:, cols], second)
            lse2, c2 = l_ref[:, cols], c_ref[:, cols]
            lse_rows = jnp.concatenate([lse2[:, 0:1], lse2[:, HEAD_A:HEAD_A + 1]], axis=0)
            c_rows = jnp.concatenate([c2[:, 0:1], c2[:, HEAD_A:HEAD_A + 1]], axis=0)
            s = lax.dot_general(qs, kcat, (NT, ((), ())), preferred_element_type=F32)
            p = jnp.exp(s * scale + b_ref[2 * pp:2 * pp + 2].reshape(2 * QB_A, 3 * QB_A) + edge - lse_rows)
            dp = lax.dot_general(dos, vcat, (NT, ((), ())), preferred_element_type=F32)
            ds = p * (dp + c_rows)
            db_ref[2 * pp:2 * pp + 2] += ds.reshape(2, QB_A, 3 * QB_A)
            pb, dsb = p.astype(BF16), (ds * scale).astype(BF16)
            dq_ref[:, cols] = _unstack_heads(jnp.dot(dsb, kcat, preferred_element_type=F32), second).astype(dq_ref.dtype)
            dkc = lax.dot_general(dsb, qs, (TN, ((), ())), preferred_element_type=F32)
            dvc = lax.dot_general(pb, dos, (TN, ((), ())), preferred_element_type=F32)
            for b, start in enumerate(starts):
                dk_ref[pl.ds(start, QB_A), cols] += dkc[b * QB_A:(b + 1) * QB_A]
                dv_ref[pl.ds(start, QB_A), cols] += dvc[b * QB_A:(b + 1) * QB_A]

    def spec(part, dn):
        return _bs((QB_A, WS), lambda hp, r, n: (jnp.clip(n + dn, 0, nblk - 1), r * cb + b0 + part * ob + hp))

    in_specs = [spec(0, 0)] + [spec(1, dn) for dn in (-1, 0, 1)] + [spec(2, dn) for dn in (-1, 0, 1)]
    in_specs += [_bs((QB_A, WS), lambda hp, r, n: (n, r * ob + hp))] * 3
    in_specs += [_bs((2 * PPS, QB_A, 3 * QB_A), lambda hp, r, n: (hp, 0, 0))]
    out_shape = [jax.ShapeDtypeStruct((L, d * WIDTH_A), BF16), jax.ShapeDtypeStruct((L, d * WIDTH_A), F32),
                 jax.ShapeDtypeStruct((L, d * WIDTH_A), F32), jax.ShapeDtypeStruct((HEADS_A, QB_A, 3 * QB_A), F32)]
    out_specs = [_bs((QB_A, WS), lambda hp, r, n: (n, r * ob + hp)),
                 _bs((L, WS), lambda hp, r, n: (0, r * ob + hp)), _bs((L, WS), lambda hp, r, n: (0, r * ob + hp)),
                 _bs((2 * PPS, QB_A, 3 * QB_A), lambda hp, r, n: (hp, 0, 0))]
    dq, dk, dv, db = pl.pallas_call(
        body, out_shape=out_shape, grid=(ob, d, nblk), in_specs=in_specs, out_specs=out_specs,
        compiler_params=_params("arbitrary", "arbitrary", "arbitrary"), name=f"a_bwd_d{d}")(
            pv, pv, pv, pv, pv, pv, pv, view(do), view(lse), view(cterm), bias)
    return dq.reshape(T, WIDTH_A), dk.reshape(T, WIDTH_A), dv.reshape(T, WIDTH_A), db


def _segment_ones():
    i = np.arange(WIDTH_A)
    return jnp.asarray((i[:, None] // HEAD_A == i[None, :] // HEAD_A).astype(np.float32), dtype=BF16)


def _group_weights(l0, l1, l2):
    m = jnp.maximum(jnp.maximum(l0, l1), l2)
    e = [jnp.exp(l - m) for l in (l0, l1, l2)]
    z = e[0] + e[1] + e[2]
    return [ei / z for ei in e]


def _combine_fwd(outs, lses):
    T = outs[0].shape[0]

    def fn(o0, o1, o2, l0, l1, l2):
        w = _group_weights(l0, l1, l2)
        return w[0] * o0.astype(F32) + w[1] * o1.astype(F32) + w[2] * o2.astype(F32)

    (oa,) = _ew("a_combine", fn, [_tiled(o) for o in outs] + [_tiled(l) for l in lses], [(BF16, WIDTH_A)], n_rows=T, rows=512)
    return oa


def _combine_bwd(doa, outs, lses):
    T = doa.shape[0]

    def fn(d, o0, o1, o2, l0, l1, l2, seg):
        d = d.astype(F32)
        w = _group_weights(l0, l1, l2)
        tot = jnp.zeros(d.shape, F32)
        for wg, og in zip(w, (o0, o1, o2)):
            prod = wg * d * og.astype(F32)
            hi = prod.astype(BF16)
            lo = (prod - hi.astype(F32)).astype(BF16)
            tot = tot + jnp.dot(hi, seg, preferred_element_type=F32) + jnp.dot(lo, seg, preferred_element_type=F32)
        return tuple(wg * d for wg in w) + tuple(-wg * tot for wg in w)

    res = _ew("a_combine_bwd", fn, [_tiled(doa)] + [_tiled(o) for o in outs] + [_tiled(l) for l in lses] + [_whole(_segment_ones())],
              [(BF16, WIDTH_A)] * 3 + [(F32, WIDTH_A)] * 3, n_rows=T, rows=256)
    return res[:3], res[3:]


def _rope_tables(T):
    rows = T // GRID_W
    row = jnp.repeat(jnp.arange(rows, dtype=F32), GRID_W)
    col = jnp.tile(jnp.arange(GRID_W, dtype=F32), rows)
    n_freq = HEAD_B // 4
    freq = ROPE_THETA ** (-jnp.arange(n_freq, dtype=F32) / n_freq)
    ang = jnp.concatenate([row[:, None] * freq, col[:, None] * freq], axis=-1)
    cos, sin = jnp.repeat(jnp.cos(ang), 2, axis=1), jnp.repeat(jnp.sin(ang), 2, axis=1)
    sign = jnp.where(jnp.arange(HEAD_B) % 2 == 0, -1.0, 1.0).astype(F32)
    return cos, sin * sign


def _swap_pairs(v):
    even = lax.broadcasted_iota(jnp.int32, v.shape, v.ndim - 1) % 2 == 0
    n = v.shape[-1]
    return jnp.where(even, pltpu.roll(v, n - 1, v.ndim - 1), pltpu.roll(v, 1, v.ndim - 1))


def _qk_fwd(name, proj, col0, n_heads, gain, cos, sin, out_scale=1.0):
    T = proj.shape[0]

    def fn(xr, g, c, s):
        xn = _norm_fwd(xr.astype(F32), g)
        return (xn * c + _swap_pairs(xn) * s) * out_scale

    (out,) = _ew(name, fn, [_tiled(proj, HEAD_B, col0 // HEAD_B), _whole(gain), _table(cos), _table(sin)],
                 [(BF16, HEAD_B)], n_rows=T, rows=2048, ncols=n_heads)
    return out


def _qk_bwd(name, dout, proj, col0, n_heads, gain, cos, sin, in_scale=1.0):
    T = proj.shape[0]

    def fn(dv, xr, g, c, s):
        dv = dv.astype(F32) * in_scale
        dxn = c * dv + _swap_pairs(s * dv)
        dx, dgr = _norm_bwd(xr.astype(F32), g, dxn)
        return dx, _colsum(dgr)

    dx, dg = _ew(name, fn, [_tiled(dout, HEAD_B, 0), _tiled(proj, HEAD_B, col0 // HEAD_B), _whole(gain),
                            _table(cos), _table(sin)],
                 [(BF16, HEAD_B)], n_rows=T, rows=2048, reds=(HEAD_B,), ncols=n_heads)
    return dx, jnp.sum(dg, axis=0)


def _gqa_fwd(qn, kn, proj):
    T = qn.shape[0]
    GW = 4 * HEAD_B

    def body(q_ref, k_ref, v_ref, o_ref, l_ref):
        k, v = k_ref[...], v_ref[...]
        lane = lax.broadcasted_iota(jnp.int32, (QB_B, HEAD_B), 1)
        lse_all = jnp.zeros((QB_B, HEAD_B), F32)
        for g in range(4):
            cols = slice(g * HEAD_B, (g + 1) * HEAD_B)
            s = lax.dot_general(q_ref[:, cols], k, (NT, ((), ())), preferred_element_type=F32)
            m = jnp.max(s, axis=-1, keepdims=True)
            p = jnp.exp2(s - m)
            l = jnp.sum(p, axis=-1, keepdims=True)
            o = jnp.dot(p.astype(BF16), v, preferred_element_type=F32) / l
            o_ref[:, cols] = o.astype(o_ref.dtype)
            lse_all = jnp.where(lane == g, m + jnp.log2(l), lse_all)
        l_ref[...] = lse_all

    return pl.pallas_call(
        body, out_shape=[jax.ShapeDtypeStruct((T, 2 * GW), BF16), jax.ShapeDtypeStruct((2, T, HEAD_B), F32)],
        grid=(2, T // QB_B),
        in_specs=[_bs((QB_B, GW), lambda kv, i: (i, kv)), _bs((T, HEAD_B), lambda kv, i: (0, kv)),
                  _bs((T, HEAD_B), lambda kv, i: (0, B_V // HEAD_B + kv))],
        out_specs=[_bs((QB_B, GW), lambda kv, i: (i, kv)), _bs((None, QB_B, HEAD_B), lambda kv, i: (kv, i, 0))],
        compiler_params=_params("parallel", "parallel"), name="b_fwd")(qn, kn, proj)


def _gqa_bwd(qn, kn, proj, o, lse, do):
    T = qn.shape[0]
    GW = 4 * HEAD_B

    def body(q_ref, k_ref, v_ref, o_ref, l_ref, do_ref, dq_ref, dk_ref, dv_ref):
        i = pl.program_id(1)

        @pl.when(i == 0)
        def _():
            dk_ref[...] = jnp.zeros_like(dk_ref)
            dv_ref[...] = jnp.zeros_like(dv_ref)

        k, v = k_ref[...], v_ref[...]
        lse_all = l_ref[...]
        for g in range(4):
            cols = slice(g * HEAD_B, (g + 1) * HEAD_B)
            q, dob = q_ref[:, cols], do_ref[:, cols]
            delta = jnp.sum(dob.astype(F32) * o_ref[:, cols].astype(F32), axis=-1, keepdims=True)
            s = lax.dot_general(q, k, (NT, ((), ())), preferred_element_type=F32)
            p = jnp.exp2(s - lse_all[:, g:g + 1])
            dp = lax.dot_general(dob, v, (NT, ((), ())), preferred_element_type=F32)
            ds = (p * (dp - delta)).astype(BF16)
            dq_ref[:, cols] = jnp.dot(ds, k, preferred_element_type=F32).astype(dq_ref.dtype)
            dk_ref[...] += lax.dot_general(ds, q, (TN, ((), ())), preferred_element_type=F32)
            dv_ref[...] += lax.dot_general(p.astype(BF16), dob, (TN, ((), ())), preferred_element_type=F32)

    return pl.pallas_call(
        body, out_shape=[jax.ShapeDtypeStruct((T, 2 * GW), BF16), jax.ShapeDtypeStruct((T, 2 * HEAD_B), F32),
                         jax.ShapeDtypeStruct((T, 2 * HEAD_B), F32)],
        grid=(2, T // QB_B),
        in_specs=[_bs((QB_B, GW), lambda kv, i: (i, kv)), _bs((T, HEAD_B), lambda kv, i: (0, kv)),
                  _bs((T, HEAD_B), lambda kv, i: (0, B_V // HEAD_B + kv)), _bs((QB_B, GW), lambda kv, i: (i, kv)),
                  _bs((None, QB_B, HEAD_B), lambda kv, i: (kv, i, 0)), _bs((QB_B, GW), lambda kv, i: (i, kv))],
        out_specs=[_bs((QB_B, GW), lambda kv, i: (i, kv)), _bs((T, HEAD_B), lambda kv, i: (0, kv)),
                   _bs((T, HEAD_B), lambda kv, i: (0, kv))],
        compiler_params=_params("parallel", "arbitrary"), name="b_bwd")(qn, kn, proj, o, lse, do)


def _local_step(x, target, small, get_w, put_g, deps=()):
    T, D = x.shape
    gs = {}

    x1, ffn1_saved = _ffn_fwd("ffn1", x, small["ffn1_norm"], get_w, deps)
    (h2,) = _ew("mix_norm", lambda xv, g: _norm_fwd(xv, g), [_tiled(x1), _whole(small["mix_norm"])], [(BF16, D)], n_rows=T, rows=512)
    w_in = get_w("w_in", h2)
    nq = w_in.shape[2]
    tpq = nq // WIDTH_A

    def proj_tile(j, k):
        c = j * tpq + k
        return jnp.where(c < 3 * len(DILATIONS), (c % 3) * 3 + c // 3, c)

    proj = _mm("mix_in", (4, T // 1024, tpq),
               [(h2, _bs((1024, D), lambda j, i, k: (i, 0)), w_in, _bs((None, D, WIDTH_A), lambda j, i, k: (j, 0, k)))],
               jax.ShapeDtypeStruct((T, IN_WIDTH), BF16), _bs((1024, WIDTH_A), lambda j, i, k: (i, proj_tile(j, k))), NN)

    bias = _bias_tiles(small["rel_bias"])
    a_outs, a_lses = [], []
    for grp, d in enumerate(DILATIONS):
        o, l = _dil_fwd(proj, bias[grp * HEADS_A:(grp + 1) * HEADS_A], grp, d)
        a_outs.append(o)
        a_lses.append(l)
    o_a = _combine_fwd(a_outs, a_lses)

    cos, sin = _rope_tables(T)
    qn = _qk_fwd("b_qnorm", proj, B_Q, 8, small["q_norm"], cos, sin, out_scale=QK_SCALE_LOG2)
    kn = _qk_fwd("b_knorm", proj, B_K, 2, small["k_norm"], cos, sin)
    o_b, lse_b = _gqa_fwd(qn, kn, proj)

    wa, wb3, w_out3 = get_w("w_branch_a", o_b), get_w("w_branch_b", o_b).reshape(1, D, D), get_w("w_out", o_b).reshape(1, D, D)
    t_a = _mm_cols("mix_branch_a", o_a, wa, tm=512, tn=256, out_dtype=BF16, cat=True)
    t_b = _mm_cols("mix_branch_b", o_b, wb3, tm=512, tn=512, out_dtype=BF16, cat=True)
    bg_a, bg_b = small["b_gate"][:, :D], small["b_gate"][:, D:]

    def merge(ta, tb, ga, gb_, ba, bb):
        sa, sb = _sigmoid(ga.astype(F32) + ba), _sigmoid(gb_.astype(F32) + bb)
        return sa * ta.astype(F32) + sb * tb.astype(F32)

    gate_ins = [_tiled(proj, D, G_A // D), _tiled(proj, D, G_B // D), _whole(bg_a), _whole(bg_b)]
    (merged,) = _ew("mix_merge", merge, [_tiled(t_a), _tiled(t_b)] + gate_ins, [(BF16, D)], n_rows=T, rows=512)
    x2 = _mm_cols("mix_out", merged, w_out3, tm=512, tn=512, out_dtype=F32, cat=True,
                  extras=[x1], epilogue=lambda acc, xv: xv + acc)
    x3, ffn2_saved = _ffn_fwd("ffn2", x2, small["ffn2_norm"], get_w)

    def head(xv, g, tv):
        r = _rstd(xv)
        xh = xv * r
        e = xh * g - tv
        dy = e * (1.0 / D)
        dxh = dy * g
        dx = r * (dxh - xh * jnp.mean(dxh * xh, axis=-1, keepdims=True))
        return dx, 0.5 * dx, _colsum(e * e) * (0.5 / D), _colsum(dy * xh)

    dx3, dx3_half, loss_cols, g_final = _ew("loss_head", head, [_tiled(x3), _whole(small["final_norm"].reshape(1, D)), _tiled(target)],
                                            [(F32, D), (BF16, D)], n_rows=T, rows=256, reds=(D, D))
    gs["final_norm"] = g_final.reshape(D)

    dx2, _, gs["ffn2_norm"] = _ffn_bwd("ffn2", x2, small["ffn2_norm"], get_w, put_g, ffn2_saved, dx3, dx3_half)

    (dmix,) = _ew("mix_bwd_cast", lambda v: v, [_tiled(dx2)], [(BF16, D)], n_rows=T, rows=512)
    g_out = _mm_wgrad("mix_bwd_dwout", merged, dmix, a_cols=D // 4, b_cols=None, tm=256, tn=512, J=4).reshape(D, D)
    dmerged = _mm_rows_t("mix_bwd_dmerged", dmix, w_out3, tm=512, out_dtype=BF16).reshape(T, D)

    def merge_bwd(dm, ta, tb, ga, gb_, ba, bb):
        dm, ta, tb = dm.astype(F32), ta.astype(F32), tb.astype(F32)
        sa, sb = _sigmoid(ga.astype(F32) + ba), _sigmoid(gb_.astype(F32) + bb)
        dga, dgb = dm * ta * sa * (1.0 - sa), dm * tb * sb * (1.0 - sb)
        return dm * sa, dm * sb, dga, dgb, _colsum(dga), _colsum(dgb)

    dta, dtb, dga, dgb, dba, dbb = _ew("mix_bwd_merge", merge_bwd, [_tiled(dmerged), _tiled(t_a), _tiled(t_b)] + gate_ins,
                                       [(BF16, D)] * 4, n_rows=T, rows=256, reds=(D, D))
    gs["b_gate"] = jnp.concatenate([dba.reshape(1, D), dbb.reshape(1, D)], axis=1)

    g_a = _mm_wgrad("mix_bwd_dwa", o_a, dta, a_cols=None, b_cols=D // 4, tm=WIDTH_A, tn=256, J=4)
    g_b = _mm_wgrad("mix_bwd_dwb", o_b, dtb, a_cols=D // 4, b_cols=None, tm=256, tn=512, J=4).reshape(D, D)
    deps = put_g({"w_out": g_out, "w_branch_a": g_a, "w_branch_b": g_b})
    do_a = _mm("mix_bwd_doa", (T // 1024,),
               [(dta, _bs((1024, D // 4), lambda i, j=j: (i, j)), wa, _bs((None, WIDTH_A, D // 4), lambda i, j=j: (j, 0, 0)))
                for j in range(4)],
               jax.ShapeDtypeStruct((T, WIDTH_A), BF16), _bs((1024, WIDTH_A), lambda i: (i, 0)), NT, deps=deps)
    do_b = _mm_rows_t("mix_bwd_dob", dtb, wb3, tm=512, out_dtype=BF16).reshape(T, D)

    dqn, dkn, dv_b = _gqa_bwd(qn, kn, proj, o_b, lse_b, do_b)
    dq_b, gs["q_norm"] = _qk_bwd("b_bwd_qnorm", dqn, proj, B_Q, 8, small["q_norm"], cos, sin, in_scale=HEAD_B ** -0.5)
    dk_b, gs["k_norm"] = _qk_bwd("b_bwd_knorm", dkn, proj, B_K, 2, small["k_norm"], cos, sin, in_scale=1.0 / LOG2_E)

    do_groups, c_groups = _combine_bwd(do_a, a_outs, a_lses)
    dqs, dks, dvs, dbs = [], [], [], []
    for grp, d in enumerate(DILATIONS):
        dq, dk, dv, db = _dil_bwd(proj, bias[grp * HEADS_A:(grp + 1) * HEADS_A], do_groups[grp], a_lses[grp], c_groups[grp], grp, d)
        dqs.append(dq), dks.append(dk), dvs.append(dv), dbs.append(db)
    gs["rel_bias"] = _bias_grad(jnp.concatenate(dbs, axis=0))

    dproj = jnp.concatenate([p.astype(BF16) for p in dqs + dks + dvs + [dq_b, dk_b, dv_b, dga, dgb]], axis=1)
    nq = w_in.shape[2]
    deps = put_g({"w_in": _mm_wgrad("mix_bwd_dwin", h2, dproj, a_cols=None, b_cols=nq, tm=512, tn=512, J=4)})
    dh2 = _mm("mix_bwd_dh", (T // 512, D // 512),
              [(dproj, _bs((512, nq), lambda i, k, j=j: (i, j)), w_in, _bs((None, 512, nq), lambda i, k, j=j: (j, k, 0)))
               for j in range(4)],
              jax.ShapeDtypeStruct((T, D), F32), _bs((512, 512), lambda i, k: (i, k)), NT, deps=deps)

    def nb(xv, gv, dhv, dres):
        dx, dgr = _norm_bwd(xv, gv, dhv)
        dx = dx + dres
        return dx, 0.5 * dx, _colsum(dgr)

    dx1, dx1_half, g_mix = _ew("mix_bwd_norm", nb, [_tiled(x1), _whole(small["mix_norm"]), _tiled(dh2), _tiled(dx2)],
                               [(F32, D), (BF16, D)], n_rows=T, rows=256, reds=(D,))
    gs["mix_norm"] = g_mix.reshape(1, D)

    dx0, _, gs["ffn1_norm"] = _ffn_bwd("ffn1", x, small["ffn1_norm"], get_w, put_g, ffn1_saved, dx1, dx1_half)
    return loss_cols.reshape(1, D), dx0, gs


def _position():
    return lax.axis_index("x"), lax.axis_index("y"), lax.axis_index("c")


def _any_specs(n):
    return [pl.BlockSpec(memory_space=pl.ANY)] * n


HBM_SPEC = pl.BlockSpec(memory_space=pltpu.HBM)
SEM_SPEC = pl.BlockSpec(memory_space=pltpu.SEMAPHORE)
DATAFLOW_EFFECT = pltpu.SideEffectType.DATAFLOW_SIDE_EFFECTING
N_PEER_CHIPS = 3
LANES = 128


def _quarter_copies(srcs, lands, send_sems, recv_sems, scatter):
    x, y, c = _position()
    me = 2 * x + y
    peers = [(1 - x, y, c), (x, 1 - y, c), (1 - x, 1 - y, c)]
    copies = []
    for src, land, send, recv in zip(srcs, lands, send_sems, recv_sems):
        half = land.shape[1] // 2
        rows = pl.ds(c * half, half)
        for p, (px, py, pc) in enumerate(peers):
            copies.append(pltpu.make_async_remote_copy(
                src_ref=src.at[2 * px + py] if scatter else src.at[rows], dst_ref=land.at[me] if scatter else land.at[me, rows],
                send_sem=send.at[p], recv_sem=recv.at[p], device_id=(px, py, pc), device_id_type=MESH))
    return copies


def _fill_from_sibling(name, stacks):
    n = len(stacks)

    def body(*refs):
        outs = refs[n:2 * n]
        send_sems, recv_sems = refs[2 * n:]
        x, y, c = _position()
        copies = []
        for i, ref in enumerate(outs):
            half = ref.shape[1] // 2
            rows = pl.ds(c * half, half)
            for p, k in enumerate((2 * (1 - x) + y, 2 * x + (1 - y), 2 * (1 - x) + (1 - y))):
                cp = pltpu.make_async_remote_copy(ref.at[k, rows], ref.at[k, rows], send_sems.at[3 * i + p], recv_sems.at[3 * i + p],
                                                  device_id=(x, y, 1 - c), device_id_type=MESH)
                cp.start()
                copies.append(cp)
        for cp in copies:
            cp.wait()

    return pl.pallas_call(
        body, out_shape=[jax.ShapeDtypeStruct(s.shape, s.dtype) for s in stacks],
        in_specs=_any_specs(n), out_specs=_any_specs(n), input_output_aliases={i: i for i in range(n)},
        scratch_shapes=[pltpu.SemaphoreType.DMA((N_PEER_CHIPS * n,)), pltpu.SemaphoreType.DMA((N_PEER_CHIPS * n,))],
        compiler_params=pltpu.CompilerParams(has_side_effects=True), name=name)(*stacks)


def _exchange_start(name, srcs, lands, scatter):
    n = len(srcs)

    def body(*refs):
        src_refs, land_refs = refs[:n], refs[n:2 * n]
        send_sems, recv_sems = refs[2 * n:3 * n], refs[3 * n:4 * n]
        token = refs[6 * n]
        for cp in _quarter_copies(src_refs, land_refs, send_sems, recv_sems, scatter):
            cp.start()
        token[...] = jnp.zeros_like(token)

    sem = pltpu.SemaphoreType.DMA((N_PEER_CHIPS,))
    out_shape = [sem] * (2 * n) + [pltpu.HBM(a.shape, a.dtype) for a in list(srcs) + list(lands)]
    out_shape += [jax.ShapeDtypeStruct((8, LANES), F32)]
    res = pl.pallas_call(
        body, name=name, out_shape=out_shape, in_specs=[HBM_SPEC] * (2 * n),
        out_specs=[SEM_SPEC] * (2 * n) + [HBM_SPEC] * (2 * n) + [pl.BlockSpec(memory_space=pltpu.VMEM)],
        input_output_aliases={i: 2 * n + i for i in range(2 * n)},
        compiler_params=pltpu.CompilerParams(has_side_effects=DATAFLOW_EFFECT),
    )(*[pltpu.with_memory_space_constraint(a, pltpu.HBM) for a in list(srcs) + list(lands)])
    return res[:n], res[n:2 * n], res[2 * n:3 * n], res[3 * n:4 * n], res[4 * n]


def _exchange_wait(name, srcs, lands, send_sems, recv_sems, after, scatter):
    n = len(srcs)

    def body(*refs):
        src_refs, land_refs = refs[:n], refs[n:2 * n]
        sends, recvs = refs[2 * n:3 * n], refs[3 * n:4 * n]
        for cp in _quarter_copies(src_refs, land_refs, sends, recvs, scatter):
            cp.wait_send()
            cp.wait_recv()

    res = pl.pallas_call(
        body, name=name, out_shape=[pltpu.HBM(a.shape, a.dtype) for a in list(srcs) + list(lands)],
        in_specs=[HBM_SPEC] * (2 * n) + [SEM_SPEC] * (2 * n) + [pl.BlockSpec(memory_space=pl.ANY)],
        out_specs=[HBM_SPEC] * (2 * n), input_output_aliases={i: i for i in range(2 * n)},
        compiler_params=pltpu.CompilerParams(has_side_effects=DATAFLOW_EFFECT),
    )(*srcs, *lands, *send_sems, *recv_sems, after)
    return res[n:]


def _own_slot(stack_shape, own, dtype):
    me = 2 * lax.axis_index("x") + lax.axis_index("y")
    return lax.dynamic_update_slice(lax.empty(stack_shape, dtype), own[None], (me,) + (0,) * own.ndim)


def _swap_with_sibling(parts):
    n = len(parts)

    def body(*refs):
        ins, outs = refs[:n], refs[n:2 * n]
        send_sems, recv_sems = refs[2 * n:]
        x, y, c = _position()
        copies = []
        for i in range(n):
            cp = pltpu.make_async_remote_copy(ins[i], outs[i], send_sems.at[i], recv_sems.at[i],
                                              device_id=(x, y, 1 - c), device_id_type=MESH)
            cp.start()
            copies.append(cp)
        for cp in copies:
            cp.wait()

    return pl.pallas_call(
        body, out_shape=[jax.ShapeDtypeStruct(s.shape, s.dtype) for s in parts],
        in_specs=_any_specs(n), out_specs=_any_specs(n),
        scratch_shapes=[pltpu.SemaphoreType.DMA((n,)), pltpu.SemaphoreType.DMA((n,))],
        compiler_params=pltpu.CompilerParams(has_side_effects=True), name="swap_with_sibling")(*parts)


def _allreduce_small(buf):
    R, C = buf.shape
    flips = [(fx, fy, fc) for fx in (0, 1) for fy in (0, 1) for fc in (0, 1)][1:]

    def body(in_ref, out_ref, land_ref, send_sems, recv_sems):
        x, y, c = _position()
        me = 4 * x + 2 * y + c
        copies = []
        for k, (fx, fy, fc) in enumerate(flips):
            px, py, pc = (1 - x if fx else x), (1 - y if fy else y), (1 - c if fc else c)
            cp = pltpu.make_async_remote_copy(in_ref, land_ref.at[me], send_sems.at[k], recv_sems.at[k],
                                              device_id=(px, py, pc), device_id_type=MESH)
            cp.start()
            copies.append(cp)
        land_ref[me] = in_ref[...]
        for cp in copies:
            cp.wait()
        acc = land_ref[0]
        for k in range(1, 8):
            acc = acc + land_ref[k]
        out_ref[...] = acc

    return pl.pallas_call(
        body, out_shape=jax.ShapeDtypeStruct((R, C), F32),
        in_specs=[pl.BlockSpec(memory_space=pltpu.VMEM)], out_specs=pl.BlockSpec(memory_space=pltpu.VMEM),
        scratch_shapes=[pltpu.VMEM((8, R, C), F32), pltpu.SemaphoreType.DMA((7,)), pltpu.SemaphoreType.DMA((7,))],
        compiler_params=pltpu.CompilerParams(has_side_effects=True), name="allreduce_small")(buf)


def _adamw_math(w, g, m, v):
    m2 = ADAM_B1 * m + (1.0 - ADAM_B1) * g
    v2 = ADAM_B2 * v + (1.0 - ADAM_B2) * (g * g)
    m_hat = m2 / (1.0 - ADAM_B1 ** ADAM_STEP)
    v_hat = v2 / (1.0 - ADAM_B2 ** ADAM_STEP)
    delta = -ADAM_LR * (m_hat / (jnp.sqrt(v_hat) + ADAM_EPS) + ADAM_WD * w)
    return delta, m2, v2


def _adamw_big(name, w, m, v, part_mine, part_sibling):
    R, C = w.shape
    rows = 256 if R % 256 == 0 else R // 2 if (R // 2) % 8 == 0 else R

    def fn(wv, mv, vv, a, b):
        g = a + b
        return (g,) + _adamw_math(wv, g, mv, vv)

    return _ew(name, fn, [_tiled(w), _tiled(m), _tiled(v), _tiled(part_mine), _tiled(part_sibling)], [(F32, C)] * 4, n_rows=R, rows=rows)


def _sum_four(name, stack):
    _, R, C = stack.shape
    rows = 256 if R % 256 == 0 else R // 2 if (R // 2) % 8 == 0 else R
    flat = stack.reshape(4 * R, C)
    nrb = R // rows

    def fn(a, b, c, d):
        return ((a.astype(F32) + b.astype(F32)) + c.astype(F32)) + d.astype(F32)

    (out,) = _ew(name, fn, [_tiled(flat, None, 0, k * nrb) for k in range(4)], [(F32, C)], n_rows=R, rows=rows)
    return out


BIG = ("ffn1_w1", "ffn1_w3", "ffn1_w2", "w_in", "w_branch_a", "w_branch_b", "w_out", "ffn2_w1", "ffn2_w3", "ffn2_w2")
SMALL = ("ffn1_norm", "mix_norm", "b_gate", "q_norm", "k_norm", "rel_bias", "ffn2_norm", "final_norm")
ORDER = ("ffn1_norm", "ffn1_w1", "ffn1_w3", "ffn1_w2", "mix_norm", "w_in", "b_gate", "q_norm", "k_norm", "rel_bias",
         "w_branch_a", "w_branch_b", "w_out", "ffn2_norm", "ffn2_w1", "ffn2_w3", "ffn2_w2", "final_norm")
TRANSPOSED = ("ffn1_w1", "ffn1_w3", "ffn2_w1", "ffn2_w3")
GATHER_GROUPS = (("ffn1_w1", "ffn1_w3"), ("ffn1_w2",), ("w_in",), ("w_branch_a", "w_branch_b", "w_out"),
                 ("ffn2_w1", "ffn2_w3", "ffn2_w2"))


def _pack_small(d):
    rows = []
    for n in SMALL:
        flat = d[n].reshape(-1)
        pad = (-flat.shape[0]) % LANES
        rows.append(jnp.pad(flat, (0, pad)).reshape(-1, LANES))
    buf = jnp.concatenate(rows, axis=0)
    return jnp.pad(buf, ((0, (-buf.shape[0]) % 8), (0, 0)))


def _unpack_small(buf, like):
    out, r = {}, 0
    for n in SMALL:
        size = like[n].size
        nr = -(-size // LANES)
        out[n] = buf[r:r + nr].reshape(-1)[:size].reshape(like[n].shape)
        r += nr
    return out


def kernel(x, ffn1_norm, ffn1_w1, ffn1_w3, ffn1_w2, mix_norm, w_in, b_gate, q_norm, k_norm, rel_bias, w_branch_a, w_branch_b, w_out, ffn2_norm, ffn2_w1, ffn2_w3, ffn2_w2, final_norm, loss_target, m_ffn1_norm, m_ffn1_w1, m_ffn1_w3, m_ffn1_w2, m_mix_norm, m_w_in, m_b_gate, m_q_norm, m_k_norm, m_rel_bias, m_w_branch_a, m_w_branch_b, m_w_out, m_ffn2_norm, m_ffn2_w1, m_ffn2_w3, m_ffn2_w2, m_final_norm, v_ffn1_norm, v_ffn1_w1, v_ffn1_w3, v_ffn1_w2, v_mix_norm, v_w_in, v_b_gate, v_q_norm, v_k_norm, v_rel_bias, v_w_branch_a, v_w_branch_b, v_w_out, v_ffn2_norm, v_ffn2_w1, v_ffn2_w3, v_ffn2_w2, v_final_norm):
    given = dict(locals())
    w = {n: given[n] for n in ORDER}
    m = {n: given["m_" + n] for n in ORDER}
    v = {n: given["v_" + n] for n in ORDER}
    T, D = x.shape[1], x.shape[2]

    def stored(a, n):
        a = a.reshape(a.shape[1:])
        return a.T if n in TRANSPOSED else a

    def returned(a, n):
        return (a.T if n in TRANSPOSED else a).reshape(w[n].shape)

    quarter = {n: stored(w[n], n) for n in BIG}
    q16 = [quarter[n].astype(BF16) for n in BIG]
    send, recv, src_thru, land_thru, token = _exchange_start(
        "gather_start", q16, [_own_slot((4,) + q.shape, q, BF16) for q in q16], scatter=False)
    index = {n: i for i, n in enumerate(BIG)}
    ready = {}

    def get_w(name, after):
        if name not in ready:
            group = next(g for g in GATHER_GROUPS if name in g)
            ids = [index[n] for n in group]
            stacks = _exchange_wait("gather_wait_" + group[0], [src_thru[i] for i in ids], [land_thru[i] for i in ids],
                                    [send[i] for i in ids], [recv[i] for i in ids], after, scatter=False)
            stacks = _fill_from_sibling("gather_fill_" + group[0], stacks)
            for n, st in zip(group, stacks):
                ready[n] = st.reshape(D, D) if n in ("w_branch_b", "w_out") else st
        return ready[name]

    me = 2 * lax.axis_index("x") + lax.axis_index("y")
    in_flight = []

    def put_g(grads):
        names = list(grads)
        stacks = [grads[n].reshape((4,) + quarter[n].shape) for n in names]
        lands = [_own_slot(s.shape, lax.dynamic_index_in_dim(s, me, 0, keepdims=False), BF16) for s in stacks]
        started = _exchange_start("scatter_start_" + names[0], stacks, lands, scatter=True)
        in_flight.append((names,) + tuple(started[:4]))
        return [started[4]]

    small = {n: w[n] for n in SMALL}
    loss_cols, grad_x, gs = _local_step(x.reshape(T, D), loss_target.reshape(T, D), small, get_w, put_g, deps=[token])
    loss = lax.psum(jnp.sum(loss_cols), ("x", "y", "c"))

    landed = {}
    for names, s_sem, r_sem, srcs, lands in in_flight:
        got = _exchange_wait("scatter_wait_" + names[0], srcs, lands, s_sem, r_sem, grad_x, scatter=True)
        landed.update(zip(names, got))
    partial = [_sum_four(f"sum4_{n}", landed[n]) for n in BIG]
    other = _swap_with_sibling(partial)
    grads, deltas, new_m, new_v = {}, {}, {}, {}
    for n, mine, theirs in zip(BIG, partial, other):
        res = _adamw_big(f"adamw_{n}", quarter[n], stored(m[n], n), stored(v[n], n), mine, theirs)
        grads[n], deltas[n], new_m[n], new_v[n] = [returned(r, n) for r in res]

    gs = {n: gs[n].reshape(w[n].shape) for n in SMALL}
    g_small = _allreduce_small(_pack_small(gs))
    packed = [_pack_small({n: d[n] for n in SMALL}) for d in (w, m, v)]
    R = g_small.shape[0]
    res = _ew("adamw_small", lambda wv, mv, vv, g: (g,) + _adamw_math(wv, g, mv, vv),
              [_tiled(packed[0]), _tiled(packed[1]), _tiled(packed[2]), _tiled(g_small)], [(F32, LANES)] * 4, n_rows=R, rows=R)
    for d, buf in zip((grads, deltas, new_m, new_v), res):
        d.update(_unpack_small(buf, w))

    return (loss, grad_x.reshape(x.shape), *[grads[n] for n in ORDER], *[deltas[n] for n in ORDER],
            *[new_m[n] for n in ORDER], *[new_v[n] for n in ORDER])
```

```python
import functools
import math

import numpy as np
import jax
import jax.numpy as jnp
from jax import lax
from jax.experimental import pallas as pl
from jax.experimental.pallas import tpu as pltpu

F32 = jnp.float32
BF16 = jnp.bfloat16
MESH = pl.DeviceIdType.MESH

NEG_INF = -1e30
EPS = 1e-6
GRID_W = 64
ROPE_THETA = 10000.0
DILATIONS = (1, 4, 16)
BAND_HALF = 64
HEAD_A = 64
HEADS_A = 8
WIDTH_A = HEADS_A * HEAD_A
HEAD_B = 128
LOG2_E = math.log2(math.e)
QK_SCALE_LOG2 = HEAD_B ** -0.5 * LOG2_E
N_BUCKETS = 32
MAX_DISTANCE = 1024
ADAM_LR, ADAM_B1, ADAM_B2, ADAM_EPS, ADAM_WD, ADAM_STEP = 0.001, 0.9, 0.999, 1e-08, 0.01, 10

A_Q, A_K, A_V = 0, 1536, 3072
B_Q, B_K, B_V = 4608, 5632, 5888
G_A, G_B = 6144, 7168
IN_WIDTH = 8192

VMEM_LIMIT_BYTES = 56 * 1024 * 1024
QB_A = 128
QB_B = 256


def _params(*sem):
    return pltpu.CompilerParams(dimension_semantics=sem, vmem_limit_bytes=VMEM_LIMIT_BYTES)


def _bs(shape, fn):
    return pl.BlockSpec(shape, fn)


def _resident(shape, fn):
    return pl.BlockSpec(shape, fn, pipeline_mode=pl.Buffered(1))


def _mm(name, grid, pairs, out_shape, out_spec, dims, *, reduce_axis=None, extras=(), epilogue=None, deps=()):
    n_pairs, n_extra, n_deps = len(pairs), len(extras), len(deps)
    operands = [p[0] for p in pairs] + [p[2] for p in pairs] + [e[0] for e in extras] + list(deps)
    in_specs = [p[1] for p in pairs] + [p[3] for p in pairs] + [e[1] for e in extras] + _any_specs(n_deps)
    tile = tuple(s for s in out_spec.block_shape if s is not None)
    n_steps = grid[reduce_axis] if reduce_axis is not None else 1

    def body(*refs):
        a_refs, b_refs = refs[:n_pairs], refs[n_pairs:2 * n_pairs]
        e_refs = refs[2 * n_pairs:2 * n_pairs + n_extra]
        o_ref = refs[2 * n_pairs + n_extra + n_deps]
        acc = None
        for a_ref, b_ref in zip(a_refs, b_refs):
            t = lax.dot_general(a_ref[...], b_ref[...], (dims, ((), ())), preferred_element_type=F32)
            acc = t if acc is None else acc + t

        def finish(v):
            if epilogue is not None:
                v = epilogue(v, *[e[...] for e in e_refs])
            o_ref[...] = v.astype(o_ref.dtype)

        if reduce_axis is None:
            finish(acc)
        else:
            acc_ref = refs[-1]
            k = pl.program_id(reduce_axis)

            @pl.when(k == 0)
            def _():
                acc_ref[...] = acc

            @pl.when(k > 0)
            def _():
                acc_ref[...] += acc

            @pl.when(k == n_steps - 1)
            def _():
                finish(acc_ref[...])

    sem = ["parallel"] * len(grid)
    if reduce_axis is not None:
        sem[reduce_axis] = "arbitrary"
    return pl.pallas_call(
        body, out_shape=out_shape, grid=grid, in_specs=in_specs, out_specs=out_spec,
        scratch_shapes=[pltpu.VMEM(tile, F32)] if reduce_axis is not None else [],
        compiler_params=_params(*sem), name=name)(*operands)


NN = ((1,), (0,))
NT = ((1,), (1,))
TN = ((0,), (0,))


def _mm_cols(name, a, w, *, tm, tn, out_dtype, cat, extras=(), epilogue=None):
    M, K = a.shape
    J, _, n = w.shape
    tn = min(tn, n)
    nb = n // tn
    if cat:
        shape, spec = (M, J * n), _bs((tm, tn), lambda j, i, k: (i, j * nb + k))
    else:
        shape, spec = (J, M, n), _bs((None, tm, tn), lambda j, i, k: (j, i, k))
    ex = [(e, _bs((tm, tn), lambda j, i, k: (i, j * nb + k))) for e in extras]
    return _mm(name, (J, M // tm, nb),
               [(a, _bs((tm, K), lambda j, i, k: (i, 0)), w, _bs((None, K, tn), lambda j, i, k: (j, 0, k)))],
               jax.ShapeDtypeStruct(shape, out_dtype), spec, NN, extras=ex, epilogue=epilogue)


def _mm_rows_t(name, a, w, *, tm, out_dtype):
    M, N = a.shape
    J, f, _ = w.shape
    return _mm(name, (J, M // tm),
               [(a, _bs((tm, N), lambda j, i: (i, 0)), w, _bs((None, f, N), lambda j, i: (j, 0, 0)))],
               jax.ShapeDtypeStruct((J, M, f), out_dtype), _bs((None, tm, f), lambda j, i: (j, i, 0)), NT)


def _mm_wgrad(name, a, b, *, a_cols, b_cols, tm, tn, J):
    def pick(arr, cols, t):
        if arr.ndim == 3:
            T, c = arr.shape[1], arr.shape[2]
            t = min(t, c)
            return T, c, t, (lambda sel: _bs((None, T, t), lambda j, i, k: (j, 0, sel(i, k))))
        T = arr.shape[0]
        c = arr.shape[1] if cols is None else cols
        t = min(t, c)
        per = c // t
        if cols is None:
            if per == 1:
                return T, c, t, (lambda sel: _resident((T, t), lambda j, i, k: (0, 0)))
            return T, c, t, (lambda sel: _bs((T, t), lambda j, i, k: (0, sel(i, k))))
        return T, c, t, (lambda sel: _bs((T, t), lambda j, i, k: (0, j * per + sel(i, k))))
    _, ca, tm, mk_a = pick(a, a_cols, tm)
    _, cb, tn, mk_b = pick(b, b_cols, tn)
    return _mm(name, (J, ca // tm, cb // tn),
               [(a, mk_a(lambda i, k: i), b, mk_b(lambda i, k: k))],
               jax.ShapeDtypeStruct((J, ca, cb), BF16), _bs((None, tm, tn), lambda j, i, k: (j, i, k)), TN)


def _tiled(arr, width=None, col=0, rowblk=0):
    return ("t", arr, arr.shape[1] if width is None else width, col, rowblk)


def _table(arr):
    return ("f", arr)


def _whole(arr):
    return ("w", arr)


def _ew(name, fn, ins, outs, *, n_rows, rows, reds=(), ncols=1, deps=()):
    nrb = n_rows // rows
    n_deps = len(deps)
    operands, in_specs = [], []
    for spec in ins:
        if spec[0] == "t":
            _, arr, width, col, rowblk = spec
            step = 1 if ncols > 1 else 0
            in_specs.append(_bs((rows, width), lambda c, i, col=col, rowblk=rowblk, step=step: (rowblk + i, col + c * step)))
        elif spec[0] == "f":
            arr = spec[1]
            in_specs.append(_bs((rows, arr.shape[1]), lambda c, i: (i, 0)))
        else:
            arr = spec[1]
            nd = arr.ndim
            if nd == 3:
                in_specs.append(_bs((None,) + arr.shape[1:], lambda c, i: (c, 0, 0)))
            else:
                in_specs.append(_bs(arr.shape, lambda c, i, nd=nd: (0,) * nd))
        operands.append(arr)
    out_shapes = [jax.ShapeDtypeStruct((n_rows, ncols * w), dt) for dt, w in outs]
    out_specs = [_bs((rows, w), lambda c, i: (i, c)) for _, w in outs]
    out_shapes += [jax.ShapeDtypeStruct((ncols, 1, w), F32) for w in reds]
    out_specs += [_bs((None, 1, w), lambda c, i: (c, 0, 0)) for w in reds]
    n_in, n_out, n_red = len(ins), len(outs), len(reds)
    operands += list(deps)
    in_specs += _any_specs(n_deps)

    def body(*refs):
        vals = fn(*[r[...] for r in refs[:n_in]])
        if not isinstance(vals, (tuple, list)):
            vals = (vals,)
        o_refs = refs[n_in + n_deps:]
        for o_ref, v in zip(o_refs[:n_out], vals[:n_out]):
            o_ref[...] = v.astype(o_ref.dtype)
        if n_red:
            i = pl.program_id(1)
            for r_ref, v in zip(o_refs[n_out:], vals[n_out:]):
                @pl.when(i == 0)
                def _(r_ref=r_ref):
                    r_ref[...] = jnp.zeros_like(r_ref)
                r_ref[...] += v

    res = pl.pallas_call(
        body, out_shape=out_shapes, grid=(ncols, nrb), in_specs=in_specs, out_specs=out_specs,
        compiler_params=_params("parallel", "arbitrary" if n_red else "parallel"), name=name)(*operands)
    return res


def _colsum(v):
    return jnp.sum(v, axis=0, keepdims=True)


def _rstd(x):
    return lax.rsqrt(jnp.mean(x * x, axis=-1, keepdims=True) + EPS)


def _sigmoid(x):
    return 1.0 / (1.0 + jnp.exp(-x))


def _norm_fwd(x, g):
    return x * _rstd(x) * g


def _norm_bwd(x, g, dy):
    r = _rstd(x)
    xh = x * r
    dxh = dy * g
    dx = r * (dxh - xh * jnp.mean(dxh * xh, axis=-1, keepdims=True))
    return dx, dy * xh


def _ffn_fwd(tag, x, gain, get_w, deps=()):
    T, D = x.shape
    (h,) = _ew(f"{tag}_norm", lambda xv, g: _norm_fwd(xv, g), [_tiled(x), _whole(gain)], [(BF16, D)], n_rows=T, rows=512,
               deps=deps)
    w1, w3 = get_w(f"{tag}_w1", h), get_w(f"{tag}_w3", h)
    J, f, _ = w1.shape
    tm = 1024

    def up(h_ref, w1_ref, w3_ref, u_ref, g_ref, a_ref):
        hv = h_ref[...]
        u = lax.dot_general(hv, w1_ref[...], (NT, ((), ())), preferred_element_type=F32)
        g = lax.dot_general(hv, w3_ref[...], (NT, ((), ())), preferred_element_type=F32)
        u_ref[...] = u.astype(BF16)
        g_ref[...] = g.astype(BF16)
        a_ref[...] = (u * _sigmoid(u) * g).astype(BF16)

    slab = _bs((None, tm, f), lambda j, i: (j, i, 0))
    w_spec = _bs((None, f, D), lambda j, i: (j, 0, 0))
    u, g, a = pl.pallas_call(
        up, out_shape=[jax.ShapeDtypeStruct((J, T, f), BF16)] * 3, grid=(J, T // tm),
        in_specs=[_bs((tm, D), lambda j, i: (i, 0)), w_spec, w_spec], out_specs=[slab] * 3,
        compiler_params=_params("parallel", "parallel"), name=f"{tag}_up")(h, w1, w3)
    w2 = get_w(f"{tag}_w2", a)
    y = _mm(f"{tag}_down", (T // 512,),
            [(a, _bs((None, 512, f), lambda i, j=j: (j, i, 0)), w2, _resident((None, f, D), lambda i, j=j: (j, 0, 0)))
             for j in range(J)],
            jax.ShapeDtypeStruct((T, D), F32), _bs((512, D), lambda i: (i, 0)), NN,
            extras=[(x, _bs((512, D), lambda i: (i, 0)))], epilogue=lambda acc, xv: xv + 0.5 * acc)
    return y, (h, u, g, a)


def _ffn_bwd(tag, x, gain, get_w, put_g, saved, dy, dy_half):
    h, u, g, a = saved
    T, D = x.shape
    w1, w3, w2 = [get_w(f"{tag}_{n}", dy_half) for n in ("w1", "w3", "w2")]
    J, f, _ = w1.shape
    dw2 = _mm_wgrad(f"{tag}_bwd_dw2", a, dy_half, a_cols=None, b_cols=None, tm=f, tn=D, J=J)
    deps = put_g({f"{tag}_w2": dw2})
    tm = 1024

    def up_bwd(dy_ref, w2_ref, u_ref, g_ref, *rest):
        du_ref, dg_ref = rest[-2:]
        da = lax.dot_general(dy_ref[...], w2_ref[...], (NT, ((), ())), preferred_element_type=F32)
        uv, gv = u_ref[...].astype(F32), g_ref[...].astype(F32)
        s = _sigmoid(uv)
        du_ref[...] = (da * gv * (s * (1.0 + uv * (1.0 - s)))).astype(BF16)
        dg_ref[...] = (da * (uv * s)).astype(BF16)

    slab = _bs((None, tm, f), lambda j, i: (j, i, 0))
    du, dg = pl.pallas_call(
        up_bwd, out_shape=[jax.ShapeDtypeStruct((J, T, f), BF16)] * 2, grid=(J, T // tm),
        in_specs=[_bs((tm, D), lambda j, i: (i, 0)), _bs((None, f, D), lambda j, i: (j, 0, 0)), slab, slab] + _any_specs(len(deps)),
        out_specs=[slab] * 2, compiler_params=_params("parallel", "parallel"), name=f"{tag}_bwd_up")(dy_half, w2, u, g, *deps)
    dw1 = _mm_wgrad(f"{tag}_bwd_dw1", du, h, a_cols=None, b_cols=None, tm=f, tn=D, J=J)
    dw3 = _mm_wgrad(f"{tag}_bwd_dw3", dg, h, a_cols=None, b_cols=None, tm=f, tn=D, J=J)
    deps = deps + put_g({f"{tag}_w1": dw1, f"{tag}_w3": dw3})
    pairs = []
    for j in range(J):
        a_spec = _bs((None, 512, f), lambda i, j=j: (j, i, 0))
        w_spec = _resident((None, f, D), lambda i, j=j: (j, 0, 0))
        pairs += [(du, a_spec, w1, w_spec), (dg, a_spec, w3, w_spec)]
    dh = _mm(f"{tag}_bwd_dh", (T // 512,), pairs,
             jax.ShapeDtypeStruct((T, D), F32), _bs((512, D), lambda i: (i, 0)), NN, deps=deps)

    def nb(xv, gv, dhv, dres):
        dx, dgr = _norm_bwd(xv, gv, dhv)
        dx = dx + dres
        return dx, 0.5 * dx, _colsum(dgr)

    dx, dx_half, dgain = _ew(f"{tag}_bwd_norm", nb, [_tiled(x), _whole(gain), _tiled(dh), _tiled(dy)],
                             [(F32, D), (BF16, D)], n_rows=T, rows=256, reds=(D,))
    return dx, dx_half, dgain.reshape(1, D)


def _t5_bucket(rel):
    n = N_BUCKETS // 2
    max_exact = n // 2
    ret = jnp.where(rel > 0, n, 0)
    a = jnp.abs(rel)
    af = jnp.maximum(a, 1).astype(F32)
    large = max_exact + (jnp.log(af / max_exact) / math.log(MAX_DISTANCE / max_exact) * (n - max_exact)).astype(jnp.int32)
    large = jnp.minimum(large, n - 1)
    return ret + jnp.where(a < max_exact, a, large)


def _band_steps():
    qi = jnp.arange(QB_A, dtype=jnp.int32)[:, None]
    kj = jnp.arange(3 * QB_A, dtype=jnp.int32)[None, :] - QB_A
    return kj - qi


def _bias_tiles(rel_bias):
    steps = _band_steps()
    buckets = jnp.stack([_t5_bucket(steps * d) for d in DILATIONS])
    inband = (jnp.abs(steps) <= BAND_HALF).astype(jnp.int32)
    n_heads = rel_bias.shape[1]

    def body(tab_ref, b_ref, m_ref, o_ref):
        hd = pl.program_id(0)
        bkt = b_ref[...]
        acc = jnp.zeros(bkt.shape, F32)
        for b in range(N_BUCKETS):
            acc = jnp.where(bkt == b, tab_ref[b, hd], acc)
        o_ref[...] = jnp.where(m_ref[...] > 0, acc, NEG_INF)

    return pl.pallas_call(
        body, out_shape=jax.ShapeDtypeStruct((n_heads, QB_A, 3 * QB_A), F32), grid=(n_heads,),
        in_specs=[pl.BlockSpec(memory_space=pltpu.SMEM),
                  _bs((None, QB_A, 3 * QB_A), lambda hd: (hd // HEADS_A, 0, 0)),
                  _bs((QB_A, 3 * QB_A), lambda hd: (0, 0))],
        out_specs=_bs((None, QB_A, 3 * QB_A), lambda hd: (hd, 0, 0)),
        compiler_params=_params("parallel"), name="a_bias_tiles")(rel_bias, buckets, inband)


def _bias_grad(dbias):
    steps = np.arange(3 * QB_A)[None, :] - QB_A - np.arange(QB_A)[:, None]
    inband = np.abs(steps) <= BAND_HALF
    present = []
    for d in DILATIONS:
        rel = steps * d
        a = np.abs(rel)
        large = 8 + (np.log(np.maximum(a, 1) / 8.0) / math.log(MAX_DISTANCE / 8.0) * 8).astype(np.int64)
        bk = np.where(rel > 0, 16, 0) + np.where(a < 8, a, np.minimum(large, 15))
        present.append(sorted(set(bk[inband].tolist())))
    buckets = jnp.stack([_t5_bucket(_band_steps() * d) for d in DILATIONS])
    n_heads = dbias.shape[0]

    def body(b_ref, d_ref, o_ref):
        row = lax.broadcasted_iota(jnp.int32, (N_BUCKETS, n_heads), 0)
        col = lax.broadcasted_iota(jnp.int32, (N_BUCKETS, n_heads), 1)
        out = jnp.zeros((N_BUCKETS, n_heads), F32)
        for grp in range(len(DILATIONS)):
            bkt = b_ref[grp]
            for hh in range(HEADS_A):
                hd = grp * HEADS_A + hh
                ds = d_ref[hd]
                for b in present[grp]:
                    tot = jnp.sum(jnp.where(bkt == b, ds, 0.0))
                    out = jnp.where((row == b) & (col == hd), tot, out)
        o_ref[...] = out

    return pl.pallas_call(
        body, out_shape=jax.ShapeDtypeStruct((N_BUCKETS, n_heads), F32),
        compiler_params=pltpu.CompilerParams(vmem_limit_bytes=VMEM_LIMIT_BYTES), name="a_bias_grad")(buckets, dbias)


def _lane_is_second_head(shape):
    return lax.broadcasted_iota(jnp.int32, shape, len(shape) - 1) >= HEAD_A


def _group_view(proj, grp, d):
    T = proj.shape[0]
    if d == 1:
        return proj, IN_WIDTH, grp * 3 * WIDTH_A
    part = proj[:, grp * 3 * WIDTH_A:(grp + 1) * 3 * WIDTH_A]
    return part.reshape(T // d, d * 3 * WIDTH_A), 3 * WIDTH_A, 0


def _stack_heads(v2, second):
    zero = jnp.zeros_like(v2)
    return jnp.concatenate([jnp.where(second, zero, v2), jnp.where(second, v2, zero)], axis=0)


def _unstack_heads(v, second):
    return jnp.where(second, v[QB_A:], v[:QB_A])


def _edge_mask(n, nblk):
    neg_prev = jnp.where(n > 0, 0.0, NEG_INF)
    neg_next = jnp.where(n < nblk - 1, 0.0, NEG_INF)
    return jnp.concatenate([jnp.full((1, QB_A), neg_prev, F32), jnp.zeros((1, QB_A), F32),
                            jnp.full((1, QB_A), neg_next, F32)], axis=1)


def _dil_fwd(proj, bias, grp, d):
    T = proj.shape[0]
    L = T // d
    nblk = L // QB_A
    pv, width, base = _group_view(proj, grp, d)
    cb, b0 = width // WIDTH_A, base // WIDTH_A
    W2 = 2 * HEAD_A
    scale = HEAD_A ** -0.5

    def body(q_ref, kp_ref, kc_ref, kn_ref, vp_ref, vc_ref, vn_ref, b_ref, o_ref, l_ref):
        edge = _edge_mask(pl.program_id(1), nblk)
        second = _lane_is_second_head((QB_A, W2))
        for hp in range(HEADS_A // 2):
            cols = slice(hp * W2, (hp + 1) * W2)
            kcat = jnp.concatenate([kp_ref[:, cols], kc_ref[:, cols], kn_ref[:, cols]], axis=0)
            vcat = jnp.concatenate([vp_ref[:, cols], vc_ref[:, cols], vn_ref[:, cols]], axis=0)
            qs = _stack_heads(q_ref[:, cols], second)
            s = lax.dot_general(qs, kcat, (NT, ((), ())), preferred_element_type=F32)
            s = s * scale + b_ref[2 * hp:2 * hp + 2].reshape(2 * QB_A, 3 * QB_A) + edge
            m = jnp.max(s, axis=-1, keepdims=True)
            p = jnp.exp(s - m)
            l = jnp.sum(p, axis=-1, keepdims=True)
            res = jnp.dot(p.astype(BF16), vcat, preferred_element_type=F32) / l
            o_ref[:, cols] = _unstack_heads(res, second).astype(o_ref.dtype)
            l_ref[:, cols] = _unstack_heads(jnp.broadcast_to(m + jnp.log(l), (2 * QB_A, W2)), second)

    def spec(part, dn):
        return _bs((QB_A, WIDTH_A), lambda r, n: (jnp.clip(n + dn, 0, nblk - 1), r * cb + b0 + part))

    in_specs = [spec(0, 0)] + [spec(1, dn) for dn in (-1, 0, 1)] + [spec(2, dn) for dn in (-1, 0, 1)]
    in_specs += [_bs((HEADS_A, QB_A, 3 * QB_A), lambda r, n: (0, 0, 0))]
    o, lse = pl.pallas_call(
        body, out_shape=[jax.ShapeDtypeStruct((L, d * WIDTH_A), BF16), jax.ShapeDtypeStruct((L, d * WIDTH_A), F32)],
        grid=(d, nblk), in_specs=in_specs,
        out_specs=[_bs((QB_A, WIDTH_A), lambda r, n: (n, r)), _bs((QB_A, WIDTH_A), lambda r, n: (n, r))],
        compiler_params=_params("parallel", "parallel"), name=f"a_fwd_d{d}")(pv, pv, pv, pv, pv, pv, pv, bias)
    return o.reshape(T, WIDTH_A), lse.reshape(T, WIDTH_A)


def _dil_bwd(proj, bias, do, lse, cterm, grp, d):
    T = proj.shape[0]
    L = T // d
    nblk = L // QB_A
    W2 = 2 * HEAD_A
    PPS = 4
    WS = PPS * W2
    pv, width, base = _group_view(proj, grp, d)
    cb, b0 = width // WS, base // WS
    ob = WIDTH_A // WS
    view = lambda a: a.reshape(L, d * WIDTH_A)
    scale = HEAD_A ** -0.5

    def body(q_ref, kp_ref, kc_ref, kn_ref, vp_ref, vc_ref, vn_ref, do_ref, l_ref, c_ref, b_ref,
             dq_ref, dk_ref, dv_ref, db_ref):
        r, n = pl.program_id(1), pl.program_id(2)

        @pl.when(n == 0)
        def _():
            dk_ref[...] = jnp.zeros_like(dk_ref)
            dv_ref[...] = jnp.zeros_like(dv_ref)

        @pl.when((n == 0) & (r == 0))
        def _():
            db_ref[...] = jnp.zeros_like(db_ref)

        second = _lane_is_second_head((QB_A, W2))
        edge = _edge_mask(n, nblk)
        starts = [pl.multiple_of(jnp.clip(n + dn, 0, nblk - 1) * QB_A, QB_A) for dn in (-1, 0, 1)]
        for pp in range(PPS):
            cols = slice(pp * W2, (pp + 1) * W2)
            kcat = jnp.concatenate([kp_ref[:, cols], kc_ref[:, cols], kn_ref[:, cols]], axis=0)
            vcat = jnp.concatenate([vp_ref[:, cols], vc_ref[:, cols], vn_ref[:, cols]], axis=0)
            qs, dos = _stack_heads(q_ref[:, cols], second), _stack_heads(do_ref[:, cols], second)
            lse2, c2 = l_ref[:, cols], c_ref[:, cols]
            lse_rows = jnp.concatenate([lse2[:, 0:1], lse2[:, HEAD_A:HEAD_A + 1]], axis=0)
            c_rows = jnp.concatenate([c2[:, 0:1], c2[:, HEAD_A:HEAD_A + 1]], axis=0)
            s = lax.dot_general(qs, kcat, (NT, ((), ())), preferred_element_type=F32)
            p = jnp.exp(s * scale + b_ref[2 * pp:2 * pp + 2].reshape(2 * QB_A, 3 * QB_A) + edge - lse_rows)
            dp = lax.dot_general(dos, vcat, (NT, ((), ())), preferred_element_type=F32)
            ds = p * (dp + c_rows)
            db_ref[2 * pp:2 * pp + 2] += ds.reshape(2, QB_A, 3 * QB_A)
            pb, dsb = p.astype(BF16), (ds * scale).astype(BF16)
            dq_ref[:, cols] = _unstack_heads(jnp.dot(dsb, kcat, preferred_element_type=F32), second).astype(dq_ref.dtype)
            dkc = lax.dot_general(dsb, qs, (TN, ((), ())), preferred_element_type=F32)
            dvc = lax.dot_general(pb, dos, (TN, ((), ())), preferred_element_type=F32)
            for b, start in enumerate(starts):
                dk_ref[pl.ds(start, QB_A), cols] += dkc[b * QB_A:(b + 1) * QB_A]
                dv_ref[pl.ds(start, QB_A), cols] += dvc[b * QB_A:(b + 1) * QB_A]

    def spec(part, dn):
        return _bs((QB_A, WS), lambda hp, r, n: (jnp.clip(n + dn, 0, nblk - 1), r * cb + b0 + part * ob + hp))

    in_specs = [spec(0, 0)] + [spec(1, dn) for dn in (-1, 0, 1)] + [spec(2, dn) for dn in (-1, 0, 1)]
    in_specs += [_bs((QB_A, WS), lambda hp, r, n: (n, r * ob + hp))] * 3
    in_specs += [_bs((2 * PPS, QB_A, 3 * QB_A), lambda hp, r, n: (hp, 0, 0))]
    out_shape = [jax.ShapeDtypeStruct((L, d * WIDTH_A), BF16), jax.ShapeDtypeStruct((L, d * WIDTH_A), F32),
                 jax.ShapeDtypeStruct((L, d * WIDTH_A), F32), jax.ShapeDtypeStruct((HEADS_A, QB_A, 3 * QB_A), F32)]
    out_specs = [_bs((QB_A, WS), lambda hp, r, n: (n, r * ob + hp)),
                 _bs((L, WS), lambda hp, r, n: (0, r * ob + hp)), _bs((L, WS), lambda hp, r, n: (0, r * ob + hp)),
                 _bs((2 * PPS, QB_A, 3 * QB_A), lambda hp, r, n: (hp, 0, 0))]
    dq, dk, dv, db = pl.pallas_call(
        body, out_shape=out_shape, grid=(ob, d, nblk), in_specs=in_specs, out_specs=out_specs,
        compiler_params=_params("arbitrary", "arbitrary", "arbitrary"), name=f"a_bwd_d{d}")(
            pv, pv, pv, pv, pv, pv, pv, view(do), view(lse), view(cterm), bias)
    return dq.reshape(T, WIDTH_A), dk.reshape(T, WIDTH_A), dv.reshape(T, WIDTH_A), db


def _segment_ones():
    i = np.arange(WIDTH_A)
    return jnp.asarray((i[:, None] // HEAD_A == i[None, :] // HEAD_A).astype(np.float32), dtype=BF16)


def _group_weights(l0, l1, l2):
    m = jnp.maximum(jnp.maximum(l0, l1), l2)
    e = [jnp.exp(l - m) for l in (l0, l1, l2)]
    z = e[0] + e[1] + e[2]
    return [ei / z for ei in e]


def _combine_fwd(outs, lses):
    T = outs[0].shape[0]

    def fn(o0, o1, o2, l0, l1, l2):
        w = _group_weights(l0, l1, l2)
        return w[0] * o0.astype(F32) + w[1] * o1.astype(F32) + w[2] * o2.astype(F32)

    (oa,) = _ew("a_combine", fn, [_tiled(o) for o in outs] + [_tiled(l) for l in lses], [(BF16, WIDTH_A)], n_rows=T, rows=512)
    return oa


def _combine_bwd(doa, outs, lses):
    T = doa.shape[0]

    def fn(d, o0, o1, o2, l0, l1, l2, seg):
        d = d.astype(F32)
        w = _group_weights(l0, l1, l2)
        tot = jnp.zeros(d.shape, F32)
        for wg, og in zip(w, (o0, o1, o2)):
            prod = wg * d * og.astype(F32)
            hi = prod.astype(BF16)
            lo = (prod - hi.astype(F32)).astype(BF16)
            tot = tot + jnp.dot(hi, seg, preferred_element_type=F32) + jnp.dot(lo, seg, preferred_element_type=F32)
        return tuple(wg * d for wg in w) + tuple(-wg * tot for wg in w)

    res = _ew("a_combine_bwd", fn, [_tiled(doa)] + [_tiled(o) for o in outs] + [_tiled(l) for l in lses] + [_whole(_segment_ones())],
              [(BF16, WIDTH_A)] * 3 + [(F32, WIDTH_A)] * 3, n_rows=T, rows=256)
    return res[:3], res[3:]


def _rope_tables(T):
    rows = T // GRID_W
    row = jnp.repeat(jnp.arange(rows, dtype=F32), GRID_W)
    col = jnp.tile(jnp.arange(GRID_W, dtype=F32), rows)
    n_freq = HEAD_B // 4
    freq = ROPE_THETA ** (-jnp.arange(n_freq, dtype=F32) / n_freq)
    ang = jnp.concatenate([row[:, None] * freq, col[:, None] * freq], axis=-1)
    cos, sin = jnp.repeat(jnp.cos(ang), 2, axis=1), jnp.repeat(jnp.sin(ang), 2, axis=1)
    sign = jnp.where(jnp.arange(HEAD_B) % 2 == 0, -1.0, 1.0).astype(F32)
    return cos, sin * sign


def _swap_pairs(v):
    even = lax.broadcasted_iota(jnp.int32, v.shape, v.ndim - 1) % 2 == 0
    n = v.shape[-1]
    return jnp.where(even, pltpu.roll(v, n - 1, v.ndim - 1), pltpu.roll(v, 1, v.ndim - 1))


def _qk_fwd(name, proj, col0, n_heads, gain, cos, sin, out_scale=1.0):
    T = proj.shape[0]

    def fn(xr, g, c, s):
        xn = _norm_fwd(xr.astype(F32), g)
        return (xn * c + _swap_pairs(xn) * s) * out_scale

    (out,) = _ew(name, fn, [_tiled(proj, HEAD_B, col0 // HEAD_B), _whole(gain), _table(cos), _table(sin)],
                 [(BF16, HEAD_B)], n_rows=T, rows=2048, ncols=n_heads)
    return out


def _qk_bwd(name, dout, proj, col0, n_heads, gain, cos, sin, in_scale=1.0):
    T = proj.shape[0]

    def fn(dv, xr, g, c, s):
        dv = dv.astype(F32) * in_scale
        dxn = c * dv + _swap_pairs(s * dv)
        dx, dgr = _norm_bwd(xr.astype(F32), g, dxn)
        return dx, _colsum(dgr)

    dx, dg = _ew(name, fn, [_tiled(dout, HEAD_B, 0), _tiled(proj, HEAD_B, col0 // HEAD_B), _whole(gain),
                            _table(cos), _table(sin)],
                 [(BF16, HEAD_B)], n_rows=T, rows=2048, reds=(HEAD_B,), ncols=n_heads)
    return dx, jnp.sum(dg, axis=0)


def _gqa_fwd(qn, kn, proj):
    T = qn.shape[0]
    GW = 4 * HEAD_B

    def body(q_ref, k_ref, v_ref, o_ref, l_ref):
        k, v = k_ref[...], v_ref[...]
        lane = lax.broadcasted_iota(jnp.int32, (QB_B, HEAD_B), 1)
        lse_all = jnp.zeros((QB_B, HEAD_B), F32)
        for g in range(4):
            cols = slice(g * HEAD_B, (g + 1) * HEAD_B)
            s = lax.dot_general(q_ref[:, cols], k, (NT, ((), ())), preferred_element_type=F32)
            m = jnp.max(s, axis=-1, keepdims=True)
            p = jnp.exp2(s - m)
            l = jnp.sum(p, axis=-1, keepdims=True)
            o = jnp.dot(p.astype(BF16), v, preferred_element_type=F32) / l
            o_ref[:, cols] = o.astype(o_ref.dtype)
            lse_all = jnp.where(lane == g, m + jnp.log2(l), lse_all)
        l_ref[...] = lse_all

    return pl.pallas_call(
        body, out_shape=[jax.ShapeDtypeStruct((T, 2 * GW), BF16), jax.ShapeDtypeStruct((2, T, HEAD_B), F32)],
        grid=(2, T // QB_B),
        in_specs=[_bs((QB_B, GW), lambda kv, i: (i, kv)), _bs((T, HEAD_B), lambda kv, i: (0, kv)),
                  _bs((T, HEAD_B), lambda kv, i: (0, B_V // HEAD_B + kv))],
        out_specs=[_bs((QB_B, GW), lambda kv, i: (i, kv)), _bs((None, QB_B, HEAD_B), lambda kv, i: (kv, i, 0))],
        compiler_params=_params("parallel", "parallel"), name="b_fwd")(qn, kn, proj)


def _gqa_bwd(qn, kn, proj, o, lse, do):
    T = qn.shape[0]
    GW = 4 * HEAD_B

    def body(q_ref, k_ref, v_ref, o_ref, l_ref, do_ref, dq_ref, dk_ref, dv_ref):
        i = pl.program_id(1)

        @pl.when(i == 0)
        def _():
            dk_ref[...] = jnp.zeros_like(dk_ref)
            dv_ref[...] = jnp.zeros_like(dv_ref)

        k, v = k_ref[...], v_ref[...]
        lse_all = l_ref[...]
        for g in range(4):
            cols = slice(g * HEAD_B, (g + 1) * HEAD_B)
            q, dob = q_ref[:, cols], do_ref[:, cols]
            delta = jnp.sum(dob.astype(F32) * o_ref[:, cols].astype(F32), axis=-1, keepdims=True)
            s = lax.dot_general(q, k, (NT, ((), ())), preferred_element_type=F32)
            p = jnp.exp2(s - lse_all[:, g:g + 1])
            dp = lax.dot_general(dob, v, (NT, ((), ())), preferred_element_type=F32)
            ds = (p * (dp - delta)).astype(BF16)
            dq_ref[:, cols] = jnp.dot(ds, k, preferred_element_type=F32).astype(dq_ref.dtype)
            dk_ref[...] += lax.dot_general(ds, q, (TN, ((), ())), preferred_element_type=F32)
            dv_ref[...] += lax.dot_general(p.astype(BF16), dob, (TN, ((), ())), preferred_element_type=F32)

    return pl.pallas_call(
        body, out_shape=[jax.ShapeDtypeStruct((T, 2 * GW), BF16), jax.ShapeDtypeStruct((T, 2 * HEAD_B), F32),
                         jax.ShapeDtypeStruct((T, 2 * HEAD_B), F32)],
        grid=(2, T // QB_B),
        in_specs=[_bs((QB_B, GW), lambda kv, i: (i, kv)), _bs((T, HEAD_B), lambda kv, i: (0, kv)),
                  _bs((T, HEAD_B), lambda kv, i: (0, B_V // HEAD_B + kv)), _bs((QB_B, GW), lambda kv, i: (i, kv)),
                  _bs((None, QB_B, HEAD_B), lambda kv, i: (kv, i, 0)), _bs((QB_B, GW), lambda kv, i: (i, kv))],
        out_specs=[_bs((QB_B, GW), lambda kv, i: (i, kv)), _bs((T, HEAD_B), lambda kv, i: (0, kv)),
                   _bs((T, HEAD_B), lambda kv, i: (0, kv))],
        compiler_params=_params("parallel", "arbitrary"), name="b_bwd")(qn, kn, proj, o, lse, do)


def _local_step(x, target, small, get_w, put_g, deps=()):
    T, D = x.shape
    gs = {}

    x1, ffn1_saved = _ffn_fwd("ffn1", x, small["ffn1_norm"], get_w, deps)
    (h2,) = _ew("mix_norm", lambda xv, g: _norm_fwd(xv, g), [_tiled(x1), _whole(small["mix_norm"])], [(BF16, D)], n_rows=T, rows=512)
    w_in = get_w("w_in", h2)
    nq = w_in.shape[2]
    tpq = nq // WIDTH_A

    def proj_tile(j, k):
        c = j * tpq + k
        return jnp.where(c < 3 * len(DILATIONS), (c % 3) * 3 + c // 3, c)

    proj = _mm("mix_in", (4, tpq),
               [(h2, _resident((T, D), lambda j, k: (0, 0)), w_in, _bs((None, D, WIDTH_A), lambda j, k: (j, 0, k)))],
               jax.ShapeDtypeStruct((T, IN_WIDTH), BF16), _bs((T, WIDTH_A), lambda j, k: (0, proj_tile(j, k))), NN)

    bias = _bias_tiles(small["rel_bias"])
    a_outs, a_lses = [], []
    for grp, d in enumerate(DILATIONS):
        o, l = _dil_fwd(proj, bias[grp * HEADS_A:(grp + 1) * HEADS_A], grp, d)
        a_outs.append(o)
        a_lses.append(l)
    o_a = _combine_fwd(a_outs, a_lses)

    cos, sin = _rope_tables(T)
    qn = _qk_fwd("b_qnorm", proj, B_Q, 8, small["q_norm"], cos, sin, out_scale=QK_SCALE_LOG2)
    kn = _qk_fwd("b_knorm", proj, B_K, 2, small["k_norm"], cos, sin)
    o_b, lse_b = _gqa_fwd(qn, kn, proj)

    wa, wb3, w_out3 = get_w("w_branch_a", o_b), get_w("w_branch_b", o_b).reshape(1, D, D), get_w("w_out", o_b).reshape(1, D, D)
    t_a = _mm_cols("mix_branch_a", o_a, wa, tm=512, tn=256, out_dtype=BF16, cat=True)
    t_b = _mm_cols("mix_branch_b", o_b, wb3, tm=512, tn=512, out_dtype=BF16, cat=True)
    bg_a, bg_b = small["b_gate"][:, :D], small["b_gate"][:, D:]

    def merge(ta, tb, ga, gb_, ba, bb):
        sa, sb = _sigmoid(ga.astype(F32) + ba), _sigmoid(gb_.astype(F32) + bb)
        return sa * ta.astype(F32) + sb * tb.astype(F32)

    gate_ins = [_tiled(proj, D, G_A // D), _tiled(proj, D, G_B // D), _whole(bg_a), _whole(bg_b)]
    (merged,) = _ew("mix_merge", merge, [_tiled(t_a), _tiled(t_b)] + gate_ins, [(BF16, D)], n_rows=T, rows=512)
    x2 = _mm_cols("mix_out", merged, w_out3, tm=512, tn=512, out_dtype=F32, cat=True,
                  extras=[x1], epilogue=lambda acc, xv: xv + acc)
    x3, ffn2_saved = _ffn_fwd("ffn2", x2, small["ffn2_norm"], get_w)

    def head(xv, g, tv):
        r = _rstd(xv)
        xh = xv * r
        e = xh * g - tv
        dy = e * (1.0 / D)
        dxh = dy * g
        dx = r * (dxh - xh * jnp.mean(dxh * xh, axis=-1, keepdims=True))
        return dx, 0.5 * dx, _colsum(e * e) * (0.5 / D), _colsum(dy * xh)

    dx3, dx3_half, loss_cols, g_final = _ew("loss_head", head, [_tiled(x3), _whole(small["final_norm"].reshape(1, D)), _tiled(target)],
                                            [(F32, D), (BF16, D)], n_rows=T, rows=256, reds=(D, D))
    gs["final_norm"] = g_final.reshape(D)

    dx2, _, gs["ffn2_norm"] = _ffn_bwd("ffn2", x2, small["ffn2_norm"], get_w, put_g, ffn2_saved, dx3, dx3_half)

    (dmix,) = _ew("mix_bwd_cast", lambda v: v, [_tiled(dx2)], [(BF16, D)], n_rows=T, rows=512)
    g_out = _mm_wgrad("mix_bwd_dwout", merged, dmix, a_cols=D // 4, b_cols=None, tm=256, tn=512, J=4).reshape(D, D)
    dmerged = _mm_rows_t("mix_bwd_dmerged", dmix, w_out3, tm=512, out_dtype=BF16).reshape(T, D)

    def merge_bwd(dm, ta, tb, ga, gb_, ba, bb):
        dm, ta, tb = dm.astype(F32), ta.astype(F32), tb.astype(F32)
        sa, sb = _sigmoid(ga.astype(F32) + ba), _sigmoid(gb_.astype(F32) + bb)
        dga, dgb = dm * ta * sa * (1.0 - sa), dm * tb * sb * (1.0 - sb)
        return dm * sa, dm * sb, dga, dgb, _colsum(dga), _colsum(dgb)

    dta, dtb, dga, dgb, dba, dbb = _ew("mix_bwd_merge", merge_bwd, [_tiled(dmerged), _tiled(t_a), _tiled(t_b)] + gate_ins,
                                       [(BF16, D)] * 4, n_rows=T, rows=256, reds=(D, D))
    gs["b_gate"] = jnp.concatenate([dba.reshape(1, D), dbb.reshape(1, D)], axis=1)

    g_a = _mm_wgrad("mix_bwd_dwa", o_a, dta, a_cols=None, b_cols=D // 4, tm=WIDTH_A, tn=256, J=4)
    g_b = _mm_wgrad("mix_bwd_dwb", o_b, dtb, a_cols=D // 4, b_cols=None, tm=256, tn=512, J=4).reshape(D, D)
    deps = put_g({"w_out": g_out, "w_branch_a": g_a, "w_branch_b": g_b})
    do_a = _mm("mix_bwd_doa", (T // 1024,),
               [(dta, _bs((1024, D // 4), lambda i, j=j: (i, j)), wa, _bs((None, WIDTH_A, D // 4), lambda i, j=j: (j, 0, 0)))
                for j in range(4)],
               jax.ShapeDtypeStruct((T, WIDTH_A), BF16), _bs((1024, WIDTH_A), lambda i: (i, 0)), NT, deps=deps)
    do_b = _mm_rows_t("mix_bwd_dob", dtb, wb3, tm=512, out_dtype=BF16).reshape(T, D)

    dqn, dkn, dv_b = _gqa_bwd(qn, kn, proj, o_b, lse_b, do_b)
    dq_b, gs["q_norm"] = _qk_bwd("b_bwd_qnorm", dqn, proj, B_Q, 8, small["q_norm"], cos, sin, in_scale=HEAD_B ** -0.5)
    dk_b, gs["k_norm"] = _qk_bwd("b_bwd_knorm", dkn, proj, B_K, 2, small["k_norm"], cos, sin, in_scale=1.0 / LOG2_E)

    do_groups, c_groups = _combine_bwd(do_a, a_outs, a_lses)
    dqs, dks, dvs, dbs = [], [], [], []
    for grp, d in enumerate(DILATIONS):
        dq, dk, dv, db = _dil_bwd(proj, bias[grp * HEADS_A:(grp + 1) * HEADS_A], do_groups[grp], a_lses[grp], c_groups[grp], grp, d)
        dqs.append(dq), dks.append(dk), dvs.append(dv), dbs.append(db)
    gs["rel_bias"] = _bias_grad(jnp.concatenate(dbs, axis=0))

    dproj = jnp.concatenate([p.astype(BF16) for p in dqs + dks + dvs + [dq_b, dk_b, dv_b, dga, dgb]], axis=1)
    nq = w_in.shape[2]
    g_in = _mm("mix_bwd_dwin", (4, tpq),
               [(h2, _resident((T, D), lambda j, k: (0, 0)), dproj, _bs((T, WIDTH_A), lambda j, k: (0, j * tpq + k)))],
               jax.ShapeDtypeStruct((4, D, nq), BF16), _bs((None, D, WIDTH_A), lambda j, k: (j, 0, k)), TN)
    deps = put_g({"w_in": g_in})
    dh2 = _mm("mix_bwd_dh", (T // 256,),
              [(dproj, _bs((256, nq), lambda i, j=j: (i, j)), w_in, _resident((None, D, nq), lambda i, j=j: (j, 0, 0)))
               for j in range(4)],
              jax.ShapeDtypeStruct((T, D), F32), _bs((256, D), lambda i: (i, 0)), NT, deps=deps)

    def nb(xv, gv, dhv, dres):
        dx, dgr = _norm_bwd(xv, gv, dhv)
        dx = dx + dres
        return dx, 0.5 * dx, _colsum(dgr)

    dx1, dx1_half, g_mix = _ew("mix_bwd_norm", nb, [_tiled(x1), _whole(small["mix_norm"]), _tiled(dh2), _tiled(dx2)],
                               [(F32, D), (BF16, D)], n_rows=T, rows=256, reds=(D,))
    gs["mix_norm"] = g_mix.reshape(1, D)

    dx0, _, gs["ffn1_norm"] = _ffn_bwd("ffn1", x, small["ffn1_norm"], get_w, put_g, ffn1_saved, dx1, dx1_half)
    return loss_cols.reshape(1, D), dx0, gs


def _position():
    return lax.axis_index("x"), lax.axis_index("y"), lax.axis_index("c")


def _any_specs(n):
    return [pl.BlockSpec(memory_space=pl.ANY)] * n


HBM_SPEC = pl.BlockSpec(memory_space=pltpu.HBM)
SEM_SPEC = pl.BlockSpec(memory_space=pltpu.SEMAPHORE)
DATAFLOW_EFFECT = pltpu.SideEffectType.DATAFLOW_SIDE_EFFECTING
N_PEER_CHIPS = 3
LANES = 128


def _quarter_copies(srcs, lands, send_sems, recv_sems, scatter):
    x, y, c = _position()
    me = 2 * x + y
    peers = [(1 - x, y, c), (x, 1 - y, c), (1 - x, 1 - y, c)]
    copies = []
    for src, land, send, recv in zip(srcs, lands, send_sems, recv_sems):
        half = land.shape[1] // 2
        rows = pl.ds(c * half, half)
        for p, (px, py, pc) in enumerate(peers):
            copies.append(pltpu.make_async_remote_copy(
                src_ref=src.at[2 * px + py] if scatter else src.at[rows], dst_ref=land.at[me] if scatter else land.at[me, rows],
                send_sem=send.at[p], recv_sem=recv.at[p], device_id=(px, py, pc), device_id_type=MESH))
    return copies


def _fill_from_sibling(name, stacks):
    n = len(stacks)

    def body(*refs):
        outs = refs[n:2 * n]
        send_sems, recv_sems = refs[2 * n:]
        x, y, c = _position()
        copies = []
        for i, ref in enumerate(outs):
            half = ref.shape[1] // 2
            rows = pl.ds(c * half, half)
            for p, k in enumerate((2 * (1 - x) + y, 2 * x + (1 - y), 2 * (1 - x) + (1 - y))):
                cp = pltpu.make_async_remote_copy(ref.at[k, rows], ref.at[k, rows], send_sems.at[3 * i + p], recv_sems.at[3 * i + p],
                                                  device_id=(x, y, 1 - c), device_id_type=MESH)
                cp.start()
                copies.append(cp)
        for cp in copies:
            cp.wait()

    return pl.pallas_call(
        body, out_shape=[jax.ShapeDtypeStruct(s.shape, s.dtype) for s in stacks],
        in_specs=_any_specs(n), out_specs=_any_specs(n), input_output_aliases={i: i for i in range(n)},
        scratch_shapes=[pltpu.SemaphoreType.DMA((N_PEER_CHIPS * n,)), pltpu.SemaphoreType.DMA((N_PEER_CHIPS * n,))],
        compiler_params=pltpu.CompilerParams(has_side_effects=True), name=name)(*stacks)


def _exchange_start(name, srcs, lands, scatter):
    n = len(srcs)

    def body(*refs):
        src_refs, land_refs = refs[:n], refs[n:2 * n]
        send_sems, recv_sems = refs[2 * n:3 * n], refs[3 * n:4 * n]
        token = refs[6 * n]
        for cp in _quarter_copies(src_refs, land_refs, send_sems, recv_sems, scatter):
            cp.start()
        token[...] = jnp.zeros_like(token)

    sem = pltpu.SemaphoreType.DMA((N_PEER_CHIPS,))
    out_shape = [sem] * (2 * n) + [pltpu.HBM(a.shape, a.dtype) for a in list(srcs) + list(lands)]
    out_shape += [jax.ShapeDtypeStruct((8, LANES), F32)]
    res = pl.pallas_call(
        body, name=name, out_shape=out_shape, in_specs=[HBM_SPEC] * (2 * n),
        out_specs=[SEM_SPEC] * (2 * n) + [HBM_SPEC] * (2 * n) + [pl.BlockSpec(memory_space=pltpu.VMEM)],
        input_output_aliases={i: 2 * n + i for i in range(2 * n)},
        compiler_params=pltpu.CompilerParams(has_side_effects=DATAFLOW_EFFECT),
    )(*[pltpu.with_memory_space_constraint(a, pltpu.HBM) for a in list(srcs) + list(lands)])
    return res[:n], res[n:2 * n], res[2 * n:3 * n], res[3 * n:4 * n], res[4 * n]


def _exchange_wait(name, srcs, lands, send_sems, recv_sems, after, scatter):
    n = len(srcs)

    def body(*refs):
        src_refs, land_refs = refs[:n], refs[n:2 * n]
        sends, recvs = refs[2 * n:3 * n], refs[3 * n:4 * n]
        for cp in _quarter_copies(src_refs, land_refs, sends, recvs, scatter):
            cp.wait_send()
            cp.wait_recv()

    res = pl.pallas_call(
        body, name=name, out_shape=[pltpu.HBM(a.shape, a.dtype) for a in list(srcs) + list(lands)],
        in_specs=[HBM_SPEC] * (2 * n) + [SEM_SPEC] * (2 * n) + [pl.BlockSpec(memory_space=pl.ANY)],
        out_specs=[HBM_SPEC] * (2 * n), input_output_aliases={i: i for i in range(2 * n)},
        compiler_params=pltpu.CompilerParams(has_side_effects=DATAFLOW_EFFECT),
    )(*srcs, *lands, *send_sems, *recv_sems, after)
    return res[n:]


def _own_slot(stack_shape, own, dtype):
    me = 2 * lax.axis_index("x") + lax.axis_index("y")
    return lax.dynamic_update_slice(lax.empty(stack_shape, dtype), own[None], (me,) + (0,) * own.ndim)


def _swap_with_sibling(parts):
    n = len(parts)

    def body(*refs):
        ins, outs = refs[:n], refs[n:2 * n]
        send_sems, recv_sems = refs[2 * n:]
        x, y, c = _position()
        copies = []
        for i in range(n):
            cp = pltpu.make_async_remote_copy(ins[i], outs[i], send_sems.at[i], recv_sems.at[i],
                                              device_id=(x, y, 1 - c), device_id_type=MESH)
            cp.start()
            copies.append(cp)
        for cp in copies:
            cp.wait()

    return pl.pallas_call(
        body, out_shape=[jax.ShapeDtypeStruct(s.shape, s.dtype) for s in parts],
        in_specs=_any_specs(n), out_specs=_any_specs(n),
        scratch_shapes=[pltpu.SemaphoreType.DMA((n,)), pltpu.SemaphoreType.DMA((n,))],
        compiler_params=pltpu.CompilerParams(has_side_effects=True), name="swap_with_sibling")(*parts)


def _allreduce_small(buf):
    R, C = buf.shape
    flips = [(fx, fy, fc) for fx in (0, 1) for fy in (0, 1) for fc in (0, 1)][1:]

    def body(in_ref, out_ref, land_ref, send_sems, recv_sems):
        x, y, c = _position()
        me = 4 * x + 2 * y + c
        copies = []
        for k, (fx, fy, fc) in enumerate(flips):
            px, py, pc = (1 - x if fx else x), (1 - y if fy else y), (1 - c if fc else c)
            cp = pltpu.make_async_remote_copy(in_ref, land_ref.at[me], send_sems.at[k], recv_sems.at[k],
                                              device_id=(px, py, pc), device_id_type=MESH)
            cp.start()
            copies.append(cp)
        land_ref[me] = in_ref[...]
        for cp in copies:
            cp.wait()
        acc = land_ref[0]
        for k in range(1, 8):
            acc = acc + land_ref[k]
        out_ref[...] = acc

    return pl.pallas_call(
        body, out_shape=jax.ShapeDtypeStruct((R, C), F32),
        in_specs=[pl.BlockSpec(memory_space=pltpu.VMEM)], out_specs=pl.BlockSpec(memory_space=pltpu.VMEM),
        scratch_shapes=[pltpu.VMEM((8, R, C), F32), pltpu.SemaphoreType.DMA((7,)), pltpu.SemaphoreType.DMA((7,))],
        compiler_params=pltpu.CompilerParams(has_side_effects=True), name="allreduce_small")(buf)


def _adamw_math(w, g, m, v):
    m2 = ADAM_B1 * m + (1.0 - ADAM_B1) * g
    v2 = ADAM_B2 * v + (1.0 - ADAM_B2) * (g * g)
    m_hat = m2 / (1.0 - ADAM_B1 ** ADAM_STEP)
    v_hat = v2 / (1.0 - ADAM_B2 ** ADAM_STEP)
    delta = -ADAM_LR * (m_hat / (jnp.sqrt(v_hat) + ADAM_EPS) + ADAM_WD * w)
    return delta, m2, v2


def _adamw_big(name, w, m, v, part_mine, part_sibling):
    R, C = w.shape
    rows = 256 if R % 256 == 0 else R // 2 if (R // 2) % 8 == 0 else R

    def fn(wv, mv, vv, a, b):
        g = a + b
        return (g,) + _adamw_math(wv, g, mv, vv)

    return _ew(name, fn, [_tiled(w), _tiled(m), _tiled(v), _tiled(part_mine), _tiled(part_sibling)], [(F32, C)] * 4, n_rows=R, rows=rows)


def _sum_four(name, stack):
    _, R, C = stack.shape
    rows = 256 if R % 256 == 0 else R // 2 if (R // 2) % 8 == 0 else R
    flat = stack.reshape(4 * R, C)
    nrb = R // rows

    def fn(a, b, c, d):
        return ((a.astype(F32) + b.astype(F32)) + c.astype(F32)) + d.astype(F32)

    (out,) = _ew(name, fn, [_tiled(flat, None, 0, k * nrb) for k in range(4)], [(F32, C)], n_rows=R, rows=rows)
    return out


BIG = ("ffn1_w1", "ffn1_w3", "ffn1_w2", "w_in", "w_branch_a", "w_branch_b", "w_out", "ffn2_w1", "ffn2_w3", "ffn2_w2")
SMALL = ("ffn1_norm", "mix_norm", "b_gate", "q_norm", "k_norm", "rel_bias", "ffn2_norm", "final_norm")
ORDER = ("ffn1_norm", "ffn1_w1", "ffn1_w3", "ffn1_w2", "mix_norm", "w_in", "b_gate", "q_norm", "k_norm", "rel_bias",
         "w_branch_a", "w_branch_b", "w_out", "ffn2_norm", "ffn2_w1", "ffn2_w3", "ffn2_w2", "final_norm")
TRANSPOSED = ("ffn1_w1", "ffn1_w3", "ffn2_w1", "ffn2_w3")
GATHER_GROUPS = (("ffn1_w1", "ffn1_w3"), ("ffn1_w2",), ("w_in",), ("w_branch_a", "w_branch_b", "w_out"),
                 ("ffn2_w1", "ffn2_w3", "ffn2_w2"))


def _pack_small(d):
    rows = []
    for n in SMALL:
        flat = d[n].reshape(-1)
        pad = (-flat.shape[0]) % LANES
        rows.append(jnp.pad(flat, (0, pad)).reshape(-1, LANES))
    buf = jnp.concatenate(rows, axis=0)
    return jnp.pad(buf, ((0, (-buf.shape[0]) % 8), (0, 0)))


def _unpack_small(buf, like):
    out, r = {}, 0
    for n in SMALL:
        size = like[n].size
        nr = -(-size // LANES)
        out[n] = buf[r:r + nr].reshape(-1)[:size].reshape(like[n].shape)
        r += nr
    return out


def kernel(x, ffn1_norm, ffn1_w1, ffn1_w3, ffn1_w2, mix_norm, w_in, b_gate, q_norm, k_norm, rel_bias, w_branch_a, w_branch_b, w_out, ffn2_norm, ffn2_w1, ffn2_w3, ffn2_w2, final_norm, loss_target, m_ffn1_norm, m_ffn1_w1, m_ffn1_w3, m_ffn1_w2, m_mix_norm, m_w_in, m_b_gate, m_q_norm, m_k_norm, m_rel_bias, m_w_branch_a, m_w_branch_b, m_w_out, m_ffn2_norm, m_ffn2_w1, m_ffn2_w3, m_ffn2_w2, m_final_norm, v_ffn1_norm, v_ffn1_w1, v_ffn1_w3, v_ffn1_w2, v_mix_norm, v_w_in, v_b_gate, v_q_norm, v_k_norm, v_rel_bias, v_w_branch_a, v_w_branch_b, v_w_out, v_ffn2_norm, v_ffn2_w1, v_ffn2_w3, v_ffn2_w2, v_final_norm):
    given = dict(locals())
    w = {n: given[n] for n in ORDER}
    m = {n: given["m_" + n] for n in ORDER}
    v = {n: given["v_" + n] for n in ORDER}
    T, D = x.shape[1], x.shape[2]

    def stored(a, n):
        a = a.reshape(a.shape[1:])
        return a.T if n in TRANSPOSED else a

    def returned(a, n):
        return (a.T if n in TRANSPOSED else a).reshape(w[n].shape)

    quarter = {n: stored(w[n], n) for n in BIG}
    q16 = [quarter[n].astype(BF16) for n in BIG]
    send, recv, src_thru, land_thru, token = _exchange_start(
        "gather_start", q16, [_own_slot((4,) + q.shape, q, BF16) for q in q16], scatter=False)
    index = {n: i for i, n in enumerate(BIG)}
    ready = {}

    def get_w(name, after):
        if name not in ready:
            group = next(g for g in GATHER_GROUPS if name in g)
            ids = [index[n] for n in group]
            stacks = _exchange_wait("gather_wait_" + group[0], [src_thru[i] for i in ids], [land_thru[i] for i in ids],
                                    [send[i] for i in ids], [recv[i] for i in ids], after, scatter=False)
            stacks = _fill_from_sibling("gather_fill_" + group[0], stacks)
            for n, st in zip(group, stacks):
                ready[n] = st.reshape(D, D) if n in ("w_branch_b", "w_out") else st
        return ready[name]

    me = 2 * lax.axis_index("x") + lax.axis_index("y")
    in_flight = []

    def put_g(grads):
        names = list(grads)
        stacks = [grads[n].reshape((4,) + quarter[n].shape) for n in names]
        lands = [_own_slot(s.shape, lax.dynamic_index_in_dim(s, me, 0, keepdims=False), BF16) for s in stacks]
        started = _exchange_start("scatter_start_" + names[0], stacks, lands, scatter=True)
        in_flight.append((names,) + tuple(started[:4]))
        return [started[4]]

    small = {n: w[n] for n in SMALL}
    loss_cols, grad_x, gs = _local_step(x.reshape(T, D), loss_target.reshape(T, D), small, get_w, put_g, deps=[token])
    loss = lax.psum(jnp.sum(loss_cols), ("x", "y", "c"))

    landed = {}
    for names, s_sem, r_sem, srcs, lands in in_flight:
        got = _exchange_wait("scatter_wait_" + names[0], srcs, lands, s_sem, r_sem, grad_x, scatter=True)
        landed.update(zip(names, got))
    partial = [_sum_four(f"sum4_{n}", landed[n]) for n in BIG]
    other = _swap_with_sibling(partial)
    grads, deltas, new_m, new_v = {}, {}, {}, {}
    for n, mine, theirs in zip(BIG, partial, other):
        res = _adamw_big(f"adamw_{n}", quarter[n], stored(m[n], n), stored(v[n], n), mine, theirs)
        grads[n], deltas[n], new_m[n], new_v[n] = [returned(r, n) for r in res]

    gs = {n: gs[n].reshape(w[n].shape) for n in SMALL}
    g_small = _allreduce_small(_pack_small(gs))
    packed = [_pack_small({n: d[n] for n in SMALL}) for d in (w, m, v)]
    R = g_small.shape[0]
    res = _ew("adamw_small", lambda wv, mv, vv, g: (g,) + _adamw_math(wv, g, mv, vv),
              [_tiled(packed[0]), _tiled(packed[1]), _tiled(packed[2]), _tiled(g_small)], [(F32, LANES)] * 4, n_rows=R, rows=R)
    for d, buf in zip((grads, deltas, new_m, new_v), res):
        d.update(_unpack_small(buf, w))

    return (loss, grad_x.reshape(x.shape), *[grads[n] for n in ORDER], *[deltas[n] for n in ORDER],
            *[new_m[n] for n in ORDER], *[new_v[n] for n in ORDER])
```

```python
import functools
import math

import numpy as np
import jax
import jax.numpy as jnp
from jax import lax
from jax.experimental import pallas as pl
from jax.experimental.pallas import tpu as pltpu

F32 = jnp.float32
BF16 = jnp.bfloat16
MESH = pl.DeviceIdType.MESH

NEG_INF = -1e30
EPS = 1e-6
GRID_W = 64
ROPE_THETA = 10000.0
DILATIONS = (1, 4, 16)
BAND_HALF = 64
HEAD_A = 64
HEADS_A = 8
WIDTH_A = HEADS_A * HEAD_A
HEAD_B = 128
LOG2_E = math.log2(math.e)
QK_SCALE_LOG2 = HEAD_B ** -0.5 * LOG2_E
N_BUCKETS = 32
MAX_DISTANCE = 1024
ADAM_LR, ADAM_B1, ADAM_B2, ADAM_EPS, ADAM_WD, ADAM_STEP = 0.001, 0.9, 0.999, 1e-08, 0.01, 10

A_Q, A_K, A_V = 0, 1536, 3072
B_Q, B_K, B_V = 4608, 5632, 5888
G_A, G_B = 6144, 7168
IN_WIDTH = 8192

VMEM_LIMIT_BYTES = 56 * 1024 * 1024
QB_A = 128
QB_B = 256


def _params(*sem):
    return pltpu.CompilerParams(dimension_semantics=sem, vmem_limit_bytes=VMEM_LIMIT_BYTES)


def _bs(shape, fn):
    return pl.BlockSpec(shape, fn)


def _resident(shape, fn):
    return pl.BlockSpec(shape, fn, pipeline_mode=pl.Buffered(1))


def _mm(name, grid, pairs, out_shape, out_spec, dims, *, extras=(), epilogue=None, deps=(), reds=()):
    n_pairs, n_extra, n_deps = len(pairs), len(extras), len(deps)
    operands = [p[0] for p in pairs] + [p[2] for p in pairs] + [e[0] for e in extras] + list(deps)
    in_specs = [p[1] for p in pairs] + [p[3] for p in pairs] + [e[1] for e in extras] + _any_specs(n_deps)
    single = not isinstance(out_shape, (list, tuple))
    out_shapes = [out_shape] if single else list(out_shape)
    out_specs = [out_spec] if single else list(out_spec)
    n_out = len(out_shapes)
    out_shapes += [jax.ShapeDtypeStruct((1, w), F32) for w in reds]
    out_specs += [_bs((1, w), lambda *_: (0, 0)) for w in reds]

    def body(*refs):
        a_refs, b_refs = refs[:n_pairs], refs[n_pairs:2 * n_pairs]
        e_refs = refs[2 * n_pairs:2 * n_pairs + n_extra]
        o_refs = refs[2 * n_pairs + n_extra + n_deps:]
        acc = None
        for a_ref, b_ref in zip(a_refs, b_refs):
            t = lax.dot_general(a_ref[...], b_ref[...], (dims, ((), ())), preferred_element_type=F32)
            acc = t if acc is None else acc + t
        vals = acc if epilogue is None else epilogue(acc, *[e[...] for e in e_refs])
        if not isinstance(vals, (list, tuple)):
            vals = (vals,)
        for o_ref, v in zip(o_refs[:n_out], vals[:n_out]):
            o_ref[...] = v.astype(o_ref.dtype)
        if reds:
            first = functools.reduce(jnp.logical_and, [pl.program_id(ax) == 0 for ax in range(len(grid))])
            for r_ref, v in zip(o_refs[n_out:], vals[n_out:]):
                @pl.when(first)
                def _(r_ref=r_ref):
                    r_ref[...] = jnp.zeros_like(r_ref)
                r_ref[...] += v

    sem = ["arbitrary" if reds else "parallel"] * len(grid)
    res = pl.pallas_call(
        body, out_shape=out_shapes, grid=grid, in_specs=in_specs, out_specs=out_specs,
        compiler_params=_params(*sem), name=name)(*operands)
    return res[0] if (single and not reds) else res


NN = ((1,), (0,))
NT = ((1,), (1,))
TN = ((0,), (0,))


def _mm_cols(name, a, w, *, tm, tn, out_dtype, cat, extras=(), epilogue=None):
    M, K = a.shape
    J, _, n = w.shape
    tn = min(tn, n)
    nb = n // tn
    if cat:
        shape, spec = (M, J * n), _bs((tm, tn), lambda j, i, k: (i, j * nb + k))
    else:
        shape, spec = (J, M, n), _bs((None, tm, tn), lambda j, i, k: (j, i, k))
    ex = [(e, _bs((tm, tn), lambda j, i, k: (i, j * nb + k))) for e in extras]
    return _mm(name, (J, M // tm, nb),
               [(a, _bs((tm, K), lambda j, i, k: (i, 0)), w, _bs((None, K, tn), lambda j, i, k: (j, 0, k)))],
               jax.ShapeDtypeStruct(shape, out_dtype), spec, NN, extras=ex, epilogue=epilogue)


def _mm_rows_t(name, a, w, *, tm, out_dtype):
    M, N = a.shape
    J, f, _ = w.shape
    return _mm(name, (J, M // tm),
               [(a, _bs((tm, N), lambda j, i: (i, 0)), w, _bs((None, f, N), lambda j, i: (j, 0, 0)))],
               jax.ShapeDtypeStruct((J, M, f), out_dtype), _bs((None, tm, f), lambda j, i: (j, i, 0)), NT)


def _mm_wgrad(name, a, b, *, a_cols, b_cols, tm, tn, J):
    def pick(arr, cols, t):
        if arr.ndim == 3:
            T, c = arr.shape[1], arr.shape[2]
            t = min(t, c)
            return T, c, t, (lambda sel: _bs((None, T, t), lambda j, i, k: (j, 0, sel(i, k))))
        T = arr.shape[0]
        c = arr.shape[1] if cols is None else cols
        t = min(t, c)
        per = c // t
        if cols is None:
            if per == 1:
                return T, c, t, (lambda sel: _resident((T, t), lambda j, i, k: (0, 0)))
            return T, c, t, (lambda sel: _bs((T, t), lambda j, i, k: (0, sel(i, k))))
        return T, c, t, (lambda sel: _bs((T, t), lambda j, i, k: (0, j * per + sel(i, k))))
    _, ca, tm, mk_a = pick(a, a_cols, tm)
    _, cb, tn, mk_b = pick(b, b_cols, tn)
    return _mm(name, (J, ca // tm, cb // tn),
               [(a, mk_a(lambda i, k: i), b, mk_b(lambda i, k: k))],
               jax.ShapeDtypeStruct((J, ca, cb), BF16), _bs((None, tm, tn), lambda j, i, k: (j, i, k)), TN)


def _tiled(arr, width=None, col=0, rowblk=0):
    return ("t", arr, arr.shape[1] if width is None else width, col, rowblk)


def _table(arr):
    return ("f", arr)


def _whole(arr):
    return ("w", arr)


def _ew(name, fn, ins, outs, *, n_rows, rows, reds=(), ncols=1, deps=()):
    nrb = n_rows // rows
    n_deps = len(deps)
    operands, in_specs = [], []
    for spec in ins:
        if spec[0] == "t":
            _, arr, width, col, rowblk = spec
            step = 1 if ncols > 1 else 0
            in_specs.append(_bs((rows, width), lambda c, i, col=col, rowblk=rowblk, step=step: (rowblk + i, col + c * step)))
        elif spec[0] == "f":
            arr = spec[1]
            in_specs.append(_bs((rows, arr.shape[1]), lambda c, i: (i, 0)))
        else:
            arr = spec[1]
            nd = arr.ndim
            if nd == 3:
                in_specs.append(_bs((None,) + arr.shape[1:], lambda c, i: (c, 0, 0)))
            else:
                in_specs.append(_bs(arr.shape, lambda c, i, nd=nd: (0,) * nd))
        operands.append(arr)
    out_shapes = [jax.ShapeDtypeStruct((n_rows, ncols * w), dt) for dt, w in outs]
    out_specs = [_bs((rows, w), lambda c, i: (i, c)) for _, w in outs]
    out_shapes += [jax.ShapeDtypeStruct((ncols, 1, w), F32) for w in reds]
    out_specs += [_bs((None, 1, w), lambda c, i: (c, 0, 0)) for w in reds]
    n_in, n_out, n_red = len(ins), len(outs), len(reds)
    operands += list(deps)
    in_specs += _any_specs(n_deps)

    def body(*refs):
        vals = fn(*[r[...] for r in refs[:n_in]])
        if not isinstance(vals, (tuple, list)):
            vals = (vals,)
        o_refs = refs[n_in + n_deps:]
        for o_ref, v in zip(o_refs[:n_out], vals[:n_out]):
            o_ref[...] = v.astype(o_ref.dtype)
        if n_red:
            i = pl.program_id(1)
            for r_ref, v in zip(o_refs[n_out:], vals[n_out:]):
                @pl.when(i == 0)
                def _(r_ref=r_ref):
                    r_ref[...] = jnp.zeros_like(r_ref)
                r_ref[...] += v

    res = pl.pallas_call(
        body, out_shape=out_shapes, grid=(ncols, nrb), in_specs=in_specs, out_specs=out_specs,
        compiler_params=_params("parallel", "arbitrary" if n_red else "parallel"), name=name)(*operands)
    return res


def _colsum(v):
    return jnp.sum(v, axis=0, keepdims=True)


def _rstd(x):
    return lax.rsqrt(jnp.mean(x * x, axis=-1, keepdims=True) + EPS)


def _sigmoid(x):
    return 1.0 / (1.0 + jnp.exp(-x))


def _norm_fwd(x, g):
    return x * _rstd(x) * g


def _norm_bwd(x, g, dy):
    r = _rstd(x)
    xh = x * r
    dxh = dy * g
    dx = r * (dxh - xh * jnp.mean(dxh * xh, axis=-1, keepdims=True))
    return dx, dy * xh


def _row_spec(arr, rows):
    if arr.shape[0] == 1:
        return _bs(arr.shape, lambda i: (0, 0))
    return _bs((rows, arr.shape[1]), lambda i: (i, 0))


def _ffn_fwd(tag, x, gain, get_w, deps=(), *, h=None, tail_ins=(), tail_fn=None, tail_outs=(F32,), tail_reds=()):
    T, D = x.shape
    if h is None:
        (h,) = _ew(f"{tag}_norm", lambda xv, g: _norm_fwd(xv, g), [_tiled(x), _whole(gain)], [(BF16, D)], n_rows=T, rows=512,
                   deps=deps)
    w1, w3 = get_w(f"{tag}_w1", h), get_w(f"{tag}_w3", h)
    J, f, _ = w1.shape
    tm = 1024

    def up(h_ref, w1_ref, w3_ref, u_ref, g_ref, a_ref):
        hv = h_ref[...]
        u = lax.dot_general(hv, w1_ref[...], (NT, ((), ())), preferred_element_type=F32)
        g = lax.dot_general(hv, w3_ref[...], (NT, ((), ())), preferred_element_type=F32)
        u_ref[...] = u.astype(BF16)
        g_ref[...] = g.astype(BF16)
        a_ref[...] = (u * _sigmoid(u) * g).astype(BF16)

    slab = _bs((None, tm, f), lambda j, i: (j, i, 0))
    w_spec = _bs((None, f, D), lambda j, i: (j, 0, 0))
    u, g, a = pl.pallas_call(
        up, out_shape=[jax.ShapeDtypeStruct((J, T, f), BF16)] * 3, grid=(J, T // tm),
        in_specs=[_bs((tm, D), lambda j, i: (i, 0)), w_spec, w_spec], out_specs=[slab] * 3,
        compiler_params=_params("parallel", "parallel"), name=f"{tag}_up")(h, w1, w3)
    w2 = get_w(f"{tag}_w2", a)
    def tail(acc, xv, *rest):
        y = xv + 0.5 * acc
        return y if tail_fn is None else tail_fn(y, *rest)

    row = _bs((512, D), lambda i: (i, 0))
    res = _mm(f"{tag}_down", (T // 512,),
              [(a, _bs((None, 512, f), lambda i, j=j: (j, i, 0)), w2, _resident((None, f, D), lambda i, j=j: (j, 0, 0)))
               for j in range(J)],
              [jax.ShapeDtypeStruct((T, D), dt) for dt in tail_outs], [row] * len(tail_outs), NN,
              extras=[(x, row)] + [(t, _row_spec(t, 512)) for t in tail_ins], epilogue=tail, reds=tail_reds)
    return res, (h, u, g, a)


def _dh_norm_bwd(name, rows, pairs, dims, x, gain, dres, deps, also_bf16=False):
    T, D = x.shape

    def epilogue(dh, xv, gv, dr):
        dx, dgr = _norm_bwd(xv, gv, dh)
        dx = dx + dr
        return (dx, 0.5 * dx) + ((dx,) if also_bf16 else ()) + (_colsum(dgr),)

    dts = [F32, BF16] + ([BF16] if also_bf16 else [])
    row = _bs((rows, D), lambda i: (i, 0))
    return _mm(name, (T // rows,), pairs, [jax.ShapeDtypeStruct((T, D), dt) for dt in dts], [row] * len(dts), dims,
               extras=[(x, row), (gain, _row_spec(gain, rows)), (dres, row)], epilogue=epilogue, deps=deps, reds=(D,))


def _ffn_bwd(tag, x, gain, get_w, put_g, saved, dy, dy_half, also_bf16=False):
    h, u, g, a = saved
    T, D = x.shape
    w1, w3, w2 = [get_w(f"{tag}_{n}", dy_half) for n in ("w1", "w3", "w2")]
    J, f, _ = w1.shape
    dw2 = _mm_wgrad(f"{tag}_bwd_dw2", a, dy_half, a_cols=None, b_cols=None, tm=f, tn=D, J=J)
    deps = put_g({f"{tag}_w2": dw2})
    tm = 1024

    def up_bwd(dy_ref, w2_ref, u_ref, g_ref, *rest):
        du_ref, dg_ref = rest[-2:]
        da = lax.dot_general(dy_ref[...], w2_ref[...], (NT, ((), ())), preferred_element_type=F32)
        uv, gv = u_ref[...].astype(F32), g_ref[...].astype(F32)
        s = _sigmoid(uv)
        du_ref[...] = (da * gv * (s * (1.0 + uv * (1.0 - s)))).astype(BF16)
        dg_ref[...] = (da * (uv * s)).astype(BF16)

    slab = _bs((None, tm, f), lambda j, i: (j, i, 0))
    du, dg = pl.pallas_call(
        up_bwd, out_shape=[jax.ShapeDtypeStruct((J, T, f), BF16)] * 2, grid=(J, T // tm),
        in_specs=[_bs((tm, D), lambda j, i: (i, 0)), _bs((None, f, D), lambda j, i: (j, 0, 0)), slab, slab] + _any_specs(len(deps)),
        out_specs=[slab] * 2, compiler_params=_params("parallel", "parallel"), name=f"{tag}_bwd_up")(dy_half, w2, u, g, *deps)
    dw1 = _mm_wgrad(f"{tag}_bwd_dw1", du, h, a_cols=None, b_cols=None, tm=f, tn=D, J=J)
    dw3 = _mm_wgrad(f"{tag}_bwd_dw3", dg, h, a_cols=None, b_cols=None, tm=f, tn=D, J=J)
    deps = deps + put_g({f"{tag}_w1": dw1, f"{tag}_w3": dw3})
    pairs = []
    for j in range(J):
        a_spec = _bs((None, 512, f), lambda i, j=j: (j, i, 0))
        w_spec = _resident((None, f, D), lambda i, j=j: (j, 0, 0))
        pairs += [(du, a_spec, w1, w_spec), (dg, a_spec, w3, w_spec)]
    return _dh_norm_bwd(f"{tag}_bwd_dh", 512, pairs, NN, x, gain, dy, deps, also_bf16)


def _t5_bucket(rel):
    n = N_BUCKETS // 2
    max_exact = n // 2
    ret = jnp.where(rel > 0, n, 0)
    a = jnp.abs(rel)
    af = jnp.maximum(a, 1).astype(F32)
    large = max_exact + (jnp.log(af / max_exact) / math.log(MAX_DISTANCE / max_exact) * (n - max_exact)).astype(jnp.int32)
    large = jnp.minimum(large, n - 1)
    return ret + jnp.where(a < max_exact, a, large)


def _band_steps():
    qi = jnp.arange(QB_A, dtype=jnp.int32)[:, None]
    kj = jnp.arange(3 * QB_A, dtype=jnp.int32)[None, :] - QB_A
    return kj - qi


def _bias_tiles(rel_bias):
    steps = _band_steps()
    buckets = jnp.stack([_t5_bucket(steps * d) for d in DILATIONS])
    inband = (jnp.abs(steps) <= BAND_HALF).astype(jnp.int32)
    n_heads = rel_bias.shape[1]

    def body(tab_ref, b_ref, m_ref, o_ref):
        hd = pl.program_id(0)
        bkt = b_ref[...]
        acc = jnp.zeros(bkt.shape, F32)
        for b in range(N_BUCKETS):
            acc = jnp.where(bkt == b, tab_ref[b, hd], acc)
        o_ref[...] = jnp.where(m_ref[...] > 0, acc, NEG_INF)

    return pl.pallas_call(
        body, out_shape=jax.ShapeDtypeStruct((n_heads, QB_A, 3 * QB_A), F32), grid=(n_heads,),
        in_specs=[pl.BlockSpec(memory_space=pltpu.SMEM),
                  _bs((None, QB_A, 3 * QB_A), lambda hd: (hd // HEADS_A, 0, 0)),
                  _bs((QB_A, 3 * QB_A), lambda hd: (0, 0))],
        out_specs=_bs((None, QB_A, 3 * QB_A), lambda hd: (hd, 0, 0)),
        compiler_params=_params("parallel"), name="a_bias_tiles")(rel_bias, buckets, inband)


def _bias_grad(dbias):
    steps = np.arange(3 * QB_A)[None, :] - QB_A - np.arange(QB_A)[:, None]
    inband = np.abs(steps) <= BAND_HALF
    present = []
    for d in DILATIONS:
        rel = steps * d
        a = np.abs(rel)
        large = 8 + (np.log(np.maximum(a, 1) / 8.0) / math.log(MAX_DISTANCE / 8.0) * 8).astype(np.int64)
        bk = np.where(rel > 0, 16, 0) + np.where(a < 8, a, np.minimum(large, 15))
        present.append(sorted(set(bk[inband].tolist())))
    buckets = jnp.stack([_t5_bucket(_band_steps() * d) for d in DILATIONS])
    n_heads = dbias.shape[0]

    def body(b_ref, d_ref, o_ref):
        row = lax.broadcasted_iota(jnp.int32, (N_BUCKETS, n_heads), 0)
        col = lax.broadcasted_iota(jnp.int32, (N_BUCKETS, n_heads), 1)
        out = jnp.zeros((N_BUCKETS, n_heads), F32)
        for grp in range(len(DILATIONS)):
            bkt = b_ref[grp]
            for hh in range(HEADS_A):
                hd = grp * HEADS_A + hh
                ds = d_ref[hd]
                for b in present[grp]:
                    tot = jnp.sum(jnp.where(bkt == b, ds, 0.0))
                    out = jnp.where((row == b) & (col == hd), tot, out)
        o_ref[...] = out

    return pl.pallas_call(
        body, out_shape=jax.ShapeDtypeStruct((N_BUCKETS, n_heads), F32),
        compiler_params=pltpu.CompilerParams(vmem_limit_bytes=VMEM_LIMIT_BYTES), name="a_bias_grad")(buckets, dbias)


def _lane_is_second_head(shape):
    return lax.broadcasted_iota(jnp.int32, shape, len(shape) - 1) >= HEAD_A


def _group_view(proj, grp, d):
    T = proj.shape[0]
    if d == 1:
        return proj, IN_WIDTH, grp * 3 * WIDTH_A
    part = proj[:, grp * 3 * WIDTH_A:(grp + 1) * 3 * WIDTH_A]
    return part.reshape(T // d, d * 3 * WIDTH_A), 3 * WIDTH_A, 0


def _stack_heads(v2, second):
    zero = jnp.zeros_like(v2)
    return jnp.concatenate([jnp.where(second, zero, v2), jnp.where(second, v2, zero)], axis=0)


def _unstack_heads(v, second):
    return jnp.where(second, v[QB_A:], v[:QB_A])


def _edge_mask(n, nblk):
    neg_prev = jnp.where(n > 0, 0.0, NEG_INF)
    neg_next = jnp.where(n < nblk - 1, 0.0, NEG_INF)
    return jnp.concatenate([jnp.full((1, QB_A), neg_prev, F32), jnp.zeros((1, QB_A), F32),
                            jnp.full((1, QB_A), neg_next, F32)], axis=1)


def _dil_fwd(proj, bias, grp, d):
    T = proj.shape[0]
    L = T // d
    nblk = L // QB_A
    pv, width, base = _group_view(proj, grp, d)
    cb, b0 = width // WIDTH_A, base // WIDTH_A
    W2 = 2 * HEAD_A
    scale = HEAD_A ** -0.5

    def body(q_ref, kp_ref, kc_ref, kn_ref, vp_ref, vc_ref, vn_ref, b_ref, o_ref, l_ref):
        edge = _edge_mask(pl.program_id(1), nblk)
        second = _lane_is_second_head((QB_A, W2))
        for hp in range(HEADS_A // 2):
            cols = slice(hp * W2, (hp + 1) * W2)
            kcat = jnp.concatenate([kp_ref[:, cols], kc_ref[:, cols], kn_ref[:, cols]], axis=0)
            vcat = jnp.concatenate([vp_ref[:, cols], vc_ref[:, cols], vn_ref[:, cols]], axis=0)
            qs = _stack_heads(q_ref[:, cols], second)
            s = lax.dot_general(qs, kcat, (NT, ((), ())), preferred_element_type=F32)
            s = s * scale + b_ref[2 * hp:2 * hp + 2].reshape(2 * QB_A, 3 * QB_A) + edge
            m = jnp.max(s, axis=-1, keepdims=True)
            p = jnp.exp(s - m)
            l = jnp.sum(p, axis=-1, keepdims=True)
            res = jnp.dot(p.astype(BF16), vcat, preferred_element_type=F32) / l
            o_ref[:, cols] = _unstack_heads(res, second).astype(o_ref.dtype)
            l_ref[:, cols] = _unstack_heads(jnp.broadcast_to(m + jnp.log(l), (2 * QB_A, W2)), second)

    def spec(part, dn):
        return _bs((QB_A, WIDTH_A), lambda r, n: (jnp.clip(n + dn, 0, nblk - 1), r * cb + b0 + part))

    in_specs = [spec(0, 0)] + [spec(1, dn) for dn in (-1, 0, 1)] + [spec(2, dn) for dn in (-1, 0, 1)]
    in_specs += [_bs((HEADS_A, QB_A, 3 * QB_A), lambda r, n: (0, 0, 0))]
    o, lse = pl.pallas_call(
        body, out_shape=[jax.ShapeDtypeStruct((L, d * WIDTH_A), BF16), jax.ShapeDtypeStruct((L, d * WIDTH_A), F32)],
        grid=(d, nblk), in_specs=in_specs,
        out_specs=[_bs((QB_A, WIDTH_A), lambda r, n: (n, r)), _bs((QB_A, WIDTH_A), lambda r, n: (n, r))],
        compiler_params=_params("parallel", "parallel"), name=f"a_fwd_d{d}")(pv, pv, pv, pv, pv, pv, pv, bias)
    return o.reshape(T, WIDTH_A), lse.reshape(T, WIDTH_A)


def _dil_bwd(proj, bias, do, lse, cterm, grp, d):
    T = proj.shape[0]
    L = T // d
    nblk = L // QB_A
    W2 = 2 * HEAD_A
    PPS = 4
    WS = PPS * W2
    pv, width, base = _group_view(proj, grp, d)
    cb, b0 = width // WS, base // WS
    ob = WIDTH_A // WS
    view = lambda a: a.reshape(L, d * WIDTH_A)
    scale = HEAD_A ** -0.5

    def body(q_ref, kp_ref, kc_ref, kn_ref, vp_ref, vc_ref, vn_ref, do_ref, l_ref, c_ref, b_ref,
             dq_ref, dk_ref, dv_ref, db_ref):
        r, n = pl.program_id(1), pl.program_id(2)

        @pl.when(n == 0)
        def _():
            dk_ref[...] = jnp.zeros_like(dk_ref)
            dv_ref[...] = jnp.zeros_like(dv_ref)

        @pl.when((n == 0) & (r == 0))
        def _():
            db_ref[...] = jnp.zeros_like(db_ref)

        second = _lane_is_second_head((QB_A, W2))
        edge = _edge_mask(n, nblk)
        starts = [pl.multiple_of(jnp.clip(n + dn, 0, nblk - 1) * QB_A, QB_A) for dn in (-1, 0, 1)]
        for pp in range(PPS):
            cols = slice(pp * W2, (pp + 1) * W2)
            kcat = jnp.concatenate([kp_ref[:, cols], kc_ref[:, cols], kn_ref[:, cols]], axis=0)
            vcat = jnp.concatenate([vp_ref[:, cols], vc_ref[:, cols], vn_ref[:, cols]], axis=0)
            qs, dos = _stack_heads(q_ref[:, cols], second), _stack_heads(do_ref[:, cols], second)
            lse2, c2 = l_ref[:, cols], c_ref[:, cols]
            lse_rows = jnp.concatenate([lse2[:, 0:1], lse2[:, HEAD_A:HEAD_A + 1]], axis=0)
            c_rows = jnp.concatenate([c2[:, 0:1], c2[:, HEAD_A:HEAD_A + 1]], axis=0)
            s = lax.dot_general(qs, kcat, (NT, ((), ())), preferred_element_type=F32)
            p = jnp.exp(s * scale + b_ref[2 * pp:2 * pp + 2].reshape(2 * QB_A, 3 * QB_A) + edge - lse_rows)
            dp = lax.dot_general(dos, vcat, (NT, ((), ())), preferred_element_type=F32)
            ds = p * (dp + c_rows)
            db_ref[2 * pp:2 * pp + 2] += ds.reshape(2, QB_A, 3 * QB_A)
            pb, dsb = p.astype(BF16), (ds * scale).astype(BF16)
            dq_ref[:, cols] = _unstack_heads(jnp.dot(dsb, kcat, preferred_element_type=F32), second).astype(dq_ref.dtype)
            dkc = lax.dot_general(dsb, qs, (TN, ((), ())), preferred_element_type=F32)
            dvc = lax.dot_general(pb, dos, (TN, ((), ())), preferred_element_type=F32)
            for b, start in enumerate(starts):
                dk_ref[pl.ds(start, QB_A), cols] += dkc[b * QB_A:(b + 1) * QB_A]
                dv_ref[pl.ds(start, QB_A), cols] += dvc[b * QB_A:(b + 1) * QB_A]

    def spec(part, dn):
        return _bs((QB_A, WS), lambda hp, r, n: (jnp.clip(n + dn, 0, nblk - 1), r * cb + b0 + part * ob + hp))

    in_specs = [spec(0, 0)] + [spec(1, dn) for dn in (-1, 0, 1)] + [spec(2, dn) for dn in (-1, 0, 1)]
    in_specs += [_bs((QB_A, WS), lambda hp, r, n: (n, r * ob + hp))] * 3
    in_specs += [_bs((2 * PPS, QB_A, 3 * QB_A), lambda hp, r, n: (hp, 0, 0))]
    out_shape = [jax.ShapeDtypeStruct((L, d * WIDTH_A), BF16), jax.ShapeDtypeStruct((L, d * WIDTH_A), F32),
                 jax.ShapeDtypeStruct((L, d * WIDTH_A), F32), jax.ShapeDtypeStruct((HEADS_A, QB_A, 3 * QB_A), F32)]
    out_specs = [_bs((QB_A, WS), lambda hp, r, n: (n, r * ob + hp)),
                 _bs((L, WS), lambda hp, r, n: (0, r * ob + hp)), _bs((L, WS), lambda hp, r, n: (0, r * ob + hp)),
                 _bs((2 * PPS, QB_A, 3 * QB_A), lambda hp, r, n: (hp, 0, 0))]
    dq, dk, dv, db = pl.pallas_call(
        body, out_shape=out_shape, grid=(ob, d, nblk), in_specs=in_specs, out_specs=out_specs,
        compiler_params=_params("arbitrary", "arbitrary", "arbitrary"), name=f"a_bwd_d{d}")(
            pv, pv, pv, pv, pv, pv, pv, view(do), view(lse), view(cterm), bias)
    return dq.reshape(T, WIDTH_A), dk.reshape(T, WIDTH_A), dv.reshape(T, WIDTH_A), db


def _segment_ones():
    i = np.arange(WIDTH_A)
    return jnp.asarray((i[:, None] // HEAD_A == i[None, :] // HEAD_A).astype(np.float32), dtype=BF16)


def _group_weights(l0, l1, l2):
    m = jnp.maximum(jnp.maximum(l0, l1), l2)
    e = [jnp.exp(l - m) for l in (l0, l1, l2)]
    z = e[0] + e[1] + e[2]
    return [ei / z for ei in e]


def _combine_fwd(outs, lses):
    T = outs[0].shape[0]

    def fn(o0, o1, o2, l0, l1, l2):
        w = _group_weights(l0, l1, l2)
        return w[0] * o0.astype(F32) + w[1] * o1.astype(F32) + w[2] * o2.astype(F32)

    (oa,) = _ew("a_combine", fn, [_tiled(o) for o in outs] + [_tiled(l) for l in lses], [(BF16, WIDTH_A)], n_rows=T, rows=512)
    return oa


def _combine_bwd(doa, outs, lses):
    T = doa.shape[0]

    def fn(d, o0, o1, o2, l0, l1, l2, seg):
        d = d.astype(F32)
        w = _group_weights(l0, l1, l2)
        tot = jnp.zeros(d.shape, F32)
        for wg, og in zip(w, (o0, o1, o2)):
            prod = wg * d * og.astype(F32)
            hi = prod.astype(BF16)
            lo = (prod - hi.astype(F32)).astype(BF16)
            tot = tot + jnp.dot(hi, seg, preferred_element_type=F32) + jnp.dot(lo, seg, preferred_element_type=F32)
        return tuple(wg * d for wg in w) + tuple(-wg * tot for wg in w)

    res = _ew("a_combine_bwd", fn, [_tiled(doa)] + [_tiled(o) for o in outs] + [_tiled(l) for l in lses] + [_whole(_segment_ones())],
              [(BF16, WIDTH_A)] * 3 + [(F32, WIDTH_A)] * 3, n_rows=T, rows=256)
    return res[:3], res[3:]


def _rope_tables(T):
    rows = T // GRID_W
    row = jnp.repeat(jnp.arange(rows, dtype=F32), GRID_W)
    col = jnp.tile(jnp.arange(GRID_W, dtype=F32), rows)
    n_freq = HEAD_B // 4
    freq = ROPE_THETA ** (-jnp.arange(n_freq, dtype=F32) / n_freq)
    ang = jnp.concatenate([row[:, None] * freq, col[:, None] * freq], axis=-1)
    cos, sin = jnp.repeat(jnp.cos(ang), 2, axis=1), jnp.repeat(jnp.sin(ang), 2, axis=1)
    sign = jnp.where(jnp.arange(HEAD_B) % 2 == 0, -1.0, 1.0).astype(F32)
    return cos, sin * sign


def _swap_pairs(v):
    even = lax.broadcasted_iota(jnp.int32, v.shape, v.ndim - 1) % 2 == 0
    n = v.shape[-1]
    return jnp.where(even, pltpu.roll(v, n - 1, v.ndim - 1), pltpu.roll(v, 1, v.ndim - 1))


def _qk_fwd(name, proj, col0, n_heads, gain, cos, sin, out_scale=1.0):
    T = proj.shape[0]

    def fn(xr, g, c, s):
        xn = _norm_fwd(xr.astype(F32), g)
        return (xn * c + _swap_pairs(xn) * s) * out_scale

    (out,) = _ew(name, fn, [_tiled(proj, HEAD_B, col0 // HEAD_B), _whole(gain), _table(cos), _table(sin)],
                 [(BF16, HEAD_B)], n_rows=T, rows=2048, ncols=n_heads)
    return out


def _qk_bwd(name, dout, proj, col0, n_heads, gain, cos, sin, in_scale=1.0):
    T = proj.shape[0]

    def fn(dv, xr, g, c, s):
        dv = dv.astype(F32) * in_scale
        dxn = c * dv + _swap_pairs(s * dv)
        dx, dgr = _norm_bwd(xr.astype(F32), g, dxn)
        return dx, _colsum(dgr)

    dx, dg = _ew(name, fn, [_tiled(dout, HEAD_B, 0), _tiled(proj, HEAD_B, col0 // HEAD_B), _whole(gain),
                            _table(cos), _table(sin)],
                 [(BF16, HEAD_B)], n_rows=T, rows=2048, reds=(HEAD_B,), ncols=n_heads)
    return dx, jnp.sum(dg, axis=0)


def _gqa_fwd(qn, kn, proj):
    T = qn.shape[0]
    GW = 4 * HEAD_B

    def body(q_ref, k_ref, v_ref, o_ref, l_ref):
        k, v = k_ref[...], v_ref[...]
        lane = lax.broadcasted_iota(jnp.int32, (QB_B, HEAD_B), 1)
        lse_all = jnp.zeros((QB_B, HEAD_B), F32)
        for g in range(4):
            cols = slice(g * HEAD_B, (g + 1) * HEAD_B)
            s = lax.dot_general(q_ref[:, cols], k, (NT, ((), ())), preferred_element_type=F32)
            m = jnp.max(s, axis=-1, keepdims=True)
            p = jnp.exp2(s - m)
            l = jnp.sum(p, axis=-1, keepdims=True)
            o = jnp.dot(p.astype(BF16), v, preferred_element_type=F32) / l
            o_ref[:, cols] = o.astype(o_ref.dtype)
            lse_all = jnp.where(lane == g, m + jnp.log2(l), lse_all)
        l_ref[...] = lse_all

    return pl.pallas_call(
        body, out_shape=[jax.ShapeDtypeStruct((T, 2 * GW), BF16), jax.ShapeDtypeStruct((2, T, HEAD_B), F32)],
        grid=(2, T // QB_B),
        in_specs=[_bs((QB_B, GW), lambda kv, i: (i, kv)), _bs((T, HEAD_B), lambda kv, i: (0, kv)),
                  _bs((T, HEAD_B), lambda kv, i: (0, B_V // HEAD_B + kv))],
        out_specs=[_bs((QB_B, GW), lambda kv, i: (i, kv)), _bs((None, QB_B, HEAD_B), lambda kv, i: (kv, i, 0))],
        compiler_params=_params("parallel", "parallel"), name="b_fwd")(qn, kn, proj)


def _gqa_bwd(qn, kn, proj, o, lse, do):
    T = qn.shape[0]
    GW = 4 * HEAD_B

    def body(q_ref, k_ref, v_ref, o_ref, l_ref, do_ref, dq_ref, dk_ref, dv_ref):
        i = pl.program_id(1)

        @pl.when(i == 0)
        def _():
            dk_ref[...] = jnp.zeros_like(dk_ref)
            dv_ref[...] = jnp.zeros_like(dv_ref)

        k, v = k_ref[...], v_ref[...]
        lse_all = l_ref[...]
        for g in range(4):
            cols = slice(g * HEAD_B, (g + 1) * HEAD_B)
            q, dob = q_ref[:, cols], do_ref[:, cols]
            delta = jnp.sum(dob.astype(F32) * o_ref[:, cols].astype(F32), axis=-1, keepdims=True)
            s = lax.dot_general(q, k, (NT, ((), ())), preferred_element_type=F32)
            p = jnp.exp2(s - lse_all[:, g:g + 1])
            dp = lax.dot_general(dob, v, (NT, ((), ())), preferred_element_type=F32)
            ds = (p * (dp - delta)).astype(BF16)
            dq_ref[:, cols] = jnp.dot(ds, k, preferred_element_type=F32).astype(dq_ref.dtype)
            dk_ref[...] += lax.dot_general(ds, q, (TN, ((), ())), preferred_element_type=F32)
            dv_ref[...] += lax.dot_general(p.astype(BF16), dob, (TN, ((), ())), preferred_element_type=F32)

    return pl.pallas_call(
        body, out_shape=[jax.ShapeDtypeStruct((T, 2 * GW), BF16), jax.ShapeDtypeStruct((T, 2 * HEAD_B), F32),
                         jax.ShapeDtypeStruct((T, 2 * HEAD_B), F32)],
        grid=(2, T // QB_B),
        in_specs=[_bs((QB_B, GW), lambda kv, i: (i, kv)), _bs((T, HEAD_B), lambda kv, i: (0, kv)),
                  _bs((T, HEAD_B), lambda kv, i: (0, B_V // HEAD_B + kv)), _bs((QB_B, GW), lambda kv, i: (i, kv)),
                  _bs((None, QB_B, HEAD_B), lambda kv, i: (kv, i, 0)), _bs((QB_B, GW), lambda kv, i: (i, kv))],
        out_specs=[_bs((QB_B, GW), lambda kv, i: (i, kv)), _bs((T, HEAD_B), lambda kv, i: (0, kv)),
                   _bs((T, HEAD_B), lambda kv, i: (0, kv))],
        compiler_params=_params("parallel", "arbitrary"), name="b_bwd")(qn, kn, proj, o, lse, do)


def _local_step(x, target, small, get_w, put_g, deps=()):
    T, D = x.shape
    gs = {}

    (x1, h2), ffn1_saved = _ffn_fwd("ffn1", x, small["ffn1_norm"], get_w, deps, tail_ins=[small["mix_norm"]],
                                    tail_fn=lambda y, g: (y, _norm_fwd(y, g)), tail_outs=(F32, BF16))
    w_in = get_w("w_in", h2)
    nq = w_in.shape[2]
    tpq = nq // WIDTH_A

    def proj_tile(j, k):
        c = j * tpq + k
        return jnp.where(c < 3 * len(DILATIONS), (c % 3) * 3 + c // 3, c)

    proj = _mm("mix_in", (4, tpq),
               [(h2, _resident((T, D), lambda j, k: (0, 0)), w_in, _bs((None, D, WIDTH_A), lambda j, k: (j, 0, k)))],
               jax.ShapeDtypeStruct((T, IN_WIDTH), BF16), _bs((T, WIDTH_A), lambda j, k: (0, proj_tile(j, k))), NN)

    bias = _bias_tiles(small["rel_bias"])
    a_outs, a_lses = [], []
    for grp, d in enumerate(DILATIONS):
        o, l = _dil_fwd(proj, bias[grp * HEADS_A:(grp + 1) * HEADS_A], grp, d)
        a_outs.append(o)
        a_lses.append(l)
    o_a = _combine_fwd(a_outs, a_lses)

    cos, sin = _rope_tables(T)
    qn = _qk_fwd("b_qnorm", proj, B_Q, 8, small["q_norm"], cos, sin, out_scale=QK_SCALE_LOG2)
    kn = _qk_fwd("b_knorm", proj, B_K, 2, small["k_norm"], cos, sin)
    o_b, lse_b = _gqa_fwd(qn, kn, proj)

    wa, wb3, w_out3 = get_w("w_branch_a", o_b), get_w("w_branch_b", o_b).reshape(1, D, D), get_w("w_out", o_b).reshape(1, D, D)
    t_a = _mm_cols("mix_branch_a", o_a, wa, tm=512, tn=256, out_dtype=BF16, cat=True)
    t_b = _mm_cols("mix_branch_b", o_b, wb3, tm=512, tn=512, out_dtype=BF16, cat=True)
    bg_a, bg_b = small["b_gate"][:, :D], small["b_gate"][:, D:]

    def merge(ta, tb, ga, gb_, ba, bb):
        sa, sb = _sigmoid(ga.astype(F32) + ba), _sigmoid(gb_.astype(F32) + bb)
        return sa * ta.astype(F32) + sb * tb.astype(F32)

    gate_ins = [_tiled(proj, D, G_A // D), _tiled(proj, D, G_B // D), _whole(bg_a), _whole(bg_b)]
    (merged,) = _ew("mix_merge", merge, [_tiled(t_a), _tiled(t_b)] + gate_ins, [(BF16, D)], n_rows=T, rows=512)
    def mix_tail(acc, xv, g):
        y = xv + acc
        return y, _norm_fwd(y, g)

    row = _bs((512, D), lambda i: (i, 0))
    x2, hn2 = _mm("mix_out", (T // 512,), [(merged, row, w_out3, _resident((None, D, D), lambda i: (0, 0, 0)))],
                  [jax.ShapeDtypeStruct((T, D), F32), jax.ShapeDtypeStruct((T, D), BF16)], [row, row], NN,
                  extras=[(x1, row), (small["ffn2_norm"], _row_spec(small["ffn2_norm"], 512))], epilogue=mix_tail)

    def head(xv, g, tv):
        r = _rstd(xv)
        xh = xv * r
        e = xh * g - tv
        dy = e * (1.0 / D)
        dxh = dy * g
        dx = r * (dxh - xh * jnp.mean(dxh * xh, axis=-1, keepdims=True))
        return dx, 0.5 * dx, _colsum(e * e) * (0.5 / D), _colsum(dy * xh)

    (dx3, dx3_half, loss_cols, g_final), ffn2_saved = _ffn_fwd(
        "ffn2", x2, small["ffn2_norm"], get_w, h=hn2, tail_ins=[small["final_norm"].reshape(1, D), target], tail_fn=head,
        tail_outs=(F32, BF16), tail_reds=(D, D))
    gs["final_norm"] = g_final.reshape(D)

    dx2, _, dmix, gs["ffn2_norm"] = _ffn_bwd("ffn2", x2, small["ffn2_norm"], get_w, put_g, ffn2_saved, dx3, dx3_half,
                                             also_bf16=True)
    g_out = _mm_wgrad("mix_bwd_dwout", merged, dmix, a_cols=D // 4, b_cols=None, tm=256, tn=512, J=4).reshape(D, D)
    dmerged = _mm_rows_t("mix_bwd_dmerged", dmix, w_out3, tm=512, out_dtype=BF16).reshape(T, D)

    def merge_bwd(dm, ta, tb, ga, gb_, ba, bb):
        dm, ta, tb = dm.astype(F32), ta.astype(F32), tb.astype(F32)
        sa, sb = _sigmoid(ga.astype(F32) + ba), _sigmoid(gb_.astype(F32) + bb)
        dga, dgb = dm * ta * sa * (1.0 - sa), dm * tb * sb * (1.0 - sb)
        return dm * sa, dm * sb, dga, dgb, _colsum(dga), _colsum(dgb)

    dta, dtb, dga, dgb, dba, dbb = _ew("mix_bwd_merge", merge_bwd, [_tiled(dmerged), _tiled(t_a), _tiled(t_b)] + gate_ins,
                                       [(BF16, D)] * 4, n_rows=T, rows=256, reds=(D, D))
    gs["b_gate"] = jnp.concatenate([dba.reshape(1, D), dbb.reshape(1, D)], axis=1)

    g_a = _mm_wgrad("mix_bwd_dwa", o_a, dta, a_cols=None, b_cols=D // 4, tm=WIDTH_A, tn=256, J=4)
    g_b = _mm_wgrad("mix_bwd_dwb", o_b, dtb, a_cols=D // 4, b_cols=None, tm=256, tn=512, J=4).reshape(D, D)
    deps = put_g({"w_out": g_out, "w_branch_a": g_a, "w_branch_b": g_b})
    do_a = _mm("mix_bwd_doa", (T // 1024,),
               [(dta, _bs((1024, D // 4), lambda i, j=j: (i, j)), wa, _bs((None, WIDTH_A, D // 4), lambda i, j=j: (j, 0, 0)))
                for j in range(4)],
               jax.ShapeDtypeStruct((T, WIDTH_A), BF16), _bs((1024, WIDTH_A), lambda i: (i, 0)), NT, deps=deps)
    do_b = _mm_rows_t("mix_bwd_dob", dtb, wb3, tm=512, out_dtype=BF16).reshape(T, D)

    dqn, dkn, dv_b = _gqa_bwd(qn, kn, proj, o_b, lse_b, do_b)
    dq_b, gs["q_norm"] = _qk_bwd("b_bwd_qnorm", dqn, proj, B_Q, 8, small["q_norm"], cos, sin, in_scale=HEAD_B ** -0.5)
    dk_b, gs["k_norm"] = _qk_bwd("b_bwd_knorm", dkn, proj, B_K, 2, small["k_norm"], cos, sin, in_scale=1.0 / LOG2_E)

    do_groups, c_groups = _combine_bwd(do_a, a_outs, a_lses)
    dqs, dks, dvs, dbs = [], [], [], []
    for grp, d in enumerate(DILATIONS):
        dq, dk, dv, db = _dil_bwd(proj, bias[grp * HEADS_A:(grp + 1) * HEADS_A], do_groups[grp], a_lses[grp], c_groups[grp], grp, d)
        dqs.append(dq), dks.append(dk), dvs.append(dv), dbs.append(db)
    gs["rel_bias"] = _bias_grad(jnp.concatenate(dbs, axis=0))

    dproj = jnp.concatenate([p.astype(BF16) for p in dqs + dks + dvs + [dq_b, dk_b, dv_b, dga, dgb]], axis=1)
    nq = w_in.shape[2]
    g_in = _mm("mix_bwd_dwin", (4, tpq),
               [(h2, _resident((T, D), lambda j, k: (0, 0)), dproj, _bs((T, WIDTH_A), lambda j, k: (0, j * tpq + k)))],
               jax.ShapeDtypeStruct((4, D, nq), BF16), _bs((None, D, WIDTH_A), lambda j, k: (j, 0, k)), TN)
    deps = put_g({"w_in": g_in})
    dx1, dx1_half, gs["mix_norm"] = _dh_norm_bwd(
        "mix_bwd_dh", 256,
        [(dproj, _bs((256, nq), lambda i, j=j: (i, j)), w_in, _resident((None, D, nq), lambda i, j=j: (j, 0, 0))) for j in range(4)],
        NT, x1, small["mix_norm"], dx2, deps)

    dx0, _, gs["ffn1_norm"] = _ffn_bwd("ffn1", x, small["ffn1_norm"], get_w, put_g, ffn1_saved, dx1, dx1_half)
    return loss_cols, dx0, gs


def _position():
    return lax.axis_index("x"), lax.axis_index("y"), lax.axis_index("c")


def _any_specs(n):
    return [pl.BlockSpec(memory_space=pl.ANY)] * n


HBM_SPEC = pl.BlockSpec(memory_space=pltpu.HBM)
SEM_SPEC = pl.BlockSpec(memory_space=pltpu.SEMAPHORE)
DATAFLOW_EFFECT = pltpu.SideEffectType.DATAFLOW_SIDE_EFFECTING
N_PEER_CHIPS = 3
LANES = 128


def _quarter_copies(srcs, lands, send_sems, recv_sems, scatter):
    x, y, c = _position()
    me = 2 * x + y
    peers = [(1 - x, y, c), (x, 1 - y, c), (1 - x, 1 - y, c)]
    copies = []
    for src, land, send, recv in zip(srcs, lands, send_sems, recv_sems):
        half = land.shape[1] // 2
        mine = land.at[me, pl.ds(c * half, half)]
        for p, (px, py, pc) in enumerate(peers):
            copies.append(pltpu.make_async_remote_copy(
                src_ref=src.at[2 * px + py] if scatter else mine, dst_ref=land.at[me] if scatter else mine,
                send_sem=send.at[p], recv_sem=recv.at[p], device_id=(px, py, pc), device_id_type=MESH))
    return copies


def _fill_from_sibling(name, stacks):
    n = len(stacks)

    def body(*refs):
        outs = refs[n:2 * n]
        send_sems, recv_sems = refs[2 * n:]
        x, y, c = _position()
        copies = []
        for i, ref in enumerate(outs):
            half = ref.shape[1] // 2
            rows = pl.ds(c * half, half)
            for p, k in enumerate((2 * (1 - x) + y, 2 * x + (1 - y), 2 * (1 - x) + (1 - y))):
                cp = pltpu.make_async_remote_copy(ref.at[k, rows], ref.at[k, rows], send_sems.at[3 * i + p], recv_sems.at[3 * i + p],
                                                  device_id=(x, y, 1 - c), device_id_type=MESH)
                cp.start()
                copies.append(cp)
        for cp in copies:
            cp.wait()

    return pl.pallas_call(
        body, out_shape=[jax.ShapeDtypeStruct(s.shape, s.dtype) for s in stacks],
        in_specs=_any_specs(n), out_specs=_any_specs(n), input_output_aliases={i: i for i in range(n)},
        scratch_shapes=[pltpu.SemaphoreType.DMA((N_PEER_CHIPS * n,)), pltpu.SemaphoreType.DMA((N_PEER_CHIPS * n,))],
        compiler_params=pltpu.CompilerParams(has_side_effects=True), name=name)(*stacks)


def _exchange_start(name, srcs, lands, scatter):
    n = len(lands)
    arrays = list(lands) if srcs is None else list(srcs) + list(lands)
    k = len(arrays)

    def body(*refs):
        land_refs = refs[k - n:k]
        send_sems, recv_sems = refs[k:k + n], refs[k + n:k + 2 * n]
        token = refs[2 * k + 2 * n]
        for cp in _quarter_copies(refs[:n], land_refs, send_sems, recv_sems, scatter):
            cp.start()
        token[...] = jnp.zeros_like(token)

    sem = pltpu.SemaphoreType.DMA((N_PEER_CHIPS,))
    out_shape = [sem] * (2 * n) + [pltpu.HBM(a.shape, a.dtype) for a in arrays] + [jax.ShapeDtypeStruct((8, LANES), F32)]
    res = pl.pallas_call(
        body, name=name, out_shape=out_shape, in_specs=[HBM_SPEC] * k,
        out_specs=[SEM_SPEC] * (2 * n) + [HBM_SPEC] * k + [pl.BlockSpec(memory_space=pltpu.VMEM)],
        input_output_aliases={i: 2 * n + i for i in range(k)},
        compiler_params=pltpu.CompilerParams(has_side_effects=DATAFLOW_EFFECT),
    )(*[pltpu.with_memory_space_constraint(a, pltpu.HBM) for a in arrays])
    thru = res[2 * n:2 * n + k]
    return res[:n], res[n:2 * n], (None if srcs is None else thru[:n]), thru[k - n:], res[2 * n + k]


def _exchange_wait(name, srcs, lands, send_sems, recv_sems, after, scatter):
    n = len(lands)
    arrays = list(lands) if srcs is None else list(srcs) + list(lands)
    k = len(arrays)

    def body(*refs):
        sends, recvs = refs[k:k + n], refs[k + n:k + 2 * n]
        for cp in _quarter_copies(refs[:n], refs[k - n:k], sends, recvs, scatter):
            cp.wait_send()
            cp.wait_recv()

    res = pl.pallas_call(
        body, name=name, out_shape=[pltpu.HBM(a.shape, a.dtype) for a in arrays],
        in_specs=[HBM_SPEC] * k + [SEM_SPEC] * (2 * n) + [pl.BlockSpec(memory_space=pl.ANY)],
        out_specs=[HBM_SPEC] * k, input_output_aliases={i: i for i in range(k)},
        compiler_params=pltpu.CompilerParams(has_side_effects=DATAFLOW_EFFECT),
    )(*arrays, *send_sems, *recv_sems, after)
    return res[k - n:]


def _own_slot(stack_shape, own, dtype):
    me = 2 * lax.axis_index("x") + lax.axis_index("y")
    return lax.dynamic_update_slice(lax.empty(stack_shape, dtype), own[None], (me,) + (0,) * own.ndim)


def _swap_with_sibling(parts):
    n = len(parts)

    def body(*refs):
        ins, outs = refs[:n], refs[n:2 * n]
        send_sems, recv_sems = refs[2 * n:]
        x, y, c = _position()
        copies = []
        for i in range(n):
            cp = pltpu.make_async_remote_copy(ins[i], outs[i], send_sems.at[i], recv_sems.at[i],
                                              device_id=(x, y, 1 - c), device_id_type=MESH)
            cp.start()
            copies.append(cp)
        for cp in copies:
            cp.wait()

    return pl.pallas_call(
        body, out_shape=[jax.ShapeDtypeStruct(s.shape, s.dtype) for s in parts],
        in_specs=_any_specs(n), out_specs=_any_specs(n),
        scratch_shapes=[pltpu.SemaphoreType.DMA((n,)), pltpu.SemaphoreType.DMA((n,))],
        compiler_params=pltpu.CompilerParams(has_side_effects=True), name="swap_with_sibling")(*parts)


def _allreduce_small(buf):
    R, C = buf.shape
    flips = [(fx, fy, fc) for fx in (0, 1) for fy in (0, 1) for fc in (0, 1)][1:]

    def body(in_ref, out_ref, land_ref, send_sems, recv_sems):
        x, y, c = _position()
        me = 4 * x + 2 * y + c
        copies = []
        for k, (fx, fy, fc) in enumerate(flips):
            px, py, pc = (1 - x if fx else x), (1 - y if fy else y), (1 - c if fc else c)
            cp = pltpu.make_async_remote_copy(in_ref, land_ref.at[me], send_sems.at[k], recv_sems.at[k],
                                              device_id=(px, py, pc), device_id_type=MESH)
            cp.start()
            copies.append(cp)
        land_ref[me] = in_ref[...]
        for cp in copies:
            cp.wait()
        acc = land_ref[0]
        for k in range(1, 8):
            acc = acc + land_ref[k]
        out_ref[...] = acc

    return pl.pallas_call(
        body, out_shape=jax.ShapeDtypeStruct((R, C), F32),
        in_specs=[pl.BlockSpec(memory_space=pltpu.VMEM)], out_specs=pl.BlockSpec(memory_space=pltpu.VMEM),
        scratch_shapes=[pltpu.VMEM((8, R, C), F32), pltpu.SemaphoreType.DMA((7,)), pltpu.SemaphoreType.DMA((7,))],
        compiler_params=pltpu.CompilerParams(has_side_effects=True), name="allreduce_small")(buf)


def _adamw_math(w, g, m, v):
    m2 = ADAM_B1 * m + (1.0 - ADAM_B1) * g
    v2 = ADAM_B2 * v + (1.0 - ADAM_B2) * (g * g)
    m_hat = m2 / (1.0 - ADAM_B1 ** ADAM_STEP)
    v_hat = v2 / (1.0 - ADAM_B2 ** ADAM_STEP)
    delta = -ADAM_LR * (m_hat / (jnp.sqrt(v_hat) + ADAM_EPS) + ADAM_WD * w)
    return delta, m2, v2


def _adamw_big(name, w, m, v, part_mine, part_sibling):
    R, C = w.shape
    rows = 256 if R % 256 == 0 else R // 2 if (R // 2) % 8 == 0 else R

    def fn(wv, mv, vv, a, b):
        g = a + b
        return (g,) + _adamw_math(wv, g, mv, vv)

    return _ew(name, fn, [_tiled(w), _tiled(m), _tiled(v), _tiled(part_mine), _tiled(part_sibling)], [(F32, C)] * 4, n_rows=R, rows=rows)


def _sum_four(name, stack):
    _, R, C = stack.shape
    rows = 256 if R % 256 == 0 else R // 2 if (R // 2) % 8 == 0 else R
    flat = stack.reshape(4 * R, C)
    nrb = R // rows

    def fn(a, b, c, d):
        return ((a.astype(F32) + b.astype(F32)) + c.astype(F32)) + d.astype(F32)

    (out,) = _ew(name, fn, [_tiled(flat, None, 0, k * nrb) for k in range(4)], [(F32, C)], n_rows=R, rows=rows)
    return out


BIG = ("ffn1_w1", "ffn1_w3", "ffn1_w2", "w_in", "w_branch_a", "w_branch_b", "w_out", "ffn2_w1", "ffn2_w3", "ffn2_w2")
SMALL = ("ffn1_norm", "mix_norm", "b_gate", "q_norm", "k_norm", "rel_bias", "ffn2_norm", "final_norm")
ORDER = ("ffn1_norm", "ffn1_w1", "ffn1_w3", "ffn1_w2", "mix_norm", "w_in", "b_gate", "q_norm", "k_norm", "rel_bias",
         "w_branch_a", "w_branch_b", "w_out", "ffn2_norm", "ffn2_w1", "ffn2_w3", "ffn2_w2", "final_norm")
TRANSPOSED = ("ffn1_w1", "ffn1_w3", "ffn2_w1", "ffn2_w3")
GATHER_GROUPS = (("ffn1_w1", "ffn1_w3"), ("ffn1_w2",), ("w_in",), ("w_branch_a", "w_branch_b", "w_out"),
                 ("ffn2_w1", "ffn2_w3", "ffn2_w2"))


def _pack_small(d):
    rows = []
    for n in SMALL:
        flat = d[n].reshape(-1)
        pad = (-flat.shape[0]) % LANES
        rows.append(jnp.pad(flat, (0, pad)).reshape(-1, LANES))
    buf = jnp.concatenate(rows, axis=0)
    return jnp.pad(buf, ((0, (-buf.shape[0]) % 8), (0, 0)))


def _unpack_small(buf, like):
    out, r = {}, 0
    for n in SMALL:
        size = like[n].size
        nr = -(-size // LANES)
        out[n] = buf[r:r + nr].reshape(-1)[:size].reshape(like[n].shape)
        r += nr
    return out


def kernel(x, ffn1_norm, ffn1_w1, ffn1_w3, ffn1_w2, mix_norm, w_in, b_gate, q_norm, k_norm, rel_bias, w_branch_a, w_branch_b, w_out, ffn2_norm, ffn2_w1, ffn2_w3, ffn2_w2, final_norm, loss_target, m_ffn1_norm, m_ffn1_w1, m_ffn1_w3, m_ffn1_w2, m_mix_norm, m_w_in, m_b_gate, m_q_norm, m_k_norm, m_rel_bias, m_w_branch_a, m_w_branch_b, m_w_out, m_ffn2_norm, m_ffn2_w1, m_ffn2_w3, m_ffn2_w2, m_final_norm, v_ffn1_norm, v_ffn1_w1, v_ffn1_w3, v_ffn1_w2, v_mix_norm, v_w_in, v_b_gate, v_q_norm, v_k_norm, v_rel_bias, v_w_branch_a, v_w_branch_b, v_w_out, v_ffn2_norm, v_ffn2_w1, v_ffn2_w3, v_ffn2_w2, v_final_norm):
    given = dict(locals())
    w = {n: given[n] for n in ORDER}
    m = {n: given["m_" + n] for n in ORDER}
    v = {n: given["v_" + n] for n in ORDER}
    T, D = x.shape[1], x.shape[2]

    def stored(a, n):
        a = a.reshape(a.shape[1:])
        return a.T if n in TRANSPOSED else a

    def returned(a, n):
        return (a.T if n in TRANSPOSED else a).reshape(w[n].shape)

    quarter = {n: stored(w[n], n) for n in BIG}
    send, recv, _, land_thru, token = _exchange_start(
        "gather_start", None, [_own_slot((4,) + quarter[n].shape, quarter[n].astype(BF16), BF16) for n in BIG], scatter=False)
    index = {n: i for i, n in enumerate(BIG)}
    ready = {}

    def get_w(name, after):
        if name not in ready:
            group = next(g for g in GATHER_GROUPS if name in g)
            ids = [index[n] for n in group]
            stacks = _exchange_wait("gather_wait_" + group[0], None, [land_thru[i] for i in ids],
                                    [send[i] for i in ids], [recv[i] for i in ids], after, scatter=False)
            stacks = _fill_from_sibling("gather_fill_" + group[0], stacks)
            for n, st in zip(group, stacks):
                ready[n] = st.reshape(D, D) if n in ("w_branch_b", "w_out") else st
        return ready[name]

    me = 2 * lax.axis_index("x") + lax.axis_index("y")
    in_flight = []

    def put_g(grads):
        names = list(grads)
        stacks = [grads[n].reshape((4,) + quarter[n].shape) for n in names]
        lands = [_own_slot(s.shape, lax.dynamic_index_in_dim(s, me, 0, keepdims=False), BF16) for s in stacks]
        started = _exchange_start("scatter_start_" + names[0], stacks, lands, scatter=True)
        in_flight.append((names,) + tuple(started[:4]))
        return [started[4]]

    small = {n: w[n] for n in SMALL}
    loss_cols, grad_x, gs = _local_step(x.reshape(T, D), loss_target.reshape(T, D), small, get_w, put_g, deps=[token])
    loss = lax.psum(jnp.sum(loss_cols), ("x", "y", "c"))

    landed = {}
    for names, s_sem, r_sem, srcs, lands in in_flight:
        got = _exchange_wait("scatter_wait_" + names[0], srcs, lands, s_sem, r_sem, grad_x, scatter=True)
        landed.update(zip(names, got))
    partial = [_sum_four(f"sum4_{n}", landed[n]) for n in BIG]
    other = _swap_with_sibling(partial)
    grads, deltas, new_m, new_v = {}, {}, {}, {}
    for n, mine, theirs in zip(BIG, partial, other):
        res = _adamw_big(f"adamw_{n}", quarter[n], stored(m[n], n), stored(v[n], n), mine, theirs)
        grads[n], deltas[n], new_m[n], new_v[n] = [returned(r, n) for r in res]

    gs = {n: gs[n].reshape(w[n].shape) for n in SMALL}
    g_small = _allreduce_small(_pack_small(gs))
    packed = [_pack_small({n: d[n] for n in SMALL}) for d in (w, m, v)]
    R = g_small.shape[0]
    res = _ew("adamw_small", lambda wv, mv, vv, g: (g,) + _adamw_math(wv, g, mv, vv),
              [_tiled(packed[0]), _tiled(packed[1]), _tiled(packed[2]), _tiled(g_small)], [(F32, LANES)] * 4, n_rows=R, rows=R)
    for d, buf in zip((grads, deltas, new_m, new_v), res):
        d.update(_unpack_small(buf, w))

    return (loss, grad_x.reshape(x.shape), *[grads[n] for n in ORDER], *[deltas[n] for n in ORDER],
            *[new_m[n] for n in ORDER], *[new_v[n] for n in ORDER])
```

```python
import functools
import math

import numpy as np
import jax
import jax.numpy as jnp
from jax import lax
from jax.experimental import pallas as pl
from jax.experimental.pallas import tpu as pltpu

F32 = jnp.float32
BF16 = jnp.bfloat16
MESH = pl.DeviceIdType.MESH

NEG_INF = -1e30
EPS = 1e-6
GRID_W = 64
ROPE_THETA = 10000.0
DILATIONS = (1, 4, 16)
BAND_HALF = 64
HEAD_A = 64
HEADS_A = 8
WIDTH_A = HEADS_A * HEAD_A
HEAD_B = 128
LOG2_E = math.log2(math.e)
QK_SCALE_LOG2 = HEAD_B ** -0.5 * LOG2_E
N_BUCKETS = 32
MAX_DISTANCE = 1024
ADAM_LR, ADAM_B1, ADAM_B2, ADAM_EPS, ADAM_WD, ADAM_STEP = 0.001, 0.9, 0.999, 1e-08, 0.01, 10

A_Q, A_K, A_V = 0, 1536, 3072
B_Q, B_K, B_V = 4608, 5632, 5888
G_A, G_B = 6144, 7168
IN_WIDTH = 8192

VMEM_LIMIT_BYTES = 56 * 1024 * 1024
QB_A = 128
QB_B = 256


def _params(*sem):
    return pltpu.CompilerParams(dimension_semantics=sem, vmem_limit_bytes=VMEM_LIMIT_BYTES)


def _bs(shape, fn):
    return pl.BlockSpec(shape, fn)


def _resident(shape, fn):
    return pl.BlockSpec(shape, fn, pipeline_mode=pl.Buffered(1))


def _mm(name, grid, pairs, out_shape, out_spec, dims, *, extras=(), epilogue=None, deps=(), reds=()):
    n_pairs, n_extra, n_deps = len(pairs), len(extras), len(deps)
    operands = [p[0] for p in pairs] + [p[2] for p in pairs] + [e[0] for e in extras] + list(deps)
    in_specs = [p[1] for p in pairs] + [p[3] for p in pairs] + [e[1] for e in extras] + _any_specs(n_deps)
    single = not isinstance(out_shape, (list, tuple))
    out_shapes = [out_shape] if single else list(out_shape)
    out_specs = [out_spec] if single else list(out_spec)
    n_out = len(out_shapes)
    out_shapes += [jax.ShapeDtypeStruct((1, w), F32) for w in reds]
    out_specs += [_bs((1, w), lambda *_: (0, 0)) for w in reds]

    def body(*refs):
        a_refs, b_refs = refs[:n_pairs], refs[n_pairs:2 * n_pairs]
        e_refs = refs[2 * n_pairs:2 * n_pairs + n_extra]
        o_refs = refs[2 * n_pairs + n_extra + n_deps:]
        acc = None
        for a_ref, b_ref in zip(a_refs, b_refs):
            t = lax.dot_general(a_ref[...], b_ref[...], (dims, ((), ())), preferred_element_type=F32)
            acc = t if acc is None else acc + t
        vals = acc if epilogue is None else epilogue(acc, *[e[...] for e in e_refs])
        if not isinstance(vals, (list, tuple)):
            vals = (vals,)
        for o_ref, v in zip(o_refs[:n_out], vals[:n_out]):
            o_ref[...] = v.astype(o_ref.dtype)
        if reds:
            first = functools.reduce(jnp.logical_and, [pl.program_id(ax) == 0 for ax in range(len(grid))])
            for r_ref, v in zip(o_refs[n_out:], vals[n_out:]):
                @pl.when(first)
                def _(r_ref=r_ref):
                    r_ref[...] = jnp.zeros_like(r_ref)
                r_ref[...] += v

    sem = ["arbitrary" if reds else "parallel"] * len(grid)
    res = pl.pallas_call(
        body, out_shape=out_shapes, grid=grid, in_specs=in_specs, out_specs=out_specs,
        compiler_params=_params(*sem), name=name)(*operands)
    return res[0] if (single and not reds) else res


NN = ((1,), (0,))
NT = ((1,), (1,))
TN = ((0,), (0,))


def _mm_cols(name, a, w, *, tm, tn, out_dtype, cat, extras=(), epilogue=None):
    M, K = a.shape
    J, _, n = w.shape
    tn = min(tn, n)
    nb = n // tn
    if cat:
        shape, spec = (M, J * n), _bs((tm, tn), lambda j, i, k: (i, j * nb + k))
    else:
        shape, spec = (J, M, n), _bs((None, tm, tn), lambda j, i, k: (j, i, k))
    ex = [(e, _bs((tm, tn), lambda j, i, k: (i, j * nb + k))) for e in extras]
    return _mm(name, (J, M // tm, nb),
               [(a, _bs((tm, K), lambda j, i, k: (i, 0)), w, _bs((None, K, tn), lambda j, i, k: (j, 0, k)))],
               jax.ShapeDtypeStruct(shape, out_dtype), spec, NN, extras=ex, epilogue=epilogue)


def _mm_rows_t(name, a, w, *, tm, out_dtype):
    M, N = a.shape
    J, f, _ = w.shape
    return _mm(name, (J, M // tm),
               [(a, _bs((tm, N), lambda j, i: (i, 0)), w, _bs((None, f, N), lambda j, i: (j, 0, 0)))],
               jax.ShapeDtypeStruct((J, M, f), out_dtype), _bs((None, tm, f), lambda j, i: (j, i, 0)), NT)


def _mm_wgrad(name, a, b, *, a_cols, b_cols, tm, tn, J):
    def pick(arr, cols, t):
        if arr.ndim == 3:
            T, c = arr.shape[1], arr.shape[2]
            t = min(t, c)
            return T, c, t, (lambda sel: _bs((None, T, t), lambda j, i, k: (j, 0, sel(i, k))))
        T = arr.shape[0]
        c = arr.shape[1] if cols is None else cols
        t = min(t, c)
        per = c // t
        if cols is None:
            if per == 1:
                return T, c, t, (lambda sel: _resident((T, t), lambda j, i, k: (0, 0)))
            return T, c, t, (lambda sel: _bs((T, t), lambda j, i, k: (0, sel(i, k))))
        return T, c, t, (lambda sel: _bs((T, t), lambda j, i, k: (0, j * per + sel(i, k))))
    _, ca, tm, mk_a = pick(a, a_cols, tm)
    _, cb, tn, mk_b = pick(b, b_cols, tn)
    return _mm(name, (J, ca // tm, cb // tn),
               [(a, mk_a(lambda i, k: i), b, mk_b(lambda i, k: k))],
               jax.ShapeDtypeStruct((J, ca, cb), BF16), _bs((None, tm, tn), lambda j, i, k: (j, i, k)), TN)


def _tiled(arr, width=None, col=0, rowblk=0):
    return ("t", arr, arr.shape[1] if width is None else width, col, rowblk)


def _table(arr):
    return ("f", arr)


def _whole(arr):
    return ("w", arr)


def _ew(name, fn, ins, outs, *, n_rows, rows, reds=(), ncols=1, deps=()):
    nrb = n_rows // rows
    n_deps = len(deps)
    operands, in_specs = [], []
    for spec in ins:
        if spec[0] == "t":
            _, arr, width, col, rowblk = spec
            step = 1 if ncols > 1 else 0
            in_specs.append(_bs((rows, width), lambda c, i, col=col, rowblk=rowblk, step=step: (rowblk + i, col + c * step)))
        elif spec[0] == "f":
            arr = spec[1]
            in_specs.append(_bs((rows, arr.shape[1]), lambda c, i: (i, 0)))
        else:
            arr = spec[1]
            nd = arr.ndim
            if nd == 3:
                in_specs.append(_bs((None,) + arr.shape[1:], lambda c, i: (c, 0, 0)))
            else:
                in_specs.append(_bs(arr.shape, lambda c, i, nd=nd: (0,) * nd))
        operands.append(arr)
    out_shapes = [jax.ShapeDtypeStruct((n_rows, ncols * w), dt) for dt, w in outs]
    out_specs = [_bs((rows, w), lambda c, i: (i, c)) for _, w in outs]
    out_shapes += [jax.ShapeDtypeStruct((ncols, 1, w), F32) for w in reds]
    out_specs += [_bs((None, 1, w), lambda c, i: (c, 0, 0)) for w in reds]
    n_in, n_out, n_red = len(ins), len(outs), len(reds)
    operands += list(deps)
    in_specs += _any_specs(n_deps)

    def body(*refs):
        vals = fn(*[r[...] for r in refs[:n_in]])
        if not isinstance(vals, (tuple, list)):
            vals = (vals,)
        o_refs = refs[n_in + n_deps:]
        for o_ref, v in zip(o_refs[:n_out], vals[:n_out]):
            o_ref[...] = v.astype(o_ref.dtype)
        if n_red:
            i = pl.program_id(1)
            for r_ref, v in zip(o_refs[n_out:], vals[n_out:]):
                @pl.when(i == 0)
                def _(r_ref=r_ref):
                    r_ref[...] = jnp.zeros_like(r_ref)
                r_ref[...] += v

    res = pl.pallas_call(
        body, out_shape=out_shapes, grid=(ncols, nrb), in_specs=in_specs, out_specs=out_specs,
        compiler_params=_params("parallel", "arbitrary" if n_red else "parallel"), name=name)(*operands)
    return res


def _colsum(v):
    return jnp.sum(v, axis=0, keepdims=True)


def _rstd(x):
    return lax.rsqrt(jnp.mean(x * x, axis=-1, keepdims=True) + EPS)


def _sigmoid(x):
    return 1.0 / (1.0 + jnp.exp(-x))


def _norm_fwd(x, g):
    return x * _rstd(x) * g


def _norm_bwd(x, g, dy):
    r = _rstd(x)
    xh = x * r
    dxh = dy * g
    dx = r * (dxh - xh * jnp.mean(dxh * xh, axis=-1, keepdims=True))
    return dx, dy * xh


def _row_spec(arr, rows):
    if arr.shape[0] == 1:
        return _bs(arr.shape, lambda i: (0, 0))
    return _bs((rows, arr.shape[1]), lambda i: (i, 0))


def _ffn_fwd(tag, x, gain, get_w, deps=(), *, h=None, tail_ins=(), tail_fn=None, tail_outs=(F32,), tail_reds=()):
    T, D = x.shape
    if h is None:
        (h,) = _ew(f"{tag}_norm", lambda xv, g: _norm_fwd(xv, g), [_tiled(x), _whole(gain)], [(BF16, D)], n_rows=T, rows=512,
                   deps=deps)
    w1, w3 = get_w(f"{tag}_w1", h), get_w(f"{tag}_w3", h)
    J, f, _ = w1.shape
    tm = 1024

    def up(h_ref, w1_ref, w3_ref, u_ref, g_ref, a_ref):
        hv = h_ref[...]
        u = lax.dot_general(hv, w1_ref[...], (NT, ((), ())), preferred_element_type=F32)
        g = lax.dot_general(hv, w3_ref[...], (NT, ((), ())), preferred_element_type=F32)
        u_ref[...] = u.astype(BF16)
        g_ref[...] = g.astype(BF16)
        a_ref[...] = (u * _sigmoid(u) * g).astype(BF16)

    slab = _bs((None, tm, f), lambda j, i: (j, i, 0))
    w_spec = _bs((None, f, D), lambda j, i: (j, 0, 0))
    u, g, a = pl.pallas_call(
        up, out_shape=[jax.ShapeDtypeStruct((J, T, f), BF16)] * 3, grid=(J, T // tm),
        in_specs=[_bs((tm, D), lambda j, i: (i, 0)), w_spec, w_spec], out_specs=[slab] * 3,
        compiler_params=_params("parallel", "parallel"), name=f"{tag}_up")(h, w1, w3)
    w2 = get_w(f"{tag}_w2", a)
    def tail(acc, xv, *rest):
        y = xv + 0.5 * acc
        return y if tail_fn is None else tail_fn(y, *rest)

    row = _bs((512, D), lambda i: (i, 0))
    res = _mm(f"{tag}_down", (T // 512,),
              [(a, _bs((None, 512, f), lambda i, j=j: (j, i, 0)), w2, _resident((None, f, D), lambda i, j=j: (j, 0, 0)))
               for j in range(J)],
              [jax.ShapeDtypeStruct((T, D), dt) for dt in tail_outs], [row] * len(tail_outs), NN,
              extras=[(x, row)] + [(t, _row_spec(t, 512)) for t in tail_ins], epilogue=tail, reds=tail_reds)
    return res, (h, u, g, a)


def _dh_norm_bwd(name, rows, pairs, dims, x, gain, dres, deps, also_bf16=False):
    T, D = x.shape

    def epilogue(dh, xv, gv, dr):
        dx, dgr = _norm_bwd(xv, gv, dh)
        dx = dx + dr
        return (dx, 0.5 * dx) + ((dx,) if also_bf16 else ()) + (_colsum(dgr),)

    dts = [F32, BF16] + ([BF16] if also_bf16 else [])
    row = _bs((rows, D), lambda i: (i, 0))
    return _mm(name, (T // rows,), pairs, [jax.ShapeDtypeStruct((T, D), dt) for dt in dts], [row] * len(dts), dims,
               extras=[(x, row), (gain, _row_spec(gain, rows)), (dres, row)], epilogue=epilogue, deps=deps, reds=(D,))


def _ffn_bwd(tag, x, gain, get_w, put_g, saved, dy, dy_half, also_bf16=False):
    h, u, g, a = saved
    T, D = x.shape
    w1, w3, w2 = [get_w(f"{tag}_{n}", dy_half) for n in ("w1", "w3", "w2")]
    J, f, _ = w1.shape
    dw2 = _mm_wgrad(f"{tag}_bwd_dw2", a, dy_half, a_cols=None, b_cols=None, tm=f, tn=D, J=J)
    deps = put_g({f"{tag}_w2": dw2})
    tm = 1024

    def up_bwd(dy_ref, w2_ref, u_ref, g_ref, *rest):
        du_ref, dg_ref = rest[-2:]
        da = lax.dot_general(dy_ref[...], w2_ref[...], (NT, ((), ())), preferred_element_type=F32)
        uv, gv = u_ref[...].astype(F32), g_ref[...].astype(F32)
        s = _sigmoid(uv)
        du_ref[...] = (da * gv * (s * (1.0 + uv * (1.0 - s)))).astype(BF16)
        dg_ref[...] = (da * (uv * s)).astype(BF16)

    slab = _bs((None, tm, f), lambda j, i: (j, i, 0))
    du, dg = pl.pallas_call(
        up_bwd, out_shape=[jax.ShapeDtypeStruct((J, T, f), BF16)] * 2, grid=(J, T // tm),
        in_specs=[_bs((tm, D), lambda j, i: (i, 0)), _bs((None, f, D), lambda j, i: (j, 0, 0)), slab, slab] + _any_specs(len(deps)),
        out_specs=[slab] * 2, compiler_params=_params("parallel", "parallel"), name=f"{tag}_bwd_up")(dy_half, w2, u, g, *deps)
    dw1 = _mm_wgrad(f"{tag}_bwd_dw1", du, h, a_cols=None, b_cols=None, tm=f, tn=D, J=J)
    dw3 = _mm_wgrad(f"{tag}_bwd_dw3", dg, h, a_cols=None, b_cols=None, tm=f, tn=D, J=J)
    deps = deps + put_g({f"{tag}_w1": dw1, f"{tag}_w3": dw3})
    pairs = []
    for j in range(J):
        a_spec = _bs((None, 512, f), lambda i, j=j: (j, i, 0))
        w_spec = _resident((None, f, D), lambda i, j=j: (j, 0, 0))
        pairs += [(du, a_spec, w1, w_spec), (dg, a_spec, w3, w_spec)]
    return _dh_norm_bwd(f"{tag}_bwd_dh", 512, pairs, NN, x, gain, dy, deps, also_bf16)


def _t5_bucket(rel):
    n = N_BUCKETS // 2
    max_exact = n // 2
    ret = jnp.where(rel > 0, n, 0)
    a = jnp.abs(rel)
    af = jnp.maximum(a, 1).astype(F32)
    large = max_exact + (jnp.log(af / max_exact) / math.log(MAX_DISTANCE / max_exact) * (n - max_exact)).astype(jnp.int32)
    large = jnp.minimum(large, n - 1)
    return ret + jnp.where(a < max_exact, a, large)


WIN_A = QB_A + 2 * BAND_HALF
WIN_SHIFTS = (0, BAND_HALF, 2 * BAND_HALF)


def _window_variant(n, nblk):
    return jnp.where(n == 0, 0, jnp.where(n == nblk - 1, 2, 1))


def _window_start(n, nblk):
    return pl.multiple_of(jnp.clip(n * QB_A - BAND_HALF, 0, nblk * QB_A - WIN_A), BAND_HALF)


def _band_steps(xp=jnp):
    qi = xp.arange(QB_A, dtype=xp.int32)[None, :, None]
    kj = xp.arange(WIN_A, dtype=xp.int32)[None, None, :]
    return kj - qi - xp.asarray(WIN_SHIFTS, dtype=xp.int32)[:, None, None]


def _bias_tiles(rel_bias):
    steps = _band_steps()
    buckets = jnp.stack([_t5_bucket(steps * d) for d in DILATIONS])
    inband = (jnp.abs(steps) <= BAND_HALF).astype(jnp.int32)
    n_groups = len(DILATIONS)

    def body(tab_ref, b_ref, m_ref, o_ref):
        hd = pl.program_id(0) * HEADS_A + pl.program_id(2)
        bkt = b_ref[...]
        acc = jnp.zeros(bkt.shape, F32)
        for b in range(N_BUCKETS):
            acc = jnp.where(bkt == b, tab_ref[b, hd], acc)
        o_ref[...] = jnp.where(m_ref[...] > 0, acc, NEG_INF)

    return pl.pallas_call(
        body, out_shape=jax.ShapeDtypeStruct((n_groups, 3, HEADS_A, QB_A, WIN_A), F32), grid=(n_groups, 3, HEADS_A),
        in_specs=[pl.BlockSpec(memory_space=pltpu.SMEM),
                  _bs((None, None, QB_A, WIN_A), lambda g, v, h: (g, v, 0, 0)),
                  _bs((None, QB_A, WIN_A), lambda g, v, h: (v, 0, 0))],
        out_specs=_bs((None, None, None, QB_A, WIN_A), lambda g, v, h: (g, v, h, 0, 0)),
        compiler_params=_params("parallel", "parallel", "parallel"), name="a_bias_tiles")(rel_bias, buckets, inband)


def _bias_grad(dbias):
    steps = _band_steps(np)
    inband = np.abs(steps) <= BAND_HALF
    present = []
    for d in DILATIONS:
        rel = steps * d
        a = np.abs(rel)
        large = 8 + (np.log(np.maximum(a, 1) / 8.0) / math.log(MAX_DISTANCE / 8.0) * 8).astype(np.int64)
        bk = np.where(rel > 0, 16, 0) + np.where(a < 8, a, np.minimum(large, 15))
        present.append([sorted(set(bk[v][inband[v]].tolist())) for v in range(3)])
    buckets = jnp.stack([_t5_bucket(_band_steps() * d) for d in DILATIONS])
    n_heads = len(DILATIONS) * HEADS_A

    def body(b_ref, d_ref, o_ref):
        row = lax.broadcasted_iota(jnp.int32, (N_BUCKETS, n_heads), 0)
        col = lax.broadcasted_iota(jnp.int32, (N_BUCKETS, n_heads), 1)
        out = jnp.zeros((N_BUCKETS, n_heads), F32)
        for grp in range(len(DILATIONS)):
            for hh in range(HEADS_A):
                hd = grp * HEADS_A + hh
                for b in sorted(set(sum(present[grp], []))):
                    tot = jnp.zeros((), F32)
                    for v in range(3):
                        if b in present[grp][v]:
                            tot = tot + jnp.sum(jnp.where(b_ref[grp, v] == b, d_ref[grp, v, hh], 0.0))
                    out = jnp.where((row == b) & (col == hd), tot, out)
        o_ref[...] = out

    return pl.pallas_call(
        body, out_shape=jax.ShapeDtypeStruct((N_BUCKETS, n_heads), F32),
        compiler_params=pltpu.CompilerParams(vmem_limit_bytes=VMEM_LIMIT_BYTES), name="a_bias_grad")(buckets, dbias)


def _lane_is_second_head(shape):
    return lax.broadcasted_iota(jnp.int32, shape, len(shape) - 1) >= HEAD_A


def _group_view(proj, grp, d):
    T = proj.shape[0]
    if d == 1:
        return proj, IN_WIDTH, grp * 3 * WIDTH_A
    part = proj[:, grp * 3 * WIDTH_A:(grp + 1) * 3 * WIDTH_A]
    return part.reshape(T // d, d * 3 * WIDTH_A), 3 * WIDTH_A, 0


def _stack_heads(v2, second):
    zero = jnp.zeros_like(v2)
    return jnp.concatenate([jnp.where(second, zero, v2), jnp.where(second, v2, zero)], axis=0)


def _unstack_heads(v, second):
    return jnp.where(second, v[QB_A:], v[:QB_A])


def _dil_fwd(proj, bias, grp, d):
    T = proj.shape[0]
    L = T // d
    nblk = L // QB_A
    pv, width, base = _group_view(proj, grp, d)
    cb, b0 = width // WIDTH_A, base // WIDTH_A
    W2 = 2 * HEAD_A
    scale = HEAD_A ** -0.5

    def body(q_ref, k_ref, v_ref, b_ref, o_ref, l_ref):
        win = pl.ds(_window_start(pl.program_id(1), nblk), WIN_A)
        second = _lane_is_second_head((QB_A, W2))
        for hp in range(HEADS_A // 2):
            cols = slice(hp * W2, (hp + 1) * W2)
            kw, vw = k_ref[win, cols], v_ref[win, cols]
            qs = _stack_heads(q_ref[:, cols], second)
            s = lax.dot_general(qs, kw, (NT, ((), ())), preferred_element_type=F32)
            s = s * scale + b_ref[2 * hp:2 * hp + 2].reshape(2 * QB_A, WIN_A)
            m = jnp.max(s, axis=-1, keepdims=True)
            p = jnp.exp(s - m)
            l = jnp.sum(p, axis=-1, keepdims=True)
            res = jnp.dot(p.astype(BF16), vw, preferred_element_type=F32) / l
            o_ref[:, cols] = _unstack_heads(res, second).astype(o_ref.dtype)
            l_ref[:, cols] = _unstack_heads(jnp.broadcast_to(m + jnp.log(l), (2 * QB_A, W2)), second)

    in_specs = [_bs((QB_A, WIDTH_A), lambda r, n: (n, r * cb + b0)),
                _bs((L, WIDTH_A), lambda r, n: (0, r * cb + b0 + 1)), _bs((L, WIDTH_A), lambda r, n: (0, r * cb + b0 + 2)),
                _bs((None, HEADS_A, QB_A, WIN_A), lambda r, n: (_window_variant(n, nblk), 0, 0, 0))]
    o, lse = pl.pallas_call(
        body, out_shape=[jax.ShapeDtypeStruct((L, d * WIDTH_A), BF16), jax.ShapeDtypeStruct((L, d * WIDTH_A), F32)],
        grid=(d, nblk), in_specs=in_specs,
        out_specs=[_bs((QB_A, WIDTH_A), lambda r, n: (n, r)), _bs((QB_A, WIDTH_A), lambda r, n: (n, r))],
        compiler_params=_params("parallel", "parallel"), name=f"a_fwd_d{d}")(pv, pv, pv, bias)
    return o.reshape(T, WIDTH_A), lse.reshape(T, WIDTH_A)


def _dil_bwd(proj, bias, do, lse, cterm, grp, d):
    T = proj.shape[0]
    L = T // d
    nblk = L // QB_A
    W2 = 2 * HEAD_A
    PPS = 2
    WS = PPS * W2
    pv, width, base = _group_view(proj, grp, d)
    cb, b0 = width // WS, base // WS
    ob = WIDTH_A // WS
    view = lambda a: a.reshape(L, d * WIDTH_A)
    scale = HEAD_A ** -0.5

    def body(q_ref, k_ref, v_ref, do_ref, l_ref, c_ref, b_ref, dq_ref, dk_ref, dv_ref, db_ref):
        r, n = pl.program_id(1), pl.program_id(2)

        @pl.when(n == 0)
        def _():
            dk_ref[...] = jnp.zeros_like(dk_ref)
            dv_ref[...] = jnp.zeros_like(dv_ref)

        @pl.when((n == 0) & (r == 0))
        def _():
            db_ref[...] = jnp.zeros_like(db_ref)

        second = _lane_is_second_head((QB_A, W2))
        win = pl.ds(_window_start(n, nblk), WIN_A)
        variant = _window_variant(n, nblk)
        for pp in range(PPS):
            cols = slice(pp * W2, (pp + 1) * W2)
            kw, vw = k_ref[win, cols], v_ref[win, cols]
            qs, dos = _stack_heads(q_ref[:, cols], second), _stack_heads(do_ref[:, cols], second)
            lse2, c2 = l_ref[:, cols], c_ref[:, cols]
            lse_rows = jnp.concatenate([lse2[:, 0:1], lse2[:, HEAD_A:HEAD_A + 1]], axis=0)
            c_rows = jnp.concatenate([c2[:, 0:1], c2[:, HEAD_A:HEAD_A + 1]], axis=0)
            s = lax.dot_general(qs, kw, (NT, ((), ())), preferred_element_type=F32)
            p = jnp.exp(s * scale + b_ref[2 * pp:2 * pp + 2].reshape(2 * QB_A, WIN_A) - lse_rows)
            dp = lax.dot_general(dos, vw, (NT, ((), ())), preferred_element_type=F32)
            ds = p * (dp + c_rows)
            db_ref[variant, 2 * pp:2 * pp + 2] += ds.reshape(2, QB_A, WIN_A)
            pb, dsb = p.astype(BF16), (ds * scale).astype(BF16)
            dq_ref[:, cols] = _unstack_heads(jnp.dot(dsb, kw, preferred_element_type=F32), second).astype(dq_ref.dtype)
            dk_ref[win, cols] += lax.dot_general(dsb, qs, (TN, ((), ())), preferred_element_type=F32)
            dv_ref[win, cols] += lax.dot_general(pb, dos, (TN, ((), ())), preferred_element_type=F32)

    in_specs = [_bs((QB_A, WS), lambda hp, r, n: (n, r * cb + b0 + hp)),
                _bs((L, WS), lambda hp, r, n: (0, r * cb + b0 + ob + hp)), _bs((L, WS), lambda hp, r, n: (0, r * cb + b0 + 2 * ob + hp))]
    in_specs += [_bs((QB_A, WS), lambda hp, r, n: (n, r * ob + hp))] * 3
    in_specs += [_bs((None, 2 * PPS, QB_A, WIN_A), lambda hp, r, n: (_window_variant(n, nblk), hp, 0, 0))]
    out_shape = [jax.ShapeDtypeStruct((L, d * WIDTH_A), BF16), jax.ShapeDtypeStruct((L, d * WIDTH_A), F32),
                 jax.ShapeDtypeStruct((L, d * WIDTH_A), F32), jax.ShapeDtypeStruct((3, HEADS_A, QB_A, WIN_A), F32)]
    out_specs = [_bs((QB_A, WS), lambda hp, r, n: (n, r * ob + hp)),
                 _bs((L, WS), lambda hp, r, n: (0, r * ob + hp)), _bs((L, WS), lambda hp, r, n: (0, r * ob + hp)),
                 _bs((3, 2 * PPS, QB_A, WIN_A), lambda hp, r, n: (0, hp, 0, 0))]
    dq, dk, dv, db = pl.pallas_call(
        body, out_shape=out_shape, grid=(ob, d, nblk), in_specs=in_specs, out_specs=out_specs,
        compiler_params=_params("arbitrary", "arbitrary", "arbitrary"), name=f"a_bwd_d{d}")(
            pv, pv, pv, view(do), view(lse), view(cterm), bias)
    return dq.reshape(T, WIDTH_A), dk.reshape(T, WIDTH_A), dv.reshape(T, WIDTH_A), db


def _segment_ones():
    i = np.arange(WIDTH_A)
    return jnp.asarray((i[:, None] // HEAD_A == i[None, :] // HEAD_A).astype(np.float32), dtype=BF16)


def _group_weights(l0, l1, l2):
    m = jnp.maximum(jnp.maximum(l0, l1), l2)
    e = [jnp.exp(l - m) for l in (l0, l1, l2)]
    z = e[0] + e[1] + e[2]
    return [ei / z for ei in e]


def _combine_fwd(outs, lses):
    T = outs[0].shape[0]

    def fn(o0, o1, o2, l0, l1, l2):
        w = _group_weights(l0, l1, l2)
        return w[0] * o0.astype(F32) + w[1] * o1.astype(F32) + w[2] * o2.astype(F32)

    (oa,) = _ew("a_combine", fn, [_tiled(o) for o in outs] + [_tiled(l) for l in lses], [(BF16, WIDTH_A)], n_rows=T, rows=512)
    return oa


def _combine_bwd(doa, outs, lses):
    T = doa.shape[0]

    def fn(d, o0, o1, o2, l0, l1, l2, seg):
        d = d.astype(F32)
        w = _group_weights(l0, l1, l2)
        tot = jnp.zeros(d.shape, F32)
        for wg, og in zip(w, (o0, o1, o2)):
            prod = wg * d * og.astype(F32)
            hi = prod.astype(BF16)
            lo = (prod - hi.astype(F32)).astype(BF16)
            tot = tot + jnp.dot(hi, seg, preferred_element_type=F32) + jnp.dot(lo, seg, preferred_element_type=F32)
        return tuple(wg * d for wg in w) + tuple(-wg * tot for wg in w)

    res = _ew("a_combine_bwd", fn, [_tiled(doa)] + [_tiled(o) for o in outs] + [_tiled(l) for l in lses] + [_whole(_segment_ones())],
              [(BF16, WIDTH_A)] * 3 + [(F32, WIDTH_A)] * 3, n_rows=T, rows=256)
    return res[:3], res[3:]


def _rope_tables(T):
    rows = T // GRID_W
    row = jnp.repeat(jnp.arange(rows, dtype=F32), GRID_W)
    col = jnp.tile(jnp.arange(GRID_W, dtype=F32), rows)
    n_freq = HEAD_B // 4
    freq = ROPE_THETA ** (-jnp.arange(n_freq, dtype=F32) / n_freq)
    ang = jnp.concatenate([row[:, None] * freq, col[:, None] * freq], axis=-1)
    cos, sin = jnp.repeat(jnp.cos(ang), 2, axis=1), jnp.repeat(jnp.sin(ang), 2, axis=1)
    sign = jnp.where(jnp.arange(HEAD_B) % 2 == 0, -1.0, 1.0).astype(F32)
    return cos, sin * sign


def _swap_pairs(v):
    even = lax.broadcasted_iota(jnp.int32, v.shape, v.ndim - 1) % 2 == 0
    n = v.shape[-1]
    return jnp.where(even, pltpu.roll(v, n - 1, v.ndim - 1), pltpu.roll(v, 1, v.ndim - 1))


def _qk_fwd(name, proj, col0, n_heads, gain, cos, sin, out_scale=1.0):
    T = proj.shape[0]

    def fn(xr, g, c, s):
        xn = _norm_fwd(xr.astype(F32), g)
        return (xn * c + _swap_pairs(xn) * s) * out_scale

    (out,) = _ew(name, fn, [_tiled(proj, HEAD_B, col0 // HEAD_B), _whole(gain), _table(cos), _table(sin)],
                 [(BF16, HEAD_B)], n_rows=T, rows=2048, ncols=n_heads)
    return out


def _qk_bwd(name, dout, proj, col0, n_heads, gain, cos, sin, in_scale=1.0):
    T = proj.shape[0]

    def fn(dv, xr, g, c, s):
        dv = dv.astype(F32) * in_scale
        dxn = c * dv + _swap_pairs(s * dv)
        dx, dgr = _norm_bwd(xr.astype(F32), g, dxn)
        return dx, _colsum(dgr)

    dx, dg = _ew(name, fn, [_tiled(dout, HEAD_B, 0), _tiled(proj, HEAD_B, col0 // HEAD_B), _whole(gain),
                            _table(cos), _table(sin)],
                 [(BF16, HEAD_B)], n_rows=T, rows=2048, reds=(HEAD_B,), ncols=n_heads)
    return dx, jnp.sum(dg, axis=0)


def _gqa_fwd(qn, kn, proj):
    T = qn.shape[0]
    GW = 4 * HEAD_B

    def body(q_ref, k_ref, v_ref, o_ref, l_ref):
        k, v = k_ref[...], v_ref[...]
        lane = lax.broadcasted_iota(jnp.int32, (QB_B, HEAD_B), 1)
        lse_all = jnp.zeros((QB_B, HEAD_B), F32)
        for g in range(4):
            cols = slice(g * HEAD_B, (g + 1) * HEAD_B)
            s = lax.dot_general(q_ref[:, cols], k, (NT, ((), ())), preferred_element_type=F32)
            m = jnp.max(s, axis=-1, keepdims=True)
            p = jnp.exp2(s - m)
            l = jnp.sum(p, axis=-1, keepdims=True)
            o = jnp.dot(p.astype(BF16), v, preferred_element_type=F32) / l
            o_ref[:, cols] = o.astype(o_ref.dtype)
            lse_all = jnp.where(lane == g, m + jnp.log2(l), lse_all)
        l_ref[...] = lse_all

    return pl.pallas_call(
        body, out_shape=[jax.ShapeDtypeStruct((T, 2 * GW), BF16), jax.ShapeDtypeStruct((2, T, HEAD_B), F32)],
        grid=(2, T // QB_B),
        in_specs=[_bs((QB_B, GW), lambda kv, i: (i, kv)), _bs((T, HEAD_B), lambda kv, i: (0, kv)),
                  _bs((T, HEAD_B), lambda kv, i: (0, B_V // HEAD_B + kv))],
        out_specs=[_bs((QB_B, GW), lambda kv, i: (i, kv)), _bs((None, QB_B, HEAD_B), lambda kv, i: (kv, i, 0))],
        compiler_params=_params("parallel", "parallel"), name="b_fwd")(qn, kn, proj)


def _gqa_bwd(qn, kn, proj, o, lse, do):
    T = qn.shape[0]
    GW = 4 * HEAD_B

    def body(q_ref, k_ref, v_ref, o_ref, l_ref, do_ref, dq_ref, dk_ref, dv_ref):
        i = pl.program_id(1)

        @pl.when(i == 0)
        def _():
            dk_ref[...] = jnp.zeros_like(dk_ref)
            dv_ref[...] = jnp.zeros_like(dv_ref)

        k, v = k_ref[...], v_ref[...]
        lse_all = l_ref[...]
        for g in range(4):
            cols = slice(g * HEAD_B, (g + 1) * HEAD_B)
            q, dob = q_ref[:, cols], do_ref[:, cols]
            delta = jnp.sum(dob.astype(F32) * o_ref[:, cols].astype(F32), axis=-1, keepdims=True)
            s = lax.dot_general(q, k, (NT, ((), ())), preferred_element_type=F32)
            p = jnp.exp2(s - lse_all[:, g:g + 1])
            dp = lax.dot_general(dob, v, (NT, ((), ())), preferred_element_type=F32)
            ds = (p * (dp - delta)).astype(BF16)
            dq_ref[:, cols] = jnp.dot(ds, k, preferred_element_type=F32).astype(dq_ref.dtype)
            dk_ref[...] += lax.dot_general(ds, q, (TN, ((), ())), preferred_element_type=F32)
            dv_ref[...] += lax.dot_general(p.astype(BF16), dob, (TN, ((), ())), preferred_element_type=F32)

    return pl.pallas_call(
        body, out_shape=[jax.ShapeDtypeStruct((T, 2 * GW), BF16), jax.ShapeDtypeStruct((T, 2 * HEAD_B), F32),
                         jax.ShapeDtypeStruct((T, 2 * HEAD_B), F32)],
        grid=(2, T // QB_B),
        in_specs=[_bs((QB_B, GW), lambda kv, i: (i, kv)), _bs((T, HEAD_B), lambda kv, i: (0, kv)),
                  _bs((T, HEAD_B), lambda kv, i: (0, B_V // HEAD_B + kv)), _bs((QB_B, GW), lambda kv, i: (i, kv)),
                  _bs((None, QB_B, HEAD_B), lambda kv, i: (kv, i, 0)), _bs((QB_B, GW), lambda kv, i: (i, kv))],
        out_specs=[_bs((QB_B, GW), lambda kv, i: (i, kv)), _bs((T, HEAD_B), lambda kv, i: (0, kv)),
                   _bs((T, HEAD_B), lambda kv, i: (0, kv))],
        compiler_params=_params("parallel", "arbitrary"), name="b_bwd")(qn, kn, proj, o, lse, do)


def _local_step(x, target, small, get_w, put_g, deps=()):
    T, D = x.shape
    gs = {}

    (x1, h2), ffn1_saved = _ffn_fwd("ffn1", x, small["ffn1_norm"], get_w, deps, tail_ins=[small["mix_norm"]],
                                    tail_fn=lambda y, g: (y, _norm_fwd(y, g)), tail_outs=(F32, BF16))
    w_in = get_w("w_in", h2)
    nq = w_in.shape[2]
    tpq = nq // WIDTH_A

    def proj_tile(j, k):
        c = j * tpq + k
        return jnp.where(c < 3 * len(DILATIONS), (c % 3) * 3 + c // 3, c)

    proj = _mm("mix_in", (4, tpq),
               [(h2, _resident((T, D), lambda j, k: (0, 0)), w_in, _bs((None, D, WIDTH_A), lambda j, k: (j, 0, k)))],
               jax.ShapeDtypeStruct((T, IN_WIDTH), BF16), _bs((T, WIDTH_A), lambda j, k: (0, proj_tile(j, k))), NN)

    bias = _bias_tiles(small["rel_bias"])
    a_outs, a_lses = [], []
    for grp, d in enumerate(DILATIONS):
        o, l = _dil_fwd(proj, bias[grp], grp, d)
        a_outs.append(o)
        a_lses.append(l)
    o_a = _combine_fwd(a_outs, a_lses)

    cos, sin = _rope_tables(T)
    qn = _qk_fwd("b_qnorm", proj, B_Q, 8, small["q_norm"], cos, sin, out_scale=QK_SCALE_LOG2)
    kn = _qk_fwd("b_knorm", proj, B_K, 2, small["k_norm"], cos, sin)
    o_b, lse_b = _gqa_fwd(qn, kn, proj)

    wa, wb3, w_out3 = get_w("w_branch_a", o_b), get_w("w_branch_b", o_b).reshape(1, D, D), get_w("w_out", o_b).reshape(1, D, D)
    t_a = _mm_cols("mix_branch_a", o_a, wa, tm=512, tn=256, out_dtype=BF16, cat=True)
    t_b = _mm_cols("mix_branch_b", o_b, wb3, tm=512, tn=512, out_dtype=BF16, cat=True)
    bg_a, bg_b = small["b_gate"][:, :D], small["b_gate"][:, D:]

    def merge(ta, tb, ga, gb_, ba, bb):
        sa, sb = _sigmoid(ga.astype(F32) + ba), _sigmoid(gb_.astype(F32) + bb)
        return sa * ta.astype(F32) + sb * tb.astype(F32)

    gate_ins = [_tiled(proj, D, G_A // D), _tiled(proj, D, G_B // D), _whole(bg_a), _whole(bg_b)]
    (merged,) = _ew("mix_merge", merge, [_tiled(t_a), _tiled(t_b)] + gate_ins, [(BF16, D)], n_rows=T, rows=512)
    def mix_tail(acc, xv, g):
        y = xv + acc
        return y, _norm_fwd(y, g)

    row = _bs((512, D), lambda i: (i, 0))
    x2, hn2 = _mm("mix_out", (T // 512,), [(merged, row, w_out3, _resident((None, D, D), lambda i: (0, 0, 0)))],
                  [jax.ShapeDtypeStruct((T, D), F32), jax.ShapeDtypeStruct((T, D), BF16)], [row, row], NN,
                  extras=[(x1, row), (small["ffn2_norm"], _row_spec(small["ffn2_norm"], 512))], epilogue=mix_tail)

    def head(xv, g, tv):
        r = _rstd(xv)
        xh = xv * r
        e = xh * g - tv
        dy = e * (1.0 / D)
        dxh = dy * g
        dx = r * (dxh - xh * jnp.mean(dxh * xh, axis=-1, keepdims=True))
        return dx, 0.5 * dx, _colsum(e * e) * (0.5 / D), _colsum(dy * xh)

    (dx3, dx3_half, loss_cols, g_final), ffn2_saved = _ffn_fwd(
        "ffn2", x2, small["ffn2_norm"], get_w, h=hn2, tail_ins=[small["final_norm"].reshape(1, D), target], tail_fn=head,
        tail_outs=(F32, BF16), tail_reds=(D, D))
    gs["final_norm"] = g_final.reshape(D)

    dx2, _, dmix, gs["ffn2_norm"] = _ffn_bwd("ffn2", x2, small["ffn2_norm"], get_w, put_g, ffn2_saved, dx3, dx3_half,
                                             also_bf16=True)
    g_out = _mm_wgrad("mix_bwd_dwout", merged, dmix, a_cols=D // 4, b_cols=None, tm=256, tn=512, J=4).reshape(D, D)
    dmerged = _mm_rows_t("mix_bwd_dmerged", dmix, w_out3, tm=512, out_dtype=BF16).reshape(T, D)

    def merge_bwd(dm, ta, tb, ga, gb_, ba, bb):
        dm, ta, tb = dm.astype(F32), ta.astype(F32), tb.astype(F32)
        sa, sb = _sigmoid(ga.astype(F32) + ba), _sigmoid(gb_.astype(F32) + bb)
        dga, dgb = dm * ta * sa * (1.0 - sa), dm * tb * sb * (1.0 - sb)
        return dm * sa, dm * sb, dga, dgb, _colsum(dga), _colsum(dgb)

    dta, dtb, dga, dgb, dba, dbb = _ew("mix_bwd_merge", merge_bwd, [_tiled(dmerged), _tiled(t_a), _tiled(t_b)] + gate_ins,
                                       [(BF16, D)] * 4, n_rows=T, rows=256, reds=(D, D))
    gs["b_gate"] = jnp.concatenate([dba.reshape(1, D), dbb.reshape(1, D)], axis=1)

    g_a = _mm_wgrad("mix_bwd_dwa", o_a, dta, a_cols=None, b_cols=D // 4, tm=WIDTH_A, tn=256, J=4)
    g_b = _mm_wgrad("mix_bwd_dwb", o_b, dtb, a_cols=D // 4, b_cols=None, tm=256, tn=512, J=4).reshape(D, D)
    deps = put_g({"w_out": g_out, "w_branch_a": g_a, "w_branch_b": g_b})
    do_a = _mm("mix_bwd_doa", (T // 1024,),
               [(dta, _bs((1024, D // 4), lambda i, j=j: (i, j)), wa, _bs((None, WIDTH_A, D // 4), lambda i, j=j: (j, 0, 0)))
                for j in range(4)],
               jax.ShapeDtypeStruct((T, WIDTH_A), BF16), _bs((1024, WIDTH_A), lambda i: (i, 0)), NT, deps=deps)
    do_b = _mm_rows_t("mix_bwd_dob", dtb, wb3, tm=512, out_dtype=BF16).reshape(T, D)

    dqn, dkn, dv_b = _gqa_bwd(qn, kn, proj, o_b, lse_b, do_b)
    dq_b, gs["q_norm"] = _qk_bwd("b_bwd_qnorm", dqn, proj, B_Q, 8, small["q_norm"], cos, sin, in_scale=HEAD_B ** -0.5)
    dk_b, gs["k_norm"] = _qk_bwd("b_bwd_knorm", dkn, proj, B_K, 2, small["k_norm"], cos, sin, in_scale=1.0 / LOG2_E)

    do_groups, c_groups = _combine_bwd(do_a, a_outs, a_lses)
    dqs, dks, dvs, dbs = [], [], [], []
    for grp, d in enumerate(DILATIONS):
        dq, dk, dv, db = _dil_bwd(proj, bias[grp], do_groups[grp], a_lses[grp], c_groups[grp], grp, d)
        dqs.append(dq), dks.append(dk), dvs.append(dv), dbs.append(db)
    gs["rel_bias"] = _bias_grad(jnp.stack(dbs))

    dproj = jnp.concatenate([p.astype(BF16) for p in dqs + dks + dvs + [dq_b, dk_b, dv_b, dga, dgb]], axis=1)
    nq = w_in.shape[2]
    g_in = _mm("mix_bwd_dwin", (4, tpq),
               [(h2, _resident((T, D), lambda j, k: (0, 0)), dproj, _bs((T, WIDTH_A), lambda j, k: (0, j * tpq + k)))],
               jax.ShapeDtypeStruct((4, D, nq), BF16), _bs((None, D, WIDTH_A), lambda j, k: (j, 0, k)), TN)
    deps = put_g({"w_in": g_in})
    dx1, dx1_half, gs["mix_norm"] = _dh_norm_bwd(
        "mix_bwd_dh", 256,
        [(dproj, _bs((256, nq), lambda i, j=j: (i, j)), w_in, _resident((None, D, nq), lambda i, j=j: (j, 0, 0))) for j in range(4)],
        NT, x1, small["mix_norm"], dx2, deps)

    dx0, _, gs["ffn1_norm"] = _ffn_bwd("ffn1", x, small["ffn1_norm"], get_w, put_g, ffn1_saved, dx1, dx1_half)
    return loss_cols, dx0, gs


def _position():
    return lax.axis_index("x"), lax.axis_index("y"), lax.axis_index("c")


def _any_specs(n):
    return [pl.BlockSpec(memory_space=pl.ANY)] * n


HBM_SPEC = pl.BlockSpec(memory_space=pltpu.HBM)
SEM_SPEC = pl.BlockSpec(memory_space=pltpu.SEMAPHORE)
DATAFLOW_EFFECT = pltpu.SideEffectType.DATAFLOW_SIDE_EFFECTING
N_PEER_CHIPS = 3
LANES = 128


def _quarter_copies(srcs, lands, send_sems, recv_sems, scatter):
    x, y, c = _position()
    me = 2 * x + y
    peers = [(1 - x, y, c), (x, 1 - y, c), (1 - x, 1 - y, c)]
    copies = []
    for src, land, send, recv in zip(srcs, lands, send_sems, recv_sems):
        half = land.shape[1] // 2
        mine = land.at[me, pl.ds(c * half, half)]
        for p, (px, py, pc) in enumerate(peers):
            copies.append(pltpu.make_async_remote_copy(
                src_ref=src.at[2 * px + py] if scatter else mine, dst_ref=land.at[me] if scatter else mine,
                send_sem=send.at[p], recv_sem=recv.at[p], device_id=(px, py, pc), device_id_type=MESH))
    return copies


def _fill_from_sibling(name, stacks):
    n = len(stacks)

    def body(*refs):
        outs = refs[n:2 * n]
        send_sems, recv_sems = refs[2 * n:]
        x, y, c = _position()
        copies = []
        for i, ref in enumerate(outs):
            half = ref.shape[1] // 2
            rows = pl.ds(c * half, half)
            for p, k in enumerate((2 * (1 - x) + y, 2 * x + (1 - y), 2 * (1 - x) + (1 - y))):
                cp = pltpu.make_async_remote_copy(ref.at[k, rows], ref.at[k, rows], send_sems.at[3 * i + p], recv_sems.at[3 * i + p],
                                                  device_id=(x, y, 1 - c), device_id_type=MESH)
                cp.start()
                copies.append(cp)
        for cp in copies:
            cp.wait()

    return pl.pallas_call(
        body, out_shape=[jax.ShapeDtypeStruct(s.shape, s.dtype) for s in stacks],
        in_specs=_any_specs(n), out_specs=_any_specs(n), input_output_aliases={i: i for i in range(n)},
        scratch_shapes=[pltpu.SemaphoreType.DMA((N_PEER_CHIPS * n,)), pltpu.SemaphoreType.DMA((N_PEER_CHIPS * n,))],
        compiler_params=pltpu.CompilerParams(has_side_effects=True), name=name)(*stacks)


def _exchange_start(name, srcs, lands, scatter):
    n = len(lands)
    arrays = list(lands) if srcs is None else list(srcs) + list(lands)
    k = len(arrays)

    def body(*refs):
        land_refs = refs[k - n:k]
        send_sems, recv_sems = refs[k:k + n], refs[k + n:k + 2 * n]
        token = refs[2 * k + 2 * n]
        for cp in _quarter_copies(refs[:n], land_refs, send_sems, recv_sems, scatter):
            cp.start()
        token[...] = jnp.zeros_like(token)

    sem = pltpu.SemaphoreType.DMA((N_PEER_CHIPS,))
    out_shape = [sem] * (2 * n) + [pltpu.HBM(a.shape, a.dtype) for a in arrays] + [jax.ShapeDtypeStruct((8, LANES), F32)]
    res = pl.pallas_call(
        body, name=name, out_shape=out_shape, in_specs=[HBM_SPEC] * k,
        out_specs=[SEM_SPEC] * (2 * n) + [HBM_SPEC] * k + [pl.BlockSpec(memory_space=pltpu.VMEM)],
        input_output_aliases={i: 2 * n + i for i in range(k)},
        compiler_params=pltpu.CompilerParams(has_side_effects=DATAFLOW_EFFECT),
    )(*[pltpu.with_memory_space_constraint(a, pltpu.HBM) for a in arrays])
    thru = res[2 * n:2 * n + k]
    return res[:n], res[n:2 * n], (None if srcs is None else thru[:n]), thru[k - n:], res[2 * n + k]


def _exchange_wait(name, srcs, lands, send_sems, recv_sems, after, scatter):
    n = len(lands)
    arrays = list(lands) if srcs is None else list(srcs) + list(lands)
    k = len(arrays)

    def body(*refs):
        sends, recvs = refs[k:k + n], refs[k + n:k + 2 * n]
        for cp in _quarter_copies(refs[:n], refs[k - n:k], sends, recvs, scatter):
            cp.wait_send()
            cp.wait_recv()

    res = pl.pallas_call(
        body, name=name, out_shape=[pltpu.HBM(a.shape, a.dtype) for a in arrays],
        in_specs=[HBM_SPEC] * k + [SEM_SPEC] * (2 * n) + [pl.BlockSpec(memory_space=pl.ANY)],
        out_specs=[HBM_SPEC] * k, input_output_aliases={i: i for i in range(k)},
        compiler_params=pltpu.CompilerParams(has_side_effects=DATAFLOW_EFFECT),
    )(*arrays, *send_sems, *recv_sems, after)
    return res[k - n:]


def _own_slot(stack_shape, own, dtype):
    me = 2 * lax.axis_index("x") + lax.axis_index("y")
    return lax.dynamic_update_slice(lax.empty(stack_shape, dtype), own[None], (me,) + (0,) * own.ndim)


def _swap_with_sibling(parts):
    n = len(parts)

    def body(*refs):
        ins, outs = refs[:n], refs[n:2 * n]
        send_sems, recv_sems = refs[2 * n:]
        x, y, c = _position()
        copies = []
        for i in range(n):
            cp = pltpu.make_async_remote_copy(ins[i], outs[i], send_sems.at[i], recv_sems.at[i],
                                              device_id=(x, y, 1 - c), device_id_type=MESH)
            cp.start()
            copies.append(cp)
        for cp in copies:
            cp.wait()

    return pl.pallas_call(
        body, out_shape=[jax.ShapeDtypeStruct(s.shape, s.dtype) for s in parts],
        in_specs=_any_specs(n), out_specs=_any_specs(n),
        scratch_shapes=[pltpu.SemaphoreType.DMA((n,)), pltpu.SemaphoreType.DMA((n,))],
        compiler_params=pltpu.CompilerParams(has_side_effects=True), name="swap_with_sibling")(*parts)


def _allreduce_small(buf):
    R, C = buf.shape
    flips = [(fx, fy, fc) for fx in (0, 1) for fy in (0, 1) for fc in (0, 1)][1:]

    def body(in_ref, out_ref, land_ref, send_sems, recv_sems):
        x, y, c = _position()
        me = 4 * x + 2 * y + c
        copies = []
        for k, (fx, fy, fc) in enumerate(flips):
            px, py, pc = (1 - x if fx else x), (1 - y if fy else y), (1 - c if fc else c)
            cp = pltpu.make_async_remote_copy(in_ref, land_ref.at[me], send_sems.at[k], recv_sems.at[k],
                                              device_id=(px, py, pc), device_id_type=MESH)
            cp.start()
            copies.append(cp)
        land_ref[me] = in_ref[...]
        for cp in copies:
            cp.wait()
        acc = land_ref[0]
        for k in range(1, 8):
            acc = acc + land_ref[k]
        out_ref[...] = acc

    return pl.pallas_call(
        body, out_shape=jax.ShapeDtypeStruct((R, C), F32),
        in_specs=[pl.BlockSpec(memory_space=pltpu.VMEM)], out_specs=pl.BlockSpec(memory_space=pltpu.VMEM),
        scratch_shapes=[pltpu.VMEM((8, R, C), F32), pltpu.SemaphoreType.DMA((7,)), pltpu.SemaphoreType.DMA((7,))],
        compiler_params=pltpu.CompilerParams(has_side_effects=True), name="allreduce_small")(buf)


def _adamw_math(w, g, m, v):
    m2 = ADAM_B1 * m + (1.0 - ADAM_B1) * g
    v2 = ADAM_B2 * v + (1.0 - ADAM_B2) * (g * g)
    m_hat = m2 / (1.0 - ADAM_B1 ** ADAM_STEP)
    v_hat = v2 / (1.0 - ADAM_B2 ** ADAM_STEP)
    delta = -ADAM_LR * (m_hat / (jnp.sqrt(v_hat) + ADAM_EPS) + ADAM_WD * w)
    return delta, m2, v2


def _adamw_big(name, w, m, v, part_mine, part_sibling):
    R, C = w.shape
    rows = 256 if R % 256 == 0 else R // 2 if (R // 2) % 8 == 0 else R

    def fn(wv, mv, vv, a, b):
        g = a + b
        return (g,) + _adamw_math(wv, g, mv, vv)

    return _ew(name, fn, [_tiled(w), _tiled(m), _tiled(v), _tiled(part_mine), _tiled(part_sibling)], [(F32, C)] * 4, n_rows=R, rows=rows)


def _sum_four(name, stack):
    _, R, C = stack.shape
    rows = 256 if R % 256 == 0 else R // 2 if (R // 2) % 8 == 0 else R
    flat = stack.reshape(4 * R, C)
    nrb = R // rows

    def fn(a, b, c, d):
        return ((a.astype(F32) + b.astype(F32)) + c.astype(F32)) + d.astype(F32)

    (out,) = _ew(name, fn, [_tiled(flat, None, 0, k * nrb) for k in range(4)], [(F32, C)], n_rows=R, rows=rows)
    return out


BIG = ("ffn1_w1", "ffn1_w3", "ffn1_w2", "w_in", "w_branch_a", "w_branch_b", "w_out", "ffn2_w1", "ffn2_w3", "ffn2_w2")
SMALL = ("ffn1_norm", "mix_norm", "b_gate", "q_norm", "k_norm", "rel_bias", "ffn2_norm", "final_norm")
ORDER = ("ffn1_norm", "ffn1_w1", "ffn1_w3", "ffn1_w2", "mix_norm", "w_in", "b_gate", "q_norm", "k_norm", "rel_bias",
         "w_branch_a", "w_branch_b", "w_out", "ffn2_norm", "ffn2_w1", "ffn2_w3", "ffn2_w2", "final_norm")
TRANSPOSED = ("ffn1_w1", "ffn1_w3", "ffn2_w1", "ffn2_w3")
GATHER_GROUPS = (("ffn1_w1", "ffn1_w3"), ("ffn1_w2",), ("w_in",), ("w_branch_a", "w_branch_b", "w_out"),
                 ("ffn2_w1", "ffn2_w3", "ffn2_w2"))


def _pack_small(d):
    rows = []
    for n in SMALL:
        flat = d[n].reshape(-1)
        pad = (-flat.shape[0]) % LANES
        rows.append(jnp.pad(flat, (0, pad)).reshape(-1, LANES))
    buf = jnp.concatenate(rows, axis=0)
    return jnp.pad(buf, ((0, (-buf.shape[0]) % 8), (0, 0)))


def _unpack_small(buf, like):
    out, r = {}, 0
    for n in SMALL:
        size = like[n].size
        nr = -(-size // LANES)
        out[n] = buf[r:r + nr].reshape(-1)[:size].reshape(like[n].shape)
        r += nr
    return out


def kernel(x, ffn1_norm, ffn1_w1, ffn1_w3, ffn1_w2, mix_norm, w_in, b_gate, q_norm, k_norm, rel_bias, w_branch_a, w_branch_b, w_out, ffn2_norm, ffn2_w1, ffn2_w3, ffn2_w2, final_norm, loss_target, m_ffn1_norm, m_ffn1_w1, m_ffn1_w3, m_ffn1_w2, m_mix_norm, m_w_in, m_b_gate, m_q_norm, m_k_norm, m_rel_bias, m_w_branch_a, m_w_branch_b, m_w_out, m_ffn2_norm, m_ffn2_w1, m_ffn2_w3, m_ffn2_w2, m_final_norm, v_ffn1_norm, v_ffn1_w1, v_ffn1_w3, v_ffn1_w2, v_mix_norm, v_w_in, v_b_gate, v_q_norm, v_k_norm, v_rel_bias, v_w_branch_a, v_w_branch_b, v_w_out, v_ffn2_norm, v_ffn2_w1, v_ffn2_w3, v_ffn2_w2, v_final_norm):
    given = dict(locals())
    w = {n: given[n] for n in ORDER}
    m = {n: given["m_" + n] for n in ORDER}
    v = {n: given["v_" + n] for n in ORDER}
    T, D = x.shape[1], x.shape[2]

    def stored(a, n):
        a = a.reshape(a.shape[1:])
        return a.T if n in TRANSPOSED else a

    def returned(a, n):
        return (a.T if n in TRANSPOSED else a).reshape(w[n].shape)

    quarter = {n: stored(w[n], n) for n in BIG}
    send, recv, _, land_thru, token = _exchange_start(
        "gather_start", None, [_own_slot((4,) + quarter[n].shape, quarter[n].astype(BF16), BF16) for n in BIG], scatter=False)
    index = {n: i for i, n in enumerate(BIG)}
    ready = {}

    def get_w(name, after):
        if name not in ready:
            group = next(g for g in GATHER_GROUPS if name in g)
            ids = [index[n] for n in group]
            stacks = _exchange_wait("gather_wait_" + group[0], None, [land_thru[i] for i in ids],
                                    [send[i] for i in ids], [recv[i] for i in ids], after, scatter=False)
            stacks = _fill_from_sibling("gather_fill_" + group[0], stacks)
            for n, st in zip(group, stacks):
                ready[n] = st.reshape(D, D) if n in ("w_branch_b", "w_out") else st
        return ready[name]

    me = 2 * lax.axis_index("x") + lax.axis_index("y")
    in_flight = []

    def put_g(grads):
        names = list(grads)
        stacks = [grads[n].reshape((4,) + quarter[n].shape) for n in names]
        lands = [_own_slot(s.shape, lax.dynamic_index_in_dim(s, me, 0, keepdims=False), BF16) for s in stacks]
        started = _exchange_start("scatter_start_" + names[0], stacks, lands, scatter=True)
        in_flight.append((names,) + tuple(started[:4]))
        return [started[4]]

    small = {n: w[n] for n in SMALL}
    loss_cols, grad_x, gs = _local_step(x.reshape(T, D), loss_target.reshape(T, D), small, get_w, put_g, deps=[token])
    loss = lax.psum(jnp.sum(loss_cols), ("x", "y", "c"))

    landed = {}
    for names, s_sem, r_sem, srcs, lands in in_flight:
        got = _exchange_wait("scatter_wait_" + names[0], srcs, lands, s_sem, r_sem, grad_x, scatter=True)
        landed.update(zip(names, got))
    partial = [_sum_four(f"sum4_{n}", landed[n]) for n in BIG]
    other = _swap_with_sibling(partial)
    grads, deltas, new_m, new_v = {}, {}, {}, {}
    for n, mine, theirs in zip(BIG, partial, other):
        res = _adamw_big(f"adamw_{n}", quarter[n], stored(m[n], n), stored(v[n], n), mine, theirs)
        grads[n], deltas[n], new_m[n], new_v[n] = [returned(r, n) for r in res]

    gs = {n: gs[n].reshape(w[n].shape) for n in SMALL}
    g_small = _allreduce_small(_pack_small(gs))
    packed = [_pack_small({n: d[n] for n in SMALL}) for d in (w, m, v)]
    R = g_small.shape[0]
    res = _ew("adamw_small", lambda wv, mv, vv, g: (g,) + _adamw_math(wv, g, mv, vv),
              [_tiled(packed[0]), _tiled(packed[1]), _tiled(packed[2]), _tiled(g_small)], [(F32, LANES)] * 4, n_rows=R, rows=R)
    for d, buf in zip((grads, deltas, new_m, new_v), res):
        d.update(_unpack_small(buf, w))

    return (loss, grad_x.reshape(x.shape), *[grads[n] for n in ORDER], *[deltas[n] for n in ORDER],
            *[new_m[n] for n in ORDER], *[new_v[n] for n in ORDER])
```

```python
import functools
import math

import numpy as np
import jax
import jax.numpy as jnp
from jax import lax
from jax.experimental import pallas as pl
from jax.experimental.pallas import tpu as pltpu

F32 = jnp.float32
BF16 = jnp.bfloat16
MESH = pl.DeviceIdType.MESH

NEG_INF = -1e30
EPS = 1e-6
GRID_W = 64
ROPE_THETA = 10000.0
DILATIONS = (1, 4, 16)
BAND_HALF = 64
HEAD_A = 64
HEADS_A = 8
WIDTH_A = HEADS_A * HEAD_A
HEAD_B = 128
LOG2_E = math.log2(math.e)
QK_SCALE_LOG2 = HEAD_B ** -0.5 * LOG2_E
N_BUCKETS = 32
MAX_DISTANCE = 1024
ADAM_LR, ADAM_B1, ADAM_B2, ADAM_EPS, ADAM_WD, ADAM_STEP = 0.001, 0.9, 0.999, 1e-08, 0.01, 10

A_Q, A_K, A_V = 0, 1536, 3072
B_Q, B_K, B_V = 4608, 5632, 5888
G_A, G_B = 6144, 7168
IN_WIDTH = 8192

VMEM_LIMIT_BYTES = 56 * 1024 * 1024
QB_A = 128
QB_B = 256


def _params(*sem):
    return pltpu.CompilerParams(dimension_semantics=sem, vmem_limit_bytes=VMEM_LIMIT_BYTES)


def _bs(shape, fn):
    return pl.BlockSpec(shape, fn)


def _resident(shape, fn):
    return pl.BlockSpec(shape, fn, pipeline_mode=pl.Buffered(1))


def _mm(name, grid, pairs, out_shape, out_spec, dims, *, extras=(), epilogue=None, deps=(), reds=()):
    n_pairs, n_extra, n_deps = len(pairs), len(extras), len(deps)
    operands = [p[0] for p in pairs] + [p[2] for p in pairs] + [e[0] for e in extras] + list(deps)
    in_specs = [p[1] for p in pairs] + [p[3] for p in pairs] + [e[1] for e in extras] + _any_specs(n_deps)
    single = not isinstance(out_shape, (list, tuple))
    out_shapes = [out_shape] if single else list(out_shape)
    out_specs = [out_spec] if single else list(out_spec)
    n_out = len(out_shapes)
    out_shapes += [jax.ShapeDtypeStruct((1, w), F32) for w in reds]
    out_specs += [_bs((1, w), lambda *_: (0, 0)) for w in reds]

    def body(*refs):
        a_refs, b_refs = refs[:n_pairs], refs[n_pairs:2 * n_pairs]
        e_refs = refs[2 * n_pairs:2 * n_pairs + n_extra]
        o_refs = refs[2 * n_pairs + n_extra + n_deps:]
        acc = None
        for a_ref, b_ref in zip(a_refs, b_refs):
            t = lax.dot_general(a_ref[...], b_ref[...], (dims, ((), ())), preferred_element_type=F32)
            acc = t if acc is None else acc + t
        vals = acc if epilogue is None else epilogue(acc, *[e[...] for e in e_refs])
        if not isinstance(vals, (list, tuple)):
            vals = (vals,)
        for o_ref, v in zip(o_refs[:n_out], vals[:n_out]):
            o_ref[...] = v.astype(o_ref.dtype)
        if reds:
            first = functools.reduce(jnp.logical_and, [pl.program_id(ax) == 0 for ax in range(len(grid))])
            for r_ref, v in zip(o_refs[n_out:], vals[n_out:]):
                @pl.when(first)
                def _(r_ref=r_ref):
                    r_ref[...] = jnp.zeros_like(r_ref)
                r_ref[...] += v

    sem = ["arbitrary" if reds else "parallel"] * len(grid)
    res = pl.pallas_call(
        body, out_shape=out_shapes, grid=grid, in_specs=in_specs, out_specs=out_specs,
        compiler_params=_params(*sem), name=name)(*operands)
    return res[0] if (single and not reds) else res


NN = ((1,), (0,))
NT = ((1,), (1,))
TN = ((0,), (0,))


def _mm_cols(name, a, w, *, tm, tn, out_dtype, cat, extras=(), epilogue=None):
    M, K = a.shape
    J, _, n = w.shape
    tn = min(tn, n)
    nb = n // tn
    if cat:
        shape, spec = (M, J * n), _bs((tm, tn), lambda j, i, k: (i, j * nb + k))
    else:
        shape, spec = (J, M, n), _bs((None, tm, tn), lambda j, i, k: (j, i, k))
    ex = [(e, _bs((tm, tn), lambda j, i, k: (i, j * nb + k))) for e in extras]
    return _mm(name, (J, M // tm, nb),
               [(a, _bs((tm, K), lambda j, i, k: (i, 0)), w, _bs((None, K, tn), lambda j, i, k: (j, 0, k)))],
               jax.ShapeDtypeStruct(shape, out_dtype), spec, NN, extras=ex, epilogue=epilogue)


def _mm_rows_t(name, a, w, *, tm, out_dtype):
    M, N = a.shape
    J, f, _ = w.shape
    return _mm(name, (J, M // tm),
               [(a, _bs((tm, N), lambda j, i: (i, 0)), w, _bs((None, f, N), lambda j, i: (j, 0, 0)))],
               jax.ShapeDtypeStruct((J, M, f), out_dtype), _bs((None, tm, f), lambda j, i: (j, i, 0)), NT)


def _mm_wgrad(name, a, b, *, a_cols, b_cols, tm, tn, J):
    def pick(arr, cols, t):
        if arr.ndim == 3:
            T, c = arr.shape[1], arr.shape[2]
            t = min(t, c)
            return T, c, t, (lambda sel: _bs((None, T, t), lambda j, i, k: (j, 0, sel(i, k))))
        T = arr.shape[0]
        c = arr.shape[1] if cols is None else cols
        t = min(t, c)
        per = c // t
        if cols is None:
            if per == 1:
                return T, c, t, (lambda sel: _resident((T, t), lambda j, i, k: (0, 0)))
            return T, c, t, (lambda sel: _bs((T, t), lambda j, i, k: (0, sel(i, k))))
        return T, c, t, (lambda sel: _bs((T, t), lambda j, i, k: (0, j * per + sel(i, k))))
    _, ca, tm, mk_a = pick(a, a_cols, tm)
    _, cb, tn, mk_b = pick(b, b_cols, tn)
    return _mm(name, (J, ca // tm, cb // tn),
               [(a, mk_a(lambda i, k: i), b, mk_b(lambda i, k: k))],
               jax.ShapeDtypeStruct((J, ca, cb), BF16), _bs((None, tm, tn), lambda j, i, k: (j, i, k)), TN)


def _tiled(arr, width=None, col=0, rowblk=0):
    return ("t", arr, arr.shape[1] if width is None else width, col, rowblk)


def _table(arr):
    return ("f", arr)


def _whole(arr):
    return ("w", arr)


def _ew(name, fn, ins, outs, *, n_rows, rows, reds=(), ncols=1, deps=()):
    nrb = n_rows // rows
    n_deps = len(deps)
    operands, in_specs = [], []
    for spec in ins:
        if spec[0] == "t":
            _, arr, width, col, rowblk = spec
            step = 1 if ncols > 1 else 0
            in_specs.append(_bs((rows, width), lambda c, i, col=col, rowblk=rowblk, step=step: (rowblk + i, col + c * step)))
        elif spec[0] == "f":
            arr = spec[1]
            in_specs.append(_bs((rows, arr.shape[1]), lambda c, i: (i, 0)))
        else:
            arr = spec[1]
            nd = arr.ndim
            if nd == 3:
                in_specs.append(_bs((None,) + arr.shape[1:], lambda c, i: (c, 0, 0)))
            else:
                in_specs.append(_bs(arr.shape, lambda c, i, nd=nd: (0,) * nd))
        operands.append(arr)
    out_shapes = [jax.ShapeDtypeStruct((n_rows, ncols * w), dt) for dt, w in outs]
    out_specs = [_bs((rows, w), lambda c, i: (i, c)) for _, w in outs]
    out_shapes += [jax.ShapeDtypeStruct((ncols, 1, w), F32) for w in reds]
    out_specs += [_bs((None, 1, w), lambda c, i: (c, 0, 0)) for w in reds]
    n_in, n_out, n_red = len(ins), len(outs), len(reds)
    operands += list(deps)
    in_specs += _any_specs(n_deps)

    def body(*refs):
        vals = fn(*[r[...] for r in refs[:n_in]])
        if not isinstance(vals, (tuple, list)):
            vals = (vals,)
        o_refs = refs[n_in + n_deps:]
        for o_ref, v in zip(o_refs[:n_out], vals[:n_out]):
            o_ref[...] = v.astype(o_ref.dtype)
        if n_red:
            i = pl.program_id(1)
            for r_ref, v in zip(o_refs[n_out:], vals[n_out:]):
                @pl.when(i == 0)
                def _(r_ref=r_ref):
                    r_ref[...] = jnp.zeros_like(r_ref)
                r_ref[...] += v

    res = pl.pallas_call(
        body, out_shape=out_shapes, grid=(ncols, nrb), in_specs=in_specs, out_specs=out_specs,
        compiler_params=_params("parallel", "arbitrary" if n_red else "parallel"), name=name)(*operands)
    return res


def _colsum(v):
    return jnp.sum(v, axis=0, keepdims=True)


def _rstd(x):
    return lax.rsqrt(jnp.mean(x * x, axis=-1, keepdims=True) + EPS)


def _sigmoid(x):
    return 1.0 / (1.0 + jnp.exp(-x))


def _norm_fwd(x, g):
    return x * _rstd(x) * g


def _norm_bwd(x, g, dy):
    r = _rstd(x)
    xh = x * r
    dxh = dy * g
    dx = r * (dxh - xh * jnp.mean(dxh * xh, axis=-1, keepdims=True))
    return dx, dy * xh


def _row_spec(arr, rows):
    if arr.shape[0] == 1:
        return _bs(arr.shape, lambda i: (0, 0))
    return _bs((rows, arr.shape[1]), lambda i: (i, 0))


def _ffn_fwd(tag, x, gain, get_w, deps=(), *, h=None, tail_ins=(), tail_fn=None, tail_outs=(F32,), tail_reds=()):
    T, D = x.shape
    if h is None:
        (h,) = _ew(f"{tag}_norm", lambda xv, g: _norm_fwd(xv, g), [_tiled(x), _whole(gain)], [(BF16, D)], n_rows=T, rows=512,
                   deps=deps)
    w1, w3 = get_w(f"{tag}_w1", h), get_w(f"{tag}_w3", h)
    J, f, _ = w1.shape
    tm = 1024

    def up(h_ref, w1_ref, w3_ref, u_ref, g_ref, a_ref):
        hv = h_ref[...]
        u = lax.dot_general(hv, w1_ref[...], (NT, ((), ())), preferred_element_type=F32)
        g = lax.dot_general(hv, w3_ref[...], (NT, ((), ())), preferred_element_type=F32)
        u_ref[...] = u.astype(BF16)
        g_ref[...] = g.astype(BF16)
        a_ref[...] = (u * _sigmoid(u) * g).astype(BF16)

    slab = _bs((None, tm, f), lambda j, i: (j, i, 0))
    w_spec = _bs((None, f, D), lambda j, i: (j, 0, 0))
    u, g, a = pl.pallas_call(
        up, out_shape=[jax.ShapeDtypeStruct((J, T, f), BF16)] * 3, grid=(J, T // tm),
        in_specs=[_bs((tm, D), lambda j, i: (i, 0)), w_spec, w_spec], out_specs=[slab] * 3,
        compiler_params=_params("parallel", "parallel"), name=f"{tag}_up")(h, w1, w3)
    w2 = get_w(f"{tag}_w2", a)
    def tail(acc, xv, *rest):
        y = xv + 0.5 * acc
        return y if tail_fn is None else tail_fn(y, *rest)

    row = _bs((512, D), lambda i: (i, 0))
    res = _mm(f"{tag}_down", (T // 512,),
              [(a, _bs((None, 512, f), lambda i, j=j: (j, i, 0)), w2, _resident((None, f, D), lambda i, j=j: (j, 0, 0)))
               for j in range(J)],
              [jax.ShapeDtypeStruct((T, D), dt) for dt in tail_outs], [row] * len(tail_outs), NN,
              extras=[(x, row)] + [(t, _row_spec(t, 512)) for t in tail_ins], epilogue=tail, reds=tail_reds)
    return res, (h, u, g, a)


def _dh_norm_bwd(name, rows, pairs, dims, x, gain, dres, deps, also_bf16=False):
    T, D = x.shape

    def epilogue(dh, xv, gv, dr):
        dx, dgr = _norm_bwd(xv, gv, dh)
        dx = dx + dr
        return (dx, 0.5 * dx) + ((dx,) if also_bf16 else ()) + (_colsum(dgr),)

    dts = [F32, BF16] + ([BF16] if also_bf16 else [])
    row = _bs((rows, D), lambda i: (i, 0))
    return _mm(name, (T // rows,), pairs, [jax.ShapeDtypeStruct((T, D), dt) for dt in dts], [row] * len(dts), dims,
               extras=[(x, row), (gain, _row_spec(gain, rows)), (dres, row)], epilogue=epilogue, deps=deps, reds=(D,))


def _ffn_bwd(tag, x, gain, get_w, put_g, saved, dy, dy_half, also_bf16=False):
    h, u, g, a = saved
    T, D = x.shape
    w1, w3, w2 = [get_w(f"{tag}_{n}", dy_half) for n in ("w1", "w3", "w2")]
    J, f, _ = w1.shape
    dw2 = _mm_wgrad(f"{tag}_bwd_dw2", a, dy_half, a_cols=None, b_cols=None, tm=f, tn=D, J=J)
    deps = put_g({f"{tag}_w2": dw2})
    tm = 1024

    def up_bwd(dy_ref, w2_ref, u_ref, g_ref, *rest):
        du_ref, dg_ref = rest[-2:]
        da = lax.dot_general(dy_ref[...], w2_ref[...], (NT, ((), ())), preferred_element_type=F32)
        uv, gv = u_ref[...].astype(F32), g_ref[...].astype(F32)
        s = _sigmoid(uv)
        du_ref[...] = (da * gv * (s * (1.0 + uv * (1.0 - s)))).astype(BF16)
        dg_ref[...] = (da * (uv * s)).astype(BF16)

    slab = _bs((None, tm, f), lambda j, i: (j, i, 0))
    du, dg = pl.pallas_call(
        up_bwd, out_shape=[jax.ShapeDtypeStruct((J, T, f), BF16)] * 2, grid=(J, T // tm),
        in_specs=[_bs((tm, D), lambda j, i: (i, 0)), _bs((None, f, D), lambda j, i: (j, 0, 0)), slab, slab] + _any_specs(len(deps)),
        out_specs=[slab] * 2, compiler_params=_params("parallel", "parallel"), name=f"{tag}_bwd_up")(dy_half, w2, u, g, *deps)
    dw1 = _mm_wgrad(f"{tag}_bwd_dw1", du, h, a_cols=None, b_cols=None, tm=f, tn=D, J=J)
    dw3 = _mm_wgrad(f"{tag}_bwd_dw3", dg, h, a_cols=None, b_cols=None, tm=f, tn=D, J=J)
    deps = deps + put_g({f"{tag}_w1": dw1, f"{tag}_w3": dw3})
    pairs = []
    for j in range(J):
        a_spec = _bs((None, 512, f), lambda i, j=j: (j, i, 0))
        w_spec = _resident((None, f, D), lambda i, j=j: (j, 0, 0))
        pairs += [(du, a_spec, w1, w_spec), (dg, a_spec, w3, w_spec)]
    return _dh_norm_bwd(f"{tag}_bwd_dh", 512, pairs, NN, x, gain, dy, deps, also_bf16)


def _t5_bucket(rel):
    n = N_BUCKETS // 2
    max_exact = n // 2
    ret = jnp.where(rel > 0, n, 0)
    a = jnp.abs(rel)
    af = jnp.maximum(a, 1).astype(F32)
    large = max_exact + (jnp.log(af / max_exact) / math.log(MAX_DISTANCE / max_exact) * (n - max_exact)).astype(jnp.int32)
    large = jnp.minimum(large, n - 1)
    return ret + jnp.where(a < max_exact, a, large)


WIN_A = QB_A + 2 * BAND_HALF
WIN_SHIFTS = (0, BAND_HALF, 2 * BAND_HALF)


def _window_variant(n, nblk):
    return jnp.where(n == 0, 0, jnp.where(n == nblk - 1, 2, 1))


def _window_start(n, nblk):
    return pl.multiple_of(jnp.clip(n * QB_A - BAND_HALF, 0, nblk * QB_A - WIN_A), BAND_HALF)


def _band_steps(xp=jnp):
    qi = xp.arange(QB_A, dtype=xp.int32)[None, :, None]
    kj = xp.arange(WIN_A, dtype=xp.int32)[None, None, :]
    return kj - qi - xp.asarray(WIN_SHIFTS, dtype=xp.int32)[:, None, None]


def _bias_tiles(rel_bias):
    wide = QB_A + 2 * WIN_SHIFTS[-1]
    qi = jnp.arange(QB_A, dtype=jnp.int32)[:, None]
    steps = jnp.arange(wide, dtype=jnp.int32)[None, :] - WIN_SHIFTS[-1] - qi
    buckets = jnp.stack([_t5_bucket(steps * d) for d in DILATIONS])
    inband = (jnp.abs(steps) <= BAND_HALF).astype(jnp.int32)
    n_heads = rel_bias.shape[1]

    def body(tab_ref, b_ref, m_ref, o_ref):
        hd = pl.program_id(0)
        bkt = b_ref[...]
        acc = jnp.zeros(bkt.shape, F32)
        for b in range(N_BUCKETS):
            acc = jnp.where(bkt == b, tab_ref[b, hd], acc)
        o_ref[...] = jnp.where(m_ref[...] > 0, acc, NEG_INF)

    base = pl.pallas_call(
        body, out_shape=jax.ShapeDtypeStruct((n_heads, QB_A, wide), F32), grid=(n_heads,),
        in_specs=[pl.BlockSpec(memory_space=pltpu.SMEM),
                  _bs((None, QB_A, wide), lambda hd: (hd // HEADS_A, 0, 0)),
                  _bs((QB_A, wide), lambda hd: (0, 0))],
        out_specs=_bs((None, QB_A, wide), lambda hd: (hd, 0, 0)),
        compiler_params=_params("parallel"), name="a_bias_tiles")(rel_bias, buckets, inband)
    base = base.reshape(len(DILATIONS), HEADS_A, QB_A, wide)
    return jnp.stack([base[..., WIN_SHIFTS[-1] - s:WIN_SHIFTS[-1] - s + WIN_A] for s in WIN_SHIFTS], axis=1)


def _bias_grad(dbias):
    steps = _band_steps(np)
    inband = np.abs(steps) <= BAND_HALF
    present = []
    for d in DILATIONS:
        rel = steps * d
        a = np.abs(rel)
        large = 8 + (np.log(np.maximum(a, 1) / 8.0) / math.log(MAX_DISTANCE / 8.0) * 8).astype(np.int64)
        bk = np.where(rel > 0, 16, 0) + np.where(a < 8, a, np.minimum(large, 15))
        present.append([sorted(set(bk[v][inband[v]].tolist())) for v in range(3)])
    buckets = jnp.stack([_t5_bucket(_band_steps() * d) for d in DILATIONS])
    n_heads = len(DILATIONS) * HEADS_A

    def body(b_ref, d_ref, o_ref):
        row = lax.broadcasted_iota(jnp.int32, (N_BUCKETS, n_heads), 0)
        col = lax.broadcasted_iota(jnp.int32, (N_BUCKETS, n_heads), 1)
        out = jnp.zeros((N_BUCKETS, n_heads), F32)
        for grp in range(len(DILATIONS)):
            for hh in range(HEADS_A):
                hd = grp * HEADS_A + hh
                for b in sorted(set(sum(present[grp], []))):
                    tot = jnp.zeros((), F32)
                    for v in range(3):
                        if b in present[grp][v]:
                            tot = tot + jnp.sum(jnp.where(b_ref[grp, v] == b, d_ref[grp, v, hh], 0.0))
                    out = jnp.where((row == b) & (col == hd), tot, out)
        o_ref[...] = out

    return pl.pallas_call(
        body, out_shape=jax.ShapeDtypeStruct((N_BUCKETS, n_heads), F32),
        compiler_params=pltpu.CompilerParams(vmem_limit_bytes=VMEM_LIMIT_BYTES), name="a_bias_grad")(buckets, dbias)


def _lane_is_second_head(shape):
    return lax.broadcasted_iota(jnp.int32, shape, len(shape) - 1) >= HEAD_A


def _group_view(proj, grp, d):
    T = proj.shape[0]
    if d == 1:
        return proj, IN_WIDTH, grp * 3 * WIDTH_A
    part = proj[:, grp * 3 * WIDTH_A:(grp + 1) * 3 * WIDTH_A]
    return part.reshape(T // d, d * 3 * WIDTH_A), 3 * WIDTH_A, 0


def _stack_heads(v2, second):
    zero = jnp.zeros_like(v2)
    return jnp.concatenate([jnp.where(second, zero, v2), jnp.where(second, v2, zero)], axis=0)


def _unstack_heads(v, second):
    return jnp.where(second, v[QB_A:], v[:QB_A])


def _dil_fwd(proj, bias, grp, d):
    T = proj.shape[0]
    L = T // d
    nblk = L // QB_A
    pv, width, base = _group_view(proj, grp, d)
    cb, b0 = width // WIDTH_A, base // WIDTH_A
    W2 = 2 * HEAD_A
    scale = HEAD_A ** -0.5

    def body(q_ref, k_ref, v_ref, b_ref, o_ref, l_ref):
        win = pl.ds(_window_start(pl.program_id(1), nblk), WIN_A)
        second = _lane_is_second_head((QB_A, W2))
        for hp in range(HEADS_A // 2):
            cols = slice(hp * W2, (hp + 1) * W2)
            kw, vw = k_ref[win, cols], v_ref[win, cols]
            qs = _stack_heads(q_ref[:, cols], second)
            s = lax.dot_general(qs, kw, (NT, ((), ())), preferred_element_type=F32)
            s = s * scale + b_ref[2 * hp:2 * hp + 2].reshape(2 * QB_A, WIN_A)
            m = jnp.max(s, axis=-1, keepdims=True)
            p = jnp.exp(s - m)
            l = jnp.sum(p, axis=-1, keepdims=True)
            res = jnp.dot(p.astype(BF16), vw, preferred_element_type=F32) / l
            o_ref[:, cols] = _unstack_heads(res, second).astype(o_ref.dtype)
            l_ref[:, cols] = _unstack_heads(jnp.broadcast_to(m + jnp.log(l), (2 * QB_A, W2)), second)

    in_specs = [_bs((QB_A, WIDTH_A), lambda r, n: (n, r * cb + b0)),
                _bs((L, WIDTH_A), lambda r, n: (0, r * cb + b0 + 1)), _bs((L, WIDTH_A), lambda r, n: (0, r * cb + b0 + 2)),
                _bs((None, HEADS_A, QB_A, WIN_A), lambda r, n: (_window_variant(n, nblk), 0, 0, 0))]
    o, lse = pl.pallas_call(
        body, out_shape=[jax.ShapeDtypeStruct((L, d * WIDTH_A), BF16), jax.ShapeDtypeStruct((L, d * WIDTH_A), F32)],
        grid=(d, nblk), in_specs=in_specs,
        out_specs=[_bs((QB_A, WIDTH_A), lambda r, n: (n, r)), _bs((QB_A, WIDTH_A), lambda r, n: (n, r))],
        compiler_params=_params("parallel", "parallel"), name=f"a_fwd_d{d}")(pv, pv, pv, bias)
    return o.reshape(T, WIDTH_A), lse.reshape(T, WIDTH_A)


def _dil_bwd(proj, bias, do, lse, cterm, grp, d):
    T = proj.shape[0]
    L = T // d
    nblk = L // QB_A
    W2 = 2 * HEAD_A
    PPS = 4
    WS = PPS * W2
    pv, width, base = _group_view(proj, grp, d)
    cb, b0 = width // WS, base // WS
    ob = WIDTH_A // WS
    view = lambda a: a.reshape(L, d * WIDTH_A)
    scale = HEAD_A ** -0.5

    def body(q_ref, k_ref, v_ref, do_ref, l_ref, c_ref, b_ref, dq_ref, dk_ref, dv_ref, db_ref):
        r, n = pl.program_id(1), pl.program_id(2)

        @pl.when(n == 0)
        def _():
            dk_ref[...] = jnp.zeros_like(dk_ref)
            dv_ref[...] = jnp.zeros_like(dv_ref)

        @pl.when((n == 0) & (r == 0))
        def _():
            db_ref[...] = jnp.zeros_like(db_ref)

        second = _lane_is_second_head((QB_A, W2))
        win = pl.ds(_window_start(n, nblk), WIN_A)
        variant = _window_variant(n, nblk)
        for pp in range(PPS):
            cols = slice(pp * W2, (pp + 1) * W2)
            kw, vw = k_ref[win, cols], v_ref[win, cols]
            qs, dos = _stack_heads(q_ref[:, cols], second), _stack_heads(do_ref[:, cols], second)
            lse2, c2 = l_ref[:, cols], c_ref[:, cols]
            lse_rows = jnp.concatenate([lse2[:, 0:1], lse2[:, HEAD_A:HEAD_A + 1]], axis=0)
            c_rows = jnp.concatenate([c2[:, 0:1], c2[:, HEAD_A:HEAD_A + 1]], axis=0)
            s = lax.dot_general(qs, kw, (NT, ((), ())), preferred_element_type=F32)
            p = jnp.exp(s * scale + b_ref[2 * pp:2 * pp + 2].reshape(2 * QB_A, WIN_A) - lse_rows)
            dp = lax.dot_general(dos, vw, (NT, ((), ())), preferred_element_type=F32)
            ds = p * (dp + c_rows)
            db_ref[variant, 2 * pp:2 * pp + 2] += ds.reshape(2, QB_A, WIN_A)
            pb, dsb = p.astype(BF16), (ds * scale).astype(BF16)
            dq_ref[:, cols] = _unstack_heads(jnp.dot(dsb, kw, preferred_element_type=F32), second).astype(dq_ref.dtype)
            dk_ref[win, cols] += lax.dot_general(dsb, qs, (TN, ((), ())), preferred_element_type=F32)
            dv_ref[win, cols] += lax.dot_general(pb, dos, (TN, ((), ())), preferred_element_type=F32)

    in_specs = [_bs((QB_A, WS), lambda hp, r, n: (n, r * cb + b0 + hp)),
                _resident((L, WS), lambda hp, r, n: (0, r * cb + b0 + ob + hp)),
                _resident((L, WS), lambda hp, r, n: (0, r * cb + b0 + 2 * ob + hp))]
    in_specs += [_bs((QB_A, WS), lambda hp, r, n: (n, r * ob + hp))] * 3
    in_specs += [_bs((None, 2 * PPS, QB_A, WIN_A), lambda hp, r, n: (_window_variant(n, nblk), hp, 0, 0))]
    out_shape = [jax.ShapeDtypeStruct((L, d * WIDTH_A), BF16), jax.ShapeDtypeStruct((L, d * WIDTH_A), F32),
                 jax.ShapeDtypeStruct((L, d * WIDTH_A), F32), jax.ShapeDtypeStruct((3, HEADS_A, QB_A, WIN_A), F32)]
    out_specs = [_bs((QB_A, WS), lambda hp, r, n: (n, r * ob + hp)),
                 _bs((L, WS), lambda hp, r, n: (0, r * ob + hp)), _bs((L, WS), lambda hp, r, n: (0, r * ob + hp)),
                 _bs((3, 2 * PPS, QB_A, WIN_A), lambda hp, r, n: (0, hp, 0, 0))]
    dq, dk, dv, db = pl.pallas_call(
        body, out_shape=out_shape, grid=(ob, d, nblk), in_specs=in_specs, out_specs=out_specs,
        compiler_params=_params("arbitrary", "arbitrary", "arbitrary"), name=f"a_bwd_d{d}")(
            pv, pv, pv, view(do), view(lse), view(cterm), bias)
    return dq.reshape(T, WIDTH_A), dk.reshape(T, WIDTH_A), dv.reshape(T, WIDTH_A), db


def _segment_ones():
    i = np.arange(WIDTH_A)
    return jnp.asarray((i[:, None] // HEAD_A == i[None, :] // HEAD_A).astype(np.float32), dtype=BF16)


def _group_weights(l0, l1, l2):
    m = jnp.maximum(jnp.maximum(l0, l1), l2)
    e = [jnp.exp(l - m) for l in (l0, l1, l2)]
    z = e[0] + e[1] + e[2]
    return [ei / z for ei in e]


def _combine_fwd(outs, lses):
    T = outs[0].shape[0]

    def fn(o0, o1, o2, l0, l1, l2):
        w = _group_weights(l0, l1, l2)
        return w[0] * o0.astype(F32) + w[1] * o1.astype(F32) + w[2] * o2.astype(F32)

    (oa,) = _ew("a_combine", fn, [_tiled(o) for o in outs] + [_tiled(l) for l in lses], [(BF16, WIDTH_A)], n_rows=T, rows=512)
    return oa


def _combine_bwd(doa, outs, lses):
    T = doa.shape[0]

    def fn(d, o0, o1, o2, l0, l1, l2, seg):
        d = d.astype(F32)
        w = _group_weights(l0, l1, l2)
        tot = jnp.zeros(d.shape, F32)
        for wg, og in zip(w, (o0, o1, o2)):
            prod = wg * d * og.astype(F32)
            hi = prod.astype(BF16)
            lo = (prod - hi.astype(F32)).astype(BF16)
            tot = tot + jnp.dot(hi, seg, preferred_element_type=F32) + jnp.dot(lo, seg, preferred_element_type=F32)
        return tuple(wg * d for wg in w) + tuple(-wg * tot for wg in w)

    res = _ew("a_combine_bwd", fn, [_tiled(doa)] + [_tiled(o) for o in outs] + [_tiled(l) for l in lses] + [_whole(_segment_ones())],
              [(BF16, WIDTH_A)] * 3 + [(F32, WIDTH_A)] * 3, n_rows=T, rows=256)
    return res[:3], res[3:]


def _rope_tables(T):
    rows = T // GRID_W
    row = jnp.repeat(jnp.arange(rows, dtype=F32), GRID_W)
    col = jnp.tile(jnp.arange(GRID_W, dtype=F32), rows)
    n_freq = HEAD_B // 4
    freq = ROPE_THETA ** (-jnp.arange(n_freq, dtype=F32) / n_freq)
    ang = jnp.concatenate([row[:, None] * freq, col[:, None] * freq], axis=-1)
    cos, sin = jnp.repeat(jnp.cos(ang), 2, axis=1), jnp.repeat(jnp.sin(ang), 2, axis=1)
    sign = jnp.where(jnp.arange(HEAD_B) % 2 == 0, -1.0, 1.0).astype(F32)
    return cos, sin * sign


def _swap_pairs(v):
    even = lax.broadcasted_iota(jnp.int32, v.shape, v.ndim - 1) % 2 == 0
    n = v.shape[-1]
    return jnp.where(even, pltpu.roll(v, n - 1, v.ndim - 1), pltpu.roll(v, 1, v.ndim - 1))


def _qk_fwd(name, proj, col0, n_heads, gain, cos, sin, out_scale=1.0):
    T = proj.shape[0]

    def fn(xr, g, c, s):
        xn = _norm_fwd(xr.astype(F32), g)
        return (xn * c + _swap_pairs(xn) * s) * out_scale

    (out,) = _ew(name, fn, [_tiled(proj, HEAD_B, col0 // HEAD_B), _whole(gain), _table(cos), _table(sin)],
                 [(BF16, HEAD_B)], n_rows=T, rows=2048, ncols=n_heads)
    return out


def _qk_bwd(name, dout, proj, col0, n_heads, gain, cos, sin, in_scale=1.0):
    T = proj.shape[0]

    def fn(dv, xr, g, c, s):
        dv = dv.astype(F32) * in_scale
        dxn = c * dv + _swap_pairs(s * dv)
        dx, dgr = _norm_bwd(xr.astype(F32), g, dxn)
        return dx, _colsum(dgr)

    dx, dg = _ew(name, fn, [_tiled(dout, HEAD_B, 0), _tiled(proj, HEAD_B, col0 // HEAD_B), _whole(gain),
                            _table(cos), _table(sin)],
                 [(BF16, HEAD_B)], n_rows=T, rows=2048, reds=(HEAD_B,), ncols=n_heads)
    return dx, jnp.sum(dg, axis=0)


def _gqa_fwd(qn, kn, proj):
    T = qn.shape[0]
    GW = 4 * HEAD_B

    def body(q_ref, k_ref, v_ref, o_ref, l_ref):
        k, v = k_ref[...], v_ref[...]
        lane = lax.broadcasted_iota(jnp.int32, (QB_B, HEAD_B), 1)
        lse_all = jnp.zeros((QB_B, HEAD_B), F32)
        for g in range(4):
            cols = slice(g * HEAD_B, (g + 1) * HEAD_B)
            s = lax.dot_general(q_ref[:, cols], k, (NT, ((), ())), preferred_element_type=F32)
            m = jnp.max(s, axis=-1, keepdims=True)
            p = jnp.exp2(s - m)
            l = jnp.sum(p, axis=-1, keepdims=True)
            o = jnp.dot(p.astype(BF16), v, preferred_element_type=F32) / l
            o_ref[:, cols] = o.astype(o_ref.dtype)
            lse_all = jnp.where(lane == g, m + jnp.log2(l), lse_all)
        l_ref[...] = lse_all

    return pl.pallas_call(
        body, out_shape=[jax.ShapeDtypeStruct((T, 2 * GW), BF16), jax.ShapeDtypeStruct((2, T, HEAD_B), F32)],
        grid=(2, T // QB_B),
        in_specs=[_bs((QB_B, GW), lambda kv, i: (i, kv)), _bs((T, HEAD_B), lambda kv, i: (0, kv)),
                  _bs((T, HEAD_B), lambda kv, i: (0, B_V // HEAD_B + kv))],
        out_specs=[_bs((QB_B, GW), lambda kv, i: (i, kv)), _bs((None, QB_B, HEAD_B), lambda kv, i: (kv, i, 0))],
        compiler_params=_params("parallel", "parallel"), name="b_fwd")(qn, kn, proj)


def _gqa_bwd(qn, kn, proj, o, lse, do):
    T = qn.shape[0]
    GW = 4 * HEAD_B

    def body(q_ref, k_ref, v_ref, o_ref, l_ref, do_ref, dq_ref, dk_ref, dv_ref):
        i = pl.program_id(1)

        @pl.when(i == 0)
        def _():
            dk_ref[...] = jnp.zeros_like(dk_ref)
            dv_ref[...] = jnp.zeros_like(dv_ref)

        k, v = k_ref[...], v_ref[...]
        lse_all = l_ref[...]
        for g in range(4):
            cols = slice(g * HEAD_B, (g + 1) * HEAD_B)
            q, dob = q_ref[:, cols], do_ref[:, cols]
            delta = jnp.sum(dob.astype(F32) * o_ref[:, cols].astype(F32), axis=-1, keepdims=True)
            s = lax.dot_general(q, k, (NT, ((), ())), preferred_element_type=F32)
            p = jnp.exp2(s - lse_all[:, g:g + 1])
            dp = lax.dot_general(dob, v, (NT, ((), ())), preferred_element_type=F32)
            ds = (p * (dp - delta)).astype(BF16)
            dq_ref[:, cols] = jnp.dot(ds, k, preferred_element_type=F32).astype(dq_ref.dtype)
            dk_ref[...] += lax.dot_general(ds, q, (TN, ((), ())), preferred_element_type=F32)
            dv_ref[...] += lax.dot_general(p.astype(BF16), dob, (TN, ((), ())), preferred_element_type=F32)

    return pl.pallas_call(
        body, out_shape=[jax.ShapeDtypeStruct((T, 2 * GW), BF16), jax.ShapeDtypeStruct((T, 2 * HEAD_B), F32),
                         jax.ShapeDtypeStruct((T, 2 * HEAD_B), F32)],
        grid=(2, T // QB_B),
        in_specs=[_bs((QB_B, GW), lambda kv, i: (i, kv)), _bs((T, HEAD_B), lambda kv, i: (0, kv)),
                  _bs((T, HEAD_B), lambda kv, i: (0, B_V // HEAD_B + kv)), _bs((QB_B, GW), lambda kv, i: (i, kv)),
                  _bs((None, QB_B, HEAD_B), lambda kv, i: (kv, i, 0)), _bs((QB_B, GW), lambda kv, i: (i, kv))],
        out_specs=[_bs((QB_B, GW), lambda kv, i: (i, kv)), _bs((T, HEAD_B), lambda kv, i: (0, kv)),
                   _bs((T, HEAD_B), lambda kv, i: (0, kv))],
        compiler_params=_params("parallel", "arbitrary"), name="b_bwd")(qn, kn, proj, o, lse, do)


def _local_step(x, target, small, get_w, put_g, deps=()):
    T, D = x.shape
    gs = {}

    (x1, h2), ffn1_saved = _ffn_fwd("ffn1", x, small["ffn1_norm"], get_w, deps, tail_ins=[small["mix_norm"]],
                                    tail_fn=lambda y, g: (y, _norm_fwd(y, g)), tail_outs=(F32, BF16))
    w_in = get_w("w_in", h2)
    nq = w_in.shape[2]
    tpq = nq // WIDTH_A

    def proj_tile(j, k):
        c = j * tpq + k
        return jnp.where(c < 3 * len(DILATIONS), (c % 3) * 3 + c // 3, c)

    proj = _mm("mix_in", (4, tpq),
               [(h2, _resident((T, D), lambda j, k: (0, 0)), w_in, _bs((None, D, WIDTH_A), lambda j, k: (j, 0, k)))],
               jax.ShapeDtypeStruct((T, IN_WIDTH), BF16), _bs((T, WIDTH_A), lambda j, k: (0, proj_tile(j, k))), NN)

    bias = _bias_tiles(small["rel_bias"])
    a_outs, a_lses = [], []
    for grp, d in enumerate(DILATIONS):
        o, l = _dil_fwd(proj, bias[grp], grp, d)
        a_outs.append(o)
        a_lses.append(l)
    o_a = _combine_fwd(a_outs, a_lses)

    cos, sin = _rope_tables(T)
    qn = _qk_fwd("b_qnorm", proj, B_Q, 8, small["q_norm"], cos, sin, out_scale=QK_SCALE_LOG2)
    kn = _qk_fwd("b_knorm", proj, B_K, 2, small["k_norm"], cos, sin)
    o_b, lse_b = _gqa_fwd(qn, kn, proj)

    wa, wb3, w_out3 = get_w("w_branch_a", o_b), get_w("w_branch_b", o_b).reshape(1, D, D), get_w("w_out", o_b).reshape(1, D, D)
    t_a = _mm_cols("mix_branch_a", o_a, wa, tm=512, tn=256, out_dtype=BF16, cat=True)
    t_b = _mm_cols("mix_branch_b", o_b, wb3, tm=512, tn=512, out_dtype=BF16, cat=True)
    bg_a, bg_b = small["b_gate"][:, :D], small["b_gate"][:, D:]

    def merge(ta, tb, ga, gb_, ba, bb):
        sa, sb = _sigmoid(ga.astype(F32) + ba), _sigmoid(gb_.astype(F32) + bb)
        return sa * ta.astype(F32) + sb * tb.astype(F32)

    gate_ins = [_tiled(proj, D, G_A // D), _tiled(proj, D, G_B // D), _whole(bg_a), _whole(bg_b)]
    (merged,) = _ew("mix_merge", merge, [_tiled(t_a), _tiled(t_b)] + gate_ins, [(BF16, D)], n_rows=T, rows=512)
    def mix_tail(acc, xv, g):
        y = xv + acc
        return y, _norm_fwd(y, g)

    row = _bs((512, D), lambda i: (i, 0))
    x2, hn2 = _mm("mix_out", (T // 512,), [(merged, row, w_out3, _resident((None, D, D), lambda i: (0, 0, 0)))],
                  [jax.ShapeDtypeStruct((T, D), F32), jax.ShapeDtypeStruct((T, D), BF16)], [row, row], NN,
                  extras=[(x1, row), (small["ffn2_norm"], _row_spec(small["ffn2_norm"], 512))], epilogue=mix_tail)

    def head(xv, g, tv):
        r = _rstd(xv)
        xh = xv * r
        e = xh * g - tv
        dy = e * (1.0 / D)
        dxh = dy * g
        dx = r * (dxh - xh * jnp.mean(dxh * xh, axis=-1, keepdims=True))
        return dx, 0.5 * dx, _colsum(e * e) * (0.5 / D), _colsum(dy * xh)

    (dx3, dx3_half, loss_cols, g_final), ffn2_saved = _ffn_fwd(
        "ffn2", x2, small["ffn2_norm"], get_w, h=hn2, tail_ins=[small["final_norm"].reshape(1, D), target], tail_fn=head,
        tail_outs=(F32, BF16), tail_reds=(D, D))
    gs["final_norm"] = g_final.reshape(D)

    dx2, _, dmix, gs["ffn2_norm"] = _ffn_bwd("ffn2", x2, small["ffn2_norm"], get_w, put_g, ffn2_saved, dx3, dx3_half,
                                             also_bf16=True)
    g_out = _mm_wgrad("mix_bwd_dwout", merged, dmix, a_cols=D // 4, b_cols=None, tm=256, tn=512, J=4).reshape(D, D)
    dmerged = _mm_rows_t("mix_bwd_dmerged", dmix, w_out3, tm=512, out_dtype=BF16).reshape(T, D)

    def merge_bwd(dm, ta, tb, ga, gb_, ba, bb):
        dm, ta, tb = dm.astype(F32), ta.astype(F32), tb.astype(F32)
        sa, sb = _sigmoid(ga.astype(F32) + ba), _sigmoid(gb_.astype(F32) + bb)
        dga, dgb = dm * ta * sa * (1.0 - sa), dm * tb * sb * (1.0 - sb)
        return dm * sa, dm * sb, dga, dgb, _colsum(dga), _colsum(dgb)

    dta, dtb, dga, dgb, dba, dbb = _ew("mix_bwd_merge", merge_bwd, [_tiled(dmerged), _tiled(t_a), _tiled(t_b)] + gate_ins,
                                       [(BF16, D)] * 4, n_rows=T, rows=256, reds=(D, D))
    gs["b_gate"] = jnp.concatenate([dba.reshape(1, D), dbb.reshape(1, D)], axis=1)

    g_a = _mm_wgrad("mix_bwd_dwa", o_a, dta, a_cols=None, b_cols=D // 4, tm=WIDTH_A, tn=256, J=4)
    g_b = _mm_wgrad("mix_bwd_dwb", o_b, dtb, a_cols=D // 4, b_cols=None, tm=256, tn=512, J=4).reshape(D, D)
    deps = put_g({"w_out": g_out, "w_branch_a": g_a, "w_branch_b": g_b})
    do_a = _mm("mix_bwd_doa", (T // 1024,),
               [(dta, _bs((1024, D // 4), lambda i, j=j: (i, j)), wa, _bs((None, WIDTH_A, D // 4), lambda i, j=j: (j, 0, 0)))
                for j in range(4)],
               jax.ShapeDtypeStruct((T, WIDTH_A), BF16), _bs((1024, WIDTH_A), lambda i: (i, 0)), NT, deps=deps)
    do_b = _mm_rows_t("mix_bwd_dob", dtb, wb3, tm=512, out_dtype=BF16).reshape(T, D)

    dqn, dkn, dv_b = _gqa_bwd(qn, kn, proj, o_b, lse_b, do_b)
    dq_b, gs["q_norm"] = _qk_bwd("b_bwd_qnorm", dqn, proj, B_Q, 8, small["q_norm"], cos, sin, in_scale=HEAD_B ** -0.5)
    dk_b, gs["k_norm"] = _qk_bwd("b_bwd_knorm", dkn, proj, B_K, 2, small["k_norm"], cos, sin, in_scale=1.0 / LOG2_E)

    do_groups, c_groups = _combine_bwd(do_a, a_outs, a_lses)
    dqs, dks, dvs, dbs = [], [], [], []
    for grp, d in enumerate(DILATIONS):
        dq, dk, dv, db = _dil_bwd(proj, bias[grp], do_groups[grp], a_lses[grp], c_groups[grp], grp, d)
        dqs.append(dq), dks.append(dk), dvs.append(dv), dbs.append(db)
    gs["rel_bias"] = _bias_grad(jnp.stack(dbs))

    dproj = jnp.concatenate([p.astype(BF16) for p in dqs + dks + dvs + [dq_b, dk_b, dv_b, dga, dgb]], axis=1)
    nq = w_in.shape[2]
    g_in = _mm("mix_bwd_dwin", (4, tpq),
               [(h2, _resident((T, D), lambda j, k: (0, 0)), dproj, _bs((T, WIDTH_A), lambda j, k: (0, j * tpq + k)))],
               jax.ShapeDtypeStruct((4, D, nq), BF16), _bs((None, D, WIDTH_A), lambda j, k: (j, 0, k)), TN)
    deps = put_g({"w_in": g_in})
    dx1, dx1_half, gs["mix_norm"] = _dh_norm_bwd(
        "mix_bwd_dh", 256,
        [(dproj, _bs((256, nq), lambda i, j=j: (i, j)), w_in, _resident((None, D, nq), lambda i, j=j: (j, 0, 0))) for j in range(4)],
        NT, x1, small["mix_norm"], dx2, deps)

    dx0, _, gs["ffn1_norm"] = _ffn_bwd("ffn1", x, small["ffn1_norm"], get_w, put_g, ffn1_saved, dx1, dx1_half)
    return loss_cols, dx0, gs


def _position():
    return lax.axis_index("x"), lax.axis_index("y"), lax.axis_index("c")


def _any_specs(n):
    return [pl.BlockSpec(memory_space=pl.ANY)] * n


HBM_SPEC = pl.BlockSpec(memory_space=pltpu.HBM)
SEM_SPEC = pl.BlockSpec(memory_space=pltpu.SEMAPHORE)
DATAFLOW_EFFECT = pltpu.SideEffectType.DATAFLOW_SIDE_EFFECTING
N_PEER_CHIPS = 3
LANES = 128


def _quarter_copies(srcs, lands, send_sems, recv_sems, scatter):
    x, y, c = _position()
    me = 2 * x + y
    peers = [(1 - x, y, c), (x, 1 - y, c), (1 - x, 1 - y, c)]
    copies = []
    for src, land, send, recv in zip(srcs, lands, send_sems, recv_sems):
        half = land.shape[1] // 2
        mine = land.at[me, pl.ds(c * half, half)]
        for p, (px, py, pc) in enumerate(peers):
            copies.append(pltpu.make_async_remote_copy(
                src_ref=src.at[2 * px + py] if scatter else mine, dst_ref=land.at[me] if scatter else mine,
                send_sem=send.at[p], recv_sem=recv.at[p], device_id=(px, py, pc), device_id_type=MESH))
    return copies


def _fill_from_sibling(name, stacks):
    n = len(stacks)

    def body(*refs):
        outs = refs[n:2 * n]
        send_sems, recv_sems = refs[2 * n:]
        x, y, c = _position()
        copies = []
        for i, ref in enumerate(outs):
            half = ref.shape[1] // 2
            rows = pl.ds(c * half, half)
            for p, k in enumerate((2 * (1 - x) + y, 2 * x + (1 - y), 2 * (1 - x) + (1 - y))):
                cp = pltpu.make_async_remote_copy(ref.at[k, rows], ref.at[k, rows], send_sems.at[3 * i + p], recv_sems.at[3 * i + p],
                                                  device_id=(x, y, 1 - c), device_id_type=MESH)
                cp.start()
                copies.append(cp)
        for cp in copies:
            cp.wait()

    return pl.pallas_call(
        body, out_shape=[jax.ShapeDtypeStruct(s.shape, s.dtype) for s in stacks],
        in_specs=_any_specs(n), out_specs=_any_specs(n), input_output_aliases={i: i for i in range(n)},
        scratch_shapes=[pltpu.SemaphoreType.DMA((N_PEER_CHIPS * n,)), pltpu.SemaphoreType.DMA((N_PEER_CHIPS * n,))],
        compiler_params=pltpu.CompilerParams(has_side_effects=True), name=name)(*stacks)


def _exchange_start(name, srcs, lands, scatter):
    n = len(lands)
    arrays = list(lands) if srcs is None else list(srcs) + list(lands)
    k = len(arrays)

    def body(*refs):
        land_refs = refs[k - n:k]
        send_sems, recv_sems = refs[k:k + n], refs[k + n:k + 2 * n]
        token = refs[2 * k + 2 * n]
        for cp in _quarter_copies(refs[:n], land_refs, send_sems, recv_sems, scatter):
            cp.start()
        token[...] = jnp.zeros_like(token)

    sem = pltpu.SemaphoreType.DMA((N_PEER_CHIPS,))
    out_shape = [sem] * (2 * n) + [pltpu.HBM(a.shape, a.dtype) for a in arrays] + [jax.ShapeDtypeStruct((8, LANES), F32)]
    res = pl.pallas_call(
        body, name=name, out_shape=out_shape, in_specs=[HBM_SPEC] * k,
        out_specs=[SEM_SPEC] * (2 * n) + [HBM_SPEC] * k + [pl.BlockSpec(memory_space=pltpu.VMEM)],
        input_output_aliases={i: 2 * n + i for i in range(k)},
        compiler_params=pltpu.CompilerParams(has_side_effects=DATAFLOW_EFFECT),
    )(*[pltpu.with_memory_space_constraint(a, pltpu.HBM) for a in arrays])
    thru = res[2 * n:2 * n + k]
    return res[:n], res[n:2 * n], (None if srcs is None else thru[:n]), thru[k - n:], res[2 * n + k]


def _exchange_wait(name, srcs, lands, send_sems, recv_sems, after, scatter):
    n = len(lands)
    arrays = list(lands) if srcs is None else list(srcs) + list(lands)
    k = len(arrays)

    def body(*refs):
        sends, recvs = refs[k:k + n], refs[k + n:k + 2 * n]
        for cp in _quarter_copies(refs[:n], refs[k - n:k], sends, recvs, scatter):
            cp.wait_send()
            cp.wait_recv()

    res = pl.pallas_call(
        body, name=name, out_shape=[pltpu.HBM(a.shape, a.dtype) for a in arrays],
        in_specs=[HBM_SPEC] * k + [SEM_SPEC] * (2 * n) + [pl.BlockSpec(memory_space=pl.ANY)],
        out_specs=[HBM_SPEC] * k, input_output_aliases={i: i for i in range(k)},
        compiler_params=pltpu.CompilerParams(has_side_effects=DATAFLOW_EFFECT),
    )(*arrays, *send_sems, *recv_sems, after)
    return res[k - n:]


def _own_slot(name, src, from_stack=False):
    R, C = src.shape[-2:]
    rows = R // 2
    me = (2 * lax.axis_index("x") + lax.axis_index("y")).astype(jnp.int32).reshape(1)

    def body(me_ref, x_ref, o_ref):
        o_ref[...] = x_ref[...].astype(o_ref.dtype)

    in_spec = (pl.BlockSpec((None, rows, C), lambda i, me_ref: (me_ref[0], i, 0)) if from_stack
               else pl.BlockSpec((rows, C), lambda i, me_ref: (i, 0)))
    grid_spec = pltpu.PrefetchScalarGridSpec(
        num_scalar_prefetch=1, grid=(R // rows,), in_specs=[in_spec],
        out_specs=pl.BlockSpec((None, rows, C), lambda i, me_ref: (me_ref[0], i, 0)))
    return pl.pallas_call(body, out_shape=jax.ShapeDtypeStruct((4, R, C), BF16), grid_spec=grid_spec,
                          compiler_params=_params("parallel"), name=name)(me, src)


def _swap_with_sibling(parts):
    n = len(parts)

    def body(*refs):
        ins, outs = refs[:n], refs[n:2 * n]
        send_sems, recv_sems = refs[2 * n:]
        x, y, c = _position()
        copies = []
        for i in range(n):
            cp = pltpu.make_async_remote_copy(ins[i], outs[i], send_sems.at[i], recv_sems.at[i],
                                              device_id=(x, y, 1 - c), device_id_type=MESH)
            cp.start()
            copies.append(cp)
        for cp in copies:
            cp.wait()

    return pl.pallas_call(
        body, out_shape=[jax.ShapeDtypeStruct(s.shape, s.dtype) for s in parts],
        in_specs=_any_specs(n), out_specs=_any_specs(n),
        scratch_shapes=[pltpu.SemaphoreType.DMA((n,)), pltpu.SemaphoreType.DMA((n,))],
        compiler_params=pltpu.CompilerParams(has_side_effects=True), name="swap_with_sibling")(*parts)


def _allreduce_small(buf):
    R, C = buf.shape
    flips = [(fx, fy, fc) for fx in (0, 1) for fy in (0, 1) for fc in (0, 1)][1:]

    def body(in_ref, out_ref, land_ref, send_sems, recv_sems):
        x, y, c = _position()
        me = 4 * x + 2 * y + c
        copies = []
        for k, (fx, fy, fc) in enumerate(flips):
            px, py, pc = (1 - x if fx else x), (1 - y if fy else y), (1 - c if fc else c)
            cp = pltpu.make_async_remote_copy(in_ref, land_ref.at[me], send_sems.at[k], recv_sems.at[k],
                                              device_id=(px, py, pc), device_id_type=MESH)
            cp.start()
            copies.append(cp)
        land_ref[me] = in_ref[...]
        for cp in copies:
            cp.wait()
        acc = land_ref[0]
        for k in range(1, 8):
            acc = acc + land_ref[k]
        out_ref[...] = acc

    return pl.pallas_call(
        body, out_shape=jax.ShapeDtypeStruct((R, C), F32),
        in_specs=[pl.BlockSpec(memory_space=pltpu.VMEM)], out_specs=pl.BlockSpec(memory_space=pltpu.VMEM),
        scratch_shapes=[pltpu.VMEM((8, R, C), F32), pltpu.SemaphoreType.DMA((7,)), pltpu.SemaphoreType.DMA((7,))],
        compiler_params=pltpu.CompilerParams(has_side_effects=True), name="allreduce_small")(buf)


def _adamw_math(w, g, m, v):
    m2 = ADAM_B1 * m + (1.0 - ADAM_B1) * g
    v2 = ADAM_B2 * v + (1.0 - ADAM_B2) * (g * g)
    m_hat = m2 / (1.0 - ADAM_B1 ** ADAM_STEP)
    v_hat = v2 / (1.0 - ADAM_B2 ** ADAM_STEP)
    delta = -ADAM_LR * (m_hat / (jnp.sqrt(v_hat) + ADAM_EPS) + ADAM_WD * w)
    return delta, m2, v2


def _adamw_big(name, w, m, v, part_mine, part_sibling):
    R, C = w.shape
    rows = 256 if R % 256 == 0 else R // 2 if (R // 2) % 8 == 0 else R

    def fn(wv, mv, vv, a, b):
        g = a + b
        return (g,) + _adamw_math(wv, g, mv, vv)

    return _ew(name, fn, [_tiled(w), _tiled(m), _tiled(v), _tiled(part_mine), _tiled(part_sibling)], [(F32, C)] * 4, n_rows=R, rows=rows)


def _sum_four(name, stack):
    _, R, C = stack.shape
    rows = 256 if R % 256 == 0 else R // 2 if (R // 2) % 8 == 0 else R
    flat = stack.reshape(4 * R, C)
    nrb = R // rows

    def fn(a, b, c, d):
        return ((a.astype(F32) + b.astype(F32)) + c.astype(F32)) + d.astype(F32)

    (out,) = _ew(name, fn, [_tiled(flat, None, 0, k * nrb) for k in range(4)], [(F32, C)], n_rows=R, rows=rows)
    return out


BIG = ("ffn1_w1", "ffn1_w3", "ffn1_w2", "w_in", "w_branch_a", "w_branch_b", "w_out", "ffn2_w1", "ffn2_w3", "ffn2_w2")
SMALL = ("ffn1_norm", "mix_norm", "b_gate", "q_norm", "k_norm", "rel_bias", "ffn2_norm", "final_norm")
ORDER = ("ffn1_norm", "ffn1_w1", "ffn1_w3", "ffn1_w2", "mix_norm", "w_in", "b_gate", "q_norm", "k_norm", "rel_bias",
         "w_branch_a", "w_branch_b", "w_out", "ffn2_norm", "ffn2_w1", "ffn2_w3", "ffn2_w2", "final_norm")
TRANSPOSED = ("ffn1_w1", "ffn1_w3", "ffn2_w1", "ffn2_w3")
GATHER_GROUPS = (("ffn1_w1", "ffn1_w3"), ("ffn1_w2",), ("w_in",), ("w_branch_a", "w_branch_b", "w_out"),
                 ("ffn2_w1", "ffn2_w3", "ffn2_w2"))


def _pack_small(d):
    rows = []
    for n in SMALL:
        flat = d[n].reshape(-1)
        pad = (-flat.shape[0]) % LANES
        rows.append(jnp.pad(flat, (0, pad)).reshape(-1, LANES))
    buf = jnp.concatenate(rows, axis=0)
    return jnp.pad(buf, ((0, (-buf.shape[0]) % 8), (0, 0)))


def _unpack_small(buf, like):
    out, r = {}, 0
    for n in SMALL:
        size = like[n].size
        nr = -(-size // LANES)
        out[n] = buf[r:r + nr].reshape(-1)[:size].reshape(like[n].shape)
        r += nr
    return out


def kernel(x, ffn1_norm, ffn1_w1, ffn1_w3, ffn1_w2, mix_norm, w_in, b_gate, q_norm, k_norm, rel_bias, w_branch_a, w_branch_b, w_out, ffn2_norm, ffn2_w1, ffn2_w3, ffn2_w2, final_norm, loss_target, m_ffn1_norm, m_ffn1_w1, m_ffn1_w3, m_ffn1_w2, m_mix_norm, m_w_in, m_b_gate, m_q_norm, m_k_norm, m_rel_bias, m_w_branch_a, m_w_branch_b, m_w_out, m_ffn2_norm, m_ffn2_w1, m_ffn2_w3, m_ffn2_w2, m_final_norm, v_ffn1_norm, v_ffn1_w1, v_ffn1_w3, v_ffn1_w2, v_mix_norm, v_w_in, v_b_gate, v_q_norm, v_k_norm, v_rel_bias, v_w_branch_a, v_w_branch_b, v_w_out, v_ffn2_norm, v_ffn2_w1, v_ffn2_w3, v_ffn2_w2, v_final_norm):
    given = dict(locals())
    w = {n: given[n] for n in ORDER}
    m = {n: given["m_" + n] for n in ORDER}
    v = {n: given["v_" + n] for n in ORDER}
    T, D = x.shape[1], x.shape[2]

    def stored(a, n):
        a = a.reshape(a.shape[1:])
        return a.T if n in TRANSPOSED else a

    def returned(a, n):
        return (a.T if n in TRANSPOSED else a).reshape(w[n].shape)

    quarter = {n: stored(w[n], n) for n in BIG}
    send, recv, _, land_thru, token = _exchange_start(
        "gather_start", None, [_own_slot(f"own_{n}", quarter[n]) for n in BIG], scatter=False)
    index = {n: i for i, n in enumerate(BIG)}
    ready = {}

    def get_w(name, after):
        if name not in ready:
            group = next(g for g in GATHER_GROUPS if name in g)
            ids = [index[n] for n in group]
            stacks = _exchange_wait("gather_wait_" + group[0], None, [land_thru[i] for i in ids],
                                    [send[i] for i in ids], [recv[i] for i in ids], after, scatter=False)
            stacks = _fill_from_sibling("gather_fill_" + group[0], stacks)
            for n, st in zip(group, stacks):
                ready[n] = st.reshape(D, D) if n in ("w_branch_b", "w_out") else st
        return ready[name]

    in_flight = []

    def put_g(grads):
        names = list(grads)
        stacks = [grads[n].reshape((4,) + quarter[n].shape) for n in names]
        lands = [_own_slot(f"own_grad_{n}", s, from_stack=True) for n, s in zip(names, stacks)]
        started = _exchange_start("scatter_start_" + names[0], stacks, lands, scatter=True)
        in_flight.append((names,) + tuple(started[:4]))
        return [started[4]]

    small = {n: w[n] for n in SMALL}
    loss_cols, grad_x, gs = _local_step(x.reshape(T, D), loss_target.reshape(T, D), small, get_w, put_g, deps=[token])
    loss = lax.psum(jnp.sum(loss_cols), ("x", "y", "c"))

    landed = {}
    for names, s_sem, r_sem, srcs, lands in in_flight:
        got = _exchange_wait("scatter_wait_" + names[0], srcs, lands, s_sem, r_sem, grad_x, scatter=True)
        landed.update(zip(names, got))
    partial = [_sum_four(f"sum4_{n}", landed[n]) for n in BIG]
    other = _swap_with_sibling(partial)
    grads, deltas, new_m, new_v = {}, {}, {}, {}
    for n, mine, theirs in zip(BIG, partial, other):
        res = _adamw_big(f"adamw_{n}", quarter[n], stored(m[n], n), stored(v[n], n), mine, theirs)
        grads[n], deltas[n], new_m[n], new_v[n] = [returned(r, n) for r in res]

    gs = {n: gs[n].reshape(w[n].shape) for n in SMALL}
    g_small = _allreduce_small(_pack_small(gs))
    packed = [_pack_small({n: d[n] for n in SMALL}) for d in (w, m, v)]
    R = g_small.shape[0]
    res = _ew("adamw_small", lambda wv, mv, vv, g: (g,) + _adamw_math(wv, g, mv, vv),
              [_tiled(packed[0]), _tiled(packed[1]), _tiled(packed[2]), _tiled(g_small)], [(F32, LANES)] * 4, n_rows=R, rows=R)
    for d, buf in zip((grads, deltas, new_m, new_v), res):
        d.update(_unpack_small(buf, w))

    return (loss, grad_x.reshape(x.shape), *[grads[n] for n in ORDER], *[deltas[n] for n in ORDER],
            *[new_m[n] for n in ORDER], *[new_v[n] for n in ORDER])
```

```python
import functools
import math

import numpy as np
import jax
import jax.numpy as jnp
from jax import lax
from jax.experimental import pallas as pl
from jax.experimental.pallas import tpu as pltpu

F32 = jnp.float32
BF16 = jnp.bfloat16
MESH = pl.DeviceIdType.MESH

NEG_INF = -1e30
EPS = 1e-6
GRID_W = 64
ROPE_THETA = 10000.0
DILATIONS = (1, 4, 16)
BAND_HALF = 64
HEAD_A = 64
HEADS_A = 8
WIDTH_A = HEADS_A * HEAD_A
HEAD_B = 128
LOG2_E = math.log2(math.e)
QK_SCALE_LOG2 = HEAD_B ** -0.5 * LOG2_E
N_BUCKETS = 32
MAX_DISTANCE = 1024
ADAM_LR, ADAM_B1, ADAM_B2, ADAM_EPS, ADAM_WD, ADAM_STEP = 0.001, 0.9, 0.999, 1e-08, 0.01, 10

A_Q, A_K, A_V = 0, 1536, 3072
B_Q, B_K, B_V = 4608, 5632, 5888
G_A, G_B = 6144, 7168
IN_WIDTH = 8192

VMEM_LIMIT_BYTES = 56 * 1024 * 1024
QB_A = 128
QB_B = 256


def _params(*sem):
    return pltpu.CompilerParams(dimension_semantics=sem, vmem_limit_bytes=VMEM_LIMIT_BYTES)


def _bs(shape, fn):
    return pl.BlockSpec(shape, fn)


def _resident(shape, fn):
    return pl.BlockSpec(shape, fn, pipeline_mode=pl.Buffered(1))


def _mm(name, grid, pairs, out_shape, out_spec, dims, *, extras=(), epilogue=None, deps=(), reds=()):
    n_pairs, n_extra, n_deps = len(pairs), len(extras), len(deps)
    operands = [p[0] for p in pairs] + [p[2] for p in pairs] + [e[0] for e in extras] + list(deps)
    in_specs = [p[1] for p in pairs] + [p[3] for p in pairs] + [e[1] for e in extras] + _any_specs(n_deps)
    single = not isinstance(out_shape, (list, tuple))
    out_shapes = [out_shape] if single else list(out_shape)
    out_specs = [out_spec] if single else list(out_spec)
    n_out = len(out_shapes)
    out_shapes += [jax.ShapeDtypeStruct((1, w), F32) for w in reds]
    out_specs += [_bs((1, w), lambda *_: (0, 0)) for w in reds]

    def body(*refs):
        a_refs, b_refs = refs[:n_pairs], refs[n_pairs:2 * n_pairs]
        e_refs = refs[2 * n_pairs:2 * n_pairs + n_extra]
        o_refs = refs[2 * n_pairs + n_extra + n_deps:]
        acc = None
        for a_ref, b_ref in zip(a_refs, b_refs):
            t = lax.dot_general(a_ref[...], b_ref[...], (dims, ((), ())), preferred_element_type=F32)
            acc = t if acc is None else acc + t
        vals = acc if epilogue is None else epilogue(acc, *[e[...] for e in e_refs])
        if not isinstance(vals, (list, tuple)):
            vals = (vals,)
        for o_ref, v in zip(o_refs[:n_out], vals[:n_out]):
            o_ref[...] = v.astype(o_ref.dtype)
        if reds:
            first = functools.reduce(jnp.logical_and, [pl.program_id(ax) == 0 for ax in range(len(grid))])
            for r_ref, v in zip(o_refs[n_out:], vals[n_out:]):
                @pl.when(first)
                def _(r_ref=r_ref):
                    r_ref[...] = jnp.zeros_like(r_ref)
                r_ref[...] += v

    sem = ["arbitrary" if reds else "parallel"] * len(grid)
    res = pl.pallas_call(
        body, out_shape=out_shapes, grid=grid, in_specs=in_specs, out_specs=out_specs,
        compiler_params=_params(*sem), name=name)(*operands)
    return res[0] if (single and not reds) else res


NN = ((1,), (0,))
NT = ((1,), (1,))
TN = ((0,), (0,))


def _mm_cols(name, a, w, *, tm, tn, out_dtype, cat, extras=(), epilogue=None):
    M, K = a.shape
    J, _, n = w.shape
    tn = min(tn, n)
    nb = n // tn
    if cat:
        shape, spec = (M, J * n), _bs((tm, tn), lambda j, i, k: (i, j * nb + k))
    else:
        shape, spec = (J, M, n), _bs((None, tm, tn), lambda j, i, k: (j, i, k))
    ex = [(e, _bs((tm, tn), lambda j, i, k: (i, j * nb + k))) for e in extras]
    return _mm(name, (J, M // tm, nb),
               [(a, _bs((tm, K), lambda j, i, k: (i, 0)), w, _bs((None, K, tn), lambda j, i, k: (j, 0, k)))],
               jax.ShapeDtypeStruct(shape, out_dtype), spec, NN, extras=ex, epilogue=epilogue)


def _mm_rows_t(name, a, w, *, tm, out_dtype):
    M, N = a.shape
    J, f, _ = w.shape
    return _mm(name, (J, M // tm),
               [(a, _bs((tm, N), lambda j, i: (i, 0)), w, _bs((None, f, N), lambda j, i: (j, 0, 0)))],
               jax.ShapeDtypeStruct((J, M, f), out_dtype), _bs((None, tm, f), lambda j, i: (j, i, 0)), NT)


def _mm_wgrad(name, a, b, *, a_cols, b_cols, tm, tn, J):
    def pick(arr, cols, t):
        if arr.ndim == 3:
            T, c = arr.shape[1], arr.shape[2]
            t = min(t, c)
            return T, c, t, (lambda sel: _bs((None, T, t), lambda j, i, k: (j, 0, sel(i, k))))
        T = arr.shape[0]
        c = arr.shape[1] if cols is None else cols
        t = min(t, c)
        per = c // t
        if cols is None:
            if per == 1:
                return T, c, t, (lambda sel: _resident((T, t), lambda j, i, k: (0, 0)))
            return T, c, t, (lambda sel: _bs((T, t), lambda j, i, k: (0, sel(i, k))))
        return T, c, t, (lambda sel: _bs((T, t), lambda j, i, k: (0, j * per + sel(i, k))))
    _, ca, tm, mk_a = pick(a, a_cols, tm)
    _, cb, tn, mk_b = pick(b, b_cols, tn)
    return _mm(name, (J, ca // tm, cb // tn),
               [(a, mk_a(lambda i, k: i), b, mk_b(lambda i, k: k))],
               jax.ShapeDtypeStruct((J, ca, cb), BF16), _bs((None, tm, tn), lambda j, i, k: (j, i, k)), TN)


def _tiled(arr, width=None, col=0, rowblk=0):
    return ("t", arr, arr.shape[1] if width is None else width, col, rowblk)


def _table(arr):
    return ("f", arr)


def _whole(arr):
    return ("w", arr)


def _ew(name, fn, ins, outs, *, n_rows, rows, reds=(), ncols=1, deps=()):
    nrb = n_rows // rows
    n_deps = len(deps)
    operands, in_specs = [], []
    for spec in ins:
        if spec[0] == "t":
            _, arr, width, col, rowblk = spec
            step = 1 if ncols > 1 else 0
            in_specs.append(_bs((rows, width), lambda c, i, col=col, rowblk=rowblk, step=step: (rowblk + i, col + c * step)))
        elif spec[0] == "f":
            arr = spec[1]
            in_specs.append(_bs((rows, arr.shape[1]), lambda c, i: (i, 0)))
        else:
            arr = spec[1]
            nd = arr.ndim
            if nd == 3:
                in_specs.append(_bs((None,) + arr.shape[1:], lambda c, i: (c, 0, 0)))
            else:
                in_specs.append(_bs(arr.shape, lambda c, i, nd=nd: (0,) * nd))
        operands.append(arr)
    out_shapes = [jax.ShapeDtypeStruct((n_rows, ncols * w), dt) for dt, w in outs]
    out_specs = [_bs((rows, w), lambda c, i: (i, c)) for _, w in outs]
    out_shapes += [jax.ShapeDtypeStruct((ncols, 1, w), F32) for w in reds]
    out_specs += [_bs((None, 1, w), lambda c, i: (c, 0, 0)) for w in reds]
    n_in, n_out, n_red = len(ins), len(outs), len(reds)
    operands += list(deps)
    in_specs += _any_specs(n_deps)

    def body(*refs):
        vals = fn(*[r[...] for r in refs[:n_in]])
        if not isinstance(vals, (tuple, list)):
            vals = (vals,)
        o_refs = refs[n_in + n_deps:]
        for o_ref, v in zip(o_refs[:n_out], vals[:n_out]):
            o_ref[...] = v.astype(o_ref.dtype)
        if n_red:
            i = pl.program_id(1)
            for r_ref, v in zip(o_refs[n_out:], vals[n_out:]):
                @pl.when(i == 0)
                def _(r_ref=r_ref):
                    r_ref[...] = jnp.zeros_like(r_ref)
                r_ref[...] += v

    res = pl.pallas_call(
        body, out_shape=out_shapes, grid=(ncols, nrb), in_specs=in_specs, out_specs=out_specs,
        compiler_params=_params("parallel", "arbitrary" if n_red else "parallel"), name=name)(*operands)
    return res


def _colsum(v):
    return jnp.sum(v, axis=0, keepdims=True)


def _rstd(x):
    return lax.rsqrt(jnp.mean(x * x, axis=-1, keepdims=True) + EPS)


def _sigmoid(x):
    return 1.0 / (1.0 + jnp.exp(-x))


def _norm_fwd(x, g):
    return x * _rstd(x) * g


def _norm_bwd(x, g, dy):
    r = _rstd(x)
    xh = x * r
    dxh = dy * g
    dx = r * (dxh - xh * jnp.mean(dxh * xh, axis=-1, keepdims=True))
    return dx, dy * xh


def _row_spec(arr, rows):
    if arr.shape[0] == 1:
        return _bs(arr.shape, lambda i: (0, 0))
    return _bs((rows, arr.shape[1]), lambda i: (i, 0))


def _ffn_fwd(tag, x, gain, get_w, deps=(), *, h=None, tail_ins=(), tail_fn=None, tail_outs=(F32,), tail_reds=()):
    T, D = x.shape
    if h is None:
        (h,) = _ew(f"{tag}_norm", lambda xv, g: _norm_fwd(xv, g), [_tiled(x), _whole(gain)], [(BF16, D)], n_rows=T, rows=512,
                   deps=deps)
    w1, w3 = get_w(f"{tag}_w1", h), get_w(f"{tag}_w3", h)
    J, f, _ = w1.shape
    tm = 1024

    def up(h_ref, w1_ref, w3_ref, u_ref, g_ref, a_ref):
        hv = h_ref[...]
        u = lax.dot_general(hv, w1_ref[...], (NT, ((), ())), preferred_element_type=F32)
        g = lax.dot_general(hv, w3_ref[...], (NT, ((), ())), preferred_element_type=F32)
        u_ref[...] = u.astype(BF16)
        g_ref[...] = g.astype(BF16)
        a_ref[...] = (u * _sigmoid(u) * g).astype(BF16)

    slab = _bs((None, tm, f), lambda j, i: (j, i, 0))
    w_spec = _bs((None, f, D), lambda j, i: (j, 0, 0))
    u, g, a = pl.pallas_call(
        up, out_shape=[jax.ShapeDtypeStruct((J, T, f), BF16)] * 3, grid=(J, T // tm),
        in_specs=[_bs((tm, D), lambda j, i: (i, 0)), w_spec, w_spec], out_specs=[slab] * 3,
        compiler_params=_params("parallel", "parallel"), name=f"{tag}_up")(h, w1, w3)
    w2 = get_w(f"{tag}_w2", a)
    def tail(acc, xv, *rest):
        y = xv + 0.5 * acc
        return y if tail_fn is None else tail_fn(y, *rest)

    row = _bs((512, D), lambda i: (i, 0))
    res = _mm(f"{tag}_down", (T // 512,),
              [(a, _bs((None, 512, f), lambda i, j=j: (j, i, 0)), w2, _resident((None, f, D), lambda i, j=j: (j, 0, 0)))
               for j in range(J)],
              [jax.ShapeDtypeStruct((T, D), dt) for dt in tail_outs], [row] * len(tail_outs), NN,
              extras=[(x, row)] + [(t, _row_spec(t, 512)) for t in tail_ins], epilogue=tail, reds=tail_reds)
    return res, (h, u, g, a)


def _dh_norm_bwd(name, rows, pairs, dims, x, gain, dres, deps, also_bf16=False):
    T, D = x.shape

    def epilogue(dh, xv, gv, dr):
        dx, dgr = _norm_bwd(xv, gv, dh)
        dx = dx + dr
        return (dx, 0.5 * dx) + ((dx,) if also_bf16 else ()) + (_colsum(dgr),)

    dts = [F32, BF16] + ([BF16] if also_bf16 else [])
    row = _bs((rows, D), lambda i: (i, 0))
    return _mm(name, (T // rows,), pairs, [jax.ShapeDtypeStruct((T, D), dt) for dt in dts], [row] * len(dts), dims,
               extras=[(x, row), (gain, _row_spec(gain, rows)), (dres, row)], epilogue=epilogue, deps=deps, reds=(D,))


def _ffn_bwd(tag, x, gain, get_w, put_g, saved, dy, dy_half, also_bf16=False):
    h, u, g, a = saved
    T, D = x.shape
    w1, w3, w2 = [get_w(f"{tag}_{n}", dy_half) for n in ("w1", "w3", "w2")]
    J, f, _ = w1.shape
    dw2 = _mm_wgrad(f"{tag}_bwd_dw2", a, dy_half, a_cols=None, b_cols=None, tm=f, tn=D, J=J)
    deps = put_g({f"{tag}_w2": dw2})
    tm = 1024

    def up_bwd(dy_ref, w2_ref, u_ref, g_ref, *rest):
        du_ref, dg_ref = rest[-2:]
        da = lax.dot_general(dy_ref[...], w2_ref[...], (NT, ((), ())), preferred_element_type=F32)
        uv, gv = u_ref[...].astype(F32), g_ref[...].astype(F32)
        s = _sigmoid(uv)
        du_ref[...] = (da * gv * (s * (1.0 + uv * (1.0 - s)))).astype(BF16)
        dg_ref[...] = (da * (uv * s)).astype(BF16)

    slab = _bs((None, tm, f), lambda j, i: (j, i, 0))
    du, dg = pl.pallas_call(
        up_bwd, out_shape=[jax.ShapeDtypeStruct((J, T, f), BF16)] * 2, grid=(J, T // tm),
        in_specs=[_bs((tm, D), lambda j, i: (i, 0)), _bs((None, f, D), lambda j, i: (j, 0, 0)), slab, slab] + _any_specs(len(deps)),
        out_specs=[slab] * 2, compiler_params=_params("parallel", "parallel"), name=f"{tag}_bwd_up")(dy_half, w2, u, g, *deps)
    dw1 = _mm_wgrad(f"{tag}_bwd_dw1", du, h, a_cols=None, b_cols=None, tm=f, tn=D, J=J)
    dw3 = _mm_wgrad(f"{tag}_bwd_dw3", dg, h, a_cols=None, b_cols=None, tm=f, tn=D, J=J)
    deps = deps + put_g({f"{tag}_w1": dw1, f"{tag}_w3": dw3})
    pairs = []
    for j in range(J):
        a_spec = _bs((None, 512, f), lambda i, j=j: (j, i, 0))
        w_spec = _resident((None, f, D), lambda i, j=j: (j, 0, 0))
        pairs += [(du, a_spec, w1, w_spec), (dg, a_spec, w3, w_spec)]
    return _dh_norm_bwd(f"{tag}_bwd_dh", 512, pairs, NN, x, gain, dy, deps, also_bf16)


def _t5_bucket(rel):
    n = N_BUCKETS // 2
    max_exact = n // 2
    ret = jnp.where(rel > 0, n, 0)
    a = jnp.abs(rel)
    af = jnp.maximum(a, 1).astype(F32)
    large = max_exact + (jnp.log(af / max_exact) / math.log(MAX_DISTANCE / max_exact) * (n - max_exact)).astype(jnp.int32)
    large = jnp.minimum(large, n - 1)
    return ret + jnp.where(a < max_exact, a, large)


WIN_A = QB_A + 2 * BAND_HALF
WIN_SHIFTS = (0, BAND_HALF, 2 * BAND_HALF)


def _window_variant(n, nblk):
    return jnp.where(n == 0, 0, jnp.where(n == nblk - 1, 2, 1))


def _window_start(n, nblk):
    return pl.multiple_of(jnp.clip(n * QB_A - BAND_HALF, 0, nblk * QB_A - WIN_A), BAND_HALF)


def _band_steps(xp=jnp):
    qi = xp.arange(QB_A, dtype=xp.int32)[None, :, None]
    kj = xp.arange(WIN_A, dtype=xp.int32)[None, None, :]
    return kj - qi - xp.asarray(WIN_SHIFTS, dtype=xp.int32)[:, None, None]


def _bias_tiles(rel_bias):
    wide = QB_A + 2 * WIN_SHIFTS[-1]
    qi = jnp.arange(QB_A, dtype=jnp.int32)[:, None]
    steps = jnp.arange(wide, dtype=jnp.int32)[None, :] - WIN_SHIFTS[-1] - qi
    buckets = jnp.stack([_t5_bucket(steps * d) for d in DILATIONS])
    inband = (jnp.abs(steps) <= BAND_HALF).astype(jnp.int32)
    n_heads = rel_bias.shape[1]

    def body(tab_ref, b_ref, m_ref, o_ref):
        hd = pl.program_id(0)
        bkt = b_ref[...]
        acc = jnp.zeros(bkt.shape, F32)
        for b in range(N_BUCKETS):
            acc = jnp.where(bkt == b, tab_ref[b, hd], acc)
        o_ref[...] = jnp.where(m_ref[...] > 0, acc, NEG_INF)

    base = pl.pallas_call(
        body, out_shape=jax.ShapeDtypeStruct((n_heads, QB_A, wide), F32), grid=(n_heads,),
        in_specs=[pl.BlockSpec(memory_space=pltpu.SMEM),
                  _bs((None, QB_A, wide), lambda hd: (hd // HEADS_A, 0, 0)),
                  _bs((QB_A, wide), lambda hd: (0, 0))],
        out_specs=_bs((None, QB_A, wide), lambda hd: (hd, 0, 0)),
        compiler_params=_params("parallel"), name="a_bias_tiles")(rel_bias, buckets, inband)
    base = base.reshape(len(DILATIONS), HEADS_A, QB_A, wide)
    return jnp.stack([base[..., WIN_SHIFTS[-1] - s:WIN_SHIFTS[-1] - s + WIN_A] for s in WIN_SHIFTS], axis=1)


def _bias_grad(dbias):
    steps = _band_steps(np)
    inband = np.abs(steps) <= BAND_HALF
    present = []
    for d in DILATIONS:
        rel = steps * d
        a = np.abs(rel)
        large = 8 + (np.log(np.maximum(a, 1) / 8.0) / math.log(MAX_DISTANCE / 8.0) * 8).astype(np.int64)
        bk = np.where(rel > 0, 16, 0) + np.where(a < 8, a, np.minimum(large, 15))
        present.append([sorted(set(bk[v][inband[v]].tolist())) for v in range(3)])
    buckets = jnp.stack([_t5_bucket(_band_steps() * d) for d in DILATIONS])
    n_heads = len(DILATIONS) * HEADS_A

    def body(b_ref, d_ref, o_ref):
        row = lax.broadcasted_iota(jnp.int32, (N_BUCKETS, n_heads), 0)
        col = lax.broadcasted_iota(jnp.int32, (N_BUCKETS, n_heads), 1)
        out = jnp.zeros((N_BUCKETS, n_heads), F32)
        for grp in range(len(DILATIONS)):
            for hh in range(HEADS_A):
                hd = grp * HEADS_A + hh
                for b in sorted(set(sum(present[grp], []))):
                    tot = jnp.zeros((), F32)
                    for v in range(3):
                        if b in present[grp][v]:
                            tot = tot + jnp.sum(jnp.where(b_ref[grp, v] == b, d_ref[grp, v, hh], 0.0))
                    out = jnp.where((row == b) & (col == hd), tot, out)
        o_ref[...] = out

    return pl.pallas_call(
        body, out_shape=jax.ShapeDtypeStruct((N_BUCKETS, n_heads), F32),
        compiler_params=pltpu.CompilerParams(vmem_limit_bytes=VMEM_LIMIT_BYTES), name="a_bias_grad")(buckets, dbias)


def _lane_is_second_head(shape):
    return lax.broadcasted_iota(jnp.int32, shape, len(shape) - 1) >= HEAD_A


VIEW_ROWS = 512


def _view_chunks():
    return [pltpu.VMEM((VIEW_ROWS, LANES), F32)] * (WIDTH_A // LANES)


def _rows_to_view(x_ref, col, o_ref, ocol, d, chunks):
    n = VIEW_ROWS // d
    for c, scr in enumerate(chunks):
        scr[...] = x_ref[:, col + c * LANES:col + (c + 1) * LANES].astype(F32)
        for r in range(d):
            at = ocol + r * WIDTH_A + c * LANES
            o_ref[:, at:at + LANES] = scr[pl.ds(r, n, stride=d), :].astype(o_ref.dtype)


def _view_to_rows(v_ref, o_ref, col, d, chunks):
    n = VIEW_ROWS // d
    for c, scr in enumerate(chunks):
        if d == 1:
            o_ref[:, col + c * LANES:col + (c + 1) * LANES] = v_ref[:, c * LANES:(c + 1) * LANES].astype(o_ref.dtype)
            continue
        for r in range(d):
            scr[pl.ds(r, n, stride=d), :] = v_ref[:, r * WIDTH_A + c * LANES:r * WIDTH_A + (c + 1) * LANES].astype(F32)
        o_ref[:, col + c * LANES:col + (c + 1) * LANES] = scr[...].astype(o_ref.dtype)


def _group_view(proj, grp, d):
    T = proj.shape[0]
    if d == 1:
        return proj, (lambda part, r: grp * 3 + part)

    def body(x_ref, o_ref, *chunks):
        for part in range(3):
            _rows_to_view(x_ref, part * WIDTH_A, o_ref, part * d * WIDTH_A, d, chunks)

    view = pl.pallas_call(
        body, out_shape=jax.ShapeDtypeStruct((T // d, 3 * d * WIDTH_A), proj.dtype), grid=(T // VIEW_ROWS,),
        in_specs=[_bs((VIEW_ROWS, 3 * WIDTH_A), lambda i: (i, grp))],
        out_specs=_bs((VIEW_ROWS // d, 3 * d * WIDTH_A), lambda i: (i, 0)),
        scratch_shapes=_view_chunks(), compiler_params=_params("parallel"), name=f"a_view_d{d}")(proj)
    return view, (lambda part, r: part * d + r)


def _stack_heads(v2, second):
    zero = jnp.zeros_like(v2)
    return jnp.concatenate([jnp.where(second, zero, v2), jnp.where(second, v2, zero)], axis=0)


def _unstack_heads(v, second):
    return jnp.where(second, v[QB_A:], v[:QB_A])


def _dil_fwd(view, bias, d):
    pv, colblk = view
    L = pv.shape[0]
    nblk = L // QB_A
    W2 = 2 * HEAD_A
    scale = HEAD_A ** -0.5

    def body(q_ref, k_ref, v_ref, b_ref, o_ref, l_ref):
        win = pl.ds(_window_start(pl.program_id(1), nblk), WIN_A)
        second = _lane_is_second_head((QB_A, W2))
        for hp in range(HEADS_A // 2):
            cols = slice(hp * W2, (hp + 1) * W2)
            kw, vw = k_ref[win, cols], v_ref[win, cols]
            qs = _stack_heads(q_ref[:, cols], second)
            s = lax.dot_general(qs, kw, (NT, ((), ())), preferred_element_type=F32)
            s = s * scale + b_ref[2 * hp:2 * hp + 2].reshape(2 * QB_A, WIN_A)
            m = jnp.max(s, axis=-1, keepdims=True)
            p = jnp.exp(s - m)
            l = jnp.sum(p, axis=-1, keepdims=True)
            res = jnp.dot(p.astype(BF16), vw, preferred_element_type=F32) / l
            o_ref[:, cols] = _unstack_heads(res, second).astype(o_ref.dtype)
            l_ref[:, cols] = _unstack_heads(jnp.broadcast_to(m + jnp.log(l), (2 * QB_A, W2)), second)

    in_specs = [_bs((QB_A, WIDTH_A), lambda r, n: (n, colblk(0, r))),
                _bs((L, WIDTH_A), lambda r, n: (0, colblk(1, r))), _bs((L, WIDTH_A), lambda r, n: (0, colblk(2, r))),
                _bs((None, HEADS_A, QB_A, WIN_A), lambda r, n: (_window_variant(n, nblk), 0, 0, 0))]
    o, lse = pl.pallas_call(
        body, out_shape=[jax.ShapeDtypeStruct((L, d * WIDTH_A), BF16), jax.ShapeDtypeStruct((L, d * WIDTH_A), F32)],
        grid=(d, nblk), in_specs=in_specs,
        out_specs=[_bs((QB_A, WIDTH_A), lambda r, n: (n, r)), _bs((QB_A, WIDTH_A), lambda r, n: (n, r))],
        compiler_params=_params("parallel", "parallel"), name=f"a_fwd_d{d}")(pv, pv, pv, bias)
    return o, lse


def _dil_bwd(view_qkv, bias, do, lse, cterm, d):
    pv, colblk = view_qkv
    L = pv.shape[0]
    nblk = L // QB_A
    W2 = 2 * HEAD_A
    PPS = 4
    WS = PPS * W2
    ob = WIDTH_A // WS
    scale = HEAD_A ** -0.5

    def body(q_ref, k_ref, v_ref, do_ref, l_ref, c_ref, b_ref, dq_ref, dk_ref, dv_ref, db_ref):
        r, n = pl.program_id(1), pl.program_id(2)

        @pl.when(n == 0)
        def _():
            dk_ref[...] = jnp.zeros_like(dk_ref)
            dv_ref[...] = jnp.zeros_like(dv_ref)

        @pl.when((n == 0) & (r == 0))
        def _():
            db_ref[...] = jnp.zeros_like(db_ref)

        second = _lane_is_second_head((QB_A, W2))
        win = pl.ds(_window_start(n, nblk), WIN_A)
        variant = _window_variant(n, nblk)
        for pp in range(PPS):
            cols = slice(pp * W2, (pp + 1) * W2)
            kw, vw = k_ref[win, cols], v_ref[win, cols]
            qs, dos = _stack_heads(q_ref[:, cols], second), _stack_heads(do_ref[:, cols], second)
            lse2, c2 = l_ref[:, cols], c_ref[:, cols]
            lse_rows = jnp.concatenate([lse2[:, 0:1], lse2[:, HEAD_A:HEAD_A + 1]], axis=0)
            c_rows = jnp.concatenate([c2[:, 0:1], c2[:, HEAD_A:HEAD_A + 1]], axis=0)
            s = lax.dot_general(qs, kw, (NT, ((), ())), preferred_element_type=F32)
            p = jnp.exp(s * scale + b_ref[2 * pp:2 * pp + 2].reshape(2 * QB_A, WIN_A) - lse_rows)
            dp = lax.dot_general(dos, vw, (NT, ((), ())), preferred_element_type=F32)
            ds = p * (dp + c_rows)
            db_ref[variant, 2 * pp:2 * pp + 2] += ds.reshape(2, QB_A, WIN_A)
            pb, dsb = p.astype(BF16), (ds * scale).astype(BF16)
            dq_ref[:, cols] = _unstack_heads(jnp.dot(dsb, kw, preferred_element_type=F32), second).astype(dq_ref.dtype)
            dk_ref[win, cols] += lax.dot_general(dsb, qs, (TN, ((), ())), preferred_element_type=F32)
            dv_ref[win, cols] += lax.dot_general(pb, dos, (TN, ((), ())), preferred_element_type=F32)

    in_specs = [_bs((QB_A, WS), lambda hp, r, n: (n, colblk(0, r) * ob + hp)),
                _resident((L, WS), lambda hp, r, n: (0, colblk(1, r) * ob + hp)),
                _resident((L, WS), lambda hp, r, n: (0, colblk(2, r) * ob + hp))]
    in_specs += [_bs((QB_A, WS), lambda hp, r, n: (n, r * ob + hp))] * 3
    in_specs += [_bs((None, 2 * PPS, QB_A, WIN_A), lambda hp, r, n: (_window_variant(n, nblk), hp, 0, 0))]
    out_shape = [jax.ShapeDtypeStruct((L, d * WIDTH_A), BF16), jax.ShapeDtypeStruct((L, d * WIDTH_A), F32),
                 jax.ShapeDtypeStruct((L, d * WIDTH_A), F32), jax.ShapeDtypeStruct((3, HEADS_A, QB_A, WIN_A), F32)]
    out_specs = [_bs((QB_A, WS), lambda hp, r, n: (n, r * ob + hp)),
                 _bs((L, WS), lambda hp, r, n: (0, r * ob + hp)), _bs((L, WS), lambda hp, r, n: (0, r * ob + hp)),
                 _bs((3, 2 * PPS, QB_A, WIN_A), lambda hp, r, n: (0, hp, 0, 0))]
    dq, dk, dv, db = pl.pallas_call(
        body, out_shape=out_shape, grid=(ob, d, nblk), in_specs=in_specs, out_specs=out_specs,
        compiler_params=_params("arbitrary", "arbitrary", "arbitrary"), name=f"a_bwd_d{d}")(
            pv, pv, pv, do, lse, cterm, bias)
    return dq, dk, dv, db


def _assemble_dproj(a_parts, dq_b, dk_b, dv_b, dga, dgb):
    T = dq_b.shape[0]
    flat = [(a_parts[part][g], d) for part in range(3) for g, d in enumerate(DILATIONS)]
    rest = [dq_b, dk_b, dv_b, dga, dgb]

    def body(*refs):
        views, others = refs[:len(flat)], refs[len(flat):len(flat) + len(rest)]
        o_ref, chunks = refs[len(flat) + len(rest)], refs[len(flat) + len(rest) + 1:]
        col = 0
        for v_ref, (_, d) in zip(views, flat):
            _view_to_rows(v_ref, o_ref, col, d, chunks)
            col += WIDTH_A
        for x_ref in others:
            w = x_ref.shape[1]
            o_ref[:, col:col + w] = x_ref[...].astype(o_ref.dtype)
            col += w

    in_specs = [_bs((VIEW_ROWS // d, d * WIDTH_A), lambda i: (i, 0)) for _, d in flat]
    in_specs += [_bs((VIEW_ROWS, x.shape[1]), lambda i: (i, 0)) for x in rest]
    return pl.pallas_call(
        body, out_shape=jax.ShapeDtypeStruct((T, IN_WIDTH), BF16), grid=(T // VIEW_ROWS,), in_specs=in_specs,
        out_specs=_bs((VIEW_ROWS, IN_WIDTH), lambda i: (i, 0)), scratch_shapes=_view_chunks(),
        compiler_params=_params("parallel"), name="mix_bwd_dproj")(*[a for a, _ in flat], *rest)


def _segment_ones():
    i = np.arange(WIDTH_A)
    return jnp.asarray((i[:, None] // HEAD_A == i[None, :] // HEAD_A).astype(np.float32), dtype=BF16)


def _group_weights(l0, l1, l2):
    m = jnp.maximum(jnp.maximum(l0, l1), l2)
    e = [jnp.exp(l - m) for l in (l0, l1, l2)]
    z = e[0] + e[1] + e[2]
    return [ei / z for ei in e]


def _view_specs():
    return [_bs((VIEW_ROWS // d, d * WIDTH_A), lambda i: (i, 0)) for d in DILATIONS]


def _stage_tiles(n):
    return [pltpu.VMEM((VIEW_ROWS, WIDTH_A), F32)] * n


def _combine_fwd(outs, lses):
    T = outs[0].shape[0] * DILATIONS[0]
    n = len(DILATIONS)

    def body(*refs):
        o_refs, l_refs, oa_ref = refs[:n], refs[n:2 * n], refs[2 * n]
        o_st, l_st, chunks = refs[2 * n + 1:3 * n + 1], refs[3 * n + 1:4 * n + 1], refs[4 * n + 1:]
        for g, d in enumerate(DILATIONS):
            _view_to_rows(o_refs[g], o_st[g], 0, d, chunks)
            _view_to_rows(l_refs[g], l_st[g], 0, d, chunks)
        w = _group_weights(*[l[...] for l in l_st])
        oa_ref[...] = (w[0] * o_st[0][...] + w[1] * o_st[1][...] + w[2] * o_st[2][...]).astype(oa_ref.dtype)

    return pl.pallas_call(
        body, out_shape=jax.ShapeDtypeStruct((T, WIDTH_A), BF16), grid=(T // VIEW_ROWS,),
        in_specs=_view_specs() * 2, out_specs=_bs((VIEW_ROWS, WIDTH_A), lambda i: (i, 0)),
        scratch_shapes=_stage_tiles(2 * n) + _view_chunks(), compiler_params=_params("parallel"), name="a_combine")(*outs, *lses)


def _combine_bwd(doa, outs, lses):
    T = doa.shape[0]
    n = len(DILATIONS)

    def body(*refs):
        d_ref, o_refs, l_refs, seg_ref = refs[0], refs[1:n + 1], refs[n + 1:2 * n + 1], refs[2 * n + 1]
        do_refs, c_refs = refs[2 * n + 2:3 * n + 2], refs[3 * n + 2:4 * n + 2]
        o_st, l_st = refs[4 * n + 2:5 * n + 2], refs[5 * n + 2:6 * n + 2]
        tmp, chunks = refs[6 * n + 2], refs[6 * n + 3:]
        for g, d in enumerate(DILATIONS):
            _view_to_rows(o_refs[g], o_st[g], 0, d, chunks)
            _view_to_rows(l_refs[g], l_st[g], 0, d, chunks)
        dv = d_ref[...].astype(F32)
        w = _group_weights(*[l[...] for l in l_st])
        seg = seg_ref[...]
        tot = jnp.zeros(dv.shape, F32)
        for g in range(n):
            prod = w[g] * dv * o_st[g][...]
            hi = prod.astype(BF16)
            lo = (prod - hi.astype(F32)).astype(BF16)
            tot = tot + jnp.dot(hi, seg, preferred_element_type=F32) + jnp.dot(lo, seg, preferred_element_type=F32)
        for g, d in enumerate(DILATIONS):
            tmp[...] = w[g] * dv
            _rows_to_view(tmp, 0, do_refs[g], 0, d, chunks)
            tmp[...] = -w[g] * tot
            _rows_to_view(tmp, 0, c_refs[g], 0, d, chunks)

    views = [jax.ShapeDtypeStruct((T // d, d * WIDTH_A), dt) for dt in (BF16, F32) for d in DILATIONS]
    res = pl.pallas_call(
        body, out_shape=views, grid=(T // VIEW_ROWS,),
        in_specs=[_bs((VIEW_ROWS, WIDTH_A), lambda i: (i, 0))] + _view_specs() * 2 + [_bs((WIDTH_A, WIDTH_A), lambda i: (0, 0))],
        out_specs=_view_specs() * 2, scratch_shapes=_stage_tiles(2 * n + 1) + _view_chunks(),
        compiler_params=_params("parallel"), name="a_combine_bwd")(doa, *outs, *lses, _segment_ones())
    return res[:n], res[n:]


def _rope_tables(T):
    rows = T // GRID_W
    row = jnp.repeat(jnp.arange(rows, dtype=F32), GRID_W)
    col = jnp.tile(jnp.arange(GRID_W, dtype=F32), rows)
    n_freq = HEAD_B // 4
    freq = ROPE_THETA ** (-jnp.arange(n_freq, dtype=F32) / n_freq)
    ang = jnp.concatenate([row[:, None] * freq, col[:, None] * freq], axis=-1)
    cos, sin = jnp.repeat(jnp.cos(ang), 2, axis=1), jnp.repeat(jnp.sin(ang), 2, axis=1)
    sign = jnp.where(jnp.arange(HEAD_B) % 2 == 0, -1.0, 1.0).astype(F32)
    return cos, sin * sign


def _swap_pairs(v):
    even = lax.broadcasted_iota(jnp.int32, v.shape, v.ndim - 1) % 2 == 0
    n = v.shape[-1]
    return jnp.where(even, pltpu.roll(v, n - 1, v.ndim - 1), pltpu.roll(v, 1, v.ndim - 1))


def _qk_fwd(name, proj, col0, n_heads, gain, cos, sin, out_scale=1.0):
    T = proj.shape[0]

    def fn(xr, g, c, s):
        xn = _norm_fwd(xr.astype(F32), g)
        return (xn * c + _swap_pairs(xn) * s) * out_scale

    (out,) = _ew(name, fn, [_tiled(proj, HEAD_B, col0 // HEAD_B), _whole(gain), _table(cos), _table(sin)],
                 [(BF16, HEAD_B)], n_rows=T, rows=2048, ncols=n_heads)
    return out


def _qk_bwd(name, dout, proj, col0, n_heads, gain, cos, sin, in_scale=1.0):
    T = proj.shape[0]

    def fn(dv, xr, g, c, s):
        dv = dv.astype(F32) * in_scale
        dxn = c * dv + _swap_pairs(s * dv)
        dx, dgr = _norm_bwd(xr.astype(F32), g, dxn)
        return dx, _colsum(dgr)

    dx, dg = _ew(name, fn, [_tiled(dout, HEAD_B, 0), _tiled(proj, HEAD_B, col0 // HEAD_B), _whole(gain),
                            _table(cos), _table(sin)],
                 [(BF16, HEAD_B)], n_rows=T, rows=2048, reds=(HEAD_B,), ncols=n_heads)
    return dx, jnp.sum(dg, axis=0)


def _gqa_fwd(qn, kn, proj):
    T = qn.shape[0]
    GW = 4 * HEAD_B

    def body(q_ref, k_ref, v_ref, o_ref, l_ref):
        k, v = k_ref[...], v_ref[...]
        lane = lax.broadcasted_iota(jnp.int32, (QB_B, HEAD_B), 1)
        lse_all = jnp.zeros((QB_B, HEAD_B), F32)
        for g in range(4):
            cols = slice(g * HEAD_B, (g + 1) * HEAD_B)
            s = lax.dot_general(q_ref[:, cols], k, (NT, ((), ())), preferred_element_type=F32)
            m = jnp.max(s, axis=-1, keepdims=True)
            p = jnp.exp2(s - m)
            l = jnp.sum(p, axis=-1, keepdims=True)
            o = jnp.dot(p.astype(BF16), v, preferred_element_type=F32) / l
            o_ref[:, cols] = o.astype(o_ref.dtype)
            lse_all = jnp.where(lane == g, m + jnp.log2(l), lse_all)
        l_ref[...] = lse_all

    return pl.pallas_call(
        body, out_shape=[jax.ShapeDtypeStruct((T, 2 * GW), BF16), jax.ShapeDtypeStruct((2, T, HEAD_B), F32)],
        grid=(2, T // QB_B),
        in_specs=[_bs((QB_B, GW), lambda kv, i: (i, kv)), _bs((T, HEAD_B), lambda kv, i: (0, kv)),
                  _bs((T, HEAD_B), lambda kv, i: (0, B_V // HEAD_B + kv))],
        out_specs=[_bs((QB_B, GW), lambda kv, i: (i, kv)), _bs((None, QB_B, HEAD_B), lambda kv, i: (kv, i, 0))],
        compiler_params=_params("parallel", "parallel"), name="b_fwd")(qn, kn, proj)


def _gqa_bwd(qn, kn, proj, o, lse, do):
    T = qn.shape[0]
    GW = 4 * HEAD_B

    def body(q_ref, k_ref, v_ref, o_ref, l_ref, do_ref, dq_ref, dk_ref, dv_ref):
        i = pl.program_id(1)

        @pl.when(i == 0)
        def _():
            dk_ref[...] = jnp.zeros_like(dk_ref)
            dv_ref[...] = jnp.zeros_like(dv_ref)

        k, v = k_ref[...], v_ref[...]
        lse_all = l_ref[...]
        for g in range(4):
            cols = slice(g * HEAD_B, (g + 1) * HEAD_B)
            q, dob = q_ref[:, cols], do_ref[:, cols]
            delta = jnp.sum(dob.astype(F32) * o_ref[:, cols].astype(F32), axis=-1, keepdims=True)
            s = lax.dot_general(q, k, (NT, ((), ())), preferred_element_type=F32)
            p = jnp.exp2(s - lse_all[:, g:g + 1])
            dp = lax.dot_general(dob, v, (NT, ((), ())), preferred_element_type=F32)
            ds = (p * (dp - delta)).astype(BF16)
            dq_ref[:, cols] = jnp.dot(ds, k, preferred_element_type=F32).astype(dq_ref.dtype)
            dk_ref[...] += lax.dot_general(ds, q, (TN, ((), ())), preferred_element_type=F32)
            dv_ref[...] += lax.dot_general(p.astype(BF16), dob, (TN, ((), ())), preferred_element_type=F32)

    return pl.pallas_call(
        body, out_shape=[jax.ShapeDtypeStruct((T, 2 * GW), BF16), jax.ShapeDtypeStruct((T, 2 * HEAD_B), F32),
                         jax.ShapeDtypeStruct((T, 2 * HEAD_B), F32)],
        grid=(2, T // QB_B),
        in_specs=[_bs((QB_B, GW), lambda kv, i: (i, kv)), _bs((T, HEAD_B), lambda kv, i: (0, kv)),
                  _bs((T, HEAD_B), lambda kv, i: (0, B_V // HEAD_B + kv)), _bs((QB_B, GW), lambda kv, i: (i, kv)),
                  _bs((None, QB_B, HEAD_B), lambda kv, i: (kv, i, 0)), _bs((QB_B, GW), lambda kv, i: (i, kv))],
        out_specs=[_bs((QB_B, GW), lambda kv, i: (i, kv)), _bs((T, HEAD_B), lambda kv, i: (0, kv)),
                   _bs((T, HEAD_B), lambda kv, i: (0, kv))],
        compiler_params=_params("parallel", "arbitrary"), name="b_bwd")(qn, kn, proj, o, lse, do)


def _local_step(x, target, small, get_w, put_g, deps=()):
    T, D = x.shape
    gs = {}

    (x1, h2), ffn1_saved = _ffn_fwd("ffn1", x, small["ffn1_norm"], get_w, deps, tail_ins=[small["mix_norm"]],
                                    tail_fn=lambda y, g: (y, _norm_fwd(y, g)), tail_outs=(F32, BF16))
    w_in = get_w("w_in", h2)
    nq = w_in.shape[2]
    tpq = nq // WIDTH_A

    def proj_tile(j, k):
        c = j * tpq + k
        return jnp.where(c < 3 * len(DILATIONS), (c % 3) * 3 + c // 3, c)

    proj = _mm("mix_in", (4, tpq),
               [(h2, _resident((T, D), lambda j, k: (0, 0)), w_in, _bs((None, D, WIDTH_A), lambda j, k: (j, 0, k)))],
               jax.ShapeDtypeStruct((T, IN_WIDTH), BF16), _bs((T, WIDTH_A), lambda j, k: (0, proj_tile(j, k))), NN)

    bias = _bias_tiles(small["rel_bias"])
    a_views = [_group_view(proj, grp, d) for grp, d in enumerate(DILATIONS)]
    a_outs, a_lses = [], []
    for grp, d in enumerate(DILATIONS):
        o, l = _dil_fwd(a_views[grp], bias[grp], d)
        a_outs.append(o)
        a_lses.append(l)
    o_a = _combine_fwd(a_outs, a_lses)

    cos, sin = _rope_tables(T)
    qn = _qk_fwd("b_qnorm", proj, B_Q, 8, small["q_norm"], cos, sin, out_scale=QK_SCALE_LOG2)
    kn = _qk_fwd("b_knorm", proj, B_K, 2, small["k_norm"], cos, sin)
    o_b, lse_b = _gqa_fwd(qn, kn, proj)

    wa, wb3, w_out3 = get_w("w_branch_a", o_b), get_w("w_branch_b", o_b).reshape(1, D, D), get_w("w_out", o_b).reshape(1, D, D)
    t_a = _mm_cols("mix_branch_a", o_a, wa, tm=512, tn=256, out_dtype=BF16, cat=True)
    t_b = _mm_cols("mix_branch_b", o_b, wb3, tm=512, tn=512, out_dtype=BF16, cat=True)
    bg_a, bg_b = small["b_gate"][:, :D], small["b_gate"][:, D:]

    def merge(ta, tb, ga, gb_, ba, bb):
        sa, sb = _sigmoid(ga.astype(F32) + ba), _sigmoid(gb_.astype(F32) + bb)
        return sa * ta.astype(F32) + sb * tb.astype(F32)

    gate_ins = [_tiled(proj, D, G_A // D), _tiled(proj, D, G_B // D), _whole(bg_a), _whole(bg_b)]
    (merged,) = _ew("mix_merge", merge, [_tiled(t_a), _tiled(t_b)] + gate_ins, [(BF16, D)], n_rows=T, rows=512)
    def mix_tail(acc, xv, g):
        y = xv + acc
        return y, _norm_fwd(y, g)

    row = _bs((512, D), lambda i: (i, 0))
    x2, hn2 = _mm("mix_out", (T // 512,), [(merged, row, w_out3, _resident((None, D, D), lambda i: (0, 0, 0)))],
                  [jax.ShapeDtypeStruct((T, D), F32), jax.ShapeDtypeStruct((T, D), BF16)], [row, row], NN,
                  extras=[(x1, row), (small["ffn2_norm"], _row_spec(small["ffn2_norm"], 512))], epilogue=mix_tail)

    def head(xv, g, tv):
        r = _rstd(xv)
        xh = xv * r
        e = xh * g - tv
        dy = e * (1.0 / D)
        dxh = dy * g
        dx = r * (dxh - xh * jnp.mean(dxh * xh, axis=-1, keepdims=True))
        return dx, 0.5 * dx, _colsum(e * e) * (0.5 / D), _colsum(dy * xh)

    (dx3, dx3_half, loss_cols, g_final), ffn2_saved = _ffn_fwd(
        "ffn2", x2, small["ffn2_norm"], get_w, h=hn2, tail_ins=[small["final_norm"].reshape(1, D), target], tail_fn=head,
        tail_outs=(F32, BF16), tail_reds=(D, D))
    gs["final_norm"] = g_final.reshape(D)

    dx2, _, dmix, gs["ffn2_norm"] = _ffn_bwd("ffn2", x2, small["ffn2_norm"], get_w, put_g, ffn2_saved, dx3, dx3_half,
                                             also_bf16=True)
    g_out = _mm_wgrad("mix_bwd_dwout", merged, dmix, a_cols=D // 4, b_cols=None, tm=256, tn=512, J=4).reshape(D, D)
    dmerged = _mm_rows_t("mix_bwd_dmerged", dmix, w_out3, tm=512, out_dtype=BF16).reshape(T, D)

    def merge_bwd(dm, ta, tb, ga, gb_, ba, bb):
        dm, ta, tb = dm.astype(F32), ta.astype(F32), tb.astype(F32)
        sa, sb = _sigmoid(ga.astype(F32) + ba), _sigmoid(gb_.astype(F32) + bb)
        dga, dgb = dm * ta * sa * (1.0 - sa), dm * tb * sb * (1.0 - sb)
        return dm * sa, dm * sb, dga, dgb, _colsum(dga), _colsum(dgb)

    dta, dtb, dga, dgb, dba, dbb = _ew("mix_bwd_merge", merge_bwd, [_tiled(dmerged), _tiled(t_a), _tiled(t_b)] + gate_ins,
                                       [(BF16, D)] * 4, n_rows=T, rows=256, reds=(D, D))
    gs["b_gate"] = jnp.concatenate([dba.reshape(1, D), dbb.reshape(1, D)], axis=1)

    g_a = _mm_wgrad("mix_bwd_dwa", o_a, dta, a_cols=None, b_cols=D // 4, tm=WIDTH_A, tn=256, J=4)
    g_b = _mm_wgrad("mix_bwd_dwb", o_b, dtb, a_cols=D // 4, b_cols=None, tm=256, tn=512, J=4).reshape(D, D)
    deps = put_g({"w_out": g_out, "w_branch_a": g_a, "w_branch_b": g_b})
    do_a = _mm("mix_bwd_doa", (T // 1024,),
               [(dta, _bs((1024, D // 4), lambda i, j=j: (i, j)), wa, _bs((None, WIDTH_A, D // 4), lambda i, j=j: (j, 0, 0)))
                for j in range(4)],
               jax.ShapeDtypeStruct((T, WIDTH_A), BF16), _bs((1024, WIDTH_A), lambda i: (i, 0)), NT, deps=deps)
    do_b = _mm_rows_t("mix_bwd_dob", dtb, wb3, tm=512, out_dtype=BF16).reshape(T, D)

    dqn, dkn, dv_b = _gqa_bwd(qn, kn, proj, o_b, lse_b, do_b)
    dq_b, gs["q_norm"] = _qk_bwd("b_bwd_qnorm", dqn, proj, B_Q, 8, small["q_norm"], cos, sin, in_scale=HEAD_B ** -0.5)
    dk_b, gs["k_norm"] = _qk_bwd("b_bwd_knorm", dkn, proj, B_K, 2, small["k_norm"], cos, sin, in_scale=1.0 / LOG2_E)

    do_groups, c_groups = _combine_bwd(do_a, a_outs, a_lses)
    dqs, dks, dvs, dbs = [], [], [], []
    for grp, d in enumerate(DILATIONS):
        dq, dk, dv, db = _dil_bwd(a_views[grp], bias[grp], do_groups[grp], a_lses[grp], c_groups[grp], d)
        dqs.append(dq), dks.append(dk), dvs.append(dv), dbs.append(db)
    gs["rel_bias"] = _bias_grad(jnp.stack(dbs))

    dproj = _assemble_dproj([dqs, dks, dvs], dq_b, dk_b, dv_b, dga, dgb)
    nq = w_in.shape[2]
    g_in = _mm("mix_bwd_dwin", (4, tpq),
               [(h2, _resident((T, D), lambda j, k: (0, 0)), dproj, _bs((T, WIDTH_A), lambda j, k: (0, j * tpq + k)))],
               jax.ShapeDtypeStruct((4, D, nq), BF16), _bs((None, D, WIDTH_A), lambda j, k: (j, 0, k)), TN)
    deps = put_g({"w_in": g_in})
    dx1, dx1_half, gs["mix_norm"] = _dh_norm_bwd(
        "mix_bwd_dh", 256,
        [(dproj, _bs((256, nq), lambda i, j=j: (i, j)), w_in, _resident((None, D, nq), lambda i, j=j: (j, 0, 0))) for j in range(4)],
        NT, x1, small["mix_norm"], dx2, deps)

    dx0, _, gs["ffn1_norm"] = _ffn_bwd("ffn1", x, small["ffn1_norm"], get_w, put_g, ffn1_saved, dx1, dx1_half)
    return loss_cols, dx0, gs


def _position():
    return lax.axis_index("x"), lax.axis_index("y"), lax.axis_index("c")


def _any_specs(n):
    return [pl.BlockSpec(memory_space=pl.ANY)] * n


HBM_SPEC = pl.BlockSpec(memory_space=pltpu.HBM)
SEM_SPEC = pl.BlockSpec(memory_space=pltpu.SEMAPHORE)
DATAFLOW_EFFECT = pltpu.SideEffectType.DATAFLOW_SIDE_EFFECTING
N_PEER_CHIPS = 3
LANES = 128


def _quarter_copies(srcs, lands, send_sems, recv_sems, scatter):
    x, y, c = _position()
    me = 2 * x + y
    peers = [(1 - x, y, c), (x, 1 - y, c), (1 - x, 1 - y, c)]
    copies = []
    for src, land, send, recv in zip(srcs, lands, send_sems, recv_sems):
        half = land.shape[1] // 2
        mine = land.at[me, pl.ds(c * half, half)]
        for p, (px, py, pc) in enumerate(peers):
            copies.append(pltpu.make_async_remote_copy(
                src_ref=src.at[2 * px + py] if scatter else mine, dst_ref=land.at[me] if scatter else mine,
                send_sem=send.at[p], recv_sem=recv.at[p], device_id=(px, py, pc), device_id_type=MESH))
    return copies


def _fill_from_sibling(name, stacks):
    n = len(stacks)

    def body(*refs):
        outs = refs[n:2 * n]
        send_sems, recv_sems = refs[2 * n:]
        x, y, c = _position()
        copies = []
        for i, ref in enumerate(outs):
            half = ref.shape[1] // 2
            rows = pl.ds(c * half, half)
            for p, k in enumerate((2 * (1 - x) + y, 2 * x + (1 - y), 2 * (1 - x) + (1 - y))):
                cp = pltpu.make_async_remote_copy(ref.at[k, rows], ref.at[k, rows], send_sems.at[3 * i + p], recv_sems.at[3 * i + p],
                                                  device_id=(x, y, 1 - c), device_id_type=MESH)
                cp.start()
                copies.append(cp)
        for cp in copies:
            cp.wait()

    return pl.pallas_call(
        body, out_shape=[jax.ShapeDtypeStruct(s.shape, s.dtype) for s in stacks],
        in_specs=_any_specs(n), out_specs=_any_specs(n), input_output_aliases={i: i for i in range(n)},
        scratch_shapes=[pltpu.SemaphoreType.DMA((N_PEER_CHIPS * n,)), pltpu.SemaphoreType.DMA((N_PEER_CHIPS * n,))],
        compiler_params=pltpu.CompilerParams(has_side_effects=True), name=name)(*stacks)


def _exchange_start(name, srcs, lands, scatter):
    n = len(lands)
    arrays = list(lands) if srcs is None else list(srcs) + list(lands)
    k = len(arrays)

    def body(*refs):
        land_refs = refs[k - n:k]
        send_sems, recv_sems = refs[k:k + n], refs[k + n:k + 2 * n]
        token = refs[2 * k + 2 * n]
        for cp in _quarter_copies(refs[:n], land_refs, send_sems, recv_sems, scatter):
            cp.start()
        token[...] = jnp.zeros_like(token)

    sem = pltpu.SemaphoreType.DMA((N_PEER_CHIPS,))
    out_shape = [sem] * (2 * n) + [pltpu.HBM(a.shape, a.dtype) for a in arrays] + [jax.ShapeDtypeStruct((8, LANES), F32)]
    res = pl.pallas_call(
        body, name=name, out_shape=out_shape, in_specs=[HBM_SPEC] * k,
        out_specs=[SEM_SPEC] * (2 * n) + [HBM_SPEC] * k + [pl.BlockSpec(memory_space=pltpu.VMEM)],
        input_output_aliases={i: 2 * n + i for i in range(k)},
        compiler_params=pltpu.CompilerParams(has_side_effects=DATAFLOW_EFFECT),
    )(*[pltpu.with_memory_space_constraint(a, pltpu.HBM) for a in arrays])
    thru = res[2 * n:2 * n + k]
    return res[:n], res[n:2 * n], (None if srcs is None else thru[:n]), thru[k - n:], res[2 * n + k]


def _exchange_wait(name, srcs, lands, send_sems, recv_sems, after, scatter):
    n = len(lands)
    arrays = list(lands) if srcs is None else list(srcs) + list(lands)
    k = len(arrays)

    def body(*refs):
        sends, recvs = refs[k:k + n], refs[k + n:k + 2 * n]
        for cp in _quarter_copies(refs[:n], refs[k - n:k], sends, recvs, scatter):
            cp.wait_send()
            cp.wait_recv()

    res = pl.pallas_call(
        body, name=name, out_shape=[pltpu.HBM(a.shape, a.dtype) for a in arrays],
        in_specs=[HBM_SPEC] * k + [SEM_SPEC] * (2 * n) + [pl.BlockSpec(memory_space=pl.ANY)],
        out_specs=[HBM_SPEC] * k, input_output_aliases={i: i for i in range(k)},
        compiler_params=pltpu.CompilerParams(has_side_effects=DATAFLOW_EFFECT),
    )(*arrays, *send_sems, *recv_sems, after)
    return res[k - n:]


def _own_slot(name, src, from_stack=False):
    R, C = src.shape[-2:]
    rows = R // 2
    me = (2 * lax.axis_index("x") + lax.axis_index("y")).astype(jnp.int32).reshape(1)

    def body(me_ref, x_ref, o_ref):
        o_ref[...] = x_ref[...].astype(o_ref.dtype)

    in_spec = (pl.BlockSpec((None, rows, C), lambda i, me_ref: (me_ref[0], i, 0)) if from_stack
               else pl.BlockSpec((rows, C), lambda i, me_ref: (i, 0)))
    grid_spec = pltpu.PrefetchScalarGridSpec(
        num_scalar_prefetch=1, grid=(R // rows,), in_specs=[in_spec],
        out_specs=pl.BlockSpec((None, rows, C), lambda i, me_ref: (me_ref[0], i, 0)))
    return pl.pallas_call(body, out_shape=jax.ShapeDtypeStruct((4, R, C), BF16), grid_spec=grid_spec,
                          compiler_params=_params("parallel"), name=name)(me, src)


def _swap_with_sibling(parts):
    n = len(parts)

    def body(*refs):
        ins, outs = refs[:n], refs[n:2 * n]
        send_sems, recv_sems = refs[2 * n:]
        x, y, c = _position()
        copies = []
        for i in range(n):
            cp = pltpu.make_async_remote_copy(ins[i], outs[i], send_sems.at[i], recv_sems.at[i],
                                              device_id=(x, y, 1 - c), device_id_type=MESH)
            cp.start()
            copies.append(cp)
        for cp in copies:
            cp.wait()

    return pl.pallas_call(
        body, out_shape=[jax.ShapeDtypeStruct(s.shape, s.dtype) for s in parts],
        in_specs=_any_specs(n), out_specs=_any_specs(n),
        scratch_shapes=[pltpu.SemaphoreType.DMA((n,)), pltpu.SemaphoreType.DMA((n,))],
        compiler_params=pltpu.CompilerParams(has_side_effects=True), name="swap_with_sibling")(*parts)


def _allreduce_small(buf):
    R, C = buf.shape
    flips = [(fx, fy, fc) for fx in (0, 1) for fy in (0, 1) for fc in (0, 1)][1:]

    def body(in_ref, out_ref, land_ref, send_sems, recv_sems):
        x, y, c = _position()
        me = 4 * x + 2 * y + c
        copies = []
        for k, (fx, fy, fc) in enumerate(flips):
            px, py, pc = (1 - x if fx else x), (1 - y if fy else y), (1 - c if fc else c)
            cp = pltpu.make_async_remote_copy(in_ref, land_ref.at[me], send_sems.at[k], recv_sems.at[k],
                                              device_id=(px, py, pc), device_id_type=MESH)
            cp.start()
            copies.append(cp)
        land_ref[me] = in_ref[...]
        for cp in copies:
            cp.wait()
        acc = land_ref[0]
        for k in range(1, 8):
            acc = acc + land_ref[k]
        out_ref[...] = acc

    return pl.pallas_call(
        body, out_shape=jax.ShapeDtypeStruct((R, C), F32),
        in_specs=[pl.BlockSpec(memory_space=pltpu.VMEM)], out_specs=pl.BlockSpec(memory_space=pltpu.VMEM),
        scratch_shapes=[pltpu.VMEM((8, R, C), F32), pltpu.SemaphoreType.DMA((7,)), pltpu.SemaphoreType.DMA((7,))],
        compiler_params=pltpu.CompilerParams(has_side_effects=True), name="allreduce_small")(buf)


def _adamw_math(w, g, m, v):
    m2 = ADAM_B1 * m + (1.0 - ADAM_B1) * g
    v2 = ADAM_B2 * v + (1.0 - ADAM_B2) * (g * g)
    m_hat = m2 / (1.0 - ADAM_B1 ** ADAM_STEP)
    v_hat = v2 / (1.0 - ADAM_B2 ** ADAM_STEP)
    delta = -ADAM_LR * (m_hat / (jnp.sqrt(v_hat) + ADAM_EPS) + ADAM_WD * w)
    return delta, m2, v2


def _adamw_big(name, w, m, v, part_mine, part_sibling):
    R, C = w.shape
    rows = 256 if R % 256 == 0 else R // 2 if (R // 2) % 8 == 0 else R

    def fn(wv, mv, vv, a, b):
        g = a + b
        return (g,) + _adamw_math(wv, g, mv, vv)

    return _ew(name, fn, [_tiled(w), _tiled(m), _tiled(v), _tiled(part_mine), _tiled(part_sibling)], [(F32, C)] * 4, n_rows=R, rows=rows)


def _sum_four(name, stack):
    _, R, C = stack.shape
    rows = 256 if R % 256 == 0 else R // 2 if (R // 2) % 8 == 0 else R
    flat = stack.reshape(4 * R, C)
    nrb = R // rows

    def fn(a, b, c, d):
        return ((a.astype(F32) + b.astype(F32)) + c.astype(F32)) + d.astype(F32)

    (out,) = _ew(name, fn, [_tiled(flat, None, 0, k * nrb) for k in range(4)], [(F32, C)], n_rows=R, rows=rows)
    return out


BIG = ("ffn1_w1", "ffn1_w3", "ffn1_w2", "w_in", "w_branch_a", "w_branch_b", "w_out", "ffn2_w1", "ffn2_w3", "ffn2_w2")
SMALL = ("ffn1_norm", "mix_norm", "b_gate", "q_norm", "k_norm", "rel_bias", "ffn2_norm", "final_norm")
ORDER = ("ffn1_norm", "ffn1_w1", "ffn1_w3", "ffn1_w2", "mix_norm", "w_in", "b_gate", "q_norm", "k_norm", "rel_bias",
         "w_branch_a", "w_branch_b", "w_out", "ffn2_norm", "ffn2_w1", "ffn2_w3", "ffn2_w2", "final_norm")
TRANSPOSED = ("ffn1_w1", "ffn1_w3", "ffn2_w1", "ffn2_w3")
GATHER_GROUPS = (("ffn1_w1", "ffn1_w3"), ("ffn1_w2",), ("w_in",), ("w_branch_a", "w_branch_b", "w_out"),
                 ("ffn2_w1", "ffn2_w3", "ffn2_w2"))


def _pack_small(d):
    rows = []
    for n in SMALL:
        flat = d[n].reshape(-1)
        pad = (-flat.shape[0]) % LANES
        rows.append(jnp.pad(flat, (0, pad)).reshape(-1, LANES))
    buf = jnp.concatenate(rows, axis=0)
    return jnp.pad(buf, ((0, (-buf.shape[0]) % 8), (0, 0)))


def _unpack_small(buf, like):
    out, r = {}, 0
    for n in SMALL:
        size = like[n].size
        nr = -(-size // LANES)
        out[n] = buf[r:r + nr].reshape(-1)[:size].reshape(like[n].shape)
        r += nr
    return out


def kernel(x, ffn1_norm, ffn1_w1, ffn1_w3, ffn1_w2, mix_norm, w_in, b_gate, q_norm, k_norm, rel_bias, w_branch_a, w_branch_b, w_out, ffn2_norm, ffn2_w1, ffn2_w3, ffn2_w2, final_norm, loss_target, m_ffn1_norm, m_ffn1_w1, m_ffn1_w3, m_ffn1_w2, m_mix_norm, m_w_in, m_b_gate, m_q_norm, m_k_norm, m_rel_bias, m_w_branch_a, m_w_branch_b, m_w_out, m_ffn2_norm, m_ffn2_w1, m_ffn2_w3, m_ffn2_w2, m_final_norm, v_ffn1_norm, v_ffn1_w1, v_ffn1_w3, v_ffn1_w2, v_mix_norm, v_w_in, v_b_gate, v_q_norm, v_k_norm, v_rel_bias, v_w_branch_a, v_w_branch_b, v_w_out, v_ffn2_norm, v_ffn2_w1, v_ffn2_w3, v_ffn2_w2, v_final_norm):
    given = dict(locals())
    w = {n: given[n] for n in ORDER}
    m = {n: given["m_" + n] for n in ORDER}
    v = {n: given["v_" + n] for n in ORDER}
    T, D = x.shape[1], x.shape[2]

    def stored(a, n):
        a = a.reshape(a.shape[1:])
        return a.T if n in TRANSPOSED else a

    def returned(a, n):
        return (a.T if n in TRANSPOSED else a).reshape(w[n].shape)

    quarter = {n: stored(w[n], n) for n in BIG}
    send, recv, _, land_thru, token = _exchange_start(
        "gather_start", None, [_own_slot(f"own_{n}", quarter[n]) for n in BIG], scatter=False)
    index = {n: i for i, n in enumerate(BIG)}
    ready = {}

    def get_w(name, after):
        if name not in ready:
            group = next(g for g in GATHER_GROUPS if name in g)
            ids = [index[n] for n in group]
            stacks = _exchange_wait("gather_wait_" + group[0], None, [land_thru[i] for i in ids],
                                    [send[i] for i in ids], [recv[i] for i in ids], after, scatter=False)
            stacks = _fill_from_sibling("gather_fill_" + group[0], stacks)
            for n, st in zip(group, stacks):
                ready[n] = st.reshape(D, D) if n in ("w_branch_b", "w_out") else st
        return ready[name]

    in_flight = []

    def put_g(grads):
        names = list(grads)
        stacks = [grads[n].reshape((4,) + quarter[n].shape) for n in names]
        lands = [_own_slot(f"own_grad_{n}", s, from_stack=True) for n, s in zip(names, stacks)]
        started = _exchange_start("scatter_start_" + names[0], stacks, lands, scatter=True)
        in_flight.append((names,) + tuple(started[:4]))
        return [started[4]]

    small = {n: w[n] for n in SMALL}
    loss_cols, grad_x, gs = _local_step(x.reshape(T, D), loss_target.reshape(T, D), small, get_w, put_g, deps=[token])

    landed = {}
    for names, s_sem, r_sem, srcs, lands in in_flight:
        got = _exchange_wait("scatter_wait_" + names[0], srcs, lands, s_sem, r_sem, grad_x, scatter=True)
        landed.update(zip(names, got))
    partial = [_sum_four(f"sum4_{n}", landed[n]) for n in BIG]
    other = _swap_with_sibling(partial)
    grads, deltas, new_m, new_v = {}, {}, {}, {}
    for n, mine, theirs in zip(BIG, partial, other):
        res = _adamw_big(f"adamw_{n}", quarter[n], stored(m[n], n), stored(v[n], n), mine, theirs)
        grads[n], deltas[n], new_m[n], new_v[n] = [returned(r, n) for r in res]

    gs = {n: gs[n].reshape(w[n].shape) for n in SMALL}
    packed_g = _pack_small(gs)
    n_small = packed_g.shape[0]
    summed = _allreduce_small(jnp.concatenate([packed_g, loss_cols.reshape(-1, LANES)], axis=0))
    g_small, loss = summed[:n_small], jnp.sum(summed[n_small:])
    packed = [_pack_small({n: d[n] for n in SMALL}) for d in (w, m, v)]
    R = g_small.shape[0]
    res = _ew("adamw_small", lambda wv, mv, vv, g: (g,) + _adamw_math(wv, g, mv, vv),
              [_tiled(packed[0]), _tiled(packed[1]), _tiled(packed[2]), _tiled(g_small)], [(F32, LANES)] * 4, n_rows=R, rows=R)
    for d, buf in zip((grads, deltas, new_m, new_v), res):
        d.update(_unpack_small(buf, w))

    return (loss, grad_x.reshape(x.shape), *[grads[n] for n in ORDER], *[deltas[n] for n in ORDER],
            *[new_m[n] for n in ORDER], *[new_v[n] for n in ORDER])
```

```python
import functools
import math

import numpy as np
import jax
import jax.numpy as jnp
from jax import lax
from jax.experimental import pallas as pl
from jax.experimental.pallas import tpu as pltpu

F32 = jnp.float32
BF16 = jnp.bfloat16
MESH = pl.DeviceIdType.MESH

NEG_INF = -1e30
EPS = 1e-6
GRID_W = 64
ROPE_THETA = 10000.0
DILATIONS = (1, 4, 16)
BAND_HALF = 64
HEAD_A = 64
HEADS_A = 8
WIDTH_A = HEADS_A * HEAD_A
HEAD_B = 128
LOG2_E = math.log2(math.e)
QK_SCALE_LOG2 = HEAD_B ** -0.5 * LOG2_E
N_BUCKETS = 32
MAX_DISTANCE = 1024
ADAM_LR, ADAM_B1, ADAM_B2, ADAM_EPS, ADAM_WD, ADAM_STEP = 0.001, 0.9, 0.999, 1e-08, 0.01, 10

A_Q, A_K, A_V = 0, 1536, 3072
B_Q, B_K, B_V = 4608, 5632, 5888
G_A, G_B = 6144, 7168
IN_WIDTH = 8192

VMEM_LIMIT_BYTES = 56 * 1024 * 1024
QB_A = 128
QB_B = 256


def _params(*sem):
    return pltpu.CompilerParams(dimension_semantics=sem, vmem_limit_bytes=VMEM_LIMIT_BYTES)


def _bs(shape, fn):
    return pl.BlockSpec(shape, fn)


def _resident(shape, fn):
    return pl.BlockSpec(shape, fn, pipeline_mode=pl.Buffered(1))


def _mm(name, grid, pairs, out_shape, out_spec, dims, *, extras=(), epilogue=None, deps=(), reds=()):
    n_pairs, n_extra, n_deps = len(pairs), len(extras), len(deps)
    operands = [p[0] for p in pairs] + [p[2] for p in pairs] + [e[0] for e in extras] + list(deps)
    in_specs = [p[1] for p in pairs] + [p[3] for p in pairs] + [e[1] for e in extras] + _any_specs(n_deps)
    single = not isinstance(out_shape, (list, tuple))
    out_shapes = [out_shape] if single else list(out_shape)
    out_specs = [out_spec] if single else list(out_spec)
    n_out = len(out_shapes)
    out_shapes += [jax.ShapeDtypeStruct((1, w), F32) for w in reds]
    out_specs += [_bs((1, w), lambda *_: (0, 0)) for w in reds]

    def body(*refs):
        a_refs, b_refs = refs[:n_pairs], refs[n_pairs:2 * n_pairs]
        e_refs = refs[2 * n_pairs:2 * n_pairs + n_extra]
        o_refs = refs[2 * n_pairs + n_extra + n_deps:]
        acc = None
        for a_ref, b_ref in zip(a_refs, b_refs):
            t = lax.dot_general(a_ref[...], b_ref[...], (dims, ((), ())), preferred_element_type=F32)
            acc = t if acc is None else acc + t
        vals = acc if epilogue is None else epilogue(acc, *[e[...] for e in e_refs])
        if not isinstance(vals, (list, tuple)):
            vals = (vals,)
        for o_ref, v in zip(o_refs[:n_out], vals[:n_out]):
            o_ref[...] = v.astype(o_ref.dtype)
        if reds:
            first = functools.reduce(jnp.logical_and, [pl.program_id(ax) == 0 for ax in range(len(grid))])
            for r_ref, v in zip(o_refs[n_out:], vals[n_out:]):
                @pl.when(first)
                def _(r_ref=r_ref):
                    r_ref[...] = jnp.zeros_like(r_ref)
                r_ref[...] += v

    sem = ["arbitrary" if reds else "parallel"] * len(grid)
    res = pl.pallas_call(
        body, out_shape=out_shapes, grid=grid, in_specs=in_specs, out_specs=out_specs,
        compiler_params=_params(*sem), name=name)(*operands)
    return res[0] if (single and not reds) else res


NN = ((1,), (0,))
NT = ((1,), (1,))
TN = ((0,), (0,))


def _mm_cols(name, a, w, *, tm, tn, out_dtype, cat, extras=(), epilogue=None):
    M, K = a.shape
    J, _, n = w.shape
    tn = min(tn, n)
    nb = n // tn
    if cat:
        shape, spec = (M, J * n), _bs((tm, tn), lambda j, i, k: (i, j * nb + k))
    else:
        shape, spec = (J, M, n), _bs((None, tm, tn), lambda j, i, k: (j, i, k))
    ex = [(e, _bs((tm, tn), lambda j, i, k: (i, j * nb + k))) for e in extras]
    return _mm(name, (J, M // tm, nb),
               [(a, _bs((tm, K), lambda j, i, k: (i, 0)), w, _bs((None, K, tn), lambda j, i, k: (j, 0, k)))],
               jax.ShapeDtypeStruct(shape, out_dtype), spec, NN, extras=ex, epilogue=epilogue)


def _mm_rows_t(name, a, w, *, tm, out_dtype):
    M, N = a.shape
    J, f, _ = w.shape
    return _mm(name, (J, M // tm),
               [(a, _bs((tm, N), lambda j, i: (i, 0)), w, _bs((None, f, N), lambda j, i: (j, 0, 0)))],
               jax.ShapeDtypeStruct((J, M, f), out_dtype), _bs((None, tm, f), lambda j, i: (j, i, 0)), NT)


def _mm_wgrad(name, a, b, *, a_cols, b_cols, tm, tn, J):
    def pick(arr, cols, t):
        if arr.ndim == 3:
            T, c = arr.shape[1], arr.shape[2]
            t = min(t, c)
            return T, c, t, (lambda sel: _bs((None, T, t), lambda j, i, k: (j, 0, sel(i, k))))
        T = arr.shape[0]
        c = arr.shape[1] if cols is None else cols
        t = min(t, c)
        per = c // t
        if cols is None:
            if per == 1:
                return T, c, t, (lambda sel: _resident((T, t), lambda j, i, k: (0, 0)))
            return T, c, t, (lambda sel: _bs((T, t), lambda j, i, k: (0, sel(i, k))))
        return T, c, t, (lambda sel: _bs((T, t), lambda j, i, k: (0, j * per + sel(i, k))))
    _, ca, tm, mk_a = pick(a, a_cols, tm)
    _, cb, tn, mk_b = pick(b, b_cols, tn)
    return _mm(name, (J, ca // tm, cb // tn),
               [(a, mk_a(lambda i, k: i), b, mk_b(lambda i, k: k))],
               jax.ShapeDtypeStruct((J, ca, cb), BF16), _bs((None, tm, tn), lambda j, i, k: (j, i, k)), TN)


def _tiled(arr, width=None, col=0, rowblk=0):
    return ("t", arr, arr.shape[1] if width is None else width, col, rowblk)


def _table(arr):
    return ("f", arr)


def _whole(arr):
    return ("w", arr)


def _ew(name, fn, ins, outs, *, n_rows, rows, reds=(), ncols=1, deps=()):
    nrb = n_rows // rows
    n_deps = len(deps)
    operands, in_specs = [], []
    for spec in ins:
        if spec[0] == "t":
            _, arr, width, col, rowblk = spec
            step = 1 if ncols > 1 else 0
            in_specs.append(_bs((rows, width), lambda c, i, col=col, rowblk=rowblk, step=step: (rowblk + i, col + c * step)))
        elif spec[0] == "f":
            arr = spec[1]
            in_specs.append(_bs((rows, arr.shape[1]), lambda c, i: (i, 0)))
        else:
            arr = spec[1]
            nd = arr.ndim
            if nd == 3:
                in_specs.append(_bs((None,) + arr.shape[1:], lambda c, i: (c, 0, 0)))
            else:
                in_specs.append(_bs(arr.shape, lambda c, i, nd=nd: (0,) * nd))
        operands.append(arr)
    out_shapes = [jax.ShapeDtypeStruct((n_rows, ncols * w), dt) for dt, w in outs]
    out_specs = [_bs((rows, w), lambda c, i: (i, c)) for _, w in outs]
    out_shapes += [jax.ShapeDtypeStruct((ncols, 1, w), F32) for w in reds]
    out_specs += [_bs((None, 1, w), lambda c, i: (c, 0, 0)) for w in reds]
    n_in, n_out, n_red = len(ins), len(outs), len(reds)
    operands += list(deps)
    in_specs += _any_specs(n_deps)

    def body(*refs):
        vals = fn(*[r[...] for r in refs[:n_in]])
        if not isinstance(vals, (tuple, list)):
            vals = (vals,)
        o_refs = refs[n_in + n_deps:]
        for o_ref, v in zip(o_refs[:n_out], vals[:n_out]):
            o_ref[...] = v.astype(o_ref.dtype)
        if n_red:
            i = pl.program_id(1)
            for r_ref, v in zip(o_refs[n_out:], vals[n_out:]):
                @pl.when(i == 0)
                def _(r_ref=r_ref):
                    r_ref[...] = jnp.zeros_like(r_ref)
                r_ref[...] += v

    res = pl.pallas_call(
        body, out_shape=out_shapes, grid=(ncols, nrb), in_specs=in_specs, out_specs=out_specs,
        compiler_params=_params("parallel", "arbitrary" if n_red else "parallel"), name=name)(*operands)
    return res


def _colsum(v):
    return jnp.sum(v, axis=0, keepdims=True)


def _rstd(x):
    return lax.rsqrt(jnp.mean(x * x, axis=-1, keepdims=True) + EPS)


def _sigmoid(x):
    return 1.0 / (1.0 + jnp.exp(-x))


def _norm_fwd(x, g):
    return x * _rstd(x) * g


def _norm_bwd(x, g, dy):
    r = _rstd(x)
    xh = x * r
    dxh = dy * g
    dx = r * (dxh - xh * jnp.mean(dxh * xh, axis=-1, keepdims=True))
    return dx, dy * xh


def _row_spec(arr, rows):
    if arr.shape[0] == 1:
        return _bs(arr.shape, lambda i: (0, 0))
    return _bs((rows, arr.shape[1]), lambda i: (i, 0))


def _ffn_fwd(tag, x, gain, get_w, deps=(), *, h=None, tail_ins=(), tail_fn=None, tail_outs=(F32,), tail_reds=()):
    T, D = x.shape
    if h is None:
        (h,) = _ew(f"{tag}_norm", lambda xv, g: _norm_fwd(xv, g), [_tiled(x), _whole(gain)], [(BF16, D)], n_rows=T, rows=512,
                   deps=deps)
    w1, w3 = get_w(f"{tag}_w1", h), get_w(f"{tag}_w3", h)
    J, f, _ = w1.shape
    tm = 1024

    def up(h_ref, w1_ref, w3_ref, u_ref, g_ref, a_ref):
        hv = h_ref[...]
        u = lax.dot_general(hv, w1_ref[...], (NT, ((), ())), preferred_element_type=F32)
        g = lax.dot_general(hv, w3_ref[...], (NT, ((), ())), preferred_element_type=F32)
        u_ref[...] = u.astype(BF16)
        g_ref[...] = g.astype(BF16)
        a_ref[...] = (u * _sigmoid(u) * g).astype(BF16)

    slab = _bs((None, tm, f), lambda j, i: (j, i, 0))
    w_spec = _bs((None, f, D), lambda j, i: (j, 0, 0))
    u, g, a = pl.pallas_call(
        up, out_shape=[jax.ShapeDtypeStruct((J, T, f), BF16)] * 3, grid=(J, T // tm),
        in_specs=[_bs((tm, D), lambda j, i: (i, 0)), w_spec, w_spec], out_specs=[slab] * 3,
        compiler_params=_params("parallel", "parallel"), name=f"{tag}_up")(h, w1, w3)
    w2 = get_w(f"{tag}_w2", a)
    def tail(acc, xv, *rest):
        y = xv + 0.5 * acc
        return y if tail_fn is None else tail_fn(y, *rest)

    row = _bs((512, D), lambda i: (i, 0))
    res = _mm(f"{tag}_down", (T // 512,),
              [(a, _bs((None, 512, f), lambda i, j=j: (j, i, 0)), w2, _resident((None, f, D), lambda i, j=j: (j, 0, 0)))
               for j in range(J)],
              [jax.ShapeDtypeStruct((T, D), dt) for dt in tail_outs], [row] * len(tail_outs), NN,
              extras=[(x, row)] + [(t, _row_spec(t, 512)) for t in tail_ins], epilogue=tail, reds=tail_reds)
    return res, (h, u, g, a)


def _dh_norm_bwd(name, rows, pairs, dims, x, gain, dres, deps, also_bf16=False):
    T, D = x.shape

    def epilogue(dh, xv, gv, dr):
        dx, dgr = _norm_bwd(xv, gv, dh)
        dx = dx + dr
        return (dx, 0.5 * dx) + ((dx,) if also_bf16 else ()) + (_colsum(dgr),)

    dts = [F32, BF16] + ([BF16] if also_bf16 else [])
    row = _bs((rows, D), lambda i: (i, 0))
    return _mm(name, (T // rows,), pairs, [jax.ShapeDtypeStruct((T, D), dt) for dt in dts], [row] * len(dts), dims,
               extras=[(x, row), (gain, _row_spec(gain, rows)), (dres, row)], epilogue=epilogue, deps=deps, reds=(D,))


def _ffn_bwd(tag, x, gain, get_w, put_g, saved, dy, dy_half, also_bf16=False):
    h, u, g, a = saved
    T, D = x.shape
    w1, w3, w2 = [get_w(f"{tag}_{n}", dy_half) for n in ("w1", "w3", "w2")]
    J, f, _ = w1.shape
    dw2 = _mm_wgrad(f"{tag}_bwd_dw2", a, dy_half, a_cols=None, b_cols=None, tm=f, tn=D, J=J)
    deps = put_g({f"{tag}_w2": dw2})
    tm = 1024

    def up_bwd(dy_ref, w2_ref, u_ref, g_ref, *rest):
        du_ref, dg_ref = rest[-2:]
        da = lax.dot_general(dy_ref[...], w2_ref[...], (NT, ((), ())), preferred_element_type=F32)
        uv, gv = u_ref[...].astype(F32), g_ref[...].astype(F32)
        s = _sigmoid(uv)
        du_ref[...] = (da * gv * (s * (1.0 + uv * (1.0 - s)))).astype(BF16)
        dg_ref[...] = (da * (uv * s)).astype(BF16)

    slab = _bs((None, tm, f), lambda j, i: (j, i, 0))
    du, dg = pl.pallas_call(
        up_bwd, out_shape=[jax.ShapeDtypeStruct((J, T, f), BF16)] * 2, grid=(J, T // tm),
        in_specs=[_bs((tm, D), lambda j, i: (i, 0)), _bs((None, f, D), lambda j, i: (j, 0, 0)), slab, slab] + _any_specs(len(deps)),
        out_specs=[slab] * 2, compiler_params=_params("parallel", "parallel"), name=f"{tag}_bwd_up")(dy_half, w2, u, g, *deps)
    dw1 = _mm_wgrad(f"{tag}_bwd_dw1", du, h, a_cols=None, b_cols=None, tm=f, tn=D, J=J)
    dw3 = _mm_wgrad(f"{tag}_bwd_dw3", dg, h, a_cols=None, b_cols=None, tm=f, tn=D, J=J)
    deps = deps + put_g({f"{tag}_w1": dw1, f"{tag}_w3": dw3})
    pairs = []
    for j in range(J):
        a_spec = _bs((None, 512, f), lambda i, j=j: (j, i, 0))
        w_spec = _resident((None, f, D), lambda i, j=j: (j, 0, 0))
        pairs += [(du, a_spec, w1, w_spec), (dg, a_spec, w3, w_spec)]
    return _dh_norm_bwd(f"{tag}_bwd_dh", 512, pairs, NN, x, gain, dy, deps, also_bf16)


def _t5_bucket(rel):
    n = N_BUCKETS // 2
    max_exact = n // 2
    ret = jnp.where(rel > 0, n, 0)
    a = jnp.abs(rel)
    af = jnp.maximum(a, 1).astype(F32)
    large = max_exact + (jnp.log(af / max_exact) / math.log(MAX_DISTANCE / max_exact) * (n - max_exact)).astype(jnp.int32)
    large = jnp.minimum(large, n - 1)
    return ret + jnp.where(a < max_exact, a, large)


WIN_A = QB_A + 2 * BAND_HALF
WIN_SHIFTS = (0, BAND_HALF, 2 * BAND_HALF)


def _window_variant(n, nblk):
    return jnp.where(n == 0, 0, jnp.where(n == nblk - 1, 2, 1))


def _window_start(n, nblk):
    return pl.multiple_of(jnp.clip(n * QB_A - BAND_HALF, 0, nblk * QB_A - WIN_A), BAND_HALF)


def _band_steps(xp=jnp):
    qi = xp.arange(QB_A, dtype=xp.int32)[None, :, None]
    kj = xp.arange(WIN_A, dtype=xp.int32)[None, None, :]
    return kj - qi - xp.asarray(WIN_SHIFTS, dtype=xp.int32)[:, None, None]


def _bias_tiles(rel_bias):
    wide = QB_A + 2 * WIN_SHIFTS[-1]
    qi = jnp.arange(QB_A, dtype=jnp.int32)[:, None]
    steps = jnp.arange(wide, dtype=jnp.int32)[None, :] - WIN_SHIFTS[-1] - qi
    buckets = jnp.stack([_t5_bucket(steps * d) for d in DILATIONS])
    inband = (jnp.abs(steps) <= BAND_HALF).astype(jnp.int32)
    n_heads = rel_bias.shape[1]

    def body(tab_ref, b_ref, m_ref, o_ref):
        hd = pl.program_id(0)
        bkt = b_ref[...]
        acc = jnp.zeros(bkt.shape, F32)
        for b in range(N_BUCKETS):
            acc = jnp.where(bkt == b, tab_ref[b, hd], acc)
        o_ref[...] = jnp.where(m_ref[...] > 0, acc, NEG_INF)

    base = pl.pallas_call(
        body, out_shape=jax.ShapeDtypeStruct((n_heads, QB_A, wide), F32), grid=(n_heads,),
        in_specs=[pl.BlockSpec(memory_space=pltpu.SMEM),
                  _bs((None, QB_A, wide), lambda hd: (hd // HEADS_A, 0, 0)),
                  _bs((QB_A, wide), lambda hd: (0, 0))],
        out_specs=_bs((None, QB_A, wide), lambda hd: (hd, 0, 0)),
        compiler_params=_params("parallel"), name="a_bias_tiles")(rel_bias, buckets, inband)
    base = base.reshape(len(DILATIONS), HEADS_A, QB_A, wide)
    return jnp.stack([base[..., WIN_SHIFTS[-1] - s:WIN_SHIFTS[-1] - s + WIN_A] for s in WIN_SHIFTS], axis=1)


def _bias_grad(dbias):
    steps = _band_steps(np)
    inband = np.abs(steps) <= BAND_HALF
    present = []
    for d in DILATIONS:
        rel = steps * d
        a = np.abs(rel)
        large = 8 + (np.log(np.maximum(a, 1) / 8.0) / math.log(MAX_DISTANCE / 8.0) * 8).astype(np.int64)
        bk = np.where(rel > 0, 16, 0) + np.where(a < 8, a, np.minimum(large, 15))
        present.append([sorted(set(bk[v][inband[v]].tolist())) for v in range(3)])
    buckets = jnp.stack([_t5_bucket(_band_steps() * d) for d in DILATIONS])
    n_heads = len(DILATIONS) * HEADS_A

    def body(b_ref, d_ref, o_ref):
        row = lax.broadcasted_iota(jnp.int32, (N_BUCKETS, n_heads), 0)
        col = lax.broadcasted_iota(jnp.int32, (N_BUCKETS, n_heads), 1)
        out = jnp.zeros((N_BUCKETS, n_heads), F32)
        for grp in range(len(DILATIONS)):
            for hh in range(HEADS_A):
                hd = grp * HEADS_A + hh
                for b in sorted(set(sum(present[grp], []))):
                    tot = jnp.zeros((), F32)
                    for v in range(3):
                        if b in present[grp][v]:
                            tot = tot + jnp.sum(jnp.where(b_ref[grp, v] == b, d_ref[grp, v, hh], 0.0))
                    out = jnp.where((row == b) & (col == hd), tot, out)
        o_ref[...] = out

    return pl.pallas_call(
        body, out_shape=jax.ShapeDtypeStruct((N_BUCKETS, n_heads), F32),
        compiler_params=pltpu.CompilerParams(vmem_limit_bytes=VMEM_LIMIT_BYTES), name="a_bias_grad")(buckets, dbias)


def _lane_is_second_head(shape):
    return lax.broadcasted_iota(jnp.int32, shape, len(shape) - 1) >= HEAD_A


VIEW_ROWS = 512


def _view_chunks():
    return [pltpu.VMEM((VIEW_ROWS, LANES), F32)] * (WIDTH_A // LANES)


def _rows_to_view(x_ref, col, o_ref, ocol, d, chunks):
    n = VIEW_ROWS // d
    for c, scr in enumerate(chunks):
        scr[...] = x_ref[:, col + c * LANES:col + (c + 1) * LANES].astype(F32)
        for r in range(d):
            at = ocol + r * WIDTH_A + c * LANES
            o_ref[:, at:at + LANES] = scr[pl.ds(r, n, stride=d), :].astype(o_ref.dtype)


def _view_to_rows(v_ref, o_ref, col, d, chunks):
    n = VIEW_ROWS // d
    for c, scr in enumerate(chunks):
        if d == 1:
            o_ref[:, col + c * LANES:col + (c + 1) * LANES] = v_ref[:, c * LANES:(c + 1) * LANES].astype(o_ref.dtype)
            continue
        for r in range(d):
            scr[pl.ds(r, n, stride=d), :] = v_ref[:, r * WIDTH_A + c * LANES:r * WIDTH_A + (c + 1) * LANES].astype(F32)
        o_ref[:, col + c * LANES:col + (c + 1) * LANES] = scr[...].astype(o_ref.dtype)


def _group_view(proj, grp, d):
    T = proj.shape[0]
    if d == 1:
        return proj, (lambda part, r: grp * 3 + part)

    def body(x_ref, o_ref, *chunks):
        for part in range(3):
            _rows_to_view(x_ref, part * WIDTH_A, o_ref, part * d * WIDTH_A, d, chunks)

    view = pl.pallas_call(
        body, out_shape=jax.ShapeDtypeStruct((T // d, 3 * d * WIDTH_A), proj.dtype), grid=(T // VIEW_ROWS,),
        in_specs=[_bs((VIEW_ROWS, 3 * WIDTH_A), lambda i: (i, grp))],
        out_specs=_bs((VIEW_ROWS // d, 3 * d * WIDTH_A), lambda i: (i, 0)),
        scratch_shapes=_view_chunks(), compiler_params=_params("parallel"), name=f"a_view_d{d}")(proj)
    return view, (lambda part, r: part * d + r)


def _stack_heads(v2, second):
    zero = jnp.zeros_like(v2)
    return jnp.concatenate([jnp.where(second, zero, v2), jnp.where(second, v2, zero)], axis=0)


def _unstack_heads(v, second):
    return jnp.where(second, v[QB_A:], v[:QB_A])


def _dil_fwd(view, bias, d):
    pv, colblk = view
    L = pv.shape[0]
    nblk = L // QB_A
    W2 = 2 * HEAD_A
    scale = HEAD_A ** -0.5

    def body(q_ref, k_ref, v_ref, b_ref, o_ref, l_ref):
        win = pl.ds(_window_start(pl.program_id(1), nblk), WIN_A)
        second = _lane_is_second_head((QB_A, W2))
        for hp in range(HEADS_A // 2):
            cols = slice(hp * W2, (hp + 1) * W2)
            kw, vw = k_ref[win, cols], v_ref[win, cols]
            qs = _stack_heads(q_ref[:, cols], second)
            s = lax.dot_general(qs, kw, (NT, ((), ())), preferred_element_type=F32)
            s = s * scale + b_ref[2 * hp:2 * hp + 2].reshape(2 * QB_A, WIN_A)
            m = jnp.max(s, axis=-1, keepdims=True)
            p = jnp.exp(s - m)
            l = jnp.sum(p, axis=-1, keepdims=True)
            res = jnp.dot(p.astype(BF16), vw, preferred_element_type=F32) / l
            o_ref[:, cols] = _unstack_heads(res, second).astype(o_ref.dtype)
            l_ref[:, cols] = _unstack_heads(jnp.broadcast_to(m + jnp.log(l), (2 * QB_A, W2)), second)

    in_specs = [_bs((QB_A, WIDTH_A), lambda r, n: (n, colblk(0, r))),
                _bs((L, WIDTH_A), lambda r, n: (0, colblk(1, r))), _bs((L, WIDTH_A), lambda r, n: (0, colblk(2, r))),
                _bs((None, HEADS_A, QB_A, WIN_A), lambda r, n: (_window_variant(n, nblk), 0, 0, 0))]
    o, lse = pl.pallas_call(
        body, out_shape=[jax.ShapeDtypeStruct((L, d * WIDTH_A), BF16), jax.ShapeDtypeStruct((L, d * WIDTH_A), F32)],
        grid=(d, nblk), in_specs=in_specs,
        out_specs=[_bs((QB_A, WIDTH_A), lambda r, n: (n, r)), _bs((QB_A, WIDTH_A), lambda r, n: (n, r))],
        compiler_params=_params("parallel", "parallel"), name=f"a_fwd_d{d}")(pv, pv, pv, bias)
    return o, lse


def _dil_bwd(view_qkv, bias, do, lse, cterm, d):
    pv, colblk = view_qkv
    L = pv.shape[0]
    nblk = L // QB_A
    W2 = 2 * HEAD_A
    PPS = 4
    WS = PPS * W2
    ob = WIDTH_A // WS
    scale = HEAD_A ** -0.5

    def body(q_ref, k_ref, v_ref, do_ref, l_ref, c_ref, b_ref, dq_ref, dk_ref, dv_ref, db_ref):
        r, n = pl.program_id(1), pl.program_id(2)

        @pl.when(n == 0)
        def _():
            dk_ref[...] = jnp.zeros_like(dk_ref)
            dv_ref[...] = jnp.zeros_like(dv_ref)

        @pl.when((n == 0) & (r == 0))
        def _():
            db_ref[...] = jnp.zeros_like(db_ref)

        second = _lane_is_second_head((QB_A, W2))
        win = pl.ds(_window_start(n, nblk), WIN_A)
        variant = _window_variant(n, nblk)
        for pp in range(PPS):
            cols = slice(pp * W2, (pp + 1) * W2)
            kw, vw = k_ref[win, cols], v_ref[win, cols]
            qs, dos = _stack_heads(q_ref[:, cols], second), _stack_heads(do_ref[:, cols], second)
            lse2, c2 = l_ref[:, cols], c_ref[:, cols]
            lse_rows = jnp.concatenate([lse2[:, 0:1], lse2[:, HEAD_A:HEAD_A + 1]], axis=0)
            c_rows = jnp.concatenate([c2[:, 0:1], c2[:, HEAD_A:HEAD_A + 1]], axis=0)
            s = lax.dot_general(qs, kw, (NT, ((), ())), preferred_element_type=F32)
            p = jnp.exp(s * scale + b_ref[2 * pp:2 * pp + 2].reshape(2 * QB_A, WIN_A) - lse_rows)
            dp = lax.dot_general(dos, vw, (NT, ((), ())), preferred_element_type=F32)
            ds = p * (dp + c_rows)
            db_ref[variant, 2 * pp:2 * pp + 2] += ds.reshape(2, QB_A, WIN_A)
            pb, dsb = p.astype(BF16), (ds * scale).astype(BF16)
            dq_ref[:, cols] = _unstack_heads(jnp.dot(dsb, kw, preferred_element_type=F32), second).astype(dq_ref.dtype)
            dk_ref[win, cols] += lax.dot_general(dsb, qs, (TN, ((), ())), preferred_element_type=F32)
            dv_ref[win, cols] += lax.dot_general(pb, dos, (TN, ((), ())), preferred_element_type=F32)

    in_specs = [_bs((QB_A, WS), lambda hp, r, n: (n, colblk(0, r) * ob + hp)),
                _resident((L, WS), lambda hp, r, n: (0, colblk(1, r) * ob + hp)),
                _resident((L, WS), lambda hp, r, n: (0, colblk(2, r) * ob + hp))]
    in_specs += [_bs((QB_A, WS), lambda hp, r, n: (n, r * ob + hp))] * 3
    in_specs += [_bs((None, 2 * PPS, QB_A, WIN_A), lambda hp, r, n: (_window_variant(n, nblk), hp, 0, 0))]
    out_shape = [jax.ShapeDtypeStruct((L, d * WIDTH_A), BF16), jax.ShapeDtypeStruct((L, d * WIDTH_A), F32),
                 jax.ShapeDtypeStruct((L, d * WIDTH_A), F32), jax.ShapeDtypeStruct((3, HEADS_A, QB_A, WIN_A), F32)]
    out_specs = [_bs((QB_A, WS), lambda hp, r, n: (n, r * ob + hp)),
                 _bs((L, WS), lambda hp, r, n: (0, r * ob + hp)), _bs((L, WS), lambda hp, r, n: (0, r * ob + hp)),
                 _bs((3, 2 * PPS, QB_A, WIN_A), lambda hp, r, n: (0, hp, 0, 0))]
    dq, dk, dv, db = pl.pallas_call(
        body, out_shape=out_shape, grid=(ob, d, nblk), in_specs=in_specs, out_specs=out_specs,
        compiler_params=_params("arbitrary", "arbitrary", "arbitrary"), name=f"a_bwd_d{d}")(
            pv, pv, pv, do, lse, cterm, bias)
    return dq, dk, dv, db


def _assemble_dproj(a_parts, dq_b, dk_b, dv_b, dga, dgb):
    T = dq_b.shape[0]
    flat = [(a_parts[part][g], d) for part in range(3) for g, d in enumerate(DILATIONS)]
    rest = [dq_b, dk_b, dv_b, dga, dgb]

    def body(*refs):
        views, others = refs[:len(flat)], refs[len(flat):len(flat) + len(rest)]
        o_ref, chunks = refs[len(flat) + len(rest)], refs[len(flat) + len(rest) + 1:]
        col = 0
        for v_ref, (_, d) in zip(views, flat):
            _view_to_rows(v_ref, o_ref, col, d, chunks)
            col += WIDTH_A
        for x_ref in others:
            w = x_ref.shape[1]
            o_ref[:, col:col + w] = x_ref[...].astype(o_ref.dtype)
            col += w

    in_specs = [_bs((VIEW_ROWS // d, d * WIDTH_A), lambda i: (i, 0)) for _, d in flat]
    in_specs += [_bs((VIEW_ROWS, x.shape[1]), lambda i: (i, 0)) for x in rest]
    return pl.pallas_call(
        body, out_shape=jax.ShapeDtypeStruct((T, IN_WIDTH), BF16), grid=(T // VIEW_ROWS,), in_specs=in_specs,
        out_specs=_bs((VIEW_ROWS, IN_WIDTH), lambda i: (i, 0)), scratch_shapes=_view_chunks(),
        compiler_params=_params("parallel"), name="mix_bwd_dproj")(*[a for a, _ in flat], *rest)


def _segment_ones():
    i = np.arange(WIDTH_A)
    return jnp.asarray((i[:, None] // HEAD_A == i[None, :] // HEAD_A).astype(np.float32), dtype=BF16)


def _group_weights(l0, l1, l2):
    m = jnp.maximum(jnp.maximum(l0, l1), l2)
    e = [jnp.exp(l - m) for l in (l0, l1, l2)]
    z = e[0] + e[1] + e[2]
    return [ei / z for ei in e]


def _view_specs():
    return [_bs((VIEW_ROWS // d, d * WIDTH_A), lambda i: (i, 0)) for d in DILATIONS]


def _stage_tiles(n):
    return [pltpu.VMEM((VIEW_ROWS, WIDTH_A), F32)] * n


def _combine_fwd(outs, lses):
    T = outs[0].shape[0] * DILATIONS[0]
    n = len(DILATIONS)

    def body(*refs):
        o_refs, l_refs, oa_ref = refs[:n], refs[n:2 * n], refs[2 * n]
        o_st, l_st, chunks = refs[2 * n + 1:3 * n + 1], refs[3 * n + 1:4 * n + 1], refs[4 * n + 1:]
        for g, d in enumerate(DILATIONS):
            _view_to_rows(o_refs[g], o_st[g], 0, d, chunks)
            _view_to_rows(l_refs[g], l_st[g], 0, d, chunks)
        w = _group_weights(*[l[...] for l in l_st])
        oa_ref[...] = (w[0] * o_st[0][...] + w[1] * o_st[1][...] + w[2] * o_st[2][...]).astype(oa_ref.dtype)

    return pl.pallas_call(
        body, out_shape=jax.ShapeDtypeStruct((T, WIDTH_A), BF16), grid=(T // VIEW_ROWS,),
        in_specs=_view_specs() * 2, out_specs=_bs((VIEW_ROWS, WIDTH_A), lambda i: (i, 0)),
        scratch_shapes=_stage_tiles(2 * n) + _view_chunks(), compiler_params=_params("parallel"), name="a_combine")(*outs, *lses)


def _combine_bwd(doa, outs, lses):
    T = doa.shape[0]
    n = len(DILATIONS)

    def body(*refs):
        d_ref, o_refs, l_refs, seg_ref = refs[0], refs[1:n + 1], refs[n + 1:2 * n + 1], refs[2 * n + 1]
        do_refs, c_refs = refs[2 * n + 2:3 * n + 2], refs[3 * n + 2:4 * n + 2]
        o_st, l_st = refs[4 * n + 2:5 * n + 2], refs[5 * n + 2:6 * n + 2]
        tmp, chunks = refs[6 * n + 2], refs[6 * n + 3:]
        for g, d in enumerate(DILATIONS):
            _view_to_rows(o_refs[g], o_st[g], 0, d, chunks)
            _view_to_rows(l_refs[g], l_st[g], 0, d, chunks)
        dv = d_ref[...].astype(F32)
        w = _group_weights(*[l[...] for l in l_st])
        seg = seg_ref[...]
        tot = jnp.zeros(dv.shape, F32)
        for g in range(n):
            prod = w[g] * dv * o_st[g][...]
            hi = prod.astype(BF16)
            lo = (prod - hi.astype(F32)).astype(BF16)
            tot = tot + jnp.dot(hi, seg, preferred_element_type=F32) + jnp.dot(lo, seg, preferred_element_type=F32)
        for g, d in enumerate(DILATIONS):
            tmp[...] = w[g] * dv
            _rows_to_view(tmp, 0, do_refs[g], 0, d, chunks)
            tmp[...] = -w[g] * tot
            _rows_to_view(tmp, 0, c_refs[g], 0, d, chunks)

    views = [jax.ShapeDtypeStruct((T // d, d * WIDTH_A), dt) for dt in (BF16, F32) for d in DILATIONS]
    res = pl.pallas_call(
        body, out_shape=views, grid=(T // VIEW_ROWS,),
        in_specs=[_bs((VIEW_ROWS, WIDTH_A), lambda i: (i, 0))] + _view_specs() * 2 + [_bs((WIDTH_A, WIDTH_A), lambda i: (0, 0))],
        out_specs=_view_specs() * 2, scratch_shapes=_stage_tiles(2 * n + 1) + _view_chunks(),
        compiler_params=_params("parallel"), name="a_combine_bwd")(doa, *outs, *lses, _segment_ones())
    return res[:n], res[n:]


def _rope_tables(T):
    rows = T // GRID_W
    row = jnp.repeat(jnp.arange(rows, dtype=F32), GRID_W)
    col = jnp.tile(jnp.arange(GRID_W, dtype=F32), rows)
    n_freq = HEAD_B // 4
    freq = ROPE_THETA ** (-jnp.arange(n_freq, dtype=F32) / n_freq)
    ang = jnp.concatenate([row[:, None] * freq, col[:, None] * freq], axis=-1)
    cos, sin = jnp.repeat(jnp.cos(ang), 2, axis=1), jnp.repeat(jnp.sin(ang), 2, axis=1)
    sign = jnp.where(jnp.arange(HEAD_B) % 2 == 0, -1.0, 1.0).astype(F32)
    return cos, sin * sign


def _swap_pairs(v):
    even = lax.broadcasted_iota(jnp.int32, v.shape, v.ndim - 1) % 2 == 0
    n = v.shape[-1]
    return jnp.where(even, pltpu.roll(v, n - 1, v.ndim - 1), pltpu.roll(v, 1, v.ndim - 1))


def _qk_fwd(name, proj, col0, n_heads, gain, cos, sin, out_scale=1.0):
    T = proj.shape[0]

    def fn(xr, g, c, s):
        xn = _norm_fwd(xr.astype(F32), g)
        return (xn * c + _swap_pairs(xn) * s) * out_scale

    (out,) = _ew(name, fn, [_tiled(proj, HEAD_B, col0 // HEAD_B), _whole(gain), _table(cos), _table(sin)],
                 [(BF16, HEAD_B)], n_rows=T, rows=2048, ncols=n_heads)
    return out


def _qk_bwd(name, dout, proj, col0, n_heads, gain, cos, sin, in_scale=1.0):
    T = proj.shape[0]

    def fn(dv, xr, g, c, s):
        dv = dv.astype(F32) * in_scale
        dxn = c * dv + _swap_pairs(s * dv)
        dx, dgr = _norm_bwd(xr.astype(F32), g, dxn)
        return dx, _colsum(dgr)

    dx, dg = _ew(name, fn, [_tiled(dout, HEAD_B, 0), _tiled(proj, HEAD_B, col0 // HEAD_B), _whole(gain),
                            _table(cos), _table(sin)],
                 [(BF16, HEAD_B)], n_rows=T, rows=2048, reds=(HEAD_B,), ncols=n_heads)
    return dx, jnp.sum(dg, axis=0)


def _gqa_fwd(qn, kn, proj):
    T = qn.shape[0]
    GW = 4 * HEAD_B

    def body(q_ref, k_ref, v_ref, o_ref, l_ref):
        k, v = k_ref[...], v_ref[...]
        lane = lax.broadcasted_iota(jnp.int32, (QB_B, HEAD_B), 1)
        lse_all = jnp.zeros((QB_B, HEAD_B), F32)
        for g in range(4):
            cols = slice(g * HEAD_B, (g + 1) * HEAD_B)
            s = lax.dot_general(q_ref[:, cols], k, (NT, ((), ())), preferred_element_type=F32)
            m = jnp.max(s, axis=-1, keepdims=True)
            p = jnp.exp2(s - m)
            l = jnp.sum(p, axis=-1, keepdims=True)
            o = jnp.dot(p.astype(BF16), v, preferred_element_type=F32) / l
            o_ref[:, cols] = o.astype(o_ref.dtype)
            lse_all = jnp.where(lane == g, m + jnp.log2(l), lse_all)
        l_ref[...] = lse_all

    return pl.pallas_call(
        body, out_shape=[jax.ShapeDtypeStruct((T, 2 * GW), BF16), jax.ShapeDtypeStruct((2, T, HEAD_B), F32)],
        grid=(2, T // QB_B),
        in_specs=[_bs((QB_B, GW), lambda kv, i: (i, kv)), _bs((T, HEAD_B), lambda kv, i: (0, kv)),
                  _bs((T, HEAD_B), lambda kv, i: (0, B_V // HEAD_B + kv))],
        out_specs=[_bs((QB_B, GW), lambda kv, i: (i, kv)), _bs((None, QB_B, HEAD_B), lambda kv, i: (kv, i, 0))],
        compiler_params=_params("parallel", "parallel"), name="b_fwd")(qn, kn, proj)


def _gqa_bwd(qn, kn, proj, o, lse, do):
    T = qn.shape[0]
    GW = 4 * HEAD_B

    def body(q_ref, k_ref, v_ref, o_ref, l_ref, do_ref, dq_ref, dk_ref, dv_ref):
        i = pl.program_id(1)

        @pl.when(i == 0)
        def _():
            dk_ref[...] = jnp.zeros_like(dk_ref)
            dv_ref[...] = jnp.zeros_like(dv_ref)

        k, v = k_ref[...], v_ref[...]
        lse_all = l_ref[...]
        for g in range(4):
            cols = slice(g * HEAD_B, (g + 1) * HEAD_B)
            q, dob = q_ref[:, cols], do_ref[:, cols]
            delta = jnp.sum(dob.astype(F32) * o_ref[:, cols].astype(F32), axis=-1, keepdims=True)
            s = lax.dot_general(q, k, (NT, ((), ())), preferred_element_type=F32)
            p = jnp.exp2(s - lse_all[:, g:g + 1])
            dp = lax.dot_general(dob, v, (NT, ((), ())), preferred_element_type=F32)
            ds = (p * (dp - delta)).astype(BF16)
            dq_ref[:, cols] = jnp.dot(ds, k, preferred_element_type=F32).astype(dq_ref.dtype)
            dk_ref[...] += lax.dot_general(ds, q, (TN, ((), ())), preferred_element_type=F32)
            dv_ref[...] += lax.dot_general(p.astype(BF16), dob, (TN, ((), ())), preferred_element_type=F32)

    return pl.pallas_call(
        body, out_shape=[jax.ShapeDtypeStruct((T, 2 * GW), BF16), jax.ShapeDtypeStruct((T, 2 * HEAD_B), F32),
                         jax.ShapeDtypeStruct((T, 2 * HEAD_B), F32)],
        grid=(2, T // QB_B),
        in_specs=[_bs((QB_B, GW), lambda kv, i: (i, kv)), _bs((T, HEAD_B), lambda kv, i: (0, kv)),
                  _bs((T, HEAD_B), lambda kv, i: (0, B_V // HEAD_B + kv)), _bs((QB_B, GW), lambda kv, i: (i, kv)),
                  _bs((None, QB_B, HEAD_B), lambda kv, i: (kv, i, 0)), _bs((QB_B, GW), lambda kv, i: (i, kv))],
        out_specs=[_bs((QB_B, GW), lambda kv, i: (i, kv)), _bs((T, HEAD_B), lambda kv, i: (0, kv)),
                   _bs((T, HEAD_B), lambda kv, i: (0, kv))],
        compiler_params=_params("parallel", "arbitrary"), name="b_bwd")(qn, kn, proj, o, lse, do)


def _local_step(x, target, small, get_w, put_g, deps=()):
    T, D = x.shape
    gs = {}

    (x1, h2), ffn1_saved = _ffn_fwd("ffn1", x, small["ffn1_norm"], get_w, deps, tail_ins=[small["mix_norm"]],
                                    tail_fn=lambda y, g: (y, _norm_fwd(y, g)), tail_outs=(F32, BF16))
    w_in = get_w("w_in", h2)
    nq = w_in.shape[2]
    tpq = nq // WIDTH_A

    def proj_tile(j, k):
        c = j * tpq + k
        return jnp.where(c < 3 * len(DILATIONS), (c % 3) * 3 + c // 3, c)

    proj = _mm("mix_in", (4, tpq),
               [(h2, _resident((T, D), lambda j, k: (0, 0)), w_in, _bs((None, D, WIDTH_A), lambda j, k: (j, 0, k)))],
               jax.ShapeDtypeStruct((T, IN_WIDTH), BF16), _bs((T, WIDTH_A), lambda j, k: (0, proj_tile(j, k))), NN)

    bias = _bias_tiles(small["rel_bias"])
    a_views = [_group_view(proj, grp, d) for grp, d in enumerate(DILATIONS)]
    a_outs, a_lses = [], []
    for grp, d in enumerate(DILATIONS):
        o, l = _dil_fwd(a_views[grp], bias[grp], d)
        a_outs.append(o)
        a_lses.append(l)
    o_a = _combine_fwd(a_outs, a_lses)

    cos, sin = _rope_tables(T)
    qn = _qk_fwd("b_qnorm", proj, B_Q, 8, small["q_norm"], cos, sin, out_scale=QK_SCALE_LOG2)
    kn = _qk_fwd("b_knorm", proj, B_K, 2, small["k_norm"], cos, sin)
    o_b, lse_b = _gqa_fwd(qn, kn, proj)

    wa, wb3, w_out3 = get_w("w_branch_a", o_b), get_w("w_branch_b", o_b).reshape(1, D, D), get_w("w_out", o_b).reshape(1, D, D)
    t_a = _mm_cols("mix_branch_a", o_a, wa, tm=512, tn=256, out_dtype=BF16, cat=True)
    t_b = _mm_cols("mix_branch_b", o_b, wb3, tm=512, tn=512, out_dtype=BF16, cat=True)
    bg_a, bg_b = small["b_gate"][:, :D], small["b_gate"][:, D:]

    def merge(ta, tb, ga, gb_, ba, bb):
        sa, sb = _sigmoid(ga.astype(F32) + ba), _sigmoid(gb_.astype(F32) + bb)
        return sa * ta.astype(F32) + sb * tb.astype(F32)

    gate_ins = [_tiled(proj, D, G_A // D), _tiled(proj, D, G_B // D), _whole(bg_a), _whole(bg_b)]
    (merged,) = _ew("mix_merge", merge, [_tiled(t_a), _tiled(t_b)] + gate_ins, [(BF16, D)], n_rows=T, rows=512)
    def mix_tail(acc, xv, g):
        y = xv + acc
        return y, _norm_fwd(y, g)

    row = _bs((512, D), lambda i: (i, 0))
    x2, hn2 = _mm("mix_out", (T // 512,), [(merged, row, w_out3, _resident((None, D, D), lambda i: (0, 0, 0)))],
                  [jax.ShapeDtypeStruct((T, D), F32), jax.ShapeDtypeStruct((T, D), BF16)], [row, row], NN,
                  extras=[(x1, row), (small["ffn2_norm"], _row_spec(small["ffn2_norm"], 512))], epilogue=mix_tail)

    def head(xv, g, tv):
        r = _rstd(xv)
        xh = xv * r
        e = xh * g - tv
        dy = e * (1.0 / D)
        dxh = dy * g
        dx = r * (dxh - xh * jnp.mean(dxh * xh, axis=-1, keepdims=True))
        return dx, 0.5 * dx, _colsum(e * e) * (0.5 / D), _colsum(dy * xh)

    (dx3, dx3_half, loss_cols, g_final), ffn2_saved = _ffn_fwd(
        "ffn2", x2, small["ffn2_norm"], get_w, h=hn2, tail_ins=[small["final_norm"].reshape(1, D), target], tail_fn=head,
        tail_outs=(F32, BF16), tail_reds=(D, D))
    gs["final_norm"] = g_final.reshape(D)

    dx2, _, dmix, gs["ffn2_norm"] = _ffn_bwd("ffn2", x2, small["ffn2_norm"], get_w, put_g, ffn2_saved, dx3, dx3_half,
                                             also_bf16=True)
    g_out = _mm_wgrad("mix_bwd_dwout", merged, dmix, a_cols=D // 4, b_cols=None, tm=256, tn=512, J=4).reshape(D, D)
    dmerged = _mm_rows_t("mix_bwd_dmerged", dmix, w_out3, tm=512, out_dtype=BF16).reshape(T, D)

    def merge_bwd(dm, ta, tb, ga, gb_, ba, bb):
        dm, ta, tb = dm.astype(F32), ta.astype(F32), tb.astype(F32)
        sa, sb = _sigmoid(ga.astype(F32) + ba), _sigmoid(gb_.astype(F32) + bb)
        dga, dgb = dm * ta * sa * (1.0 - sa), dm * tb * sb * (1.0 - sb)
        return dm * sa, dm * sb, dga, dgb, _colsum(dga), _colsum(dgb)

    dta, dtb, dga, dgb, dba, dbb = _ew("mix_bwd_merge", merge_bwd, [_tiled(dmerged), _tiled(t_a), _tiled(t_b)] + gate_ins,
                                       [(BF16, D)] * 4, n_rows=T, rows=256, reds=(D, D))
    gs["b_gate"] = jnp.concatenate([dba.reshape(1, D), dbb.reshape(1, D)], axis=1)

    g_a = _mm_wgrad("mix_bwd_dwa", o_a, dta, a_cols=None, b_cols=D // 4, tm=WIDTH_A, tn=256, J=4)
    g_b = _mm_wgrad("mix_bwd_dwb", o_b, dtb, a_cols=D // 4, b_cols=None, tm=256, tn=512, J=4).reshape(D, D)
    deps = put_g({"w_out": g_out, "w_branch_a": g_a, "w_branch_b": g_b})
    do_a = _mm("mix_bwd_doa", (T // 1024,),
               [(dta, _bs((1024, D // 4), lambda i, j=j: (i, j)), wa, _bs((None, WIDTH_A, D // 4), lambda i, j=j: (j, 0, 0)))
                for j in range(4)],
               jax.ShapeDtypeStruct((T, WIDTH_A), BF16), _bs((1024, WIDTH_A), lambda i: (i, 0)), NT, deps=deps)
    do_b = _mm_rows_t("mix_bwd_dob", dtb, wb3, tm=512, out_dtype=BF16).reshape(T, D)

    dqn, dkn, dv_b = _gqa_bwd(qn, kn, proj, o_b, lse_b, do_b)
    dq_b, gs["q_norm"] = _qk_bwd("b_bwd_qnorm", dqn, proj, B_Q, 8, small["q_norm"], cos, sin, in_scale=HEAD_B ** -0.5)
    dk_b, gs["k_norm"] = _qk_bwd("b_bwd_knorm", dkn, proj, B_K, 2, small["k_norm"], cos, sin, in_scale=1.0 / LOG2_E)

    do_groups, c_groups = _combine_bwd(do_a, a_outs, a_lses)
    dqs, dks, dvs, dbs = [], [], [], []
    for grp, d in enumerate(DILATIONS):
        dq, dk, dv, db = _dil_bwd(a_views[grp], bias[grp], do_groups[grp], a_lses[grp], c_groups[grp], d)
        dqs.append(dq), dks.append(dk), dvs.append(dv), dbs.append(db)
    gs["rel_bias"] = _bias_grad(jnp.stack(dbs))

    dproj = _assemble_dproj([dqs, dks, dvs], dq_b, dk_b, dv_b, dga, dgb)
    nq = w_in.shape[2]
    g_in = _mm("mix_bwd_dwin", (4, tpq),
               [(h2, _resident((T, D), lambda j, k: (0, 0)), dproj, _bs((T, WIDTH_A), lambda j, k: (0, j * tpq + k)))],
               jax.ShapeDtypeStruct((4, D, nq), BF16), _bs((None, D, WIDTH_A), lambda j, k: (j, 0, k)), TN)
    deps = put_g({"w_in": g_in})
    dx1, dx1_half, gs["mix_norm"] = _dh_norm_bwd(
        "mix_bwd_dh", 256,
        [(dproj, _bs((256, nq), lambda i, j=j: (i, j)), w_in, _resident((None, D, nq), lambda i, j=j: (j, 0, 0))) for j in range(4)],
        NT, x1, small["mix_norm"], dx2, deps)

    dx0, _, gs["ffn1_norm"] = _ffn_bwd("ffn1", x, small["ffn1_norm"], get_w, put_g, ffn1_saved, dx1, dx1_half)
    return loss_cols, dx0, gs


def _position():
    return lax.axis_index("x"), lax.axis_index("y"), lax.axis_index("c")


def _any_specs(n):
    return [pl.BlockSpec(memory_space=pl.ANY)] * n


HBM_SPEC = pl.BlockSpec(memory_space=pltpu.HBM)
SEM_SPEC = pl.BlockSpec(memory_space=pltpu.SEMAPHORE)
DATAFLOW_EFFECT = pltpu.SideEffectType.DATAFLOW_SIDE_EFFECTING
N_PEER_CHIPS = 3
LANES = 128


def _quarter_copies(srcs, lands, send_sems, recv_sems, mode):
    x, y, c = _position()
    me = 2 * x + y
    peers = [(1 - x, y, c), (x, 1 - y, c), (1 - x, 1 - y, c)]
    copies = []
    for src, land, send, recv in zip(srcs, lands, send_sems, recv_sems):
        if mode == "sibling":
            copies.append(pltpu.make_async_remote_copy(src_ref=src, dst_ref=land, send_sem=send.at[0], recv_sem=recv.at[0],
                                                       device_id=(x, y, 1 - c), device_id_type=MESH))
            continue
        scatter = mode == "scatter"
        half = land.shape[1] // 2
        mine = land.at[me, pl.ds(c * half, half)]
        for p, (px, py, pc) in enumerate(peers):
            copies.append(pltpu.make_async_remote_copy(
                src_ref=src.at[2 * px + py] if scatter else mine, dst_ref=land.at[me] if scatter else mine,
                send_sem=send.at[p], recv_sem=recv.at[p], device_id=(px, py, pc), device_id_type=MESH))
    return copies


def _fill_from_sibling(name, stacks):
    n = len(stacks)

    def body(*refs):
        outs = refs[n:2 * n]
        send_sems, recv_sems = refs[2 * n:]
        x, y, c = _position()
        copies = []
        for i, ref in enumerate(outs):
            half = ref.shape[1] // 2
            rows = pl.ds(c * half, half)
            for p, k in enumerate((2 * (1 - x) + y, 2 * x + (1 - y), 2 * (1 - x) + (1 - y))):
                cp = pltpu.make_async_remote_copy(ref.at[k, rows], ref.at[k, rows], send_sems.at[3 * i + p], recv_sems.at[3 * i + p],
                                                  device_id=(x, y, 1 - c), device_id_type=MESH)
                cp.start()
                copies.append(cp)
        for cp in copies:
            cp.wait()

    return pl.pallas_call(
        body, out_shape=[jax.ShapeDtypeStruct(s.shape, s.dtype) for s in stacks],
        in_specs=_any_specs(n), out_specs=_any_specs(n), input_output_aliases={i: i for i in range(n)},
        scratch_shapes=[pltpu.SemaphoreType.DMA((N_PEER_CHIPS * n,)), pltpu.SemaphoreType.DMA((N_PEER_CHIPS * n,))],
        compiler_params=pltpu.CompilerParams(has_side_effects=True), name=name)(*stacks)


def _exchange_start(name, srcs, lands, mode):
    n = len(lands)
    arrays = list(lands) if srcs is None else list(srcs) + list(lands)
    k = len(arrays)

    def body(*refs):
        land_refs = refs[k - n:k]
        send_sems, recv_sems = refs[k:k + n], refs[k + n:k + 2 * n]
        token = refs[2 * k + 2 * n]
        for cp in _quarter_copies(refs[:n], land_refs, send_sems, recv_sems, mode):
            cp.start()
        token[...] = jnp.zeros_like(token)

    sem = pltpu.SemaphoreType.DMA((N_PEER_CHIPS,))
    out_shape = [sem] * (2 * n) + [pltpu.HBM(a.shape, a.dtype) for a in arrays] + [jax.ShapeDtypeStruct((8, LANES), F32)]
    res = pl.pallas_call(
        body, name=name, out_shape=out_shape, in_specs=[HBM_SPEC] * k,
        out_specs=[SEM_SPEC] * (2 * n) + [HBM_SPEC] * k + [pl.BlockSpec(memory_space=pltpu.VMEM)],
        input_output_aliases={i: 2 * n + i for i in range(k)},
        compiler_params=pltpu.CompilerParams(has_side_effects=DATAFLOW_EFFECT),
    )(*[pltpu.with_memory_space_constraint(a, pltpu.HBM) for a in arrays])
    thru = res[2 * n:2 * n + k]
    return res[:n], res[n:2 * n], (None if srcs is None else thru[:n]), thru[k - n:], res[2 * n + k]


def _exchange_wait(name, srcs, lands, send_sems, recv_sems, after, mode):
    n = len(lands)
    arrays = list(lands) if srcs is None else list(srcs) + list(lands)
    k = len(arrays)

    def body(*refs):
        sends, recvs = refs[k:k + n], refs[k + n:k + 2 * n]
        for cp in _quarter_copies(refs[:n], refs[k - n:k], sends, recvs, mode):
            cp.wait_send()
            cp.wait_recv()

    res = pl.pallas_call(
        body, name=name, out_shape=[pltpu.HBM(a.shape, a.dtype) for a in arrays],
        in_specs=[HBM_SPEC] * k + [SEM_SPEC] * (2 * n) + [pl.BlockSpec(memory_space=pl.ANY)],
        out_specs=[HBM_SPEC] * k, input_output_aliases={i: i for i in range(k)},
        compiler_params=pltpu.CompilerParams(has_side_effects=DATAFLOW_EFFECT),
    )(*arrays, *send_sems, *recv_sems, after)
    return (None if srcs is None else res[:n]), res[k - n:]


def _own_slot(name, src, from_stack=False):
    R, C = src.shape[-2:]
    rows = R // 2
    me = (2 * lax.axis_index("x") + lax.axis_index("y")).astype(jnp.int32).reshape(1)

    def body(me_ref, x_ref, o_ref):
        o_ref[...] = x_ref[...].astype(o_ref.dtype)

    in_spec = (pl.BlockSpec((None, rows, C), lambda i, me_ref: (me_ref[0], i, 0)) if from_stack
               else pl.BlockSpec((rows, C), lambda i, me_ref: (i, 0)))
    grid_spec = pltpu.PrefetchScalarGridSpec(
        num_scalar_prefetch=1, grid=(R // rows,), in_specs=[in_spec],
        out_specs=pl.BlockSpec((None, rows, C), lambda i, me_ref: (me_ref[0], i, 0)))
    return pl.pallas_call(body, out_shape=jax.ShapeDtypeStruct((4, R, C), BF16), grid_spec=grid_spec,
                          compiler_params=_params("parallel"), name=name)(me, src)


def _swap_with_sibling(parts):
    n = len(parts)

    def body(*refs):
        ins, outs = refs[:n], refs[n:2 * n]
        send_sems, recv_sems = refs[2 * n:]
        x, y, c = _position()
        copies = []
        for i in range(n):
            cp = pltpu.make_async_remote_copy(ins[i], outs[i], send_sems.at[i], recv_sems.at[i],
                                              device_id=(x, y, 1 - c), device_id_type=MESH)
            cp.start()
            copies.append(cp)
        for cp in copies:
            cp.wait()

    return pl.pallas_call(
        body, out_shape=[jax.ShapeDtypeStruct(s.shape, s.dtype) for s in parts],
        in_specs=_any_specs(n), out_specs=_any_specs(n),
        scratch_shapes=[pltpu.SemaphoreType.DMA((n,)), pltpu.SemaphoreType.DMA((n,))],
        compiler_params=pltpu.CompilerParams(has_side_effects=True), name="swap_with_sibling")(*parts)


def _allreduce_small(buf):
    R, C = buf.shape
    flips = [(fx, fy, fc) for fx in (0, 1) for fy in (0, 1) for fc in (0, 1)][1:]

    def body(in_ref, out_ref, land_ref, send_sems, recv_sems):
        x, y, c = _position()
        me = 4 * x + 2 * y + c
        copies = []
        for k, (fx, fy, fc) in enumerate(flips):
            px, py, pc = (1 - x if fx else x), (1 - y if fy else y), (1 - c if fc else c)
            cp = pltpu.make_async_remote_copy(in_ref, land_ref.at[me], send_sems.at[k], recv_sems.at[k],
                                              device_id=(px, py, pc), device_id_type=MESH)
            cp.start()
            copies.append(cp)
        land_ref[me] = in_ref[...]
        for cp in copies:
            cp.wait()
        acc = land_ref[0]
        for k in range(1, 8):
            acc = acc + land_ref[k]
        out_ref[...] = acc

    return pl.pallas_call(
        body, out_shape=jax.ShapeDtypeStruct((R, C), F32),
        in_specs=[pl.BlockSpec(memory_space=pltpu.VMEM)], out_specs=pl.BlockSpec(memory_space=pltpu.VMEM),
        scratch_shapes=[pltpu.VMEM((8, R, C), F32), pltpu.SemaphoreType.DMA((7,)), pltpu.SemaphoreType.DMA((7,))],
        compiler_params=pltpu.CompilerParams(has_side_effects=True), name="allreduce_small")(buf)


def _adamw_math(w, g, m, v):
    m2 = ADAM_B1 * m + (1.0 - ADAM_B1) * g
    v2 = ADAM_B2 * v + (1.0 - ADAM_B2) * (g * g)
    m_hat = m2 / (1.0 - ADAM_B1 ** ADAM_STEP)
    v_hat = v2 / (1.0 - ADAM_B2 ** ADAM_STEP)
    delta = -ADAM_LR * (m_hat / (jnp.sqrt(v_hat) + ADAM_EPS) + ADAM_WD * w)
    return delta, m2, v2


def _adamw_big(name, w, m, v, mine, theirs):
    R, C = w.shape
    rows = 256 if R % 256 == 0 else R // 2 if (R // 2) % 16 == 0 else R
    nrb = R // rows

    def four(a, b, c, d):
        return ((a.astype(F32) + b.astype(F32)) + c.astype(F32)) + d.astype(F32)

    def fn(wv, mv, vv, *parts):
        g = four(*parts[:4]) + four(*parts[4:])
        return (g,) + _adamw_math(wv, g, mv, vv)

    slots = [_tiled(s.reshape(4 * R, C), None, 0, k * nrb) for s in (mine, theirs) for k in range(4)]
    return _ew(name, fn, [_tiled(w), _tiled(m), _tiled(v)] + slots, [(F32, C)] * 4, n_rows=R, rows=rows)


BIG = ("ffn1_w1", "ffn1_w3", "ffn1_w2", "w_in", "w_branch_a", "w_branch_b", "w_out", "ffn2_w1", "ffn2_w3", "ffn2_w2")
SMALL = ("ffn1_norm", "mix_norm", "b_gate", "q_norm", "k_norm", "rel_bias", "ffn2_norm", "final_norm")
ORDER = ("ffn1_norm", "ffn1_w1", "ffn1_w3", "ffn1_w2", "mix_norm", "w_in", "b_gate", "q_norm", "k_norm", "rel_bias",
         "w_branch_a", "w_branch_b", "w_out", "ffn2_norm", "ffn2_w1", "ffn2_w3", "ffn2_w2", "final_norm")
TRANSPOSED = ("ffn1_w1", "ffn1_w3", "ffn2_w1", "ffn2_w3")
SIBLING_LAG = 2
GATHER_GROUPS = (("ffn1_w1", "ffn1_w3"), ("ffn1_w2",), ("w_in",), ("w_branch_a", "w_branch_b", "w_out"),
                 ("ffn2_w1", "ffn2_w3", "ffn2_w2"))


def _pack_small(d):
    rows = []
    for n in SMALL:
        flat = d[n].reshape(-1)
        pad = (-flat.shape[0]) % LANES
        rows.append(jnp.pad(flat, (0, pad)).reshape(-1, LANES))
    buf = jnp.concatenate(rows, axis=0)
    return jnp.pad(buf, ((0, (-buf.shape[0]) % 8), (0, 0)))


def _unpack_small(buf, like):
    out, r = {}, 0
    for n in SMALL:
        size = like[n].size
        nr = -(-size // LANES)
        out[n] = buf[r:r + nr].reshape(-1)[:size].reshape(like[n].shape)
        r += nr
    return out


def kernel(x, ffn1_norm, ffn1_w1, ffn1_w3, ffn1_w2, mix_norm, w_in, b_gate, q_norm, k_norm, rel_bias, w_branch_a, w_branch_b, w_out, ffn2_norm, ffn2_w1, ffn2_w3, ffn2_w2, final_norm, loss_target, m_ffn1_norm, m_ffn1_w1, m_ffn1_w3, m_ffn1_w2, m_mix_norm, m_w_in, m_b_gate, m_q_norm, m_k_norm, m_rel_bias, m_w_branch_a, m_w_branch_b, m_w_out, m_ffn2_norm, m_ffn2_w1, m_ffn2_w3, m_ffn2_w2, m_final_norm, v_ffn1_norm, v_ffn1_w1, v_ffn1_w3, v_ffn1_w2, v_mix_norm, v_w_in, v_b_gate, v_q_norm, v_k_norm, v_rel_bias, v_w_branch_a, v_w_branch_b, v_w_out, v_ffn2_norm, v_ffn2_w1, v_ffn2_w3, v_ffn2_w2, v_final_norm):
    given = dict(locals())
    w = {n: given[n] for n in ORDER}
    m = {n: given["m_" + n] for n in ORDER}
    v = {n: given["v_" + n] for n in ORDER}
    T, D = x.shape[1], x.shape[2]

    def stored(a, n):
        a = a.reshape(a.shape[1:])
        return a.T if n in TRANSPOSED else a

    def returned(a, n):
        return (a.T if n in TRANSPOSED else a).reshape(w[n].shape)

    quarter = {n: stored(w[n], n) for n in BIG}
    send, recv, _, land_thru, token = _exchange_start(
        "gather_start", None, [_own_slot(f"own_{n}", quarter[n]) for n in BIG], "gather")
    index = {n: i for i, n in enumerate(BIG)}
    ready = {}

    def get_w(name, after):
        if name not in ready:
            group = next(g for g in GATHER_GROUPS if name in g)
            ids = [index[n] for n in group]
            _, stacks = _exchange_wait("gather_wait_" + group[0], None, [land_thru[i] for i in ids],
                                       [send[i] for i in ids], [recv[i] for i in ids], after, "gather")
            stacks = _fill_from_sibling("gather_fill_" + group[0], stacks)
            for n, st in zip(group, stacks):
                ready[n] = st.reshape(D, D) if n in ("w_branch_b", "w_out") else st
        return ready[name]

    scattered, forwarded = [], []

    def forward_oldest(after):
        names, s_sem, r_sem, srcs, lands = scattered.pop(0)
        _, landed = _exchange_wait("scatter_wait_" + names[0], srcs, lands, s_sem, r_sem, after, "scatter")
        started = _exchange_start("sibling_start_" + names[0], landed, [lax.empty(a.shape, a.dtype) for a in landed], "sibling")
        forwarded.append((names,) + tuple(started[:4]))
        return started[4]

    def put_g(grads):
        names = list(grads)
        stacks = [grads[n].reshape((4,) + quarter[n].shape) for n in names]
        lands = [_own_slot(f"own_grad_{n}", s, from_stack=True) for n, s in zip(names, stacks)]
        started = _exchange_start("scatter_start_" + names[0], stacks, lands, "scatter")
        scattered.append((names,) + tuple(started[:4]))
        tokens = [started[4]]
        if len(scattered) > SIBLING_LAG:
            tokens.append(forward_oldest(started[4]))
        return tokens

    small = {n: w[n] for n in SMALL}
    loss_cols, grad_x, gs = _local_step(x.reshape(T, D), loss_target.reshape(T, D), small, get_w, put_g, deps=[token])

    after = grad_x
    while scattered:
        after = forward_oldest(after)
    grads, deltas, new_m, new_v = {}, {}, {}, {}
    for names, s_sem, r_sem, srcs, lands in forwarded:
        mine, theirs = _exchange_wait("sibling_wait_" + names[0], srcs, lands, s_sem, r_sem, after, "sibling")
        for n, a, b in zip(names, mine, theirs):
            res = _adamw_big(f"adamw_{n}", quarter[n], stored(m[n], n), stored(v[n], n), a, b)
            grads[n], deltas[n], new_m[n], new_v[n] = [returned(r, n) for r in res]

    gs = {n: gs[n].reshape(w[n].shape) for n in SMALL}
    packed_g = _pack_small(gs)
    n_small = packed_g.shape[0]
    summed = _allreduce_small(jnp.concatenate([packed_g, loss_cols.reshape(-1, LANES)], axis=0))
    g_small, loss = summed[:n_small], jnp.sum(summed[n_small:])
    packed = [_pack_small({n: d[n] for n in SMALL}) for d in (w, m, v)]
    R = g_small.shape[0]
    res = _ew("adamw_small", lambda wv, mv, vv, g: (g,) + _adamw_math(wv, g, mv, vv),
              [_tiled(packed[0]), _tiled(packed[1]), _tiled(packed[2]), _tiled(g_small)], [(F32, LANES)] * 4, n_rows=R, rows=R)
    for d, buf in zip((grads, deltas, new_m, new_v), res):
        d.update(_unpack_small(buf, w))

    return (loss, grad_x.reshape(x.shape), *[grads[n] for n in ORDER], *[deltas[n] for n in ORDER],
            *[new_m[n] for n in ORDER], *[new_v[n] for n in ORDER])
```

```python
import functools
import math

import numpy as np
import jax
import jax.numpy as jnp
from jax import lax
from jax.experimental import pallas as pl
from jax.experimental.pallas import tpu as pltpu

F32 = jnp.float32
BF16 = jnp.bfloat16
MESH = pl.DeviceIdType.MESH

NEG_INF = -1e30
EPS = 1e-6
GRID_W = 64
ROPE_THETA = 10000.0
DILATIONS = (1, 4, 16)
BAND_HALF = 64
HEAD_A = 64
HEADS_A = 8
WIDTH_A = HEADS_A * HEAD_A
HEAD_B = 128
LOG2_E = math.log2(math.e)
QK_SCALE_LOG2 = HEAD_B ** -0.5 * LOG2_E
N_BUCKETS = 32
MAX_DISTANCE = 1024
ADAM_LR, ADAM_B1, ADAM_B2, ADAM_EPS, ADAM_WD, ADAM_STEP = 0.001, 0.9, 0.999, 1e-08, 0.01, 10

B_Q, B_K, B_V = 4608, 5632, 5888
G_A, G_B = 6144, 7168
IN_WIDTH = 8192

VMEM_LIMIT_BYTES = 56 * 1024 * 1024
QB_A = 128
QB_B = 256


def _params(*sem):
    return pltpu.CompilerParams(dimension_semantics=sem, vmem_limit_bytes=VMEM_LIMIT_BYTES)


def _bs(shape, fn):
    return pl.BlockSpec(shape, fn)


def _resident(shape, fn):
    return pl.BlockSpec(shape, fn, pipeline_mode=pl.Buffered(1))


def _mm(name, grid, pairs, out_shape, out_spec, dims, *, extras=(), epilogue=None, deps=(), reds=()):
    n_pairs, n_extra, n_deps = len(pairs), len(extras), len(deps)
    operands = [p[0] for p in pairs] + [p[2] for p in pairs] + [e[0] for e in extras] + list(deps)
    in_specs = [p[1] for p in pairs] + [p[3] for p in pairs] + [e[1] for e in extras] + _any_specs(n_deps)
    single = not isinstance(out_shape, (list, tuple))
    out_shapes = [out_shape] if single else list(out_shape)
    out_specs = [out_spec] if single else list(out_spec)
    n_out = len(out_shapes)
    out_shapes += [jax.ShapeDtypeStruct((1, w), F32) for w in reds]
    out_specs += [_bs((1, w), lambda *_: (0, 0)) for w in reds]

    def body(*refs):
        a_refs, b_refs = refs[:n_pairs], refs[n_pairs:2 * n_pairs]
        e_refs = refs[2 * n_pairs:2 * n_pairs + n_extra]
        o_refs = refs[2 * n_pairs + n_extra + n_deps:]
        acc = None
        for a_ref, b_ref in zip(a_refs, b_refs):
            t = lax.dot_general(a_ref[...], b_ref[...], (dims, ((), ())), preferred_element_type=F32)
            acc = t if acc is None else acc + t
        vals = acc if epilogue is None else epilogue(acc, *[e[...] for e in e_refs])
        if not isinstance(vals, (list, tuple)):
            vals = (vals,)
        for o_ref, v in zip(o_refs[:n_out], vals[:n_out]):
            o_ref[...] = v.astype(o_ref.dtype)
        if reds:
            first = functools.reduce(jnp.logical_and, [pl.program_id(ax) == 0 for ax in range(len(grid))])
            for r_ref, v in zip(o_refs[n_out:], vals[n_out:]):
                @pl.when(first)
                def _(r_ref=r_ref):
                    r_ref[...] = jnp.zeros_like(r_ref)
                r_ref[...] += v

    sem = ["arbitrary" if reds else "parallel"] * len(grid)
    res = pl.pallas_call(
        body, out_shape=out_shapes, grid=grid, in_specs=in_specs, out_specs=out_specs,
        compiler_params=_params(*sem), name=name)(*operands)
    return res[0] if (single and not reds) else res


NN = ((1,), (0,))
NT = ((1,), (1,))
TN = ((0,), (0,))


def _mm_cols(name, a, w, *, tm, tn, out_dtype, cat, extras=(), epilogue=None):
    M, K = a.shape
    J, _, n = w.shape
    tn = min(tn, n)
    nb = n // tn
    if cat:
        shape, spec = (M, J * n), _bs((tm, tn), lambda j, i, k: (i, j * nb + k))
    else:
        shape, spec = (J, M, n), _bs((None, tm, tn), lambda j, i, k: (j, i, k))
    ex = [(e, _bs((tm, tn), lambda j, i, k: (i, j * nb + k))) for e in extras]
    return _mm(name, (J, M // tm, nb),
               [(a, _bs((tm, K), lambda j, i, k: (i, 0)), w, _bs((None, K, tn), lambda j, i, k: (j, 0, k)))],
               jax.ShapeDtypeStruct(shape, out_dtype), spec, NN, extras=ex, epilogue=epilogue)


def _mm_rows_t(name, a, w, *, tm, out_dtype):
    M, N = a.shape
    J, f, _ = w.shape
    return _mm(name, (J, M // tm),
               [(a, _bs((tm, N), lambda j, i: (i, 0)), w, _bs((None, f, N), lambda j, i: (j, 0, 0)))],
               jax.ShapeDtypeStruct((J, M, f), out_dtype), _bs((None, tm, f), lambda j, i: (j, i, 0)), NT)


def _mm_wgrad(name, a, b, *, a_cols, b_cols, tm, tn, J):
    def pick(arr, cols, t):
        if arr.ndim == 3:
            T, c = arr.shape[1], arr.shape[2]
            t = min(t, c)
            return T, c, t, (lambda sel: _bs((None, T, t), lambda j, i, k: (j, 0, sel(i, k))))
        T = arr.shape[0]
        c = arr.shape[1] if cols is None else cols
        t = min(t, c)
        per = c // t
        if cols is None:
            if per == 1:
                return T, c, t, (lambda sel: _resident((T, t), lambda j, i, k: (0, 0)))
            return T, c, t, (lambda sel: _bs((T, t), lambda j, i, k: (0, sel(i, k))))
        return T, c, t, (lambda sel: _bs((T, t), lambda j, i, k: (0, j * per + sel(i, k))))
    _, ca, tm, mk_a = pick(a, a_cols, tm)
    _, cb, tn, mk_b = pick(b, b_cols, tn)
    return _mm(name, (J, ca // tm, cb // tn),
               [(a, mk_a(lambda i, k: i), b, mk_b(lambda i, k: k))],
               jax.ShapeDtypeStruct((J, ca, cb), BF16), _bs((None, tm, tn), lambda j, i, k: (j, i, k)), TN)


def _tiled(arr, width=None, col=0, rowblk=0):
    return ("t", arr, arr.shape[1] if width is None else width, col, rowblk)


def _table(arr):
    return ("f", arr)


def _whole(arr):
    return ("w", arr)


def _ew(name, fn, ins, outs, *, n_rows, rows, reds=(), ncols=1, deps=()):
    nrb = n_rows // rows
    n_deps = len(deps)
    operands, in_specs = [], []
    for spec in ins:
        if spec[0] == "t":
            _, arr, width, col, rowblk = spec
            step = 1 if ncols > 1 else 0
            in_specs.append(_bs((rows, width), lambda c, i, col=col, rowblk=rowblk, step=step: (rowblk + i, col + c * step)))
        elif spec[0] == "f":
            arr = spec[1]
            in_specs.append(_bs((rows, arr.shape[1]), lambda c, i: (i, 0)))
        else:
            arr = spec[1]
            nd = arr.ndim
            if nd == 3:
                in_specs.append(_bs((None,) + arr.shape[1:], lambda c, i: (c, 0, 0)))
            else:
                in_specs.append(_bs(arr.shape, lambda c, i, nd=nd: (0,) * nd))
        operands.append(arr)
    out_shapes = [jax.ShapeDtypeStruct((n_rows, ncols * w), dt) for dt, w in outs]
    out_specs = [_bs((rows, w), lambda c, i: (i, c)) for _, w in outs]
    out_shapes += [jax.ShapeDtypeStruct((ncols, 1, w), F32) for w in reds]
    out_specs += [_bs((None, 1, w), lambda c, i: (c, 0, 0)) for w in reds]
    n_in, n_out, n_red = len(ins), len(outs), len(reds)
    operands += list(deps)
    in_specs += _any_specs(n_deps)

    def body(*refs):
        vals = fn(*[r[...] for r in refs[:n_in]])
        if not isinstance(vals, (tuple, list)):
            vals = (vals,)
        o_refs = refs[n_in + n_deps:]
        for o_ref, v in zip(o_refs[:n_out], vals[:n_out]):
            o_ref[...] = v.astype(o_ref.dtype)
        if n_red:
            i = pl.program_id(1)
            for r_ref, v in zip(o_refs[n_out:], vals[n_out:]):
                @pl.when(i == 0)
                def _(r_ref=r_ref):
                    r_ref[...] = jnp.zeros_like(r_ref)
                r_ref[...] += v

    res = pl.pallas_call(
        body, out_shape=out_shapes, grid=(ncols, nrb), in_specs=in_specs, out_specs=out_specs,
        compiler_params=_params("parallel", "arbitrary" if n_red else "parallel"), name=name)(*operands)
    return res


def _colsum(v):
    return jnp.sum(v, axis=0, keepdims=True)


def _rstd(x):
    return lax.rsqrt(jnp.mean(x * x, axis=-1, keepdims=True) + EPS)


def _sigmoid(x):
    return 1.0 / (1.0 + jnp.exp(-x))


def _norm_fwd(x, g):
    return x * _rstd(x) * g


def _norm_bwd(x, g, dy):
    r = _rstd(x)
    xh = x * r
    dxh = dy * g
    dx = r * (dxh - xh * jnp.mean(dxh * xh, axis=-1, keepdims=True))
    return dx, dy * xh


def _row_spec(arr, rows):
    if arr.shape[0] == 1:
        return _bs(arr.shape, lambda i: (0, 0))
    return _bs((rows, arr.shape[1]), lambda i: (i, 0))


def _ffn_fwd(tag, x, gain, get_w, deps=(), *, h=None, tail_ins=(), tail_fn=None, tail_outs=(F32,), tail_reds=()):
    T, D = x.shape
    if h is None:
        (h,) = _ew(f"{tag}_norm", lambda xv, g: _norm_fwd(xv, g), [_tiled(x), _whole(gain)], [(BF16, D)], n_rows=T, rows=512,
                   deps=deps)
    w1, w3 = get_w(f"{tag}_w1", h), get_w(f"{tag}_w3", h)
    J, f, _ = w1.shape
    tm = 1024

    def up(h_ref, w1_ref, w3_ref, u_ref, g_ref, a_ref):
        hv = h_ref[...]
        u = lax.dot_general(hv, w1_ref[...], (NT, ((), ())), preferred_element_type=F32)
        g = lax.dot_general(hv, w3_ref[...], (NT, ((), ())), preferred_element_type=F32)
        u_ref[...] = u.astype(BF16)
        g_ref[...] = g.astype(BF16)
        a_ref[...] = (u * _sigmoid(u) * g).astype(BF16)

    slab = _bs((None, tm, f), lambda j, i: (j, i, 0))
    w_spec = _bs((None, f, D), lambda j, i: (j, 0, 0))
    u, g, a = pl.pallas_call(
        up, out_shape=[jax.ShapeDtypeStruct((J, T, f), BF16)] * 3, grid=(J, T // tm),
        in_specs=[_bs((tm, D), lambda j, i: (i, 0)), w_spec, w_spec], out_specs=[slab] * 3,
        compiler_params=_params("parallel", "parallel"), name=f"{tag}_up")(h, w1, w3)
    w2 = get_w(f"{tag}_w2", a)
    def tail(acc, xv, *rest):
        y = xv + 0.5 * acc
        return y if tail_fn is None else tail_fn(y, *rest)

    row = _bs((512, D), lambda i: (i, 0))
    res = _mm(f"{tag}_down", (T // 512,),
              [(a, _bs((None, 512, f), lambda i, j=j: (j, i, 0)), w2, _resident((None, f, D), lambda i, j=j: (j, 0, 0)))
               for j in range(J)],
              [jax.ShapeDtypeStruct((T, D), dt) for dt in tail_outs], [row] * len(tail_outs), NN,
              extras=[(x, row)] + [(t, _row_spec(t, 512)) for t in tail_ins], epilogue=tail, reds=tail_reds)
    return res, (h, u, g, a)


def _dh_norm_bwd(name, rows, pairs, dims, x, gain, dres, deps, also_bf16=False):
    T, D = x.shape

    def epilogue(dh, xv, gv, dr):
        dx, dgr = _norm_bwd(xv, gv, dh)
        dx = dx + dr
        return (dx, 0.5 * dx) + ((dx,) if also_bf16 else ()) + (_colsum(dgr),)

    dts = [F32, BF16] + ([BF16] if also_bf16 else [])
    row = _bs((rows, D), lambda i: (i, 0))
    return _mm(name, (T // rows,), pairs, [jax.ShapeDtypeStruct((T, D), dt) for dt in dts], [row] * len(dts), dims,
               extras=[(x, row), (gain, _row_spec(gain, rows)), (dres, row)], epilogue=epilogue, deps=deps, reds=(D,))


def _ffn_bwd(tag, x, gain, get_w, put_g, saved, dy, dy_half, also_bf16=False):
    h, u, g, a = saved
    T, D = x.shape
    w1, w3, w2 = [get_w(f"{tag}_{n}", dy_half) for n in ("w1", "w3", "w2")]
    J, f, _ = w1.shape
    dw2 = _mm_wgrad(f"{tag}_bwd_dw2", a, dy_half, a_cols=None, b_cols=None, tm=f, tn=D, J=J)
    deps = put_g({f"{tag}_w2": dw2})
    tm = 1024

    def up_bwd(dy_ref, w2_ref, u_ref, g_ref, *rest):
        du_ref, dg_ref = rest[-2:]
        da = lax.dot_general(dy_ref[...], w2_ref[...], (NT, ((), ())), preferred_element_type=F32)
        uv, gv = u_ref[...].astype(F32), g_ref[...].astype(F32)
        s = _sigmoid(uv)
        du_ref[...] = (da * gv * (s * (1.0 + uv * (1.0 - s)))).astype(BF16)
        dg_ref[...] = (da * (uv * s)).astype(BF16)

    slab = _bs((None, tm, f), lambda j, i: (j, i, 0))
    du, dg = pl.pallas_call(
        up_bwd, out_shape=[jax.ShapeDtypeStruct((J, T, f), BF16)] * 2, grid=(J, T // tm),
        in_specs=[_bs((tm, D), lambda j, i: (i, 0)), _bs((None, f, D), lambda j, i: (j, 0, 0)), slab, slab] + _any_specs(len(deps)),
        out_specs=[slab] * 2, compiler_params=_params("parallel", "parallel"), name=f"{tag}_bwd_up")(dy_half, w2, u, g, *deps)
    dw1 = _mm_wgrad(f"{tag}_bwd_dw1", du, h, a_cols=None, b_cols=None, tm=f, tn=D, J=J)
    dw3 = _mm_wgrad(f"{tag}_bwd_dw3", dg, h, a_cols=None, b_cols=None, tm=f, tn=D, J=J)
    deps = deps + put_g({f"{tag}_w1": dw1, f"{tag}_w3": dw3})
    pairs = []
    for j in range(J):
        a_spec = _bs((None, 512, f), lambda i, j=j: (j, i, 0))
        w_spec = _resident((None, f, D), lambda i, j=j: (j, 0, 0))
        pairs += [(du, a_spec, w1, w_spec), (dg, a_spec, w3, w_spec)]
    return _dh_norm_bwd(f"{tag}_bwd_dh", 512, pairs, NN, x, gain, dy, deps, also_bf16)


def _t5_bucket(rel):
    n = N_BUCKETS // 2
    max_exact = n // 2
    ret = jnp.where(rel > 0, n, 0)
    a = jnp.abs(rel)
    af = jnp.maximum(a, 1).astype(F32)
    large = max_exact + (jnp.log(af / max_exact) / math.log(MAX_DISTANCE / max_exact) * (n - max_exact)).astype(jnp.int32)
    large = jnp.minimum(large, n - 1)
    return ret + jnp.where(a < max_exact, a, large)


WIN_A = QB_A + 2 * BAND_HALF
WIN_SHIFTS = (0, BAND_HALF, 2 * BAND_HALF)


def _window_variant(n, nblk):
    return jnp.where(n == 0, 0, jnp.where(n == nblk - 1, 2, 1))


def _window_start(n, nblk):
    return pl.multiple_of(jnp.clip(n * QB_A - BAND_HALF, 0, nblk * QB_A - WIN_A), BAND_HALF)


def _band_steps(xp=jnp):
    qi = xp.arange(QB_A, dtype=xp.int32)[None, :, None]
    kj = xp.arange(WIN_A, dtype=xp.int32)[None, None, :]
    return kj - qi - xp.asarray(WIN_SHIFTS, dtype=xp.int32)[:, None, None]


def _bias_tiles(rel_bias):
    wide = QB_A + 2 * WIN_SHIFTS[-1]
    qi = jnp.arange(QB_A, dtype=jnp.int32)[:, None]
    steps = jnp.arange(wide, dtype=jnp.int32)[None, :] - WIN_SHIFTS[-1] - qi
    buckets = jnp.stack([_t5_bucket(steps * d) for d in DILATIONS])
    inband = (jnp.abs(steps) <= BAND_HALF).astype(jnp.int32)
    n_heads = rel_bias.shape[1]

    def body(tab_ref, b_ref, m_ref, o_ref):
        hd = pl.program_id(0)
        bkt = b_ref[...]
        acc = jnp.zeros(bkt.shape, F32)
        for b in range(N_BUCKETS):
            acc = jnp.where(bkt == b, tab_ref[b, hd], acc)
        o_ref[...] = jnp.where(m_ref[...] > 0, acc, NEG_INF)

    base = pl.pallas_call(
        body, out_shape=jax.ShapeDtypeStruct((n_heads, QB_A, wide), F32), grid=(n_heads,),
        in_specs=[pl.BlockSpec(memory_space=pltpu.SMEM),
                  _bs((None, QB_A, wide), lambda hd: (hd // HEADS_A, 0, 0)),
                  _bs((QB_A, wide), lambda hd: (0, 0))],
        out_specs=_bs((None, QB_A, wide), lambda hd: (hd, 0, 0)),
        compiler_params=_params("parallel"), name="a_bias_tiles")(rel_bias, buckets, inband)
    base = base.reshape(len(DILATIONS), HEADS_A, QB_A, wide)
    return jnp.stack([base[..., WIN_SHIFTS[-1] - s:WIN_SHIFTS[-1] - s + WIN_A] for s in WIN_SHIFTS], axis=1)


def _bias_grad(dbias):
    steps = _band_steps(np)
    inband = np.abs(steps) <= BAND_HALF
    present = []
    for d in DILATIONS:
        rel = steps * d
        a = np.abs(rel)
        large = 8 + (np.log(np.maximum(a, 1) / 8.0) / math.log(MAX_DISTANCE / 8.0) * 8).astype(np.int64)
        bk = np.where(rel > 0, 16, 0) + np.where(a < 8, a, np.minimum(large, 15))
        present.append([sorted(set(bk[v][inband[v]].tolist())) for v in range(3)])
    buckets = jnp.stack([_t5_bucket(_band_steps() * d) for d in DILATIONS])
    n_heads = len(DILATIONS) * HEADS_A

    def body(b_ref, d_ref, o_ref):
        row = lax.broadcasted_iota(jnp.int32, (N_BUCKETS, n_heads), 0)
        col = lax.broadcasted_iota(jnp.int32, (N_BUCKETS, n_heads), 1)
        out = jnp.zeros((N_BUCKETS, n_heads), F32)
        for grp in range(len(DILATIONS)):
            for hh in range(HEADS_A):
                hd = grp * HEADS_A + hh
                for b in sorted(set(sum(present[grp], []))):
                    tot = jnp.zeros((), F32)
                    for v in range(3):
                        if b in present[grp][v]:
                            tot = tot + jnp.sum(jnp.where(b_ref[grp, v] == b, d_ref[grp, v, hh], 0.0))
                    out = jnp.where((row == b) & (col == hd), tot, out)
        o_ref[...] = out

    return pl.pallas_call(
        body, out_shape=jax.ShapeDtypeStruct((N_BUCKETS, n_heads), F32),
        compiler_params=pltpu.CompilerParams(vmem_limit_bytes=VMEM_LIMIT_BYTES), name="a_bias_grad")(buckets, dbias)


def _lane_is_second_head(shape):
    return lax.broadcasted_iota(jnp.int32, shape, len(shape) - 1) >= HEAD_A


VIEW_ROWS = 512


def _view_chunks():
    return [pltpu.VMEM((VIEW_ROWS, LANES), F32)] * (WIDTH_A // LANES)


def _rows_to_view(x_ref, col, o_ref, ocol, d, chunks):
    n = VIEW_ROWS // d
    for c, scr in enumerate(chunks):
        scr[...] = x_ref[:, col + c * LANES:col + (c + 1) * LANES].astype(F32)
        for r in range(d):
            at = ocol + r * WIDTH_A + c * LANES
            o_ref[:, at:at + LANES] = scr[pl.ds(r, n, stride=d), :].astype(o_ref.dtype)


def _view_to_rows(v_ref, o_ref, col, d, chunks):
    n = VIEW_ROWS // d
    for c, scr in enumerate(chunks):
        if d == 1:
            o_ref[:, col + c * LANES:col + (c + 1) * LANES] = v_ref[:, c * LANES:(c + 1) * LANES].astype(o_ref.dtype)
            continue
        for r in range(d):
            scr[pl.ds(r, n, stride=d), :] = v_ref[:, r * WIDTH_A + c * LANES:r * WIDTH_A + (c + 1) * LANES].astype(F32)
        o_ref[:, col + c * LANES:col + (c + 1) * LANES] = scr[...].astype(o_ref.dtype)


def _group_view(proj, grp, d):
    T = proj.shape[0]
    if d == 1:
        return proj, (lambda part, r: grp * 3 + part)

    def body(x_ref, o_ref, *chunks):
        for part in range(3):
            _rows_to_view(x_ref, part * WIDTH_A, o_ref, part * d * WIDTH_A, d, chunks)

    view = pl.pallas_call(
        body, out_shape=jax.ShapeDtypeStruct((T // d, 3 * d * WIDTH_A), proj.dtype), grid=(T // VIEW_ROWS,),
        in_specs=[_bs((VIEW_ROWS, 3 * WIDTH_A), lambda i: (i, grp))],
        out_specs=_bs((VIEW_ROWS // d, 3 * d * WIDTH_A), lambda i: (i, 0)),
        scratch_shapes=_view_chunks(), compiler_params=_params("parallel"), name=f"a_view_d{d}")(proj)
    return view, (lambda part, r: part * d + r)


def _stack_heads(v2, second):
    zero = jnp.zeros_like(v2)
    return jnp.concatenate([jnp.where(second, zero, v2), jnp.where(second, v2, zero)], axis=0)


def _unstack_heads(v, second):
    return jnp.where(second, v[QB_A:], v[:QB_A])


def _dil_fwd(view, bias, d):
    pv, colblk = view
    L = pv.shape[0]
    nblk = L // QB_A
    W2 = 2 * HEAD_A
    scale = HEAD_A ** -0.5

    def body(q_ref, k_ref, v_ref, b_ref, o_ref, l_ref):
        win = pl.ds(_window_start(pl.program_id(1), nblk), WIN_A)
        second = _lane_is_second_head((QB_A, W2))
        for hp in range(HEADS_A // 2):
            cols = slice(hp * W2, (hp + 1) * W2)
            kw, vw = k_ref[win, cols], v_ref[win, cols]
            qs = _stack_heads(q_ref[:, cols], second)
            s = lax.dot_general(qs, kw, (NT, ((), ())), preferred_element_type=F32)
            s = s * scale + b_ref[2 * hp:2 * hp + 2].reshape(2 * QB_A, WIN_A)
            m = jnp.max(s, axis=-1, keepdims=True)
            p = jnp.exp(s - m)
            l = jnp.sum(p, axis=-1, keepdims=True)
            res = jnp.dot(p.astype(BF16), vw, preferred_element_type=F32) / l
            o_ref[:, cols] = _unstack_heads(res, second).astype(o_ref.dtype)
            l_ref[:, cols] = _unstack_heads(jnp.broadcast_to(m + jnp.log(l), (2 * QB_A, W2)), second)

    in_specs = [_bs((QB_A, WIDTH_A), lambda r, n: (n, colblk(0, r))),
                _bs((L, WIDTH_A), lambda r, n: (0, colblk(1, r))), _bs((L, WIDTH_A), lambda r, n: (0, colblk(2, r))),
                _bs((None, HEADS_A, QB_A, WIN_A), lambda r, n: (_window_variant(n, nblk), 0, 0, 0))]
    o, lse = pl.pallas_call(
        body, out_shape=[jax.ShapeDtypeStruct((L, d * WIDTH_A), BF16), jax.ShapeDtypeStruct((L, d * WIDTH_A), F32)],
        grid=(d, nblk), in_specs=in_specs,
        out_specs=[_bs((QB_A, WIDTH_A), lambda r, n: (n, r)), _bs((QB_A, WIDTH_A), lambda r, n: (n, r))],
        compiler_params=_params("parallel", "parallel"), name=f"a_fwd_d{d}")(pv, pv, pv, bias)
    return o, lse


def _dil_bwd(view_qkv, bias, do, lse, cterm, d):
    pv, colblk = view_qkv
    L = pv.shape[0]
    nblk = L // QB_A
    W2 = 2 * HEAD_A
    PPS = 4
    WS = PPS * W2
    ob = WIDTH_A // WS
    scale = HEAD_A ** -0.5

    def body(q_ref, k_ref, v_ref, do_ref, l_ref, c_ref, b_ref, dq_ref, dk_ref, dv_ref, db_ref):
        r, n = pl.program_id(1), pl.program_id(2)

        @pl.when(n == 0)
        def _():
            dk_ref[...] = jnp.zeros_like(dk_ref)
            dv_ref[...] = jnp.zeros_like(dv_ref)

        @pl.when((n == 0) & (r == 0))
        def _():
            db_ref[...] = jnp.zeros_like(db_ref)

        second = _lane_is_second_head((QB_A, W2))
        win = pl.ds(_window_start(n, nblk), WIN_A)
        variant = _window_variant(n, nblk)
        for pp in range(PPS):
            cols = slice(pp * W2, (pp + 1) * W2)
            kw, vw = k_ref[win, cols], v_ref[win, cols]
            qs, dos = _stack_heads(q_ref[:, cols], second), _stack_heads(do_ref[:, cols], second)
            lse2, c2 = l_ref[:, cols], c_ref[:, cols]
            lse_rows = jnp.concatenate([lse2[:, 0:1], lse2[:, HEAD_A:HEAD_A + 1]], axis=0)
            c_rows = jnp.concatenate([c2[:, 0:1], c2[:, HEAD_A:HEAD_A + 1]], axis=0)
            s = lax.dot_general(qs, kw, (NT, ((), ())), preferred_element_type=F32)
            p = jnp.exp(s * scale + b_ref[2 * pp:2 * pp + 2].reshape(2 * QB_A, WIN_A) - lse_rows)
            dp = lax.dot_general(dos, vw, (NT, ((), ())), preferred_element_type=F32)
            ds = p * (dp + c_rows)
            db_ref[variant, 2 * pp:2 * pp + 2] += ds.reshape(2, QB_A, WIN_A)
            pb, dsb = p.astype(BF16), (ds * scale).astype(BF16)
            dq_ref[:, cols] = _unstack_heads(jnp.dot(dsb, kw, preferred_element_type=F32), second).astype(dq_ref.dtype)
            dk_ref[win, cols] += lax.dot_general(dsb, qs, (TN, ((), ())), preferred_element_type=F32)
            dv_ref[win, cols] += lax.dot_general(pb, dos, (TN, ((), ())), preferred_element_type=F32)

    kv_spec = _resident if d == 1 else _bs
    in_specs = [_bs((QB_A, WS), lambda hp, r, n: (n, colblk(0, r) * ob + hp)),
                kv_spec((L, WS), lambda hp, r, n: (0, colblk(1, r) * ob + hp)),
                kv_spec((L, WS), lambda hp, r, n: (0, colblk(2, r) * ob + hp))]
    in_specs += [_bs((QB_A, WS), lambda hp, r, n: (n, r * ob + hp))] * 3
    in_specs += [_bs((None, 2 * PPS, QB_A, WIN_A), lambda hp, r, n: (_window_variant(n, nblk), hp, 0, 0))]
    out_shape = [jax.ShapeDtypeStruct((L, d * WIDTH_A), BF16), jax.ShapeDtypeStruct((L, d * WIDTH_A), F32),
                 jax.ShapeDtypeStruct((L, d * WIDTH_A), F32), jax.ShapeDtypeStruct((3, HEADS_A, QB_A, WIN_A), F32)]
    out_specs = [_bs((QB_A, WS), lambda hp, r, n: (n, r * ob + hp)),
                 _bs((L, WS), lambda hp, r, n: (0, r * ob + hp)), _bs((L, WS), lambda hp, r, n: (0, r * ob + hp)),
                 _bs((3, 2 * PPS, QB_A, WIN_A), lambda hp, r, n: (0, hp, 0, 0))]
    dq, dk, dv, db = pl.pallas_call(
        body, out_shape=out_shape, grid=(ob, d, nblk), in_specs=in_specs, out_specs=out_specs,
        compiler_params=_params("arbitrary", "arbitrary", "arbitrary"), name=f"a_bwd_d{d}")(
            pv, pv, pv, do, lse, cterm, bias)
    return dq, dk, dv, db


def _assemble_dproj(a_parts, dq_b, dk_b, dv_b, dga, dgb):
    T = dq_b.shape[0]
    flat = [(a_parts[part][g], d) for part in range(3) for g, d in enumerate(DILATIONS)]
    rest = [dq_b, dk_b, dv_b, dga, dgb]

    def body(*refs):
        views, others = refs[:len(flat)], refs[len(flat):len(flat) + len(rest)]
        o_ref, chunks = refs[len(flat) + len(rest)], refs[len(flat) + len(rest) + 1:]
        col = 0
        for v_ref, (_, d) in zip(views, flat):
            _view_to_rows(v_ref, o_ref, col, d, chunks)
            col += WIDTH_A
        for x_ref in others:
            w = x_ref.shape[1]
            o_ref[:, col:col + w] = x_ref[...].astype(o_ref.dtype)
            col += w

    in_specs = [_bs((VIEW_ROWS // d, d * WIDTH_A), lambda i: (i, 0)) for _, d in flat]
    in_specs += [_bs((VIEW_ROWS, x.shape[1]), lambda i: (i, 0)) for x in rest]
    return pl.pallas_call(
        body, out_shape=jax.ShapeDtypeStruct((T, IN_WIDTH), BF16), grid=(T // VIEW_ROWS,), in_specs=in_specs,
        out_specs=_bs((VIEW_ROWS, IN_WIDTH), lambda i: (i, 0)), scratch_shapes=_view_chunks(),
        compiler_params=_params("parallel"), name="mix_bwd_dproj")(*[a for a, _ in flat], *rest)


def _segment_ones():
    i = np.arange(WIDTH_A)
    return jnp.asarray((i[:, None] // HEAD_A == i[None, :] // HEAD_A).astype(np.float32), dtype=BF16)


def _group_weights(l0, l1, l2):
    m = jnp.maximum(jnp.maximum(l0, l1), l2)
    e = [jnp.exp(l - m) for l in (l0, l1, l2)]
    z = e[0] + e[1] + e[2]
    return [ei / z for ei in e]


def _view_specs():
    return [_bs((VIEW_ROWS // d, d * WIDTH_A), lambda i: (i, 0)) for d in DILATIONS]


def _stage_tiles(n):
    return [pltpu.VMEM((VIEW_ROWS, WIDTH_A), F32)] * n


def _combine_fwd(outs, lses):
    T = outs[0].shape[0] * DILATIONS[0]
    n = len(DILATIONS)

    def body(*refs):
        o_refs, l_refs, oa_ref = refs[:n], refs[n:2 * n], refs[2 * n]
        o_st, l_st, chunks = refs[2 * n + 1:3 * n + 1], refs[3 * n + 1:4 * n + 1], refs[4 * n + 1:]
        for g, d in enumerate(DILATIONS):
            _view_to_rows(o_refs[g], o_st[g], 0, d, chunks)
            _view_to_rows(l_refs[g], l_st[g], 0, d, chunks)
        w = _group_weights(*[l[...] for l in l_st])
        oa_ref[...] = (w[0] * o_st[0][...] + w[1] * o_st[1][...] + w[2] * o_st[2][...]).astype(oa_ref.dtype)

    return pl.pallas_call(
        body, out_shape=jax.ShapeDtypeStruct((T, WIDTH_A), BF16), grid=(T // VIEW_ROWS,),
        in_specs=_view_specs() * 2, out_specs=_bs((VIEW_ROWS, WIDTH_A), lambda i: (i, 0)),
        scratch_shapes=_stage_tiles(2 * n) + _view_chunks(), compiler_params=_params("parallel"), name="a_combine")(*outs, *lses)


def _combine_bwd(doa, outs, lses):
    T = doa.shape[0]
    n = len(DILATIONS)

    def body(*refs):
        d_ref, o_refs, l_refs, seg_ref = refs[0], refs[1:n + 1], refs[n + 1:2 * n + 1], refs[2 * n + 1]
        do_refs, c_refs = refs[2 * n + 2:3 * n + 2], refs[3 * n + 2:4 * n + 2]
        o_st, l_st = refs[4 * n + 2:5 * n + 2], refs[5 * n + 2:6 * n + 2]
        tmp, chunks = refs[6 * n + 2], refs[6 * n + 3:]
        for g, d in enumerate(DILATIONS):
            _view_to_rows(o_refs[g], o_st[g], 0, d, chunks)
            _view_to_rows(l_refs[g], l_st[g], 0, d, chunks)
        dv = d_ref[...].astype(F32)
        w = _group_weights(*[l[...] for l in l_st])
        seg = seg_ref[...]
        tot = jnp.zeros(dv.shape, F32)
        for g in range(n):
            prod = w[g] * dv * o_st[g][...]
            hi = prod.astype(BF16)
            lo = (prod - hi.astype(F32)).astype(BF16)
            tot = tot + jnp.dot(hi, seg, preferred_element_type=F32) + jnp.dot(lo, seg, preferred_element_type=F32)
        for g, d in enumerate(DILATIONS):
            tmp[...] = w[g] * dv
            _rows_to_view(tmp, 0, do_refs[g], 0, d, chunks)
            tmp[...] = -w[g] * tot
            _rows_to_view(tmp, 0, c_refs[g], 0, d, chunks)

    views = [jax.ShapeDtypeStruct((T // d, d * WIDTH_A), dt) for dt in (BF16, F32) for d in DILATIONS]
    res = pl.pallas_call(
        body, out_shape=views, grid=(T // VIEW_ROWS,),
        in_specs=[_bs((VIEW_ROWS, WIDTH_A), lambda i: (i, 0))] + _view_specs() * 2 + [_bs((WIDTH_A, WIDTH_A), lambda i: (0, 0))],
        out_specs=_view_specs() * 2, scratch_shapes=_stage_tiles(2 * n + 1) + _view_chunks(),
        compiler_params=_params("parallel"), name="a_combine_bwd")(doa, *outs, *lses, _segment_ones())
    return res[:n], res[n:]


def _rope_tables(T):
    rows = T // GRID_W
    row = jnp.repeat(jnp.arange(rows, dtype=F32), GRID_W)
    col = jnp.tile(jnp.arange(GRID_W, dtype=F32), rows)
    n_freq = HEAD_B // 4
    freq = ROPE_THETA ** (-jnp.arange(n_freq, dtype=F32) / n_freq)
    ang = jnp.concatenate([row[:, None] * freq, col[:, None] * freq], axis=-1)
    cos, sin = jnp.repeat(jnp.cos(ang), 2, axis=1), jnp.repeat(jnp.sin(ang), 2, axis=1)
    sign = jnp.where(jnp.arange(HEAD_B) % 2 == 0, -1.0, 1.0).astype(F32)
    return cos, sin * sign


def _swap_pairs(v):
    even = lax.broadcasted_iota(jnp.int32, v.shape, v.ndim - 1) % 2 == 0
    n = v.shape[-1]
    return jnp.where(even, pltpu.roll(v, n - 1, v.ndim - 1), pltpu.roll(v, 1, v.ndim - 1))


def _qk_fwd(name, proj, col0, n_heads, gain, cos, sin, out_scale=1.0):
    T = proj.shape[0]

    def fn(xr, g, c, s):
        xn = _norm_fwd(xr.astype(F32), g)
        return (xn * c + _swap_pairs(xn) * s) * out_scale

    (out,) = _ew(name, fn, [_tiled(proj, HEAD_B, col0 // HEAD_B), _whole(gain), _table(cos), _table(sin)],
                 [(BF16, HEAD_B)], n_rows=T, rows=2048, ncols=n_heads)
    return out


def _qk_bwd(name, dout, proj, col0, n_heads, gain, cos, sin, in_scale=1.0):
    T = proj.shape[0]

    def fn(dv, xr, g, c, s):
        dv = dv.astype(F32) * in_scale
        dxn = c * dv + _swap_pairs(s * dv)
        dx, dgr = _norm_bwd(xr.astype(F32), g, dxn)
        return dx, _colsum(dgr)

    dx, dg = _ew(name, fn, [_tiled(dout, HEAD_B, 0), _tiled(proj, HEAD_B, col0 // HEAD_B), _whole(gain),
                            _table(cos), _table(sin)],
                 [(BF16, HEAD_B)], n_rows=T, rows=2048, reds=(HEAD_B,), ncols=n_heads)
    return dx, jnp.sum(dg, axis=0)


def _gqa_fwd(qn, kn, proj):
    T = qn.shape[0]
    GW = 4 * HEAD_B

    def body(q_ref, k_ref, v_ref, o_ref, l_ref):
        k, v = k_ref[...], v_ref[...]
        lane = lax.broadcasted_iota(jnp.int32, (QB_B, HEAD_B), 1)
        lse_all = jnp.zeros((QB_B, HEAD_B), F32)
        for g in range(4):
            cols = slice(g * HEAD_B, (g + 1) * HEAD_B)
            s = lax.dot_general(q_ref[:, cols], k, (NT, ((), ())), preferred_element_type=F32)
            m = jnp.max(s, axis=-1, keepdims=True)
            p = jnp.exp2(s - m)
            l = jnp.sum(p, axis=-1, keepdims=True)
            o = jnp.dot(p.astype(BF16), v, preferred_element_type=F32) / l
            o_ref[:, cols] = o.astype(o_ref.dtype)
            lse_all = jnp.where(lane == g, m + jnp.log2(l), lse_all)
        l_ref[...] = lse_all

    return pl.pallas_call(
        body, out_shape=[jax.ShapeDtypeStruct((T, 2 * GW), BF16), jax.ShapeDtypeStruct((2, T, HEAD_B), F32)],
        grid=(2, T // QB_B),
        in_specs=[_bs((QB_B, GW), lambda kv, i: (i, kv)), _bs((T, HEAD_B), lambda kv, i: (0, kv)),
                  _bs((T, HEAD_B), lambda kv, i: (0, B_V // HEAD_B + kv))],
        out_specs=[_bs((QB_B, GW), lambda kv, i: (i, kv)), _bs((None, QB_B, HEAD_B), lambda kv, i: (kv, i, 0))],
        compiler_params=_params("parallel", "parallel"), name="b_fwd")(qn, kn, proj)


def _gqa_bwd(qn, kn, proj, o, lse, do, deps=()):
    T = qn.shape[0]
    GW = 4 * HEAD_B

    def body(q_ref, k_ref, v_ref, o_ref, l_ref, do_ref, *rest):
        dq_ref, dk_ref, dv_ref = rest[-3:]
        i = pl.program_id(1)

        @pl.when(i == 0)
        def _():
            dk_ref[...] = jnp.zeros_like(dk_ref)
            dv_ref[...] = jnp.zeros_like(dv_ref)

        k, v = k_ref[...], v_ref[...]
        lse_all = l_ref[...]
        for g in range(4):
            cols = slice(g * HEAD_B, (g + 1) * HEAD_B)
            q, dob = q_ref[:, cols], do_ref[:, cols]
            delta = jnp.sum(dob.astype(F32) * o_ref[:, cols].astype(F32), axis=-1, keepdims=True)
            s = lax.dot_general(q, k, (NT, ((), ())), preferred_element_type=F32)
            p = jnp.exp2(s - lse_all[:, g:g + 1])
            dp = lax.dot_general(dob, v, (NT, ((), ())), preferred_element_type=F32)
            ds = (p * (dp - delta)).astype(BF16)
            dq_ref[:, cols] = jnp.dot(ds, k, preferred_element_type=F32).astype(dq_ref.dtype)
            dk_ref[...] += lax.dot_general(ds, q, (TN, ((), ())), preferred_element_type=F32)
            dv_ref[...] += lax.dot_general(p.astype(BF16), dob, (TN, ((), ())), preferred_element_type=F32)

    return pl.pallas_call(
        body, out_shape=[jax.ShapeDtypeStruct((T, 2 * GW), BF16), jax.ShapeDtypeStruct((T, 2 * HEAD_B), F32),
                         jax.ShapeDtypeStruct((T, 2 * HEAD_B), F32)],
        grid=(2, T // QB_B),
        in_specs=[_bs((QB_B, GW), lambda kv, i: (i, kv)), _bs((T, HEAD_B), lambda kv, i: (0, kv)),
                  _bs((T, HEAD_B), lambda kv, i: (0, B_V // HEAD_B + kv)), _bs((QB_B, GW), lambda kv, i: (i, kv)),
                  _bs((None, QB_B, HEAD_B), lambda kv, i: (kv, i, 0)), _bs((QB_B, GW), lambda kv, i: (i, kv))] + _any_specs(len(deps)),
        out_specs=[_bs((QB_B, GW), lambda kv, i: (i, kv)), _bs((T, HEAD_B), lambda kv, i: (0, kv)),
                   _bs((T, HEAD_B), lambda kv, i: (0, kv))],
        compiler_params=_params("parallel", "arbitrary"), name="b_bwd")(qn, kn, proj, o, lse, do, *deps)


def _local_step(x, target, small, get_w, put_g, deps=()):
    T, D = x.shape
    gs = {}

    (x1, h2), ffn1_saved = _ffn_fwd("ffn1", x, small["ffn1_norm"], get_w, deps, tail_ins=[small["mix_norm"]],
                                    tail_fn=lambda y, g: (y, _norm_fwd(y, g)), tail_outs=(F32, BF16))
    w_in = get_w("w_in", h2)
    nq = w_in.shape[2]
    tpq = nq // WIDTH_A

    def proj_tile(j, k):
        c = j * tpq + k
        return jnp.where(c < 3 * len(DILATIONS), (c % 3) * 3 + c // 3, c)

    proj = _mm("mix_in", (4, tpq),
               [(h2, _resident((T, D), lambda j, k: (0, 0)), w_in, _bs((None, D, WIDTH_A), lambda j, k: (j, 0, k)))],
               jax.ShapeDtypeStruct((T, IN_WIDTH), BF16), _bs((T, WIDTH_A), lambda j, k: (0, proj_tile(j, k))), NN)

    bias = _bias_tiles(small["rel_bias"])
    a_views = [_group_view(proj, grp, d) for grp, d in enumerate(DILATIONS)]
    a_outs, a_lses = [], []
    for grp, d in enumerate(DILATIONS):
        o, l = _dil_fwd(a_views[grp], bias[grp], d)
        a_outs.append(o)
        a_lses.append(l)
    o_a = _combine_fwd(a_outs, a_lses)

    cos, sin = _rope_tables(T)
    qn = _qk_fwd("b_qnorm", proj, B_Q, 8, small["q_norm"], cos, sin, out_scale=QK_SCALE_LOG2)
    kn = _qk_fwd("b_knorm", proj, B_K, 2, small["k_norm"], cos, sin)
    o_b, lse_b = _gqa_fwd(qn, kn, proj)

    wa, wb, wo = get_w("w_branch_a", o_b), get_w("w_branch_b", o_b), get_w("w_out", o_b)
    bg_a, bg_b = small["b_gate"][:, :D], small["b_gate"][:, D:]
    n_a = wa.shape[0]

    def merge_out(oa_ref, ob_ref, ga_ref, gb_ref, x1_ref, wa_ref, wb_ref, wo_ref, ba_ref, bb_ref, g2_ref,
                  ta_ref, tb_ref, mg_ref, x2_ref, hn_ref):
        oa = oa_ref[...]
        ta = jnp.concatenate([jnp.dot(oa, wa_ref[j], preferred_element_type=F32) for j in range(n_a)], axis=1)
        tb = jnp.dot(ob_ref[...], wb_ref[...], preferred_element_type=F32)
        sa = _sigmoid(ga_ref[...].astype(F32) + ba_ref[...])
        sb = _sigmoid(gb_ref[...].astype(F32) + bb_ref[...])
        merged = (sa * ta + sb * tb).astype(BF16)
        ta_ref[...], tb_ref[...], mg_ref[...] = ta.astype(BF16), tb.astype(BF16), merged
        y = x1_ref[...] + jnp.dot(merged, wo_ref[...], preferred_element_type=F32)
        x2_ref[...] = y
        hn_ref[...] = _norm_fwd(y, g2_ref[...]).astype(BF16)

    row = _bs((512, D), lambda i: (i, 0))
    gate_specs = [_bs((512, D), lambda i: (i, G_A // D)), _bs((512, D), lambda i: (i, G_B // D))]
    whole2, whole3 = (lambda i: (0, 0)), (lambda i: (0, 0, 0))
    vec = _bs((1, D), whole2)
    t_a, t_b, merged, x2, hn2 = pl.pallas_call(
        merge_out, out_shape=[jax.ShapeDtypeStruct((T, D), BF16)] * 3 + [jax.ShapeDtypeStruct((T, D), F32), jax.ShapeDtypeStruct((T, D), BF16)],
        grid=(T // 512,),
        in_specs=[_bs((512, WIDTH_A), lambda i: (i, 0)), row] + gate_specs + [row, _resident(wa.shape, whole3), _resident((D, D), whole2),
                                                                                _resident((D, D), whole2), vec, vec, vec],
        out_specs=[row] * 5, compiler_params=_params("parallel"), name="mix_merge_out")(
            o_a, o_b, proj, proj, x1, wa, wb, wo, bg_a, bg_b, small["ffn2_norm"])

    def head(xv, g, tv):
        r = _rstd(xv)
        xh = xv * r
        e = xh * g - tv
        dy = e * (1.0 / D)
        dxh = dy * g
        dx = r * (dxh - xh * jnp.mean(dxh * xh, axis=-1, keepdims=True))
        return dx, 0.5 * dx, _colsum(e * e) * (0.5 / D), _colsum(dy * xh)

    (dx3, dx3_half, loss_cols, g_final), ffn2_saved = _ffn_fwd(
        "ffn2", x2, small["ffn2_norm"], get_w, h=hn2, tail_ins=[small["final_norm"].reshape(1, D), target], tail_fn=head,
        tail_outs=(F32, BF16), tail_reds=(D, D))
    gs["final_norm"] = g_final.reshape(D)

    dx2, _, dmix, gs["ffn2_norm"] = _ffn_bwd("ffn2", x2, small["ffn2_norm"], get_w, put_g, ffn2_saved, dx3, dx3_half,
                                             also_bf16=True)
    g_out = _mm_wgrad("mix_bwd_dwout", merged, dmix, a_cols=D // 4, b_cols=None, tm=256, tn=512, J=4).reshape(D, D)

    def merge_out_bwd(dx_ref, ta_ref, tb_ref, ga_ref, gb_ref, wa_ref, wb_ref, wo_ref, ba_ref, bb_ref,
                      dta_ref, dtb_ref, dga_ref, dgb_ref, doa_ref, dob_ref, dba_ref, dbb_ref):
        dm = lax.dot_general(dx_ref[...], wo_ref[...], (NT, ((), ())), preferred_element_type=F32)
        ta, tb = ta_ref[...].astype(F32), tb_ref[...].astype(F32)
        sa = _sigmoid(ga_ref[...].astype(F32) + ba_ref[...])
        sb = _sigmoid(gb_ref[...].astype(F32) + bb_ref[...])
        dga, dgb = dm * ta * sa * (1.0 - sa), dm * tb * sb * (1.0 - sb)
        dta, dtb = (dm * sa).astype(BF16), (dm * sb).astype(BF16)
        dta_ref[...], dtb_ref[...] = dta, dtb
        dga_ref[...], dgb_ref[...] = dga.astype(BF16), dgb.astype(BF16)
        w = wa_ref.shape[2]
        doa = sum(lax.dot_general(dta[:, j * w:(j + 1) * w], wa_ref[j], (NT, ((), ())), preferred_element_type=F32) for j in range(n_a))
        doa_ref[...] = doa.astype(BF16)
        dob_ref[...] = lax.dot_general(dtb, wb_ref[...], (NT, ((), ())), preferred_element_type=F32).astype(BF16)

        @pl.when(pl.program_id(0) == 0)
        def _():
            dba_ref[...] = jnp.zeros_like(dba_ref)
            dbb_ref[...] = jnp.zeros_like(dbb_ref)
        dba_ref[...] += _colsum(dga)
        dbb_ref[...] += _colsum(dgb)

    rowb = _bs((256, D), lambda i: (i, 0))
    gate_specs = [_bs((256, D), lambda i: (i, G_A // D)), _bs((256, D), lambda i: (i, G_B // D))]
    dta, dtb, dga, dgb, do_a, do_b, dba, dbb = pl.pallas_call(
        merge_out_bwd,
        out_shape=[jax.ShapeDtypeStruct((T, D), BF16)] * 4 + [jax.ShapeDtypeStruct((T, WIDTH_A), BF16), jax.ShapeDtypeStruct((T, D), BF16)]
        + [jax.ShapeDtypeStruct((1, D), F32)] * 2,
        grid=(T // 256,),
        in_specs=[rowb, rowb, rowb] + gate_specs + [_resident(wa.shape, whole3), _resident((D, D), whole2), _resident((D, D), whole2), vec, vec],
        out_specs=[rowb] * 4 + [_bs((256, WIDTH_A), lambda i: (i, 0)), rowb, vec, vec],
        compiler_params=_params("arbitrary"), name="mix_merge_out_bwd")(dmix, t_a, t_b, proj, proj, wa, wb, wo, bg_a, bg_b)
    gs["b_gate"] = jnp.concatenate([dba, dbb], axis=1)

    g_a = _mm_wgrad("mix_bwd_dwa", o_a, dta, a_cols=None, b_cols=D // 4, tm=WIDTH_A, tn=256, J=4)
    g_b = _mm_wgrad("mix_bwd_dwb", o_b, dtb, a_cols=D // 4, b_cols=None, tm=256, tn=512, J=4).reshape(D, D)
    deps = put_g({"w_out": g_out, "w_branch_a": g_a, "w_branch_b": g_b})

    dqn, dkn, dv_b = _gqa_bwd(qn, kn, proj, o_b, lse_b, do_b, deps)
    dq_b, gs["q_norm"] = _qk_bwd("b_bwd_qnorm", dqn, proj, B_Q, 8, small["q_norm"], cos, sin, in_scale=HEAD_B ** -0.5)
    dk_b, gs["k_norm"] = _qk_bwd("b_bwd_knorm", dkn, proj, B_K, 2, small["k_norm"], cos, sin, in_scale=1.0 / LOG2_E)

    do_groups, c_groups = _combine_bwd(do_a, a_outs, a_lses)
    dqs, dks, dvs, dbs = [], [], [], []
    for grp, d in enumerate(DILATIONS):
        dq, dk, dv, db = _dil_bwd(a_views[grp], bias[grp], do_groups[grp], a_lses[grp], c_groups[grp], d)
        dqs.append(dq), dks.append(dk), dvs.append(dv), dbs.append(db)
    gs["rel_bias"] = _bias_grad(jnp.stack(dbs))

    dproj = _assemble_dproj([dqs, dks, dvs], dq_b, dk_b, dv_b, dga, dgb)
    nq = w_in.shape[2]
    g_in = _mm("mix_bwd_dwin", (4, tpq),
               [(h2, _resident((T, D), lambda j, k: (0, 0)), dproj, _bs((T, WIDTH_A), lambda j, k: (0, j * tpq + k)))],
               jax.ShapeDtypeStruct((4, D, nq), BF16), _bs((None, D, WIDTH_A), lambda j, k: (j, 0, k)), TN)
    deps = put_g({"w_in": g_in})
    dx1, dx1_half, gs["mix_norm"] = _dh_norm_bwd(
        "mix_bwd_dh", 256,
        [(dproj, _bs((256, nq), lambda i, j=j: (i, j)), w_in, _resident((None, D, nq), lambda i, j=j: (j, 0, 0))) for j in range(4)],
        NT, x1, small["mix_norm"], dx2, deps)

    dx0, _, gs["ffn1_norm"] = _ffn_bwd("ffn1", x, small["ffn1_norm"], get_w, put_g, ffn1_saved, dx1, dx1_half)
    return loss_cols, dx0, gs


def _position():
    return lax.axis_index("x"), lax.axis_index("y"), lax.axis_index("c")


def _any_specs(n):
    return [pl.BlockSpec(memory_space=pl.ANY)] * n


HBM_SPEC = pl.BlockSpec(memory_space=pltpu.HBM)
SEM_SPEC = pl.BlockSpec(memory_space=pltpu.SEMAPHORE)
DATAFLOW_EFFECT = pltpu.SideEffectType.DATAFLOW_SIDE_EFFECTING
N_PEER_CHIPS = 3
LANES = 128


def _quarter_copies(srcs, lands, send_sems, recv_sems, mode):
    x, y, c = _position()
    me = 2 * x + y
    peers = [(1 - x, y, c), (x, 1 - y, c), (1 - x, 1 - y, c)]
    copies = []
    for src, land, send, recv in zip(srcs, lands, send_sems, recv_sems):
        if mode == "sibling":
            copies.append(pltpu.make_async_remote_copy(src_ref=src, dst_ref=land, send_sem=send.at[0], recv_sem=recv.at[0],
                                                       device_id=(x, y, 1 - c), device_id_type=MESH))
            continue
        scatter = mode == "scatter"
        half = land.shape[1] // 2
        mine = land.at[me, pl.ds(c * half, half)]
        for p, (px, py, pc) in enumerate(peers):
            copies.append(pltpu.make_async_remote_copy(
                src_ref=src.at[2 * px + py] if scatter else mine, dst_ref=land.at[me] if scatter else mine,
                send_sem=send.at[p], recv_sem=recv.at[p], device_id=(px, py, pc), device_id_type=MESH))
    return copies


def _fill_from_sibling(name, stacks):
    n = len(stacks)

    def body(*refs):
        outs = refs[n:2 * n]
        send_sems, recv_sems = refs[2 * n:]
        x, y, c = _position()
        copies = []
        for i, ref in enumerate(outs):
            half = ref.shape[1] // 2
            rows = pl.ds(c * half, half)
            for p, k in enumerate((2 * (1 - x) + y, 2 * x + (1 - y), 2 * (1 - x) + (1 - y))):
                cp = pltpu.make_async_remote_copy(ref.at[k, rows], ref.at[k, rows], send_sems.at[3 * i + p], recv_sems.at[3 * i + p],
                                                  device_id=(x, y, 1 - c), device_id_type=MESH)
                cp.start()
                copies.append(cp)
        for cp in copies:
            cp.wait()

    return pl.pallas_call(
        body, out_shape=[jax.ShapeDtypeStruct(s.shape, s.dtype) for s in stacks],
        in_specs=_any_specs(n), out_specs=_any_specs(n), input_output_aliases={i: i for i in range(n)},
        scratch_shapes=[pltpu.SemaphoreType.DMA((N_PEER_CHIPS * n,)), pltpu.SemaphoreType.DMA((N_PEER_CHIPS * n,))],
        compiler_params=pltpu.CompilerParams(has_side_effects=True), name=name)(*stacks)


def _exchange_start(name, srcs, lands, mode):
    n = len(lands)
    arrays = list(lands) if srcs is None else list(srcs) + list(lands)
    k = len(arrays)

    def body(*refs):
        land_refs = refs[k - n:k]
        send_sems, recv_sems = refs[k:k + n], refs[k + n:k + 2 * n]
        token = refs[2 * k + 2 * n]
        for cp in _quarter_copies(refs[:n], land_refs, send_sems, recv_sems, mode):
            cp.start()
        token[...] = jnp.zeros_like(token)

    sem = pltpu.SemaphoreType.DMA((N_PEER_CHIPS,))
    out_shape = [sem] * (2 * n) + [pltpu.HBM(a.shape, a.dtype) for a in arrays] + [jax.ShapeDtypeStruct((8, LANES), F32)]
    res = pl.pallas_call(
        body, name=name, out_shape=out_shape, in_specs=[HBM_SPEC] * k,
        out_specs=[SEM_SPEC] * (2 * n) + [HBM_SPEC] * k + [pl.BlockSpec(memory_space=pltpu.VMEM)],
        input_output_aliases={i: 2 * n + i for i in range(k)},
        compiler_params=pltpu.CompilerParams(has_side_effects=DATAFLOW_EFFECT),
    )(*[pltpu.with_memory_space_constraint(a, pltpu.HBM) for a in arrays])
    thru = res[2 * n:2 * n + k]
    return res[:n], res[n:2 * n], (None if srcs is None else thru[:n]), thru[k - n:], res[2 * n + k]


def _exchange_wait(name, srcs, lands, send_sems, recv_sems, after, mode):
    n = len(lands)
    arrays = list(lands) if srcs is None else list(srcs) + list(lands)
    k = len(arrays)

    def body(*refs):
        sends, recvs = refs[k:k + n], refs[k + n:k + 2 * n]
        for cp in _quarter_copies(refs[:n], refs[k - n:k], sends, recvs, mode):
            cp.wait_send()
            cp.wait_recv()

    res = pl.pallas_call(
        body, name=name, out_shape=[pltpu.HBM(a.shape, a.dtype) for a in arrays],
        in_specs=[HBM_SPEC] * k + [SEM_SPEC] * (2 * n) + [pl.BlockSpec(memory_space=pl.ANY)],
        out_specs=[HBM_SPEC] * k, input_output_aliases={i: i for i in range(k)},
        compiler_params=pltpu.CompilerParams(has_side_effects=DATAFLOW_EFFECT),
    )(*arrays, *send_sems, *recv_sems, after)
    return (None if srcs is None else res[:n]), res[k - n:]


def _own_slot(name, src, from_stack=False):
    R, C = src.shape[-2:]
    rows = R // 2
    me = (2 * lax.axis_index("x") + lax.axis_index("y")).astype(jnp.int32).reshape(1)

    def body(me_ref, x_ref, o_ref):
        o_ref[...] = x_ref[...].astype(o_ref.dtype)

    in_spec = (pl.BlockSpec((None, rows, C), lambda i, me_ref: (me_ref[0], i, 0)) if from_stack
               else pl.BlockSpec((rows, C), lambda i, me_ref: (i, 0)))
    grid_spec = pltpu.PrefetchScalarGridSpec(
        num_scalar_prefetch=1, grid=(R // rows,), in_specs=[in_spec],
        out_specs=pl.BlockSpec((None, rows, C), lambda i, me_ref: (me_ref[0], i, 0)))
    return pl.pallas_call(body, out_shape=jax.ShapeDtypeStruct((4, R, C), BF16), grid_spec=grid_spec,
                          compiler_params=_params("parallel"), name=name)(me, src)


def _allreduce_small(buf):
    R, C = buf.shape
    flips = [(fx, fy, fc) for fx in (0, 1) for fy in (0, 1) for fc in (0, 1)][1:]

    def body(in_ref, out_ref, land_ref, send_sems, recv_sems):
        x, y, c = _position()
        me = 4 * x + 2 * y + c
        copies = []
        for k, (fx, fy, fc) in enumerate(flips):
            px, py, pc = (1 - x if fx else x), (1 - y if fy else y), (1 - c if fc else c)
            cp = pltpu.make_async_remote_copy(in_ref, land_ref.at[me], send_sems.at[k], recv_sems.at[k],
                                              device_id=(px, py, pc), device_id_type=MESH)
            cp.start()
            copies.append(cp)
        land_ref[me] = in_ref[...]
        for cp in copies:
            cp.wait()
        acc = land_ref[0]
        for k in range(1, 8):
            acc = acc + land_ref[k]
        out_ref[...] = acc

    return pl.pallas_call(
        body, out_shape=jax.ShapeDtypeStruct((R, C), F32),
        in_specs=[pl.BlockSpec(memory_space=pltpu.VMEM)], out_specs=pl.BlockSpec(memory_space=pltpu.VMEM),
        scratch_shapes=[pltpu.VMEM((8, R, C), F32), pltpu.SemaphoreType.DMA((7,)), pltpu.SemaphoreType.DMA((7,))],
        compiler_params=pltpu.CompilerParams(has_side_effects=True), name="allreduce_small")(buf)


def _adamw_math(w, g, m, v):
    m2 = ADAM_B1 * m + (1.0 - ADAM_B1) * g
    v2 = ADAM_B2 * v + (1.0 - ADAM_B2) * (g * g)
    m_hat = m2 / (1.0 - ADAM_B1 ** ADAM_STEP)
    v_hat = v2 / (1.0 - ADAM_B2 ** ADAM_STEP)
    delta = -ADAM_LR * (m_hat / (jnp.sqrt(v_hat) + ADAM_EPS) + ADAM_WD * w)
    return delta, m2, v2


def _adamw_big(name, w, m, v, mine, theirs):
    R, C = w.shape
    rows = 128 if R % 128 == 0 else 64
    nrb = R // rows

    def four(a, b, c, d):
        return ((a.astype(F32) + b.astype(F32)) + c.astype(F32)) + d.astype(F32)

    def fn(wv, mv, vv, *parts):
        g = four(*parts[:4]) + four(*parts[4:])
        return (g,) + _adamw_math(wv, g, mv, vv)

    slots = [_tiled(s.reshape(4 * R, C), None, 0, k * nrb) for s in (mine, theirs) for k in range(4)]
    return _ew(name, fn, [_tiled(w), _tiled(m), _tiled(v)] + slots, [(F32, C)] * 4, n_rows=R, rows=rows)


BIG = ("ffn1_w1", "ffn1_w3", "ffn1_w2", "w_in", "w_branch_a", "w_branch_b", "w_out", "ffn2_w1", "ffn2_w3", "ffn2_w2")
SMALL = ("ffn1_norm", "mix_norm", "b_gate", "q_norm", "k_norm", "rel_bias", "ffn2_norm", "final_norm")
ORDER = ("ffn1_norm", "ffn1_w1", "ffn1_w3", "ffn1_w2", "mix_norm", "w_in", "b_gate", "q_norm", "k_norm", "rel_bias",
         "w_branch_a", "w_branch_b", "w_out", "ffn2_norm", "ffn2_w1", "ffn2_w3", "ffn2_w2", "final_norm")
TRANSPOSED = ("ffn1_w1", "ffn1_w3", "ffn2_w1", "ffn2_w3")
SIBLING_LAG = 2
GATHER_GROUPS = (("ffn1_w1", "ffn1_w3"), ("ffn1_w2",), ("w_in",), ("w_branch_a", "w_branch_b", "w_out"),
                 ("ffn2_w1", "ffn2_w3", "ffn2_w2"))


def _pack_small(d):
    rows = []
    for n in SMALL:
        flat = d[n].reshape(-1)
        pad = (-flat.shape[0]) % LANES
        rows.append(jnp.pad(flat, (0, pad)).reshape(-1, LANES))
    buf = jnp.concatenate(rows, axis=0)
    return jnp.pad(buf, ((0, (-buf.shape[0]) % 8), (0, 0)))


def _unpack_small(buf, like):
    out, r = {}, 0
    for n in SMALL:
        size = like[n].size
        nr = -(-size // LANES)
        out[n] = buf[r:r + nr].reshape(-1)[:size].reshape(like[n].shape)
        r += nr
    return out


def kernel(x, ffn1_norm, ffn1_w1, ffn1_w3, ffn1_w2, mix_norm, w_in, b_gate, q_norm, k_norm, rel_bias, w_branch_a, w_branch_b, w_out, ffn2_norm, ffn2_w1, ffn2_w3, ffn2_w2, final_norm, loss_target, m_ffn1_norm, m_ffn1_w1, m_ffn1_w3, m_ffn1_w2, m_mix_norm, m_w_in, m_b_gate, m_q_norm, m_k_norm, m_rel_bias, m_w_branch_a, m_w_branch_b, m_w_out, m_ffn2_norm, m_ffn2_w1, m_ffn2_w3, m_ffn2_w2, m_final_norm, v_ffn1_norm, v_ffn1_w1, v_ffn1_w3, v_ffn1_w2, v_mix_norm, v_w_in, v_b_gate, v_q_norm, v_k_norm, v_rel_bias, v_w_branch_a, v_w_branch_b, v_w_out, v_ffn2_norm, v_ffn2_w1, v_ffn2_w3, v_ffn2_w2, v_final_norm):
    given = dict(locals())
    w = {n: given[n] for n in ORDER}
    m = {n: given["m_" + n] for n in ORDER}
    v = {n: given["v_" + n] for n in ORDER}
    T, D = x.shape[1], x.shape[2]

    def stored(a, n):
        a = a.reshape(a.shape[1:])
        return a.T if n in TRANSPOSED else a

    def returned(a, n):
        return (a.T if n in TRANSPOSED else a).reshape(w[n].shape)

    quarter = {n: stored(w[n], n) for n in BIG}
    send, recv, _, land_thru, token = _exchange_start(
        "gather_start", None, [_own_slot(f"own_{n}", quarter[n]) for n in BIG], "gather")
    index = {n: i for i, n in enumerate(BIG)}
    ready = {}

    def get_w(name, after):
        if name not in ready:
            group = next(g for g in GATHER_GROUPS if name in g)
            ids = [index[n] for n in group]
            _, stacks = _exchange_wait("gather_wait_" + group[0], None, [land_thru[i] for i in ids],
                                       [send[i] for i in ids], [recv[i] for i in ids], after, "gather")
            stacks = _fill_from_sibling("gather_fill_" + group[0], stacks)
            for n, st in zip(group, stacks):
                ready[n] = st.reshape(D, D) if n in ("w_branch_b", "w_out") else st
        return ready[name]

    scattered, forwarded = [], []

    def forward_oldest(after):
        names, s_sem, r_sem, srcs, lands = scattered.pop(0)
        _, landed = _exchange_wait("scatter_wait_" + names[0], srcs, lands, s_sem, r_sem, after, "scatter")
        started = _exchange_start("sibling_start_" + names[0], landed, [lax.empty(a.shape, a.dtype) for a in landed], "sibling")
        forwarded.append((names,) + tuple(started[:4]))
        return started[4]

    def put_g(grads):
        names = list(grads)
        stacks = [grads[n].reshape((4,) + quarter[n].shape) for n in names]
        lands = [_own_slot(f"own_grad_{n}", s, from_stack=True) for n, s in zip(names, stacks)]
        started = _exchange_start("scatter_start_" + names[0], stacks, lands, "scatter")
        scattered.append((names,) + tuple(started[:4]))
        tokens = [started[4]]
        if len(scattered) > SIBLING_LAG:
            tokens.append(forward_oldest(started[4]))
        return tokens

    small = {n: w[n] for n in SMALL}
    loss_cols, grad_x, gs = _local_step(x.reshape(T, D), loss_target.reshape(T, D), small, get_w, put_g, deps=[token])

    after = grad_x
    while scattered:
        after = forward_oldest(after)
    grads, deltas, new_m, new_v = {}, {}, {}, {}
    for names, s_sem, r_sem, srcs, lands in forwarded:
        mine, theirs = _exchange_wait("sibling_wait_" + names[0], srcs, lands, s_sem, r_sem, after, "sibling")
        for n, a, b in zip(names, mine, theirs):
            res = _adamw_big(f"adamw_{n}", quarter[n], stored(m[n], n), stored(v[n], n), a, b)
            grads[n], deltas[n], new_m[n], new_v[n] = [returned(r, n) for r in res]

    gs = {n: gs[n].reshape(w[n].shape) for n in SMALL}
    packed_g = _pack_small(gs)
    n_small = packed_g.shape[0]
    summed = _allreduce_small(jnp.concatenate([packed_g, loss_cols.reshape(-1, LANES)], axis=0))
    g_small, loss = summed[:n_small], jnp.sum(summed[n_small:])
    packed = [_pack_small({n: d[n] for n in SMALL}) for d in (w, m, v)]
    R = g_small.shape[0]
    res = _ew("adamw_small", lambda wv, mv, vv, g: (g,) + _adamw_math(wv, g, mv, vv),
              [_tiled(packed[0]), _tiled(packed[1]), _tiled(packed[2]), _tiled(g_small)], [(F32, LANES)] * 4, n_rows=R, rows=R)
    for d, buf in zip((grads, deltas, new_m, new_v), res):
        d.update(_unpack_small(buf, w))

    return (loss, grad_x.reshape(x.shape), *[grads[n] for n in ORDER], *[deltas[n] for n in ORDER],
            *[new_m[n] for n in ORDER], *[new_v[n] for n in ORDER])
```

```python
import functools
import math

import numpy as np
import jax
import jax.numpy as jnp
from jax import lax
from jax.experimental import pallas as pl
from jax.experimental.pallas import tpu as pltpu

F32 = jnp.float32
BF16 = jnp.bfloat16
MESH = pl.DeviceIdType.MESH

NEG_INF = -1e30
EPS = 1e-6
GRID_W = 64
ROPE_THETA = 10000.0
DILATIONS = (1, 4, 16)
BAND_HALF = 64
HEAD_A = 64
HEADS_A = 8
WIDTH_A = HEADS_A * HEAD_A
HEAD_B = 128
LOG2_E = math.log2(math.e)
QK_SCALE_LOG2 = HEAD_B ** -0.5 * LOG2_E
N_BUCKETS = 32
MAX_DISTANCE = 1024
ADAM_LR, ADAM_B1, ADAM_B2, ADAM_EPS, ADAM_WD, ADAM_STEP = 0.001, 0.9, 0.999, 1e-08, 0.01, 10

B_Q, B_K, B_V = 4608, 5632, 5888
G_A, G_B = 6144, 7168
IN_WIDTH = 8192

VMEM_LIMIT_BYTES = 56 * 1024 * 1024
QB_A = 128
QB_B = 256


def _params(*sem):
    return pltpu.CompilerParams(dimension_semantics=sem, vmem_limit_bytes=VMEM_LIMIT_BYTES)


def _bs(shape, fn):
    return pl.BlockSpec(shape, fn)


def _resident(shape, fn):
    return pl.BlockSpec(shape, fn, pipeline_mode=pl.Buffered(1))


def _mm(name, grid, pairs, out_shape, out_spec, dims, *, extras=(), epilogue=None, deps=(), reds=()):
    n_pairs, n_extra, n_deps = len(pairs), len(extras), len(deps)
    operands = [p[0] for p in pairs] + [p[2] for p in pairs] + [e[0] for e in extras] + list(deps)
    in_specs = [p[1] for p in pairs] + [p[3] for p in pairs] + [e[1] for e in extras] + _any_specs(n_deps)
    single = not isinstance(out_shape, (list, tuple))
    out_shapes = [out_shape] if single else list(out_shape)
    out_specs = [out_spec] if single else list(out_spec)
    n_out = len(out_shapes)
    out_shapes += [jax.ShapeDtypeStruct((1, w), F32) for w in reds]
    out_specs += [_bs((1, w), lambda *_: (0, 0)) for w in reds]

    def body(*refs):
        a_refs, b_refs = refs[:n_pairs], refs[n_pairs:2 * n_pairs]
        e_refs = refs[2 * n_pairs:2 * n_pairs + n_extra]
        o_refs = refs[2 * n_pairs + n_extra + n_deps:]
        acc = None
        for a_ref, b_ref in zip(a_refs, b_refs):
            t = lax.dot_general(a_ref[...], b_ref[...], (dims, ((), ())), preferred_element_type=F32)
            acc = t if acc is None else acc + t
        vals = acc if epilogue is None else epilogue(acc, *[e[...] for e in e_refs])
        if not isinstance(vals, (list, tuple)):
            vals = (vals,)
        for o_ref, v in zip(o_refs[:n_out], vals[:n_out]):
            o_ref[...] = v.astype(o_ref.dtype)
        if reds:
            first = functools.reduce(jnp.logical_and, [pl.program_id(ax) == 0 for ax in range(len(grid))])
            for r_ref, v in zip(o_refs[n_out:], vals[n_out:]):
                @pl.when(first)
                def _(r_ref=r_ref):
                    r_ref[...] = jnp.zeros_like(r_ref)
                r_ref[...] += v

    sem = ["arbitrary" if reds else "parallel"] * len(grid)
    res = pl.pallas_call(
        body, out_shape=out_shapes, grid=grid, in_specs=in_specs, out_specs=out_specs,
        compiler_params=_params(*sem), name=name)(*operands)
    return res[0] if (single and not reds) else res


NN = ((1,), (0,))
NT = ((1,), (1,))
TN = ((0,), (0,))


def _mm_cols(name, a, w, *, tm, tn, out_dtype, cat, extras=(), epilogue=None):
    M, K = a.shape
    J, _, n = w.shape
    tn = min(tn, n)
    nb = n // tn
    if cat:
        shape, spec = (M, J * n), _bs((tm, tn), lambda j, i, k: (i, j * nb + k))
    else:
        shape, spec = (J, M, n), _bs((None, tm, tn), lambda j, i, k: (j, i, k))
    ex = [(e, _bs((tm, tn), lambda j, i, k: (i, j * nb + k))) for e in extras]
    return _mm(name, (J, M // tm, nb),
               [(a, _bs((tm, K), lambda j, i, k: (i, 0)), w, _bs((None, K, tn), lambda j, i, k: (j, 0, k)))],
               jax.ShapeDtypeStruct(shape, out_dtype), spec, NN, extras=ex, epilogue=epilogue)


def _mm_rows_t(name, a, w, *, tm, out_dtype):
    M, N = a.shape
    J, f, _ = w.shape
    return _mm(name, (J, M // tm),
               [(a, _bs((tm, N), lambda j, i: (i, 0)), w, _bs((None, f, N), lambda j, i: (j, 0, 0)))],
               jax.ShapeDtypeStruct((J, M, f), out_dtype), _bs((None, tm, f), lambda j, i: (j, i, 0)), NT)


def _mm_wgrad(name, a, b, *, a_cols, b_cols, tm, tn, J):
    def pick(arr, cols, t):
        if arr.ndim == 3:
            T, c = arr.shape[1], arr.shape[2]
            t = min(t, c)
            return T, c, t, (lambda sel: _bs((None, T, t), lambda j, i, k: (j, 0, sel(i, k))))
        T = arr.shape[0]
        c = arr.shape[1] if cols is None else cols
        t = min(t, c)
        per = c // t
        if cols is None:
            if per == 1:
                return T, c, t, (lambda sel: _resident((T, t), lambda j, i, k: (0, 0)))
            return T, c, t, (lambda sel: _bs((T, t), lambda j, i, k: (0, sel(i, k))))
        return T, c, t, (lambda sel: _bs((T, t), lambda j, i, k: (0, j * per + sel(i, k))))
    _, ca, tm, mk_a = pick(a, a_cols, tm)
    _, cb, tn, mk_b = pick(b, b_cols, tn)
    return _mm(name, (J, ca // tm, cb // tn),
               [(a, mk_a(lambda i, k: i), b, mk_b(lambda i, k: k))],
               jax.ShapeDtypeStruct((J, ca, cb), BF16), _bs((None, tm, tn), lambda j, i, k: (j, i, k)), TN)


def _tiled(arr, width=None, col=0, rowblk=0):
    return ("t", arr, arr.shape[1] if width is None else width, col, rowblk)


def _table(arr):
    return ("f", arr)


def _whole(arr):
    return ("w", arr)


def _ew(name, fn, ins, outs, *, n_rows, rows, reds=(), ncols=1, deps=()):
    nrb = n_rows // rows
    n_deps = len(deps)
    operands, in_specs = [], []
    for spec in ins:
        if spec[0] == "t":
            _, arr, width, col, rowblk = spec
            step = 1 if ncols > 1 else 0
            in_specs.append(_bs((rows, width), lambda c, i, col=col, rowblk=rowblk, step=step: (rowblk + i, col + c * step)))
        elif spec[0] == "f":
            arr = spec[1]
            in_specs.append(_bs((rows, arr.shape[1]), lambda c, i: (i, 0)))
        else:
            arr = spec[1]
            nd = arr.ndim
            if nd == 3:
                in_specs.append(_bs((None,) + arr.shape[1:], lambda c, i: (c, 0, 0)))
            else:
                in_specs.append(_bs(arr.shape, lambda c, i, nd=nd: (0,) * nd))
        operands.append(arr)
    out_shapes = [jax.ShapeDtypeStruct((n_rows, ncols * w), dt) for dt, w in outs]
    out_specs = [_bs((rows, w), lambda c, i: (i, c)) for _, w in outs]
    out_shapes += [jax.ShapeDtypeStruct((ncols, 1, w), F32) for w in reds]
    out_specs += [_bs((None, 1, w), lambda c, i: (c, 0, 0)) for w in reds]
    n_in, n_out, n_red = len(ins), len(outs), len(reds)
    operands += list(deps)
    in_specs += _any_specs(n_deps)

    def body(*refs):
        vals = fn(*[r[...] for r in refs[:n_in]])
        if not isinstance(vals, (tuple, list)):
            vals = (vals,)
        o_refs = refs[n_in + n_deps:]
        for o_ref, v in zip(o_refs[:n_out], vals[:n_out]):
            o_ref[...] = v.astype(o_ref.dtype)
        if n_red:
            i = pl.program_id(1)
            for r_ref, v in zip(o_refs[n_out:], vals[n_out:]):
                @pl.when(i == 0)
                def _(r_ref=r_ref):
                    r_ref[...] = jnp.zeros_like(r_ref)
                r_ref[...] += v

    res = pl.pallas_call(
        body, out_shape=out_shapes, grid=(ncols, nrb), in_specs=in_specs, out_specs=out_specs,
        compiler_params=_params("parallel", "arbitrary" if n_red else "parallel"), name=name)(*operands)
    return res


def _colsum(v):
    return jnp.sum(v, axis=0, keepdims=True)


def _rstd(x):
    return lax.rsqrt(jnp.mean(x * x, axis=-1, keepdims=True) + EPS)


def _sigmoid(x):
    return 1.0 / (1.0 + jnp.exp(-x))


def _norm_fwd(x, g):
    return x * _rstd(x) * g


def _norm_bwd(x, g, dy):
    r = _rstd(x)
    xh = x * r
    dxh = dy * g
    dx = r * (dxh - xh * jnp.mean(dxh * xh, axis=-1, keepdims=True))
    return dx, dy * xh


def _row_spec(arr, rows):
    if arr.shape[0] == 1:
        return _bs(arr.shape, lambda i: (0, 0))
    return _bs((rows, arr.shape[1]), lambda i: (i, 0))


def _ffn_fwd(tag, x, gain, get_w, deps=(), *, h=None, tail_ins=(), tail_fn=None, tail_outs=(F32,), tail_reds=()):
    T, D = x.shape
    if h is None:
        (h,) = _ew(f"{tag}_norm", lambda xv, g: _norm_fwd(xv, g), [_tiled(x), _whole(gain)], [(BF16, D)], n_rows=T, rows=512,
                   deps=deps)
    w1, w3 = get_w(f"{tag}_w1", h), get_w(f"{tag}_w3", h)
    J, f, _ = w1.shape
    tm = 1024

    def up(h_ref, w1_ref, w3_ref, u_ref, g_ref, a_ref):
        hv = h_ref[...]
        u = lax.dot_general(hv, w1_ref[...], (NT, ((), ())), preferred_element_type=F32)
        g = lax.dot_general(hv, w3_ref[...], (NT, ((), ())), preferred_element_type=F32)
        u_ref[...] = u.astype(BF16)
        g_ref[...] = g.astype(BF16)
        a_ref[...] = (u * _sigmoid(u) * g).astype(BF16)

    slab = _bs((None, tm, f), lambda j, i: (j, i, 0))
    w_spec = _bs((None, f, D), lambda j, i: (j, 0, 0))
    u, g, a = pl.pallas_call(
        up, out_shape=[jax.ShapeDtypeStruct((J, T, f), BF16)] * 3, grid=(J, T // tm),
        in_specs=[_bs((tm, D), lambda j, i: (i, 0)), w_spec, w_spec], out_specs=[slab] * 3,
        compiler_params=_params("parallel", "parallel"), name=f"{tag}_up")(h, w1, w3)
    w2 = get_w(f"{tag}_w2", a)
    def tail(acc, xv, *rest):
        y = xv + 0.5 * acc
        return y if tail_fn is None else tail_fn(y, *rest)

    row = _bs((512, D), lambda i: (i, 0))
    res = _mm(f"{tag}_down", (T // 512,),
              [(a, _bs((None, 512, f), lambda i, j=j: (j, i, 0)), w2, _resident((None, f, D), lambda i, j=j: (j, 0, 0)))
               for j in range(J)],
              [jax.ShapeDtypeStruct((T, D), dt) for dt in tail_outs], [row] * len(tail_outs), NN,
              extras=[(x, row)] + [(t, _row_spec(t, 512)) for t in tail_ins], epilogue=tail, reds=tail_reds)
    return res, (h, u, g, a)


def _dh_norm_bwd(name, rows, pairs, dims, x, gain, dres, deps, also_bf16=False):
    T, D = x.shape

    def epilogue(dh, xv, gv, dr):
        dx, dgr = _norm_bwd(xv, gv, dh)
        dx = dx + dr
        return (dx, 0.5 * dx) + ((dx,) if also_bf16 else ()) + (_colsum(dgr),)

    dts = [F32, BF16] + ([BF16] if also_bf16 else [])
    row = _bs((rows, D), lambda i: (i, 0))
    return _mm(name, (T // rows,), pairs, [jax.ShapeDtypeStruct((T, D), dt) for dt in dts], [row] * len(dts), dims,
               extras=[(x, row), (gain, _row_spec(gain, rows)), (dres, row)], epilogue=epilogue, deps=deps, reds=(D,))


def _ffn_bwd(tag, x, gain, get_w, put_g, saved, dy, dy_half, also_bf16=False):
    h, u, g, a = saved
    T, D = x.shape
    w1, w3, w2 = [get_w(f"{tag}_{n}", dy_half) for n in ("w1", "w3", "w2")]
    J, f, _ = w1.shape
    dw2 = _mm_wgrad(f"{tag}_bwd_dw2", a, dy_half, a_cols=None, b_cols=None, tm=f, tn=D, J=J)
    deps = put_g({f"{tag}_w2": dw2})
    tm = 1024

    def up_bwd(dy_ref, w2_ref, u_ref, g_ref, *rest):
        du_ref, dg_ref = rest[-2:]
        da = lax.dot_general(dy_ref[...], w2_ref[...], (NT, ((), ())), preferred_element_type=F32)
        uv, gv = u_ref[...].astype(F32), g_ref[...].astype(F32)
        s = _sigmoid(uv)
        du_ref[...] = (da * gv * (s * (1.0 + uv * (1.0 - s)))).astype(BF16)
        dg_ref[...] = (da * (uv * s)).astype(BF16)

    slab = _bs((None, tm, f), lambda j, i: (j, i, 0))
    du, dg = pl.pallas_call(
        up_bwd, out_shape=[jax.ShapeDtypeStruct((J, T, f), BF16)] * 2, grid=(J, T // tm),
        in_specs=[_bs((tm, D), lambda j, i: (i, 0)), _bs((None, f, D), lambda j, i: (j, 0, 0)), slab, slab] + _any_specs(len(deps)),
        out_specs=[slab] * 2, compiler_params=_params("parallel", "parallel"), name=f"{tag}_bwd_up")(dy_half, w2, u, g, *deps)
    dw1 = _mm_wgrad(f"{tag}_bwd_dw1", du, h, a_cols=None, b_cols=None, tm=f, tn=D, J=J)
    dw3 = _mm_wgrad(f"{tag}_bwd_dw3", dg, h, a_cols=None, b_cols=None, tm=f, tn=D, J=J)
    deps = deps + put_g({f"{tag}_w1": dw1, f"{tag}_w3": dw3})
    pairs = []
    for j in range(J):
        a_spec = _bs((None, 512, f), lambda i, j=j: (j, i, 0))
        w_spec = _resident((None, f, D), lambda i, j=j: (j, 0, 0))
        pairs += [(du, a_spec, w1, w_spec), (dg, a_spec, w3, w_spec)]
    return _dh_norm_bwd(f"{tag}_bwd_dh", 512, pairs, NN, x, gain, dy, deps, also_bf16)


def _t5_bucket(rel):
    n = N_BUCKETS // 2
    max_exact = n // 2
    ret = jnp.where(rel > 0, n, 0)
    a = jnp.abs(rel)
    af = jnp.maximum(a, 1).astype(F32)
    large = max_exact + (jnp.log(af / max_exact) / math.log(MAX_DISTANCE / max_exact) * (n - max_exact)).astype(jnp.int32)
    large = jnp.minimum(large, n - 1)
    return ret + jnp.where(a < max_exact, a, large)


WIN_A = QB_A + 2 * BAND_HALF
WIN_SHIFTS = (0, BAND_HALF, 2 * BAND_HALF)


def _window_variant(n, nblk):
    return jnp.where(n == 0, 0, jnp.where(n == nblk - 1, 2, 1))


def _window_start(n, nblk):
    return pl.multiple_of(jnp.clip(n * QB_A - BAND_HALF, 0, nblk * QB_A - WIN_A), BAND_HALF)


def _band_steps(xp=jnp):
    qi = xp.arange(QB_A, dtype=xp.int32)[None, :, None]
    kj = xp.arange(WIN_A, dtype=xp.int32)[None, None, :]
    return kj - qi - xp.asarray(WIN_SHIFTS, dtype=xp.int32)[:, None, None]


def _bias_tiles(rel_bias):
    wide = QB_A + 2 * WIN_SHIFTS[-1]
    qi = jnp.arange(QB_A, dtype=jnp.int32)[:, None]
    steps = jnp.arange(wide, dtype=jnp.int32)[None, :] - WIN_SHIFTS[-1] - qi
    buckets = jnp.stack([_t5_bucket(steps * d) for d in DILATIONS])
    inband = (jnp.abs(steps) <= BAND_HALF).astype(jnp.int32)
    n_heads = rel_bias.shape[1]

    def body(tab_ref, b_ref, m_ref, o_ref):
        hd = pl.program_id(0)
        bkt = b_ref[...]
        acc = jnp.zeros(bkt.shape, F32)
        for b in range(N_BUCKETS):
            acc = jnp.where(bkt == b, tab_ref[b, hd], acc)
        o_ref[...] = jnp.where(m_ref[...] > 0, acc, NEG_INF)

    base = pl.pallas_call(
        body, out_shape=jax.ShapeDtypeStruct((n_heads, QB_A, wide), F32), grid=(n_heads,),
        in_specs=[pl.BlockSpec(memory_space=pltpu.SMEM),
                  _bs((None, QB_A, wide), lambda hd: (hd // HEADS_A, 0, 0)),
                  _bs((QB_A, wide), lambda hd: (0, 0))],
        out_specs=_bs((None, QB_A, wide), lambda hd: (hd, 0, 0)),
        compiler_params=_params("parallel"), name="a_bias_tiles")(rel_bias, buckets, inband)
    base = base.reshape(len(DILATIONS), HEADS_A, QB_A, wide)
    return jnp.stack([base[..., WIN_SHIFTS[-1] - s:WIN_SHIFTS[-1] - s + WIN_A] for s in WIN_SHIFTS], axis=1)


def _bias_grad(dbias):
    steps = _band_steps(np)
    inband = np.abs(steps) <= BAND_HALF
    present = []
    for d in DILATIONS:
        rel = steps * d
        a = np.abs(rel)
        large = 8 + (np.log(np.maximum(a, 1) / 8.0) / math.log(MAX_DISTANCE / 8.0) * 8).astype(np.int64)
        bk = np.where(rel > 0, 16, 0) + np.where(a < 8, a, np.minimum(large, 15))
        present.append([sorted(set(bk[v][inband[v]].tolist())) for v in range(3)])
    buckets = jnp.stack([_t5_bucket(_band_steps() * d) for d in DILATIONS])
    n_heads = len(DILATIONS) * HEADS_A

    def body(b_ref, d_ref, o_ref):
        row = lax.broadcasted_iota(jnp.int32, (N_BUCKETS, n_heads), 0)
        col = lax.broadcasted_iota(jnp.int32, (N_BUCKETS, n_heads), 1)
        out = jnp.zeros((N_BUCKETS, n_heads), F32)
        for grp in range(len(DILATIONS)):
            for hh in range(HEADS_A):
                hd = grp * HEADS_A + hh
                for b in sorted(set(sum(present[grp], []))):
                    tot = jnp.zeros((), F32)
                    for v in range(3):
                        if b in present[grp][v]:
                            tot = tot + jnp.sum(jnp.where(b_ref[grp, v] == b, d_ref[grp, v, hh], 0.0))
                    out = jnp.where((row == b) & (col == hd), tot, out)
        o_ref[...] = out

    return pl.pallas_call(
        body, out_shape=jax.ShapeDtypeStruct((N_BUCKETS, n_heads), F32),
        compiler_params=pltpu.CompilerParams(vmem_limit_bytes=VMEM_LIMIT_BYTES), name="a_bias_grad")(buckets, dbias)


def _lane_is_second_head(shape):
    return lax.broadcasted_iota(jnp.int32, shape, len(shape) - 1) >= HEAD_A


VIEW_ROWS = 512


def _view_chunks():
    return [pltpu.VMEM((VIEW_ROWS, LANES), F32)] * (WIDTH_A // LANES)


def _rows_to_view(x_ref, col, o_ref, ocol, d, chunks):
    n = VIEW_ROWS // d
    for c, scr in enumerate(chunks):
        scr[...] = x_ref[:, col + c * LANES:col + (c + 1) * LANES].astype(F32)
        for r in range(d):
            at = ocol + r * WIDTH_A + c * LANES
            o_ref[:, at:at + LANES] = scr[pl.ds(r, n, stride=d), :].astype(o_ref.dtype)


def _view_to_rows(v_ref, o_ref, col, d, chunks):
    n = VIEW_ROWS // d
    for c, scr in enumerate(chunks):
        if d == 1:
            o_ref[:, col + c * LANES:col + (c + 1) * LANES] = v_ref[:, c * LANES:(c + 1) * LANES].astype(o_ref.dtype)
            continue
        for r in range(d):
            scr[pl.ds(r, n, stride=d), :] = v_ref[:, r * WIDTH_A + c * LANES:r * WIDTH_A + (c + 1) * LANES].astype(F32)
        o_ref[:, col + c * LANES:col + (c + 1) * LANES] = scr[...].astype(o_ref.dtype)


def _group_view(proj, grp, d):
    T = proj.shape[0]
    if d == 1:
        return proj, (lambda part, r: grp * 3 + part)

    def body(x_ref, o_ref, *chunks):
        for part in range(3):
            _rows_to_view(x_ref, part * WIDTH_A, o_ref, part * d * WIDTH_A, d, chunks)

    view = pl.pallas_call(
        body, out_shape=jax.ShapeDtypeStruct((T // d, 3 * d * WIDTH_A), proj.dtype), grid=(T // VIEW_ROWS,),
        in_specs=[_bs((VIEW_ROWS, 3 * WIDTH_A), lambda i: (i, grp))],
        out_specs=_bs((VIEW_ROWS // d, 3 * d * WIDTH_A), lambda i: (i, 0)),
        scratch_shapes=_view_chunks(), compiler_params=_params("parallel"), name=f"a_view_d{d}")(proj)
    return view, (lambda part, r: part * d + r)


def _stack_heads(v2, second):
    zero = jnp.zeros_like(v2)
    return jnp.concatenate([jnp.where(second, zero, v2), jnp.where(second, v2, zero)], axis=0)


def _unstack_heads(v, second):
    return jnp.where(second, v[QB_A:], v[:QB_A])


def _dil_fwd(view, bias, d):
    pv, colblk = view
    L = pv.shape[0]
    nblk = L // QB_A
    W2 = 2 * HEAD_A
    scale = HEAD_A ** -0.5

    def body(q_ref, k_ref, v_ref, b_ref, o_ref, l_ref):
        win = pl.ds(_window_start(pl.program_id(1), nblk), WIN_A)
        second = _lane_is_second_head((QB_A, W2))
        pairs = range(HEADS_A // 2)
        cols = [slice(hp * W2, (hp + 1) * W2) for hp in pairs]
        s = [lax.dot_general(_stack_heads(q_ref[:, cols[hp]], second), k_ref[win, cols[hp]], (NT, ((), ())),
                             preferred_element_type=F32) * scale + b_ref[2 * hp:2 * hp + 2].reshape(2 * QB_A, WIN_A)
             for hp in pairs]
        m = [jnp.max(x, axis=-1, keepdims=True) for x in s]
        p = [jnp.exp(x - mx) for x, mx in zip(s, m)]
        l = [jnp.sum(x, axis=-1, keepdims=True) for x in p]
        res = [jnp.dot(p[hp].astype(BF16), v_ref[win, cols[hp]], preferred_element_type=F32) / l[hp] for hp in pairs]
        o_ref[...] = jnp.concatenate([_unstack_heads(x, second) for x in res], axis=1).astype(o_ref.dtype)
        l_ref[...] = jnp.concatenate([_unstack_heads(jnp.broadcast_to(mx + jnp.log(lx), (2 * QB_A, W2)), second)
                                      for mx, lx in zip(m, l)], axis=1)

    in_specs = [_bs((QB_A, WIDTH_A), lambda r, n: (n, colblk(0, r))),
                _bs((L, WIDTH_A), lambda r, n: (0, colblk(1, r))), _bs((L, WIDTH_A), lambda r, n: (0, colblk(2, r))),
                _bs((None, HEADS_A, QB_A, WIN_A), lambda r, n: (_window_variant(n, nblk), 0, 0, 0))]
    o, lse = pl.pallas_call(
        body, out_shape=[jax.ShapeDtypeStruct((L, d * WIDTH_A), BF16), jax.ShapeDtypeStruct((L, d * WIDTH_A), F32)],
        grid=(d, nblk), in_specs=in_specs,
        out_specs=[_bs((QB_A, WIDTH_A), lambda r, n: (n, r)), _bs((QB_A, WIDTH_A), lambda r, n: (n, r))],
        compiler_params=_params("parallel", "parallel"), name=f"a_fwd_d{d}")(pv, pv, pv, bias)
    return o, lse


def _dil_bwd(view_qkv, bias, do, lse, cterm, d):
    pv, colblk = view_qkv
    L = pv.shape[0]
    nblk = L // QB_A
    W2 = 2 * HEAD_A
    PPS = 4
    WS = PPS * W2
    ob = WIDTH_A // WS
    scale = HEAD_A ** -0.5

    def body(q_ref, k_ref, v_ref, do_ref, l_ref, c_ref, b_ref, dq_ref, dk_ref, dv_ref, db_ref):
        r, n = pl.program_id(1), pl.program_id(2)

        @pl.when(n == 0)
        def _():
            dk_ref[...] = jnp.zeros_like(dk_ref)
            dv_ref[...] = jnp.zeros_like(dv_ref)

        @pl.when((n == 0) & (r == 0))
        def _():
            db_ref[...] = jnp.zeros_like(db_ref)

        second = _lane_is_second_head((QB_A, W2))
        win = pl.ds(_window_start(n, nblk), WIN_A)
        variant = _window_variant(n, nblk)
        pairs = range(PPS)
        cols = [slice(pp * W2, (pp + 1) * W2) for pp in pairs]

        def head_rows(ref, pp):
            v2 = ref[:, cols[pp]]
            return jnp.concatenate([v2[:, 0:1], v2[:, HEAD_A:HEAD_A + 1]], axis=0)

        kw = [k_ref[win, c] for c in cols]
        vw = [v_ref[win, c] for c in cols]
        qs = [_stack_heads(q_ref[:, c], second) for c in cols]
        dos = [_stack_heads(do_ref[:, c], second) for c in cols]
        s = [lax.dot_general(qs[pp], kw[pp], (NT, ((), ())), preferred_element_type=F32) for pp in pairs]
        dp = [lax.dot_general(dos[pp], vw[pp], (NT, ((), ())), preferred_element_type=F32) for pp in pairs]
        p = [jnp.exp(s[pp] * scale + b_ref[2 * pp:2 * pp + 2].reshape(2 * QB_A, WIN_A) - head_rows(l_ref, pp)) for pp in pairs]
        ds = [p[pp] * (dp[pp] + head_rows(c_ref, pp)) for pp in pairs]
        db_ref[variant] += jnp.concatenate([x.reshape(2, QB_A, WIN_A) for x in ds], axis=0)
        pb = [x.astype(BF16) for x in p]
        dsb = [(x * scale).astype(BF16) for x in ds]
        dq_ref[...] = jnp.concatenate([_unstack_heads(jnp.dot(dsb[pp], kw[pp], preferred_element_type=F32), second)
                                       for pp in pairs], axis=1).astype(dq_ref.dtype)
        dk_ref[win, :] += jnp.concatenate([lax.dot_general(dsb[pp], qs[pp], (TN, ((), ())), preferred_element_type=F32)
                                           for pp in pairs], axis=1)
        dv_ref[win, :] += jnp.concatenate([lax.dot_general(pb[pp], dos[pp], (TN, ((), ())), preferred_element_type=F32)
                                           for pp in pairs], axis=1)

    kv_spec = _resident if d == 1 else _bs
    in_specs = [_bs((QB_A, WS), lambda hp, r, n: (n, colblk(0, r) * ob + hp)),
                kv_spec((L, WS), lambda hp, r, n: (0, colblk(1, r) * ob + hp)),
                kv_spec((L, WS), lambda hp, r, n: (0, colblk(2, r) * ob + hp))]
    in_specs += [_bs((QB_A, WS), lambda hp, r, n: (n, r * ob + hp))] * 3
    in_specs += [_bs((None, 2 * PPS, QB_A, WIN_A), lambda hp, r, n: (_window_variant(n, nblk), hp, 0, 0))]
    out_shape = [jax.ShapeDtypeStruct((L, d * WIDTH_A), BF16), jax.ShapeDtypeStruct((L, d * WIDTH_A), F32),
                 jax.ShapeDtypeStruct((L, d * WIDTH_A), F32), jax.ShapeDtypeStruct((3, HEADS_A, QB_A, WIN_A), F32)]
    out_specs = [_bs((QB_A, WS), lambda hp, r, n: (n, r * ob + hp)),
                 _bs((L, WS), lambda hp, r, n: (0, r * ob + hp)), _bs((L, WS), lambda hp, r, n: (0, r * ob + hp)),
                 _bs((3, 2 * PPS, QB_A, WIN_A), lambda hp, r, n: (0, hp, 0, 0))]
    dq, dk, dv, db = pl.pallas_call(
        body, out_shape=out_shape, grid=(ob, d, nblk), in_specs=in_specs, out_specs=out_specs,
        compiler_params=_params("arbitrary", "arbitrary", "arbitrary"), name=f"a_bwd_d{d}")(
            pv, pv, pv, do, lse, cterm, bias)
    return dq, dk, dv, db


def _assemble_dproj(a_parts, dq_b, dk_b, dv_b, dga, dgb):
    T = dq_b.shape[0]
    flat = [(a_parts[part][g], d) for part in range(3) for g, d in enumerate(DILATIONS)]
    rest = [dq_b, dk_b, dv_b, dga, dgb]

    def body(*refs):
        views, others = refs[:len(flat)], refs[len(flat):len(flat) + len(rest)]
        o_ref, chunks = refs[len(flat) + len(rest)], refs[len(flat) + len(rest) + 1:]
        col = 0
        for v_ref, (_, d) in zip(views, flat):
            _view_to_rows(v_ref, o_ref, col, d, chunks)
            col += WIDTH_A
        for x_ref in others:
            w = x_ref.shape[1]
            o_ref[:, col:col + w] = x_ref[...].astype(o_ref.dtype)
            col += w

    in_specs = [_bs((VIEW_ROWS // d, d * WIDTH_A), lambda i: (i, 0)) for _, d in flat]
    in_specs += [_bs((VIEW_ROWS, x.shape[1]), lambda i: (i, 0)) for x in rest]
    return pl.pallas_call(
        body, out_shape=jax.ShapeDtypeStruct((T, IN_WIDTH), BF16), grid=(T // VIEW_ROWS,), in_specs=in_specs,
        out_specs=_bs((VIEW_ROWS, IN_WIDTH), lambda i: (i, 0)), scratch_shapes=_view_chunks(),
        compiler_params=_params("parallel"), name="mix_bwd_dproj")(*[a for a, _ in flat], *rest)


def _segment_ones():
    i = np.arange(WIDTH_A)
    return jnp.asarray((i[:, None] // HEAD_A == i[None, :] // HEAD_A).astype(np.float32), dtype=BF16)


def _group_weights(l0, l1, l2):
    m = jnp.maximum(jnp.maximum(l0, l1), l2)
    e = [jnp.exp(l - m) for l in (l0, l1, l2)]
    z = e[0] + e[1] + e[2]
    return [ei / z for ei in e]


def _view_specs():
    return [_bs((VIEW_ROWS // d, d * WIDTH_A), lambda i: (i, 0)) for d in DILATIONS]


def _stage_tiles(n):
    return [pltpu.VMEM((VIEW_ROWS, WIDTH_A), F32)] * n


def _combine_fwd(outs, lses):
    T = outs[0].shape[0] * DILATIONS[0]
    n = len(DILATIONS)

    def body(*refs):
        o_refs, l_refs, oa_ref = refs[:n], refs[n:2 * n], refs[2 * n]
        o_st, l_st, chunks = refs[2 * n + 1:3 * n + 1], refs[3 * n + 1:4 * n + 1], refs[4 * n + 1:]
        for g, d in enumerate(DILATIONS):
            _view_to_rows(o_refs[g], o_st[g], 0, d, chunks)
            _view_to_rows(l_refs[g], l_st[g], 0, d, chunks)
        w = _group_weights(*[l[...] for l in l_st])
        oa_ref[...] = (w[0] * o_st[0][...] + w[1] * o_st[1][...] + w[2] * o_st[2][...]).astype(oa_ref.dtype)

    return pl.pallas_call(
        body, out_shape=jax.ShapeDtypeStruct((T, WIDTH_A), BF16), grid=(T // VIEW_ROWS,),
        in_specs=_view_specs() * 2, out_specs=_bs((VIEW_ROWS, WIDTH_A), lambda i: (i, 0)),
        scratch_shapes=_stage_tiles(2 * n) + _view_chunks(), compiler_params=_params("parallel"), name="a_combine")(*outs, *lses)


def _combine_bwd(doa, outs, lses):
    T = doa.shape[0]
    n = len(DILATIONS)

    def body(*refs):
        d_ref, o_refs, l_refs, seg_ref = refs[0], refs[1:n + 1], refs[n + 1:2 * n + 1], refs[2 * n + 1]
        do_refs, c_refs = refs[2 * n + 2:3 * n + 2], refs[3 * n + 2:4 * n + 2]
        o_st, l_st = refs[4 * n + 2:5 * n + 2], refs[5 * n + 2:6 * n + 2]
        tmp, chunks = refs[6 * n + 2], refs[6 * n + 3:]
        for g, d in enumerate(DILATIONS):
            _view_to_rows(o_refs[g], o_st[g], 0, d, chunks)
            _view_to_rows(l_refs[g], l_st[g], 0, d, chunks)
        dv = d_ref[...].astype(F32)
        w = _group_weights(*[l[...] for l in l_st])
        seg = seg_ref[...]
        tot = jnp.zeros(dv.shape, F32)
        for g in range(n):
            prod = w[g] * dv * o_st[g][...]
            hi = prod.astype(BF16)
            lo = (prod - hi.astype(F32)).astype(BF16)
            tot = tot + jnp.dot(hi, seg, preferred_element_type=F32) + jnp.dot(lo, seg, preferred_element_type=F32)
        for g, d in enumerate(DILATIONS):
            tmp[...] = w[g] * dv
            _rows_to_view(tmp, 0, do_refs[g], 0, d, chunks)
            tmp[...] = -w[g] * tot
            _rows_to_view(tmp, 0, c_refs[g], 0, d, chunks)

    views = [jax.ShapeDtypeStruct((T // d, d * WIDTH_A), dt) for dt in (BF16, F32) for d in DILATIONS]
    res = pl.pallas_call(
        body, out_shape=views, grid=(T // VIEW_ROWS,),
        in_specs=[_bs((VIEW_ROWS, WIDTH_A), lambda i: (i, 0))] + _view_specs() * 2 + [_bs((WIDTH_A, WIDTH_A), lambda i: (0, 0))],
        out_specs=_view_specs() * 2, scratch_shapes=_stage_tiles(2 * n + 1) + _view_chunks(),
        compiler_params=_params("parallel"), name="a_combine_bwd")(doa, *outs, *lses, _segment_ones())
    return res[:n], res[n:]


def _rope_tables(T):
    rows = T // GRID_W
    row = jnp.repeat(jnp.arange(rows, dtype=F32), GRID_W)
    col = jnp.tile(jnp.arange(GRID_W, dtype=F32), rows)
    n_freq = HEAD_B // 4
    freq = ROPE_THETA ** (-jnp.arange(n_freq, dtype=F32) / n_freq)
    ang = jnp.concatenate([row[:, None] * freq, col[:, None] * freq], axis=-1)
    cos, sin = jnp.repeat(jnp.cos(ang), 2, axis=1), jnp.repeat(jnp.sin(ang), 2, axis=1)
    sign = jnp.where(jnp.arange(HEAD_B) % 2 == 0, -1.0, 1.0).astype(F32)
    return cos, sin * sign


def _swap_pairs(v):
    even = lax.broadcasted_iota(jnp.int32, v.shape, v.ndim - 1) % 2 == 0
    n = v.shape[-1]
    return jnp.where(even, pltpu.roll(v, n - 1, v.ndim - 1), pltpu.roll(v, 1, v.ndim - 1))


def _qk_fwd(name, proj, col0, n_heads, gain, cos, sin, out_scale=1.0):
    T = proj.shape[0]

    def fn(xr, g, c, s):
        xn = _norm_fwd(xr.astype(F32), g)
        return (xn * c + _swap_pairs(xn) * s) * out_scale

    (out,) = _ew(name, fn, [_tiled(proj, HEAD_B, col0 // HEAD_B), _whole(gain), _table(cos), _table(sin)],
                 [(BF16, HEAD_B)], n_rows=T, rows=2048, ncols=n_heads)
    return out


def _qk_bwd(name, dout, proj, col0, n_heads, gain, cos, sin, in_scale=1.0):
    T = proj.shape[0]

    def fn(dv, xr, g, c, s):
        dv = dv.astype(F32) * in_scale
        dxn = c * dv + _swap_pairs(s * dv)
        dx, dgr = _norm_bwd(xr.astype(F32), g, dxn)
        return dx, _colsum(dgr)

    dx, dg = _ew(name, fn, [_tiled(dout, HEAD_B, 0), _tiled(proj, HEAD_B, col0 // HEAD_B), _whole(gain),
                            _table(cos), _table(sin)],
                 [(BF16, HEAD_B)], n_rows=T, rows=2048, reds=(HEAD_B,), ncols=n_heads)
    return dx, jnp.sum(dg, axis=0)


def _gqa_fwd(qn, kn, proj):
    T = qn.shape[0]
    GW = 4 * HEAD_B

    def body(q_ref, k_ref, v_ref, o_ref, l_ref):
        k, v = k_ref[...], v_ref[...]
        lane = lax.broadcasted_iota(jnp.int32, (QB_B, HEAD_B), 1)
        heads = range(4)
        s = [lax.dot_general(q_ref[:, g * HEAD_B:(g + 1) * HEAD_B], k, (NT, ((), ())), preferred_element_type=F32)
             for g in heads]
        m = [jnp.max(x, axis=-1, keepdims=True) for x in s]
        p = [jnp.exp2(x - mx) for x, mx in zip(s, m)]
        l = [jnp.sum(x, axis=-1, keepdims=True) for x in p]
        o = [jnp.dot(p[g].astype(BF16), v, preferred_element_type=F32) / l[g] for g in heads]
        o_ref[...] = jnp.concatenate(o, axis=1).astype(o_ref.dtype)
        lse_all = jnp.zeros((QB_B, HEAD_B), F32)
        for g in heads:
            lse_all = jnp.where(lane == g, m[g] + jnp.log2(l[g]), lse_all)
        l_ref[...] = lse_all

    return pl.pallas_call(
        body, out_shape=[jax.ShapeDtypeStruct((T, 2 * GW), BF16), jax.ShapeDtypeStruct((2, T, HEAD_B), F32)],
        grid=(2, T // QB_B),
        in_specs=[_bs((QB_B, GW), lambda kv, i: (i, kv)), _bs((T, HEAD_B), lambda kv, i: (0, kv)),
                  _bs((T, HEAD_B), lambda kv, i: (0, B_V // HEAD_B + kv))],
        out_specs=[_bs((QB_B, GW), lambda kv, i: (i, kv)), _bs((None, QB_B, HEAD_B), lambda kv, i: (kv, i, 0))],
        compiler_params=_params("parallel", "parallel"), name="b_fwd")(qn, kn, proj)


def _gqa_bwd(qn, kn, proj, o, lse, do, deps=()):
    T = qn.shape[0]
    GW = 4 * HEAD_B

    def body(q_ref, k_ref, v_ref, o_ref, l_ref, do_ref, *rest):
        dq_ref, dk_ref, dv_ref = rest[-3:]
        i = pl.program_id(1)

        @pl.when(i == 0)
        def _():
            dk_ref[...] = jnp.zeros_like(dk_ref)
            dv_ref[...] = jnp.zeros_like(dv_ref)

        k, v = k_ref[...], v_ref[...]
        lse_all = l_ref[...]
        for g in range(4):
            cols = slice(g * HEAD_B, (g + 1) * HEAD_B)
            q, dob = q_ref[:, cols], do_ref[:, cols]
            delta = jnp.sum(dob.astype(F32) * o_ref[:, cols].astype(F32), axis=-1, keepdims=True)
            s = lax.dot_general(q, k, (NT, ((), ())), preferred_element_type=F32)
            p = jnp.exp2(s - lse_all[:, g:g + 1])
            dp = lax.dot_general(dob, v, (NT, ((), ())), preferred_element_type=F32)
            ds = (p * (dp - delta)).astype(BF16)
            dq_ref[:, cols] = jnp.dot(ds, k, preferred_element_type=F32).astype(dq_ref.dtype)
            dk_ref[...] += lax.dot_general(ds, q, (TN, ((), ())), preferred_element_type=F32)
            dv_ref[...] += lax.dot_general(p.astype(BF16), dob, (TN, ((), ())), preferred_element_type=F32)

    return pl.pallas_call(
        body, out_shape=[jax.ShapeDtypeStruct((T, 2 * GW), BF16), jax.ShapeDtypeStruct((T, 2 * HEAD_B), F32),
                         jax.ShapeDtypeStruct((T, 2 * HEAD_B), F32)],
        grid=(2, T // QB_B),
        in_specs=[_bs((QB_B, GW), lambda kv, i: (i, kv)), _bs((T, HEAD_B), lambda kv, i: (0, kv)),
                  _bs((T, HEAD_B), lambda kv, i: (0, B_V // HEAD_B + kv)), _bs((QB_B, GW), lambda kv, i: (i, kv)),
                  _bs((None, QB_B, HEAD_B), lambda kv, i: (kv, i, 0)), _bs((QB_B, GW), lambda kv, i: (i, kv))] + _any_specs(len(deps)),
        out_specs=[_bs((QB_B, GW), lambda kv, i: (i, kv)), _bs((T, HEAD_B), lambda kv, i: (0, kv)),
                   _bs((T, HEAD_B), lambda kv, i: (0, kv))],
        compiler_params=_params("parallel", "arbitrary"), name="b_bwd")(qn, kn, proj, o, lse, do, *deps)


def _local_step(x, target, small, get_w, put_g, deps=()):
    T, D = x.shape
    gs = {}

    (x1, h2), ffn1_saved = _ffn_fwd("ffn1", x, small["ffn1_norm"], get_w, deps, tail_ins=[small["mix_norm"]],
                                    tail_fn=lambda y, g: (y, _norm_fwd(y, g)), tail_outs=(F32, BF16))
    w_in = get_w("w_in", h2)
    nq = w_in.shape[2]
    tpq = nq // WIDTH_A

    def proj_tile(j, k):
        c = j * tpq + k
        return jnp.where(c < 3 * len(DILATIONS), (c % 3) * 3 + c // 3, c)

    proj = _mm("mix_in", (4, tpq),
               [(h2, _resident((T, D), lambda j, k: (0, 0)), w_in, _bs((None, D, WIDTH_A), lambda j, k: (j, 0, k)))],
               jax.ShapeDtypeStruct((T, IN_WIDTH), BF16), _bs((T, WIDTH_A), lambda j, k: (0, proj_tile(j, k))), NN)

    bias = _bias_tiles(small["rel_bias"])
    a_views = [_group_view(proj, grp, d) for grp, d in enumerate(DILATIONS)]
    a_outs, a_lses = [], []
    for grp, d in enumerate(DILATIONS):
        o, l = _dil_fwd(a_views[grp], bias[grp], d)
        a_outs.append(o)
        a_lses.append(l)
    o_a = _combine_fwd(a_outs, a_lses)

    cos, sin = _rope_tables(T)
    qn = _qk_fwd("b_qnorm", proj, B_Q, 8, small["q_norm"], cos, sin, out_scale=QK_SCALE_LOG2)
    kn = _qk_fwd("b_knorm", proj, B_K, 2, small["k_norm"], cos, sin)
    o_b, lse_b = _gqa_fwd(qn, kn, proj)

    wa, wb, wo = get_w("w_branch_a", o_b), get_w("w_branch_b", o_b), get_w("w_out", o_b)
    bg_a, bg_b = small["b_gate"][:, :D], small["b_gate"][:, D:]
    n_a = wa.shape[0]

    def merge_out(oa_ref, ob_ref, ga_ref, gb_ref, x1_ref, wa_ref, wb_ref, wo_ref, ba_ref, bb_ref, g2_ref,
                  ta_ref, tb_ref, mg_ref, x2_ref, hn_ref):
        oa = oa_ref[...]
        ta = jnp.concatenate([jnp.dot(oa, wa_ref[j], preferred_element_type=F32) for j in range(n_a)], axis=1)
        tb = jnp.dot(ob_ref[...], wb_ref[...], preferred_element_type=F32)
        sa = _sigmoid(ga_ref[...].astype(F32) + ba_ref[...])
        sb = _sigmoid(gb_ref[...].astype(F32) + bb_ref[...])
        merged = (sa * ta + sb * tb).astype(BF16)
        ta_ref[...], tb_ref[...], mg_ref[...] = ta.astype(BF16), tb.astype(BF16), merged
        y = x1_ref[...] + jnp.dot(merged, wo_ref[...], preferred_element_type=F32)
        x2_ref[...] = y
        hn_ref[...] = _norm_fwd(y, g2_ref[...]).astype(BF16)

    row = _bs((512, D), lambda i: (i, 0))
    gate_specs = [_bs((512, D), lambda i: (i, G_A // D)), _bs((512, D), lambda i: (i, G_B // D))]
    whole2, whole3 = (lambda i: (0, 0)), (lambda i: (0, 0, 0))
    vec = _bs((1, D), whole2)
    t_a, t_b, merged, x2, hn2 = pl.pallas_call(
        merge_out, out_shape=[jax.ShapeDtypeStruct((T, D), BF16)] * 3 + [jax.ShapeDtypeStruct((T, D), F32), jax.ShapeDtypeStruct((T, D), BF16)],
        grid=(T // 512,),
        in_specs=[_bs((512, WIDTH_A), lambda i: (i, 0)), row] + gate_specs + [row, _resident(wa.shape, whole3), _resident((D, D), whole2),
                                                                                _resident((D, D), whole2), vec, vec, vec],
        out_specs=[row] * 5, compiler_params=_params("parallel"), name="mix_merge_out")(
            o_a, o_b, proj, proj, x1, wa, wb, wo, bg_a, bg_b, small["ffn2_norm"])

    def head(xv, g, tv):
        r = _rstd(xv)
        xh = xv * r
        e = xh * g - tv
        dy = e * (1.0 / D)
        dxh = dy * g
        dx = r * (dxh - xh * jnp.mean(dxh * xh, axis=-1, keepdims=True))
        return dx, 0.5 * dx, _colsum(e * e) * (0.5 / D), _colsum(dy * xh)

    (dx3, dx3_half, loss_cols, g_final), ffn2_saved = _ffn_fwd(
        "ffn2", x2, small["ffn2_norm"], get_w, h=hn2, tail_ins=[small["final_norm"].reshape(1, D), target], tail_fn=head,
        tail_outs=(F32, BF16), tail_reds=(D, D))
    gs["final_norm"] = g_final.reshape(D)

    dx2, _, dmix, gs["ffn2_norm"] = _ffn_bwd("ffn2", x2, small["ffn2_norm"], get_w, put_g, ffn2_saved, dx3, dx3_half,
                                             also_bf16=True)
    g_out = _mm_wgrad("mix_bwd_dwout", merged, dmix, a_cols=D // 4, b_cols=None, tm=256, tn=512, J=4).reshape(D, D)

    def merge_out_bwd(dx_ref, ta_ref, tb_ref, ga_ref, gb_ref, wa_ref, wb_ref, wo_ref, ba_ref, bb_ref,
                      dta_ref, dtb_ref, dga_ref, dgb_ref, doa_ref, dob_ref, dba_ref, dbb_ref):
        dm = lax.dot_general(dx_ref[...], wo_ref[...], (NT, ((), ())), preferred_element_type=F32)
        ta, tb = ta_ref[...].astype(F32), tb_ref[...].astype(F32)
        sa = _sigmoid(ga_ref[...].astype(F32) + ba_ref[...])
        sb = _sigmoid(gb_ref[...].astype(F32) + bb_ref[...])
        dga, dgb = dm * ta * sa * (1.0 - sa), dm * tb * sb * (1.0 - sb)
        dta, dtb = (dm * sa).astype(BF16), (dm * sb).astype(BF16)
        dta_ref[...], dtb_ref[...] = dta, dtb
        dga_ref[...], dgb_ref[...] = dga.astype(BF16), dgb.astype(BF16)
        w = wa_ref.shape[2]
        doa = sum(lax.dot_general(dta[:, j * w:(j + 1) * w], wa_ref[j], (NT, ((), ())), preferred_element_type=F32) for j in range(n_a))
        doa_ref[...] = doa.astype(BF16)
        dob_ref[...] = lax.dot_general(dtb, wb_ref[...], (NT, ((), ())), preferred_element_type=F32).astype(BF16)

        @pl.when(pl.program_id(0) == 0)
        def _():
            dba_ref[...] = jnp.zeros_like(dba_ref)
            dbb_ref[...] = jnp.zeros_like(dbb_ref)
        dba_ref[...] += _colsum(dga)
        dbb_ref[...] += _colsum(dgb)

    rowb = _bs((256, D), lambda i: (i, 0))
    gate_specs = [_bs((256, D), lambda i: (i, G_A // D)), _bs((256, D), lambda i: (i, G_B // D))]
    dta, dtb, dga, dgb, do_a, do_b, dba, dbb = pl.pallas_call(
        merge_out_bwd,
        out_shape=[jax.ShapeDtypeStruct((T, D), BF16)] * 4 + [jax.ShapeDtypeStruct((T, WIDTH_A), BF16), jax.ShapeDtypeStruct((T, D), BF16)]
        + [jax.ShapeDtypeStruct((1, D), F32)] * 2,
        grid=(T // 256,),
        in_specs=[rowb, rowb, rowb] + gate_specs + [_resident(wa.shape, whole3), _resident((D, D), whole2), _resident((D, D), whole2), vec, vec],
        out_specs=[rowb] * 4 + [_bs((256, WIDTH_A), lambda i: (i, 0)), rowb, vec, vec],
        compiler_params=_params("arbitrary"), name="mix_merge_out_bwd")(dmix, t_a, t_b, proj, proj, wa, wb, wo, bg_a, bg_b)
    gs["b_gate"] = jnp.concatenate([dba, dbb], axis=1)

    g_a = _mm_wgrad("mix_bwd_dwa", o_a, dta, a_cols=None, b_cols=D // 4, tm=WIDTH_A, tn=256, J=4)
    g_b = _mm_wgrad("mix_bwd_dwb", o_b, dtb, a_cols=D // 4, b_cols=None, tm=256, tn=512, J=4).reshape(D, D)
    deps = put_g({"w_out": g_out, "w_branch_a": g_a, "w_branch_b": g_b})

    dqn, dkn, dv_b = _gqa_bwd(qn, kn, proj, o_b, lse_b, do_b, deps)
    dq_b, gs["q_norm"] = _qk_bwd("b_bwd_qnorm", dqn, proj, B_Q, 8, small["q_norm"], cos, sin, in_scale=HEAD_B ** -0.5)
    dk_b, gs["k_norm"] = _qk_bwd("b_bwd_knorm", dkn, proj, B_K, 2, small["k_norm"], cos, sin, in_scale=1.0 / LOG2_E)

    do_groups, c_groups = _combine_bwd(do_a, a_outs, a_lses)
    dqs, dks, dvs, dbs = [], [], [], []
    for grp, d in enumerate(DILATIONS):
        dq, dk, dv, db = _dil_bwd(a_views[grp], bias[grp], do_groups[grp], a_lses[grp], c_groups[grp], d)
        dqs.append(dq), dks.append(dk), dvs.append(dv), dbs.append(db)
    gs["rel_bias"] = _bias_grad(jnp.stack(dbs))

    dproj = _assemble_dproj([dqs, dks, dvs], dq_b, dk_b, dv_b, dga, dgb)
    nq = w_in.shape[2]
    g_in = _mm("mix_bwd_dwin", (4, tpq),
               [(h2, _resident((T, D), lambda j, k: (0, 0)), dproj, _bs((T, WIDTH_A), lambda j, k: (0, j * tpq + k)))],
               jax.ShapeDtypeStruct((4, D, nq), BF16), _bs((None, D, WIDTH_A), lambda j, k: (j, 0, k)), TN)
    deps = put_g({"w_in": g_in})
    dx1, dx1_half, gs["mix_norm"] = _dh_norm_bwd(
        "mix_bwd_dh", 256,
        [(dproj, _bs((256, nq), lambda i, j=j: (i, j)), w_in, _resident((None, D, nq), lambda i, j=j: (j, 0, 0))) for j in range(4)],
        NT, x1, small["mix_norm"], dx2, deps)

    dx0, _, gs["ffn1_norm"] = _ffn_bwd("ffn1", x, small["ffn1_norm"], get_w, put_g, ffn1_saved, dx1, dx1_half)
    return loss_cols, dx0, gs


def _position():
    return lax.axis_index("x"), lax.axis_index("y"), lax.axis_index("c")


def _any_specs(n):
    return [pl.BlockSpec(memory_space=pl.ANY)] * n


HBM_SPEC = pl.BlockSpec(memory_space=pltpu.HBM)
SEM_SPEC = pl.BlockSpec(memory_space=pltpu.SEMAPHORE)
DATAFLOW_EFFECT = pltpu.SideEffectType.DATAFLOW_SIDE_EFFECTING
N_PEER_CHIPS = 3
LANES = 128


def _quarter_copies(srcs, lands, send_sems, recv_sems, mode):
    x, y, c = _position()
    me = 2 * x + y
    peers = [(1 - x, y, c), (x, 1 - y, c), (1 - x, 1 - y, c)]
    copies = []
    for src, land, send, recv in zip(srcs, lands, send_sems, recv_sems):
        if mode == "sibling":
            copies.append(pltpu.make_async_remote_copy(src_ref=src, dst_ref=land, send_sem=send.at[0], recv_sem=recv.at[0],
                                                       device_id=(x, y, 1 - c), device_id_type=MESH))
            continue
        scatter = mode == "scatter"
        half = land.shape[1] // 2
        mine = land.at[me, pl.ds(c * half, half)]
        for p, (px, py, pc) in enumerate(peers):
            copies.append(pltpu.make_async_remote_copy(
                src_ref=src.at[2 * px + py] if scatter else mine, dst_ref=land.at[me] if scatter else mine,
                send_sem=send.at[p], recv_sem=recv.at[p], device_id=(px, py, pc), device_id_type=MESH))
    return copies


def _fill_from_sibling(name, stacks):
    n = len(stacks)

    def body(*refs):
        outs = refs[n:2 * n]
        send_sems, recv_sems = refs[2 * n:]
        x, y, c = _position()
        copies = []
        for i, ref in enumerate(outs):
            half = ref.shape[1] // 2
            rows = pl.ds(c * half, half)
            for p, k in enumerate((2 * (1 - x) + y, 2 * x + (1 - y), 2 * (1 - x) + (1 - y))):
                cp = pltpu.make_async_remote_copy(ref.at[k, rows], ref.at[k, rows], send_sems.at[3 * i + p], recv_sems.at[3 * i + p],
                                                  device_id=(x, y, 1 - c), device_id_type=MESH)
                cp.start()
                copies.append(cp)
        for cp in copies:
            cp.wait()

    return pl.pallas_call(
        body, out_shape=[jax.ShapeDtypeStruct(s.shape, s.dtype) for s in stacks],
        in_specs=_any_specs(n), out_specs=_any_specs(n), input_output_aliases={i: i for i in range(n)},
        scratch_shapes=[pltpu.SemaphoreType.DMA((N_PEER_CHIPS * n,)), pltpu.SemaphoreType.DMA((N_PEER_CHIPS * n,))],
        compiler_params=pltpu.CompilerParams(has_side_effects=True), name=name)(*stacks)


def _exchange_start(name, srcs, lands, mode):
    n = len(lands)
    arrays = list(lands) if srcs is None else list(srcs) + list(lands)
    k = len(arrays)

    def body(*refs):
        land_refs = refs[k - n:k]
        send_sems, recv_sems = refs[k:k + n], refs[k + n:k + 2 * n]
        token = refs[2 * k + 2 * n]
        for cp in _quarter_copies(refs[:n], land_refs, send_sems, recv_sems, mode):
            cp.start()
        token[...] = jnp.zeros_like(token)

    sem = pltpu.SemaphoreType.DMA((N_PEER_CHIPS,))
    out_shape = [sem] * (2 * n) + [pltpu.HBM(a.shape, a.dtype) for a in arrays] + [jax.ShapeDtypeStruct((8, LANES), F32)]
    res = pl.pallas_call(
        body, name=name, out_shape=out_shape, in_specs=[HBM_SPEC] * k,
        out_specs=[SEM_SPEC] * (2 * n) + [HBM_SPEC] * k + [pl.BlockSpec(memory_space=pltpu.VMEM)],
        input_output_aliases={i: 2 * n + i for i in range(k)},
        compiler_params=pltpu.CompilerParams(has_side_effects=DATAFLOW_EFFECT),
    )(*[pltpu.with_memory_space_constraint(a, pltpu.HBM) for a in arrays])
    thru = res[2 * n:2 * n + k]
    return res[:n], res[n:2 * n], (None if srcs is None else thru[:n]), thru[k - n:], res[2 * n + k]


def _exchange_wait(name, srcs, lands, send_sems, recv_sems, after, mode):
    n = len(lands)
    arrays = list(lands) if srcs is None else list(srcs) + list(lands)
    k = len(arrays)

    def body(*refs):
        sends, recvs = refs[k:k + n], refs[k + n:k + 2 * n]
        for cp in _quarter_copies(refs[:n], refs[k - n:k], sends, recvs, mode):
            cp.wait_send()
            cp.wait_recv()

    res = pl.pallas_call(
        body, name=name, out_shape=[pltpu.HBM(a.shape, a.dtype) for a in arrays],
        in_specs=[HBM_SPEC] * k + [SEM_SPEC] * (2 * n) + [pl.BlockSpec(memory_space=pl.ANY)],
        out_specs=[HBM_SPEC] * k, input_output_aliases={i: i for i in range(k)},
        compiler_params=pltpu.CompilerParams(has_side_effects=DATAFLOW_EFFECT),
    )(*arrays, *send_sems, *recv_sems, after)
    return (None if srcs is None else res[:n]), res[k - n:]


def _own_slot(name, src, from_stack=False):
    R, C = src.shape[-2:]
    rows = R // 2
    me = (2 * lax.axis_index("x") + lax.axis_index("y")).astype(jnp.int32).reshape(1)

    def body(me_ref, x_ref, o_ref):
        o_ref[...] = x_ref[...].astype(o_ref.dtype)

    in_spec = (pl.BlockSpec((None, rows, C), lambda i, me_ref: (me_ref[0], i, 0)) if from_stack
               else pl.BlockSpec((rows, C), lambda i, me_ref: (i, 0)))
    grid_spec = pltpu.PrefetchScalarGridSpec(
        num_scalar_prefetch=1, grid=(R // rows,), in_specs=[in_spec],
        out_specs=pl.BlockSpec((None, rows, C), lambda i, me_ref: (me_ref[0], i, 0)))
    return pl.pallas_call(body, out_shape=jax.ShapeDtypeStruct((4, R, C), BF16), grid_spec=grid_spec,
                          compiler_params=_params("parallel"), name=name)(me, src)


def _allreduce_small(buf):
    R, C = buf.shape
    flips = [(fx, fy, fc) for fx in (0, 1) for fy in (0, 1) for fc in (0, 1)][1:]

    def body(in_ref, out_ref, land_ref, send_sems, recv_sems):
        x, y, c = _position()
        me = 4 * x + 2 * y + c
        copies = []
        for k, (fx, fy, fc) in enumerate(flips):
            px, py, pc = (1 - x if fx else x), (1 - y if fy else y), (1 - c if fc else c)
            cp = pltpu.make_async_remote_copy(in_ref, land_ref.at[me], send_sems.at[k], recv_sems.at[k],
                                              device_id=(px, py, pc), device_id_type=MESH)
            cp.start()
            copies.append(cp)
        land_ref[me] = in_ref[...]
        for cp in copies:
            cp.wait()
        acc = land_ref[0]
        for k in range(1, 8):
            acc = acc + land_ref[k]
        out_ref[...] = acc

    return pl.pallas_call(
        body, out_shape=jax.ShapeDtypeStruct((R, C), F32),
        in_specs=[pl.BlockSpec(memory_space=pltpu.VMEM)], out_specs=pl.BlockSpec(memory_space=pltpu.VMEM),
        scratch_shapes=[pltpu.VMEM((8, R, C), F32), pltpu.SemaphoreType.DMA((7,)), pltpu.SemaphoreType.DMA((7,))],
        compiler_params=pltpu.CompilerParams(has_side_effects=True), name="allreduce_small")(buf)


def _adamw_math(w, g, m, v):
    m2 = ADAM_B1 * m + (1.0 - ADAM_B1) * g
    v2 = ADAM_B2 * v + (1.0 - ADAM_B2) * (g * g)
    m_hat = m2 / (1.0 - ADAM_B1 ** ADAM_STEP)
    v_hat = v2 / (1.0 - ADAM_B2 ** ADAM_STEP)
    delta = -ADAM_LR * (m_hat / (jnp.sqrt(v_hat) + ADAM_EPS) + ADAM_WD * w)
    return delta, m2, v2


def _adamw_big(name, w, m, v, mine, theirs):
    R, C = w.shape
    rows = 256 if R % 256 == 0 else R // 2
    nrb = R // rows

    def four(a, b, c, d):
        return ((a.astype(F32) + b.astype(F32)) + c.astype(F32)) + d.astype(F32)

    def fn(wv, mv, vv, *parts):
        g = four(*parts[:4]) + four(*parts[4:])
        return (g,) + _adamw_math(wv, g, mv, vv)

    slots = [_tiled(s.reshape(4 * R, C), None, 0, k * nrb) for s in (mine, theirs) for k in range(4)]
    return _ew(name, fn, [_tiled(w), _tiled(m), _tiled(v)] + slots, [(F32, C)] * 4, n_rows=R, rows=rows)


BIG = ("ffn1_w1", "ffn1_w3", "ffn1_w2", "w_in", "w_branch_a", "w_branch_b", "w_out", "ffn2_w1", "ffn2_w3", "ffn2_w2")
SMALL = ("ffn1_norm", "mix_norm", "b_gate", "q_norm", "k_norm", "rel_bias", "ffn2_norm", "final_norm")
ORDER = ("ffn1_norm", "ffn1_w1", "ffn1_w3", "ffn1_w2", "mix_norm", "w_in", "b_gate", "q_norm", "k_norm", "rel_bias",
         "w_branch_a", "w_branch_b", "w_out", "ffn2_norm", "ffn2_w1", "ffn2_w3", "ffn2_w2", "final_norm")
TRANSPOSED = ("ffn1_w1", "ffn1_w3", "ffn2_w1", "ffn2_w3")
SIBLING_LAG = 2
GATHER_GROUPS = (("ffn1_w1", "ffn1_w3"), ("ffn1_w2",), ("w_in",), ("w_branch_a", "w_branch_b", "w_out"),
                 ("ffn2_w1", "ffn2_w3", "ffn2_w2"))


def _pack_small(d):
    rows = []
    for n in SMALL:
        flat = d[n].reshape(-1)
        pad = (-flat.shape[0]) % LANES
        rows.append(jnp.pad(flat, (0, pad)).reshape(-1, LANES))
    buf = jnp.concatenate(rows, axis=0)
    return jnp.pad(buf, ((0, (-buf.shape[0]) % 8), (0, 0)))


def _unpack_small(buf, like):
    out, r = {}, 0
    for n in SMALL:
        size = like[n].size
        nr = -(-size // LANES)
        out[n] = buf[r:r + nr].reshape(-1)[:size].reshape(like[n].shape)
        r += nr
    return out


def kernel(x, ffn1_norm, ffn1_w1, ffn1_w3, ffn1_w2, mix_norm, w_in, b_gate, q_norm, k_norm, rel_bias, w_branch_a, w_branch_b, w_out, ffn2_norm, ffn2_w1, ffn2_w3, ffn2_w2, final_norm, loss_target, m_ffn1_norm, m_ffn1_w1, m_ffn1_w3, m_ffn1_w2, m_mix_norm, m_w_in, m_b_gate, m_q_norm, m_k_norm, m_rel_bias, m_w_branch_a, m_w_branch_b, m_w_out, m_ffn2_norm, m_ffn2_w1, m_ffn2_w3, m_ffn2_w2, m_final_norm, v_ffn1_norm, v_ffn1_w1, v_ffn1_w3, v_ffn1_w2, v_mix_norm, v_w_in, v_b_gate, v_q_norm, v_k_norm, v_rel_bias, v_w_branch_a, v_w_branch_b, v_w_out, v_ffn2_norm, v_ffn2_w1, v_ffn2_w3, v_ffn2_w2, v_final_norm):
    given = dict(locals())
    w = {n: given[n] for n in ORDER}
    m = {n: given["m_" + n] for n in ORDER}
    v = {n: given["v_" + n] for n in ORDER}
    T, D = x.shape[1], x.shape[2]

    def stored(a, n):
        a = a.reshape(a.shape[1:])
        return a.T if n in TRANSPOSED else a

    def returned(a, n):
        return (a.T if n in TRANSPOSED else a).reshape(w[n].shape)

    quarter = {n: stored(w[n], n) for n in BIG}
    send, recv, _, land_thru, token = _exchange_start(
        "gather_start", None, [_own_slot(f"own_{n}", quarter[n]) for n in BIG], "gather")
    index = {n: i for i, n in enumerate(BIG)}
    ready = {}

    def get_w(name, after):
        if name not in ready:
            group = next(g for g in GATHER_GROUPS if name in g)
            ids = [index[n] for n in group]
            _, stacks = _exchange_wait("gather_wait_" + group[0], None, [land_thru[i] for i in ids],
                                       [send[i] for i in ids], [recv[i] for i in ids], after, "gather")
            stacks = _fill_from_sibling("gather_fill_" + group[0], stacks)
            for n, st in zip(group, stacks):
                ready[n] = st.reshape(D, D) if n in ("w_branch_b", "w_out") else st
        return ready[name]

    scattered, forwarded = [], []

    def forward_oldest(after):
        names, s_sem, r_sem, srcs, lands = scattered.pop(0)
        _, landed = _exchange_wait("scatter_wait_" + names[0], srcs, lands, s_sem, r_sem, after, "scatter")
        started = _exchange_start("sibling_start_" + names[0], landed, [lax.empty(a.shape, a.dtype) for a in landed], "sibling")
        forwarded.append((names,) + tuple(started[:4]))
        return started[4]

    def put_g(grads):
        names = list(grads)
        stacks = [grads[n].reshape((4,) + quarter[n].shape) for n in names]
        lands = [_own_slot(f"own_grad_{n}", s, from_stack=True) for n, s in zip(names, stacks)]
        started = _exchange_start("scatter_start_" + names[0], stacks, lands, "scatter")
        scattered.append((names,) + tuple(started[:4]))
        tokens = [started[4]]
        if len(scattered) > SIBLING_LAG:
            tokens.append(forward_oldest(started[4]))
        return tokens

    small = {n: w[n] for n in SMALL}
    loss_cols, grad_x, gs = _local_step(x.reshape(T, D), loss_target.reshape(T, D), small, get_w, put_g, deps=[token])

    after = grad_x
    while scattered:
        after = forward_oldest(after)
    grads, deltas, new_m, new_v = {}, {}, {}, {}
    for names, s_sem, r_sem, srcs, lands in forwarded:
        mine, theirs = _exchange_wait("sibling_wait_" + names[0], srcs, lands, s_sem, r_sem, after, "sibling")
        for n, a, b in zip(names, mine, theirs):
            res = _adamw_big(f"adamw_{n}", quarter[n], stored(m[n], n), stored(v[n], n), a, b)
            grads[n], deltas[n], new_m[n], new_v[n] = [returned(r, n) for r in res]

    gs = {n: gs[n].reshape(w[n].shape) for n in SMALL}
    packed_g = _pack_small(gs)
    n_small = packed_g.shape[0]
    summed = _allreduce_small(jnp.concatenate([packed_g, loss_cols.reshape(-1, LANES)], axis=0))
    g_small, loss = summed[:n_small], jnp.sum(summed[n_small:])
    packed = [_pack_small({n: d[n] for n in SMALL}) for d in (w, m, v)]
    R = g_small.shape[0]
    res = _ew("adamw_small", lambda wv, mv, vv, g: (g,) + _adamw_math(wv, g, mv, vv),
              [_tiled(packed[0]), _tiled(packed[1]), _tiled(packed[2]), _tiled(g_small)], [(F32, LANES)] * 4, n_rows=R, rows=R)
    for d, buf in zip((grads, deltas, new_m, new_v), res):
        d.update(_unpack_small(buf, w))

    return (loss, grad_x.reshape(x.shape), *[grads[n] for n in ORDER], *[deltas[n] for n in ORDER],
            *[new_m[n] for n in ORDER], *[new_v[n] for n in ORDER])
```

```python
import functools
import math

import numpy as np
import jax
import jax.numpy as jnp
from jax import lax
from jax.experimental import pallas as pl
from jax.experimental.pallas import tpu as pltpu

F32 = jnp.float32
BF16 = jnp.bfloat16
MESH = pl.DeviceIdType.MESH

NEG_INF = -1e30
EPS = 1e-6
GRID_W = 64
ROPE_THETA = 10000.0
DILATIONS = (1, 4, 16)
BAND_HALF = 64
HEAD_A = 64
HEADS_A = 8
WIDTH_A = HEADS_A * HEAD_A
HEAD_B = 128
LOG2_E = math.log2(math.e)
QK_SCALE_LOG2 = HEAD_B ** -0.5 * LOG2_E
N_BUCKETS = 32
MAX_DISTANCE = 1024
ADAM_LR, ADAM_B1, ADAM_B2, ADAM_EPS, ADAM_WD, ADAM_STEP = 0.001, 0.9, 0.999, 1e-08, 0.01, 10

B_Q, B_K, B_V = 4608, 5632, 5888
G_A, G_B = 6144, 7168
IN_WIDTH = 8192

VMEM_LIMIT_BYTES = 56 * 1024 * 1024
QB_A = 128
QB_B = 256


def _params(*sem):
    return pltpu.CompilerParams(dimension_semantics=sem, vmem_limit_bytes=VMEM_LIMIT_BYTES)


def _bs(shape, fn):
    return pl.BlockSpec(shape, fn)


def _resident(shape, fn):
    return pl.BlockSpec(shape, fn, pipeline_mode=pl.Buffered(1))


def _mm(name, grid, pairs, out_shape, out_spec, dims, *, extras=(), epilogue=None, deps=(), reds=()):
    n_pairs, n_extra, n_deps = len(pairs), len(extras), len(deps)
    operands = [p[0] for p in pairs] + [p[2] for p in pairs] + [e[0] for e in extras] + list(deps)
    in_specs = [p[1] for p in pairs] + [p[3] for p in pairs] + [e[1] for e in extras] + _any_specs(n_deps)
    single = not isinstance(out_shape, (list, tuple))
    out_shapes = [out_shape] if single else list(out_shape)
    out_specs = [out_spec] if single else list(out_spec)
    n_out = len(out_shapes)
    out_shapes += [jax.ShapeDtypeStruct((1, w), F32) for w in reds]
    out_specs += [_bs((1, w), lambda *_: (0, 0)) for w in reds]

    def body(*refs):
        a_refs, b_refs = refs[:n_pairs], refs[n_pairs:2 * n_pairs]
        e_refs = refs[2 * n_pairs:2 * n_pairs + n_extra]
        o_refs = refs[2 * n_pairs + n_extra + n_deps:]
        acc = None
        for a_ref, b_ref in zip(a_refs, b_refs):
            t = lax.dot_general(a_ref[...], b_ref[...], (dims, ((), ())), preferred_element_type=F32)
            acc = t if acc is None else acc + t
        vals = acc if epilogue is None else epilogue(acc, *[e[...] for e in e_refs])
        if not isinstance(vals, (list, tuple)):
            vals = (vals,)
        for o_ref, v in zip(o_refs[:n_out], vals[:n_out]):
            o_ref[...] = v.astype(o_ref.dtype)
        if reds:
            first = functools.reduce(jnp.logical_and, [pl.program_id(ax) == 0 for ax in range(len(grid))])
            for r_ref, v in zip(o_refs[n_out:], vals[n_out:]):
                @pl.when(first)
                def _(r_ref=r_ref):
                    r_ref[...] = jnp.zeros_like(r_ref)
                r_ref[...] += v

    sem = ["arbitrary" if reds else "parallel"] * len(grid)
    res = pl.pallas_call(
        body, out_shape=out_shapes, grid=grid, in_specs=in_specs, out_specs=out_specs,
        compiler_params=_params(*sem), name=name)(*operands)
    return res[0] if (single and not reds) else res


NN = ((1,), (0,))
NT = ((1,), (1,))
TN = ((0,), (0,))


def _mm_cols(name, a, w, *, tm, tn, out_dtype, cat, extras=(), epilogue=None):
    M, K = a.shape
    J, _, n = w.shape
    tn = min(tn, n)
    nb = n // tn
    if cat:
        shape, spec = (M, J * n), _bs((tm, tn), lambda j, i, k: (i, j * nb + k))
    else:
        shape, spec = (J, M, n), _bs((None, tm, tn), lambda j, i, k: (j, i, k))
    ex = [(e, _bs((tm, tn), lambda j, i, k: (i, j * nb + k))) for e in extras]
    return _mm(name, (J, M // tm, nb),
               [(a, _bs((tm, K), lambda j, i, k: (i, 0)), w, _bs((None, K, tn), lambda j, i, k: (j, 0, k)))],
               jax.ShapeDtypeStruct(shape, out_dtype), spec, NN, extras=ex, epilogue=epilogue)


def _mm_rows_t(name, a, w, *, tm, out_dtype):
    M, N = a.shape
    J, f, _ = w.shape
    return _mm(name, (J, M // tm),
               [(a, _bs((tm, N), lambda j, i: (i, 0)), w, _bs((None, f, N), lambda j, i: (j, 0, 0)))],
               jax.ShapeDtypeStruct((J, M, f), out_dtype), _bs((None, tm, f), lambda j, i: (j, i, 0)), NT)


def _mm_wgrad(name, a, b, *, a_cols, b_cols, tm, tn, J):
    def pick(arr, cols, t):
        if arr.ndim == 3:
            T, c = arr.shape[1], arr.shape[2]
            t = min(t, c)
            return T, c, t, (lambda sel: _bs((None, T, t), lambda j, i, k: (j, 0, sel(i, k))))
        T = arr.shape[0]
        c = arr.shape[1] if cols is None else cols
        t = min(t, c)
        per = c // t
        if cols is None:
            if per == 1:
                return T, c, t, (lambda sel: _resident((T, t), lambda j, i, k: (0, 0)))
            return T, c, t, (lambda sel: _bs((T, t), lambda j, i, k: (0, sel(i, k))))
        return T, c, t, (lambda sel: _bs((T, t), lambda j, i, k: (0, j * per + sel(i, k))))
    _, ca, tm, mk_a = pick(a, a_cols, tm)
    _, cb, tn, mk_b = pick(b, b_cols, tn)
    return _mm(name, (J, ca // tm, cb // tn),
               [(a, mk_a(lambda i, k: i), b, mk_b(lambda i, k: k))],
               jax.ShapeDtypeStruct((J, ca, cb), BF16), _bs((None, tm, tn), lambda j, i, k: (j, i, k)), TN)


def _tiled(arr, width=None, col=0, rowblk=0):
    return ("t", arr, arr.shape[1] if width is None else width, col, rowblk)


def _table(arr):
    return ("f", arr)


def _whole(arr):
    return ("w", arr)


def _ew(name, fn, ins, outs, *, n_rows, rows, reds=(), ncols=1, deps=()):
    nrb = n_rows // rows
    n_deps = len(deps)
    operands, in_specs = [], []
    for spec in ins:
        if spec[0] == "t":
            _, arr, width, col, rowblk = spec
            step = 1 if ncols > 1 else 0
            in_specs.append(_bs((rows, width), lambda c, i, col=col, rowblk=rowblk, step=step: (rowblk + i, col + c * step)))
        elif spec[0] == "f":
            arr = spec[1]
            in_specs.append(_bs((rows, arr.shape[1]), lambda c, i: (i, 0)))
        else:
            arr = spec[1]
            nd = arr.ndim
            if nd == 3:
                in_specs.append(_bs((None,) + arr.shape[1:], lambda c, i: (c, 0, 0)))
            else:
                in_specs.append(_bs(arr.shape, lambda c, i, nd=nd: (0,) * nd))
        operands.append(arr)
    out_shapes = [jax.ShapeDtypeStruct((n_rows, ncols * w), dt) for dt, w in outs]
    out_specs = [_bs((rows, w), lambda c, i: (i, c)) for _, w in outs]
    out_shapes += [jax.ShapeDtypeStruct((ncols, 1, w), F32) for w in reds]
    out_specs += [_bs((None, 1, w), lambda c, i: (c, 0, 0)) for w in reds]
    n_in, n_out, n_red = len(ins), len(outs), len(reds)
    operands += list(deps)
    in_specs += _any_specs(n_deps)

    def body(*refs):
        vals = fn(*[r[...] for r in refs[:n_in]])
        if not isinstance(vals, (tuple, list)):
            vals = (vals,)
        o_refs = refs[n_in + n_deps:]
        for o_ref, v in zip(o_refs[:n_out], vals[:n_out]):
            o_ref[...] = v.astype(o_ref.dtype)
        if n_red:
            i = pl.program_id(1)
            for r_ref, v in zip(o_refs[n_out:], vals[n_out:]):
                @pl.when(i == 0)
                def _(r_ref=r_ref):
                    r_ref[...] = jnp.zeros_like(r_ref)
                r_ref[...] += v

    res = pl.pallas_call(
        body, out_shape=out_shapes, grid=(ncols, nrb), in_specs=in_specs, out_specs=out_specs,
        compiler_params=_params("parallel", "arbitrary" if n_red else "parallel"), name=name)(*operands)
    return res


def _colsum(v):
    return jnp.sum(v, axis=0, keepdims=True)


def _rstd(x):
    return lax.rsqrt(jnp.mean(x * x, axis=-1, keepdims=True) + EPS)


def _sigmoid(x):
    return 1.0 / (1.0 + jnp.exp(-x))


def _norm_fwd(x, g):
    return x * _rstd(x) * g


def _norm_bwd(x, g, dy):
    r = _rstd(x)
    xh = x * r
    dxh = dy * g
    dx = r * (dxh - xh * jnp.mean(dxh * xh, axis=-1, keepdims=True))
    return dx, dy * xh


def _row_spec(arr, rows):
    if arr.shape[0] == 1:
        return _bs(arr.shape, lambda i: (0, 0))
    return _bs((rows, arr.shape[1]), lambda i: (i, 0))


def _ffn_fwd(tag, x, gain, get_w, deps=(), *, h=None, tail_ins=(), tail_fn=None, tail_outs=(F32,), tail_reds=()):
    T, D = x.shape
    if h is None:
        (h,) = _ew(f"{tag}_norm", lambda xv, g: _norm_fwd(xv, g), [_tiled(x), _whole(gain)], [(BF16, D)], n_rows=T, rows=512,
                   deps=deps)
    w1, w3 = get_w(f"{tag}_w1", h), get_w(f"{tag}_w3", h)
    J, f, _ = w1.shape
    tm = 1024

    def up(h_ref, w1_ref, w3_ref, u_ref, g_ref, a_ref):
        hv = h_ref[...]
        u = lax.dot_general(hv, w1_ref[...], (NT, ((), ())), preferred_element_type=F32)
        g = lax.dot_general(hv, w3_ref[...], (NT, ((), ())), preferred_element_type=F32)
        u_ref[...] = u.astype(BF16)
        g_ref[...] = g.astype(BF16)
        a_ref[...] = (u * _sigmoid(u) * g).astype(BF16)

    slab = _bs((None, tm, f), lambda j, i: (j, i, 0))
    w_spec = _bs((None, f, D), lambda j, i: (j, 0, 0))
    u, g, a = pl.pallas_call(
        up, out_shape=[jax.ShapeDtypeStruct((J, T, f), BF16)] * 3, grid=(J, T // tm),
        in_specs=[_bs((tm, D), lambda j, i: (i, 0)), w_spec, w_spec], out_specs=[slab] * 3,
        compiler_params=_params("parallel", "parallel"), name=f"{tag}_up")(h, w1, w3)
    w2 = get_w(f"{tag}_w2", a)
    def tail(acc, xv, *rest):
        y = xv + 0.5 * acc
        return y if tail_fn is None else tail_fn(y, *rest)

    row = _bs((512, D), lambda i: (i, 0))
    res = _mm(f"{tag}_down", (T // 512,),
              [(a, _bs((None, 512, f), lambda i, j=j: (j, i, 0)), w2, _resident((None, f, D), lambda i, j=j: (j, 0, 0)))
               for j in range(J)],
              [jax.ShapeDtypeStruct((T, D), dt) for dt in tail_outs], [row] * len(tail_outs), NN,
              extras=[(x, row)] + [(t, _row_spec(t, 512)) for t in tail_ins], epilogue=tail, reds=tail_reds)
    return res, (h, u, g, a)


def _dh_norm_bwd(name, rows, pairs, dims, x, gain, dres, deps, also_bf16=False):
    T, D = x.shape

    def epilogue(dh, xv, gv, dr):
        dx, dgr = _norm_bwd(xv, gv, dh)
        dx = dx + dr
        return (dx, 0.5 * dx) + ((dx,) if also_bf16 else ()) + (_colsum(dgr),)

    dts = [F32, BF16] + ([BF16] if also_bf16 else [])
    row = _bs((rows, D), lambda i: (i, 0))
    return _mm(name, (T // rows,), pairs, [jax.ShapeDtypeStruct((T, D), dt) for dt in dts], [row] * len(dts), dims,
               extras=[(x, row), (gain, _row_spec(gain, rows)), (dres, row)], epilogue=epilogue, deps=deps, reds=(D,))


def _ffn_bwd(tag, x, gain, get_w, put_g, saved, dy, dy_half, also_bf16=False):
    h, u, g, a = saved
    T, D = x.shape
    w1, w3, w2 = [get_w(f"{tag}_{n}", dy_half) for n in ("w1", "w3", "w2")]
    J, f, _ = w1.shape
    dw2 = _mm_wgrad(f"{tag}_bwd_dw2", a, dy_half, a_cols=None, b_cols=None, tm=f, tn=D, J=J)
    deps = put_g({f"{tag}_w2": dw2})
    tm = 1024

    def up_bwd(dy_ref, w2_ref, u_ref, g_ref, *rest):
        du_ref, dg_ref = rest[-2:]
        da = lax.dot_general(dy_ref[...], w2_ref[...], (NT, ((), ())), preferred_element_type=F32)
        uv, gv = u_ref[...].astype(F32), g_ref[...].astype(F32)
        s = _sigmoid(uv)
        du_ref[...] = (da * gv * (s * (1.0 + uv * (1.0 - s)))).astype(BF16)
        dg_ref[...] = (da * (uv * s)).astype(BF16)

    slab = _bs((None, tm, f), lambda j, i: (j, i, 0))
    du, dg = pl.pallas_call(
        up_bwd, out_shape=[jax.ShapeDtypeStruct((J, T, f), BF16)] * 2, grid=(J, T // tm),
        in_specs=[_bs((tm, D), lambda j, i: (i, 0)), _bs((None, f, D), lambda j, i: (j, 0, 0)), slab, slab] + _any_specs(len(deps)),
        out_specs=[slab] * 2, compiler_params=_params("parallel", "parallel"), name=f"{tag}_bwd_up")(dy_half, w2, u, g, *deps)
    dw1 = _mm_wgrad(f"{tag}_bwd_dw1", du, h, a_cols=None, b_cols=None, tm=f, tn=D, J=J)
    dw3 = _mm_wgrad(f"{tag}_bwd_dw3", dg, h, a_cols=None, b_cols=None, tm=f, tn=D, J=J)
    deps = deps + put_g({f"{tag}_w1": dw1, f"{tag}_w3": dw3})
    pairs = []
    for j in range(J):
        a_spec = _bs((None, 512, f), lambda i, j=j: (j, i, 0))
        w_spec = _resident((None, f, D), lambda i, j=j: (j, 0, 0))
        pairs += [(du, a_spec, w1, w_spec), (dg, a_spec, w3, w_spec)]
    return _dh_norm_bwd(f"{tag}_bwd_dh", 512, pairs, NN, x, gain, dy, deps, also_bf16)


def _t5_bucket(rel):
    n = N_BUCKETS // 2
    max_exact = n // 2
    ret = jnp.where(rel > 0, n, 0)
    a = jnp.abs(rel)
    af = jnp.maximum(a, 1).astype(F32)
    large = max_exact + (jnp.log(af / max_exact) / math.log(MAX_DISTANCE / max_exact) * (n - max_exact)).astype(jnp.int32)
    large = jnp.minimum(large, n - 1)
    return ret + jnp.where(a < max_exact, a, large)


WIN_A = QB_A + 2 * BAND_HALF
WIN_SHIFTS = (0, BAND_HALF, 2 * BAND_HALF)


def _window_variant(n, nblk):
    return jnp.where(n == 0, 0, jnp.where(n == nblk - 1, 2, 1))


def _window_start(n, nblk):
    return pl.multiple_of(jnp.clip(n * QB_A - BAND_HALF, 0, nblk * QB_A - WIN_A), BAND_HALF)


def _band_steps(xp=jnp):
    qi = xp.arange(QB_A, dtype=xp.int32)[None, :, None]
    kj = xp.arange(WIN_A, dtype=xp.int32)[None, None, :]
    return kj - qi - xp.asarray(WIN_SHIFTS, dtype=xp.int32)[:, None, None]


def _bias_tiles(rel_bias):
    wide = QB_A + 2 * WIN_SHIFTS[-1]
    qi = jnp.arange(QB_A, dtype=jnp.int32)[:, None]
    steps = jnp.arange(wide, dtype=jnp.int32)[None, :] - WIN_SHIFTS[-1] - qi
    buckets = jnp.stack([_t5_bucket(steps * d) for d in DILATIONS])
    inband = (jnp.abs(steps) <= BAND_HALF).astype(jnp.int32)
    n_heads = rel_bias.shape[1]

    def body(tab_ref, b_ref, m_ref, o_ref):
        hd = pl.program_id(0)
        bkt = b_ref[...]
        acc = jnp.zeros(bkt.shape, F32)
        for b in range(N_BUCKETS):
            acc = jnp.where(bkt == b, tab_ref[b, hd], acc)
        o_ref[...] = jnp.where(m_ref[...] > 0, acc, NEG_INF)

    base = pl.pallas_call(
        body, out_shape=jax.ShapeDtypeStruct((n_heads, QB_A, wide), F32), grid=(n_heads,),
        in_specs=[pl.BlockSpec(memory_space=pltpu.SMEM),
                  _bs((None, QB_A, wide), lambda hd: (hd // HEADS_A, 0, 0)),
                  _bs((QB_A, wide), lambda hd: (0, 0))],
        out_specs=_bs((None, QB_A, wide), lambda hd: (hd, 0, 0)),
        compiler_params=_params("parallel"), name="a_bias_tiles")(rel_bias, buckets, inband)
    base = base.reshape(len(DILATIONS), HEADS_A, QB_A, wide)
    return jnp.stack([base[..., WIN_SHIFTS[-1] - s:WIN_SHIFTS[-1] - s + WIN_A] for s in WIN_SHIFTS], axis=1)


def _bias_grad(dbias):
    steps = _band_steps(np)
    inband = np.abs(steps) <= BAND_HALF
    present = []
    for d in DILATIONS:
        rel = steps * d
        a = np.abs(rel)
        large = 8 + (np.log(np.maximum(a, 1) / 8.0) / math.log(MAX_DISTANCE / 8.0) * 8).astype(np.int64)
        bk = np.where(rel > 0, 16, 0) + np.where(a < 8, a, np.minimum(large, 15))
        present.append([sorted(set(bk[v][inband[v]].tolist())) for v in range(3)])
    buckets = jnp.stack([_t5_bucket(_band_steps() * d) for d in DILATIONS])
    n_heads = len(DILATIONS) * HEADS_A

    def body(b_ref, d_ref, o_ref):
        row = lax.broadcasted_iota(jnp.int32, (N_BUCKETS, n_heads), 0)
        col = lax.broadcasted_iota(jnp.int32, (N_BUCKETS, n_heads), 1)
        out = jnp.zeros((N_BUCKETS, n_heads), F32)
        for grp in range(len(DILATIONS)):
            for hh in range(HEADS_A):
                hd = grp * HEADS_A + hh
                for b in sorted(set(sum(present[grp], []))):
                    tot = jnp.zeros((), F32)
                    for v in range(3):
                        if b in present[grp][v]:
                            tot = tot + jnp.sum(jnp.where(b_ref[grp, v] == b, d_ref[grp, v, hh], 0.0))
                    out = jnp.where((row == b) & (col == hd), tot, out)
        o_ref[...] = out

    return pl.pallas_call(
        body, out_shape=jax.ShapeDtypeStruct((N_BUCKETS, n_heads), F32),
        compiler_params=pltpu.CompilerParams(vmem_limit_bytes=VMEM_LIMIT_BYTES), name="a_bias_grad")(buckets, dbias)


def _lane_is_second_head(shape):
    return lax.broadcasted_iota(jnp.int32, shape, len(shape) - 1) >= HEAD_A


VIEW_ROWS = 512


def _view_chunks():
    return [pltpu.VMEM((VIEW_ROWS, LANES), F32)] * (WIDTH_A // LANES)


def _rows_to_view(x_ref, col, o_ref, ocol, d, chunks):
    n = VIEW_ROWS // d
    for c, scr in enumerate(chunks):
        scr[...] = x_ref[:, col + c * LANES:col + (c + 1) * LANES].astype(F32)
        for r in range(d):
            at = ocol + r * WIDTH_A + c * LANES
            o_ref[:, at:at + LANES] = scr[pl.ds(r, n, stride=d), :].astype(o_ref.dtype)


def _view_to_rows(v_ref, o_ref, col, d, chunks):
    n = VIEW_ROWS // d
    for c, scr in enumerate(chunks):
        if d == 1:
            o_ref[:, col + c * LANES:col + (c + 1) * LANES] = v_ref[:, c * LANES:(c + 1) * LANES].astype(o_ref.dtype)
            continue
        for r in range(d):
            scr[pl.ds(r, n, stride=d), :] = v_ref[:, r * WIDTH_A + c * LANES:r * WIDTH_A + (c + 1) * LANES].astype(F32)
        o_ref[:, col + c * LANES:col + (c + 1) * LANES] = scr[...].astype(o_ref.dtype)


def _group_view(proj, grp, d):
    T = proj.shape[0]
    if d == 1:
        return proj, (lambda part, r: grp * 3 + part)

    def body(x_ref, o_ref, *chunks):
        for part in range(3):
            _rows_to_view(x_ref, part * WIDTH_A, o_ref, part * d * WIDTH_A, d, chunks)

    view = pl.pallas_call(
        body, out_shape=jax.ShapeDtypeStruct((T // d, 3 * d * WIDTH_A), proj.dtype), grid=(T // VIEW_ROWS,),
        in_specs=[_bs((VIEW_ROWS, 3 * WIDTH_A), lambda i: (i, grp))],
        out_specs=_bs((VIEW_ROWS // d, 3 * d * WIDTH_A), lambda i: (i, 0)),
        scratch_shapes=_view_chunks(), compiler_params=_params("parallel"), name=f"a_view_d{d}")(proj)
    return view, (lambda part, r: part * d + r)


def _stack_heads(v2, second):
    zero = jnp.zeros_like(v2)
    return jnp.concatenate([jnp.where(second, zero, v2), jnp.where(second, v2, zero)], axis=0)


def _unstack_heads(v, second):
    return jnp.where(second, v[QB_A:], v[:QB_A])


def _dil_fwd(view, bias, d):
    pv, colblk = view
    L = pv.shape[0]
    nblk = L // QB_A
    W2 = 2 * HEAD_A
    scale = HEAD_A ** -0.5

    def body(q_ref, k_ref, v_ref, b_ref, o_ref, l_ref):
        win = pl.ds(_window_start(pl.program_id(1), nblk), WIN_A)
        second = _lane_is_second_head((QB_A, W2))
        pairs = range(HEADS_A // 2)
        cols = [slice(hp * W2, (hp + 1) * W2) for hp in pairs]
        s = [lax.dot_general(_stack_heads(q_ref[:, cols[hp]], second), k_ref[win, cols[hp]], (NT, ((), ())),
                             preferred_element_type=F32) * scale + b_ref[2 * hp:2 * hp + 2].reshape(2 * QB_A, WIN_A)
             for hp in pairs]
        m = [jnp.max(x, axis=-1, keepdims=True) for x in s]
        p = [jnp.exp(x - mx) for x, mx in zip(s, m)]
        l = [jnp.sum(x, axis=-1, keepdims=True) for x in p]
        res = [jnp.dot(p[hp].astype(BF16), v_ref[win, cols[hp]], preferred_element_type=F32) / l[hp] for hp in pairs]
        o_ref[...] = jnp.concatenate([_unstack_heads(x, second) for x in res], axis=1).astype(o_ref.dtype)
        l_ref[...] = jnp.concatenate([_unstack_heads(jnp.broadcast_to(mx + jnp.log(lx), (2 * QB_A, W2)), second)
                                      for mx, lx in zip(m, l)], axis=1)

    in_specs = [_bs((QB_A, WIDTH_A), lambda r, n: (n, colblk(0, r))),
                _bs((L, WIDTH_A), lambda r, n: (0, colblk(1, r))), _bs((L, WIDTH_A), lambda r, n: (0, colblk(2, r))),
                _bs((None, HEADS_A, QB_A, WIN_A), lambda r, n: (_window_variant(n, nblk), 0, 0, 0))]
    o, lse = pl.pallas_call(
        body, out_shape=[jax.ShapeDtypeStruct((L, d * WIDTH_A), BF16), jax.ShapeDtypeStruct((L, d * WIDTH_A), F32)],
        grid=(d, nblk), in_specs=in_specs,
        out_specs=[_bs((QB_A, WIDTH_A), lambda r, n: (n, r)), _bs((QB_A, WIDTH_A), lambda r, n: (n, r))],
        compiler_params=_params("parallel", "parallel"), name=f"a_fwd_d{d}")(pv, pv, pv, bias)
    return o, lse


def _dil_bwd(view_qkv, bias, do, lse, cterm, d):
    pv, colblk = view_qkv
    L = pv.shape[0]
    nblk = L // QB_A
    W2 = 2 * HEAD_A
    PPS = 4
    WS = PPS * W2
    ob = WIDTH_A // WS
    scale = HEAD_A ** -0.5

    def body(q_ref, k_ref, v_ref, do_ref, l_ref, c_ref, b_ref, dq_ref, dk_ref, dv_ref, db_ref):
        r, n = pl.program_id(1), pl.program_id(2)

        @pl.when(n == 0)
        def _():
            dk_ref[...] = jnp.zeros_like(dk_ref)
            dv_ref[...] = jnp.zeros_like(dv_ref)

        @pl.when((n == 0) & (r == 0))
        def _():
            db_ref[...] = jnp.zeros_like(db_ref)

        second = _lane_is_second_head((QB_A, W2))
        win = pl.ds(_window_start(n, nblk), WIN_A)
        variant = _window_variant(n, nblk)
        pairs = range(PPS)
        cols = [slice(pp * W2, (pp + 1) * W2) for pp in pairs]

        def head_rows(ref, pp):
            v2 = ref[:, cols[pp]]
            return jnp.concatenate([v2[:, 0:1], v2[:, HEAD_A:HEAD_A + 1]], axis=0)

        kw = [k_ref[win, c] for c in cols]
        vw = [v_ref[win, c] for c in cols]
        qs = [_stack_heads(q_ref[:, c], second) for c in cols]
        dos = [_stack_heads(do_ref[:, c], second) for c in cols]
        s = [lax.dot_general(qs[pp], kw[pp], (NT, ((), ())), preferred_element_type=F32) for pp in pairs]
        dp = [lax.dot_general(dos[pp], vw[pp], (NT, ((), ())), preferred_element_type=F32) for pp in pairs]
        p = [jnp.exp(s[pp] * scale + b_ref[2 * pp:2 * pp + 2].reshape(2 * QB_A, WIN_A) - head_rows(l_ref, pp)) for pp in pairs]
        ds = [p[pp] * (dp[pp] + head_rows(c_ref, pp)) for pp in pairs]
        db_ref[variant] += jnp.concatenate([x.reshape(2, QB_A, WIN_A) for x in ds], axis=0)
        pb = [x.astype(BF16) for x in p]
        dsb = [(x * scale).astype(BF16) for x in ds]
        dq_ref[...] = jnp.concatenate([_unstack_heads(jnp.dot(dsb[pp], kw[pp], preferred_element_type=F32), second)
                                       for pp in pairs], axis=1).astype(dq_ref.dtype)
        dk_ref[win, :] += jnp.concatenate([lax.dot_general(dsb[pp], qs[pp], (TN, ((), ())), preferred_element_type=F32)
                                           for pp in pairs], axis=1)
        dv_ref[win, :] += jnp.concatenate([lax.dot_general(pb[pp], dos[pp], (TN, ((), ())), preferred_element_type=F32)
                                           for pp in pairs], axis=1)

    kv_spec = _resident if d == 1 else _bs
    in_specs = [_bs((QB_A, WS), lambda hp, r, n: (n, colblk(0, r) * ob + hp)),
                kv_spec((L, WS), lambda hp, r, n: (0, colblk(1, r) * ob + hp)),
                kv_spec((L, WS), lambda hp, r, n: (0, colblk(2, r) * ob + hp))]
    in_specs += [_bs((QB_A, WS), lambda hp, r, n: (n, r * ob + hp))] * 3
    in_specs += [_bs((None, 2 * PPS, QB_A, WIN_A), lambda hp, r, n: (_window_variant(n, nblk), hp, 0, 0))]
    out_shape = [jax.ShapeDtypeStruct((L, d * WIDTH_A), BF16), jax.ShapeDtypeStruct((L, d * WIDTH_A), F32),
                 jax.ShapeDtypeStruct((L, d * WIDTH_A), F32), jax.ShapeDtypeStruct((3, HEADS_A, QB_A, WIN_A), F32)]
    out_specs = [_bs((QB_A, WS), lambda hp, r, n: (n, r * ob + hp)),
                 _bs((L, WS), lambda hp, r, n: (0, r * ob + hp)), _bs((L, WS), lambda hp, r, n: (0, r * ob + hp)),
                 _bs((3, 2 * PPS, QB_A, WIN_A), lambda hp, r, n: (0, hp, 0, 0))]
    dq, dk, dv, db = pl.pallas_call(
        body, out_shape=out_shape, grid=(ob, d, nblk), in_specs=in_specs, out_specs=out_specs,
        compiler_params=_params("arbitrary", "arbitrary", "arbitrary"), name=f"a_bwd_d{d}")(
            pv, pv, pv, do, lse, cterm, bias)
    return dq, dk, dv, db


def _assemble_dproj(a_parts, dq_b, dk_b, dv_b, dga, dgb):
    T = dq_b.shape[0]
    flat = [(a_parts[part][g], d) for part in range(3) for g, d in enumerate(DILATIONS)]
    rest = [dq_b, dk_b, dv_b, dga, dgb]

    def body(*refs):
        views, others = refs[:len(flat)], refs[len(flat):len(flat) + len(rest)]
        o_ref, chunks = refs[len(flat) + len(rest)], refs[len(flat) + len(rest) + 1:]
        col = 0
        for v_ref, (_, d) in zip(views, flat):
            _view_to_rows(v_ref, o_ref, col, d, chunks)
            col += WIDTH_A
        for x_ref in others:
            w = x_ref.shape[1]
            o_ref[:, col:col + w] = x_ref[...].astype(o_ref.dtype)
            col += w

    in_specs = [_bs((VIEW_ROWS // d, d * WIDTH_A), lambda i: (i, 0)) for _, d in flat]
    in_specs += [_bs((VIEW_ROWS, x.shape[1]), lambda i: (i, 0)) for x in rest]
    return pl.pallas_call(
        body, out_shape=jax.ShapeDtypeStruct((T, IN_WIDTH), BF16), grid=(T // VIEW_ROWS,), in_specs=in_specs,
        out_specs=_bs((VIEW_ROWS, IN_WIDTH), lambda i: (i, 0)), scratch_shapes=_view_chunks(),
        compiler_params=_params("parallel"), name="mix_bwd_dproj")(*[a for a, _ in flat], *rest)


def _segment_ones():
    i = np.arange(WIDTH_A)
    return jnp.asarray((i[:, None] // HEAD_A == i[None, :] // HEAD_A).astype(np.float32), dtype=BF16)


def _group_weights(l0, l1, l2):
    m = jnp.maximum(jnp.maximum(l0, l1), l2)
    e = [jnp.exp(l - m) for l in (l0, l1, l2)]
    z = e[0] + e[1] + e[2]
    return [ei / z for ei in e]


def _view_specs():
    return [_bs((VIEW_ROWS // d, d * WIDTH_A), lambda i: (i, 0)) for d in DILATIONS]


def _stage_tiles(n):
    return [pltpu.VMEM((VIEW_ROWS, WIDTH_A), F32)] * n


def _combine_fwd(outs, lses):
    T = outs[0].shape[0] * DILATIONS[0]
    n = len(DILATIONS)

    def body(*refs):
        o_refs, l_refs, oa_ref = refs[:n], refs[n:2 * n], refs[2 * n]
        o_st, l_st, chunks = refs[2 * n + 1:3 * n + 1], refs[3 * n + 1:4 * n + 1], refs[4 * n + 1:]
        for g, d in enumerate(DILATIONS):
            _view_to_rows(o_refs[g], o_st[g], 0, d, chunks)
            _view_to_rows(l_refs[g], l_st[g], 0, d, chunks)
        w = _group_weights(*[l[...] for l in l_st])
        oa_ref[...] = (w[0] * o_st[0][...] + w[1] * o_st[1][...] + w[2] * o_st[2][...]).astype(oa_ref.dtype)

    return pl.pallas_call(
        body, out_shape=jax.ShapeDtypeStruct((T, WIDTH_A), BF16), grid=(T // VIEW_ROWS,),
        in_specs=_view_specs() * 2, out_specs=_bs((VIEW_ROWS, WIDTH_A), lambda i: (i, 0)),
        scratch_shapes=_stage_tiles(2 * n) + _view_chunks(), compiler_params=_params("parallel"), name="a_combine")(*outs, *lses)


def _combine_bwd(doa, outs, lses):
    T = doa.shape[0]
    n = len(DILATIONS)

    def body(*refs):
        d_ref, o_refs, l_refs, seg_ref = refs[0], refs[1:n + 1], refs[n + 1:2 * n + 1], refs[2 * n + 1]
        do_refs, c_refs = refs[2 * n + 2:3 * n + 2], refs[3 * n + 2:4 * n + 2]
        o_st, l_st = refs[4 * n + 2:5 * n + 2], refs[5 * n + 2:6 * n + 2]
        tmp, chunks = refs[6 * n + 2], refs[6 * n + 3:]
        for g, d in enumerate(DILATIONS):
            _view_to_rows(o_refs[g], o_st[g], 0, d, chunks)
            _view_to_rows(l_refs[g], l_st[g], 0, d, chunks)
        dv = d_ref[...].astype(F32)
        w = _group_weights(*[l[...] for l in l_st])
        seg = seg_ref[...]
        tot = jnp.zeros(dv.shape, F32)
        for g in range(n):
            prod = w[g] * dv * o_st[g][...]
            hi = prod.astype(BF16)
            lo = (prod - hi.astype(F32)).astype(BF16)
            tot = tot + jnp.dot(hi, seg, preferred_element_type=F32) + jnp.dot(lo, seg, preferred_element_type=F32)
        for g, d in enumerate(DILATIONS):
            tmp[...] = w[g] * dv
            _rows_to_view(tmp, 0, do_refs[g], 0, d, chunks)
            tmp[...] = -w[g] * tot
            _rows_to_view(tmp, 0, c_refs[g], 0, d, chunks)

    views = [jax.ShapeDtypeStruct((T // d, d * WIDTH_A), dt) for dt in (BF16, F32) for d in DILATIONS]
    res = pl.pallas_call(
        body, out_shape=views, grid=(T // VIEW_ROWS,),
        in_specs=[_bs((VIEW_ROWS, WIDTH_A), lambda i: (i, 0))] + _view_specs() * 2 + [_bs((WIDTH_A, WIDTH_A), lambda i: (0, 0))],
        out_specs=_view_specs() * 2, scratch_shapes=_stage_tiles(2 * n + 1) + _view_chunks(),
        compiler_params=_params("parallel"), name="a_combine_bwd")(doa, *outs, *lses, _segment_ones())
    return res[:n], res[n:]


def _rope_tables(T):
    rows = T // GRID_W
    row = jnp.repeat(jnp.arange(rows, dtype=F32), GRID_W)
    col = jnp.tile(jnp.arange(GRID_W, dtype=F32), rows)
    n_freq = HEAD_B // 4
    freq = ROPE_THETA ** (-jnp.arange(n_freq, dtype=F32) / n_freq)
    ang = jnp.concatenate([row[:, None] * freq, col[:, None] * freq], axis=-1)
    cos, sin = jnp.repeat(jnp.cos(ang), 2, axis=1), jnp.repeat(jnp.sin(ang), 2, axis=1)
    sign = jnp.where(jnp.arange(HEAD_B) % 2 == 0, -1.0, 1.0).astype(F32)
    return cos, sin * sign


def _swap_pairs(v):
    even = lax.broadcasted_iota(jnp.int32, v.shape, v.ndim - 1) % 2 == 0
    n = v.shape[-1]
    return jnp.where(even, pltpu.roll(v, n - 1, v.ndim - 1), pltpu.roll(v, 1, v.ndim - 1))


def _qk_fwd(name, proj, col0, n_heads, gain, cos, sin, out_scale=1.0):
    T = proj.shape[0]

    def fn(xr, g, c, s):
        xn = _norm_fwd(xr.astype(F32), g)
        return (xn * c + _swap_pairs(xn) * s) * out_scale

    (out,) = _ew(name, fn, [_tiled(proj, HEAD_B, col0 // HEAD_B), _whole(gain), _table(cos), _table(sin)],
                 [(BF16, HEAD_B)], n_rows=T, rows=2048, ncols=n_heads)
    return out


def _qk_bwd(name, dout, proj, col0, n_heads, gain, cos, sin, in_scale=1.0):
    T = proj.shape[0]

    def fn(dv, xr, g, c, s):
        dv = dv.astype(F32) * in_scale
        dxn = c * dv + _swap_pairs(s * dv)
        dx, dgr = _norm_bwd(xr.astype(F32), g, dxn)
        return dx, _colsum(dgr)

    dx, dg = _ew(name, fn, [_tiled(dout, HEAD_B, 0), _tiled(proj, HEAD_B, col0 // HEAD_B), _whole(gain),
                            _table(cos), _table(sin)],
                 [(BF16, HEAD_B)], n_rows=T, rows=2048, reds=(HEAD_B,), ncols=n_heads)
    return dx, jnp.sum(dg, axis=0)


def _gqa_fwd(qn, kn, proj):
    T = qn.shape[0]
    GW = 4 * HEAD_B
    QB = QB_B

    def body(q_ref, k_ref, v_ref, o_ref, l_ref):
        k, v = k_ref[...], v_ref[...]
        lane = lax.broadcasted_iota(jnp.int32, (QB, HEAD_B), 1)
        heads = range(4)
        s = [lax.dot_general(q_ref[:, g * HEAD_B:(g + 1) * HEAD_B], k, (NT, ((), ())), preferred_element_type=F32)
             for g in heads]
        m = [jnp.max(x, axis=-1, keepdims=True) for x in s]
        p = [jnp.exp2(x - mx) for x, mx in zip(s, m)]
        l = [jnp.sum(x, axis=-1, keepdims=True) for x in p]
        o = [jnp.dot(p[g].astype(BF16), v, preferred_element_type=F32) / l[g] for g in heads]
        o_ref[...] = jnp.concatenate(o, axis=1).astype(o_ref.dtype)
        lse_all = jnp.zeros((QB, HEAD_B), F32)
        for g in heads:
            lse_all = jnp.where(lane == g, m[g] + jnp.log2(l[g]), lse_all)
        l_ref[...] = lse_all

    return pl.pallas_call(
        body, out_shape=[jax.ShapeDtypeStruct((T, 2 * GW), BF16), jax.ShapeDtypeStruct((2, T, HEAD_B), F32)],
        grid=(2, T // QB),
        in_specs=[_bs((QB, GW), lambda kv, i: (i, kv)), _bs((T, HEAD_B), lambda kv, i: (0, kv)),
                  _bs((T, HEAD_B), lambda kv, i: (0, B_V // HEAD_B + kv))],
        out_specs=[_bs((QB, GW), lambda kv, i: (i, kv)), _bs((None, QB, HEAD_B), lambda kv, i: (kv, i, 0))],
        compiler_params=_params("parallel", "parallel"), name="b_fwd")(qn, kn, proj)


def _gqa_bwd(qn, kn, proj, o, lse, do, deps=()):
    T = qn.shape[0]
    GW = 4 * HEAD_B

    def body(q_ref, k_ref, v_ref, o_ref, l_ref, do_ref, *rest):
        dq_ref, dk_ref, dv_ref = rest[-3:]
        i = pl.program_id(1)

        @pl.when(i == 0)
        def _():
            dk_ref[...] = jnp.zeros_like(dk_ref)
            dv_ref[...] = jnp.zeros_like(dv_ref)

        k, v = k_ref[...], v_ref[...]
        lse_all = l_ref[...]
        for g in range(4):
            cols = slice(g * HEAD_B, (g + 1) * HEAD_B)
            q, dob = q_ref[:, cols], do_ref[:, cols]
            delta = jnp.sum(dob.astype(F32) * o_ref[:, cols].astype(F32), axis=-1, keepdims=True)
            s = lax.dot_general(q, k, (NT, ((), ())), preferred_element_type=F32)
            p = jnp.exp2(s - lse_all[:, g:g + 1])
            dp = lax.dot_general(dob, v, (NT, ((), ())), preferred_element_type=F32)
            ds = (p * (dp - delta)).astype(BF16)
            dq_ref[:, cols] = jnp.dot(ds, k, preferred_element_type=F32).astype(dq_ref.dtype)
            dk_ref[...] += lax.dot_general(ds, q, (TN, ((), ())), preferred_element_type=F32)
            dv_ref[...] += lax.dot_general(p.astype(BF16), dob, (TN, ((), ())), preferred_element_type=F32)

    return pl.pallas_call(
        body, out_shape=[jax.ShapeDtypeStruct((T, 2 * GW), BF16), jax.ShapeDtypeStruct((T, 2 * HEAD_B), F32),
                         jax.ShapeDtypeStruct((T, 2 * HEAD_B), F32)],
        grid=(2, T // QB_B),
        in_specs=[_bs((QB_B, GW), lambda kv, i: (i, kv)), _bs((T, HEAD_B), lambda kv, i: (0, kv)),
                  _bs((T, HEAD_B), lambda kv, i: (0, B_V // HEAD_B + kv)), _bs((QB_B, GW), lambda kv, i: (i, kv)),
                  _bs((None, QB_B, HEAD_B), lambda kv, i: (kv, i, 0)), _bs((QB_B, GW), lambda kv, i: (i, kv))] + _any_specs(len(deps)),
        out_specs=[_bs((QB_B, GW), lambda kv, i: (i, kv)), _bs((T, HEAD_B), lambda kv, i: (0, kv)),
                   _bs((T, HEAD_B), lambda kv, i: (0, kv))],
        compiler_params=_params("parallel", "arbitrary"), name="b_bwd")(qn, kn, proj, o, lse, do, *deps)


def _local_step(x, target, small, get_w, put_g, deps=()):
    T, D = x.shape
    gs = {}

    bias = _bias_tiles(small["rel_bias"])
    cos, sin = _rope_tables(T)
    (x1, h2), ffn1_saved = _ffn_fwd("ffn1", x, small["ffn1_norm"], lambda name, after: get_w(name, [after, bias, cos, sin]), deps,
                                    tail_ins=[small["mix_norm"]], tail_fn=lambda y, g: (y, _norm_fwd(y, g)), tail_outs=(F32, BF16))
    w_in = get_w("w_in", h2)
    nq = w_in.shape[2]
    tpq = nq // WIDTH_A

    def proj_tile(j, k):
        c = j * tpq + k
        return jnp.where(c < 3 * len(DILATIONS), (c % 3) * 3 + c // 3, c)

    proj = _mm("mix_in", (4, tpq),
               [(h2, _resident((T, D), lambda j, k: (0, 0)), w_in, _bs((None, D, WIDTH_A), lambda j, k: (j, 0, k)))],
               jax.ShapeDtypeStruct((T, IN_WIDTH), BF16), _bs((T, WIDTH_A), lambda j, k: (0, proj_tile(j, k))), NN)

    a_views = [_group_view(proj, grp, d) for grp, d in enumerate(DILATIONS)]
    a_outs, a_lses = [], []
    for grp, d in enumerate(DILATIONS):
        o, l = _dil_fwd(a_views[grp], bias[grp], d)
        a_outs.append(o)
        a_lses.append(l)
    o_a = _combine_fwd(a_outs, a_lses)

    qn =_qk_fwd("b_qnorm", proj, B_Q, 8, small["q_norm"], cos, sin, out_scale=QK_SCALE_LOG2)
    kn = _qk_fwd("b_knorm", proj, B_K, 2, small["k_norm"], cos, sin)
    o_b, lse_b = _gqa_fwd(qn, kn, proj)

    wa, wb, wo = get_w("w_branch_a", o_b), get_w("w_branch_b", o_b), get_w("w_out", o_b)
    bg_a, bg_b = small["b_gate"][:, :D], small["b_gate"][:, D:]
    n_a = wa.shape[0]

    def merge_out(oa_ref, ob_ref, ga_ref, gb_ref, x1_ref, wa_ref, wb_ref, wo_ref, ba_ref, bb_ref, g2_ref,
                  ta_ref, tb_ref, mg_ref, x2_ref, hn_ref):
        oa = oa_ref[...]
        ta = jnp.concatenate([jnp.dot(oa, wa_ref[j], preferred_element_type=F32) for j in range(n_a)], axis=1)
        tb = jnp.dot(ob_ref[...], wb_ref[...], preferred_element_type=F32)
        sa = _sigmoid(ga_ref[...].astype(F32) + ba_ref[...])
        sb = _sigmoid(gb_ref[...].astype(F32) + bb_ref[...])
        merged = (sa * ta + sb * tb).astype(BF16)
        ta_ref[...], tb_ref[...], mg_ref[...] = ta.astype(BF16), tb.astype(BF16), merged
        y = x1_ref[...] + jnp.dot(merged, wo_ref[...], preferred_element_type=F32)
        x2_ref[...] = y
        hn_ref[...] = _norm_fwd(y, g2_ref[...]).astype(BF16)

    row = _bs((512, D), lambda i: (i, 0))
    gate_specs = [_bs((512, D), lambda i: (i, G_A // D)), _bs((512, D), lambda i: (i, G_B // D))]
    whole2, whole3 = (lambda i: (0, 0)), (lambda i: (0, 0, 0))
    vec = _bs((1, D), whole2)
    t_a, t_b, merged, x2, hn2 = pl.pallas_call(
        merge_out, out_shape=[jax.ShapeDtypeStruct((T, D), BF16)] * 3 + [jax.ShapeDtypeStruct((T, D), F32), jax.ShapeDtypeStruct((T, D), BF16)],
        grid=(T // 512,),
        in_specs=[_bs((512, WIDTH_A), lambda i: (i, 0)), row] + gate_specs + [row, _resident(wa.shape, whole3), _resident((D, D), whole2),
                                                                                _resident((D, D), whole2), vec, vec, vec],
        out_specs=[row] * 5, compiler_params=_params("parallel"), name="mix_merge_out")(
            o_a, o_b, proj, proj, x1, wa, wb, wo, bg_a, bg_b, small["ffn2_norm"])

    def head(xv, g, tv):
        r = _rstd(xv)
        xh = xv * r
        e = xh * g - tv
        dy = e * (1.0 / D)
        dxh = dy * g
        dx = r * (dxh - xh * jnp.mean(dxh * xh, axis=-1, keepdims=True))
        return dx, 0.5 * dx, _colsum(e * e) * (0.5 / D), _colsum(dy * xh)

    (dx3, dx3_half, loss_cols, g_final), ffn2_saved = _ffn_fwd(
        "ffn2", x2, small["ffn2_norm"], get_w, h=hn2, tail_ins=[small["final_norm"].reshape(1, D), target], tail_fn=head,
        tail_outs=(F32, BF16), tail_reds=(D, D))
    gs["final_norm"] = g_final.reshape(D)

    dx2, _, dmix, gs["ffn2_norm"] = _ffn_bwd("ffn2", x2, small["ffn2_norm"], get_w, put_g, ffn2_saved, dx3, dx3_half,
                                             also_bf16=True)
    g_out = _mm_wgrad("mix_bwd_dwout", merged, dmix, a_cols=D // 4, b_cols=None, tm=256, tn=512, J=4).reshape(D, D)

    def merge_out_bwd(dx_ref, ta_ref, tb_ref, ga_ref, gb_ref, wa_ref, wb_ref, wo_ref, ba_ref, bb_ref,
                      dta_ref, dtb_ref, dga_ref, dgb_ref, doa_ref, dob_ref, dba_ref, dbb_ref):
        dm = lax.dot_general(dx_ref[...], wo_ref[...], (NT, ((), ())), preferred_element_type=F32)
        ta, tb = ta_ref[...].astype(F32), tb_ref[...].astype(F32)
        sa = _sigmoid(ga_ref[...].astype(F32) + ba_ref[...])
        sb = _sigmoid(gb_ref[...].astype(F32) + bb_ref[...])
        dga, dgb = dm * ta * sa * (1.0 - sa), dm * tb * sb * (1.0 - sb)
        dta, dtb = (dm * sa).astype(BF16), (dm * sb).astype(BF16)
        dta_ref[...], dtb_ref[...] = dta, dtb
        dga_ref[...], dgb_ref[...] = dga.astype(BF16), dgb.astype(BF16)
        w = wa_ref.shape[2]
        doa = sum(lax.dot_general(dta[:, j * w:(j + 1) * w], wa_ref[j], (NT, ((), ())), preferred_element_type=F32) for j in range(n_a))
        doa_ref[...] = doa.astype(BF16)
        dob_ref[...] = lax.dot_general(dtb, wb_ref[...], (NT, ((), ())), preferred_element_type=F32).astype(BF16)

        @pl.when(pl.program_id(0) == 0)
        def _():
            dba_ref[...] = jnp.zeros_like(dba_ref)
            dbb_ref[...] = jnp.zeros_like(dbb_ref)
        dba_ref[...] += _colsum(dga)
        dbb_ref[...] += _colsum(dgb)

    rowb = _bs((256, D), lambda i: (i, 0))
    gate_specs = [_bs((256, D), lambda i: (i, G_A // D)), _bs((256, D), lambda i: (i, G_B // D))]
    dta, dtb, dga, dgb, do_a, do_b, dba, dbb = pl.pallas_call(
        merge_out_bwd,
        out_shape=[jax.ShapeDtypeStruct((T, D), BF16)] * 4 + [jax.ShapeDtypeStruct((T, WIDTH_A), BF16), jax.ShapeDtypeStruct((T, D), BF16)]
        + [jax.ShapeDtypeStruct((1, D), F32)] * 2,
        grid=(T // 256,),
        in_specs=[rowb, rowb, rowb] + gate_specs + [_resident(wa.shape, whole3), _resident((D, D), whole2), _resident((D, D), whole2), vec, vec],
        out_specs=[rowb] * 4 + [_bs((256, WIDTH_A), lambda i: (i, 0)), rowb, vec, vec],
        compiler_params=_params("arbitrary"), name="mix_merge_out_bwd")(dmix, t_a, t_b, proj, proj, wa, wb, wo, bg_a, bg_b)
    gs["b_gate"] = jnp.concatenate([dba, dbb], axis=1)

    g_a = _mm_wgrad("mix_bwd_dwa", o_a, dta, a_cols=None, b_cols=D // 4, tm=WIDTH_A, tn=256, J=4)
    g_b = _mm_wgrad("mix_bwd_dwb", o_b, dtb, a_cols=D // 4, b_cols=None, tm=256, tn=512, J=4).reshape(D, D)
    deps = put_g({"w_out": g_out, "w_branch_a": g_a, "w_branch_b": g_b})

    dqn, dkn, dv_b = _gqa_bwd(qn, kn, proj, o_b, lse_b, do_b, deps)
    dq_b, gs["q_norm"] = _qk_bwd("b_bwd_qnorm", dqn, proj, B_Q, 8, small["q_norm"], cos, sin, in_scale=HEAD_B ** -0.5)
    dk_b, gs["k_norm"] = _qk_bwd("b_bwd_knorm", dkn, proj, B_K, 2, small["k_norm"], cos, sin, in_scale=1.0 / LOG2_E)

    do_groups, c_groups = _combine_bwd(do_a, a_outs, a_lses)
    dqs, dks, dvs, dbs = [], [], [], []
    for grp, d in enumerate(DILATIONS):
        dq, dk, dv, db = _dil_bwd(a_views[grp], bias[grp], do_groups[grp], a_lses[grp], c_groups[grp], d)
        dqs.append(dq), dks.append(dk), dvs.append(dv), dbs.append(db)
    gs["rel_bias"] = _bias_grad(jnp.stack(dbs))

    dproj = _assemble_dproj([dqs, dks, dvs], dq_b, dk_b, dv_b, dga, dgb)
    nq = w_in.shape[2]
    g_in = _mm("mix_bwd_dwin", (4, tpq),
               [(h2, _resident((T, D), lambda j, k: (0, 0)), dproj, _bs((T, WIDTH_A), lambda j, k: (0, j * tpq + k)))],
               jax.ShapeDtypeStruct((4, D, nq), BF16), _bs((None, D, WIDTH_A), lambda j, k: (j, 0, k)), TN)
    deps = put_g({"w_in": g_in})
    dx1, dx1_half, gs["mix_norm"] = _dh_norm_bwd(
        "mix_bwd_dh", 256,
        [(dproj, _bs((256, nq), lambda i, j=j: (i, j)), w_in, _resident((None, D, nq), lambda i, j=j: (j, 0, 0))) for j in range(4)],
        NT, x1, small["mix_norm"], dx2, deps)

    dx0, _, gs["ffn1_norm"] = _ffn_bwd("ffn1", x, small["ffn1_norm"], get_w, put_g, ffn1_saved, dx1, dx1_half)
    return loss_cols, dx0, gs


def _position():
    return lax.axis_index("x"), lax.axis_index("y"), lax.axis_index("c")


def _any_specs(n):
    return [pl.BlockSpec(memory_space=pl.ANY)] * n


HBM_SPEC = pl.BlockSpec(memory_space=pltpu.HBM)
SEM_SPEC = pl.BlockSpec(memory_space=pltpu.SEMAPHORE)
DATAFLOW_EFFECT = pltpu.SideEffectType.DATAFLOW_SIDE_EFFECTING
N_PEER_CHIPS = 3
LANES = 128


def _quarter_copies(srcs, lands, send_sems, recv_sems, mode):
    x, y, c = _position()
    me = 2 * x + y
    peers = [(1 - x, y, c), (x, 1 - y, c), (1 - x, 1 - y, c)]
    copies = []
    for src, land, send, recv in zip(srcs, lands, send_sems, recv_sems):
        if mode == "sibling":
            copies.append(pltpu.make_async_remote_copy(src_ref=src, dst_ref=land, send_sem=send.at[0], recv_sem=recv.at[0],
                                                       device_id=(x, y, 1 - c), device_id_type=MESH))
            continue
        scatter = mode == "scatter"
        half = land.shape[1] // 2
        mine = land.at[me, pl.ds(c * half, half)]
        for p, (px, py, pc) in enumerate(peers):
            copies.append(pltpu.make_async_remote_copy(
                src_ref=src.at[2 * px + py] if scatter else mine, dst_ref=land.at[me] if scatter else mine,
                send_sem=send.at[p], recv_sem=recv.at[p], device_id=(px, py, pc), device_id_type=MESH))
    return copies


def _fill_from_sibling(name, stacks):
    n = len(stacks)

    def body(*refs):
        outs = refs[n:2 * n]
        send_sems, recv_sems = refs[2 * n:]
        x, y, c = _position()
        copies = []
        for i, ref in enumerate(outs):
            half = ref.shape[1] // 2
            rows = pl.ds(c * half, half)
            for p, k in enumerate((2 * (1 - x) + y, 2 * x + (1 - y), 2 * (1 - x) + (1 - y))):
                cp = pltpu.make_async_remote_copy(ref.at[k, rows], ref.at[k, rows], send_sems.at[3 * i + p], recv_sems.at[3 * i + p],
                                                  device_id=(x, y, 1 - c), device_id_type=MESH)
                cp.start()
                copies.append(cp)
        for cp in copies:
            cp.wait()

    return pl.pallas_call(
        body, out_shape=[jax.ShapeDtypeStruct(s.shape, s.dtype) for s in stacks],
        in_specs=_any_specs(n), out_specs=_any_specs(n), input_output_aliases={i: i for i in range(n)},
        scratch_shapes=[pltpu.SemaphoreType.DMA((N_PEER_CHIPS * n,)), pltpu.SemaphoreType.DMA((N_PEER_CHIPS * n,))],
        compiler_params=pltpu.CompilerParams(has_side_effects=True), name=name)(*stacks)


def _exchange_start(name, srcs, lands, mode):
    n = len(lands)
    arrays = list(lands) if srcs is None else list(srcs) + list(lands)
    k = len(arrays)

    def body(*refs):
        land_refs = refs[k - n:k]
        send_sems, recv_sems = refs[k:k + n], refs[k + n:k + 2 * n]
        token = refs[2 * k + 2 * n]
        for cp in _quarter_copies(refs[:n], land_refs, send_sems, recv_sems, mode):
            cp.start()
        token[...] = jnp.zeros_like(token)

    sem = pltpu.SemaphoreType.DMA((N_PEER_CHIPS,))
    out_shape = [sem] * (2 * n) + [pltpu.HBM(a.shape, a.dtype) for a in arrays] + [jax.ShapeDtypeStruct((8, LANES), F32)]
    res = pl.pallas_call(
        body, name=name, out_shape=out_shape, in_specs=[HBM_SPEC] * k,
        out_specs=[SEM_SPEC] * (2 * n) + [HBM_SPEC] * k + [pl.BlockSpec(memory_space=pltpu.VMEM)],
        input_output_aliases={i: 2 * n + i for i in range(k)},
        compiler_params=pltpu.CompilerParams(has_side_effects=DATAFLOW_EFFECT),
    )(*[pltpu.with_memory_space_constraint(a, pltpu.HBM) for a in arrays])
    thru = res[2 * n:2 * n + k]
    return res[:n], res[n:2 * n], (None if srcs is None else thru[:n]), thru[k - n:], res[2 * n + k]


def _exchange_wait(name, srcs, lands, send_sems, recv_sems, after, mode):
    n = len(lands)
    arrays = list(lands) if srcs is None else list(srcs) + list(lands)
    k = len(arrays)
    after = list(after) if isinstance(after, (list, tuple)) else [after]

    def body(*refs):
        sends, recvs = refs[k:k + n], refs[k + n:k + 2 * n]
        for cp in _quarter_copies(refs[:n], refs[k - n:k], sends, recvs, mode):
            cp.wait_send()
            cp.wait_recv()

    res = pl.pallas_call(
        body, name=name, out_shape=[pltpu.HBM(a.shape, a.dtype) for a in arrays],
        in_specs=[HBM_SPEC] * k + [SEM_SPEC] * (2 * n) + _any_specs(len(after)),
        out_specs=[HBM_SPEC] * k, input_output_aliases={i: i for i in range(k)},
        compiler_params=pltpu.CompilerParams(has_side_effects=DATAFLOW_EFFECT),
    )(*arrays, *send_sems, *recv_sems, *after)
    return (None if srcs is None else res[:n]), res[k - n:]


def _own_slots(name, srcs, from_stack=False):
    n = len(srcs)
    me = (2 * lax.axis_index("x") + lax.axis_index("y")).astype(jnp.int32).reshape(1)

    def body(me_ref, *refs):
        for x_ref, o_ref in zip(refs[:n], refs[n:]):
            o_ref[...] = x_ref[...].astype(o_ref.dtype)

    in_specs, out_specs, out_shape = [], [], []
    for src in srcs:
        R, C = src.shape[-2:]
        in_specs.append(pl.BlockSpec((None, R // 2, C), lambda i, me_ref: (me_ref[0], i, 0)) if from_stack
                        else pl.BlockSpec((R // 2, C), lambda i, me_ref: (i, 0)))
        out_specs.append(pl.BlockSpec((None, R // 2, C), lambda i, me_ref: (me_ref[0], i, 0)))
        out_shape.append(jax.ShapeDtypeStruct((4, R, C), BF16))
    grid_spec = pltpu.PrefetchScalarGridSpec(num_scalar_prefetch=1, grid=(2,), in_specs=in_specs, out_specs=out_specs)
    return pl.pallas_call(body, out_shape=out_shape, grid_spec=grid_spec, compiler_params=_params("parallel"), name=name)(me, *srcs)


def _allreduce_small(buf):
    R, C = buf.shape
    flips = [(fx, fy, fc) for fx in (0, 1) for fy in (0, 1) for fc in (0, 1)][1:]

    def body(in_ref, out_ref, land_ref, send_sems, recv_sems):
        x, y, c = _position()
        me = 4 * x + 2 * y + c
        copies = []
        for k, (fx, fy, fc) in enumerate(flips):
            px, py, pc = (1 - x if fx else x), (1 - y if fy else y), (1 - c if fc else c)
            cp = pltpu.make_async_remote_copy(in_ref, land_ref.at[me], send_sems.at[k], recv_sems.at[k],
                                              device_id=(px, py, pc), device_id_type=MESH)
            cp.start()
            copies.append(cp)
        land_ref[me] = in_ref[...]
        for cp in copies:
            cp.wait()
        acc = land_ref[0]
        for k in range(1, 8):
            acc = acc + land_ref[k]
        out_ref[...] = acc

    return pl.pallas_call(
        body, out_shape=jax.ShapeDtypeStruct((R, C), F32),
        in_specs=[pl.BlockSpec(memory_space=pltpu.VMEM)], out_specs=pl.BlockSpec(memory_space=pltpu.VMEM),
        scratch_shapes=[pltpu.VMEM((8, R, C), F32), pltpu.SemaphoreType.DMA((7,)), pltpu.SemaphoreType.DMA((7,))],
        compiler_params=pltpu.CompilerParams(has_side_effects=True), name="allreduce_small")(buf)


def _adamw_math(w, g, m, v):
    m2 = ADAM_B1 * m + (1.0 - ADAM_B1) * g
    v2 = ADAM_B2 * v + (1.0 - ADAM_B2) * (g * g)
    m_hat = m2 / (1.0 - ADAM_B1 ** ADAM_STEP)
    v_hat = v2 / (1.0 - ADAM_B2 ** ADAM_STEP)
    delta = -ADAM_LR * (m_hat / (jnp.sqrt(v_hat) + ADAM_EPS) + ADAM_WD * w)
    return delta, m2, v2


def _adamw_big(name, w, m, v, mine, theirs):
    R, C = w.shape
    rows = 256 if R % 256 == 0 else R // 2
    nrb = R // rows

    def four(a, b, c, d):
        return ((a.astype(F32) + b.astype(F32)) + c.astype(F32)) + d.astype(F32)

    def fn(wv, mv, vv, *parts):
        g = four(*parts[:4]) + four(*parts[4:])
        return (g,) + _adamw_math(wv, g, mv, vv)

    slots = [_tiled(s.reshape(4 * R, C), None, 0, k * nrb) for s in (mine, theirs) for k in range(4)]
    return _ew(name, fn, [_tiled(w), _tiled(m), _tiled(v)] + slots, [(F32, C)] * 4, n_rows=R, rows=rows)


BIG = ("ffn1_w1", "ffn1_w3", "ffn1_w2", "w_in", "w_branch_a", "w_branch_b", "w_out", "ffn2_w1", "ffn2_w3", "ffn2_w2")
SMALL = ("ffn1_norm", "mix_norm", "b_gate", "q_norm", "k_norm", "rel_bias", "ffn2_norm", "final_norm")
ORDER = ("ffn1_norm", "ffn1_w1", "ffn1_w3", "ffn1_w2", "mix_norm", "w_in", "b_gate", "q_norm", "k_norm", "rel_bias",
         "w_branch_a", "w_branch_b", "w_out", "ffn2_norm", "ffn2_w1", "ffn2_w3", "ffn2_w2", "final_norm")
TRANSPOSED = ("ffn1_w1", "ffn1_w3", "ffn2_w1", "ffn2_w3")
SIBLING_LAG = 2
GATHER_GROUPS = (("ffn1_w1", "ffn1_w3"), ("ffn1_w2",), ("w_in",), ("w_branch_a", "w_branch_b", "w_out"),
                 ("ffn2_w1", "ffn2_w3", "ffn2_w2"))


def _pack_small(d):
    rows = []
    for n in SMALL:
        flat = d[n].reshape(-1)
        pad = (-flat.shape[0]) % LANES
        rows.append(jnp.pad(flat, (0, pad)).reshape(-1, LANES))
    buf = jnp.concatenate(rows, axis=0)
    return jnp.pad(buf, ((0, (-buf.shape[0]) % 8), (0, 0)))


def _unpack_small(buf, like):
    out, r = {}, 0
    for n in SMALL:
        size = like[n].size
        nr = -(-size // LANES)
        out[n] = buf[r:r + nr].reshape(-1)[:size].reshape(like[n].shape)
        r += nr
    return out


def kernel(x, ffn1_norm, ffn1_w1, ffn1_w3, ffn1_w2, mix_norm, w_in, b_gate, q_norm, k_norm, rel_bias, w_branch_a, w_branch_b, w_out, ffn2_norm, ffn2_w1, ffn2_w3, ffn2_w2, final_norm, loss_target, m_ffn1_norm, m_ffn1_w1, m_ffn1_w3, m_ffn1_w2, m_mix_norm, m_w_in, m_b_gate, m_q_norm, m_k_norm, m_rel_bias, m_w_branch_a, m_w_branch_b, m_w_out, m_ffn2_norm, m_ffn2_w1, m_ffn2_w3, m_ffn2_w2, m_final_norm, v_ffn1_norm, v_ffn1_w1, v_ffn1_w3, v_ffn1_w2, v_mix_norm, v_w_in, v_b_gate, v_q_norm, v_k_norm, v_rel_bias, v_w_branch_a, v_w_branch_b, v_w_out, v_ffn2_norm, v_ffn2_w1, v_ffn2_w3, v_ffn2_w2, v_final_norm):
    given = dict(locals())
    w = {n: given[n] for n in ORDER}
    m = {n: given["m_" + n] for n in ORDER}
    v = {n: given["v_" + n] for n in ORDER}
    T, D = x.shape[1], x.shape[2]

    def stored(a, n):
        a = a.reshape(a.shape[1:])
        return a.T if n in TRANSPOSED else a

    def returned(a, n):
        return (a.T if n in TRANSPOSED else a).reshape(w[n].shape)

    quarter = {n: stored(w[n], n) for n in BIG}
    send, recv, _, land_thru, token = _exchange_start(
        "gather_start", None, _own_slots("own_weights", [quarter[n] for n in BIG]), "gather")
    index = {n: i for i, n in enumerate(BIG)}
    ready = {}

    def get_w(name, after):
        if name not in ready:
            group = next(g for g in GATHER_GROUPS if name in g)
            ids = [index[n] for n in group]
            _, stacks = _exchange_wait("gather_wait_" + group[0], None, [land_thru[i] for i in ids],
                                       [send[i] for i in ids], [recv[i] for i in ids], after, "gather")
            stacks = _fill_from_sibling("gather_fill_" + group[0], stacks)
            for n, st in zip(group, stacks):
                ready[n] = st.reshape(D, D) if n in ("w_branch_b", "w_out") else st
        return ready[name]

    scattered, forwarded = [], []

    def forward_oldest(after):
        names, s_sem, r_sem, srcs, lands = scattered.pop(0)
        _, landed = _exchange_wait("scatter_wait_" + names[0], srcs, lands, s_sem, r_sem, after, "scatter")
        started = _exchange_start("sibling_start_" + names[0], landed, [lax.empty(a.shape, a.dtype) for a in landed], "sibling")
        forwarded.append((names,) + tuple(started[:4]))
        return started[4]

    def put_g(grads):
        names = list(grads)
        stacks = [grads[n].reshape((4,) + quarter[n].shape) for n in names]
        lands = _own_slots("own_grad_" + names[0], stacks, from_stack=True)
        started = _exchange_start("scatter_start_" + names[0], stacks, lands, "scatter")
        scattered.append((names,) + tuple(started[:4]))
        tokens = [started[4]]
        if len(scattered) > SIBLING_LAG:
            tokens.append(forward_oldest(started[4]))
        return tokens

    small = {n: w[n] for n in SMALL}
    packed = [_pack_small({n: d[n] for n in SMALL}) for d in (w, m, v)]
    loss_cols, grad_x, gs = _local_step(x.reshape(T, D), loss_target.reshape(T, D), small, get_w, put_g, deps=[token] + packed)

    after = grad_x
    while scattered:
        after = forward_oldest(after)
    grads, deltas, new_m, new_v = {}, {}, {}, {}
    for names, s_sem, r_sem, srcs, lands in forwarded:
        mine, theirs = _exchange_wait("sibling_wait_" + names[0], srcs, lands, s_sem, r_sem, after, "sibling")
        for n, a, b in zip(names, mine, theirs):
            res = _adamw_big(f"adamw_{n}", quarter[n], stored(m[n], n), stored(v[n], n), a, b)
            grads[n], deltas[n], new_m[n], new_v[n] = [returned(r, n) for r in res]

    gs = {n: gs[n].reshape(w[n].shape) for n in SMALL}
    packed_g = _pack_small(gs)
    n_small = packed_g.shape[0]
    summed = _allreduce_small(jnp.concatenate([packed_g, loss_cols.reshape(-1, LANES)], axis=0))
    g_small, loss = summed[:n_small], jnp.sum(summed[n_small:])
    R = g_small.shape[0]
    res = _ew("adamw_small", lambda wv, mv, vv, g: (g,) + _adamw_math(wv, g, mv, vv),
              [_tiled(packed[0]), _tiled(packed[1]), _tiled(packed[2]), _tiled(g_small)], [(F32, LANES)] * 4, n_rows=R, rows=R)
    for d, buf in zip((grads, deltas, new_m, new_v), res):
        d.update(_unpack_small(buf, w))

    return (loss, grad_x.reshape(x.shape), *[grads[n] for n in ORDER], *[deltas[n] for n in ORDER],
            *[new_m[n] for n in ORDER], *[new_v[n] for n in ORDER])
```

```python
import functools
import math

import numpy as np
import jax
import jax.numpy as jnp
from jax import lax
from jax.experimental import pallas as pl
from jax.experimental.pallas import tpu as pltpu

F32 = jnp.float32
BF16 = jnp.bfloat16
MESH = pl.DeviceIdType.MESH

NEG_INF = -1e30
EPS = 1e-6
GRID_W = 64
ROPE_THETA = 10000.0
DILATIONS = (1, 4, 16)
BAND_HALF = 64
HEAD_A = 64
HEADS_A = 8
WIDTH_A = HEADS_A * HEAD_A
HEAD_B = 128
LOG2_E = math.log2(math.e)
QK_SCALE_LOG2 = HEAD_B ** -0.5 * LOG2_E
N_BUCKETS = 32
MAX_DISTANCE = 1024
ADAM_LR, ADAM_B1, ADAM_B2, ADAM_EPS, ADAM_WD, ADAM_STEP = 0.001, 0.9, 0.999, 1e-08, 0.01, 10

B_Q, B_K, B_V = 4608, 5632, 5888
G_A, G_B = 6144, 7168
IN_WIDTH = 8192

VMEM_LIMIT_BYTES = 56 * 1024 * 1024
QB_A = 128
QB_B = 256


def _params(*sem):
    return pltpu.CompilerParams(dimension_semantics=sem, vmem_limit_bytes=VMEM_LIMIT_BYTES)


def _bs(shape, fn):
    return pl.BlockSpec(shape, fn)


def _resident(shape, fn):
    return pl.BlockSpec(shape, fn, pipeline_mode=pl.Buffered(1))


def _mm(name, grid, pairs, out_shape, out_spec, dims, *, extras=(), epilogue=None, deps=(), reds=()):
    n_pairs, n_extra, n_deps = len(pairs), len(extras), len(deps)
    operands = [p[0] for p in pairs] + [p[2] for p in pairs] + [e[0] for e in extras] + list(deps)
    in_specs = [p[1] for p in pairs] + [p[3] for p in pairs] + [e[1] for e in extras] + _any_specs(n_deps)
    single = not isinstance(out_shape, (list, tuple))
    out_shapes = [out_shape] if single else list(out_shape)
    out_specs = [out_spec] if single else list(out_spec)
    n_out = len(out_shapes)
    out_shapes += [jax.ShapeDtypeStruct((1, w), F32) for w in reds]
    out_specs += [_bs((1, w), lambda *_: (0, 0)) for w in reds]

    def body(*refs):
        a_refs, b_refs = refs[:n_pairs], refs[n_pairs:2 * n_pairs]
        e_refs = refs[2 * n_pairs:2 * n_pairs + n_extra]
        o_refs = refs[2 * n_pairs + n_extra + n_deps:]
        acc = None
        for a_ref, b_ref in zip(a_refs, b_refs):
            t = lax.dot_general(a_ref[...], b_ref[...], (dims, ((), ())), preferred_element_type=F32)
            acc = t if acc is None else acc + t
        vals = acc if epilogue is None else epilogue(acc, *[e[...] for e in e_refs])
        if not isinstance(vals, (list, tuple)):
            vals = (vals,)
        for o_ref, v in zip(o_refs[:n_out], vals[:n_out]):
            o_ref[...] = v.astype(o_ref.dtype)
        if reds:
            first = functools.reduce(jnp.logical_and, [pl.program_id(ax) == 0 for ax in range(len(grid))])
            for r_ref, v in zip(o_refs[n_out:], vals[n_out:]):
                @pl.when(first)
                def _(r_ref=r_ref):
                    r_ref[...] = jnp.zeros_like(r_ref)
                r_ref[...] += v

    sem = ["arbitrary" if reds else "parallel"] * len(grid)
    res = pl.pallas_call(
        body, out_shape=out_shapes, grid=grid, in_specs=in_specs, out_specs=out_specs,
        compiler_params=_params(*sem), name=name)(*operands)
    return res[0] if (single and not reds) else res


NN = ((1,), (0,))
NT = ((1,), (1,))
TN = ((0,), (0,))


def _mm_cols(name, a, w, *, tm, tn, out_dtype, cat, extras=(), epilogue=None):
    M, K = a.shape
    J, _, n = w.shape
    tn = min(tn, n)
    nb = n // tn
    if cat:
        shape, spec = (M, J * n), _bs((tm, tn), lambda j, i, k: (i, j * nb + k))
    else:
        shape, spec = (J, M, n), _bs((None, tm, tn), lambda j, i, k: (j, i, k))
    ex = [(e, _bs((tm, tn), lambda j, i, k: (i, j * nb + k))) for e in extras]
    return _mm(name, (J, M // tm, nb),
               [(a, _bs((tm, K), lambda j, i, k: (i, 0)), w, _bs((None, K, tn), lambda j, i, k: (j, 0, k)))],
               jax.ShapeDtypeStruct(shape, out_dtype), spec, NN, extras=ex, epilogue=epilogue)


def _mm_rows_t(name, a, w, *, tm, out_dtype):
    M, N = a.shape
    J, f, _ = w.shape
    return _mm(name, (J, M // tm),
               [(a, _bs((tm, N), lambda j, i: (i, 0)), w, _bs((None, f, N), lambda j, i: (j, 0, 0)))],
               jax.ShapeDtypeStruct((J, M, f), out_dtype), _bs((None, tm, f), lambda j, i: (j, i, 0)), NT)


def _mm_wgrad(name, a, b, *, a_cols, b_cols, tm, tn, J):
    def pick(arr, cols, t):
        if arr.ndim == 3:
            T, c = arr.shape[1], arr.shape[2]
            t = min(t, c)
            return T, c, t, (lambda sel: _bs((None, T, t), lambda j, i, k: (j, 0, sel(i, k))))
        T = arr.shape[0]
        c = arr.shape[1] if cols is None else cols
        t = min(t, c)
        per = c // t
        if cols is None:
            if per == 1:
                return T, c, t, (lambda sel: _resident((T, t), lambda j, i, k: (0, 0)))
            return T, c, t, (lambda sel: _bs((T, t), lambda j, i, k: (0, sel(i, k))))
        return T, c, t, (lambda sel: _bs((T, t), lambda j, i, k: (0, j * per + sel(i, k))))
    _, ca, tm, mk_a = pick(a, a_cols, tm)
    _, cb, tn, mk_b = pick(b, b_cols, tn)
    return _mm(name, (J, ca // tm, cb // tn),
               [(a, mk_a(lambda i, k: i), b, mk_b(lambda i, k: k))],
               jax.ShapeDtypeStruct((J, ca, cb), BF16), _bs((None, tm, tn), lambda j, i, k: (j, i, k)), TN)


def _tiled(arr, width=None, col=0, rowblk=0):
    return ("t", arr, arr.shape[1] if width is None else width, col, rowblk)


def _table(arr):
    return ("f", arr)


def _whole(arr):
    return ("w", arr)


def _ew(name, fn, ins, outs, *, n_rows, rows, reds=(), ncols=1, deps=()):
    nrb = n_rows // rows
    n_deps = len(deps)
    operands, in_specs = [], []
    for spec in ins:
        if spec[0] == "t":
            _, arr, width, col, rowblk = spec
            step = 1 if ncols > 1 else 0
            in_specs.append(_bs((rows, width), lambda c, i, col=col, rowblk=rowblk, step=step: (rowblk + i, col + c * step)))
        elif spec[0] == "f":
            arr = spec[1]
            in_specs.append(_bs((rows, arr.shape[1]), lambda c, i: (i, 0)))
        else:
            arr = spec[1]
            nd = arr.ndim
            if nd == 3:
                in_specs.append(_bs((None,) + arr.shape[1:], lambda c, i: (c, 0, 0)))
            else:
                in_specs.append(_bs(arr.shape, lambda c, i, nd=nd: (0,) * nd))
        operands.append(arr)
    out_shapes = [jax.ShapeDtypeStruct((n_rows, ncols * w), dt) for dt, w in outs]
    out_specs = [_bs((rows, w), lambda c, i: (i, c)) for _, w in outs]
    out_shapes += [jax.ShapeDtypeStruct((ncols, 1, w), F32) for w in reds]
    out_specs += [_bs((None, 1, w), lambda c, i: (c, 0, 0)) for w in reds]
    n_in, n_out, n_red = len(ins), len(outs), len(reds)
    operands += list(deps)
    in_specs += _any_specs(n_deps)

    def body(*refs):
        vals = fn(*[r[...] for r in refs[:n_in]])
        if not isinstance(vals, (tuple, list)):
            vals = (vals,)
        o_refs = refs[n_in + n_deps:]
        for o_ref, v in zip(o_refs[:n_out], vals[:n_out]):
            o_ref[...] = v.astype(o_ref.dtype)
        if n_red:
            i = pl.program_id(1)
            for r_ref, v in zip(o_refs[n_out:], vals[n_out:]):
                @pl.when(i == 0)
                def _(r_ref=r_ref):
                    r_ref[...] = jnp.zeros_like(r_ref)
                r_ref[...] += v

    res = pl.pallas_call(
        body, out_shape=out_shapes, grid=(ncols, nrb), in_specs=in_specs, out_specs=out_specs,
        compiler_params=_params("parallel", "arbitrary" if n_red else "parallel"), name=name)(*operands)
    return res


def _colsum(v):
    return jnp.sum(v, axis=0, keepdims=True)


def _rstd(x):
    return lax.rsqrt(jnp.mean(x * x, axis=-1, keepdims=True) + EPS)


def _sigmoid(x):
    return pl.reciprocal(1.0 + jnp.exp(-x), approx=True)


def _norm_fwd(x, g):
    return x * _rstd(x) * g


def _norm_bwd(x, g, dy):
    r = _rstd(x)
    xh = x * r
    dxh = dy * g
    dx = r * (dxh - xh * jnp.mean(dxh * xh, axis=-1, keepdims=True))
    return dx, dy * xh


def _row_spec(arr, rows):
    if arr.shape[0] == 1:
        return _bs(arr.shape, lambda i: (0, 0))
    return _bs((rows, arr.shape[1]), lambda i: (i, 0))


def _ffn_fwd(tag, x, gain, get_w, deps=(), *, h=None, tail_ins=(), tail_fn=None, tail_outs=(F32,), tail_reds=()):
    T, D = x.shape
    if h is None:
        (h,) = _ew(f"{tag}_norm", lambda xv, g: _norm_fwd(xv, g), [_tiled(x), _whole(gain)], [(BF16, D)], n_rows=T, rows=512,
                   deps=deps)
    w1, w3 = get_w(f"{tag}_w1", h), get_w(f"{tag}_w3", h)
    J, f, _ = w1.shape
    tm = 1024

    def up(h_ref, w1_ref, w3_ref, u_ref, g_ref, a_ref):
        hv = h_ref[...]
        u = lax.dot_general(hv, w1_ref[...], (NT, ((), ())), preferred_element_type=F32)
        g = lax.dot_general(hv, w3_ref[...], (NT, ((), ())), preferred_element_type=F32)
        u_ref[...] = u.astype(BF16)
        g_ref[...] = g.astype(BF16)
        a_ref[...] = (u * _sigmoid(u) * g).astype(BF16)

    slab = _bs((None, tm, f), lambda j, i: (j, i, 0))
    w_spec = _bs((None, f, D), lambda j, i: (j, 0, 0))
    u, g, a = pl.pallas_call(
        up, out_shape=[jax.ShapeDtypeStruct((J, T, f), BF16)] * 3, grid=(J, T // tm),
        in_specs=[_bs((tm, D), lambda j, i: (i, 0)), w_spec, w_spec], out_specs=[slab] * 3,
        compiler_params=_params("parallel", "parallel"), name=f"{tag}_up")(h, w1, w3)
    w2 = get_w(f"{tag}_w2", a)
    def tail(acc, xv, *rest):
        y = xv + 0.5 * acc
        return y if tail_fn is None else tail_fn(y, *rest)

    row = _bs((512, D), lambda i: (i, 0))
    res = _mm(f"{tag}_down", (T // 512,),
              [(a, _bs((None, 512, f), lambda i, j=j: (j, i, 0)), w2, _resident((None, f, D), lambda i, j=j: (j, 0, 0)))
               for j in range(J)],
              [jax.ShapeDtypeStruct((T, D), dt) for dt in tail_outs], [row] * len(tail_outs), NN,
              extras=[(x, row)] + [(t, _row_spec(t, 512)) for t in tail_ins], epilogue=tail, reds=tail_reds)
    return res, (h, u, g, a)


def _dh_norm_bwd(name, rows, pairs, dims, x, gain, dres, deps, also_bf16=False):
    T, D = x.shape

    def epilogue(dh, xv, gv, dr):
        dx, dgr = _norm_bwd(xv, gv, dh)
        dx = dx + dr
        return (dx, 0.5 * dx) + ((dx,) if also_bf16 else ()) + (_colsum(dgr),)

    dts = [F32, BF16] + ([BF16] if also_bf16 else [])
    row = _bs((rows, D), lambda i: (i, 0))
    return _mm(name, (T // rows,), pairs, [jax.ShapeDtypeStruct((T, D), dt) for dt in dts], [row] * len(dts), dims,
               extras=[(x, row), (gain, _row_spec(gain, rows)), (dres, row)], epilogue=epilogue, deps=deps, reds=(D,))


def _ffn_bwd(tag, x, gain, get_w, put_g, saved, dy, dy_half, also_bf16=False):
    h, u, g, a = saved
    T, D = x.shape
    w1, w3, w2 = [get_w(f"{tag}_{n}", dy_half) for n in ("w1", "w3", "w2")]
    J, f, _ = w1.shape
    dw2 = _mm_wgrad(f"{tag}_bwd_dw2", a, dy_half, a_cols=None, b_cols=None, tm=f, tn=D, J=J)
    deps = put_g({f"{tag}_w2": dw2})
    tm = 1024

    def up_bwd(dy_ref, w2_ref, u_ref, g_ref, *rest):
        du_ref, dg_ref = rest[-2:]
        da = lax.dot_general(dy_ref[...], w2_ref[...], (NT, ((), ())), preferred_element_type=F32)
        uv, gv = u_ref[...].astype(F32), g_ref[...].astype(F32)
        s = _sigmoid(uv)
        du_ref[...] = (da * gv * (s * (1.0 + uv * (1.0 - s)))).astype(BF16)
        dg_ref[...] = (da * (uv * s)).astype(BF16)

    slab = _bs((None, tm, f), lambda j, i: (j, i, 0))
    du, dg = pl.pallas_call(
        up_bwd, out_shape=[jax.ShapeDtypeStruct((J, T, f), BF16)] * 2, grid=(J, T // tm),
        in_specs=[_bs((tm, D), lambda j, i: (i, 0)), _bs((None, f, D), lambda j, i: (j, 0, 0)), slab, slab] + _any_specs(len(deps)),
        out_specs=[slab] * 2, compiler_params=_params("parallel", "parallel"), name=f"{tag}_bwd_up")(dy_half, w2, u, g, *deps)
    dw1 = _mm_wgrad(f"{tag}_bwd_dw1", du, h, a_cols=None, b_cols=None, tm=f, tn=D, J=J)
    dw3 = _mm_wgrad(f"{tag}_bwd_dw3", dg, h, a_cols=None, b_cols=None, tm=f, tn=D, J=J)
    deps = deps + put_g({f"{tag}_w1": dw1, f"{tag}_w3": dw3})
    pairs = []
    for j in range(J):
        a_spec = _bs((None, 512, f), lambda i, j=j: (j, i, 0))
        w_spec = _resident((None, f, D), lambda i, j=j: (j, 0, 0))
        pairs += [(du, a_spec, w1, w_spec), (dg, a_spec, w3, w_spec)]
    return _dh_norm_bwd(f"{tag}_bwd_dh", 512, pairs, NN, x, gain, dy, deps, also_bf16)


def _t5_bucket(rel):
    n = N_BUCKETS // 2
    max_exact = n // 2
    ret = jnp.where(rel > 0, n, 0)
    a = jnp.abs(rel)
    af = jnp.maximum(a, 1).astype(F32)
    large = max_exact + (jnp.log(af / max_exact) / math.log(MAX_DISTANCE / max_exact) * (n - max_exact)).astype(jnp.int32)
    large = jnp.minimum(large, n - 1)
    return ret + jnp.where(a < max_exact, a, large)


WIN_A = QB_A + 2 * BAND_HALF
WIN_SHIFTS = (0, BAND_HALF, 2 * BAND_HALF)


def _window_variant(n, nblk):
    return jnp.where(n == 0, 0, jnp.where(n == nblk - 1, 2, 1))


def _window_start(n, nblk):
    return pl.multiple_of(jnp.clip(n * QB_A - BAND_HALF, 0, nblk * QB_A - WIN_A), BAND_HALF)


def _band_steps(xp=jnp):
    qi = xp.arange(QB_A, dtype=xp.int32)[None, :, None]
    kj = xp.arange(WIN_A, dtype=xp.int32)[None, None, :]
    return kj - qi - xp.asarray(WIN_SHIFTS, dtype=xp.int32)[:, None, None]


def _bias_tiles(rel_bias):
    wide = QB_A + 2 * WIN_SHIFTS[-1]
    qi = jnp.arange(QB_A, dtype=jnp.int32)[:, None]
    steps = jnp.arange(wide, dtype=jnp.int32)[None, :] - WIN_SHIFTS[-1] - qi
    buckets = jnp.stack([_t5_bucket(steps * d) for d in DILATIONS])
    inband = (jnp.abs(steps) <= BAND_HALF).astype(jnp.int32)
    n_heads = rel_bias.shape[1]

    def body(tab_ref, b_ref, m_ref, o_ref):
        hd = pl.program_id(0)
        bkt = b_ref[...]
        acc = jnp.zeros(bkt.shape, F32)
        for b in range(N_BUCKETS):
            acc = jnp.where(bkt == b, tab_ref[b, hd], acc)
        o_ref[...] = jnp.where(m_ref[...] > 0, acc, NEG_INF)

    base = pl.pallas_call(
        body, out_shape=jax.ShapeDtypeStruct((n_heads, QB_A, wide), F32), grid=(n_heads,),
        in_specs=[pl.BlockSpec(memory_space=pltpu.SMEM),
                  _bs((None, QB_A, wide), lambda hd: (hd // HEADS_A, 0, 0)),
                  _bs((QB_A, wide), lambda hd: (0, 0))],
        out_specs=_bs((None, QB_A, wide), lambda hd: (hd, 0, 0)),
        compiler_params=_params("parallel"), name="a_bias_tiles")(rel_bias, buckets, inband)
    base = base.reshape(len(DILATIONS), HEADS_A, QB_A, wide)
    return jnp.stack([base[..., WIN_SHIFTS[-1] - s:WIN_SHIFTS[-1] - s + WIN_A] for s in WIN_SHIFTS], axis=1)


def _bias_grad(dbias):
    steps = _band_steps(np)
    inband = np.abs(steps) <= BAND_HALF
    present = []
    for d in DILATIONS:
        rel = steps * d
        a = np.abs(rel)
        large = 8 + (np.log(np.maximum(a, 1) / 8.0) / math.log(MAX_DISTANCE / 8.0) * 8).astype(np.int64)
        bk = np.where(rel > 0, 16, 0) + np.where(a < 8, a, np.minimum(large, 15))
        present.append([sorted(set(bk[v][inband[v]].tolist())) for v in range(3)])
    buckets = jnp.stack([_t5_bucket(_band_steps() * d) for d in DILATIONS])
    n_heads = len(DILATIONS) * HEADS_A

    def body(b_ref, d_ref, o_ref):
        row = lax.broadcasted_iota(jnp.int32, (N_BUCKETS, n_heads), 0)
        col = lax.broadcasted_iota(jnp.int32, (N_BUCKETS, n_heads), 1)
        out = jnp.zeros((N_BUCKETS, n_heads), F32)
        for grp in range(len(DILATIONS)):
            for hh in range(HEADS_A):
                hd = grp * HEADS_A + hh
                for b in sorted(set(sum(present[grp], []))):
                    tot = jnp.zeros((), F32)
                    for v in range(3):
                        if b in present[grp][v]:
                            tot = tot + jnp.sum(jnp.where(b_ref[grp, v] == b, d_ref[grp, v, hh], 0.0))
                    out = jnp.where((row == b) & (col == hd), tot, out)
        o_ref[...] = out

    return pl.pallas_call(
        body, out_shape=jax.ShapeDtypeStruct((N_BUCKETS, n_heads), F32),
        compiler_params=pltpu.CompilerParams(vmem_limit_bytes=VMEM_LIMIT_BYTES), name="a_bias_grad")(buckets, dbias)


def _lane_is_second_head(shape):
    return lax.broadcasted_iota(jnp.int32, shape, len(shape) - 1) >= HEAD_A


VIEW_ROWS = 512


def _view_chunks():
    return [pltpu.VMEM((VIEW_ROWS, LANES), F32)] * (WIDTH_A // LANES)


def _rows_to_view(x_ref, col, o_ref, ocol, d, chunks):
    n = VIEW_ROWS // d
    for c, scr in enumerate(chunks):
        scr[...] = x_ref[:, col + c * LANES:col + (c + 1) * LANES].astype(F32)
        for r in range(d):
            at = ocol + r * WIDTH_A + c * LANES
            o_ref[:, at:at + LANES] = scr[pl.ds(r, n, stride=d), :].astype(o_ref.dtype)


def _view_to_rows(v_ref, o_ref, col, d, chunks):
    n = VIEW_ROWS // d
    for c, scr in enumerate(chunks):
        if d == 1:
            o_ref[:, col + c * LANES:col + (c + 1) * LANES] = v_ref[:, c * LANES:(c + 1) * LANES].astype(o_ref.dtype)
            continue
        for r in range(d):
            scr[pl.ds(r, n, stride=d), :] = v_ref[:, r * WIDTH_A + c * LANES:r * WIDTH_A + (c + 1) * LANES].astype(F32)
        o_ref[:, col + c * LANES:col + (c + 1) * LANES] = scr[...].astype(o_ref.dtype)


def _group_view(proj, grp, d):
    T = proj.shape[0]
    if d == 1:
        return proj, (lambda part, r: grp * 3 + part)

    def body(x_ref, o_ref, *chunks):
        for part in range(3):
            _rows_to_view(x_ref, part * WIDTH_A, o_ref, part * d * WIDTH_A, d, chunks)

    view = pl.pallas_call(
        body, out_shape=jax.ShapeDtypeStruct((T // d, 3 * d * WIDTH_A), proj.dtype), grid=(T // VIEW_ROWS,),
        in_specs=[_bs((VIEW_ROWS, 3 * WIDTH_A), lambda i: (i, grp))],
        out_specs=_bs((VIEW_ROWS // d, 3 * d * WIDTH_A), lambda i: (i, 0)),
        scratch_shapes=_view_chunks(), compiler_params=_params("parallel"), name=f"a_view_d{d}")(proj)
    return view, (lambda part, r: part * d + r)


def _stack_heads(v2, second):
    zero = jnp.zeros_like(v2)
    return jnp.concatenate([jnp.where(second, zero, v2), jnp.where(second, v2, zero)], axis=0)


def _unstack_heads(v, second):
    return jnp.where(second, v[QB_A:], v[:QB_A])


def _dil_fwd(view, bias, d):
    pv, colblk = view
    L = pv.shape[0]
    nblk = L // QB_A
    W2 = 2 * HEAD_A
    scale = HEAD_A ** -0.5

    def body(q_ref, k_ref, v_ref, b_ref, o_ref, l_ref):
        win = pl.ds(_window_start(pl.program_id(1), nblk), WIN_A)
        second = _lane_is_second_head((QB_A, W2))
        pairs = range(HEADS_A // 2)
        cols = [slice(hp * W2, (hp + 1) * W2) for hp in pairs]
        s = [lax.dot_general(_stack_heads(q_ref[:, cols[hp]], second), k_ref[win, cols[hp]], (NT, ((), ())),
                             preferred_element_type=F32) * scale + b_ref[2 * hp:2 * hp + 2].reshape(2 * QB_A, WIN_A)
             for hp in pairs]
        m = [jnp.max(x, axis=-1, keepdims=True) for x in s]
        p = [jnp.exp(x - mx) for x, mx in zip(s, m)]
        l = [jnp.sum(x, axis=-1, keepdims=True) for x in p]
        res = [jnp.dot(p[hp].astype(BF16), v_ref[win, cols[hp]], preferred_element_type=F32) / l[hp] for hp in pairs]
        o_ref[...] = jnp.concatenate([_unstack_heads(x, second) for x in res], axis=1).astype(o_ref.dtype)
        l_ref[...] = jnp.concatenate([_unstack_heads(jnp.broadcast_to(mx + jnp.log(lx), (2 * QB_A, W2)), second)
                                      for mx, lx in zip(m, l)], axis=1)

    in_specs = [_bs((QB_A, WIDTH_A), lambda r, n: (n, colblk(0, r))),
                _bs((L, WIDTH_A), lambda r, n: (0, colblk(1, r))), _bs((L, WIDTH_A), lambda r, n: (0, colblk(2, r))),
                _bs((None, HEADS_A, QB_A, WIN_A), lambda r, n: (_window_variant(n, nblk), 0, 0, 0))]
    o, lse = pl.pallas_call(
        body, out_shape=[jax.ShapeDtypeStruct((L, d * WIDTH_A), BF16), jax.ShapeDtypeStruct((L, d * WIDTH_A), F32)],
        grid=(d, nblk), in_specs=in_specs,
        out_specs=[_bs((QB_A, WIDTH_A), lambda r, n: (n, r)), _bs((QB_A, WIDTH_A), lambda r, n: (n, r))],
        compiler_params=_params("parallel", "parallel"), name=f"a_fwd_d{d}")(pv, pv, pv, bias)
    return o, lse


def _dil_bwd(view_qkv, bias, do, lse, cterm, d):
    pv, colblk = view_qkv
    L = pv.shape[0]
    nblk = L // QB_A
    W2 = 2 * HEAD_A
    PPS = 4
    WS = PPS * W2
    ob = WIDTH_A // WS
    scale = HEAD_A ** -0.5

    def body(q_ref, k_ref, v_ref, do_ref, l_ref, c_ref, b_ref, dq_ref, dk_ref, dv_ref, db_ref):
        r, n = pl.program_id(1), pl.program_id(2)

        @pl.when(n == 0)
        def _():
            dk_ref[...] = jnp.zeros_like(dk_ref)
            dv_ref[...] = jnp.zeros_like(dv_ref)

        @pl.when((n == 0) & (r == 0))
        def _():
            db_ref[...] = jnp.zeros_like(db_ref)

        second = _lane_is_second_head((QB_A, W2))
        win = pl.ds(_window_start(n, nblk), WIN_A)
        variant = _window_variant(n, nblk)
        pairs = range(PPS)
        cols = [slice(pp * W2, (pp + 1) * W2) for pp in pairs]

        def head_rows(ref, pp):
            v2 = ref[:, cols[pp]]
            return jnp.concatenate([v2[:, 0:1], v2[:, HEAD_A:HEAD_A + 1]], axis=0)

        kw = [k_ref[win, c] for c in cols]
        vw = [v_ref[win, c] for c in cols]
        qs = [_stack_heads(q_ref[:, c], second) for c in cols]
        dos = [_stack_heads(do_ref[:, c], second) for c in cols]
        s = [lax.dot_general(qs[pp], kw[pp], (NT, ((), ())), preferred_element_type=F32) for pp in pairs]
        dp = [lax.dot_general(dos[pp], vw[pp], (NT, ((), ())), preferred_element_type=F32) for pp in pairs]
        p = [jnp.exp(s[pp] * scale + b_ref[2 * pp:2 * pp + 2].reshape(2 * QB_A, WIN_A) - head_rows(l_ref, pp)) for pp in pairs]
        ds = [p[pp] * (dp[pp] + head_rows(c_ref, pp)) for pp in pairs]
        db_ref[variant] += jnp.concatenate([x.reshape(2, QB_A, WIN_A) for x in ds], axis=0)
        pb = [x.astype(BF16) for x in p]
        dsb = [(x * scale).astype(BF16) for x in ds]
        dq_ref[...] = jnp.concatenate([_unstack_heads(jnp.dot(dsb[pp], kw[pp], preferred_element_type=F32), second)
                                       for pp in pairs], axis=1).astype(dq_ref.dtype)
        dk_ref[win, :] += jnp.concatenate([lax.dot_general(dsb[pp], qs[pp], (TN, ((), ())), preferred_element_type=F32)
                                           for pp in pairs], axis=1)
        dv_ref[win, :] += jnp.concatenate([lax.dot_general(pb[pp], dos[pp], (TN, ((), ())), preferred_element_type=F32)
                                           for pp in pairs], axis=1)

    kv_spec = _resident if d == 1 else _bs
    in_specs = [_bs((QB_A, WS), lambda hp, r, n: (n, colblk(0, r) * ob + hp)),
                kv_spec((L, WS), lambda hp, r, n: (0, colblk(1, r) * ob + hp)),
                kv_spec((L, WS), lambda hp, r, n: (0, colblk(2, r) * ob + hp))]
    in_specs += [_bs((QB_A, WS), lambda hp, r, n: (n, r * ob + hp))] * 3
    in_specs += [_bs((None, 2 * PPS, QB_A, WIN_A), lambda hp, r, n: (_window_variant(n, nblk), hp, 0, 0))]
    out_shape = [jax.ShapeDtypeStruct((L, d * WIDTH_A), BF16), jax.ShapeDtypeStruct((L, d * WIDTH_A), F32),
                 jax.ShapeDtypeStruct((L, d * WIDTH_A), F32), jax.ShapeDtypeStruct((3, HEADS_A, QB_A, WIN_A), F32)]
    out_specs = [_bs((QB_A, WS), lambda hp, r, n: (n, r * ob + hp)),
                 _bs((L, WS), lambda hp, r, n: (0, r * ob + hp)), _bs((L, WS), lambda hp, r, n: (0, r * ob + hp)),
                 _bs((3, 2 * PPS, QB_A, WIN_A), lambda hp, r, n: (0, hp, 0, 0))]
    dq, dk, dv, db = pl.pallas_call(
        body, out_shape=out_shape, grid=(ob, d, nblk), in_specs=in_specs, out_specs=out_specs,
        compiler_params=_params("arbitrary", "arbitrary", "arbitrary"), name=f"a_bwd_d{d}")(
            pv, pv, pv, do, lse, cterm, bias)
    return dq, dk, dv, db


def _assemble_dproj(a_parts, dq_b, dk_b, dv_b, dga, dgb):
    T = dq_b.shape[0]
    flat = [(a_parts[part][g], d) for part in range(3) for g, d in enumerate(DILATIONS)]
    rest = [dq_b, dk_b, dv_b, dga, dgb]

    def body(*refs):
        views, others = refs[:len(flat)], refs[len(flat):len(flat) + len(rest)]
        o_ref, chunks = refs[len(flat) + len(rest)], refs[len(flat) + len(rest) + 1:]
        col = 0
        for v_ref, (_, d) in zip(views, flat):
            _view_to_rows(v_ref, o_ref, col, d, chunks)
            col += WIDTH_A
        for x_ref in others:
            w = x_ref.shape[1]
            o_ref[:, col:col + w] = x_ref[...].astype(o_ref.dtype)
            col += w

    in_specs = [_bs((VIEW_ROWS // d, d * WIDTH_A), lambda i: (i, 0)) for _, d in flat]
    in_specs += [_bs((VIEW_ROWS, x.shape[1]), lambda i: (i, 0)) for x in rest]
    return pl.pallas_call(
        body, out_shape=jax.ShapeDtypeStruct((T, IN_WIDTH), BF16), grid=(T // VIEW_ROWS,), in_specs=in_specs,
        out_specs=_bs((VIEW_ROWS, IN_WIDTH), lambda i: (i, 0)), scratch_shapes=_view_chunks(),
        compiler_params=_params("parallel"), name="mix_bwd_dproj")(*[a for a, _ in flat], *rest)


def _segment_ones():
    i = np.arange(WIDTH_A)
    return jnp.asarray((i[:, None] // HEAD_A == i[None, :] // HEAD_A).astype(np.float32), dtype=BF16)


def _group_weights(l0, l1, l2):
    m = jnp.maximum(jnp.maximum(l0, l1), l2)
    e = [jnp.exp(l - m) for l in (l0, l1, l2)]
    z = e[0] + e[1] + e[2]
    return [ei / z for ei in e]


def _view_specs():
    return [_bs((VIEW_ROWS // d, d * WIDTH_A), lambda i: (i, 0)) for d in DILATIONS]


def _stage_tiles(n):
    return [pltpu.VMEM((VIEW_ROWS, WIDTH_A), F32)] * n


def _combine_fwd(outs, lses):
    T = outs[0].shape[0] * DILATIONS[0]
    n = len(DILATIONS)

    def body(*refs):
        o_refs, l_refs, oa_ref = refs[:n], refs[n:2 * n], refs[2 * n]
        o_st, l_st, chunks = refs[2 * n + 1:3 * n + 1], refs[3 * n + 1:4 * n + 1], refs[4 * n + 1:]
        for g, d in enumerate(DILATIONS):
            _view_to_rows(o_refs[g], o_st[g], 0, d, chunks)
            _view_to_rows(l_refs[g], l_st[g], 0, d, chunks)
        w = _group_weights(*[l[...] for l in l_st])
        oa_ref[...] = (w[0] * o_st[0][...] + w[1] * o_st[1][...] + w[2] * o_st[2][...]).astype(oa_ref.dtype)

    return pl.pallas_call(
        body, out_shape=jax.ShapeDtypeStruct((T, WIDTH_A), BF16), grid=(T // VIEW_ROWS,),
        in_specs=_view_specs() * 2, out_specs=_bs((VIEW_ROWS, WIDTH_A), lambda i: (i, 0)),
        scratch_shapes=_stage_tiles(2 * n) + _view_chunks(), compiler_params=_params("parallel"), name="a_combine")(*outs, *lses)


def _combine_bwd(doa, outs, lses):
    T = doa.shape[0]
    n = len(DILATIONS)

    def body(*refs):
        d_ref, o_refs, l_refs, seg_ref = refs[0], refs[1:n + 1], refs[n + 1:2 * n + 1], refs[2 * n + 1]
        do_refs, c_refs = refs[2 * n + 2:3 * n + 2], refs[3 * n + 2:4 * n + 2]
        o_st, l_st = refs[4 * n + 2:5 * n + 2], refs[5 * n + 2:6 * n + 2]
        tmp, chunks = refs[6 * n + 2], refs[6 * n + 3:]
        for g, d in enumerate(DILATIONS):
            _view_to_rows(o_refs[g], o_st[g], 0, d, chunks)
            _view_to_rows(l_refs[g], l_st[g], 0, d, chunks)
        dv = d_ref[...].astype(F32)
        w = _group_weights(*[l[...] for l in l_st])
        seg = seg_ref[...]
        tot = jnp.zeros(dv.shape, F32)
        for g in range(n):
            prod = w[g] * dv * o_st[g][...]
            hi = prod.astype(BF16)
            lo = (prod - hi.astype(F32)).astype(BF16)
            tot = tot + jnp.dot(hi, seg, preferred_element_type=F32) + jnp.dot(lo, seg, preferred_element_type=F32)
        for g, d in enumerate(DILATIONS):
            tmp[...] = w[g] * dv
            _rows_to_view(tmp, 0, do_refs[g], 0, d, chunks)
            tmp[...] = -w[g] * tot
            _rows_to_view(tmp, 0, c_refs[g], 0, d, chunks)

    views = [jax.ShapeDtypeStruct((T // d, d * WIDTH_A), dt) for dt in (BF16, F32) for d in DILATIONS]
    res = pl.pallas_call(
        body, out_shape=views, grid=(T // VIEW_ROWS,),
        in_specs=[_bs((VIEW_ROWS, WIDTH_A), lambda i: (i, 0))] + _view_specs() * 2 + [_bs((WIDTH_A, WIDTH_A), lambda i: (0, 0))],
        out_specs=_view_specs() * 2, scratch_shapes=_stage_tiles(2 * n + 1) + _view_chunks(),
        compiler_params=_params("parallel"), name="a_combine_bwd")(doa, *outs, *lses, _segment_ones())
    return res[:n], res[n:]


def _rope_tables(T):
    rows = T // GRID_W
    row = jnp.repeat(jnp.arange(rows, dtype=F32), GRID_W)
    col = jnp.tile(jnp.arange(GRID_W, dtype=F32), rows)
    n_freq = HEAD_B // 4
    freq = ROPE_THETA ** (-jnp.arange(n_freq, dtype=F32) / n_freq)
    ang = jnp.concatenate([row[:, None] * freq, col[:, None] * freq], axis=-1)
    cos, sin = jnp.repeat(jnp.cos(ang), 2, axis=1), jnp.repeat(jnp.sin(ang), 2, axis=1)
    sign = jnp.where(jnp.arange(HEAD_B) % 2 == 0, -1.0, 1.0).astype(F32)
    return cos, sin * sign


def _swap_pairs(v):
    even = lax.broadcasted_iota(jnp.int32, v.shape, v.ndim - 1) % 2 == 0
    n = v.shape[-1]
    return jnp.where(even, pltpu.roll(v, n - 1, v.ndim - 1), pltpu.roll(v, 1, v.ndim - 1))


def _qk_fwd(name, proj, col0, n_heads, gain, cos, sin, out_scale=1.0):
    T = proj.shape[0]

    def fn(xr, g, c, s):
        xn = _norm_fwd(xr.astype(F32), g)
        return (xn * c + _swap_pairs(xn) * s) * out_scale

    (out,) = _ew(name, fn, [_tiled(proj, HEAD_B, col0 // HEAD_B), _whole(gain), _table(cos), _table(sin)],
                 [(BF16, HEAD_B)], n_rows=T, rows=2048, ncols=n_heads)
    return out


def _qk_bwd(name, dout, proj, col0, n_heads, gain, cos, sin, in_scale=1.0):
    T = proj.shape[0]

    def fn(dv, xr, g, c, s):
        dv = dv.astype(F32) * in_scale
        dxn = c * dv + _swap_pairs(s * dv)
        dx, dgr = _norm_bwd(xr.astype(F32), g, dxn)
        return dx, _colsum(dgr)

    dx, dg = _ew(name, fn, [_tiled(dout, HEAD_B, 0), _tiled(proj, HEAD_B, col0 // HEAD_B), _whole(gain),
                            _table(cos), _table(sin)],
                 [(BF16, HEAD_B)], n_rows=T, rows=2048, reds=(HEAD_B,), ncols=n_heads)
    return dx, jnp.sum(dg, axis=0)


def _gqa_fwd(qn, kn, proj):
    T = qn.shape[0]
    GW = 4 * HEAD_B
    QB = QB_B

    def body(q_ref, k_ref, v_ref, o_ref, l_ref):
        k, v = k_ref[...], v_ref[...]
        lane = lax.broadcasted_iota(jnp.int32, (QB, HEAD_B), 1)
        heads = range(4)
        s = [lax.dot_general(q_ref[:, g * HEAD_B:(g + 1) * HEAD_B], k, (NT, ((), ())), preferred_element_type=F32)
             for g in heads]
        m = [jnp.max(x, axis=-1, keepdims=True) for x in s]
        p = [jnp.exp2(x - mx) for x, mx in zip(s, m)]
        l = [jnp.sum(x, axis=-1, keepdims=True) for x in p]
        o = [jnp.dot(p[g].astype(BF16), v, preferred_element_type=F32) / l[g] for g in heads]
        o_ref[...] = jnp.concatenate(o, axis=1).astype(o_ref.dtype)
        lse_all = jnp.zeros((QB, HEAD_B), F32)
        for g in heads:
            lse_all = jnp.where(lane == g, m[g] + jnp.log2(l[g]), lse_all)
        l_ref[...] = lse_all

    return pl.pallas_call(
        body, out_shape=[jax.ShapeDtypeStruct((T, 2 * GW), BF16), jax.ShapeDtypeStruct((2, T, HEAD_B), F32)],
        grid=(2, T // QB),
        in_specs=[_bs((QB, GW), lambda kv, i: (i, kv)), _bs((T, HEAD_B), lambda kv, i: (0, kv)),
                  _bs((T, HEAD_B), lambda kv, i: (0, B_V // HEAD_B + kv))],
        out_specs=[_bs((QB, GW), lambda kv, i: (i, kv)), _bs((None, QB, HEAD_B), lambda kv, i: (kv, i, 0))],
        compiler_params=_params("parallel", "parallel"), name="b_fwd")(qn, kn, proj)


def _gqa_bwd(qn, kn, proj, o, lse, do, deps=()):
    T = qn.shape[0]
    GW = 4 * HEAD_B

    def body(q_ref, k_ref, v_ref, o_ref, l_ref, do_ref, *rest):
        dq_ref, dk_ref, dv_ref = rest[-3:]
        i = pl.program_id(1)

        @pl.when(i == 0)
        def _():
            dk_ref[...] = jnp.zeros_like(dk_ref)
            dv_ref[...] = jnp.zeros_like(dv_ref)

        k, v = k_ref[...], v_ref[...]
        lse_all = l_ref[...]
        for g in range(4):
            cols = slice(g * HEAD_B, (g + 1) * HEAD_B)
            q, dob = q_ref[:, cols], do_ref[:, cols]
            delta = jnp.sum(dob.astype(F32) * o_ref[:, cols].astype(F32), axis=-1, keepdims=True)
            s = lax.dot_general(q, k, (NT, ((), ())), preferred_element_type=F32)
            p = jnp.exp2(s - lse_all[:, g:g + 1])
            dp = lax.dot_general(dob, v, (NT, ((), ())), preferred_element_type=F32)
            ds = (p * (dp - delta)).astype(BF16)
            dq_ref[:, cols] = jnp.dot(ds, k, preferred_element_type=F32).astype(dq_ref.dtype)
            dk_ref[...] += lax.dot_general(ds, q, (TN, ((), ())), preferred_element_type=F32)
            dv_ref[...] += lax.dot_general(p.astype(BF16), dob, (TN, ((), ())), preferred_element_type=F32)

    return pl.pallas_call(
        body, out_shape=[jax.ShapeDtypeStruct((T, 2 * GW), BF16), jax.ShapeDtypeStruct((T, 2 * HEAD_B), F32),
                         jax.ShapeDtypeStruct((T, 2 * HEAD_B), F32)],
        grid=(2, T // QB_B),
        in_specs=[_bs((QB_B, GW), lambda kv, i: (i, kv)), _bs((T, HEAD_B), lambda kv, i: (0, kv)),
                  _bs((T, HEAD_B), lambda kv, i: (0, B_V // HEAD_B + kv)), _bs((QB_B, GW), lambda kv, i: (i, kv)),
                  _bs((None, QB_B, HEAD_B), lambda kv, i: (kv, i, 0)), _bs((QB_B, GW), lambda kv, i: (i, kv))] + _any_specs(len(deps)),
        out_specs=[_bs((QB_B, GW), lambda kv, i: (i, kv)), _bs((T, HEAD_B), lambda kv, i: (0, kv)),
                   _bs((T, HEAD_B), lambda kv, i: (0, kv))],
        compiler_params=_params("parallel", "arbitrary"), name="b_bwd")(qn, kn, proj, o, lse, do, *deps)


def _local_step(x, target, small, get_w, put_g, deps=()):
    T, D = x.shape
    gs = {}

    bias = _bias_tiles(small["rel_bias"])
    cos, sin = _rope_tables(T)
    (x1, h2), ffn1_saved = _ffn_fwd("ffn1", x, small["ffn1_norm"], lambda name, after: get_w(name, [after, bias, cos, sin]), deps,
                                    tail_ins=[small["mix_norm"]], tail_fn=lambda y, g: (y, _norm_fwd(y, g)), tail_outs=(F32, BF16))
    w_in = get_w("w_in", h2)
    nq = w_in.shape[2]
    tpq = nq // WIDTH_A

    def proj_tile(j, k):
        c = j * tpq + k
        return jnp.where(c < 3 * len(DILATIONS), (c % 3) * 3 + c // 3, c)

    proj = _mm("mix_in", (4, tpq),
               [(h2, _resident((T, D), lambda j, k: (0, 0)), w_in, _bs((None, D, WIDTH_A), lambda j, k: (j, 0, k)))],
               jax.ShapeDtypeStruct((T, IN_WIDTH), BF16), _bs((T, WIDTH_A), lambda j, k: (0, proj_tile(j, k))), NN)

    a_views = [_group_view(proj, grp, d) for grp, d in enumerate(DILATIONS)]
    a_outs, a_lses = [], []
    for grp, d in enumerate(DILATIONS):
        o, l = _dil_fwd(a_views[grp], bias[grp], d)
        a_outs.append(o)
        a_lses.append(l)
    o_a = _combine_fwd(a_outs, a_lses)

    qn =_qk_fwd("b_qnorm", proj, B_Q, 8, small["q_norm"], cos, sin, out_scale=QK_SCALE_LOG2)
    kn = _qk_fwd("b_knorm", proj, B_K, 2, small["k_norm"], cos, sin)
    o_b, lse_b = _gqa_fwd(qn, kn, proj)

    wa, wb, wo = get_w("w_branch_a", o_b), get_w("w_branch_b", o_b), get_w("w_out", o_b)
    bg_a, bg_b = small["b_gate"][:, :D], small["b_gate"][:, D:]
    n_a = wa.shape[0]

    def merge_out(oa_ref, ob_ref, ga_ref, gb_ref, x1_ref, wa_ref, wb_ref, wo_ref, ba_ref, bb_ref, g2_ref,
                  ta_ref, tb_ref, mg_ref, x2_ref, hn_ref):
        oa = oa_ref[...]
        ta = jnp.concatenate([jnp.dot(oa, wa_ref[j], preferred_element_type=F32) for j in range(n_a)], axis=1)
        tb = jnp.dot(ob_ref[...], wb_ref[...], preferred_element_type=F32)
        sa = _sigmoid(ga_ref[...].astype(F32) + ba_ref[...])
        sb = _sigmoid(gb_ref[...].astype(F32) + bb_ref[...])
        merged = (sa * ta + sb * tb).astype(BF16)
        ta_ref[...], tb_ref[...], mg_ref[...] = ta.astype(BF16), tb.astype(BF16), merged
        y = x1_ref[...] + jnp.dot(merged, wo_ref[...], preferred_element_type=F32)
        x2_ref[...] = y
        hn_ref[...] = _norm_fwd(y, g2_ref[...]).astype(BF16)

    row = _bs((512, D), lambda i: (i, 0))
    gate_specs = [_bs((512, D), lambda i: (i, G_A // D)), _bs((512, D), lambda i: (i, G_B // D))]
    whole2, whole3 = (lambda i: (0, 0)), (lambda i: (0, 0, 0))
    vec = _bs((1, D), whole2)
    t_a, t_b, merged, x2, hn2 = pl.pallas_call(
        merge_out, out_shape=[jax.ShapeDtypeStruct((T, D), BF16)] * 3 + [jax.ShapeDtypeStruct((T, D), F32), jax.ShapeDtypeStruct((T, D), BF16)],
        grid=(T // 512,),
        in_specs=[_bs((512, WIDTH_A), lambda i: (i, 0)), row] + gate_specs + [row, _resident(wa.shape, whole3), _resident((D, D), whole2),
                                                                                _resident((D, D), whole2), vec, vec, vec],
        out_specs=[row] * 5, compiler_params=_params("parallel"), name="mix_merge_out")(
            o_a, o_b, proj, proj, x1, wa, wb, wo, bg_a, bg_b, small["ffn2_norm"])

    def head(xv, g, tv):
        r = _rstd(xv)
        xh = xv * r
        e = xh * g - tv
        dy = e * (1.0 / D)
        dxh = dy * g
        dx = r * (dxh - xh * jnp.mean(dxh * xh, axis=-1, keepdims=True))
        return dx, 0.5 * dx, _colsum(e * e) * (0.5 / D), _colsum(dy * xh)

    (dx3, dx3_half, loss_cols, g_final), ffn2_saved = _ffn_fwd(
        "ffn2", x2, small["ffn2_norm"], get_w, h=hn2, tail_ins=[small["final_norm"].reshape(1, D), target], tail_fn=head,
        tail_outs=(F32, BF16), tail_reds=(D, D))
    gs["final_norm"] = g_final.reshape(D)

    dx2, _, dmix, gs["ffn2_norm"] = _ffn_bwd("ffn2", x2, small["ffn2_norm"], get_w, put_g, ffn2_saved, dx3, dx3_half,
                                             also_bf16=True)
    g_out = _mm_wgrad("mix_bwd_dwout", merged, dmix, a_cols=D // 4, b_cols=None, tm=256, tn=512, J=4).reshape(D, D)

    def merge_out_bwd(dx_ref, ta_ref, tb_ref, ga_ref, gb_ref, wa_ref, wb_ref, wo_ref, ba_ref, bb_ref,
                      dta_ref, dtb_ref, dga_ref, dgb_ref, doa_ref, dob_ref, dba_ref, dbb_ref):
        dm = lax.dot_general(dx_ref[...], wo_ref[...], (NT, ((), ())), preferred_element_type=F32)
        ta, tb = ta_ref[...].astype(F32), tb_ref[...].astype(F32)
        sa = _sigmoid(ga_ref[...].astype(F32) + ba_ref[...])
        sb = _sigmoid(gb_ref[...].astype(F32) + bb_ref[...])
        dga, dgb = dm * ta * sa * (1.0 - sa), dm * tb * sb * (1.0 - sb)
        dta, dtb = (dm * sa).astype(BF16), (dm * sb).astype(BF16)
        dta_ref[...], dtb_ref[...] = dta, dtb
        dga_ref[...], dgb_ref[...] = dga.astype(BF16), dgb.astype(BF16)
        w = wa_ref.shape[2]
        doa = sum(lax.dot_general(dta[:, j * w:(j + 1) * w], wa_ref[j], (NT, ((), ())), preferred_element_type=F32) for j in range(n_a))
        doa_ref[...] = doa.astype(BF16)
        dob_ref[...] = lax.dot_general(dtb, wb_ref[...], (NT, ((), ())), preferred_element_type=F32).astype(BF16)

        @pl.when(pl.program_id(0) == 0)
        def _():
            dba_ref[...] = jnp.zeros_like(dba_ref)
            dbb_ref[...] = jnp.zeros_like(dbb_ref)
        dba_ref[...] += _colsum(dga)
        dbb_ref[...] += _colsum(dgb)

    rowb = _bs((256, D), lambda i: (i, 0))
    gate_specs = [_bs((256, D), lambda i: (i, G_A // D)), _bs((256, D), lambda i: (i, G_B // D))]
    dta, dtb, dga, dgb, do_a, do_b, dba, dbb = pl.pallas_call(
        merge_out_bwd,
        out_shape=[jax.ShapeDtypeStruct((T, D), BF16)] * 4 + [jax.ShapeDtypeStruct((T, WIDTH_A), BF16), jax.ShapeDtypeStruct((T, D), BF16)]
        + [jax.ShapeDtypeStruct((1, D), F32)] * 2,
        grid=(T // 256,),
        in_specs=[rowb, rowb, rowb] + gate_specs + [_resident(wa.shape, whole3), _resident((D, D), whole2), _resident((D, D), whole2), vec, vec],
        out_specs=[rowb] * 4 + [_bs((256, WIDTH_A), lambda i: (i, 0)), rowb, vec, vec],
        compiler_params=_params("arbitrary"), name="mix_merge_out_bwd")(dmix, t_a, t_b, proj, proj, wa, wb, wo, bg_a, bg_b)
    gs["b_gate"] = jnp.concatenate([dba, dbb], axis=1)

    g_a = _mm_wgrad("mix_bwd_dwa", o_a, dta, a_cols=None, b_cols=D // 4, tm=WIDTH_A, tn=256, J=4)
    g_b = _mm_wgrad("mix_bwd_dwb", o_b, dtb, a_cols=D // 4, b_cols=None, tm=256, tn=512, J=4).reshape(D, D)
    deps = put_g({"w_out": g_out, "w_branch_a": g_a, "w_branch_b": g_b})

    dqn, dkn, dv_b = _gqa_bwd(qn, kn, proj, o_b, lse_b, do_b, deps)
    dq_b, gs["q_norm"] = _qk_bwd("b_bwd_qnorm", dqn, proj, B_Q, 8, small["q_norm"], cos, sin, in_scale=HEAD_B ** -0.5)
    dk_b, gs["k_norm"] = _qk_bwd("b_bwd_knorm", dkn, proj, B_K, 2, small["k_norm"], cos, sin, in_scale=1.0 / LOG2_E)

    do_groups, c_groups = _combine_bwd(do_a, a_outs, a_lses)
    dqs, dks, dvs, dbs = [], [], [], []
    for grp, d in enumerate(DILATIONS):
        dq, dk, dv, db = _dil_bwd(a_views[grp], bias[grp], do_groups[grp], a_lses[grp], c_groups[grp], d)
        dqs.append(dq), dks.append(dk), dvs.append(dv), dbs.append(db)
    gs["rel_bias"] = _bias_grad(jnp.stack(dbs))

    dproj = _assemble_dproj([dqs, dks, dvs], dq_b, dk_b, dv_b, dga, dgb)
    nq = w_in.shape[2]
    g_in = _mm("mix_bwd_dwin", (4, tpq),
               [(h2, _resident((T, D), lambda j, k: (0, 0)), dproj, _bs((T, WIDTH_A), lambda j, k: (0, j * tpq + k)))],
               jax.ShapeDtypeStruct((4, D, nq), BF16), _bs((None, D, WIDTH_A), lambda j, k: (j, 0, k)), TN)
    deps = put_g({"w_in": g_in})
    dx1, dx1_half, gs["mix_norm"] = _dh_norm_bwd(
        "mix_bwd_dh", 256,
        [(dproj, _bs((256, nq), lambda i, j=j: (i, j)), w_in, _resident((None, D, nq), lambda i, j=j: (j, 0, 0))) for j in range(4)],
        NT, x1, small["mix_norm"], dx2, deps)

    dx0, _, gs["ffn1_norm"] = _ffn_bwd("ffn1", x, small["ffn1_norm"], get_w, put_g, ffn1_saved, dx1, dx1_half)
    return loss_cols, dx0, gs


def _position():
    return lax.axis_index("x"), lax.axis_index("y"), lax.axis_index("c")


def _any_specs(n):
    return [pl.BlockSpec(memory_space=pl.ANY)] * n


HBM_SPEC = pl.BlockSpec(memory_space=pltpu.HBM)
SEM_SPEC = pl.BlockSpec(memory_space=pltpu.SEMAPHORE)
DATAFLOW_EFFECT = pltpu.SideEffectType.DATAFLOW_SIDE_EFFECTING
N_PEER_CHIPS = 3
LANES = 128


def _quarter_copies(srcs, lands, send_sems, recv_sems, mode):
    x, y, c = _position()
    me = 2 * x + y
    peers = [(1 - x, y, c), (x, 1 - y, c), (1 - x, 1 - y, c)]
    copies = []
    for src, land, send, recv in zip(srcs, lands, send_sems, recv_sems):
        if mode == "sibling":
            copies.append(pltpu.make_async_remote_copy(src_ref=src, dst_ref=land, send_sem=send.at[0], recv_sem=recv.at[0],
                                                       device_id=(x, y, 1 - c), device_id_type=MESH))
            continue
        scatter = mode == "scatter"
        half = land.shape[1] // 2
        mine = land.at[me, pl.ds(c * half, half)]
        for p, (px, py, pc) in enumerate(peers):
            copies.append(pltpu.make_async_remote_copy(
                src_ref=src.at[2 * px + py] if scatter else mine, dst_ref=land.at[me] if scatter else mine,
                send_sem=send.at[p], recv_sem=recv.at[p], device_id=(px, py, pc), device_id_type=MESH))
    return copies


def _fill_from_sibling(name, stacks):
    n = len(stacks)

    def body(*refs):
        outs = refs[n:2 * n]
        send_sems, recv_sems = refs[2 * n:]
        x, y, c = _position()
        copies = []
        for i, ref in enumerate(outs):
            half = ref.shape[1] // 2
            rows = pl.ds(c * half, half)
            for p, k in enumerate((2 * (1 - x) + y, 2 * x + (1 - y), 2 * (1 - x) + (1 - y))):
                cp = pltpu.make_async_remote_copy(ref.at[k, rows], ref.at[k, rows], send_sems.at[3 * i + p], recv_sems.at[3 * i + p],
                                                  device_id=(x, y, 1 - c), device_id_type=MESH)
                cp.start()
                copies.append(cp)
        for cp in copies:
            cp.wait()

    return pl.pallas_call(
        body, out_shape=[jax.ShapeDtypeStruct(s.shape, s.dtype) for s in stacks],
        in_specs=_any_specs(n), out_specs=_any_specs(n), input_output_aliases={i: i for i in range(n)},
        scratch_shapes=[pltpu.SemaphoreType.DMA((N_PEER_CHIPS * n,)), pltpu.SemaphoreType.DMA((N_PEER_CHIPS * n,))],
        compiler_params=pltpu.CompilerParams(has_side_effects=True), name=name)(*stacks)


def _exchange_start(name, srcs, lands, mode):
    n = len(lands)
    arrays = list(lands) if srcs is None else list(srcs) + list(lands)
    k = len(arrays)

    def body(*refs):
        land_refs = refs[k - n:k]
        send_sems, recv_sems = refs[k:k + n], refs[k + n:k + 2 * n]
        token = refs[2 * k + 2 * n]
        for cp in _quarter_copies(refs[:n], land_refs, send_sems, recv_sems, mode):
            cp.start()
        token[...] = jnp.zeros_like(token)

    sem = pltpu.SemaphoreType.DMA((N_PEER_CHIPS,))
    out_shape = [sem] * (2 * n) + [pltpu.HBM(a.shape, a.dtype) for a in arrays] + [jax.ShapeDtypeStruct((8, LANES), F32)]
    res = pl.pallas_call(
        body, name=name, out_shape=out_shape, in_specs=[HBM_SPEC] * k,
        out_specs=[SEM_SPEC] * (2 * n) + [HBM_SPEC] * k + [pl.BlockSpec(memory_space=pltpu.VMEM)],
        input_output_aliases={i: 2 * n + i for i in range(k)},
        compiler_params=pltpu.CompilerParams(has_side_effects=DATAFLOW_EFFECT),
    )(*[pltpu.with_memory_space_constraint(a, pltpu.HBM) for a in arrays])
    thru = res[2 * n:2 * n + k]
    return res[:n], res[n:2 * n], (None if srcs is None else thru[:n]), thru[k - n:], res[2 * n + k]


def _exchange_wait(name, srcs, lands, send_sems, recv_sems, after, mode):
    n = len(lands)
    arrays = list(lands) if srcs is None else list(srcs) + list(lands)
    k = len(arrays)
    after = list(after) if isinstance(after, (list, tuple)) else [after]

    def body(*refs):
        sends, recvs = refs[k:k + n], refs[k + n:k + 2 * n]
        for cp in _quarter_copies(refs[:n], refs[k - n:k], sends, recvs, mode):
            cp.wait_send()
            cp.wait_recv()

    res = pl.pallas_call(
        body, name=name, out_shape=[pltpu.HBM(a.shape, a.dtype) for a in arrays],
        in_specs=[HBM_SPEC] * k + [SEM_SPEC] * (2 * n) + _any_specs(len(after)),
        out_specs=[HBM_SPEC] * k, input_output_aliases={i: i for i in range(k)},
        compiler_params=pltpu.CompilerParams(has_side_effects=DATAFLOW_EFFECT),
    )(*arrays, *send_sems, *recv_sems, *after)
    return (None if srcs is None else res[:n]), res[k - n:]


def _own_slots(name, srcs, from_stack=False):
    n = len(srcs)
    me = (2 * lax.axis_index("x") + lax.axis_index("y")).astype(jnp.int32).reshape(1)

    def body(me_ref, *refs):
        for x_ref, o_ref in zip(refs[:n], refs[n:]):
            o_ref[...] = x_ref[...].astype(o_ref.dtype)

    in_specs, out_specs, out_shape = [], [], []
    for src in srcs:
        R, C = src.shape[-2:]
        in_specs.append(pl.BlockSpec((None, R // 2, C), lambda i, me_ref: (me_ref[0], i, 0)) if from_stack
                        else pl.BlockSpec((R // 2, C), lambda i, me_ref: (i, 0)))
        out_specs.append(pl.BlockSpec((None, R // 2, C), lambda i, me_ref: (me_ref[0], i, 0)))
        out_shape.append(jax.ShapeDtypeStruct((4, R, C), BF16))
    grid_spec = pltpu.PrefetchScalarGridSpec(num_scalar_prefetch=1, grid=(2,), in_specs=in_specs, out_specs=out_specs)
    return pl.pallas_call(body, out_shape=out_shape, grid_spec=grid_spec, compiler_params=_params("parallel"), name=name)(me, *srcs)


def _allreduce_small(buf, after=()):
    R, C = buf.shape
    flips = [(fx, fy, fc) for fx in (0, 1) for fy in (0, 1) for fc in (0, 1)][1:]

    def body(in_ref, *rest):
        out_ref, land_ref, send_sems, recv_sems = rest[len(after):]
        x, y, c = _position()
        me = 4 * x + 2 * y + c
        copies = []
        for k, (fx, fy, fc) in enumerate(flips):
            px, py, pc = (1 - x if fx else x), (1 - y if fy else y), (1 - c if fc else c)
            cp = pltpu.make_async_remote_copy(in_ref, land_ref.at[me], send_sems.at[k], recv_sems.at[k],
                                              device_id=(px, py, pc), device_id_type=MESH)
            cp.start()
            copies.append(cp)
        land_ref[me] = in_ref[...]
        for cp in copies:
            cp.wait()
        acc = land_ref[0]
        for k in range(1, 8):
            acc = acc + land_ref[k]
        out_ref[...] = acc

    return pl.pallas_call(
        body, out_shape=jax.ShapeDtypeStruct((R, C), F32),
        in_specs=[pl.BlockSpec(memory_space=pltpu.VMEM)] + _any_specs(len(after)), out_specs=pl.BlockSpec(memory_space=pltpu.VMEM),
        scratch_shapes=[pltpu.VMEM((8, R, C), F32), pltpu.SemaphoreType.DMA((7,)), pltpu.SemaphoreType.DMA((7,))],
        compiler_params=pltpu.CompilerParams(has_side_effects=True), name="allreduce_small")(buf, *after)


def _adamw_math(w, g, m, v):
    m2 = ADAM_B1 * m + (1.0 - ADAM_B1) * g
    v2 = ADAM_B2 * v + (1.0 - ADAM_B2) * (g * g)
    m_hat = m2 / (1.0 - ADAM_B1 ** ADAM_STEP)
    v_hat = v2 / (1.0 - ADAM_B2 ** ADAM_STEP)
    delta = -ADAM_LR * (m_hat / (jnp.sqrt(v_hat) + ADAM_EPS) + ADAM_WD * w)
    return delta, m2, v2


def _adamw_big(name, w, m, v, mine, theirs):
    R, C = w.shape
    rows = 256 if R % 256 == 0 else R // 2
    nrb = R // rows

    def four(a, b, c, d):
        return ((a.astype(F32) + b.astype(F32)) + c.astype(F32)) + d.astype(F32)

    def fn(wv, mv, vv, *parts):
        g = four(*parts[:4]) + four(*parts[4:])
        return (g,) + _adamw_math(wv, g, mv, vv)

    slots = [_tiled(s.reshape(4 * R, C), None, 0, k * nrb) for s in (mine, theirs) for k in range(4)]
    return _ew(name, fn, [_tiled(w), _tiled(m), _tiled(v)] + slots, [(F32, C)] * 4, n_rows=R, rows=rows)


BIG = ("ffn1_w1", "ffn1_w3", "ffn1_w2", "w_in", "w_branch_a", "w_branch_b", "w_out", "ffn2_w1", "ffn2_w3", "ffn2_w2")
SMALL = ("ffn1_norm", "mix_norm", "b_gate", "q_norm", "k_norm", "rel_bias", "ffn2_norm", "final_norm")
ORDER = ("ffn1_norm", "ffn1_w1", "ffn1_w3", "ffn1_w2", "mix_norm", "w_in", "b_gate", "q_norm", "k_norm", "rel_bias",
         "w_branch_a", "w_branch_b", "w_out", "ffn2_norm", "ffn2_w1", "ffn2_w3", "ffn2_w2", "final_norm")
TRANSPOSED = ("ffn1_w1", "ffn1_w3", "ffn2_w1", "ffn2_w3")
SIBLING_LAG = 2
GATHER_GROUPS = (("ffn1_w1", "ffn1_w3"), ("ffn1_w2",), ("w_in",), ("w_branch_a", "w_branch_b", "w_out"),
                 ("ffn2_w1", "ffn2_w3", "ffn2_w2"))


def _pack_small(d):
    rows = []
    for n in SMALL:
        flat = d[n].reshape(-1)
        pad = (-flat.shape[0]) % LANES
        rows.append(jnp.pad(flat, (0, pad)).reshape(-1, LANES))
    buf = jnp.concatenate(rows, axis=0)
    return jnp.pad(buf, ((0, (-buf.shape[0]) % 8), (0, 0)))


def _unpack_small(buf, like):
    out, r = {}, 0
    for n in SMALL:
        size = like[n].size
        nr = -(-size // LANES)
        out[n] = buf[r:r + nr].reshape(-1)[:size].reshape(like[n].shape)
        r += nr
    return out


def kernel(x, ffn1_norm, ffn1_w1, ffn1_w3, ffn1_w2, mix_norm, w_in, b_gate, q_norm, k_norm, rel_bias, w_branch_a, w_branch_b, w_out, ffn2_norm, ffn2_w1, ffn2_w3, ffn2_w2, final_norm, loss_target, m_ffn1_norm, m_ffn1_w1, m_ffn1_w3, m_ffn1_w2, m_mix_norm, m_w_in, m_b_gate, m_q_norm, m_k_norm, m_rel_bias, m_w_branch_a, m_w_branch_b, m_w_out, m_ffn2_norm, m_ffn2_w1, m_ffn2_w3, m_ffn2_w2, m_final_norm, v_ffn1_norm, v_ffn1_w1, v_ffn1_w3, v_ffn1_w2, v_mix_norm, v_w_in, v_b_gate, v_q_norm, v_k_norm, v_rel_bias, v_w_branch_a, v_w_branch_b, v_w_out, v_ffn2_norm, v_ffn2_w1, v_ffn2_w3, v_ffn2_w2, v_final_norm):
    given = dict(locals())
    w = {n: given[n] for n in ORDER}
    m = {n: given["m_" + n] for n in ORDER}
    v = {n: given["v_" + n] for n in ORDER}
    T, D = x.shape[1], x.shape[2]

    def stored(a, n):
        a = a.reshape(a.shape[1:])
        return a.T if n in TRANSPOSED else a

    def returned(a, n):
        return (a.T if n in TRANSPOSED else a).reshape(w[n].shape)

    quarter = {n: stored(w[n], n) for n in BIG}
    send, recv, _, land_thru, token = _exchange_start(
        "gather_start", None, _own_slots("own_weights", [quarter[n] for n in BIG]), "gather")
    index = {n: i for i, n in enumerate(BIG)}
    ready = {}

    def get_w(name, after):
        if name not in ready:
            group = next(g for g in GATHER_GROUPS if name in g)
            ids = [index[n] for n in group]
            _, stacks = _exchange_wait("gather_wait_" + group[0], None, [land_thru[i] for i in ids],
                                       [send[i] for i in ids], [recv[i] for i in ids], after, "gather")
            stacks = _fill_from_sibling("gather_fill_" + group[0], stacks)
            for n, st in zip(group, stacks):
                ready[n] = st.reshape(D, D) if n in ("w_branch_b", "w_out") else st
        return ready[name]

    scattered, forwarded = [], []

    def forward_oldest(after):
        names, s_sem, r_sem, srcs, lands = scattered.pop(0)
        _, landed = _exchange_wait("scatter_wait_" + names[0], srcs, lands, s_sem, r_sem, after, "scatter")
        started = _exchange_start("sibling_start_" + names[0], landed, [lax.empty(a.shape, a.dtype) for a in landed], "sibling")
        forwarded.append((names,) + tuple(started[:4]))
        return started[4]

    def put_g(grads):
        names = list(grads)
        stacks = [grads[n].reshape((4,) + quarter[n].shape) for n in names]
        lands = _own_slots("own_grad_" + names[0], stacks, from_stack=True)
        started = _exchange_start("scatter_start_" + names[0], stacks, lands, "scatter")
        scattered.append((names,) + tuple(started[:4]))
        tokens = [started[4]]
        if len(scattered) > SIBLING_LAG:
            tokens.append(forward_oldest(started[4]))
        return tokens

    small = {n: w[n] for n in SMALL}
    packed = [_pack_small({n: d[n] for n in SMALL}) for d in (w, m, v)]
    loss_cols, grad_x, gs = _local_step(x.reshape(T, D), loss_target.reshape(T, D), small, get_w, put_g, deps=[token] + packed)

    after = grad_x
    while scattered:
        after = forward_oldest(after)
    grads, deltas, new_m, new_v = {}, {}, {}, {}
    for names, s_sem, r_sem, srcs, lands in forwarded:
        mine, theirs = _exchange_wait("sibling_wait_" + names[0], srcs, lands, s_sem, r_sem, after, "sibling")
        for n, a, b in zip(names, mine, theirs):
            res = _adamw_big(f"adamw_{n}", quarter[n], stored(m[n], n), stored(v[n], n), a, b)
            grads[n], deltas[n], new_m[n], new_v[n] = [returned(r, n) for r in res]

    gs = {n: gs[n].reshape(w[n].shape) for n in SMALL}
    packed_g = _pack_small(gs)
    n_small = packed_g.shape[0]
    summed = _allreduce_small(jnp.concatenate([packed_g, loss_cols.reshape(-1, LANES)], axis=0), after=[new_v[n] for n in BIG])
    g_small, loss = summed[:n_small], jnp.sum(summed[n_small:])
    R = g_small.shape[0]
    res = _ew("adamw_small", lambda wv, mv, vv, g: (g,) + _adamw_math(wv, g, mv, vv),
              [_tiled(packed[0]), _tiled(packed[1]), _tiled(packed[2]), _tiled(g_small)], [(F32, LANES)] * 4, n_rows=R, rows=R)
    for d, buf in zip((grads, deltas, new_m, new_v), res):
        d.update(_unpack_small(buf, w))

    return (loss, grad_x.reshape(x.shape), *[grads[n] for n in ORDER], *[deltas[n] for n in ORDER],
            *[new_m[n] for n in ORDER], *[new_v[n] for n in ORDER])
```

```python
import functools
import math

import numpy as np
import jax
import jax.numpy as jnp
from jax import lax
from jax.experimental import pallas as pl
from jax.experimental.pallas import tpu as pltpu

F32 = jnp.float32
BF16 = jnp.bfloat16
MESH = pl.DeviceIdType.MESH

NEG_INF = -1e30
EPS = 1e-6
GRID_W = 64
ROPE_THETA = 10000.0
DILATIONS = (1, 4, 16)
BAND_HALF = 64
HEAD_A = 64
HEADS_A = 8
WIDTH_A = HEADS_A * HEAD_A
HEAD_B = 128
LOG2_E = math.log2(math.e)
QK_SCALE_LOG2 = HEAD_B ** -0.5 * LOG2_E
N_BUCKETS = 32
MAX_DISTANCE = 1024
ADAM_LR, ADAM_B1, ADAM_B2, ADAM_EPS, ADAM_WD, ADAM_STEP = 0.001, 0.9, 0.999, 1e-08, 0.01, 10

B_Q, B_K, B_V = 4608, 5632, 5888
G_A, G_B = 6144, 7168
IN_WIDTH = 8192

VMEM_LIMIT_BYTES = 56 * 1024 * 1024
QB_A = 128
QB_B = 256


def _params(*sem):
    return pltpu.CompilerParams(dimension_semantics=sem, vmem_limit_bytes=VMEM_LIMIT_BYTES)


def _bs(shape, fn):
    return pl.BlockSpec(shape, fn)


def _resident(shape, fn):
    return pl.BlockSpec(shape, fn, pipeline_mode=pl.Buffered(1))


def _mm(name, grid, pairs, out_shape, out_spec, dims, *, extras=(), epilogue=None, deps=(), reds=()):
    n_pairs, n_extra, n_deps = len(pairs), len(extras), len(deps)
    operands = [p[0] for p in pairs] + [p[2] for p in pairs] + [e[0] for e in extras] + list(deps)
    in_specs = [p[1] for p in pairs] + [p[3] for p in pairs] + [e[1] for e in extras] + _any_specs(n_deps)
    single = not isinstance(out_shape, (list, tuple))
    out_shapes = [out_shape] if single else list(out_shape)
    out_specs = [out_spec] if single else list(out_spec)
    n_out = len(out_shapes)
    out_shapes += [jax.ShapeDtypeStruct((1, w), F32) for w in reds]
    out_specs += [_bs((1, w), lambda *_: (0, 0)) for w in reds]

    def body(*refs):
        a_refs, b_refs = refs[:n_pairs], refs[n_pairs:2 * n_pairs]
        e_refs = refs[2 * n_pairs:2 * n_pairs + n_extra]
        o_refs = refs[2 * n_pairs + n_extra + n_deps:]
        acc = None
        for a_ref, b_ref in zip(a_refs, b_refs):
            t = lax.dot_general(a_ref[...], b_ref[...], (dims, ((), ())), preferred_element_type=F32)
            acc = t if acc is None else acc + t
        vals = acc if epilogue is None else epilogue(acc, *[e[...] for e in e_refs])
        if not isinstance(vals, (list, tuple)):
            vals = (vals,)
        for o_ref, v in zip(o_refs[:n_out], vals[:n_out]):
            o_ref[...] = v.astype(o_ref.dtype)
        if reds:
            first = functools.reduce(jnp.logical_and, [pl.program_id(ax) == 0 for ax in range(len(grid))])
            for r_ref, v in zip(o_refs[n_out:], vals[n_out:]):
                @pl.when(first)
                def _(r_ref=r_ref):
                    r_ref[...] = jnp.zeros_like(r_ref)
                r_ref[...] += v

    sem = ["arbitrary" if reds else "parallel"] * len(grid)
    res = pl.pallas_call(
        body, out_shape=out_shapes, grid=grid, in_specs=in_specs, out_specs=out_specs,
        compiler_params=_params(*sem), name=name)(*operands)
    return res[0] if (single and not reds) else res


NN = ((1,), (0,))
NT = ((1,), (1,))
TN = ((0,), (0,))


def _mm_cols(name, a, w, *, tm, tn, out_dtype, cat, extras=(), epilogue=None):
    M, K = a.shape
    J, _, n = w.shape
    tn = min(tn, n)
    nb = n // tn
    if cat:
        shape, spec = (M, J * n), _bs((tm, tn), lambda j, i, k: (i, j * nb + k))
    else:
        shape, spec = (J, M, n), _bs((None, tm, tn), lambda j, i, k: (j, i, k))
    ex = [(e, _bs((tm, tn), lambda j, i, k: (i, j * nb + k))) for e in extras]
    return _mm(name, (J, M // tm, nb),
               [(a, _bs((tm, K), lambda j, i, k: (i, 0)), w, _bs((None, K, tn), lambda j, i, k: (j, 0, k)))],
               jax.ShapeDtypeStruct(shape, out_dtype), spec, NN, extras=ex, epilogue=epilogue)


def _mm_rows_t(name, a, w, *, tm, out_dtype):
    M, N = a.shape
    J, f, _ = w.shape
    return _mm(name, (J, M // tm),
               [(a, _bs((tm, N), lambda j, i: (i, 0)), w, _bs((None, f, N), lambda j, i: (j, 0, 0)))],
               jax.ShapeDtypeStruct((J, M, f), out_dtype), _bs((None, tm, f), lambda j, i: (j, i, 0)), NT)


def _mm_wgrad(name, a, b, *, a_cols, b_cols, tm, tn, J, deps=()):
    def pick(arr, cols, t):
        if arr.ndim == 3:
            T, c = arr.shape[1], arr.shape[2]
            t = min(t, c)
            return T, c, t, (lambda sel: _bs((None, T, t), lambda j, i, k: (j, 0, sel(i, k))))
        T = arr.shape[0]
        c = arr.shape[1] if cols is None else cols
        t = min(t, c)
        per = c // t
        if cols is None:
            if per == 1:
                return T, c, t, (lambda sel: _resident((T, t), lambda j, i, k: (0, 0)))
            return T, c, t, (lambda sel: _bs((T, t), lambda j, i, k: (0, sel(i, k))))
        return T, c, t, (lambda sel: _bs((T, t), lambda j, i, k: (0, j * per + sel(i, k))))
    _, ca, tm, mk_a = pick(a, a_cols, tm)
    _, cb, tn, mk_b = pick(b, b_cols, tn)
    return _mm(name, (J, ca // tm, cb // tn),
               [(a, mk_a(lambda i, k: i), b, mk_b(lambda i, k: k))],
               jax.ShapeDtypeStruct((J, ca, cb), BF16), _bs((None, tm, tn), lambda j, i, k: (j, i, k)), TN, deps=deps)


def _tiled(arr, width=None, col=0, rowblk=0):
    return ("t", arr, arr.shape[1] if width is None else width, col, rowblk)


def _table(arr):
    return ("f", arr)


def _whole(arr):
    return ("w", arr)


def _ew(name, fn, ins, outs, *, n_rows, rows, reds=(), ncols=1, deps=()):
    nrb = n_rows // rows
    n_deps = len(deps)
    operands, in_specs = [], []
    for spec in ins:
        if spec[0] == "t":
            _, arr, width, col, rowblk = spec
            step = 1 if ncols > 1 else 0
            in_specs.append(_bs((rows, width), lambda c, i, col=col, rowblk=rowblk, step=step: (rowblk + i, col + c * step)))
        elif spec[0] == "f":
            arr = spec[1]
            in_specs.append(_bs((rows, arr.shape[1]), lambda c, i: (i, 0)))
        else:
            arr = spec[1]
            nd = arr.ndim
            if nd == 3:
                in_specs.append(_bs((None,) + arr.shape[1:], lambda c, i: (c, 0, 0)))
            else:
                in_specs.append(_bs(arr.shape, lambda c, i, nd=nd: (0,) * nd))
        operands.append(arr)
    out_shapes = [jax.ShapeDtypeStruct((n_rows, ncols * w), dt) for dt, w in outs]
    out_specs = [_bs((rows, w), lambda c, i: (i, c)) for _, w in outs]
    out_shapes += [jax.ShapeDtypeStruct((ncols, 1, w), F32) for w in reds]
    out_specs += [_bs((None, 1, w), lambda c, i: (c, 0, 0)) for w in reds]
    n_in, n_out, n_red = len(ins), len(outs), len(reds)
    operands += list(deps)
    in_specs += _any_specs(n_deps)

    def body(*refs):
        vals = fn(*[r[...] for r in refs[:n_in]])
        if not isinstance(vals, (tuple, list)):
            vals = (vals,)
        o_refs = refs[n_in + n_deps:]
        for o_ref, v in zip(o_refs[:n_out], vals[:n_out]):
            o_ref[...] = v.astype(o_ref.dtype)
        if n_red:
            i = pl.program_id(1)
            for r_ref, v in zip(o_refs[n_out:], vals[n_out:]):
                @pl.when(i == 0)
                def _(r_ref=r_ref):
                    r_ref[...] = jnp.zeros_like(r_ref)
                r_ref[...] += v

    res = pl.pallas_call(
        body, out_shape=out_shapes, grid=(ncols, nrb), in_specs=in_specs, out_specs=out_specs,
        compiler_params=_params("parallel", "arbitrary" if n_red else "parallel"), name=name)(*operands)
    return res


def _colsum(v):
    return jnp.sum(v, axis=0, keepdims=True)


def _rstd(x):
    return lax.rsqrt(jnp.mean(x * x, axis=-1, keepdims=True) + EPS)


def _sigmoid(x):
    return 1.0 / (1.0 + jnp.exp(-x))


def _norm_fwd(x, g):
    return x * _rstd(x) * g


def _norm_bwd(x, g, dy):
    r = _rstd(x)
    xh = x * r
    dxh = dy * g
    dx = r * (dxh - xh * jnp.mean(dxh * xh, axis=-1, keepdims=True))
    return dx, dy * xh


def _row_spec(arr, rows):
    if arr.shape[0] == 1:
        return _bs(arr.shape, lambda i: (0, 0))
    return _bs((rows, arr.shape[1]), lambda i: (i, 0))


def _ffn_fwd(tag, x, gain, get_w, deps=(), *, h=None, tail_ins=(), tail_fn=None, tail_outs=(F32,), tail_reds=()):
    T, D = x.shape
    if h is None:
        (h,) = _ew(f"{tag}_norm", lambda xv, g: _norm_fwd(xv, g), [_tiled(x), _whole(gain)], [(BF16, D)], n_rows=T, rows=512,
                   deps=deps)
    w1, w3 = get_w(f"{tag}_w1", h), get_w(f"{tag}_w3", h)
    J, f, _ = w1.shape
    tm = 1024

    def up(h_ref, w1_ref, w3_ref, u_ref, g_ref, a_ref):
        hv = h_ref[...]
        u = lax.dot_general(hv, w1_ref[...], (NT, ((), ())), preferred_element_type=F32)
        g = lax.dot_general(hv, w3_ref[...], (NT, ((), ())), preferred_element_type=F32)
        u_ref[...] = u.astype(BF16)
        g_ref[...] = g.astype(BF16)
        a_ref[...] = (u * _sigmoid(u) * g).astype(BF16)

    slab = _bs((None, tm, f), lambda j, i: (j, i, 0))
    w_spec = _bs((None, f, D), lambda j, i: (j, 0, 0))
    u, g, a = pl.pallas_call(
        up, out_shape=[jax.ShapeDtypeStruct((J, T, f), BF16)] * 3, grid=(J, T // tm),
        in_specs=[_bs((tm, D), lambda j, i: (i, 0)), w_spec, w_spec], out_specs=[slab] * 3,
        compiler_params=_params("parallel", "parallel"), name=f"{tag}_up")(h, w1, w3)
    w2 = get_w(f"{tag}_w2", a)
    def tail(acc, xv, *rest):
        y = xv + 0.5 * acc
        return y if tail_fn is None else tail_fn(y, *rest)

    row = _bs((512, D), lambda i: (i, 0))
    res = _mm(f"{tag}_down", (T // 512,),
              [(a, _bs((None, 512, f), lambda i, j=j: (j, i, 0)), w2, _resident((None, f, D), lambda i, j=j: (j, 0, 0)))
               for j in range(J)],
              [jax.ShapeDtypeStruct((T, D), dt) for dt in tail_outs], [row] * len(tail_outs), NN,
              extras=[(x, row)] + [(t, _row_spec(t, 512)) for t in tail_ins], epilogue=tail, reds=tail_reds)
    return res, (h, u, g, a)


def _dh_norm_bwd(name, rows, pairs, dims, x, gain, dres, deps, also_bf16=False):
    T, D = x.shape

    def epilogue(dh, xv, gv, dr):
        dx, dgr = _norm_bwd(xv, gv, dh)
        dx = dx + dr
        return (dx, 0.5 * dx) + ((dx,) if also_bf16 else ()) + (_colsum(dgr),)

    dts = [F32, BF16] + ([BF16] if also_bf16 else [])
    row = _bs((rows, D), lambda i: (i, 0))
    return _mm(name, (T // rows,), pairs, [jax.ShapeDtypeStruct((T, D), dt) for dt in dts], [row] * len(dts), dims,
               extras=[(x, row), (gain, _row_spec(gain, rows)), (dres, row)], epilogue=epilogue, deps=deps, reds=(D,))


def _ffn_bwd(tag, x, gain, get_w, put_g, saved, dy, dy_half, also_bf16=False):
    h, u, g, a = saved
    T, D = x.shape
    w1, w3, w2 = [get_w(f"{tag}_{n}", dy_half) for n in ("w1", "w3", "w2")]
    J, f, _ = w1.shape
    dw2 = _mm_wgrad(f"{tag}_bwd_dw2", a, dy_half, a_cols=None, b_cols=None, tm=f, tn=D, J=J)
    deps = put_g({f"{tag}_w2": dw2})
    tm = 1024

    def up_bwd(dy_ref, w2_ref, u_ref, g_ref, *rest):
        du_ref, dg_ref = rest[-2:]
        da = lax.dot_general(dy_ref[...], w2_ref[...], (NT, ((), ())), preferred_element_type=F32)
        uv, gv = u_ref[...].astype(F32), g_ref[...].astype(F32)
        s = _sigmoid(uv)
        du_ref[...] = (da * gv * (s * (1.0 + uv * (1.0 - s)))).astype(BF16)
        dg_ref[...] = (da * (uv * s)).astype(BF16)

    slab = _bs((None, tm, f), lambda j, i: (j, i, 0))
    du, dg = pl.pallas_call(
        up_bwd, out_shape=[jax.ShapeDtypeStruct((J, T, f), BF16)] * 2, grid=(J, T // tm),
        in_specs=[_bs((tm, D), lambda j, i: (i, 0)), _bs((None, f, D), lambda j, i: (j, 0, 0)), slab, slab] + _any_specs(len(deps)),
        out_specs=[slab] * 2, compiler_params=_params("parallel", "parallel"), name=f"{tag}_bwd_up")(dy_half, w2, u, g, *deps)
    dw1 = _mm_wgrad(f"{tag}_bwd_dw1", du, h, a_cols=None, b_cols=None, tm=f, tn=D, J=J)
    deps = put_g({f"{tag}_w1": dw1})
    dw3 = _mm_wgrad(f"{tag}_bwd_dw3", dg, h, a_cols=None, b_cols=None, tm=f, tn=D, J=J, deps=deps)
    deps = put_g({f"{tag}_w3": dw3})
    pairs = []
    for j in range(J):
        a_spec = _bs((None, 512, f), lambda i, j=j: (j, i, 0))
        w_spec = _resident((None, f, D), lambda i, j=j: (j, 0, 0))
        pairs += [(du, a_spec, w1, w_spec), (dg, a_spec, w3, w_spec)]
    return _dh_norm_bwd(f"{tag}_bwd_dh", 512, pairs, NN, x, gain, dy, deps, also_bf16)


def _t5_bucket(rel):
    n = N_BUCKETS // 2
    max_exact = n // 2
    ret = jnp.where(rel > 0, n, 0)
    a = jnp.abs(rel)
    af = jnp.maximum(a, 1).astype(F32)
    large = max_exact + (jnp.log(af / max_exact) / math.log(MAX_DISTANCE / max_exact) * (n - max_exact)).astype(jnp.int32)
    large = jnp.minimum(large, n - 1)
    return ret + jnp.where(a < max_exact, a, large)


WIN_A = QB_A + 2 * BAND_HALF
WIN_SHIFTS = (0, BAND_HALF, 2 * BAND_HALF)


def _window_variant(n, nblk):
    return jnp.where(n == 0, 0, jnp.where(n == nblk - 1, 2, 1))


def _window_start(n, nblk):
    return pl.multiple_of(jnp.clip(n * QB_A - BAND_HALF, 0, nblk * QB_A - WIN_A), BAND_HALF)


def _band_steps(xp=jnp):
    qi = xp.arange(QB_A, dtype=xp.int32)[None, :, None]
    kj = xp.arange(WIN_A, dtype=xp.int32)[None, None, :]
    return kj - qi - xp.asarray(WIN_SHIFTS, dtype=xp.int32)[:, None, None]


def _bias_tiles(rel_bias):
    wide = QB_A + 2 * WIN_SHIFTS[-1]
    qi = jnp.arange(QB_A, dtype=jnp.int32)[:, None]
    steps = jnp.arange(wide, dtype=jnp.int32)[None, :] - WIN_SHIFTS[-1] - qi
    buckets = jnp.stack([_t5_bucket(steps * d) for d in DILATIONS])
    inband = (jnp.abs(steps) <= BAND_HALF).astype(jnp.int32)
    n_heads = rel_bias.shape[1]

    def body(tab_ref, b_ref, m_ref, o_ref):
        hd = pl.program_id(0)
        bkt = b_ref[...]
        acc = jnp.zeros(bkt.shape, F32)
        for b in range(N_BUCKETS):
            acc = jnp.where(bkt == b, tab_ref[b, hd], acc)
        o_ref[...] = jnp.where(m_ref[...] > 0, acc, NEG_INF)

    base = pl.pallas_call(
        body, out_shape=jax.ShapeDtypeStruct((n_heads, QB_A, wide), F32), grid=(n_heads,),
        in_specs=[pl.BlockSpec(memory_space=pltpu.SMEM),
                  _bs((None, QB_A, wide), lambda hd: (hd // HEADS_A, 0, 0)),
                  _bs((QB_A, wide), lambda hd: (0, 0))],
        out_specs=_bs((None, QB_A, wide), lambda hd: (hd, 0, 0)),
        compiler_params=_params("parallel"), name="a_bias_tiles")(rel_bias, buckets, inband)
    base = base.reshape(len(DILATIONS), HEADS_A, QB_A, wide)
    return jnp.stack([base[..., WIN_SHIFTS[-1] - s:WIN_SHIFTS[-1] - s + WIN_A] for s in WIN_SHIFTS], axis=1)


def _bias_grad(dbias):
    steps = _band_steps(np)
    inband = np.abs(steps) <= BAND_HALF
    present = []
    for d in DILATIONS:
        rel = steps * d
        a = np.abs(rel)
        large = 8 + (np.log(np.maximum(a, 1) / 8.0) / math.log(MAX_DISTANCE / 8.0) * 8).astype(np.int64)
        bk = np.where(rel > 0, 16, 0) + np.where(a < 8, a, np.minimum(large, 15))
        present.append([sorted(set(bk[v][inband[v]].tolist())) for v in range(3)])
    buckets = jnp.stack([_t5_bucket(_band_steps() * d) for d in DILATIONS])
    n_heads = len(DILATIONS) * HEADS_A

    def body(b_ref, d_ref, o_ref):
        row = lax.broadcasted_iota(jnp.int32, (N_BUCKETS, n_heads), 0)
        col = lax.broadcasted_iota(jnp.int32, (N_BUCKETS, n_heads), 1)
        out = jnp.zeros((N_BUCKETS, n_heads), F32)
        for grp in range(len(DILATIONS)):
            for hh in range(HEADS_A):
                hd = grp * HEADS_A + hh
                for b in sorted(set(sum(present[grp], []))):
                    tot = jnp.zeros((), F32)
                    for v in range(3):
                        if b in present[grp][v]:
                            tot = tot + jnp.sum(jnp.where(b_ref[grp, v] == b, d_ref[grp, v, hh], 0.0))
                    out = jnp.where((row == b) & (col == hd), tot, out)
        o_ref[...] = out

    return pl.pallas_call(
        body, out_shape=jax.ShapeDtypeStruct((N_BUCKETS, n_heads), F32),
        compiler_params=pltpu.CompilerParams(vmem_limit_bytes=VMEM_LIMIT_BYTES), name="a_bias_grad")(buckets, dbias)


def _lane_is_second_head(shape):
    return lax.broadcasted_iota(jnp.int32, shape, len(shape) - 1) >= HEAD_A


VIEW_ROWS = 512


def _view_chunks():
    return [pltpu.VMEM((VIEW_ROWS, LANES), F32)] * (WIDTH_A // LANES)


def _rows_to_view(x_ref, col, o_ref, ocol, d, chunks):
    n = VIEW_ROWS // d
    for c, scr in enumerate(chunks):
        scr[...] = x_ref[:, col + c * LANES:col + (c + 1) * LANES].astype(F32)
        for r in range(d):
            at = ocol + r * WIDTH_A + c * LANES
            o_ref[:, at:at + LANES] = scr[pl.ds(r, n, stride=d), :].astype(o_ref.dtype)


def _view_to_rows(v_ref, o_ref, col, d, chunks):
    n = VIEW_ROWS // d
    for c, scr in enumerate(chunks):
        if d == 1:
            o_ref[:, col + c * LANES:col + (c + 1) * LANES] = v_ref[:, c * LANES:(c + 1) * LANES].astype(o_ref.dtype)
            continue
        for r in range(d):
            scr[pl.ds(r, n, stride=d), :] = v_ref[:, r * WIDTH_A + c * LANES:r * WIDTH_A + (c + 1) * LANES].astype(F32)
        o_ref[:, col + c * LANES:col + (c + 1) * LANES] = scr[...].astype(o_ref.dtype)


def _group_view(proj, grp, d):
    T = proj.shape[0]
    if d == 1:
        return proj, (lambda part, r: grp * 3 + part)

    def body(x_ref, o_ref, *chunks):
        for part in range(3):
            _rows_to_view(x_ref, part * WIDTH_A, o_ref, part * d * WIDTH_A, d, chunks)

    view = pl.pallas_call(
        body, out_shape=jax.ShapeDtypeStruct((T // d, 3 * d * WIDTH_A), proj.dtype), grid=(T // VIEW_ROWS,),
        in_specs=[_bs((VIEW_ROWS, 3 * WIDTH_A), lambda i: (i, grp))],
        out_specs=_bs((VIEW_ROWS // d, 3 * d * WIDTH_A), lambda i: (i, 0)),
        scratch_shapes=_view_chunks(), compiler_params=_params("parallel"), name=f"a_view_d{d}")(proj)
    return view, (lambda part, r: part * d + r)


def _stack_heads(v2, second):
    zero = jnp.zeros_like(v2)
    return jnp.concatenate([jnp.where(second, zero, v2), jnp.where(second, v2, zero)], axis=0)


def _unstack_heads(v, second):
    return jnp.where(second, v[QB_A:], v[:QB_A])


def _dil_fwd(view, bias, d):
    pv, colblk = view
    L = pv.shape[0]
    nblk = L // QB_A
    W2 = 2 * HEAD_A
    scale = HEAD_A ** -0.5

    def body(q_ref, k_ref, v_ref, b_ref, o_ref, l_ref):
        win = pl.ds(_window_start(pl.program_id(1), nblk), WIN_A)
        second = _lane_is_second_head((QB_A, W2))
        pairs = range(HEADS_A // 2)
        cols = [slice(hp * W2, (hp + 1) * W2) for hp in pairs]
        s = [lax.dot_general(_stack_heads(q_ref[:, cols[hp]], second), k_ref[win, cols[hp]], (NT, ((), ())),
                             preferred_element_type=F32) * scale + b_ref[2 * hp:2 * hp + 2].reshape(2 * QB_A, WIN_A)
             for hp in pairs]
        m = [jnp.max(x, axis=-1, keepdims=True) for x in s]
        p = [jnp.exp(x - mx) for x, mx in zip(s, m)]
        l = [jnp.sum(x, axis=-1, keepdims=True) for x in p]
        res = [jnp.dot(p[hp].astype(BF16), v_ref[win, cols[hp]], preferred_element_type=F32) / l[hp] for hp in pairs]
        o_ref[...] = jnp.concatenate([_unstack_heads(x, second) for x in res], axis=1).astype(o_ref.dtype)
        l_ref[...] = jnp.concatenate([_unstack_heads(jnp.broadcast_to(mx + jnp.log(lx), (2 * QB_A, W2)), second)
                                      for mx, lx in zip(m, l)], axis=1)

    in_specs = [_bs((QB_A, WIDTH_A), lambda r, n: (n, colblk(0, r))),
                _bs((L, WIDTH_A), lambda r, n: (0, colblk(1, r))), _bs((L, WIDTH_A), lambda r, n: (0, colblk(2, r))),
                _bs((None, HEADS_A, QB_A, WIN_A), lambda r, n: (_window_variant(n, nblk), 0, 0, 0))]
    o, lse = pl.pallas_call(
        body, out_shape=[jax.ShapeDtypeStruct((L, d * WIDTH_A), BF16), jax.ShapeDtypeStruct((L, d * WIDTH_A), F32)],
        grid=(d, nblk), in_specs=in_specs,
        out_specs=[_bs((QB_A, WIDTH_A), lambda r, n: (n, r)), _bs((QB_A, WIDTH_A), lambda r, n: (n, r))],
        compiler_params=_params("parallel", "parallel"), name=f"a_fwd_d{d}")(pv, pv, pv, bias)
    return o, lse


def _dil_bwd(view_qkv, bias, do, lse, cterm, d):
    pv, colblk = view_qkv
    L = pv.shape[0]
    nblk = L // QB_A
    W2 = 2 * HEAD_A
    PPS = 4
    WS = PPS * W2
    ob = WIDTH_A // WS
    scale = HEAD_A ** -0.5

    def body(q_ref, k_ref, v_ref, do_ref, l_ref, c_ref, b_ref, dq_ref, dk_ref, dv_ref, db_ref):
        r, n = pl.program_id(1), pl.program_id(2)

        @pl.when(n == 0)
        def _():
            dk_ref[...] = jnp.zeros_like(dk_ref)
            dv_ref[...] = jnp.zeros_like(dv_ref)

        @pl.when((n == 0) & (r == 0))
        def _():
            db_ref[...] = jnp.zeros_like(db_ref)

        second = _lane_is_second_head((QB_A, W2))
        win = pl.ds(_window_start(n, nblk), WIN_A)
        variant = _window_variant(n, nblk)
        pairs = range(PPS)
        cols = [slice(pp * W2, (pp + 1) * W2) for pp in pairs]

        def head_rows(ref, pp):
            v2 = ref[:, cols[pp]]
            return jnp.concatenate([v2[:, 0:1], v2[:, HEAD_A:HEAD_A + 1]], axis=0)

        kw = [k_ref[win, c] for c in cols]
        vw = [v_ref[win, c] for c in cols]
        qs = [_stack_heads(q_ref[:, c], second) for c in cols]
        dos = [_stack_heads(do_ref[:, c], second) for c in cols]
        s = [lax.dot_general(qs[pp], kw[pp], (NT, ((), ())), preferred_element_type=F32) for pp in pairs]
        dp = [lax.dot_general(dos[pp], vw[pp], (NT, ((), ())), preferred_element_type=F32) for pp in pairs]
        p = [jnp.exp(s[pp] * scale + b_ref[2 * pp:2 * pp + 2].reshape(2 * QB_A, WIN_A) - head_rows(l_ref, pp)) for pp in pairs]
        ds = [p[pp] * (dp[pp] + head_rows(c_ref, pp)) for pp in pairs]
        db_ref[variant] += jnp.concatenate([x.reshape(2, QB_A, WIN_A) for x in ds], axis=0)
        pb = [x.astype(BF16) for x in p]
        dsb = [(x * scale).astype(BF16) for x in ds]
        dq_ref[...] = jnp.concatenate([_unstack_heads(jnp.dot(dsb[pp], kw[pp], preferred_element_type=F32), second)
                                       for pp in pairs], axis=1).astype(dq_ref.dtype)
        dk_ref[win, :] += jnp.concatenate([lax.dot_general(dsb[pp], qs[pp], (TN, ((), ())), preferred_element_type=F32)
                                           for pp in pairs], axis=1)
        dv_ref[win, :] += jnp.concatenate([lax.dot_general(pb[pp], dos[pp], (TN, ((), ())), preferred_element_type=F32)
                                           for pp in pairs], axis=1)

    kv_spec = _resident if d == 1 else _bs
    in_specs = [_bs((QB_A, WS), lambda hp, r, n: (n, colblk(0, r) * ob + hp)),
                kv_spec((L, WS), lambda hp, r, n: (0, colblk(1, r) * ob + hp)),
                kv_spec((L, WS), lambda hp, r, n: (0, colblk(2, r) * ob + hp))]
    in_specs += [_bs((QB_A, WS), lambda hp, r, n: (n, r * ob + hp))] * 3
    in_specs += [_bs((None, 2 * PPS, QB_A, WIN_A), lambda hp, r, n: (_window_variant(n, nblk), hp, 0, 0))]
    out_shape = [jax.ShapeDtypeStruct((L, d * WIDTH_A), BF16), jax.ShapeDtypeStruct((L, d * WIDTH_A), F32),
                 jax.ShapeDtypeStruct((L, d * WIDTH_A), F32), jax.ShapeDtypeStruct((3, HEADS_A, QB_A, WIN_A), F32)]
    out_specs = [_bs((QB_A, WS), lambda hp, r, n: (n, r * ob + hp)),
                 _bs((L, WS), lambda hp, r, n: (0, r * ob + hp)), _bs((L, WS), lambda hp, r, n: (0, r * ob + hp)),
                 _bs((3, 2 * PPS, QB_A, WIN_A), lambda hp, r, n: (0, hp, 0, 0))]
    dq, dk, dv, db = pl.pallas_call(
        body, out_shape=out_shape, grid=(ob, d, nblk), in_specs=in_specs, out_specs=out_specs,
        compiler_params=_params("arbitrary", "arbitrary", "arbitrary"), name=f"a_bwd_d{d}")(
            pv, pv, pv, do, lse, cterm, bias)
    return dq, dk, dv, db


def _assemble_dproj(a_parts, dq_b, dk_b, dv_b, dga, dgb):
    T = dq_b.shape[0]
    flat = [(a_parts[part][g], d) for part in range(3) for g, d in enumerate(DILATIONS)]
    rest = [dq_b, dk_b, dv_b, dga, dgb]

    def body(*refs):
        views, others = refs[:len(flat)], refs[len(flat):len(flat) + len(rest)]
        o_ref, chunks = refs[len(flat) + len(rest)], refs[len(flat) + len(rest) + 1:]
        col = 0
        for v_ref, (_, d) in zip(views, flat):
            _view_to_rows(v_ref, o_ref, col, d, chunks)
            col += WIDTH_A
        for x_ref in others:
            w = x_ref.shape[1]
            o_ref[:, col:col + w] = x_ref[...].astype(o_ref.dtype)
            col += w

    in_specs = [_bs((VIEW_ROWS // d, d * WIDTH_A), lambda i: (i, 0)) for _, d in flat]
    in_specs += [_bs((VIEW_ROWS, x.shape[1]), lambda i: (i, 0)) for x in rest]
    return pl.pallas_call(
        body, out_shape=jax.ShapeDtypeStruct((T, IN_WIDTH), BF16), grid=(T // VIEW_ROWS,), in_specs=in_specs,
        out_specs=_bs((VIEW_ROWS, IN_WIDTH), lambda i: (i, 0)), scratch_shapes=_view_chunks(),
        compiler_params=_params("parallel"), name="mix_bwd_dproj")(*[a for a, _ in flat], *rest)


def _segment_ones():
    i = np.arange(WIDTH_A)
    return jnp.asarray((i[:, None] // HEAD_A == i[None, :] // HEAD_A).astype(np.float32), dtype=BF16)


def _group_weights(l0, l1, l2):
    m = jnp.maximum(jnp.maximum(l0, l1), l2)
    e = [jnp.exp(l - m) for l in (l0, l1, l2)]
    z = e[0] + e[1] + e[2]
    return [ei / z for ei in e]


def _view_specs():
    return [_bs((VIEW_ROWS // d, d * WIDTH_A), lambda i: (i, 0)) for d in DILATIONS]


def _stage_tiles(n):
    return [pltpu.VMEM((VIEW_ROWS, WIDTH_A), F32)] * n


def _combine_fwd(outs, lses):
    T = outs[0].shape[0] * DILATIONS[0]
    n = len(DILATIONS)

    def body(*refs):
        o_refs, l_refs, oa_ref = refs[:n], refs[n:2 * n], refs[2 * n]
        o_st, l_st, chunks = refs[2 * n + 1:3 * n + 1], refs[3 * n + 1:4 * n + 1], refs[4 * n + 1:]
        for g, d in enumerate(DILATIONS):
            _view_to_rows(o_refs[g], o_st[g], 0, d, chunks)
            _view_to_rows(l_refs[g], l_st[g], 0, d, chunks)
        w = _group_weights(*[l[...] for l in l_st])
        oa_ref[...] = (w[0] * o_st[0][...] + w[1] * o_st[1][...] + w[2] * o_st[2][...]).astype(oa_ref.dtype)

    return pl.pallas_call(
        body, out_shape=jax.ShapeDtypeStruct((T, WIDTH_A), BF16), grid=(T // VIEW_ROWS,),
        in_specs=_view_specs() * 2, out_specs=_bs((VIEW_ROWS, WIDTH_A), lambda i: (i, 0)),
        scratch_shapes=_stage_tiles(2 * n) + _view_chunks(), compiler_params=_params("parallel"), name="a_combine")(*outs, *lses)


def _combine_bwd(doa, outs, lses):
    T = doa.shape[0]
    n = len(DILATIONS)

    def body(*refs):
        d_ref, o_refs, l_refs, seg_ref = refs[0], refs[1:n + 1], refs[n + 1:2 * n + 1], refs[2 * n + 1]
        do_refs, c_refs = refs[2 * n + 2:3 * n + 2], refs[3 * n + 2:4 * n + 2]
        o_st, l_st = refs[4 * n + 2:5 * n + 2], refs[5 * n + 2:6 * n + 2]
        tmp, chunks = refs[6 * n + 2], refs[6 * n + 3:]
        for g, d in enumerate(DILATIONS):
            _view_to_rows(o_refs[g], o_st[g], 0, d, chunks)
            _view_to_rows(l_refs[g], l_st[g], 0, d, chunks)
        dv = d_ref[...].astype(F32)
        w = _group_weights(*[l[...] for l in l_st])
        seg = seg_ref[...]
        tot = jnp.zeros(dv.shape, F32)
        for g in range(n):
            prod = w[g] * dv * o_st[g][...]
            hi = prod.astype(BF16)
            lo = (prod - hi.astype(F32)).astype(BF16)
            tot = tot + jnp.dot(hi, seg, preferred_element_type=F32) + jnp.dot(lo, seg, preferred_element_type=F32)
        for g, d in enumerate(DILATIONS):
            tmp[...] = w[g] * dv
            _rows_to_view(tmp, 0, do_refs[g], 0, d, chunks)
            tmp[...] = -w[g] * tot
            _rows_to_view(tmp, 0, c_refs[g], 0, d, chunks)

    views = [jax.ShapeDtypeStruct((T // d, d * WIDTH_A), dt) for dt in (BF16, F32) for d in DILATIONS]
    res = pl.pallas_call(
        body, out_shape=views, grid=(T // VIEW_ROWS,),
        in_specs=[_bs((VIEW_ROWS, WIDTH_A), lambda i: (i, 0))] + _view_specs() * 2 + [_bs((WIDTH_A, WIDTH_A), lambda i: (0, 0))],
        out_specs=_view_specs() * 2, scratch_shapes=_stage_tiles(2 * n + 1) + _view_chunks(),
        compiler_params=_params("parallel"), name="a_combine_bwd")(doa, *outs, *lses, _segment_ones())
    return res[:n], res[n:]


def _rope_tables(T):
    rows = T // GRID_W
    row = jnp.repeat(jnp.arange(rows, dtype=F32), GRID_W)
    col = jnp.tile(jnp.arange(GRID_W, dtype=F32), rows)
    n_freq = HEAD_B // 4
    freq = ROPE_THETA ** (-jnp.arange(n_freq, dtype=F32) / n_freq)
    ang = jnp.concatenate([row[:, None] * freq, col[:, None] * freq], axis=-1)
    cos, sin = jnp.repeat(jnp.cos(ang), 2, axis=1), jnp.repeat(jnp.sin(ang), 2, axis=1)
    sign = jnp.where(jnp.arange(HEAD_B) % 2 == 0, -1.0, 1.0).astype(F32)
    return cos, sin * sign


def _swap_pairs(v):
    even = lax.broadcasted_iota(jnp.int32, v.shape, v.ndim - 1) % 2 == 0
    n = v.shape[-1]
    return jnp.where(even, pltpu.roll(v, n - 1, v.ndim - 1), pltpu.roll(v, 1, v.ndim - 1))


def _qk_fwd(name, proj, col0, n_heads, gain, cos, sin, out_scale=1.0):
    T = proj.shape[0]

    def fn(xr, g, c, s):
        xn = _norm_fwd(xr.astype(F32), g)
        return (xn * c + _swap_pairs(xn) * s) * out_scale

    (out,) = _ew(name, fn, [_tiled(proj, HEAD_B, col0 // HEAD_B), _whole(gain), _table(cos), _table(sin)],
                 [(BF16, HEAD_B)], n_rows=T, rows=2048, ncols=n_heads)
    return out


def _qk_bwd(name, dout, proj, col0, n_heads, gain, cos, sin, in_scale=1.0):
    T = proj.shape[0]

    def fn(dv, xr, g, c, s):
        dv = dv.astype(F32) * in_scale
        dxn = c * dv + _swap_pairs(s * dv)
        dx, dgr = _norm_bwd(xr.astype(F32), g, dxn)
        return dx, _colsum(dgr)

    dx, dg = _ew(name, fn, [_tiled(dout, HEAD_B, 0), _tiled(proj, HEAD_B, col0 // HEAD_B), _whole(gain),
                            _table(cos), _table(sin)],
                 [(BF16, HEAD_B)], n_rows=T, rows=2048, reds=(HEAD_B,), ncols=n_heads)
    return dx, jnp.sum(dg, axis=0)


def _gqa_fwd(qn, kn, proj):
    T = qn.shape[0]
    GW = 4 * HEAD_B
    QB = QB_B

    def body(q_ref, k_ref, v_ref, o_ref, l_ref):
        k, v = k_ref[...], v_ref[...]
        lane = lax.broadcasted_iota(jnp.int32, (QB, HEAD_B), 1)
        heads = range(4)
        s = [lax.dot_general(q_ref[:, g * HEAD_B:(g + 1) * HEAD_B], k, (NT, ((), ())), preferred_element_type=F32)
             for g in heads]
        m = [jnp.max(x, axis=-1, keepdims=True) for x in s]
        p = [jnp.exp2(x - mx) for x, mx in zip(s, m)]
        l = [jnp.sum(x, axis=-1, keepdims=True) for x in p]
        o = [jnp.dot(p[g].astype(BF16), v, preferred_element_type=F32) / l[g] for g in heads]
        o_ref[...] = jnp.concatenate(o, axis=1).astype(o_ref.dtype)
        lse_all = jnp.zeros((QB, HEAD_B), F32)
        for g in heads:
            lse_all = jnp.where(lane == g, m[g] + jnp.log2(l[g]), lse_all)
        l_ref[...] = lse_all

    return pl.pallas_call(
        body, out_shape=[jax.ShapeDtypeStruct((T, 2 * GW), BF16), jax.ShapeDtypeStruct((2, T, HEAD_B), F32)],
        grid=(2, T // QB),
        in_specs=[_bs((QB, GW), lambda kv, i: (i, kv)), _bs((T, HEAD_B), lambda kv, i: (0, kv)),
                  _bs((T, HEAD_B), lambda kv, i: (0, B_V // HEAD_B + kv))],
        out_specs=[_bs((QB, GW), lambda kv, i: (i, kv)), _bs((None, QB, HEAD_B), lambda kv, i: (kv, i, 0))],
        compiler_params=_params("parallel", "parallel"), name="b_fwd")(qn, kn, proj)


def _gqa_bwd(qn, kn, proj, o, lse, do, deps=()):
    T = qn.shape[0]
    GW = 4 * HEAD_B

    def body(q_ref, k_ref, v_ref, o_ref, l_ref, do_ref, *rest):
        dq_ref, dk_ref, dv_ref = rest[-3:]
        i = pl.program_id(1)

        @pl.when(i == 0)
        def _():
            dk_ref[...] = jnp.zeros_like(dk_ref)
            dv_ref[...] = jnp.zeros_like(dv_ref)

        k, v = k_ref[...], v_ref[...]
        lse_all = l_ref[...]
        for g in range(4):
            cols = slice(g * HEAD_B, (g + 1) * HEAD_B)
            q, dob = q_ref[:, cols], do_ref[:, cols]
            delta = jnp.sum(dob.astype(F32) * o_ref[:, cols].astype(F32), axis=-1, keepdims=True)
            s = lax.dot_general(q, k, (NT, ((), ())), preferred_element_type=F32)
            p = jnp.exp2(s - lse_all[:, g:g + 1])
            dp = lax.dot_general(dob, v, (NT, ((), ())), preferred_element_type=F32)
            ds = (p * (dp - delta)).astype(BF16)
            dq_ref[:, cols] = jnp.dot(ds, k, preferred_element_type=F32).astype(dq_ref.dtype)
            dk_ref[...] += lax.dot_general(ds, q, (TN, ((), ())), preferred_element_type=F32)
            dv_ref[...] += lax.dot_general(p.astype(BF16), dob, (TN, ((), ())), preferred_element_type=F32)

    return pl.pallas_call(
        body, out_shape=[jax.ShapeDtypeStruct((T, 2 * GW), BF16), jax.ShapeDtypeStruct((T, 2 * HEAD_B), F32),
                         jax.ShapeDtypeStruct((T, 2 * HEAD_B), F32)],
        grid=(2, T // QB_B),
        in_specs=[_bs((QB_B, GW), lambda kv, i: (i, kv)), _bs((T, HEAD_B), lambda kv, i: (0, kv)),
                  _bs((T, HEAD_B), lambda kv, i: (0, B_V // HEAD_B + kv)), _bs((QB_B, GW), lambda kv, i: (i, kv)),
                  _bs((None, QB_B, HEAD_B), lambda kv, i: (kv, i, 0)), _bs((QB_B, GW), lambda kv, i: (i, kv))] + _any_specs(len(deps)),
        out_specs=[_bs((QB_B, GW), lambda kv, i: (i, kv)), _bs((T, HEAD_B), lambda kv, i: (0, kv)),
                   _bs((T, HEAD_B), lambda kv, i: (0, kv))],
        compiler_params=_params("parallel", "arbitrary"), name="b_bwd")(qn, kn, proj, o, lse, do, *deps)


def _local_step(x, target, small, get_w, put_g, deps=()):
    T, D = x.shape
    gs = {}

    bias = _bias_tiles(small["rel_bias"])
    cos, sin = _rope_tables(T)
    (x1, h2), ffn1_saved = _ffn_fwd("ffn1", x, small["ffn1_norm"], lambda name, after: get_w(name, [after, bias, cos, sin]), deps,
                                    tail_ins=[small["mix_norm"]], tail_fn=lambda y, g: (y, _norm_fwd(y, g)), tail_outs=(F32, BF16))
    w_in = get_w("w_in", h2)
    nq = w_in.shape[2]
    tpq = nq // WIDTH_A

    def proj_tile(j, k):
        c = j * tpq + k
        return jnp.where(c < 3 * len(DILATIONS), (c % 3) * 3 + c // 3, c)

    proj = _mm("mix_in", (4, tpq),
               [(h2, _resident((T, D), lambda j, k: (0, 0)), w_in, _bs((None, D, WIDTH_A), lambda j, k: (j, 0, k)))],
               jax.ShapeDtypeStruct((T, IN_WIDTH), BF16), _bs((T, WIDTH_A), lambda j, k: (0, proj_tile(j, k))), NN)

    a_views = [_group_view(proj, grp, d) for grp, d in enumerate(DILATIONS)]
    a_outs, a_lses = [], []
    for grp, d in enumerate(DILATIONS):
        o, l = _dil_fwd(a_views[grp], bias[grp], d)
        a_outs.append(o)
        a_lses.append(l)
    o_a = _combine_fwd(a_outs, a_lses)

    qn =_qk_fwd("b_qnorm", proj, B_Q, 8, small["q_norm"], cos, sin, out_scale=QK_SCALE_LOG2)
    kn = _qk_fwd("b_knorm", proj, B_K, 2, small["k_norm"], cos, sin)
    o_b, lse_b = _gqa_fwd(qn, kn, proj)

    wa, wb, wo = get_w("w_branch_a", o_b), get_w("w_branch_b", o_b), get_w("w_out", o_b)
    bg_a, bg_b = small["b_gate"][:, :D], small["b_gate"][:, D:]
    n_a = wa.shape[0]

    def merge_out(oa_ref, ob_ref, ga_ref, gb_ref, x1_ref, wa_ref, wb_ref, wo_ref, ba_ref, bb_ref, g2_ref,
                  ta_ref, tb_ref, mg_ref, x2_ref, hn_ref):
        oa = oa_ref[...]
        ta = jnp.concatenate([jnp.dot(oa, wa_ref[j], preferred_element_type=F32) for j in range(n_a)], axis=1)
        tb = jnp.dot(ob_ref[...], wb_ref[...], preferred_element_type=F32)
        sa = _sigmoid(ga_ref[...].astype(F32) + ba_ref[...])
        sb = _sigmoid(gb_ref[...].astype(F32) + bb_ref[...])
        merged = (sa * ta + sb * tb).astype(BF16)
        ta_ref[...], tb_ref[...], mg_ref[...] = ta.astype(BF16), tb.astype(BF16), merged
        y = x1_ref[...] + jnp.dot(merged, wo_ref[...], preferred_element_type=F32)
        x2_ref[...] = y
        hn_ref[...] = _norm_fwd(y, g2_ref[...]).astype(BF16)

    row = _bs((512, D), lambda i: (i, 0))
    gate_specs = [_bs((512, D), lambda i: (i, G_A // D)), _bs((512, D), lambda i: (i, G_B // D))]
    whole2, whole3 = (lambda i: (0, 0)), (lambda i: (0, 0, 0))
    vec = _bs((1, D), whole2)
    t_a, t_b, merged, x2, hn2 = pl.pallas_call(
        merge_out, out_shape=[jax.ShapeDtypeStruct((T, D), BF16)] * 3 + [jax.ShapeDtypeStruct((T, D), F32), jax.ShapeDtypeStruct((T, D), BF16)],
        grid=(T // 512,),
        in_specs=[_bs((512, WIDTH_A), lambda i: (i, 0)), row] + gate_specs + [row, _resident(wa.shape, whole3), _resident((D, D), whole2),
                                                                                _resident((D, D), whole2), vec, vec, vec],
        out_specs=[row] * 5, compiler_params=_params("parallel"), name="mix_merge_out")(
            o_a, o_b, proj, proj, x1, wa, wb, wo, bg_a, bg_b, small["ffn2_norm"])

    def head(xv, g, tv):
        r = _rstd(xv)
        xh = xv * r
        e = xh * g - tv
        dy = e * (1.0 / D)
        dxh = dy * g
        dx = r * (dxh - xh * jnp.mean(dxh * xh, axis=-1, keepdims=True))
        return dx, 0.5 * dx, _colsum(e * e) * (0.5 / D), _colsum(dy * xh)

    (dx3, dx3_half, loss_cols, g_final), ffn2_saved = _ffn_fwd(
        "ffn2", x2, small["ffn2_norm"], get_w, h=hn2, tail_ins=[small["final_norm"].reshape(1, D), target], tail_fn=head,
        tail_outs=(F32, BF16), tail_reds=(D, D))
    gs["final_norm"] = g_final.reshape(D)

    dx2, _, dmix, gs["ffn2_norm"] = _ffn_bwd("ffn2", x2, small["ffn2_norm"], get_w, put_g, ffn2_saved, dx3, dx3_half,
                                             also_bf16=True)
    g_out = _mm_wgrad("mix_bwd_dwout", merged, dmix, a_cols=D // 4, b_cols=None, tm=256, tn=512, J=4).reshape(D, D)

    def merge_out_bwd(dx_ref, ta_ref, tb_ref, ga_ref, gb_ref, wa_ref, wb_ref, wo_ref, ba_ref, bb_ref,
                      dta_ref, dtb_ref, dga_ref, dgb_ref, doa_ref, dob_ref, dba_ref, dbb_ref):
        dm = lax.dot_general(dx_ref[...], wo_ref[...], (NT, ((), ())), preferred_element_type=F32)
        ta, tb = ta_ref[...].astype(F32), tb_ref[...].astype(F32)
        sa = _sigmoid(ga_ref[...].astype(F32) + ba_ref[...])
        sb = _sigmoid(gb_ref[...].astype(F32) + bb_ref[...])
        dga, dgb = dm * ta * sa * (1.0 - sa), dm * tb * sb * (1.0 - sb)
        dta, dtb = (dm * sa).astype(BF16), (dm * sb).astype(BF16)
        dta_ref[...], dtb_ref[...] = dta, dtb
        dga_ref[...], dgb_ref[...] = dga.astype(BF16), dgb.astype(BF16)
        w = wa_ref.shape[2]
        doa = sum(lax.dot_general(dta[:, j * w:(j + 1) * w], wa_ref[j], (NT, ((), ())), preferred_element_type=F32) for j in range(n_a))
        doa_ref[...] = doa.astype(BF16)
        dob_ref[...] = lax.dot_general(dtb, wb_ref[...], (NT, ((), ())), preferred_element_type=F32).astype(BF16)

        @pl.when(pl.program_id(0) == 0)
        def _():
            dba_ref[...] = jnp.zeros_like(dba_ref)
            dbb_ref[...] = jnp.zeros_like(dbb_ref)
        dba_ref[...] += _colsum(dga)
        dbb_ref[...] += _colsum(dgb)

    rowb = _bs((256, D), lambda i: (i, 0))
    gate_specs = [_bs((256, D), lambda i: (i, G_A // D)), _bs((256, D), lambda i: (i, G_B // D))]
    dta, dtb, dga, dgb, do_a, do_b, dba, dbb = pl.pallas_call(
        merge_out_bwd,
        out_shape=[jax.ShapeDtypeStruct((T, D), BF16)] * 4 + [jax.ShapeDtypeStruct((T, WIDTH_A), BF16), jax.ShapeDtypeStruct((T, D), BF16)]
        + [jax.ShapeDtypeStruct((1, D), F32)] * 2,
        grid=(T // 256,),
        in_specs=[rowb, rowb, rowb] + gate_specs + [_resident(wa.shape, whole3), _resident((D, D), whole2), _resident((D, D), whole2), vec, vec],
        out_specs=[rowb] * 4 + [_bs((256, WIDTH_A), lambda i: (i, 0)), rowb, vec, vec],
        compiler_params=_params("arbitrary"), name="mix_merge_out_bwd")(dmix, t_a, t_b, proj, proj, wa, wb, wo, bg_a, bg_b)
    gs["b_gate"] = jnp.concatenate([dba, dbb], axis=1)

    g_a = _mm_wgrad("mix_bwd_dwa", o_a, dta, a_cols=None, b_cols=D // 4, tm=WIDTH_A, tn=256, J=4)
    g_b = _mm_wgrad("mix_bwd_dwb", o_b, dtb, a_cols=D // 4, b_cols=None, tm=256, tn=512, J=4).reshape(D, D)
    deps = put_g({"w_out": g_out, "w_branch_a": g_a, "w_branch_b": g_b})

    dqn, dkn, dv_b = _gqa_bwd(qn, kn, proj, o_b, lse_b, do_b, deps)
    dq_b, gs["q_norm"] = _qk_bwd("b_bwd_qnorm", dqn, proj, B_Q, 8, small["q_norm"], cos, sin, in_scale=HEAD_B ** -0.5)
    dk_b, gs["k_norm"] = _qk_bwd("b_bwd_knorm", dkn, proj, B_K, 2, small["k_norm"], cos, sin, in_scale=1.0 / LOG2_E)

    do_groups, c_groups = _combine_bwd(do_a, a_outs, a_lses)
    dqs, dks, dvs, dbs = [], [], [], []
    for grp, d in enumerate(DILATIONS):
        dq, dk, dv, db = _dil_bwd(a_views[grp], bias[grp], do_groups[grp], a_lses[grp], c_groups[grp], d)
        dqs.append(dq), dks.append(dk), dvs.append(dv), dbs.append(db)
    gs["rel_bias"] = _bias_grad(jnp.stack(dbs))

    dproj = _assemble_dproj([dqs, dks, dvs], dq_b, dk_b, dv_b, dga, dgb)
    nq = w_in.shape[2]
    g_in = _mm("mix_bwd_dwin", (4, tpq),
               [(h2, _resident((T, D), lambda j, k: (0, 0)), dproj, _bs((T, WIDTH_A), lambda j, k: (0, j * tpq + k)))],
               jax.ShapeDtypeStruct((4, D, nq), BF16), _bs((None, D, WIDTH_A), lambda j, k: (j, 0, k)), TN)
    deps = put_g({"w_in": g_in})
    dx1, dx1_half, gs["mix_norm"] = _dh_norm_bwd(
        "mix_bwd_dh", 256,
        [(dproj, _bs((256, nq), lambda i, j=j: (i, j)), w_in, _resident((None, D, nq), lambda i, j=j: (j, 0, 0))) for j in range(4)],
        NT, x1, small["mix_norm"], dx2, deps)

    dx0, _, gs["ffn1_norm"] = _ffn_bwd("ffn1", x, small["ffn1_norm"], get_w, put_g, ffn1_saved, dx1, dx1_half)
    return loss_cols, dx0, gs


def _position():
    return lax.axis_index("x"), lax.axis_index("y"), lax.axis_index("c")


def _any_specs(n):
    return [pl.BlockSpec(memory_space=pl.ANY)] * n


HBM_SPEC = pl.BlockSpec(memory_space=pltpu.HBM)
SEM_SPEC = pl.BlockSpec(memory_space=pltpu.SEMAPHORE)
DATAFLOW_EFFECT = pltpu.SideEffectType.DATAFLOW_SIDE_EFFECTING
N_PEER_CHIPS = 3
LANES = 128


def _quarter_copies(srcs, lands, send_sems, recv_sems, mode):
    x, y, c = _position()
    me = 2 * x + y
    peers = [(1 - x, y, c), (x, 1 - y, c), (1 - x, 1 - y, c)]
    copies = []
    for src, land, send, recv in zip(srcs, lands, send_sems, recv_sems):
        if mode == "sibling":
            copies.append(pltpu.make_async_remote_copy(src_ref=src, dst_ref=land, send_sem=send.at[0], recv_sem=recv.at[0],
                                                       device_id=(x, y, 1 - c), device_id_type=MESH))
            continue
        scatter = mode == "scatter"
        half = land.shape[1] // 2
        mine = land.at[me, pl.ds(c * half, half)]
        for p, (px, py, pc) in enumerate(peers):
            copies.append(pltpu.make_async_remote_copy(
                src_ref=src.at[2 * px + py] if scatter else mine, dst_ref=land.at[me] if scatter else mine,
                send_sem=send.at[p], recv_sem=recv.at[p], device_id=(px, py, pc), device_id_type=MESH))
    return copies


def _fill_from_sibling(name, stacks):
    n = len(stacks)

    def body(*refs):
        outs = refs[n:2 * n]
        send_sems, recv_sems = refs[2 * n:]
        x, y, c = _position()
        copies = []
        for i, ref in enumerate(outs):
            half = ref.shape[1] // 2
            rows = pl.ds(c * half, half)
            for p, k in enumerate((2 * (1 - x) + y, 2 * x + (1 - y), 2 * (1 - x) + (1 - y))):
                cp = pltpu.make_async_remote_copy(ref.at[k, rows], ref.at[k, rows], send_sems.at[3 * i + p], recv_sems.at[3 * i + p],
                                                  device_id=(x, y, 1 - c), device_id_type=MESH)
                cp.start()
                copies.append(cp)
        for cp in copies:
            cp.wait()

    return pl.pallas_call(
        body, out_shape=[jax.ShapeDtypeStruct(s.shape, s.dtype) for s in stacks],
        in_specs=_any_specs(n), out_specs=_any_specs(n), input_output_aliases={i: i for i in range(n)},
        scratch_shapes=[pltpu.SemaphoreType.DMA((N_PEER_CHIPS * n,)), pltpu.SemaphoreType.DMA((N_PEER_CHIPS * n,))],
        compiler_params=pltpu.CompilerParams(has_side_effects=True), name=name)(*stacks)


def _exchange_start(name, srcs, lands, mode):
    n = len(lands)
    arrays = list(lands) if srcs is None else list(srcs) + list(lands)
    k = len(arrays)

    def body(*refs):
        land_refs = refs[k - n:k]
        send_sems, recv_sems = refs[k:k + n], refs[k + n:k + 2 * n]
        token = refs[2 * k + 2 * n]
        for cp in _quarter_copies(refs[:n], land_refs, send_sems, recv_sems, mode):
            cp.start()
        token[...] = jnp.zeros_like(token)

    sem = pltpu.SemaphoreType.DMA((N_PEER_CHIPS,))
    out_shape = [sem] * (2 * n) + [pltpu.HBM(a.shape, a.dtype) for a in arrays] + [jax.ShapeDtypeStruct((8, LANES), F32)]
    res = pl.pallas_call(
        body, name=name, out_shape=out_shape, in_specs=[HBM_SPEC] * k,
        out_specs=[SEM_SPEC] * (2 * n) + [HBM_SPEC] * k + [pl.BlockSpec(memory_space=pltpu.VMEM)],
        input_output_aliases={i: 2 * n + i for i in range(k)},
        compiler_params=pltpu.CompilerParams(has_side_effects=DATAFLOW_EFFECT),
    )(*[pltpu.with_memory_space_constraint(a, pltpu.HBM) for a in arrays])
    thru = res[2 * n:2 * n + k]
    return res[:n], res[n:2 * n], (None if srcs is None else thru[:n]), thru[k - n:], res[2 * n + k]


def _exchange_wait(name, srcs, lands, send_sems, recv_sems, after, mode):
    n = len(lands)
    arrays = list(lands) if srcs is None else list(srcs) + list(lands)
    k = len(arrays)
    after = list(after) if isinstance(after, (list, tuple)) else [after]

    def body(*refs):
        sends, recvs = refs[k:k + n], refs[k + n:k + 2 * n]
        for cp in _quarter_copies(refs[:n], refs[k - n:k], sends, recvs, mode):
            cp.wait_send()
            cp.wait_recv()

    res = pl.pallas_call(
        body, name=name, out_shape=[pltpu.HBM(a.shape, a.dtype) for a in arrays],
        in_specs=[HBM_SPEC] * k + [SEM_SPEC] * (2 * n) + _any_specs(len(after)),
        out_specs=[HBM_SPEC] * k, input_output_aliases={i: i for i in range(k)},
        compiler_params=pltpu.CompilerParams(has_side_effects=DATAFLOW_EFFECT),
    )(*arrays, *send_sems, *recv_sems, *after)
    return (None if srcs is None else res[:n]), res[k - n:]


def _own_slots(name, srcs, from_stack=False):
    n = len(srcs)
    me = (2 * lax.axis_index("x") + lax.axis_index("y")).astype(jnp.int32).reshape(1)

    def body(me_ref, *refs):
        for x_ref, o_ref in zip(refs[:n], refs[n:]):
            o_ref[...] = x_ref[...].astype(o_ref.dtype)

    in_specs, out_specs, out_shape = [], [], []
    for src in srcs:
        R, C = src.shape[-2:]
        in_specs.append(pl.BlockSpec((None, R // 2, C), lambda i, me_ref: (me_ref[0], i, 0)) if from_stack
                        else pl.BlockSpec((R // 2, C), lambda i, me_ref: (i, 0)))
        out_specs.append(pl.BlockSpec((None, R // 2, C), lambda i, me_ref: (me_ref[0], i, 0)))
        out_shape.append(jax.ShapeDtypeStruct((4, R, C), BF16))
    grid_spec = pltpu.PrefetchScalarGridSpec(num_scalar_prefetch=1, grid=(2,), in_specs=in_specs, out_specs=out_specs)
    return pl.pallas_call(body, out_shape=out_shape, grid_spec=grid_spec, compiler_params=_params("parallel"), name=name)(me, *srcs)


def _allreduce_small(buf):
    R, C = buf.shape
    flips = [(fx, fy, fc) for fx in (0, 1) for fy in (0, 1) for fc in (0, 1)][1:]

    def body(in_ref, out_ref, land_ref, send_sems, recv_sems):
        x, y, c = _position()
        me = 4 * x + 2 * y + c
        copies = []
        for k, (fx, fy, fc) in enumerate(flips):
            px, py, pc = (1 - x if fx else x), (1 - y if fy else y), (1 - c if fc else c)
            cp = pltpu.make_async_remote_copy(in_ref, land_ref.at[me], send_sems.at[k], recv_sems.at[k],
                                              device_id=(px, py, pc), device_id_type=MESH)
            cp.start()
            copies.append(cp)
        land_ref[me] = in_ref[...]
        for cp in copies:
            cp.wait()
        acc = land_ref[0]
        for k in range(1, 8):
            acc = acc + land_ref[k]
        out_ref[...] = acc

    return pl.pallas_call(
        body, out_shape=jax.ShapeDtypeStruct((R, C), F32),
        in_specs=[pl.BlockSpec(memory_space=pltpu.VMEM)], out_specs=pl.BlockSpec(memory_space=pltpu.VMEM),
        scratch_shapes=[pltpu.VMEM((8, R, C), F32), pltpu.SemaphoreType.DMA((7,)), pltpu.SemaphoreType.DMA((7,))],
        compiler_params=pltpu.CompilerParams(has_side_effects=True), name="allreduce_small")(buf)


def _adamw_math(w, g, m, v):
    m2 = ADAM_B1 * m + (1.0 - ADAM_B1) * g
    v2 = ADAM_B2 * v + (1.0 - ADAM_B2) * (g * g)
    m_hat = m2 / (1.0 - ADAM_B1 ** ADAM_STEP)
    v_hat = v2 / (1.0 - ADAM_B2 ** ADAM_STEP)
    delta = -ADAM_LR * (m_hat / (jnp.sqrt(v_hat) + ADAM_EPS) + ADAM_WD * w)
    return delta, m2, v2


def _adamw_big(name, w, m, v, mine, theirs):
    R, C = w.shape
    rows = 256 if R % 256 == 0 else R // 2
    nrb = R // rows

    def four(a, b, c, d):
        return ((a.astype(F32) + b.astype(F32)) + c.astype(F32)) + d.astype(F32)

    def fn(wv, mv, vv, *parts):
        g = four(*parts[:4]) + four(*parts[4:])
        return (g,) + _adamw_math(wv, g, mv, vv)

    slots = [_tiled(s.reshape(4 * R, C), None, 0, k * nrb) for s in (mine, theirs) for k in range(4)]
    return _ew(name, fn, [_tiled(w), _tiled(m), _tiled(v)] + slots, [(F32, C)] * 4, n_rows=R, rows=rows)


BIG = ("ffn1_w1", "ffn1_w3", "ffn1_w2", "w_in", "w_branch_a", "w_branch_b", "w_out", "ffn2_w1", "ffn2_w3", "ffn2_w2")
SMALL = ("ffn1_norm", "mix_norm", "b_gate", "q_norm", "k_norm", "rel_bias", "ffn2_norm", "final_norm")
ORDER = ("ffn1_norm", "ffn1_w1", "ffn1_w3", "ffn1_w2", "mix_norm", "w_in", "b_gate", "q_norm", "k_norm", "rel_bias",
         "w_branch_a", "w_branch_b", "w_out", "ffn2_norm", "ffn2_w1", "ffn2_w3", "ffn2_w2", "final_norm")
TRANSPOSED = ("ffn1_w1", "ffn1_w3", "ffn2_w1", "ffn2_w3")
SIBLING_LAG = 2
GATHER_GROUPS = (("ffn1_w1", "ffn1_w3"), ("ffn1_w2",), ("w_in",), ("w_branch_a", "w_branch_b", "w_out"),
                 ("ffn2_w1", "ffn2_w3", "ffn2_w2"))


def _pack_small(d):
    rows = []
    for n in SMALL:
        flat = d[n].reshape(-1)
        pad = (-flat.shape[0]) % LANES
        rows.append(jnp.pad(flat, (0, pad)).reshape(-1, LANES))
    buf = jnp.concatenate(rows, axis=0)
    return jnp.pad(buf, ((0, (-buf.shape[0]) % 8), (0, 0)))


def _unpack_small(buf, like):
    out, r = {}, 0
    for n in SMALL:
        size = like[n].size
        nr = -(-size // LANES)
        out[n] = buf[r:r + nr].reshape(-1)[:size].reshape(like[n].shape)
        r += nr
    return out


def kernel(x, ffn1_norm, ffn1_w1, ffn1_w3, ffn1_w2, mix_norm, w_in, b_gate, q_norm, k_norm, rel_bias, w_branch_a, w_branch_b, w_out, ffn2_norm, ffn2_w1, ffn2_w3, ffn2_w2, final_norm, loss_target, m_ffn1_norm, m_ffn1_w1, m_ffn1_w3, m_ffn1_w2, m_mix_norm, m_w_in, m_b_gate, m_q_norm, m_k_norm, m_rel_bias, m_w_branch_a, m_w_branch_b, m_w_out, m_ffn2_norm, m_ffn2_w1, m_ffn2_w3, m_ffn2_w2, m_final_norm, v_ffn1_norm, v_ffn1_w1, v_ffn1_w3, v_ffn1_w2, v_mix_norm, v_w_in, v_b_gate, v_q_norm, v_k_norm, v_rel_bias, v_w_branch_a, v_w_branch_b, v_w_out, v_ffn2_norm, v_ffn2_w1, v_ffn2_w3, v_ffn2_w2, v_final_norm):
    given = dict(locals())
    w = {n: given[n] for n in ORDER}
    m = {n: given["m_" + n] for n in ORDER}
    v = {n: given["v_" + n] for n in ORDER}
    T, D = x.shape[1], x.shape[2]

    def stored(a, n):
        a = a.reshape(a.shape[1:])
        return a.T if n in TRANSPOSED else a

    def returned(a, n):
        return (a.T if n in TRANSPOSED else a).reshape(w[n].shape)

    quarter = {n: stored(w[n], n) for n in BIG}
    send, recv, _, land_thru, token = _exchange_start(
        "gather_start", None, _own_slots("own_weights", [quarter[n] for n in BIG]), "gather")
    index = {n: i for i, n in enumerate(BIG)}
    ready = {}

    def get_w(name, after):
        if name not in ready:
            group = next(g for g in GATHER_GROUPS if name in g)
            ids = [index[n] for n in group]
            _, stacks = _exchange_wait("gather_wait_" + group[0], None, [land_thru[i] for i in ids],
                                       [send[i] for i in ids], [recv[i] for i in ids], after, "gather")
            stacks = _fill_from_sibling("gather_fill_" + group[0], stacks)
            for n, st in zip(group, stacks):
                ready[n] = st.reshape(D, D) if n in ("w_branch_b", "w_out") else st
        return ready[name]

    scattered, forwarded = [], []

    def forward_oldest(after):
        names, s_sem, r_sem, srcs, lands = scattered.pop(0)
        _, landed = _exchange_wait("scatter_wait_" + names[0], srcs, lands, s_sem, r_sem, after, "scatter")
        started = _exchange_start("sibling_start_" + names[0], landed, [lax.empty(a.shape, a.dtype) for a in landed], "sibling")
        forwarded.append((names,) + tuple(started[:4]))
        return started[4]

    def put_g(grads):
        names = list(grads)
        stacks = [grads[n].reshape((4,) + quarter[n].shape) for n in names]
        lands = _own_slots("own_grad_" + names[0], stacks, from_stack=True)
        started = _exchange_start("scatter_start_" + names[0], stacks, lands, "scatter")
        scattered.append((names,) + tuple(started[:4]))
        tokens = [started[4]]
        if len(scattered) > SIBLING_LAG:
            tokens.append(forward_oldest(started[4]))
        return tokens

    small = {n: w[n] for n in SMALL}
    packed = [_pack_small({n: d[n] for n in SMALL}) for d in (w, m, v)]
    loss_cols, grad_x, gs = _local_step(x.reshape(T, D), loss_target.reshape(T, D), small, get_w, put_g, deps=[token] + packed)

    after = grad_x
    while scattered:
        after = forward_oldest(after)
    grads, deltas, new_m, new_v = {}, {}, {}, {}
    for names, s_sem, r_sem, srcs, lands in forwarded:
        mine, theirs = _exchange_wait("sibling_wait_" + names[0], srcs, lands, s_sem, r_sem, after, "sibling")
        for n, a, b in zip(names, mine, theirs):
            res = _adamw_big(f"adamw_{n}", quarter[n], stored(m[n], n), stored(v[n], n), a, b)
            grads[n], deltas[n], new_m[n], new_v[n] = [returned(r, n) for r in res]

    gs = {n: gs[n].reshape(w[n].shape) for n in SMALL}
    packed_g = _pack_small(gs)
    n_small = packed_g.shape[0]
    summed = _allreduce_small(jnp.concatenate([packed_g, loss_cols.reshape(-1, LANES)], axis=0))
    g_small, loss = summed[:n_small], jnp.sum(summed[n_small:])
    R = g_small.shape[0]
    res = _ew("adamw_small", lambda wv, mv, vv, g: (g,) + _adamw_math(wv, g, mv, vv),
              [_tiled(packed[0]), _tiled(packed[1]), _tiled(packed[2]), _tiled(g_small)], [(F32, LANES)] * 4, n_rows=R, rows=R)
    for d, buf in zip((grads, deltas, new_m, new_v), res):
        d.update(_unpack_small(buf, w))

    return (loss, grad_x.reshape(x.shape), *[grads[n] for n in ORDER], *[deltas[n] for n in ORDER],
            *[new_m[n] for n in ORDER], *[new_v[n] for n in ORDER])
```

```python
import functools
import math

import numpy as np
import jax
import jax.numpy as jnp
from jax import lax
from jax.experimental import pallas as pl
from jax.experimental.pallas import tpu as pltpu

F32 = jnp.float32
BF16 = jnp.bfloat16
MESH = pl.DeviceIdType.MESH

NEG_INF = -1e30
EPS = 1e-6
GRID_W = 64
ROPE_THETA = 10000.0
DILATIONS = (1, 4, 16)
BAND_HALF = 64
HEAD_A = 64
HEADS_A = 8
WIDTH_A = HEADS_A * HEAD_A
HEAD_B = 128
LOG2_E = math.log2(math.e)
QK_SCALE_LOG2 = HEAD_B ** -0.5 * LOG2_E
N_BUCKETS = 32
MAX_DISTANCE = 1024
ADAM_LR, ADAM_B1, ADAM_B2, ADAM_EPS, ADAM_WD, ADAM_STEP = 0.001, 0.9, 0.999, 1e-08, 0.01, 10

B_Q, B_K, B_V = 4608, 5632, 5888
G_A, G_B = 6144, 7168
IN_WIDTH = 8192

VMEM_LIMIT_BYTES = 56 * 1024 * 1024
QB_A = 128
QB_B = 256


def _params(*sem):
    return pltpu.CompilerParams(dimension_semantics=sem, vmem_limit_bytes=VMEM_LIMIT_BYTES)


def _bs(shape, fn):
    return pl.BlockSpec(shape, fn)


def _resident(shape, fn):
    return pl.BlockSpec(shape, fn, pipeline_mode=pl.Buffered(1))


def _mm(name, grid, pairs, out_shape, out_spec, dims, *, extras=(), epilogue=None, deps=(), reds=()):
    n_pairs, n_extra, n_deps = len(pairs), len(extras), len(deps)
    operands = [p[0] for p in pairs] + [p[2] for p in pairs] + [e[0] for e in extras] + list(deps)
    in_specs = [p[1] for p in pairs] + [p[3] for p in pairs] + [e[1] for e in extras] + _any_specs(n_deps)
    single = not isinstance(out_shape, (list, tuple))
    out_shapes = [out_shape] if single else list(out_shape)
    out_specs = [out_spec] if single else list(out_spec)
    n_out = len(out_shapes)
    out_shapes += [jax.ShapeDtypeStruct((1, w), F32) for w in reds]
    out_specs += [_bs((1, w), lambda *_: (0, 0)) for w in reds]

    def body(*refs):
        a_refs, b_refs = refs[:n_pairs], refs[n_pairs:2 * n_pairs]
        e_refs = refs[2 * n_pairs:2 * n_pairs + n_extra]
        o_refs = refs[2 * n_pairs + n_extra + n_deps:]
        acc = None
        for a_ref, b_ref in zip(a_refs, b_refs):
            t = lax.dot_general(a_ref[...], b_ref[...], (dims, ((), ())), preferred_element_type=F32)
            acc = t if acc is None else acc + t
        vals = acc if epilogue is None else epilogue(acc, *[e[...] for e in e_refs])
        if not isinstance(vals, (list, tuple)):
            vals = (vals,)
        for o_ref, v in zip(o_refs[:n_out], vals[:n_out]):
            o_ref[...] = v.astype(o_ref.dtype)
        if reds:
            first = functools.reduce(jnp.logical_and, [pl.program_id(ax) == 0 for ax in range(len(grid))])
            for r_ref, v in zip(o_refs[n_out:], vals[n_out:]):
                @pl.when(first)
                def _(r_ref=r_ref):
                    r_ref[...] = jnp.zeros_like(r_ref)
                r_ref[...] += v

    sem = ["arbitrary" if reds else "parallel"] * len(grid)
    res = pl.pallas_call(
        body, out_shape=out_shapes, grid=grid, in_specs=in_specs, out_specs=out_specs,
        compiler_params=_params(*sem), name=name)(*operands)
    return res[0] if (single and not reds) else res


NN = ((1,), (0,))
NT = ((1,), (1,))
TN = ((0,), (0,))


def _mm_cols(name, a, w, *, tm, tn, out_dtype, cat, extras=(), epilogue=None):
    M, K = a.shape
    J, _, n = w.shape
    tn = min(tn, n)
    nb = n // tn
    if cat:
        shape, spec = (M, J * n), _bs((tm, tn), lambda j, i, k: (i, j * nb + k))
    else:
        shape, spec = (J, M, n), _bs((None, tm, tn), lambda j, i, k: (j, i, k))
    ex = [(e, _bs((tm, tn), lambda j, i, k: (i, j * nb + k))) for e in extras]
    return _mm(name, (J, M // tm, nb),
               [(a, _bs((tm, K), lambda j, i, k: (i, 0)), w, _bs((None, K, tn), lambda j, i, k: (j, 0, k)))],
               jax.ShapeDtypeStruct(shape, out_dtype), spec, NN, extras=ex, epilogue=epilogue)


def _mm_rows_t(name, a, w, *, tm, out_dtype):
    M, N = a.shape
    J, f, _ = w.shape
    return _mm(name, (J, M // tm),
               [(a, _bs((tm, N), lambda j, i: (i, 0)), w, _bs((None, f, N), lambda j, i: (j, 0, 0)))],
               jax.ShapeDtypeStruct((J, M, f), out_dtype), _bs((None, tm, f), lambda j, i: (j, i, 0)), NT)


def _mm_wgrad(name, a, b, *, a_cols, b_cols, tm, tn, J, deps=()):
    def pick(arr, cols, t):
        if arr.ndim == 3:
            T, c = arr.shape[1], arr.shape[2]
            t = min(t, c)
            return T, c, t, (lambda sel: _bs((None, T, t), lambda j, i, k: (j, 0, sel(i, k))))
        T = arr.shape[0]
        c = arr.shape[1] if cols is None else cols
        t = min(t, c)
        per = c // t
        if cols is None:
            if per == 1:
                return T, c, t, (lambda sel: _resident((T, t), lambda j, i, k: (0, 0)))
            return T, c, t, (lambda sel: _bs((T, t), lambda j, i, k: (0, sel(i, k))))
        return T, c, t, (lambda sel: _bs((T, t), lambda j, i, k: (0, j * per + sel(i, k))))
    _, ca, tm, mk_a = pick(a, a_cols, tm)
    _, cb, tn, mk_b = pick(b, b_cols, tn)
    return _mm(name, (J, ca // tm, cb // tn),
               [(a, mk_a(lambda i, k: i), b, mk_b(lambda i, k: k))],
               jax.ShapeDtypeStruct((J, ca, cb), BF16), _bs((None, tm, tn), lambda j, i, k: (j, i, k)), TN, deps=deps)


def _tiled(arr, width=None, col=0, rowblk=0):
    return ("t", arr, arr.shape[1] if width is None else width, col, rowblk)


def _table(arr):
    return ("f", arr)


def _whole(arr):
    return ("w", arr)


def _ew(name, fn, ins, outs, *, n_rows, rows, reds=(), ncols=1, deps=()):
    nrb = n_rows // rows
    n_deps = len(deps)
    operands, in_specs = [], []
    for spec in ins:
        if spec[0] == "t":
            _, arr, width, col, rowblk = spec
            step = 1 if ncols > 1 else 0
            in_specs.append(_bs((rows, width), lambda c, i, col=col, rowblk=rowblk, step=step: (rowblk + i, col + c * step)))
        elif spec[0] == "f":
            arr = spec[1]
            in_specs.append(_bs((rows, arr.shape[1]), lambda c, i: (i, 0)))
        else:
            arr = spec[1]
            nd = arr.ndim
            if nd == 3:
                in_specs.append(_bs((None,) + arr.shape[1:], lambda c, i: (c, 0, 0)))
            else:
                in_specs.append(_bs(arr.shape, lambda c, i, nd=nd: (0,) * nd))
        operands.append(arr)
    out_shapes = [jax.ShapeDtypeStruct((n_rows, ncols * w), dt) for dt, w in outs]
    out_specs = [_bs((rows, w), lambda c, i: (i, c)) for _, w in outs]
    out_shapes += [jax.ShapeDtypeStruct((ncols, 1, w), F32) for w in reds]
    out_specs += [_bs((None, 1, w), lambda c, i: (c, 0, 0)) for w in reds]
    n_in, n_out, n_red = len(ins), len(outs), len(reds)
    operands += list(deps)
    in_specs += _any_specs(n_deps)

    def body(*refs):
        vals = fn(*[r[...] for r in refs[:n_in]])
        if not isinstance(vals, (tuple, list)):
            vals = (vals,)
        o_refs = refs[n_in + n_deps:]
        for o_ref, v in zip(o_refs[:n_out], vals[:n_out]):
            o_ref[...] = v.astype(o_ref.dtype)
        if n_red:
            i = pl.program_id(1)
            for r_ref, v in zip(o_refs[n_out:], vals[n_out:]):
                @pl.when(i == 0)
                def _(r_ref=r_ref):
                    r_ref[...] = jnp.zeros_like(r_ref)
                r_ref[...] += v

    res = pl.pallas_call(
        body, out_shape=out_shapes, grid=(ncols, nrb), in_specs=in_specs, out_specs=out_specs,
        compiler_params=_params("parallel", "arbitrary" if n_red else "parallel"), name=name)(*operands)
    return res


def _colsum(v):
    return jnp.sum(v, axis=0, keepdims=True)


def _rstd(x):
    return lax.rsqrt(jnp.mean(x * x, axis=-1, keepdims=True) + EPS)


def _sigmoid(x):
    return 1.0 / (1.0 + jnp.exp(-x))


def _norm_fwd(x, g):
    return x * _rstd(x) * g


def _norm_bwd(x, g, dy):
    r = _rstd(x)
    xh = x * r
    dxh = dy * g
    dx = r * (dxh - xh * jnp.mean(dxh * xh, axis=-1, keepdims=True))
    return dx, dy * xh


def _row_spec(arr, rows):
    if arr.shape[0] == 1:
        return _bs(arr.shape, lambda i: (0, 0))
    return _bs((rows, arr.shape[1]), lambda i: (i, 0))


def _ffn_fwd(tag, x, gain, get_w, deps=(), *, h=None, tail_ins=(), tail_fn=None, tail_outs=(F32,), tail_reds=()):
    T, D = x.shape
    if h is None:
        (h,) = _ew(f"{tag}_norm", lambda xv, g: _norm_fwd(xv, g), [_tiled(x), _whole(gain)], [(BF16, D)], n_rows=T, rows=512,
                   deps=deps)
    w1, w3 = get_w(f"{tag}_w1", h), get_w(f"{tag}_w3", h)
    J, f, _ = w1.shape
    tm = 1024

    def up(h_ref, w1_ref, w3_ref, u_ref, g_ref, a_ref):
        hv = h_ref[...]
        u = lax.dot_general(hv, w1_ref[...], (NT, ((), ())), preferred_element_type=F32)
        g = lax.dot_general(hv, w3_ref[...], (NT, ((), ())), preferred_element_type=F32)
        u_ref[...] = u.astype(BF16)
        g_ref[...] = g.astype(BF16)
        a_ref[...] = (u * _sigmoid(u) * g).astype(BF16)

    slab = _bs((None, tm, f), lambda j, i: (j, i, 0))
    w_spec = _bs((None, f, D), lambda j, i: (j, 0, 0))
    u, g, a = pl.pallas_call(
        up, out_shape=[jax.ShapeDtypeStruct((J, T, f), BF16)] * 3, grid=(J, T // tm),
        in_specs=[_bs((tm, D), lambda j, i: (i, 0)), w_spec, w_spec], out_specs=[slab] * 3,
        compiler_params=_params("parallel", "parallel"), name=f"{tag}_up")(h, w1, w3)
    w2 = get_w(f"{tag}_w2", a)
    def tail(acc, xv, *rest):
        y = xv + 0.5 * acc
        return y if tail_fn is None else tail_fn(y, *rest)

    row = _bs((512, D), lambda i: (i, 0))
    res = _mm(f"{tag}_down", (T // 512,),
              [(a, _bs((None, 512, f), lambda i, j=j: (j, i, 0)), w2, _resident((None, f, D), lambda i, j=j: (j, 0, 0)))
               for j in range(J)],
              [jax.ShapeDtypeStruct((T, D), dt) for dt in tail_outs], [row] * len(tail_outs), NN,
              extras=[(x, row)] + [(t, _row_spec(t, 512)) for t in tail_ins], epilogue=tail, reds=tail_reds)
    return res, (h, u, g, a)


def _dh_norm_bwd(name, rows, pairs, dims, x, gain, dres, deps, also_bf16=False):
    T, D = x.shape

    def epilogue(dh, xv, gv, dr):
        dx, dgr = _norm_bwd(xv, gv, dh)
        dx = dx + dr
        return (dx, 0.5 * dx) + ((dx,) if also_bf16 else ()) + (_colsum(dgr),)

    dts = [F32, BF16] + ([BF16] if also_bf16 else [])
    row = _bs((rows, D), lambda i: (i, 0))
    return _mm(name, (T // rows,), pairs, [jax.ShapeDtypeStruct((T, D), dt) for dt in dts], [row] * len(dts), dims,
               extras=[(x, row), (gain, _row_spec(gain, rows)), (dres, row)], epilogue=epilogue, deps=deps, reds=(D,))


def _ffn_bwd(tag, x, gain, get_w, put_g, saved, dy, dy_half, also_bf16=False):
    h, u, g, a = saved
    T, D = x.shape
    w1, w3, w2 = [get_w(f"{tag}_{n}", dy_half) for n in ("w1", "w3", "w2")]
    J, f, _ = w1.shape
    dw2 = _mm_wgrad(f"{tag}_bwd_dw2", a, dy_half, a_cols=None, b_cols=None, tm=f, tn=D, J=J)
    deps = put_g({f"{tag}_w2": dw2})
    tm = 1024

    def up_bwd(dy_ref, w2_ref, u_ref, g_ref, *rest):
        du_ref, dg_ref = rest[-2:]
        da = lax.dot_general(dy_ref[...], w2_ref[...], (NT, ((), ())), preferred_element_type=F32)
        uv, gv = u_ref[...].astype(F32), g_ref[...].astype(F32)
        s = _sigmoid(uv)
        du_ref[...] = (da * gv * (s * (1.0 + uv * (1.0 - s)))).astype(BF16)
        dg_ref[...] = (da * (uv * s)).astype(BF16)

    slab = _bs((None, tm, f), lambda j, i: (j, i, 0))
    du, dg = pl.pallas_call(
        up_bwd, out_shape=[jax.ShapeDtypeStruct((J, T, f), BF16)] * 2, grid=(J, T // tm),
        in_specs=[_bs((tm, D), lambda j, i: (i, 0)), _bs((None, f, D), lambda j, i: (j, 0, 0)), slab, slab] + _any_specs(len(deps)),
        out_specs=[slab] * 2, compiler_params=_params("parallel", "parallel"), name=f"{tag}_bwd_up")(dy_half, w2, u, g, *deps)
    dw1 = _mm_wgrad(f"{tag}_bwd_dw1", du, h, a_cols=None, b_cols=None, tm=f, tn=D, J=J)
    deps = put_g({f"{tag}_w1": dw1})
    dw3 = _mm_wgrad(f"{tag}_bwd_dw3", dg, h, a_cols=None, b_cols=None, tm=f, tn=D, J=J, deps=deps)
    deps = put_g({f"{tag}_w3": dw3})
    pairs = []
    for j in range(J):
        a_spec = _bs((None, 512, f), lambda i, j=j: (j, i, 0))
        w_spec = _resident((None, f, D), lambda i, j=j: (j, 0, 0))
        pairs += [(du, a_spec, w1, w_spec), (dg, a_spec, w3, w_spec)]
    return _dh_norm_bwd(f"{tag}_bwd_dh", 512, pairs, NN, x, gain, dy, deps, also_bf16)


def _t5_bucket(rel):
    n = N_BUCKETS // 2
    max_exact = n // 2
    ret = jnp.where(rel > 0, n, 0)
    a = jnp.abs(rel)
    af = jnp.maximum(a, 1).astype(F32)
    large = max_exact + (jnp.log(af / max_exact) / math.log(MAX_DISTANCE / max_exact) * (n - max_exact)).astype(jnp.int32)
    large = jnp.minimum(large, n - 1)
    return ret + jnp.where(a < max_exact, a, large)


WIN_A = QB_A + 2 * BAND_HALF
WIN_SHIFTS = (0, BAND_HALF, 2 * BAND_HALF)


def _window_variant(n, nblk):
    return jnp.where(n == 0, 0, jnp.where(n == nblk - 1, 2, 1))


def _window_start(n, nblk):
    return pl.multiple_of(jnp.clip(n * QB_A - BAND_HALF, 0, nblk * QB_A - WIN_A), BAND_HALF)


def _band_steps(xp=jnp):
    qi = xp.arange(QB_A, dtype=xp.int32)[None, :, None]
    kj = xp.arange(WIN_A, dtype=xp.int32)[None, None, :]
    return kj - qi - xp.asarray(WIN_SHIFTS, dtype=xp.int32)[:, None, None]


def _bias_tiles(rel_bias):
    wide = QB_A + 2 * WIN_SHIFTS[-1]
    qi = jnp.arange(QB_A, dtype=jnp.int32)[:, None]
    steps = jnp.arange(wide, dtype=jnp.int32)[None, :] - WIN_SHIFTS[-1] - qi
    buckets = jnp.stack([_t5_bucket(steps * d) for d in DILATIONS])
    inband = (jnp.abs(steps) <= BAND_HALF).astype(jnp.int32)
    n_heads = rel_bias.shape[1]

    def body(tab_ref, b_ref, m_ref, o_ref):
        hd = pl.program_id(0)
        bkt = b_ref[...]
        acc = jnp.zeros(bkt.shape, F32)
        for b in range(N_BUCKETS):
            acc = jnp.where(bkt == b, tab_ref[b, hd], acc)
        o_ref[...] = jnp.where(m_ref[...] > 0, acc, NEG_INF)

    base = pl.pallas_call(
        body, out_shape=jax.ShapeDtypeStruct((n_heads, QB_A, wide), F32), grid=(n_heads,),
        in_specs=[pl.BlockSpec(memory_space=pltpu.SMEM),
                  _bs((None, QB_A, wide), lambda hd: (hd // HEADS_A, 0, 0)),
                  _bs((QB_A, wide), lambda hd: (0, 0))],
        out_specs=_bs((None, QB_A, wide), lambda hd: (hd, 0, 0)),
        compiler_params=_params("parallel"), name="a_bias_tiles")(rel_bias, buckets, inband)
    base = base.reshape(len(DILATIONS), HEADS_A, QB_A, wide)
    return jnp.stack([base[..., WIN_SHIFTS[-1] - s:WIN_SHIFTS[-1] - s + WIN_A] for s in WIN_SHIFTS], axis=1)


def _bias_grad(dbias):
    steps = _band_steps(np)
    inband = np.abs(steps) <= BAND_HALF
    present = []
    for d in DILATIONS:
        rel = steps * d
        a = np.abs(rel)
        large = 8 + (np.log(np.maximum(a, 1) / 8.0) / math.log(MAX_DISTANCE / 8.0) * 8).astype(np.int64)
        bk = np.where(rel > 0, 16, 0) + np.where(a < 8, a, np.minimum(large, 15))
        present.append([sorted(set(bk[v][inband[v]].tolist())) for v in range(3)])
    buckets = jnp.stack([_t5_bucket(_band_steps() * d) for d in DILATIONS])
    n_heads = len(DILATIONS) * HEADS_A

    def body(b_ref, d_ref, o_ref):
        row = lax.broadcasted_iota(jnp.int32, (N_BUCKETS, n_heads), 0)
        col = lax.broadcasted_iota(jnp.int32, (N_BUCKETS, n_heads), 1)
        out = jnp.zeros((N_BUCKETS, n_heads), F32)
        for grp in range(len(DILATIONS)):
            for hh in range(HEADS_A):
                hd = grp * HEADS_A + hh
                for b in sorted(set(sum(present[grp], []))):
                    tot = jnp.zeros((), F32)
                    for v in range(3):
                        if b in present[grp][v]:
                            tot = tot + jnp.sum(jnp.where(b_ref[grp, v] == b, d_ref[grp, v, hh], 0.0))
                    out = jnp.where((row == b) & (col == hd), tot, out)
        o_ref[...] = out

    return pl.pallas_call(
        body, out_shape=jax.ShapeDtypeStruct((N_BUCKETS, n_heads), F32),
        compiler_params=pltpu.CompilerParams(vmem_limit_bytes=VMEM_LIMIT_BYTES), name="a_bias_grad")(buckets, dbias)


def _lane_is_second_head(shape):
    return lax.broadcasted_iota(jnp.int32, shape, len(shape) - 1) >= HEAD_A


VIEW_ROWS = 512


def _view_chunks():
    return [pltpu.VMEM((VIEW_ROWS, LANES), F32)] * (WIDTH_A // LANES)


def _rows_to_view(x_ref, col, o_ref, ocol, d, chunks):
    n = VIEW_ROWS // d
    for c, scr in enumerate(chunks):
        scr[...] = x_ref[:, col + c * LANES:col + (c + 1) * LANES].astype(F32)
        for r in range(d):
            at = ocol + r * WIDTH_A + c * LANES
            o_ref[:, at:at + LANES] = scr[pl.ds(r, n, stride=d), :].astype(o_ref.dtype)


def _view_to_rows(v_ref, o_ref, col, d, chunks):
    n = VIEW_ROWS // d
    for c, scr in enumerate(chunks):
        if d == 1:
            o_ref[:, col + c * LANES:col + (c + 1) * LANES] = v_ref[:, c * LANES:(c + 1) * LANES].astype(o_ref.dtype)
            continue
        for r in range(d):
            scr[pl.ds(r, n, stride=d), :] = v_ref[:, r * WIDTH_A + c * LANES:r * WIDTH_A + (c + 1) * LANES].astype(F32)
        o_ref[:, col + c * LANES:col + (c + 1) * LANES] = scr[...].astype(o_ref.dtype)


def _group_view(proj, grp, d):
    T = proj.shape[0]
    if d == 1:
        return proj, (lambda part, r: grp * 3 + part)

    def body(x_ref, o_ref, *chunks):
        for part in range(3):
            _rows_to_view(x_ref, part * WIDTH_A, o_ref, part * d * WIDTH_A, d, chunks)

    view = pl.pallas_call(
        body, out_shape=jax.ShapeDtypeStruct((T // d, 3 * d * WIDTH_A), proj.dtype), grid=(T // VIEW_ROWS,),
        in_specs=[_bs((VIEW_ROWS, 3 * WIDTH_A), lambda i: (i, grp))],
        out_specs=_bs((VIEW_ROWS // d, 3 * d * WIDTH_A), lambda i: (i, 0)),
        scratch_shapes=_view_chunks(), compiler_params=_params("parallel"), name=f"a_view_d{d}")(proj)
    return view, (lambda part, r: part * d + r)


def _stack_heads(v2, second):
    zero = jnp.zeros_like(v2)
    return jnp.concatenate([jnp.where(second, zero, v2), jnp.where(second, v2, zero)], axis=0)


def _unstack_heads(v, second):
    return jnp.where(second, v[QB_A:], v[:QB_A])


def _dil_fwd(view, bias, d):
    pv, colblk = view
    L = pv.shape[0]
    nblk = L // QB_A
    W2 = 2 * HEAD_A
    scale = HEAD_A ** -0.5

    def body(q_ref, k_ref, v_ref, b_ref, o_ref, l_ref):
        win = pl.ds(_window_start(pl.program_id(1), nblk), WIN_A)
        second = _lane_is_second_head((QB_A, W2))
        pairs = range(HEADS_A // 2)
        cols = [slice(hp * W2, (hp + 1) * W2) for hp in pairs]
        s = [lax.dot_general(_stack_heads(q_ref[:, cols[hp]], second), k_ref[win, cols[hp]], (NT, ((), ())),
                             preferred_element_type=F32) * scale + b_ref[2 * hp:2 * hp + 2].reshape(2 * QB_A, WIN_A)
             for hp in pairs]
        m = [jnp.max(x, axis=-1, keepdims=True) for x in s]
        p = [jnp.exp(x - mx) for x, mx in zip(s, m)]
        l = [jnp.sum(x, axis=-1, keepdims=True) for x in p]
        res = [jnp.dot(p[hp].astype(BF16), v_ref[win, cols[hp]], preferred_element_type=F32) / l[hp] for hp in pairs]
        o_ref[...] = jnp.concatenate([_unstack_heads(x, second) for x in res], axis=1).astype(o_ref.dtype)
        l_ref[...] = jnp.concatenate([_unstack_heads(jnp.broadcast_to(mx + jnp.log(lx), (2 * QB_A, W2)), second)
                                      for mx, lx in zip(m, l)], axis=1)

    in_specs = [_bs((QB_A, WIDTH_A), lambda r, n: (n, colblk(0, r))),
                _bs((L, WIDTH_A), lambda r, n: (0, colblk(1, r))), _bs((L, WIDTH_A), lambda r, n: (0, colblk(2, r))),
                _bs((None, HEADS_A, QB_A, WIN_A), lambda r, n: (_window_variant(n, nblk), 0, 0, 0))]
    o, lse = pl.pallas_call(
        body, out_shape=[jax.ShapeDtypeStruct((L, d * WIDTH_A), BF16), jax.ShapeDtypeStruct((L, d * WIDTH_A), F32)],
        grid=(d, nblk), in_specs=in_specs,
        out_specs=[_bs((QB_A, WIDTH_A), lambda r, n: (n, r)), _bs((QB_A, WIDTH_A), lambda r, n: (n, r))],
        compiler_params=_params("parallel", "parallel"), name=f"a_fwd_d{d}")(pv, pv, pv, bias)
    return o, lse


def _dil_bwd(view_qkv, bias, do, lse, cterm, d):
    pv, colblk = view_qkv
    L = pv.shape[0]
    nblk = L // QB_A
    W2 = 2 * HEAD_A
    PPS = 4
    WS = PPS * W2
    ob = WIDTH_A // WS
    scale = HEAD_A ** -0.5

    def body(q_ref, k_ref, v_ref, do_ref, l_ref, c_ref, b_ref, dq_ref, dk_ref, dv_ref, db_ref):
        r, n = pl.program_id(1), pl.program_id(2)

        @pl.when(n == 0)
        def _():
            dk_ref[...] = jnp.zeros_like(dk_ref)
            dv_ref[...] = jnp.zeros_like(dv_ref)

        @pl.when((n == 0) & (r == 0))
        def _():
            db_ref[...] = jnp.zeros_like(db_ref)

        second = _lane_is_second_head((QB_A, W2))
        win = pl.ds(_window_start(n, nblk), WIN_A)
        variant = _window_variant(n, nblk)
        pairs = range(PPS)
        cols = [slice(pp * W2, (pp + 1) * W2) for pp in pairs]

        def head_rows(ref, pp):
            v2 = ref[:, cols[pp]]
            return jnp.concatenate([v2[:, 0:1], v2[:, HEAD_A:HEAD_A + 1]], axis=0)

        kw = [k_ref[win, c] for c in cols]
        vw = [v_ref[win, c] for c in cols]
        qs = [_stack_heads(q_ref[:, c], second) for c in cols]
        dos = [_stack_heads(do_ref[:, c], second) for c in cols]
        s = [lax.dot_general(qs[pp], kw[pp], (NT, ((), ())), preferred_element_type=F32) for pp in pairs]
        dp = [lax.dot_general(dos[pp], vw[pp], (NT, ((), ())), preferred_element_type=F32) for pp in pairs]
        p = [jnp.exp(s[pp] * scale + b_ref[2 * pp:2 * pp + 2].reshape(2 * QB_A, WIN_A) - head_rows(l_ref, pp)) for pp in pairs]
        ds = [p[pp] * (dp[pp] + head_rows(c_ref, pp)) for pp in pairs]
        db_ref[variant] += jnp.concatenate([x.reshape(2, QB_A, WIN_A) for x in ds], axis=0)
        pb = [x.astype(BF16) for x in p]
        dsb = [(x * scale).astype(BF16) for x in ds]
        dq_ref[...] = jnp.concatenate([_unstack_heads(jnp.dot(dsb[pp], kw[pp], preferred_element_type=F32), second)
                                       for pp in pairs], axis=1).astype(dq_ref.dtype)
        dk_ref[win, :] += jnp.concatenate([lax.dot_general(dsb[pp], qs[pp], (TN, ((), ())), preferred_element_type=F32)
                                           for pp in pairs], axis=1)
        dv_ref[win, :] += jnp.concatenate([lax.dot_general(pb[pp], dos[pp], (TN, ((), ())), preferred_element_type=F32)
                                           for pp in pairs], axis=1)

    kv_spec = _resident if d == 1 else _bs
    in_specs = [_bs((QB_A, WS), lambda hp, r, n: (n, colblk(0, r) * ob + hp)),
                kv_spec((L, WS), lambda hp, r, n: (0, colblk(1, r) * ob + hp)),
                kv_spec((L, WS), lambda hp, r, n: (0, colblk(2, r) * ob + hp))]
    in_specs += [_bs((QB_A, WS), lambda hp, r, n: (n, r * ob + hp))] * 3
    in_specs += [_bs((None, 2 * PPS, QB_A, WIN_A), lambda hp, r, n: (_window_variant(n, nblk), hp, 0, 0))]
    out_shape = [jax.ShapeDtypeStruct((L, d * WIDTH_A), BF16), jax.ShapeDtypeStruct((L, d * WIDTH_A), F32),
                 jax.ShapeDtypeStruct((L, d * WIDTH_A), F32), jax.ShapeDtypeStruct((3, HEADS_A, QB_A, WIN_A), F32)]
    out_specs = [_bs((QB_A, WS), lambda hp, r, n: (n, r * ob + hp)),
                 _bs((L, WS), lambda hp, r, n: (0, r * ob + hp)), _bs((L, WS), lambda hp, r, n: (0, r * ob + hp)),
                 _bs((3, 2 * PPS, QB_A, WIN_A), lambda hp, r, n: (0, hp, 0, 0))]
    dq, dk, dv, db = pl.pallas_call(
        body, out_shape=out_shape, grid=(ob, d, nblk), in_specs=in_specs, out_specs=out_specs,
        compiler_params=_params("arbitrary", "arbitrary", "arbitrary"), name=f"a_bwd_d{d}")(
            pv, pv, pv, do, lse, cterm, bias)
    return dq, dk, dv, db


def _assemble_dproj(a_parts, dq_b, dk_b, dv_b, dga, dgb):
    T = dq_b.shape[0]
    flat = [(a_parts[part][g], d) for part in range(3) for g, d in enumerate(DILATIONS)]
    rest = [dq_b, dk_b, dv_b, dga, dgb]

    def body(*refs):
        views, others = refs[:len(flat)], refs[len(flat):len(flat) + len(rest)]
        o_ref, chunks = refs[len(flat) + len(rest)], refs[len(flat) + len(rest) + 1:]
        col = 0
        for v_ref, (_, d) in zip(views, flat):
            _view_to_rows(v_ref, o_ref, col, d, chunks)
            col += WIDTH_A
        for x_ref in others:
            w = x_ref.shape[1]
            o_ref[:, col:col + w] = x_ref[...].astype(o_ref.dtype)
            col += w

    in_specs = [_bs((VIEW_ROWS // d, d * WIDTH_A), lambda i: (i, 0)) for _, d in flat]
    in_specs += [_bs((VIEW_ROWS, x.shape[1]), lambda i: (i, 0)) for x in rest]
    return pl.pallas_call(
        body, out_shape=jax.ShapeDtypeStruct((T, IN_WIDTH), BF16), grid=(T // VIEW_ROWS,), in_specs=in_specs,
        out_specs=_bs((VIEW_ROWS, IN_WIDTH), lambda i: (i, 0)), scratch_shapes=_view_chunks(),
        compiler_params=_params("parallel"), name="mix_bwd_dproj")(*[a for a, _ in flat], *rest)


def _segment_ones():
    i = np.arange(WIDTH_A)
    return jnp.asarray((i[:, None] // HEAD_A == i[None, :] // HEAD_A).astype(np.float32), dtype=BF16)


def _group_weights(l0, l1, l2):
    m = jnp.maximum(jnp.maximum(l0, l1), l2)
    e = [jnp.exp(l - m) for l in (l0, l1, l2)]
    z = e[0] + e[1] + e[2]
    return [ei / z for ei in e]


def _view_specs():
    return [_bs((VIEW_ROWS // d, d * WIDTH_A), lambda i: (i, 0)) for d in DILATIONS]


def _stage_tiles(n):
    return [pltpu.VMEM((VIEW_ROWS, WIDTH_A), F32)] * n


def _combine_fwd(outs, lses):
    T = outs[0].shape[0] * DILATIONS[0]
    n = len(DILATIONS)

    def body(*refs):
        o_refs, l_refs, oa_ref = refs[:n], refs[n:2 * n], refs[2 * n]
        o_st, l_st, chunks = refs[2 * n + 1:3 * n + 1], refs[3 * n + 1:4 * n + 1], refs[4 * n + 1:]
        for g, d in enumerate(DILATIONS):
            _view_to_rows(o_refs[g], o_st[g], 0, d, chunks)
            _view_to_rows(l_refs[g], l_st[g], 0, d, chunks)
        w = _group_weights(*[l[...] for l in l_st])
        oa_ref[...] = (w[0] * o_st[0][...] + w[1] * o_st[1][...] + w[2] * o_st[2][...]).astype(oa_ref.dtype)

    return pl.pallas_call(
        body, out_shape=jax.ShapeDtypeStruct((T, WIDTH_A), BF16), grid=(T // VIEW_ROWS,),
        in_specs=_view_specs() * 2, out_specs=_bs((VIEW_ROWS, WIDTH_A), lambda i: (i, 0)),
        scratch_shapes=_stage_tiles(2 * n) + _view_chunks(), compiler_params=_params("parallel"), name="a_combine")(*outs, *lses)


def _combine_bwd(doa, outs, lses):
    T = doa.shape[0]
    n = len(DILATIONS)

    def body(*refs):
        d_ref, o_refs, l_refs, seg_ref = refs[0], refs[1:n + 1], refs[n + 1:2 * n + 1], refs[2 * n + 1]
        do_refs, c_refs = refs[2 * n + 2:3 * n + 2], refs[3 * n + 2:4 * n + 2]
        o_st, l_st = refs[4 * n + 2:5 * n + 2], refs[5 * n + 2:6 * n + 2]
        tmp, chunks = refs[6 * n + 2], refs[6 * n + 3:]
        for g, d in enumerate(DILATIONS):
            _view_to_rows(o_refs[g], o_st[g], 0, d, chunks)
            _view_to_rows(l_refs[g], l_st[g], 0, d, chunks)
        dv = d_ref[...].astype(F32)
        w = _group_weights(*[l[...] for l in l_st])
        seg = seg_ref[...]
        tot = jnp.zeros(dv.shape, F32)
        for g in range(n):
            prod = w[g] * dv * o_st[g][...]
            hi = prod.astype(BF16)
            lo = (prod - hi.astype(F32)).astype(BF16)
            tot = tot + jnp.dot(hi, seg, preferred_element_type=F32) + jnp.dot(lo, seg, preferred_element_type=F32)
        for g, d in enumerate(DILATIONS):
            tmp[...] = w[g] * dv
            _rows_to_view(tmp, 0, do_refs[g], 0, d, chunks)
            tmp[...] = -w[g] * tot
            _rows_to_view(tmp, 0, c_refs[g], 0, d, chunks)

    views = [jax.ShapeDtypeStruct((T // d, d * WIDTH_A), dt) for dt in (BF16, F32) for d in DILATIONS]
    res = pl.pallas_call(
        body, out_shape=views, grid=(T // VIEW_ROWS,),
        in_specs=[_bs((VIEW_ROWS, WIDTH_A), lambda i: (i, 0))] + _view_specs() * 2 + [_bs((WIDTH_A, WIDTH_A), lambda i: (0, 0))],
        out_specs=_view_specs() * 2, scratch_shapes=_stage_tiles(2 * n + 1) + _view_chunks(),
        compiler_params=_params("parallel"), name="a_combine_bwd")(doa, *outs, *lses, _segment_ones())
    return res[:n], res[n:]


def _rope_tables(T):
    rows = T // GRID_W
    row = jnp.repeat(jnp.arange(rows, dtype=F32), GRID_W)
    col = jnp.tile(jnp.arange(GRID_W, dtype=F32), rows)
    n_freq = HEAD_B // 4
    freq = ROPE_THETA ** (-jnp.arange(n_freq, dtype=F32) / n_freq)
    ang = jnp.concatenate([row[:, None] * freq, col[:, None] * freq], axis=-1)
    cos, sin = jnp.repeat(jnp.cos(ang), 2, axis=1), jnp.repeat(jnp.sin(ang), 2, axis=1)
    sign = jnp.where(jnp.arange(HEAD_B) % 2 == 0, -1.0, 1.0).astype(F32)
    return cos, sin * sign


def _swap_pairs(v):
    even = lax.broadcasted_iota(jnp.int32, v.shape, v.ndim - 1) % 2 == 0
    n = v.shape[-1]
    return jnp.where(even, pltpu.roll(v, n - 1, v.ndim - 1), pltpu.roll(v, 1, v.ndim - 1))


def _qk_fwd(name, proj, col0, n_heads, gain, cos, sin, out_scale=1.0, deps=()):
    T = proj.shape[0]

    def fn(xr, g, c, s):
        xn = _norm_fwd(xr.astype(F32), g)
        return (xn * c + _swap_pairs(xn) * s) * out_scale

    (out,) = _ew(name, fn, [_tiled(proj, HEAD_B, col0 // HEAD_B), _whole(gain), _table(cos), _table(sin)],
                 [(BF16, HEAD_B)], n_rows=T, rows=2048, ncols=n_heads, deps=deps)
    return out


def _qk_bwd(name, dout, proj, col0, n_heads, gain, cos, sin, in_scale=1.0):
    T = proj.shape[0]

    def fn(dv, xr, g, c, s):
        dv = dv.astype(F32) * in_scale
        dxn = c * dv + _swap_pairs(s * dv)
        dx, dgr = _norm_bwd(xr.astype(F32), g, dxn)
        return dx, _colsum(dgr)

    dx, dg = _ew(name, fn, [_tiled(dout, HEAD_B, 0), _tiled(proj, HEAD_B, col0 // HEAD_B), _whole(gain),
                            _table(cos), _table(sin)],
                 [(BF16, HEAD_B)], n_rows=T, rows=2048, reds=(HEAD_B,), ncols=n_heads)
    return dx, jnp.sum(dg, axis=0)


def _gqa_fwd(qn, kn, proj):
    T = qn.shape[0]
    GW = 4 * HEAD_B
    QB = QB_B

    def body(q_ref, k_ref, v_ref, o_ref, l_ref):
        k, v = k_ref[...], v_ref[...]
        lane = lax.broadcasted_iota(jnp.int32, (QB, HEAD_B), 1)
        heads = range(4)
        s = [lax.dot_general(q_ref[:, g * HEAD_B:(g + 1) * HEAD_B], k, (NT, ((), ())), preferred_element_type=F32)
             for g in heads]
        m = [jnp.max(x, axis=-1, keepdims=True) for x in s]
        p = [jnp.exp2(x - mx) for x, mx in zip(s, m)]
        l = [jnp.sum(x, axis=-1, keepdims=True) for x in p]
        o = [jnp.dot(p[g].astype(BF16), v, preferred_element_type=F32) / l[g] for g in heads]
        o_ref[...] = jnp.concatenate(o, axis=1).astype(o_ref.dtype)
        lse_all = jnp.zeros((QB, HEAD_B), F32)
        for g in heads:
            lse_all = jnp.where(lane == g, m[g] + jnp.log2(l[g]), lse_all)
        l_ref[...] = lse_all

    return pl.pallas_call(
        body, out_shape=[jax.ShapeDtypeStruct((T, 2 * GW), BF16), jax.ShapeDtypeStruct((2, T, HEAD_B), F32)],
        grid=(2, T // QB),
        in_specs=[_bs((QB, GW), lambda kv, i: (i, kv)), _bs((T, HEAD_B), lambda kv, i: (0, kv)),
                  _bs((T, HEAD_B), lambda kv, i: (0, B_V // HEAD_B + kv))],
        out_specs=[_bs((QB, GW), lambda kv, i: (i, kv)), _bs((None, QB, HEAD_B), lambda kv, i: (kv, i, 0))],
        compiler_params=_params("parallel", "parallel"), name="b_fwd")(qn, kn, proj)


def _gqa_bwd(qn, kn, proj, o, lse, do, deps=()):
    T = qn.shape[0]
    GW = 4 * HEAD_B

    def body(q_ref, k_ref, v_ref, o_ref, l_ref, do_ref, *rest):
        dq_ref, dk_ref, dv_ref = rest[-3:]
        i = pl.program_id(1)

        @pl.when(i == 0)
        def _():
            dk_ref[...] = jnp.zeros_like(dk_ref)
            dv_ref[...] = jnp.zeros_like(dv_ref)

        k, v = k_ref[...], v_ref[...]
        lse_all = l_ref[...]
        for g in range(4):
            cols = slice(g * HEAD_B, (g + 1) * HEAD_B)
            q, dob = q_ref[:, cols], do_ref[:, cols]
            delta = jnp.sum(dob.astype(F32) * o_ref[:, cols].astype(F32), axis=-1, keepdims=True)
            s = lax.dot_general(q, k, (NT, ((), ())), preferred_element_type=F32)
            p = jnp.exp2(s - lse_all[:, g:g + 1])
            dp = lax.dot_general(dob, v, (NT, ((), ())), preferred_element_type=F32)
            ds = (p * (dp - delta)).astype(BF16)
            dq_ref[:, cols] = jnp.dot(ds, k, preferred_element_type=F32).astype(dq_ref.dtype)
            dk_ref[...] += lax.dot_general(ds, q, (TN, ((), ())), preferred_element_type=F32)
            dv_ref[...] += lax.dot_general(p.astype(BF16), dob, (TN, ((), ())), preferred_element_type=F32)

    return pl.pallas_call(
        body, out_shape=[jax.ShapeDtypeStruct((T, 2 * GW), BF16), jax.ShapeDtypeStruct((T, 2 * HEAD_B), F32),
                         jax.ShapeDtypeStruct((T, 2 * HEAD_B), F32)],
        grid=(2, T // QB_B),
        in_specs=[_bs((QB_B, GW), lambda kv, i: (i, kv)), _bs((T, HEAD_B), lambda kv, i: (0, kv)),
                  _bs((T, HEAD_B), lambda kv, i: (0, B_V // HEAD_B + kv)), _bs((QB_B, GW), lambda kv, i: (i, kv)),
                  _bs((None, QB_B, HEAD_B), lambda kv, i: (kv, i, 0)), _bs((QB_B, GW), lambda kv, i: (i, kv))] + _any_specs(len(deps)),
        out_specs=[_bs((QB_B, GW), lambda kv, i: (i, kv)), _bs((T, HEAD_B), lambda kv, i: (0, kv)),
                   _bs((T, HEAD_B), lambda kv, i: (0, kv))],
        compiler_params=_params("parallel", "arbitrary"), name="b_bwd")(qn, kn, proj, o, lse, do, *deps)


def _local_step(x, target, small, get_w, put_g, deps=(), prefetch_w=lambda name, after: []):
    T, D = x.shape
    gs = {}

    bias = _bias_tiles(small["rel_bias"])
    cos, sin = _rope_tables(T)
    (x1, h2), ffn1_saved = _ffn_fwd("ffn1", x, small["ffn1_norm"], lambda name, after: get_w(name, [after, bias, cos, sin]), deps,
                                    tail_ins=[small["mix_norm"]], tail_fn=lambda y, g: (y, _norm_fwd(y, g)), tail_outs=(F32, BF16))
    w_in = get_w("w_in", h2)
    nq = w_in.shape[2]
    tpq = nq // WIDTH_A

    def proj_tile(j, k):
        c = j * tpq + k
        return jnp.where(c < 3 * len(DILATIONS), (c % 3) * 3 + c // 3, c)

    proj = _mm("mix_in", (4, tpq),
               [(h2, _resident((T, D), lambda j, k: (0, 0)), w_in, _bs((None, D, WIDTH_A), lambda j, k: (j, 0, k)))],
               jax.ShapeDtypeStruct((T, IN_WIDTH), BF16), _bs((T, WIDTH_A), lambda j, k: (0, proj_tile(j, k))), NN)

    a_views = [_group_view(proj, grp, d) for grp, d in enumerate(DILATIONS)]
    a_outs, a_lses = [], []
    for grp, d in enumerate(DILATIONS):
        o, l = _dil_fwd(a_views[grp], bias[grp], d)
        a_outs.append(o)
        a_lses.append(l)
    o_a = _combine_fwd(a_outs, a_lses)

    qn = _qk_fwd("b_qnorm", proj, B_Q, 8, small["q_norm"], cos, sin, out_scale=QK_SCALE_LOG2,
                 deps=prefetch_w("w_branch_a", proj))
    kn = _qk_fwd("b_knorm", proj, B_K, 2, small["k_norm"], cos, sin)
    o_b, lse_b = _gqa_fwd(qn, kn, proj)
    ahead = prefetch_w("ffn2_w1", o_b)

    wa, wb, wo = get_w("w_branch_a", o_b), get_w("w_branch_b", o_b), get_w("w_out", o_b)
    bg_a, bg_b = small["b_gate"][:, :D], small["b_gate"][:, D:]
    n_a = wa.shape[0]

    def merge_out(oa_ref, ob_ref, ga_ref, gb_ref, x1_ref, wa_ref, wb_ref, wo_ref, ba_ref, bb_ref, g2_ref, *rest):
        ta_ref, tb_ref, mg_ref, x2_ref, hn_ref = rest[-5:]
        oa = oa_ref[...]
        ta = jnp.concatenate([jnp.dot(oa, wa_ref[j], preferred_element_type=F32) for j in range(n_a)], axis=1)
        tb = jnp.dot(ob_ref[...], wb_ref[...], preferred_element_type=F32)
        sa = _sigmoid(ga_ref[...].astype(F32) + ba_ref[...])
        sb = _sigmoid(gb_ref[...].astype(F32) + bb_ref[...])
        merged = (sa * ta + sb * tb).astype(BF16)
        ta_ref[...], tb_ref[...], mg_ref[...] = ta.astype(BF16), tb.astype(BF16), merged
        y = x1_ref[...] + jnp.dot(merged, wo_ref[...], preferred_element_type=F32)
        x2_ref[...] = y
        hn_ref[...] = _norm_fwd(y, g2_ref[...]).astype(BF16)

    row = _bs((512, D), lambda i: (i, 0))
    gate_specs = [_bs((512, D), lambda i: (i, G_A // D)), _bs((512, D), lambda i: (i, G_B // D))]
    whole2, whole3 = (lambda i: (0, 0)), (lambda i: (0, 0, 0))
    vec = _bs((1, D), whole2)
    t_a, t_b, merged, x2, hn2 = pl.pallas_call(
        merge_out, out_shape=[jax.ShapeDtypeStruct((T, D), BF16)] * 3 + [jax.ShapeDtypeStruct((T, D), F32), jax.ShapeDtypeStruct((T, D), BF16)],
        grid=(T // 512,),
        in_specs=[_bs((512, WIDTH_A), lambda i: (i, 0)), row] + gate_specs + [row, _resident(wa.shape, whole3), _resident((D, D), whole2),
                                                                                _resident((D, D), whole2), vec, vec, vec]
        + _any_specs(len(ahead)),
        out_specs=[row] * 5, compiler_params=_params("parallel"), name="mix_merge_out")(
            o_a, o_b, proj, proj, x1, wa, wb, wo, bg_a, bg_b, small["ffn2_norm"], *ahead)

    def head(xv, g, tv):
        r = _rstd(xv)
        xh = xv * r
        e = xh * g - tv
        dy = e * (1.0 / D)
        dxh = dy * g
        dx = r * (dxh - xh * jnp.mean(dxh * xh, axis=-1, keepdims=True))
        return dx, 0.5 * dx, _colsum(e * e) * (0.5 / D), _colsum(dy * xh)

    (dx3, dx3_half, loss_cols, g_final), ffn2_saved = _ffn_fwd(
        "ffn2", x2, small["ffn2_norm"], get_w, h=hn2, tail_ins=[small["final_norm"].reshape(1, D), target], tail_fn=head,
        tail_outs=(F32, BF16), tail_reds=(D, D))
    gs["final_norm"] = g_final.reshape(D)

    dx2, _, dmix, gs["ffn2_norm"] = _ffn_bwd("ffn2", x2, small["ffn2_norm"], get_w, put_g, ffn2_saved, dx3, dx3_half,
                                             also_bf16=True)
    g_out = _mm_wgrad("mix_bwd_dwout", merged, dmix, a_cols=D // 4, b_cols=None, tm=256, tn=512, J=4).reshape(D, D)

    def merge_out_bwd(dx_ref, ta_ref, tb_ref, ga_ref, gb_ref, wa_ref, wb_ref, wo_ref, ba_ref, bb_ref,
                      dta_ref, dtb_ref, dga_ref, dgb_ref, doa_ref, dob_ref, dba_ref, dbb_ref):
        dm = lax.dot_general(dx_ref[...], wo_ref[...], (NT, ((), ())), preferred_element_type=F32)
        ta, tb = ta_ref[...].astype(F32), tb_ref[...].astype(F32)
        sa = _sigmoid(ga_ref[...].astype(F32) + ba_ref[...])
        sb = _sigmoid(gb_ref[...].astype(F32) + bb_ref[...])
        dga, dgb = dm * ta * sa * (1.0 - sa), dm * tb * sb * (1.0 - sb)
        dta, dtb = (dm * sa).astype(BF16), (dm * sb).astype(BF16)
        dta_ref[...], dtb_ref[...] = dta, dtb
        dga_ref[...], dgb_ref[...] = dga.astype(BF16), dgb.astype(BF16)
        w = wa_ref.shape[2]
        doa = sum(lax.dot_general(dta[:, j * w:(j + 1) * w], wa_ref[j], (NT, ((), ())), preferred_element_type=F32) for j in range(n_a))
        doa_ref[...] = doa.astype(BF16)
        dob_ref[...] = lax.dot_general(dtb, wb_ref[...], (NT, ((), ())), preferred_element_type=F32).astype(BF16)

        @pl.when(pl.program_id(0) == 0)
        def _():
            dba_ref[...] = jnp.zeros_like(dba_ref)
            dbb_ref[...] = jnp.zeros_like(dbb_ref)
        dba_ref[...] += _colsum(dga)
        dbb_ref[...] += _colsum(dgb)

    rowb = _bs((256, D), lambda i: (i, 0))
    gate_specs = [_bs((256, D), lambda i: (i, G_A // D)), _bs((256, D), lambda i: (i, G_B // D))]
    dta, dtb, dga, dgb, do_a, do_b, dba, dbb = pl.pallas_call(
        merge_out_bwd,
        out_shape=[jax.ShapeDtypeStruct((T, D), BF16)] * 4 + [jax.ShapeDtypeStruct((T, WIDTH_A), BF16), jax.ShapeDtypeStruct((T, D), BF16)]
        + [jax.ShapeDtypeStruct((1, D), F32)] * 2,
        grid=(T // 256,),
        in_specs=[rowb, rowb, rowb] + gate_specs + [_resident(wa.shape, whole3), _resident((D, D), whole2), _resident((D, D), whole2), vec, vec],
        out_specs=[rowb] * 4 + [_bs((256, WIDTH_A), lambda i: (i, 0)), rowb, vec, vec],
        compiler_params=_params("arbitrary"), name="mix_merge_out_bwd")(dmix, t_a, t_b, proj, proj, wa, wb, wo, bg_a, bg_b)
    gs["b_gate"] = jnp.concatenate([dba, dbb], axis=1)

    g_a = _mm_wgrad("mix_bwd_dwa", o_a, dta, a_cols=None, b_cols=D // 4, tm=WIDTH_A, tn=256, J=4)
    g_b = _mm_wgrad("mix_bwd_dwb", o_b, dtb, a_cols=D // 4, b_cols=None, tm=256, tn=512, J=4).reshape(D, D)
    deps = put_g({"w_out": g_out, "w_branch_a": g_a, "w_branch_b": g_b})

    dqn, dkn, dv_b = _gqa_bwd(qn, kn, proj, o_b, lse_b, do_b, deps)
    dq_b, gs["q_norm"] = _qk_bwd("b_bwd_qnorm", dqn, proj, B_Q, 8, small["q_norm"], cos, sin, in_scale=HEAD_B ** -0.5)
    dk_b, gs["k_norm"] = _qk_bwd("b_bwd_knorm", dkn, proj, B_K, 2, small["k_norm"], cos, sin, in_scale=1.0 / LOG2_E)

    do_groups, c_groups = _combine_bwd(do_a, a_outs, a_lses)
    dqs, dks, dvs, dbs = [], [], [], []
    for grp, d in enumerate(DILATIONS):
        dq, dk, dv, db = _dil_bwd(a_views[grp], bias[grp], do_groups[grp], a_lses[grp], c_groups[grp], d)
        dqs.append(dq), dks.append(dk), dvs.append(dv), dbs.append(db)
    gs["rel_bias"] = _bias_grad(jnp.stack(dbs))

    dproj = _assemble_dproj([dqs, dks, dvs], dq_b, dk_b, dv_b, dga, dgb)
    nq = w_in.shape[2]
    g_in = _mm("mix_bwd_dwin", (4, tpq),
               [(h2, _resident((T, D), lambda j, k: (0, 0)), dproj, _bs((T, WIDTH_A), lambda j, k: (0, j * tpq + k)))],
               jax.ShapeDtypeStruct((4, D, nq), BF16), _bs((None, D, WIDTH_A), lambda j, k: (j, 0, k)), TN)
    deps = put_g({"w_in": g_in})
    dx1, dx1_half, gs["mix_norm"] = _dh_norm_bwd(
        "mix_bwd_dh", 256,
        [(dproj, _bs((256, nq), lambda i, j=j: (i, j)), w_in, _resident((None, D, nq), lambda i, j=j: (j, 0, 0))) for j in range(4)],
        NT, x1, small["mix_norm"], dx2, deps)

    dx0, _, gs["ffn1_norm"] = _ffn_bwd("ffn1", x, small["ffn1_norm"], get_w, put_g, ffn1_saved, dx1, dx1_half)
    return loss_cols, dx0, gs


def _position():
    return lax.axis_index("x"), lax.axis_index("y"), lax.axis_index("c")


def _any_specs(n):
    return [pl.BlockSpec(memory_space=pl.ANY)] * n


HBM_SPEC = pl.BlockSpec(memory_space=pltpu.HBM)
SEM_SPEC = pl.BlockSpec(memory_space=pltpu.SEMAPHORE)
DATAFLOW_EFFECT = pltpu.SideEffectType.DATAFLOW_SIDE_EFFECTING
N_PEER_CHIPS = 3
LANES = 128


def _quarter_copies(srcs, lands, send_sems, recv_sems, mode):
    x, y, c = _position()
    me = 2 * x + y
    peers = [(1 - x, y, c), (x, 1 - y, c), (1 - x, 1 - y, c)]
    copies = []
    for src, land, send, recv in zip(srcs, lands, send_sems, recv_sems):
        if mode == "sibling":
            copies.append(pltpu.make_async_remote_copy(src_ref=src, dst_ref=land, send_sem=send.at[0], recv_sem=recv.at[0],
                                                       device_id=(x, y, 1 - c), device_id_type=MESH))
            continue
        if mode == "fill":
            half = land.shape[1] // 2
            for p, (px, py, _) in enumerate(peers):
                part = land.at[2 * px + py, pl.ds(c * half, half)]
                copies.append(pltpu.make_async_remote_copy(src_ref=part, dst_ref=part, send_sem=send.at[p], recv_sem=recv.at[p],
                                                           device_id=(x, y, 1 - c), device_id_type=MESH))
            continue
        scatter = mode == "scatter"
        half = land.shape[1] // 2
        mine = land.at[me, pl.ds(c * half, half)]
        for p, (px, py, pc) in enumerate(peers):
            copies.append(pltpu.make_async_remote_copy(
                src_ref=src.at[2 * px + py] if scatter else mine, dst_ref=land.at[me] if scatter else mine,
                send_sem=send.at[p], recv_sem=recv.at[p], device_id=(px, py, pc), device_id_type=MESH))
    return copies


def _fill_from_sibling(name, stacks):
    n = len(stacks)

    def body(*refs):
        outs = refs[n:2 * n]
        send_sems, recv_sems = refs[2 * n:]
        x, y, c = _position()
        copies = []
        for i, ref in enumerate(outs):
            half = ref.shape[1] // 2
            rows = pl.ds(c * half, half)
            for p, k in enumerate((2 * (1 - x) + y, 2 * x + (1 - y), 2 * (1 - x) + (1 - y))):
                cp = pltpu.make_async_remote_copy(ref.at[k, rows], ref.at[k, rows], send_sems.at[3 * i + p], recv_sems.at[3 * i + p],
                                                  device_id=(x, y, 1 - c), device_id_type=MESH)
                cp.start()
                copies.append(cp)
        for cp in copies:
            cp.wait()

    return pl.pallas_call(
        body, out_shape=[jax.ShapeDtypeStruct(s.shape, s.dtype) for s in stacks],
        in_specs=_any_specs(n), out_specs=_any_specs(n), input_output_aliases={i: i for i in range(n)},
        scratch_shapes=[pltpu.SemaphoreType.DMA((N_PEER_CHIPS * n,)), pltpu.SemaphoreType.DMA((N_PEER_CHIPS * n,))],
        compiler_params=pltpu.CompilerParams(has_side_effects=True), name=name)(*stacks)


def _exchange_start(name, srcs, lands, mode):
    n = len(lands)
    arrays = list(lands) if srcs is None else list(srcs) + list(lands)
    k = len(arrays)

    def body(*refs):
        land_refs = refs[k - n:k]
        send_sems, recv_sems = refs[k:k + n], refs[k + n:k + 2 * n]
        token = refs[2 * k + 2 * n]
        for cp in _quarter_copies(refs[:n], land_refs, send_sems, recv_sems, mode):
            cp.start()
        token[...] = jnp.zeros_like(token)

    sem = pltpu.SemaphoreType.DMA((N_PEER_CHIPS,))
    out_shape = [sem] * (2 * n) + [pltpu.HBM(a.shape, a.dtype) for a in arrays] + [jax.ShapeDtypeStruct((8, LANES), F32)]
    res = pl.pallas_call(
        body, name=name, out_shape=out_shape, in_specs=[HBM_SPEC] * k,
        out_specs=[SEM_SPEC] * (2 * n) + [HBM_SPEC] * k + [pl.BlockSpec(memory_space=pltpu.VMEM)],
        input_output_aliases={i: 2 * n + i for i in range(k)},
        compiler_params=pltpu.CompilerParams(has_side_effects=DATAFLOW_EFFECT),
    )(*[pltpu.with_memory_space_constraint(a, pltpu.HBM) for a in arrays])
    thru = res[2 * n:2 * n + k]
    return res[:n], res[n:2 * n], (None if srcs is None else thru[:n]), thru[k - n:], res[2 * n + k]


def _exchange_wait(name, srcs, lands, send_sems, recv_sems, after, mode):
    n = len(lands)
    arrays = list(lands) if srcs is None else list(srcs) + list(lands)
    k = len(arrays)
    after = list(after) if isinstance(after, (list, tuple)) else [after]

    def body(*refs):
        sends, recvs = refs[k:k + n], refs[k + n:k + 2 * n]
        for cp in _quarter_copies(refs[:n], refs[k - n:k], sends, recvs, mode):
            cp.wait_send()
            cp.wait_recv()

    res = pl.pallas_call(
        body, name=name, out_shape=[pltpu.HBM(a.shape, a.dtype) for a in arrays],
        in_specs=[HBM_SPEC] * k + [SEM_SPEC] * (2 * n) + _any_specs(len(after)),
        out_specs=[HBM_SPEC] * k, input_output_aliases={i: i for i in range(k)},
        compiler_params=pltpu.CompilerParams(has_side_effects=DATAFLOW_EFFECT),
    )(*arrays, *send_sems, *recv_sems, *after)
    return (None if srcs is None else res[:n]), res[k - n:]


def _own_slots(name, srcs, from_stack=False):
    n = len(srcs)
    me = (2 * lax.axis_index("x") + lax.axis_index("y")).astype(jnp.int32).reshape(1)

    def body(me_ref, *refs):
        for x_ref, o_ref in zip(refs[:n], refs[n:]):
            o_ref[...] = x_ref[...].astype(o_ref.dtype)

    in_specs, out_specs, out_shape = [], [], []
    for src in srcs:
        R, C = src.shape[-2:]
        in_specs.append(pl.BlockSpec((None, R // 2, C), lambda i, me_ref: (me_ref[0], i, 0)) if from_stack
                        else pl.BlockSpec((R // 2, C), lambda i, me_ref: (i, 0)))
        out_specs.append(pl.BlockSpec((None, R // 2, C), lambda i, me_ref: (me_ref[0], i, 0)))
        out_shape.append(jax.ShapeDtypeStruct((4, R, C), BF16))
    grid_spec = pltpu.PrefetchScalarGridSpec(num_scalar_prefetch=1, grid=(2,), in_specs=in_specs, out_specs=out_specs)
    return pl.pallas_call(body, out_shape=out_shape, grid_spec=grid_spec, compiler_params=_params("parallel"), name=name)(me, *srcs)


def _allreduce_small(buf):
    R, C = buf.shape
    flips = [(fx, fy, fc) for fx in (0, 1) for fy in (0, 1) for fc in (0, 1)][1:]

    def body(in_ref, out_ref, land_ref, send_sems, recv_sems):
        x, y, c = _position()
        me = 4 * x + 2 * y + c
        copies = []
        for k, (fx, fy, fc) in enumerate(flips):
            px, py, pc = (1 - x if fx else x), (1 - y if fy else y), (1 - c if fc else c)
            cp = pltpu.make_async_remote_copy(in_ref, land_ref.at[me], send_sems.at[k], recv_sems.at[k],
                                              device_id=(px, py, pc), device_id_type=MESH)
            cp.start()
            copies.append(cp)
        land_ref[me] = in_ref[...]
        for cp in copies:
            cp.wait()
        acc = land_ref[0]
        for k in range(1, 8):
            acc = acc + land_ref[k]
        out_ref[...] = acc

    return pl.pallas_call(
        body, out_shape=jax.ShapeDtypeStruct((R, C), F32),
        in_specs=[pl.BlockSpec(memory_space=pltpu.VMEM)], out_specs=pl.BlockSpec(memory_space=pltpu.VMEM),
        scratch_shapes=[pltpu.VMEM((8, R, C), F32), pltpu.SemaphoreType.DMA((7,)), pltpu.SemaphoreType.DMA((7,))],
        compiler_params=pltpu.CompilerParams(has_side_effects=True), name="allreduce_small")(buf)


def _adamw_math(w, g, m, v):
    m2 = ADAM_B1 * m + (1.0 - ADAM_B1) * g
    v2 = ADAM_B2 * v + (1.0 - ADAM_B2) * (g * g)
    m_hat = m2 / (1.0 - ADAM_B1 ** ADAM_STEP)
    v_hat = v2 / (1.0 - ADAM_B2 ** ADAM_STEP)
    delta = -ADAM_LR * (m_hat / (jnp.sqrt(v_hat) + ADAM_EPS) + ADAM_WD * w)
    return delta, m2, v2


def _adamw_big(name, w, m, v, mine, theirs):
    R, C = w.shape
    rows = 256 if R % 256 == 0 else R // 2
    nrb = R // rows

    def four(a, b, c, d):
        return ((a.astype(F32) + b.astype(F32)) + c.astype(F32)) + d.astype(F32)

    def fn(wv, mv, vv, *parts):
        g = four(*parts[:4]) + four(*parts[4:])
        return (g,) + _adamw_math(wv, g, mv, vv)

    slots = [_tiled(s.reshape(4 * R, C), None, 0, k * nrb) for s in (mine, theirs) for k in range(4)]
    return _ew(name, fn, [_tiled(w), _tiled(m), _tiled(v)] + slots, [(F32, C)] * 4, n_rows=R, rows=rows)


BIG = ("ffn1_w1", "ffn1_w3", "ffn1_w2", "w_in", "w_branch_a", "w_branch_b", "w_out", "ffn2_w1", "ffn2_w3", "ffn2_w2")
SMALL = ("ffn1_norm", "mix_norm", "b_gate", "q_norm", "k_norm", "rel_bias", "ffn2_norm", "final_norm")
ORDER = ("ffn1_norm", "ffn1_w1", "ffn1_w3", "ffn1_w2", "mix_norm", "w_in", "b_gate", "q_norm", "k_norm", "rel_bias",
         "w_branch_a", "w_branch_b", "w_out", "ffn2_norm", "ffn2_w1", "ffn2_w3", "ffn2_w2", "final_norm")
TRANSPOSED = ("ffn1_w1", "ffn1_w3", "ffn2_w1", "ffn2_w3")
SIBLING_LAG = 2
GATHER_GROUPS = (("ffn1_w1", "ffn1_w3"), ("ffn1_w2",), ("w_in",), ("w_branch_a", "w_branch_b", "w_out"),
                 ("ffn2_w1", "ffn2_w3", "ffn2_w2"))


def _pack_small(d):
    rows = []
    for n in SMALL:
        flat = d[n].reshape(-1)
        pad = (-flat.shape[0]) % LANES
        rows.append(jnp.pad(flat, (0, pad)).reshape(-1, LANES))
    buf = jnp.concatenate(rows, axis=0)
    return jnp.pad(buf, ((0, (-buf.shape[0]) % 8), (0, 0)))


def _unpack_small(buf, like):
    out, r = {}, 0
    for n in SMALL:
        size = like[n].size
        nr = -(-size // LANES)
        out[n] = buf[r:r + nr].reshape(-1)[:size].reshape(like[n].shape)
        r += nr
    return out


def kernel(x, ffn1_norm, ffn1_w1, ffn1_w3, ffn1_w2, mix_norm, w_in, b_gate, q_norm, k_norm, rel_bias, w_branch_a, w_branch_b, w_out, ffn2_norm, ffn2_w1, ffn2_w3, ffn2_w2, final_norm, loss_target, m_ffn1_norm, m_ffn1_w1, m_ffn1_w3, m_ffn1_w2, m_mix_norm, m_w_in, m_b_gate, m_q_norm, m_k_norm, m_rel_bias, m_w_branch_a, m_w_branch_b, m_w_out, m_ffn2_norm, m_ffn2_w1, m_ffn2_w3, m_ffn2_w2, m_final_norm, v_ffn1_norm, v_ffn1_w1, v_ffn1_w3, v_ffn1_w2, v_mix_norm, v_w_in, v_b_gate, v_q_norm, v_k_norm, v_rel_bias, v_w_branch_a, v_w_branch_b, v_w_out, v_ffn2_norm, v_ffn2_w1, v_ffn2_w3, v_ffn2_w2, v_final_norm):
    given = dict(locals())
    w = {n: given[n] for n in ORDER}
    m = {n: given["m_" + n] for n in ORDER}
    v = {n: given["v_" + n] for n in ORDER}
    T, D = x.shape[1], x.shape[2]

    def stored(a, n):
        a = a.reshape(a.shape[1:])
        return a.T if n in TRANSPOSED else a

    def returned(a, n):
        return (a.T if n in TRANSPOSED else a).reshape(w[n].shape)

    quarter = {n: stored(w[n], n) for n in BIG}
    send, recv, _, land_thru, token = _exchange_start(
        "gather_start", None, _own_slots("own_weights", [quarter[n] for n in BIG]), "gather")
    index = {n: i for i, n in enumerate(BIG)}
    ready, filling = {}, {}

    def landed_halves(group, after):
        ids = [index[n] for n in group]
        return _exchange_wait("gather_wait_" + group[0], None, [land_thru[i] for i in ids],
                              [send[i] for i in ids], [recv[i] for i in ids], after, "gather")[1]

    def prefetch_w(name, after):
        group = next(g for g in GATHER_GROUPS if name in g)
        started = _exchange_start("fill_start_" + group[0], None, landed_halves(group, after), "fill")
        filling[group] = started
        return [started[4]]

    def get_w(name, after):
        if name not in ready:
            group = next(g for g in GATHER_GROUPS if name in g)
            if group in filling:
                f_send, f_recv, _, thru, _ = filling[group]
                stacks = _exchange_wait("fill_wait_" + group[0], None, thru, f_send, f_recv, after, "fill")[1]
            else:
                stacks = _fill_from_sibling("gather_fill_" + group[0], landed_halves(group, after))
            for n, st in zip(group, stacks):
                ready[n] = st.reshape(D, D) if n in ("w_branch_b", "w_out") else st
        return ready[name]

    scattered, forwarded = [], []

    def forward_oldest(after):
        names, s_sem, r_sem, srcs, lands = scattered.pop(0)
        _, landed = _exchange_wait("scatter_wait_" + names[0], srcs, lands, s_sem, r_sem, after, "scatter")
        started = _exchange_start("sibling_start_" + names[0], landed, [lax.empty(a.shape, a.dtype) for a in landed], "sibling")
        forwarded.append((names,) + tuple(started[:4]))
        return started[4]

    def put_g(grads):
        names = list(grads)
        stacks = [grads[n].reshape((4,) + quarter[n].shape) for n in names]
        lands = _own_slots("own_grad_" + names[0], stacks, from_stack=True)
        started = _exchange_start("scatter_start_" + names[0], stacks, lands, "scatter")
        scattered.append((names,) + tuple(started[:4]))
        tokens = [started[4]]
        if len(scattered) > SIBLING_LAG:
            tokens.append(forward_oldest(started[4]))
        return tokens

    small = {n: w[n] for n in SMALL}
    packed = [_pack_small({n: d[n] for n in SMALL}) for d in (w, m, v)]
    loss_cols, grad_x, gs = _local_step(x.reshape(T, D), loss_target.reshape(T, D), small, get_w, put_g, deps=[token] + packed,
                                        prefetch_w=prefetch_w)

    after = grad_x
    while scattered:
        after = forward_oldest(after)
    grads, deltas, new_m, new_v = {}, {}, {}, {}
    for names, s_sem, r_sem, srcs, lands in forwarded:
        mine, theirs = _exchange_wait("sibling_wait_" + names[0], srcs, lands, s_sem, r_sem, after, "sibling")
        for n, a, b in zip(names, mine, theirs):
            res = _adamw_big(f"adamw_{n}", quarter[n], stored(m[n], n), stored(v[n], n), a, b)
            grads[n], deltas[n], new_m[n], new_v[n] = [returned(r, n) for r in res]

    gs = {n: gs[n].reshape(w[n].shape) for n in SMALL}
    packed_g = _pack_small(gs)
    n_small = packed_g.shape[0]
    summed = _allreduce_small(jnp.concatenate([packed_g, loss_cols.reshape(-1, LANES)], axis=0))
    g_small, loss = summed[:n_small], jnp.sum(summed[n_small:])
    R = g_small.shape[0]
    res = _ew("adamw_small", lambda wv, mv, vv, g: (g,) + _adamw_math(wv, g, mv, vv),
              [_tiled(packed[0]), _tiled(packed[1]), _tiled(packed[2]), _tiled(g_small)], [(F32, LANES)] * 4, n_rows=R, rows=R)
    for d, buf in zip((grads, deltas, new_m, new_v), res):
        d.update(_unpack_small(buf, w))

    return (loss, grad_x.reshape(x.shape), *[grads[n] for n in ORDER], *[deltas[n] for n in ORDER],
            *[new_m[n] for n in ORDER], *[new_v[n] for n in ORDER])
```

```python
import functools
import math

import numpy as np
import jax
import jax.numpy as jnp
from jax import lax
from jax.experimental import pallas as pl
from jax.experimental.pallas import tpu as pltpu

F32 = jnp.float32
BF16 = jnp.bfloat16
MESH = pl.DeviceIdType.MESH

NEG_INF = -1e30
EPS = 1e-6
GRID_W = 64
ROPE_THETA = 10000.0
DILATIONS = (1, 4, 16)
BAND_HALF = 64
HEAD_A = 64
HEADS_A = 8
WIDTH_A = HEADS_A * HEAD_A
HEAD_B = 128
LOG2_E = math.log2(math.e)
QK_SCALE_LOG2 = HEAD_B ** -0.5 * LOG2_E
N_BUCKETS = 32
MAX_DISTANCE = 1024
ADAM_LR, ADAM_B1, ADAM_B2, ADAM_EPS, ADAM_WD, ADAM_STEP = 0.001, 0.9, 0.999, 1e-08, 0.01, 10

B_Q, B_K, B_V = 4608, 5632, 5888
G_A, G_B = 6144, 7168
IN_WIDTH = 8192

VMEM_LIMIT_BYTES = 56 * 1024 * 1024
QB_A = 128
QB_B = 256


def _params(*sem):
    return pltpu.CompilerParams(dimension_semantics=sem, vmem_limit_bytes=VMEM_LIMIT_BYTES)


def _bs(shape, fn):
    return pl.BlockSpec(shape, fn)


def _resident(shape, fn):
    return pl.BlockSpec(shape, fn, pipeline_mode=pl.Buffered(1))


def _mm(name, grid, pairs, out_shape, out_spec, dims, *, extras=(), epilogue=None, deps=(), reds=()):
    n_pairs, n_extra, n_deps = len(pairs), len(extras), len(deps)
    operands = [p[0] for p in pairs] + [p[2] for p in pairs] + [e[0] for e in extras] + list(deps)
    in_specs = [p[1] for p in pairs] + [p[3] for p in pairs] + [e[1] for e in extras] + _any_specs(n_deps)
    single = not isinstance(out_shape, (list, tuple))
    out_shapes = [out_shape] if single else list(out_shape)
    out_specs = [out_spec] if single else list(out_spec)
    n_out = len(out_shapes)
    out_shapes += [jax.ShapeDtypeStruct((1, w), F32) for w in reds]
    out_specs += [_bs((1, w), lambda *_: (0, 0)) for w in reds]

    def body(*refs):
        a_refs, b_refs = refs[:n_pairs], refs[n_pairs:2 * n_pairs]
        e_refs = refs[2 * n_pairs:2 * n_pairs + n_extra]
        o_refs = refs[2 * n_pairs + n_extra + n_deps:]
        acc = None
        for a_ref, b_ref in zip(a_refs, b_refs):
            t = lax.dot_general(a_ref[...], b_ref[...], (dims, ((), ())), preferred_element_type=F32)
            acc = t if acc is None else acc + t
        vals = acc if epilogue is None else epilogue(acc, *[e[...] for e in e_refs])
        if not isinstance(vals, (list, tuple)):
            vals = (vals,)
        for o_ref, v in zip(o_refs[:n_out], vals[:n_out]):
            o_ref[...] = v.astype(o_ref.dtype)
        if reds:
            first = functools.reduce(jnp.logical_and, [pl.program_id(ax) == 0 for ax in range(len(grid))])
            for r_ref, v in zip(o_refs[n_out:], vals[n_out:]):
                @pl.when(first)
                def _(r_ref=r_ref):
                    r_ref[...] = jnp.zeros_like(r_ref)
                r_ref[...] += v

    sem = ["arbitrary" if reds else "parallel"] * len(grid)
    res = pl.pallas_call(
        body, out_shape=out_shapes, grid=grid, in_specs=in_specs, out_specs=out_specs,
        compiler_params=_params(*sem), name=name)(*operands)
    return res[0] if (single and not reds) else res


NN = ((1,), (0,))
NT = ((1,), (1,))
TN = ((0,), (0,))


def _mm_cols(name, a, w, *, tm, tn, out_dtype, cat, extras=(), epilogue=None):
    M, K = a.shape
    J, _, n = w.shape
    tn = min(tn, n)
    nb = n // tn
    if cat:
        shape, spec = (M, J * n), _bs((tm, tn), lambda j, i, k: (i, j * nb + k))
    else:
        shape, spec = (J, M, n), _bs((None, tm, tn), lambda j, i, k: (j, i, k))
    ex = [(e, _bs((tm, tn), lambda j, i, k: (i, j * nb + k))) for e in extras]
    return _mm(name, (J, M // tm, nb),
               [(a, _bs((tm, K), lambda j, i, k: (i, 0)), w, _bs((None, K, tn), lambda j, i, k: (j, 0, k)))],
               jax.ShapeDtypeStruct(shape, out_dtype), spec, NN, extras=ex, epilogue=epilogue)


def _mm_rows_t(name, a, w, *, tm, out_dtype):
    M, N = a.shape
    J, f, _ = w.shape
    return _mm(name, (J, M // tm),
               [(a, _bs((tm, N), lambda j, i: (i, 0)), w, _bs((None, f, N), lambda j, i: (j, 0, 0)))],
               jax.ShapeDtypeStruct((J, M, f), out_dtype), _bs((None, tm, f), lambda j, i: (j, i, 0)), NT)


def _mm_wgrad(name, a, b, *, a_cols, b_cols, tm, tn, J, deps=()):
    def pick(arr, cols, t):
        if arr.ndim == 3:
            T, c = arr.shape[1], arr.shape[2]
            t = min(t, c)
            return T, c, t, (lambda sel: _bs((None, T, t), lambda j, i, k: (j, 0, sel(i, k))))
        T = arr.shape[0]
        c = arr.shape[1] if cols is None else cols
        t = min(t, c)
        per = c // t
        if cols is None:
            if per == 1:
                return T, c, t, (lambda sel: _resident((T, t), lambda j, i, k: (0, 0)))
            return T, c, t, (lambda sel: _bs((T, t), lambda j, i, k: (0, sel(i, k))))
        return T, c, t, (lambda sel: _bs((T, t), lambda j, i, k: (0, j * per + sel(i, k))))
    _, ca, tm, mk_a = pick(a, a_cols, tm)
    _, cb, tn, mk_b = pick(b, b_cols, tn)
    return _mm(name, (J, ca // tm, cb // tn),
               [(a, mk_a(lambda i, k: i), b, mk_b(lambda i, k: k))],
               jax.ShapeDtypeStruct((J, ca, cb), BF16), _bs((None, tm, tn), lambda j, i, k: (j, i, k)), TN, deps=deps)


def _tiled(arr, width=None, col=0, rowblk=0):
    return ("t", arr, arr.shape[1] if width is None else width, col, rowblk)


def _table(arr):
    return ("f", arr)


def _whole(arr):
    return ("w", arr)


def _ew(name, fn, ins, outs, *, n_rows, rows, reds=(), ncols=1, deps=()):
    nrb = n_rows // rows
    n_deps = len(deps)
    operands, in_specs = [], []
    for spec in ins:
        if spec[0] == "t":
            _, arr, width, col, rowblk = spec
            step = 1 if ncols > 1 else 0
            in_specs.append(_bs((rows, width), lambda c, i, col=col, rowblk=rowblk, step=step: (rowblk + i, col + c * step)))
        elif spec[0] == "f":
            arr = spec[1]
            in_specs.append(_bs((rows, arr.shape[1]), lambda c, i: (i, 0)))
        else:
            arr = spec[1]
            nd = arr.ndim
            if nd == 3:
                in_specs.append(_bs((None,) + arr.shape[1:], lambda c, i: (c, 0, 0)))
            else:
                in_specs.append(_bs(arr.shape, lambda c, i, nd=nd: (0,) * nd))
        operands.append(arr)
    out_shapes = [jax.ShapeDtypeStruct((n_rows, ncols * w), dt) for dt, w in outs]
    out_specs = [_bs((rows, w), lambda c, i: (i, c)) for _, w in outs]
    out_shapes += [jax.ShapeDtypeStruct((ncols, 1, w), F32) for w in reds]
    out_specs += [_bs((None, 1, w), lambda c, i: (c, 0, 0)) for w in reds]
    n_in, n_out, n_red = len(ins), len(outs), len(reds)
    operands += list(deps)
    in_specs += _any_specs(n_deps)

    def body(*refs):
        vals = fn(*[r[...] for r in refs[:n_in]])
        if not isinstance(vals, (tuple, list)):
            vals = (vals,)
        o_refs = refs[n_in + n_deps:]
        for o_ref, v in zip(o_refs[:n_out], vals[:n_out]):
            o_ref[...] = v.astype(o_ref.dtype)
        if n_red:
            i = pl.program_id(1)
            for r_ref, v in zip(o_refs[n_out:], vals[n_out:]):
                @pl.when(i == 0)
                def _(r_ref=r_ref):
                    r_ref[...] = jnp.zeros_like(r_ref)
                r_ref[...] += v

    res = pl.pallas_call(
        body, out_shape=out_shapes, grid=(ncols, nrb), in_specs=in_specs, out_specs=out_specs,
        compiler_params=_params("parallel", "arbitrary" if n_red else "parallel"), name=name)(*operands)
    return res


def _colsum(v):
    return jnp.sum(v, axis=0, keepdims=True)


def _rstd(x):
    return lax.rsqrt(jnp.mean(x * x, axis=-1, keepdims=True) + EPS)


def _sigmoid(x):
    return 1.0 / (1.0 + jnp.exp(-x))


def _norm_fwd(x, g):
    return x * _rstd(x) * g


def _norm_bwd(x, g, dy):
    r = _rstd(x)
    xh = x * r
    dxh = dy * g
    dx = r * (dxh - xh * jnp.mean(dxh * xh, axis=-1, keepdims=True))
    return dx, dy * xh


def _row_spec(arr, rows):
    if arr.shape[0] == 1:
        return _bs(arr.shape, lambda i: (0, 0))
    return _bs((rows, arr.shape[1]), lambda i: (i, 0))


def _ffn_fwd(tag, x, gain, get_w, deps=(), *, h=None, tail_ins=(), tail_fn=None, tail_outs=(F32,), tail_reds=()):
    T, D = x.shape
    if h is None:
        (h,) = _ew(f"{tag}_norm", lambda xv, g: _norm_fwd(xv, g), [_tiled(x), _whole(gain)], [(BF16, D)], n_rows=T, rows=512,
                   deps=deps)
    w1, w3 = get_w(f"{tag}_w1", h), get_w(f"{tag}_w3", h)
    J, f, _ = w1.shape
    tm = 1024

    def up(h_ref, w1_ref, w3_ref, u_ref, g_ref, a_ref):
        hv = h_ref[...]
        u = lax.dot_general(hv, w1_ref[...], (NT, ((), ())), preferred_element_type=F32)
        g = lax.dot_general(hv, w3_ref[...], (NT, ((), ())), preferred_element_type=F32)
        u_ref[...] = u.astype(BF16)
        g_ref[...] = g.astype(BF16)
        a_ref[...] = (u * _sigmoid(u) * g).astype(BF16)

    slab = _bs((None, tm, f), lambda j, i: (j, i, 0))
    w_spec = _bs((None, f, D), lambda j, i: (j, 0, 0))
    u, g, a = pl.pallas_call(
        up, out_shape=[jax.ShapeDtypeStruct((J, T, f), BF16)] * 3, grid=(J, T // tm),
        in_specs=[_bs((tm, D), lambda j, i: (i, 0)), w_spec, w_spec], out_specs=[slab] * 3,
        compiler_params=_params("parallel", "parallel"), name=f"{tag}_up")(h, w1, w3)
    w2 = get_w(f"{tag}_w2", a)
    def tail(acc, xv, *rest):
        y = xv + 0.5 * acc
        return y if tail_fn is None else tail_fn(y, *rest)

    row = _bs((512, D), lambda i: (i, 0))
    res = _mm(f"{tag}_down", (T // 512,),
              [(a, _bs((None, 512, f), lambda i, j=j: (j, i, 0)), w2, _resident((None, f, D), lambda i, j=j: (j, 0, 0)))
               for j in range(J)],
              [jax.ShapeDtypeStruct((T, D), dt) for dt in tail_outs], [row] * len(tail_outs), NN,
              extras=[(x, row)] + [(t, _row_spec(t, 512)) for t in tail_ins], epilogue=tail, reds=tail_reds)
    return res, (h, u, g, a)


def _dh_norm_bwd(name, rows, pairs, dims, x, gain, dres, deps, also_bf16=False):
    T, D = x.shape

    def epilogue(dh, xv, gv, dr):
        dx, dgr = _norm_bwd(xv, gv, dh)
        dx = dx + dr
        return (dx, 0.5 * dx) + ((dx,) if also_bf16 else ()) + (_colsum(dgr),)

    dts = [F32, BF16] + ([BF16] if also_bf16 else [])
    row = _bs((rows, D), lambda i: (i, 0))
    return _mm(name, (T // rows,), pairs, [jax.ShapeDtypeStruct((T, D), dt) for dt in dts], [row] * len(dts), dims,
               extras=[(x, row), (gain, _row_spec(gain, rows)), (dres, row)], epilogue=epilogue, deps=deps, reds=(D,))


def _ffn_bwd(tag, x, gain, get_w, put_g, saved, dy, dy_half, also_bf16=False, last=False):
    h, u, g, a = saved
    T, D = x.shape
    w1, w3, w2 = [get_w(f"{tag}_{n}", dy_half) for n in ("w1", "w3", "w2")]
    J, f, _ = w1.shape
    dw2 = _mm_wgrad(f"{tag}_bwd_dw2", a, dy_half, a_cols=None, b_cols=None, tm=f, tn=D, J=J)
    deps = put_g({f"{tag}_w2": dw2}) if last else []
    tm = 1024

    def up_bwd(dy_ref, w2_ref, u_ref, g_ref, *rest):
        du_ref, dg_ref = rest[-2:]
        da = lax.dot_general(dy_ref[...], w2_ref[...], (NT, ((), ())), preferred_element_type=F32)
        uv, gv = u_ref[...].astype(F32), g_ref[...].astype(F32)
        s = _sigmoid(uv)
        du_ref[...] = (da * gv * (s * (1.0 + uv * (1.0 - s)))).astype(BF16)
        dg_ref[...] = (da * (uv * s)).astype(BF16)

    slab = _bs((None, tm, f), lambda j, i: (j, i, 0))
    du, dg = pl.pallas_call(
        up_bwd, out_shape=[jax.ShapeDtypeStruct((J, T, f), BF16)] * 2, grid=(J, T // tm),
        in_specs=[_bs((tm, D), lambda j, i: (i, 0)), _bs((None, f, D), lambda j, i: (j, 0, 0)), slab, slab] + _any_specs(len(deps)),
        out_specs=[slab] * 2, compiler_params=_params("parallel", "parallel"), name=f"{tag}_bwd_up")(dy_half, w2, u, g, *deps)
    dw1 = _mm_wgrad(f"{tag}_bwd_dw1", du, h, a_cols=None, b_cols=None, tm=f, tn=D, J=J)
    deps = put_g({f"{tag}_w1": dw1}) if last else []
    dw3 = _mm_wgrad(f"{tag}_bwd_dw3", dg, h, a_cols=None, b_cols=None, tm=f, tn=D, J=J, deps=deps)
    deps = put_g({f"{tag}_w3": dw3} if last else {f"{tag}_w2": dw2, f"{tag}_w1": dw1, f"{tag}_w3": dw3})
    pairs = []
    for j in range(J):
        a_spec = _bs((None, 512, f), lambda i, j=j: (j, i, 0))
        w_spec = _resident((None, f, D), lambda i, j=j: (j, 0, 0))
        pairs += [(du, a_spec, w1, w_spec), (dg, a_spec, w3, w_spec)]
    return _dh_norm_bwd(f"{tag}_bwd_dh", 512, pairs, NN, x, gain, dy, deps, also_bf16)


def _t5_bucket(rel):
    n = N_BUCKETS // 2
    max_exact = n // 2
    ret = jnp.where(rel > 0, n, 0)
    a = jnp.abs(rel)
    af = jnp.maximum(a, 1).astype(F32)
    large = max_exact + (jnp.log(af / max_exact) / math.log(MAX_DISTANCE / max_exact) * (n - max_exact)).astype(jnp.int32)
    large = jnp.minimum(large, n - 1)
    return ret + jnp.where(a < max_exact, a, large)


WIN_A = QB_A + 2 * BAND_HALF
WIN_SHIFTS = (0, BAND_HALF, 2 * BAND_HALF)


def _window_variant(n, nblk):
    return jnp.where(n == 0, 0, jnp.where(n == nblk - 1, 2, 1))


def _window_start(n, nblk):
    return pl.multiple_of(jnp.clip(n * QB_A - BAND_HALF, 0, nblk * QB_A - WIN_A), BAND_HALF)


def _band_steps(xp=jnp):
    qi = xp.arange(QB_A, dtype=xp.int32)[None, :, None]
    kj = xp.arange(WIN_A, dtype=xp.int32)[None, None, :]
    return kj - qi - xp.asarray(WIN_SHIFTS, dtype=xp.int32)[:, None, None]


def _bias_tiles(rel_bias):
    wide = QB_A + 2 * WIN_SHIFTS[-1]
    qi = jnp.arange(QB_A, dtype=jnp.int32)[:, None]
    steps = jnp.arange(wide, dtype=jnp.int32)[None, :] - WIN_SHIFTS[-1] - qi
    buckets = jnp.stack([_t5_bucket(steps * d) for d in DILATIONS])
    inband = (jnp.abs(steps) <= BAND_HALF).astype(jnp.int32)
    n_heads = rel_bias.shape[1]

    def body(tab_ref, b_ref, m_ref, o_ref):
        hd = pl.program_id(0)
        bkt = b_ref[...]
        acc = jnp.zeros(bkt.shape, F32)
        for b in range(N_BUCKETS):
            acc = jnp.where(bkt == b, tab_ref[b, hd], acc)
        o_ref[...] = jnp.where(m_ref[...] > 0, acc, NEG_INF)

    base = pl.pallas_call(
        body, out_shape=jax.ShapeDtypeStruct((n_heads, QB_A, wide), F32), grid=(n_heads,),
        in_specs=[pl.BlockSpec(memory_space=pltpu.SMEM),
                  _bs((None, QB_A, wide), lambda hd: (hd // HEADS_A, 0, 0)),
                  _bs((QB_A, wide), lambda hd: (0, 0))],
        out_specs=_bs((None, QB_A, wide), lambda hd: (hd, 0, 0)),
        compiler_params=_params("parallel"), name="a_bias_tiles")(rel_bias, buckets, inband)
    base = base.reshape(len(DILATIONS), HEADS_A, QB_A, wide)
    return jnp.stack([base[..., WIN_SHIFTS[-1] - s:WIN_SHIFTS[-1] - s + WIN_A] for s in WIN_SHIFTS], axis=1)


def _bias_grad(dbias):
    steps = _band_steps(np)
    inband = np.abs(steps) <= BAND_HALF
    present = []
    for d in DILATIONS:
        rel = steps * d
        a = np.abs(rel)
        large = 8 + (np.log(np.maximum(a, 1) / 8.0) / math.log(MAX_DISTANCE / 8.0) * 8).astype(np.int64)
        bk = np.where(rel > 0, 16, 0) + np.where(a < 8, a, np.minimum(large, 15))
        present.append([sorted(set(bk[v][inband[v]].tolist())) for v in range(3)])
    buckets = jnp.stack([_t5_bucket(_band_steps() * d) for d in DILATIONS])
    n_heads = len(DILATIONS) * HEADS_A

    def body(b_ref, d_ref, o_ref):
        row = lax.broadcasted_iota(jnp.int32, (N_BUCKETS, n_heads), 0)
        col = lax.broadcasted_iota(jnp.int32, (N_BUCKETS, n_heads), 1)
        out = jnp.zeros((N_BUCKETS, n_heads), F32)
        for grp in range(len(DILATIONS)):
            for hh in range(HEADS_A):
                hd = grp * HEADS_A + hh
                for b in sorted(set(sum(present[grp], []))):
                    tot = jnp.zeros((), F32)
                    for v in range(3):
                        if b in present[grp][v]:
                            tot = tot + jnp.sum(jnp.where(b_ref[grp, v] == b, d_ref[grp, v, hh], 0.0))
                    out = jnp.where((row == b) & (col == hd), tot, out)
        o_ref[...] = out

    return pl.pallas_call(
        body, out_shape=jax.ShapeDtypeStruct((N_BUCKETS, n_heads), F32),
        compiler_params=pltpu.CompilerParams(vmem_limit_bytes=VMEM_LIMIT_BYTES), name="a_bias_grad")(buckets, dbias)


def _lane_is_second_head(shape):
    return lax.broadcasted_iota(jnp.int32, shape, len(shape) - 1) >= HEAD_A


VIEW_ROWS = 512


def _view_chunks():
    return [pltpu.VMEM((VIEW_ROWS, LANES), F32)] * (WIDTH_A // LANES)


def _rows_to_view(x_ref, col, o_ref, ocol, d, chunks):
    n = VIEW_ROWS // d
    for c, scr in enumerate(chunks):
        scr[...] = x_ref[:, col + c * LANES:col + (c + 1) * LANES].astype(F32)
        for r in range(d):
            at = ocol + r * WIDTH_A + c * LANES
            o_ref[:, at:at + LANES] = scr[pl.ds(r, n, stride=d), :].astype(o_ref.dtype)


def _view_to_rows(v_ref, o_ref, col, d, chunks):
    n = VIEW_ROWS // d
    for c, scr in enumerate(chunks):
        if d == 1:
            o_ref[:, col + c * LANES:col + (c + 1) * LANES] = v_ref[:, c * LANES:(c + 1) * LANES].astype(o_ref.dtype)
            continue
        for r in range(d):
            scr[pl.ds(r, n, stride=d), :] = v_ref[:, r * WIDTH_A + c * LANES:r * WIDTH_A + (c + 1) * LANES].astype(F32)
        o_ref[:, col + c * LANES:col + (c + 1) * LANES] = scr[...].astype(o_ref.dtype)


def _group_view(proj, grp, d):
    T = proj.shape[0]
    if d == 1:
        return proj, (lambda part, r: grp * 3 + part)

    def body(x_ref, o_ref, *chunks):
        for part in range(3):
            _rows_to_view(x_ref, part * WIDTH_A, o_ref, part * d * WIDTH_A, d, chunks)

    view = pl.pallas_call(
        body, out_shape=jax.ShapeDtypeStruct((T // d, 3 * d * WIDTH_A), proj.dtype), grid=(T // VIEW_ROWS,),
        in_specs=[_bs((VIEW_ROWS, 3 * WIDTH_A), lambda i: (i, grp))],
        out_specs=_bs((VIEW_ROWS // d, 3 * d * WIDTH_A), lambda i: (i, 0)),
        scratch_shapes=_view_chunks(), compiler_params=_params("parallel"), name=f"a_view_d{d}")(proj)
    return view, (lambda part, r: part * d + r)


def _stack_heads(v2, second):
    zero = jnp.zeros_like(v2)
    return jnp.concatenate([jnp.where(second, zero, v2), jnp.where(second, v2, zero)], axis=0)


def _unstack_heads(v, second):
    return jnp.where(second, v[QB_A:], v[:QB_A])


def _dil_fwd(view, bias, d):
    pv, colblk = view
    L = pv.shape[0]
    nblk = L // QB_A
    W2 = 2 * HEAD_A
    scale = HEAD_A ** -0.5

    def body(q_ref, k_ref, v_ref, b_ref, o_ref, l_ref):
        win = pl.ds(_window_start(pl.program_id(1), nblk), WIN_A)
        second = _lane_is_second_head((QB_A, W2))
        pairs = range(HEADS_A // 2)
        cols = [slice(hp * W2, (hp + 1) * W2) for hp in pairs]
        s = [lax.dot_general(_stack_heads(q_ref[:, cols[hp]], second), k_ref[win, cols[hp]], (NT, ((), ())),
                             preferred_element_type=F32) * scale + b_ref[2 * hp:2 * hp + 2].reshape(2 * QB_A, WIN_A)
             for hp in pairs]
        m = [jnp.max(x, axis=-1, keepdims=True) for x in s]
        p = [jnp.exp(x - mx) for x, mx in zip(s, m)]
        l = [jnp.sum(x, axis=-1, keepdims=True) for x in p]
        res = [jnp.dot(p[hp].astype(BF16), v_ref[win, cols[hp]], preferred_element_type=F32) / l[hp] for hp in pairs]
        o_ref[...] = jnp.concatenate([_unstack_heads(x, second) for x in res], axis=1).astype(o_ref.dtype)
        l_ref[...] = jnp.concatenate([_unstack_heads(jnp.broadcast_to(mx + jnp.log(lx), (2 * QB_A, W2)), second)
                                      for mx, lx in zip(m, l)], axis=1)

    in_specs = [_bs((QB_A, WIDTH_A), lambda r, n: (n, colblk(0, r))),
                _bs((L, WIDTH_A), lambda r, n: (0, colblk(1, r))), _bs((L, WIDTH_A), lambda r, n: (0, colblk(2, r))),
                _bs((None, HEADS_A, QB_A, WIN_A), lambda r, n: (_window_variant(n, nblk), 0, 0, 0))]
    o, lse = pl.pallas_call(
        body, out_shape=[jax.ShapeDtypeStruct((L, d * WIDTH_A), BF16), jax.ShapeDtypeStruct((L, d * WIDTH_A), F32)],
        grid=(d, nblk), in_specs=in_specs,
        out_specs=[_bs((QB_A, WIDTH_A), lambda r, n: (n, r)), _bs((QB_A, WIDTH_A), lambda r, n: (n, r))],
        compiler_params=_params("parallel", "parallel"), name=f"a_fwd_d{d}")(pv, pv, pv, bias)
    return o, lse


def _dil_bwd(view_qkv, bias, do, lse, cterm, d):
    pv, colblk = view_qkv
    L = pv.shape[0]
    nblk = L // QB_A
    W2 = 2 * HEAD_A
    PPS = 4
    WS = PPS * W2
    ob = WIDTH_A // WS
    scale = HEAD_A ** -0.5

    def body(q_ref, k_ref, v_ref, do_ref, l_ref, c_ref, b_ref, dq_ref, dk_ref, dv_ref, db_ref):
        r, n = pl.program_id(1), pl.program_id(2)

        @pl.when(n == 0)
        def _():
            dk_ref[...] = jnp.zeros_like(dk_ref)
            dv_ref[...] = jnp.zeros_like(dv_ref)

        @pl.when((n == 0) & (r == 0))
        def _():
            db_ref[...] = jnp.zeros_like(db_ref)

        second = _lane_is_second_head((QB_A, W2))
        win = pl.ds(_window_start(n, nblk), WIN_A)
        variant = _window_variant(n, nblk)
        pairs = range(PPS)
        cols = [slice(pp * W2, (pp + 1) * W2) for pp in pairs]

        def head_rows(ref, pp):
            v2 = ref[:, cols[pp]]
            return jnp.concatenate([v2[:, 0:1], v2[:, HEAD_A:HEAD_A + 1]], axis=0)

        kw = [k_ref[win, c] for c in cols]
        vw = [v_ref[win, c] for c in cols]
        qs = [_stack_heads(q_ref[:, c], second) for c in cols]
        dos = [_stack_heads(do_ref[:, c], second) for c in cols]
        s = [lax.dot_general(qs[pp], kw[pp], (NT, ((), ())), preferred_element_type=F32) for pp in pairs]
        dp = [lax.dot_general(dos[pp], vw[pp], (NT, ((), ())), preferred_element_type=F32) for pp in pairs]
        p = [jnp.exp(s[pp] * scale + b_ref[2 * pp:2 * pp + 2].reshape(2 * QB_A, WIN_A) - head_rows(l_ref, pp)) for pp in pairs]
        ds = [p[pp] * (dp[pp] + head_rows(c_ref, pp)) for pp in pairs]
        db_ref[variant] += jnp.concatenate([x.reshape(2, QB_A, WIN_A) for x in ds], axis=0)
        pb = [x.astype(BF16) for x in p]
        dsb = [(x * scale).astype(BF16) for x in ds]
        dq_ref[...] = jnp.concatenate([_unstack_heads(jnp.dot(dsb[pp], kw[pp], preferred_element_type=F32), second)
                                       for pp in pairs], axis=1).astype(dq_ref.dtype)
        dk_ref[win, :] += jnp.concatenate([lax.dot_general(dsb[pp], qs[pp], (TN, ((), ())), preferred_element_type=F32)
                                           for pp in pairs], axis=1)
        dv_ref[win, :] += jnp.concatenate([lax.dot_general(pb[pp], dos[pp], (TN, ((), ())), preferred_element_type=F32)
                                           for pp in pairs], axis=1)

    kv_spec = _resident if d == 1 else _bs
    in_specs = [_bs((QB_A, WS), lambda hp, r, n: (n, colblk(0, r) * ob + hp)),
                kv_spec((L, WS), lambda hp, r, n: (0, colblk(1, r) * ob + hp)),
                kv_spec((L, WS), lambda hp, r, n: (0, colblk(2, r) * ob + hp))]
    in_specs += [_bs((QB_A, WS), lambda hp, r, n: (n, r * ob + hp))] * 3
    in_specs += [_bs((None, 2 * PPS, QB_A, WIN_A), lambda hp, r, n: (_window_variant(n, nblk), hp, 0, 0))]
    out_shape = [jax.ShapeDtypeStruct((L, d * WIDTH_A), BF16), jax.ShapeDtypeStruct((L, d * WIDTH_A), F32),
                 jax.ShapeDtypeStruct((L, d * WIDTH_A), F32), jax.ShapeDtypeStruct((3, HEADS_A, QB_A, WIN_A), F32)]
    out_specs = [_bs((QB_A, WS), lambda hp, r, n: (n, r * ob + hp)),
                 _bs((L, WS), lambda hp, r, n: (0, r * ob + hp)), _bs((L, WS), lambda hp, r, n: (0, r * ob + hp)),
                 _bs((3, 2 * PPS, QB_A, WIN_A), lambda hp, r, n: (0, hp, 0, 0))]
    dq, dk, dv, db = pl.pallas_call(
        body, out_shape=out_shape, grid=(ob, d, nblk), in_specs=in_specs, out_specs=out_specs,
        compiler_params=_params("arbitrary", "arbitrary", "arbitrary"), name=f"a_bwd_d{d}")(
            pv, pv, pv, do, lse, cterm, bias)
    return dq, dk, dv, db


def _assemble_dproj(a_parts, dq_b, dk_b, dv_b, dga, dgb):
    T = dq_b.shape[0]
    flat = [(a_parts[part][g], d) for part in range(3) for g, d in enumerate(DILATIONS)]
    rest = [dq_b, dk_b, dv_b, dga, dgb]

    def body(*refs):
        views, others = refs[:len(flat)], refs[len(flat):len(flat) + len(rest)]
        o_ref, chunks = refs[len(flat) + len(rest)], refs[len(flat) + len(rest) + 1:]
        col = 0
        for v_ref, (_, d) in zip(views, flat):
            _view_to_rows(v_ref, o_ref, col, d, chunks)
            col += WIDTH_A
        for x_ref in others:
            w = x_ref.shape[1]
            o_ref[:, col:col + w] = x_ref[...].astype(o_ref.dtype)
            col += w

    in_specs = [_bs((VIEW_ROWS // d, d * WIDTH_A), lambda i: (i, 0)) for _, d in flat]
    in_specs += [_bs((VIEW_ROWS, x.shape[1]), lambda i: (i, 0)) for x in rest]
    return pl.pallas_call(
        body, out_shape=jax.ShapeDtypeStruct((T, IN_WIDTH), BF16), grid=(T // VIEW_ROWS,), in_specs=in_specs,
        out_specs=_bs((VIEW_ROWS, IN_WIDTH), lambda i: (i, 0)), scratch_shapes=_view_chunks(),
        compiler_params=_params("parallel"), name="mix_bwd_dproj")(*[a for a, _ in flat], *rest)


def _segment_ones():
    i = np.arange(WIDTH_A)
    return jnp.asarray((i[:, None] // HEAD_A == i[None, :] // HEAD_A).astype(np.float32), dtype=BF16)


def _group_weights(l0, l1, l2):
    m = jnp.maximum(jnp.maximum(l0, l1), l2)
    e = [jnp.exp(l - m) for l in (l0, l1, l2)]
    z = e[0] + e[1] + e[2]
    return [ei / z for ei in e]


def _view_specs():
    return [_bs((VIEW_ROWS // d, d * WIDTH_A), lambda i: (i, 0)) for d in DILATIONS]


def _stage_tiles(n):
    return [pltpu.VMEM((VIEW_ROWS, WIDTH_A), F32)] * n


def _combine_fwd(outs, lses):
    T = outs[0].shape[0] * DILATIONS[0]
    n = len(DILATIONS)

    def body(*refs):
        o_refs, l_refs, oa_ref = refs[:n], refs[n:2 * n], refs[2 * n]
        o_st, l_st, chunks = refs[2 * n + 1:3 * n + 1], refs[3 * n + 1:4 * n + 1], refs[4 * n + 1:]
        for g, d in enumerate(DILATIONS):
            _view_to_rows(o_refs[g], o_st[g], 0, d, chunks)
            _view_to_rows(l_refs[g], l_st[g], 0, d, chunks)
        w = _group_weights(*[l[...] for l in l_st])
        oa_ref[...] = (w[0] * o_st[0][...] + w[1] * o_st[1][...] + w[2] * o_st[2][...]).astype(oa_ref.dtype)

    return pl.pallas_call(
        body, out_shape=jax.ShapeDtypeStruct((T, WIDTH_A), BF16), grid=(T // VIEW_ROWS,),
        in_specs=_view_specs() * 2, out_specs=_bs((VIEW_ROWS, WIDTH_A), lambda i: (i, 0)),
        scratch_shapes=_stage_tiles(2 * n) + _view_chunks(), compiler_params=_params("parallel"), name="a_combine")(*outs, *lses)


def _combine_bwd(doa, outs, lses):
    T = doa.shape[0]
    n = len(DILATIONS)

    def body(*refs):
        d_ref, o_refs, l_refs, seg_ref = refs[0], refs[1:n + 1], refs[n + 1:2 * n + 1], refs[2 * n + 1]
        do_refs, c_refs = refs[2 * n + 2:3 * n + 2], refs[3 * n + 2:4 * n + 2]
        o_st, l_st = refs[4 * n + 2:5 * n + 2], refs[5 * n + 2:6 * n + 2]
        tmp, chunks = refs[6 * n + 2], refs[6 * n + 3:]
        for g, d in enumerate(DILATIONS):
            _view_to_rows(o_refs[g], o_st[g], 0, d, chunks)
            _view_to_rows(l_refs[g], l_st[g], 0, d, chunks)
        dv = d_ref[...].astype(F32)
        w = _group_weights(*[l[...] for l in l_st])
        seg = seg_ref[...]
        tot = jnp.zeros(dv.shape, F32)
        for g in range(n):
            prod = w[g] * dv * o_st[g][...]
            hi = prod.astype(BF16)
            lo = (prod - hi.astype(F32)).astype(BF16)
            tot = tot + jnp.dot(hi, seg, preferred_element_type=F32) + jnp.dot(lo, seg, preferred_element_type=F32)
        for g, d in enumerate(DILATIONS):
            tmp[...] = w[g] * dv
            _rows_to_view(tmp, 0, do_refs[g], 0, d, chunks)
            tmp[...] = -w[g] * tot
            _rows_to_view(tmp, 0, c_refs[g], 0, d, chunks)

    views = [jax.ShapeDtypeStruct((T // d, d * WIDTH_A), dt) for dt in (BF16, F32) for d in DILATIONS]
    res = pl.pallas_call(
        body, out_shape=views, grid=(T // VIEW_ROWS,),
        in_specs=[_bs((VIEW_ROWS, WIDTH_A), lambda i: (i, 0))] + _view_specs() * 2 + [_bs((WIDTH_A, WIDTH_A), lambda i: (0, 0))],
        out_specs=_view_specs() * 2, scratch_shapes=_stage_tiles(2 * n + 1) + _view_chunks(),
        compiler_params=_params("parallel"), name="a_combine_bwd")(doa, *outs, *lses, _segment_ones())
    return res[:n], res[n:]


def _rope_tables(T):
    rows = T // GRID_W
    row = jnp.repeat(jnp.arange(rows, dtype=F32), GRID_W)
    col = jnp.tile(jnp.arange(GRID_W, dtype=F32), rows)
    n_freq = HEAD_B // 4
    freq = ROPE_THETA ** (-jnp.arange(n_freq, dtype=F32) / n_freq)
    ang = jnp.concatenate([row[:, None] * freq, col[:, None] * freq], axis=-1)
    cos, sin = jnp.repeat(jnp.cos(ang), 2, axis=1), jnp.repeat(jnp.sin(ang), 2, axis=1)
    sign = jnp.where(jnp.arange(HEAD_B) % 2 == 0, -1.0, 1.0).astype(F32)
    return cos, sin * sign


def _swap_pairs(v):
    even = lax.broadcasted_iota(jnp.int32, v.shape, v.ndim - 1) % 2 == 0
    n = v.shape[-1]
    return jnp.where(even, pltpu.roll(v, n - 1, v.ndim - 1), pltpu.roll(v, 1, v.ndim - 1))


def _qk_fwd(name, proj, col0, n_heads, gain, cos, sin, out_scale=1.0, deps=()):
    T = proj.shape[0]

    def fn(xr, g, c, s):
        xn = _norm_fwd(xr.astype(F32), g)
        return (xn * c + _swap_pairs(xn) * s) * out_scale

    (out,) = _ew(name, fn, [_tiled(proj, HEAD_B, col0 // HEAD_B), _whole(gain), _table(cos), _table(sin)],
                 [(BF16, HEAD_B)], n_rows=T, rows=2048, ncols=n_heads, deps=deps)
    return out


def _qk_bwd(name, dout, proj, col0, n_heads, gain, cos, sin, in_scale=1.0):
    T = proj.shape[0]

    def fn(dv, xr, g, c, s):
        dv = dv.astype(F32) * in_scale
        dxn = c * dv + _swap_pairs(s * dv)
        dx, dgr = _norm_bwd(xr.astype(F32), g, dxn)
        return dx, _colsum(dgr)

    dx, dg = _ew(name, fn, [_tiled(dout, HEAD_B, 0), _tiled(proj, HEAD_B, col0 // HEAD_B), _whole(gain),
                            _table(cos), _table(sin)],
                 [(BF16, HEAD_B)], n_rows=T, rows=2048, reds=(HEAD_B,), ncols=n_heads)
    return dx, jnp.sum(dg, axis=0)


def _gqa_fwd(qn, kn, proj, k_col=0):
    T = qn.shape[0]
    GW = 4 * HEAD_B
    QB = QB_B

    def body(q_ref, k_ref, v_ref, o_ref, l_ref):
        k, v = k_ref[...], v_ref[...]
        lane = lax.broadcasted_iota(jnp.int32, (QB, HEAD_B), 1)
        heads = range(4)
        s = [lax.dot_general(q_ref[:, g * HEAD_B:(g + 1) * HEAD_B], k, (NT, ((), ())), preferred_element_type=F32)
             for g in heads]
        m = [jnp.max(x, axis=-1, keepdims=True) for x in s]
        p = [jnp.exp2(x - mx) for x, mx in zip(s, m)]
        l = [jnp.sum(x, axis=-1, keepdims=True) for x in p]
        o = [jnp.dot(p[g].astype(BF16), v, preferred_element_type=F32) / l[g] for g in heads]
        o_ref[...] = jnp.concatenate(o, axis=1).astype(o_ref.dtype)
        lse_all = jnp.zeros((QB, HEAD_B), F32)
        for g in heads:
            lse_all = jnp.where(lane == g, m[g] + jnp.log2(l[g]), lse_all)
        l_ref[...] = lse_all

    return pl.pallas_call(
        body, out_shape=[jax.ShapeDtypeStruct((T, 2 * GW), BF16), jax.ShapeDtypeStruct((2, T, HEAD_B), F32)],
        grid=(2, T // QB),
        in_specs=[_bs((QB, GW), lambda kv, i: (i, kv)), _bs((T, HEAD_B), lambda kv, i: (0, k_col + kv)),
                  _bs((T, HEAD_B), lambda kv, i: (0, B_V // HEAD_B + kv))],
        out_specs=[_bs((QB, GW), lambda kv, i: (i, kv)), _bs((None, QB, HEAD_B), lambda kv, i: (kv, i, 0))],
        compiler_params=_params("parallel", "parallel"), name="b_fwd")(qn, kn, proj)


def _gqa_bwd(qn, kn, proj, o, lse, do, deps=(), k_col=0):
    T = qn.shape[0]
    GW = 4 * HEAD_B

    def body(q_ref, k_ref, v_ref, o_ref, l_ref, do_ref, *rest):
        dq_ref, dk_ref, dv_ref = rest[-3:]
        i = pl.program_id(1)

        @pl.when(i == 0)
        def _():
            dk_ref[...] = jnp.zeros_like(dk_ref)
            dv_ref[...] = jnp.zeros_like(dv_ref)

        k, v = k_ref[...], v_ref[...]
        lse_all = l_ref[...]
        for g in range(4):
            cols = slice(g * HEAD_B, (g + 1) * HEAD_B)
            q, dob = q_ref[:, cols], do_ref[:, cols]
            delta = jnp.sum(dob.astype(F32) * o_ref[:, cols].astype(F32), axis=-1, keepdims=True)
            s = lax.dot_general(q, k, (NT, ((), ())), preferred_element_type=F32)
            p = jnp.exp2(s - lse_all[:, g:g + 1])
            dp = lax.dot_general(dob, v, (NT, ((), ())), preferred_element_type=F32)
            ds = (p * (dp - delta)).astype(BF16)
            dq_ref[:, cols] = jnp.dot(ds, k, preferred_element_type=F32).astype(dq_ref.dtype)
            dk_ref[...] += lax.dot_general(ds, q, (TN, ((), ())), preferred_element_type=F32)
            dv_ref[...] += lax.dot_general(p.astype(BF16), dob, (TN, ((), ())), preferred_element_type=F32)

    return pl.pallas_call(
        body, out_shape=[jax.ShapeDtypeStruct((T, 2 * GW), BF16), jax.ShapeDtypeStruct((T, 2 * HEAD_B), F32),
                         jax.ShapeDtypeStruct((T, 2 * HEAD_B), F32)],
        grid=(2, T // QB_B),
        in_specs=[_bs((QB_B, GW), lambda kv, i: (i, kv)), _bs((T, HEAD_B), lambda kv, i: (0, k_col + kv)),
                  _bs((T, HEAD_B), lambda kv, i: (0, B_V // HEAD_B + kv)), _bs((QB_B, GW), lambda kv, i: (i, kv)),
                  _bs((None, QB_B, HEAD_B), lambda kv, i: (kv, i, 0)), _bs((QB_B, GW), lambda kv, i: (i, kv))] + _any_specs(len(deps)),
        out_specs=[_bs((QB_B, GW), lambda kv, i: (i, kv)), _bs((T, HEAD_B), lambda kv, i: (0, kv)),
                   _bs((T, HEAD_B), lambda kv, i: (0, kv))],
        compiler_params=_params("parallel", "arbitrary"), name="b_bwd")(qn, kn, proj, o, lse, do, *deps)


def _local_step(x, target, small, get_w, put_g, deps=(), prefetch_w=lambda name, after: []):
    T, D = x.shape
    gs = {}

    bias = _bias_tiles(small["rel_bias"])
    cos, sin = _rope_tables(T)
    (x1, h2), ffn1_saved = _ffn_fwd("ffn1", x, small["ffn1_norm"], lambda name, after: get_w(name, [after, bias, cos, sin]), deps,
                                    tail_ins=[small["mix_norm"]], tail_fn=lambda y, g: (y, _norm_fwd(y, g)), tail_outs=(F32, BF16))
    w_in = get_w("w_in", h2)
    nq = w_in.shape[2]
    tpq = nq // WIDTH_A

    def proj_tile(j, k):
        c = j * tpq + k
        return jnp.where(c < 3 * len(DILATIONS), (c % 3) * 3 + c // 3, c)

    proj = _mm("mix_in", (4, tpq),
               [(h2, _resident((T, D), lambda j, k: (0, 0)), w_in, _bs((None, D, WIDTH_A), lambda j, k: (j, 0, k)))],
               jax.ShapeDtypeStruct((T, IN_WIDTH), BF16), _bs((T, WIDTH_A), lambda j, k: (0, proj_tile(j, k))), NN)

    a_views = [_group_view(proj, grp, d) for grp, d in enumerate(DILATIONS)]
    a_outs, a_lses = [], []
    for grp, d in enumerate(DILATIONS):
        o, l = _dil_fwd(a_views[grp], bias[grp], d)
        a_outs.append(o)
        a_lses.append(l)
    o_a = _combine_fwd(a_outs, a_lses)

    qk_gain = jnp.concatenate([jnp.tile(small["q_norm"] * QK_SCALE_LOG2, (8, 1)), jnp.tile(small["k_norm"], (2, 1))])[:, None, :]
    qkn = _qk_fwd("b_qknorm", proj, B_Q, 10, qk_gain, cos, sin, deps=prefetch_w("w_branch_a", proj))
    qn, kn, k_col = qkn, qkn, 8
    o_b, lse_b = _gqa_fwd(qn, kn, proj, k_col)
    ahead = prefetch_w("ffn2_w1", o_b)

    wa, wb, wo = get_w("w_branch_a", o_b), get_w("w_branch_b", o_b), get_w("w_out", o_b)
    bg_a, bg_b = small["b_gate"][:, :D], small["b_gate"][:, D:]
    n_a = wa.shape[0]

    def merge_out(oa_ref, ob_ref, ga_ref, gb_ref, x1_ref, wa_ref, wb_ref, wo_ref, ba_ref, bb_ref, g2_ref, *rest):
        ta_ref, tb_ref, mg_ref, x2_ref, hn_ref = rest[-5:]
        oa = oa_ref[...]
        ta = jnp.concatenate([jnp.dot(oa, wa_ref[j], preferred_element_type=F32) for j in range(n_a)], axis=1)
        tb = jnp.dot(ob_ref[...], wb_ref[...], preferred_element_type=F32)
        sa = _sigmoid(ga_ref[...].astype(F32) + ba_ref[...])
        sb = _sigmoid(gb_ref[...].astype(F32) + bb_ref[...])
        merged = (sa * ta + sb * tb).astype(BF16)
        ta_ref[...], tb_ref[...], mg_ref[...] = ta.astype(BF16), tb.astype(BF16), merged
        y = x1_ref[...] + jnp.dot(merged, wo_ref[...], preferred_element_type=F32)
        x2_ref[...] = y
        hn_ref[...] = _norm_fwd(y, g2_ref[...]).astype(BF16)

    row = _bs((512, D), lambda i: (i, 0))
    gate_specs = [_bs((512, D), lambda i: (i, G_A // D)), _bs((512, D), lambda i: (i, G_B // D))]
    whole2, whole3 = (lambda i: (0, 0)), (lambda i: (0, 0, 0))
    vec = _bs((1, D), whole2)
    t_a, t_b, merged, x2, hn2 = pl.pallas_call(
        merge_out, out_shape=[jax.ShapeDtypeStruct((T, D), BF16)] * 3 + [jax.ShapeDtypeStruct((T, D), F32), jax.ShapeDtypeStruct((T, D), BF16)],
        grid=(T // 512,),
        in_specs=[_bs((512, WIDTH_A), lambda i: (i, 0)), row] + gate_specs + [row, _resident(wa.shape, whole3), _resident((D, D), whole2),
                                                                                _resident((D, D), whole2), vec, vec, vec]
        + _any_specs(len(ahead)),
        out_specs=[row] * 5, compiler_params=_params("parallel"), name="mix_merge_out")(
            o_a, o_b, proj, proj, x1, wa, wb, wo, bg_a, bg_b, small["ffn2_norm"], *ahead)

    def head(xv, g, tv):
        r = _rstd(xv)
        xh = xv * r
        e = xh * g - tv
        dy = e * (1.0 / D)
        dxh = dy * g
        dx = r * (dxh - xh * jnp.mean(dxh * xh, axis=-1, keepdims=True))
        return dx, 0.5 * dx, _colsum(e * e) * (0.5 / D), _colsum(dy * xh)

    (dx3, dx3_half, loss_cols, g_final), ffn2_saved = _ffn_fwd(
        "ffn2", x2, small["ffn2_norm"], get_w, h=hn2, tail_ins=[small["final_norm"].reshape(1, D), target], tail_fn=head,
        tail_outs=(F32, BF16), tail_reds=(D, D))
    gs["final_norm"] = g_final.reshape(D)

    dx2, _, dmix, gs["ffn2_norm"] = _ffn_bwd("ffn2", x2, small["ffn2_norm"], get_w, put_g, ffn2_saved, dx3, dx3_half,
                                             also_bf16=True)
    g_out = _mm_wgrad("mix_bwd_dwout", merged, dmix, a_cols=D // 4, b_cols=None, tm=256, tn=512, J=4).reshape(D, D)

    def merge_out_bwd(dx_ref, ta_ref, tb_ref, ga_ref, gb_ref, wa_ref, wb_ref, wo_ref, ba_ref, bb_ref,
                      dta_ref, dtb_ref, dga_ref, dgb_ref, doa_ref, dob_ref, dba_ref, dbb_ref):
        dm = lax.dot_general(dx_ref[...], wo_ref[...], (NT, ((), ())), preferred_element_type=F32)
        ta, tb = ta_ref[...].astype(F32), tb_ref[...].astype(F32)
        sa = _sigmoid(ga_ref[...].astype(F32) + ba_ref[...])
        sb = _sigmoid(gb_ref[...].astype(F32) + bb_ref[...])
        dga, dgb = dm * ta * sa * (1.0 - sa), dm * tb * sb * (1.0 - sb)
        dta, dtb = (dm * sa).astype(BF16), (dm * sb).astype(BF16)
        dta_ref[...], dtb_ref[...] = dta, dtb
        dga_ref[...], dgb_ref[...] = dga.astype(BF16), dgb.astype(BF16)
        w = wa_ref.shape[2]
        doa = sum(lax.dot_general(dta[:, j * w:(j + 1) * w], wa_ref[j], (NT, ((), ())), preferred_element_type=F32) for j in range(n_a))
        doa_ref[...] = doa.astype(BF16)
        dob_ref[...] = lax.dot_general(dtb, wb_ref[...], (NT, ((), ())), preferred_element_type=F32).astype(BF16)

        @pl.when(pl.program_id(0) == 0)
        def _():
            dba_ref[...] = jnp.zeros_like(dba_ref)
            dbb_ref[...] = jnp.zeros_like(dbb_ref)
        dba_ref[...] += _colsum(dga)
        dbb_ref[...] += _colsum(dgb)

    rowb = _bs((256, D), lambda i: (i, 0))
    gate_specs = [_bs((256, D), lambda i: (i, G_A // D)), _bs((256, D), lambda i: (i, G_B // D))]
    dta, dtb, dga, dgb, do_a, do_b, dba, dbb = pl.pallas_call(
        merge_out_bwd,
        out_shape=[jax.ShapeDtypeStruct((T, D), BF16)] * 4 + [jax.ShapeDtypeStruct((T, WIDTH_A), BF16), jax.ShapeDtypeStruct((T, D), BF16)]
        + [jax.ShapeDtypeStruct((1, D), F32)] * 2,
        grid=(T // 256,),
        in_specs=[rowb, rowb, rowb] + gate_specs + [_resident(wa.shape, whole3), _resident((D, D), whole2), _resident((D, D), whole2), vec, vec],
        out_specs=[rowb] * 4 + [_bs((256, WIDTH_A), lambda i: (i, 0)), rowb, vec, vec],
        compiler_params=_params("arbitrary"), name="mix_merge_out_bwd")(dmix, t_a, t_b, proj, proj, wa, wb, wo, bg_a, bg_b)
    gs["b_gate"] = jnp.concatenate([dba, dbb], axis=1)

    g_a = _mm_wgrad("mix_bwd_dwa", o_a, dta, a_cols=None, b_cols=D // 4, tm=WIDTH_A, tn=256, J=4)
    g_b = _mm_wgrad("mix_bwd_dwb", o_b, dtb, a_cols=D // 4, b_cols=None, tm=256, tn=512, J=4).reshape(D, D)
    deps = put_g({"w_out": g_out, "w_branch_a": g_a, "w_branch_b": g_b})

    dqn, dkn, dv_b = _gqa_bwd(qn, kn, proj, o_b, lse_b, do_b, deps, k_col)
    dq_b, gs["q_norm"] = _qk_bwd("b_bwd_qnorm", dqn, proj, B_Q, 8, small["q_norm"], cos, sin, in_scale=HEAD_B ** -0.5)
    dk_b, gs["k_norm"] = _qk_bwd("b_bwd_knorm", dkn, proj, B_K, 2, small["k_norm"], cos, sin, in_scale=1.0 / LOG2_E)

    do_groups, c_groups = _combine_bwd(do_a, a_outs, a_lses)
    dqs, dks, dvs, dbs = [], [], [], []
    for grp, d in enumerate(DILATIONS):
        dq, dk, dv, db = _dil_bwd(a_views[grp], bias[grp], do_groups[grp], a_lses[grp], c_groups[grp], d)
        dqs.append(dq), dks.append(dk), dvs.append(dv), dbs.append(db)
    gs["rel_bias"] = _bias_grad(jnp.stack(dbs))

    dproj = _assemble_dproj([dqs, dks, dvs], dq_b, dk_b, dv_b, dga, dgb)
    nq = w_in.shape[2]
    g_in = _mm("mix_bwd_dwin", (4, tpq),
               [(h2, _resident((T, D), lambda j, k: (0, 0)), dproj, _bs((T, WIDTH_A), lambda j, k: (0, j * tpq + k)))],
               jax.ShapeDtypeStruct((4, D, nq), BF16), _bs((None, D, WIDTH_A), lambda j, k: (j, 0, k)), TN)
    deps = put_g({"w_in": g_in})
    dx1, dx1_half, gs["mix_norm"] = _dh_norm_bwd(
        "mix_bwd_dh", 256,
        [(dproj, _bs((256, nq), lambda i, j=j: (i, j)), w_in, _resident((None, D, nq), lambda i, j=j: (j, 0, 0))) for j in range(4)],
        NT, x1, small["mix_norm"], dx2, deps)

    dx0, _, gs["ffn1_norm"] = _ffn_bwd("ffn1", x, small["ffn1_norm"], get_w, put_g, ffn1_saved, dx1, dx1_half, last=True)
    return loss_cols, dx0, gs


def _position():
    return lax.axis_index("x"), lax.axis_index("y"), lax.axis_index("c")


def _any_specs(n):
    return [pl.BlockSpec(memory_space=pl.ANY)] * n


HBM_SPEC = pl.BlockSpec(memory_space=pltpu.HBM)
SEM_SPEC = pl.BlockSpec(memory_space=pltpu.SEMAPHORE)
DATAFLOW_EFFECT = pltpu.SideEffectType.DATAFLOW_SIDE_EFFECTING
N_PEER_CHIPS = 3
LANES = 128


def _quarter_copies(srcs, lands, send_sems, recv_sems, mode):
    x, y, c = _position()
    me = 2 * x + y
    peers = [(1 - x, y, c), (x, 1 - y, c), (1 - x, 1 - y, c)]
    copies = []
    for src, land, send, recv in zip(srcs, lands, send_sems, recv_sems):
        if mode == "sibling":
            copies.append(pltpu.make_async_remote_copy(src_ref=src, dst_ref=land, send_sem=send.at[0], recv_sem=recv.at[0],
                                                       device_id=(x, y, 1 - c), device_id_type=MESH))
            continue
        if mode == "fill":
            half = land.shape[1] // 2
            for p, (px, py, _) in enumerate(peers):
                part = land.at[2 * px + py, pl.ds(c * half, half)]
                copies.append(pltpu.make_async_remote_copy(src_ref=part, dst_ref=part, send_sem=send.at[p], recv_sem=recv.at[p],
                                                           device_id=(x, y, 1 - c), device_id_type=MESH))
            continue
        scatter = mode == "scatter"
        half = land.shape[1] // 2
        mine = land.at[me, pl.ds(c * half, half)]
        for p, (px, py, pc) in enumerate(peers):
            copies.append(pltpu.make_async_remote_copy(
                src_ref=src.at[2 * px + py] if scatter else mine, dst_ref=land.at[me] if scatter else mine,
                send_sem=send.at[p], recv_sem=recv.at[p], device_id=(px, py, pc), device_id_type=MESH))
    return copies


def _fill_from_sibling(name, stacks):
    n = len(stacks)

    def body(*refs):
        outs = refs[n:2 * n]
        send_sems, recv_sems = refs[2 * n:]
        x, y, c = _position()
        copies = []
        for i, ref in enumerate(outs):
            half = ref.shape[1] // 2
            rows = pl.ds(c * half, half)
            for p, k in enumerate((2 * (1 - x) + y, 2 * x + (1 - y), 2 * (1 - x) + (1 - y))):
                cp = pltpu.make_async_remote_copy(ref.at[k, rows], ref.at[k, rows], send_sems.at[3 * i + p], recv_sems.at[3 * i + p],
                                                  device_id=(x, y, 1 - c), device_id_type=MESH)
                cp.start()
                copies.append(cp)
        for cp in copies:
            cp.wait()

    return pl.pallas_call(
        body, out_shape=[jax.ShapeDtypeStruct(s.shape, s.dtype) for s in stacks],
        in_specs=_any_specs(n), out_specs=_any_specs(n), input_output_aliases={i: i for i in range(n)},
        scratch_shapes=[pltpu.SemaphoreType.DMA((N_PEER_CHIPS * n,)), pltpu.SemaphoreType.DMA((N_PEER_CHIPS * n,))],
        compiler_params=pltpu.CompilerParams(has_side_effects=True), name=name)(*stacks)


def _exchange_start(name, srcs, lands, mode):
    n = len(lands)
    arrays = list(lands) if srcs is None else list(srcs) + list(lands)
    k = len(arrays)

    def body(*refs):
        land_refs = refs[k - n:k]
        send_sems, recv_sems = refs[k:k + n], refs[k + n:k + 2 * n]
        token = refs[2 * k + 2 * n]
        for cp in _quarter_copies(refs[:n], land_refs, send_sems, recv_sems, mode):
            cp.start()
        token[...] = jnp.zeros_like(token)

    sem = pltpu.SemaphoreType.DMA((N_PEER_CHIPS,))
    out_shape = [sem] * (2 * n) + [pltpu.HBM(a.shape, a.dtype) for a in arrays] + [jax.ShapeDtypeStruct((8, LANES), F32)]
    res = pl.pallas_call(
        body, name=name, out_shape=out_shape, in_specs=[HBM_SPEC] * k,
        out_specs=[SEM_SPEC] * (2 * n) + [HBM_SPEC] * k + [pl.BlockSpec(memory_space=pltpu.VMEM)],
        input_output_aliases={i: 2 * n + i for i in range(k)},
        compiler_params=pltpu.CompilerParams(has_side_effects=DATAFLOW_EFFECT),
    )(*[pltpu.with_memory_space_constraint(a, pltpu.HBM) for a in arrays])
    thru = res[2 * n:2 * n + k]
    return res[:n], res[n:2 * n], (None if srcs is None else thru[:n]), thru[k - n:], res[2 * n + k]


def _exchange_wait(name, srcs, lands, send_sems, recv_sems, after, mode):
    n = len(lands)
    arrays = list(lands) if srcs is None else list(srcs) + list(lands)
    k = len(arrays)
    after = list(after) if isinstance(after, (list, tuple)) else [after]

    def body(*refs):
        sends, recvs = refs[k:k + n], refs[k + n:k + 2 * n]
        for cp in _quarter_copies(refs[:n], refs[k - n:k], sends, recvs, mode):
            cp.wait_send()
            cp.wait_recv()

    res = pl.pallas_call(
        body, name=name, out_shape=[pltpu.HBM(a.shape, a.dtype) for a in arrays],
        in_specs=[HBM_SPEC] * k + [SEM_SPEC] * (2 * n) + _any_specs(len(after)),
        out_specs=[HBM_SPEC] * k, input_output_aliases={i: i for i in range(k)},
        compiler_params=pltpu.CompilerParams(has_side_effects=DATAFLOW_EFFECT),
    )(*arrays, *send_sems, *recv_sems, *after)
    return (None if srcs is None else res[:n]), res[k - n:]


def _own_slots(name, srcs, from_stack=False):
    n = len(srcs)
    me = (2 * lax.axis_index("x") + lax.axis_index("y")).astype(jnp.int32).reshape(1)

    def body(me_ref, *refs):
        for x_ref, o_ref in zip(refs[:n], refs[n:]):
            o_ref[...] = x_ref[...].astype(o_ref.dtype)

    in_specs, out_specs, out_shape = [], [], []
    for src in srcs:
        R, C = src.shape[-2:]
        in_specs.append(pl.BlockSpec((None, R // 2, C), lambda i, me_ref: (me_ref[0], i, 0)) if from_stack
                        else pl.BlockSpec((R // 2, C), lambda i, me_ref: (i, 0)))
        out_specs.append(pl.BlockSpec((None, R // 2, C), lambda i, me_ref: (me_ref[0], i, 0)))
        out_shape.append(jax.ShapeDtypeStruct((4, R, C), BF16))
    grid_spec = pltpu.PrefetchScalarGridSpec(num_scalar_prefetch=1, grid=(2,), in_specs=in_specs, out_specs=out_specs)
    return pl.pallas_call(body, out_shape=out_shape, grid_spec=grid_spec, compiler_params=_params("parallel"), name=name)(me, *srcs)


def _allreduce_small(buf):
    R, C = buf.shape
    flips = [(fx, fy, fc) for fx in (0, 1) for fy in (0, 1) for fc in (0, 1)][1:]

    def body(in_ref, out_ref, land_ref, send_sems, recv_sems):
        x, y, c = _position()
        me = 4 * x + 2 * y + c
        copies = []
        for k, (fx, fy, fc) in enumerate(flips):
            px, py, pc = (1 - x if fx else x), (1 - y if fy else y), (1 - c if fc else c)
            cp = pltpu.make_async_remote_copy(in_ref, land_ref.at[me], send_sems.at[k], recv_sems.at[k],
                                              device_id=(px, py, pc), device_id_type=MESH)
            cp.start()
            copies.append(cp)
        land_ref[me] = in_ref[...]
        for cp in copies:
            cp.wait()
        acc = land_ref[0]
        for k in range(1, 8):
            acc = acc + land_ref[k]
        out_ref[...] = acc

    return pl.pallas_call(
        body, out_shape=jax.ShapeDtypeStruct((R, C), F32),
        in_specs=[pl.BlockSpec(memory_space=pltpu.VMEM)], out_specs=pl.BlockSpec(memory_space=pltpu.VMEM),
        scratch_shapes=[pltpu.VMEM((8, R, C), F32), pltpu.SemaphoreType.DMA((7,)), pltpu.SemaphoreType.DMA((7,))],
        compiler_params=pltpu.CompilerParams(has_side_effects=True), name="allreduce_small")(buf)


def _adamw_math(w, g, m, v):
    m2 = ADAM_B1 * m + (1.0 - ADAM_B1) * g
    v2 = ADAM_B2 * v + (1.0 - ADAM_B2) * (g * g)
    m_hat = m2 / (1.0 - ADAM_B1 ** ADAM_STEP)
    v_hat = v2 / (1.0 - ADAM_B2 ** ADAM_STEP)
    delta = -ADAM_LR * (m_hat / (jnp.sqrt(v_hat) + ADAM_EPS) + ADAM_WD * w)
    return delta, m2, v2


def _adamw_big(name, w, m, v, mine, theirs):
    R, C = w.shape
    rows = 256 if R % 256 == 0 else R // 2
    nrb = R // rows

    def four(a, b, c, d):
        return ((a.astype(F32) + b.astype(F32)) + c.astype(F32)) + d.astype(F32)

    def fn(wv, mv, vv, *parts):
        g = four(*parts[:4]) + four(*parts[4:])
        return (g,) + _adamw_math(wv, g, mv, vv)

    slots = [_tiled(s.reshape(4 * R, C), None, 0, k * nrb) for s in (mine, theirs) for k in range(4)]
    return _ew(name, fn, [_tiled(w), _tiled(m), _tiled(v)] + slots, [(F32, C)] * 4, n_rows=R, rows=rows)


BIG = ("ffn1_w1", "ffn1_w3", "ffn1_w2", "w_in", "w_branch_a", "w_branch_b", "w_out", "ffn2_w1", "ffn2_w3", "ffn2_w2")
SMALL = ("ffn1_norm", "mix_norm", "b_gate", "q_norm", "k_norm", "rel_bias", "ffn2_norm", "final_norm")
ORDER = ("ffn1_norm", "ffn1_w1", "ffn1_w3", "ffn1_w2", "mix_norm", "w_in", "b_gate", "q_norm", "k_norm", "rel_bias",
         "w_branch_a", "w_branch_b", "w_out", "ffn2_norm", "ffn2_w1", "ffn2_w3", "ffn2_w2", "final_norm")
TRANSPOSED = ("ffn1_w1", "ffn1_w3", "ffn2_w1", "ffn2_w3")
SIBLING_LAG = 2
GATHER_GROUPS = (("ffn1_w1", "ffn1_w3"), ("ffn1_w2",), ("w_in",), ("w_branch_a", "w_branch_b", "w_out"),
                 ("ffn2_w1", "ffn2_w3", "ffn2_w2"))


def _pack_small(d):
    rows = []
    for n in SMALL:
        flat = d[n].reshape(-1)
        pad = (-flat.shape[0]) % LANES
        rows.append(jnp.pad(flat, (0, pad)).reshape(-1, LANES))
    buf = jnp.concatenate(rows, axis=0)
    return jnp.pad(buf, ((0, (-buf.shape[0]) % 8), (0, 0)))


def _unpack_small(buf, like):
    out, r = {}, 0
    for n in SMALL:
        size = like[n].size
        nr = -(-size // LANES)
        out[n] = buf[r:r + nr].reshape(-1)[:size].reshape(like[n].shape)
        r += nr
    return out


def kernel(x, ffn1_norm, ffn1_w1, ffn1_w3, ffn1_w2, mix_norm, w_in, b_gate, q_norm, k_norm, rel_bias, w_branch_a, w_branch_b, w_out, ffn2_norm, ffn2_w1, ffn2_w3, ffn2_w2, final_norm, loss_target, m_ffn1_norm, m_ffn1_w1, m_ffn1_w3, m_ffn1_w2, m_mix_norm, m_w_in, m_b_gate, m_q_norm, m_k_norm, m_rel_bias, m_w_branch_a, m_w_branch_b, m_w_out, m_ffn2_norm, m_ffn2_w1, m_ffn2_w3, m_ffn2_w2, m_final_norm, v_ffn1_norm, v_ffn1_w1, v_ffn1_w3, v_ffn1_w2, v_mix_norm, v_w_in, v_b_gate, v_q_norm, v_k_norm, v_rel_bias, v_w_branch_a, v_w_branch_b, v_w_out, v_ffn2_norm, v_ffn2_w1, v_ffn2_w3, v_ffn2_w2, v_final_norm):
    given = dict(locals())
    w = {n: given[n] for n in ORDER}
    m = {n: given["m_" + n] for n in ORDER}
    v = {n: given["v_" + n] for n in ORDER}
    T, D = x.shape[1], x.shape[2]

    def stored(a, n):
        a = a.reshape(a.shape[1:])
        return a.T if n in TRANSPOSED else a

    def returned(a, n):
        return (a.T if n in TRANSPOSED else a).reshape(w[n].shape)

    quarter = {n: stored(w[n], n) for n in BIG}
    send, recv, _, land_thru, token = _exchange_start(
        "gather_start", None, _own_slots("own_weights", [quarter[n] for n in BIG]), "gather")
    index = {n: i for i, n in enumerate(BIG)}
    ready, filling = {}, {}

    def landed_halves(group, after):
        ids = [index[n] for n in group]
        return _exchange_wait("gather_wait_" + group[0], None, [land_thru[i] for i in ids],
                              [send[i] for i in ids], [recv[i] for i in ids], after, "gather")[1]

    def prefetch_w(name, after):
        group = next(g for g in GATHER_GROUPS if name in g)
        started = _exchange_start("fill_start_" + group[0], None, landed_halves(group, after), "fill")
        filling[group] = started
        return [started[4]]

    def get_w(name, after):
        if name not in ready:
            group = next(g for g in GATHER_GROUPS if name in g)
            if group in filling:
                f_send, f_recv, _, thru, _ = filling[group]
                stacks = _exchange_wait("fill_wait_" + group[0], None, thru, f_send, f_recv, after, "fill")[1]
            else:
                stacks = _fill_from_sibling("gather_fill_" + group[0], landed_halves(group, after))
            for n, st in zip(group, stacks):
                ready[n] = st.reshape(D, D) if n in ("w_branch_b", "w_out") else st
        return ready[name]

    scattered, forwarded = [], []

    def forward_oldest(after):
        names, s_sem, r_sem, srcs, lands = scattered.pop(0)
        _, landed = _exchange_wait("scatter_wait_" + names[0], srcs, lands, s_sem, r_sem, after, "scatter")
        started = _exchange_start("sibling_start_" + names[0], landed, [lax.empty(a.shape, a.dtype) for a in landed], "sibling")
        forwarded.append((names,) + tuple(started[:4]))
        return started[4]

    def put_g(grads):
        names = list(grads)
        stacks = [grads[n].reshape((4,) + quarter[n].shape) for n in names]
        lands = _own_slots("own_grad_" + names[0], stacks, from_stack=True)
        started = _exchange_start("scatter_start_" + names[0], stacks, lands, "scatter")
        scattered.append((names,) + tuple(started[:4]))
        tokens = [started[4]]
        if len(scattered) > SIBLING_LAG:
            tokens.append(forward_oldest(started[4]))
        return tokens

    small = {n: w[n] for n in SMALL}
    packed = [_pack_small({n: d[n] for n in SMALL}) for d in (w, m, v)]
    loss_cols, grad_x, gs = _local_step(x.reshape(T, D), loss_target.reshape(T, D), small, get_w, put_g, deps=[token] + packed,
                                        prefetch_w=prefetch_w)

    after = grad_x
    while scattered:
        after = forward_oldest(after)
    grads, deltas, new_m, new_v = {}, {}, {}, {}
    for names, s_sem, r_sem, srcs, lands in forwarded:
        mine, theirs = _exchange_wait("sibling_wait_" + names[0], srcs, lands, s_sem, r_sem, after, "sibling")
        for n, a, b in zip(names, mine, theirs):
            res = _adamw_big(f"adamw_{n}", quarter[n], stored(m[n], n), stored(v[n], n), a, b)
            grads[n], deltas[n], new_m[n], new_v[n] = [returned(r, n) for r in res]

    gs = {n: gs[n].reshape(w[n].shape) for n in SMALL}
    packed_g = _pack_small(gs)
    n_small = packed_g.shape[0]
    summed = _allreduce_small(jnp.concatenate([packed_g, loss_cols.reshape(-1, LANES)], axis=0))
    g_small, loss = summed[:n_small], jnp.sum(summed[n_small:])
    R = g_small.shape[0]
    res = _ew("adamw_small", lambda wv, mv, vv, g: (g,) + _adamw_math(wv, g, mv, vv),
              [_tiled(packed[0]), _tiled(packed[1]), _tiled(packed[2]), _tiled(g_small)], [(F32, LANES)] * 4, n_rows=R, rows=R)
    for d, buf in zip((grads, deltas, new_m, new_v), res):
        d.update(_unpack_small(buf, w))

    return (loss, grad_x.reshape(x.shape), *[grads[n] for n in ORDER], *[deltas[n] for n in ORDER],
            *[new_m[n] for n in ORDER], *[new_v[n] for n in ORDER])
```

```python
import functools
import math

import numpy as np
import jax
import jax.numpy as jnp
from jax import lax
from jax.experimental import pallas as pl
from jax.experimental.pallas import tpu as pltpu

F32 = jnp.float32
BF16 = jnp.bfloat16
MESH = pl.DeviceIdType.MESH

NEG_INF = -1e30
EPS = 1e-6
GRID_W = 64
ROPE_THETA = 10000.0
DILATIONS = (1, 4, 16)
BAND_HALF = 64
HEAD_A = 64
HEADS_A = 8
WIDTH_A = HEADS_A * HEAD_A
HEAD_B = 128
LOG2_E = math.log2(math.e)
QK_SCALE_LOG2 = HEAD_B ** -0.5 * LOG2_E
N_BUCKETS = 32
MAX_DISTANCE = 1024
ADAM_LR, ADAM_B1, ADAM_B2, ADAM_EPS, ADAM_WD, ADAM_STEP = 0.001, 0.9, 0.999, 1e-08, 0.01, 10

B_Q, B_K, B_V = 4608, 5632, 5888
G_A, G_B = 6144, 7168
IN_WIDTH = 8192

VMEM_LIMIT_BYTES = 56 * 1024 * 1024
QB_A = 128
QB_B = 256


def _params(*sem):
    return pltpu.CompilerParams(dimension_semantics=sem, vmem_limit_bytes=VMEM_LIMIT_BYTES)


def _bs(shape, fn):
    return pl.BlockSpec(shape, fn)


def _resident(shape, fn):
    return pl.BlockSpec(shape, fn, pipeline_mode=pl.Buffered(1))


def _mm(name, grid, pairs, out_shape, out_spec, dims, *, extras=(), epilogue=None, deps=(), reds=(), rider=None):
    n_pairs, n_extra, n_deps = len(pairs), len(extras), len(deps)
    operands = [p[0] for p in pairs] + [p[2] for p in pairs] + [e[0] for e in extras] + list(deps)
    in_specs = [p[1] for p in pairs] + [p[3] for p in pairs] + [e[1] for e in extras] + _any_specs(n_deps)
    single = not isinstance(out_shape, (list, tuple))
    out_shapes = [out_shape] if single else list(out_shape)
    out_specs = [out_spec] if single else list(out_spec)
    n_out = len(out_shapes)
    out_shapes += [jax.ShapeDtypeStruct((1, w), F32) for w in reds]
    out_specs += [_bs((1, w), lambda *_: (0, 0)) for w in reds]
    n_rin = 0
    if rider is not None:
        assert rider["n_blocks"] <= grid[0]
        n_rin = len(rider["operands"])
        operands += list(rider["operands"])
        in_specs += list(rider["in_specs"])
        out_shapes += list(rider["out_shape"])
        out_specs += list(rider["out_specs"])

    def body(*refs):
        a_refs, b_refs = refs[:n_pairs], refs[n_pairs:2 * n_pairs]
        e_refs = refs[2 * n_pairs:2 * n_pairs + n_extra]
        o_refs = refs[2 * n_pairs + n_extra + n_deps + n_rin:]
        if rider is not None:
            r_in = refs[2 * n_pairs + n_extra + n_deps:2 * n_pairs + n_extra + n_deps + n_rin]
            r_out = o_refs[n_out + len(reds):]

            @pl.when(pl.program_id(0) < rider["n_blocks"])
            def _():
                for ref, val in zip(r_out, rider["fn"](*[r[...] for r in r_in])):
                    ref[...] = val.astype(ref.dtype)
        acc = None
        for a_ref, b_ref in zip(a_refs, b_refs):
            t = lax.dot_general(a_ref[...], b_ref[...], (dims, ((), ())), preferred_element_type=F32)
            acc = t if acc is None else acc + t
        vals = acc if epilogue is None else epilogue(acc, *[e[...] for e in e_refs])
        if not isinstance(vals, (list, tuple)):
            vals = (vals,)
        for o_ref, v in zip(o_refs[:n_out], vals[:n_out]):
            o_ref[...] = v.astype(o_ref.dtype)
        if reds:
            first = functools.reduce(jnp.logical_and, [pl.program_id(ax) == 0 for ax in range(len(grid))])
            for r_ref, v in zip(o_refs[n_out:], vals[n_out:]):
                @pl.when(first)
                def _(r_ref=r_ref):
                    r_ref[...] = jnp.zeros_like(r_ref)
                r_ref[...] += v

    sem = ["arbitrary" if (reds or rider is not None) else "parallel"] * len(grid)
    res = pl.pallas_call(
        body, out_shape=out_shapes, grid=grid, in_specs=in_specs, out_specs=out_specs,
        compiler_params=_params(*sem), name=name)(*operands)
    if rider is not None:
        rider["deliver"](res[n_out + len(reds):])
        res = res[:n_out + len(reds)]
    return res[0] if (single and not reds) else res


NN = ((1,), (0,))
NT = ((1,), (1,))
TN = ((0,), (0,))


def _mm_wgrad(name, a, b, *, a_cols, b_cols, tm, tn, J, deps=(), rider=None):
    def pick(arr, cols, t):
        if arr.ndim == 3:
            T, c = arr.shape[1], arr.shape[2]
            t = min(t, c)
            return T, c, t, (lambda sel: _bs((None, T, t), lambda j, i, k: (j, 0, sel(i, k))))
        T = arr.shape[0]
        c = arr.shape[1] if cols is None else cols
        t = min(t, c)
        per = c // t
        if cols is None:
            if per == 1:
                return T, c, t, (lambda sel: _resident((T, t), lambda j, i, k: (0, 0)))
            return T, c, t, (lambda sel: _bs((T, t), lambda j, i, k: (0, sel(i, k))))
        return T, c, t, (lambda sel: _bs((T, t), lambda j, i, k: (0, j * per + sel(i, k))))
    _, ca, tm, mk_a = pick(a, a_cols, tm)
    _, cb, tn, mk_b = pick(b, b_cols, tn)
    return _mm(name, (J, ca // tm, cb // tn),
               [(a, mk_a(lambda i, k: i), b, mk_b(lambda i, k: k))],
               jax.ShapeDtypeStruct((J, ca, cb), BF16), _bs((None, tm, tn), lambda j, i, k: (j, i, k)), TN, deps=deps, rider=rider)


def _tiled(arr, width=None, col=0, rowblk=0):
    return ("t", arr, arr.shape[1] if width is None else width, col, rowblk)


def _table(arr):
    return ("f", arr)


def _whole(arr):
    return ("w", arr)


def _ew(name, fn, ins, outs, *, n_rows, rows, reds=(), ncols=1, deps=()):
    nrb = n_rows // rows
    n_deps = len(deps)
    operands, in_specs = [], []
    for spec in ins:
        if spec[0] == "t":
            _, arr, width, col, rowblk = spec
            step = 1 if ncols > 1 else 0
            in_specs.append(_bs((rows, width), lambda c, i, col=col, rowblk=rowblk, step=step: (rowblk + i, col + c * step)))
        elif spec[0] == "f":
            arr = spec[1]
            in_specs.append(_bs((rows, arr.shape[1]), lambda c, i: (i, 0)))
        else:
            arr = spec[1]
            nd = arr.ndim
            if nd == 3:
                in_specs.append(_bs((None,) + arr.shape[1:], lambda c, i: (c, 0, 0)))
            else:
                in_specs.append(_bs(arr.shape, lambda c, i, nd=nd: (0,) * nd))
        operands.append(arr)
    out_shapes = [jax.ShapeDtypeStruct((n_rows, ncols * w), dt) for dt, w in outs]
    out_specs = [_bs((rows, w), lambda c, i: (i, c)) for _, w in outs]
    out_shapes += [jax.ShapeDtypeStruct((ncols, 1, w), F32) for w in reds]
    out_specs += [_bs((None, 1, w), lambda c, i: (c, 0, 0)) for w in reds]
    n_in, n_out, n_red = len(ins), len(outs), len(reds)
    operands += list(deps)
    in_specs += _any_specs(n_deps)

    def body(*refs):
        vals = fn(*[r[...] for r in refs[:n_in]])
        if not isinstance(vals, (tuple, list)):
            vals = (vals,)
        o_refs = refs[n_in + n_deps:]
        for o_ref, v in zip(o_refs[:n_out], vals[:n_out]):
            o_ref[...] = v.astype(o_ref.dtype)
        if n_red:
            i = pl.program_id(1)
            for r_ref, v in zip(o_refs[n_out:], vals[n_out:]):
                @pl.when(i == 0)
                def _(r_ref=r_ref):
                    r_ref[...] = jnp.zeros_like(r_ref)
                r_ref[...] += v

    res = pl.pallas_call(
        body, out_shape=out_shapes, grid=(ncols, nrb), in_specs=in_specs, out_specs=out_specs,
        compiler_params=_params("parallel", "arbitrary" if n_red else "parallel"), name=name)(*operands)
    return res


def _colsum(v):
    return jnp.sum(v, axis=0, keepdims=True)


def _rstd(x):
    return lax.rsqrt(jnp.mean(x * x, axis=-1, keepdims=True) + EPS)


def _sigmoid(x):
    return 1.0 / (1.0 + jnp.exp(-x))


def _norm_fwd(x, g):
    return x * _rstd(x) * g


def _norm_bwd(x, g, dy):
    r = _rstd(x)
    xh = x * r
    dxh = dy * g
    dx = r * (dxh - xh * jnp.mean(dxh * xh, axis=-1, keepdims=True))
    return dx, dy * xh


def _row_spec(arr, rows):
    if arr.shape[0] == 1:
        return _bs(arr.shape, lambda i: (0, 0))
    return _bs((rows, arr.shape[1]), lambda i: (i, 0))


def _ffn_fwd(tag, x, gain, get_w, deps=(), *, h=None, tail_ins=(), tail_fn=None, tail_outs=(F32,), tail_reds=()):
    T, D = x.shape
    if h is None:
        (h,) = _ew(f"{tag}_norm", lambda xv, g: _norm_fwd(xv, g), [_tiled(x), _whole(gain)], [(BF16, D)], n_rows=T, rows=512,
                   deps=deps)
    w1, w3 = get_w(f"{tag}_w1", h), get_w(f"{tag}_w3", h)
    J, f, _ = w1.shape
    tm = 1024

    def up(h_ref, w1_ref, w3_ref, u_ref, g_ref, a_ref):
        hv = h_ref[...]
        u = lax.dot_general(hv, w1_ref[...], (NT, ((), ())), preferred_element_type=F32)
        g = lax.dot_general(hv, w3_ref[...], (NT, ((), ())), preferred_element_type=F32)
        u_ref[...] = u.astype(BF16)
        g_ref[...] = g.astype(BF16)
        a_ref[...] = (u * _sigmoid(u) * g).astype(BF16)

    slab = _bs((None, tm, f), lambda j, i: (j, i, 0))
    w_spec = _bs((None, f, D), lambda j, i: (j, 0, 0))
    u, g, a = pl.pallas_call(
        up, out_shape=[jax.ShapeDtypeStruct((J, T, f), BF16)] * 3, grid=(J, T // tm),
        in_specs=[_bs((tm, D), lambda j, i: (i, 0)), w_spec, w_spec], out_specs=[slab] * 3,
        compiler_params=_params("parallel", "parallel"), name=f"{tag}_up")(h, w1, w3)
    w2 = get_w(f"{tag}_w2", a)
    def tail(acc, xv, *rest):
        y = xv + 0.5 * acc
        return y if tail_fn is None else tail_fn(y, *rest)

    row = _bs((512, D), lambda i: (i, 0))
    res = _mm(f"{tag}_down", (T // 512,),
              [(a, _bs((None, 512, f), lambda i, j=j: (j, i, 0)), w2, _resident((None, f, D), lambda i, j=j: (j, 0, 0)))
               for j in range(J)],
              [jax.ShapeDtypeStruct((T, D), dt) for dt in tail_outs], [row] * len(tail_outs), NN,
              extras=[(x, row)] + [(t, _row_spec(t, 512)) for t in tail_ins], epilogue=tail, reds=tail_reds)
    return res, (h, u, g, a)


def _dh_norm_bwd(name, rows, pairs, dims, x, gain, dres, deps, also_bf16=False, rider=None):
    T, D = x.shape

    def epilogue(dh, xv, gv, dr):
        dx, dgr = _norm_bwd(xv, gv, dh)
        dx = dx + dr
        return (dx, 0.5 * dx) + ((dx,) if also_bf16 else ()) + (_colsum(dgr),)

    dts = [F32, BF16] + ([BF16] if also_bf16 else [])
    row = _bs((rows, D), lambda i: (i, 0))
    return _mm(name, (T // rows,), pairs, [jax.ShapeDtypeStruct((T, D), dt) for dt in dts], [row] * len(dts), dims,
               extras=[(x, row), (gain, _row_spec(gain, rows)), (dres, row)], epilogue=epilogue, deps=deps, reds=(D,), rider=rider)


def _ffn_bwd(tag, x, gain, get_w, put_g, saved, dy, dy_half, also_bf16=False, last=False, take_rider=lambda steps, after: None):
    h, u, g, a = saved
    T, D = x.shape
    w1, w3, w2 = [get_w(f"{tag}_{n}", dy_half) for n in ("w1", "w3", "w2")]
    J, f, _ = w1.shape
    dw2 = _mm_wgrad(f"{tag}_bwd_dw2", a, dy_half, a_cols=None, b_cols=None, tm=f, tn=D, J=J, rider=take_rider(J, dy_half))
    deps = put_g({f"{tag}_w2": dw2}) if last else []
    tm = 1024

    def up_bwd(dy_ref, w2_ref, u_ref, g_ref, *rest):
        du_ref, dg_ref = rest[-2:]
        da = lax.dot_general(dy_ref[...], w2_ref[...], (NT, ((), ())), preferred_element_type=F32)
        uv, gv = u_ref[...].astype(F32), g_ref[...].astype(F32)
        s = _sigmoid(uv)
        du_ref[...] = (da * gv * (s * (1.0 + uv * (1.0 - s)))).astype(BF16)
        dg_ref[...] = (da * (uv * s)).astype(BF16)

    slab = _bs((None, tm, f), lambda j, i: (j, i, 0))
    du, dg = pl.pallas_call(
        up_bwd, out_shape=[jax.ShapeDtypeStruct((J, T, f), BF16)] * 2, grid=(J, T // tm),
        in_specs=[_bs((tm, D), lambda j, i: (i, 0)), _bs((None, f, D), lambda j, i: (j, 0, 0)), slab, slab] + _any_specs(len(deps)),
        out_specs=[slab] * 2, compiler_params=_params("parallel", "parallel"), name=f"{tag}_bwd_up")(dy_half, w2, u, g, *deps)
    dw1 = _mm_wgrad(f"{tag}_bwd_dw1", du, h, a_cols=None, b_cols=None, tm=f, tn=D, J=J, rider=take_rider(J, du))
    deps = put_g({f"{tag}_w1": dw1}) if last else []
    dw3 = _mm_wgrad(f"{tag}_bwd_dw3", dg, h, a_cols=None, b_cols=None, tm=f, tn=D, J=J, deps=deps, rider=take_rider(J, dw1))
    deps = put_g({f"{tag}_w3": dw3} if last else {f"{tag}_w2": dw2, f"{tag}_w1": dw1, f"{tag}_w3": dw3})
    pairs = []
    for j in range(J):
        a_spec = _bs((None, 512, f), lambda i, j=j: (j, i, 0))
        w_spec = _resident((None, f, D), lambda i, j=j: (j, 0, 0))
        pairs += [(du, a_spec, w1, w_spec), (dg, a_spec, w3, w_spec)]
    return _dh_norm_bwd(f"{tag}_bwd_dh", 512, pairs, NN, x, gain, dy, deps, also_bf16, rider=take_rider(T // 512, dw3))


def _t5_bucket(rel):
    n = N_BUCKETS // 2
    max_exact = n // 2
    ret = jnp.where(rel > 0, n, 0)
    a = jnp.abs(rel)
    af = jnp.maximum(a, 1).astype(F32)
    large = max_exact + (jnp.log(af / max_exact) / math.log(MAX_DISTANCE / max_exact) * (n - max_exact)).astype(jnp.int32)
    large = jnp.minimum(large, n - 1)
    return ret + jnp.where(a < max_exact, a, large)


WIN_A = QB_A + 2 * BAND_HALF
WIN_SHIFTS = (0, BAND_HALF, 2 * BAND_HALF)


def _window_variant(n, nblk):
    return jnp.where(n == 0, 0, jnp.where(n == nblk - 1, 2, 1))


def _window_start(n, nblk):
    return pl.multiple_of(jnp.clip(n * QB_A - BAND_HALF, 0, nblk * QB_A - WIN_A), BAND_HALF)


def _band_steps(xp=jnp):
    qi = xp.arange(QB_A, dtype=xp.int32)[None, :, None]
    kj = xp.arange(WIN_A, dtype=xp.int32)[None, None, :]
    return kj - qi - xp.asarray(WIN_SHIFTS, dtype=xp.int32)[:, None, None]


def _bias_tiles(rel_bias):
    wide = QB_A + 2 * WIN_SHIFTS[-1]
    qi = jnp.arange(QB_A, dtype=jnp.int32)[:, None]
    steps = jnp.arange(wide, dtype=jnp.int32)[None, :] - WIN_SHIFTS[-1] - qi
    buckets = jnp.stack([_t5_bucket(steps * d) for d in DILATIONS])
    inband = (jnp.abs(steps) <= BAND_HALF).astype(jnp.int32)
    n_heads = rel_bias.shape[1]

    def body(tab_ref, b_ref, m_ref, o_ref):
        hd = pl.program_id(0)
        bkt = b_ref[...]
        acc = jnp.zeros(bkt.shape, F32)
        for b in range(N_BUCKETS):
            acc = jnp.where(bkt == b, tab_ref[b, hd], acc)
        o_ref[...] = jnp.where(m_ref[...] > 0, acc, NEG_INF)

    base = pl.pallas_call(
        body, out_shape=jax.ShapeDtypeStruct((n_heads, QB_A, wide), F32), grid=(n_heads,),
        in_specs=[pl.BlockSpec(memory_space=pltpu.SMEM),
                  _bs((None, QB_A, wide), lambda hd: (hd // HEADS_A, 0, 0)),
                  _bs((QB_A, wide), lambda hd: (0, 0))],
        out_specs=_bs((None, QB_A, wide), lambda hd: (hd, 0, 0)),
        compiler_params=_params("parallel"), name="a_bias_tiles")(rel_bias, buckets, inband)
    base = base.reshape(len(DILATIONS), HEADS_A, QB_A, wide)
    return jnp.stack([base[..., WIN_SHIFTS[-1] - s:WIN_SHIFTS[-1] - s + WIN_A] for s in WIN_SHIFTS], axis=1)


def _bias_grad(dbias):
    steps = _band_steps(np)
    inband = np.abs(steps) <= BAND_HALF
    present = []
    for d in DILATIONS:
        rel = steps * d
        a = np.abs(rel)
        large = 8 + (np.log(np.maximum(a, 1) / 8.0) / math.log(MAX_DISTANCE / 8.0) * 8).astype(np.int64)
        bk = np.where(rel > 0, 16, 0) + np.where(a < 8, a, np.minimum(large, 15))
        present.append([sorted(set(bk[v][inband[v]].tolist())) for v in range(3)])
    buckets = jnp.stack([_t5_bucket(_band_steps() * d) for d in DILATIONS])
    n_heads = len(DILATIONS) * HEADS_A

    def body(b_ref, d_ref, o_ref):
        row = lax.broadcasted_iota(jnp.int32, (N_BUCKETS, n_heads), 0)
        col = lax.broadcasted_iota(jnp.int32, (N_BUCKETS, n_heads), 1)
        out = jnp.zeros((N_BUCKETS, n_heads), F32)
        for grp in range(len(DILATIONS)):
            for hh in range(HEADS_A):
                hd = grp * HEADS_A + hh
                for b in sorted(set(sum(present[grp], []))):
                    tot = jnp.zeros((), F32)
                    for v in range(3):
                        if b in present[grp][v]:
                            tot = tot + jnp.sum(jnp.where(b_ref[grp, v] == b, d_ref[grp, v, hh], 0.0))
                    out = jnp.where((row == b) & (col == hd), tot, out)
        o_ref[...] = out

    return pl.pallas_call(
        body, out_shape=jax.ShapeDtypeStruct((N_BUCKETS, n_heads), F32),
        compiler_params=pltpu.CompilerParams(vmem_limit_bytes=VMEM_LIMIT_BYTES), name="a_bias_grad")(buckets, dbias)


def _lane_is_second_head(shape):
    return lax.broadcasted_iota(jnp.int32, shape, len(shape) - 1) >= HEAD_A


VIEW_ROWS = 512


def _view_chunks():
    return [pltpu.VMEM((VIEW_ROWS, LANES), F32)] * (WIDTH_A // LANES)


def _rows_to_view(x_ref, col, o_ref, ocol, d, chunks):
    n = VIEW_ROWS // d
    for c, scr in enumerate(chunks):
        scr[...] = x_ref[:, col + c * LANES:col + (c + 1) * LANES].astype(F32)
        for r in range(d):
            at = ocol + r * WIDTH_A + c * LANES
            o_ref[:, at:at + LANES] = scr[pl.ds(r, n, stride=d), :].astype(o_ref.dtype)


def _view_to_rows(v_ref, o_ref, col, d, chunks):
    n = VIEW_ROWS // d
    for c, scr in enumerate(chunks):
        if d == 1:
            o_ref[:, col + c * LANES:col + (c + 1) * LANES] = v_ref[:, c * LANES:(c + 1) * LANES].astype(o_ref.dtype)
            continue
        for r in range(d):
            scr[pl.ds(r, n, stride=d), :] = v_ref[:, r * WIDTH_A + c * LANES:r * WIDTH_A + (c + 1) * LANES].astype(F32)
        o_ref[:, col + c * LANES:col + (c + 1) * LANES] = scr[...].astype(o_ref.dtype)


def _group_view(proj, grp, d):
    T = proj.shape[0]
    if d == 1:
        return proj, (lambda part, r: grp * 3 + part)

    def body(x_ref, o_ref, *chunks):
        for part in range(3):
            _rows_to_view(x_ref, part * WIDTH_A, o_ref, part * d * WIDTH_A, d, chunks)

    view = pl.pallas_call(
        body, out_shape=jax.ShapeDtypeStruct((T // d, 3 * d * WIDTH_A), proj.dtype), grid=(T // VIEW_ROWS,),
        in_specs=[_bs((VIEW_ROWS, 3 * WIDTH_A), lambda i: (i, grp))],
        out_specs=_bs((VIEW_ROWS // d, 3 * d * WIDTH_A), lambda i: (i, 0)),
        scratch_shapes=_view_chunks(), compiler_params=_params("parallel"), name=f"a_view_d{d}")(proj)
    return view, (lambda part, r: part * d + r)


def _stack_heads(v2, second):
    zero = jnp.zeros_like(v2)
    return jnp.concatenate([jnp.where(second, zero, v2), jnp.where(second, v2, zero)], axis=0)


def _unstack_heads(v, second):
    return jnp.where(second, v[QB_A:], v[:QB_A])


def _dil_fwd(view, bias, d):
    pv, colblk = view
    L = pv.shape[0]
    nblk = L // QB_A
    W2 = 2 * HEAD_A
    scale = HEAD_A ** -0.5

    def body(q_ref, k_ref, v_ref, b_ref, o_ref, l_ref):
        win = pl.ds(_window_start(pl.program_id(1), nblk), WIN_A)
        second = _lane_is_second_head((QB_A, W2))
        pairs = range(HEADS_A // 2)
        cols = [slice(hp * W2, (hp + 1) * W2) for hp in pairs]
        s = [lax.dot_general(_stack_heads(q_ref[:, cols[hp]], second), k_ref[win, cols[hp]], (NT, ((), ())),
                             preferred_element_type=F32) * scale + b_ref[2 * hp:2 * hp + 2].reshape(2 * QB_A, WIN_A)
             for hp in pairs]
        m = [jnp.max(x, axis=-1, keepdims=True) for x in s]
        p = [jnp.exp(x - mx) for x, mx in zip(s, m)]
        l = [jnp.sum(x, axis=-1, keepdims=True) for x in p]
        res = [jnp.dot(p[hp].astype(BF16), v_ref[win, cols[hp]], preferred_element_type=F32) / l[hp] for hp in pairs]
        o_ref[...] = jnp.concatenate([_unstack_heads(x, second) for x in res], axis=1).astype(o_ref.dtype)
        l_ref[...] = jnp.concatenate([_unstack_heads(jnp.broadcast_to(mx + jnp.log(lx), (2 * QB_A, W2)), second)
                                      for mx, lx in zip(m, l)], axis=1)

    in_specs = [_bs((QB_A, WIDTH_A), lambda r, n: (n, colblk(0, r))),
                _bs((L, WIDTH_A), lambda r, n: (0, colblk(1, r))), _bs((L, WIDTH_A), lambda r, n: (0, colblk(2, r))),
                _bs((None, HEADS_A, QB_A, WIN_A), lambda r, n: (_window_variant(n, nblk), 0, 0, 0))]
    o, lse = pl.pallas_call(
        body, out_shape=[jax.ShapeDtypeStruct((L, d * WIDTH_A), BF16), jax.ShapeDtypeStruct((L, d * WIDTH_A), F32)],
        grid=(d, nblk), in_specs=in_specs,
        out_specs=[_bs((QB_A, WIDTH_A), lambda r, n: (n, r)), _bs((QB_A, WIDTH_A), lambda r, n: (n, r))],
        compiler_params=_params("parallel", "parallel"), name=f"a_fwd_d{d}")(pv, pv, pv, bias)
    return o, lse


def _dil_bwd(view_qkv, bias, do, lse, cterm, d):
    pv, colblk = view_qkv
    L = pv.shape[0]
    nblk = L // QB_A
    W2 = 2 * HEAD_A
    PPS = 4
    WS = PPS * W2
    ob = WIDTH_A // WS
    scale = HEAD_A ** -0.5

    def body(q_ref, k_ref, v_ref, do_ref, l_ref, c_ref, b_ref, dq_ref, dk_ref, dv_ref, db_ref):
        r, n = pl.program_id(1), pl.program_id(2)

        @pl.when(n == 0)
        def _():
            dk_ref[...] = jnp.zeros_like(dk_ref)
            dv_ref[...] = jnp.zeros_like(dv_ref)

        @pl.when((n == 0) & (r == 0))
        def _():
            db_ref[...] = jnp.zeros_like(db_ref)

        second = _lane_is_second_head((QB_A, W2))
        win = pl.ds(_window_start(n, nblk), WIN_A)
        variant = _window_variant(n, nblk)
        pairs = range(PPS)
        cols = [slice(pp * W2, (pp + 1) * W2) for pp in pairs]

        def head_rows(ref, pp):
            v2 = ref[:, cols[pp]]
            return jnp.concatenate([v2[:, 0:1], v2[:, HEAD_A:HEAD_A + 1]], axis=0)

        kw = [k_ref[win, c] for c in cols]
        vw = [v_ref[win, c] for c in cols]
        qs = [_stack_heads(q_ref[:, c], second) for c in cols]
        dos = [_stack_heads(do_ref[:, c], second) for c in cols]
        s = [lax.dot_general(qs[pp], kw[pp], (NT, ((), ())), preferred_element_type=F32) for pp in pairs]
        dp = [lax.dot_general(dos[pp], vw[pp], (NT, ((), ())), preferred_element_type=F32) for pp in pairs]
        p = [jnp.exp(s[pp] * scale + b_ref[2 * pp:2 * pp + 2].reshape(2 * QB_A, WIN_A) - head_rows(l_ref, pp)) for pp in pairs]
        ds = [p[pp] * (dp[pp] + head_rows(c_ref, pp)) for pp in pairs]
        db_ref[variant] += jnp.concatenate([x.reshape(2, QB_A, WIN_A) for x in ds], axis=0)
        pb = [x.astype(BF16) for x in p]
        dsb = [(x * scale).astype(BF16) for x in ds]
        dq_ref[...] = jnp.concatenate([_unstack_heads(jnp.dot(dsb[pp], kw[pp], preferred_element_type=F32), second)
                                       for pp in pairs], axis=1).astype(dq_ref.dtype)
        dk_ref[win, :] += jnp.concatenate([lax.dot_general(dsb[pp], qs[pp], (TN, ((), ())), preferred_element_type=F32)
                                           for pp in pairs], axis=1)
        dv_ref[win, :] += jnp.concatenate([lax.dot_general(pb[pp], dos[pp], (TN, ((), ())), preferred_element_type=F32)
                                           for pp in pairs], axis=1)

    kv_spec = _resident if d == 1 else _bs
    in_specs = [_bs((QB_A, WS), lambda hp, r, n: (n, colblk(0, r) * ob + hp)),
                kv_spec((L, WS), lambda hp, r, n: (0, colblk(1, r) * ob + hp)),
                kv_spec((L, WS), lambda hp, r, n: (0, colblk(2, r) * ob + hp))]
    in_specs += [_bs((QB_A, WS), lambda hp, r, n: (n, r * ob + hp))] * 3
    in_specs += [_bs((None, 2 * PPS, QB_A, WIN_A), lambda hp, r, n: (_window_variant(n, nblk), hp, 0, 0))]
    out_shape = [jax.ShapeDtypeStruct((L, d * WIDTH_A), BF16), jax.ShapeDtypeStruct((L, d * WIDTH_A), F32),
                 jax.ShapeDtypeStruct((L, d * WIDTH_A), F32), jax.ShapeDtypeStruct((3, HEADS_A, QB_A, WIN_A), F32)]
    out_specs = [_bs((QB_A, WS), lambda hp, r, n: (n, r * ob + hp)),
                 _bs((L, WS), lambda hp, r, n: (0, r * ob + hp)), _bs((L, WS), lambda hp, r, n: (0, r * ob + hp)),
                 _bs((3, 2 * PPS, QB_A, WIN_A), lambda hp, r, n: (0, hp, 0, 0))]
    dq, dk, dv, db = pl.pallas_call(
        body, out_shape=out_shape, grid=(ob, d, nblk), in_specs=in_specs, out_specs=out_specs,
        compiler_params=_params("arbitrary", "arbitrary", "arbitrary"), name=f"a_bwd_d{d}")(
            pv, pv, pv, do, lse, cterm, bias)
    return dq, dk, dv, db


def _assemble_dproj(a_parts, dq_b, dk_b, dv_b, dga, dgb):
    T = dq_b.shape[0]
    flat = [(a_parts[part][g], d) for part in range(3) for g, d in enumerate(DILATIONS)]
    rest = [dq_b, dk_b, dv_b, dga, dgb]

    def body(*refs):
        views, others = refs[:len(flat)], refs[len(flat):len(flat) + len(rest)]
        o_ref, chunks = refs[len(flat) + len(rest)], refs[len(flat) + len(rest) + 1:]
        col = 0
        for v_ref, (_, d) in zip(views, flat):
            _view_to_rows(v_ref, o_ref, col, d, chunks)
            col += WIDTH_A
        for x_ref in others:
            w = x_ref.shape[1]
            o_ref[:, col:col + w] = x_ref[...].astype(o_ref.dtype)
            col += w

    in_specs = [_bs((VIEW_ROWS // d, d * WIDTH_A), lambda i: (i, 0)) for _, d in flat]
    in_specs += [_bs((VIEW_ROWS, x.shape[1]), lambda i: (i, 0)) for x in rest]
    return pl.pallas_call(
        body, out_shape=jax.ShapeDtypeStruct((T, IN_WIDTH), BF16), grid=(T // VIEW_ROWS,), in_specs=in_specs,
        out_specs=_bs((VIEW_ROWS, IN_WIDTH), lambda i: (i, 0)), scratch_shapes=_view_chunks(),
        compiler_params=_params("parallel"), name="mix_bwd_dproj")(*[a for a, _ in flat], *rest)


def _segment_ones():
    i = np.arange(WIDTH_A)
    return jnp.asarray((i[:, None] // HEAD_A == i[None, :] // HEAD_A).astype(np.float32), dtype=BF16)


def _group_weights(l0, l1, l2):
    m = jnp.maximum(jnp.maximum(l0, l1), l2)
    e = [jnp.exp(l - m) for l in (l0, l1, l2)]
    z = e[0] + e[1] + e[2]
    return [ei / z for ei in e]


def _view_specs():
    return [_bs((VIEW_ROWS // d, d * WIDTH_A), lambda i: (i, 0)) for d in DILATIONS]


def _stage_tiles(n):
    return [pltpu.VMEM((VIEW_ROWS, WIDTH_A), F32)] * n


def _combine_fwd(outs, lses):
    T = outs[0].shape[0] * DILATIONS[0]
    n = len(DILATIONS)

    def body(*refs):
        o_refs, l_refs, oa_ref = refs[:n], refs[n:2 * n], refs[2 * n]
        o_st, l_st, chunks = refs[2 * n + 1:3 * n + 1], refs[3 * n + 1:4 * n + 1], refs[4 * n + 1:]
        for g, d in enumerate(DILATIONS):
            _view_to_rows(o_refs[g], o_st[g], 0, d, chunks)
            _view_to_rows(l_refs[g], l_st[g], 0, d, chunks)
        w = _group_weights(*[l[...] for l in l_st])
        oa_ref[...] = (w[0] * o_st[0][...] + w[1] * o_st[1][...] + w[2] * o_st[2][...]).astype(oa_ref.dtype)

    return pl.pallas_call(
        body, out_shape=jax.ShapeDtypeStruct((T, WIDTH_A), BF16), grid=(T // VIEW_ROWS,),
        in_specs=_view_specs() * 2, out_specs=_bs((VIEW_ROWS, WIDTH_A), lambda i: (i, 0)),
        scratch_shapes=_stage_tiles(2 * n) + _view_chunks(), compiler_params=_params("parallel"), name="a_combine")(*outs, *lses)


def _combine_bwd(doa, outs, lses):
    T = doa.shape[0]
    n = len(DILATIONS)

    def body(*refs):
        d_ref, o_refs, l_refs, seg_ref = refs[0], refs[1:n + 1], refs[n + 1:2 * n + 1], refs[2 * n + 1]
        do_refs, c_refs = refs[2 * n + 2:3 * n + 2], refs[3 * n + 2:4 * n + 2]
        o_st, l_st = refs[4 * n + 2:5 * n + 2], refs[5 * n + 2:6 * n + 2]
        tmp, chunks = refs[6 * n + 2], refs[6 * n + 3:]
        for g, d in enumerate(DILATIONS):
            _view_to_rows(o_refs[g], o_st[g], 0, d, chunks)
            _view_to_rows(l_refs[g], l_st[g], 0, d, chunks)
        dv = d_ref[...].astype(F32)
        w = _group_weights(*[l[...] for l in l_st])
        seg = seg_ref[...]
        tot = jnp.zeros(dv.shape, F32)
        for g in range(n):
            prod = w[g] * dv * o_st[g][...]
            hi = prod.astype(BF16)
            lo = (prod - hi.astype(F32)).astype(BF16)
            tot = tot + jnp.dot(hi, seg, preferred_element_type=F32) + jnp.dot(lo, seg, preferred_element_type=F32)
        for g, d in enumerate(DILATIONS):
            tmp[...] = w[g] * dv
            _rows_to_view(tmp, 0, do_refs[g], 0, d, chunks)
            tmp[...] = -w[g] * tot
            _rows_to_view(tmp, 0, c_refs[g], 0, d, chunks)

    views = [jax.ShapeDtypeStruct((T // d, d * WIDTH_A), dt) for dt in (BF16, F32) for d in DILATIONS]
    res = pl.pallas_call(
        body, out_shape=views, grid=(T // VIEW_ROWS,),
        in_specs=[_bs((VIEW_ROWS, WIDTH_A), lambda i: (i, 0))] + _view_specs() * 2 + [_bs((WIDTH_A, WIDTH_A), lambda i: (0, 0))],
        out_specs=_view_specs() * 2, scratch_shapes=_stage_tiles(2 * n + 1) + _view_chunks(),
        compiler_params=_params("parallel"), name="a_combine_bwd")(doa, *outs, *lses, _segment_ones())
    return res[:n], res[n:]


def _rope_tables(T):
    rows = T // GRID_W
    row = jnp.repeat(jnp.arange(rows, dtype=F32), GRID_W)
    col = jnp.tile(jnp.arange(GRID_W, dtype=F32), rows)
    n_freq = HEAD_B // 4
    freq = ROPE_THETA ** (-jnp.arange(n_freq, dtype=F32) / n_freq)
    ang = jnp.concatenate([row[:, None] * freq, col[:, None] * freq], axis=-1)
    cos, sin = jnp.repeat(jnp.cos(ang), 2, axis=1), jnp.repeat(jnp.sin(ang), 2, axis=1)
    sign = jnp.where(jnp.arange(HEAD_B) % 2 == 0, -1.0, 1.0).astype(F32)
    return cos, sin * sign


def _swap_pairs(v):
    even = lax.broadcasted_iota(jnp.int32, v.shape, v.ndim - 1) % 2 == 0
    n = v.shape[-1]
    return jnp.where(even, pltpu.roll(v, n - 1, v.ndim - 1), pltpu.roll(v, 1, v.ndim - 1))


def _qk_fwd(name, proj, col0, n_heads, gain, cos, sin, out_scale=1.0, deps=()):
    T = proj.shape[0]

    def fn(xr, g, c, s):
        xn = _norm_fwd(xr.astype(F32), g)
        return (xn * c + _swap_pairs(xn) * s) * out_scale

    (out,) = _ew(name, fn, [_tiled(proj, HEAD_B, col0 // HEAD_B), _whole(gain), _table(cos), _table(sin)],
                 [(BF16, HEAD_B)], n_rows=T, rows=2048, ncols=n_heads, deps=deps)
    return out


def _qk_bwd(name, dout, proj, col0, n_heads, gain, cos, sin, in_scale=1.0):
    T = proj.shape[0]

    def fn(dv, xr, g, c, s):
        dv = dv.astype(F32) * in_scale
        dxn = c * dv + _swap_pairs(s * dv)
        dx, dgr = _norm_bwd(xr.astype(F32), g, dxn)
        return dx, _colsum(dgr)

    dx, dg = _ew(name, fn, [_tiled(dout, HEAD_B, 0), _tiled(proj, HEAD_B, col0 // HEAD_B), _whole(gain),
                            _table(cos), _table(sin)],
                 [(BF16, HEAD_B)], n_rows=T, rows=2048, reds=(HEAD_B,), ncols=n_heads)
    return dx, jnp.sum(dg, axis=0)


def _gqa_fwd(qn, kn, proj, k_col=0):
    T = qn.shape[0]
    GW = 4 * HEAD_B
    QB = QB_B

    def body(q_ref, k_ref, v_ref, o_ref, l_ref):
        k, v = k_ref[...], v_ref[...]
        lane = lax.broadcasted_iota(jnp.int32, (QB, HEAD_B), 1)
        heads = range(4)
        s = [lax.dot_general(q_ref[:, g * HEAD_B:(g + 1) * HEAD_B], k, (NT, ((), ())), preferred_element_type=F32)
             for g in heads]
        m = [jnp.max(x, axis=-1, keepdims=True) for x in s]
        p = [jnp.exp2(x - mx) for x, mx in zip(s, m)]
        l = [jnp.sum(x, axis=-1, keepdims=True) for x in p]
        o = [jnp.dot(p[g].astype(BF16), v, preferred_element_type=F32) / l[g] for g in heads]
        o_ref[...] = jnp.concatenate(o, axis=1).astype(o_ref.dtype)
        lse_all = jnp.zeros((QB, HEAD_B), F32)
        for g in heads:
            lse_all = jnp.where(lane == g, m[g] + jnp.log2(l[g]), lse_all)
        l_ref[...] = lse_all

    return pl.pallas_call(
        body, out_shape=[jax.ShapeDtypeStruct((T, 2 * GW), BF16), jax.ShapeDtypeStruct((2, T, HEAD_B), F32)],
        grid=(2, T // QB),
        in_specs=[_bs((QB, GW), lambda kv, i: (i, kv)), _bs((T, HEAD_B), lambda kv, i: (0, k_col + kv)),
                  _bs((T, HEAD_B), lambda kv, i: (0, B_V // HEAD_B + kv))],
        out_specs=[_bs((QB, GW), lambda kv, i: (i, kv)), _bs((None, QB, HEAD_B), lambda kv, i: (kv, i, 0))],
        compiler_params=_params("parallel", "parallel"), name="b_fwd")(qn, kn, proj)


def _gqa_bwd(qn, kn, proj, o, lse, do, deps=(), k_col=0):
    T = qn.shape[0]
    GW = 4 * HEAD_B

    def body(q_ref, k_ref, v_ref, o_ref, l_ref, do_ref, *rest):
        dq_ref, dk_ref, dv_ref = rest[-3:]
        i = pl.program_id(1)

        @pl.when(i == 0)
        def _():
            dk_ref[...] = jnp.zeros_like(dk_ref)
            dv_ref[...] = jnp.zeros_like(dv_ref)

        k, v = k_ref[...], v_ref[...]
        lse_all = l_ref[...]
        for g in range(4):
            cols = slice(g * HEAD_B, (g + 1) * HEAD_B)
            q, dob = q_ref[:, cols], do_ref[:, cols]
            delta = jnp.sum(dob.astype(F32) * o_ref[:, cols].astype(F32), axis=-1, keepdims=True)
            s = lax.dot_general(q, k, (NT, ((), ())), preferred_element_type=F32)
            p = jnp.exp2(s - lse_all[:, g:g + 1])
            dp = lax.dot_general(dob, v, (NT, ((), ())), preferred_element_type=F32)
            ds = (p * (dp - delta)).astype(BF16)
            dq_ref[:, cols] = jnp.dot(ds, k, preferred_element_type=F32).astype(dq_ref.dtype)
            dk_ref[...] += lax.dot_general(ds, q, (TN, ((), ())), preferred_element_type=F32)
            dv_ref[...] += lax.dot_general(p.astype(BF16), dob, (TN, ((), ())), preferred_element_type=F32)

    return pl.pallas_call(
        body, out_shape=[jax.ShapeDtypeStruct((T, 2 * GW), BF16), jax.ShapeDtypeStruct((T, 2 * HEAD_B), F32),
                         jax.ShapeDtypeStruct((T, 2 * HEAD_B), F32)],
        grid=(2, T // QB_B),
        in_specs=[_bs((QB_B, GW), lambda kv, i: (i, kv)), _bs((T, HEAD_B), lambda kv, i: (0, k_col + kv)),
                  _bs((T, HEAD_B), lambda kv, i: (0, B_V // HEAD_B + kv)), _bs((QB_B, GW), lambda kv, i: (i, kv)),
                  _bs((None, QB_B, HEAD_B), lambda kv, i: (kv, i, 0)), _bs((QB_B, GW), lambda kv, i: (i, kv))] + _any_specs(len(deps)),
        out_specs=[_bs((QB_B, GW), lambda kv, i: (i, kv)), _bs((T, HEAD_B), lambda kv, i: (0, kv)),
                   _bs((T, HEAD_B), lambda kv, i: (0, kv))],
        compiler_params=_params("parallel", "arbitrary"), name="b_bwd")(qn, kn, proj, o, lse, do, *deps)


def _local_step(x, target, small, get_w, put_g, deps=(), prefetch_w=lambda name, after: [], take_rider=lambda steps, after: None):
    T, D = x.shape
    gs = {}

    bias = _bias_tiles(small["rel_bias"])
    cos, sin = _rope_tables(T)
    (x1, h2), ffn1_saved = _ffn_fwd("ffn1", x, small["ffn1_norm"], lambda name, after: get_w(name, [after, bias, cos, sin]), deps,
                                    tail_ins=[small["mix_norm"]], tail_fn=lambda y, g: (y, _norm_fwd(y, g)), tail_outs=(F32, BF16))
    w_in = get_w("w_in", h2)
    nq = w_in.shape[2]
    tpq = nq // WIDTH_A

    def proj_tile(j, k):
        c = j * tpq + k
        return jnp.where(c < 3 * len(DILATIONS), (c % 3) * 3 + c // 3, c)

    proj = _mm("mix_in", (4, tpq),
               [(h2, _resident((T, D), lambda j, k: (0, 0)), w_in, _bs((None, D, WIDTH_A), lambda j, k: (j, 0, k)))],
               jax.ShapeDtypeStruct((T, IN_WIDTH), BF16), _bs((T, WIDTH_A), lambda j, k: (0, proj_tile(j, k))), NN)

    a_views = [_group_view(proj, grp, d) for grp, d in enumerate(DILATIONS)]
    a_outs, a_lses = [], []
    for grp, d in enumerate(DILATIONS):
        o, l = _dil_fwd(a_views[grp], bias[grp], d)
        a_outs.append(o)
        a_lses.append(l)
    o_a = _combine_fwd(a_outs, a_lses)

    qk_gain = jnp.concatenate([jnp.tile(small["q_norm"] * QK_SCALE_LOG2, (8, 1)), jnp.tile(small["k_norm"], (2, 1))])[:, None, :]
    qkn = _qk_fwd("b_qknorm", proj, B_Q, 10, qk_gain, cos, sin, deps=prefetch_w("w_branch_a", proj))
    qn, kn, k_col = qkn, qkn, 8
    o_b, lse_b = _gqa_fwd(qn, kn, proj, k_col)
    ahead = prefetch_w("ffn2_w1", o_b)

    wa, wb, wo = get_w("w_branch_a", o_b), get_w("w_branch_b", o_b), get_w("w_out", o_b)
    bg_a, bg_b = small["b_gate"][:, :D], small["b_gate"][:, D:]
    n_a = wa.shape[0]

    def merge_out(oa_ref, ob_ref, ga_ref, gb_ref, x1_ref, wa_ref, wb_ref, wo_ref, ba_ref, bb_ref, g2_ref, *rest):
        ta_ref, tb_ref, mg_ref, x2_ref, hn_ref = rest[-5:]
        oa = oa_ref[...]
        ta = jnp.concatenate([jnp.dot(oa, wa_ref[j], preferred_element_type=F32) for j in range(n_a)], axis=1)
        tb = jnp.dot(ob_ref[...], wb_ref[...], preferred_element_type=F32)
        sa = _sigmoid(ga_ref[...].astype(F32) + ba_ref[...])
        sb = _sigmoid(gb_ref[...].astype(F32) + bb_ref[...])
        merged = (sa * ta + sb * tb).astype(BF16)
        ta_ref[...], tb_ref[...], mg_ref[...] = ta.astype(BF16), tb.astype(BF16), merged
        y = x1_ref[...] + jnp.dot(merged, wo_ref[...], preferred_element_type=F32)
        x2_ref[...] = y
        hn_ref[...] = _norm_fwd(y, g2_ref[...]).astype(BF16)

    row = _bs((512, D), lambda i: (i, 0))
    gate_specs = [_bs((512, D), lambda i: (i, G_A // D)), _bs((512, D), lambda i: (i, G_B // D))]
    whole2, whole3 = (lambda i: (0, 0)), (lambda i: (0, 0, 0))
    vec = _bs((1, D), whole2)
    t_a, t_b, merged, x2, hn2 = pl.pallas_call(
        merge_out, out_shape=[jax.ShapeDtypeStruct((T, D), BF16)] * 3 + [jax.ShapeDtypeStruct((T, D), F32), jax.ShapeDtypeStruct((T, D), BF16)],
        grid=(T // 512,),
        in_specs=[_bs((512, WIDTH_A), lambda i: (i, 0)), row] + gate_specs + [row, _resident(wa.shape, whole3), _resident((D, D), whole2),
                                                                                _resident((D, D), whole2), vec, vec, vec]
        + _any_specs(len(ahead)),
        out_specs=[row] * 5, compiler_params=_params("parallel"), name="mix_merge_out")(
            o_a, o_b, proj, proj, x1, wa, wb, wo, bg_a, bg_b, small["ffn2_norm"], *ahead)

    def head(xv, g, tv):
        r = _rstd(xv)
        xh = xv * r
        e = xh * g - tv
        dy = e * (1.0 / D)
        dxh = dy * g
        dx = r * (dxh - xh * jnp.mean(dxh * xh, axis=-1, keepdims=True))
        return dx, 0.5 * dx, _colsum(e * e) * (0.5 / D), _colsum(dy * xh)

    (dx3, dx3_half, loss_cols, g_final), ffn2_saved = _ffn_fwd(
        "ffn2", x2, small["ffn2_norm"], get_w, h=hn2, tail_ins=[small["final_norm"].reshape(1, D), target], tail_fn=head,
        tail_outs=(F32, BF16), tail_reds=(D, D))
    gs["final_norm"] = g_final.reshape(D)

    dx2, _, dmix, gs["ffn2_norm"] = _ffn_bwd("ffn2", x2, small["ffn2_norm"], get_w, put_g, ffn2_saved, dx3, dx3_half,
                                             also_bf16=True)
    g_out = _mm_wgrad("mix_bwd_dwout", merged, dmix, a_cols=D // 4, b_cols=None, tm=256, tn=512, J=4).reshape(D, D)

    def merge_out_bwd(dx_ref, ta_ref, tb_ref, ga_ref, gb_ref, wa_ref, wb_ref, wo_ref, ba_ref, bb_ref,
                      dta_ref, dtb_ref, dga_ref, dgb_ref, doa_ref, dob_ref, dba_ref, dbb_ref):
        dm = lax.dot_general(dx_ref[...], wo_ref[...], (NT, ((), ())), preferred_element_type=F32)
        ta, tb = ta_ref[...].astype(F32), tb_ref[...].astype(F32)
        sa = _sigmoid(ga_ref[...].astype(F32) + ba_ref[...])
        sb = _sigmoid(gb_ref[...].astype(F32) + bb_ref[...])
        dga, dgb = dm * ta * sa * (1.0 - sa), dm * tb * sb * (1.0 - sb)
        dta, dtb = (dm * sa).astype(BF16), (dm * sb).astype(BF16)
        dta_ref[...], dtb_ref[...] = dta, dtb
        dga_ref[...], dgb_ref[...] = dga.astype(BF16), dgb.astype(BF16)
        w = wa_ref.shape[2]
        doa = sum(lax.dot_general(dta[:, j * w:(j + 1) * w], wa_ref[j], (NT, ((), ())), preferred_element_type=F32) for j in range(n_a))
        doa_ref[...] = doa.astype(BF16)
        dob_ref[...] = lax.dot_general(dtb, wb_ref[...], (NT, ((), ())), preferred_element_type=F32).astype(BF16)

        @pl.when(pl.program_id(0) == 0)
        def _():
            dba_ref[...] = jnp.zeros_like(dba_ref)
            dbb_ref[...] = jnp.zeros_like(dbb_ref)
        dba_ref[...] += _colsum(dga)
        dbb_ref[...] += _colsum(dgb)

    rowb = _bs((256, D), lambda i: (i, 0))
    gate_specs = [_bs((256, D), lambda i: (i, G_A // D)), _bs((256, D), lambda i: (i, G_B // D))]
    dta, dtb, dga, dgb, do_a, do_b, dba, dbb = pl.pallas_call(
        merge_out_bwd,
        out_shape=[jax.ShapeDtypeStruct((T, D), BF16)] * 4 + [jax.ShapeDtypeStruct((T, WIDTH_A), BF16), jax.ShapeDtypeStruct((T, D), BF16)]
        + [jax.ShapeDtypeStruct((1, D), F32)] * 2,
        grid=(T // 256,),
        in_specs=[rowb, rowb, rowb] + gate_specs + [_resident(wa.shape, whole3), _resident((D, D), whole2), _resident((D, D), whole2), vec, vec],
        out_specs=[rowb] * 4 + [_bs((256, WIDTH_A), lambda i: (i, 0)), rowb, vec, vec],
        compiler_params=_params("arbitrary"), name="mix_merge_out_bwd")(dmix, t_a, t_b, proj, proj, wa, wb, wo, bg_a, bg_b)
    gs["b_gate"] = jnp.concatenate([dba, dbb], axis=1)

    g_a = _mm_wgrad("mix_bwd_dwa", o_a, dta, a_cols=None, b_cols=D // 4, tm=WIDTH_A, tn=256, J=4)
    g_b = _mm_wgrad("mix_bwd_dwb", o_b, dtb, a_cols=D // 4, b_cols=None, tm=256, tn=512, J=4).reshape(D, D)
    deps = put_g({"w_out": g_out, "w_branch_a": g_a, "w_branch_b": g_b})

    dqn, dkn, dv_b = _gqa_bwd(qn, kn, proj, o_b, lse_b, do_b, deps, k_col)
    dq_b, gs["q_norm"] = _qk_bwd("b_bwd_qnorm", dqn, proj, B_Q, 8, small["q_norm"], cos, sin, in_scale=HEAD_B ** -0.5)
    dk_b, gs["k_norm"] = _qk_bwd("b_bwd_knorm", dkn, proj, B_K, 2, small["k_norm"], cos, sin, in_scale=1.0 / LOG2_E)

    do_groups, c_groups = _combine_bwd(do_a, a_outs, a_lses)
    dqs, dks, dvs, dbs = [], [], [], []
    for grp, d in enumerate(DILATIONS):
        dq, dk, dv, db = _dil_bwd(a_views[grp], bias[grp], do_groups[grp], a_lses[grp], c_groups[grp], d)
        dqs.append(dq), dks.append(dk), dvs.append(dv), dbs.append(db)
    gs["rel_bias"] = _bias_grad(jnp.stack(dbs))

    dproj = _assemble_dproj([dqs, dks, dvs], dq_b, dk_b, dv_b, dga, dgb)
    nq = w_in.shape[2]
    g_in = _mm("mix_bwd_dwin", (4, tpq),
               [(h2, _resident((T, D), lambda j, k: (0, 0)), dproj, _bs((T, WIDTH_A), lambda j, k: (0, j * tpq + k)))],
               jax.ShapeDtypeStruct((4, D, nq), BF16), _bs((None, D, WIDTH_A), lambda j, k: (j, 0, k)), TN)
    deps = put_g({"w_in": g_in})
    dx1, dx1_half, gs["mix_norm"] = _dh_norm_bwd(
        "mix_bwd_dh", 256,
        [(dproj, _bs((256, nq), lambda i, j=j: (i, j)), w_in, _resident((None, D, nq), lambda i, j=j: (j, 0, 0))) for j in range(4)],
        NT, x1, small["mix_norm"], dx2, deps)

    dx0, _, gs["ffn1_norm"] = _ffn_bwd("ffn1", x, small["ffn1_norm"], get_w, put_g, ffn1_saved, dx1, dx1_half, last=True,
                                       take_rider=take_rider)
    return loss_cols, dx0, gs


def _position():
    return lax.axis_index("x"), lax.axis_index("y"), lax.axis_index("c")


def _any_specs(n):
    return [pl.BlockSpec(memory_space=pl.ANY)] * n


HBM_SPEC = pl.BlockSpec(memory_space=pltpu.HBM)
SEM_SPEC = pl.BlockSpec(memory_space=pltpu.SEMAPHORE)
DATAFLOW_EFFECT = pltpu.SideEffectType.DATAFLOW_SIDE_EFFECTING
N_PEER_CHIPS = 3
LANES = 128


def _quarter_copies(srcs, lands, send_sems, recv_sems, mode):
    x, y, c = _position()
    me = 2 * x + y
    peers = [(1 - x, y, c), (x, 1 - y, c), (1 - x, 1 - y, c)]
    copies = []
    for src, land, send, recv in zip(srcs, lands, send_sems, recv_sems):
        if mode == "sibling":
            copies.append(pltpu.make_async_remote_copy(src_ref=src, dst_ref=land, send_sem=send.at[0], recv_sem=recv.at[0],
                                                       device_id=(x, y, 1 - c), device_id_type=MESH))
            continue
        if mode == "fill":
            half = land.shape[1] // 2
            for p, (px, py, _) in enumerate(peers):
                part = land.at[2 * px + py, pl.ds(c * half, half)]
                copies.append(pltpu.make_async_remote_copy(src_ref=part, dst_ref=part, send_sem=send.at[p], recv_sem=recv.at[p],
                                                           device_id=(x, y, 1 - c), device_id_type=MESH))
            continue
        scatter = mode == "scatter"
        half = land.shape[1] // 2
        mine = land.at[me, pl.ds(c * half, half)]
        for p, (px, py, pc) in enumerate(peers):
            copies.append(pltpu.make_async_remote_copy(
                src_ref=src.at[2 * px + py] if scatter else mine, dst_ref=land.at[me] if scatter else mine,
                send_sem=send.at[p], recv_sem=recv.at[p], device_id=(px, py, pc), device_id_type=MESH))
    return copies


def _fill_from_sibling(name, stacks):
    n = len(stacks)

    def body(*refs):
        outs = refs[n:2 * n]
        send_sems, recv_sems = refs[2 * n:]
        x, y, c = _position()
        copies = []
        for i, ref in enumerate(outs):
            half = ref.shape[1] // 2
            rows = pl.ds(c * half, half)
            for p, k in enumerate((2 * (1 - x) + y, 2 * x + (1 - y), 2 * (1 - x) + (1 - y))):
                cp = pltpu.make_async_remote_copy(ref.at[k, rows], ref.at[k, rows], send_sems.at[3 * i + p], recv_sems.at[3 * i + p],
                                                  device_id=(x, y, 1 - c), device_id_type=MESH)
                cp.start()
                copies.append(cp)
        for cp in copies:
            cp.wait()

    return pl.pallas_call(
        body, out_shape=[jax.ShapeDtypeStruct(s.shape, s.dtype) for s in stacks],
        in_specs=_any_specs(n), out_specs=_any_specs(n), input_output_aliases={i: i for i in range(n)},
        scratch_shapes=[pltpu.SemaphoreType.DMA((N_PEER_CHIPS * n,)), pltpu.SemaphoreType.DMA((N_PEER_CHIPS * n,))],
        compiler_params=pltpu.CompilerParams(has_side_effects=True), name=name)(*stacks)


def _exchange_start(name, srcs, lands, mode):
    n = len(lands)
    arrays = list(lands) if srcs is None else list(srcs) + list(lands)
    k = len(arrays)

    def body(*refs):
        land_refs = refs[k - n:k]
        send_sems, recv_sems = refs[k:k + n], refs[k + n:k + 2 * n]
        token = refs[2 * k + 2 * n]
        for cp in _quarter_copies(refs[:n], land_refs, send_sems, recv_sems, mode):
            cp.start()
        token[...] = jnp.zeros_like(token)

    sem = pltpu.SemaphoreType.DMA((N_PEER_CHIPS,))
    out_shape = [sem] * (2 * n) + [pltpu.HBM(a.shape, a.dtype) for a in arrays] + [jax.ShapeDtypeStruct((8, LANES), F32)]
    res = pl.pallas_call(
        body, name=name, out_shape=out_shape, in_specs=[HBM_SPEC] * k,
        out_specs=[SEM_SPEC] * (2 * n) + [HBM_SPEC] * k + [pl.BlockSpec(memory_space=pltpu.VMEM)],
        input_output_aliases={i: 2 * n + i for i in range(k)},
        compiler_params=pltpu.CompilerParams(has_side_effects=DATAFLOW_EFFECT),
    )(*[pltpu.with_memory_space_constraint(a, pltpu.HBM) for a in arrays])
    thru = res[2 * n:2 * n + k]
    return res[:n], res[n:2 * n], (None if srcs is None else thru[:n]), thru[k - n:], res[2 * n + k]


def _exchange_wait(name, srcs, lands, send_sems, recv_sems, after, mode):
    n = len(lands)
    arrays = list(lands) if srcs is None else list(srcs) + list(lands)
    k = len(arrays)
    after = list(after) if isinstance(after, (list, tuple)) else [after]

    def body(*refs):
        sends, recvs = refs[k:k + n], refs[k + n:k + 2 * n]
        for cp in _quarter_copies(refs[:n], refs[k - n:k], sends, recvs, mode):
            cp.wait_send()
            cp.wait_recv()

    res = pl.pallas_call(
        body, name=name, out_shape=[pltpu.HBM(a.shape, a.dtype) for a in arrays],
        in_specs=[HBM_SPEC] * k + [SEM_SPEC] * (2 * n) + _any_specs(len(after)),
        out_specs=[HBM_SPEC] * k, input_output_aliases={i: i for i in range(k)},
        compiler_params=pltpu.CompilerParams(has_side_effects=DATAFLOW_EFFECT),
    )(*arrays, *send_sems, *recv_sems, *after)
    return (None if srcs is None else res[:n]), res[k - n:]


def _own_slots(name, srcs, from_stack=False):
    n = len(srcs)
    me = (2 * lax.axis_index("x") + lax.axis_index("y")).astype(jnp.int32).reshape(1)

    def body(me_ref, *refs):
        for x_ref, o_ref in zip(refs[:n], refs[n:]):
            o_ref[...] = x_ref[...].astype(o_ref.dtype)

    in_specs, out_specs, out_shape = [], [], []
    for src in srcs:
        R, C = src.shape[-2:]
        in_specs.append(pl.BlockSpec((None, R // 2, C), lambda i, me_ref: (me_ref[0], i, 0)) if from_stack
                        else pl.BlockSpec((R // 2, C), lambda i, me_ref: (i, 0)))
        out_specs.append(pl.BlockSpec((None, R // 2, C), lambda i, me_ref: (me_ref[0], i, 0)))
        out_shape.append(jax.ShapeDtypeStruct((4, R, C), BF16))
    grid_spec = pltpu.PrefetchScalarGridSpec(num_scalar_prefetch=1, grid=(2,), in_specs=in_specs, out_specs=out_specs)
    return pl.pallas_call(body, out_shape=out_shape, grid_spec=grid_spec, compiler_params=_params("parallel"), name=name)(me, *srcs)


def _allreduce_small(buf):
    R, C = buf.shape
    flips = [(fx, fy, fc) for fx in (0, 1) for fy in (0, 1) for fc in (0, 1)][1:]

    def body(in_ref, out_ref, land_ref, send_sems, recv_sems):
        x, y, c = _position()
        me = 4 * x + 2 * y + c
        copies = []
        for k, (fx, fy, fc) in enumerate(flips):
            px, py, pc = (1 - x if fx else x), (1 - y if fy else y), (1 - c if fc else c)
            cp = pltpu.make_async_remote_copy(in_ref, land_ref.at[me], send_sems.at[k], recv_sems.at[k],
                                              device_id=(px, py, pc), device_id_type=MESH)
            cp.start()
            copies.append(cp)
        land_ref[me] = in_ref[...]
        for cp in copies:
            cp.wait()
        acc = land_ref[0]
        for k in range(1, 8):
            acc = acc + land_ref[k]
        out_ref[...] = acc

    return pl.pallas_call(
        body, out_shape=jax.ShapeDtypeStruct((R, C), F32),
        in_specs=[pl.BlockSpec(memory_space=pltpu.VMEM)], out_specs=pl.BlockSpec(memory_space=pltpu.VMEM),
        scratch_shapes=[pltpu.VMEM((8, R, C), F32), pltpu.SemaphoreType.DMA((7,)), pltpu.SemaphoreType.DMA((7,))],
        compiler_params=pltpu.CompilerParams(has_side_effects=True), name="allreduce_small")(buf)


def _adamw_math(w, g, m, v):
    m2 = ADAM_B1 * m + (1.0 - ADAM_B1) * g
    v2 = ADAM_B2 * v + (1.0 - ADAM_B2) * (g * g)
    m_hat = m2 / (1.0 - ADAM_B1 ** ADAM_STEP)
    v_hat = v2 / (1.0 - ADAM_B2 ** ADAM_STEP)
    delta = -ADAM_LR * (m_hat / (jnp.sqrt(v_hat) + ADAM_EPS) + ADAM_WD * w)
    return delta, m2, v2


def _adamw_from_partials(wv, mv, vv, *parts):
    def four(a, b, c, d):
        return ((a.astype(F32) + b.astype(F32)) + c.astype(F32)) + d.astype(F32)

    g = four(*parts[:4]) + four(*parts[4:])
    return (g,) + _adamw_math(wv, g, mv, vv)


def _adamw_big(name, w, m, v, mine, theirs):
    R, C = w.shape
    rows = 256 if R % 256 == 0 else R // 2
    nrb = R // rows
    slots = [_tiled(s.reshape(4 * R, C), None, 0, k * nrb) for s in (mine, theirs) for k in range(4)]
    return _ew(name, _adamw_from_partials, [_tiled(w), _tiled(m), _tiled(v)] + slots, [(F32, C)] * 4, n_rows=R, rows=rows)


def _adamw_rider(w, m, v, mine, theirs, steps, deliver):
    R, C = w.shape
    fits = [nb for nb in range(1, steps + 1) if R % nb == 0 and (R // nb) % 16 == 0]
    if not fits:
        return None
    nb = fits[-1]
    rows = R // nb

    def blocks(first):
        return pl.BlockSpec((rows, C), lambda *g: (first + jnp.minimum(g[0], nb - 1), 0))

    flat = [s.reshape(4 * R, C) for s in (mine, theirs)]
    return dict(operands=[w, m, v] + [f for f in flat for _ in range(4)],
                in_specs=[blocks(0)] * 3 + [blocks(k * nb) for _ in flat for k in range(4)],
                out_shape=[jax.ShapeDtypeStruct((R, C), F32)] * 4, out_specs=[blocks(0)] * 4,
                n_blocks=nb, fn=_adamw_from_partials, deliver=lambda outs: deliver(*outs))


BIG = ("ffn1_w1", "ffn1_w3", "ffn1_w2", "w_in", "w_branch_a", "w_branch_b", "w_out", "ffn2_w1", "ffn2_w3", "ffn2_w2")
SMALL = ("ffn1_norm", "mix_norm", "b_gate", "q_norm", "k_norm", "rel_bias", "ffn2_norm", "final_norm")
ORDER = ("ffn1_norm", "ffn1_w1", "ffn1_w3", "ffn1_w2", "mix_norm", "w_in", "b_gate", "q_norm", "k_norm", "rel_bias",
         "w_branch_a", "w_branch_b", "w_out", "ffn2_norm", "ffn2_w1", "ffn2_w3", "ffn2_w2", "final_norm")
TRANSPOSED = ("ffn1_w1", "ffn1_w3", "ffn2_w1", "ffn2_w3")
SIBLING_LAG = 2
GATHER_GROUPS = (("ffn1_w1", "ffn1_w3"), ("ffn1_w2",), ("w_in",), ("w_branch_a", "w_branch_b", "w_out"),
                 ("ffn2_w1", "ffn2_w3", "ffn2_w2"))


def _pack_small(d):
    rows = []
    for n in SMALL:
        flat = d[n].reshape(-1)
        pad = (-flat.shape[0]) % LANES
        rows.append(jnp.pad(flat, (0, pad)).reshape(-1, LANES))
    buf = jnp.concatenate(rows, axis=0)
    return jnp.pad(buf, ((0, (-buf.shape[0]) % 8), (0, 0)))


def _unpack_small(buf, like):
    out, r = {}, 0
    for n in SMALL:
        size = like[n].size
        nr = -(-size // LANES)
        out[n] = buf[r:r + nr].reshape(-1)[:size].reshape(like[n].shape)
        r += nr
    return out


def kernel(x, ffn1_norm, ffn1_w1, ffn1_w3, ffn1_w2, mix_norm, w_in, b_gate, q_norm, k_norm, rel_bias, w_branch_a, w_branch_b, w_out, ffn2_norm, ffn2_w1, ffn2_w3, ffn2_w2, final_norm, loss_target, m_ffn1_norm, m_ffn1_w1, m_ffn1_w3, m_ffn1_w2, m_mix_norm, m_w_in, m_b_gate, m_q_norm, m_k_norm, m_rel_bias, m_w_branch_a, m_w_branch_b, m_w_out, m_ffn2_norm, m_ffn2_w1, m_ffn2_w3, m_ffn2_w2, m_final_norm, v_ffn1_norm, v_ffn1_w1, v_ffn1_w3, v_ffn1_w2, v_mix_norm, v_w_in, v_b_gate, v_q_norm, v_k_norm, v_rel_bias, v_w_branch_a, v_w_branch_b, v_w_out, v_ffn2_norm, v_ffn2_w1, v_ffn2_w3, v_ffn2_w2, v_final_norm):
    given = dict(locals())
    w = {n: given[n] for n in ORDER}
    m = {n: given["m_" + n] for n in ORDER}
    v = {n: given["v_" + n] for n in ORDER}
    T, D = x.shape[1], x.shape[2]

    def stored(a, n):
        a = a.reshape(a.shape[1:])
        return a.T if n in TRANSPOSED else a

    def returned(a, n):
        return (a.T if n in TRANSPOSED else a).reshape(w[n].shape)

    quarter = {n: stored(w[n], n) for n in BIG}
    send, recv, _, land_thru, token = _exchange_start(
        "gather_start", None, _own_slots("own_weights", [quarter[n] for n in BIG]), "gather")
    index = {n: i for i, n in enumerate(BIG)}
    ready, filling = {}, {}

    def landed_halves(group, after):
        ids = [index[n] for n in group]
        return _exchange_wait("gather_wait_" + group[0], None, [land_thru[i] for i in ids],
                              [send[i] for i in ids], [recv[i] for i in ids], after, "gather")[1]

    def prefetch_w(name, after):
        group = next(g for g in GATHER_GROUPS if name in g)
        started = _exchange_start("fill_start_" + group[0], None, landed_halves(group, after), "fill")
        filling[group] = started
        return [started[4]]

    def get_w(name, after):
        if name not in ready:
            group = next(g for g in GATHER_GROUPS if name in g)
            if group in filling:
                f_send, f_recv, _, thru, _ = filling[group]
                stacks = _exchange_wait("fill_wait_" + group[0], None, thru, f_send, f_recv, after, "fill")[1]
            else:
                stacks = _fill_from_sibling("gather_fill_" + group[0], landed_halves(group, after))
            for n, st in zip(group, stacks):
                ready[n] = st.reshape(D, D) if n in ("w_branch_b", "w_out") else st
        return ready[name]

    scattered, forwarded = [], []

    def forward_oldest(after):
        names, s_sem, r_sem, srcs, lands = scattered.pop(0)
        _, landed = _exchange_wait("scatter_wait_" + names[0], srcs, lands, s_sem, r_sem, after, "scatter")
        started = _exchange_start("sibling_start_" + names[0], landed, [lax.empty(a.shape, a.dtype) for a in landed], "sibling")
        forwarded.append((names,) + tuple(started[:4]))
        return started[4]

    def put_g(grads):
        names = list(grads)
        stacks = [grads[n].reshape((4,) + quarter[n].shape) for n in names]
        lands = _own_slots("own_grad_" + names[0], stacks, from_stack=True)
        started = _exchange_start("scatter_start_" + names[0], stacks, lands, "scatter")
        scattered.append((names,) + tuple(started[:4]))
        tokens = [started[4]]
        if len(scattered) > SIBLING_LAG:
            tokens.append(forward_oldest(started[4]))
        return tokens

    grads, deltas, new_m, new_v = {}, {}, {}, {}
    arrived, riding = {}, set()

    def partials(gi, after):
        if gi not in arrived:
            names, s_sem, r_sem, srcs, lands = forwarded[gi]
            arrived[gi] = _exchange_wait("sibling_wait_" + names[0], srcs, lands, s_sem, r_sem, after, "sibling")
        return arrived[gi]

    def deliver_to(n):
        def deliver(*res):
            grads[n], deltas[n], new_m[n], new_v[n] = [returned(r, n) for r in res]
        return deliver

    def take_rider(steps, after):
        for gi, entry in enumerate(forwarded):
            for k, n in enumerate(entry[0]):
                if n not in riding:
                    mine, theirs = partials(gi, after)
                    rider = _adamw_rider(quarter[n], stored(m[n], n), stored(v[n], n), mine[k], theirs[k], steps, deliver_to(n))
                    if rider is not None:
                        riding.add(n)
                        return rider
        return None

    small = {n: w[n] for n in SMALL}
    packed = [_pack_small({n: d[n] for n in SMALL}) for d in (w, m, v)]
    loss_cols, grad_x, gs = _local_step(x.reshape(T, D), loss_target.reshape(T, D), small, get_w, put_g, deps=[token] + packed,
                                        prefetch_w=prefetch_w, take_rider=take_rider)

    after = grad_x
    while scattered:
        after = forward_oldest(after)
    for gi, entry in enumerate(forwarded):
        mine, theirs = partials(gi, after)
        for n, a, b in zip(entry[0], mine, theirs):
            if n not in riding:
                deliver_to(n)(*_adamw_big(f"adamw_{n}", quarter[n], stored(m[n], n), stored(v[n], n), a, b))

    gs = {n: gs[n].reshape(w[n].shape) for n in SMALL}
    packed_g = _pack_small(gs)
    n_small = packed_g.shape[0]
    summed = _allreduce_small(jnp.concatenate([packed_g, loss_cols.reshape(-1, LANES)], axis=0))
    g_small, loss = summed[:n_small], jnp.sum(summed[n_small:])
    R = g_small.shape[0]
    res = _ew("adamw_small", lambda wv, mv, vv, g: (g,) + _adamw_math(wv, g, mv, vv),
              [_tiled(packed[0]), _tiled(packed[1]), _tiled(packed[2]), _tiled(g_small)], [(F32, LANES)] * 4, n_rows=R, rows=R)
    for d, buf in zip((grads, deltas, new_m, new_v), res):
        d.update(_unpack_small(buf, w))

    return (loss, grad_x.reshape(x.shape), *[grads[n] for n in ORDER], *[deltas[n] for n in ORDER],
            *[new_m[n] for n in ORDER], *[new_v[n] for n in ORDER])
```

```python
import functools
import math

import numpy as np
import jax
import jax.numpy as jnp
from jax import lax
from jax.experimental import pallas as pl
from jax.experimental.pallas import tpu as pltpu

F32 = jnp.float32
BF16 = jnp.bfloat16
MESH = pl.DeviceIdType.MESH

NEG_INF = -1e30
EPS = 1e-6
GRID_W = 64
ROPE_THETA = 10000.0
DILATIONS = (1, 4, 16)
BAND_HALF = 64
HEAD_A = 64
HEADS_A = 8
WIDTH_A = HEADS_A * HEAD_A
HEAD_B = 128
LOG2_E = math.log2(math.e)
QK_SCALE_LOG2 = HEAD_B ** -0.5 * LOG2_E
N_BUCKETS = 32
MAX_DISTANCE = 1024
ADAM_LR, ADAM_B1, ADAM_B2, ADAM_EPS, ADAM_WD, ADAM_STEP = 0.001, 0.9, 0.999, 1e-08, 0.01, 10

B_Q, B_K, B_V = 4608, 5632, 5888
G_A, G_B = 6144, 7168
IN_WIDTH = 8192

VMEM_LIMIT_BYTES = 56 * 1024 * 1024
QB_A = 128
QB_B = 256


def _params(*sem):
    return pltpu.CompilerParams(dimension_semantics=sem, vmem_limit_bytes=VMEM_LIMIT_BYTES)


def _bs(shape, fn):
    return pl.BlockSpec(shape, fn)


def _resident(shape, fn):
    return pl.BlockSpec(shape, fn, pipeline_mode=pl.Buffered(1))


def _mm(name, grid, pairs, out_shape, out_spec, dims, *, extras=(), epilogue=None, deps=(), reds=(), rider=None):
    n_pairs, n_extra, n_deps = len(pairs), len(extras), len(deps)
    operands = [p[0] for p in pairs] + [p[2] for p in pairs] + [e[0] for e in extras] + list(deps)
    in_specs = [p[1] for p in pairs] + [p[3] for p in pairs] + [e[1] for e in extras] + _any_specs(n_deps)
    single = not isinstance(out_shape, (list, tuple))
    out_shapes = [out_shape] if single else list(out_shape)
    out_specs = [out_spec] if single else list(out_spec)
    n_out = len(out_shapes)
    out_shapes += [jax.ShapeDtypeStruct((1, w), F32) for w in reds]
    out_specs += [_bs((1, w), lambda *_: (0, 0)) for w in reds]
    n_rin = 0
    if rider is not None:
        assert rider["n_blocks"] <= grid[0]
        n_rin = len(rider["operands"])
        operands += list(rider["operands"])
        in_specs += list(rider["in_specs"])
        out_shapes += list(rider["out_shape"])
        out_specs += list(rider["out_specs"])

    def body(*refs):
        a_refs, b_refs = refs[:n_pairs], refs[n_pairs:2 * n_pairs]
        e_refs = refs[2 * n_pairs:2 * n_pairs + n_extra]
        o_refs = refs[2 * n_pairs + n_extra + n_deps + n_rin:]
        if rider is not None:
            r_in = refs[2 * n_pairs + n_extra + n_deps:2 * n_pairs + n_extra + n_deps + n_rin]
            r_out = o_refs[n_out + len(reds):]

            @pl.when(pl.program_id(0) < rider["n_blocks"])
            def _():
                for ref, val in zip(r_out, rider["fn"](*[r[...] for r in r_in])):
                    ref[...] = val.astype(ref.dtype)
        acc = None
        for a_ref, b_ref in zip(a_refs, b_refs):
            t = lax.dot_general(a_ref[...], b_ref[...], (dims, ((), ())), preferred_element_type=F32)
            acc = t if acc is None else acc + t
        vals = acc if epilogue is None else epilogue(acc, *[e[...] for e in e_refs])
        if not isinstance(vals, (list, tuple)):
            vals = (vals,)
        for o_ref, v in zip(o_refs[:n_out], vals[:n_out]):
            o_ref[...] = v.astype(o_ref.dtype)
        if reds:
            first = functools.reduce(jnp.logical_and, [pl.program_id(ax) == 0 for ax in range(len(grid))])
            for r_ref, v in zip(o_refs[n_out:], vals[n_out:]):
                @pl.when(first)
                def _(r_ref=r_ref):
                    r_ref[...] = jnp.zeros_like(r_ref)
                r_ref[...] += v

    sem = ["arbitrary" if (reds or rider is not None) else "parallel"] * len(grid)
    res = pl.pallas_call(
        body, out_shape=out_shapes, grid=grid, in_specs=in_specs, out_specs=out_specs,
        compiler_params=_params(*sem), name=name)(*operands)
    if rider is not None:
        rider["deliver"](res[n_out + len(reds):])
        res = res[:n_out + len(reds)]
    return res[0] if (single and not reds) else res


NN = ((1,), (0,))
NT = ((1,), (1,))
TN = ((0,), (0,))


def _mm_wgrad(name, a, b, *, a_cols, b_cols, tm, tn, J, deps=(), rider=None):
    def pick(arr, cols, t):
        if arr.ndim == 3:
            T, c = arr.shape[1], arr.shape[2]
            t = min(t, c)
            return T, c, t, (lambda sel: _bs((None, T, t), lambda j, i, k: (j, 0, sel(i, k))))
        T = arr.shape[0]
        c = arr.shape[1] if cols is None else cols
        t = min(t, c)
        per = c // t
        if cols is None:
            if per == 1:
                return T, c, t, (lambda sel: _resident((T, t), lambda j, i, k: (0, 0)))
            return T, c, t, (lambda sel: _bs((T, t), lambda j, i, k: (0, sel(i, k))))
        return T, c, t, (lambda sel: _bs((T, t), lambda j, i, k: (0, j * per + sel(i, k))))
    _, ca, tm, mk_a = pick(a, a_cols, tm)
    _, cb, tn, mk_b = pick(b, b_cols, tn)
    return _mm(name, (J, ca // tm, cb // tn),
               [(a, mk_a(lambda i, k: i), b, mk_b(lambda i, k: k))],
               jax.ShapeDtypeStruct((J, ca, cb), BF16), _bs((None, tm, tn), lambda j, i, k: (j, i, k)), TN, deps=deps, rider=rider)


def _tiled(arr, width=None, col=0, rowblk=0):
    return ("t", arr, arr.shape[1] if width is None else width, col, rowblk)


def _table(arr):
    return ("f", arr)


def _whole(arr):
    return ("w", arr)


def _ew(name, fn, ins, outs, *, n_rows, rows, reds=(), ncols=1, deps=()):
    nrb = n_rows // rows
    n_deps = len(deps)
    operands, in_specs = [], []
    for spec in ins:
        if spec[0] == "t":
            _, arr, width, col, rowblk = spec
            step = 1 if ncols > 1 else 0
            in_specs.append(_bs((rows, width), lambda c, i, col=col, rowblk=rowblk, step=step: (rowblk + i, col + c * step)))
        elif spec[0] == "f":
            arr = spec[1]
            in_specs.append(_bs((rows, arr.shape[1]), lambda c, i: (i, 0)))
        else:
            arr = spec[1]
            nd = arr.ndim
            if nd == 3:
                in_specs.append(_bs((None,) + arr.shape[1:], lambda c, i: (c, 0, 0)))
            else:
                in_specs.append(_bs(arr.shape, lambda c, i, nd=nd: (0,) * nd))
        operands.append(arr)
    out_shapes = [jax.ShapeDtypeStruct((n_rows, ncols * w), dt) for dt, w in outs]
    out_specs = [_bs((rows, w), lambda c, i: (i, c)) for _, w in outs]
    out_shapes += [jax.ShapeDtypeStruct((ncols, 1, w), F32) for w in reds]
    out_specs += [_bs((None, 1, w), lambda c, i: (c, 0, 0)) for w in reds]
    n_in, n_out, n_red = len(ins), len(outs), len(reds)
    operands += list(deps)
    in_specs += _any_specs(n_deps)

    def body(*refs):
        vals = fn(*[r[...] for r in refs[:n_in]])
        if not isinstance(vals, (tuple, list)):
            vals = (vals,)
        o_refs = refs[n_in + n_deps:]
        for o_ref, v in zip(o_refs[:n_out], vals[:n_out]):
            o_ref[...] = v.astype(o_ref.dtype)
        if n_red:
            i = pl.program_id(1)
            for r_ref, v in zip(o_refs[n_out:], vals[n_out:]):
                @pl.when(i == 0)
                def _(r_ref=r_ref):
                    r_ref[...] = jnp.zeros_like(r_ref)
                r_ref[...] += v

    res = pl.pallas_call(
        body, out_shape=out_shapes, grid=(ncols, nrb), in_specs=in_specs, out_specs=out_specs,
        compiler_params=_params("parallel", "arbitrary" if n_red else "parallel"), name=name)(*operands)
    return res


def _colsum(v):
    return jnp.sum(v, axis=0, keepdims=True)


def _rstd(x):
    return lax.rsqrt(jnp.mean(x * x, axis=-1, keepdims=True) + EPS)


def _sigmoid(x):
    return 1.0 / (1.0 + jnp.exp(-x))


def _norm_fwd(x, g):
    return x * _rstd(x) * g


def _norm_bwd(x, g, dy):
    r = _rstd(x)
    xh = x * r
    dxh = dy * g
    dx = r * (dxh - xh * jnp.mean(dxh * xh, axis=-1, keepdims=True))
    return dx, dy * xh


def _row_spec(arr, rows):
    if arr.shape[0] == 1:
        return _bs(arr.shape, lambda i: (0, 0))
    return _bs((rows, arr.shape[1]), lambda i: (i, 0))


def _ffn_fwd(tag, x, gain, get_w, deps=(), *, h=None, tail_ins=(), tail_fn=None, tail_outs=(F32,), tail_reds=()):
    T, D = x.shape
    if h is None:
        (h,) = _ew(f"{tag}_norm", lambda xv, g: _norm_fwd(xv, g), [_tiled(x), _whole(gain)], [(BF16, D)], n_rows=T, rows=512,
                   deps=deps)
    w1, w3 = get_w(f"{tag}_w1", h), get_w(f"{tag}_w3", h)
    J, f, _ = w1.shape
    tm = 1024

    def up(h_ref, w1_ref, w3_ref, u_ref, g_ref, a_ref):
        hv = h_ref[...]
        u = lax.dot_general(hv, w1_ref[...], (NT, ((), ())), preferred_element_type=F32)
        g = lax.dot_general(hv, w3_ref[...], (NT, ((), ())), preferred_element_type=F32)
        u_ref[...] = u.astype(BF16)
        g_ref[...] = g.astype(BF16)
        a_ref[...] = (u * _sigmoid(u) * g).astype(BF16)

    slab = _bs((None, tm, f), lambda j, i: (j, i, 0))
    w_spec = _bs((None, f, D), lambda j, i: (j, 0, 0))
    u, g, a = pl.pallas_call(
        up, out_shape=[jax.ShapeDtypeStruct((J, T, f), BF16)] * 3, grid=(J, T // tm),
        in_specs=[_bs((tm, D), lambda j, i: (i, 0)), w_spec, w_spec], out_specs=[slab] * 3,
        compiler_params=_params("parallel", "parallel"), name=f"{tag}_up")(h, w1, w3)
    w2 = get_w(f"{tag}_w2", a)
    def tail(acc, xv, *rest):
        y = xv + 0.5 * acc
        return y if tail_fn is None else tail_fn(y, *rest)

    row = _bs((512, D), lambda i: (i, 0))
    res = _mm(f"{tag}_down", (T // 512,),
              [(a, _bs((None, 512, f), lambda i, j=j: (j, i, 0)), w2, _resident((None, f, D), lambda i, j=j: (j, 0, 0)))
               for j in range(J)],
              [jax.ShapeDtypeStruct((T, D), dt) for dt in tail_outs], [row] * len(tail_outs), NN,
              extras=[(x, row)] + [(t, _row_spec(t, 512)) for t in tail_ins], epilogue=tail, reds=tail_reds)
    return res, (h, u, g, a)


def _dh_norm_bwd(name, rows, pairs, dims, x, gain, dres, deps, also_bf16=False, rider=None):
    T, D = x.shape

    def epilogue(dh, xv, gv, dr):
        dx, dgr = _norm_bwd(xv, gv, dh)
        dx = dx + dr
        return (dx, 0.5 * dx) + ((dx,) if also_bf16 else ()) + (_colsum(dgr),)

    dts = [F32, BF16] + ([BF16] if also_bf16 else [])
    row = _bs((rows, D), lambda i: (i, 0))
    return _mm(name, (T // rows,), pairs, [jax.ShapeDtypeStruct((T, D), dt) for dt in dts], [row] * len(dts), dims,
               extras=[(x, row), (gain, _row_spec(gain, rows)), (dres, row)], epilogue=epilogue, deps=deps, reds=(D,), rider=rider)


def _ffn_bwd(tag, x, gain, get_w, put_g, saved, dy, dy_half, also_bf16=False, last=False, take_rider=lambda steps, after: None):
    h, u, g, a = saved
    T, D = x.shape
    w1, w3, w2 = [get_w(f"{tag}_{n}", dy_half) for n in ("w1", "w3", "w2")]
    J, f, _ = w1.shape
    dw2 = _mm_wgrad(f"{tag}_bwd_dw2", a, dy_half, a_cols=None, b_cols=None, tm=f, tn=D, J=J, rider=take_rider(J, dy_half))
    deps = put_g({f"{tag}_w2": dw2}) if last else []
    tm = 1024

    def up_bwd(dy_ref, w2_ref, u_ref, g_ref, *rest):
        du_ref, dg_ref = rest[-2:]
        da = lax.dot_general(dy_ref[...], w2_ref[...], (NT, ((), ())), preferred_element_type=F32)
        uv, gv = u_ref[...].astype(F32), g_ref[...].astype(F32)
        s = _sigmoid(uv)
        du_ref[...] = (da * gv * (s * (1.0 + uv * (1.0 - s)))).astype(BF16)
        dg_ref[...] = (da * (uv * s)).astype(BF16)

    slab = _bs((None, tm, f), lambda j, i: (j, i, 0))
    du, dg = pl.pallas_call(
        up_bwd, out_shape=[jax.ShapeDtypeStruct((J, T, f), BF16)] * 2, grid=(J, T // tm),
        in_specs=[_bs((tm, D), lambda j, i: (i, 0)), _bs((None, f, D), lambda j, i: (j, 0, 0)), slab, slab] + _any_specs(len(deps)),
        out_specs=[slab] * 2, compiler_params=_params("parallel", "parallel"), name=f"{tag}_bwd_up")(dy_half, w2, u, g, *deps)
    dw1 = _mm_wgrad(f"{tag}_bwd_dw1", du, h, a_cols=None, b_cols=None, tm=f, tn=D, J=J, rider=take_rider(J, du))
    deps = put_g({f"{tag}_w1": dw1}) if last else []
    dw3 = _mm_wgrad(f"{tag}_bwd_dw3", dg, h, a_cols=None, b_cols=None, tm=f, tn=D, J=J, deps=deps, rider=take_rider(J, dw1))
    deps = put_g({f"{tag}_w3": dw3} if last else {f"{tag}_w2": dw2, f"{tag}_w1": dw1, f"{tag}_w3": dw3})
    pairs = []
    for j in range(J):
        a_spec = _bs((None, 256, f), lambda i, j=j: (j, i, 0))
        w_spec = _resident((None, f, D), lambda i, j=j: (j, 0, 0))
        pairs += [(du, a_spec, w1, w_spec), (dg, a_spec, w3, w_spec)]
    return _dh_norm_bwd(f"{tag}_bwd_dh", 256, pairs, NN, x, gain, dy, deps, also_bf16, rider=take_rider(T // 256, dw3))


def _t5_bucket(rel):
    n = N_BUCKETS // 2
    max_exact = n // 2
    ret = jnp.where(rel > 0, n, 0)
    a = jnp.abs(rel)
    af = jnp.maximum(a, 1).astype(F32)
    large = max_exact + (jnp.log(af / max_exact) / math.log(MAX_DISTANCE / max_exact) * (n - max_exact)).astype(jnp.int32)
    large = jnp.minimum(large, n - 1)
    return ret + jnp.where(a < max_exact, a, large)


WIN_A = QB_A + 2 * BAND_HALF
WIN_SHIFTS = (0, BAND_HALF, 2 * BAND_HALF)


def _window_variant(n, nblk):
    return jnp.where(n == 0, 0, jnp.where(n == nblk - 1, 2, 1))


def _window_start(n, nblk):
    return pl.multiple_of(jnp.clip(n * QB_A - BAND_HALF, 0, nblk * QB_A - WIN_A), BAND_HALF)


def _band_steps(xp=jnp):
    qi = xp.arange(QB_A, dtype=xp.int32)[None, :, None]
    kj = xp.arange(WIN_A, dtype=xp.int32)[None, None, :]
    return kj - qi - xp.asarray(WIN_SHIFTS, dtype=xp.int32)[:, None, None]


def _bias_tiles(rel_bias):
    wide = QB_A + 2 * WIN_SHIFTS[-1]
    qi = jnp.arange(QB_A, dtype=jnp.int32)[:, None]
    steps = jnp.arange(wide, dtype=jnp.int32)[None, :] - WIN_SHIFTS[-1] - qi
    buckets = jnp.stack([_t5_bucket(steps * d) for d in DILATIONS])
    inband = (jnp.abs(steps) <= BAND_HALF).astype(jnp.int32)
    n_heads = rel_bias.shape[1]

    def body(tab_ref, b_ref, m_ref, o_ref):
        hd = pl.program_id(0)
        bkt = b_ref[...]
        acc = jnp.zeros(bkt.shape, F32)
        for b in range(N_BUCKETS):
            acc = jnp.where(bkt == b, tab_ref[b, hd], acc)
        o_ref[...] = jnp.where(m_ref[...] > 0, acc, NEG_INF)

    base = pl.pallas_call(
        body, out_shape=jax.ShapeDtypeStruct((n_heads, QB_A, wide), F32), grid=(n_heads,),
        in_specs=[pl.BlockSpec(memory_space=pltpu.SMEM),
                  _bs((None, QB_A, wide), lambda hd: (hd // HEADS_A, 0, 0)),
                  _bs((QB_A, wide), lambda hd: (0, 0))],
        out_specs=_bs((None, QB_A, wide), lambda hd: (hd, 0, 0)),
        compiler_params=_params("parallel"), name="a_bias_tiles")(rel_bias, buckets, inband)
    base = base.reshape(len(DILATIONS), HEADS_A, QB_A, wide)
    return jnp.stack([base[..., WIN_SHIFTS[-1] - s:WIN_SHIFTS[-1] - s + WIN_A] for s in WIN_SHIFTS], axis=1)


def _bias_grad(dbias):
    steps = _band_steps(np)
    inband = np.abs(steps) <= BAND_HALF
    present = []
    for d in DILATIONS:
        rel = steps * d
        a = np.abs(rel)
        large = 8 + (np.log(np.maximum(a, 1) / 8.0) / math.log(MAX_DISTANCE / 8.0) * 8).astype(np.int64)
        bk = np.where(rel > 0, 16, 0) + np.where(a < 8, a, np.minimum(large, 15))
        present.append([sorted(set(bk[v][inband[v]].tolist())) for v in range(3)])
    buckets = jnp.stack([_t5_bucket(_band_steps() * d) for d in DILATIONS])
    n_heads = len(DILATIONS) * HEADS_A

    def body(b_ref, d_ref, o_ref):
        row = lax.broadcasted_iota(jnp.int32, (N_BUCKETS, n_heads), 0)
        col = lax.broadcasted_iota(jnp.int32, (N_BUCKETS, n_heads), 1)
        out = jnp.zeros((N_BUCKETS, n_heads), F32)
        for grp in range(len(DILATIONS)):
            for hh in range(HEADS_A):
                hd = grp * HEADS_A + hh
                for b in sorted(set(sum(present[grp], []))):
                    tot = jnp.zeros((), F32)
                    for v in range(3):
                        if b in present[grp][v]:
                            tot = tot + jnp.sum(jnp.where(b_ref[grp, v] == b, d_ref[grp, v, hh], 0.0))
                    out = jnp.where((row == b) & (col == hd), tot, out)
        o_ref[...] = out

    return pl.pallas_call(
        body, out_shape=jax.ShapeDtypeStruct((N_BUCKETS, n_heads), F32),
        compiler_params=pltpu.CompilerParams(vmem_limit_bytes=VMEM_LIMIT_BYTES), name="a_bias_grad")(buckets, dbias)


def _lane_is_second_head(shape):
    return lax.broadcasted_iota(jnp.int32, shape, len(shape) - 1) >= HEAD_A


VIEW_ROWS = 512


def _view_chunks():
    return [pltpu.VMEM((VIEW_ROWS, LANES), F32)] * (WIDTH_A // LANES)


def _rows_to_view(x_ref, col, o_ref, ocol, d, chunks):
    n = VIEW_ROWS // d
    for c, scr in enumerate(chunks):
        scr[...] = x_ref[:, col + c * LANES:col + (c + 1) * LANES].astype(F32)
        for r in range(d):
            at = ocol + r * WIDTH_A + c * LANES
            o_ref[:, at:at + LANES] = scr[pl.ds(r, n, stride=d), :].astype(o_ref.dtype)


def _view_to_rows(v_ref, o_ref, col, d, chunks):
    n = VIEW_ROWS // d
    for c, scr in enumerate(chunks):
        if d == 1:
            o_ref[:, col + c * LANES:col + (c + 1) * LANES] = v_ref[:, c * LANES:(c + 1) * LANES].astype(o_ref.dtype)
            continue
        for r in range(d):
            scr[pl.ds(r, n, stride=d), :] = v_ref[:, r * WIDTH_A + c * LANES:r * WIDTH_A + (c + 1) * LANES].astype(F32)
        o_ref[:, col + c * LANES:col + (c + 1) * LANES] = scr[...].astype(o_ref.dtype)


def _group_view(proj, grp, d):
    T = proj.shape[0]
    if d == 1:
        return proj, (lambda part, r: grp * 3 + part)

    def body(x_ref, o_ref, *chunks):
        for part in range(3):
            _rows_to_view(x_ref, part * WIDTH_A, o_ref, part * d * WIDTH_A, d, chunks)

    view = pl.pallas_call(
        body, out_shape=jax.ShapeDtypeStruct((T // d, 3 * d * WIDTH_A), proj.dtype), grid=(T // VIEW_ROWS,),
        in_specs=[_bs((VIEW_ROWS, 3 * WIDTH_A), lambda i: (i, grp))],
        out_specs=_bs((VIEW_ROWS // d, 3 * d * WIDTH_A), lambda i: (i, 0)),
        scratch_shapes=_view_chunks(), compiler_params=_params("parallel"), name=f"a_view_d{d}")(proj)
    return view, (lambda part, r: part * d + r)


def _stack_heads(v2, second):
    zero = jnp.zeros_like(v2)
    return jnp.concatenate([jnp.where(second, zero, v2), jnp.where(second, v2, zero)], axis=0)


def _unstack_heads(v, second):
    return jnp.where(second, v[QB_A:], v[:QB_A])


def _dil_fwd(view, bias, d):
    pv, colblk = view
    L = pv.shape[0]
    nblk = L // QB_A
    W2 = 2 * HEAD_A
    scale = HEAD_A ** -0.5

    def body(q_ref, k_ref, v_ref, b_ref, o_ref, l_ref):
        win = pl.ds(_window_start(pl.program_id(1), nblk), WIN_A)
        second = _lane_is_second_head((QB_A, W2))
        pairs = range(HEADS_A // 2)
        cols = [slice(hp * W2, (hp + 1) * W2) for hp in pairs]
        s = [lax.dot_general(_stack_heads(q_ref[:, cols[hp]], second), k_ref[win, cols[hp]], (NT, ((), ())),
                             preferred_element_type=F32) * scale + b_ref[2 * hp:2 * hp + 2].reshape(2 * QB_A, WIN_A)
             for hp in pairs]
        m = [jnp.max(x, axis=-1, keepdims=True) for x in s]
        p = [jnp.exp(x - mx) for x, mx in zip(s, m)]
        l = [jnp.sum(x, axis=-1, keepdims=True) for x in p]
        res = [jnp.dot(p[hp].astype(BF16), v_ref[win, cols[hp]], preferred_element_type=F32) / l[hp] for hp in pairs]
        o_ref[...] = jnp.concatenate([_unstack_heads(x, second) for x in res], axis=1).astype(o_ref.dtype)
        l_ref[...] = jnp.concatenate([_unstack_heads(jnp.broadcast_to(mx + jnp.log(lx), (2 * QB_A, W2)), second)
                                      for mx, lx in zip(m, l)], axis=1)

    in_specs = [_bs((QB_A, WIDTH_A), lambda r, n: (n, colblk(0, r))),
                _bs((L, WIDTH_A), lambda r, n: (0, colblk(1, r))), _bs((L, WIDTH_A), lambda r, n: (0, colblk(2, r))),
                _bs((None, HEADS_A, QB_A, WIN_A), lambda r, n: (_window_variant(n, nblk), 0, 0, 0))]
    o, lse = pl.pallas_call(
        body, out_shape=[jax.ShapeDtypeStruct((L, d * WIDTH_A), BF16), jax.ShapeDtypeStruct((L, d * WIDTH_A), F32)],
        grid=(d, nblk), in_specs=in_specs,
        out_specs=[_bs((QB_A, WIDTH_A), lambda r, n: (n, r)), _bs((QB_A, WIDTH_A), lambda r, n: (n, r))],
        compiler_params=_params("parallel", "parallel"), name=f"a_fwd_d{d}")(pv, pv, pv, bias)
    return o, lse


def _dil_bwd(view_qkv, bias, do, lse, cterm, d):
    pv, colblk = view_qkv
    L = pv.shape[0]
    nblk = L // QB_A
    W2 = 2 * HEAD_A
    PPS = 4
    WS = PPS * W2
    ob = WIDTH_A // WS
    scale = HEAD_A ** -0.5

    def body(q_ref, k_ref, v_ref, do_ref, l_ref, c_ref, b_ref, dq_ref, dk_ref, dv_ref, db_ref):
        r, n = pl.program_id(1), pl.program_id(2)

        @pl.when(n == 0)
        def _():
            dk_ref[...] = jnp.zeros_like(dk_ref)
            dv_ref[...] = jnp.zeros_like(dv_ref)

        @pl.when((n == 0) & (r == 0))
        def _():
            db_ref[...] = jnp.zeros_like(db_ref)

        second = _lane_is_second_head((QB_A, W2))
        win = pl.ds(_window_start(n, nblk), WIN_A)
        variant = _window_variant(n, nblk)
        pairs = range(PPS)
        cols = [slice(pp * W2, (pp + 1) * W2) for pp in pairs]

        def head_rows(ref, pp):
            v2 = ref[:, cols[pp]]
            return jnp.concatenate([v2[:, 0:1], v2[:, HEAD_A:HEAD_A + 1]], axis=0)

        kw = [k_ref[win, c] for c in cols]
        vw = [v_ref[win, c] for c in cols]
        qs = [_stack_heads(q_ref[:, c], second) for c in cols]
        dos = [_stack_heads(do_ref[:, c], second) for c in cols]
        s = [lax.dot_general(qs[pp], kw[pp], (NT, ((), ())), preferred_element_type=F32) for pp in pairs]
        dp = [lax.dot_general(dos[pp], vw[pp], (NT, ((), ())), preferred_element_type=F32) for pp in pairs]
        p = [jnp.exp(s[pp] * scale + b_ref[2 * pp:2 * pp + 2].reshape(2 * QB_A, WIN_A) - head_rows(l_ref, pp)) for pp in pairs]
        ds = [p[pp] * (dp[pp] + head_rows(c_ref, pp)) for pp in pairs]
        db_ref[variant] += jnp.concatenate([x.reshape(2, QB_A, WIN_A) for x in ds], axis=0)
        pb = [x.astype(BF16) for x in p]
        dsb = [(x * scale).astype(BF16) for x in ds]
        dq_ref[...] = jnp.concatenate([_unstack_heads(jnp.dot(dsb[pp], kw[pp], preferred_element_type=F32), second)
                                       for pp in pairs], axis=1).astype(dq_ref.dtype)
        dk_ref[win, :] += jnp.concatenate([lax.dot_general(dsb[pp], qs[pp], (TN, ((), ())), preferred_element_type=F32)
                                           for pp in pairs], axis=1)
        dv_ref[win, :] += jnp.concatenate([lax.dot_general(pb[pp], dos[pp], (TN, ((), ())), preferred_element_type=F32)
                                           for pp in pairs], axis=1)

    kv_spec = _resident if d == 1 else _bs
    in_specs = [_bs((QB_A, WS), lambda hp, r, n: (n, colblk(0, r) * ob + hp)),
                kv_spec((L, WS), lambda hp, r, n: (0, colblk(1, r) * ob + hp)),
                kv_spec((L, WS), lambda hp, r, n: (0, colblk(2, r) * ob + hp))]
    in_specs += [_bs((QB_A, WS), lambda hp, r, n: (n, r * ob + hp))] * 3
    in_specs += [_bs((None, 2 * PPS, QB_A, WIN_A), lambda hp, r, n: (_window_variant(n, nblk), hp, 0, 0))]
    out_shape = [jax.ShapeDtypeStruct((L, d * WIDTH_A), BF16), jax.ShapeDtypeStruct((L, d * WIDTH_A), F32),
                 jax.ShapeDtypeStruct((L, d * WIDTH_A), F32), jax.ShapeDtypeStruct((3, HEADS_A, QB_A, WIN_A), F32)]
    out_specs = [_bs((QB_A, WS), lambda hp, r, n: (n, r * ob + hp)),
                 _bs((L, WS), lambda hp, r, n: (0, r * ob + hp)), _bs((L, WS), lambda hp, r, n: (0, r * ob + hp)),
                 _bs((3, 2 * PPS, QB_A, WIN_A), lambda hp, r, n: (0, hp, 0, 0))]
    dq, dk, dv, db = pl.pallas_call(
        body, out_shape=out_shape, grid=(ob, d, nblk), in_specs=in_specs, out_specs=out_specs,
        compiler_params=_params("arbitrary", "arbitrary", "arbitrary"), name=f"a_bwd_d{d}")(
            pv, pv, pv, do, lse, cterm, bias)
    return dq, dk, dv, db


def _assemble_dproj(a_parts, dq_b, dk_b, dv_b, dga, dgb):
    T = dq_b.shape[0]
    flat = [(a_parts[part][g], d) for part in range(3) for g, d in enumerate(DILATIONS)]
    rest = [dq_b, dk_b, dv_b, dga, dgb]

    def body(*refs):
        views, others = refs[:len(flat)], refs[len(flat):len(flat) + len(rest)]
        o_ref, chunks = refs[len(flat) + len(rest)], refs[len(flat) + len(rest) + 1:]
        col = 0
        for v_ref, (_, d) in zip(views, flat):
            _view_to_rows(v_ref, o_ref, col, d, chunks)
            col += WIDTH_A
        for x_ref in others:
            w = x_ref.shape[1]
            o_ref[:, col:col + w] = x_ref[...].astype(o_ref.dtype)
            col += w

    in_specs = [_bs((VIEW_ROWS // d, d * WIDTH_A), lambda i: (i, 0)) for _, d in flat]
    in_specs += [_bs((VIEW_ROWS, x.shape[1]), lambda i: (i, 0)) for x in rest]
    return pl.pallas_call(
        body, out_shape=jax.ShapeDtypeStruct((T, IN_WIDTH), BF16), grid=(T // VIEW_ROWS,), in_specs=in_specs,
        out_specs=_bs((VIEW_ROWS, IN_WIDTH), lambda i: (i, 0)), scratch_shapes=_view_chunks(),
        compiler_params=_params("parallel"), name="mix_bwd_dproj")(*[a for a, _ in flat], *rest)


def _segment_ones():
    i = np.arange(WIDTH_A)
    return jnp.asarray((i[:, None] // HEAD_A == i[None, :] // HEAD_A).astype(np.float32), dtype=BF16)


def _group_weights(l0, l1, l2):
    m = jnp.maximum(jnp.maximum(l0, l1), l2)
    e = [jnp.exp(l - m) for l in (l0, l1, l2)]
    z = e[0] + e[1] + e[2]
    return [ei / z for ei in e]


def _view_specs():
    return [_bs((VIEW_ROWS // d, d * WIDTH_A), lambda i: (i, 0)) for d in DILATIONS]


def _stage_tiles(n):
    return [pltpu.VMEM((VIEW_ROWS, WIDTH_A), F32)] * n


def _combine_fwd(outs, lses):
    T = outs[0].shape[0] * DILATIONS[0]
    n = len(DILATIONS)

    def body(*refs):
        o_refs, l_refs, oa_ref = refs[:n], refs[n:2 * n], refs[2 * n]
        o_st, l_st, chunks = refs[2 * n + 1:3 * n + 1], refs[3 * n + 1:4 * n + 1], refs[4 * n + 1:]
        for g, d in enumerate(DILATIONS):
            _view_to_rows(o_refs[g], o_st[g], 0, d, chunks)
            _view_to_rows(l_refs[g], l_st[g], 0, d, chunks)
        w = _group_weights(*[l[...] for l in l_st])
        oa_ref[...] = (w[0] * o_st[0][...] + w[1] * o_st[1][...] + w[2] * o_st[2][...]).astype(oa_ref.dtype)

    return pl.pallas_call(
        body, out_shape=jax.ShapeDtypeStruct((T, WIDTH_A), BF16), grid=(T // VIEW_ROWS,),
        in_specs=_view_specs() * 2, out_specs=_bs((VIEW_ROWS, WIDTH_A), lambda i: (i, 0)),
        scratch_shapes=_stage_tiles(2 * n) + _view_chunks(), compiler_params=_params("parallel"), name="a_combine")(*outs, *lses)


def _combine_bwd(doa, outs, lses):
    T = doa.shape[0]
    n = len(DILATIONS)

    def body(*refs):
        d_ref, o_refs, l_refs, seg_ref = refs[0], refs[1:n + 1], refs[n + 1:2 * n + 1], refs[2 * n + 1]
        do_refs, c_refs = refs[2 * n + 2:3 * n + 2], refs[3 * n + 2:4 * n + 2]
        o_st, l_st = refs[4 * n + 2:5 * n + 2], refs[5 * n + 2:6 * n + 2]
        tmp, chunks = refs[6 * n + 2], refs[6 * n + 3:]
        for g, d in enumerate(DILATIONS):
            _view_to_rows(o_refs[g], o_st[g], 0, d, chunks)
            _view_to_rows(l_refs[g], l_st[g], 0, d, chunks)
        dv = d_ref[...].astype(F32)
        w = _group_weights(*[l[...] for l in l_st])
        seg = seg_ref[...]
        tot = jnp.zeros(dv.shape, F32)
        for g in range(n):
            prod = w[g] * dv * o_st[g][...]
            hi = prod.astype(BF16)
            lo = (prod - hi.astype(F32)).astype(BF16)
            tot = tot + jnp.dot(hi, seg, preferred_element_type=F32) + jnp.dot(lo, seg, preferred_element_type=F32)
        for g, d in enumerate(DILATIONS):
            tmp[...] = w[g] * dv
            _rows_to_view(tmp, 0, do_refs[g], 0, d, chunks)
            tmp[...] = -w[g] * tot
            _rows_to_view(tmp, 0, c_refs[g], 0, d, chunks)

    views = [jax.ShapeDtypeStruct((T // d, d * WIDTH_A), dt) for dt in (BF16, F32) for d in DILATIONS]
    res = pl.pallas_call(
        body, out_shape=views, grid=(T // VIEW_ROWS,),
        in_specs=[_bs((VIEW_ROWS, WIDTH_A), lambda i: (i, 0))] + _view_specs() * 2 + [_bs((WIDTH_A, WIDTH_A), lambda i: (0, 0))],
        out_specs=_view_specs() * 2, scratch_shapes=_stage_tiles(2 * n + 1) + _view_chunks(),
        compiler_params=_params("parallel"), name="a_combine_bwd")(doa, *outs, *lses, _segment_ones())
    return res[:n], res[n:]


def _rope_tables(T):
    rows = T // GRID_W
    row = jnp.repeat(jnp.arange(rows, dtype=F32), GRID_W)
    col = jnp.tile(jnp.arange(GRID_W, dtype=F32), rows)
    n_freq = HEAD_B // 4
    freq = ROPE_THETA ** (-jnp.arange(n_freq, dtype=F32) / n_freq)
    ang = jnp.concatenate([row[:, None] * freq, col[:, None] * freq], axis=-1)
    cos, sin = jnp.repeat(jnp.cos(ang), 2, axis=1), jnp.repeat(jnp.sin(ang), 2, axis=1)
    sign = jnp.where(jnp.arange(HEAD_B) % 2 == 0, -1.0, 1.0).astype(F32)
    return cos, sin * sign


def _swap_pairs(v):
    even = lax.broadcasted_iota(jnp.int32, v.shape, v.ndim - 1) % 2 == 0
    n = v.shape[-1]
    return jnp.where(even, pltpu.roll(v, n - 1, v.ndim - 1), pltpu.roll(v, 1, v.ndim - 1))


def _qk_fwd(name, proj, col0, n_heads, gain, cos, sin, out_scale=1.0, deps=()):
    T = proj.shape[0]

    def fn(xr, g, c, s):
        xn = _norm_fwd(xr.astype(F32), g)
        return (xn * c + _swap_pairs(xn) * s) * out_scale

    (out,) = _ew(name, fn, [_tiled(proj, HEAD_B, col0 // HEAD_B), _whole(gain), _table(cos), _table(sin)],
                 [(BF16, HEAD_B)], n_rows=T, rows=2048, ncols=n_heads, deps=deps)
    return out


def _qk_bwd(name, dout, proj, col0, n_heads, gain, cos, sin, in_scale=1.0):
    T = proj.shape[0]

    def fn(dv, xr, g, c, s):
        dv = dv.astype(F32) * in_scale
        dxn = c * dv + _swap_pairs(s * dv)
        dx, dgr = _norm_bwd(xr.astype(F32), g, dxn)
        return dx, _colsum(dgr)

    dx, dg = _ew(name, fn, [_tiled(dout, HEAD_B, 0), _tiled(proj, HEAD_B, col0 // HEAD_B), _whole(gain),
                            _table(cos), _table(sin)],
                 [(BF16, HEAD_B)], n_rows=T, rows=2048, reds=(HEAD_B,), ncols=n_heads)
    return dx, jnp.sum(dg, axis=0)


def _gqa_fwd(qn, kn, proj, k_col=0):
    T = qn.shape[0]
    GW = 4 * HEAD_B
    QB = QB_B

    def body(q_ref, k_ref, v_ref, o_ref, l_ref):
        k, v = k_ref[...], v_ref[...]
        lane = lax.broadcasted_iota(jnp.int32, (QB, HEAD_B), 1)
        heads = range(4)
        s = [lax.dot_general(q_ref[:, g * HEAD_B:(g + 1) * HEAD_B], k, (NT, ((), ())), preferred_element_type=F32)
             for g in heads]
        m = [jnp.max(x, axis=-1, keepdims=True) for x in s]
        p = [jnp.exp2(x - mx) for x, mx in zip(s, m)]
        l = [jnp.sum(x, axis=-1, keepdims=True) for x in p]
        o = [jnp.dot(p[g].astype(BF16), v, preferred_element_type=F32) / l[g] for g in heads]
        o_ref[...] = jnp.concatenate(o, axis=1).astype(o_ref.dtype)
        lse_all = jnp.zeros((QB, HEAD_B), F32)
        for g in heads:
            lse_all = jnp.where(lane == g, m[g] + jnp.log2(l[g]), lse_all)
        l_ref[...] = lse_all

    return pl.pallas_call(
        body, out_shape=[jax.ShapeDtypeStruct((T, 2 * GW), BF16), jax.ShapeDtypeStruct((2, T, HEAD_B), F32)],
        grid=(2, T // QB),
        in_specs=[_bs((QB, GW), lambda kv, i: (i, kv)), _bs((T, HEAD_B), lambda kv, i: (0, k_col + kv)),
                  _bs((T, HEAD_B), lambda kv, i: (0, B_V // HEAD_B + kv))],
        out_specs=[_bs((QB, GW), lambda kv, i: (i, kv)), _bs((None, QB, HEAD_B), lambda kv, i: (kv, i, 0))],
        compiler_params=_params("parallel", "parallel"), name="b_fwd")(qn, kn, proj)


def _gqa_bwd(qn, kn, proj, o, lse, do, deps=(), k_col=0):
    T = qn.shape[0]
    GW = 4 * HEAD_B

    def body(q_ref, k_ref, v_ref, o_ref, l_ref, do_ref, *rest):
        dq_ref, dk_ref, dv_ref = rest[-3:]
        i = pl.program_id(1)

        @pl.when(i == 0)
        def _():
            dk_ref[...] = jnp.zeros_like(dk_ref)
            dv_ref[...] = jnp.zeros_like(dv_ref)

        k, v = k_ref[...], v_ref[...]
        lse_all = l_ref[...]
        for g in range(4):
            cols = slice(g * HEAD_B, (g + 1) * HEAD_B)
            q, dob = q_ref[:, cols], do_ref[:, cols]
            delta = jnp.sum(dob.astype(F32) * o_ref[:, cols].astype(F32), axis=-1, keepdims=True)
            s = lax.dot_general(q, k, (NT, ((), ())), preferred_element_type=F32)
            p = jnp.exp2(s - lse_all[:, g:g + 1])
            dp = lax.dot_general(dob, v, (NT, ((), ())), preferred_element_type=F32)
            ds = (p * (dp - delta)).astype(BF16)
            dq_ref[:, cols] = jnp.dot(ds, k, preferred_element_type=F32).astype(dq_ref.dtype)
            dk_ref[...] += lax.dot_general(ds, q, (TN, ((), ())), preferred_element_type=F32)
            dv_ref[...] += lax.dot_general(p.astype(BF16), dob, (TN, ((), ())), preferred_element_type=F32)

    return pl.pallas_call(
        body, out_shape=[jax.ShapeDtypeStruct((T, 2 * GW), BF16), jax.ShapeDtypeStruct((T, 2 * HEAD_B), F32),
                         jax.ShapeDtypeStruct((T, 2 * HEAD_B), F32)],
        grid=(2, T // QB_B),
        in_specs=[_bs((QB_B, GW), lambda kv, i: (i, kv)), _bs((T, HEAD_B), lambda kv, i: (0, k_col + kv)),
                  _bs((T, HEAD_B), lambda kv, i: (0, B_V // HEAD_B + kv)), _bs((QB_B, GW), lambda kv, i: (i, kv)),
                  _bs((None, QB_B, HEAD_B), lambda kv, i: (kv, i, 0)), _bs((QB_B, GW), lambda kv, i: (i, kv))] + _any_specs(len(deps)),
        out_specs=[_bs((QB_B, GW), lambda kv, i: (i, kv)), _bs((T, HEAD_B), lambda kv, i: (0, kv)),
                   _bs((T, HEAD_B), lambda kv, i: (0, kv))],
        compiler_params=_params("parallel", "arbitrary"), name="b_bwd")(qn, kn, proj, o, lse, do, *deps)


def _local_step(x, target, small, get_w, put_g, deps=(), prefetch_w=lambda name, after: [], take_rider=lambda steps, after: None):
    T, D = x.shape
    gs = {}

    bias = _bias_tiles(small["rel_bias"])
    cos, sin = _rope_tables(T)
    (x1, h2), ffn1_saved = _ffn_fwd("ffn1", x, small["ffn1_norm"], lambda name, after: get_w(name, [after, bias, cos, sin]), deps,
                                    tail_ins=[small["mix_norm"]], tail_fn=lambda y, g: (y, _norm_fwd(y, g)), tail_outs=(F32, BF16))
    w_in = get_w("w_in", h2)
    nq = w_in.shape[2]
    tpq = nq // WIDTH_A

    def proj_tile(j, k):
        c = j * tpq + k
        return jnp.where(c < 3 * len(DILATIONS), (c % 3) * 3 + c // 3, c)

    proj = _mm("mix_in", (4, tpq),
               [(h2, _resident((T, D), lambda j, k: (0, 0)), w_in, _bs((None, D, WIDTH_A), lambda j, k: (j, 0, k)))],
               jax.ShapeDtypeStruct((T, IN_WIDTH), BF16), _bs((T, WIDTH_A), lambda j, k: (0, proj_tile(j, k))), NN)

    a_views = [_group_view(proj, grp, d) for grp, d in enumerate(DILATIONS)]
    a_outs, a_lses = [], []
    for grp, d in enumerate(DILATIONS):
        o, l = _dil_fwd(a_views[grp], bias[grp], d)
        a_outs.append(o)
        a_lses.append(l)
    o_a = _combine_fwd(a_outs, a_lses)

    qk_gain = jnp.concatenate([jnp.tile(small["q_norm"] * QK_SCALE_LOG2, (8, 1)), jnp.tile(small["k_norm"], (2, 1))])[:, None, :]
    qkn = _qk_fwd("b_qknorm", proj, B_Q, 10, qk_gain, cos, sin, deps=prefetch_w("w_branch_a", proj))
    qn, kn, k_col = qkn, qkn, 8
    o_b, lse_b = _gqa_fwd(qn, kn, proj, k_col)
    ahead = prefetch_w("ffn2_w1", o_b)

    wa, wb, wo = get_w("w_branch_a", o_b), get_w("w_branch_b", o_b), get_w("w_out", o_b)
    bg_a, bg_b = small["b_gate"][:, :D], small["b_gate"][:, D:]
    n_a = wa.shape[0]

    def merge_out(oa_ref, ob_ref, ga_ref, gb_ref, x1_ref, wa_ref, wb_ref, wo_ref, ba_ref, bb_ref, g2_ref, *rest):
        ta_ref, tb_ref, mg_ref, x2_ref, hn_ref = rest[-5:]
        oa = oa_ref[...]
        ta = jnp.concatenate([jnp.dot(oa, wa_ref[j], preferred_element_type=F32) for j in range(n_a)], axis=1)
        tb = jnp.dot(ob_ref[...], wb_ref[...], preferred_element_type=F32)
        sa = _sigmoid(ga_ref[...].astype(F32) + ba_ref[...])
        sb = _sigmoid(gb_ref[...].astype(F32) + bb_ref[...])
        merged = (sa * ta + sb * tb).astype(BF16)
        ta_ref[...], tb_ref[...], mg_ref[...] = ta.astype(BF16), tb.astype(BF16), merged
        y = x1_ref[...] + jnp.dot(merged, wo_ref[...], preferred_element_type=F32)
        x2_ref[...] = y
        hn_ref[...] = _norm_fwd(y, g2_ref[...]).astype(BF16)

    row = _bs((512, D), lambda i: (i, 0))
    gate_specs = [_bs((512, D), lambda i: (i, G_A // D)), _bs((512, D), lambda i: (i, G_B // D))]
    whole2, whole3 = (lambda i: (0, 0)), (lambda i: (0, 0, 0))
    vec = _bs((1, D), whole2)
    t_a, t_b, merged, x2, hn2 = pl.pallas_call(
        merge_out, out_shape=[jax.ShapeDtypeStruct((T, D), BF16)] * 3 + [jax.ShapeDtypeStruct((T, D), F32), jax.ShapeDtypeStruct((T, D), BF16)],
        grid=(T // 512,),
        in_specs=[_bs((512, WIDTH_A), lambda i: (i, 0)), row] + gate_specs + [row, _resident(wa.shape, whole3), _resident((D, D), whole2),
                                                                                _resident((D, D), whole2), vec, vec, vec]
        + _any_specs(len(ahead)),
        out_specs=[row] * 5, compiler_params=_params("parallel"), name="mix_merge_out")(
            o_a, o_b, proj, proj, x1, wa, wb, wo, bg_a, bg_b, small["ffn2_norm"], *ahead)

    def head(xv, g, tv):
        r = _rstd(xv)
        xh = xv * r
        e = xh * g - tv
        dy = e * (1.0 / D)
        dxh = dy * g
        dx = r * (dxh - xh * jnp.mean(dxh * xh, axis=-1, keepdims=True))
        return dx, 0.5 * dx, _colsum(e * e) * (0.5 / D), _colsum(dy * xh)

    (dx3, dx3_half, loss_cols, g_final), ffn2_saved = _ffn_fwd(
        "ffn2", x2, small["ffn2_norm"], get_w, h=hn2, tail_ins=[small["final_norm"].reshape(1, D), target], tail_fn=head,
        tail_outs=(F32, BF16), tail_reds=(D, D))
    gs["final_norm"] = g_final.reshape(D)

    dx2, _, dmix, gs["ffn2_norm"] = _ffn_bwd("ffn2", x2, small["ffn2_norm"], get_w, put_g, ffn2_saved, dx3, dx3_half,
                                             also_bf16=True)
    g_out = _mm_wgrad("mix_bwd_dwout", merged, dmix, a_cols=D // 4, b_cols=None, tm=256, tn=512, J=4).reshape(D, D)

    def merge_out_bwd(dx_ref, ta_ref, tb_ref, ga_ref, gb_ref, wa_ref, wb_ref, wo_ref, ba_ref, bb_ref,
                      dta_ref, dtb_ref, dga_ref, dgb_ref, doa_ref, dob_ref, dba_ref, dbb_ref):
        dm = lax.dot_general(dx_ref[...], wo_ref[...], (NT, ((), ())), preferred_element_type=F32)
        ta, tb = ta_ref[...].astype(F32), tb_ref[...].astype(F32)
        sa = _sigmoid(ga_ref[...].astype(F32) + ba_ref[...])
        sb = _sigmoid(gb_ref[...].astype(F32) + bb_ref[...])
        dga, dgb = dm * ta * sa * (1.0 - sa), dm * tb * sb * (1.0 - sb)
        dta, dtb = (dm * sa).astype(BF16), (dm * sb).astype(BF16)
        dta_ref[...], dtb_ref[...] = dta, dtb
        dga_ref[...], dgb_ref[...] = dga.astype(BF16), dgb.astype(BF16)
        w = wa_ref.shape[2]
        doa = sum(lax.dot_general(dta[:, j * w:(j + 1) * w], wa_ref[j], (NT, ((), ())), preferred_element_type=F32) for j in range(n_a))
        doa_ref[...] = doa.astype(BF16)
        dob_ref[...] = lax.dot_general(dtb, wb_ref[...], (NT, ((), ())), preferred_element_type=F32).astype(BF16)

        @pl.when(pl.program_id(0) == 0)
        def _():
            dba_ref[...] = jnp.zeros_like(dba_ref)
            dbb_ref[...] = jnp.zeros_like(dbb_ref)
        dba_ref[...] += _colsum(dga)
        dbb_ref[...] += _colsum(dgb)

    rowb = _bs((256, D), lambda i: (i, 0))
    gate_specs = [_bs((256, D), lambda i: (i, G_A // D)), _bs((256, D), lambda i: (i, G_B // D))]
    dta, dtb, dga, dgb, do_a, do_b, dba, dbb = pl.pallas_call(
        merge_out_bwd,
        out_shape=[jax.ShapeDtypeStruct((T, D), BF16)] * 4 + [jax.ShapeDtypeStruct((T, WIDTH_A), BF16), jax.ShapeDtypeStruct((T, D), BF16)]
        + [jax.ShapeDtypeStruct((1, D), F32)] * 2,
        grid=(T // 256,),
        in_specs=[rowb, rowb, rowb] + gate_specs + [_resident(wa.shape, whole3), _resident((D, D), whole2), _resident((D, D), whole2), vec, vec],
        out_specs=[rowb] * 4 + [_bs((256, WIDTH_A), lambda i: (i, 0)), rowb, vec, vec],
        compiler_params=_params("arbitrary"), name="mix_merge_out_bwd")(dmix, t_a, t_b, proj, proj, wa, wb, wo, bg_a, bg_b)
    gs["b_gate"] = jnp.concatenate([dba, dbb], axis=1)

    g_a = _mm_wgrad("mix_bwd_dwa", o_a, dta, a_cols=None, b_cols=D // 4, tm=WIDTH_A, tn=256, J=4)
    g_b = _mm_wgrad("mix_bwd_dwb", o_b, dtb, a_cols=D // 4, b_cols=None, tm=256, tn=512, J=4).reshape(D, D)
    deps = put_g({"w_out": g_out, "w_branch_a": g_a, "w_branch_b": g_b})

    dqn, dkn, dv_b = _gqa_bwd(qn, kn, proj, o_b, lse_b, do_b, deps, k_col)
    dq_b, gs["q_norm"] = _qk_bwd("b_bwd_qnorm", dqn, proj, B_Q, 8, small["q_norm"], cos, sin, in_scale=HEAD_B ** -0.5)
    dk_b, gs["k_norm"] = _qk_bwd("b_bwd_knorm", dkn, proj, B_K, 2, small["k_norm"], cos, sin, in_scale=1.0 / LOG2_E)

    do_groups, c_groups = _combine_bwd(do_a, a_outs, a_lses)
    dqs, dks, dvs, dbs = [], [], [], []
    for grp, d in enumerate(DILATIONS):
        dq, dk, dv, db = _dil_bwd(a_views[grp], bias[grp], do_groups[grp], a_lses[grp], c_groups[grp], d)
        dqs.append(dq), dks.append(dk), dvs.append(dv), dbs.append(db)
    gs["rel_bias"] = _bias_grad(jnp.stack(dbs))

    dproj = _assemble_dproj([dqs, dks, dvs], dq_b, dk_b, dv_b, dga, dgb)
    nq = w_in.shape[2]
    g_in = _mm("mix_bwd_dwin", (4, tpq),
               [(h2, _resident((T, D), lambda j, k: (0, 0)), dproj, _bs((T, WIDTH_A), lambda j, k: (0, j * tpq + k)))],
               jax.ShapeDtypeStruct((4, D, nq), BF16), _bs((None, D, WIDTH_A), lambda j, k: (j, 0, k)), TN)
    deps = put_g({"w_in": g_in})
    dx1, dx1_half, gs["mix_norm"] = _dh_norm_bwd(
        "mix_bwd_dh", 256,
        [(dproj, _bs((256, nq), lambda i, j=j: (i, j)), w_in, _resident((None, D, nq), lambda i, j=j: (j, 0, 0))) for j in range(4)],
        NT, x1, small["mix_norm"], dx2, deps)

    dx0, _, gs["ffn1_norm"] = _ffn_bwd("ffn1", x, small["ffn1_norm"], get_w, put_g, ffn1_saved, dx1, dx1_half, last=True,
                                       take_rider=take_rider)
    return loss_cols, dx0, gs


def _position():
    return lax.axis_index("x"), lax.axis_index("y"), lax.axis_index("c")


def _any_specs(n):
    return [pl.BlockSpec(memory_space=pl.ANY)] * n


HBM_SPEC = pl.BlockSpec(memory_space=pltpu.HBM)
SEM_SPEC = pl.BlockSpec(memory_space=pltpu.SEMAPHORE)
DATAFLOW_EFFECT = pltpu.SideEffectType.DATAFLOW_SIDE_EFFECTING
N_PEER_CHIPS = 3
LANES = 128


def _quarter_copies(srcs, lands, send_sems, recv_sems, mode):
    x, y, c = _position()
    me = 2 * x + y
    peers = [(1 - x, y, c), (x, 1 - y, c), (1 - x, 1 - y, c)]
    copies = []
    for src, land, send, recv in zip(srcs, lands, send_sems, recv_sems):
        if mode == "sibling":
            copies.append(pltpu.make_async_remote_copy(src_ref=src, dst_ref=land, send_sem=send.at[0], recv_sem=recv.at[0],
                                                       device_id=(x, y, 1 - c), device_id_type=MESH))
            continue
        if mode == "fill":
            half = land.shape[1] // 2
            for p, (px, py, _) in enumerate(peers):
                part = land.at[2 * px + py, pl.ds(c * half, half)]
                copies.append(pltpu.make_async_remote_copy(src_ref=part, dst_ref=part, send_sem=send.at[p], recv_sem=recv.at[p],
                                                           device_id=(x, y, 1 - c), device_id_type=MESH))
            continue
        scatter = mode == "scatter"
        half = land.shape[1] // 2
        mine = land.at[me, pl.ds(c * half, half)]
        for p, (px, py, pc) in enumerate(peers):
            copies.append(pltpu.make_async_remote_copy(
                src_ref=src.at[2 * px + py] if scatter else mine, dst_ref=land.at[me] if scatter else mine,
                send_sem=send.at[p], recv_sem=recv.at[p], device_id=(px, py, pc), device_id_type=MESH))
    return copies


def _fill_from_sibling(name, stacks):
    n = len(stacks)

    def body(*refs):
        outs = refs[n:2 * n]
        send_sems, recv_sems = refs[2 * n:]
        x, y, c = _position()
        copies = []
        for i, ref in enumerate(outs):
            half = ref.shape[1] // 2
            rows = pl.ds(c * half, half)
            for p, k in enumerate((2 * (1 - x) + y, 2 * x + (1 - y), 2 * (1 - x) + (1 - y))):
                cp = pltpu.make_async_remote_copy(ref.at[k, rows], ref.at[k, rows], send_sems.at[3 * i + p], recv_sems.at[3 * i + p],
                                                  device_id=(x, y, 1 - c), device_id_type=MESH)
                cp.start()
                copies.append(cp)
        for cp in copies:
            cp.wait()

    return pl.pallas_call(
        body, out_shape=[jax.ShapeDtypeStruct(s.shape, s.dtype) for s in stacks],
        in_specs=_any_specs(n), out_specs=_any_specs(n), input_output_aliases={i: i for i in range(n)},
        scratch_shapes=[pltpu.SemaphoreType.DMA((N_PEER_CHIPS * n,)), pltpu.SemaphoreType.DMA((N_PEER_CHIPS * n,))],
        compiler_params=pltpu.CompilerParams(has_side_effects=True), name=name)(*stacks)


def _exchange_start(name, srcs, lands, mode):
    n = len(lands)
    arrays = list(lands) if srcs is None else list(srcs) + list(lands)
    k = len(arrays)

    def body(*refs):
        land_refs = refs[k - n:k]
        send_sems, recv_sems = refs[k:k + n], refs[k + n:k + 2 * n]
        token = refs[2 * k + 2 * n]
        for cp in _quarter_copies(refs[:n], land_refs, send_sems, recv_sems, mode):
            cp.start()
        token[...] = jnp.zeros_like(token)

    sem = pltpu.SemaphoreType.DMA((N_PEER_CHIPS,))
    out_shape = [sem] * (2 * n) + [pltpu.HBM(a.shape, a.dtype) for a in arrays] + [jax.ShapeDtypeStruct((8, LANES), F32)]
    res = pl.pallas_call(
        body, name=name, out_shape=out_shape, in_specs=[HBM_SPEC] * k,
        out_specs=[SEM_SPEC] * (2 * n) + [HBM_SPEC] * k + [pl.BlockSpec(memory_space=pltpu.VMEM)],
        input_output_aliases={i: 2 * n + i for i in range(k)},
        compiler_params=pltpu.CompilerParams(has_side_effects=DATAFLOW_EFFECT),
    )(*[pltpu.with_memory_space_constraint(a, pltpu.HBM) for a in arrays])
    thru = res[2 * n:2 * n + k]
    return res[:n], res[n:2 * n], (None if srcs is None else thru[:n]), thru[k - n:], res[2 * n + k]


def _exchange_wait(name, srcs, lands, send_sems, recv_sems, after, mode):
    n = len(lands)
    arrays = list(lands) if srcs is None else list(srcs) + list(lands)
    k = len(arrays)
    after = list(after) if isinstance(after, (list, tuple)) else [after]

    def body(*refs):
        sends, recvs = refs[k:k + n], refs[k + n:k + 2 * n]
        for cp in _quarter_copies(refs[:n], refs[k - n:k], sends, recvs, mode):
            cp.wait_send()
            cp.wait_recv()

    res = pl.pallas_call(
        body, name=name, out_shape=[pltpu.HBM(a.shape, a.dtype) for a in arrays],
        in_specs=[HBM_SPEC] * k + [SEM_SPEC] * (2 * n) + _any_specs(len(after)),
        out_specs=[HBM_SPEC] * k, input_output_aliases={i: i for i in range(k)},
        compiler_params=pltpu.CompilerParams(has_side_effects=DATAFLOW_EFFECT),
    )(*arrays, *send_sems, *recv_sems, *after)
    return (None if srcs is None else res[:n]), res[k - n:]


def _own_slots(name, srcs, from_stack=False):
    n = len(srcs)
    me = (2 * lax.axis_index("x") + lax.axis_index("y")).astype(jnp.int32).reshape(1)

    def body(me_ref, *refs):
        for x_ref, o_ref in zip(refs[:n], refs[n:]):
            o_ref[...] = x_ref[...].astype(o_ref.dtype)

    in_specs, out_specs, out_shape = [], [], []
    for src in srcs:
        R, C = src.shape[-2:]
        in_specs.append(pl.BlockSpec((None, R // 2, C), lambda i, me_ref: (me_ref[0], i, 0)) if from_stack
                        else pl.BlockSpec((R // 2, C), lambda i, me_ref: (i, 0)))
        out_specs.append(pl.BlockSpec((None, R // 2, C), lambda i, me_ref: (me_ref[0], i, 0)))
        out_shape.append(jax.ShapeDtypeStruct((4, R, C), BF16))
    grid_spec = pltpu.PrefetchScalarGridSpec(num_scalar_prefetch=1, grid=(2,), in_specs=in_specs, out_specs=out_specs)
    return pl.pallas_call(body, out_shape=out_shape, grid_spec=grid_spec, compiler_params=_params("parallel"), name=name)(me, *srcs)


def _allreduce_small(buf):
    R, C = buf.shape
    flips = [(fx, fy, fc) for fx in (0, 1) for fy in (0, 1) for fc in (0, 1)][1:]

    def body(in_ref, out_ref, land_ref, send_sems, recv_sems):
        x, y, c = _position()
        me = 4 * x + 2 * y + c
        copies = []
        for k, (fx, fy, fc) in enumerate(flips):
            px, py, pc = (1 - x if fx else x), (1 - y if fy else y), (1 - c if fc else c)
            cp = pltpu.make_async_remote_copy(in_ref, land_ref.at[me], send_sems.at[k], recv_sems.at[k],
                                              device_id=(px, py, pc), device_id_type=MESH)
            cp.start()
            copies.append(cp)
        land_ref[me] = in_ref[...]
        for cp in copies:
            cp.wait()
        acc = land_ref[0]
        for k in range(1, 8):
            acc = acc + land_ref[k]
        out_ref[...] = acc

    return pl.pallas_call(
        body, out_shape=jax.ShapeDtypeStruct((R, C), F32),
        in_specs=[pl.BlockSpec(memory_space=pltpu.VMEM)], out_specs=pl.BlockSpec(memory_space=pltpu.VMEM),
        scratch_shapes=[pltpu.VMEM((8, R, C), F32), pltpu.SemaphoreType.DMA((7,)), pltpu.SemaphoreType.DMA((7,))],
        compiler_params=pltpu.CompilerParams(has_side_effects=True), name="allreduce_small")(buf)


def _adamw_math(w, g, m, v):
    m2 = ADAM_B1 * m + (1.0 - ADAM_B1) * g
    v2 = ADAM_B2 * v + (1.0 - ADAM_B2) * (g * g)
    m_hat = m2 / (1.0 - ADAM_B1 ** ADAM_STEP)
    v_hat = v2 / (1.0 - ADAM_B2 ** ADAM_STEP)
    delta = -ADAM_LR * (m_hat / (jnp.sqrt(v_hat) + ADAM_EPS) + ADAM_WD * w)
    return delta, m2, v2


def _adamw_from_partials(wv, mv, vv, *parts):
    def four(a, b, c, d):
        return ((a.astype(F32) + b.astype(F32)) + c.astype(F32)) + d.astype(F32)

    g = four(*parts[:4]) + four(*parts[4:])
    return (g,) + _adamw_math(wv, g, mv, vv)


def _adamw_big(name, w, m, v, mine, theirs):
    R, C = w.shape
    rows = 256 if R % 256 == 0 else R // 2
    nrb = R // rows
    slots = [_tiled(s.reshape(4 * R, C), None, 0, k * nrb) for s in (mine, theirs) for k in range(4)]
    return _ew(name, _adamw_from_partials, [_tiled(w), _tiled(m), _tiled(v)] + slots, [(F32, C)] * 4, n_rows=R, rows=rows)


def _adamw_rider(w, m, v, mine, theirs, steps, deliver):
    R, C = w.shape
    fits = [nb for nb in range(1, steps + 1) if R % nb == 0 and (R // nb) % 16 == 0]
    if not fits:
        return None
    nb = fits[-1]
    rows = R // nb

    def blocks(first):
        return pl.BlockSpec((rows, C), lambda *g: (first + jnp.minimum(g[0], nb - 1), 0))

    flat = [s.reshape(4 * R, C) for s in (mine, theirs)]
    return dict(operands=[w, m, v] + [f for f in flat for _ in range(4)],
                in_specs=[blocks(0)] * 3 + [blocks(k * nb) for _ in flat for k in range(4)],
                out_shape=[jax.ShapeDtypeStruct((R, C), F32)] * 4, out_specs=[blocks(0)] * 4,
                n_blocks=nb, fn=_adamw_from_partials, deliver=lambda outs: deliver(*outs))


BIG = ("ffn1_w1", "ffn1_w3", "ffn1_w2", "w_in", "w_branch_a", "w_branch_b", "w_out", "ffn2_w1", "ffn2_w3", "ffn2_w2")
SMALL = ("ffn1_norm", "mix_norm", "b_gate", "q_norm", "k_norm", "rel_bias", "ffn2_norm", "final_norm")
ORDER = ("ffn1_norm", "ffn1_w1", "ffn1_w3", "ffn1_w2", "mix_norm", "w_in", "b_gate", "q_norm", "k_norm", "rel_bias",
         "w_branch_a", "w_branch_b", "w_out", "ffn2_norm", "ffn2_w1", "ffn2_w3", "ffn2_w2", "final_norm")
TRANSPOSED = ("ffn1_w1", "ffn1_w3", "ffn2_w1", "ffn2_w3")
SIBLING_LAG = 2
LONG_HOST_STEPS = 8
GATHER_GROUPS = (("ffn1_w1", "ffn1_w3"), ("ffn1_w2",), ("w_in",), ("w_branch_a", "w_branch_b", "w_out"),
                 ("ffn2_w1", "ffn2_w3", "ffn2_w2"))


def _pack_small(d):
    rows = []
    for n in SMALL:
        flat = d[n].reshape(-1)
        pad = (-flat.shape[0]) % LANES
        rows.append(jnp.pad(flat, (0, pad)).reshape(-1, LANES))
    buf = jnp.concatenate(rows, axis=0)
    return jnp.pad(buf, ((0, (-buf.shape[0]) % 8), (0, 0)))


def _unpack_small(buf, like):
    out, r = {}, 0
    for n in SMALL:
        size = like[n].size
        nr = -(-size // LANES)
        out[n] = buf[r:r + nr].reshape(-1)[:size].reshape(like[n].shape)
        r += nr
    return out


def kernel(x, ffn1_norm, ffn1_w1, ffn1_w3, ffn1_w2, mix_norm, w_in, b_gate, q_norm, k_norm, rel_bias, w_branch_a, w_branch_b, w_out, ffn2_norm, ffn2_w1, ffn2_w3, ffn2_w2, final_norm, loss_target, m_ffn1_norm, m_ffn1_w1, m_ffn1_w3, m_ffn1_w2, m_mix_norm, m_w_in, m_b_gate, m_q_norm, m_k_norm, m_rel_bias, m_w_branch_a, m_w_branch_b, m_w_out, m_ffn2_norm, m_ffn2_w1, m_ffn2_w3, m_ffn2_w2, m_final_norm, v_ffn1_norm, v_ffn1_w1, v_ffn1_w3, v_ffn1_w2, v_mix_norm, v_w_in, v_b_gate, v_q_norm, v_k_norm, v_rel_bias, v_w_branch_a, v_w_branch_b, v_w_out, v_ffn2_norm, v_ffn2_w1, v_ffn2_w3, v_ffn2_w2, v_final_norm):
    given = dict(locals())
    w = {n: given[n] for n in ORDER}
    m = {n: given["m_" + n] for n in ORDER}
    v = {n: given["v_" + n] for n in ORDER}
    T, D = x.shape[1], x.shape[2]

    def stored(a, n):
        a = a.reshape(a.shape[1:])
        return a.T if n in TRANSPOSED else a

    def returned(a, n):
        return (a.T if n in TRANSPOSED else a).reshape(w[n].shape)

    quarter = {n: stored(w[n], n) for n in BIG}
    send, recv, _, land_thru, token = _exchange_start(
        "gather_start", None, _own_slots("own_weights", [quarter[n] for n in BIG]), "gather")
    index = {n: i for i, n in enumerate(BIG)}
    ready, filling = {}, {}

    def landed_halves(group, after):
        ids = [index[n] for n in group]
        return _exchange_wait("gather_wait_" + group[0], None, [land_thru[i] for i in ids],
                              [send[i] for i in ids], [recv[i] for i in ids], after, "gather")[1]

    def prefetch_w(name, after):
        group = next(g for g in GATHER_GROUPS if name in g)
        started = _exchange_start("fill_start_" + group[0], None, landed_halves(group, after), "fill")
        filling[group] = started
        return [started[4]]

    def get_w(name, after):
        if name not in ready:
            group = next(g for g in GATHER_GROUPS if name in g)
            if group in filling:
                f_send, f_recv, _, thru, _ = filling[group]
                stacks = _exchange_wait("fill_wait_" + group[0], None, thru, f_send, f_recv, after, "fill")[1]
            else:
                stacks = _fill_from_sibling("gather_fill_" + group[0], landed_halves(group, after))
            for n, st in zip(group, stacks):
                ready[n] = st.reshape(D, D) if n in ("w_branch_b", "w_out") else st
        return ready[name]

    scattered, forwarded = [], []

    def forward_oldest(after):
        names, s_sem, r_sem, srcs, lands = scattered.pop(0)
        _, landed = _exchange_wait("scatter_wait_" + names[0], srcs, lands, s_sem, r_sem, after, "scatter")
        started = _exchange_start("sibling_start_" + names[0], landed, [lax.empty(a.shape, a.dtype) for a in landed], "sibling")
        forwarded.append((names,) + tuple(started[:4]))
        return started[4]

    def put_g(grads):
        names = list(grads)
        stacks = [grads[n].reshape((4,) + quarter[n].shape) for n in names]
        lands = _own_slots("own_grad_" + names[0], stacks, from_stack=True)
        started = _exchange_start("scatter_start_" + names[0], stacks, lands, "scatter")
        scattered.append((names,) + tuple(started[:4]))
        tokens = [started[4]]
        if len(scattered) > SIBLING_LAG:
            tokens.append(forward_oldest(started[4]))
        return tokens

    grads, deltas, new_m, new_v = {}, {}, {}, {}
    arrived, riding = {}, set()

    def partials(gi, after):
        if gi not in arrived:
            names, s_sem, r_sem, srcs, lands = forwarded[gi]
            arrived[gi] = _exchange_wait("sibling_wait_" + names[0], srcs, lands, s_sem, r_sem, after, "sibling")
        return arrived[gi]

    def deliver_to(n):
        def deliver(*res):
            grads[n], deltas[n], new_m[n], new_v[n] = [returned(r, n) for r in res]
        return deliver

    def take_rider(steps, after):
        waiting = [(quarter[n].size, gi, k, n) for gi, entry in enumerate(forwarded) for k, n in enumerate(entry[0]) if n not in riding]
        for _, gi, k, n in sorted(waiting, reverse=steps >= LONG_HOST_STEPS):
            mine, theirs = partials(gi, after)
            rider = _adamw_rider(quarter[n], stored(m[n], n), stored(v[n], n), mine[k], theirs[k], steps, deliver_to(n))
            if rider is not None:
                riding.add(n)
                return rider
        return None

    small = {n: w[n] for n in SMALL}
    packed = [_pack_small({n: d[n] for n in SMALL}) for d in (w, m, v)]
    loss_cols, grad_x, gs = _local_step(x.reshape(T, D), loss_target.reshape(T, D), small, get_w, put_g, deps=[token] + packed,
                                        prefetch_w=prefetch_w, take_rider=take_rider)

    after = grad_x
    while scattered:
        after = forward_oldest(after)
    for gi, entry in enumerate(forwarded):
        mine, theirs = partials(gi, after)
        for n, a, b in zip(entry[0], mine, theirs):
            if n not in riding:
                deliver_to(n)(*_adamw_big(f"adamw_{n}", quarter[n], stored(m[n], n), stored(v[n], n), a, b))

    gs = {n: gs[n].reshape(w[n].shape) for n in SMALL}
    packed_g = _pack_small(gs)
    n_small = packed_g.shape[0]
    summed = _allreduce_small(jnp.concatenate([packed_g, loss_cols.reshape(-1, LANES)], axis=0))
    g_small, loss = summed[:n_small], jnp.sum(summed[n_small:])
    R = g_small.shape[0]
    res = _ew("adamw_small", lambda wv, mv, vv, g: (g,) + _adamw_math(wv, g, mv, vv),
              [_tiled(packed[0]), _tiled(packed[1]), _tiled(packed[2]), _tiled(g_small)], [(F32, LANES)] * 4, n_rows=R, rows=R)
    for d, buf in zip((grads, deltas, new_m, new_v), res):
        d.update(_unpack_small(buf, w))

    return (loss, grad_x.reshape(x.shape), *[grads[n] for n in ORDER], *[deltas[n] for n in ORDER],
            *[new_m[n] for n in ORDER], *[new_v[n] for n in ORDER])
```

```python
import functools
import math

import numpy as np
import jax
import jax.numpy as jnp
from jax import lax
from jax.experimental import pallas as pl
from jax.experimental.pallas import tpu as pltpu

F32 = jnp.float32
BF16 = jnp.bfloat16
MESH = pl.DeviceIdType.MESH

NEG_INF = -1e30
EPS = 1e-6
GRID_W = 64
ROPE_THETA = 10000.0
DILATIONS = (1, 4, 16)
BAND_HALF = 64
HEAD_A = 64
HEADS_A = 8
WIDTH_A = HEADS_A * HEAD_A
HEAD_B = 128
LOG2_E = math.log2(math.e)
QK_SCALE_LOG2 = HEAD_B ** -0.5 * LOG2_E
N_BUCKETS = 32
MAX_DISTANCE = 1024
ADAM_LR, ADAM_B1, ADAM_B2, ADAM_EPS, ADAM_WD, ADAM_STEP = 0.001, 0.9, 0.999, 1e-08, 0.01, 10

B_Q, B_K, B_V = 4608, 5632, 5888
G_A, G_B = 6144, 7168
IN_WIDTH = 8192

VMEM_LIMIT_BYTES = 56 * 1024 * 1024
QB_A = 128
QB_B = 256


def _params(*sem):
    return pltpu.CompilerParams(dimension_semantics=sem, vmem_limit_bytes=VMEM_LIMIT_BYTES)


def _bs(shape, fn):
    return pl.BlockSpec(shape, fn)


def _resident(shape, fn):
    return pl.BlockSpec(shape, fn, pipeline_mode=pl.Buffered(1))


def _mm(name, grid, pairs, out_shape, out_spec, dims, *, extras=(), epilogue=None, deps=(), reds=(), rider=None):
    n_pairs, n_extra, n_deps = len(pairs), len(extras), len(deps)
    operands = [p[0] for p in pairs] + [p[2] for p in pairs] + [e[0] for e in extras] + list(deps)
    in_specs = [p[1] for p in pairs] + [p[3] for p in pairs] + [e[1] for e in extras] + _any_specs(n_deps)
    single = not isinstance(out_shape, (list, tuple))
    out_shapes = [out_shape] if single else list(out_shape)
    out_specs = [out_spec] if single else list(out_spec)
    n_out = len(out_shapes)
    out_shapes += [jax.ShapeDtypeStruct((1, w), F32) for w in reds]
    out_specs += [_bs((1, w), lambda *_: (0, 0)) for w in reds]
    n_rin = 0
    if rider is not None:
        assert rider["n_blocks"] <= grid[0]
        n_rin = len(rider["operands"])
        operands += list(rider["operands"])
        in_specs += list(rider["in_specs"])
        out_shapes += list(rider["out_shape"])
        out_specs += list(rider["out_specs"])

    def body(*refs):
        a_refs, b_refs = refs[:n_pairs], refs[n_pairs:2 * n_pairs]
        e_refs = refs[2 * n_pairs:2 * n_pairs + n_extra]
        o_refs = refs[2 * n_pairs + n_extra + n_deps + n_rin:]
        if rider is not None:
            r_in = refs[2 * n_pairs + n_extra + n_deps:2 * n_pairs + n_extra + n_deps + n_rin]
            r_out = o_refs[n_out + len(reds):]

            @pl.when(pl.program_id(0) < rider["n_blocks"])
            def _():
                for ref, val in zip(r_out, rider["fn"](*[r[...] for r in r_in])):
                    ref[...] = val.astype(ref.dtype)
        acc = None
        for a_ref, b_ref in zip(a_refs, b_refs):
            t = lax.dot_general(a_ref[...], b_ref[...], (dims, ((), ())), preferred_element_type=F32)
            acc = t if acc is None else acc + t
        vals = acc if epilogue is None else epilogue(acc, *[e[...] for e in e_refs])
        if not isinstance(vals, (list, tuple)):
            vals = (vals,)
        for o_ref, v in zip(o_refs[:n_out], vals[:n_out]):
            o_ref[...] = v.astype(o_ref.dtype)
        if reds:
            first = functools.reduce(jnp.logical_and, [pl.program_id(ax) == 0 for ax in range(len(grid))])
            for r_ref, v in zip(o_refs[n_out:], vals[n_out:]):
                @pl.when(first)
                def _(r_ref=r_ref):
                    r_ref[...] = jnp.zeros_like(r_ref)
                r_ref[...] += v

    sem = ["arbitrary" if (reds or rider is not None) else "parallel"] * len(grid)
    res = pl.pallas_call(
        body, out_shape=out_shapes, grid=grid, in_specs=in_specs, out_specs=out_specs,
        compiler_params=_params(*sem), name=name)(*operands)
    if rider is not None:
        rider["deliver"](res[n_out + len(reds):])
        res = res[:n_out + len(reds)]
    return res[0] if (single and not reds) else res


NN = ((1,), (0,))
NT = ((1,), (1,))
TN = ((0,), (0,))


def _mm_wgrad(name, a, b, *, a_cols, b_cols, tm, tn, J, deps=(), rider=None):
    def pick(arr, cols, t):
        if arr.ndim == 3:
            T, c = arr.shape[1], arr.shape[2]
            t = min(t, c)
            return T, c, t, (lambda sel: _bs((None, T, t), lambda j, i, k: (j, 0, sel(i, k))))
        T = arr.shape[0]
        c = arr.shape[1] if cols is None else cols
        t = min(t, c)
        per = c // t
        if cols is None:
            if per == 1:
                return T, c, t, (lambda sel: _resident((T, t), lambda j, i, k: (0, 0)))
            return T, c, t, (lambda sel: _bs((T, t), lambda j, i, k: (0, sel(i, k))))
        return T, c, t, (lambda sel: _bs((T, t), lambda j, i, k: (0, j * per + sel(i, k))))
    _, ca, tm, mk_a = pick(a, a_cols, tm)
    _, cb, tn, mk_b = pick(b, b_cols, tn)
    return _mm(name, (J, ca // tm, cb // tn),
               [(a, mk_a(lambda i, k: i), b, mk_b(lambda i, k: k))],
               jax.ShapeDtypeStruct((J, ca, cb), BF16), _bs((None, tm, tn), lambda j, i, k: (j, i, k)), TN, deps=deps, rider=rider)


def _tiled(arr, width=None, col=0, rowblk=0):
    return ("t", arr, arr.shape[1] if width is None else width, col, rowblk)


def _table(arr):
    return ("f", arr)


def _whole(arr):
    return ("w", arr)


def _ew(name, fn, ins, outs, *, n_rows, rows, reds=(), ncols=1, deps=()):
    nrb = n_rows // rows
    n_deps = len(deps)
    operands, in_specs = [], []
    for spec in ins:
        if spec[0] == "t":
            _, arr, width, col, rowblk = spec
            step = 1 if ncols > 1 else 0
            in_specs.append(_bs((rows, width), lambda c, i, col=col, rowblk=rowblk, step=step: (rowblk + i, col + c * step)))
        elif spec[0] == "f":
            arr = spec[1]
            in_specs.append(_bs((rows, arr.shape[1]), lambda c, i: (i, 0)))
        else:
            arr = spec[1]
            nd = arr.ndim
            if nd == 3:
                in_specs.append(_bs((None,) + arr.shape[1:], lambda c, i: (c, 0, 0)))
            else:
                in_specs.append(_bs(arr.shape, lambda c, i, nd=nd: (0,) * nd))
        operands.append(arr)
    out_shapes = [jax.ShapeDtypeStruct((n_rows, ncols * w), dt) for dt, w in outs]
    out_specs = [_bs((rows, w), lambda c, i: (i, c)) for _, w in outs]
    out_shapes += [jax.ShapeDtypeStruct((ncols, 1, w), F32) for w in reds]
    out_specs += [_bs((None, 1, w), lambda c, i: (c, 0, 0)) for w in reds]
    n_in, n_out, n_red = len(ins), len(outs), len(reds)
    operands += list(deps)
    in_specs += _any_specs(n_deps)

    def body(*refs):
        vals = fn(*[r[...] for r in refs[:n_in]])
        if not isinstance(vals, (tuple, list)):
            vals = (vals,)
        o_refs = refs[n_in + n_deps:]
        for o_ref, v in zip(o_refs[:n_out], vals[:n_out]):
            o_ref[...] = v.astype(o_ref.dtype)
        if n_red:
            i = pl.program_id(1)
            for r_ref, v in zip(o_refs[n_out:], vals[n_out:]):
                @pl.when(i == 0)
                def _(r_ref=r_ref):
                    r_ref[...] = jnp.zeros_like(r_ref)
                r_ref[...] += v

    res = pl.pallas_call(
        body, out_shape=out_shapes, grid=(ncols, nrb), in_specs=in_specs, out_specs=out_specs,
        compiler_params=_params("parallel", "arbitrary" if n_red else "parallel"), name=name)(*operands)
    return res


def _colsum(v):
    return jnp.sum(v, axis=0, keepdims=True)


def _rstd(x):
    return lax.rsqrt(jnp.mean(x * x, axis=-1, keepdims=True) + EPS)


def _sigmoid(x):
    return 1.0 / (1.0 + jnp.exp(-x))


def _norm_fwd(x, g):
    return x * _rstd(x) * g


def _norm_bwd(x, g, dy):
    r = _rstd(x)
    xh = x * r
    dxh = dy * g
    dx = r * (dxh - xh * jnp.mean(dxh * xh, axis=-1, keepdims=True))
    return dx, dy * xh


def _row_spec(arr, rows):
    if arr.shape[0] == 1:
        return _bs(arr.shape, lambda i: (0, 0))
    return _bs((rows, arr.shape[1]), lambda i: (i, 0))


def _ffn_fwd(tag, x, gain, get_w, deps=(), *, h=None, tail_ins=(), tail_fn=None, tail_outs=(F32,), tail_reds=()):
    T, D = x.shape
    if h is None:
        (h,) = _ew(f"{tag}_norm", lambda xv, g: _norm_fwd(xv, g), [_tiled(x), _whole(gain)], [(BF16, D)], n_rows=T, rows=512,
                   deps=deps)
    w1, w3 = get_w(f"{tag}_w1", h), get_w(f"{tag}_w3", h)
    J, f, _ = w1.shape
    tm = 1024

    def up(h_ref, w1_ref, w3_ref, u_ref, g_ref, a_ref):
        hv = h_ref[...]
        u = lax.dot_general(hv, w1_ref[...], (NT, ((), ())), preferred_element_type=F32)
        g = lax.dot_general(hv, w3_ref[...], (NT, ((), ())), preferred_element_type=F32)
        u_ref[...] = u.astype(BF16)
        g_ref[...] = g.astype(BF16)
        a_ref[...] = (u * _sigmoid(u) * g).astype(BF16)

    slab = _bs((None, tm, f), lambda j, i: (j, i, 0))
    w_spec = _bs((None, f, D), lambda j, i: (j, 0, 0))
    u, g, a = pl.pallas_call(
        up, out_shape=[jax.ShapeDtypeStruct((J, T, f), BF16)] * 3, grid=(J, T // tm),
        in_specs=[_bs((tm, D), lambda j, i: (i, 0)), w_spec, w_spec], out_specs=[slab] * 3,
        compiler_params=_params("parallel", "parallel"), name=f"{tag}_up")(h, w1, w3)
    w2 = get_w(f"{tag}_w2", a)
    def tail(acc, xv, *rest):
        y = xv + 0.5 * acc
        return y if tail_fn is None else tail_fn(y, *rest)

    row = _bs((512, D), lambda i: (i, 0))
    res = _mm(f"{tag}_down", (T // 512,),
              [(a, _bs((None, 512, f), lambda i, j=j: (j, i, 0)), w2, _resident((None, f, D), lambda i, j=j: (j, 0, 0)))
               for j in range(J)],
              [jax.ShapeDtypeStruct((T, D), dt) for dt in tail_outs], [row] * len(tail_outs), NN,
              extras=[(x, row)] + [(t, _row_spec(t, 512)) for t in tail_ins], epilogue=tail, reds=tail_reds)
    return res, (h, u, g, a)


def _dh_norm_bwd(name, rows, pairs, dims, x, gain, dres, deps, also_bf16=False, rider=None):
    T, D = x.shape

    def epilogue(dh, xv, gv, dr):
        dx, dgr = _norm_bwd(xv, gv, dh)
        dx = dx + dr
        return (dx, 0.5 * dx) + ((dx,) if also_bf16 else ()) + (_colsum(dgr),)

    dts = [F32, BF16] + ([BF16] if also_bf16 else [])
    row = _bs((rows, D), lambda i: (i, 0))
    return _mm(name, (T // rows,), pairs, [jax.ShapeDtypeStruct((T, D), dt) for dt in dts], [row] * len(dts), dims,
               extras=[(x, row), (gain, _row_spec(gain, rows)), (dres, row)], epilogue=epilogue, deps=deps, reds=(D,), rider=rider)


def _ffn_bwd(tag, x, gain, get_w, put_g, saved, dy, dy_half, also_bf16=False, last=False, take_rider=lambda steps, after: None):
    h, u, g, a = saved
    T, D = x.shape
    w1, w3, w2 = [get_w(f"{tag}_{n}", dy_half) for n in ("w1", "w3", "w2")]
    J, f, _ = w1.shape
    dw2 = _mm_wgrad(f"{tag}_bwd_dw2", a, dy_half, a_cols=None, b_cols=None, tm=f, tn=D, J=J, rider=take_rider(J, dy_half))
    deps = put_g({f"{tag}_w2": dw2}) if last else []
    tm = 1024

    def up_bwd(dy_ref, w2_ref, u_ref, g_ref, *rest):
        du_ref, dg_ref = rest[-2:]
        da = lax.dot_general(dy_ref[...], w2_ref[...], (NT, ((), ())), preferred_element_type=F32)
        uv, gv = u_ref[...].astype(F32), g_ref[...].astype(F32)
        s = _sigmoid(uv)
        du_ref[...] = (da * gv * (s * (1.0 + uv * (1.0 - s)))).astype(BF16)
        dg_ref[...] = (da * (uv * s)).astype(BF16)

    slab = _bs((None, tm, f), lambda j, i: (j, i, 0))
    du, dg = pl.pallas_call(
        up_bwd, out_shape=[jax.ShapeDtypeStruct((J, T, f), BF16)] * 2, grid=(J, T // tm),
        in_specs=[_bs((tm, D), lambda j, i: (i, 0)), _bs((None, f, D), lambda j, i: (j, 0, 0)), slab, slab] + _any_specs(len(deps)),
        out_specs=[slab] * 2, compiler_params=_params("parallel", "parallel"), name=f"{tag}_bwd_up")(dy_half, w2, u, g, *deps)
    dw1 = _mm_wgrad(f"{tag}_bwd_dw1", du, h, a_cols=None, b_cols=None, tm=f, tn=D, J=J)
    deps = put_g({f"{tag}_w1": dw1}) if last else []
    dw3 = _mm_wgrad(f"{tag}_bwd_dw3", dg, h, a_cols=None, b_cols=None, tm=f, tn=D, J=J, deps=deps)
    deps = put_g({f"{tag}_w3": dw3} if last else {f"{tag}_w2": dw2, f"{tag}_w1": dw1, f"{tag}_w3": dw3})
    pairs = []
    for j in range(J):
        a_spec = _bs((None, 256, f), lambda i, j=j: (j, i, 0))
        w_spec = _resident((None, f, D), lambda i, j=j: (j, 0, 0))
        pairs += [(du, a_spec, w1, w_spec), (dg, a_spec, w3, w_spec)]
    return _dh_norm_bwd(f"{tag}_bwd_dh", 256, pairs, NN, x, gain, dy, deps, also_bf16, rider=take_rider(T // 256, dw3))


def _t5_bucket(rel):
    n = N_BUCKETS // 2
    max_exact = n // 2
    ret = jnp.where(rel > 0, n, 0)
    a = jnp.abs(rel)
    af = jnp.maximum(a, 1).astype(F32)
    large = max_exact + (jnp.log(af / max_exact) / math.log(MAX_DISTANCE / max_exact) * (n - max_exact)).astype(jnp.int32)
    large = jnp.minimum(large, n - 1)
    return ret + jnp.where(a < max_exact, a, large)


WIN_A = QB_A + 2 * BAND_HALF
WIN_SHIFTS = (0, BAND_HALF, 2 * BAND_HALF)


def _window_variant(n, nblk):
    return jnp.where(n == 0, 0, jnp.where(n == nblk - 1, 2, 1))


def _window_start(n, nblk):
    return pl.multiple_of(jnp.clip(n * QB_A - BAND_HALF, 0, nblk * QB_A - WIN_A), BAND_HALF)


def _band_steps(xp=jnp):
    qi = xp.arange(QB_A, dtype=xp.int32)[None, :, None]
    kj = xp.arange(WIN_A, dtype=xp.int32)[None, None, :]
    return kj - qi - xp.asarray(WIN_SHIFTS, dtype=xp.int32)[:, None, None]


def _bias_tiles(rel_bias):
    wide = QB_A + 2 * WIN_SHIFTS[-1]
    qi = jnp.arange(QB_A, dtype=jnp.int32)[:, None]
    steps = jnp.arange(wide, dtype=jnp.int32)[None, :] - WIN_SHIFTS[-1] - qi
    buckets = jnp.stack([_t5_bucket(steps * d) for d in DILATIONS])
    inband = (jnp.abs(steps) <= BAND_HALF).astype(jnp.int32)
    n_heads = rel_bias.shape[1]

    def body(tab_ref, b_ref, m_ref, o_ref):
        hd = pl.program_id(0)
        bkt = b_ref[...]
        acc = jnp.zeros(bkt.shape, F32)
        for b in range(N_BUCKETS):
            acc = jnp.where(bkt == b, tab_ref[b, hd], acc)
        o_ref[...] = jnp.where(m_ref[...] > 0, acc, NEG_INF)

    base = pl.pallas_call(
        body, out_shape=jax.ShapeDtypeStruct((n_heads, QB_A, wide), F32), grid=(n_heads,),
        in_specs=[pl.BlockSpec(memory_space=pltpu.SMEM),
                  _bs((None, QB_A, wide), lambda hd: (hd // HEADS_A, 0, 0)),
                  _bs((QB_A, wide), lambda hd: (0, 0))],
        out_specs=_bs((None, QB_A, wide), lambda hd: (hd, 0, 0)),
        compiler_params=_params("parallel"), name="a_bias_tiles")(rel_bias, buckets, inband)
    base = base.reshape(len(DILATIONS), HEADS_A, QB_A, wide)
    return jnp.stack([base[..., WIN_SHIFTS[-1] - s:WIN_SHIFTS[-1] - s + WIN_A] for s in WIN_SHIFTS], axis=1)


def _bias_grad(dbias):
    steps = _band_steps(np)
    inband = np.abs(steps) <= BAND_HALF
    present = []
    for d in DILATIONS:
        rel = steps * d
        a = np.abs(rel)
        large = 8 + (np.log(np.maximum(a, 1) / 8.0) / math.log(MAX_DISTANCE / 8.0) * 8).astype(np.int64)
        bk = np.where(rel > 0, 16, 0) + np.where(a < 8, a, np.minimum(large, 15))
        present.append([sorted(set(bk[v][inband[v]].tolist())) for v in range(3)])
    buckets = jnp.stack([_t5_bucket(_band_steps() * d) for d in DILATIONS])
    n_heads = len(DILATIONS) * HEADS_A

    def body(b_ref, d_ref, o_ref):
        row = lax.broadcasted_iota(jnp.int32, (N_BUCKETS, n_heads), 0)
        col = lax.broadcasted_iota(jnp.int32, (N_BUCKETS, n_heads), 1)
        out = jnp.zeros((N_BUCKETS, n_heads), F32)
        for grp in range(len(DILATIONS)):
            for hh in range(HEADS_A):
                hd = grp * HEADS_A + hh
                for b in sorted(set(sum(present[grp], []))):
                    tot = jnp.zeros((), F32)
                    for v in range(3):
                        if b in present[grp][v]:
                            tot = tot + jnp.sum(jnp.where(b_ref[grp, v] == b, d_ref[grp, v, hh], 0.0))
                    out = jnp.where((row == b) & (col == hd), tot, out)
        o_ref[...] = out

    return pl.pallas_call(
        body, out_shape=jax.ShapeDtypeStruct((N_BUCKETS, n_heads), F32),
        compiler_params=pltpu.CompilerParams(vmem_limit_bytes=VMEM_LIMIT_BYTES), name="a_bias_grad")(buckets, dbias)


def _lane_is_second_head(shape):
    return lax.broadcasted_iota(jnp.int32, shape, len(shape) - 1) >= HEAD_A


VIEW_ROWS = 512


def _view_chunks():
    return [pltpu.VMEM((VIEW_ROWS, LANES), F32)] * (WIDTH_A // LANES)


def _rows_to_view(x_ref, col, o_ref, ocol, d, chunks):
    n = VIEW_ROWS // d
    for c, scr in enumerate(chunks):
        scr[...] = x_ref[:, col + c * LANES:col + (c + 1) * LANES].astype(F32)
        for r in range(d):
            at = ocol + r * WIDTH_A + c * LANES
            o_ref[:, at:at + LANES] = scr[pl.ds(r, n, stride=d), :].astype(o_ref.dtype)


def _view_to_rows(v_ref, o_ref, col, d, chunks):
    n = VIEW_ROWS // d
    for c, scr in enumerate(chunks):
        if d == 1:
            o_ref[:, col + c * LANES:col + (c + 1) * LANES] = v_ref[:, c * LANES:(c + 1) * LANES].astype(o_ref.dtype)
            continue
        for r in range(d):
            scr[pl.ds(r, n, stride=d), :] = v_ref[:, r * WIDTH_A + c * LANES:r * WIDTH_A + (c + 1) * LANES].astype(F32)
        o_ref[:, col + c * LANES:col + (c + 1) * LANES] = scr[...].astype(o_ref.dtype)


def _group_view(proj, grp, d):
    T = proj.shape[0]
    if d == 1:
        return proj, (lambda part, r: grp * 3 + part)

    def body(x_ref, o_ref, *chunks):
        for part in range(3):
            _rows_to_view(x_ref, part * WIDTH_A, o_ref, part * d * WIDTH_A, d, chunks)

    view = pl.pallas_call(
        body, out_shape=jax.ShapeDtypeStruct((T // d, 3 * d * WIDTH_A), proj.dtype), grid=(T // VIEW_ROWS,),
        in_specs=[_bs((VIEW_ROWS, 3 * WIDTH_A), lambda i: (i, grp))],
        out_specs=_bs((VIEW_ROWS // d, 3 * d * WIDTH_A), lambda i: (i, 0)),
        scratch_shapes=_view_chunks(), compiler_params=_params("parallel"), name=f"a_view_d{d}")(proj)
    return view, (lambda part, r: part * d + r)


def _stack_heads(v2, second):
    zero = jnp.zeros_like(v2)
    return jnp.concatenate([jnp.where(second, zero, v2), jnp.where(second, v2, zero)], axis=0)


def _unstack_heads(v, second):
    return jnp.where(second, v[QB_A:], v[:QB_A])


def _dil_fwd(view, bias, d):
    pv, colblk = view
    L = pv.shape[0]
    nblk = L // QB_A
    W2 = 2 * HEAD_A
    scale = HEAD_A ** -0.5

    def body(q_ref, k_ref, v_ref, b_ref, o_ref, l_ref):
        win = pl.ds(_window_start(pl.program_id(1), nblk), WIN_A)
        second = _lane_is_second_head((QB_A, W2))
        pairs = range(HEADS_A // 2)
        cols = [slice(hp * W2, (hp + 1) * W2) for hp in pairs]
        s = [lax.dot_general(_stack_heads(q_ref[:, cols[hp]], second), k_ref[win, cols[hp]], (NT, ((), ())),
                             preferred_element_type=F32) * scale + b_ref[2 * hp:2 * hp + 2].reshape(2 * QB_A, WIN_A)
             for hp in pairs]
        m = [jnp.max(x, axis=-1, keepdims=True) for x in s]
        p = [jnp.exp(x - mx) for x, mx in zip(s, m)]
        l = [jnp.sum(x, axis=-1, keepdims=True) for x in p]
        res = [jnp.dot(p[hp].astype(BF16), v_ref[win, cols[hp]], preferred_element_type=F32) / l[hp] for hp in pairs]
        o_ref[...] = jnp.concatenate([_unstack_heads(x, second) for x in res], axis=1).astype(o_ref.dtype)
        l_ref[...] = jnp.concatenate([_unstack_heads(jnp.broadcast_to(mx + jnp.log(lx), (2 * QB_A, W2)), second)
                                      for mx, lx in zip(m, l)], axis=1)

    in_specs = [_bs((QB_A, WIDTH_A), lambda r, n: (n, colblk(0, r))),
                _bs((L, WIDTH_A), lambda r, n: (0, colblk(1, r))), _bs((L, WIDTH_A), lambda r, n: (0, colblk(2, r))),
                _bs((None, HEADS_A, QB_A, WIN_A), lambda r, n: (_window_variant(n, nblk), 0, 0, 0))]
    o, lse = pl.pallas_call(
        body, out_shape=[jax.ShapeDtypeStruct((L, d * WIDTH_A), BF16), jax.ShapeDtypeStruct((L, d * WIDTH_A), F32)],
        grid=(d, nblk), in_specs=in_specs,
        out_specs=[_bs((QB_A, WIDTH_A), lambda r, n: (n, r)), _bs((QB_A, WIDTH_A), lambda r, n: (n, r))],
        compiler_params=_params("parallel", "parallel"), name=f"a_fwd_d{d}")(pv, pv, pv, bias)
    return o, lse


def _dil_bwd(view_qkv, bias, do, lse, cterm, d):
    pv, colblk = view_qkv
    L = pv.shape[0]
    nblk = L // QB_A
    W2 = 2 * HEAD_A
    PPS = 4
    WS = PPS * W2
    ob = WIDTH_A // WS
    scale = HEAD_A ** -0.5

    def body(q_ref, k_ref, v_ref, do_ref, l_ref, c_ref, b_ref, dq_ref, dk_ref, dv_ref, db_ref):
        r, n = pl.program_id(1), pl.program_id(2)

        @pl.when(n == 0)
        def _():
            dk_ref[...] = jnp.zeros_like(dk_ref)
            dv_ref[...] = jnp.zeros_like(dv_ref)

        @pl.when((n == 0) & (r == 0))
        def _():
            db_ref[...] = jnp.zeros_like(db_ref)

        second = _lane_is_second_head((QB_A, W2))
        win = pl.ds(_window_start(n, nblk), WIN_A)
        variant = _window_variant(n, nblk)
        pairs = range(PPS)
        cols = [slice(pp * W2, (pp + 1) * W2) for pp in pairs]

        def head_rows(ref, pp):
            v2 = ref[:, cols[pp]]
            return jnp.concatenate([v2[:, 0:1], v2[:, HEAD_A:HEAD_A + 1]], axis=0)

        kw = [k_ref[win, c] for c in cols]
        vw = [v_ref[win, c] for c in cols]
        qs = [_stack_heads(q_ref[:, c], second) for c in cols]
        dos = [_stack_heads(do_ref[:, c], second) for c in cols]
        s = [lax.dot_general(qs[pp], kw[pp], (NT, ((), ())), preferred_element_type=F32) for pp in pairs]
        dp = [lax.dot_general(dos[pp], vw[pp], (NT, ((), ())), preferred_element_type=F32) for pp in pairs]
        p = [jnp.exp(s[pp] * scale + b_ref[2 * pp:2 * pp + 2].reshape(2 * QB_A, WIN_A) - head_rows(l_ref, pp)) for pp in pairs]
        ds = [p[pp] * (dp[pp] + head_rows(c_ref, pp)) for pp in pairs]
        db_ref[variant] += jnp.concatenate([x.reshape(2, QB_A, WIN_A) for x in ds], axis=0)
        pb = [x.astype(BF16) for x in p]
        dsb = [(x * scale).astype(BF16) for x in ds]
        dq_ref[...] = jnp.concatenate([_unstack_heads(jnp.dot(dsb[pp], kw[pp], preferred_element_type=F32), second)
                                       for pp in pairs], axis=1).astype(dq_ref.dtype)
        dk_ref[win, :] += jnp.concatenate([lax.dot_general(dsb[pp], qs[pp], (TN, ((), ())), preferred_element_type=F32)
                                           for pp in pairs], axis=1)
        dv_ref[win, :] += jnp.concatenate([lax.dot_general(pb[pp], dos[pp], (TN, ((), ())), preferred_element_type=F32)
                                           for pp in pairs], axis=1)

    kv_spec = _resident if d == 1 else _bs
    in_specs = [_bs((QB_A, WS), lambda hp, r, n: (n, colblk(0, r) * ob + hp)),
                kv_spec((L, WS), lambda hp, r, n: (0, colblk(1, r) * ob + hp)),
                kv_spec((L, WS), lambda hp, r, n: (0, colblk(2, r) * ob + hp))]
    in_specs += [_bs((QB_A, WS), lambda hp, r, n: (n, r * ob + hp))] * 3
    in_specs += [_bs((None, 2 * PPS, QB_A, WIN_A), lambda hp, r, n: (_window_variant(n, nblk), hp, 0, 0))]
    out_shape = [jax.ShapeDtypeStruct((L, d * WIDTH_A), BF16), jax.ShapeDtypeStruct((L, d * WIDTH_A), F32),
                 jax.ShapeDtypeStruct((L, d * WIDTH_A), F32), jax.ShapeDtypeStruct((3, HEADS_A, QB_A, WIN_A), F32)]
    out_specs = [_bs((QB_A, WS), lambda hp, r, n: (n, r * ob + hp)),
                 _bs((L, WS), lambda hp, r, n: (0, r * ob + hp)), _bs((L, WS), lambda hp, r, n: (0, r * ob + hp)),
                 _bs((3, 2 * PPS, QB_A, WIN_A), lambda hp, r, n: (0, hp, 0, 0))]
    dq, dk, dv, db = pl.pallas_call(
        body, out_shape=out_shape, grid=(ob, d, nblk), in_specs=in_specs, out_specs=out_specs,
        compiler_params=_params("arbitrary", "arbitrary", "arbitrary"), name=f"a_bwd_d{d}")(
            pv, pv, pv, do, lse, cterm, bias)
    return dq, dk, dv, db


def _assemble_dproj(a_parts, dq_b, dk_b, dv_b, dga, dgb):
    T = dq_b.shape[0]
    flat = [(a_parts[part][g], d) for part in range(3) for g, d in enumerate(DILATIONS)]
    rest = [dq_b, dk_b, dv_b, dga, dgb]

    def body(*refs):
        views, others = refs[:len(flat)], refs[len(flat):len(flat) + len(rest)]
        o_ref, chunks = refs[len(flat) + len(rest)], refs[len(flat) + len(rest) + 1:]
        col = 0
        for v_ref, (_, d) in zip(views, flat):
            _view_to_rows(v_ref, o_ref, col, d, chunks)
            col += WIDTH_A
        for x_ref in others:
            w = x_ref.shape[1]
            o_ref[:, col:col + w] = x_ref[...].astype(o_ref.dtype)
            col += w

    in_specs = [_bs((VIEW_ROWS // d, d * WIDTH_A), lambda i: (i, 0)) for _, d in flat]
    in_specs += [_bs((VIEW_ROWS, x.shape[1]), lambda i: (i, 0)) for x in rest]
    return pl.pallas_call(
        body, out_shape=jax.ShapeDtypeStruct((T, IN_WIDTH), BF16), grid=(T // VIEW_ROWS,), in_specs=in_specs,
        out_specs=_bs((VIEW_ROWS, IN_WIDTH), lambda i: (i, 0)), scratch_shapes=_view_chunks(),
        compiler_params=_params("parallel"), name="mix_bwd_dproj")(*[a for a, _ in flat], *rest)


def _segment_ones():
    i = np.arange(WIDTH_A)
    return jnp.asarray((i[:, None] // HEAD_A == i[None, :] // HEAD_A).astype(np.float32), dtype=BF16)


def _group_weights(l0, l1, l2):
    m = jnp.maximum(jnp.maximum(l0, l1), l2)
    e = [jnp.exp(l - m) for l in (l0, l1, l2)]
    z = e[0] + e[1] + e[2]
    return [ei / z for ei in e]


def _view_specs():
    return [_bs((VIEW_ROWS // d, d * WIDTH_A), lambda i: (i, 0)) for d in DILATIONS]


def _stage_tiles(n):
    return [pltpu.VMEM((VIEW_ROWS, WIDTH_A), F32)] * n


def _combine_fwd(outs, lses):
    T = outs[0].shape[0] * DILATIONS[0]
    n = len(DILATIONS)

    def body(*refs):
        o_refs, l_refs, oa_ref = refs[:n], refs[n:2 * n], refs[2 * n]
        o_st, l_st, chunks = refs[2 * n + 1:3 * n + 1], refs[3 * n + 1:4 * n + 1], refs[4 * n + 1:]
        for g, d in enumerate(DILATIONS):
            _view_to_rows(o_refs[g], o_st[g], 0, d, chunks)
            _view_to_rows(l_refs[g], l_st[g], 0, d, chunks)
        w = _group_weights(*[l[...] for l in l_st])
        oa_ref[...] = (w[0] * o_st[0][...] + w[1] * o_st[1][...] + w[2] * o_st[2][...]).astype(oa_ref.dtype)

    return pl.pallas_call(
        body, out_shape=jax.ShapeDtypeStruct((T, WIDTH_A), BF16), grid=(T // VIEW_ROWS,),
        in_specs=_view_specs() * 2, out_specs=_bs((VIEW_ROWS, WIDTH_A), lambda i: (i, 0)),
        scratch_shapes=_stage_tiles(2 * n) + _view_chunks(), compiler_params=_params("parallel"), name="a_combine")(*outs, *lses)


def _combine_bwd(doa, outs, lses):
    T = doa.shape[0]
    n = len(DILATIONS)

    def body(*refs):
        d_ref, o_refs, l_refs, seg_ref = refs[0], refs[1:n + 1], refs[n + 1:2 * n + 1], refs[2 * n + 1]
        do_refs, c_refs = refs[2 * n + 2:3 * n + 2], refs[3 * n + 2:4 * n + 2]
        o_st, l_st = refs[4 * n + 2:5 * n + 2], refs[5 * n + 2:6 * n + 2]
        tmp, chunks = refs[6 * n + 2], refs[6 * n + 3:]
        for g, d in enumerate(DILATIONS):
            _view_to_rows(o_refs[g], o_st[g], 0, d, chunks)
            _view_to_rows(l_refs[g], l_st[g], 0, d, chunks)
        dv = d_ref[...].astype(F32)
        w = _group_weights(*[l[...] for l in l_st])
        seg = seg_ref[...]
        tot = jnp.zeros(dv.shape, F32)
        for g in range(n):
            prod = w[g] * dv * o_st[g][...]
            hi = prod.astype(BF16)
            lo = (prod - hi.astype(F32)).astype(BF16)
            tot = tot + jnp.dot(hi, seg, preferred_element_type=F32) + jnp.dot(lo, seg, preferred_element_type=F32)
        for g, d in enumerate(DILATIONS):
            tmp[...] = w[g] * dv
            _rows_to_view(tmp, 0, do_refs[g], 0, d, chunks)
            tmp[...] = -w[g] * tot
            _rows_to_view(tmp, 0, c_refs[g], 0, d, chunks)

    views = [jax.ShapeDtypeStruct((T // d, d * WIDTH_A), dt) for dt in (BF16, F32) for d in DILATIONS]
    res = pl.pallas_call(
        body, out_shape=views, grid=(T // VIEW_ROWS,),
        in_specs=[_bs((VIEW_ROWS, WIDTH_A), lambda i: (i, 0))] + _view_specs() * 2 + [_bs((WIDTH_A, WIDTH_A), lambda i: (0, 0))],
        out_specs=_view_specs() * 2, scratch_shapes=_stage_tiles(2 * n + 1) + _view_chunks(),
        compiler_params=_params("parallel"), name="a_combine_bwd")(doa, *outs, *lses, _segment_ones())
    return res[:n], res[n:]


def _rope_tables(T):
    rows = T // GRID_W
    row = jnp.repeat(jnp.arange(rows, dtype=F32), GRID_W)
    col = jnp.tile(jnp.arange(GRID_W, dtype=F32), rows)
    n_freq = HEAD_B // 4
    freq = ROPE_THETA ** (-jnp.arange(n_freq, dtype=F32) / n_freq)
    ang = jnp.concatenate([row[:, None] * freq, col[:, None] * freq], axis=-1)
    cos, sin = jnp.repeat(jnp.cos(ang), 2, axis=1), jnp.repeat(jnp.sin(ang), 2, axis=1)
    sign = jnp.where(jnp.arange(HEAD_B) % 2 == 0, -1.0, 1.0).astype(F32)
    return cos, sin * sign


def _swap_pairs(v):
    even = lax.broadcasted_iota(jnp.int32, v.shape, v.ndim - 1) % 2 == 0
    n = v.shape[-1]
    return jnp.where(even, pltpu.roll(v, n - 1, v.ndim - 1), pltpu.roll(v, 1, v.ndim - 1))


def _qk_fwd(name, proj, col0, n_heads, gain, cos, sin, out_scale=1.0, deps=()):
    T = proj.shape[0]

    def fn(xr, g, c, s):
        xn = _norm_fwd(xr.astype(F32), g)
        return (xn * c + _swap_pairs(xn) * s) * out_scale

    (out,) = _ew(name, fn, [_tiled(proj, HEAD_B, col0 // HEAD_B), _whole(gain), _table(cos), _table(sin)],
                 [(BF16, HEAD_B)], n_rows=T, rows=2048, ncols=n_heads, deps=deps)
    return out


def _qk_bwd(name, dout, proj, col0, n_heads, gain, cos, sin, in_scale=1.0):
    T = proj.shape[0]

    def fn(dv, xr, g, c, s):
        dv = dv.astype(F32) * in_scale
        dxn = c * dv + _swap_pairs(s * dv)
        dx, dgr = _norm_bwd(xr.astype(F32), g, dxn)
        return dx, _colsum(dgr)

    dx, dg = _ew(name, fn, [_tiled(dout, HEAD_B, 0), _tiled(proj, HEAD_B, col0 // HEAD_B), _whole(gain),
                            _table(cos), _table(sin)],
                 [(BF16, HEAD_B)], n_rows=T, rows=2048, reds=(HEAD_B,), ncols=n_heads)
    return dx, jnp.sum(dg, axis=0)


def _gqa_fwd(qn, kn, proj, k_col=0):
    T = qn.shape[0]
    GW = 4 * HEAD_B
    QB = QB_B

    def body(q_ref, k_ref, v_ref, o_ref, l_ref):
        k, v = k_ref[...], v_ref[...]
        lane = lax.broadcasted_iota(jnp.int32, (QB, HEAD_B), 1)
        heads = range(4)
        s = [lax.dot_general(q_ref[:, g * HEAD_B:(g + 1) * HEAD_B], k, (NT, ((), ())), preferred_element_type=F32)
             for g in heads]
        m = [jnp.max(x, axis=-1, keepdims=True) for x in s]
        p = [jnp.exp2(x - mx) for x, mx in zip(s, m)]
        l = [jnp.sum(x, axis=-1, keepdims=True) for x in p]
        o = [jnp.dot(p[g].astype(BF16), v, preferred_element_type=F32) / l[g] for g in heads]
        o_ref[...] = jnp.concatenate(o, axis=1).astype(o_ref.dtype)
        lse_all = jnp.zeros((QB, HEAD_B), F32)
        for g in heads:
            lse_all = jnp.where(lane == g, m[g] + jnp.log2(l[g]), lse_all)
        l_ref[...] = lse_all

    return pl.pallas_call(
        body, out_shape=[jax.ShapeDtypeStruct((T, 2 * GW), BF16), jax.ShapeDtypeStruct((2, T, HEAD_B), F32)],
        grid=(2, T // QB),
        in_specs=[_bs((QB, GW), lambda kv, i: (i, kv)), _bs((T, HEAD_B), lambda kv, i: (0, k_col + kv)),
                  _bs((T, HEAD_B), lambda kv, i: (0, B_V // HEAD_B + kv))],
        out_specs=[_bs((QB, GW), lambda kv, i: (i, kv)), _bs((None, QB, HEAD_B), lambda kv, i: (kv, i, 0))],
        compiler_params=_params("parallel", "parallel"), name="b_fwd")(qn, kn, proj)


def _gqa_bwd(qn, kn, proj, o, lse, do, deps=(), k_col=0):
    T = qn.shape[0]
    GW = 4 * HEAD_B

    def body(q_ref, k_ref, v_ref, o_ref, l_ref, do_ref, *rest):
        dq_ref, dk_ref, dv_ref = rest[-3:]
        i = pl.program_id(1)

        @pl.when(i == 0)
        def _():
            dk_ref[...] = jnp.zeros_like(dk_ref)
            dv_ref[...] = jnp.zeros_like(dv_ref)

        k, v = k_ref[...], v_ref[...]
        lse_all = l_ref[...]
        for g in range(4):
            cols = slice(g * HEAD_B, (g + 1) * HEAD_B)
            q, dob = q_ref[:, cols], do_ref[:, cols]
            delta = jnp.sum(dob.astype(F32) * o_ref[:, cols].astype(F32), axis=-1, keepdims=True)
            s = lax.dot_general(q, k, (NT, ((), ())), preferred_element_type=F32)
            p = jnp.exp2(s - lse_all[:, g:g + 1])
            dp = lax.dot_general(dob, v, (NT, ((), ())), preferred_element_type=F32)
            ds = (p * (dp - delta)).astype(BF16)
            dq_ref[:, cols] = jnp.dot(ds, k, preferred_element_type=F32).astype(dq_ref.dtype)
            dk_ref[...] += lax.dot_general(ds, q, (TN, ((), ())), preferred_element_type=F32)
            dv_ref[...] += lax.dot_general(p.astype(BF16), dob, (TN, ((), ())), preferred_element_type=F32)

    return pl.pallas_call(
        body, out_shape=[jax.ShapeDtypeStruct((T, 2 * GW), BF16), jax.ShapeDtypeStruct((T, 2 * HEAD_B), F32),
                         jax.ShapeDtypeStruct((T, 2 * HEAD_B), F32)],
        grid=(2, T // QB_B),
        in_specs=[_bs((QB_B, GW), lambda kv, i: (i, kv)), _bs((T, HEAD_B), lambda kv, i: (0, k_col + kv)),
                  _bs((T, HEAD_B), lambda kv, i: (0, B_V // HEAD_B + kv)), _bs((QB_B, GW), lambda kv, i: (i, kv)),
                  _bs((None, QB_B, HEAD_B), lambda kv, i: (kv, i, 0)), _bs((QB_B, GW), lambda kv, i: (i, kv))] + _any_specs(len(deps)),
        out_specs=[_bs((QB_B, GW), lambda kv, i: (i, kv)), _bs((T, HEAD_B), lambda kv, i: (0, kv)),
                   _bs((T, HEAD_B), lambda kv, i: (0, kv))],
        compiler_params=_params("parallel", "arbitrary"), name="b_bwd")(qn, kn, proj, o, lse, do, *deps)


def _local_step(x, target, small, get_w, put_g, deps=(), prefetch_w=lambda name, after: [], take_rider=lambda steps, after: None):
    T, D = x.shape
    gs = {}

    bias = _bias_tiles(small["rel_bias"])
    cos, sin = _rope_tables(T)
    (x1, h2), ffn1_saved = _ffn_fwd("ffn1", x, small["ffn1_norm"], lambda name, after: get_w(name, [after, bias, cos, sin]), deps,
                                    tail_ins=[small["mix_norm"]], tail_fn=lambda y, g: (y, _norm_fwd(y, g)), tail_outs=(F32, BF16))
    w_in = get_w("w_in", h2)
    nq = w_in.shape[2]
    tpq = nq // WIDTH_A

    def proj_tile(j, k):
        c = j * tpq + k
        return jnp.where(c < 3 * len(DILATIONS), (c % 3) * 3 + c // 3, c)

    proj = _mm("mix_in", (4, tpq),
               [(h2, _resident((T, D), lambda j, k: (0, 0)), w_in, _bs((None, D, WIDTH_A), lambda j, k: (j, 0, k)))],
               jax.ShapeDtypeStruct((T, IN_WIDTH), BF16), _bs((T, WIDTH_A), lambda j, k: (0, proj_tile(j, k))), NN)

    a_views = [_group_view(proj, grp, d) for grp, d in enumerate(DILATIONS)]
    a_outs, a_lses = [], []
    for grp, d in enumerate(DILATIONS):
        o, l = _dil_fwd(a_views[grp], bias[grp], d)
        a_outs.append(o)
        a_lses.append(l)
    o_a = _combine_fwd(a_outs, a_lses)

    qk_gain = jnp.concatenate([jnp.tile(small["q_norm"] * QK_SCALE_LOG2, (8, 1)), jnp.tile(small["k_norm"], (2, 1))])[:, None, :]
    qkn = _qk_fwd("b_qknorm", proj, B_Q, 10, qk_gain, cos, sin, deps=prefetch_w("w_branch_a", proj))
    qn, kn, k_col = qkn, qkn, 8
    o_b, lse_b = _gqa_fwd(qn, kn, proj, k_col)
    ahead = prefetch_w("ffn2_w1", o_b)

    wa, wb, wo = get_w("w_branch_a", o_b), get_w("w_branch_b", o_b), get_w("w_out", o_b)
    bg_a, bg_b = small["b_gate"][:, :D], small["b_gate"][:, D:]
    n_a = wa.shape[0]

    def merge_out(oa_ref, ob_ref, ga_ref, gb_ref, x1_ref, wa_ref, wb_ref, wo_ref, ba_ref, bb_ref, g2_ref, *rest):
        ta_ref, tb_ref, mg_ref, x2_ref, hn_ref = rest[-5:]
        oa = oa_ref[...]
        ta = jnp.concatenate([jnp.dot(oa, wa_ref[j], preferred_element_type=F32) for j in range(n_a)], axis=1)
        tb = jnp.dot(ob_ref[...], wb_ref[...], preferred_element_type=F32)
        sa = _sigmoid(ga_ref[...].astype(F32) + ba_ref[...])
        sb = _sigmoid(gb_ref[...].astype(F32) + bb_ref[...])
        merged = (sa * ta + sb * tb).astype(BF16)
        ta_ref[...], tb_ref[...], mg_ref[...] = ta.astype(BF16), tb.astype(BF16), merged
        y = x1_ref[...] + jnp.dot(merged, wo_ref[...], preferred_element_type=F32)
        x2_ref[...] = y
        hn_ref[...] = _norm_fwd(y, g2_ref[...]).astype(BF16)

    row = _bs((512, D), lambda i: (i, 0))
    gate_specs = [_bs((512, D), lambda i: (i, G_A // D)), _bs((512, D), lambda i: (i, G_B // D))]
    whole2, whole3 = (lambda i: (0, 0)), (lambda i: (0, 0, 0))
    vec = _bs((1, D), whole2)
    t_a, t_b, merged, x2, hn2 = pl.pallas_call(
        merge_out, out_shape=[jax.ShapeDtypeStruct((T, D), BF16)] * 3 + [jax.ShapeDtypeStruct((T, D), F32), jax.ShapeDtypeStruct((T, D), BF16)],
        grid=(T // 512,),
        in_specs=[_bs((512, WIDTH_A), lambda i: (i, 0)), row] + gate_specs + [row, _resident(wa.shape, whole3), _resident((D, D), whole2),
                                                                                _resident((D, D), whole2), vec, vec, vec]
        + _any_specs(len(ahead)),
        out_specs=[row] * 5, compiler_params=_params("parallel"), name="mix_merge_out")(
            o_a, o_b, proj, proj, x1, wa, wb, wo, bg_a, bg_b, small["ffn2_norm"], *ahead)

    def head(xv, g, tv):
        r = _rstd(xv)
        xh = xv * r
        e = xh * g - tv
        dy = e * (1.0 / D)
        dxh = dy * g
        dx = r * (dxh - xh * jnp.mean(dxh * xh, axis=-1, keepdims=True))
        return dx, 0.5 * dx, _colsum(e * e) * (0.5 / D), _colsum(dy * xh)

    (dx3, dx3_half, loss_cols, g_final), ffn2_saved = _ffn_fwd(
        "ffn2", x2, small["ffn2_norm"], get_w, h=hn2, tail_ins=[small["final_norm"].reshape(1, D), target], tail_fn=head,
        tail_outs=(F32, BF16), tail_reds=(D, D))
    gs["final_norm"] = g_final.reshape(D)

    dx2, _, dmix, gs["ffn2_norm"] = _ffn_bwd("ffn2", x2, small["ffn2_norm"], get_w, put_g, ffn2_saved, dx3, dx3_half,
                                             also_bf16=True)
    g_out = _mm_wgrad("mix_bwd_dwout", merged, dmix, a_cols=D // 4, b_cols=None, tm=256, tn=512, J=4).reshape(D, D)

    def merge_out_bwd(dx_ref, ta_ref, tb_ref, ga_ref, gb_ref, wa_ref, wb_ref, wo_ref, ba_ref, bb_ref,
                      dta_ref, dtb_ref, dga_ref, dgb_ref, doa_ref, dob_ref, dba_ref, dbb_ref):
        dm = lax.dot_general(dx_ref[...], wo_ref[...], (NT, ((), ())), preferred_element_type=F32)
        ta, tb = ta_ref[...].astype(F32), tb_ref[...].astype(F32)
        sa = _sigmoid(ga_ref[...].astype(F32) + ba_ref[...])
        sb = _sigmoid(gb_ref[...].astype(F32) + bb_ref[...])
        dga, dgb = dm * ta * sa * (1.0 - sa), dm * tb * sb * (1.0 - sb)
        dta, dtb = (dm * sa).astype(BF16), (dm * sb).astype(BF16)
        dta_ref[...], dtb_ref[...] = dta, dtb
        dga_ref[...], dgb_ref[...] = dga.astype(BF16), dgb.astype(BF16)
        w = wa_ref.shape[2]
        doa = sum(lax.dot_general(dta[:, j * w:(j + 1) * w], wa_ref[j], (NT, ((), ())), preferred_element_type=F32) for j in range(n_a))
        doa_ref[...] = doa.astype(BF16)
        dob_ref[...] = lax.dot_general(dtb, wb_ref[...], (NT, ((), ())), preferred_element_type=F32).astype(BF16)

        @pl.when(pl.program_id(0) == 0)
        def _():
            dba_ref[...] = jnp.zeros_like(dba_ref)
            dbb_ref[...] = jnp.zeros_like(dbb_ref)
        dba_ref[...] += _colsum(dga)
        dbb_ref[...] += _colsum(dgb)

    rowb = _bs((256, D), lambda i: (i, 0))
    gate_specs = [_bs((256, D), lambda i: (i, G_A // D)), _bs((256, D), lambda i: (i, G_B // D))]
    dta, dtb, dga, dgb, do_a, do_b, dba, dbb = pl.pallas_call(
        merge_out_bwd,
        out_shape=[jax.ShapeDtypeStruct((T, D), BF16)] * 4 + [jax.ShapeDtypeStruct((T, WIDTH_A), BF16), jax.ShapeDtypeStruct((T, D), BF16)]
        + [jax.ShapeDtypeStruct((1, D), F32)] * 2,
        grid=(T // 256,),
        in_specs=[rowb, rowb, rowb] + gate_specs + [_resident(wa.shape, whole3), _resident((D, D), whole2), _resident((D, D), whole2), vec, vec],
        out_specs=[rowb] * 4 + [_bs((256, WIDTH_A), lambda i: (i, 0)), rowb, vec, vec],
        compiler_params=_params("arbitrary"), name="mix_merge_out_bwd")(dmix, t_a, t_b, proj, proj, wa, wb, wo, bg_a, bg_b)
    gs["b_gate"] = jnp.concatenate([dba, dbb], axis=1)

    g_a = _mm_wgrad("mix_bwd_dwa", o_a, dta, a_cols=None, b_cols=D // 4, tm=WIDTH_A, tn=256, J=4)
    g_b = _mm_wgrad("mix_bwd_dwb", o_b, dtb, a_cols=D // 4, b_cols=None, tm=256, tn=512, J=4).reshape(D, D)
    deps = put_g({"w_out": g_out, "w_branch_a": g_a, "w_branch_b": g_b})

    dqn, dkn, dv_b = _gqa_bwd(qn, kn, proj, o_b, lse_b, do_b, deps, k_col)
    dq_b, gs["q_norm"] = _qk_bwd("b_bwd_qnorm", dqn, proj, B_Q, 8, small["q_norm"], cos, sin, in_scale=HEAD_B ** -0.5)
    dk_b, gs["k_norm"] = _qk_bwd("b_bwd_knorm", dkn, proj, B_K, 2, small["k_norm"], cos, sin, in_scale=1.0 / LOG2_E)

    do_groups, c_groups = _combine_bwd(do_a, a_outs, a_lses)
    dqs, dks, dvs, dbs = [], [], [], []
    for grp, d in enumerate(DILATIONS):
        dq, dk, dv, db = _dil_bwd(a_views[grp], bias[grp], do_groups[grp], a_lses[grp], c_groups[grp], d)
        dqs.append(dq), dks.append(dk), dvs.append(dv), dbs.append(db)
    gs["rel_bias"] = _bias_grad(jnp.stack(dbs))

    dproj = _assemble_dproj([dqs, dks, dvs], dq_b, dk_b, dv_b, dga, dgb)
    nq = w_in.shape[2]
    g_in = _mm("mix_bwd_dwin", (4, tpq),
               [(h2, _resident((T, D), lambda j, k: (0, 0)), dproj, _bs((T, WIDTH_A), lambda j, k: (0, j * tpq + k)))],
               jax.ShapeDtypeStruct((4, D, nq), BF16), _bs((None, D, WIDTH_A), lambda j, k: (j, 0, k)), TN)
    deps = put_g({"w_in": g_in})
    dx1, dx1_half, gs["mix_norm"] = _dh_norm_bwd(
        "mix_bwd_dh", 256,
        [(dproj, _bs((256, nq), lambda i, j=j: (i, j)), w_in, _resident((None, D, nq), lambda i, j=j: (j, 0, 0))) for j in range(4)],
        NT, x1, small["mix_norm"], dx2, deps)

    dx0, _, gs["ffn1_norm"] = _ffn_bwd("ffn1", x, small["ffn1_norm"], get_w, put_g, ffn1_saved, dx1, dx1_half, last=True,
                                       take_rider=take_rider)
    return loss_cols, dx0, gs


def _position():
    return lax.axis_index("x"), lax.axis_index("y"), lax.axis_index("c")


def _any_specs(n):
    return [pl.BlockSpec(memory_space=pl.ANY)] * n


HBM_SPEC = pl.BlockSpec(memory_space=pltpu.HBM)
SEM_SPEC = pl.BlockSpec(memory_space=pltpu.SEMAPHORE)
DATAFLOW_EFFECT = pltpu.SideEffectType.DATAFLOW_SIDE_EFFECTING
N_PEER_CHIPS = 3
LANES = 128


def _quarter_copies(srcs, lands, send_sems, recv_sems, mode):
    x, y, c = _position()
    me = 2 * x + y
    peers = [(1 - x, y, c), (x, 1 - y, c), (1 - x, 1 - y, c)]
    copies = []
    for src, land, send, recv in zip(srcs, lands, send_sems, recv_sems):
        if mode == "sibling":
            copies.append(pltpu.make_async_remote_copy(src_ref=src, dst_ref=land, send_sem=send.at[0], recv_sem=recv.at[0],
                                                       device_id=(x, y, 1 - c), device_id_type=MESH))
            continue
        if mode == "fill":
            half = land.shape[1] // 2
            for p, (px, py, _) in enumerate(peers):
                part = land.at[2 * px + py, pl.ds(c * half, half)]
                copies.append(pltpu.make_async_remote_copy(src_ref=part, dst_ref=part, send_sem=send.at[p], recv_sem=recv.at[p],
                                                           device_id=(x, y, 1 - c), device_id_type=MESH))
            continue
        scatter = mode == "scatter"
        half = land.shape[1] // 2
        mine = land.at[me, pl.ds(c * half, half)]
        for p, (px, py, pc) in enumerate(peers):
            copies.append(pltpu.make_async_remote_copy(
                src_ref=src.at[2 * px + py] if scatter else mine, dst_ref=land.at[me] if scatter else mine,
                send_sem=send.at[p], recv_sem=recv.at[p], device_id=(px, py, pc), device_id_type=MESH))
    return copies


def _fill_from_sibling(name, stacks):
    n = len(stacks)

    def body(*refs):
        outs = refs[n:2 * n]
        send_sems, recv_sems = refs[2 * n:]
        x, y, c = _position()
        copies = []
        for i, ref in enumerate(outs):
            half = ref.shape[1] // 2
            rows = pl.ds(c * half, half)
            for p, k in enumerate((2 * (1 - x) + y, 2 * x + (1 - y), 2 * (1 - x) + (1 - y))):
                cp = pltpu.make_async_remote_copy(ref.at[k, rows], ref.at[k, rows], send_sems.at[3 * i + p], recv_sems.at[3 * i + p],
                                                  device_id=(x, y, 1 - c), device_id_type=MESH)
                cp.start()
                copies.append(cp)
        for cp in copies:
            cp.wait()

    return pl.pallas_call(
        body, out_shape=[jax.ShapeDtypeStruct(s.shape, s.dtype) for s in stacks],
        in_specs=_any_specs(n), out_specs=_any_specs(n), input_output_aliases={i: i for i in range(n)},
        scratch_shapes=[pltpu.SemaphoreType.DMA((N_PEER_CHIPS * n,)), pltpu.SemaphoreType.DMA((N_PEER_CHIPS * n,))],
        compiler_params=pltpu.CompilerParams(has_side_effects=True), name=name)(*stacks)


def _exchange_start(name, srcs, lands, mode):
    n = len(lands)
    arrays = list(lands) if srcs is None else list(srcs) + list(lands)
    k = len(arrays)

    def body(*refs):
        land_refs = refs[k - n:k]
        send_sems, recv_sems = refs[k:k + n], refs[k + n:k + 2 * n]
        token = refs[2 * k + 2 * n]
        for cp in _quarter_copies(refs[:n], land_refs, send_sems, recv_sems, mode):
            cp.start()
        token[...] = jnp.zeros_like(token)

    sem = pltpu.SemaphoreType.DMA((N_PEER_CHIPS,))
    out_shape = [sem] * (2 * n) + [pltpu.HBM(a.shape, a.dtype) for a in arrays] + [jax.ShapeDtypeStruct((8, LANES), F32)]
    res = pl.pallas_call(
        body, name=name, out_shape=out_shape, in_specs=[HBM_SPEC] * k,
        out_specs=[SEM_SPEC] * (2 * n) + [HBM_SPEC] * k + [pl.BlockSpec(memory_space=pltpu.VMEM)],
        input_output_aliases={i: 2 * n + i for i in range(k)},
        compiler_params=pltpu.CompilerParams(has_side_effects=DATAFLOW_EFFECT),
    )(*[pltpu.with_memory_space_constraint(a, pltpu.HBM) for a in arrays])
    thru = res[2 * n:2 * n + k]
    return res[:n], res[n:2 * n], (None if srcs is None else thru[:n]), thru[k - n:], res[2 * n + k]


def _exchange_wait(name, srcs, lands, send_sems, recv_sems, after, mode):
    n = len(lands)
    arrays = list(lands) if srcs is None else list(srcs) + list(lands)
    k = len(arrays)
    after = list(after) if isinstance(after, (list, tuple)) else [after]

    def body(*refs):
        sends, recvs = refs[k:k + n], refs[k + n:k + 2 * n]
        for cp in _quarter_copies(refs[:n], refs[k - n:k], sends, recvs, mode):
            cp.wait_send()
            cp.wait_recv()

    res = pl.pallas_call(
        body, name=name, out_shape=[pltpu.HBM(a.shape, a.dtype) for a in arrays],
        in_specs=[HBM_SPEC] * k + [SEM_SPEC] * (2 * n) + _any_specs(len(after)),
        out_specs=[HBM_SPEC] * k, input_output_aliases={i: i for i in range(k)},
        compiler_params=pltpu.CompilerParams(has_side_effects=DATAFLOW_EFFECT),
    )(*arrays, *send_sems, *recv_sems, *after)
    return (None if srcs is None else res[:n]), res[k - n:]


def _own_slots(name, srcs, from_stack=False):
    n = len(srcs)
    me = (2 * lax.axis_index("x") + lax.axis_index("y")).astype(jnp.int32).reshape(1)

    def body(me_ref, *refs):
        for x_ref, o_ref in zip(refs[:n], refs[n:]):
            o_ref[...] = x_ref[...].astype(o_ref.dtype)

    in_specs, out_specs, out_shape = [], [], []
    for src in srcs:
        R, C = src.shape[-2:]
        in_specs.append(pl.BlockSpec((None, R // 2, C), lambda i, me_ref: (me_ref[0], i, 0)) if from_stack
                        else pl.BlockSpec((R // 2, C), lambda i, me_ref: (i, 0)))
        out_specs.append(pl.BlockSpec((None, R // 2, C), lambda i, me_ref: (me_ref[0], i, 0)))
        out_shape.append(jax.ShapeDtypeStruct((4, R, C), BF16))
    grid_spec = pltpu.PrefetchScalarGridSpec(num_scalar_prefetch=1, grid=(2,), in_specs=in_specs, out_specs=out_specs)
    return pl.pallas_call(body, out_shape=out_shape, grid_spec=grid_spec, compiler_params=_params("parallel"), name=name)(me, *srcs)


def _allreduce_small(buf):
    R, C = buf.shape
    flips = [(fx, fy, fc) for fx in (0, 1) for fy in (0, 1) for fc in (0, 1)][1:]

    def body(in_ref, out_ref, land_ref, send_sems, recv_sems):
        x, y, c = _position()
        me = 4 * x + 2 * y + c
        copies = []
        for k, (fx, fy, fc) in enumerate(flips):
            px, py, pc = (1 - x if fx else x), (1 - y if fy else y), (1 - c if fc else c)
            cp = pltpu.make_async_remote_copy(in_ref, land_ref.at[me], send_sems.at[k], recv_sems.at[k],
                                              device_id=(px, py, pc), device_id_type=MESH)
            cp.start()
            copies.append(cp)
        land_ref[me] = in_ref[...]
        for cp in copies:
            cp.wait()
        acc = land_ref[0]
        for k in range(1, 8):
            acc = acc + land_ref[k]
        out_ref[...] = acc

    return pl.pallas_call(
        body, out_shape=jax.ShapeDtypeStruct((R, C), F32),
        in_specs=[pl.BlockSpec(memory_space=pltpu.VMEM)], out_specs=pl.BlockSpec(memory_space=pltpu.VMEM),
        scratch_shapes=[pltpu.VMEM((8, R, C), F32), pltpu.SemaphoreType.DMA((7,)), pltpu.SemaphoreType.DMA((7,))],
        compiler_params=pltpu.CompilerParams(has_side_effects=True), name="allreduce_small")(buf)


def _adamw_math(w, g, m, v):
    m2 = ADAM_B1 * m + (1.0 - ADAM_B1) * g
    v2 = ADAM_B2 * v + (1.0 - ADAM_B2) * (g * g)
    m_hat = m2 / (1.0 - ADAM_B1 ** ADAM_STEP)
    v_hat = v2 / (1.0 - ADAM_B2 ** ADAM_STEP)
    delta = -ADAM_LR * (m_hat / (jnp.sqrt(v_hat) + ADAM_EPS) + ADAM_WD * w)
    return delta, m2, v2


def _adamw_from_partials(wv, mv, vv, *parts):
    def four(a, b, c, d):
        return ((a.astype(F32) + b.astype(F32)) + c.astype(F32)) + d.astype(F32)

    g = four(*parts[:4]) + four(*parts[4:])
    return (g,) + _adamw_math(wv, g, mv, vv)


def _adamw_big(name, w, m, v, mine, theirs):
    R, C = w.shape
    rows = 256 if R % 256 == 0 else R // 2
    nrb = R // rows
    slots = [_tiled(s.reshape(4 * R, C), None, 0, k * nrb) for s in (mine, theirs) for k in range(4)]
    return _ew(name, _adamw_from_partials, [_tiled(w), _tiled(m), _tiled(v)] + slots, [(F32, C)] * 4, n_rows=R, rows=rows)


def _adamw_rider(w, m, v, mine, theirs, steps, deliver):
    R, C = w.shape
    fits = [nb for nb in range(1, steps + 1) if R % nb == 0 and (R // nb) % 16 == 0]
    if not fits:
        return None
    nb = fits[-1]
    rows = R // nb

    def blocks(first):
        return pl.BlockSpec((rows, C), lambda *g: (first + jnp.minimum(g[0], nb - 1), 0))

    flat = [s.reshape(4 * R, C) for s in (mine, theirs)]
    return dict(operands=[w, m, v] + [f for f in flat for _ in range(4)],
                in_specs=[blocks(0)] * 3 + [blocks(k * nb) for _ in flat for k in range(4)],
                out_shape=[jax.ShapeDtypeStruct((R, C), F32)] * 4, out_specs=[blocks(0)] * 4,
                n_blocks=nb, fn=_adamw_from_partials, deliver=lambda outs: deliver(*outs))


BIG = ("ffn1_w1", "ffn1_w3", "ffn1_w2", "w_in", "w_branch_a", "w_branch_b", "w_out", "ffn2_w1", "ffn2_w3", "ffn2_w2")
SMALL = ("ffn1_norm", "mix_norm", "b_gate", "q_norm", "k_norm", "rel_bias", "ffn2_norm", "final_norm")
ORDER = ("ffn1_norm", "ffn1_w1", "ffn1_w3", "ffn1_w2", "mix_norm", "w_in", "b_gate", "q_norm", "k_norm", "rel_bias",
         "w_branch_a", "w_branch_b", "w_out", "ffn2_norm", "ffn2_w1", "ffn2_w3", "ffn2_w2", "final_norm")
TRANSPOSED = ("ffn1_w1", "ffn1_w3", "ffn2_w1", "ffn2_w3")
SIBLING_LAG = 2
LONG_HOST_STEPS = 8
GATHER_GROUPS = (("ffn1_w1", "ffn1_w3"), ("ffn1_w2",), ("w_in",), ("w_branch_a", "w_branch_b", "w_out"),
                 ("ffn2_w1", "ffn2_w3", "ffn2_w2"))


def _pack_small(d):
    rows = []
    for n in SMALL:
        flat = d[n].reshape(-1)
        pad = (-flat.shape[0]) % LANES
        rows.append(jnp.pad(flat, (0, pad)).reshape(-1, LANES))
    buf = jnp.concatenate(rows, axis=0)
    return jnp.pad(buf, ((0, (-buf.shape[0]) % 8), (0, 0)))


def _unpack_small(buf, like):
    out, r = {}, 0
    for n in SMALL:
        size = like[n].size
        nr = -(-size // LANES)
        out[n] = buf[r:r + nr].reshape(-1)[:size].reshape(like[n].shape)
        r += nr
    return out


def kernel(x, ffn1_norm, ffn1_w1, ffn1_w3, ffn1_w2, mix_norm, w_in, b_gate, q_norm, k_norm, rel_bias, w_branch_a, w_branch_b, w_out, ffn2_norm, ffn2_w1, ffn2_w3, ffn2_w2, final_norm, loss_target, m_ffn1_norm, m_ffn1_w1, m_ffn1_w3, m_ffn1_w2, m_mix_norm, m_w_in, m_b_gate, m_q_norm, m_k_norm, m_rel_bias, m_w_branch_a, m_w_branch_b, m_w_out, m_ffn2_norm, m_ffn2_w1, m_ffn2_w3, m_ffn2_w2, m_final_norm, v_ffn1_norm, v_ffn1_w1, v_ffn1_w3, v_ffn1_w2, v_mix_norm, v_w_in, v_b_gate, v_q_norm, v_k_norm, v_rel_bias, v_w_branch_a, v_w_branch_b, v_w_out, v_ffn2_norm, v_ffn2_w1, v_ffn2_w3, v_ffn2_w2, v_final_norm):
    given = dict(locals())
    w = {n: given[n] for n in ORDER}
    m = {n: given["m_" + n] for n in ORDER}
    v = {n: given["v_" + n] for n in ORDER}
    T, D = x.shape[1], x.shape[2]

    def stored(a, n):
        a = a.reshape(a.shape[1:])
        return a.T if n in TRANSPOSED else a

    def returned(a, n):
        return (a.T if n in TRANSPOSED else a).reshape(w[n].shape)

    quarter = {n: stored(w[n], n) for n in BIG}
    send, recv, _, land_thru, token = _exchange_start(
        "gather_start", None, _own_slots("own_weights", [quarter[n] for n in BIG]), "gather")
    index = {n: i for i, n in enumerate(BIG)}
    ready, filling = {}, {}

    def landed_halves(group, after):
        ids = [index[n] for n in group]
        return _exchange_wait("gather_wait_" + group[0], None, [land_thru[i] for i in ids],
                              [send[i] for i in ids], [recv[i] for i in ids], after, "gather")[1]

    def prefetch_w(name, after):
        group = next(g for g in GATHER_GROUPS if name in g)
        started = _exchange_start("fill_start_" + group[0], None, landed_halves(group, after), "fill")
        filling[group] = started
        return [started[4]]

    def get_w(name, after):
        if name not in ready:
            group = next(g for g in GATHER_GROUPS if name in g)
            if group in filling:
                f_send, f_recv, _, thru, _ = filling[group]
                stacks = _exchange_wait("fill_wait_" + group[0], None, thru, f_send, f_recv, after, "fill")[1]
            else:
                stacks = _fill_from_sibling("gather_fill_" + group[0], landed_halves(group, after))
            for n, st in zip(group, stacks):
                ready[n] = st.reshape(D, D) if n in ("w_branch_b", "w_out") else st
        return ready[name]

    scattered, forwarded = [], []

    def forward_oldest(after):
        names, s_sem, r_sem, srcs, lands = scattered.pop(0)
        _, landed = _exchange_wait("scatter_wait_" + names[0], srcs, lands, s_sem, r_sem, after, "scatter")
        started = _exchange_start("sibling_start_" + names[0], landed, [lax.empty(a.shape, a.dtype) for a in landed], "sibling")
        forwarded.append((names,) + tuple(started[:4]))
        return started[4]

    def put_g(grads):
        names = list(grads)
        stacks = [grads[n].reshape((4,) + quarter[n].shape) for n in names]
        lands = _own_slots("own_grad_" + names[0], stacks, from_stack=True)
        started = _exchange_start("scatter_start_" + names[0], stacks, lands, "scatter")
        scattered.append((names,) + tuple(started[:4]))
        tokens = [started[4]]
        if len(scattered) > SIBLING_LAG:
            tokens.append(forward_oldest(started[4]))
        return tokens

    grads, deltas, new_m, new_v = {}, {}, {}, {}
    arrived, riding = {}, set()

    def partials(gi, after):
        if gi not in arrived:
            names, s_sem, r_sem, srcs, lands = forwarded[gi]
            arrived[gi] = _exchange_wait("sibling_wait_" + names[0], srcs, lands, s_sem, r_sem, after, "sibling")
        return arrived[gi]

    def deliver_to(n):
        def deliver(*res):
            grads[n], deltas[n], new_m[n], new_v[n] = [returned(r, n) for r in res]
        return deliver

    def take_rider(steps, after):
        waiting = [(quarter[n].size, gi, k, n) for gi, entry in enumerate(forwarded) for k, n in enumerate(entry[0]) if n not in riding]
        for _, gi, k, n in sorted(waiting, reverse=steps >= LONG_HOST_STEPS):
            mine, theirs = partials(gi, after)
            rider = _adamw_rider(quarter[n], stored(m[n], n), stored(v[n], n), mine[k], theirs[k], steps, deliver_to(n))
            if rider is not None:
                riding.add(n)
                return rider
        return None

    small = {n: w[n] for n in SMALL}
    packed = [_pack_small({n: d[n] for n in SMALL}) for d in (w, m, v)]
    loss_cols, grad_x, gs = _local_step(x.reshape(T, D), loss_target.reshape(T, D), small, get_w, put_g, deps=[token] + packed,
                                        prefetch_w=prefetch_w, take_rider=take_rider)

    after = grad_x
    while scattered:
        after = forward_oldest(after)
    for gi, entry in enumerate(forwarded):
        mine, theirs = partials(gi, after)
        for n, a, b in zip(entry[0], mine, theirs):
            if n not in riding:
                deliver_to(n)(*_adamw_big(f"adamw_{n}", quarter[n], stored(m[n], n), stored(v[n], n), a, b))

    gs = {n: gs[n].reshape(w[n].shape) for n in SMALL}
    packed_g = _pack_small(gs)
    n_small = packed_g.shape[0]
    summed = _allreduce_small(jnp.concatenate([packed_g, loss_cols.reshape(-1, LANES)], axis=0))
    g_small, loss = summed[:n_small], jnp.sum(summed[n_small:])
    R = g_small.shape[0]
    res = _ew("adamw_small", lambda wv, mv, vv, g: (g,) + _adamw_math(wv, g, mv, vv),
              [_tiled(packed[0]), _tiled(packed[1]), _tiled(packed[2]), _tiled(g_small)], [(F32, LANES)] * 4, n_rows=R, rows=R)
    for d, buf in zip((grads, deltas, new_m, new_v), res):
        d.update(_unpack_small(buf, w))

    return (loss, grad_x.reshape(x.shape), *[grads[n] for n in ORDER], *[deltas[n] for n in ORDER],
            *[new_m[n] for n in ORDER], *[new_v[n] for n in ORDER])
```

```python
import functools
import math

import numpy as np
import jax
import jax.numpy as jnp
from jax import lax
from jax.experimental import pallas as pl
from jax.experimental.pallas import tpu as pltpu

F32 = jnp.float32
BF16 = jnp.bfloat16
MESH = pl.DeviceIdType.MESH

NEG_INF = -1e30
EPS = 1e-6
GRID_W = 64
ROPE_THETA = 10000.0
DILATIONS = (1, 4, 16)
BAND_HALF = 64
HEAD_A = 64
HEADS_A = 8
WIDTH_A = HEADS_A * HEAD_A
HEAD_B = 128
LOG2_E = math.log2(math.e)
QK_SCALE_LOG2 = HEAD_B ** -0.5 * LOG2_E
N_BUCKETS = 32
MAX_DISTANCE = 1024
ADAM_LR, ADAM_B1, ADAM_B2, ADAM_EPS, ADAM_WD, ADAM_STEP = 0.001, 0.9, 0.999, 1e-08, 0.01, 10

B_Q, B_K, B_V = 4608, 5632, 5888
G_A, G_B = 6144, 7168
IN_WIDTH = 8192

VMEM_LIMIT_BYTES = 56 * 1024 * 1024
QB_A = 128
QB_B = 256


def _params(*sem):
    return pltpu.CompilerParams(dimension_semantics=sem, vmem_limit_bytes=VMEM_LIMIT_BYTES)


def _bs(shape, fn):
    return pl.BlockSpec(shape, fn)


def _resident(shape, fn):
    return pl.BlockSpec(shape, fn, pipeline_mode=pl.Buffered(1))


def _mm(name, grid, pairs, out_shape, out_spec, dims, *, extras=(), epilogue=None, deps=(), reds=(), rider=None):
    n_pairs, n_extra, n_deps = len(pairs), len(extras), len(deps)
    operands = [p[0] for p in pairs] + [p[2] for p in pairs] + [e[0] for e in extras] + list(deps)
    in_specs = [p[1] for p in pairs] + [p[3] for p in pairs] + [e[1] for e in extras] + _any_specs(n_deps)
    single = not isinstance(out_shape, (list, tuple))
    out_shapes = [out_shape] if single else list(out_shape)
    out_specs = [out_spec] if single else list(out_spec)
    n_out = len(out_shapes)
    out_shapes += [jax.ShapeDtypeStruct((1, w), F32) for w in reds]
    out_specs += [_bs((1, w), lambda *_: (0, 0)) for w in reds]
    n_rin = 0
    if rider is not None:
        assert rider["n_blocks"] <= grid[0]
        n_rin = len(rider["operands"])
        operands += list(rider["operands"])
        in_specs += list(rider["in_specs"])
        out_shapes += list(rider["out_shape"])
        out_specs += list(rider["out_specs"])

    def body(*refs):
        a_refs, b_refs = refs[:n_pairs], refs[n_pairs:2 * n_pairs]
        e_refs = refs[2 * n_pairs:2 * n_pairs + n_extra]
        o_refs = refs[2 * n_pairs + n_extra + n_deps + n_rin:]
        if rider is not None:
            r_in = refs[2 * n_pairs + n_extra + n_deps:2 * n_pairs + n_extra + n_deps + n_rin]
            r_out = o_refs[n_out + len(reds):]

            @pl.when(pl.program_id(0) < rider["n_blocks"])
            def _():
                for ref, val in zip(r_out, rider["fn"](*[r[...] for r in r_in])):
                    ref[...] = val.astype(ref.dtype)
        acc = None
        for a_ref, b_ref in zip(a_refs, b_refs):
            t = lax.dot_general(a_ref[...], b_ref[...], (dims, ((), ())), preferred_element_type=F32)
            acc = t if acc is None else acc + t
        vals = acc if epilogue is None else epilogue(acc, *[e[...] for e in e_refs])
        if not isinstance(vals, (list, tuple)):
            vals = (vals,)
        for o_ref, v in zip(o_refs[:n_out], vals[:n_out]):
            o_ref[...] = v.astype(o_ref.dtype)
        if reds:
            first = functools.reduce(jnp.logical_and, [pl.program_id(ax) == 0 for ax in range(len(grid))])
            for r_ref, v in zip(o_refs[n_out:], vals[n_out:]):
                @pl.when(first)
                def _(r_ref=r_ref):
                    r_ref[...] = jnp.zeros_like(r_ref)
                r_ref[...] += v

    sem = ["arbitrary" if (reds or rider is not None) else "parallel"] * len(grid)
    res = pl.pallas_call(
        body, out_shape=out_shapes, grid=grid, in_specs=in_specs, out_specs=out_specs,
        compiler_params=_params(*sem), name=name)(*operands)
    if rider is not None:
        rider["deliver"](res[n_out + len(reds):])
        res = res[:n_out + len(reds)]
    return res[0] if (single and not reds) else res


NN = ((1,), (0,))
NT = ((1,), (1,))
TN = ((0,), (0,))


def _mm_wgrad(name, a, b, *, a_cols, b_cols, tm, tn, J, deps=(), rider=None):
    def pick(arr, cols, t):
        if arr.ndim == 3:
            T, c = arr.shape[1], arr.shape[2]
            t = min(t, c)
            return T, c, t, (lambda sel: _bs((None, T, t), lambda j, i, k: (j, 0, sel(i, k))))
        T = arr.shape[0]
        c = arr.shape[1] if cols is None else cols
        t = min(t, c)
        per = c // t
        if cols is None:
            if per == 1:
                return T, c, t, (lambda sel: _resident((T, t), lambda j, i, k: (0, 0)))
            return T, c, t, (lambda sel: _bs((T, t), lambda j, i, k: (0, sel(i, k))))
        return T, c, t, (lambda sel: _bs((T, t), lambda j, i, k: (0, j * per + sel(i, k))))
    _, ca, tm, mk_a = pick(a, a_cols, tm)
    _, cb, tn, mk_b = pick(b, b_cols, tn)
    return _mm(name, (J, ca // tm, cb // tn),
               [(a, mk_a(lambda i, k: i), b, mk_b(lambda i, k: k))],
               jax.ShapeDtypeStruct((J, ca, cb), BF16), _bs((None, tm, tn), lambda j, i, k: (j, i, k)), TN, deps=deps, rider=rider)


def _tiled(arr, width=None, col=0, rowblk=0):
    return ("t", arr, arr.shape[1] if width is None else width, col, rowblk)


def _table(arr):
    return ("f", arr)


def _whole(arr):
    return ("w", arr)


def _ew(name, fn, ins, outs, *, n_rows, rows, reds=(), ncols=1, deps=()):
    nrb = n_rows // rows
    n_deps = len(deps)
    operands, in_specs = [], []
    for spec in ins:
        if spec[0] == "t":
            _, arr, width, col, rowblk = spec
            step = 1 if ncols > 1 else 0
            in_specs.append(_bs((rows, width), lambda c, i, col=col, rowblk=rowblk, step=step: (rowblk + i, col + c * step)))
        elif spec[0] == "f":
            arr = spec[1]
            in_specs.append(_bs((rows, arr.shape[1]), lambda c, i: (i, 0)))
        else:
            arr = spec[1]
            nd = arr.ndim
            if nd == 3:
                in_specs.append(_bs((None,) + arr.shape[1:], lambda c, i: (c, 0, 0)))
            else:
                in_specs.append(_bs(arr.shape, lambda c, i, nd=nd: (0,) * nd))
        operands.append(arr)
    out_shapes = [jax.ShapeDtypeStruct((n_rows, ncols * w), dt) for dt, w in outs]
    out_specs = [_bs((rows, w), lambda c, i: (i, c)) for _, w in outs]
    out_shapes += [jax.ShapeDtypeStruct((ncols, 1, w), F32) for w in reds]
    out_specs += [_bs((None, 1, w), lambda c, i: (c, 0, 0)) for w in reds]
    n_in, n_out, n_red = len(ins), len(outs), len(reds)
    operands += list(deps)
    in_specs += _any_specs(n_deps)

    def body(*refs):
        vals = fn(*[r[...] for r in refs[:n_in]])
        if not isinstance(vals, (tuple, list)):
            vals = (vals,)
        o_refs = refs[n_in + n_deps:]
        for o_ref, v in zip(o_refs[:n_out], vals[:n_out]):
            o_ref[...] = v.astype(o_ref.dtype)
        if n_red:
            i = pl.program_id(1)
            for r_ref, v in zip(o_refs[n_out:], vals[n_out:]):
                @pl.when(i == 0)
                def _(r_ref=r_ref):
                    r_ref[...] = jnp.zeros_like(r_ref)
                r_ref[...] += v

    res = pl.pallas_call(
        body, out_shape=out_shapes, grid=(ncols, nrb), in_specs=in_specs, out_specs=out_specs,
        compiler_params=_params("parallel", "arbitrary" if n_red else "parallel"), name=name)(*operands)
    return res


def _colsum(v):
    return jnp.sum(v, axis=0, keepdims=True)


def _rstd(x):
    return lax.rsqrt(jnp.mean(x * x, axis=-1, keepdims=True) + EPS)


def _sigmoid(x):
    return 1.0 / (1.0 + jnp.exp(-x))


def _norm_fwd(x, g):
    return x * _rstd(x) * g


def _norm_bwd(x, g, dy):
    r = _rstd(x)
    xh = x * r
    dxh = dy * g
    dx = r * (dxh - xh * jnp.mean(dxh * xh, axis=-1, keepdims=True))
    return dx, dy * xh


def _row_spec(arr, rows):
    if arr.shape[0] == 1:
        return _bs(arr.shape, lambda i: (0, 0))
    return _bs((rows, arr.shape[1]), lambda i: (i, 0))


def _ffn_fwd(tag, x, gain, get_w, deps=(), *, h=None, tail_ins=(), tail_fn=None, tail_outs=(F32,), tail_reds=()):
    T, D = x.shape
    if h is None:
        (h,) = _ew(f"{tag}_norm", lambda xv, g: _norm_fwd(xv, g), [_tiled(x), _whole(gain)], [(BF16, D)], n_rows=T, rows=512,
                   deps=deps)
    w1, w3 = get_w(f"{tag}_w1", h), get_w(f"{tag}_w3", h)
    J, f, _ = w1.shape
    tm = 1024

    def up(h_ref, w1_ref, w3_ref, u_ref, g_ref, a_ref):
        hv = h_ref[...]
        u = lax.dot_general(hv, w1_ref[...], (NT, ((), ())), preferred_element_type=F32)
        g = lax.dot_general(hv, w3_ref[...], (NT, ((), ())), preferred_element_type=F32)
        u_ref[...] = u.astype(BF16)
        g_ref[...] = g.astype(BF16)
        a_ref[...] = (u * _sigmoid(u) * g).astype(BF16)

    slab = _bs((None, tm, f), lambda j, i: (j, i, 0))
    w_spec = _bs((None, f, D), lambda j, i: (j, 0, 0))
    u, g, a = pl.pallas_call(
        up, out_shape=[jax.ShapeDtypeStruct((J, T, f), BF16)] * 3, grid=(J, T // tm),
        in_specs=[_bs((tm, D), lambda j, i: (i, 0)), w_spec, w_spec], out_specs=[slab] * 3,
        compiler_params=_params("parallel", "parallel"), name=f"{tag}_up")(h, w1, w3)
    w2 = get_w(f"{tag}_w2", a)
    def tail(acc, xv, *rest):
        y = xv + 0.5 * acc
        return y if tail_fn is None else tail_fn(y, *rest)

    row = _bs((512, D), lambda i: (i, 0))
    res = _mm(f"{tag}_down", (T // 512,),
              [(a, _bs((None, 512, f), lambda i, j=j: (j, i, 0)), w2, _resident((None, f, D), lambda i, j=j: (j, 0, 0)))
               for j in range(J)],
              [jax.ShapeDtypeStruct((T, D), dt) for dt in tail_outs], [row] * len(tail_outs), NN,
              extras=[(x, row)] + [(t, _row_spec(t, 512)) for t in tail_ins], epilogue=tail, reds=tail_reds)
    return res, (h, u, g, a)


def _dh_norm_bwd(name, rows, pairs, dims, x, gain, dres, deps, also_bf16=False, rider=None):
    T, D = x.shape

    def epilogue(dh, xv, gv, dr):
        dx, dgr = _norm_bwd(xv, gv, dh)
        dx = dx + dr
        return (dx, 0.5 * dx) + ((dx,) if also_bf16 else ()) + (_colsum(dgr),)

    dts = [F32, BF16] + ([BF16] if also_bf16 else [])
    row = _bs((rows, D), lambda i: (i, 0))
    return _mm(name, (T // rows,), pairs, [jax.ShapeDtypeStruct((T, D), dt) for dt in dts], [row] * len(dts), dims,
               extras=[(x, row), (gain, _row_spec(gain, rows)), (dres, row)], epilogue=epilogue, deps=deps, reds=(D,), rider=rider)


def _ffn_bwd(tag, x, gain, get_w, put_g, saved, dy, dy_half, also_bf16=False, last=False, take_rider=lambda steps, after: None):
    h, u, g, a = saved
    T, D = x.shape
    w1, w3, w2 = [get_w(f"{tag}_{n}", dy_half) for n in ("w1", "w3", "w2")]
    J, f, _ = w1.shape
    dw2 = _mm_wgrad(f"{tag}_bwd_dw2", a, dy_half, a_cols=None, b_cols=None, tm=f, tn=D, J=J, rider=take_rider(J, dy_half))
    deps = put_g({f"{tag}_w2": dw2}) if last else []
    tm = 1024

    def up_bwd(dy_ref, w2_ref, u_ref, g_ref, *rest):
        du_ref, dg_ref = rest[-2:]
        da = lax.dot_general(dy_ref[...], w2_ref[...], (NT, ((), ())), preferred_element_type=F32)
        uv, gv = u_ref[...].astype(F32), g_ref[...].astype(F32)
        s = _sigmoid(uv)
        du_ref[...] = (da * gv * (s * (1.0 + uv * (1.0 - s)))).astype(BF16)
        dg_ref[...] = (da * (uv * s)).astype(BF16)

    slab = _bs((None, tm, f), lambda j, i: (j, i, 0))
    du, dg = pl.pallas_call(
        up_bwd, out_shape=[jax.ShapeDtypeStruct((J, T, f), BF16)] * 2, grid=(J, T // tm),
        in_specs=[_bs((tm, D), lambda j, i: (i, 0)), _bs((None, f, D), lambda j, i: (j, 0, 0)), slab, slab] + _any_specs(len(deps)),
        out_specs=[slab] * 2, compiler_params=_params("parallel", "parallel"), name=f"{tag}_bwd_up")(dy_half, w2, u, g, *deps)
    dw1 = _mm_wgrad(f"{tag}_bwd_dw1", du, h, a_cols=None, b_cols=None, tm=f, tn=D, J=J)
    deps = put_g({f"{tag}_w1": dw1}) if last else []
    dw3 = _mm_wgrad(f"{tag}_bwd_dw3", dg, h, a_cols=None, b_cols=None, tm=f, tn=D, J=J, deps=deps)
    deps = put_g({f"{tag}_w3": dw3} if last else {f"{tag}_w2": dw2, f"{tag}_w1": dw1, f"{tag}_w3": dw3})
    pairs = []
    for j in range(J):
        a_spec = _bs((None, 256, f), lambda i, j=j: (j, i, 0))
        w_spec = _resident((None, f, D), lambda i, j=j: (j, 0, 0))
        pairs += [(du, a_spec, w1, w_spec), (dg, a_spec, w3, w_spec)]
    return _dh_norm_bwd(f"{tag}_bwd_dh", 256, pairs, NN, x, gain, dy, deps, also_bf16, rider=take_rider(T // 256, dw3))


def _t5_bucket(rel):
    n = N_BUCKETS // 2
    max_exact = n // 2
    ret = jnp.where(rel > 0, n, 0)
    a = jnp.abs(rel)
    af = jnp.maximum(a, 1).astype(F32)
    large = max_exact + (jnp.log(af / max_exact) / math.log(MAX_DISTANCE / max_exact) * (n - max_exact)).astype(jnp.int32)
    large = jnp.minimum(large, n - 1)
    return ret + jnp.where(a < max_exact, a, large)


WIN_A = QB_A + 2 * BAND_HALF
WIN_SHIFTS = (0, BAND_HALF, 2 * BAND_HALF)


def _window_variant(n, nblk):
    return jnp.where(n == 0, 0, jnp.where(n == nblk - 1, 2, 1))


def _window_start(n, nblk):
    return pl.multiple_of(jnp.clip(n * QB_A - BAND_HALF, 0, nblk * QB_A - WIN_A), BAND_HALF)


def _band_steps(xp=jnp):
    qi = xp.arange(QB_A, dtype=xp.int32)[None, :, None]
    kj = xp.arange(WIN_A, dtype=xp.int32)[None, None, :]
    return kj - qi - xp.asarray(WIN_SHIFTS, dtype=xp.int32)[:, None, None]


def _bias_tiles(rel_bias):
    wide = QB_A + 2 * WIN_SHIFTS[-1]
    qi = jnp.arange(QB_A, dtype=jnp.int32)[:, None]
    steps = jnp.arange(wide, dtype=jnp.int32)[None, :] - WIN_SHIFTS[-1] - qi
    buckets = jnp.stack([_t5_bucket(steps * d) for d in DILATIONS])
    inband = (jnp.abs(steps) <= BAND_HALF).astype(jnp.int32)
    n_heads = rel_bias.shape[1]

    def body(tab_ref, b_ref, m_ref, o_ref):
        hd = pl.program_id(0)
        bkt = b_ref[...]
        acc = jnp.zeros(bkt.shape, F32)
        for b in range(N_BUCKETS):
            acc = jnp.where(bkt == b, tab_ref[b, hd], acc)
        o_ref[...] = jnp.where(m_ref[...] > 0, acc, NEG_INF)

    base = pl.pallas_call(
        body, out_shape=jax.ShapeDtypeStruct((n_heads, QB_A, wide), F32), grid=(n_heads,),
        in_specs=[pl.BlockSpec(memory_space=pltpu.SMEM),
                  _bs((None, QB_A, wide), lambda hd: (hd // HEADS_A, 0, 0)),
                  _bs((QB_A, wide), lambda hd: (0, 0))],
        out_specs=_bs((None, QB_A, wide), lambda hd: (hd, 0, 0)),
        compiler_params=_params("parallel"), name="a_bias_tiles")(rel_bias, buckets, inband)
    base = base.reshape(len(DILATIONS), HEADS_A, QB_A, wide)
    return jnp.stack([base[..., WIN_SHIFTS[-1] - s:WIN_SHIFTS[-1] - s + WIN_A] for s in WIN_SHIFTS], axis=1)


def _bias_grad(dbias):
    steps = _band_steps(np)
    inband = np.abs(steps) <= BAND_HALF
    present = []
    for d in DILATIONS:
        rel = steps * d
        a = np.abs(rel)
        large = 8 + (np.log(np.maximum(a, 1) / 8.0) / math.log(MAX_DISTANCE / 8.0) * 8).astype(np.int64)
        bk = np.where(rel > 0, 16, 0) + np.where(a < 8, a, np.minimum(large, 15))
        present.append([sorted(set(bk[v][inband[v]].tolist())) for v in range(3)])
    buckets = jnp.stack([_t5_bucket(_band_steps() * d) for d in DILATIONS])
    n_heads = len(DILATIONS) * HEADS_A

    def body(b_ref, d_ref, o_ref):
        row = lax.broadcasted_iota(jnp.int32, (N_BUCKETS, n_heads), 0)
        col = lax.broadcasted_iota(jnp.int32, (N_BUCKETS, n_heads), 1)
        out = jnp.zeros((N_BUCKETS, n_heads), F32)
        for grp in range(len(DILATIONS)):
            for hh in range(HEADS_A):
                hd = grp * HEADS_A + hh
                for b in sorted(set(sum(present[grp], []))):
                    tot = jnp.zeros((), F32)
                    for v in range(3):
                        if b in present[grp][v]:
                            tot = tot + jnp.sum(jnp.where(b_ref[grp, v] == b, d_ref[grp, v, hh], 0.0))
                    out = jnp.where((row == b) & (col == hd), tot, out)
        o_ref[...] = out

    return pl.pallas_call(
        body, out_shape=jax.ShapeDtypeStruct((N_BUCKETS, n_heads), F32),
        compiler_params=pltpu.CompilerParams(vmem_limit_bytes=VMEM_LIMIT_BYTES), name="a_bias_grad")(buckets, dbias)


def _lane_is_second_head(shape):
    return lax.broadcasted_iota(jnp.int32, shape, len(shape) - 1) >= HEAD_A


VIEW_ROWS = 512


def _view_chunks():
    return [pltpu.VMEM((VIEW_ROWS, LANES), F32)] * (WIDTH_A // LANES)


def _rows_to_view(x_ref, col, o_ref, ocol, d, chunks):
    n = VIEW_ROWS // d
    for c, scr in enumerate(chunks):
        scr[...] = x_ref[:, col + c * LANES:col + (c + 1) * LANES].astype(F32)
        for r in range(d):
            at = ocol + r * WIDTH_A + c * LANES
            o_ref[:, at:at + LANES] = scr[pl.ds(r, n, stride=d), :].astype(o_ref.dtype)


def _view_to_rows(v_ref, o_ref, col, d, chunks):
    n = VIEW_ROWS // d
    for c, scr in enumerate(chunks):
        if d == 1:
            o_ref[:, col + c * LANES:col + (c + 1) * LANES] = v_ref[:, c * LANES:(c + 1) * LANES].astype(o_ref.dtype)
            continue
        for r in range(d):
            scr[pl.ds(r, n, stride=d), :] = v_ref[:, r * WIDTH_A + c * LANES:r * WIDTH_A + (c + 1) * LANES].astype(F32)
        o_ref[:, col + c * LANES:col + (c + 1) * LANES] = scr[...].astype(o_ref.dtype)


def _group_view(proj, grp, d):
    T = proj.shape[0]
    if d == 1:
        return proj, (lambda part, r: grp * 3 + part)

    def body(x_ref, o_ref, *chunks):
        for part in range(3):
            _rows_to_view(x_ref, part * WIDTH_A, o_ref, part * d * WIDTH_A, d, chunks)

    view = pl.pallas_call(
        body, out_shape=jax.ShapeDtypeStruct((T // d, 3 * d * WIDTH_A), proj.dtype), grid=(T // VIEW_ROWS,),
        in_specs=[_bs((VIEW_ROWS, 3 * WIDTH_A), lambda i: (i, grp))],
        out_specs=_bs((VIEW_ROWS // d, 3 * d * WIDTH_A), lambda i: (i, 0)),
        scratch_shapes=_view_chunks(), compiler_params=_params("parallel"), name=f"a_view_d{d}")(proj)
    return view, (lambda part, r: part * d + r)


def _stack_heads(v2, second):
    zero = jnp.zeros_like(v2)
    return jnp.concatenate([jnp.where(second, zero, v2), jnp.where(second, v2, zero)], axis=0)


def _unstack_heads(v, second):
    return jnp.where(second, v[QB_A:], v[:QB_A])


def _dil_fwd(view, bias, d):
    pv, colblk = view
    L = pv.shape[0]
    nblk = L // QB_A
    W2 = 2 * HEAD_A
    scale = HEAD_A ** -0.5

    def body(q_ref, k_ref, v_ref, b_ref, o_ref, l_ref):
        win = pl.ds(_window_start(pl.program_id(1), nblk), WIN_A)
        second = _lane_is_second_head((QB_A, W2))
        pairs = range(HEADS_A // 2)
        cols = [slice(hp * W2, (hp + 1) * W2) for hp in pairs]
        s = [lax.dot_general(_stack_heads(q_ref[:, cols[hp]], second), k_ref[win, cols[hp]], (NT, ((), ())),
                             preferred_element_type=F32) * scale + b_ref[2 * hp:2 * hp + 2].reshape(2 * QB_A, WIN_A)
             for hp in pairs]
        m = [jnp.max(x, axis=-1, keepdims=True) for x in s]
        p = [jnp.exp(x - mx) for x, mx in zip(s, m)]
        l = [jnp.sum(x, axis=-1, keepdims=True) for x in p]
        res = [jnp.dot(p[hp].astype(BF16), v_ref[win, cols[hp]], preferred_element_type=F32) / l[hp] for hp in pairs]
        o_ref[...] = jnp.concatenate([_unstack_heads(x, second) for x in res], axis=1).astype(o_ref.dtype)
        l_ref[...] = jnp.concatenate([_unstack_heads(jnp.broadcast_to(mx + jnp.log(lx), (2 * QB_A, W2)), second)
                                      for mx, lx in zip(m, l)], axis=1)

    in_specs = [_bs((QB_A, WIDTH_A), lambda r, n: (n, colblk(0, r))),
                _bs((L, WIDTH_A), lambda r, n: (0, colblk(1, r))), _bs((L, WIDTH_A), lambda r, n: (0, colblk(2, r))),
                _bs((None, HEADS_A, QB_A, WIN_A), lambda r, n: (_window_variant(n, nblk), 0, 0, 0))]
    o, lse = pl.pallas_call(
        body, out_shape=[jax.ShapeDtypeStruct((L, d * WIDTH_A), BF16), jax.ShapeDtypeStruct((L, d * WIDTH_A), F32)],
        grid=(d, nblk), in_specs=in_specs,
        out_specs=[_bs((QB_A, WIDTH_A), lambda r, n: (n, r)), _bs((QB_A, WIDTH_A), lambda r, n: (n, r))],
        compiler_params=_params("parallel", "parallel"), name=f"a_fwd_d{d}")(pv, pv, pv, bias)
    return o, lse


def _dil_bwd(view_qkv, bias, do, lse, cterm, d):
    pv, colblk = view_qkv
    L = pv.shape[0]
    nblk = L // QB_A
    W2 = 2 * HEAD_A
    PPS = 4
    WS = PPS * W2
    ob = WIDTH_A // WS
    scale = HEAD_A ** -0.5

    def body(q_ref, k_ref, v_ref, do_ref, l_ref, c_ref, b_ref, dq_ref, dk_ref, dv_ref, db_ref):
        r, n = pl.program_id(1), pl.program_id(2)

        @pl.when(n == 0)
        def _():
            dk_ref[...] = jnp.zeros_like(dk_ref)
            dv_ref[...] = jnp.zeros_like(dv_ref)

        @pl.when((n == 0) & (r == 0))
        def _():
            db_ref[...] = jnp.zeros_like(db_ref)

        second = _lane_is_second_head((QB_A, W2))
        win = pl.ds(_window_start(n, nblk), WIN_A)
        variant = _window_variant(n, nblk)
        pairs = range(PPS)
        cols = [slice(pp * W2, (pp + 1) * W2) for pp in pairs]

        def head_rows(ref, pp):
            v2 = ref[:, cols[pp]]
            return jnp.concatenate([v2[:, 0:1], v2[:, HEAD_A:HEAD_A + 1]], axis=0)

        kw = [k_ref[win, c] for c in cols]
        vw = [v_ref[win, c] for c in cols]
        qs = [_stack_heads(q_ref[:, c], second) for c in cols]
        dos = [_stack_heads(do_ref[:, c], second) for c in cols]
        s = [lax.dot_general(qs[pp], kw[pp], (NT, ((), ())), preferred_element_type=F32) for pp in pairs]
        dp = [lax.dot_general(dos[pp], vw[pp], (NT, ((), ())), preferred_element_type=F32) for pp in pairs]
        p = [jnp.exp(s[pp] * scale + b_ref[2 * pp:2 * pp + 2].reshape(2 * QB_A, WIN_A) - head_rows(l_ref, pp)) for pp in pairs]
        ds = [p[pp] * (dp[pp] + head_rows(c_ref, pp)) for pp in pairs]
        db_ref[variant] += jnp.concatenate([x.reshape(2, QB_A, WIN_A) for x in ds], axis=0)
        pb = [x.astype(BF16) for x in p]
        dsb = [(x * scale).astype(BF16) for x in ds]
        dq_ref[...] = jnp.concatenate([_unstack_heads(jnp.dot(dsb[pp], kw[pp], preferred_element_type=F32), second)
                                       for pp in pairs], axis=1).astype(dq_ref.dtype)
        dk_ref[win, :] += jnp.concatenate([lax.dot_general(dsb[pp], qs[pp], (TN, ((), ())), preferred_element_type=F32)
                                           for pp in pairs], axis=1)
        dv_ref[win, :] += jnp.concatenate([lax.dot_general(pb[pp], dos[pp], (TN, ((), ())), preferred_element_type=F32)
                                           for pp in pairs], axis=1)

    kv_spec = _resident if d == 1 else _bs
    in_specs = [_bs((QB_A, WS), lambda hp, r, n: (n, colblk(0, r) * ob + hp)),
                kv_spec((L, WS), lambda hp, r, n: (0, colblk(1, r) * ob + hp)),
                kv_spec((L, WS), lambda hp, r, n: (0, colblk(2, r) * ob + hp))]
    in_specs += [_bs((QB_A, WS), lambda hp, r, n: (n, r * ob + hp))] * 3
    in_specs += [_bs((None, 2 * PPS, QB_A, WIN_A), lambda hp, r, n: (_window_variant(n, nblk), hp, 0, 0))]
    out_shape = [jax.ShapeDtypeStruct((L, d * WIDTH_A), BF16), jax.ShapeDtypeStruct((L, d * WIDTH_A), F32),
                 jax.ShapeDtypeStruct((L, d * WIDTH_A), F32), jax.ShapeDtypeStruct((3, HEADS_A, QB_A, WIN_A), F32)]
    out_specs = [_bs((QB_A, WS), lambda hp, r, n: (n, r * ob + hp)),
                 _bs((L, WS), lambda hp, r, n: (0, r * ob + hp)), _bs((L, WS), lambda hp, r, n: (0, r * ob + hp)),
                 _bs((3, 2 * PPS, QB_A, WIN_A), lambda hp, r, n: (0, hp, 0, 0))]
    dq, dk, dv, db = pl.pallas_call(
        body, out_shape=out_shape, grid=(ob, d, nblk), in_specs=in_specs, out_specs=out_specs,
        compiler_params=_params("arbitrary", "arbitrary", "arbitrary"), name=f"a_bwd_d{d}")(
            pv, pv, pv, do, lse, cterm, bias)
    return dq, dk, dv, db


def _assemble_dproj(a_parts, dq_b, dk_b, dv_b, dga, dgb):
    T = dq_b.shape[0]
    flat = [(a_parts[part][g], d) for part in range(3) for g, d in enumerate(DILATIONS)]
    rest = [dq_b, dk_b, dv_b, dga, dgb]

    def body(*refs):
        views, others = refs[:len(flat)], refs[len(flat):len(flat) + len(rest)]
        o_ref, chunks = refs[len(flat) + len(rest)], refs[len(flat) + len(rest) + 1:]
        col = 0
        for v_ref, (_, d) in zip(views, flat):
            _view_to_rows(v_ref, o_ref, col, d, chunks)
            col += WIDTH_A
        for x_ref in others:
            w = x_ref.shape[1]
            o_ref[:, col:col + w] = x_ref[...].astype(o_ref.dtype)
            col += w

    in_specs = [_bs((VIEW_ROWS // d, d * WIDTH_A), lambda i: (i, 0)) for _, d in flat]
    in_specs += [_bs((VIEW_ROWS, x.shape[1]), lambda i: (i, 0)) for x in rest]
    return pl.pallas_call(
        body, out_shape=jax.ShapeDtypeStruct((T, IN_WIDTH), BF16), grid=(T // VIEW_ROWS,), in_specs=in_specs,
        out_specs=_bs((VIEW_ROWS, IN_WIDTH), lambda i: (i, 0)), scratch_shapes=_view_chunks(),
        compiler_params=_params("parallel"), name="mix_bwd_dproj")(*[a for a, _ in flat], *rest)


def _segment_ones():
    i = np.arange(WIDTH_A)
    return jnp.asarray((i[:, None] // HEAD_A == i[None, :] // HEAD_A).astype(np.float32), dtype=BF16)


def _group_weights(l0, l1, l2):
    m = jnp.maximum(jnp.maximum(l0, l1), l2)
    e = [jnp.exp(l - m) for l in (l0, l1, l2)]
    z = e[0] + e[1] + e[2]
    return [ei / z for ei in e]


def _view_specs():
    return [_bs((VIEW_ROWS // d, d * WIDTH_A), lambda i: (i, 0)) for d in DILATIONS]


def _stage_tiles(n):
    return [pltpu.VMEM((VIEW_ROWS, WIDTH_A), F32)] * n


def _combine_fwd(outs, lses):
    T = outs[0].shape[0] * DILATIONS[0]
    n = len(DILATIONS)

    def body(*refs):
        o_refs, l_refs, oa_ref = refs[:n], refs[n:2 * n], refs[2 * n]
        o_st, l_st, chunks = refs[2 * n + 1:3 * n + 1], refs[3 * n + 1:4 * n + 1], refs[4 * n + 1:]
        for g, d in enumerate(DILATIONS):
            _view_to_rows(o_refs[g], o_st[g], 0, d, chunks)
            _view_to_rows(l_refs[g], l_st[g], 0, d, chunks)
        w = _group_weights(*[l[...] for l in l_st])
        oa_ref[...] = (w[0] * o_st[0][...] + w[1] * o_st[1][...] + w[2] * o_st[2][...]).astype(oa_ref.dtype)

    return pl.pallas_call(
        body, out_shape=jax.ShapeDtypeStruct((T, WIDTH_A), BF16), grid=(T // VIEW_ROWS,),
        in_specs=_view_specs() * 2, out_specs=_bs((VIEW_ROWS, WIDTH_A), lambda i: (i, 0)),
        scratch_shapes=_stage_tiles(2 * n) + _view_chunks(), compiler_params=_params("parallel"), name="a_combine")(*outs, *lses)


def _combine_bwd(doa, outs, lses):
    T = doa.shape[0]
    n = len(DILATIONS)

    def body(*refs):
        d_ref, o_refs, l_refs, seg_ref = refs[0], refs[1:n + 1], refs[n + 1:2 * n + 1], refs[2 * n + 1]
        do_refs, c_refs = refs[2 * n + 2:3 * n + 2], refs[3 * n + 2:4 * n + 2]
        o_st, l_st = refs[4 * n + 2:5 * n + 2], refs[5 * n + 2:6 * n + 2]
        tmp, chunks = refs[6 * n + 2], refs[6 * n + 3:]
        for g, d in enumerate(DILATIONS):
            _view_to_rows(o_refs[g], o_st[g], 0, d, chunks)
            _view_to_rows(l_refs[g], l_st[g], 0, d, chunks)
        dv = d_ref[...].astype(F32)
        w = _group_weights(*[l[...] for l in l_st])
        seg = seg_ref[...]
        tot = jnp.zeros(dv.shape, F32)
        for g in range(n):
            prod = w[g] * dv * o_st[g][...]
            hi = prod.astype(BF16)
            lo = (prod - hi.astype(F32)).astype(BF16)
            tot = tot + jnp.dot(hi, seg, preferred_element_type=F32) + jnp.dot(lo, seg, preferred_element_type=F32)
        for g, d in enumerate(DILATIONS):
            tmp[...] = w[g] * dv
            _rows_to_view(tmp, 0, do_refs[g], 0, d, chunks)
            tmp[...] = -w[g] * tot
            _rows_to_view(tmp, 0, c_refs[g], 0, d, chunks)

    views = [jax.ShapeDtypeStruct((T // d, d * WIDTH_A), dt) for dt in (BF16, F32) for d in DILATIONS]
    res = pl.pallas_call(
        body, out_shape=views, grid=(T // VIEW_ROWS,),
        in_specs=[_bs((VIEW_ROWS, WIDTH_A), lambda i: (i, 0))] + _view_specs() * 2 + [_bs((WIDTH_A, WIDTH_A), lambda i: (0, 0))],
        out_specs=_view_specs() * 2, scratch_shapes=_stage_tiles(2 * n + 1) + _view_chunks(),
        compiler_params=_params("parallel"), name="a_combine_bwd")(doa, *outs, *lses, _segment_ones())
    return res[:n], res[n:]


def _rope_tables(T):
    rows = T // GRID_W
    row = jnp.repeat(jnp.arange(rows, dtype=F32), GRID_W)
    col = jnp.tile(jnp.arange(GRID_W, dtype=F32), rows)
    n_freq = HEAD_B // 4
    freq = ROPE_THETA ** (-jnp.arange(n_freq, dtype=F32) / n_freq)
    ang = jnp.concatenate([row[:, None] * freq, col[:, None] * freq], axis=-1)
    cos, sin = jnp.repeat(jnp.cos(ang), 2, axis=1), jnp.repeat(jnp.sin(ang), 2, axis=1)
    sign = jnp.where(jnp.arange(HEAD_B) % 2 == 0, -1.0, 1.0).astype(F32)
    return cos, sin * sign


def _swap_pairs(v):
    even = lax.broadcasted_iota(jnp.int32, v.shape, v.ndim - 1) % 2 == 0
    n = v.shape[-1]
    return jnp.where(even, pltpu.roll(v, n - 1, v.ndim - 1), pltpu.roll(v, 1, v.ndim - 1))


def _qk_fwd(name, proj, col0, n_heads, gain, cos, sin, out_scale=1.0, deps=()):
    T = proj.shape[0]

    def fn(xr, g, c, s):
        xn = _norm_fwd(xr.astype(F32), g)
        return (xn * c + _swap_pairs(xn) * s) * out_scale

    (out,) = _ew(name, fn, [_tiled(proj, HEAD_B, col0 // HEAD_B), _whole(gain), _table(cos), _table(sin)],
                 [(BF16, HEAD_B)], n_rows=T, rows=2048, ncols=n_heads, deps=deps)
    return out


def _qk_bwd(name, dout, proj, col0, n_heads, gain, cos, sin, in_scale=1.0):
    T = proj.shape[0]

    def fn(dv, xr, g, c, s):
        dv = dv.astype(F32) * in_scale
        dxn = c * dv + _swap_pairs(s * dv)
        dx, dgr = _norm_bwd(xr.astype(F32), g, dxn)
        return dx, _colsum(dgr)

    dx, dg = _ew(name, fn, [_tiled(dout, HEAD_B, 0), _tiled(proj, HEAD_B, col0 // HEAD_B), _whole(gain),
                            _table(cos), _table(sin)],
                 [(BF16, HEAD_B)], n_rows=T, rows=2048, reds=(HEAD_B,), ncols=n_heads)
    return dx, jnp.sum(dg, axis=0)


def _gqa_fwd(qn, kn, proj, k_col=0):
    T = qn.shape[0]
    GW = 4 * HEAD_B
    QB = QB_B

    def body(q_ref, k_ref, v_ref, o_ref, l_ref):
        k = k_ref[...]
        v_ones = jnp.concatenate([v_ref[...], jnp.ones((T, HEAD_B), BF16)], axis=1)
        lane = lax.broadcasted_iota(jnp.int32, (QB, HEAD_B), 1)
        heads = range(4)
        s = [lax.dot_general(q_ref[:, g * HEAD_B:(g + 1) * HEAD_B], k, (NT, ((), ())), preferred_element_type=F32)
             for g in heads]
        m = [jnp.max(x, axis=-1, keepdims=True) for x in s]
        pv = [jnp.dot(jnp.exp2(x - mx).astype(BF16), v_ones, preferred_element_type=F32) for x, mx in zip(s, m)]
        l = [x[:, HEAD_B:HEAD_B + 1] for x in pv]
        o = [x[:, :HEAD_B] / lx for x, lx in zip(pv, l)]
        o_ref[...] = jnp.concatenate(o, axis=1).astype(o_ref.dtype)
        lse_all = jnp.zeros((QB, HEAD_B), F32)
        for g in heads:
            lse_all = jnp.where(lane == g, m[g] + jnp.log2(l[g]), lse_all)
        l_ref[...] = lse_all

    return pl.pallas_call(
        body, out_shape=[jax.ShapeDtypeStruct((T, 2 * GW), BF16), jax.ShapeDtypeStruct((2, T, HEAD_B), F32)],
        grid=(2, T // QB),
        in_specs=[_bs((QB, GW), lambda kv, i: (i, kv)), _bs((T, HEAD_B), lambda kv, i: (0, k_col + kv)),
                  _bs((T, HEAD_B), lambda kv, i: (0, B_V // HEAD_B + kv))],
        out_specs=[_bs((QB, GW), lambda kv, i: (i, kv)), _bs((None, QB, HEAD_B), lambda kv, i: (kv, i, 0))],
        compiler_params=_params("parallel", "parallel"), name="b_fwd")(qn, kn, proj)


def _gqa_bwd(qn, kn, proj, o, lse, do, deps=(), k_col=0):
    T = qn.shape[0]
    GW = 4 * HEAD_B

    def body(q_ref, k_ref, v_ref, o_ref, l_ref, do_ref, *rest):
        dq_ref, dk_ref, dv_ref = rest[-3:]
        i = pl.program_id(1)

        @pl.when(i == 0)
        def _():
            dk_ref[...] = jnp.zeros_like(dk_ref)
            dv_ref[...] = jnp.zeros_like(dv_ref)

        k, v = k_ref[...], v_ref[...]
        lse_all = l_ref[...]
        for g in range(4):
            cols = slice(g * HEAD_B, (g + 1) * HEAD_B)
            q, dob = q_ref[:, cols], do_ref[:, cols]
            delta = jnp.sum(dob.astype(F32) * o_ref[:, cols].astype(F32), axis=-1, keepdims=True)
            s = lax.dot_general(q, k, (NT, ((), ())), preferred_element_type=F32)
            p = jnp.exp2(s - lse_all[:, g:g + 1])
            dp = lax.dot_general(dob, v, (NT, ((), ())), preferred_element_type=F32)
            ds = (p * (dp - delta)).astype(BF16)
            dq_ref[:, cols] = jnp.dot(ds, k, preferred_element_type=F32).astype(dq_ref.dtype)
            dk_ref[...] += lax.dot_general(ds, q, (TN, ((), ())), preferred_element_type=F32)
            dv_ref[...] += lax.dot_general(p.astype(BF16), dob, (TN, ((), ())), preferred_element_type=F32)

    return pl.pallas_call(
        body, out_shape=[jax.ShapeDtypeStruct((T, 2 * GW), BF16), jax.ShapeDtypeStruct((T, 2 * HEAD_B), F32),
                         jax.ShapeDtypeStruct((T, 2 * HEAD_B), F32)],
        grid=(2, T // QB_B),
        in_specs=[_bs((QB_B, GW), lambda kv, i: (i, kv)), _bs((T, HEAD_B), lambda kv, i: (0, k_col + kv)),
                  _bs((T, HEAD_B), lambda kv, i: (0, B_V // HEAD_B + kv)), _bs((QB_B, GW), lambda kv, i: (i, kv)),
                  _bs((None, QB_B, HEAD_B), lambda kv, i: (kv, i, 0)), _bs((QB_B, GW), lambda kv, i: (i, kv))] + _any_specs(len(deps)),
        out_specs=[_bs((QB_B, GW), lambda kv, i: (i, kv)), _bs((T, HEAD_B), lambda kv, i: (0, kv)),
                   _bs((T, HEAD_B), lambda kv, i: (0, kv))],
        compiler_params=_params("parallel", "arbitrary"), name="b_bwd")(qn, kn, proj, o, lse, do, *deps)


def _local_step(x, target, small, get_w, put_g, deps=(), prefetch_w=lambda name, after: [], take_rider=lambda steps, after: None):
    T, D = x.shape
    gs = {}

    bias = _bias_tiles(small["rel_bias"])
    cos, sin = _rope_tables(T)
    (x1, h2), ffn1_saved = _ffn_fwd("ffn1", x, small["ffn1_norm"], lambda name, after: get_w(name, [after, bias, cos, sin]), deps,
                                    tail_ins=[small["mix_norm"]], tail_fn=lambda y, g: (y, _norm_fwd(y, g)), tail_outs=(F32, BF16))
    w_in = get_w("w_in", h2)
    nq = w_in.shape[2]
    tpq = nq // WIDTH_A

    def proj_tile(j, k):
        c = j * tpq + k
        return jnp.where(c < 3 * len(DILATIONS), (c % 3) * 3 + c // 3, c)

    proj = _mm("mix_in", (4, tpq),
               [(h2, _resident((T, D), lambda j, k: (0, 0)), w_in, _bs((None, D, WIDTH_A), lambda j, k: (j, 0, k)))],
               jax.ShapeDtypeStruct((T, IN_WIDTH), BF16), _bs((T, WIDTH_A), lambda j, k: (0, proj_tile(j, k))), NN)

    a_views = [_group_view(proj, grp, d) for grp, d in enumerate(DILATIONS)]
    a_outs, a_lses = [], []
    for grp, d in enumerate(DILATIONS):
        o, l = _dil_fwd(a_views[grp], bias[grp], d)
        a_outs.append(o)
        a_lses.append(l)
    o_a = _combine_fwd(a_outs, a_lses)

    qk_gain = jnp.concatenate([jnp.tile(small["q_norm"] * QK_SCALE_LOG2, (8, 1)), jnp.tile(small["k_norm"], (2, 1))])[:, None, :]
    qkn = _qk_fwd("b_qknorm", proj, B_Q, 10, qk_gain, cos, sin, deps=prefetch_w("w_branch_a", proj))
    qn, kn, k_col = qkn, qkn, 8
    o_b, lse_b = _gqa_fwd(qn, kn, proj, k_col)
    ahead = prefetch_w("ffn2_w1", o_b)

    wa, wb, wo = get_w("w_branch_a", o_b), get_w("w_branch_b", o_b), get_w("w_out", o_b)
    bg_a, bg_b = small["b_gate"][:, :D], small["b_gate"][:, D:]
    n_a = wa.shape[0]

    def merge_out(oa_ref, ob_ref, ga_ref, gb_ref, x1_ref, wa_ref, wb_ref, wo_ref, ba_ref, bb_ref, g2_ref, *rest):
        ta_ref, tb_ref, mg_ref, x2_ref, hn_ref = rest[-5:]
        oa = oa_ref[...]
        ta = jnp.concatenate([jnp.dot(oa, wa_ref[j], preferred_element_type=F32) for j in range(n_a)], axis=1)
        tb = jnp.dot(ob_ref[...], wb_ref[...], preferred_element_type=F32)
        sa = _sigmoid(ga_ref[...].astype(F32) + ba_ref[...])
        sb = _sigmoid(gb_ref[...].astype(F32) + bb_ref[...])
        merged = (sa * ta + sb * tb).astype(BF16)
        ta_ref[...], tb_ref[...], mg_ref[...] = ta.astype(BF16), tb.astype(BF16), merged
        y = x1_ref[...] + jnp.dot(merged, wo_ref[...], preferred_element_type=F32)
        x2_ref[...] = y
        hn_ref[...] = _norm_fwd(y, g2_ref[...]).astype(BF16)

    row = _bs((512, D), lambda i: (i, 0))
    gate_specs = [_bs((512, D), lambda i: (i, G_A // D)), _bs((512, D), lambda i: (i, G_B // D))]
    whole2, whole3 = (lambda i: (0, 0)), (lambda i: (0, 0, 0))
    vec = _bs((1, D), whole2)
    t_a, t_b, merged, x2, hn2 = pl.pallas_call(
        merge_out, out_shape=[jax.ShapeDtypeStruct((T, D), BF16)] * 3 + [jax.ShapeDtypeStruct((T, D), F32), jax.ShapeDtypeStruct((T, D), BF16)],
        grid=(T // 512,),
        in_specs=[_bs((512, WIDTH_A), lambda i: (i, 0)), row] + gate_specs + [row, _resident(wa.shape, whole3), _resident((D, D), whole2),
                                                                                _resident((D, D), whole2), vec, vec, vec]
        + _any_specs(len(ahead)),
        out_specs=[row] * 5, compiler_params=_params("parallel"), name="mix_merge_out")(
            o_a, o_b, proj, proj, x1, wa, wb, wo, bg_a, bg_b, small["ffn2_norm"], *ahead)

    def head(xv, g, tv):
        r = _rstd(xv)
        xh = xv * r
        e = xh * g - tv
        dy = e * (1.0 / D)
        dxh = dy * g
        dx = r * (dxh - xh * jnp.mean(dxh * xh, axis=-1, keepdims=True))
        return dx, 0.5 * dx, _colsum(e * e) * (0.5 / D), _colsum(dy * xh)

    (dx3, dx3_half, loss_cols, g_final), ffn2_saved = _ffn_fwd(
        "ffn2", x2, small["ffn2_norm"], get_w, h=hn2, tail_ins=[small["final_norm"].reshape(1, D), target], tail_fn=head,
        tail_outs=(F32, BF16), tail_reds=(D, D))
    gs["final_norm"] = g_final.reshape(D)

    dx2, _, dmix, gs["ffn2_norm"] = _ffn_bwd("ffn2", x2, small["ffn2_norm"], get_w, put_g, ffn2_saved, dx3, dx3_half,
                                             also_bf16=True)
    g_out = _mm_wgrad("mix_bwd_dwout", merged, dmix, a_cols=D // 4, b_cols=None, tm=256, tn=512, J=4).reshape(D, D)

    def merge_out_bwd(dx_ref, ta_ref, tb_ref, ga_ref, gb_ref, wa_ref, wb_ref, wo_ref, ba_ref, bb_ref,
                      dta_ref, dtb_ref, dga_ref, dgb_ref, doa_ref, dob_ref, dba_ref, dbb_ref):
        dm = lax.dot_general(dx_ref[...], wo_ref[...], (NT, ((), ())), preferred_element_type=F32)
        ta, tb = ta_ref[...].astype(F32), tb_ref[...].astype(F32)
        sa = _sigmoid(ga_ref[...].astype(F32) + ba_ref[...])
        sb = _sigmoid(gb_ref[...].astype(F32) + bb_ref[...])
        dga, dgb = dm * ta * sa * (1.0 - sa), dm * tb * sb * (1.0 - sb)
        dta, dtb = (dm * sa).astype(BF16), (dm * sb).astype(BF16)
        dta_ref[...], dtb_ref[...] = dta, dtb
        dga_ref[...], dgb_ref[...] = dga.astype(BF16), dgb.astype(BF16)
        w = wa_ref.shape[2]
        doa = sum(lax.dot_general(dta[:, j * w:(j + 1) * w], wa_ref[j], (NT, ((), ())), preferred_element_type=F32) for j in range(n_a))
        doa_ref[...] = doa.astype(BF16)
        dob_ref[...] = lax.dot_general(dtb, wb_ref[...], (NT, ((), ())), preferred_element_type=F32).astype(BF16)

        @pl.when(pl.program_id(0) == 0)
        def _():
            dba_ref[...] = jnp.zeros_like(dba_ref)
            dbb_ref[...] = jnp.zeros_like(dbb_ref)
        dba_ref[...] += _colsum(dga)
        dbb_ref[...] += _colsum(dgb)

    rowb = _bs((256, D), lambda i: (i, 0))
    gate_specs = [_bs((256, D), lambda i: (i, G_A // D)), _bs((256, D), lambda i: (i, G_B // D))]
    dta, dtb, dga, dgb, do_a, do_b, dba, dbb = pl.pallas_call(
        merge_out_bwd,
        out_shape=[jax.ShapeDtypeStruct((T, D), BF16)] * 4 + [jax.ShapeDtypeStruct((T, WIDTH_A), BF16), jax.ShapeDtypeStruct((T, D), BF16)]
        + [jax.ShapeDtypeStruct((1, D), F32)] * 2,
        grid=(T // 256,),
        in_specs=[rowb, rowb, rowb] + gate_specs + [_resident(wa.shape, whole3), _resident((D, D), whole2), _resident((D, D), whole2), vec, vec],
        out_specs=[rowb] * 4 + [_bs((256, WIDTH_A), lambda i: (i, 0)), rowb, vec, vec],
        compiler_params=_params("arbitrary"), name="mix_merge_out_bwd")(dmix, t_a, t_b, proj, proj, wa, wb, wo, bg_a, bg_b)
    gs["b_gate"] = jnp.concatenate([dba, dbb], axis=1)

    g_a = _mm_wgrad("mix_bwd_dwa", o_a, dta, a_cols=None, b_cols=D // 4, tm=WIDTH_A, tn=256, J=4)
    g_b = _mm_wgrad("mix_bwd_dwb", o_b, dtb, a_cols=D // 4, b_cols=None, tm=256, tn=512, J=4).reshape(D, D)
    deps = put_g({"w_out": g_out, "w_branch_a": g_a, "w_branch_b": g_b})

    dqn, dkn, dv_b = _gqa_bwd(qn, kn, proj, o_b, lse_b, do_b, deps, k_col)
    dq_b, gs["q_norm"] = _qk_bwd("b_bwd_qnorm", dqn, proj, B_Q, 8, small["q_norm"], cos, sin, in_scale=HEAD_B ** -0.5)
    dk_b, gs["k_norm"] = _qk_bwd("b_bwd_knorm", dkn, proj, B_K, 2, small["k_norm"], cos, sin, in_scale=1.0 / LOG2_E)

    do_groups, c_groups = _combine_bwd(do_a, a_outs, a_lses)
    dqs, dks, dvs, dbs = [], [], [], []
    for grp, d in enumerate(DILATIONS):
        dq, dk, dv, db = _dil_bwd(a_views[grp], bias[grp], do_groups[grp], a_lses[grp], c_groups[grp], d)
        dqs.append(dq), dks.append(dk), dvs.append(dv), dbs.append(db)
    gs["rel_bias"] = _bias_grad(jnp.stack(dbs))

    dproj = _assemble_dproj([dqs, dks, dvs], dq_b, dk_b, dv_b, dga, dgb)
    nq = w_in.shape[2]
    g_in = _mm("mix_bwd_dwin", (4, tpq),
               [(h2, _resident((T, D), lambda j, k: (0, 0)), dproj, _bs((T, WIDTH_A), lambda j, k: (0, j * tpq + k)))],
               jax.ShapeDtypeStruct((4, D, nq), BF16), _bs((None, D, WIDTH_A), lambda j, k: (j, 0, k)), TN)
    deps = put_g({"w_in": g_in})
    dx1, dx1_half, gs["mix_norm"] = _dh_norm_bwd(
        "mix_bwd_dh", 256,
        [(dproj, _bs((256, nq), lambda i, j=j: (i, j)), w_in, _resident((None, D, nq), lambda i, j=j: (j, 0, 0))) for j in range(4)],
        NT, x1, small["mix_norm"], dx2, deps)

    dx0, _, gs["ffn1_norm"] = _ffn_bwd("ffn1", x, small["ffn1_norm"], get_w, put_g, ffn1_saved, dx1, dx1_half, last=True,
                                       take_rider=take_rider)
    return loss_cols, dx0, gs


def _position():
    return lax.axis_index("x"), lax.axis_index("y"), lax.axis_index("c")


def _any_specs(n):
    return [pl.BlockSpec(memory_space=pl.ANY)] * n


HBM_SPEC = pl.BlockSpec(memory_space=pltpu.HBM)
SEM_SPEC = pl.BlockSpec(memory_space=pltpu.SEMAPHORE)
DATAFLOW_EFFECT = pltpu.SideEffectType.DATAFLOW_SIDE_EFFECTING
N_PEER_CHIPS = 3
LANES = 128


def _quarter_copies(srcs, lands, send_sems, recv_sems, mode):
    x, y, c = _position()
    me = 2 * x + y
    peers = [(1 - x, y, c), (x, 1 - y, c), (1 - x, 1 - y, c)]
    copies = []
    for src, land, send, recv in zip(srcs, lands, send_sems, recv_sems):
        if mode == "sibling":
            copies.append(pltpu.make_async_remote_copy(src_ref=src, dst_ref=land, send_sem=send.at[0], recv_sem=recv.at[0],
                                                       device_id=(x, y, 1 - c), device_id_type=MESH))
            continue
        if mode == "fill":
            half = land.shape[1] // 2
            for p, (px, py, _) in enumerate(peers):
                part = land.at[2 * px + py, pl.ds(c * half, half)]
                copies.append(pltpu.make_async_remote_copy(src_ref=part, dst_ref=part, send_sem=send.at[p], recv_sem=recv.at[p],
                                                           device_id=(x, y, 1 - c), device_id_type=MESH))
            continue
        scatter = mode == "scatter"
        half = land.shape[1] // 2
        mine = land.at[me, pl.ds(c * half, half)]
        for p, (px, py, pc) in enumerate(peers):
            copies.append(pltpu.make_async_remote_copy(
                src_ref=src.at[2 * px + py] if scatter else mine, dst_ref=land.at[me] if scatter else mine,
                send_sem=send.at[p], recv_sem=recv.at[p], device_id=(px, py, pc), device_id_type=MESH))
    return copies


def _fill_from_sibling(name, stacks):
    n = len(stacks)

    def body(*refs):
        outs = refs[n:2 * n]
        send_sems, recv_sems = refs[2 * n:]
        x, y, c = _position()
        copies = []
        for i, ref in enumerate(outs):
            half = ref.shape[1] // 2
            rows = pl.ds(c * half, half)
            for p, k in enumerate((2 * (1 - x) + y, 2 * x + (1 - y), 2 * (1 - x) + (1 - y))):
                cp = pltpu.make_async_remote_copy(ref.at[k, rows], ref.at[k, rows], send_sems.at[3 * i + p], recv_sems.at[3 * i + p],
                                                  device_id=(x, y, 1 - c), device_id_type=MESH)
                cp.start()
                copies.append(cp)
        for cp in copies:
            cp.wait()

    return pl.pallas_call(
        body, out_shape=[jax.ShapeDtypeStruct(s.shape, s.dtype) for s in stacks],
        in_specs=_any_specs(n), out_specs=_any_specs(n), input_output_aliases={i: i for i in range(n)},
        scratch_shapes=[pltpu.SemaphoreType.DMA((N_PEER_CHIPS * n,)), pltpu.SemaphoreType.DMA((N_PEER_CHIPS * n,))],
        compiler_params=pltpu.CompilerParams(has_side_effects=True), name=name)(*stacks)


def _exchange_start(name, srcs, lands, mode):
    n = len(lands)
    arrays = list(lands) if srcs is None else list(srcs) + list(lands)
    k = len(arrays)

    def body(*refs):
        land_refs = refs[k - n:k]
        send_sems, recv_sems = refs[k:k + n], refs[k + n:k + 2 * n]
        token = refs[2 * k + 2 * n]
        for cp in _quarter_copies(refs[:n], land_refs, send_sems, recv_sems, mode):
            cp.start()
        token[...] = jnp.zeros_like(token)

    sem = pltpu.SemaphoreType.DMA((N_PEER_CHIPS,))
    out_shape = [sem] * (2 * n) + [pltpu.HBM(a.shape, a.dtype) for a in arrays] + [jax.ShapeDtypeStruct((8, LANES), F32)]
    res = pl.pallas_call(
        body, name=name, out_shape=out_shape, in_specs=[HBM_SPEC] * k,
        out_specs=[SEM_SPEC] * (2 * n) + [HBM_SPEC] * k + [pl.BlockSpec(memory_space=pltpu.VMEM)],
        input_output_aliases={i: 2 * n + i for i in range(k)},
        compiler_params=pltpu.CompilerParams(has_side_effects=DATAFLOW_EFFECT),
    )(*[pltpu.with_memory_space_constraint(a, pltpu.HBM) for a in arrays])
    thru = res[2 * n:2 * n + k]
    return res[:n], res[n:2 * n], (None if srcs is None else thru[:n]), thru[k - n:], res[2 * n + k]


def _exchange_wait(name, srcs, lands, send_sems, recv_sems, after, mode):
    n = len(lands)
    arrays = list(lands) if srcs is None else list(srcs) + list(lands)
    k = len(arrays)
    after = list(after) if isinstance(after, (list, tuple)) else [after]

    def body(*refs):
        sends, recvs = refs[k:k + n], refs[k + n:k + 2 * n]
        for cp in _quarter_copies(refs[:n], refs[k - n:k], sends, recvs, mode):
            cp.wait_send()
            cp.wait_recv()

    res = pl.pallas_call(
        body, name=name, out_shape=[pltpu.HBM(a.shape, a.dtype) for a in arrays],
        in_specs=[HBM_SPEC] * k + [SEM_SPEC] * (2 * n) + _any_specs(len(after)),
        out_specs=[HBM_SPEC] * k, input_output_aliases={i: i for i in range(k)},
        compiler_params=pltpu.CompilerParams(has_side_effects=DATAFLOW_EFFECT),
    )(*arrays, *send_sems, *recv_sems, *after)
    return (None if srcs is None else res[:n]), res[k - n:]


def _own_slots(name, srcs, from_stack=False):
    n = len(srcs)
    me = (2 * lax.axis_index("x") + lax.axis_index("y")).astype(jnp.int32).reshape(1)

    def body(me_ref, *refs):
        for x_ref, o_ref in zip(refs[:n], refs[n:]):
            o_ref[...] = x_ref[...].astype(o_ref.dtype)

    in_specs, out_specs, out_shape = [], [], []
    for src in srcs:
        R, C = src.shape[-2:]
        in_specs.append(pl.BlockSpec((None, R // 2, C), lambda i, me_ref: (me_ref[0], i, 0)) if from_stack
                        else pl.BlockSpec((R // 2, C), lambda i, me_ref: (i, 0)))
        out_specs.append(pl.BlockSpec((None, R // 2, C), lambda i, me_ref: (me_ref[0], i, 0)))
        out_shape.append(jax.ShapeDtypeStruct((4, R, C), BF16))
    grid_spec = pltpu.PrefetchScalarGridSpec(num_scalar_prefetch=1, grid=(2,), in_specs=in_specs, out_specs=out_specs)
    return pl.pallas_call(body, out_shape=out_shape, grid_spec=grid_spec, compiler_params=_params("parallel"), name=name)(me, *srcs)


def _allreduce_small(buf):
    R, C = buf.shape
    flips = [(fx, fy, fc) for fx in (0, 1) for fy in (0, 1) for fc in (0, 1)][1:]

    def body(in_ref, out_ref, land_ref, send_sems, recv_sems):
        x, y, c = _position()
        me = 4 * x + 2 * y + c
        copies = []
        for k, (fx, fy, fc) in enumerate(flips):
            px, py, pc = (1 - x if fx else x), (1 - y if fy else y), (1 - c if fc else c)
            cp = pltpu.make_async_remote_copy(in_ref, land_ref.at[me], send_sems.at[k], recv_sems.at[k],
                                              device_id=(px, py, pc), device_id_type=MESH)
            cp.start()
            copies.append(cp)
        land_ref[me] = in_ref[...]
        for cp in copies:
            cp.wait()
        acc = land_ref[0]
        for k in range(1, 8):
            acc = acc + land_ref[k]
        out_ref[...] = acc

    return pl.pallas_call(
        body, out_shape=jax.ShapeDtypeStruct((R, C), F32),
        in_specs=[pl.BlockSpec(memory_space=pltpu.VMEM)], out_specs=pl.BlockSpec(memory_space=pltpu.VMEM),
        scratch_shapes=[pltpu.VMEM((8, R, C), F32), pltpu.SemaphoreType.DMA((7,)), pltpu.SemaphoreType.DMA((7,))],
        compiler_params=pltpu.CompilerParams(has_side_effects=True), name="allreduce_small")(buf)


def _adamw_math(w, g, m, v):
    m2 = ADAM_B1 * m + (1.0 - ADAM_B1) * g
    v2 = ADAM_B2 * v + (1.0 - ADAM_B2) * (g * g)
    m_hat = m2 / (1.0 - ADAM_B1 ** ADAM_STEP)
    v_hat = v2 / (1.0 - ADAM_B2 ** ADAM_STEP)
    delta = -ADAM_LR * (m_hat / (jnp.sqrt(v_hat) + ADAM_EPS) + ADAM_WD * w)
    return delta, m2, v2


def _adamw_from_partials(wv, mv, vv, *parts):
    def four(a, b, c, d):
        return ((a.astype(F32) + b.astype(F32)) + c.astype(F32)) + d.astype(F32)

    g = four(*parts[:4]) + four(*parts[4:])
    return (g,) + _adamw_math(wv, g, mv, vv)


def _adamw_big(name, w, m, v, mine, theirs):
    R, C = w.shape
    rows = 256 if R % 256 == 0 else R // 2
    nrb = R // rows
    slots = [_tiled(s.reshape(4 * R, C), None, 0, k * nrb) for s in (mine, theirs) for k in range(4)]
    return _ew(name, _adamw_from_partials, [_tiled(w), _tiled(m), _tiled(v)] + slots, [(F32, C)] * 4, n_rows=R, rows=rows)


def _adamw_rider(w, m, v, mine, theirs, steps, deliver):
    R, C = w.shape
    fits = [nb for nb in range(1, steps + 1) if R % nb == 0 and (R // nb) % 16 == 0]
    if not fits:
        return None
    nb = fits[-1]
    rows = R // nb

    def blocks(first):
        return pl.BlockSpec((rows, C), lambda *g: (first + jnp.minimum(g[0], nb - 1), 0))

    flat = [s.reshape(4 * R, C) for s in (mine, theirs)]
    return dict(operands=[w, m, v] + [f for f in flat for _ in range(4)],
                in_specs=[blocks(0)] * 3 + [blocks(k * nb) for _ in flat for k in range(4)],
                out_shape=[jax.ShapeDtypeStruct((R, C), F32)] * 4, out_specs=[blocks(0)] * 4,
                n_blocks=nb, fn=_adamw_from_partials, deliver=lambda outs: deliver(*outs))


BIG = ("ffn1_w1", "ffn1_w3", "ffn1_w2", "w_in", "w_branch_a", "w_branch_b", "w_out", "ffn2_w1", "ffn2_w3", "ffn2_w2")
SMALL = ("ffn1_norm", "mix_norm", "b_gate", "q_norm", "k_norm", "rel_bias", "ffn2_norm", "final_norm")
ORDER = ("ffn1_norm", "ffn1_w1", "ffn1_w3", "ffn1_w2", "mix_norm", "w_in", "b_gate", "q_norm", "k_norm", "rel_bias",
         "w_branch_a", "w_branch_b", "w_out", "ffn2_norm", "ffn2_w1", "ffn2_w3", "ffn2_w2", "final_norm")
TRANSPOSED = ("ffn1_w1", "ffn1_w3", "ffn2_w1", "ffn2_w3")
SIBLING_LAG = 2
LONG_HOST_STEPS = 8
GATHER_GROUPS = (("ffn1_w1", "ffn1_w3"), ("ffn1_w2",), ("w_in",), ("w_branch_a", "w_branch_b", "w_out"),
                 ("ffn2_w1", "ffn2_w3", "ffn2_w2"))


def _pack_small(d):
    rows = []
    for n in SMALL:
        flat = d[n].reshape(-1)
        pad = (-flat.shape[0]) % LANES
        rows.append(jnp.pad(flat, (0, pad)).reshape(-1, LANES))
    buf = jnp.concatenate(rows, axis=0)
    return jnp.pad(buf, ((0, (-buf.shape[0]) % 8), (0, 0)))


def _unpack_small(buf, like):
    out, r = {}, 0
    for n in SMALL:
        size = like[n].size
        nr = -(-size // LANES)
        out[n] = buf[r:r + nr].reshape(-1)[:size].reshape(like[n].shape)
        r += nr
    return out


def kernel(x, ffn1_norm, ffn1_w1, ffn1_w3, ffn1_w2, mix_norm, w_in, b_gate, q_norm, k_norm, rel_bias, w_branch_a, w_branch_b, w_out, ffn2_norm, ffn2_w1, ffn2_w3, ffn2_w2, final_norm, loss_target, m_ffn1_norm, m_ffn1_w1, m_ffn1_w3, m_ffn1_w2, m_mix_norm, m_w_in, m_b_gate, m_q_norm, m_k_norm, m_rel_bias, m_w_branch_a, m_w_branch_b, m_w_out, m_ffn2_norm, m_ffn2_w1, m_ffn2_w3, m_ffn2_w2, m_final_norm, v_ffn1_norm, v_ffn1_w1, v_ffn1_w3, v_ffn1_w2, v_mix_norm, v_w_in, v_b_gate, v_q_norm, v_k_norm, v_rel_bias, v_w_branch_a, v_w_branch_b, v_w_out, v_ffn2_norm, v_ffn2_w1, v_ffn2_w3, v_ffn2_w2, v_final_norm):
    given = dict(locals())
    w = {n: given[n] for n in ORDER}
    m = {n: given["m_" + n] for n in ORDER}
    v = {n: given["v_" + n] for n in ORDER}
    T, D = x.shape[1], x.shape[2]

    def stored(a, n):
        a = a.reshape(a.shape[1:])
        return a.T if n in TRANSPOSED else a

    def returned(a, n):
        return (a.T if n in TRANSPOSED else a).reshape(w[n].shape)

    quarter = {n: stored(w[n], n) for n in BIG}
    send, recv, _, land_thru, token = _exchange_start(
        "gather_start", None, _own_slots("own_weights", [quarter[n] for n in BIG]), "gather")
    index = {n: i for i, n in enumerate(BIG)}
    ready, filling = {}, {}

    def landed_halves(group, after):
        ids = [index[n] for n in group]
        return _exchange_wait("gather_wait_" + group[0], None, [land_thru[i] for i in ids],
                              [send[i] for i in ids], [recv[i] for i in ids], after, "gather")[1]

    def prefetch_w(name, after):
        group = next(g for g in GATHER_GROUPS if name in g)
        started = _exchange_start("fill_start_" + group[0], None, landed_halves(group, after), "fill")
        filling[group] = started
        return [started[4]]

    def get_w(name, after):
        if name not in ready:
            group = next(g for g in GATHER_GROUPS if name in g)
            if group in filling:
                f_send, f_recv, _, thru, _ = filling[group]
                stacks = _exchange_wait("fill_wait_" + group[0], None, thru, f_send, f_recv, after, "fill")[1]
            else:
                stacks = _fill_from_sibling("gather_fill_" + group[0], landed_halves(group, after))
            for n, st in zip(group, stacks):
                ready[n] = st.reshape(D, D) if n in ("w_branch_b", "w_out") else st
        return ready[name]

    scattered, forwarded = [], []

    def forward_oldest(after):
        names, s_sem, r_sem, srcs, lands = scattered.pop(0)
        _, landed = _exchange_wait("scatter_wait_" + names[0], srcs, lands, s_sem, r_sem, after, "scatter")
        started = _exchange_start("sibling_start_" + names[0], landed, [lax.empty(a.shape, a.dtype) for a in landed], "sibling")
        forwarded.append((names,) + tuple(started[:4]))
        return started[4]

    def put_g(grads):
        names = list(grads)
        stacks = [grads[n].reshape((4,) + quarter[n].shape) for n in names]
        lands = _own_slots("own_grad_" + names[0], stacks, from_stack=True)
        started = _exchange_start("scatter_start_" + names[0], stacks, lands, "scatter")
        scattered.append((names,) + tuple(started[:4]))
        tokens = [started[4]]
        if len(scattered) > SIBLING_LAG:
            tokens.append(forward_oldest(started[4]))
        return tokens

    grads, deltas, new_m, new_v = {}, {}, {}, {}
    arrived, riding = {}, set()

    def partials(gi, after):
        if gi not in arrived:
            names, s_sem, r_sem, srcs, lands = forwarded[gi]
            arrived[gi] = _exchange_wait("sibling_wait_" + names[0], srcs, lands, s_sem, r_sem, after, "sibling")
        return arrived[gi]

    def deliver_to(n):
        def deliver(*res):
            grads[n], deltas[n], new_m[n], new_v[n] = [returned(r, n) for r in res]
        return deliver

    def take_rider(steps, after):
        waiting = [(quarter[n].size, gi, k, n) for gi, entry in enumerate(forwarded) for k, n in enumerate(entry[0]) if n not in riding]
        for _, gi, k, n in sorted(waiting, reverse=steps >= LONG_HOST_STEPS):
            mine, theirs = partials(gi, after)
            rider = _adamw_rider(quarter[n], stored(m[n], n), stored(v[n], n), mine[k], theirs[k], steps, deliver_to(n))
            if rider is not None:
                riding.add(n)
                return rider
        return None

    small = {n: w[n] for n in SMALL}
    packed = [_pack_small({n: d[n] for n in SMALL}) for d in (w, m, v)]
    loss_cols, grad_x, gs = _local_step(x.reshape(T, D), loss_target.reshape(T, D), small, get_w, put_g, deps=[token] + packed,
                                        prefetch_w=prefetch_w, take_rider=take_rider)

    after = grad_x
    while scattered:
        after = forward_oldest(after)
    for gi, entry in enumerate(forwarded):
        mine, theirs = partials(gi, after)
        for n, a, b in zip(entry[0], mine, theirs):
            if n not in riding:
                deliver_to(n)(*_adamw_big(f"adamw_{n}", quarter[n], stored(m[n], n), stored(v[n], n), a, b))

    gs = {n: gs[n].reshape(w[n].shape) for n in SMALL}
    packed_g = _pack_small(gs)
    n_small = packed_g.shape[0]
    summed = _allreduce_small(jnp.concatenate([packed_g, loss_cols.reshape(-1, LANES)], axis=0))
    g_small, loss = summed[:n_small], jnp.sum(summed[n_small:])
    R = g_small.shape[0]
    res = _ew("adamw_small", lambda wv, mv, vv, g: (g,) + _adamw_math(wv, g, mv, vv),
              [_tiled(packed[0]), _tiled(packed[1]), _tiled(packed[2]), _tiled(g_small)], [(F32, LANES)] * 4, n_rows=R, rows=R)
    for d, buf in zip((grads, deltas, new_m, new_v), res):
        d.update(_unpack_small(buf, w))

    return (loss, grad_x.reshape(x.shape), *[grads[n] for n in ORDER], *[deltas[n] for n in ORDER],
            *[new_m[n] for n in ORDER], *[new_v[n] for n in ORDER])
```

```python
import functools
import math

import numpy as np
import jax
import jax.numpy as jnp
from jax import lax
from jax.experimental import pallas as pl
from jax.experimental.pallas import tpu as pltpu

F32 = jnp.float32
BF16 = jnp.bfloat16
MESH = pl.DeviceIdType.MESH

NEG_INF = -1e30
EPS = 1e-6
GRID_W = 64
ROPE_THETA = 10000.0
DILATIONS = (1, 4, 16)
BAND_HALF = 64
HEAD_A = 64
HEADS_A = 8
WIDTH_A = HEADS_A * HEAD_A
HEAD_B = 128
LOG2_E = math.log2(math.e)
QK_SCALE_LOG2 = HEAD_B ** -0.5 * LOG2_E
N_BUCKETS = 32
MAX_DISTANCE = 1024
ADAM_LR, ADAM_B1, ADAM_B2, ADAM_EPS, ADAM_WD, ADAM_STEP = 0.001, 0.9, 0.999, 1e-08, 0.01, 10

B_Q, B_K, B_V = 4608, 5632, 5888
G_A, G_B = 6144, 7168
IN_WIDTH = 8192

VMEM_LIMIT_BYTES = 56 * 1024 * 1024
QB_A = 128
QB_B = 256


def _params(*sem):
    return pltpu.CompilerParams(dimension_semantics=sem, vmem_limit_bytes=VMEM_LIMIT_BYTES)


def _bs(shape, fn):
    return pl.BlockSpec(shape, fn)


def _resident(shape, fn):
    return pl.BlockSpec(shape, fn, pipeline_mode=pl.Buffered(1))


def _mm(name, grid, pairs, out_shape, out_spec, dims, *, extras=(), epilogue=None, deps=(), reds=(), rider=None):
    n_pairs, n_extra, n_deps = len(pairs), len(extras), len(deps)
    operands = [p[0] for p in pairs] + [p[2] for p in pairs] + [e[0] for e in extras] + list(deps)
    in_specs = [p[1] for p in pairs] + [p[3] for p in pairs] + [e[1] for e in extras] + _any_specs(n_deps)
    single = not isinstance(out_shape, (list, tuple))
    out_shapes = [out_shape] if single else list(out_shape)
    out_specs = [out_spec] if single else list(out_spec)
    n_out = len(out_shapes)
    out_shapes += [jax.ShapeDtypeStruct((1, w), F32) for w in reds]
    out_specs += [_bs((1, w), lambda *_: (0, 0)) for w in reds]
    n_rin = 0
    if rider is not None:
        assert rider["n_blocks"] <= grid[0]
        n_rin = len(rider["operands"])
        operands += list(rider["operands"])
        in_specs += list(rider["in_specs"])
        out_shapes += list(rider["out_shape"])
        out_specs += list(rider["out_specs"])

    def body(*refs):
        a_refs, b_refs = refs[:n_pairs], refs[n_pairs:2 * n_pairs]
        e_refs = refs[2 * n_pairs:2 * n_pairs + n_extra]
        o_refs = refs[2 * n_pairs + n_extra + n_deps + n_rin:]
        if rider is not None:
            r_in = refs[2 * n_pairs + n_extra + n_deps:2 * n_pairs + n_extra + n_deps + n_rin]
            r_out = o_refs[n_out + len(reds):]

            @pl.when(pl.program_id(0) < rider["n_blocks"])
            def _():
                for ref, val in zip(r_out, rider["fn"](*[r[...] for r in r_in])):
                    ref[...] = val.astype(ref.dtype)
        acc = None
        for a_ref, b_ref in zip(a_refs, b_refs):
            t = lax.dot_general(a_ref[...], b_ref[...], (dims, ((), ())), preferred_element_type=F32)
            acc = t if acc is None else acc + t
        vals = acc if epilogue is None else epilogue(acc, *[e[...] for e in e_refs])
        if not isinstance(vals, (list, tuple)):
            vals = (vals,)
        for o_ref, v in zip(o_refs[:n_out], vals[:n_out]):
            o_ref[...] = v.astype(o_ref.dtype)
        if reds:
            first = functools.reduce(jnp.logical_and, [pl.program_id(ax) == 0 for ax in range(len(grid))])
            for r_ref, v in zip(o_refs[n_out:], vals[n_out:]):
                @pl.when(first)
                def _(r_ref=r_ref):
                    r_ref[...] = jnp.zeros_like(r_ref)
                r_ref[...] += v

    sem = ["arbitrary" if (reds or rider is not None) else "parallel"] * len(grid)
    res = pl.pallas_call(
        body, out_shape=out_shapes, grid=grid, in_specs=in_specs, out_specs=out_specs,
        compiler_params=_params(*sem), name=name)(*operands)
    if rider is not None:
        rider["deliver"](res[n_out + len(reds):])
        res = res[:n_out + len(reds)]
    return res[0] if (single and not reds) else res


NN = ((1,), (0,))
NT = ((1,), (1,))
TN = ((0,), (0,))


def _mm_wgrad(name, a, b, *, a_cols, b_cols, tm, tn, J, deps=(), rider=None):
    def pick(arr, cols, t):
        if arr.ndim == 3:
            T, c = arr.shape[1], arr.shape[2]
            t = min(t, c)
            return T, c, t, (lambda sel: _bs((None, T, t), lambda j, i, k: (j, 0, sel(i, k))))
        T = arr.shape[0]
        c = arr.shape[1] if cols is None else cols
        t = min(t, c)
        per = c // t
        if cols is None:
            if per == 1:
                return T, c, t, (lambda sel: _resident((T, t), lambda j, i, k: (0, 0)))
            return T, c, t, (lambda sel: _bs((T, t), lambda j, i, k: (0, sel(i, k))))
        return T, c, t, (lambda sel: _bs((T, t), lambda j, i, k: (0, j * per + sel(i, k))))
    _, ca, tm, mk_a = pick(a, a_cols, tm)
    _, cb, tn, mk_b = pick(b, b_cols, tn)
    return _mm(name, (J, ca // tm, cb // tn),
               [(a, mk_a(lambda i, k: i), b, mk_b(lambda i, k: k))],
               jax.ShapeDtypeStruct((J, ca, cb), BF16), _bs((None, tm, tn), lambda j, i, k: (j, i, k)), TN, deps=deps, rider=rider)


def _tiled(arr, width=None, col=0, rowblk=0):
    return ("t", arr, arr.shape[1] if width is None else width, col, rowblk)


def _table(arr):
    return ("f", arr)


def _whole(arr):
    return ("w", arr)


def _ew(name, fn, ins, outs, *, n_rows, rows, reds=(), ncols=1, deps=()):
    nrb = n_rows // rows
    n_deps = len(deps)
    operands, in_specs = [], []
    for spec in ins:
        if spec[0] == "t":
            _, arr, width, col, rowblk = spec
            step = 1 if ncols > 1 else 0
            in_specs.append(_bs((rows, width), lambda c, i, col=col, rowblk=rowblk, step=step: (rowblk + i, col + c * step)))
        elif spec[0] == "f":
            arr = spec[1]
            in_specs.append(_bs((rows, arr.shape[1]), lambda c, i: (i, 0)))
        else:
            arr = spec[1]
            nd = arr.ndim
            if nd == 3:
                in_specs.append(_bs((None,) + arr.shape[1:], lambda c, i: (c, 0, 0)))
            else:
                in_specs.append(_bs(arr.shape, lambda c, i, nd=nd: (0,) * nd))
        operands.append(arr)
    out_shapes = [jax.ShapeDtypeStruct((n_rows, ncols * w), dt) for dt, w in outs]
    out_specs = [_bs((rows, w), lambda c, i: (i, c)) for _, w in outs]
    out_shapes += [jax.ShapeDtypeStruct((ncols, 1, w), F32) for w in reds]
    out_specs += [_bs((None, 1, w), lambda c, i: (c, 0, 0)) for w in reds]
    n_in, n_out, n_red = len(ins), len(outs), len(reds)
    operands += list(deps)
    in_specs += _any_specs(n_deps)

    def body(*refs):
        vals = fn(*[r[...] for r in refs[:n_in]])
        if not isinstance(vals, (tuple, list)):
            vals = (vals,)
        o_refs = refs[n_in + n_deps:]
        for o_ref, v in zip(o_refs[:n_out], vals[:n_out]):
            o_ref[...] = v.astype(o_ref.dtype)
        if n_red:
            i = pl.program_id(1)
            for r_ref, v in zip(o_refs[n_out:], vals[n_out:]):
                @pl.when(i == 0)
                def _(r_ref=r_ref):
                    r_ref[...] = jnp.zeros_like(r_ref)
                r_ref[...] += v

    res = pl.pallas_call(
        body, out_shape=out_shapes, grid=(ncols, nrb), in_specs=in_specs, out_specs=out_specs,
        compiler_params=_params("parallel", "arbitrary" if n_red else "parallel"), name=name)(*operands)
    return res


def _colsum(v):
    return jnp.sum(v, axis=0, keepdims=True)


def _rstd(x):
    return lax.rsqrt(jnp.mean(x * x, axis=-1, keepdims=True) + EPS)


def _sigmoid(x):
    return 0.5 * jnp.tanh(0.5 * x) + 0.5


def _norm_fwd(x, g):
    return x * _rstd(x) * g


def _norm_bwd(x, g, dy):
    r = _rstd(x)
    xh = x * r
    dxh = dy * g
    dx = r * (dxh - xh * jnp.mean(dxh * xh, axis=-1, keepdims=True))
    return dx, dy * xh


def _row_spec(arr, rows):
    if arr.shape[0] == 1:
        return _bs(arr.shape, lambda i: (0, 0))
    return _bs((rows, arr.shape[1]), lambda i: (i, 0))


def _ffn_fwd(tag, x, gain, get_w, deps=(), *, h=None, tail_ins=(), tail_fn=None, tail_outs=(F32,), tail_reds=()):
    T, D = x.shape
    if h is None:
        (h,) = _ew(f"{tag}_norm", lambda xv, g: _norm_fwd(xv, g), [_tiled(x), _whole(gain)], [(BF16, D)], n_rows=T, rows=512,
                   deps=deps)
    w1, w3 = get_w(f"{tag}_w1", h), get_w(f"{tag}_w3", h)
    J, f, _ = w1.shape
    tm = 1024

    def up(h_ref, w1_ref, w3_ref, u_ref, g_ref, a_ref):
        hv = h_ref[...]
        u = lax.dot_general(hv, w1_ref[...], (NT, ((), ())), preferred_element_type=F32)
        g = lax.dot_general(hv, w3_ref[...], (NT, ((), ())), preferred_element_type=F32)
        u_ref[...] = u.astype(BF16)
        g_ref[...] = g.astype(BF16)
        a_ref[...] = (u * _sigmoid(u) * g).astype(BF16)

    slab = _bs((None, tm, f), lambda j, i: (j, i, 0))
    w_spec = _bs((None, f, D), lambda j, i: (j, 0, 0))
    u, g, a = pl.pallas_call(
        up, out_shape=[jax.ShapeDtypeStruct((J, T, f), BF16)] * 3, grid=(J, T // tm),
        in_specs=[_bs((tm, D), lambda j, i: (i, 0)), w_spec, w_spec], out_specs=[slab] * 3,
        compiler_params=_params("parallel", "parallel"), name=f"{tag}_up")(h, w1, w3)
    w2 = get_w(f"{tag}_w2", a)
    def tail(acc, xv, *rest):
        y = xv + 0.5 * acc
        return y if tail_fn is None else tail_fn(y, *rest)

    row = _bs((512, D), lambda i: (i, 0))
    res = _mm(f"{tag}_down", (T // 512,),
              [(a, _bs((None, 512, f), lambda i, j=j: (j, i, 0)), w2, _resident((None, f, D), lambda i, j=j: (j, 0, 0)))
               for j in range(J)],
              [jax.ShapeDtypeStruct((T, D), dt) for dt in tail_outs], [row] * len(tail_outs), NN,
              extras=[(x, row)] + [(t, _row_spec(t, 512)) for t in tail_ins], epilogue=tail, reds=tail_reds)
    return res, (h, u, g, a)


def _dh_norm_bwd(name, rows, pairs, dims, x, gain, dres, deps, also_bf16=False, rider=None):
    T, D = x.shape

    def epilogue(dh, xv, gv, dr):
        dx, dgr = _norm_bwd(xv, gv, dh)
        dx = dx + dr
        return (dx, 0.5 * dx) + ((dx,) if also_bf16 else ()) + (_colsum(dgr),)

    dts = [F32, BF16] + ([BF16] if also_bf16 else [])
    row = _bs((rows, D), lambda i: (i, 0))
    return _mm(name, (T // rows,), pairs, [jax.ShapeDtypeStruct((T, D), dt) for dt in dts], [row] * len(dts), dims,
               extras=[(x, row), (gain, _row_spec(gain, rows)), (dres, row)], epilogue=epilogue, deps=deps, reds=(D,), rider=rider)


def _ffn_bwd(tag, x, gain, get_w, put_g, saved, dy, dy_half, also_bf16=False, last=False, take_rider=lambda steps, after: None):
    h, u, g, a = saved
    T, D = x.shape
    w1, w3, w2 = [get_w(f"{tag}_{n}", dy_half) for n in ("w1", "w3", "w2")]
    J, f, _ = w1.shape
    dw2 = _mm_wgrad(f"{tag}_bwd_dw2", a, dy_half, a_cols=None, b_cols=None, tm=f, tn=D, J=J, rider=take_rider(J, dy_half))
    deps = put_g({f"{tag}_w2": dw2}) if last else []
    tm = 1024

    def up_bwd(dy_ref, w2_ref, u_ref, g_ref, *rest):
        du_ref, dg_ref = rest[-2:]
        da = lax.dot_general(dy_ref[...], w2_ref[...], (NT, ((), ())), preferred_element_type=F32)
        uv, gv = u_ref[...].astype(F32), g_ref[...].astype(F32)
        s = _sigmoid(uv)
        silu = uv * s
        du_ref[...] = (da * gv * (s + silu - silu * s)).astype(BF16)
        dg_ref[...] = (da * silu).astype(BF16)

    slab = _bs((None, tm, f), lambda j, i: (j, i, 0))
    du, dg = pl.pallas_call(
        up_bwd, out_shape=[jax.ShapeDtypeStruct((J, T, f), BF16)] * 2, grid=(J, T // tm),
        in_specs=[_bs((tm, D), lambda j, i: (i, 0)), _bs((None, f, D), lambda j, i: (j, 0, 0)), slab, slab] + _any_specs(len(deps)),
        out_specs=[slab] * 2, compiler_params=_params("parallel", "parallel"), name=f"{tag}_bwd_up")(dy_half, w2, u, g, *deps)
    dw1 = _mm_wgrad(f"{tag}_bwd_dw1", du, h, a_cols=None, b_cols=None, tm=f, tn=D, J=J)
    deps = put_g({f"{tag}_w1": dw1}) if last else []
    dw3 = _mm_wgrad(f"{tag}_bwd_dw3", dg, h, a_cols=None, b_cols=None, tm=f, tn=D, J=J, deps=deps)
    deps = put_g({f"{tag}_w3": dw3} if last else {f"{tag}_w2": dw2, f"{tag}_w1": dw1, f"{tag}_w3": dw3})
    pairs = []
    for j in range(J):
        a_spec = _bs((None, 256, f), lambda i, j=j: (j, i, 0))
        w_spec = _resident((None, f, D), lambda i, j=j: (j, 0, 0))
        pairs += [(du, a_spec, w1, w_spec), (dg, a_spec, w3, w_spec)]
    return _dh_norm_bwd(f"{tag}_bwd_dh", 256, pairs, NN, x, gain, dy, deps, also_bf16, rider=take_rider(T // 256, dw3))


def _t5_bucket(rel):
    n = N_BUCKETS // 2
    max_exact = n // 2
    ret = jnp.where(rel > 0, n, 0)
    a = jnp.abs(rel)
    af = jnp.maximum(a, 1).astype(F32)
    large = max_exact + (jnp.log(af / max_exact) / math.log(MAX_DISTANCE / max_exact) * (n - max_exact)).astype(jnp.int32)
    large = jnp.minimum(large, n - 1)
    return ret + jnp.where(a < max_exact, a, large)


WIN_A = QB_A + 2 * BAND_HALF
WIN_SHIFTS = (0, BAND_HALF, 2 * BAND_HALF)


def _window_variant(n, nblk):
    return jnp.where(n == 0, 0, jnp.where(n == nblk - 1, 2, 1))


def _window_start(n, nblk):
    return pl.multiple_of(jnp.clip(n * QB_A - BAND_HALF, 0, nblk * QB_A - WIN_A), BAND_HALF)


def _band_steps(xp=jnp):
    qi = xp.arange(QB_A, dtype=xp.int32)[None, :, None]
    kj = xp.arange(WIN_A, dtype=xp.int32)[None, None, :]
    return kj - qi - xp.asarray(WIN_SHIFTS, dtype=xp.int32)[:, None, None]


def _bias_tiles(rel_bias):
    wide = QB_A + 2 * WIN_SHIFTS[-1]
    qi = jnp.arange(QB_A, dtype=jnp.int32)[:, None]
    steps = jnp.arange(wide, dtype=jnp.int32)[None, :] - WIN_SHIFTS[-1] - qi
    buckets = jnp.stack([_t5_bucket(steps * d) for d in DILATIONS])
    inband = (jnp.abs(steps) <= BAND_HALF).astype(jnp.int32)
    n_heads = rel_bias.shape[1]

    def body(tab_ref, b_ref, m_ref, o_ref):
        hd = pl.program_id(0)
        bkt = b_ref[...]
        acc = jnp.zeros(bkt.shape, F32)
        for b in range(N_BUCKETS):
            acc = jnp.where(bkt == b, tab_ref[b, hd], acc)
        o_ref[...] = jnp.where(m_ref[...] > 0, acc, NEG_INF)

    base = pl.pallas_call(
        body, out_shape=jax.ShapeDtypeStruct((n_heads, QB_A, wide), F32), grid=(n_heads,),
        in_specs=[pl.BlockSpec(memory_space=pltpu.SMEM),
                  _bs((None, QB_A, wide), lambda hd: (hd // HEADS_A, 0, 0)),
                  _bs((QB_A, wide), lambda hd: (0, 0))],
        out_specs=_bs((None, QB_A, wide), lambda hd: (hd, 0, 0)),
        compiler_params=_params("parallel"), name="a_bias_tiles")(rel_bias, buckets, inband)
    base = base.reshape(len(DILATIONS), HEADS_A, QB_A, wide)
    return jnp.stack([base[..., WIN_SHIFTS[-1] - s:WIN_SHIFTS[-1] - s + WIN_A] for s in WIN_SHIFTS], axis=1)


def _bias_grad(dbias):
    steps = _band_steps(np)
    inband = np.abs(steps) <= BAND_HALF
    present = []
    for d in DILATIONS:
        rel = steps * d
        a = np.abs(rel)
        large = 8 + (np.log(np.maximum(a, 1) / 8.0) / math.log(MAX_DISTANCE / 8.0) * 8).astype(np.int64)
        bk = np.where(rel > 0, 16, 0) + np.where(a < 8, a, np.minimum(large, 15))
        present.append([sorted(set(bk[v][inband[v]].tolist())) for v in range(3)])
    buckets = jnp.stack([_t5_bucket(_band_steps() * d) for d in DILATIONS])
    n_heads = len(DILATIONS) * HEADS_A

    def body(b_ref, d_ref, o_ref):
        row = lax.broadcasted_iota(jnp.int32, (N_BUCKETS, n_heads), 0)
        col = lax.broadcasted_iota(jnp.int32, (N_BUCKETS, n_heads), 1)
        out = jnp.zeros((N_BUCKETS, n_heads), F32)
        for grp in range(len(DILATIONS)):
            for hh in range(HEADS_A):
                hd = grp * HEADS_A + hh
                for b in sorted(set(sum(present[grp], []))):
                    tot = jnp.zeros((), F32)
                    for v in range(3):
                        if b in present[grp][v]:
                            tot = tot + jnp.sum(jnp.where(b_ref[grp, v] == b, d_ref[grp, v, hh], 0.0))
                    out = jnp.where((row == b) & (col == hd), tot, out)
        o_ref[...] = out

    return pl.pallas_call(
        body, out_shape=jax.ShapeDtypeStruct((N_BUCKETS, n_heads), F32),
        compiler_params=pltpu.CompilerParams(vmem_limit_bytes=VMEM_LIMIT_BYTES), name="a_bias_grad")(buckets, dbias)


def _lane_is_second_head(shape):
    return lax.broadcasted_iota(jnp.int32, shape, len(shape) - 1) >= HEAD_A


VIEW_ROWS = 512


def _view_chunks():
    return [pltpu.VMEM((VIEW_ROWS, LANES), F32)] * (WIDTH_A // LANES)


def _rows_to_view(x_ref, col, o_ref, ocol, d, chunks):
    n = VIEW_ROWS // d
    for c, scr in enumerate(chunks):
        scr[...] = x_ref[:, col + c * LANES:col + (c + 1) * LANES].astype(F32)
        for r in range(d):
            at = ocol + r * WIDTH_A + c * LANES
            o_ref[:, at:at + LANES] = scr[pl.ds(r, n, stride=d), :].astype(o_ref.dtype)


def _view_to_rows(v_ref, o_ref, col, d, chunks):
    n = VIEW_ROWS // d
    for c, scr in enumerate(chunks):
        if d == 1:
            o_ref[:, col + c * LANES:col + (c + 1) * LANES] = v_ref[:, c * LANES:(c + 1) * LANES].astype(o_ref.dtype)
            continue
        for r in range(d):
            scr[pl.ds(r, n, stride=d), :] = v_ref[:, r * WIDTH_A + c * LANES:r * WIDTH_A + (c + 1) * LANES].astype(F32)
        o_ref[:, col + c * LANES:col + (c + 1) * LANES] = scr[...].astype(o_ref.dtype)


def _group_view(proj, grp, d):
    T = proj.shape[0]
    if d == 1:
        return proj, (lambda part, r: grp * 3 + part)

    def body(x_ref, o_ref, *chunks):
        for part in range(3):
            _rows_to_view(x_ref, part * WIDTH_A, o_ref, part * d * WIDTH_A, d, chunks)

    view = pl.pallas_call(
        body, out_shape=jax.ShapeDtypeStruct((T // d, 3 * d * WIDTH_A), proj.dtype), grid=(T // VIEW_ROWS,),
        in_specs=[_bs((VIEW_ROWS, 3 * WIDTH_A), lambda i: (i, grp))],
        out_specs=_bs((VIEW_ROWS // d, 3 * d * WIDTH_A), lambda i: (i, 0)),
        scratch_shapes=_view_chunks(), compiler_params=_params("parallel"), name=f"a_view_d{d}")(proj)
    return view, (lambda part, r: part * d + r)


def _stack_heads(v2, second):
    zero = jnp.zeros_like(v2)
    return jnp.concatenate([jnp.where(second, zero, v2), jnp.where(second, v2, zero)], axis=0)


def _unstack_heads(v, second):
    return jnp.where(second, v[QB_A:], v[:QB_A])


def _dil_fwd(view, bias, d):
    pv, colblk = view
    L = pv.shape[0]
    nblk = L // QB_A
    W2 = 2 * HEAD_A
    scale = HEAD_A ** -0.5

    def body(q_ref, k_ref, v_ref, b_ref, o_ref, l_ref):
        win = pl.ds(_window_start(pl.program_id(1), nblk), WIN_A)
        second = _lane_is_second_head((QB_A, W2))
        pairs = range(HEADS_A // 2)
        cols = [slice(hp * W2, (hp + 1) * W2) for hp in pairs]
        s = [lax.dot_general(_stack_heads(q_ref[:, cols[hp]], second), k_ref[win, cols[hp]], (NT, ((), ())),
                             preferred_element_type=F32) * scale + b_ref[2 * hp:2 * hp + 2].reshape(2 * QB_A, WIN_A)
             for hp in pairs]
        m = [jnp.max(x, axis=-1, keepdims=True) for x in s]
        p = [jnp.exp(x - mx) for x, mx in zip(s, m)]
        l = [jnp.sum(x, axis=-1, keepdims=True) for x in p]
        res = [jnp.dot(p[hp].astype(BF16), v_ref[win, cols[hp]], preferred_element_type=F32) / l[hp] for hp in pairs]
        o_ref[...] = jnp.concatenate([_unstack_heads(x, second) for x in res], axis=1).astype(o_ref.dtype)
        l_ref[...] = jnp.concatenate([_unstack_heads(jnp.broadcast_to(mx + jnp.log(lx), (2 * QB_A, W2)), second)
                                      for mx, lx in zip(m, l)], axis=1)

    in_specs = [_bs((QB_A, WIDTH_A), lambda r, n: (n, colblk(0, r))),
                _bs((L, WIDTH_A), lambda r, n: (0, colblk(1, r))), _bs((L, WIDTH_A), lambda r, n: (0, colblk(2, r))),
                _bs((None, HEADS_A, QB_A, WIN_A), lambda r, n: (_window_variant(n, nblk), 0, 0, 0))]
    o, lse = pl.pallas_call(
        body, out_shape=[jax.ShapeDtypeStruct((L, d * WIDTH_A), BF16), jax.ShapeDtypeStruct((L, d * WIDTH_A), F32)],
        grid=(d, nblk), in_specs=in_specs,
        out_specs=[_bs((QB_A, WIDTH_A), lambda r, n: (n, r)), _bs((QB_A, WIDTH_A), lambda r, n: (n, r))],
        compiler_params=_params("parallel", "parallel"), name=f"a_fwd_d{d}")(pv, pv, pv, bias)
    return o, lse


def _dil_bwd(view_qkv, bias, do, lse, cterm, d):
    pv, colblk = view_qkv
    L = pv.shape[0]
    nblk = L // QB_A
    W2 = 2 * HEAD_A
    PPS = 4
    WS = PPS * W2
    ob = WIDTH_A // WS
    scale = HEAD_A ** -0.5

    def body(q_ref, k_ref, v_ref, do_ref, l_ref, c_ref, b_ref, dq_ref, dk_ref, dv_ref, db_ref):
        r, n = pl.program_id(1), pl.program_id(2)

        @pl.when(n == 0)
        def _():
            dk_ref[...] = jnp.zeros_like(dk_ref)
            dv_ref[...] = jnp.zeros_like(dv_ref)

        @pl.when((n == 0) & (r == 0))
        def _():
            db_ref[...] = jnp.zeros_like(db_ref)

        second = _lane_is_second_head((QB_A, W2))
        win = pl.ds(_window_start(n, nblk), WIN_A)
        variant = _window_variant(n, nblk)
        pairs = range(PPS)
        cols = [slice(pp * W2, (pp + 1) * W2) for pp in pairs]

        def head_rows(ref, pp):
            v2 = ref[:, cols[pp]]
            return jnp.concatenate([v2[:, 0:1], v2[:, HEAD_A:HEAD_A + 1]], axis=0)

        kw = [k_ref[win, c] for c in cols]
        vw = [v_ref[win, c] for c in cols]
        qs = [_stack_heads(q_ref[:, c], second) for c in cols]
        dos = [_stack_heads(do_ref[:, c], second) for c in cols]
        s = [lax.dot_general(qs[pp], kw[pp], (NT, ((), ())), preferred_element_type=F32) for pp in pairs]
        dp = [lax.dot_general(dos[pp], vw[pp], (NT, ((), ())), preferred_element_type=F32) for pp in pairs]
        p = [jnp.exp(s[pp] * scale + b_ref[2 * pp:2 * pp + 2].reshape(2 * QB_A, WIN_A) - head_rows(l_ref, pp)) for pp in pairs]
        ds = [p[pp] * (dp[pp] + head_rows(c_ref, pp)) for pp in pairs]
        db_ref[variant] += jnp.concatenate([x.reshape(2, QB_A, WIN_A) for x in ds], axis=0)
        pb = [x.astype(BF16) for x in p]
        dsb = [(x * scale).astype(BF16) for x in ds]
        dq_ref[...] = jnp.concatenate([_unstack_heads(jnp.dot(dsb[pp], kw[pp], preferred_element_type=F32), second)
                                       for pp in pairs], axis=1).astype(dq_ref.dtype)
        dk_ref[win, :] += jnp.concatenate([lax.dot_general(dsb[pp], qs[pp], (TN, ((), ())), preferred_element_type=F32)
                                           for pp in pairs], axis=1)
        dv_ref[win, :] += jnp.concatenate([lax.dot_general(pb[pp], dos[pp], (TN, ((), ())), preferred_element_type=F32)
                                           for pp in pairs], axis=1)

    kv_spec = _resident if d == 1 else _bs
    in_specs = [_bs((QB_A, WS), lambda hp, r, n: (n, colblk(0, r) * ob + hp)),
                kv_spec((L, WS), lambda hp, r, n: (0, colblk(1, r) * ob + hp)),
                kv_spec((L, WS), lambda hp, r, n: (0, colblk(2, r) * ob + hp))]
    in_specs += [_bs((QB_A, WS), lambda hp, r, n: (n, r * ob + hp))] * 3
    in_specs += [_bs((None, 2 * PPS, QB_A, WIN_A), lambda hp, r, n: (_window_variant(n, nblk), hp, 0, 0))]
    out_shape = [jax.ShapeDtypeStruct((L, d * WIDTH_A), BF16), jax.ShapeDtypeStruct((L, d * WIDTH_A), F32),
                 jax.ShapeDtypeStruct((L, d * WIDTH_A), F32), jax.ShapeDtypeStruct((3, HEADS_A, QB_A, WIN_A), F32)]
    out_specs = [_bs((QB_A, WS), lambda hp, r, n: (n, r * ob + hp)),
                 _bs((L, WS), lambda hp, r, n: (0, r * ob + hp)), _bs((L, WS), lambda hp, r, n: (0, r * ob + hp)),
                 _bs((3, 2 * PPS, QB_A, WIN_A), lambda hp, r, n: (0, hp, 0, 0))]
    dq, dk, dv, db = pl.pallas_call(
        body, out_shape=out_shape, grid=(ob, d, nblk), in_specs=in_specs, out_specs=out_specs,
        compiler_params=_params("arbitrary", "arbitrary", "arbitrary"), name=f"a_bwd_d{d}")(
            pv, pv, pv, do, lse, cterm, bias)
    return dq, dk, dv, db


def _assemble_dproj(a_parts, dq_b, dk_b, dv_b, dga, dgb):
    T = dq_b.shape[0]
    flat = [(a_parts[part][g], d) for part in range(3) for g, d in enumerate(DILATIONS)]
    rest = [dq_b, dk_b, dv_b, dga, dgb]

    def body(*refs):
        views, others = refs[:len(flat)], refs[len(flat):len(flat) + len(rest)]
        o_ref, chunks = refs[len(flat) + len(rest)], refs[len(flat) + len(rest) + 1:]
        col = 0
        for v_ref, (_, d) in zip(views, flat):
            _view_to_rows(v_ref, o_ref, col, d, chunks)
            col += WIDTH_A
        for x_ref in others:
            w = x_ref.shape[1]
            o_ref[:, col:col + w] = x_ref[...].astype(o_ref.dtype)
            col += w

    in_specs = [_bs((VIEW_ROWS // d, d * WIDTH_A), lambda i: (i, 0)) for _, d in flat]
    in_specs += [_bs((VIEW_ROWS, x.shape[1]), lambda i: (i, 0)) for x in rest]
    return pl.pallas_call(
        body, out_shape=jax.ShapeDtypeStruct((T, IN_WIDTH), BF16), grid=(T // VIEW_ROWS,), in_specs=in_specs,
        out_specs=_bs((VIEW_ROWS, IN_WIDTH), lambda i: (i, 0)), scratch_shapes=_view_chunks(),
        compiler_params=_params("parallel"), name="mix_bwd_dproj")(*[a for a, _ in flat], *rest)


def _segment_ones():
    i = np.arange(WIDTH_A)
    return jnp.asarray((i[:, None] // HEAD_A == i[None, :] // HEAD_A).astype(np.float32), dtype=BF16)


def _group_weights(l0, l1, l2):
    m = jnp.maximum(jnp.maximum(l0, l1), l2)
    e = [jnp.exp(l - m) for l in (l0, l1, l2)]
    z = e[0] + e[1] + e[2]
    return [ei / z for ei in e]


def _view_specs():
    return [_bs((VIEW_ROWS // d, d * WIDTH_A), lambda i: (i, 0)) for d in DILATIONS]


def _stage_tiles(n):
    return [pltpu.VMEM((VIEW_ROWS, WIDTH_A), F32)] * n


def _token_rows(v_ref, stage, d, chunks):
    if d == 1:
        return v_ref[...].astype(F32)
    _view_to_rows(v_ref, stage, 0, d, chunks)
    return stage[...]


def _combine_fwd(outs, lses):
    T = outs[0].shape[0] * DILATIONS[0]
    n = len(DILATIONS)

    def body(*refs):
        o_refs, l_refs, oa_ref = refs[:n], refs[n:2 * n], refs[2 * n]
        o_st, l_st, chunks = refs[2 * n + 1:3 * n + 1], refs[3 * n + 1:4 * n + 1], refs[4 * n + 1:]
        o = [_token_rows(o_refs[g], o_st[g], d, chunks) for g, d in enumerate(DILATIONS)]
        w = _group_weights(*[_token_rows(l_refs[g], l_st[g], d, chunks) for g, d in enumerate(DILATIONS)])
        oa_ref[...] = (w[0] * o[0] + w[1] * o[1] + w[2] * o[2]).astype(oa_ref.dtype)

    return pl.pallas_call(
        body, out_shape=jax.ShapeDtypeStruct((T, WIDTH_A), BF16), grid=(T // VIEW_ROWS,),
        in_specs=_view_specs() * 2, out_specs=_bs((VIEW_ROWS, WIDTH_A), lambda i: (i, 0)),
        scratch_shapes=_stage_tiles(2 * n) + _view_chunks(), compiler_params=_params("parallel"), name="a_combine")(*outs, *lses)


def _combine_bwd(doa, outs, lses):
    T = doa.shape[0]
    n = len(DILATIONS)

    def body(*refs):
        d_ref, o_refs, l_refs, seg_ref = refs[0], refs[1:n + 1], refs[n + 1:2 * n + 1], refs[2 * n + 1]
        do_refs, c_refs = refs[2 * n + 2:3 * n + 2], refs[3 * n + 2:4 * n + 2]
        o_st, l_st = refs[4 * n + 2:5 * n + 2], refs[5 * n + 2:6 * n + 2]
        tmp, chunks = refs[6 * n + 2], refs[6 * n + 3:]
        o = [_token_rows(o_refs[g], o_st[g], d, chunks) for g, d in enumerate(DILATIONS)]
        w = _group_weights(*[_token_rows(l_refs[g], l_st[g], d, chunks) for g, d in enumerate(DILATIONS)])
        dv = d_ref[...].astype(F32)
        seg = seg_ref[...]
        tot = jnp.zeros(dv.shape, F32)
        for g in range(n):
            prod = w[g] * dv * o[g]
            hi = prod.astype(BF16)
            lo = (prod - hi.astype(F32)).astype(BF16)
            tot = tot + jnp.dot(hi, seg, preferred_element_type=F32) + jnp.dot(lo, seg, preferred_element_type=F32)
        for g, d in enumerate(DILATIONS):
            for ref, val in ((do_refs[g], w[g] * dv), (c_refs[g], -w[g] * tot)):
                if d == 1:
                    ref[...] = val.astype(ref.dtype)
                else:
                    tmp[...] = val
                    _rows_to_view(tmp, 0, ref, 0, d, chunks)

    views = [jax.ShapeDtypeStruct((T // d, d * WIDTH_A), dt) for dt in (BF16, F32) for d in DILATIONS]
    res = pl.pallas_call(
        body, out_shape=views, grid=(T // VIEW_ROWS,),
        in_specs=[_bs((VIEW_ROWS, WIDTH_A), lambda i: (i, 0))] + _view_specs() * 2 + [_bs((WIDTH_A, WIDTH_A), lambda i: (0, 0))],
        out_specs=_view_specs() * 2, scratch_shapes=_stage_tiles(2 * n + 1) + _view_chunks(),
        compiler_params=_params("parallel"), name="a_combine_bwd")(doa, *outs, *lses, _segment_ones())
    return res[:n], res[n:]


def _rope_tables(T):
    rows = T // GRID_W
    row = jnp.repeat(jnp.arange(rows, dtype=F32), GRID_W)
    col = jnp.tile(jnp.arange(GRID_W, dtype=F32), rows)
    n_freq = HEAD_B // 4
    freq = ROPE_THETA ** (-jnp.arange(n_freq, dtype=F32) / n_freq)
    ang = jnp.concatenate([row[:, None] * freq, col[:, None] * freq], axis=-1)
    cos, sin = jnp.repeat(jnp.cos(ang), 2, axis=1), jnp.repeat(jnp.sin(ang), 2, axis=1)
    sign = jnp.where(jnp.arange(HEAD_B) % 2 == 0, -1.0, 1.0).astype(F32)
    return cos, sin * sign


def _swap_pairs(v):
    even = lax.broadcasted_iota(jnp.int32, v.shape, v.ndim - 1) % 2 == 0
    n = v.shape[-1]
    return jnp.where(even, pltpu.roll(v, n - 1, v.ndim - 1), pltpu.roll(v, 1, v.ndim - 1))


def _qk_fwd(name, proj, col0, n_heads, gain, cos, sin, out_scale=1.0, deps=()):
    T = proj.shape[0]

    def fn(xr, g, c, s):
        xn = _norm_fwd(xr.astype(F32), g)
        return (xn * c + _swap_pairs(xn) * s) * out_scale

    (out,) = _ew(name, fn, [_tiled(proj, HEAD_B, col0 // HEAD_B), _whole(gain), _table(cos), _table(sin)],
                 [(BF16, HEAD_B)], n_rows=T, rows=2048, ncols=n_heads, deps=deps)
    return out


def _qk_bwd(name, dout, proj, col0, n_heads, gain, cos, sin, in_scale=1.0):
    T = proj.shape[0]

    def fn(dv, xr, g, c, s):
        dv = dv.astype(F32) * in_scale
        dxn = c * dv + _swap_pairs(s * dv)
        dx, dgr = _norm_bwd(xr.astype(F32), g, dxn)
        return dx, _colsum(dgr)

    dx, dg = _ew(name, fn, [_tiled(dout, HEAD_B, 0), _tiled(proj, HEAD_B, col0 // HEAD_B), _whole(gain),
                            _table(cos), _table(sin)],
                 [(BF16, HEAD_B)], n_rows=T, rows=2048, reds=(HEAD_B,), ncols=n_heads)
    return dx, jnp.sum(dg, axis=0)


def _gqa_fwd(qn, kn, proj, k_col=0):
    T = qn.shape[0]
    GW = 4 * HEAD_B
    QB = QB_B

    def body(q_ref, k_ref, v_ref, o_ref, l_ref):
        k = k_ref[...]
        v_ones = jnp.concatenate([v_ref[...], jnp.ones((T, HEAD_B), BF16)], axis=1)
        lane = lax.broadcasted_iota(jnp.int32, (QB, HEAD_B), 1)
        heads = range(4)
        s = [lax.dot_general(q_ref[:, g * HEAD_B:(g + 1) * HEAD_B], k, (NT, ((), ())), preferred_element_type=F32)
             for g in heads]
        m = [jnp.max(x, axis=-1, keepdims=True) for x in s]
        pv = [jnp.dot(jnp.exp2(x - mx).astype(BF16), v_ones, preferred_element_type=F32) for x, mx in zip(s, m)]
        l = [x[:, HEAD_B:HEAD_B + 1] for x in pv]
        o = [x[:, :HEAD_B] / lx for x, lx in zip(pv, l)]
        o_ref[...] = jnp.concatenate(o, axis=1).astype(o_ref.dtype)
        lse_all = jnp.zeros((QB, HEAD_B), F32)
        for g in heads:
            lse_all = jnp.where(lane == g, m[g] + jnp.log2(l[g]), lse_all)
        l_ref[...] = lse_all

    return pl.pallas_call(
        body, out_shape=[jax.ShapeDtypeStruct((T, 2 * GW), BF16), jax.ShapeDtypeStruct((2, T, HEAD_B), F32)],
        grid=(2, T // QB),
        in_specs=[_bs((QB, GW), lambda kv, i: (i, kv)), _bs((T, HEAD_B), lambda kv, i: (0, k_col + kv)),
                  _bs((T, HEAD_B), lambda kv, i: (0, B_V // HEAD_B + kv))],
        out_specs=[_bs((QB, GW), lambda kv, i: (i, kv)), _bs((None, QB, HEAD_B), lambda kv, i: (kv, i, 0))],
        compiler_params=_params("parallel", "parallel"), name="b_fwd")(qn, kn, proj)


def _gqa_bwd(qn, kn, proj, o, lse, do, deps=(), k_col=0):
    T = qn.shape[0]
    GW = 4 * HEAD_B

    def body(q_ref, k_ref, v_ref, o_ref, l_ref, do_ref, *rest):
        dq_ref, dk_ref, dv_ref = rest[-3:]
        i = pl.program_id(1)

        @pl.when(i == 0)
        def _():
            dk_ref[...] = jnp.zeros_like(dk_ref)
            dv_ref[...] = jnp.zeros_like(dv_ref)

        k, v = k_ref[...], v_ref[...]
        lse_all = l_ref[...]
        for g in range(4):
            cols = slice(g * HEAD_B, (g + 1) * HEAD_B)
            q, dob = q_ref[:, cols], do_ref[:, cols]
            delta = jnp.sum(dob.astype(F32) * o_ref[:, cols].astype(F32), axis=-1, keepdims=True)
            s = lax.dot_general(q, k, (NT, ((), ())), preferred_element_type=F32)
            p = jnp.exp2(s - lse_all[:, g:g + 1])
            dp = lax.dot_general(dob, v, (NT, ((), ())), preferred_element_type=F32)
            ds = (p * (dp - delta)).astype(BF16)
            dq_ref[:, cols] = jnp.dot(ds, k, preferred_element_type=F32).astype(dq_ref.dtype)
            dk_ref[...] += lax.dot_general(ds, q, (TN, ((), ())), preferred_element_type=F32)
            dv_ref[...] += lax.dot_general(p.astype(BF16), dob, (TN, ((), ())), preferred_element_type=F32)

    return pl.pallas_call(
        body, out_shape=[jax.ShapeDtypeStruct((T, 2 * GW), BF16), jax.ShapeDtypeStruct((T, 2 * HEAD_B), F32),
                         jax.ShapeDtypeStruct((T, 2 * HEAD_B), F32)],
        grid=(2, T // QB_B),
        in_specs=[_bs((QB_B, GW), lambda kv, i: (i, kv)), _bs((T, HEAD_B), lambda kv, i: (0, k_col + kv)),
                  _bs((T, HEAD_B), lambda kv, i: (0, B_V // HEAD_B + kv)), _bs((QB_B, GW), lambda kv, i: (i, kv)),
                  _bs((None, QB_B, HEAD_B), lambda kv, i: (kv, i, 0)), _bs((QB_B, GW), lambda kv, i: (i, kv))] + _any_specs(len(deps)),
        out_specs=[_bs((QB_B, GW), lambda kv, i: (i, kv)), _bs((T, HEAD_B), lambda kv, i: (0, kv)),
                   _bs((T, HEAD_B), lambda kv, i: (0, kv))],
        compiler_params=_params("parallel", "arbitrary"), name="b_bwd")(qn, kn, proj, o, lse, do, *deps)


def _local_step(x, target, small, get_w, put_g, deps=(), prefetch_w=lambda name, after: [], take_rider=lambda steps, after: None):
    T, D = x.shape
    gs = {}

    bias = _bias_tiles(small["rel_bias"])
    cos, sin = _rope_tables(T)
    (x1, h2), ffn1_saved = _ffn_fwd("ffn1", x, small["ffn1_norm"], lambda name, after: get_w(name, [after, bias, cos, sin]), deps,
                                    tail_ins=[small["mix_norm"]], tail_fn=lambda y, g: (y, _norm_fwd(y, g)), tail_outs=(F32, BF16))
    w_in = get_w("w_in", h2)
    nq = w_in.shape[2]
    tpq = nq // WIDTH_A

    def proj_tile(j, k):
        c = j * tpq + k
        return jnp.where(c < 3 * len(DILATIONS), (c % 3) * 3 + c // 3, c)

    proj = _mm("mix_in", (4, tpq),
               [(h2, _resident((T, D), lambda j, k: (0, 0)), w_in, _bs((None, D, WIDTH_A), lambda j, k: (j, 0, k)))],
               jax.ShapeDtypeStruct((T, IN_WIDTH), BF16), _bs((T, WIDTH_A), lambda j, k: (0, proj_tile(j, k))), NN)

    a_views = [_group_view(proj, grp, d) for grp, d in enumerate(DILATIONS)]
    a_outs, a_lses = [], []
    for grp, d in enumerate(DILATIONS):
        o, l = _dil_fwd(a_views[grp], bias[grp], d)
        a_outs.append(o)
        a_lses.append(l)
    o_a = _combine_fwd(a_outs, a_lses)

    qk_gain = jnp.concatenate([jnp.tile(small["q_norm"] * QK_SCALE_LOG2, (8, 1)), jnp.tile(small["k_norm"], (2, 1))])[:, None, :]
    qkn = _qk_fwd("b_qknorm", proj, B_Q, 10, qk_gain, cos, sin, deps=prefetch_w("w_branch_a", proj))
    qn, kn, k_col = qkn, qkn, 8
    o_b, lse_b = _gqa_fwd(qn, kn, proj, k_col)
    ahead = prefetch_w("ffn2_w1", o_b)

    wa, wb, wo = get_w("w_branch_a", o_b), get_w("w_branch_b", o_b), get_w("w_out", o_b)
    bg_a, bg_b = small["b_gate"][:, :D], small["b_gate"][:, D:]
    n_a = wa.shape[0]

    def merge_out(oa_ref, ob_ref, ga_ref, gb_ref, x1_ref, wa_ref, wb_ref, wo_ref, ba_ref, bb_ref, g2_ref, *rest):
        ta_ref, tb_ref, mg_ref, x2_ref, hn_ref = rest[-5:]
        oa = oa_ref[...]
        ta = jnp.concatenate([jnp.dot(oa, wa_ref[j], preferred_element_type=F32) for j in range(n_a)], axis=1)
        tb = jnp.dot(ob_ref[...], wb_ref[...], preferred_element_type=F32)
        sa = _sigmoid(ga_ref[...].astype(F32) + ba_ref[...])
        sb = _sigmoid(gb_ref[...].astype(F32) + bb_ref[...])
        merged = (sa * ta + sb * tb).astype(BF16)
        ta_ref[...], tb_ref[...], mg_ref[...] = ta.astype(BF16), tb.astype(BF16), merged
        y = x1_ref[...] + jnp.dot(merged, wo_ref[...], preferred_element_type=F32)
        x2_ref[...] = y
        hn_ref[...] = _norm_fwd(y, g2_ref[...]).astype(BF16)

    row = _bs((512, D), lambda i: (i, 0))
    gate_specs = [_bs((512, D), lambda i: (i, G_A // D)), _bs((512, D), lambda i: (i, G_B // D))]
    whole2, whole3 = (lambda i: (0, 0)), (lambda i: (0, 0, 0))
    vec = _bs((1, D), whole2)
    t_a, t_b, merged, x2, hn2 = pl.pallas_call(
        merge_out, out_shape=[jax.ShapeDtypeStruct((T, D), BF16)] * 3 + [jax.ShapeDtypeStruct((T, D), F32), jax.ShapeDtypeStruct((T, D), BF16)],
        grid=(T // 512,),
        in_specs=[_bs((512, WIDTH_A), lambda i: (i, 0)), row] + gate_specs + [row, _resident(wa.shape, whole3), _resident((D, D), whole2),
                                                                                _resident((D, D), whole2), vec, vec, vec]
        + _any_specs(len(ahead)),
        out_specs=[row] * 5, compiler_params=_params("parallel"), name="mix_merge_out")(
            o_a, o_b, proj, proj, x1, wa, wb, wo, bg_a, bg_b, small["ffn2_norm"], *ahead)

    def head(xv, g, tv):
        r = _rstd(xv)
        xh = xv * r
        e = xh * g - tv
        dy = e * (1.0 / D)
        dxh = dy * g
        dx = r * (dxh - xh * jnp.mean(dxh * xh, axis=-1, keepdims=True))
        return dx, 0.5 * dx, _colsum(e * e) * (0.5 / D), _colsum(dy * xh)

    (dx3, dx3_half, loss_cols, g_final), ffn2_saved = _ffn_fwd(
        "ffn2", x2, small["ffn2_norm"], get_w, h=hn2, tail_ins=[small["final_norm"].reshape(1, D), target], tail_fn=head,
        tail_outs=(F32, BF16), tail_reds=(D, D))
    gs["final_norm"] = g_final.reshape(D)

    dx2, _, dmix, gs["ffn2_norm"] = _ffn_bwd("ffn2", x2, small["ffn2_norm"], get_w, put_g, ffn2_saved, dx3, dx3_half,
                                             also_bf16=True)
    g_out = _mm_wgrad("mix_bwd_dwout", merged, dmix, a_cols=D // 4, b_cols=None, tm=256, tn=512, J=4).reshape(D, D)

    def merge_out_bwd(dx_ref, ta_ref, tb_ref, ga_ref, gb_ref, wa_ref, wb_ref, wo_ref, ba_ref, bb_ref,
                      dta_ref, dtb_ref, dga_ref, dgb_ref, doa_ref, dob_ref, dba_ref, dbb_ref):
        dm = lax.dot_general(dx_ref[...], wo_ref[...], (NT, ((), ())), preferred_element_type=F32)
        ta, tb = ta_ref[...].astype(F32), tb_ref[...].astype(F32)
        sa = _sigmoid(ga_ref[...].astype(F32) + ba_ref[...])
        sb = _sigmoid(gb_ref[...].astype(F32) + bb_ref[...])
        dga, dgb = dm * ta * sa * (1.0 - sa), dm * tb * sb * (1.0 - sb)
        dta, dtb = (dm * sa).astype(BF16), (dm * sb).astype(BF16)
        dta_ref[...], dtb_ref[...] = dta, dtb
        dga_ref[...], dgb_ref[...] = dga.astype(BF16), dgb.astype(BF16)
        w = wa_ref.shape[2]
        doa = sum(lax.dot_general(dta[:, j * w:(j + 1) * w], wa_ref[j], (NT, ((), ())), preferred_element_type=F32) for j in range(n_a))
        doa_ref[...] = doa.astype(BF16)
        dob_ref[...] = lax.dot_general(dtb, wb_ref[...], (NT, ((), ())), preferred_element_type=F32).astype(BF16)

        @pl.when(pl.program_id(0) == 0)
        def _():
            dba_ref[...] = jnp.zeros_like(dba_ref)
            dbb_ref[...] = jnp.zeros_like(dbb_ref)
        dba_ref[...] += _colsum(dga)
        dbb_ref[...] += _colsum(dgb)

    rowb = _bs((256, D), lambda i: (i, 0))
    gate_specs = [_bs((256, D), lambda i: (i, G_A // D)), _bs((256, D), lambda i: (i, G_B // D))]
    dta, dtb, dga, dgb, do_a, do_b, dba, dbb = pl.pallas_call(
        merge_out_bwd,
        out_shape=[jax.ShapeDtypeStruct((T, D), BF16)] * 4 + [jax.ShapeDtypeStruct((T, WIDTH_A), BF16), jax.ShapeDtypeStruct((T, D), BF16)]
        + [jax.ShapeDtypeStruct((1, D), F32)] * 2,
        grid=(T // 256,),
        in_specs=[rowb, rowb, rowb] + gate_specs + [_resident(wa.shape, whole3), _resident((D, D), whole2), _resident((D, D), whole2), vec, vec],
        out_specs=[rowb] * 4 + [_bs((256, WIDTH_A), lambda i: (i, 0)), rowb, vec, vec],
        compiler_params=_params("arbitrary"), name="mix_merge_out_bwd")(dmix, t_a, t_b, proj, proj, wa, wb, wo, bg_a, bg_b)
    gs["b_gate"] = jnp.concatenate([dba, dbb], axis=1)

    g_a = _mm_wgrad("mix_bwd_dwa", o_a, dta, a_cols=None, b_cols=D // 4, tm=WIDTH_A, tn=256, J=4)
    g_b = _mm_wgrad("mix_bwd_dwb", o_b, dtb, a_cols=D // 4, b_cols=None, tm=256, tn=512, J=4).reshape(D, D)
    deps = put_g({"w_out": g_out, "w_branch_a": g_a, "w_branch_b": g_b})

    dqn, dkn, dv_b = _gqa_bwd(qn, kn, proj, o_b, lse_b, do_b, deps, k_col)
    dq_b, gs["q_norm"] = _qk_bwd("b_bwd_qnorm", dqn, proj, B_Q, 8, small["q_norm"], cos, sin, in_scale=HEAD_B ** -0.5)
    dk_b, gs["k_norm"] = _qk_bwd("b_bwd_knorm", dkn, proj, B_K, 2, small["k_norm"], cos, sin, in_scale=1.0 / LOG2_E)

    do_groups, c_groups = _combine_bwd(do_a, a_outs, a_lses)
    dqs, dks, dvs, dbs = [], [], [], []
    for grp, d in enumerate(DILATIONS):
        dq, dk, dv, db = _dil_bwd(a_views[grp], bias[grp], do_groups[grp], a_lses[grp], c_groups[grp], d)
        dqs.append(dq), dks.append(dk), dvs.append(dv), dbs.append(db)
    gs["rel_bias"] = _bias_grad(jnp.stack(dbs))

    dproj = _assemble_dproj([dqs, dks, dvs], dq_b, dk_b, dv_b, dga, dgb)
    nq = w_in.shape[2]
    g_in = _mm("mix_bwd_dwin", (4, tpq),
               [(h2, _resident((T, D), lambda j, k: (0, 0)), dproj, _bs((T, WIDTH_A), lambda j, k: (0, j * tpq + k)))],
               jax.ShapeDtypeStruct((4, D, nq), BF16), _bs((None, D, WIDTH_A), lambda j, k: (j, 0, k)), TN)
    deps = put_g({"w_in": g_in})
    dx1, dx1_half, gs["mix_norm"] = _dh_norm_bwd(
        "mix_bwd_dh", 256,
        [(dproj, _bs((256, nq), lambda i, j=j: (i, j)), w_in, _resident((None, D, nq), lambda i, j=j: (j, 0, 0))) for j in range(4)],
        NT, x1, small["mix_norm"], dx2, deps)

    dx0, _, gs["ffn1_norm"] = _ffn_bwd("ffn1", x, small["ffn1_norm"], get_w, put_g, ffn1_saved, dx1, dx1_half, last=True,
                                       take_rider=take_rider)
    return loss_cols, dx0, gs


def _position():
    return lax.axis_index("x"), lax.axis_index("y"), lax.axis_index("c")


def _any_specs(n):
    return [pl.BlockSpec(memory_space=pl.ANY)] * n


HBM_SPEC = pl.BlockSpec(memory_space=pltpu.HBM)
SEM_SPEC = pl.BlockSpec(memory_space=pltpu.SEMAPHORE)
DATAFLOW_EFFECT = pltpu.SideEffectType.DATAFLOW_SIDE_EFFECTING
N_PEER_CHIPS = 3
LANES = 128


def _quarter_copies(srcs, lands, send_sems, recv_sems, mode):
    x, y, c = _position()
    me = 2 * x + y
    peers = [(1 - x, y, c), (x, 1 - y, c), (1 - x, 1 - y, c)]
    copies = []
    for src, land, send, recv in zip(srcs, lands, send_sems, recv_sems):
        if mode == "sibling":
            copies.append(pltpu.make_async_remote_copy(src_ref=src, dst_ref=land, send_sem=send.at[0], recv_sem=recv.at[0],
                                                       device_id=(x, y, 1 - c), device_id_type=MESH))
            continue
        if mode == "fill":
            half = land.shape[1] // 2
            for p, (px, py, _) in enumerate(peers):
                part = land.at[2 * px + py, pl.ds(c * half, half)]
                copies.append(pltpu.make_async_remote_copy(src_ref=part, dst_ref=part, send_sem=send.at[p], recv_sem=recv.at[p],
                                                           device_id=(x, y, 1 - c), device_id_type=MESH))
            continue
        scatter = mode == "scatter"
        half = land.shape[1] // 2
        mine = land.at[me, pl.ds(c * half, half)]
        for p, (px, py, pc) in enumerate(peers):
            copies.append(pltpu.make_async_remote_copy(
                src_ref=src.at[2 * px + py] if scatter else mine, dst_ref=land.at[me] if scatter else mine,
                send_sem=send.at[p], recv_sem=recv.at[p], device_id=(px, py, pc), device_id_type=MESH))
    return copies


def _fill_from_sibling(name, stacks):
    n = len(stacks)

    def body(*refs):
        outs = refs[n:2 * n]
        send_sems, recv_sems = refs[2 * n:]
        x, y, c = _position()
        copies = []
        for i, ref in enumerate(outs):
            half = ref.shape[1] // 2
            rows = pl.ds(c * half, half)
            for p, k in enumerate((2 * (1 - x) + y, 2 * x + (1 - y), 2 * (1 - x) + (1 - y))):
                cp = pltpu.make_async_remote_copy(ref.at[k, rows], ref.at[k, rows], send_sems.at[3 * i + p], recv_sems.at[3 * i + p],
                                                  device_id=(x, y, 1 - c), device_id_type=MESH)
                cp.start()
                copies.append(cp)
        for cp in copies:
            cp.wait()

    return pl.pallas_call(
        body, out_shape=[jax.ShapeDtypeStruct(s.shape, s.dtype) for s in stacks],
        in_specs=_any_specs(n), out_specs=_any_specs(n), input_output_aliases={i: i for i in range(n)},
        scratch_shapes=[pltpu.SemaphoreType.DMA((N_PEER_CHIPS * n,)), pltpu.SemaphoreType.DMA((N_PEER_CHIPS * n,))],
        compiler_params=pltpu.CompilerParams(has_side_effects=True), name=name)(*stacks)


def _exchange_start(name, srcs, lands, mode):
    n = len(lands)
    arrays = list(lands) if srcs is None else list(srcs) + list(lands)
    k = len(arrays)

    def body(*refs):
        land_refs = refs[k - n:k]
        send_sems, recv_sems = refs[k:k + n], refs[k + n:k + 2 * n]
        token = refs[2 * k + 2 * n]
        for cp in _quarter_copies(refs[:n], land_refs, send_sems, recv_sems, mode):
            cp.start()
        token[...] = jnp.zeros_like(token)

    sem = pltpu.SemaphoreType.DMA((N_PEER_CHIPS,))
    out_shape = [sem] * (2 * n) + [pltpu.HBM(a.shape, a.dtype) for a in arrays] + [jax.ShapeDtypeStruct((8, LANES), F32)]
    res = pl.pallas_call(
        body, name=name, out_shape=out_shape, in_specs=[HBM_SPEC] * k,
        out_specs=[SEM_SPEC] * (2 * n) + [HBM_SPEC] * k + [pl.BlockSpec(memory_space=pltpu.VMEM)],
        input_output_aliases={i: 2 * n + i for i in range(k)},
        compiler_params=pltpu.CompilerParams(has_side_effects=DATAFLOW_EFFECT),
    )(*[pltpu.with_memory_space_constraint(a, pltpu.HBM) for a in arrays])
    thru = res[2 * n:2 * n + k]
    return res[:n], res[n:2 * n], (None if srcs is None else thru[:n]), thru[k - n:], res[2 * n + k]


def _exchange_wait(name, srcs, lands, send_sems, recv_sems, after, mode):
    n = len(lands)
    arrays = list(lands) if srcs is None else list(srcs) + list(lands)
    k = len(arrays)
    after = list(after) if isinstance(after, (list, tuple)) else [after]

    def body(*refs):
        sends, recvs = refs[k:k + n], refs[k + n:k + 2 * n]
        for cp in _quarter_copies(refs[:n], refs[k - n:k], sends, recvs, mode):
            cp.wait_send()
            cp.wait_recv()

    res = pl.pallas_call(
        body, name=name, out_shape=[pltpu.HBM(a.shape, a.dtype) for a in arrays],
        in_specs=[HBM_SPEC] * k + [SEM_SPEC] * (2 * n) + _any_specs(len(after)),
        out_specs=[HBM_SPEC] * k, input_output_aliases={i: i for i in range(k)},
        compiler_params=pltpu.CompilerParams(has_side_effects=DATAFLOW_EFFECT),
    )(*arrays, *send_sems, *recv_sems, *after)
    return (None if srcs is None else res[:n]), res[k - n:]


def _own_slots(name, srcs, from_stack=False):
    n = len(srcs)
    me = (2 * lax.axis_index("x") + lax.axis_index("y")).astype(jnp.int32).reshape(1)

    def body(me_ref, *refs):
        for x_ref, o_ref in zip(refs[:n], refs[n:]):
            o_ref[...] = x_ref[...].astype(o_ref.dtype)

    in_specs, out_specs, out_shape = [], [], []
    for src in srcs:
        R, C = src.shape[-2:]
        in_specs.append(pl.BlockSpec((None, R // 2, C), lambda i, me_ref: (me_ref[0], i, 0)) if from_stack
                        else pl.BlockSpec((R // 2, C), lambda i, me_ref: (i, 0)))
        out_specs.append(pl.BlockSpec((None, R // 2, C), lambda i, me_ref: (me_ref[0], i, 0)))
        out_shape.append(jax.ShapeDtypeStruct((4, R, C), BF16))
    grid_spec = pltpu.PrefetchScalarGridSpec(num_scalar_prefetch=1, grid=(2,), in_specs=in_specs, out_specs=out_specs)
    return pl.pallas_call(body, out_shape=out_shape, grid_spec=grid_spec, compiler_params=_params("parallel"), name=name)(me, *srcs)


def _allreduce_small(buf):
    R, C = buf.shape
    flips = [(fx, fy, fc) for fx in (0, 1) for fy in (0, 1) for fc in (0, 1)][1:]

    def body(in_ref, out_ref, land_ref, send_sems, recv_sems):
        x, y, c = _position()
        me = 4 * x + 2 * y + c
        copies = []
        for k, (fx, fy, fc) in enumerate(flips):
            px, py, pc = (1 - x if fx else x), (1 - y if fy else y), (1 - c if fc else c)
            cp = pltpu.make_async_remote_copy(in_ref, land_ref.at[me], send_sems.at[k], recv_sems.at[k],
                                              device_id=(px, py, pc), device_id_type=MESH)
            cp.start()
            copies.append(cp)
        land_ref[me] = in_ref[...]
        for cp in copies:
            cp.wait()
        acc = land_ref[0]
        for k in range(1, 8):
            acc = acc + land_ref[k]
        out_ref[...] = acc

    return pl.pallas_call(
        body, out_shape=jax.ShapeDtypeStruct((R, C), F32),
        in_specs=[pl.BlockSpec(memory_space=pltpu.VMEM)], out_specs=pl.BlockSpec(memory_space=pltpu.VMEM),
        scratch_shapes=[pltpu.VMEM((8, R, C), F32), pltpu.SemaphoreType.DMA((7,)), pltpu.SemaphoreType.DMA((7,))],
        compiler_params=pltpu.CompilerParams(has_side_effects=True), name="allreduce_small")(buf)


def _adamw_math(w, g, m, v):
    m2 = ADAM_B1 * m + (1.0 - ADAM_B1) * g
    v2 = ADAM_B2 * v + (1.0 - ADAM_B2) * (g * g)
    m_hat = m2 / (1.0 - ADAM_B1 ** ADAM_STEP)
    v_hat = v2 / (1.0 - ADAM_B2 ** ADAM_STEP)
    delta = -ADAM_LR * (m_hat / (jnp.sqrt(v_hat) + ADAM_EPS) + ADAM_WD * w)
    return delta, m2, v2


def _adamw_from_partials(wv, mv, vv, *parts):
    def four(a, b, c, d):
        return ((a.astype(F32) + b.astype(F32)) + c.astype(F32)) + d.astype(F32)

    g = four(*parts[:4]) + four(*parts[4:])
    return (g,) + _adamw_math(wv, g, mv, vv)


def _adamw_big(name, w, m, v, mine, theirs):
    R, C = w.shape
    rows = 256 if R % 256 == 0 else R // 2
    nrb = R // rows
    slots = [_tiled(s.reshape(4 * R, C), None, 0, k * nrb) for s in (mine, theirs) for k in range(4)]
    return _ew(name, _adamw_from_partials, [_tiled(w), _tiled(m), _tiled(v)] + slots, [(F32, C)] * 4, n_rows=R, rows=rows)


def _adamw_rider(w, m, v, mine, theirs, steps, deliver):
    R, C = w.shape
    fits = [nb for nb in range(1, steps + 1) if R % nb == 0 and (R // nb) % 16 == 0]
    if not fits:
        return None
    nb = fits[-1]
    rows = R // nb

    def blocks(first):
        return pl.BlockSpec((rows, C), lambda *g: (first + jnp.minimum(g[0], nb - 1), 0))

    flat = [s.reshape(4 * R, C) for s in (mine, theirs)]
    return dict(operands=[w, m, v] + [f for f in flat for _ in range(4)],
                in_specs=[blocks(0)] * 3 + [blocks(k * nb) for _ in flat for k in range(4)],
                out_shape=[jax.ShapeDtypeStruct((R, C), F32)] * 4, out_specs=[blocks(0)] * 4,
                n_blocks=nb, fn=_adamw_from_partials, deliver=lambda outs: deliver(*outs))


BIG = ("ffn1_w1", "ffn1_w3", "ffn1_w2", "w_in", "w_branch_a", "w_branch_b", "w_out", "ffn2_w1", "ffn2_w3", "ffn2_w2")
SMALL = ("ffn1_norm", "mix_norm", "b_gate", "q_norm", "k_norm", "rel_bias", "ffn2_norm", "final_norm")
ORDER = ("ffn1_norm", "ffn1_w1", "ffn1_w3", "ffn1_w2", "mix_norm", "w_in", "b_gate", "q_norm", "k_norm", "rel_bias",
         "w_branch_a", "w_branch_b", "w_out", "ffn2_norm", "ffn2_w1", "ffn2_w3", "ffn2_w2", "final_norm")
TRANSPOSED = ("ffn1_w1", "ffn1_w3", "ffn2_w1", "ffn2_w3")
SIBLING_LAG = 2
LONG_HOST_STEPS = 8
GATHER_GROUPS = (("ffn1_w1", "ffn1_w3"), ("ffn1_w2",), ("w_in",), ("w_branch_a", "w_branch_b", "w_out"),
                 ("ffn2_w1", "ffn2_w3", "ffn2_w2"))


def _pack_small(d):
    rows = []
    for n in SMALL:
        flat = d[n].reshape(-1)
        pad = (-flat.shape[0]) % LANES
        rows.append(jnp.pad(flat, (0, pad)).reshape(-1, LANES))
    buf = jnp.concatenate(rows, axis=0)
    return jnp.pad(buf, ((0, (-buf.shape[0]) % 8), (0, 0)))


def _unpack_small(buf, like):
    out, r = {}, 0
    for n in SMALL:
        size = like[n].size
        nr = -(-size // LANES)
        out[n] = buf[r:r + nr].reshape(-1)[:size].reshape(like[n].shape)
        r += nr
    return out


def kernel(x, ffn1_norm, ffn1_w1, ffn1_w3, ffn1_w2, mix_norm, w_in, b_gate, q_norm, k_norm, rel_bias, w_branch_a, w_branch_b, w_out, ffn2_norm, ffn2_w1, ffn2_w3, ffn2_w2, final_norm, loss_target, m_ffn1_norm, m_ffn1_w1, m_ffn1_w3, m_ffn1_w2, m_mix_norm, m_w_in, m_b_gate, m_q_norm, m_k_norm, m_rel_bias, m_w_branch_a, m_w_branch_b, m_w_out, m_ffn2_norm, m_ffn2_w1, m_ffn2_w3, m_ffn2_w2, m_final_norm, v_ffn1_norm, v_ffn1_w1, v_ffn1_w3, v_ffn1_w2, v_mix_norm, v_w_in, v_b_gate, v_q_norm, v_k_norm, v_rel_bias, v_w_branch_a, v_w_branch_b, v_w_out, v_ffn2_norm, v_ffn2_w1, v_ffn2_w3, v_ffn2_w2, v_final_norm):
    given = dict(locals())
    w = {n: given[n] for n in ORDER}
    m = {n: given["m_" + n] for n in ORDER}
    v = {n: given["v_" + n] for n in ORDER}
    T, D = x.shape[1], x.shape[2]

    def stored(a, n):
        a = a.reshape(a.shape[1:])
        return a.T if n in TRANSPOSED else a

    def returned(a, n):
        return (a.T if n in TRANSPOSED else a).reshape(w[n].shape)

    quarter = {n: stored(w[n], n) for n in BIG}
    send, recv, _, land_thru, token = _exchange_start(
        "gather_start", None, _own_slots("own_weights", [quarter[n] for n in BIG]), "gather")
    index = {n: i for i, n in enumerate(BIG)}
    ready, filling = {}, {}

    def landed_halves(group, after):
        ids = [index[n] for n in group]
        return _exchange_wait("gather_wait_" + group[0], None, [land_thru[i] for i in ids],
                              [send[i] for i in ids], [recv[i] for i in ids], after, "gather")[1]

    def prefetch_w(name, after):
        group = next(g for g in GATHER_GROUPS if name in g)
        started = _exchange_start("fill_start_" + group[0], None, landed_halves(group, after), "fill")
        filling[group] = started
        return [started[4]]

    def get_w(name, after):
        if name not in ready:
            group = next(g for g in GATHER_GROUPS if name in g)
            if group in filling:
                f_send, f_recv, _, thru, _ = filling[group]
                stacks = _exchange_wait("fill_wait_" + group[0], None, thru, f_send, f_recv, after, "fill")[1]
            else:
                stacks = _fill_from_sibling("gather_fill_" + group[0], landed_halves(group, after))
            for n, st in zip(group, stacks):
                ready[n] = st.reshape(D, D) if n in ("w_branch_b", "w_out") else st
        return ready[name]

    scattered, forwarded = [], []

    def forward_oldest(after):
        names, s_sem, r_sem, srcs, lands = scattered.pop(0)
        _, landed = _exchange_wait("scatter_wait_" + names[0], srcs, lands, s_sem, r_sem, after, "scatter")
        started = _exchange_start("sibling_start_" + names[0], landed, [lax.empty(a.shape, a.dtype) for a in landed], "sibling")
        forwarded.append((names,) + tuple(started[:4]))
        return started[4]

    def put_g(grads):
        names = list(grads)
        stacks = [grads[n].reshape((4,) + quarter[n].shape) for n in names]
        lands = _own_slots("own_grad_" + names[0], stacks, from_stack=True)
        started = _exchange_start("scatter_start_" + names[0], stacks, lands, "scatter")
        scattered.append((names,) + tuple(started[:4]))
        tokens = [started[4]]
        if len(scattered) > SIBLING_LAG:
            tokens.append(forward_oldest(started[4]))
        return tokens

    grads, deltas, new_m, new_v = {}, {}, {}, {}
    arrived, riding = {}, set()

    def partials(gi, after):
        if gi not in arrived:
            names, s_sem, r_sem, srcs, lands = forwarded[gi]
            arrived[gi] = _exchange_wait("sibling_wait_" + names[0], srcs, lands, s_sem, r_sem, after, "sibling")
        return arrived[gi]

    def deliver_to(n):
        def deliver(*res):
            grads[n], deltas[n], new_m[n], new_v[n] = [returned(r, n) for r in res]
        return deliver

    def take_rider(steps, after):
        waiting = [(quarter[n].size, gi, k, n) for gi, entry in enumerate(forwarded) for k, n in enumerate(entry[0]) if n not in riding]
        for _, gi, k, n in sorted(waiting, reverse=steps >= LONG_HOST_STEPS):
            mine, theirs = partials(gi, after)
            rider = _adamw_rider(quarter[n], stored(m[n], n), stored(v[n], n), mine[k], theirs[k], steps, deliver_to(n))
            if rider is not None:
                riding.add(n)
                return rider
        return None

    small = {n: w[n] for n in SMALL}
    packed = [_pack_small({n: d[n] for n in SMALL}) for d in (w, m, v)]
    loss_cols, grad_x, gs = _local_step(x.reshape(T, D), loss_target.reshape(T, D), small, get_w, put_g, deps=[token] + packed,
                                        prefetch_w=prefetch_w, take_rider=take_rider)

    after = grad_x
    while scattered:
        after = forward_oldest(after)
    for gi, entry in enumerate(forwarded):
        mine, theirs = partials(gi, after)
        for n, a, b in zip(entry[0], mine, theirs):
            if n not in riding:
                deliver_to(n)(*_adamw_big(f"adamw_{n}", quarter[n], stored(m[n], n), stored(v[n], n), a, b))

    gs = {n: gs[n].reshape(w[n].shape) for n in SMALL}
    packed_g = _pack_small(gs)
    n_small = packed_g.shape[0]
    summed = _allreduce_small(jnp.concatenate([packed_g, loss_cols.reshape(-1, LANES)], axis=0))
    g_small, loss = summed[:n_small], jnp.sum(summed[n_small:])
    R = g_small.shape[0]
    res = _ew("adamw_small", lambda wv, mv, vv, g: (g,) + _adamw_math(wv, g, mv, vv),
              [_tiled(packed[0]), _tiled(packed[1]), _tiled(packed[2]), _tiled(g_small)], [(F32, LANES)] * 4, n_rows=R, rows=R)
    for d, buf in zip((grads, deltas, new_m, new_v), res):
        d.update(_unpack_small(buf, w))

    return (loss, grad_x.reshape(x.shape), *[grads[n] for n in ORDER], *[deltas[n] for n in ORDER],
            *[new_m[n] for n in ORDER], *[new_v[n] for n in ORDER])
```

```python
import functools
import math

import numpy as np
import jax
import jax.numpy as jnp
from jax import lax
from jax.experimental import pallas as pl
from jax.experimental.pallas import tpu as pltpu

F32 = jnp.float32
BF16 = jnp.bfloat16
MESH = pl.DeviceIdType.MESH

NEG_INF = -1e30
EPS = 1e-6
GRID_W = 64
ROPE_THETA = 10000.0
DILATIONS = (1, 4, 16)
BAND_HALF = 64
HEAD_A = 64
HEADS_A = 8
WIDTH_A = HEADS_A * HEAD_A
HEAD_B = 128
LOG2_E = math.log2(math.e)
QK_SCALE_LOG2 = HEAD_B ** -0.5 * LOG2_E
N_BUCKETS = 32
MAX_DISTANCE = 1024
ADAM_LR, ADAM_B1, ADAM_B2, ADAM_EPS, ADAM_WD, ADAM_STEP = 0.001, 0.9, 0.999, 1e-08, 0.01, 10

B_Q, B_K, B_V = 4608, 5632, 5888
G_A, G_B = 6144, 7168
IN_WIDTH = 8192

VMEM_LIMIT_BYTES = 56 * 1024 * 1024
QB_A = 128
QB_B = 256


def _params(*sem):
    return pltpu.CompilerParams(dimension_semantics=sem, vmem_limit_bytes=VMEM_LIMIT_BYTES)


def _bs(shape, fn):
    return pl.BlockSpec(shape, fn)


def _resident(shape, fn):
    return pl.BlockSpec(shape, fn, pipeline_mode=pl.Buffered(1))


def _mm(name, grid, pairs, out_shape, out_spec, dims, *, extras=(), epilogue=None, deps=(), reds=(), rider=None):
    n_pairs, n_extra, n_deps = len(pairs), len(extras), len(deps)
    operands = [p[0] for p in pairs] + [p[2] for p in pairs] + [e[0] for e in extras] + list(deps)
    in_specs = [p[1] for p in pairs] + [p[3] for p in pairs] + [e[1] for e in extras] + _any_specs(n_deps)
    single = not isinstance(out_shape, (list, tuple))
    out_shapes = [out_shape] if single else list(out_shape)
    out_specs = [out_spec] if single else list(out_spec)
    n_out = len(out_shapes)
    out_shapes += [jax.ShapeDtypeStruct((1, w), F32) for w in reds]
    out_specs += [_bs((1, w), lambda *_: (0, 0)) for w in reds]
    n_rin = 0
    if rider is not None:
        assert rider["n_blocks"] <= grid[0]
        n_rin = len(rider["operands"])
        operands += list(rider["operands"])
        in_specs += list(rider["in_specs"])
        out_shapes += list(rider["out_shape"])
        out_specs += list(rider["out_specs"])

    def body(*refs):
        a_refs, b_refs = refs[:n_pairs], refs[n_pairs:2 * n_pairs]
        e_refs = refs[2 * n_pairs:2 * n_pairs + n_extra]
        o_refs = refs[2 * n_pairs + n_extra + n_deps + n_rin:]
        if rider is not None:
            r_in = refs[2 * n_pairs + n_extra + n_deps:2 * n_pairs + n_extra + n_deps + n_rin]
            r_out = o_refs[n_out + len(reds):]

            @pl.when(pl.program_id(0) < rider["n_blocks"])
            def _():
                for ref, val in zip(r_out, rider["fn"](*[r[...] for r in r_in])):
                    ref[...] = val.astype(ref.dtype)
        acc = None
        for a_ref, b_ref in zip(a_refs, b_refs):
            t = lax.dot_general(a_ref[...], b_ref[...], (dims, ((), ())), preferred_element_type=F32)
            acc = t if acc is None else acc + t
        vals = acc if epilogue is None else epilogue(acc, *[e[...] for e in e_refs])
        if not isinstance(vals, (list, tuple)):
            vals = (vals,)
        for o_ref, v in zip(o_refs[:n_out], vals[:n_out]):
            o_ref[...] = v.astype(o_ref.dtype)
        if reds:
            first = functools.reduce(jnp.logical_and, [pl.program_id(ax) == 0 for ax in range(len(grid))])
            for r_ref, v in zip(o_refs[n_out:], vals[n_out:]):
                @pl.when(first)
                def _(r_ref=r_ref):
                    r_ref[...] = jnp.zeros_like(r_ref)
                r_ref[...] += v

    sem = ["arbitrary" if (reds or rider is not None) else "parallel"] * len(grid)
    res = pl.pallas_call(
        body, out_shape=out_shapes, grid=grid, in_specs=in_specs, out_specs=out_specs,
        compiler_params=_params(*sem), name=name)(*operands)
    if rider is not None:
        rider["deliver"](res[n_out + len(reds):])
        res = res[:n_out + len(reds)]
    return res[0] if (single and not reds) else res


NN = ((1,), (0,))
NT = ((1,), (1,))
TN = ((0,), (0,))


def _mm_wgrad(name, a, b, *, a_cols, b_cols, tm, tn, J, deps=(), rider=None):
    def pick(arr, cols, t):
        if arr.ndim == 3:
            T, c = arr.shape[1], arr.shape[2]
            t = min(t, c)
            return T, c, t, (lambda sel: _bs((None, T, t), lambda j, i, k: (j, 0, sel(i, k))))
        T = arr.shape[0]
        c = arr.shape[1] if cols is None else cols
        t = min(t, c)
        per = c // t
        if cols is None:
            if per == 1:
                return T, c, t, (lambda sel: _resident((T, t), lambda j, i, k: (0, 0)))
            return T, c, t, (lambda sel: _bs((T, t), lambda j, i, k: (0, sel(i, k))))
        return T, c, t, (lambda sel: _bs((T, t), lambda j, i, k: (0, j * per + sel(i, k))))
    _, ca, tm, mk_a = pick(a, a_cols, tm)
    _, cb, tn, mk_b = pick(b, b_cols, tn)
    return _mm(name, (J, ca // tm, cb // tn),
               [(a, mk_a(lambda i, k: i), b, mk_b(lambda i, k: k))],
               jax.ShapeDtypeStruct((J, ca, cb), BF16), _bs((None, tm, tn), lambda j, i, k: (j, i, k)), TN, deps=deps, rider=rider)


def _tiled(arr, width=None, col=0, rowblk=0):
    return ("t", arr, arr.shape[1] if width is None else width, col, rowblk)


def _table(arr):
    return ("f", arr)


def _whole(arr):
    return ("w", arr)


def _ew(name, fn, ins, outs, *, n_rows, rows, reds=(), ncols=1, deps=()):
    nrb = n_rows // rows
    n_deps = len(deps)
    operands, in_specs = [], []
    for spec in ins:
        if spec[0] == "t":
            _, arr, width, col, rowblk = spec
            step = 1 if ncols > 1 else 0
            in_specs.append(_bs((rows, width), lambda c, i, col=col, rowblk=rowblk, step=step: (rowblk + i, col + c * step)))
        elif spec[0] == "f":
            arr = spec[1]
            in_specs.append(_bs((rows, arr.shape[1]), lambda c, i: (i, 0)))
        else:
            arr = spec[1]
            nd = arr.ndim
            if nd == 3:
                in_specs.append(_bs((None,) + arr.shape[1:], lambda c, i: (c, 0, 0)))
            else:
                in_specs.append(_bs(arr.shape, lambda c, i, nd=nd: (0,) * nd))
        operands.append(arr)
    out_shapes = [jax.ShapeDtypeStruct((n_rows, ncols * w), dt) for dt, w in outs]
    out_specs = [_bs((rows, w), lambda c, i: (i, c)) for _, w in outs]
    out_shapes += [jax.ShapeDtypeStruct((ncols, 1, w), F32) for w in reds]
    out_specs += [_bs((None, 1, w), lambda c, i: (c, 0, 0)) for w in reds]
    n_in, n_out, n_red = len(ins), len(outs), len(reds)
    operands += list(deps)
    in_specs += _any_specs(n_deps)

    def body(*refs):
        vals = fn(*[r[...] for r in refs[:n_in]])
        if not isinstance(vals, (tuple, list)):
            vals = (vals,)
        o_refs = refs[n_in + n_deps:]
        for o_ref, v in zip(o_refs[:n_out], vals[:n_out]):
            o_ref[...] = v.astype(o_ref.dtype)
        if n_red:
            i = pl.program_id(1)
            for r_ref, v in zip(o_refs[n_out:], vals[n_out:]):
                @pl.when(i == 0)
                def _(r_ref=r_ref):
                    r_ref[...] = jnp.zeros_like(r_ref)
                r_ref[...] += v

    res = pl.pallas_call(
        body, out_shape=out_shapes, grid=(ncols, nrb), in_specs=in_specs, out_specs=out_specs,
        compiler_params=_params("parallel", "arbitrary" if n_red else "parallel"), name=name)(*operands)
    return res


def _colsum(v):
    return jnp.sum(v, axis=0, keepdims=True)


def _rstd(x):
    return lax.rsqrt(jnp.mean(x * x, axis=-1, keepdims=True) + EPS)


def _sigmoid(x):
    return 0.5 * jnp.tanh(0.5 * x) + 0.5


def _norm_fwd(x, g):
    return x * _rstd(x) * g


def _norm_bwd(x, g, dy):
    r = _rstd(x)
    xh = x * r
    dxh = dy * g
    dx = r * (dxh - xh * jnp.mean(dxh * xh, axis=-1, keepdims=True))
    return dx, dy * xh


def _row_spec(arr, rows):
    if arr.shape[0] == 1:
        return _bs(arr.shape, lambda i: (0, 0))
    return _bs((rows, arr.shape[1]), lambda i: (i, 0))


def _ffn_fwd(tag, x, gain, get_w, deps=(), *, h=None, tail_ins=(), tail_fn=None, tail_outs=(F32,), tail_reds=()):
    T, D = x.shape
    if h is None:
        (h,) = _ew(f"{tag}_norm", lambda xv, g: _norm_fwd(xv, g), [_tiled(x), _whole(gain)], [(BF16, D)], n_rows=T, rows=512,
                   deps=deps)
    w1, w3 = get_w(f"{tag}_w1", h), get_w(f"{tag}_w3", h)
    J, f, _ = w1.shape
    tm = 1024

    def up(h_ref, w1_ref, w3_ref, u_ref, g_ref, a_ref):
        hv = h_ref[...]
        u = lax.dot_general(hv, w1_ref[...], (NT, ((), ())), preferred_element_type=F32)
        g = lax.dot_general(hv, w3_ref[...], (NT, ((), ())), preferred_element_type=F32)
        u_ref[...] = u.astype(BF16)
        g_ref[...] = g.astype(BF16)
        a_ref[...] = (u * _sigmoid(u) * g).astype(BF16)

    slab = _bs((None, tm, f), lambda j, i: (j, i, 0))
    w_spec = _bs((None, f, D), lambda j, i: (j, 0, 0))
    u, g, a = pl.pallas_call(
        up, out_shape=[jax.ShapeDtypeStruct((J, T, f), BF16)] * 3, grid=(J, T // tm),
        in_specs=[_bs((tm, D), lambda j, i: (i, 0)), w_spec, w_spec], out_specs=[slab] * 3,
        compiler_params=_params("parallel", "parallel"), name=f"{tag}_up")(h, w1, w3)
    w2 = get_w(f"{tag}_w2", a)
    def tail(acc, xv, *rest):
        y = xv + 0.5 * acc
        return y if tail_fn is None else tail_fn(y, *rest)

    row = _bs((512, D), lambda i: (i, 0))
    res = _mm(f"{tag}_down", (T // 512,),
              [(a, _bs((None, 512, f), lambda i, j=j: (j, i, 0)), w2, _resident((None, f, D), lambda i, j=j: (j, 0, 0)))
               for j in range(J)],
              [jax.ShapeDtypeStruct((T, D), dt) for dt in tail_outs], [row] * len(tail_outs), NN,
              extras=[(x, row)] + [(t, _row_spec(t, 512)) for t in tail_ins], epilogue=tail, reds=tail_reds)
    return res, (h, u, g, a)


def _dh_norm_bwd(name, rows, pairs, dims, x, gain, dres, deps, also_bf16=False, rider=None):
    T, D = x.shape

    def epilogue(dh, xv, gv, dr):
        dx, dgr = _norm_bwd(xv, gv, dh)
        dx = dx + dr
        return (dx, 0.5 * dx) + ((dx,) if also_bf16 else ()) + (_colsum(dgr),)

    dts = [F32, BF16] + ([BF16] if also_bf16 else [])
    row = _bs((rows, D), lambda i: (i, 0))
    return _mm(name, (T // rows,), pairs, [jax.ShapeDtypeStruct((T, D), dt) for dt in dts], [row] * len(dts), dims,
               extras=[(x, row), (gain, _row_spec(gain, rows)), (dres, row)], epilogue=epilogue, deps=deps, reds=(D,), rider=rider)


def _ffn_bwd(tag, x, gain, get_w, put_g, saved, dy, dy_half, also_bf16=False, last=False, take_rider=lambda steps, after: None):
    h, u, g, a = saved
    T, D = x.shape
    w1, w3, w2 = [get_w(f"{tag}_{n}", dy_half) for n in ("w1", "w3", "w2")]
    J, f, _ = w1.shape
    dw2 = _mm_wgrad(f"{tag}_bwd_dw2", a, dy_half, a_cols=None, b_cols=None, tm=f, tn=D, J=J, rider=take_rider(J, dy_half))
    deps = put_g({f"{tag}_w2": dw2}) if last else []
    tm = 1024

    def up_bwd(dy_ref, w2_ref, u_ref, g_ref, *rest):
        du_ref, dg_ref = rest[-2:]
        da = lax.dot_general(dy_ref[...], w2_ref[...], (NT, ((), ())), preferred_element_type=F32)
        uv, gv = u_ref[...].astype(F32), g_ref[...].astype(F32)
        s = _sigmoid(uv)
        silu = uv * s
        du_ref[...] = (da * gv * (s + silu - silu * s)).astype(BF16)
        dg_ref[...] = (da * silu).astype(BF16)

    slab = _bs((None, tm, f), lambda j, i: (j, i, 0))
    du, dg = pl.pallas_call(
        up_bwd, out_shape=[jax.ShapeDtypeStruct((J, T, f), BF16)] * 2, grid=(J, T // tm),
        in_specs=[_bs((tm, D), lambda j, i: (i, 0)), _bs((None, f, D), lambda j, i: (j, 0, 0)), slab, slab] + _any_specs(len(deps)),
        out_specs=[slab] * 2, compiler_params=_params("parallel", "parallel"), name=f"{tag}_bwd_up")(dy_half, w2, u, g, *deps)
    dw1 = _mm_wgrad(f"{tag}_bwd_dw1", du, h, a_cols=None, b_cols=None, tm=f, tn=D, J=J)
    deps = put_g({f"{tag}_w1": dw1}) if last else []
    dw3 = _mm_wgrad(f"{tag}_bwd_dw3", dg, h, a_cols=None, b_cols=None, tm=f, tn=D, J=J, deps=deps)
    deps = put_g({f"{tag}_w3": dw3} if last else {f"{tag}_w2": dw2, f"{tag}_w1": dw1, f"{tag}_w3": dw3})
    pairs = []
    for j in range(J):
        a_spec = _bs((None, 256, f), lambda i, j=j: (j, i, 0))
        w_spec = _resident((None, f, D), lambda i, j=j: (j, 0, 0))
        pairs += [(du, a_spec, w1, w_spec), (dg, a_spec, w3, w_spec)]
    return _dh_norm_bwd(f"{tag}_bwd_dh", 256, pairs, NN, x, gain, dy, deps, also_bf16, rider=take_rider(T // 256, dw3))


def _t5_bucket(rel):
    n = N_BUCKETS // 2
    max_exact = n // 2
    ret = jnp.where(rel > 0, n, 0)
    a = jnp.abs(rel)
    af = jnp.maximum(a, 1).astype(F32)
    large = max_exact + (jnp.log(af / max_exact) / math.log(MAX_DISTANCE / max_exact) * (n - max_exact)).astype(jnp.int32)
    large = jnp.minimum(large, n - 1)
    return ret + jnp.where(a < max_exact, a, large)


WIN_A = QB_A + 2 * BAND_HALF
WIN_SHIFTS = (0, BAND_HALF, 2 * BAND_HALF)


def _window_variant(n, nblk):
    return jnp.where(n == 0, 0, jnp.where(n == nblk - 1, 2, 1))


def _window_start(n, nblk):
    return pl.multiple_of(jnp.clip(n * QB_A - BAND_HALF, 0, nblk * QB_A - WIN_A), BAND_HALF)


def _band_steps(xp=jnp):
    qi = xp.arange(QB_A, dtype=xp.int32)[None, :, None]
    kj = xp.arange(WIN_A, dtype=xp.int32)[None, None, :]
    return kj - qi - xp.asarray(WIN_SHIFTS, dtype=xp.int32)[:, None, None]


def _bias_tiles(rel_bias):
    wide = QB_A + 2 * WIN_SHIFTS[-1]
    qi = jnp.arange(QB_A, dtype=jnp.int32)[:, None]
    steps = jnp.arange(wide, dtype=jnp.int32)[None, :] - WIN_SHIFTS[-1] - qi
    buckets = jnp.stack([_t5_bucket(steps * d) for d in DILATIONS])
    inband = (jnp.abs(steps) <= BAND_HALF).astype(jnp.int32)
    n_heads = rel_bias.shape[1]

    def body(tab_ref, b_ref, m_ref, o_ref):
        hd = pl.program_id(0)
        bkt = b_ref[...]
        acc = jnp.zeros(bkt.shape, F32)
        for b in range(N_BUCKETS):
            acc = jnp.where(bkt == b, tab_ref[b, hd], acc)
        o_ref[...] = jnp.where(m_ref[...] > 0, acc, NEG_INF)

    base = pl.pallas_call(
        body, out_shape=jax.ShapeDtypeStruct((n_heads, QB_A, wide), F32), grid=(n_heads,),
        in_specs=[pl.BlockSpec(memory_space=pltpu.SMEM),
                  _bs((None, QB_A, wide), lambda hd: (hd // HEADS_A, 0, 0)),
                  _bs((QB_A, wide), lambda hd: (0, 0))],
        out_specs=_bs((None, QB_A, wide), lambda hd: (hd, 0, 0)),
        compiler_params=_params("parallel"), name="a_bias_tiles")(rel_bias, buckets, inband)
    base = base.reshape(len(DILATIONS), HEADS_A, QB_A, wide)
    return jnp.stack([base[..., WIN_SHIFTS[-1] - s:WIN_SHIFTS[-1] - s + WIN_A] for s in WIN_SHIFTS], axis=1)


def _bias_grad(dbias):
    steps = _band_steps(np)
    inband = np.abs(steps) <= BAND_HALF
    present = []
    for d in DILATIONS:
        rel = steps * d
        a = np.abs(rel)
        large = 8 + (np.log(np.maximum(a, 1) / 8.0) / math.log(MAX_DISTANCE / 8.0) * 8).astype(np.int64)
        bk = np.where(rel > 0, 16, 0) + np.where(a < 8, a, np.minimum(large, 15))
        present.append([sorted(set(bk[v][inband[v]].tolist())) for v in range(3)])
    buckets = jnp.stack([_t5_bucket(_band_steps() * d) for d in DILATIONS])
    n_heads = len(DILATIONS) * HEADS_A

    def body(b_ref, d_ref, o_ref):
        row = lax.broadcasted_iota(jnp.int32, (N_BUCKETS, n_heads), 0)
        col = lax.broadcasted_iota(jnp.int32, (N_BUCKETS, n_heads), 1)
        out = jnp.zeros((N_BUCKETS, n_heads), F32)
        for grp in range(len(DILATIONS)):
            for hh in range(HEADS_A):
                hd = grp * HEADS_A + hh
                for b in sorted(set(sum(present[grp], []))):
                    tot = jnp.zeros((), F32)
                    for v in range(3):
                        if b in present[grp][v]:
                            tot = tot + jnp.sum(jnp.where(b_ref[grp, v] == b, d_ref[grp, v, hh], 0.0))
                    out = jnp.where((row == b) & (col == hd), tot, out)
        o_ref[...] = out

    return pl.pallas_call(
        body, out_shape=jax.ShapeDtypeStruct((N_BUCKETS, n_heads), F32),
        compiler_params=pltpu.CompilerParams(vmem_limit_bytes=VMEM_LIMIT_BYTES), name="a_bias_grad")(buckets, dbias)


def _lane_is_second_head(shape):
    return lax.broadcasted_iota(jnp.int32, shape, len(shape) - 1) >= HEAD_A


VIEW_ROWS = 512


def _view_chunks():
    return [pltpu.VMEM((VIEW_ROWS, LANES), F32)] * (WIDTH_A // LANES)


def _rows_to_view(x_ref, col, o_ref, ocol, d, chunks):
    n = VIEW_ROWS // d
    for c, scr in enumerate(chunks):
        scr[...] = x_ref[:, col + c * LANES:col + (c + 1) * LANES].astype(F32)
        for r in range(d):
            at = ocol + r * WIDTH_A + c * LANES
            o_ref[:, at:at + LANES] = scr[pl.ds(r, n, stride=d), :].astype(o_ref.dtype)


def _view_to_rows(v_ref, o_ref, col, d, chunks):
    n = VIEW_ROWS // d
    for c, scr in enumerate(chunks):
        if d == 1:
            o_ref[:, col + c * LANES:col + (c + 1) * LANES] = v_ref[:, c * LANES:(c + 1) * LANES].astype(o_ref.dtype)
            continue
        for r in range(d):
            scr[pl.ds(r, n, stride=d), :] = v_ref[:, r * WIDTH_A + c * LANES:r * WIDTH_A + (c + 1) * LANES].astype(F32)
        o_ref[:, col + c * LANES:col + (c + 1) * LANES] = scr[...].astype(o_ref.dtype)


def _group_view(proj, grp, d):
    T = proj.shape[0]
    if d == 1:
        return proj, (lambda part, r: grp * 3 + part)

    def body(x_ref, o_ref, *chunks):
        for part in range(3):
            _rows_to_view(x_ref, part * WIDTH_A, o_ref, part * d * WIDTH_A, d, chunks)

    view = pl.pallas_call(
        body, out_shape=jax.ShapeDtypeStruct((T // d, 3 * d * WIDTH_A), proj.dtype), grid=(T // VIEW_ROWS,),
        in_specs=[_bs((VIEW_ROWS, 3 * WIDTH_A), lambda i: (i, grp))],
        out_specs=_bs((VIEW_ROWS // d, 3 * d * WIDTH_A), lambda i: (i, 0)),
        scratch_shapes=_view_chunks(), compiler_params=_params("parallel"), name=f"a_view_d{d}")(proj)
    return view, (lambda part, r: part * d + r)


def _stack_heads(v2, second):
    zero = jnp.zeros_like(v2)
    return jnp.concatenate([jnp.where(second, zero, v2), jnp.where(second, v2, zero)], axis=0)


def _unstack_heads(v, second):
    return jnp.where(second, v[QB_A:], v[:QB_A])


def _dil_fwd(view, bias, d):
    pv, colblk = view
    L = pv.shape[0]
    nblk = L // QB_A
    W2 = 2 * HEAD_A
    scale = HEAD_A ** -0.5

    def body(q_ref, k_ref, v_ref, b_ref, o_ref, l_ref):
        win = pl.ds(_window_start(pl.program_id(1), nblk), WIN_A)
        second = _lane_is_second_head((QB_A, W2))
        pairs = range(HEADS_A // 2)
        cols = [slice(hp * W2, (hp + 1) * W2) for hp in pairs]
        s = [lax.dot_general(_stack_heads(q_ref[:, cols[hp]], second), k_ref[win, cols[hp]], (NT, ((), ())),
                             preferred_element_type=F32) * scale + b_ref[2 * hp:2 * hp + 2].reshape(2 * QB_A, WIN_A)
             for hp in pairs]
        m = [jnp.max(x, axis=-1, keepdims=True) for x in s]
        p = [jnp.exp(x - mx) for x, mx in zip(s, m)]
        l = [jnp.sum(x, axis=-1, keepdims=True) for x in p]
        res = [jnp.dot(p[hp].astype(BF16), v_ref[win, cols[hp]], preferred_element_type=F32) / l[hp] for hp in pairs]
        o_ref[...] = jnp.concatenate([_unstack_heads(x, second) for x in res], axis=1).astype(o_ref.dtype)
        l_ref[...] = jnp.concatenate([_unstack_heads(jnp.broadcast_to(mx + jnp.log(lx), (2 * QB_A, W2)), second)
                                      for mx, lx in zip(m, l)], axis=1)

    in_specs = [_bs((QB_A, WIDTH_A), lambda r, n: (n, colblk(0, r))),
                _bs((L, WIDTH_A), lambda r, n: (0, colblk(1, r))), _bs((L, WIDTH_A), lambda r, n: (0, colblk(2, r))),
                _bs((None, HEADS_A, QB_A, WIN_A), lambda r, n: (_window_variant(n, nblk), 0, 0, 0))]
    o, lse = pl.pallas_call(
        body, out_shape=[jax.ShapeDtypeStruct((L, d * WIDTH_A), BF16), jax.ShapeDtypeStruct((L, d * WIDTH_A), F32)],
        grid=(d, nblk), in_specs=in_specs,
        out_specs=[_bs((QB_A, WIDTH_A), lambda r, n: (n, r)), _bs((QB_A, WIDTH_A), lambda r, n: (n, r))],
        compiler_params=_params("parallel", "parallel"), name=f"a_fwd_d{d}")(pv, pv, pv, bias)
    return o, lse


def _dil_bwd(view_qkv, bias, do, lse, cterm, d):
    pv, colblk = view_qkv
    L = pv.shape[0]
    nblk = L // QB_A
    W2 = 2 * HEAD_A
    PPS = 4
    WS = PPS * W2
    ob = WIDTH_A // WS
    scale = HEAD_A ** -0.5

    def body(q_ref, k_ref, v_ref, do_ref, l_ref, c_ref, b_ref, dq_ref, dk_ref, dv_ref, db_ref):
        r, n = pl.program_id(1), pl.program_id(2)

        @pl.when(n == 0)
        def _():
            dk_ref[...] = jnp.zeros_like(dk_ref)
            dv_ref[...] = jnp.zeros_like(dv_ref)

        @pl.when((n == 0) & (r == 0))
        def _():
            db_ref[...] = jnp.zeros_like(db_ref)

        second = _lane_is_second_head((QB_A, W2))
        win = pl.ds(_window_start(n, nblk), WIN_A)
        variant = _window_variant(n, nblk)
        pairs = range(PPS)
        cols = [slice(pp * W2, (pp + 1) * W2) for pp in pairs]

        def head_rows(ref, pp):
            v2 = ref[:, cols[pp]]
            return jnp.concatenate([v2[:, 0:1], v2[:, HEAD_A:HEAD_A + 1]], axis=0)

        kw = [k_ref[win, c] for c in cols]
        vw = [v_ref[win, c] for c in cols]
        qs = [_stack_heads(q_ref[:, c], second) for c in cols]
        dos = [_stack_heads(do_ref[:, c], second) for c in cols]
        s = [lax.dot_general(qs[pp], kw[pp], (NT, ((), ())), preferred_element_type=F32) for pp in pairs]
        dp = [lax.dot_general(dos[pp], vw[pp], (NT, ((), ())), preferred_element_type=F32) for pp in pairs]
        p = [jnp.exp(s[pp] * scale + b_ref[2 * pp:2 * pp + 2].reshape(2 * QB_A, WIN_A) - head_rows(l_ref, pp)) for pp in pairs]
        ds = [p[pp] * (dp[pp] + head_rows(c_ref, pp)) for pp in pairs]
        db_ref[variant] += jnp.concatenate([x.reshape(2, QB_A, WIN_A) for x in ds], axis=0)
        pb = [x.astype(BF16) for x in p]
        dsb = [(x * scale).astype(BF16) for x in ds]
        dq_ref[...] = jnp.concatenate([_unstack_heads(jnp.dot(dsb[pp], kw[pp], preferred_element_type=F32), second)
                                       for pp in pairs], axis=1).astype(dq_ref.dtype)
        dk_ref[win, :] += jnp.concatenate([lax.dot_general(dsb[pp], qs[pp], (TN, ((), ())), preferred_element_type=F32)
                                           for pp in pairs], axis=1)
        dv_ref[win, :] += jnp.concatenate([lax.dot_general(pb[pp], dos[pp], (TN, ((), ())), preferred_element_type=F32)
                                           for pp in pairs], axis=1)

    kv_spec = _resident if d == 1 else _bs
    in_specs = [_bs((QB_A, WS), lambda hp, r, n: (n, colblk(0, r) * ob + hp)),
                kv_spec((L, WS), lambda hp, r, n: (0, colblk(1, r) * ob + hp)),
                kv_spec((L, WS), lambda hp, r, n: (0, colblk(2, r) * ob + hp))]
    in_specs += [_bs((QB_A, WS), lambda hp, r, n: (n, r * ob + hp))] * 3
    in_specs += [_bs((None, 2 * PPS, QB_A, WIN_A), lambda hp, r, n: (_window_variant(n, nblk), hp, 0, 0))]
    out_shape = [jax.ShapeDtypeStruct((L, d * WIDTH_A), BF16), jax.ShapeDtypeStruct((L, d * WIDTH_A), F32),
                 jax.ShapeDtypeStruct((L, d * WIDTH_A), F32), jax.ShapeDtypeStruct((3, HEADS_A, QB_A, WIN_A), F32)]
    out_specs = [_bs((QB_A, WS), lambda hp, r, n: (n, r * ob + hp)),
                 _bs((L, WS), lambda hp, r, n: (0, r * ob + hp)), _bs((L, WS), lambda hp, r, n: (0, r * ob + hp)),
                 _bs((3, 2 * PPS, QB_A, WIN_A), lambda hp, r, n: (0, hp, 0, 0))]
    dq, dk, dv, db = pl.pallas_call(
        body, out_shape=out_shape, grid=(ob, d, nblk), in_specs=in_specs, out_specs=out_specs,
        compiler_params=_params("arbitrary", "arbitrary", "arbitrary"), name=f"a_bwd_d{d}")(
            pv, pv, pv, do, lse, cterm, bias)
    return dq, dk, dv, db


def _assemble_dproj(a_parts, dq_b, dk_b, dv_b, dga, dgb):
    T = dq_b.shape[0]
    flat = [(a_parts[part][g], d) for part in range(3) for g, d in enumerate(DILATIONS)]
    rest = [dq_b, dk_b, dv_b, dga, dgb]

    def body(*refs):
        views, others = refs[:len(flat)], refs[len(flat):len(flat) + len(rest)]
        o_ref, chunks = refs[len(flat) + len(rest)], refs[len(flat) + len(rest) + 1:]
        col = 0
        for v_ref, (_, d) in zip(views, flat):
            _view_to_rows(v_ref, o_ref, col, d, chunks)
            col += WIDTH_A
        for x_ref in others:
            w = x_ref.shape[1]
            o_ref[:, col:col + w] = x_ref[...].astype(o_ref.dtype)
            col += w

    in_specs = [_bs((VIEW_ROWS // d, d * WIDTH_A), lambda i: (i, 0)) for _, d in flat]
    in_specs += [_bs((VIEW_ROWS, x.shape[1]), lambda i: (i, 0)) for x in rest]
    return pl.pallas_call(
        body, out_shape=jax.ShapeDtypeStruct((T, IN_WIDTH), BF16), grid=(T // VIEW_ROWS,), in_specs=in_specs,
        out_specs=_bs((VIEW_ROWS, IN_WIDTH), lambda i: (i, 0)), scratch_shapes=_view_chunks(),
        compiler_params=_params("parallel"), name="mix_bwd_dproj")(*[a for a, _ in flat], *rest)


def _segment_ones():
    i = np.arange(WIDTH_A)
    return jnp.asarray((i[:, None] // HEAD_A == i[None, :] // HEAD_A).astype(np.float32), dtype=BF16)


def _group_weights(l0, l1, l2):
    m = jnp.maximum(jnp.maximum(l0, l1), l2)
    e = [jnp.exp(l - m) for l in (l0, l1, l2)]
    z = e[0] + e[1] + e[2]
    return [ei / z for ei in e]


def _view_specs():
    return [_bs((VIEW_ROWS // d, d * WIDTH_A), lambda i: (i, 0)) for d in DILATIONS]


def _stage_tiles(n):
    return [pltpu.VMEM((VIEW_ROWS, WIDTH_A), F32)] * n


def _token_rows(v_ref, stage, d, chunks):
    if d == 1:
        return v_ref[...].astype(F32)
    _view_to_rows(v_ref, stage, 0, d, chunks)
    return stage[...]


def _combine_fwd(outs, lses):
    T = outs[0].shape[0] * DILATIONS[0]
    n = len(DILATIONS)

    def body(*refs):
        o_refs, l_refs, oa_ref = refs[:n], refs[n:2 * n], refs[2 * n]
        o_st, l_st, chunks = refs[2 * n + 1:3 * n + 1], refs[3 * n + 1:4 * n + 1], refs[4 * n + 1:]
        o = [_token_rows(o_refs[g], o_st[g], d, chunks) for g, d in enumerate(DILATIONS)]
        w = _group_weights(*[_token_rows(l_refs[g], l_st[g], d, chunks) for g, d in enumerate(DILATIONS)])
        oa_ref[...] = (w[0] * o[0] + w[1] * o[1] + w[2] * o[2]).astype(oa_ref.dtype)

    return pl.pallas_call(
        body, out_shape=jax.ShapeDtypeStruct((T, WIDTH_A), BF16), grid=(T // VIEW_ROWS,),
        in_specs=_view_specs() * 2, out_specs=_bs((VIEW_ROWS, WIDTH_A), lambda i: (i, 0)),
        scratch_shapes=_stage_tiles(2 * n) + _view_chunks(), compiler_params=_params("parallel"), name="a_combine")(*outs, *lses)


def _combine_bwd(doa, outs, lses):
    T = doa.shape[0]
    n = len(DILATIONS)

    def body(*refs):
        d_ref, o_refs, l_refs, seg_ref = refs[0], refs[1:n + 1], refs[n + 1:2 * n + 1], refs[2 * n + 1]
        do_refs, c_refs = refs[2 * n + 2:3 * n + 2], refs[3 * n + 2:4 * n + 2]
        o_st, l_st = refs[4 * n + 2:5 * n + 2], refs[5 * n + 2:6 * n + 2]
        tmp, chunks = refs[6 * n + 2], refs[6 * n + 3:]
        o = [_token_rows(o_refs[g], o_st[g], d, chunks) for g, d in enumerate(DILATIONS)]
        w = _group_weights(*[_token_rows(l_refs[g], l_st[g], d, chunks) for g, d in enumerate(DILATIONS)])
        dv = d_ref[...].astype(F32)
        seg = seg_ref[...]
        tot = jnp.zeros(dv.shape, F32)
        for g in range(n):
            prod = w[g] * dv * o[g]
            hi = prod.astype(BF16)
            lo = (prod - hi.astype(F32)).astype(BF16)
            tot = tot + jnp.dot(hi, seg, preferred_element_type=F32) + jnp.dot(lo, seg, preferred_element_type=F32)
        for g, d in enumerate(DILATIONS):
            for ref, val in ((do_refs[g], w[g] * dv), (c_refs[g], -w[g] * tot)):
                if d == 1:
                    ref[...] = val.astype(ref.dtype)
                else:
                    tmp[...] = val
                    _rows_to_view(tmp, 0, ref, 0, d, chunks)

    views = [jax.ShapeDtypeStruct((T // d, d * WIDTH_A), dt) for dt in (BF16, F32) for d in DILATIONS]
    res = pl.pallas_call(
        body, out_shape=views, grid=(T // VIEW_ROWS,),
        in_specs=[_bs((VIEW_ROWS, WIDTH_A), lambda i: (i, 0))] + _view_specs() * 2 + [_bs((WIDTH_A, WIDTH_A), lambda i: (0, 0))],
        out_specs=_view_specs() * 2, scratch_shapes=_stage_tiles(2 * n + 1) + _view_chunks(),
        compiler_params=_params("parallel"), name="a_combine_bwd")(doa, *outs, *lses, _segment_ones())
    return res[:n], res[n:]


def _rope_tables(T):
    rows = T // GRID_W
    row = jnp.repeat(jnp.arange(rows, dtype=F32), GRID_W)
    col = jnp.tile(jnp.arange(GRID_W, dtype=F32), rows)
    n_freq = HEAD_B // 4
    freq = ROPE_THETA ** (-jnp.arange(n_freq, dtype=F32) / n_freq)
    ang = jnp.concatenate([row[:, None] * freq, col[:, None] * freq], axis=-1)
    cos, sin = jnp.repeat(jnp.cos(ang), 2, axis=1), jnp.repeat(jnp.sin(ang), 2, axis=1)
    sign = jnp.where(jnp.arange(HEAD_B) % 2 == 0, -1.0, 1.0).astype(F32)
    return cos, sin * sign


def _swap_pairs(v):
    even = lax.broadcasted_iota(jnp.int32, v.shape, v.ndim - 1) % 2 == 0
    n = v.shape[-1]
    return jnp.where(even, pltpu.roll(v, n - 1, v.ndim - 1), pltpu.roll(v, 1, v.ndim - 1))


def _qk_fwd(name, proj, col0, n_heads, gain, cos, sin, out_scale=1.0, deps=()):
    T = proj.shape[0]

    def fn(xr, g, c, s):
        xn = _norm_fwd(xr.astype(F32), g)
        return (xn * c + _swap_pairs(xn) * s) * out_scale

    (out,) = _ew(name, fn, [_tiled(proj, HEAD_B, col0 // HEAD_B), _whole(gain), _table(cos), _table(sin)],
                 [(BF16, HEAD_B)], n_rows=T, rows=2048, ncols=n_heads, deps=deps)
    return out


def _qk_bwd(name, dout, proj, col0, n_heads, gain, cos, sin, in_scale=1.0):
    T = proj.shape[0]

    def fn(dv, xr, g, c, s):
        dv = dv.astype(F32) * in_scale
        dxn = c * dv + _swap_pairs(s * dv)
        dx, dgr = _norm_bwd(xr.astype(F32), g, dxn)
        return dx, _colsum(dgr)

    dx, dg = _ew(name, fn, [_tiled(dout, HEAD_B, 0), _tiled(proj, HEAD_B, col0 // HEAD_B), _whole(gain),
                            _table(cos), _table(sin)],
                 [(BF16, HEAD_B)], n_rows=T, rows=2048, reds=(HEAD_B,), ncols=n_heads)
    return dx, jnp.sum(dg, axis=0)


def _gqa_fwd(qn, kn, proj, k_col=0):
    T = qn.shape[0]
    GW = 4 * HEAD_B
    QB = QB_B

    def body(q_ref, k_ref, v_ref, o_ref, l_ref):
        k = k_ref[...]
        v_ones = jnp.concatenate([v_ref[...], jnp.ones((T, HEAD_B), BF16)], axis=1)
        lane = lax.broadcasted_iota(jnp.int32, (QB, HEAD_B), 1)
        heads = range(4)
        s = [lax.dot_general(q_ref[:, g * HEAD_B:(g + 1) * HEAD_B], k, (NT, ((), ())), preferred_element_type=F32)
             for g in heads]
        m = [jnp.max(x, axis=-1, keepdims=True) for x in s]
        pv = [jnp.dot(jnp.exp2(x - mx).astype(BF16), v_ones, preferred_element_type=F32) for x, mx in zip(s, m)]
        l = [x[:, HEAD_B:HEAD_B + 1] for x in pv]
        o = [x[:, :HEAD_B] / lx for x, lx in zip(pv, l)]
        o_ref[...] = jnp.concatenate(o, axis=1).astype(o_ref.dtype)
        lse_all = jnp.zeros((QB, HEAD_B), F32)
        for g in heads:
            lse_all = jnp.where(lane == g, m[g] + jnp.log2(l[g]), lse_all)
        l_ref[...] = lse_all

    return pl.pallas_call(
        body, out_shape=[jax.ShapeDtypeStruct((T, 2 * GW), BF16), jax.ShapeDtypeStruct((2, T, HEAD_B), F32)],
        grid=(2, T // QB),
        in_specs=[_bs((QB, GW), lambda kv, i: (i, kv)), _bs((T, HEAD_B), lambda kv, i: (0, k_col + kv)),
                  _bs((T, HEAD_B), lambda kv, i: (0, B_V // HEAD_B + kv))],
        out_specs=[_bs((QB, GW), lambda kv, i: (i, kv)), _bs((None, QB, HEAD_B), lambda kv, i: (kv, i, 0))],
        compiler_params=_params("parallel", "parallel"), name="b_fwd")(qn, kn, proj)


def _gqa_bwd(qn, kn, proj, o, lse, do, deps=(), k_col=0):
    T = qn.shape[0]
    GW = 4 * HEAD_B

    def body(q_ref, k_ref, v_ref, o_ref, l_ref, do_ref, *rest):
        dq_ref, dk_ref, dv_ref = rest[-3:]
        i = pl.program_id(1)

        @pl.when(i == 0)
        def _():
            dk_ref[...] = jnp.zeros_like(dk_ref)
            dv_ref[...] = jnp.zeros_like(dv_ref)

        k, v = k_ref[...], v_ref[...]
        lse_all = l_ref[...]
        for g in range(4):
            cols = slice(g * HEAD_B, (g + 1) * HEAD_B)
            q, dob = q_ref[:, cols], do_ref[:, cols]
            delta = jnp.sum(dob.astype(F32) * o_ref[:, cols].astype(F32), axis=-1, keepdims=True)
            s = lax.dot_general(q, k, (NT, ((), ())), preferred_element_type=F32)
            p = jnp.exp2(s - lse_all[:, g:g + 1])
            dp = lax.dot_general(dob, v, (NT, ((), ())), preferred_element_type=F32)
            ds = (p * (dp - delta)).astype(BF16)
            dq_ref[:, cols] = jnp.dot(ds, k, preferred_element_type=F32).astype(dq_ref.dtype)
            dk_ref[...] += lax.dot_general(ds, q, (TN, ((), ())), preferred_element_type=F32)
            dv_ref[...] += lax.dot_general(p.astype(BF16), dob, (TN, ((), ())), preferred_element_type=F32)

    return pl.pallas_call(
        body, out_shape=[jax.ShapeDtypeStruct((T, 2 * GW), BF16), jax.ShapeDtypeStruct((T, 2 * HEAD_B), F32),
                         jax.ShapeDtypeStruct((T, 2 * HEAD_B), F32)],
        grid=(2, T // QB_B),
        in_specs=[_bs((QB_B, GW), lambda kv, i: (i, kv)), _bs((T, HEAD_B), lambda kv, i: (0, k_col + kv)),
                  _bs((T, HEAD_B), lambda kv, i: (0, B_V // HEAD_B + kv)), _bs((QB_B, GW), lambda kv, i: (i, kv)),
                  _bs((None, QB_B, HEAD_B), lambda kv, i: (kv, i, 0)), _bs((QB_B, GW), lambda kv, i: (i, kv))] + _any_specs(len(deps)),
        out_specs=[_bs((QB_B, GW), lambda kv, i: (i, kv)), _bs((T, HEAD_B), lambda kv, i: (0, kv)),
                   _bs((T, HEAD_B), lambda kv, i: (0, kv))],
        compiler_params=_params("parallel", "arbitrary"), name="b_bwd")(qn, kn, proj, o, lse, do, *deps)


def _local_step(x, target, small, get_w, put_g, deps=(), prefetch_w=lambda name, after: [], take_rider=lambda steps, after: None):
    T, D = x.shape
    gs = {}

    bias = _bias_tiles(small["rel_bias"])
    cos, sin = _rope_tables(T)
    (x1, h2), ffn1_saved = _ffn_fwd("ffn1", x, small["ffn1_norm"], lambda name, after: get_w(name, [after, bias, cos, sin]), deps,
                                    tail_ins=[small["mix_norm"]], tail_fn=lambda y, g: (y, _norm_fwd(y, g)), tail_outs=(F32, BF16))
    w_in = get_w("w_in", h2)
    nq = w_in.shape[2]
    tpq = nq // WIDTH_A

    def proj_tile(j, k):
        c = j * tpq + k
        return jnp.where(c < 3 * len(DILATIONS), (c % 3) * 3 + c // 3, c)

    proj = _mm("mix_in", (4, tpq),
               [(h2, _resident((T, D), lambda j, k: (0, 0)), w_in, _bs((None, D, WIDTH_A), lambda j, k: (j, 0, k)))],
               jax.ShapeDtypeStruct((T, IN_WIDTH), BF16), _bs((T, WIDTH_A), lambda j, k: (0, proj_tile(j, k))), NN)

    a_views = [_group_view(proj, grp, d) for grp, d in enumerate(DILATIONS)]
    a_outs, a_lses = [], []
    for grp, d in enumerate(DILATIONS):
        o, l = _dil_fwd(a_views[grp], bias[grp], d)
        a_outs.append(o)
        a_lses.append(l)
    o_a = _combine_fwd(a_outs, a_lses)

    qk_gain = jnp.concatenate([jnp.tile(small["q_norm"] * QK_SCALE_LOG2, (8, 1)), jnp.tile(small["k_norm"], (2, 1))])[:, None, :]
    qkn = _qk_fwd("b_qknorm", proj, B_Q, 10, qk_gain, cos, sin, deps=prefetch_w("w_branch_a", proj))
    qn, kn, k_col = qkn, qkn, 8
    o_b, lse_b = _gqa_fwd(qn, kn, proj, k_col)
    ahead = prefetch_w("ffn2_w1", o_b)

    wa, wb, wo = get_w("w_branch_a", o_b), get_w("w_branch_b", o_b), get_w("w_out", o_b)
    bg_a, bg_b = small["b_gate"][:, :D], small["b_gate"][:, D:]
    n_a = wa.shape[0]

    def merge_out(oa_ref, ob_ref, ga_ref, gb_ref, x1_ref, wa_ref, wb_ref, wo_ref, ba_ref, bb_ref, g2_ref, *rest):
        ta_ref, tb_ref, mg_ref, x2_ref, hn_ref = rest[-5:]
        oa = oa_ref[...]
        ta = jnp.concatenate([jnp.dot(oa, wa_ref[j], preferred_element_type=F32) for j in range(n_a)], axis=1)
        tb = jnp.dot(ob_ref[...], wb_ref[...], preferred_element_type=F32)
        sa = _sigmoid(ga_ref[...].astype(F32) + ba_ref[...])
        sb = _sigmoid(gb_ref[...].astype(F32) + bb_ref[...])
        merged = (sa * ta + sb * tb).astype(BF16)
        ta_ref[...], tb_ref[...], mg_ref[...] = ta.astype(BF16), tb.astype(BF16), merged
        y = x1_ref[...] + jnp.dot(merged, wo_ref[...], preferred_element_type=F32)
        x2_ref[...] = y
        hn_ref[...] = _norm_fwd(y, g2_ref[...]).astype(BF16)

    row = _bs((512, D), lambda i: (i, 0))
    gate_specs = [_bs((512, D), lambda i: (i, G_A // D)), _bs((512, D), lambda i: (i, G_B // D))]
    whole2, whole3 = (lambda i: (0, 0)), (lambda i: (0, 0, 0))
    vec = _bs((1, D), whole2)
    t_a, t_b, merged, x2, hn2 = pl.pallas_call(
        merge_out, out_shape=[jax.ShapeDtypeStruct((T, D), BF16)] * 3 + [jax.ShapeDtypeStruct((T, D), F32), jax.ShapeDtypeStruct((T, D), BF16)],
        grid=(T // 512,),
        in_specs=[_bs((512, WIDTH_A), lambda i: (i, 0)), row] + gate_specs + [row, _resident(wa.shape, whole3), _resident((D, D), whole2),
                                                                                _resident((D, D), whole2), vec, vec, vec]
        + _any_specs(len(ahead)),
        out_specs=[row] * 5, compiler_params=_params("parallel"), name="mix_merge_out")(
            o_a, o_b, proj, proj, x1, wa, wb, wo, bg_a, bg_b, small["ffn2_norm"], *ahead)

    def head(xv, g, tv):
        r = _rstd(xv)
        xh = xv * r
        e = xh * g - tv
        dy = e * (1.0 / D)
        dxh = dy * g
        dx = r * (dxh - xh * jnp.mean(dxh * xh, axis=-1, keepdims=True))
        return dx, 0.5 * dx, _colsum(e * e) * (0.5 / D), _colsum(dy * xh)

    (dx3, dx3_half, loss_cols, g_final), ffn2_saved = _ffn_fwd(
        "ffn2", x2, small["ffn2_norm"], get_w, h=hn2, tail_ins=[small["final_norm"].reshape(1, D), target], tail_fn=head,
        tail_outs=(F32, BF16), tail_reds=(D, D))
    gs["final_norm"] = g_final.reshape(D)

    dx2, _, dmix, gs["ffn2_norm"] = _ffn_bwd("ffn2", x2, small["ffn2_norm"], get_w, put_g, ffn2_saved, dx3, dx3_half,
                                             also_bf16=True)
    g_out = _mm_wgrad("mix_bwd_dwout", merged, dmix, a_cols=D // 4, b_cols=None, tm=256, tn=512, J=4).reshape(D, D)

    def merge_out_bwd(dx_ref, ta_ref, tb_ref, ga_ref, gb_ref, wa_ref, wb_ref, wo_ref, ba_ref, bb_ref,
                      dta_ref, dtb_ref, dga_ref, dgb_ref, doa_ref, dob_ref, dba_ref, dbb_ref):
        dm = lax.dot_general(dx_ref[...], wo_ref[...], (NT, ((), ())), preferred_element_type=F32)
        ta, tb = ta_ref[...].astype(F32), tb_ref[...].astype(F32)
        sa = _sigmoid(ga_ref[...].astype(F32) + ba_ref[...])
        sb = _sigmoid(gb_ref[...].astype(F32) + bb_ref[...])
        dga, dgb = dm * ta * sa * (1.0 - sa), dm * tb * sb * (1.0 - sb)
        dta, dtb = (dm * sa).astype(BF16), (dm * sb).astype(BF16)
        dta_ref[...], dtb_ref[...] = dta, dtb
        dga_ref[...], dgb_ref[...] = dga.astype(BF16), dgb.astype(BF16)
        w = wa_ref.shape[2]
        doa = sum(lax.dot_general(dta[:, j * w:(j + 1) * w], wa_ref[j], (NT, ((), ())), preferred_element_type=F32) for j in range(n_a))
        doa_ref[...] = doa.astype(BF16)
        dob_ref[...] = lax.dot_general(dtb, wb_ref[...], (NT, ((), ())), preferred_element_type=F32).astype(BF16)

        @pl.when(pl.program_id(0) == 0)
        def _():
            dba_ref[...] = jnp.zeros_like(dba_ref)
            dbb_ref[...] = jnp.zeros_like(dbb_ref)
        dba_ref[...] += _colsum(dga)
        dbb_ref[...] += _colsum(dgb)

    rowb = _bs((256, D), lambda i: (i, 0))
    gate_specs = [_bs((256, D), lambda i: (i, G_A // D)), _bs((256, D), lambda i: (i, G_B // D))]
    dta, dtb, dga, dgb, do_a, do_b, dba, dbb = pl.pallas_call(
        merge_out_bwd,
        out_shape=[jax.ShapeDtypeStruct((T, D), BF16)] * 4 + [jax.ShapeDtypeStruct((T, WIDTH_A), BF16), jax.ShapeDtypeStruct((T, D), BF16)]
        + [jax.ShapeDtypeStruct((1, D), F32)] * 2,
        grid=(T // 256,),
        in_specs=[rowb, rowb, rowb] + gate_specs + [_resident(wa.shape, whole3), _resident((D, D), whole2), _resident((D, D), whole2), vec, vec],
        out_specs=[rowb] * 4 + [_bs((256, WIDTH_A), lambda i: (i, 0)), rowb, vec, vec],
        compiler_params=_params("arbitrary"), name="mix_merge_out_bwd")(dmix, t_a, t_b, proj, proj, wa, wb, wo, bg_a, bg_b)
    gs["b_gate"] = jnp.concatenate([dba, dbb], axis=1)

    g_a = _mm_wgrad("mix_bwd_dwa", o_a, dta, a_cols=None, b_cols=D // 4, tm=WIDTH_A, tn=256, J=4)
    g_b = _mm_wgrad("mix_bwd_dwb", o_b, dtb, a_cols=D // 4, b_cols=None, tm=256, tn=512, J=4).reshape(D, D)
    deps = put_g({"w_out": g_out, "w_branch_a": g_a, "w_branch_b": g_b})

    dqn, dkn, dv_b = _gqa_bwd(qn, kn, proj, o_b, lse_b, do_b, deps, k_col)
    dq_b, gs["q_norm"] = _qk_bwd("b_bwd_qnorm", dqn, proj, B_Q, 8, small["q_norm"], cos, sin, in_scale=HEAD_B ** -0.5)
    dk_b, gs["k_norm"] = _qk_bwd("b_bwd_knorm", dkn, proj, B_K, 2, small["k_norm"], cos, sin, in_scale=1.0 / LOG2_E)

    do_groups, c_groups = _combine_bwd(do_a, a_outs, a_lses)
    dqs, dks, dvs, dbs = [], [], [], []
    for grp, d in enumerate(DILATIONS):
        dq, dk, dv, db = _dil_bwd(a_views[grp], bias[grp], do_groups[grp], a_lses[grp], c_groups[grp], d)
        dqs.append(dq), dks.append(dk), dvs.append(dv), dbs.append(db)
    gs["rel_bias"] = _bias_grad(jnp.stack(dbs))

    dproj = _assemble_dproj([dqs, dks, dvs], dq_b, dk_b, dv_b, dga, dgb)
    nq = w_in.shape[2]
    g_in = _mm("mix_bwd_dwin", (4, tpq),
               [(h2, _resident((T, D), lambda j, k: (0, 0)), dproj, _bs((T, WIDTH_A), lambda j, k: (0, j * tpq + k)))],
               jax.ShapeDtypeStruct((4, D, nq), BF16), _bs((None, D, WIDTH_A), lambda j, k: (j, 0, k)), TN)
    deps = put_g({"w_in": g_in})
    dx1, dx1_half, gs["mix_norm"] = _dh_norm_bwd(
        "mix_bwd_dh", 512,
        [(dproj, _bs((512, nq), lambda i, j=j: (i, j)), w_in, _resident((None, D, nq), lambda i, j=j: (j, 0, 0))) for j in range(4)],
        NT, x1, small["mix_norm"], dx2, deps)

    dx0, _, gs["ffn1_norm"] = _ffn_bwd("ffn1", x, small["ffn1_norm"], get_w, put_g, ffn1_saved, dx1, dx1_half, last=True,
                                       take_rider=take_rider)
    return loss_cols, dx0, gs


def _position():
    return lax.axis_index("x"), lax.axis_index("y"), lax.axis_index("c")


def _any_specs(n):
    return [pl.BlockSpec(memory_space=pl.ANY)] * n


HBM_SPEC = pl.BlockSpec(memory_space=pltpu.HBM)
SEM_SPEC = pl.BlockSpec(memory_space=pltpu.SEMAPHORE)
DATAFLOW_EFFECT = pltpu.SideEffectType.DATAFLOW_SIDE_EFFECTING
N_PEER_CHIPS = 3
LANES = 128


def _quarter_copies(srcs, lands, send_sems, recv_sems, mode):
    x, y, c = _position()
    me = 2 * x + y
    peers = [(1 - x, y, c), (x, 1 - y, c), (1 - x, 1 - y, c)]
    copies = []
    for src, land, send, recv in zip(srcs, lands, send_sems, recv_sems):
        if mode == "sibling":
            copies.append(pltpu.make_async_remote_copy(src_ref=src, dst_ref=land, send_sem=send.at[0], recv_sem=recv.at[0],
                                                       device_id=(x, y, 1 - c), device_id_type=MESH))
            continue
        if mode == "fill":
            half = land.shape[1] // 2
            for p, (px, py, _) in enumerate(peers):
                part = land.at[2 * px + py, pl.ds(c * half, half)]
                copies.append(pltpu.make_async_remote_copy(src_ref=part, dst_ref=part, send_sem=send.at[p], recv_sem=recv.at[p],
                                                           device_id=(x, y, 1 - c), device_id_type=MESH))
            continue
        scatter = mode == "scatter"
        half = land.shape[1] // 2
        mine = land.at[me, pl.ds(c * half, half)]
        for p, (px, py, pc) in enumerate(peers):
            copies.append(pltpu.make_async_remote_copy(
                src_ref=src.at[2 * px + py] if scatter else mine, dst_ref=land.at[me] if scatter else mine,
                send_sem=send.at[p], recv_sem=recv.at[p], device_id=(px, py, pc), device_id_type=MESH))
    return copies


def _fill_from_sibling(name, stacks):
    n = len(stacks)

    def body(*refs):
        outs = refs[n:2 * n]
        send_sems, recv_sems = refs[2 * n:]
        x, y, c = _position()
        copies = []
        for i, ref in enumerate(outs):
            half = ref.shape[1] // 2
            rows = pl.ds(c * half, half)
            for p, k in enumerate((2 * (1 - x) + y, 2 * x + (1 - y), 2 * (1 - x) + (1 - y))):
                cp = pltpu.make_async_remote_copy(ref.at[k, rows], ref.at[k, rows], send_sems.at[3 * i + p], recv_sems.at[3 * i + p],
                                                  device_id=(x, y, 1 - c), device_id_type=MESH)
                cp.start()
                copies.append(cp)
        for cp in copies:
            cp.wait()

    return pl.pallas_call(
        body, out_shape=[jax.ShapeDtypeStruct(s.shape, s.dtype) for s in stacks],
        in_specs=_any_specs(n), out_specs=_any_specs(n), input_output_aliases={i: i for i in range(n)},
        scratch_shapes=[pltpu.SemaphoreType.DMA((N_PEER_CHIPS * n,)), pltpu.SemaphoreType.DMA((N_PEER_CHIPS * n,))],
        compiler_params=pltpu.CompilerParams(has_side_effects=True), name=name)(*stacks)


def _exchange_start(name, srcs, lands, mode):
    n = len(lands)
    arrays = list(lands) if srcs is None else list(srcs) + list(lands)
    k = len(arrays)

    def body(*refs):
        land_refs = refs[k - n:k]
        send_sems, recv_sems = refs[k:k + n], refs[k + n:k + 2 * n]
        token = refs[2 * k + 2 * n]
        for cp in _quarter_copies(refs[:n], land_refs, send_sems, recv_sems, mode):
            cp.start()
        token[...] = jnp.zeros_like(token)

    sem = pltpu.SemaphoreType.DMA((N_PEER_CHIPS,))
    out_shape = [sem] * (2 * n) + [pltpu.HBM(a.shape, a.dtype) for a in arrays] + [jax.ShapeDtypeStruct((8, LANES), F32)]
    res = pl.pallas_call(
        body, name=name, out_shape=out_shape, in_specs=[HBM_SPEC] * k,
        out_specs=[SEM_SPEC] * (2 * n) + [HBM_SPEC] * k + [pl.BlockSpec(memory_space=pltpu.VMEM)],
        input_output_aliases={i: 2 * n + i for i in range(k)},
        compiler_params=pltpu.CompilerParams(has_side_effects=DATAFLOW_EFFECT),
    )(*[pltpu.with_memory_space_constraint(a, pltpu.HBM) for a in arrays])
    thru = res[2 * n:2 * n + k]
    return res[:n], res[n:2 * n], (None if srcs is None else thru[:n]), thru[k - n:], res[2 * n + k]


def _exchange_wait(name, srcs, lands, send_sems, recv_sems, after, mode):
    n = len(lands)
    arrays = list(lands) if srcs is None else list(srcs) + list(lands)
    k = len(arrays)
    after = list(after) if isinstance(after, (list, tuple)) else [after]

    def body(*refs):
        sends, recvs = refs[k:k + n], refs[k + n:k + 2 * n]
        for cp in _quarter_copies(refs[:n], refs[k - n:k], sends, recvs, mode):
            cp.wait_send()
            cp.wait_recv()

    res = pl.pallas_call(
        body, name=name, out_shape=[pltpu.HBM(a.shape, a.dtype) for a in arrays],
        in_specs=[HBM_SPEC] * k + [SEM_SPEC] * (2 * n) + _any_specs(len(after)),
        out_specs=[HBM_SPEC] * k, input_output_aliases={i: i for i in range(k)},
        compiler_params=pltpu.CompilerParams(has_side_effects=DATAFLOW_EFFECT),
    )(*arrays, *send_sems, *recv_sems, *after)
    return (None if srcs is None else res[:n]), res[k - n:]


def _scatter_and_forward(name, stacks, lands, old_srcs, old_lands, old_sends, old_recvs):
    n1, n0 = len(stacks), len(old_lands)
    sibling_lands = [lax.empty(a.shape, a.dtype) for a in old_lands]
    arrays = list(stacks) + list(lands) + list(old_srcs) + list(old_lands) + sibling_lands
    k, s = len(arrays), 2 * n1 + 2 * n0

    def body(*refs):
        new_srcs, new_lands = refs[:n1], refs[n1:2 * n1]
        was_srcs, landed, to_sibling = refs[2 * n1:2 * n1 + n0], refs[2 * n1 + n0:2 * n1 + 2 * n0], refs[2 * n1 + 2 * n0:k]
        was_sends, was_recvs = refs[k:k + n0], refs[k + n0:k + 2 * n0]
        sems = refs[k + 2 * n0:k + 2 * n0 + s]
        token = refs[k + 2 * n0 + s + k]
        for cp in _quarter_copies(new_srcs, new_lands, sems[:n1], sems[n1:2 * n1], "scatter"):
            cp.start()
        for cp in _quarter_copies(was_srcs, landed, was_sends, was_recvs, "scatter"):
            cp.wait_send()
            cp.wait_recv()
        for cp in _quarter_copies(landed, to_sibling, sems[2 * n1:2 * n1 + n0], sems[2 * n1 + n0:], "sibling"):
            cp.start()
        token[...] = jnp.zeros_like(token)

    sem = pltpu.SemaphoreType.DMA((N_PEER_CHIPS,))
    res = pl.pallas_call(
        body, name=name,
        out_shape=[sem] * s + [pltpu.HBM(a.shape, a.dtype) for a in arrays] + [jax.ShapeDtypeStruct((8, LANES), F32)],
        in_specs=[HBM_SPEC] * k + [SEM_SPEC] * (2 * n0),
        out_specs=[SEM_SPEC] * s + [HBM_SPEC] * k + [pl.BlockSpec(memory_space=pltpu.VMEM)],
        input_output_aliases={i: s + i for i in range(k)},
        compiler_params=pltpu.CompilerParams(has_side_effects=DATAFLOW_EFFECT),
    )(*[pltpu.with_memory_space_constraint(a, pltpu.HBM) for a in arrays], *old_sends, *old_recvs)
    thru = res[s:s + k]
    scatter = (res[:n1], res[n1:2 * n1], thru[:n1], thru[n1:2 * n1])
    sibling = (res[2 * n1:2 * n1 + n0], res[2 * n1 + n0:s], thru[2 * n1 + n0:2 * n1 + 2 * n0], thru[2 * n1 + 2 * n0:])
    return scatter, sibling, res[s + k]


def _own_slots(name, srcs, from_stack=False):
    n = len(srcs)
    me = (2 * lax.axis_index("x") + lax.axis_index("y")).astype(jnp.int32).reshape(1)

    def body(me_ref, *refs):
        for x_ref, o_ref in zip(refs[:n], refs[n:]):
            o_ref[...] = x_ref[...].astype(o_ref.dtype)

    in_specs, out_specs, out_shape = [], [], []
    for src in srcs:
        R, C = src.shape[-2:]
        in_specs.append(pl.BlockSpec((None, R // 2, C), lambda i, me_ref: (me_ref[0], i, 0)) if from_stack
                        else pl.BlockSpec((R // 2, C), lambda i, me_ref: (i, 0)))
        out_specs.append(pl.BlockSpec((None, R // 2, C), lambda i, me_ref: (me_ref[0], i, 0)))
        out_shape.append(jax.ShapeDtypeStruct((4, R, C), BF16))
    grid_spec = pltpu.PrefetchScalarGridSpec(num_scalar_prefetch=1, grid=(2,), in_specs=in_specs, out_specs=out_specs)
    return pl.pallas_call(body, out_shape=out_shape, grid_spec=grid_spec, compiler_params=_params("parallel"), name=name)(me, *srcs)


def _allreduce_small(buf):
    R, C = buf.shape
    flips = [(fx, fy, fc) for fx in (0, 1) for fy in (0, 1) for fc in (0, 1)][1:]

    def body(in_ref, out_ref, land_ref, send_sems, recv_sems):
        x, y, c = _position()
        me = 4 * x + 2 * y + c
        copies = []
        for k, (fx, fy, fc) in enumerate(flips):
            px, py, pc = (1 - x if fx else x), (1 - y if fy else y), (1 - c if fc else c)
            cp = pltpu.make_async_remote_copy(in_ref, land_ref.at[me], send_sems.at[k], recv_sems.at[k],
                                              device_id=(px, py, pc), device_id_type=MESH)
            cp.start()
            copies.append(cp)
        land_ref[me] = in_ref[...]
        for cp in copies:
            cp.wait()
        acc = land_ref[0]
        for k in range(1, 8):
            acc = acc + land_ref[k]
        out_ref[...] = acc

    return pl.pallas_call(
        body, out_shape=jax.ShapeDtypeStruct((R, C), F32),
        in_specs=[pl.BlockSpec(memory_space=pltpu.VMEM)], out_specs=pl.BlockSpec(memory_space=pltpu.VMEM),
        scratch_shapes=[pltpu.VMEM((8, R, C), F32), pltpu.SemaphoreType.DMA((7,)), pltpu.SemaphoreType.DMA((7,))],
        compiler_params=pltpu.CompilerParams(has_side_effects=True), name="allreduce_small")(buf)


def _adamw_math(w, g, m, v):
    m2 = ADAM_B1 * m + (1.0 - ADAM_B1) * g
    v2 = ADAM_B2 * v + (1.0 - ADAM_B2) * (g * g)
    m_hat = m2 / (1.0 - ADAM_B1 ** ADAM_STEP)
    v_hat = v2 / (1.0 - ADAM_B2 ** ADAM_STEP)
    delta = -ADAM_LR * (m_hat / (jnp.sqrt(v_hat) + ADAM_EPS) + ADAM_WD * w)
    return delta, m2, v2


def _adamw_from_partials(wv, mv, vv, *parts):
    def four(a, b, c, d):
        return ((a.astype(F32) + b.astype(F32)) + c.astype(F32)) + d.astype(F32)

    g = four(*parts[:4]) + four(*parts[4:])
    return (g,) + _adamw_math(wv, g, mv, vv)


def _adamw_big(name, w, m, v, mine, theirs):
    R, C = w.shape
    rows = 256 if R % 256 == 0 else R // 2
    nrb = R // rows
    slots = [_tiled(s.reshape(4 * R, C), None, 0, k * nrb) for s in (mine, theirs) for k in range(4)]
    return _ew(name, _adamw_from_partials, [_tiled(w), _tiled(m), _tiled(v)] + slots, [(F32, C)] * 4, n_rows=R, rows=rows)


def _adamw_rider(w, m, v, mine, theirs, steps, deliver):
    R, C = w.shape
    fits = [nb for nb in range(1, steps + 1) if R % nb == 0 and (R // nb) % 16 == 0]
    if not fits:
        return None
    nb = fits[-1]
    rows = R // nb

    def blocks(first):
        return pl.BlockSpec((rows, C), lambda *g: (first + jnp.minimum(g[0], nb - 1), 0))

    flat = [s.reshape(4 * R, C) for s in (mine, theirs)]
    return dict(operands=[w, m, v] + [f for f in flat for _ in range(4)],
                in_specs=[blocks(0)] * 3 + [blocks(k * nb) for _ in flat for k in range(4)],
                out_shape=[jax.ShapeDtypeStruct((R, C), F32)] * 4, out_specs=[blocks(0)] * 4,
                n_blocks=nb, fn=_adamw_from_partials, deliver=lambda outs: deliver(*outs))


BIG = ("ffn1_w1", "ffn1_w3", "ffn1_w2", "w_in", "w_branch_a", "w_branch_b", "w_out", "ffn2_w1", "ffn2_w3", "ffn2_w2")
SMALL = ("ffn1_norm", "mix_norm", "b_gate", "q_norm", "k_norm", "rel_bias", "ffn2_norm", "final_norm")
ORDER = ("ffn1_norm", "ffn1_w1", "ffn1_w3", "ffn1_w2", "mix_norm", "w_in", "b_gate", "q_norm", "k_norm", "rel_bias",
         "w_branch_a", "w_branch_b", "w_out", "ffn2_norm", "ffn2_w1", "ffn2_w3", "ffn2_w2", "final_norm")
TRANSPOSED = ("ffn1_w1", "ffn1_w3", "ffn2_w1", "ffn2_w3")
SIBLING_LAG = 2
LONG_HOST_STEPS = 8
GATHER_GROUPS = (("ffn1_w1", "ffn1_w3"), ("ffn1_w2",), ("w_in",), ("w_branch_a", "w_branch_b", "w_out"),
                 ("ffn2_w1", "ffn2_w3", "ffn2_w2"))


def _pack_small(d):
    rows = []
    for n in SMALL:
        flat = d[n].reshape(-1)
        pad = (-flat.shape[0]) % LANES
        rows.append(jnp.pad(flat, (0, pad)).reshape(-1, LANES))
    buf = jnp.concatenate(rows, axis=0)
    return jnp.pad(buf, ((0, (-buf.shape[0]) % 8), (0, 0)))


def _unpack_small(buf, like):
    out, r = {}, 0
    for n in SMALL:
        size = like[n].size
        nr = -(-size // LANES)
        out[n] = buf[r:r + nr].reshape(-1)[:size].reshape(like[n].shape)
        r += nr
    return out


def kernel(x, ffn1_norm, ffn1_w1, ffn1_w3, ffn1_w2, mix_norm, w_in, b_gate, q_norm, k_norm, rel_bias, w_branch_a, w_branch_b, w_out, ffn2_norm, ffn2_w1, ffn2_w3, ffn2_w2, final_norm, loss_target, m_ffn1_norm, m_ffn1_w1, m_ffn1_w3, m_ffn1_w2, m_mix_norm, m_w_in, m_b_gate, m_q_norm, m_k_norm, m_rel_bias, m_w_branch_a, m_w_branch_b, m_w_out, m_ffn2_norm, m_ffn2_w1, m_ffn2_w3, m_ffn2_w2, m_final_norm, v_ffn1_norm, v_ffn1_w1, v_ffn1_w3, v_ffn1_w2, v_mix_norm, v_w_in, v_b_gate, v_q_norm, v_k_norm, v_rel_bias, v_w_branch_a, v_w_branch_b, v_w_out, v_ffn2_norm, v_ffn2_w1, v_ffn2_w3, v_ffn2_w2, v_final_norm):
    given = dict(locals())
    w = {n: given[n] for n in ORDER}
    m = {n: given["m_" + n] for n in ORDER}
    v = {n: given["v_" + n] for n in ORDER}
    T, D = x.shape[1], x.shape[2]

    def stored(a, n):
        a = a.reshape(a.shape[1:])
        return a.T if n in TRANSPOSED else a

    def returned(a, n):
        return (a.T if n in TRANSPOSED else a).reshape(w[n].shape)

    quarter = {n: stored(w[n], n) for n in BIG}
    send, recv, _, land_thru, token = _exchange_start(
        "gather_start", None, _own_slots("own_weights", [quarter[n] for n in BIG]), "gather")
    index = {n: i for i, n in enumerate(BIG)}
    ready, filling = {}, {}

    def landed_halves(group, after):
        ids = [index[n] for n in group]
        return _exchange_wait("gather_wait_" + group[0], None, [land_thru[i] for i in ids],
                              [send[i] for i in ids], [recv[i] for i in ids], after, "gather")[1]

    def prefetch_w(name, after):
        group = next(g for g in GATHER_GROUPS if name in g)
        started = _exchange_start("fill_start_" + group[0], None, landed_halves(group, after), "fill")
        filling[group] = started
        return [started[4]]

    def get_w(name, after):
        if name not in ready:
            group = next(g for g in GATHER_GROUPS if name in g)
            if group in filling:
                f_send, f_recv, _, thru, _ = filling[group]
                stacks = _exchange_wait("fill_wait_" + group[0], None, thru, f_send, f_recv, after, "fill")[1]
            else:
                stacks = _fill_from_sibling("gather_fill_" + group[0], landed_halves(group, after))
            for n, st in zip(group, stacks):
                ready[n] = st.reshape(D, D) if n in ("w_branch_b", "w_out") else st
        return ready[name]

    scattered, forwarded = [], []

    def forward_oldest(after):
        names, s_sem, r_sem, srcs, lands = scattered.pop(0)
        _, landed = _exchange_wait("scatter_wait_" + names[0], srcs, lands, s_sem, r_sem, after, "scatter")
        started = _exchange_start("sibling_start_" + names[0], landed, [lax.empty(a.shape, a.dtype) for a in landed], "sibling")
        forwarded.append((names,) + tuple(started[:4]))
        return started[4]

    def put_g(grads):
        names = list(grads)
        stacks = [grads[n].reshape((4,) + quarter[n].shape) for n in names]
        lands = _own_slots("own_grad_" + names[0], stacks, from_stack=True)
        if len(scattered) < SIBLING_LAG:
            started = _exchange_start("scatter_start_" + names[0], stacks, lands, "scatter")
            scattered.append((names,) + tuple(started[:4]))
            return [started[4]]
        old_names, s_sem, r_sem, old_srcs, old_lands = scattered.pop(0)
        scatter, sibling, token = _scatter_and_forward("scatter_start_" + names[0], stacks, lands, old_srcs, old_lands, s_sem, r_sem)
        scattered.append((names,) + scatter)
        forwarded.append((old_names,) + sibling)
        return [token]

    grads, deltas, new_m, new_v = {}, {}, {}, {}
    arrived, riding = {}, set()

    def partials(gi, after):
        if gi not in arrived:
            names, s_sem, r_sem, srcs, lands = forwarded[gi]
            arrived[gi] = _exchange_wait("sibling_wait_" + names[0], srcs, lands, s_sem, r_sem, after, "sibling")
        return arrived[gi]

    def deliver_to(n):
        def deliver(*res):
            grads[n], deltas[n], new_m[n], new_v[n] = [returned(r, n) for r in res]
        return deliver

    def take_rider(steps, after):
        waiting = [(quarter[n].size, gi, k, n) for gi, entry in enumerate(forwarded) for k, n in enumerate(entry[0]) if n not in riding]
        for _, gi, k, n in sorted(waiting, reverse=steps >= LONG_HOST_STEPS):
            mine, theirs = partials(gi, after)
            rider = _adamw_rider(quarter[n], stored(m[n], n), stored(v[n], n), mine[k], theirs[k], steps, deliver_to(n))
            if rider is not None:
                riding.add(n)
                return rider
        return None

    small = {n: w[n] for n in SMALL}
    packed = [_pack_small({n: d[n] for n in SMALL}) for d in (w, m, v)]
    loss_cols, grad_x, gs = _local_step(x.reshape(T, D), loss_target.reshape(T, D), small, get_w, put_g, deps=[token] + packed,
                                        prefetch_w=prefetch_w, take_rider=take_rider)

    after = grad_x
    while scattered:
        after = forward_oldest(after)
    for gi, entry in enumerate(forwarded):
        mine, theirs = partials(gi, after)
        for n, a, b in zip(entry[0], mine, theirs):
            if n not in riding:
                deliver_to(n)(*_adamw_big(f"adamw_{n}", quarter[n], stored(m[n], n), stored(v[n], n), a, b))

    gs = {n: gs[n].reshape(w[n].shape) for n in SMALL}
    packed_g = _pack_small(gs)
    n_small = packed_g.shape[0]
    summed = _allreduce_small(jnp.concatenate([packed_g, loss_cols.reshape(-1, LANES)], axis=0))
    g_small, loss = summed[:n_small], jnp.sum(summed[n_small:])
    R = g_small.shape[0]
    res = _ew("adamw_small", lambda wv, mv, vv, g: (g,) + _adamw_math(wv, g, mv, vv),
              [_tiled(packed[0]), _tiled(packed[1]), _tiled(packed[2]), _tiled(g_small)], [(F32, LANES)] * 4, n_rows=R, rows=R)
    for d, buf in zip((grads, deltas, new_m, new_v), res):
        d.update(_unpack_small(buf, w))

    return (loss, grad_x.reshape(x.shape), *[grads[n] for n in ORDER], *[deltas[n] for n in ORDER],
            *[new_m[n] for n in ORDER], *[new_v[n] for n in ORDER])
```

```python
import functools
import math

import numpy as np
import jax
import jax.numpy as jnp
from jax import lax
from jax.experimental import pallas as pl
from jax.experimental.pallas import tpu as pltpu

F32 = jnp.float32
BF16 = jnp.bfloat16
MESH = pl.DeviceIdType.MESH

NEG_INF = -1e30
EPS = 1e-6
GRID_W = 64
ROPE_THETA = 10000.0
DILATIONS = (1, 4, 16)
BAND_HALF = 64
HEAD_A = 64
HEADS_A = 8
WIDTH_A = HEADS_A * HEAD_A
HEAD_B = 128
LOG2_E = math.log2(math.e)
QK_SCALE_LOG2 = HEAD_B ** -0.5 * LOG2_E
N_BUCKETS = 32
MAX_DISTANCE = 1024
ADAM_LR, ADAM_B1, ADAM_B2, ADAM_EPS, ADAM_WD, ADAM_STEP = 0.001, 0.9, 0.999, 1e-08, 0.01, 10

B_Q, B_K, B_V = 4608, 5632, 5888
G_A, G_B = 6144, 7168
IN_WIDTH = 8192

VMEM_LIMIT_BYTES = 56 * 1024 * 1024
QB_A = 128
QB_B = 256


def _params(*sem):
    return pltpu.CompilerParams(dimension_semantics=sem, vmem_limit_bytes=VMEM_LIMIT_BYTES)


def _bs(shape, fn):
    return pl.BlockSpec(shape, fn)


def _resident(shape, fn):
    return pl.BlockSpec(shape, fn, pipeline_mode=pl.Buffered(1))


def _mm(name, grid, pairs, out_shape, out_spec, dims, *, extras=(), epilogue=None, deps=(), reds=(), rider=None):
    n_pairs, n_extra, n_deps = len(pairs), len(extras), len(deps)
    operands = [p[0] for p in pairs] + [p[2] for p in pairs] + [e[0] for e in extras] + list(deps)
    in_specs = [p[1] for p in pairs] + [p[3] for p in pairs] + [e[1] for e in extras] + _any_specs(n_deps)
    single = not isinstance(out_shape, (list, tuple))
    out_shapes = [out_shape] if single else list(out_shape)
    out_specs = [out_spec] if single else list(out_spec)
    n_out = len(out_shapes)
    out_shapes += [jax.ShapeDtypeStruct((1, w), F32) for w in reds]
    out_specs += [_bs((1, w), lambda *_: (0, 0)) for w in reds]
    n_rin = 0
    if rider is not None:
        assert rider["n_blocks"] <= grid[0]
        n_rin = len(rider["operands"])
        operands += list(rider["operands"])
        in_specs += list(rider["in_specs"])
        out_shapes += list(rider["out_shape"])
        out_specs += list(rider["out_specs"])

    def body(*refs):
        a_refs, b_refs = refs[:n_pairs], refs[n_pairs:2 * n_pairs]
        e_refs = refs[2 * n_pairs:2 * n_pairs + n_extra]
        o_refs = refs[2 * n_pairs + n_extra + n_deps + n_rin:]
        if rider is not None:
            r_in = refs[2 * n_pairs + n_extra + n_deps:2 * n_pairs + n_extra + n_deps + n_rin]
            r_out = o_refs[n_out + len(reds):]

            @pl.when(pl.program_id(0) < rider["n_blocks"])
            def _():
                for ref, val in zip(r_out, rider["fn"](*[r[...] for r in r_in])):
                    ref[...] = val.astype(ref.dtype)
        acc = None
        for a_ref, b_ref in zip(a_refs, b_refs):
            t = lax.dot_general(a_ref[...], b_ref[...], (dims, ((), ())), preferred_element_type=F32)
            acc = t if acc is None else acc + t
        vals = acc if epilogue is None else epilogue(acc, *[e[...] for e in e_refs])
        if not isinstance(vals, (list, tuple)):
            vals = (vals,)
        for o_ref, v in zip(o_refs[:n_out], vals[:n_out]):
            o_ref[...] = v.astype(o_ref.dtype)
        if reds:
            first = functools.reduce(jnp.logical_and, [pl.program_id(ax) == 0 for ax in range(len(grid))])
            for r_ref, v in zip(o_refs[n_out:], vals[n_out:]):
                @pl.when(first)
                def _(r_ref=r_ref):
                    r_ref[...] = jnp.zeros_like(r_ref)
                r_ref[...] += v

    sem = ["arbitrary" if (reds or rider is not None) else "parallel"] * len(grid)
    res = pl.pallas_call(
        body, out_shape=out_shapes, grid=grid, in_specs=in_specs, out_specs=out_specs,
        compiler_params=_params(*sem), name=name)(*operands)
    if rider is not None:
        rider["deliver"](res[n_out + len(reds):])
        res = res[:n_out + len(reds)]
    return res[0] if (single and not reds) else res


NN = ((1,), (0,))
NT = ((1,), (1,))
TN = ((0,), (0,))


def _mm_wgrad(name, a, b, *, a_cols, b_cols, tm, tn, J, deps=(), rider=None):
    def pick(arr, cols, t):
        if arr.ndim == 3:
            T, c = arr.shape[1], arr.shape[2]
            t = min(t, c)
            return T, c, t, (lambda sel: _bs((None, T, t), lambda j, i, k: (j, 0, sel(i, k))))
        T = arr.shape[0]
        c = arr.shape[1] if cols is None else cols
        t = min(t, c)
        per = c // t
        if cols is None:
            if per == 1:
                return T, c, t, (lambda sel: _resident((T, t), lambda j, i, k: (0, 0)))
            return T, c, t, (lambda sel: _bs((T, t), lambda j, i, k: (0, sel(i, k))))
        return T, c, t, (lambda sel: _bs((T, t), lambda j, i, k: (0, j * per + sel(i, k))))
    _, ca, tm, mk_a = pick(a, a_cols, tm)
    _, cb, tn, mk_b = pick(b, b_cols, tn)
    return _mm(name, (J, ca // tm, cb // tn),
               [(a, mk_a(lambda i, k: i), b, mk_b(lambda i, k: k))],
               jax.ShapeDtypeStruct((J, ca, cb), BF16), _bs((None, tm, tn), lambda j, i, k: (j, i, k)), TN, deps=deps, rider=rider)


def _tiled(arr, width=None, col=0, rowblk=0):
    return ("t", arr, arr.shape[1] if width is None else width, col, rowblk)


def _table(arr):
    return ("f", arr)


def _whole(arr):
    return ("w", arr)


def _ew(name, fn, ins, outs, *, n_rows, rows, reds=(), ncols=1, deps=()):
    nrb = n_rows // rows
    n_deps = len(deps)
    operands, in_specs = [], []
    for spec in ins:
        if spec[0] == "t":
            _, arr, width, col, rowblk = spec
            step = 1 if ncols > 1 else 0
            in_specs.append(_bs((rows, width), lambda c, i, col=col, rowblk=rowblk, step=step: (rowblk + i, col + c * step)))
        elif spec[0] == "f":
            arr = spec[1]
            in_specs.append(_bs((rows, arr.shape[1]), lambda c, i: (i, 0)))
        else:
            arr = spec[1]
            nd = arr.ndim
            if nd == 3:
                in_specs.append(_bs((None,) + arr.shape[1:], lambda c, i: (c, 0, 0)))
            else:
                in_specs.append(_bs(arr.shape, lambda c, i, nd=nd: (0,) * nd))
        operands.append(arr)
    out_shapes = [jax.ShapeDtypeStruct((n_rows, ncols * w), dt) for dt, w in outs]
    out_specs = [_bs((rows, w), lambda c, i: (i, c)) for _, w in outs]
    out_shapes += [jax.ShapeDtypeStruct((ncols, 1, w), F32) for w in reds]
    out_specs += [_bs((None, 1, w), lambda c, i: (c, 0, 0)) for w in reds]
    n_in, n_out, n_red = len(ins), len(outs), len(reds)
    operands += list(deps)
    in_specs += _any_specs(n_deps)

    def body(*refs):
        vals = fn(*[r[...] for r in refs[:n_in]])
        if not isinstance(vals, (tuple, list)):
            vals = (vals,)
        o_refs = refs[n_in + n_deps:]
        for o_ref, v in zip(o_refs[:n_out], vals[:n_out]):
            o_ref[...] = v.astype(o_ref.dtype)
        if n_red:
            i = pl.program_id(1)
            for r_ref, v in zip(o_refs[n_out:], vals[n_out:]):
                @pl.when(i == 0)
                def _(r_ref=r_ref):
                    r_ref[...] = jnp.zeros_like(r_ref)
                r_ref[...] += v

    res = pl.pallas_call(
        body, out_shape=out_shapes, grid=(ncols, nrb), in_specs=in_specs, out_specs=out_specs,
        compiler_params=_params("parallel", "arbitrary" if n_red else "parallel"), name=name)(*operands)
    return res


def _colsum(v):
    return jnp.sum(v, axis=0, keepdims=True)


def _rstd(x):
    return lax.rsqrt(jnp.mean(x * x, axis=-1, keepdims=True) + EPS)


def _sigmoid(x):
    return 0.5 * jnp.tanh(0.5 * x) + 0.5


def _norm_fwd(x, g):
    return x * _rstd(x) * g


def _norm_bwd(x, g, dy):
    r = _rstd(x)
    xh = x * r
    dxh = dy * g
    dx = r * (dxh - xh * jnp.mean(dxh * xh, axis=-1, keepdims=True))
    return dx, dy * xh


def _row_spec(arr, rows):
    if arr.shape[0] == 1:
        return _bs(arr.shape, lambda i: (0, 0))
    return _bs((rows, arr.shape[1]), lambda i: (i, 0))


def _ffn_fwd(tag, x, gain, get_w, deps=(), *, h=None, tail_ins=(), tail_fn=None, tail_outs=(F32,), tail_reds=(),
             ahead_of_up=None, ahead_of_down=None):
    T, D = x.shape
    if h is None:
        (h,) = _ew(f"{tag}_norm", lambda xv, g: _norm_fwd(xv, g), [_tiled(x), _whole(gain)], [(BF16, D)], n_rows=T, rows=512,
                   deps=deps)
    w1, w3 = get_w(f"{tag}_w1", h), get_w(f"{tag}_w3", h)
    J, f, _ = w1.shape
    tm = 1024

    up_deps = list(ahead_of_up(w1)) if ahead_of_up else []

    def up(h_ref, w1_ref, w3_ref, *rest):
        u_ref, g_ref, a_ref = rest[len(up_deps):]
        hv = h_ref[...]
        u = lax.dot_general(hv, w1_ref[...], (NT, ((), ())), preferred_element_type=F32)
        g = lax.dot_general(hv, w3_ref[...], (NT, ((), ())), preferred_element_type=F32)
        u_ref[...] = u.astype(BF16)
        g_ref[...] = g.astype(BF16)
        a_ref[...] = (u * _sigmoid(u) * g).astype(BF16)

    slab = _bs((None, tm, f), lambda j, i: (j, i, 0))
    w_spec = _bs((None, f, D), lambda j, i: (j, 0, 0))
    u, g, a = pl.pallas_call(
        up, out_shape=[jax.ShapeDtypeStruct((J, T, f), BF16)] * 3, grid=(J, T // tm),
        in_specs=[_bs((tm, D), lambda j, i: (i, 0)), w_spec, w_spec] + _any_specs(len(up_deps)), out_specs=[slab] * 3,
        compiler_params=_params("parallel", "parallel"), name=f"{tag}_up")(h, w1, w3, *up_deps)
    w2 = get_w(f"{tag}_w2", a)
    down_deps = list(ahead_of_down(w2)) if ahead_of_down else []
    def tail(acc, xv, *rest):
        y = xv + 0.5 * acc
        return y if tail_fn is None else tail_fn(y, *rest)

    row = _bs((512, D), lambda i: (i, 0))
    res = _mm(f"{tag}_down", (T // 512,),
              [(a, _bs((None, 512, f), lambda i, j=j: (j, i, 0)), w2, _resident((None, f, D), lambda i, j=j: (j, 0, 0)))
               for j in range(J)],
              [jax.ShapeDtypeStruct((T, D), dt) for dt in tail_outs], [row] * len(tail_outs), NN,
              extras=[(x, row)] + [(t, _row_spec(t, 512)) for t in tail_ins], epilogue=tail, reds=tail_reds, deps=down_deps)
    return res, (h, u, g, a)


def _dh_norm_bwd(name, rows, pairs, dims, x, gain, dres, deps, also_bf16=False, rider=None):
    T, D = x.shape

    def epilogue(dh, xv, gv, dr):
        dx, dgr = _norm_bwd(xv, gv, dh)
        dx = dx + dr
        return (dx, 0.5 * dx) + ((dx,) if also_bf16 else ()) + (_colsum(dgr),)

    dts = [F32, BF16] + ([BF16] if also_bf16 else [])
    row = _bs((rows, D), lambda i: (i, 0))
    return _mm(name, (T // rows,), pairs, [jax.ShapeDtypeStruct((T, D), dt) for dt in dts], [row] * len(dts), dims,
               extras=[(x, row), (gain, _row_spec(gain, rows)), (dres, row)], epilogue=epilogue, deps=deps, reds=(D,), rider=rider)


def _ffn_bwd(tag, x, gain, get_w, put_g, saved, dy, dy_half, also_bf16=False, last=False, take_rider=lambda steps, after: None):
    h, u, g, a = saved
    T, D = x.shape
    w1, w3, w2 = [get_w(f"{tag}_{n}", dy_half) for n in ("w1", "w3", "w2")]
    J, f, _ = w1.shape
    dw2 = _mm_wgrad(f"{tag}_bwd_dw2", a, dy_half, a_cols=None, b_cols=None, tm=f, tn=D, J=J, rider=take_rider(J, dy_half))
    deps = put_g({f"{tag}_w2": dw2}) if last else []
    tm = 1024

    def up_bwd(dy_ref, w2_ref, u_ref, g_ref, *rest):
        du_ref, dg_ref = rest[-2:]
        da = lax.dot_general(dy_ref[...], w2_ref[...], (NT, ((), ())), preferred_element_type=F32)
        uv, gv = u_ref[...].astype(F32), g_ref[...].astype(F32)
        s = _sigmoid(uv)
        silu = uv * s
        du_ref[...] = (da * gv * (s + silu - silu * s)).astype(BF16)
        dg_ref[...] = (da * silu).astype(BF16)

    slab = _bs((None, tm, f), lambda j, i: (j, i, 0))
    du, dg = pl.pallas_call(
        up_bwd, out_shape=[jax.ShapeDtypeStruct((J, T, f), BF16)] * 2, grid=(J, T // tm),
        in_specs=[_bs((tm, D), lambda j, i: (i, 0)), _bs((None, f, D), lambda j, i: (j, 0, 0)), slab, slab] + _any_specs(len(deps)),
        out_specs=[slab] * 2, compiler_params=_params("parallel", "parallel"), name=f"{tag}_bwd_up")(dy_half, w2, u, g, *deps)
    dw1 = _mm_wgrad(f"{tag}_bwd_dw1", du, h, a_cols=None, b_cols=None, tm=f, tn=D, J=J)
    deps = put_g({f"{tag}_w1": dw1}) if last else []
    dw3 = _mm_wgrad(f"{tag}_bwd_dw3", dg, h, a_cols=None, b_cols=None, tm=f, tn=D, J=J, deps=deps)
    deps = put_g({f"{tag}_w3": dw3} if last else {f"{tag}_w2": dw2, f"{tag}_w1": dw1, f"{tag}_w3": dw3})
    pairs = []
    for j in range(J):
        a_spec = _bs((None, 256, f), lambda i, j=j: (j, i, 0))
        w_spec = _resident((None, f, D), lambda i, j=j: (j, 0, 0))
        pairs += [(du, a_spec, w1, w_spec), (dg, a_spec, w3, w_spec)]
    return _dh_norm_bwd(f"{tag}_bwd_dh", 256, pairs, NN, x, gain, dy, deps, also_bf16, rider=take_rider(T // 256, dw3))


def _t5_bucket(rel):
    n = N_BUCKETS // 2
    max_exact = n // 2
    ret = jnp.where(rel > 0, n, 0)
    a = jnp.abs(rel)
    af = jnp.maximum(a, 1).astype(F32)
    large = max_exact + (jnp.log(af / max_exact) / math.log(MAX_DISTANCE / max_exact) * (n - max_exact)).astype(jnp.int32)
    large = jnp.minimum(large, n - 1)
    return ret + jnp.where(a < max_exact, a, large)


WIN_A = QB_A + 2 * BAND_HALF
WIN_SHIFTS = (0, BAND_HALF, 2 * BAND_HALF)


def _window_variant(n, nblk):
    return jnp.where(n == 0, 0, jnp.where(n == nblk - 1, 2, 1))


def _window_start(n, nblk):
    return pl.multiple_of(jnp.clip(n * QB_A - BAND_HALF, 0, nblk * QB_A - WIN_A), BAND_HALF)


def _band_steps(xp=jnp):
    qi = xp.arange(QB_A, dtype=xp.int32)[None, :, None]
    kj = xp.arange(WIN_A, dtype=xp.int32)[None, None, :]
    return kj - qi - xp.asarray(WIN_SHIFTS, dtype=xp.int32)[:, None, None]


def _bias_tiles(rel_bias):
    wide = QB_A + 2 * WIN_SHIFTS[-1]
    qi = jnp.arange(QB_A, dtype=jnp.int32)[:, None]
    steps = jnp.arange(wide, dtype=jnp.int32)[None, :] - WIN_SHIFTS[-1] - qi
    buckets = jnp.stack([_t5_bucket(steps * d) for d in DILATIONS])
    inband = (jnp.abs(steps) <= BAND_HALF).astype(jnp.int32)
    n_heads = rel_bias.shape[1]

    def body(tab_ref, b_ref, m_ref, o_ref):
        hd = pl.program_id(0)
        bkt = b_ref[...]
        acc = jnp.zeros(bkt.shape, F32)
        for b in range(N_BUCKETS):
            acc = jnp.where(bkt == b, tab_ref[b, hd], acc)
        o_ref[...] = jnp.where(m_ref[...] > 0, acc, NEG_INF)

    base = pl.pallas_call(
        body, out_shape=jax.ShapeDtypeStruct((n_heads, QB_A, wide), F32), grid=(n_heads,),
        in_specs=[pl.BlockSpec(memory_space=pltpu.SMEM),
                  _bs((None, QB_A, wide), lambda hd: (hd // HEADS_A, 0, 0)),
                  _bs((QB_A, wide), lambda hd: (0, 0))],
        out_specs=_bs((None, QB_A, wide), lambda hd: (hd, 0, 0)),
        compiler_params=_params("parallel"), name="a_bias_tiles")(rel_bias, buckets, inband)
    base = base.reshape(len(DILATIONS), HEADS_A, QB_A, wide)
    return jnp.stack([base[..., WIN_SHIFTS[-1] - s:WIN_SHIFTS[-1] - s + WIN_A] for s in WIN_SHIFTS], axis=1)


def _bias_grad(dbias):
    steps = _band_steps(np)
    inband = np.abs(steps) <= BAND_HALF
    present = []
    for d in DILATIONS:
        rel = steps * d
        a = np.abs(rel)
        large = 8 + (np.log(np.maximum(a, 1) / 8.0) / math.log(MAX_DISTANCE / 8.0) * 8).astype(np.int64)
        bk = np.where(rel > 0, 16, 0) + np.where(a < 8, a, np.minimum(large, 15))
        present.append([sorted(set(bk[v][inband[v]].tolist())) for v in range(3)])
    buckets = jnp.stack([_t5_bucket(_band_steps() * d) for d in DILATIONS])
    n_heads = len(DILATIONS) * HEADS_A

    def body(b_ref, d_ref, o_ref):
        row = lax.broadcasted_iota(jnp.int32, (N_BUCKETS, n_heads), 0)
        col = lax.broadcasted_iota(jnp.int32, (N_BUCKETS, n_heads), 1)
        out = jnp.zeros((N_BUCKETS, n_heads), F32)
        for grp in range(len(DILATIONS)):
            for hh in range(HEADS_A):
                hd = grp * HEADS_A + hh
                for b in sorted(set(sum(present[grp], []))):
                    tot = jnp.zeros((), F32)
                    for v in range(3):
                        if b in present[grp][v]:
                            tot = tot + jnp.sum(jnp.where(b_ref[grp, v] == b, d_ref[grp, v, hh], 0.0))
                    out = jnp.where((row == b) & (col == hd), tot, out)
        o_ref[...] = out

    return pl.pallas_call(
        body, out_shape=jax.ShapeDtypeStruct((N_BUCKETS, n_heads), F32),
        compiler_params=pltpu.CompilerParams(vmem_limit_bytes=VMEM_LIMIT_BYTES), name="a_bias_grad")(buckets, dbias)


def _lane_is_second_head(shape):
    return lax.broadcasted_iota(jnp.int32, shape, len(shape) - 1) >= HEAD_A


VIEW_ROWS = 512


def _view_chunks():
    return [pltpu.VMEM((VIEW_ROWS, LANES), F32)] * (WIDTH_A // LANES)


def _rows_to_view(x_ref, col, o_ref, ocol, d, chunks):
    n = VIEW_ROWS // d
    for c, scr in enumerate(chunks):
        scr[...] = x_ref[:, col + c * LANES:col + (c + 1) * LANES].astype(F32)
        for r in range(d):
            at = ocol + r * WIDTH_A + c * LANES
            o_ref[:, at:at + LANES] = scr[pl.ds(r, n, stride=d), :].astype(o_ref.dtype)


def _view_to_rows(v_ref, o_ref, col, d, chunks):
    n = VIEW_ROWS // d
    for c, scr in enumerate(chunks):
        if d == 1:
            o_ref[:, col + c * LANES:col + (c + 1) * LANES] = v_ref[:, c * LANES:(c + 1) * LANES].astype(o_ref.dtype)
            continue
        for r in range(d):
            scr[pl.ds(r, n, stride=d), :] = v_ref[:, r * WIDTH_A + c * LANES:r * WIDTH_A + (c + 1) * LANES].astype(F32)
        o_ref[:, col + c * LANES:col + (c + 1) * LANES] = scr[...].astype(o_ref.dtype)


def _group_view(proj, grp, d):
    T = proj.shape[0]
    if d == 1:
        return proj, (lambda part, r: grp * 3 + part)

    def body(x_ref, o_ref, *chunks):
        for part in range(3):
            _rows_to_view(x_ref, part * WIDTH_A, o_ref, part * d * WIDTH_A, d, chunks)

    view = pl.pallas_call(
        body, out_shape=jax.ShapeDtypeStruct((T // d, 3 * d * WIDTH_A), proj.dtype), grid=(T // VIEW_ROWS,),
        in_specs=[_bs((VIEW_ROWS, 3 * WIDTH_A), lambda i: (i, grp))],
        out_specs=_bs((VIEW_ROWS // d, 3 * d * WIDTH_A), lambda i: (i, 0)),
        scratch_shapes=_view_chunks(), compiler_params=_params("parallel"), name=f"a_view_d{d}")(proj)
    return view, (lambda part, r: part * d + r)


def _stack_heads(v2, second):
    zero = jnp.zeros_like(v2)
    return jnp.concatenate([jnp.where(second, zero, v2), jnp.where(second, v2, zero)], axis=0)


def _unstack_heads(v, second):
    return jnp.where(second, v[QB_A:], v[:QB_A])


def _dil_fwd(view, bias, d):
    pv, colblk = view
    L = pv.shape[0]
    nblk = L // QB_A
    W2 = 2 * HEAD_A
    scale = HEAD_A ** -0.5

    def body(q_ref, k_ref, v_ref, b_ref, o_ref, l_ref):
        win = pl.ds(_window_start(pl.program_id(1), nblk), WIN_A)
        second = _lane_is_second_head((QB_A, W2))
        pairs = range(HEADS_A // 2)
        cols = [slice(hp * W2, (hp + 1) * W2) for hp in pairs]
        s = [lax.dot_general(_stack_heads(q_ref[:, cols[hp]], second), k_ref[win, cols[hp]], (NT, ((), ())),
                             preferred_element_type=F32) * scale + b_ref[2 * hp:2 * hp + 2].reshape(2 * QB_A, WIN_A)
             for hp in pairs]
        m = [jnp.max(x, axis=-1, keepdims=True) for x in s]
        p = [jnp.exp(x - mx) for x, mx in zip(s, m)]
        l = [jnp.sum(x, axis=-1, keepdims=True) for x in p]
        res = [jnp.dot(p[hp].astype(BF16), v_ref[win, cols[hp]], preferred_element_type=F32) / l[hp] for hp in pairs]
        o_ref[...] = jnp.concatenate([_unstack_heads(x, second) for x in res], axis=1).astype(o_ref.dtype)
        l_ref[...] = jnp.concatenate([_unstack_heads(jnp.broadcast_to(mx + jnp.log(lx), (2 * QB_A, W2)), second)
                                      for mx, lx in zip(m, l)], axis=1)

    in_specs = [_bs((QB_A, WIDTH_A), lambda r, n: (n, colblk(0, r))),
                _bs((L, WIDTH_A), lambda r, n: (0, colblk(1, r))), _bs((L, WIDTH_A), lambda r, n: (0, colblk(2, r))),
                _bs((None, HEADS_A, QB_A, WIN_A), lambda r, n: (_window_variant(n, nblk), 0, 0, 0))]
    o, lse = pl.pallas_call(
        body, out_shape=[jax.ShapeDtypeStruct((L, d * WIDTH_A), BF16), jax.ShapeDtypeStruct((L, d * WIDTH_A), F32)],
        grid=(d, nblk), in_specs=in_specs,
        out_specs=[_bs((QB_A, WIDTH_A), lambda r, n: (n, r)), _bs((QB_A, WIDTH_A), lambda r, n: (n, r))],
        compiler_params=_params("parallel", "parallel"), name=f"a_fwd_d{d}")(pv, pv, pv, bias)
    return o, lse


def _dil_bwd(view_qkv, bias, do, lse, cterm, d):
    pv, colblk = view_qkv
    L = pv.shape[0]
    nblk = L // QB_A
    W2 = 2 * HEAD_A
    PPS = 4
    WS = PPS * W2
    ob = WIDTH_A // WS
    scale = HEAD_A ** -0.5

    def body(q_ref, k_ref, v_ref, do_ref, l_ref, c_ref, b_ref, dq_ref, dk_ref, dv_ref, db_ref):
        r, n = pl.program_id(1), pl.program_id(2)

        @pl.when(n == 0)
        def _():
            dk_ref[...] = jnp.zeros_like(dk_ref)
            dv_ref[...] = jnp.zeros_like(dv_ref)

        @pl.when((n == 0) & (r == 0))
        def _():
            db_ref[...] = jnp.zeros_like(db_ref)

        second = _lane_is_second_head((QB_A, W2))
        win = pl.ds(_window_start(n, nblk), WIN_A)
        variant = _window_variant(n, nblk)
        pairs = range(PPS)
        cols = [slice(pp * W2, (pp + 1) * W2) for pp in pairs]

        def head_rows(ref, pp):
            v2 = ref[:, cols[pp]]
            return jnp.concatenate([v2[:, 0:1], v2[:, HEAD_A:HEAD_A + 1]], axis=0)

        kw = [k_ref[win, c] for c in cols]
        vw = [v_ref[win, c] for c in cols]
        qs = [_stack_heads(q_ref[:, c], second) for c in cols]
        dos = [_stack_heads(do_ref[:, c], second) for c in cols]
        s = [lax.dot_general(qs[pp], kw[pp], (NT, ((), ())), preferred_element_type=F32) for pp in pairs]
        dp = [lax.dot_general(dos[pp], vw[pp], (NT, ((), ())), preferred_element_type=F32) for pp in pairs]
        p = [jnp.exp(s[pp] * scale + b_ref[2 * pp:2 * pp + 2].reshape(2 * QB_A, WIN_A) - head_rows(l_ref, pp)) for pp in pairs]
        ds = [p[pp] * (dp[pp] + head_rows(c_ref, pp)) for pp in pairs]
        db_ref[variant] += jnp.concatenate([x.reshape(2, QB_A, WIN_A) for x in ds], axis=0)
        pb = [x.astype(BF16) for x in p]
        dsb = [(x * scale).astype(BF16) for x in ds]
        dq_ref[...] = jnp.concatenate([_unstack_heads(jnp.dot(dsb[pp], kw[pp], preferred_element_type=F32), second)
                                       for pp in pairs], axis=1).astype(dq_ref.dtype)
        dk_ref[win, :] += jnp.concatenate([lax.dot_general(dsb[pp], qs[pp], (TN, ((), ())), preferred_element_type=F32)
                                           for pp in pairs], axis=1)
        dv_ref[win, :] += jnp.concatenate([lax.dot_general(pb[pp], dos[pp], (TN, ((), ())), preferred_element_type=F32)
                                           for pp in pairs], axis=1)

    kv_spec = _resident if d == 1 else _bs
    in_specs = [_bs((QB_A, WS), lambda hp, r, n: (n, colblk(0, r) * ob + hp)),
                kv_spec((L, WS), lambda hp, r, n: (0, colblk(1, r) * ob + hp)),
                kv_spec((L, WS), lambda hp, r, n: (0, colblk(2, r) * ob + hp))]
    in_specs += [_bs((QB_A, WS), lambda hp, r, n: (n, r * ob + hp))] * 3
    in_specs += [_bs((None, 2 * PPS, QB_A, WIN_A), lambda hp, r, n: (_window_variant(n, nblk), hp, 0, 0))]
    out_shape = [jax.ShapeDtypeStruct((L, d * WIDTH_A), BF16), jax.ShapeDtypeStruct((L, d * WIDTH_A), F32),
                 jax.ShapeDtypeStruct((L, d * WIDTH_A), F32), jax.ShapeDtypeStruct((3, HEADS_A, QB_A, WIN_A), F32)]
    out_specs = [_bs((QB_A, WS), lambda hp, r, n: (n, r * ob + hp)),
                 _bs((L, WS), lambda hp, r, n: (0, r * ob + hp)), _bs((L, WS), lambda hp, r, n: (0, r * ob + hp)),
                 _bs((3, 2 * PPS, QB_A, WIN_A), lambda hp, r, n: (0, hp, 0, 0))]
    dq, dk, dv, db = pl.pallas_call(
        body, out_shape=out_shape, grid=(ob, d, nblk), in_specs=in_specs, out_specs=out_specs,
        compiler_params=_params("arbitrary", "arbitrary", "arbitrary"), name=f"a_bwd_d{d}")(
            pv, pv, pv, do, lse, cterm, bias)
    return dq, dk, dv, db


def _assemble_dproj(a_parts, dq_b, dk_b, dv_b, dga, dgb):
    T = dq_b.shape[0]
    flat = [(a_parts[part][g], d) for part in range(3) for g, d in enumerate(DILATIONS)]
    rest = [dq_b, dk_b, dv_b, dga, dgb]

    def body(*refs):
        views, others = refs[:len(flat)], refs[len(flat):len(flat) + len(rest)]
        o_ref, chunks = refs[len(flat) + len(rest)], refs[len(flat) + len(rest) + 1:]
        col = 0
        for v_ref, (_, d) in zip(views, flat):
            _view_to_rows(v_ref, o_ref, col, d, chunks)
            col += WIDTH_A
        for x_ref in others:
            w = x_ref.shape[1]
            o_ref[:, col:col + w] = x_ref[...].astype(o_ref.dtype)
            col += w

    in_specs = [_bs((VIEW_ROWS // d, d * WIDTH_A), lambda i: (i, 0)) for _, d in flat]
    in_specs += [_bs((VIEW_ROWS, x.shape[1]), lambda i: (i, 0)) for x in rest]
    return pl.pallas_call(
        body, out_shape=jax.ShapeDtypeStruct((T, IN_WIDTH), BF16), grid=(T // VIEW_ROWS,), in_specs=in_specs,
        out_specs=_bs((VIEW_ROWS, IN_WIDTH), lambda i: (i, 0)), scratch_shapes=_view_chunks(),
        compiler_params=_params("parallel"), name="mix_bwd_dproj")(*[a for a, _ in flat], *rest)


def _segment_ones():
    i = np.arange(WIDTH_A)
    return jnp.asarray((i[:, None] // HEAD_A == i[None, :] // HEAD_A).astype(np.float32), dtype=BF16)


def _group_weights(l0, l1, l2):
    m = jnp.maximum(jnp.maximum(l0, l1), l2)
    e = [jnp.exp(l - m) for l in (l0, l1, l2)]
    z = e[0] + e[1] + e[2]
    return [ei / z for ei in e]


def _view_specs():
    return [_bs((VIEW_ROWS // d, d * WIDTH_A), lambda i: (i, 0)) for d in DILATIONS]


def _stage_tiles(n):
    return [pltpu.VMEM((VIEW_ROWS, WIDTH_A), F32)] * n


def _token_rows(v_ref, stage, d, chunks):
    if d == 1:
        return v_ref[...].astype(F32)
    _view_to_rows(v_ref, stage, 0, d, chunks)
    return stage[...]


def _combine_fwd(outs, lses):
    T = outs[0].shape[0] * DILATIONS[0]
    n = len(DILATIONS)

    def body(*refs):
        o_refs, l_refs, oa_ref = refs[:n], refs[n:2 * n], refs[2 * n]
        o_st, l_st, chunks = refs[2 * n + 1:3 * n + 1], refs[3 * n + 1:4 * n + 1], refs[4 * n + 1:]
        o = [_token_rows(o_refs[g], o_st[g], d, chunks) for g, d in enumerate(DILATIONS)]
        w = _group_weights(*[_token_rows(l_refs[g], l_st[g], d, chunks) for g, d in enumerate(DILATIONS)])
        oa_ref[...] = (w[0] * o[0] + w[1] * o[1] + w[2] * o[2]).astype(oa_ref.dtype)

    return pl.pallas_call(
        body, out_shape=jax.ShapeDtypeStruct((T, WIDTH_A), BF16), grid=(T // VIEW_ROWS,),
        in_specs=_view_specs() * 2, out_specs=_bs((VIEW_ROWS, WIDTH_A), lambda i: (i, 0)),
        scratch_shapes=_stage_tiles(2 * n) + _view_chunks(), compiler_params=_params("parallel"), name="a_combine")(*outs, *lses)


def _combine_bwd(doa, outs, lses):
    T = doa.shape[0]
    n = len(DILATIONS)

    def body(*refs):
        d_ref, o_refs, l_refs, seg_ref = refs[0], refs[1:n + 1], refs[n + 1:2 * n + 1], refs[2 * n + 1]
        do_refs, c_refs = refs[2 * n + 2:3 * n + 2], refs[3 * n + 2:4 * n + 2]
        o_st, l_st = refs[4 * n + 2:5 * n + 2], refs[5 * n + 2:6 * n + 2]
        tmp, chunks = refs[6 * n + 2], refs[6 * n + 3:]
        o = [_token_rows(o_refs[g], o_st[g], d, chunks) for g, d in enumerate(DILATIONS)]
        w = _group_weights(*[_token_rows(l_refs[g], l_st[g], d, chunks) for g, d in enumerate(DILATIONS)])
        dv = d_ref[...].astype(F32)
        seg = seg_ref[...]
        tot = jnp.zeros(dv.shape, F32)
        for g in range(n):
            prod = w[g] * dv * o[g]
            hi = prod.astype(BF16)
            lo = (prod - hi.astype(F32)).astype(BF16)
            tot = tot + jnp.dot(hi, seg, preferred_element_type=F32) + jnp.dot(lo, seg, preferred_element_type=F32)
        for g, d in enumerate(DILATIONS):
            for ref, val in ((do_refs[g], w[g] * dv), (c_refs[g], -w[g] * tot)):
                if d == 1:
                    ref[...] = val.astype(ref.dtype)
                else:
                    tmp[...] = val
                    _rows_to_view(tmp, 0, ref, 0, d, chunks)

    views = [jax.ShapeDtypeStruct((T // d, d * WIDTH_A), dt) for dt in (BF16, F32) for d in DILATIONS]
    res = pl.pallas_call(
        body, out_shape=views, grid=(T // VIEW_ROWS,),
        in_specs=[_bs((VIEW_ROWS, WIDTH_A), lambda i: (i, 0))] + _view_specs() * 2 + [_bs((WIDTH_A, WIDTH_A), lambda i: (0, 0))],
        out_specs=_view_specs() * 2, scratch_shapes=_stage_tiles(2 * n + 1) + _view_chunks(),
        compiler_params=_params("parallel"), name="a_combine_bwd")(doa, *outs, *lses, _segment_ones())
    return res[:n], res[n:]


def _rope_tables(T):
    rows = T // GRID_W
    row = jnp.repeat(jnp.arange(rows, dtype=F32), GRID_W)
    col = jnp.tile(jnp.arange(GRID_W, dtype=F32), rows)
    n_freq = HEAD_B // 4
    freq = ROPE_THETA ** (-jnp.arange(n_freq, dtype=F32) / n_freq)
    ang = jnp.concatenate([row[:, None] * freq, col[:, None] * freq], axis=-1)
    cos, sin = jnp.repeat(jnp.cos(ang), 2, axis=1), jnp.repeat(jnp.sin(ang), 2, axis=1)
    sign = jnp.where(jnp.arange(HEAD_B) % 2 == 0, -1.0, 1.0).astype(F32)
    return cos, sin * sign


def _swap_pairs(v):
    even = lax.broadcasted_iota(jnp.int32, v.shape, v.ndim - 1) % 2 == 0
    n = v.shape[-1]
    return jnp.where(even, pltpu.roll(v, n - 1, v.ndim - 1), pltpu.roll(v, 1, v.ndim - 1))


def _qk_fwd(name, proj, col0, n_heads, gain, cos, sin, out_scale=1.0, deps=()):
    T = proj.shape[0]

    def fn(xr, g, c, s):
        xn = _norm_fwd(xr.astype(F32), g)
        return (xn * c + _swap_pairs(xn) * s) * out_scale

    (out,) = _ew(name, fn, [_tiled(proj, HEAD_B, col0 // HEAD_B), _whole(gain), _table(cos), _table(sin)],
                 [(BF16, HEAD_B)], n_rows=T, rows=2048, ncols=n_heads, deps=deps)
    return out


def _qk_bwd(name, dout, proj, col0, n_heads, gain, cos, sin, in_scale=1.0):
    T = proj.shape[0]

    def fn(dv, xr, g, c, s):
        dv = dv.astype(F32) * in_scale
        dxn = c * dv + _swap_pairs(s * dv)
        dx, dgr = _norm_bwd(xr.astype(F32), g, dxn)
        return dx, _colsum(dgr)

    dx, dg = _ew(name, fn, [_tiled(dout, HEAD_B, 0), _tiled(proj, HEAD_B, col0 // HEAD_B), _whole(gain),
                            _table(cos), _table(sin)],
                 [(BF16, HEAD_B)], n_rows=T, rows=2048, reds=(HEAD_B,), ncols=n_heads)
    return dx, jnp.sum(dg, axis=0)


def _gqa_fwd(qn, kn, proj, k_col=0):
    T = qn.shape[0]
    GW = 4 * HEAD_B
    QB = QB_B

    def body(q_ref, k_ref, v_ref, o_ref, l_ref):
        k = k_ref[...]
        v_ones = jnp.concatenate([v_ref[...], jnp.ones((T, HEAD_B), BF16)], axis=1)
        lane = lax.broadcasted_iota(jnp.int32, (QB, HEAD_B), 1)
        heads = range(4)
        s = [lax.dot_general(q_ref[:, g * HEAD_B:(g + 1) * HEAD_B], k, (NT, ((), ())), preferred_element_type=F32)
             for g in heads]
        m = [jnp.max(x, axis=-1, keepdims=True) for x in s]
        pv = [jnp.dot(jnp.exp2(x - mx).astype(BF16), v_ones, preferred_element_type=F32) for x, mx in zip(s, m)]
        l = [x[:, HEAD_B:HEAD_B + 1] for x in pv]
        o = [x[:, :HEAD_B] / lx for x, lx in zip(pv, l)]
        o_ref[...] = jnp.concatenate(o, axis=1).astype(o_ref.dtype)
        lse_all = jnp.zeros((QB, HEAD_B), F32)
        for g in heads:
            lse_all = jnp.where(lane == g, m[g] + jnp.log2(l[g]), lse_all)
        l_ref[...] = lse_all

    return pl.pallas_call(
        body, out_shape=[jax.ShapeDtypeStruct((T, 2 * GW), BF16), jax.ShapeDtypeStruct((2, T, HEAD_B), F32)],
        grid=(2, T // QB),
        in_specs=[_bs((QB, GW), lambda kv, i: (i, kv)), _bs((T, HEAD_B), lambda kv, i: (0, k_col + kv)),
                  _bs((T, HEAD_B), lambda kv, i: (0, B_V // HEAD_B + kv))],
        out_specs=[_bs((QB, GW), lambda kv, i: (i, kv)), _bs((None, QB, HEAD_B), lambda kv, i: (kv, i, 0))],
        compiler_params=_params("parallel", "parallel"), name="b_fwd")(qn, kn, proj)


def _gqa_bwd(qn, kn, proj, o, lse, do, deps=(), k_col=0):
    T = qn.shape[0]
    GW = 4 * HEAD_B

    def body(q_ref, k_ref, v_ref, o_ref, l_ref, do_ref, *rest):
        dq_ref, dk_ref, dv_ref = rest[-3:]
        i = pl.program_id(1)

        @pl.when(i == 0)
        def _():
            dk_ref[...] = jnp.zeros_like(dk_ref)
            dv_ref[...] = jnp.zeros_like(dv_ref)

        k, v = k_ref[...], v_ref[...]
        lse_all = l_ref[...]
        for g in range(4):
            cols = slice(g * HEAD_B, (g + 1) * HEAD_B)
            q, dob = q_ref[:, cols], do_ref[:, cols]
            delta = jnp.sum(dob.astype(F32) * o_ref[:, cols].astype(F32), axis=-1, keepdims=True)
            s = lax.dot_general(q, k, (NT, ((), ())), preferred_element_type=F32)
            p = jnp.exp2(s - lse_all[:, g:g + 1])
            dp = lax.dot_general(dob, v, (NT, ((), ())), preferred_element_type=F32)
            ds = (p * (dp - delta)).astype(BF16)
            dq_ref[:, cols] = jnp.dot(ds, k, preferred_element_type=F32).astype(dq_ref.dtype)
            dk_ref[...] += lax.dot_general(ds, q, (TN, ((), ())), preferred_element_type=F32)
            dv_ref[...] += lax.dot_general(p.astype(BF16), dob, (TN, ((), ())), preferred_element_type=F32)

    return pl.pallas_call(
        body, out_shape=[jax.ShapeDtypeStruct((T, 2 * GW), BF16), jax.ShapeDtypeStruct((T, 2 * HEAD_B), F32),
                         jax.ShapeDtypeStruct((T, 2 * HEAD_B), F32)],
        grid=(2, T // QB_B),
        in_specs=[_bs((QB_B, GW), lambda kv, i: (i, kv)), _bs((T, HEAD_B), lambda kv, i: (0, k_col + kv)),
                  _bs((T, HEAD_B), lambda kv, i: (0, B_V // HEAD_B + kv)), _bs((QB_B, GW), lambda kv, i: (i, kv)),
                  _bs((None, QB_B, HEAD_B), lambda kv, i: (kv, i, 0)), _bs((QB_B, GW), lambda kv, i: (i, kv))] + _any_specs(len(deps)),
        out_specs=[_bs((QB_B, GW), lambda kv, i: (i, kv)), _bs((T, HEAD_B), lambda kv, i: (0, kv)),
                   _bs((T, HEAD_B), lambda kv, i: (0, kv))],
        compiler_params=_params("parallel", "arbitrary"), name="b_bwd")(qn, kn, proj, o, lse, do, *deps)


def _local_step(x, target, small, get_w, put_g, deps=(), prefetch_w=lambda name, after: [], take_rider=lambda steps, after: None):
    T, D = x.shape
    gs = {}

    bias = _bias_tiles(small["rel_bias"])
    cos, sin = _rope_tables(T)
    (x1, h2), ffn1_saved = _ffn_fwd("ffn1", x, small["ffn1_norm"], lambda name, after: get_w(name, [after, bias, cos, sin]), deps,
                                    tail_ins=[small["mix_norm"]], tail_fn=lambda y, g: (y, _norm_fwd(y, g)), tail_outs=(F32, BF16),
                                    ahead_of_up=lambda after: prefetch_w("ffn1_w2", after),
                                    ahead_of_down=lambda after: prefetch_w("w_in", after))
    w_in = get_w("w_in", h2)
    nq = w_in.shape[2]
    tpq = nq // WIDTH_A

    def proj_tile(j, k):
        c = j * tpq + k
        return jnp.where(c < 3 * len(DILATIONS), (c % 3) * 3 + c // 3, c)

    proj = _mm("mix_in", (4, tpq),
               [(h2, _resident((T, D), lambda j, k: (0, 0)), w_in, _bs((None, D, WIDTH_A), lambda j, k: (j, 0, k)))],
               jax.ShapeDtypeStruct((T, IN_WIDTH), BF16), _bs((T, WIDTH_A), lambda j, k: (0, proj_tile(j, k))), NN)

    a_views = [_group_view(proj, grp, d) for grp, d in enumerate(DILATIONS)]
    a_outs, a_lses = [], []
    for grp, d in enumerate(DILATIONS):
        o, l = _dil_fwd(a_views[grp], bias[grp], d)
        a_outs.append(o)
        a_lses.append(l)
    o_a = _combine_fwd(a_outs, a_lses)

    qk_gain = jnp.concatenate([jnp.tile(small["q_norm"] * QK_SCALE_LOG2, (8, 1)), jnp.tile(small["k_norm"], (2, 1))])[:, None, :]
    qkn = _qk_fwd("b_qknorm", proj, B_Q, 10, qk_gain, cos, sin, deps=prefetch_w("w_branch_a", proj))
    qn, kn, k_col = qkn, qkn, 8
    o_b, lse_b = _gqa_fwd(qn, kn, proj, k_col)
    ahead = prefetch_w("ffn2_w1", o_b)

    wa, wb, wo = get_w("w_branch_a", o_b), get_w("w_branch_b", o_b), get_w("w_out", o_b)
    bg_a, bg_b = small["b_gate"][:, :D], small["b_gate"][:, D:]
    n_a = wa.shape[0]

    def merge_out(oa_ref, ob_ref, ga_ref, gb_ref, x1_ref, wa_ref, wb_ref, wo_ref, ba_ref, bb_ref, g2_ref, *rest):
        ta_ref, tb_ref, mg_ref, x2_ref, hn_ref = rest[-5:]
        oa = oa_ref[...]
        ta = jnp.concatenate([jnp.dot(oa, wa_ref[j], preferred_element_type=F32) for j in range(n_a)], axis=1)
        tb = jnp.dot(ob_ref[...], wb_ref[...], preferred_element_type=F32)
        sa = _sigmoid(ga_ref[...].astype(F32) + ba_ref[...])
        sb = _sigmoid(gb_ref[...].astype(F32) + bb_ref[...])
        merged = (sa * ta + sb * tb).astype(BF16)
        ta_ref[...], tb_ref[...], mg_ref[...] = ta.astype(BF16), tb.astype(BF16), merged
        y = x1_ref[...] + jnp.dot(merged, wo_ref[...], preferred_element_type=F32)
        x2_ref[...] = y
        hn_ref[...] = _norm_fwd(y, g2_ref[...]).astype(BF16)

    row = _bs((512, D), lambda i: (i, 0))
    gate_specs = [_bs((512, D), lambda i: (i, G_A // D)), _bs((512, D), lambda i: (i, G_B // D))]
    whole2, whole3 = (lambda i: (0, 0)), (lambda i: (0, 0, 0))
    vec = _bs((1, D), whole2)
    t_a, t_b, merged, x2, hn2 = pl.pallas_call(
        merge_out, out_shape=[jax.ShapeDtypeStruct((T, D), BF16)] * 3 + [jax.ShapeDtypeStruct((T, D), F32), jax.ShapeDtypeStruct((T, D), BF16)],
        grid=(T // 512,),
        in_specs=[_bs((512, WIDTH_A), lambda i: (i, 0)), row] + gate_specs + [row, _resident(wa.shape, whole3), _resident((D, D), whole2),
                                                                                _resident((D, D), whole2), vec, vec, vec]
        + _any_specs(len(ahead)),
        out_specs=[row] * 5, compiler_params=_params("parallel"), name="mix_merge_out")(
            o_a, o_b, proj, proj, x1, wa, wb, wo, bg_a, bg_b, small["ffn2_norm"], *ahead)

    def head(xv, g, tv):
        r = _rstd(xv)
        xh = xv * r
        e = xh * g - tv
        dy = e * (1.0 / D)
        dxh = dy * g
        dx = r * (dxh - xh * jnp.mean(dxh * xh, axis=-1, keepdims=True))
        return dx, 0.5 * dx, _colsum(e * e) * (0.5 / D), _colsum(dy * xh)

    (dx3, dx3_half, loss_cols, g_final), ffn2_saved = _ffn_fwd(
        "ffn2", x2, small["ffn2_norm"], get_w, h=hn2, tail_ins=[small["final_norm"].reshape(1, D), target], tail_fn=head,
        tail_outs=(F32, BF16), tail_reds=(D, D))
    gs["final_norm"] = g_final.reshape(D)

    dx2, _, dmix, gs["ffn2_norm"] = _ffn_bwd("ffn2", x2, small["ffn2_norm"], get_w, put_g, ffn2_saved, dx3, dx3_half,
                                             also_bf16=True)
    g_out = _mm_wgrad("mix_bwd_dwout", merged, dmix, a_cols=D // 4, b_cols=None, tm=256, tn=512, J=4).reshape(D, D)

    def merge_out_bwd(dx_ref, ta_ref, tb_ref, ga_ref, gb_ref, wa_ref, wb_ref, wo_ref, ba_ref, bb_ref,
                      dta_ref, dtb_ref, dga_ref, dgb_ref, doa_ref, dob_ref, dba_ref, dbb_ref):
        dm = lax.dot_general(dx_ref[...], wo_ref[...], (NT, ((), ())), preferred_element_type=F32)
        ta, tb = ta_ref[...].astype(F32), tb_ref[...].astype(F32)
        sa = _sigmoid(ga_ref[...].astype(F32) + ba_ref[...])
        sb = _sigmoid(gb_ref[...].astype(F32) + bb_ref[...])
        dga, dgb = dm * ta * sa * (1.0 - sa), dm * tb * sb * (1.0 - sb)
        dta, dtb = (dm * sa).astype(BF16), (dm * sb).astype(BF16)
        dta_ref[...], dtb_ref[...] = dta, dtb
        dga_ref[...], dgb_ref[...] = dga.astype(BF16), dgb.astype(BF16)
        w = wa_ref.shape[2]
        doa = sum(lax.dot_general(dta[:, j * w:(j + 1) * w], wa_ref[j], (NT, ((), ())), preferred_element_type=F32) for j in range(n_a))
        doa_ref[...] = doa.astype(BF16)
        dob_ref[...] = lax.dot_general(dtb, wb_ref[...], (NT, ((), ())), preferred_element_type=F32).astype(BF16)

        @pl.when(pl.program_id(0) == 0)
        def _():
            dba_ref[...] = jnp.zeros_like(dba_ref)
            dbb_ref[...] = jnp.zeros_like(dbb_ref)
        dba_ref[...] += _colsum(dga)
        dbb_ref[...] += _colsum(dgb)

    rowb = _bs((256, D), lambda i: (i, 0))
    gate_specs = [_bs((256, D), lambda i: (i, G_A // D)), _bs((256, D), lambda i: (i, G_B // D))]
    dta, dtb, dga, dgb, do_a, do_b, dba, dbb = pl.pallas_call(
        merge_out_bwd,
        out_shape=[jax.ShapeDtypeStruct((T, D), BF16)] * 4 + [jax.ShapeDtypeStruct((T, WIDTH_A), BF16), jax.ShapeDtypeStruct((T, D), BF16)]
        + [jax.ShapeDtypeStruct((1, D), F32)] * 2,
        grid=(T // 256,),
        in_specs=[rowb, rowb, rowb] + gate_specs + [_resident(wa.shape, whole3), _resident((D, D), whole2), _resident((D, D), whole2), vec, vec],
        out_specs=[rowb] * 4 + [_bs((256, WIDTH_A), lambda i: (i, 0)), rowb, vec, vec],
        compiler_params=_params("arbitrary"), name="mix_merge_out_bwd")(dmix, t_a, t_b, proj, proj, wa, wb, wo, bg_a, bg_b)
    gs["b_gate"] = jnp.concatenate([dba, dbb], axis=1)

    g_a = _mm_wgrad("mix_bwd_dwa", o_a, dta, a_cols=None, b_cols=D // 4, tm=WIDTH_A, tn=256, J=4)
    g_b = _mm_wgrad("mix_bwd_dwb", o_b, dtb, a_cols=D // 4, b_cols=None, tm=256, tn=512, J=4).reshape(D, D)
    deps = put_g({"w_out": g_out, "w_branch_a": g_a, "w_branch_b": g_b})

    dqn, dkn, dv_b = _gqa_bwd(qn, kn, proj, o_b, lse_b, do_b, deps, k_col)
    dq_b, gs["q_norm"] = _qk_bwd("b_bwd_qnorm", dqn, proj, B_Q, 8, small["q_norm"], cos, sin, in_scale=HEAD_B ** -0.5)
    dk_b, gs["k_norm"] = _qk_bwd("b_bwd_knorm", dkn, proj, B_K, 2, small["k_norm"], cos, sin, in_scale=1.0 / LOG2_E)

    do_groups, c_groups = _combine_bwd(do_a, a_outs, a_lses)
    dqs, dks, dvs, dbs = [], [], [], []
    for grp, d in enumerate(DILATIONS):
        dq, dk, dv, db = _dil_bwd(a_views[grp], bias[grp], do_groups[grp], a_lses[grp], c_groups[grp], d)
        dqs.append(dq), dks.append(dk), dvs.append(dv), dbs.append(db)
    gs["rel_bias"] = _bias_grad(jnp.stack(dbs))

    dproj = _assemble_dproj([dqs, dks, dvs], dq_b, dk_b, dv_b, dga, dgb)
    nq = w_in.shape[2]
    g_in = _mm("mix_bwd_dwin", (4, tpq),
               [(h2, _resident((T, D), lambda j, k: (0, 0)), dproj, _bs((T, WIDTH_A), lambda j, k: (0, j * tpq + k)))],
               jax.ShapeDtypeStruct((4, D, nq), BF16), _bs((None, D, WIDTH_A), lambda j, k: (j, 0, k)), TN)
    deps = put_g({"w_in": g_in})
    dx1, dx1_half, gs["mix_norm"] = _dh_norm_bwd(
        "mix_bwd_dh", 256,
        [(dproj, _bs((256, nq), lambda i, j=j: (i, j)), w_in, _resident((None, D, nq), lambda i, j=j: (j, 0, 0))) for j in range(4)],
        NT, x1, small["mix_norm"], dx2, deps)

    dx0, _, gs["ffn1_norm"] = _ffn_bwd("ffn1", x, small["ffn1_norm"], get_w, put_g, ffn1_saved, dx1, dx1_half, last=True,
                                       take_rider=take_rider)
    return loss_cols, dx0, gs


def _position():
    return lax.axis_index("x"), lax.axis_index("y"), lax.axis_index("c")


def _any_specs(n):
    return [pl.BlockSpec(memory_space=pl.ANY)] * n


HBM_SPEC = pl.BlockSpec(memory_space=pltpu.HBM)
SEM_SPEC = pl.BlockSpec(memory_space=pltpu.SEMAPHORE)
DATAFLOW_EFFECT = pltpu.SideEffectType.DATAFLOW_SIDE_EFFECTING
N_PEER_CHIPS = 3
LANES = 128


def _quarter_copies(srcs, lands, send_sems, recv_sems, mode):
    x, y, c = _position()
    me = 2 * x + y
    peers = [(1 - x, y, c), (x, 1 - y, c), (1 - x, 1 - y, c)]
    copies = []
    for src, land, send, recv in zip(srcs, lands, send_sems, recv_sems):
        if mode == "sibling":
            copies.append(pltpu.make_async_remote_copy(src_ref=src, dst_ref=land, send_sem=send.at[0], recv_sem=recv.at[0],
                                                       device_id=(x, y, 1 - c), device_id_type=MESH))
            continue
        if mode == "fill":
            half = land.shape[1] // 2
            for p, (px, py, _) in enumerate(peers):
                part = land.at[2 * px + py, pl.ds(c * half, half)]
                copies.append(pltpu.make_async_remote_copy(src_ref=part, dst_ref=part, send_sem=send.at[p], recv_sem=recv.at[p],
                                                           device_id=(x, y, 1 - c), device_id_type=MESH))
            continue
        scatter = mode == "scatter"
        half = land.shape[1] // 2
        mine = land.at[me, pl.ds(c * half, half)]
        for p, (px, py, pc) in enumerate(peers):
            copies.append(pltpu.make_async_remote_copy(
                src_ref=src.at[2 * px + py] if scatter else mine, dst_ref=land.at[me] if scatter else mine,
                send_sem=send.at[p], recv_sem=recv.at[p], device_id=(px, py, pc), device_id_type=MESH))
    return copies


def _fill_from_sibling(name, stacks):
    n = len(stacks)

    def body(*refs):
        outs = refs[n:2 * n]
        send_sems, recv_sems = refs[2 * n:]
        x, y, c = _position()
        copies = []
        for i, ref in enumerate(outs):
            half = ref.shape[1] // 2
            rows = pl.ds(c * half, half)
            for p, k in enumerate((2 * (1 - x) + y, 2 * x + (1 - y), 2 * (1 - x) + (1 - y))):
                cp = pltpu.make_async_remote_copy(ref.at[k, rows], ref.at[k, rows], send_sems.at[3 * i + p], recv_sems.at[3 * i + p],
                                                  device_id=(x, y, 1 - c), device_id_type=MESH)
                cp.start()
                copies.append(cp)
        for cp in copies:
            cp.wait()

    return pl.pallas_call(
        body, out_shape=[jax.ShapeDtypeStruct(s.shape, s.dtype) for s in stacks],
        in_specs=_any_specs(n), out_specs=_any_specs(n), input_output_aliases={i: i for i in range(n)},
        scratch_shapes=[pltpu.SemaphoreType.DMA((N_PEER_CHIPS * n,)), pltpu.SemaphoreType.DMA((N_PEER_CHIPS * n,))],
        compiler_params=pltpu.CompilerParams(has_side_effects=True), name=name)(*stacks)


def _exchange_start(name, srcs, lands, mode):
    n = len(lands)
    arrays = list(lands) if srcs is None else list(srcs) + list(lands)
    k = len(arrays)

    def body(*refs):
        land_refs = refs[k - n:k]
        send_sems, recv_sems = refs[k:k + n], refs[k + n:k + 2 * n]
        token = refs[2 * k + 2 * n]
        for cp in _quarter_copies(refs[:n], land_refs, send_sems, recv_sems, mode):
            cp.start()
        token[...] = jnp.zeros_like(token)

    sem = pltpu.SemaphoreType.DMA((N_PEER_CHIPS,))
    out_shape = [sem] * (2 * n) + [pltpu.HBM(a.shape, a.dtype) for a in arrays] + [jax.ShapeDtypeStruct((8, LANES), F32)]
    res = pl.pallas_call(
        body, name=name, out_shape=out_shape, in_specs=[HBM_SPEC] * k,
        out_specs=[SEM_SPEC] * (2 * n) + [HBM_SPEC] * k + [pl.BlockSpec(memory_space=pltpu.VMEM)],
        input_output_aliases={i: 2 * n + i for i in range(k)},
        compiler_params=pltpu.CompilerParams(has_side_effects=DATAFLOW_EFFECT),
    )(*[pltpu.with_memory_space_constraint(a, pltpu.HBM) for a in arrays])
    thru = res[2 * n:2 * n + k]
    return res[:n], res[n:2 * n], (None if srcs is None else thru[:n]), thru[k - n:], res[2 * n + k]


def _exchange_wait(name, srcs, lands, send_sems, recv_sems, after, mode):
    n = len(lands)
    arrays = list(lands) if srcs is None else list(srcs) + list(lands)
    k = len(arrays)
    after = list(after) if isinstance(after, (list, tuple)) else [after]

    def body(*refs):
        sends, recvs = refs[k:k + n], refs[k + n:k + 2 * n]
        for cp in _quarter_copies(refs[:n], refs[k - n:k], sends, recvs, mode):
            cp.wait_send()
            cp.wait_recv()

    res = pl.pallas_call(
        body, name=name, out_shape=[pltpu.HBM(a.shape, a.dtype) for a in arrays],
        in_specs=[HBM_SPEC] * k + [SEM_SPEC] * (2 * n) + _any_specs(len(after)),
        out_specs=[HBM_SPEC] * k, input_output_aliases={i: i for i in range(k)},
        compiler_params=pltpu.CompilerParams(has_side_effects=DATAFLOW_EFFECT),
    )(*arrays, *send_sems, *recv_sems, *after)
    return (None if srcs is None else res[:n]), res[k - n:]


def _scatter_and_forward(name, stacks, lands, old_srcs, old_lands, old_sends, old_recvs):
    n1, n0 = len(stacks), len(old_lands)
    sibling_lands = [lax.empty(a.shape, a.dtype) for a in old_lands]
    arrays = list(stacks) + list(lands) + list(old_srcs) + list(old_lands) + sibling_lands
    k, s = len(arrays), 2 * n1 + 2 * n0

    def body(*refs):
        new_srcs, new_lands = refs[:n1], refs[n1:2 * n1]
        was_srcs, landed, to_sibling = refs[2 * n1:2 * n1 + n0], refs[2 * n1 + n0:2 * n1 + 2 * n0], refs[2 * n1 + 2 * n0:k]
        was_sends, was_recvs = refs[k:k + n0], refs[k + n0:k + 2 * n0]
        sems = refs[k + 2 * n0:k + 2 * n0 + s]
        token = refs[k + 2 * n0 + s + k]
        for cp in _quarter_copies(new_srcs, new_lands, sems[:n1], sems[n1:2 * n1], "scatter"):
            cp.start()
        for cp in _quarter_copies(was_srcs, landed, was_sends, was_recvs, "scatter"):
            cp.wait_send()
            cp.wait_recv()
        for cp in _quarter_copies(landed, to_sibling, sems[2 * n1:2 * n1 + n0], sems[2 * n1 + n0:], "sibling"):
            cp.start()
        token[...] = jnp.zeros_like(token)

    sem = pltpu.SemaphoreType.DMA((N_PEER_CHIPS,))
    res = pl.pallas_call(
        body, name=name,
        out_shape=[sem] * s + [pltpu.HBM(a.shape, a.dtype) for a in arrays] + [jax.ShapeDtypeStruct((8, LANES), F32)],
        in_specs=[HBM_SPEC] * k + [SEM_SPEC] * (2 * n0),
        out_specs=[SEM_SPEC] * s + [HBM_SPEC] * k + [pl.BlockSpec(memory_space=pltpu.VMEM)],
        input_output_aliases={i: s + i for i in range(k)},
        compiler_params=pltpu.CompilerParams(has_side_effects=DATAFLOW_EFFECT),
    )(*[pltpu.with_memory_space_constraint(a, pltpu.HBM) for a in arrays], *old_sends, *old_recvs)
    thru = res[s:s + k]
    scatter = (res[:n1], res[n1:2 * n1], thru[:n1], thru[n1:2 * n1])
    sibling = (res[2 * n1:2 * n1 + n0], res[2 * n1 + n0:s], thru[2 * n1 + n0:2 * n1 + 2 * n0], thru[2 * n1 + 2 * n0:])
    return scatter, sibling, res[s + k]


def _own_slots(name, srcs, from_stack=False):
    n = len(srcs)
    me = (2 * lax.axis_index("x") + lax.axis_index("y")).astype(jnp.int32).reshape(1)

    def body(me_ref, *refs):
        for x_ref, o_ref in zip(refs[:n], refs[n:]):
            o_ref[...] = x_ref[...].astype(o_ref.dtype)

    in_specs, out_specs, out_shape = [], [], []
    for src in srcs:
        R, C = src.shape[-2:]
        in_specs.append(pl.BlockSpec((None, R // 2, C), lambda i, me_ref: (me_ref[0], i, 0)) if from_stack
                        else pl.BlockSpec((R // 2, C), lambda i, me_ref: (i, 0)))
        out_specs.append(pl.BlockSpec((None, R // 2, C), lambda i, me_ref: (me_ref[0], i, 0)))
        out_shape.append(jax.ShapeDtypeStruct((4, R, C), BF16))
    grid_spec = pltpu.PrefetchScalarGridSpec(num_scalar_prefetch=1, grid=(2,), in_specs=in_specs, out_specs=out_specs)
    return pl.pallas_call(body, out_shape=out_shape, grid_spec=grid_spec, compiler_params=_params("parallel"), name=name)(me, *srcs)


def _allreduce_small(buf):
    R, C = buf.shape
    flips = [(fx, fy, fc) for fx in (0, 1) for fy in (0, 1) for fc in (0, 1)][1:]

    def body(in_ref, out_ref, land_ref, send_sems, recv_sems):
        x, y, c = _position()
        me = 4 * x + 2 * y + c
        copies = []
        for k, (fx, fy, fc) in enumerate(flips):
            px, py, pc = (1 - x if fx else x), (1 - y if fy else y), (1 - c if fc else c)
            cp = pltpu.make_async_remote_copy(in_ref, land_ref.at[me], send_sems.at[k], recv_sems.at[k],
                                              device_id=(px, py, pc), device_id_type=MESH)
            cp.start()
            copies.append(cp)
        land_ref[me] = in_ref[...]
        for cp in copies:
            cp.wait()
        acc = land_ref[0]
        for k in range(1, 8):
            acc = acc + land_ref[k]
        out_ref[...] = acc

    return pl.pallas_call(
        body, out_shape=jax.ShapeDtypeStruct((R, C), F32),
        in_specs=[pl.BlockSpec(memory_space=pltpu.VMEM)], out_specs=pl.BlockSpec(memory_space=pltpu.VMEM),
        scratch_shapes=[pltpu.VMEM((8, R, C), F32), pltpu.SemaphoreType.DMA((7,)), pltpu.SemaphoreType.DMA((7,))],
        compiler_params=pltpu.CompilerParams(has_side_effects=True), name="allreduce_small")(buf)


def _adamw_math(w, g, m, v):
    m2 = ADAM_B1 * m + (1.0 - ADAM_B1) * g
    v2 = ADAM_B2 * v + (1.0 - ADAM_B2) * (g * g)
    m_hat = m2 / (1.0 - ADAM_B1 ** ADAM_STEP)
    v_hat = v2 / (1.0 - ADAM_B2 ** ADAM_STEP)
    delta = -ADAM_LR * (m_hat / (jnp.sqrt(v_hat) + ADAM_EPS) + ADAM_WD * w)
    return delta, m2, v2


def _adamw_from_partials(wv, mv, vv, *parts):
    def four(a, b, c, d):
        return ((a.astype(F32) + b.astype(F32)) + c.astype(F32)) + d.astype(F32)

    g = four(*parts[:4]) + four(*parts[4:])
    return (g,) + _adamw_math(wv, g, mv, vv)


def _adamw_big(name, w, m, v, mine, theirs):
    R, C = w.shape
    rows = 256 if R % 256 == 0 else R // 2
    nrb = R // rows
    slots = [_tiled(s.reshape(4 * R, C), None, 0, k * nrb) for s in (mine, theirs) for k in range(4)]
    return _ew(name, _adamw_from_partials, [_tiled(w), _tiled(m), _tiled(v)] + slots, [(F32, C)] * 4, n_rows=R, rows=rows)


def _adamw_rider(w, m, v, mine, theirs, steps, deliver):
    R, C = w.shape
    fits = [nb for nb in range(1, steps + 1) if R % nb == 0 and (R // nb) % 16 == 0]
    if not fits:
        return None
    nb = fits[-1]
    rows = R // nb

    def blocks(first):
        return pl.BlockSpec((rows, C), lambda *g: (first + jnp.minimum(g[0], nb - 1), 0))

    flat = [s.reshape(4 * R, C) for s in (mine, theirs)]
    return dict(operands=[w, m, v] + [f for f in flat for _ in range(4)],
                in_specs=[blocks(0)] * 3 + [blocks(k * nb) for _ in flat for k in range(4)],
                out_shape=[jax.ShapeDtypeStruct((R, C), F32)] * 4, out_specs=[blocks(0)] * 4,
                n_blocks=nb, fn=_adamw_from_partials, deliver=lambda outs: deliver(*outs))


BIG = ("ffn1_w1", "ffn1_w3", "ffn1_w2", "w_in", "w_branch_a", "w_branch_b", "w_out", "ffn2_w1", "ffn2_w3", "ffn2_w2")
SMALL = ("ffn1_norm", "mix_norm", "b_gate", "q_norm", "k_norm", "rel_bias", "ffn2_norm", "final_norm")
ORDER = ("ffn1_norm", "ffn1_w1", "ffn1_w3", "ffn1_w2", "mix_norm", "w_in", "b_gate", "q_norm", "k_norm", "rel_bias",
         "w_branch_a", "w_branch_b", "w_out", "ffn2_norm", "ffn2_w1", "ffn2_w3", "ffn2_w2", "final_norm")
TRANSPOSED = ("ffn1_w1", "ffn1_w3", "ffn2_w1", "ffn2_w3")
SIBLING_LAG = 2
LONG_HOST_STEPS = 8
GATHER_GROUPS = (("ffn1_w1", "ffn1_w3"), ("ffn1_w2",), ("w_in",), ("w_branch_a", "w_branch_b", "w_out"),
                 ("ffn2_w1", "ffn2_w3", "ffn2_w2"))


def _pack_small(d):
    rows = []
    for n in SMALL:
        flat = d[n].reshape(-1)
        pad = (-flat.shape[0]) % LANES
        rows.append(jnp.pad(flat, (0, pad)).reshape(-1, LANES))
    buf = jnp.concatenate(rows, axis=0)
    return jnp.pad(buf, ((0, (-buf.shape[0]) % 8), (0, 0)))


def _unpack_small(buf, like):
    out, r = {}, 0
    for n in SMALL:
        size = like[n].size
        nr = -(-size // LANES)
        out[n] = buf[r:r + nr].reshape(-1)[:size].reshape(like[n].shape)
        r += nr
    return out


def kernel(x, ffn1_norm, ffn1_w1, ffn1_w3, ffn1_w2, mix_norm, w_in, b_gate, q_norm, k_norm, rel_bias, w_branch_a, w_branch_b, w_out, ffn2_norm, ffn2_w1, ffn2_w3, ffn2_w2, final_norm, loss_target, m_ffn1_norm, m_ffn1_w1, m_ffn1_w3, m_ffn1_w2, m_mix_norm, m_w_in, m_b_gate, m_q_norm, m_k_norm, m_rel_bias, m_w_branch_a, m_w_branch_b, m_w_out, m_ffn2_norm, m_ffn2_w1, m_ffn2_w3, m_ffn2_w2, m_final_norm, v_ffn1_norm, v_ffn1_w1, v_ffn1_w3, v_ffn1_w2, v_mix_norm, v_w_in, v_b_gate, v_q_norm, v_k_norm, v_rel_bias, v_w_branch_a, v_w_branch_b, v_w_out, v_ffn2_norm, v_ffn2_w1, v_ffn2_w3, v_ffn2_w2, v_final_norm):
    given = dict(locals())
    w = {n: given[n] for n in ORDER}
    m = {n: given["m_" + n] for n in ORDER}
    v = {n: given["v_" + n] for n in ORDER}
    T, D = x.shape[1], x.shape[2]

    def stored(a, n):
        a = a.reshape(a.shape[1:])
        return a.T if n in TRANSPOSED else a

    def returned(a, n):
        return (a.T if n in TRANSPOSED else a).reshape(w[n].shape)

    quarter = {n: stored(w[n], n) for n in BIG}
    send, recv, _, land_thru, token = _exchange_start(
        "gather_start", None, _own_slots("own_weights", [quarter[n] for n in BIG]), "gather")
    index = {n: i for i, n in enumerate(BIG)}
    ready, filling = {}, {}

    def landed_halves(group, after):
        ids = [index[n] for n in group]
        return _exchange_wait("gather_wait_" + group[0], None, [land_thru[i] for i in ids],
                              [send[i] for i in ids], [recv[i] for i in ids], after, "gather")[1]

    def prefetch_w(name, after):
        group = next(g for g in GATHER_GROUPS if name in g)
        started = _exchange_start("fill_start_" + group[0], None, landed_halves(group, after), "fill")
        filling[group] = started
        return [started[4]]

    def get_w(name, after):
        if name not in ready:
            group = next(g for g in GATHER_GROUPS if name in g)
            if group in filling:
                f_send, f_recv, _, thru, _ = filling[group]
                stacks = _exchange_wait("fill_wait_" + group[0], None, thru, f_send, f_recv, after, "fill")[1]
            else:
                stacks = _fill_from_sibling("gather_fill_" + group[0], landed_halves(group, after))
            for n, st in zip(group, stacks):
                ready[n] = st.reshape(D, D) if n in ("w_branch_b", "w_out") else st
        return ready[name]

    scattered, forwarded = [], []

    def forward_oldest(after):
        names, s_sem, r_sem, srcs, lands = scattered.pop(0)
        _, landed = _exchange_wait("scatter_wait_" + names[0], srcs, lands, s_sem, r_sem, after, "scatter")
        started = _exchange_start("sibling_start_" + names[0], landed, [lax.empty(a.shape, a.dtype) for a in landed], "sibling")
        forwarded.append((names,) + tuple(started[:4]))
        return started[4]

    def put_g(grads):
        names = list(grads)
        stacks = [grads[n].reshape((4,) + quarter[n].shape) for n in names]
        lands = _own_slots("own_grad_" + names[0], stacks, from_stack=True)
        if len(scattered) < SIBLING_LAG:
            started = _exchange_start("scatter_start_" + names[0], stacks, lands, "scatter")
            scattered.append((names,) + tuple(started[:4]))
            return [started[4]]
        old_names, s_sem, r_sem, old_srcs, old_lands = scattered.pop(0)
        scatter, sibling, token = _scatter_and_forward("scatter_start_" + names[0], stacks, lands, old_srcs, old_lands, s_sem, r_sem)
        scattered.append((names,) + scatter)
        forwarded.append((old_names,) + sibling)
        return [token]

    grads, deltas, new_m, new_v = {}, {}, {}, {}
    arrived, riding = {}, set()

    def partials(gi, after):
        if gi not in arrived:
            names, s_sem, r_sem, srcs, lands = forwarded[gi]
            arrived[gi] = _exchange_wait("sibling_wait_" + names[0], srcs, lands, s_sem, r_sem, after, "sibling")
        return arrived[gi]

    def deliver_to(n):
        def deliver(*res):
            grads[n], deltas[n], new_m[n], new_v[n] = [returned(r, n) for r in res]
        return deliver

    def take_rider(steps, after):
        waiting = [(quarter[n].size, gi, k, n) for gi, entry in enumerate(forwarded) for k, n in enumerate(entry[0]) if n not in riding]
        for _, gi, k, n in sorted(waiting, reverse=steps >= LONG_HOST_STEPS):
            mine, theirs = partials(gi, after)
            rider = _adamw_rider(quarter[n], stored(m[n], n), stored(v[n], n), mine[k], theirs[k], steps, deliver_to(n))
            if rider is not None:
                riding.add(n)
                return rider
        return None

    small = {n: w[n] for n in SMALL}
    packed = [_pack_small({n: d[n] for n in SMALL}) for d in (w, m, v)]
    loss_cols, grad_x, gs = _local_step(x.reshape(T, D), loss_target.reshape(T, D), small, get_w, put_g, deps=[token] + packed,
                                        prefetch_w=prefetch_w, take_rider=take_rider)

    after = grad_x
    while scattered:
        after = forward_oldest(after)
    for gi, entry in enumerate(forwarded):
        mine, theirs = partials(gi, after)
        for n, a, b in zip(entry[0], mine, theirs):
            if n not in riding:
                deliver_to(n)(*_adamw_big(f"adamw_{n}", quarter[n], stored(m[n], n), stored(v[n], n), a, b))

    gs = {n: gs[n].reshape(w[n].shape) for n in SMALL}
    packed_g = _pack_small(gs)
    n_small = packed_g.shape[0]
    summed = _allreduce_small(jnp.concatenate([packed_g, loss_cols.reshape(-1, LANES)], axis=0))
    g_small, loss = summed[:n_small], jnp.sum(summed[n_small:])
    R = g_small.shape[0]
    res = _ew("adamw_small", lambda wv, mv, vv, g: (g,) + _adamw_math(wv, g, mv, vv),
              [_tiled(packed[0]), _tiled(packed[1]), _tiled(packed[2]), _tiled(g_small)], [(F32, LANES)] * 4, n_rows=R, rows=R)
    for d, buf in zip((grads, deltas, new_m, new_v), res):
        d.update(_unpack_small(buf, w))

    return (loss, grad_x.reshape(x.shape), *[grads[n] for n in ORDER], *[deltas[n] for n in ORDER],
            *[new_m[n] for n in ORDER], *[new_v[n] for n in ORDER])
```

```python
import functools
import math

import numpy as np
import jax
import jax.numpy as jnp
from jax import lax
from jax.experimental import pallas as pl
from jax.experimental.pallas import tpu as pltpu

F32 = jnp.float32
BF16 = jnp.bfloat16
MESH = pl.DeviceIdType.MESH

NEG_INF = -1e30
EPS = 1e-6
GRID_W = 64
ROPE_THETA = 10000.0
DILATIONS = (1, 4, 16)
BAND_HALF = 64
HEAD_A = 64
HEADS_A = 8
WIDTH_A = HEADS_A * HEAD_A
HEAD_B = 128
LOG2_E = math.log2(math.e)
QK_SCALE_LOG2 = HEAD_B ** -0.5 * LOG2_E
N_BUCKETS = 32
MAX_DISTANCE = 1024
ADAM_LR, ADAM_B1, ADAM_B2, ADAM_EPS, ADAM_WD, ADAM_STEP = 0.001, 0.9, 0.999, 1e-08, 0.01, 10

B_Q, B_K, B_V = 4608, 5632, 5888
G_A, G_B = 6144, 7168
IN_WIDTH = 8192

VMEM_LIMIT_BYTES = 56 * 1024 * 1024
QB_A = 128
QB_B = 256


def _params(*sem):
    return pltpu.CompilerParams(dimension_semantics=sem, vmem_limit_bytes=VMEM_LIMIT_BYTES)


def _bs(shape, fn):
    return pl.BlockSpec(shape, fn)


def _resident(shape, fn):
    return pl.BlockSpec(shape, fn, pipeline_mode=pl.Buffered(1))


def _mm(name, grid, pairs, out_shape, out_spec, dims, *, extras=(), epilogue=None, deps=(), reds=(), rider=None):
    n_pairs, n_extra, n_deps = len(pairs), len(extras), len(deps)
    operands = [p[0] for p in pairs] + [p[2] for p in pairs] + [e[0] for e in extras] + list(deps)
    in_specs = [p[1] for p in pairs] + [p[3] for p in pairs] + [e[1] for e in extras] + _any_specs(n_deps)
    single = not isinstance(out_shape, (list, tuple))
    out_shapes = [out_shape] if single else list(out_shape)
    out_specs = [out_spec] if single else list(out_spec)
    n_out = len(out_shapes)
    out_shapes += [jax.ShapeDtypeStruct((1, w), F32) for w in reds]
    out_specs += [_bs((1, w), lambda *_: (0, 0)) for w in reds]
    n_rin = 0
    if rider is not None:
        assert rider["n_blocks"] <= grid[0]
        n_rin = len(rider["operands"])
        operands += list(rider["operands"])
        in_specs += list(rider["in_specs"])
        out_shapes += list(rider["out_shape"])
        out_specs += list(rider["out_specs"])

    def body(*refs):
        a_refs, b_refs = refs[:n_pairs], refs[n_pairs:2 * n_pairs]
        e_refs = refs[2 * n_pairs:2 * n_pairs + n_extra]
        o_refs = refs[2 * n_pairs + n_extra + n_deps + n_rin:]
        if rider is not None:
            r_in = refs[2 * n_pairs + n_extra + n_deps:2 * n_pairs + n_extra + n_deps + n_rin]
            r_out = o_refs[n_out + len(reds):]

            @pl.when(pl.program_id(0) < rider["n_blocks"])
            def _():
                for ref, val in zip(r_out, rider["fn"](*[r[...] for r in r_in])):
                    ref[...] = val.astype(ref.dtype)
        acc = None
        for a_ref, b_ref in zip(a_refs, b_refs):
            t = lax.dot_general(a_ref[...], b_ref[...], (dims, ((), ())), preferred_element_type=F32)
            acc = t if acc is None else acc + t
        vals = acc if epilogue is None else epilogue(acc, *[e[...] for e in e_refs])
        if not isinstance(vals, (list, tuple)):
            vals = (vals,)
        for o_ref, v in zip(o_refs[:n_out], vals[:n_out]):
            o_ref[...] = v.astype(o_ref.dtype)
        if reds:
            first = functools.reduce(jnp.logical_and, [pl.program_id(ax) == 0 for ax in range(len(grid))])
            for r_ref, v in zip(o_refs[n_out:], vals[n_out:]):
                @pl.when(first)
                def _(r_ref=r_ref):
                    r_ref[...] = jnp.zeros_like(r_ref)
                r_ref[...] += v

    sem = ["arbitrary" if (reds or rider is not None) else "parallel"] * len(grid)
    res = pl.pallas_call(
        body, out_shape=out_shapes, grid=grid, in_specs=in_specs, out_specs=out_specs,
        compiler_params=_params(*sem), name=name)(*operands)
    if rider is not None:
        rider["deliver"](res[n_out + len(reds):])
        res = res[:n_out + len(reds)]
    return res[0] if (single and not reds) else res


NN = ((1,), (0,))
NT = ((1,), (1,))
TN = ((0,), (0,))


def _mm_wgrad(name, a, b, *, a_cols, b_cols, tm, tn, J, deps=(), rider=None):
    def pick(arr, cols, t):
        if arr.ndim == 3:
            T, c = arr.shape[1], arr.shape[2]
            t = min(t, c)
            return T, c, t, (lambda sel: _bs((None, T, t), lambda j, i, k: (j, 0, sel(i, k))))
        T = arr.shape[0]
        c = arr.shape[1] if cols is None else cols
        t = min(t, c)
        per = c // t
        if cols is None:
            if per == 1:
                return T, c, t, (lambda sel: _resident((T, t), lambda j, i, k: (0, 0)))
            return T, c, t, (lambda sel: _bs((T, t), lambda j, i, k: (0, sel(i, k))))
        return T, c, t, (lambda sel: _bs((T, t), lambda j, i, k: (0, j * per + sel(i, k))))
    _, ca, tm, mk_a = pick(a, a_cols, tm)
    _, cb, tn, mk_b = pick(b, b_cols, tn)
    return _mm(name, (J, ca // tm, cb // tn),
               [(a, mk_a(lambda i, k: i), b, mk_b(lambda i, k: k))],
               jax.ShapeDtypeStruct((J, ca, cb), BF16), _bs((None, tm, tn), lambda j, i, k: (j, i, k)), TN, deps=deps, rider=rider)


def _tiled(arr, width=None, col=0, rowblk=0):
    return ("t", arr, arr.shape[1] if width is None else width, col, rowblk)


def _table(arr):
    return ("f", arr)


def _whole(arr):
    return ("w", arr)


def _ew(name, fn, ins, outs, *, n_rows, rows, reds=(), ncols=1, deps=()):
    nrb = n_rows // rows
    n_deps = len(deps)
    operands, in_specs = [], []
    for spec in ins:
        if spec[0] == "t":
            _, arr, width, col, rowblk = spec
            step = 1 if ncols > 1 else 0
            in_specs.append(_bs((rows, width), lambda c, i, col=col, rowblk=rowblk, step=step: (rowblk + i, col + c * step)))
        elif spec[0] == "f":
            arr = spec[1]
            in_specs.append(_bs((rows, arr.shape[1]), lambda c, i: (i, 0)))
        else:
            arr = spec[1]
            nd = arr.ndim
            if nd == 3:
                in_specs.append(_bs((None,) + arr.shape[1:], lambda c, i: (c, 0, 0)))
            else:
                in_specs.append(_bs(arr.shape, lambda c, i, nd=nd: (0,) * nd))
        operands.append(arr)
    out_shapes = [jax.ShapeDtypeStruct((n_rows, ncols * w), dt) for dt, w in outs]
    out_specs = [_bs((rows, w), lambda c, i: (i, c)) for _, w in outs]
    out_shapes += [jax.ShapeDtypeStruct((ncols, 1, w), F32) for w in reds]
    out_specs += [_bs((None, 1, w), lambda c, i: (c, 0, 0)) for w in reds]
    n_in, n_out, n_red = len(ins), len(outs), len(reds)
    operands += list(deps)
    in_specs += _any_specs(n_deps)

    def body(*refs):
        vals = fn(*[r[...] for r in refs[:n_in]])
        if not isinstance(vals, (tuple, list)):
            vals = (vals,)
        o_refs = refs[n_in + n_deps:]
        for o_ref, v in zip(o_refs[:n_out], vals[:n_out]):
            o_ref[...] = v.astype(o_ref.dtype)
        if n_red:
            i = pl.program_id(1)
            for r_ref, v in zip(o_refs[n_out:], vals[n_out:]):
                @pl.when(i == 0)
                def _(r_ref=r_ref):
                    r_ref[...] = jnp.zeros_like(r_ref)
                r_ref[...] += v

    res = pl.pallas_call(
        body, out_shape=out_shapes, grid=(ncols, nrb), in_specs=in_specs, out_specs=out_specs,
        compiler_params=_params("parallel", "arbitrary" if n_red else "parallel"), name=name)(*operands)
    return res


def _colsum(v):
    return jnp.sum(v, axis=0, keepdims=True)


def _rstd(x):
    return lax.rsqrt(jnp.mean(x * x, axis=-1, keepdims=True) + EPS)


def _sigmoid(x):
    return 0.5 * jnp.tanh(0.5 * x) + 0.5


def _norm_fwd(x, g):
    return x * _rstd(x) * g


def _norm_bwd(x, g, dy):
    r = _rstd(x)
    xh = x * r
    dxh = dy * g
    dx = r * (dxh - xh * jnp.mean(dxh * xh, axis=-1, keepdims=True))
    return dx, dy * xh


def _row_spec(arr, rows):
    if arr.shape[0] == 1:
        return _bs(arr.shape, lambda i: (0, 0))
    return _bs((rows, arr.shape[1]), lambda i: (i, 0))


def _ffn_fwd(tag, x, gain, get_w, deps=(), *, h=None, tail_ins=(), tail_fn=None, tail_outs=(F32,), tail_reds=()):
    T, D = x.shape
    if h is None:
        (h,) = _ew(f"{tag}_norm", lambda xv, g: _norm_fwd(xv, g), [_tiled(x), _whole(gain)], [(BF16, D)], n_rows=T, rows=512,
                   deps=deps)
    w1, w3 = get_w(f"{tag}_w1", h), get_w(f"{tag}_w3", h)
    J, f, _ = w1.shape
    tm = 1024

    def up(h_ref, w1_ref, w3_ref, u_ref, g_ref, a_ref):
        hv = h_ref[...]
        u = lax.dot_general(hv, w1_ref[...], (NT, ((), ())), preferred_element_type=F32)
        g = lax.dot_general(hv, w3_ref[...], (NT, ((), ())), preferred_element_type=F32)
        u_ref[...] = u.astype(BF16)
        g_ref[...] = g.astype(BF16)
        a_ref[...] = (u * _sigmoid(u) * g).astype(BF16)

    slab = _bs((None, tm, f), lambda j, i: (j, i, 0))
    w_spec = _bs((None, f, D), lambda j, i: (j, 0, 0))
    u, g, a = pl.pallas_call(
        up, out_shape=[jax.ShapeDtypeStruct((J, T, f), BF16)] * 3, grid=(J, T // tm),
        in_specs=[_bs((tm, D), lambda j, i: (i, 0)), w_spec, w_spec], out_specs=[slab] * 3,
        compiler_params=_params("parallel", "parallel"), name=f"{tag}_up")(h, w1, w3)
    w2 = get_w(f"{tag}_w2", a)
    def tail(acc, xv, *rest):
        y = xv + 0.5 * acc
        return y if tail_fn is None else tail_fn(y, *rest)

    row = _bs((512, D), lambda i: (i, 0))
    res = _mm(f"{tag}_down", (T // 512,),
              [(a, _bs((None, 512, f), lambda i, j=j: (j, i, 0)), w2, _resident((None, f, D), lambda i, j=j: (j, 0, 0)))
               for j in range(J)],
              [jax.ShapeDtypeStruct((T, D), dt) for dt in tail_outs], [row] * len(tail_outs), NN,
              extras=[(x, row)] + [(t, _row_spec(t, 512)) for t in tail_ins], epilogue=tail, reds=tail_reds)
    return res, (h, u, g, a)


def _dh_norm_bwd(name, rows, pairs, dims, x, gain, dres, deps, also_bf16=False, rider=None):
    T, D = x.shape

    def epilogue(dh, xv, gv, dr):
        dx, dgr = _norm_bwd(xv, gv, dh)
        dx = dx + dr
        return (dx, 0.5 * dx) + ((dx,) if also_bf16 else ()) + (_colsum(dgr),)

    dts = [F32, BF16] + ([BF16] if also_bf16 else [])
    row = _bs((rows, D), lambda i: (i, 0))
    return _mm(name, (T // rows,), pairs, [jax.ShapeDtypeStruct((T, D), dt) for dt in dts], [row] * len(dts), dims,
               extras=[(x, row), (gain, _row_spec(gain, rows)), (dres, row)], epilogue=epilogue, deps=deps, reds=(D,), rider=rider)


def _ffn_bwd(tag, x, gain, get_w, put_g, saved, dy, dy_half, also_bf16=False, last=False, take_rider=lambda steps, after: None):
    h, u, g, a = saved
    T, D = x.shape
    w1, w3, w2 = [get_w(f"{tag}_{n}", dy_half) for n in ("w1", "w3", "w2")]
    J, f, _ = w1.shape
    dw2 = _mm_wgrad(f"{tag}_bwd_dw2", a, dy_half, a_cols=None, b_cols=None, tm=f, tn=D, J=J, rider=take_rider(J, dy_half))
    deps = put_g({f"{tag}_w2": dw2}) if last else []
    tm = 1024

    def up_bwd(dy_ref, w2_ref, u_ref, g_ref, *rest):
        du_ref, dg_ref = rest[-2:]
        da = lax.dot_general(dy_ref[...], w2_ref[...], (NT, ((), ())), preferred_element_type=F32)
        uv, gv = u_ref[...].astype(F32), g_ref[...].astype(F32)
        s = _sigmoid(uv)
        silu = uv * s
        du_ref[...] = (da * gv * (s + silu - silu * s)).astype(BF16)
        dg_ref[...] = (da * silu).astype(BF16)

    slab = _bs((None, tm, f), lambda j, i: (j, i, 0))
    du, dg = pl.pallas_call(
        up_bwd, out_shape=[jax.ShapeDtypeStruct((J, T, f), BF16)] * 2, grid=(J, T // tm),
        in_specs=[_bs((tm, D), lambda j, i: (i, 0)), _bs((None, f, D), lambda j, i: (j, 0, 0)), slab, slab] + _any_specs(len(deps)),
        out_specs=[slab] * 2, compiler_params=_params("parallel", "parallel"), name=f"{tag}_bwd_up")(dy_half, w2, u, g, *deps)
    dw1 = _mm_wgrad(f"{tag}_bwd_dw1", du, h, a_cols=None, b_cols=None, tm=f, tn=D, J=J)
    deps = put_g({f"{tag}_w1": dw1}) if last else []
    dw3 = _mm_wgrad(f"{tag}_bwd_dw3", dg, h, a_cols=None, b_cols=None, tm=f, tn=D, J=J, deps=deps)
    deps = put_g({f"{tag}_w3": dw3} if last else {f"{tag}_w2": dw2, f"{tag}_w1": dw1, f"{tag}_w3": dw3})
    pairs = []
    for j in range(J):
        a_spec = _bs((None, 256, f), lambda i, j=j: (j, i, 0))
        w_spec = _resident((None, f, D), lambda i, j=j: (j, 0, 0))
        pairs += [(du, a_spec, w1, w_spec), (dg, a_spec, w3, w_spec)]
    return _dh_norm_bwd(f"{tag}_bwd_dh", 256, pairs, NN, x, gain, dy, deps, also_bf16, rider=take_rider(T // 256, dw3))


def _t5_bucket(rel):
    n = N_BUCKETS // 2
    max_exact = n // 2
    ret = jnp.where(rel > 0, n, 0)
    a = jnp.abs(rel)
    af = jnp.maximum(a, 1).astype(F32)
    large = max_exact + (jnp.log(af / max_exact) / math.log(MAX_DISTANCE / max_exact) * (n - max_exact)).astype(jnp.int32)
    large = jnp.minimum(large, n - 1)
    return ret + jnp.where(a < max_exact, a, large)


WIN_A = QB_A + 2 * BAND_HALF
WIN_SHIFTS = (0, BAND_HALF, 2 * BAND_HALF)


def _window_variant(n, nblk):
    return jnp.where(n == 0, 0, jnp.where(n == nblk - 1, 2, 1))


def _window_start(n, nblk):
    return pl.multiple_of(jnp.clip(n * QB_A - BAND_HALF, 0, nblk * QB_A - WIN_A), BAND_HALF)


def _band_steps(xp=jnp):
    qi = xp.arange(QB_A, dtype=xp.int32)[None, :, None]
    kj = xp.arange(WIN_A, dtype=xp.int32)[None, None, :]
    return kj - qi - xp.asarray(WIN_SHIFTS, dtype=xp.int32)[:, None, None]


def _bias_tiles(rel_bias):
    wide = QB_A + 2 * WIN_SHIFTS[-1]
    qi = jnp.arange(QB_A, dtype=jnp.int32)[:, None]
    steps = jnp.arange(wide, dtype=jnp.int32)[None, :] - WIN_SHIFTS[-1] - qi
    buckets = jnp.stack([_t5_bucket(steps * d) for d in DILATIONS])
    inband = (jnp.abs(steps) <= BAND_HALF).astype(jnp.int32)
    n_heads = rel_bias.shape[1]

    def body(tab_ref, b_ref, m_ref, o_ref):
        hd = pl.program_id(0)
        bkt = b_ref[...]
        acc = jnp.zeros(bkt.shape, F32)
        for b in range(N_BUCKETS):
            acc = jnp.where(bkt == b, tab_ref[b, hd], acc)
        o_ref[...] = jnp.where(m_ref[...] > 0, acc, NEG_INF)

    base = pl.pallas_call(
        body, out_shape=jax.ShapeDtypeStruct((n_heads, QB_A, wide), F32), grid=(n_heads,),
        in_specs=[pl.BlockSpec(memory_space=pltpu.SMEM),
                  _bs((None, QB_A, wide), lambda hd: (hd // HEADS_A, 0, 0)),
                  _bs((QB_A, wide), lambda hd: (0, 0))],
        out_specs=_bs((None, QB_A, wide), lambda hd: (hd, 0, 0)),
        compiler_params=_params("parallel"), name="a_bias_tiles")(rel_bias, buckets, inband)
    base = base.reshape(len(DILATIONS), HEADS_A, QB_A, wide)
    return jnp.stack([base[..., WIN_SHIFTS[-1] - s:WIN_SHIFTS[-1] - s + WIN_A] for s in WIN_SHIFTS], axis=1)


def _bias_grad(dbias):
    steps = _band_steps(np)
    inband = np.abs(steps) <= BAND_HALF
    present = []
    for d in DILATIONS:
        rel = steps * d
        a = np.abs(rel)
        large = 8 + (np.log(np.maximum(a, 1) / 8.0) / math.log(MAX_DISTANCE / 8.0) * 8).astype(np.int64)
        bk = np.where(rel > 0, 16, 0) + np.where(a < 8, a, np.minimum(large, 15))
        present.append([sorted(set(bk[v][inband[v]].tolist())) for v in range(3)])
    buckets = jnp.stack([_t5_bucket(_band_steps() * d) for d in DILATIONS])
    n_heads = len(DILATIONS) * HEADS_A

    def body(b_ref, d_ref, o_ref):
        row = lax.broadcasted_iota(jnp.int32, (N_BUCKETS, n_heads), 0)
        col = lax.broadcasted_iota(jnp.int32, (N_BUCKETS, n_heads), 1)
        out = jnp.zeros((N_BUCKETS, n_heads), F32)
        for grp in range(len(DILATIONS)):
            for hh in range(HEADS_A):
                hd = grp * HEADS_A + hh
                for b in sorted(set(sum(present[grp], []))):
                    tot = jnp.zeros((), F32)
                    for v in range(3):
                        if b in present[grp][v]:
                            tot = tot + jnp.sum(jnp.where(b_ref[grp, v] == b, d_ref[grp, v, hh], 0.0))
                    out = jnp.where((row == b) & (col == hd), tot, out)
        o_ref[...] = out

    return pl.pallas_call(
        body, out_shape=jax.ShapeDtypeStruct((N_BUCKETS, n_heads), F32),
        compiler_params=pltpu.CompilerParams(vmem_limit_bytes=VMEM_LIMIT_BYTES), name="a_bias_grad")(buckets, dbias)


def _lane_is_second_head(shape):
    return lax.broadcasted_iota(jnp.int32, shape, len(shape) - 1) >= HEAD_A


VIEW_ROWS = 512


def _view_chunks():
    return [pltpu.VMEM((VIEW_ROWS, LANES), F32)] * (WIDTH_A // LANES)


def _rows_to_view(x_ref, col, o_ref, ocol, d, chunks):
    n = VIEW_ROWS // d
    for c, scr in enumerate(chunks):
        scr[...] = x_ref[:, col + c * LANES:col + (c + 1) * LANES].astype(F32)
        for r in range(d):
            at = ocol + r * WIDTH_A + c * LANES
            o_ref[:, at:at + LANES] = scr[pl.ds(r, n, stride=d), :].astype(o_ref.dtype)


def _view_to_rows(v_ref, o_ref, col, d, chunks):
    n = VIEW_ROWS // d
    for c, scr in enumerate(chunks):
        if d == 1:
            o_ref[:, col + c * LANES:col + (c + 1) * LANES] = v_ref[:, c * LANES:(c + 1) * LANES].astype(o_ref.dtype)
            continue
        for r in range(d):
            scr[pl.ds(r, n, stride=d), :] = v_ref[:, r * WIDTH_A + c * LANES:r * WIDTH_A + (c + 1) * LANES].astype(F32)
        o_ref[:, col + c * LANES:col + (c + 1) * LANES] = scr[...].astype(o_ref.dtype)


def _group_view(proj, grp, d):
    T = proj.shape[0]
    if d == 1:
        return proj, (lambda part, r: grp * 3 + part)

    def body(x_ref, o_ref, *chunks):
        for part in range(3):
            _rows_to_view(x_ref, part * WIDTH_A, o_ref, part * d * WIDTH_A, d, chunks)

    view = pl.pallas_call(
        body, out_shape=jax.ShapeDtypeStruct((T // d, 3 * d * WIDTH_A), proj.dtype), grid=(T // VIEW_ROWS,),
        in_specs=[_bs((VIEW_ROWS, 3 * WIDTH_A), lambda i: (i, grp))],
        out_specs=_bs((VIEW_ROWS // d, 3 * d * WIDTH_A), lambda i: (i, 0)),
        scratch_shapes=_view_chunks(), compiler_params=_params("parallel"), name=f"a_view_d{d}")(proj)
    return view, (lambda part, r: part * d + r)


def _stack_heads(v2, second):
    zero = jnp.zeros_like(v2)
    return jnp.concatenate([jnp.where(second, zero, v2), jnp.where(second, v2, zero)], axis=0)


def _unstack_heads(v, second):
    return jnp.where(second, v[QB_A:], v[:QB_A])


def _dil_fwd(view, bias, d):
    pv, colblk = view
    L = pv.shape[0]
    nblk = L // QB_A
    W2 = 2 * HEAD_A
    scale = HEAD_A ** -0.5

    def body(q_ref, k_ref, v_ref, b_ref, o_ref, l_ref):
        win = pl.ds(_window_start(pl.program_id(1), nblk), WIN_A)
        second = _lane_is_second_head((QB_A, W2))
        pairs = range(HEADS_A // 2)
        cols = [slice(hp * W2, (hp + 1) * W2) for hp in pairs]
        s = [lax.dot_general(_stack_heads(q_ref[:, cols[hp]], second), k_ref[win, cols[hp]], (NT, ((), ())),
                             preferred_element_type=F32) * scale + b_ref[2 * hp:2 * hp + 2].reshape(2 * QB_A, WIN_A)
             for hp in pairs]
        m = [jnp.max(x, axis=-1, keepdims=True) for x in s]
        p = [jnp.exp(x - mx) for x, mx in zip(s, m)]
        l = [jnp.sum(x, axis=-1, keepdims=True) for x in p]
        res = [jnp.dot(p[hp].astype(BF16), v_ref[win, cols[hp]], preferred_element_type=F32) / l[hp] for hp in pairs]
        o_ref[...] = jnp.concatenate([_unstack_heads(x, second) for x in res], axis=1).astype(o_ref.dtype)
        l_ref[...] = jnp.concatenate([_unstack_heads(jnp.broadcast_to(mx + jnp.log(lx), (2 * QB_A, W2)), second)
                                      for mx, lx in zip(m, l)], axis=1)

    in_specs = [_bs((QB_A, WIDTH_A), lambda r, n: (n, colblk(0, r))),
                _bs((L, WIDTH_A), lambda r, n: (0, colblk(1, r))), _bs((L, WIDTH_A), lambda r, n: (0, colblk(2, r))),
                _bs((None, HEADS_A, QB_A, WIN_A), lambda r, n: (_window_variant(n, nblk), 0, 0, 0))]
    o, lse = pl.pallas_call(
        body, out_shape=[jax.ShapeDtypeStruct((L, d * WIDTH_A), BF16), jax.ShapeDtypeStruct((L, d * WIDTH_A), F32)],
        grid=(d, nblk), in_specs=in_specs,
        out_specs=[_bs((QB_A, WIDTH_A), lambda r, n: (n, r)), _bs((QB_A, WIDTH_A), lambda r, n: (n, r))],
        compiler_params=_params("parallel", "parallel"), name=f"a_fwd_d{d}")(pv, pv, pv, bias)
    return o, lse


def _dil_bwd(view_qkv, bias, do, lse, cterm, d):
    pv, colblk = view_qkv
    L = pv.shape[0]
    nblk = L // QB_A
    W2 = 2 * HEAD_A
    PPS = 4
    WS = PPS * W2
    ob = WIDTH_A // WS
    scale = HEAD_A ** -0.5

    def body(q_ref, k_ref, v_ref, do_ref, l_ref, c_ref, b_ref, dq_ref, dk_ref, dv_ref, db_ref):
        r, n = pl.program_id(1), pl.program_id(2)

        @pl.when(n == 0)
        def _():
            dk_ref[...] = jnp.zeros_like(dk_ref)
            dv_ref[...] = jnp.zeros_like(dv_ref)

        @pl.when((n == 0) & (r == 0))
        def _():
            db_ref[...] = jnp.zeros_like(db_ref)

        second = _lane_is_second_head((QB_A, W2))
        win = pl.ds(_window_start(n, nblk), WIN_A)
        variant = _window_variant(n, nblk)
        pairs = range(PPS)
        cols = [slice(pp * W2, (pp + 1) * W2) for pp in pairs]

        def head_rows(ref, pp):
            v2 = ref[:, cols[pp]]
            return jnp.concatenate([v2[:, 0:1], v2[:, HEAD_A:HEAD_A + 1]], axis=0)

        kw = [k_ref[win, c] for c in cols]
        vw = [v_ref[win, c] for c in cols]
        qs = [_stack_heads(q_ref[:, c], second) for c in cols]
        dos = [_stack_heads(do_ref[:, c], second) for c in cols]
        s = [lax.dot_general(qs[pp], kw[pp], (NT, ((), ())), preferred_element_type=F32) for pp in pairs]
        dp = [lax.dot_general(dos[pp], vw[pp], (NT, ((), ())), preferred_element_type=F32) for pp in pairs]
        p = [jnp.exp(s[pp] * scale + b_ref[2 * pp:2 * pp + 2].reshape(2 * QB_A, WIN_A) - head_rows(l_ref, pp)) for pp in pairs]
        ds = [p[pp] * (dp[pp] + head_rows(c_ref, pp)) for pp in pairs]
        db_ref[variant] += jnp.concatenate([x.reshape(2, QB_A, WIN_A) for x in ds], axis=0)
        pb = [x.astype(BF16) for x in p]
        dsb = [(x * scale).astype(BF16) for x in ds]
        dq_ref[...] = jnp.concatenate([_unstack_heads(jnp.dot(dsb[pp], kw[pp], preferred_element_type=F32), second)
                                       for pp in pairs], axis=1).astype(dq_ref.dtype)
        dk_ref[win, :] += jnp.concatenate([lax.dot_general(dsb[pp], qs[pp], (TN, ((), ())), preferred_element_type=F32)
                                           for pp in pairs], axis=1)
        dv_ref[win, :] += jnp.concatenate([lax.dot_general(pb[pp], dos[pp], (TN, ((), ())), preferred_element_type=F32)
                                           for pp in pairs], axis=1)

    kv_spec = _resident if d == 1 else _bs
    in_specs = [_bs((QB_A, WS), lambda hp, r, n: (n, colblk(0, r) * ob + hp)),
                kv_spec((L, WS), lambda hp, r, n: (0, colblk(1, r) * ob + hp)),
                kv_spec((L, WS), lambda hp, r, n: (0, colblk(2, r) * ob + hp))]
    in_specs += [_bs((QB_A, WS), lambda hp, r, n: (n, r * ob + hp))] * 3
    in_specs += [_bs((None, 2 * PPS, QB_A, WIN_A), lambda hp, r, n: (_window_variant(n, nblk), hp, 0, 0))]
    out_shape = [jax.ShapeDtypeStruct((L, d * WIDTH_A), BF16), jax.ShapeDtypeStruct((L, d * WIDTH_A), F32),
                 jax.ShapeDtypeStruct((L, d * WIDTH_A), F32), jax.ShapeDtypeStruct((3, HEADS_A, QB_A, WIN_A), F32)]
    out_specs = [_bs((QB_A, WS), lambda hp, r, n: (n, r * ob + hp)),
                 _bs((L, WS), lambda hp, r, n: (0, r * ob + hp)), _bs((L, WS), lambda hp, r, n: (0, r * ob + hp)),
                 _bs((3, 2 * PPS, QB_A, WIN_A), lambda hp, r, n: (0, hp, 0, 0))]
    dq, dk, dv, db = pl.pallas_call(
        body, out_shape=out_shape, grid=(ob, d, nblk), in_specs=in_specs, out_specs=out_specs,
        compiler_params=_params("arbitrary", "arbitrary", "arbitrary"), name=f"a_bwd_d{d}")(
            pv, pv, pv, do, lse, cterm, bias)
    return dq, dk, dv, db


def _assemble_dproj(a_parts, dq_b, dk_b, dv_b, dga, dgb):
    T = dq_b.shape[0]
    flat = [(a_parts[part][g], d) for part in range(3) for g, d in enumerate(DILATIONS)]
    rest = [dq_b, dk_b, dv_b, dga, dgb]

    def body(*refs):
        views, others = refs[:len(flat)], refs[len(flat):len(flat) + len(rest)]
        o_ref, chunks = refs[len(flat) + len(rest)], refs[len(flat) + len(rest) + 1:]
        col = 0
        for v_ref, (_, d) in zip(views, flat):
            _view_to_rows(v_ref, o_ref, col, d, chunks)
            col += WIDTH_A
        for x_ref in others:
            w = x_ref.shape[1]
            o_ref[:, col:col + w] = x_ref[...].astype(o_ref.dtype)
            col += w

    in_specs = [_bs((VIEW_ROWS // d, d * WIDTH_A), lambda i: (i, 0)) for _, d in flat]
    in_specs += [_bs((VIEW_ROWS, x.shape[1]), lambda i: (i, 0)) for x in rest]
    return pl.pallas_call(
        body, out_shape=jax.ShapeDtypeStruct((T, IN_WIDTH), BF16), grid=(T // VIEW_ROWS,), in_specs=in_specs,
        out_specs=_bs((VIEW_ROWS, IN_WIDTH), lambda i: (i, 0)), scratch_shapes=_view_chunks(),
        compiler_params=_params("parallel"), name="mix_bwd_dproj")(*[a for a, _ in flat], *rest)


def _segment_ones():
    i = np.arange(WIDTH_A)
    return jnp.asarray((i[:, None] // HEAD_A == i[None, :] // HEAD_A).astype(np.float32), dtype=BF16)


def _group_weights(l0, l1, l2):
    m = jnp.maximum(jnp.maximum(l0, l1), l2)
    e = [jnp.exp(l - m) for l in (l0, l1, l2)]
    z = e[0] + e[1] + e[2]
    return [ei / z for ei in e]


def _view_specs():
    return [_bs((VIEW_ROWS // d, d * WIDTH_A), lambda i: (i, 0)) for d in DILATIONS]


def _stage_tiles(n):
    return [pltpu.VMEM((VIEW_ROWS, WIDTH_A), F32)] * n


def _token_rows(v_ref, stage, d, chunks):
    if d == 1:
        return v_ref[...].astype(F32)
    _view_to_rows(v_ref, stage, 0, d, chunks)
    return stage[...]


def _combine_fwd(outs, lses):
    T = outs[0].shape[0] * DILATIONS[0]
    n = len(DILATIONS)

    def body(*refs):
        o_refs, l_refs, oa_ref = refs[:n], refs[n:2 * n], refs[2 * n]
        o_st, l_st, chunks = refs[2 * n + 1:3 * n + 1], refs[3 * n + 1:4 * n + 1], refs[4 * n + 1:]
        o = [_token_rows(o_refs[g], o_st[g], d, chunks) for g, d in enumerate(DILATIONS)]
        w = _group_weights(*[_token_rows(l_refs[g], l_st[g], d, chunks) for g, d in enumerate(DILATIONS)])
        oa_ref[...] = (w[0] * o[0] + w[1] * o[1] + w[2] * o[2]).astype(oa_ref.dtype)

    return pl.pallas_call(
        body, out_shape=jax.ShapeDtypeStruct((T, WIDTH_A), BF16), grid=(T // VIEW_ROWS,),
        in_specs=_view_specs() * 2, out_specs=_bs((VIEW_ROWS, WIDTH_A), lambda i: (i, 0)),
        scratch_shapes=_stage_tiles(2 * n) + _view_chunks(), compiler_params=_params("parallel"), name="a_combine")(*outs, *lses)


def _combine_bwd(doa, outs, lses):
    T = doa.shape[0]
    n = len(DILATIONS)

    def body(*refs):
        d_ref, o_refs, l_refs, seg_ref = refs[0], refs[1:n + 1], refs[n + 1:2 * n + 1], refs[2 * n + 1]
        do_refs, c_refs = refs[2 * n + 2:3 * n + 2], refs[3 * n + 2:4 * n + 2]
        o_st, l_st = refs[4 * n + 2:5 * n + 2], refs[5 * n + 2:6 * n + 2]
        tmp, chunks = refs[6 * n + 2], refs[6 * n + 3:]
        o = [_token_rows(o_refs[g], o_st[g], d, chunks) for g, d in enumerate(DILATIONS)]
        w = _group_weights(*[_token_rows(l_refs[g], l_st[g], d, chunks) for g, d in enumerate(DILATIONS)])
        dv = d_ref[...].astype(F32)
        seg = seg_ref[...]
        tot = jnp.zeros(dv.shape, F32)
        for g in range(n):
            prod = w[g] * dv * o[g]
            hi = prod.astype(BF16)
            lo = (prod - hi.astype(F32)).astype(BF16)
            tot = tot + jnp.dot(hi, seg, preferred_element_type=F32) + jnp.dot(lo, seg, preferred_element_type=F32)
        for g, d in enumerate(DILATIONS):
            for ref, val in ((do_refs[g], w[g] * dv), (c_refs[g], -w[g] * tot)):
                if d == 1:
                    ref[...] = val.astype(ref.dtype)
                else:
                    tmp[...] = val
                    _rows_to_view(tmp, 0, ref, 0, d, chunks)

    views = [jax.ShapeDtypeStruct((T // d, d * WIDTH_A), dt) for dt in (BF16, F32) for d in DILATIONS]
    res = pl.pallas_call(
        body, out_shape=views, grid=(T // VIEW_ROWS,),
        in_specs=[_bs((VIEW_ROWS, WIDTH_A), lambda i: (i, 0))] + _view_specs() * 2 + [_bs((WIDTH_A, WIDTH_A), lambda i: (0, 0))],
        out_specs=_view_specs() * 2, scratch_shapes=_stage_tiles(2 * n + 1) + _view_chunks(),
        compiler_params=_params("parallel"), name="a_combine_bwd")(doa, *outs, *lses, _segment_ones())
    return res[:n], res[n:]


def _rope_tables(T):
    rows = T // GRID_W
    row = jnp.repeat(jnp.arange(rows, dtype=F32), GRID_W)
    col = jnp.tile(jnp.arange(GRID_W, dtype=F32), rows)
    n_freq = HEAD_B // 4
    freq = ROPE_THETA ** (-jnp.arange(n_freq, dtype=F32) / n_freq)
    ang = jnp.concatenate([row[:, None] * freq, col[:, None] * freq], axis=-1)
    cos, sin = jnp.repeat(jnp.cos(ang), 2, axis=1), jnp.repeat(jnp.sin(ang), 2, axis=1)
    sign = jnp.where(jnp.arange(HEAD_B) % 2 == 0, -1.0, 1.0).astype(F32)
    return cos, sin * sign


def _swap_pairs(v):
    even = lax.broadcasted_iota(jnp.int32, v.shape, v.ndim - 1) % 2 == 0
    n = v.shape[-1]
    return jnp.where(even, pltpu.roll(v, n - 1, v.ndim - 1), pltpu.roll(v, 1, v.ndim - 1))


def _qk_fwd(name, proj, col0, n_heads, gain, cos, sin, out_scale=1.0, deps=()):
    T = proj.shape[0]

    def fn(xr, g, c, s):
        xn = _norm_fwd(xr.astype(F32), g)
        return (xn * c + _swap_pairs(xn) * s) * out_scale

    (out,) = _ew(name, fn, [_tiled(proj, HEAD_B, col0 // HEAD_B), _whole(gain), _table(cos), _table(sin)],
                 [(BF16, HEAD_B)], n_rows=T, rows=2048, ncols=n_heads, deps=deps)
    return out


def _qk_bwd(name, dout, proj, col0, n_heads, gain, cos, sin, in_scale=1.0):
    T = proj.shape[0]

    def fn(dv, xr, g, c, s):
        dv = dv.astype(F32) * in_scale
        dxn = c * dv + _swap_pairs(s * dv)
        dx, dgr = _norm_bwd(xr.astype(F32), g, dxn)
        return dx, _colsum(dgr)

    dx, dg = _ew(name, fn, [_tiled(dout, HEAD_B, 0), _tiled(proj, HEAD_B, col0 // HEAD_B), _whole(gain),
                            _table(cos), _table(sin)],
                 [(BF16, HEAD_B)], n_rows=T, rows=2048, reds=(HEAD_B,), ncols=n_heads)
    return dx, jnp.sum(dg, axis=0)


def _gqa_fwd(qn, kn, proj, k_col=0):
    T = qn.shape[0]
    GW = 4 * HEAD_B
    QB = QB_B

    def body(q_ref, k_ref, v_ref, o_ref, l_ref):
        k = k_ref[...]
        v_ones = jnp.concatenate([v_ref[...], jnp.ones((T, HEAD_B), BF16)], axis=1)
        lane = lax.broadcasted_iota(jnp.int32, (QB, HEAD_B), 1)
        heads = range(4)
        s = [lax.dot_general(q_ref[:, g * HEAD_B:(g + 1) * HEAD_B], k, (NT, ((), ())), preferred_element_type=F32)
             for g in heads]
        m = [jnp.max(x, axis=-1, keepdims=True) for x in s]
        pv = [jnp.dot(jnp.exp2(x - mx).astype(BF16), v_ones, preferred_element_type=F32) for x, mx in zip(s, m)]
        l = [x[:, HEAD_B:HEAD_B + 1] for x in pv]
        o = [x[:, :HEAD_B] / lx for x, lx in zip(pv, l)]
        o_ref[...] = jnp.concatenate(o, axis=1).astype(o_ref.dtype)
        lse_all = jnp.zeros((QB, HEAD_B), F32)
        for g in heads:
            lse_all = jnp.where(lane == g, m[g] + jnp.log2(l[g]), lse_all)
        l_ref[...] = lse_all

    return pl.pallas_call(
        body, out_shape=[jax.ShapeDtypeStruct((T, 2 * GW), BF16), jax.ShapeDtypeStruct((2, T, HEAD_B), F32)],
        grid=(2, T // QB),
        in_specs=[_bs((QB, GW), lambda kv, i: (i, kv)), _bs((T, HEAD_B), lambda kv, i: (0, k_col + kv)),
                  _bs((T, HEAD_B), lambda kv, i: (0, B_V // HEAD_B + kv))],
        out_specs=[_bs((QB, GW), lambda kv, i: (i, kv)), _bs((None, QB, HEAD_B), lambda kv, i: (kv, i, 0))],
        compiler_params=_params("parallel", "parallel"), name="b_fwd")(qn, kn, proj)


def _gqa_bwd(qn, kn, proj, o, lse, do, deps=(), k_col=0):
    T = qn.shape[0]
    GW = 4 * HEAD_B

    def body(q_ref, k_ref, v_ref, o_ref, l_ref, do_ref, *rest):
        dq_ref, dk_ref, dv_ref = rest[-3:]
        i = pl.program_id(1)

        @pl.when(i == 0)
        def _():
            dk_ref[...] = jnp.zeros_like(dk_ref)
            dv_ref[...] = jnp.zeros_like(dv_ref)

        k, v = k_ref[...], v_ref[...]
        lse_all = l_ref[...]
        for g in range(4):
            cols = slice(g * HEAD_B, (g + 1) * HEAD_B)
            q, dob = q_ref[:, cols], do_ref[:, cols]
            delta = jnp.sum(dob.astype(F32) * o_ref[:, cols].astype(F32), axis=-1, keepdims=True)
            s = lax.dot_general(q, k, (NT, ((), ())), preferred_element_type=F32)
            p = jnp.exp2(s - lse_all[:, g:g + 1])
            dp = lax.dot_general(dob, v, (NT, ((), ())), preferred_element_type=F32)
            ds = (p * (dp - delta)).astype(BF16)
            dq_ref[:, cols] = jnp.dot(ds, k, preferred_element_type=F32).astype(dq_ref.dtype)
            dk_ref[...] += lax.dot_general(ds, q, (TN, ((), ())), preferred_element_type=F32)
            dv_ref[...] += lax.dot_general(p.astype(BF16), dob, (TN, ((), ())), preferred_element_type=F32)

    return pl.pallas_call(
        body, out_shape=[jax.ShapeDtypeStruct((T, 2 * GW), BF16), jax.ShapeDtypeStruct((T, 2 * HEAD_B), F32),
                         jax.ShapeDtypeStruct((T, 2 * HEAD_B), F32)],
        grid=(2, T // QB_B),
        in_specs=[_bs((QB_B, GW), lambda kv, i: (i, kv)), _bs((T, HEAD_B), lambda kv, i: (0, k_col + kv)),
                  _bs((T, HEAD_B), lambda kv, i: (0, B_V // HEAD_B + kv)), _bs((QB_B, GW), lambda kv, i: (i, kv)),
                  _bs((None, QB_B, HEAD_B), lambda kv, i: (kv, i, 0)), _bs((QB_B, GW), lambda kv, i: (i, kv))] + _any_specs(len(deps)),
        out_specs=[_bs((QB_B, GW), lambda kv, i: (i, kv)), _bs((T, HEAD_B), lambda kv, i: (0, kv)),
                   _bs((T, HEAD_B), lambda kv, i: (0, kv))],
        compiler_params=_params("parallel", "arbitrary"), name="b_bwd")(qn, kn, proj, o, lse, do, *deps)


def _local_step(x, target, small, get_w, put_g, deps=(), prefetch_w=lambda name, after: [], take_rider=lambda steps, after: None):
    T, D = x.shape
    gs = {}

    bias = _bias_tiles(small["rel_bias"])
    cos, sin = _rope_tables(T)
    (x1, h2), ffn1_saved = _ffn_fwd("ffn1", x, small["ffn1_norm"], lambda name, after: get_w(name, [after, bias, cos, sin]), deps,
                                    tail_ins=[small["mix_norm"]], tail_fn=lambda y, g: (y, _norm_fwd(y, g)), tail_outs=(F32, BF16))
    w_in = get_w("w_in", h2)
    nq = w_in.shape[2]
    tpq = nq // WIDTH_A

    def proj_tile(j, k):
        c = j * tpq + k
        return jnp.where(c < 3 * len(DILATIONS), (c % 3) * 3 + c // 3, c)

    proj = _mm("mix_in", (4, tpq),
               [(h2, _resident((T, D), lambda j, k: (0, 0)), w_in, _bs((None, D, WIDTH_A), lambda j, k: (j, 0, k)))],
               jax.ShapeDtypeStruct((T, IN_WIDTH), BF16), _bs((T, WIDTH_A), lambda j, k: (0, proj_tile(j, k))), NN)

    a_views = [_group_view(proj, grp, d) for grp, d in enumerate(DILATIONS)]
    a_outs, a_lses = [], []
    for grp, d in enumerate(DILATIONS):
        o, l = _dil_fwd(a_views[grp], bias[grp], d)
        a_outs.append(o)
        a_lses.append(l)
    o_a = _combine_fwd(a_outs, a_lses)

    qk_gain = jnp.concatenate([jnp.tile(small["q_norm"] * QK_SCALE_LOG2, (8, 1)), jnp.tile(small["k_norm"], (2, 1))])[:, None, :]
    qkn = _qk_fwd("b_qknorm", proj, B_Q, 10, qk_gain, cos, sin, deps=prefetch_w("w_branch_a", proj))
    qn, kn, k_col = qkn, qkn, 8
    o_b, lse_b = _gqa_fwd(qn, kn, proj, k_col)
    ahead = prefetch_w("ffn2_w1", o_b)

    wa, wb, wo = get_w("w_branch_a", o_b), get_w("w_branch_b", o_b), get_w("w_out", o_b)
    bg_a, bg_b = small["b_gate"][:, :D], small["b_gate"][:, D:]
    n_a = wa.shape[0]

    def merge_out(oa_ref, ob_ref, ga_ref, gb_ref, x1_ref, wa_ref, wb_ref, wo_ref, ba_ref, bb_ref, g2_ref, *rest):
        ta_ref, tb_ref, mg_ref, x2_ref, hn_ref = rest[-5:]
        oa = oa_ref[...]
        ta = jnp.concatenate([jnp.dot(oa, wa_ref[j], preferred_element_type=F32) for j in range(n_a)], axis=1)
        tb = jnp.dot(ob_ref[...], wb_ref[...], preferred_element_type=F32)
        sa = _sigmoid(ga_ref[...].astype(F32) + ba_ref[...])
        sb = _sigmoid(gb_ref[...].astype(F32) + bb_ref[...])
        merged = (sa * ta + sb * tb).astype(BF16)
        ta_ref[...], tb_ref[...], mg_ref[...] = ta.astype(BF16), tb.astype(BF16), merged
        y = x1_ref[...] + jnp.dot(merged, wo_ref[...], preferred_element_type=F32)
        x2_ref[...] = y
        hn_ref[...] = _norm_fwd(y, g2_ref[...]).astype(BF16)

    row = _bs((512, D), lambda i: (i, 0))
    gate_specs = [_bs((512, D), lambda i: (i, G_A // D)), _bs((512, D), lambda i: (i, G_B // D))]
    whole2, whole3 = (lambda i: (0, 0)), (lambda i: (0, 0, 0))
    vec = _bs((1, D), whole2)
    t_a, t_b, merged, x2, hn2 = pl.pallas_call(
        merge_out, out_shape=[jax.ShapeDtypeStruct((T, D), BF16)] * 3 + [jax.ShapeDtypeStruct((T, D), F32), jax.ShapeDtypeStruct((T, D), BF16)],
        grid=(T // 512,),
        in_specs=[_bs((512, WIDTH_A), lambda i: (i, 0)), row] + gate_specs + [row, _resident(wa.shape, whole3), _resident((D, D), whole2),
                                                                                _resident((D, D), whole2), vec, vec, vec]
        + _any_specs(len(ahead)),
        out_specs=[row] * 5, compiler_params=_params("parallel"), name="mix_merge_out")(
            o_a, o_b, proj, proj, x1, wa, wb, wo, bg_a, bg_b, small["ffn2_norm"], *ahead)

    def head(xv, g, tv):
        r = _rstd(xv)
        xh = xv * r
        e = xh * g - tv
        dy = e * (1.0 / D)
        dxh = dy * g
        dx = r * (dxh - xh * jnp.mean(dxh * xh, axis=-1, keepdims=True))
        return dx, 0.5 * dx, _colsum(e * e) * (0.5 / D), _colsum(dy * xh)

    (dx3, dx3_half, loss_cols, g_final), ffn2_saved = _ffn_fwd(
        "ffn2", x2, small["ffn2_norm"], get_w, h=hn2, tail_ins=[small["final_norm"].reshape(1, D), target], tail_fn=head,
        tail_outs=(F32, BF16), tail_reds=(D, D))
    gs["final_norm"] = g_final.reshape(D)

    dx2, _, dmix, gs["ffn2_norm"] = _ffn_bwd("ffn2", x2, small["ffn2_norm"], get_w, put_g, ffn2_saved, dx3, dx3_half,
                                             also_bf16=True)
    g_out = _mm_wgrad("mix_bwd_dwout", merged, dmix, a_cols=D // 4, b_cols=None, tm=256, tn=512, J=4).reshape(D, D)

    def merge_out_bwd(dx_ref, ta_ref, tb_ref, ga_ref, gb_ref, wa_ref, wb_ref, wo_ref, ba_ref, bb_ref,
                      dta_ref, dtb_ref, dga_ref, dgb_ref, doa_ref, dob_ref, dba_ref, dbb_ref):
        dm = lax.dot_general(dx_ref[...], wo_ref[...], (NT, ((), ())), preferred_element_type=F32)
        ta, tb = ta_ref[...].astype(F32), tb_ref[...].astype(F32)
        sa = _sigmoid(ga_ref[...].astype(F32) + ba_ref[...])
        sb = _sigmoid(gb_ref[...].astype(F32) + bb_ref[...])
        dga, dgb = dm * ta * sa * (1.0 - sa), dm * tb * sb * (1.0 - sb)
        dta, dtb = (dm * sa).astype(BF16), (dm * sb).astype(BF16)
        dta_ref[...], dtb_ref[...] = dta, dtb
        dga_ref[...], dgb_ref[...] = dga.astype(BF16), dgb.astype(BF16)
        w = wa_ref.shape[2]
        doa = sum(lax.dot_general(dta[:, j * w:(j + 1) * w], wa_ref[j], (NT, ((), ())), preferred_element_type=F32) for j in range(n_a))
        doa_ref[...] = doa.astype(BF16)
        dob_ref[...] = lax.dot_general(dtb, wb_ref[...], (NT, ((), ())), preferred_element_type=F32).astype(BF16)

        @pl.when(pl.program_id(0) == 0)
        def _():
            dba_ref[...] = jnp.zeros_like(dba_ref)
            dbb_ref[...] = jnp.zeros_like(dbb_ref)
        dba_ref[...] += _colsum(dga)
        dbb_ref[...] += _colsum(dgb)

    rowb = _bs((256, D), lambda i: (i, 0))
    gate_specs = [_bs((256, D), lambda i: (i, G_A // D)), _bs((256, D), lambda i: (i, G_B // D))]
    dta, dtb, dga, dgb, do_a, do_b, dba, dbb = pl.pallas_call(
        merge_out_bwd,
        out_shape=[jax.ShapeDtypeStruct((T, D), BF16)] * 4 + [jax.ShapeDtypeStruct((T, WIDTH_A), BF16), jax.ShapeDtypeStruct((T, D), BF16)]
        + [jax.ShapeDtypeStruct((1, D), F32)] * 2,
        grid=(T // 256,),
        in_specs=[rowb, rowb, rowb] + gate_specs + [_resident(wa.shape, whole3), _resident((D, D), whole2), _resident((D, D), whole2), vec, vec],
        out_specs=[rowb] * 4 + [_bs((256, WIDTH_A), lambda i: (i, 0)), rowb, vec, vec],
        compiler_params=_params("arbitrary"), name="mix_merge_out_bwd")(dmix, t_a, t_b, proj, proj, wa, wb, wo, bg_a, bg_b)
    gs["b_gate"] = jnp.concatenate([dba, dbb], axis=1)

    g_a = _mm_wgrad("mix_bwd_dwa", o_a, dta, a_cols=None, b_cols=D // 4, tm=WIDTH_A, tn=256, J=4)
    g_b = _mm_wgrad("mix_bwd_dwb", o_b, dtb, a_cols=D // 4, b_cols=None, tm=256, tn=512, J=4).reshape(D, D)
    deps = put_g({"w_out": g_out, "w_branch_a": g_a, "w_branch_b": g_b})

    dqn, dkn, dv_b = _gqa_bwd(qn, kn, proj, o_b, lse_b, do_b, deps, k_col)
    dq_b, gs["q_norm"] = _qk_bwd("b_bwd_qnorm", dqn, proj, B_Q, 8, small["q_norm"], cos, sin, in_scale=HEAD_B ** -0.5)
    dk_b, gs["k_norm"] = _qk_bwd("b_bwd_knorm", dkn, proj, B_K, 2, small["k_norm"], cos, sin, in_scale=1.0 / LOG2_E)

    do_groups, c_groups = _combine_bwd(do_a, a_outs, a_lses)
    dqs, dks, dvs, dbs = [], [], [], []
    for grp, d in enumerate(DILATIONS):
        dq, dk, dv, db = _dil_bwd(a_views[grp], bias[grp], do_groups[grp], a_lses[grp], c_groups[grp], d)
        dqs.append(dq), dks.append(dk), dvs.append(dv), dbs.append(db)
    gs["rel_bias"] = _bias_grad(jnp.stack(dbs))

    dproj = _assemble_dproj([dqs, dks, dvs], dq_b, dk_b, dv_b, dga, dgb)
    nq = w_in.shape[2]
    g_in = _mm("mix_bwd_dwin", (4, tpq),
               [(h2, _resident((T, D), lambda j, k: (0, 0)), dproj, _bs((T, WIDTH_A), lambda j, k: (0, j * tpq + k)))],
               jax.ShapeDtypeStruct((4, D, nq), BF16), _bs((None, D, WIDTH_A), lambda j, k: (j, 0, k)), TN)
    deps = put_g({"w_in": g_in})
    dx1, dx1_half, gs["mix_norm"] = _dh_norm_bwd(
        "mix_bwd_dh", 256,
        [(dproj, _bs((256, nq), lambda i, j=j: (i, j)), w_in, _resident((None, D, nq), lambda i, j=j: (j, 0, 0))) for j in range(4)],
        NT, x1, small["mix_norm"], dx2, deps, rider=take_rider(T // 256, deps))

    dx0, _, gs["ffn1_norm"] = _ffn_bwd("ffn1", x, small["ffn1_norm"], get_w, put_g, ffn1_saved, dx1, dx1_half, last=True,
                                       take_rider=take_rider)
    return loss_cols, dx0, gs


def _position():
    return lax.axis_index("x"), lax.axis_index("y"), lax.axis_index("c")


def _any_specs(n):
    return [pl.BlockSpec(memory_space=pl.ANY)] * n


HBM_SPEC = pl.BlockSpec(memory_space=pltpu.HBM)
SEM_SPEC = pl.BlockSpec(memory_space=pltpu.SEMAPHORE)
DATAFLOW_EFFECT = pltpu.SideEffectType.DATAFLOW_SIDE_EFFECTING
N_PEER_CHIPS = 3
LANES = 128


def _quarter_copies(srcs, lands, send_sems, recv_sems, mode):
    x, y, c = _position()
    me = 2 * x + y
    peers = [(1 - x, y, c), (x, 1 - y, c), (1 - x, 1 - y, c)]
    copies = []
    for src, land, send, recv in zip(srcs, lands, send_sems, recv_sems):
        if mode == "sibling":
            copies.append(pltpu.make_async_remote_copy(src_ref=src, dst_ref=land, send_sem=send.at[0], recv_sem=recv.at[0],
                                                       device_id=(x, y, 1 - c), device_id_type=MESH))
            continue
        if mode == "fill":
            half = land.shape[1] // 2
            for p, (px, py, _) in enumerate(peers):
                part = land.at[2 * px + py, pl.ds(c * half, half)]
                copies.append(pltpu.make_async_remote_copy(src_ref=part, dst_ref=part, send_sem=send.at[p], recv_sem=recv.at[p],
                                                           device_id=(x, y, 1 - c), device_id_type=MESH))
            continue
        scatter = mode == "scatter"
        half = land.shape[1] // 2
        mine = land.at[me, pl.ds(c * half, half)]
        for p, (px, py, pc) in enumerate(peers):
            copies.append(pltpu.make_async_remote_copy(
                src_ref=src.at[2 * px + py] if scatter else mine, dst_ref=land.at[me] if scatter else mine,
                send_sem=send.at[p], recv_sem=recv.at[p], device_id=(px, py, pc), device_id_type=MESH))
    return copies


def _fill_from_sibling(name, stacks):
    n = len(stacks)

    def body(*refs):
        outs = refs[n:2 * n]
        send_sems, recv_sems = refs[2 * n:]
        x, y, c = _position()
        copies = []
        for i, ref in enumerate(outs):
            half = ref.shape[1] // 2
            rows = pl.ds(c * half, half)
            for p, k in enumerate((2 * (1 - x) + y, 2 * x + (1 - y), 2 * (1 - x) + (1 - y))):
                cp = pltpu.make_async_remote_copy(ref.at[k, rows], ref.at[k, rows], send_sems.at[3 * i + p], recv_sems.at[3 * i + p],
                                                  device_id=(x, y, 1 - c), device_id_type=MESH)
                cp.start()
                copies.append(cp)
        for cp in copies:
            cp.wait()

    return pl.pallas_call(
        body, out_shape=[jax.ShapeDtypeStruct(s.shape, s.dtype) for s in stacks],
        in_specs=_any_specs(n), out_specs=_any_specs(n), input_output_aliases={i: i for i in range(n)},
        scratch_shapes=[pltpu.SemaphoreType.DMA((N_PEER_CHIPS * n,)), pltpu.SemaphoreType.DMA((N_PEER_CHIPS * n,))],
        compiler_params=pltpu.CompilerParams(has_side_effects=True), name=name)(*stacks)


def _exchange_start(name, srcs, lands, mode):
    n = len(lands)
    arrays = list(lands) if srcs is None else list(srcs) + list(lands)
    k = len(arrays)

    def body(*refs):
        land_refs = refs[k - n:k]
        send_sems, recv_sems = refs[k:k + n], refs[k + n:k + 2 * n]
        token = refs[2 * k + 2 * n]
        for cp in _quarter_copies(refs[:n], land_refs, send_sems, recv_sems, mode):
            cp.start()
        token[...] = jnp.zeros_like(token)

    sem = pltpu.SemaphoreType.DMA((N_PEER_CHIPS,))
    out_shape = [sem] * (2 * n) + [pltpu.HBM(a.shape, a.dtype) for a in arrays] + [jax.ShapeDtypeStruct((8, LANES), F32)]
    res = pl.pallas_call(
        body, name=name, out_shape=out_shape, in_specs=[HBM_SPEC] * k,
        out_specs=[SEM_SPEC] * (2 * n) + [HBM_SPEC] * k + [pl.BlockSpec(memory_space=pltpu.VMEM)],
        input_output_aliases={i: 2 * n + i for i in range(k)},
        compiler_params=pltpu.CompilerParams(has_side_effects=DATAFLOW_EFFECT),
    )(*[pltpu.with_memory_space_constraint(a, pltpu.HBM) for a in arrays])
    thru = res[2 * n:2 * n + k]
    return res[:n], res[n:2 * n], (None if srcs is None else thru[:n]), thru[k - n:], res[2 * n + k]


def _exchange_wait(name, srcs, lands, send_sems, recv_sems, after, mode):
    n = len(lands)
    arrays = list(lands) if srcs is None else list(srcs) + list(lands)
    k = len(arrays)
    after = list(after) if isinstance(after, (list, tuple)) else [after]

    def body(*refs):
        sends, recvs = refs[k:k + n], refs[k + n:k + 2 * n]
        for cp in _quarter_copies(refs[:n], refs[k - n:k], sends, recvs, mode):
            cp.wait_send()
            cp.wait_recv()

    res = pl.pallas_call(
        body, name=name, out_shape=[pltpu.HBM(a.shape, a.dtype) for a in arrays],
        in_specs=[HBM_SPEC] * k + [SEM_SPEC] * (2 * n) + _any_specs(len(after)),
        out_specs=[HBM_SPEC] * k, input_output_aliases={i: i for i in range(k)},
        compiler_params=pltpu.CompilerParams(has_side_effects=DATAFLOW_EFFECT),
    )(*arrays, *send_sems, *recv_sems, *after)
    return (None if srcs is None else res[:n]), res[k - n:]


def _scatter_and_forward(name, stacks, lands, old_srcs, old_lands, old_sends, old_recvs):
    n1, n0 = len(stacks), len(old_lands)
    sibling_lands = [lax.empty(a.shape, a.dtype) for a in old_lands]
    arrays = list(stacks) + list(lands) + list(old_srcs) + list(old_lands) + sibling_lands
    k, s = len(arrays), 2 * n1 + 2 * n0

    def body(*refs):
        new_srcs, new_lands = refs[:n1], refs[n1:2 * n1]
        was_srcs, landed, to_sibling = refs[2 * n1:2 * n1 + n0], refs[2 * n1 + n0:2 * n1 + 2 * n0], refs[2 * n1 + 2 * n0:k]
        was_sends, was_recvs = refs[k:k + n0], refs[k + n0:k + 2 * n0]
        sems = refs[k + 2 * n0:k + 2 * n0 + s]
        token = refs[k + 2 * n0 + s + k]
        for cp in _quarter_copies(new_srcs, new_lands, sems[:n1], sems[n1:2 * n1], "scatter"):
            cp.start()
        for cp in _quarter_copies(was_srcs, landed, was_sends, was_recvs, "scatter"):
            cp.wait_send()
            cp.wait_recv()
        for cp in _quarter_copies(landed, to_sibling, sems[2 * n1:2 * n1 + n0], sems[2 * n1 + n0:], "sibling"):
            cp.start()
        token[...] = jnp.zeros_like(token)

    sem = pltpu.SemaphoreType.DMA((N_PEER_CHIPS,))
    res = pl.pallas_call(
        body, name=name,
        out_shape=[sem] * s + [pltpu.HBM(a.shape, a.dtype) for a in arrays] + [jax.ShapeDtypeStruct((8, LANES), F32)],
        in_specs=[HBM_SPEC] * k + [SEM_SPEC] * (2 * n0),
        out_specs=[SEM_SPEC] * s + [HBM_SPEC] * k + [pl.BlockSpec(memory_space=pltpu.VMEM)],
        input_output_aliases={i: s + i for i in range(k)},
        compiler_params=pltpu.CompilerParams(has_side_effects=DATAFLOW_EFFECT),
    )(*[pltpu.with_memory_space_constraint(a, pltpu.HBM) for a in arrays], *old_sends, *old_recvs)
    thru = res[s:s + k]
    scatter = (res[:n1], res[n1:2 * n1], thru[:n1], thru[n1:2 * n1])
    sibling = (res[2 * n1:2 * n1 + n0], res[2 * n1 + n0:s], thru[2 * n1 + n0:2 * n1 + 2 * n0], thru[2 * n1 + 2 * n0:])
    return scatter, sibling, res[s + k]


def _own_slots(name, srcs, from_stack=False):
    n = len(srcs)
    me = (2 * lax.axis_index("x") + lax.axis_index("y")).astype(jnp.int32).reshape(1)

    def body(me_ref, *refs):
        for x_ref, o_ref in zip(refs[:n], refs[n:]):
            o_ref[...] = x_ref[...].astype(o_ref.dtype)

    in_specs, out_specs, out_shape = [], [], []
    for src in srcs:
        R, C = src.shape[-2:]
        in_specs.append(pl.BlockSpec((None, R // 2, C), lambda i, me_ref: (me_ref[0], i, 0)) if from_stack
                        else pl.BlockSpec((R // 2, C), lambda i, me_ref: (i, 0)))
        out_specs.append(pl.BlockSpec((None, R // 2, C), lambda i, me_ref: (me_ref[0], i, 0)))
        out_shape.append(jax.ShapeDtypeStruct((4, R, C), BF16))
    grid_spec = pltpu.PrefetchScalarGridSpec(num_scalar_prefetch=1, grid=(2,), in_specs=in_specs, out_specs=out_specs)
    return pl.pallas_call(body, out_shape=out_shape, grid_spec=grid_spec, compiler_params=_params("parallel"), name=name)(me, *srcs)


def _allreduce_small(buf):
    R, C = buf.shape
    flips = [(fx, fy, fc) for fx in (0, 1) for fy in (0, 1) for fc in (0, 1)][1:]

    def body(in_ref, out_ref, land_ref, send_sems, recv_sems):
        x, y, c = _position()
        me = 4 * x + 2 * y + c
        copies = []
        for k, (fx, fy, fc) in enumerate(flips):
            px, py, pc = (1 - x if fx else x), (1 - y if fy else y), (1 - c if fc else c)
            cp = pltpu.make_async_remote_copy(in_ref, land_ref.at[me], send_sems.at[k], recv_sems.at[k],
                                              device_id=(px, py, pc), device_id_type=MESH)
            cp.start()
            copies.append(cp)
        land_ref[me] = in_ref[...]
        for cp in copies:
            cp.wait()
        acc = land_ref[0]
        for k in range(1, 8):
            acc = acc + land_ref[k]
        out_ref[...] = acc

    return pl.pallas_call(
        body, out_shape=jax.ShapeDtypeStruct((R, C), F32),
        in_specs=[pl.BlockSpec(memory_space=pltpu.VMEM)], out_specs=pl.BlockSpec(memory_space=pltpu.VMEM),
        scratch_shapes=[pltpu.VMEM((8, R, C), F32), pltpu.SemaphoreType.DMA((7,)), pltpu.SemaphoreType.DMA((7,))],
        compiler_params=pltpu.CompilerParams(has_side_effects=True), name="allreduce_small")(buf)


def _adamw_math(w, g, m, v):
    m2 = ADAM_B1 * m + (1.0 - ADAM_B1) * g
    v2 = ADAM_B2 * v + (1.0 - ADAM_B2) * (g * g)
    m_hat = m2 / (1.0 - ADAM_B1 ** ADAM_STEP)
    v_hat = v2 / (1.0 - ADAM_B2 ** ADAM_STEP)
    delta = -ADAM_LR * (m_hat / (jnp.sqrt(v_hat) + ADAM_EPS) + ADAM_WD * w)
    return delta, m2, v2


def _adamw_from_partials(wv, mv, vv, *parts):
    def four(a, b, c, d):
        return ((a.astype(F32) + b.astype(F32)) + c.astype(F32)) + d.astype(F32)

    g = four(*parts[:4]) + four(*parts[4:])
    return (g,) + _adamw_math(wv, g, mv, vv)


def _adamw_big(name, w, m, v, mine, theirs):
    R, C = w.shape
    rows = 256 if R % 256 == 0 else R // 2
    nrb = R // rows
    slots = [_tiled(s.reshape(4 * R, C), None, 0, k * nrb) for s in (mine, theirs) for k in range(4)]
    return _ew(name, _adamw_from_partials, [_tiled(w), _tiled(m), _tiled(v)] + slots, [(F32, C)] * 4, n_rows=R, rows=rows)


def _adamw_rider(w, m, v, mine, theirs, steps, deliver):
    R, C = w.shape
    fits = [nb for nb in range(1, steps + 1) if R % nb == 0 and (R // nb) % 16 == 0]
    if not fits:
        return None
    nb = fits[-1]
    rows = R // nb

    def blocks(first):
        return pl.BlockSpec((rows, C), lambda *g: (first + jnp.minimum(g[0], nb - 1), 0))

    flat = [s.reshape(4 * R, C) for s in (mine, theirs)]
    return dict(operands=[w, m, v] + [f for f in flat for _ in range(4)],
                in_specs=[blocks(0)] * 3 + [blocks(k * nb) for _ in flat for k in range(4)],
                out_shape=[jax.ShapeDtypeStruct((R, C), F32)] * 4, out_specs=[blocks(0)] * 4,
                n_blocks=nb, fn=_adamw_from_partials, deliver=lambda outs: deliver(*outs))


BIG = ("ffn1_w1", "ffn1_w3", "ffn1_w2", "w_in", "w_branch_a", "w_branch_b", "w_out", "ffn2_w1", "ffn2_w3", "ffn2_w2")
SMALL = ("ffn1_norm", "mix_norm", "b_gate", "q_norm", "k_norm", "rel_bias", "ffn2_norm", "final_norm")
ORDER = ("ffn1_norm", "ffn1_w1", "ffn1_w3", "ffn1_w2", "mix_norm", "w_in", "b_gate", "q_norm", "k_norm", "rel_bias",
         "w_branch_a", "w_branch_b", "w_out", "ffn2_norm", "ffn2_w1", "ffn2_w3", "ffn2_w2", "final_norm")
TRANSPOSED = ("ffn1_w1", "ffn1_w3", "ffn2_w1", "ffn2_w3")
SIBLING_LAG = 2
EARLY_FORWARDS = 2
LONG_HOST_STEPS = 8
GATHER_GROUPS = (("ffn1_w1", "ffn1_w3"), ("ffn1_w2",), ("w_in",), ("w_branch_a", "w_branch_b", "w_out"),
                 ("ffn2_w1", "ffn2_w3", "ffn2_w2"))


def _pack_small(d):
    rows = []
    for n in SMALL:
        flat = d[n].reshape(-1)
        pad = (-flat.shape[0]) % LANES
        rows.append(jnp.pad(flat, (0, pad)).reshape(-1, LANES))
    buf = jnp.concatenate(rows, axis=0)
    return jnp.pad(buf, ((0, (-buf.shape[0]) % 8), (0, 0)))


def _unpack_small(buf, like):
    out, r = {}, 0
    for n in SMALL:
        size = like[n].size
        nr = -(-size // LANES)
        out[n] = buf[r:r + nr].reshape(-1)[:size].reshape(like[n].shape)
        r += nr
    return out


def kernel(x, ffn1_norm, ffn1_w1, ffn1_w3, ffn1_w2, mix_norm, w_in, b_gate, q_norm, k_norm, rel_bias, w_branch_a, w_branch_b, w_out, ffn2_norm, ffn2_w1, ffn2_w3, ffn2_w2, final_norm, loss_target, m_ffn1_norm, m_ffn1_w1, m_ffn1_w3, m_ffn1_w2, m_mix_norm, m_w_in, m_b_gate, m_q_norm, m_k_norm, m_rel_bias, m_w_branch_a, m_w_branch_b, m_w_out, m_ffn2_norm, m_ffn2_w1, m_ffn2_w3, m_ffn2_w2, m_final_norm, v_ffn1_norm, v_ffn1_w1, v_ffn1_w3, v_ffn1_w2, v_mix_norm, v_w_in, v_b_gate, v_q_norm, v_k_norm, v_rel_bias, v_w_branch_a, v_w_branch_b, v_w_out, v_ffn2_norm, v_ffn2_w1, v_ffn2_w3, v_ffn2_w2, v_final_norm):
    given = dict(locals())
    w = {n: given[n] for n in ORDER}
    m = {n: given["m_" + n] for n in ORDER}
    v = {n: given["v_" + n] for n in ORDER}
    T, D = x.shape[1], x.shape[2]

    def stored(a, n):
        a = a.reshape(a.shape[1:])
        return a.T if n in TRANSPOSED else a

    def returned(a, n):
        return (a.T if n in TRANSPOSED else a).reshape(w[n].shape)

    quarter = {n: stored(w[n], n) for n in BIG}
    send, recv, _, land_thru, token = _exchange_start(
        "gather_start", None, _own_slots("own_weights", [quarter[n] for n in BIG]), "gather")
    index = {n: i for i, n in enumerate(BIG)}
    ready, filling = {}, {}

    def landed_halves(group, after):
        ids = [index[n] for n in group]
        return _exchange_wait("gather_wait_" + group[0], None, [land_thru[i] for i in ids],
                              [send[i] for i in ids], [recv[i] for i in ids], after, "gather")[1]

    def prefetch_w(name, after):
        group = next(g for g in GATHER_GROUPS if name in g)
        started = _exchange_start("fill_start_" + group[0], None, landed_halves(group, after), "fill")
        filling[group] = started
        return [started[4]]

    def get_w(name, after):
        if name not in ready:
            group = next(g for g in GATHER_GROUPS if name in g)
            if group in filling:
                f_send, f_recv, _, thru, _ = filling[group]
                stacks = _exchange_wait("fill_wait_" + group[0], None, thru, f_send, f_recv, after, "fill")[1]
            else:
                stacks = _fill_from_sibling("gather_fill_" + group[0], landed_halves(group, after))
            for n, st in zip(group, stacks):
                ready[n] = st.reshape(D, D) if n in ("w_branch_b", "w_out") else st
        return ready[name]

    scattered, forwarded = [], []

    def forward_oldest(after):
        names, s_sem, r_sem, srcs, lands = scattered.pop(0)
        _, landed = _exchange_wait("scatter_wait_" + names[0], srcs, lands, s_sem, r_sem, after, "scatter")
        started = _exchange_start("sibling_start_" + names[0], landed, [lax.empty(a.shape, a.dtype) for a in landed], "sibling")
        forwarded.append((names,) + tuple(started[:4]))
        return started[4]

    def put_g(grads):
        names = list(grads)
        stacks = [grads[n].reshape((4,) + quarter[n].shape) for n in names]
        lands = _own_slots("own_grad_" + names[0], stacks, from_stack=True)
        if len(scattered) < (1 if len(forwarded) < EARLY_FORWARDS else SIBLING_LAG):
            started = _exchange_start("scatter_start_" + names[0], stacks, lands, "scatter")
            scattered.append((names,) + tuple(started[:4]))
            return [started[4]]
        old_names, s_sem, r_sem, old_srcs, old_lands = scattered.pop(0)
        scatter, sibling, token = _scatter_and_forward("scatter_start_" + names[0], stacks, lands, old_srcs, old_lands, s_sem, r_sem)
        scattered.append((names,) + scatter)
        forwarded.append((old_names,) + sibling)
        return [token]

    grads, deltas, new_m, new_v = {}, {}, {}, {}
    arrived, riding = {}, set()

    def partials(gi, after):
        if gi not in arrived:
            names, s_sem, r_sem, srcs, lands = forwarded[gi]
            arrived[gi] = _exchange_wait("sibling_wait_" + names[0], srcs, lands, s_sem, r_sem, after, "sibling")
        return arrived[gi]

    def deliver_to(n):
        def deliver(*res):
            grads[n], deltas[n], new_m[n], new_v[n] = [returned(r, n) for r in res]
        return deliver

    def take_rider(steps, after):
        cap = None if steps >= LONG_HOST_STEPS else quarter["ffn1_w2"].size
        waiting = [(quarter[n].size, gi, k, n) for gi, entry in enumerate(forwarded) for k, n in enumerate(entry[0])
                   if n not in riding and (cap is None or quarter[n].size <= cap)]
        for _, gi, k, n in sorted(waiting, reverse=True):
            mine, theirs = partials(gi, after)
            rider = _adamw_rider(quarter[n], stored(m[n], n), stored(v[n], n), mine[k], theirs[k], steps, deliver_to(n))
            if rider is not None:
                riding.add(n)
                return rider
        return None

    small = {n: w[n] for n in SMALL}
    packed = [_pack_small({n: d[n] for n in SMALL}) for d in (w, m, v)]
    loss_cols, grad_x, gs = _local_step(x.reshape(T, D), loss_target.reshape(T, D), small, get_w, put_g, deps=[token] + packed,
                                        prefetch_w=prefetch_w, take_rider=take_rider)

    after = grad_x
    while scattered:
        after = forward_oldest(after)
    for gi, entry in enumerate(forwarded):
        mine, theirs = partials(gi, after)
        for n, a, b in zip(entry[0], mine, theirs):
            if n not in riding:
                deliver_to(n)(*_adamw_big(f"adamw_{n}", quarter[n], stored(m[n], n), stored(v[n], n), a, b))

    gs = {n: gs[n].reshape(w[n].shape) for n in SMALL}
    packed_g = _pack_small(gs)
    n_small = packed_g.shape[0]
    summed = _allreduce_small(jnp.concatenate([packed_g, loss_cols.reshape(-1, LANES)], axis=0))
    g_small, loss = summed[:n_small], jnp.sum(summed[n_small:])
    R = g_small.shape[0]
    res = _ew("adamw_small", lambda wv, mv, vv, g: (g,) + _adamw_math(wv, g, mv, vv),
              [_tiled(packed[0]), _tiled(packed[1]), _tiled(packed[2]), _tiled(g_small)], [(F32, LANES)] * 4, n_rows=R, rows=R)
    for d, buf in zip((grads, deltas, new_m, new_v), res):
        d.update(_unpack_small(buf, w))

    return (loss, grad_x.reshape(x.shape), *[grads[n] for n in ORDER], *[deltas[n] for n in ORDER],
            *[new_m[n] for n in ORDER], *[new_v[n] for n in ORDER])
```

```python
import functools
import math

import numpy as np
import jax
import jax.numpy as jnp
from jax import lax
from jax.experimental import pallas as pl
from jax.experimental.pallas import tpu as pltpu

F32 = jnp.float32
BF16 = jnp.bfloat16
MESH = pl.DeviceIdType.MESH

NEG_INF = -1e30
EPS = 1e-6
GRID_W = 64
ROPE_THETA = 10000.0
DILATIONS = (1, 4, 16)
BAND_HALF = 64
HEAD_A = 64
HEADS_A = 8
WIDTH_A = HEADS_A * HEAD_A
HEAD_B = 128
LOG2_E = math.log2(math.e)
QK_SCALE_LOG2 = HEAD_B ** -0.5 * LOG2_E
N_BUCKETS = 32
MAX_DISTANCE = 1024
ADAM_LR, ADAM_B1, ADAM_B2, ADAM_EPS, ADAM_WD, ADAM_STEP = 0.001, 0.9, 0.999, 1e-08, 0.01, 10

B_Q, B_K, B_V = 4608, 5632, 5888
G_A, G_B = 6144, 7168
IN_WIDTH = 8192

VMEM_LIMIT_BYTES = 56 * 1024 * 1024
QB_A = 128
QB_B = 256


def _params(*sem):
    return pltpu.CompilerParams(dimension_semantics=sem, vmem_limit_bytes=VMEM_LIMIT_BYTES)


def _bs(shape, fn):
    return pl.BlockSpec(shape, fn)


def _resident(shape, fn):
    return pl.BlockSpec(shape, fn, pipeline_mode=pl.Buffered(1))


def _mm(name, grid, pairs, out_shape, out_spec, dims, *, extras=(), epilogue=None, deps=(), reds=(), rider=None):
    n_pairs, n_extra, n_deps = len(pairs), len(extras), len(deps)
    operands = [p[0] for p in pairs] + [p[2] for p in pairs] + [e[0] for e in extras] + list(deps)
    in_specs = [p[1] for p in pairs] + [p[3] for p in pairs] + [e[1] for e in extras] + _any_specs(n_deps)
    single = not isinstance(out_shape, (list, tuple))
    out_shapes = [out_shape] if single else list(out_shape)
    out_specs = [out_spec] if single else list(out_spec)
    n_out = len(out_shapes)
    out_shapes += [jax.ShapeDtypeStruct((1, w), F32) for w in reds]
    out_specs += [_bs((1, w), lambda *_: (0, 0)) for w in reds]
    n_rin = 0
    if rider is not None:
        assert rider["n_blocks"] <= grid[0]
        n_rin = len(rider["operands"])
        operands += list(rider["operands"])
        in_specs += list(rider["in_specs"])
        out_shapes += list(rider["out_shape"])
        out_specs += list(rider["out_specs"])

    def body(*refs):
        a_refs, b_refs = refs[:n_pairs], refs[n_pairs:2 * n_pairs]
        e_refs = refs[2 * n_pairs:2 * n_pairs + n_extra]
        o_refs = refs[2 * n_pairs + n_extra + n_deps + n_rin:]
        if rider is not None:
            r_in = refs[2 * n_pairs + n_extra + n_deps:2 * n_pairs + n_extra + n_deps + n_rin]
            r_out = o_refs[n_out + len(reds):]

            @pl.when(functools.reduce(jnp.logical_and, [pl.program_id(0) < rider["n_blocks"]]
                                      + [pl.program_id(ax) == 0 for ax in range(1, len(grid))]))
            def _():
                for ref, val in zip(r_out, rider["fn"](*[r[...] for r in r_in])):
                    ref[...] = val.astype(ref.dtype)
        acc = None
        for a_ref, b_ref in zip(a_refs, b_refs):
            t = lax.dot_general(a_ref[...], b_ref[...], (dims, ((), ())), preferred_element_type=F32)
            acc = t if acc is None else acc + t
        vals = acc if epilogue is None else epilogue(acc, *[e[...] for e in e_refs])
        if not isinstance(vals, (list, tuple)):
            vals = (vals,)
        for o_ref, v in zip(o_refs[:n_out], vals[:n_out]):
            o_ref[...] = v.astype(o_ref.dtype)
        if reds:
            first = functools.reduce(jnp.logical_and, [pl.program_id(ax) == 0 for ax in range(len(grid))])
            for r_ref, v in zip(o_refs[n_out:], vals[n_out:]):
                @pl.when(first)
                def _(r_ref=r_ref):
                    r_ref[...] = jnp.zeros_like(r_ref)
                r_ref[...] += v

    sem = ["arbitrary" if (reds or rider is not None) else "parallel"] * len(grid)
    res = pl.pallas_call(
        body, out_shape=out_shapes, grid=grid, in_specs=in_specs, out_specs=out_specs,
        compiler_params=_params(*sem), name=name)(*operands)
    if rider is not None:
        rider["deliver"](res[n_out + len(reds):])
        res = res[:n_out + len(reds)]
    return res[0] if (single and not reds) else res


NN = ((1,), (0,))
NT = ((1,), (1,))
TN = ((0,), (0,))


def _mm_wgrad(name, a, b, *, a_cols, b_cols, tm, tn, J, deps=(), rider=None):
    def pick(arr, cols, t):
        if arr.ndim == 3:
            T, c = arr.shape[1], arr.shape[2]
            t = min(t, c)
            return T, c, t, (lambda sel: _bs((None, T, t), lambda j, i, k: (j, 0, sel(i, k))))
        T = arr.shape[0]
        c = arr.shape[1] if cols is None else cols
        t = min(t, c)
        per = c // t
        if cols is None:
            if per == 1:
                return T, c, t, (lambda sel: _resident((T, t), lambda j, i, k: (0, 0)))
            return T, c, t, (lambda sel: _bs((T, t), lambda j, i, k: (0, sel(i, k))))
        return T, c, t, (lambda sel: _bs((T, t), lambda j, i, k: (0, j * per + sel(i, k))))
    _, ca, tm, mk_a = pick(a, a_cols, tm)
    _, cb, tn, mk_b = pick(b, b_cols, tn)
    return _mm(name, (J, ca // tm, cb // tn),
               [(a, mk_a(lambda i, k: i), b, mk_b(lambda i, k: k))],
               jax.ShapeDtypeStruct((J, ca, cb), BF16), _bs((None, tm, tn), lambda j, i, k: (j, i, k)), TN, deps=deps, rider=rider)


def _tiled(arr, width=None, col=0, rowblk=0):
    return ("t", arr, arr.shape[1] if width is None else width, col, rowblk)


def _table(arr):
    return ("f", arr)


def _whole(arr):
    return ("w", arr)


def _ew(name, fn, ins, outs, *, n_rows, rows, reds=(), ncols=1, deps=()):
    nrb = n_rows // rows
    n_deps = len(deps)
    operands, in_specs = [], []
    for spec in ins:
        if spec[0] == "t":
            _, arr, width, col, rowblk = spec
            step = 1 if ncols > 1 else 0
            in_specs.append(_bs((rows, width), lambda c, i, col=col, rowblk=rowblk, step=step: (rowblk + i, col + c * step)))
        elif spec[0] == "f":
            arr = spec[1]
            in_specs.append(_bs((rows, arr.shape[1]), lambda c, i: (i, 0)))
        else:
            arr = spec[1]
            nd = arr.ndim
            if nd == 3:
                in_specs.append(_bs((None,) + arr.shape[1:], lambda c, i: (c, 0, 0)))
            else:
                in_specs.append(_bs(arr.shape, lambda c, i, nd=nd: (0,) * nd))
        operands.append(arr)
    out_shapes = [jax.ShapeDtypeStruct((n_rows, ncols * w), dt) for dt, w in outs]
    out_specs = [_bs((rows, w), lambda c, i: (i, c)) for _, w in outs]
    out_shapes += [jax.ShapeDtypeStruct((ncols, 1, w), F32) for w in reds]
    out_specs += [_bs((None, 1, w), lambda c, i: (c, 0, 0)) for w in reds]
    n_in, n_out, n_red = len(ins), len(outs), len(reds)
    operands += list(deps)
    in_specs += _any_specs(n_deps)

    def body(*refs):
        vals = fn(*[r[...] for r in refs[:n_in]])
        if not isinstance(vals, (tuple, list)):
            vals = (vals,)
        o_refs = refs[n_in + n_deps:]
        for o_ref, v in zip(o_refs[:n_out], vals[:n_out]):
            o_ref[...] = v.astype(o_ref.dtype)
        if n_red:
            i = pl.program_id(1)
            for r_ref, v in zip(o_refs[n_out:], vals[n_out:]):
                @pl.when(i == 0)
                def _(r_ref=r_ref):
                    r_ref[...] = jnp.zeros_like(r_ref)
                r_ref[...] += v

    res = pl.pallas_call(
        body, out_shape=out_shapes, grid=(ncols, nrb), in_specs=in_specs, out_specs=out_specs,
        compiler_params=_params("parallel", "arbitrary" if n_red else "parallel"), name=name)(*operands)
    return res


def _colsum(v):
    return jnp.sum(v, axis=0, keepdims=True)


def _rstd(x):
    return lax.rsqrt(jnp.mean(x * x, axis=-1, keepdims=True) + EPS)


def _sigmoid(x):
    return 0.5 * jnp.tanh(0.5 * x) + 0.5


def _norm_fwd(x, g):
    return x * _rstd(x) * g


def _norm_bwd(x, g, dy):
    r = _rstd(x)
    xh = x * r
    dxh = dy * g
    dx = r * (dxh - xh * jnp.mean(dxh * xh, axis=-1, keepdims=True))
    return dx, dy * xh


def _row_spec(arr, rows):
    if arr.shape[0] == 1:
        return _bs(arr.shape, lambda i: (0, 0))
    return _bs((rows, arr.shape[1]), lambda i: (i, 0))


def _ffn_fwd(tag, x, gain, get_w, deps=(), *, h=None, tail_ins=(), tail_fn=None, tail_outs=(F32,), tail_reds=()):
    T, D = x.shape
    if h is None:
        (h,) = _ew(f"{tag}_norm", lambda xv, g: _norm_fwd(xv, g), [_tiled(x), _whole(gain)], [(BF16, D)], n_rows=T, rows=512,
                   deps=deps)
    w1, w3 = get_w(f"{tag}_w1", h), get_w(f"{tag}_w3", h)
    J, f, _ = w1.shape
    tm = 1024

    def up(h_ref, w1_ref, w3_ref, u_ref, g_ref, a_ref):
        hv = h_ref[...]
        u = lax.dot_general(hv, w1_ref[...], (NT, ((), ())), preferred_element_type=F32)
        g = lax.dot_general(hv, w3_ref[...], (NT, ((), ())), preferred_element_type=F32)
        u_ref[...] = u.astype(BF16)
        g_ref[...] = g.astype(BF16)
        a_ref[...] = (u * _sigmoid(u) * g).astype(BF16)

    slab = _bs((None, tm, f), lambda j, i: (j, i, 0))
    w_spec = _bs((None, f, D), lambda j, i: (j, 0, 0))
    u, g, a = pl.pallas_call(
        up, out_shape=[jax.ShapeDtypeStruct((J, T, f), BF16)] * 3, grid=(J, T // tm),
        in_specs=[_bs((tm, D), lambda j, i: (i, 0)), w_spec, w_spec], out_specs=[slab] * 3,
        compiler_params=_params("parallel", "parallel"), name=f"{tag}_up")(h, w1, w3)
    w2 = get_w(f"{tag}_w2", a)
    def tail(acc, xv, *rest):
        y = xv + 0.5 * acc
        return y if tail_fn is None else tail_fn(y, *rest)

    row = _bs((512, D), lambda i: (i, 0))
    res = _mm(f"{tag}_down", (T // 512,),
              [(a, _bs((None, 512, f), lambda i, j=j: (j, i, 0)), w2, _resident((None, f, D), lambda i, j=j: (j, 0, 0)))
               for j in range(J)],
              [jax.ShapeDtypeStruct((T, D), dt) for dt in tail_outs], [row] * len(tail_outs), NN,
              extras=[(x, row)] + [(t, _row_spec(t, 512)) for t in tail_ins], epilogue=tail, reds=tail_reds)
    return res, (h, u, g, a)


def _dh_norm_bwd(name, rows, pairs, dims, x, gain, dres, deps, also_bf16=False, rider=None):
    T, D = x.shape

    def epilogue(dh, xv, gv, dr):
        dx, dgr = _norm_bwd(xv, gv, dh)
        dx = dx + dr
        return (dx, 0.5 * dx) + ((dx,) if also_bf16 else ()) + (_colsum(dgr),)

    dts = [F32, BF16] + ([BF16] if also_bf16 else [])
    row = _bs((rows, D), lambda i: (i, 0))
    return _mm(name, (T // rows,), pairs, [jax.ShapeDtypeStruct((T, D), dt) for dt in dts], [row] * len(dts), dims,
               extras=[(x, row), (gain, _row_spec(gain, rows)), (dres, row)], epilogue=epilogue, deps=deps, reds=(D,), rider=rider)


def _ffn_bwd(tag, x, gain, get_w, put_g, saved, dy, dy_half, also_bf16=False, last=False, take_rider=lambda steps, after: None):
    h, u, g, a = saved
    T, D = x.shape
    w1, w3, w2 = [get_w(f"{tag}_{n}", dy_half) for n in ("w1", "w3", "w2")]
    J, f, _ = w1.shape
    dw2 = _mm_wgrad(f"{tag}_bwd_dw2", a, dy_half, a_cols=None, b_cols=None, tm=f, tn=D, J=J, rider=take_rider(J, dy_half))
    deps = put_g({f"{tag}_w2": dw2}) if last else []
    tm = 1024

    def up_bwd(dy_ref, w2_ref, u_ref, g_ref, *rest):
        du_ref, dg_ref = rest[-2:]
        da = lax.dot_general(dy_ref[...], w2_ref[...], (NT, ((), ())), preferred_element_type=F32)
        uv, gv = u_ref[...].astype(F32), g_ref[...].astype(F32)
        s = _sigmoid(uv)
        silu = uv * s
        du_ref[...] = (da * gv * (s + silu - silu * s)).astype(BF16)
        dg_ref[...] = (da * silu).astype(BF16)

    slab = _bs((None, tm, f), lambda j, i: (j, i, 0))
    du, dg = pl.pallas_call(
        up_bwd, out_shape=[jax.ShapeDtypeStruct((J, T, f), BF16)] * 2, grid=(J, T // tm),
        in_specs=[_bs((tm, D), lambda j, i: (i, 0)), _bs((None, f, D), lambda j, i: (j, 0, 0)), slab, slab] + _any_specs(len(deps)),
        out_specs=[slab] * 2, compiler_params=_params("parallel", "parallel"), name=f"{tag}_bwd_up")(dy_half, w2, u, g, *deps)
    dw1 = _mm_wgrad(f"{tag}_bwd_dw1", du, h, a_cols=None, b_cols=None, tm=f, tn=D, J=J)
    deps = put_g({f"{tag}_w1": dw1}) if last else []
    dw3 = _mm_wgrad(f"{tag}_bwd_dw3", dg, h, a_cols=None, b_cols=None, tm=f, tn=D, J=J, deps=deps)
    deps = put_g({f"{tag}_w3": dw3} if last else {f"{tag}_w2": dw2, f"{tag}_w1": dw1, f"{tag}_w3": dw3})
    pairs = []
    for j in range(J):
        a_spec = _bs((None, 256, f), lambda i, j=j: (j, i, 0))
        w_spec = _resident((None, f, D), lambda i, j=j: (j, 0, 0))
        pairs += [(du, a_spec, w1, w_spec), (dg, a_spec, w3, w_spec)]
    return _dh_norm_bwd(f"{tag}_bwd_dh", 256, pairs, NN, x, gain, dy, deps, also_bf16, rider=take_rider(T // 256, dw3))


def _t5_bucket(rel):
    n = N_BUCKETS // 2
    max_exact = n // 2
    ret = jnp.where(rel > 0, n, 0)
    a = jnp.abs(rel)
    af = jnp.maximum(a, 1).astype(F32)
    large = max_exact + (jnp.log(af / max_exact) / math.log(MAX_DISTANCE / max_exact) * (n - max_exact)).astype(jnp.int32)
    large = jnp.minimum(large, n - 1)
    return ret + jnp.where(a < max_exact, a, large)


WIN_A = QB_A + 2 * BAND_HALF
WIN_SHIFTS = (0, BAND_HALF, 2 * BAND_HALF)


def _window_variant(n, nblk):
    return jnp.where(n == 0, 0, jnp.where(n == nblk - 1, 2, 1))


def _window_start(n, nblk):
    return pl.multiple_of(jnp.clip(n * QB_A - BAND_HALF, 0, nblk * QB_A - WIN_A), BAND_HALF)


def _band_steps(xp=jnp):
    qi = xp.arange(QB_A, dtype=xp.int32)[None, :, None]
    kj = xp.arange(WIN_A, dtype=xp.int32)[None, None, :]
    return kj - qi - xp.asarray(WIN_SHIFTS, dtype=xp.int32)[:, None, None]


def _bias_tiles(rel_bias):
    wide = QB_A + 2 * WIN_SHIFTS[-1]
    qi = jnp.arange(QB_A, dtype=jnp.int32)[:, None]
    steps = jnp.arange(wide, dtype=jnp.int32)[None, :] - WIN_SHIFTS[-1] - qi
    buckets = jnp.stack([_t5_bucket(steps * d) for d in DILATIONS])
    inband = (jnp.abs(steps) <= BAND_HALF).astype(jnp.int32)
    n_heads = rel_bias.shape[1]

    def body(tab_ref, b_ref, m_ref, o_ref):
        hd = pl.program_id(0)
        bkt = b_ref[...]
        acc = jnp.zeros(bkt.shape, F32)
        for b in range(N_BUCKETS):
            acc = jnp.where(bkt == b, tab_ref[b, hd], acc)
        o_ref[...] = jnp.where(m_ref[...] > 0, acc, NEG_INF)

    base = pl.pallas_call(
        body, out_shape=jax.ShapeDtypeStruct((n_heads, QB_A, wide), F32), grid=(n_heads,),
        in_specs=[pl.BlockSpec(memory_space=pltpu.SMEM),
                  _bs((None, QB_A, wide), lambda hd: (hd // HEADS_A, 0, 0)),
                  _bs((QB_A, wide), lambda hd: (0, 0))],
        out_specs=_bs((None, QB_A, wide), lambda hd: (hd, 0, 0)),
        compiler_params=_params("parallel"), name="a_bias_tiles")(rel_bias, buckets, inband)
    base = base.reshape(len(DILATIONS), HEADS_A, QB_A, wide)
    return jnp.stack([base[..., WIN_SHIFTS[-1] - s:WIN_SHIFTS[-1] - s + WIN_A] for s in WIN_SHIFTS], axis=1)


def _bias_grad(dbias):
    steps = _band_steps(np)
    inband = np.abs(steps) <= BAND_HALF
    present = []
    for d in DILATIONS:
        rel = steps * d
        a = np.abs(rel)
        large = 8 + (np.log(np.maximum(a, 1) / 8.0) / math.log(MAX_DISTANCE / 8.0) * 8).astype(np.int64)
        bk = np.where(rel > 0, 16, 0) + np.where(a < 8, a, np.minimum(large, 15))
        present.append([sorted(set(bk[v][inband[v]].tolist())) for v in range(3)])
    buckets = jnp.stack([_t5_bucket(_band_steps() * d) for d in DILATIONS])
    n_heads = len(DILATIONS) * HEADS_A

    def body(b_ref, d_ref, o_ref):
        row = lax.broadcasted_iota(jnp.int32, (N_BUCKETS, n_heads), 0)
        col = lax.broadcasted_iota(jnp.int32, (N_BUCKETS, n_heads), 1)
        out = jnp.zeros((N_BUCKETS, n_heads), F32)
        for grp in range(len(DILATIONS)):
            for hh in range(HEADS_A):
                hd = grp * HEADS_A + hh
                for b in sorted(set(sum(present[grp], []))):
                    tot = jnp.zeros((), F32)
                    for v in range(3):
                        if b in present[grp][v]:
                            tot = tot + jnp.sum(jnp.where(b_ref[grp, v] == b, d_ref[grp, v, hh], 0.0))
                    out = jnp.where((row == b) & (col == hd), tot, out)
        o_ref[...] = out

    return pl.pallas_call(
        body, out_shape=jax.ShapeDtypeStruct((N_BUCKETS, n_heads), F32),
        compiler_params=pltpu.CompilerParams(vmem_limit_bytes=VMEM_LIMIT_BYTES), name="a_bias_grad")(buckets, dbias)


def _lane_is_second_head(shape):
    return lax.broadcasted_iota(jnp.int32, shape, len(shape) - 1) >= HEAD_A


VIEW_ROWS = 512


def _view_chunks():
    return [pltpu.VMEM((VIEW_ROWS, LANES), F32)] * (WIDTH_A // LANES)


def _rows_to_view(x_ref, col, o_ref, ocol, d, chunks):
    n = VIEW_ROWS // d
    for c, scr in enumerate(chunks):
        scr[...] = x_ref[:, col + c * LANES:col + (c + 1) * LANES].astype(F32)
        for r in range(d):
            at = ocol + r * WIDTH_A + c * LANES
            o_ref[:, at:at + LANES] = scr[pl.ds(r, n, stride=d), :].astype(o_ref.dtype)


def _view_to_rows(v_ref, o_ref, col, d, chunks):
    n = VIEW_ROWS // d
    for c, scr in enumerate(chunks):
        if d == 1:
            o_ref[:, col + c * LANES:col + (c + 1) * LANES] = v_ref[:, c * LANES:(c + 1) * LANES].astype(o_ref.dtype)
            continue
        for r in range(d):
            scr[pl.ds(r, n, stride=d), :] = v_ref[:, r * WIDTH_A + c * LANES:r * WIDTH_A + (c + 1) * LANES].astype(F32)
        o_ref[:, col + c * LANES:col + (c + 1) * LANES] = scr[...].astype(o_ref.dtype)


def _group_view(proj, grp, d):
    T = proj.shape[0]
    if d == 1:
        return proj, (lambda part, r: grp * 3 + part)

    def body(x_ref, o_ref, *chunks):
        for part in range(3):
            _rows_to_view(x_ref, part * WIDTH_A, o_ref, part * d * WIDTH_A, d, chunks)

    view = pl.pallas_call(
        body, out_shape=jax.ShapeDtypeStruct((T // d, 3 * d * WIDTH_A), proj.dtype), grid=(T // VIEW_ROWS,),
        in_specs=[_bs((VIEW_ROWS, 3 * WIDTH_A), lambda i: (i, grp))],
        out_specs=_bs((VIEW_ROWS // d, 3 * d * WIDTH_A), lambda i: (i, 0)),
        scratch_shapes=_view_chunks(), compiler_params=_params("parallel"), name=f"a_view_d{d}")(proj)
    return view, (lambda part, r: part * d + r)


def _stack_heads(v2, second):
    zero = jnp.zeros_like(v2)
    return jnp.concatenate([jnp.where(second, zero, v2), jnp.where(second, v2, zero)], axis=0)


def _unstack_heads(v, second):
    return jnp.where(second, v[QB_A:], v[:QB_A])


def _dil_fwd(view, bias, d):
    pv, colblk = view
    L = pv.shape[0]
    nblk = L // QB_A
    W2 = 2 * HEAD_A
    scale = HEAD_A ** -0.5

    def body(q_ref, k_ref, v_ref, b_ref, o_ref, l_ref):
        win = pl.ds(_window_start(pl.program_id(1), nblk), WIN_A)
        second = _lane_is_second_head((QB_A, W2))
        pairs = range(HEADS_A // 2)
        cols = [slice(hp * W2, (hp + 1) * W2) for hp in pairs]
        s = [lax.dot_general(_stack_heads(q_ref[:, cols[hp]], second), k_ref[win, cols[hp]], (NT, ((), ())),
                             preferred_element_type=F32) * scale + b_ref[2 * hp:2 * hp + 2].reshape(2 * QB_A, WIN_A)
             for hp in pairs]
        m = [jnp.max(x, axis=-1, keepdims=True) for x in s]
        p = [jnp.exp(x - mx) for x, mx in zip(s, m)]
        l = [jnp.sum(x, axis=-1, keepdims=True) for x in p]
        res = [jnp.dot(p[hp].astype(BF16), v_ref[win, cols[hp]], preferred_element_type=F32) / l[hp] for hp in pairs]
        o_ref[...] = jnp.concatenate([_unstack_heads(x, second) for x in res], axis=1).astype(o_ref.dtype)
        l_ref[...] = jnp.concatenate([_unstack_heads(jnp.broadcast_to(mx + jnp.log(lx), (2 * QB_A, W2)), second)
                                      for mx, lx in zip(m, l)], axis=1)

    in_specs = [_bs((QB_A, WIDTH_A), lambda r, n: (n, colblk(0, r))),
                _bs((L, WIDTH_A), lambda r, n: (0, colblk(1, r))), _bs((L, WIDTH_A), lambda r, n: (0, colblk(2, r))),
                _bs((None, HEADS_A, QB_A, WIN_A), lambda r, n: (_window_variant(n, nblk), 0, 0, 0))]
    o, lse = pl.pallas_call(
        body, out_shape=[jax.ShapeDtypeStruct((L, d * WIDTH_A), BF16), jax.ShapeDtypeStruct((L, d * WIDTH_A), F32)],
        grid=(d, nblk), in_specs=in_specs,
        out_specs=[_bs((QB_A, WIDTH_A), lambda r, n: (n, r)), _bs((QB_A, WIDTH_A), lambda r, n: (n, r))],
        compiler_params=_params("parallel", "parallel"), name=f"a_fwd_d{d}")(pv, pv, pv, bias)
    return o, lse


def _dil_bwd(view_qkv, bias, do, lse, cterm, d):
    pv, colblk = view_qkv
    L = pv.shape[0]
    nblk = L // QB_A
    W2 = 2 * HEAD_A
    PPS = 4
    WS = PPS * W2
    ob = WIDTH_A // WS
    scale = HEAD_A ** -0.5

    def body(q_ref, k_ref, v_ref, do_ref, l_ref, c_ref, b_ref, dq_ref, dk_ref, dv_ref, db_ref):
        r, n = pl.program_id(1), pl.program_id(2)

        @pl.when(n == 0)
        def _():
            dk_ref[...] = jnp.zeros_like(dk_ref)
            dv_ref[...] = jnp.zeros_like(dv_ref)

        @pl.when((n == 0) & (r == 0))
        def _():
            db_ref[...] = jnp.zeros_like(db_ref)

        second = _lane_is_second_head((QB_A, W2))
        win = pl.ds(_window_start(n, nblk), WIN_A)
        variant = _window_variant(n, nblk)
        pairs = range(PPS)
        cols = [slice(pp * W2, (pp + 1) * W2) for pp in pairs]

        def head_rows(ref, pp):
            v2 = ref[:, cols[pp]]
            return jnp.concatenate([v2[:, 0:1], v2[:, HEAD_A:HEAD_A + 1]], axis=0)

        kw = [k_ref[win, c] for c in cols]
        vw = [v_ref[win, c] for c in cols]
        qs = [_stack_heads(q_ref[:, c], second) for c in cols]
        dos = [_stack_heads(do_ref[:, c], second) for c in cols]
        s = [lax.dot_general(qs[pp], kw[pp], (NT, ((), ())), preferred_element_type=F32) for pp in pairs]
        dp = [lax.dot_general(dos[pp], vw[pp], (NT, ((), ())), preferred_element_type=F32) for pp in pairs]
        p = [jnp.exp(s[pp] * scale + b_ref[2 * pp:2 * pp + 2].reshape(2 * QB_A, WIN_A) - head_rows(l_ref, pp)) for pp in pairs]
        ds = [p[pp] * (dp[pp] + head_rows(c_ref, pp)) for pp in pairs]
        db_ref[variant] += jnp.concatenate([x.reshape(2, QB_A, WIN_A) for x in ds], axis=0)
        pb = [x.astype(BF16) for x in p]
        dsb = [(x * scale).astype(BF16) for x in ds]
        dq_ref[...] = jnp.concatenate([_unstack_heads(jnp.dot(dsb[pp], kw[pp], preferred_element_type=F32), second)
                                       for pp in pairs], axis=1).astype(dq_ref.dtype)
        dk_ref[win, :] += jnp.concatenate([lax.dot_general(dsb[pp], qs[pp], (TN, ((), ())), preferred_element_type=F32)
                                           for pp in pairs], axis=1)
        dv_ref[win, :] += jnp.concatenate([lax.dot_general(pb[pp], dos[pp], (TN, ((), ())), preferred_element_type=F32)
                                           for pp in pairs], axis=1)

    kv_spec = _resident if d == 1 else _bs
    in_specs = [_bs((QB_A, WS), lambda hp, r, n: (n, colblk(0, r) * ob + hp)),
                kv_spec((L, WS), lambda hp, r, n: (0, colblk(1, r) * ob + hp)),
                kv_spec((L, WS), lambda hp, r, n: (0, colblk(2, r) * ob + hp))]
    in_specs += [_bs((QB_A, WS), lambda hp, r, n: (n, r * ob + hp))] * 3
    in_specs += [_bs((None, 2 * PPS, QB_A, WIN_A), lambda hp, r, n: (_window_variant(n, nblk), hp, 0, 0))]
    out_shape = [jax.ShapeDtypeStruct((L, d * WIDTH_A), BF16), jax.ShapeDtypeStruct((L, d * WIDTH_A), F32),
                 jax.ShapeDtypeStruct((L, d * WIDTH_A), F32), jax.ShapeDtypeStruct((3, HEADS_A, QB_A, WIN_A), F32)]
    out_specs = [_bs((QB_A, WS), lambda hp, r, n: (n, r * ob + hp)),
                 _bs((L, WS), lambda hp, r, n: (0, r * ob + hp)), _bs((L, WS), lambda hp, r, n: (0, r * ob + hp)),
                 _bs((3, 2 * PPS, QB_A, WIN_A), lambda hp, r, n: (0, hp, 0, 0))]
    dq, dk, dv, db = pl.pallas_call(
        body, out_shape=out_shape, grid=(ob, d, nblk), in_specs=in_specs, out_specs=out_specs,
        compiler_params=_params("arbitrary", "arbitrary", "arbitrary"), name=f"a_bwd_d{d}")(
            pv, pv, pv, do, lse, cterm, bias)
    return dq, dk, dv, db


def _assemble_dproj(a_parts, dq_b, dk_b, dv_b, dga, dgb):
    T = dq_b.shape[0]
    flat = [(a_parts[part][g], d) for part in range(3) for g, d in enumerate(DILATIONS)]
    rest = [dq_b, dk_b, dv_b, dga, dgb]

    def body(*refs):
        views, others = refs[:len(flat)], refs[len(flat):len(flat) + len(rest)]
        o_ref, chunks = refs[len(flat) + len(rest)], refs[len(flat) + len(rest) + 1:]
        col = 0
        for v_ref, (_, d) in zip(views, flat):
            _view_to_rows(v_ref, o_ref, col, d, chunks)
            col += WIDTH_A
        for x_ref in others:
            w = x_ref.shape[1]
            o_ref[:, col:col + w] = x_ref[...].astype(o_ref.dtype)
            col += w

    in_specs = [_bs((VIEW_ROWS // d, d * WIDTH_A), lambda i: (i, 0)) for _, d in flat]
    in_specs += [_bs((VIEW_ROWS, x.shape[1]), lambda i: (i, 0)) for x in rest]
    return pl.pallas_call(
        body, out_shape=jax.ShapeDtypeStruct((T, IN_WIDTH), BF16), grid=(T // VIEW_ROWS,), in_specs=in_specs,
        out_specs=_bs((VIEW_ROWS, IN_WIDTH), lambda i: (i, 0)), scratch_shapes=_view_chunks(),
        compiler_params=_params("parallel"), name="mix_bwd_dproj")(*[a for a, _ in flat], *rest)


def _segment_ones():
    i = np.arange(WIDTH_A)
    return jnp.asarray((i[:, None] // HEAD_A == i[None, :] // HEAD_A).astype(np.float32), dtype=BF16)


def _group_weights(l0, l1, l2):
    m = jnp.maximum(jnp.maximum(l0, l1), l2)
    e = [jnp.exp(l - m) for l in (l0, l1, l2)]
    z = e[0] + e[1] + e[2]
    return [ei / z for ei in e]


def _view_specs():
    return [_bs((VIEW_ROWS // d, d * WIDTH_A), lambda i: (i, 0)) for d in DILATIONS]


def _stage_tiles(n):
    return [pltpu.VMEM((VIEW_ROWS, WIDTH_A), F32)] * n


def _token_rows(v_ref, stage, d, chunks):
    if d == 1:
        return v_ref[...].astype(F32)
    _view_to_rows(v_ref, stage, 0, d, chunks)
    return stage[...]


def _combine_fwd(outs, lses):
    T = outs[0].shape[0] * DILATIONS[0]
    n = len(DILATIONS)

    def body(*refs):
        o_refs, l_refs, oa_ref = refs[:n], refs[n:2 * n], refs[2 * n]
        o_st, l_st, chunks = refs[2 * n + 1:3 * n + 1], refs[3 * n + 1:4 * n + 1], refs[4 * n + 1:]
        o = [_token_rows(o_refs[g], o_st[g], d, chunks) for g, d in enumerate(DILATIONS)]
        w = _group_weights(*[_token_rows(l_refs[g], l_st[g], d, chunks) for g, d in enumerate(DILATIONS)])
        oa_ref[...] = (w[0] * o[0] + w[1] * o[1] + w[2] * o[2]).astype(oa_ref.dtype)

    return pl.pallas_call(
        body, out_shape=jax.ShapeDtypeStruct((T, WIDTH_A), BF16), grid=(T // VIEW_ROWS,),
        in_specs=_view_specs() * 2, out_specs=_bs((VIEW_ROWS, WIDTH_A), lambda i: (i, 0)),
        scratch_shapes=_stage_tiles(2 * n) + _view_chunks(), compiler_params=_params("parallel"), name="a_combine")(*outs, *lses)


def _combine_bwd(doa, outs, lses):
    T = doa.shape[0]
    n = len(DILATIONS)

    def body(*refs):
        d_ref, o_refs, l_refs, seg_ref = refs[0], refs[1:n + 1], refs[n + 1:2 * n + 1], refs[2 * n + 1]
        do_refs, c_refs = refs[2 * n + 2:3 * n + 2], refs[3 * n + 2:4 * n + 2]
        o_st, l_st = refs[4 * n + 2:5 * n + 2], refs[5 * n + 2:6 * n + 2]
        tmp, chunks = refs[6 * n + 2], refs[6 * n + 3:]
        o = [_token_rows(o_refs[g], o_st[g], d, chunks) for g, d in enumerate(DILATIONS)]
        w = _group_weights(*[_token_rows(l_refs[g], l_st[g], d, chunks) for g, d in enumerate(DILATIONS)])
        dv = d_ref[...].astype(F32)
        seg = seg_ref[...]
        tot = jnp.zeros(dv.shape, F32)
        for g in range(n):
            prod = w[g] * dv * o[g]
            hi = prod.astype(BF16)
            lo = (prod - hi.astype(F32)).astype(BF16)
            tot = tot + jnp.dot(hi, seg, preferred_element_type=F32) + jnp.dot(lo, seg, preferred_element_type=F32)
        for g, d in enumerate(DILATIONS):
            for ref, val in ((do_refs[g], w[g] * dv), (c_refs[g], -w[g] * tot)):
                if d == 1:
                    ref[...] = val.astype(ref.dtype)
                else:
                    tmp[...] = val
                    _rows_to_view(tmp, 0, ref, 0, d, chunks)

    views = [jax.ShapeDtypeStruct((T // d, d * WIDTH_A), dt) for dt in (BF16, F32) for d in DILATIONS]
    res = pl.pallas_call(
        body, out_shape=views, grid=(T // VIEW_ROWS,),
        in_specs=[_bs((VIEW_ROWS, WIDTH_A), lambda i: (i, 0))] + _view_specs() * 2 + [_bs((WIDTH_A, WIDTH_A), lambda i: (0, 0))],
        out_specs=_view_specs() * 2, scratch_shapes=_stage_tiles(2 * n + 1) + _view_chunks(),
        compiler_params=_params("parallel"), name="a_combine_bwd")(doa, *outs, *lses, _segment_ones())
    return res[:n], res[n:]


def _rope_tables(T):
    rows = T // GRID_W
    row = jnp.repeat(jnp.arange(rows, dtype=F32), GRID_W)
    col = jnp.tile(jnp.arange(GRID_W, dtype=F32), rows)
    n_freq = HEAD_B // 4
    freq = ROPE_THETA ** (-jnp.arange(n_freq, dtype=F32) / n_freq)
    ang = jnp.concatenate([row[:, None] * freq, col[:, None] * freq], axis=-1)
    cos, sin = jnp.repeat(jnp.cos(ang), 2, axis=1), jnp.repeat(jnp.sin(ang), 2, axis=1)
    sign = jnp.where(jnp.arange(HEAD_B) % 2 == 0, -1.0, 1.0).astype(F32)
    return cos, sin * sign


def _swap_pairs(v):
    even = lax.broadcasted_iota(jnp.int32, v.shape, v.ndim - 1) % 2 == 0
    n = v.shape[-1]
    return jnp.where(even, pltpu.roll(v, n - 1, v.ndim - 1), pltpu.roll(v, 1, v.ndim - 1))


def _qk_fwd(name, proj, col0, n_heads, gain, cos, sin, out_scale=1.0, deps=()):
    T = proj.shape[0]

    def fn(xr, g, c, s):
        xn = _norm_fwd(xr.astype(F32), g)
        return (xn * c + _swap_pairs(xn) * s) * out_scale

    (out,) = _ew(name, fn, [_tiled(proj, HEAD_B, col0 // HEAD_B), _whole(gain), _table(cos), _table(sin)],
                 [(BF16, HEAD_B)], n_rows=T, rows=2048, ncols=n_heads, deps=deps)
    return out


def _qk_bwd(name, dout, proj, col0, n_heads, gain, cos, sin, in_scale=1.0):
    T = proj.shape[0]

    def fn(dv, xr, g, c, s):
        dv = dv.astype(F32) * in_scale
        dxn = c * dv + _swap_pairs(s * dv)
        dx, dgr = _norm_bwd(xr.astype(F32), g, dxn)
        return dx, _colsum(dgr)

    dx, dg = _ew(name, fn, [_tiled(dout, HEAD_B, 0), _tiled(proj, HEAD_B, col0 // HEAD_B), _whole(gain),
                            _table(cos), _table(sin)],
                 [(BF16, HEAD_B)], n_rows=T, rows=2048, reds=(HEAD_B,), ncols=n_heads)
    return dx, jnp.sum(dg, axis=0)


def _gqa_fwd(qn, kn, proj, k_col=0):
    T = qn.shape[0]
    GW = 4 * HEAD_B
    QB = QB_B

    def body(q_ref, k_ref, v_ref, o_ref, l_ref):
        k = k_ref[...]
        v_ones = jnp.concatenate([v_ref[...], jnp.ones((T, HEAD_B), BF16)], axis=1)
        lane = lax.broadcasted_iota(jnp.int32, (QB, HEAD_B), 1)
        heads = range(4)
        s = [lax.dot_general(q_ref[:, g * HEAD_B:(g + 1) * HEAD_B], k, (NT, ((), ())), preferred_element_type=F32)
             for g in heads]
        m = [jnp.max(x, axis=-1, keepdims=True) for x in s]
        pv = [jnp.dot(jnp.exp2(x - mx).astype(BF16), v_ones, preferred_element_type=F32) for x, mx in zip(s, m)]
        l = [x[:, HEAD_B:HEAD_B + 1] for x in pv]
        o = [x[:, :HEAD_B] / lx for x, lx in zip(pv, l)]
        o_ref[...] = jnp.concatenate(o, axis=1).astype(o_ref.dtype)
        lse_all = jnp.zeros((QB, HEAD_B), F32)
        for g in heads:
            lse_all = jnp.where(lane == g, m[g] + jnp.log2(l[g]), lse_all)
        l_ref[...] = lse_all

    return pl.pallas_call(
        body, out_shape=[jax.ShapeDtypeStruct((T, 2 * GW), BF16), jax.ShapeDtypeStruct((2, T, HEAD_B), F32)],
        grid=(2, T // QB),
        in_specs=[_bs((QB, GW), lambda kv, i: (i, kv)), _bs((T, HEAD_B), lambda kv, i: (0, k_col + kv)),
                  _bs((T, HEAD_B), lambda kv, i: (0, B_V // HEAD_B + kv))],
        out_specs=[_bs((QB, GW), lambda kv, i: (i, kv)), _bs((None, QB, HEAD_B), lambda kv, i: (kv, i, 0))],
        compiler_params=_params("parallel", "parallel"), name="b_fwd")(qn, kn, proj)


def _gqa_bwd(qn, kn, proj, o, lse, do, deps=(), k_col=0):
    T = qn.shape[0]
    GW = 4 * HEAD_B

    def body(q_ref, k_ref, v_ref, o_ref, l_ref, do_ref, *rest):
        dq_ref, dk_ref, dv_ref = rest[-3:]
        i = pl.program_id(1)

        @pl.when(i == 0)
        def _():
            dk_ref[...] = jnp.zeros_like(dk_ref)
            dv_ref[...] = jnp.zeros_like(dv_ref)

        k, v = k_ref[...], v_ref[...]
        lse_all = l_ref[...]
        for g in range(4):
            cols = slice(g * HEAD_B, (g + 1) * HEAD_B)
            q, dob = q_ref[:, cols], do_ref[:, cols]
            delta = jnp.sum(dob.astype(F32) * o_ref[:, cols].astype(F32), axis=-1, keepdims=True)
            s = lax.dot_general(q, k, (NT, ((), ())), preferred_element_type=F32)
            p = jnp.exp2(s - lse_all[:, g:g + 1])
            dp = lax.dot_general(dob, v, (NT, ((), ())), preferred_element_type=F32)
            ds = (p * (dp - delta)).astype(BF16)
            dq_ref[:, cols] = jnp.dot(ds, k, preferred_element_type=F32).astype(dq_ref.dtype)
            dk_ref[...] += lax.dot_general(ds, q, (TN, ((), ())), preferred_element_type=F32)
            dv_ref[...] += lax.dot_general(p.astype(BF16), dob, (TN, ((), ())), preferred_element_type=F32)

    return pl.pallas_call(
        body, out_shape=[jax.ShapeDtypeStruct((T, 2 * GW), BF16), jax.ShapeDtypeStruct((T, 2 * HEAD_B), F32),
                         jax.ShapeDtypeStruct((T, 2 * HEAD_B), F32)],
        grid=(2, T // QB_B),
        in_specs=[_bs((QB_B, GW), lambda kv, i: (i, kv)), _bs((T, HEAD_B), lambda kv, i: (0, k_col + kv)),
                  _bs((T, HEAD_B), lambda kv, i: (0, B_V // HEAD_B + kv)), _bs((QB_B, GW), lambda kv, i: (i, kv)),
                  _bs((None, QB_B, HEAD_B), lambda kv, i: (kv, i, 0)), _bs((QB_B, GW), lambda kv, i: (i, kv))] + _any_specs(len(deps)),
        out_specs=[_bs((QB_B, GW), lambda kv, i: (i, kv)), _bs((T, HEAD_B), lambda kv, i: (0, kv)),
                   _bs((T, HEAD_B), lambda kv, i: (0, kv))],
        compiler_params=_params("parallel", "arbitrary"), name="b_bwd")(qn, kn, proj, o, lse, do, *deps)


def _local_step(x, target, small, get_w, put_g, deps=(), prefetch_w=lambda name, after: [], take_rider=lambda steps, after: None):
    T, D = x.shape
    gs = {}

    bias = _bias_tiles(small["rel_bias"])
    cos, sin = _rope_tables(T)
    (x1, h2), ffn1_saved = _ffn_fwd("ffn1", x, small["ffn1_norm"], lambda name, after: get_w(name, [after, bias, cos, sin]), deps,
                                    tail_ins=[small["mix_norm"]], tail_fn=lambda y, g: (y, _norm_fwd(y, g)), tail_outs=(F32, BF16))
    w_in = get_w("w_in", h2)
    nq = w_in.shape[2]
    tpq = nq // WIDTH_A

    def proj_tile(j, k):
        c = j * tpq + k
        return jnp.where(c < 3 * len(DILATIONS), (c % 3) * 3 + c // 3, c)

    proj = _mm("mix_in", (4, tpq),
               [(h2, _resident((T, D), lambda j, k: (0, 0)), w_in, _bs((None, D, WIDTH_A), lambda j, k: (j, 0, k)))],
               jax.ShapeDtypeStruct((T, IN_WIDTH), BF16), _bs((T, WIDTH_A), lambda j, k: (0, proj_tile(j, k))), NN)

    a_views = [_group_view(proj, grp, d) for grp, d in enumerate(DILATIONS)]
    a_outs, a_lses = [], []
    for grp, d in enumerate(DILATIONS):
        o, l = _dil_fwd(a_views[grp], bias[grp], d)
        a_outs.append(o)
        a_lses.append(l)
    o_a = _combine_fwd(a_outs, a_lses)

    qk_gain = jnp.concatenate([jnp.tile(small["q_norm"] * QK_SCALE_LOG2, (8, 1)), jnp.tile(small["k_norm"], (2, 1))])[:, None, :]
    qkn = _qk_fwd("b_qknorm", proj, B_Q, 10, qk_gain, cos, sin, deps=prefetch_w("w_branch_a", proj))
    qn, kn, k_col = qkn, qkn, 8
    o_b, lse_b = _gqa_fwd(qn, kn, proj, k_col)
    ahead = prefetch_w("ffn2_w1", o_b)

    wa, wb, wo = get_w("w_branch_a", o_b), get_w("w_branch_b", o_b), get_w("w_out", o_b)
    bg_a, bg_b = small["b_gate"][:, :D], small["b_gate"][:, D:]
    n_a = wa.shape[0]

    def merge_out(oa_ref, ob_ref, ga_ref, gb_ref, x1_ref, wa_ref, wb_ref, wo_ref, ba_ref, bb_ref, g2_ref, *rest):
        ta_ref, tb_ref, mg_ref, x2_ref, hn_ref = rest[-5:]
        oa = oa_ref[...]
        ta = jnp.concatenate([jnp.dot(oa, wa_ref[j], preferred_element_type=F32) for j in range(n_a)], axis=1)
        tb = jnp.dot(ob_ref[...], wb_ref[...], preferred_element_type=F32)
        sa = _sigmoid(ga_ref[...].astype(F32) + ba_ref[...])
        sb = _sigmoid(gb_ref[...].astype(F32) + bb_ref[...])
        merged = (sa * ta + sb * tb).astype(BF16)
        ta_ref[...], tb_ref[...], mg_ref[...] = ta.astype(BF16), tb.astype(BF16), merged
        y = x1_ref[...] + jnp.dot(merged, wo_ref[...], preferred_element_type=F32)
        x2_ref[...] = y
        hn_ref[...] = _norm_fwd(y, g2_ref[...]).astype(BF16)

    row = _bs((512, D), lambda i: (i, 0))
    gate_specs = [_bs((512, D), lambda i: (i, G_A // D)), _bs((512, D), lambda i: (i, G_B // D))]
    whole2, whole3 = (lambda i: (0, 0)), (lambda i: (0, 0, 0))
    vec = _bs((1, D), whole2)
    t_a, t_b, merged, x2, hn2 = pl.pallas_call(
        merge_out, out_shape=[jax.ShapeDtypeStruct((T, D), BF16)] * 3 + [jax.ShapeDtypeStruct((T, D), F32), jax.ShapeDtypeStruct((T, D), BF16)],
        grid=(T // 512,),
        in_specs=[_bs((512, WIDTH_A), lambda i: (i, 0)), row] + gate_specs + [row, _resident(wa.shape, whole3), _resident((D, D), whole2),
                                                                                _resident((D, D), whole2), vec, vec, vec]
        + _any_specs(len(ahead)),
        out_specs=[row] * 5, compiler_params=_params("parallel"), name="mix_merge_out")(
            o_a, o_b, proj, proj, x1, wa, wb, wo, bg_a, bg_b, small["ffn2_norm"], *ahead)

    def head(xv, g, tv):
        r = _rstd(xv)
        xh = xv * r
        e = xh * g - tv
        dy = e * (1.0 / D)
        dxh = dy * g
        dx = r * (dxh - xh * jnp.mean(dxh * xh, axis=-1, keepdims=True))
        return dx, 0.5 * dx, _colsum(e * e) * (0.5 / D), _colsum(dy * xh)

    (dx3, dx3_half, loss_cols, g_final), ffn2_saved = _ffn_fwd(
        "ffn2", x2, small["ffn2_norm"], get_w, h=hn2, tail_ins=[small["final_norm"].reshape(1, D), target], tail_fn=head,
        tail_outs=(F32, BF16), tail_reds=(D, D))
    gs["final_norm"] = g_final.reshape(D)

    dx2, _, dmix, gs["ffn2_norm"] = _ffn_bwd("ffn2", x2, small["ffn2_norm"], get_w, put_g, ffn2_saved, dx3, dx3_half,
                                             also_bf16=True)
    g_out = _mm_wgrad("mix_bwd_dwout", merged, dmix, a_cols=D // 4, b_cols=None, tm=256, tn=512, J=4).reshape(D, D)

    def merge_out_bwd(dx_ref, ta_ref, tb_ref, ga_ref, gb_ref, wa_ref, wb_ref, wo_ref, ba_ref, bb_ref,
                      dta_ref, dtb_ref, dga_ref, dgb_ref, doa_ref, dob_ref, dba_ref, dbb_ref):
        dm = lax.dot_general(dx_ref[...], wo_ref[...], (NT, ((), ())), preferred_element_type=F32)
        ta, tb = ta_ref[...].astype(F32), tb_ref[...].astype(F32)
        sa = _sigmoid(ga_ref[...].astype(F32) + ba_ref[...])
        sb = _sigmoid(gb_ref[...].astype(F32) + bb_ref[...])
        dga, dgb = dm * ta * sa * (1.0 - sa), dm * tb * sb * (1.0 - sb)
        dta, dtb = (dm * sa).astype(BF16), (dm * sb).astype(BF16)
        dta_ref[...], dtb_ref[...] = dta, dtb
        dga_ref[...], dgb_ref[...] = dga.astype(BF16), dgb.astype(BF16)
        w = wa_ref.shape[2]
        doa = sum(lax.dot_general(dta[:, j * w:(j + 1) * w], wa_ref[j], (NT, ((), ())), preferred_element_type=F32) for j in range(n_a))
        doa_ref[...] = doa.astype(BF16)
        dob_ref[...] = lax.dot_general(dtb, wb_ref[...], (NT, ((), ())), preferred_element_type=F32).astype(BF16)

        @pl.when(pl.program_id(0) == 0)
        def _():
            dba_ref[...] = jnp.zeros_like(dba_ref)
            dbb_ref[...] = jnp.zeros_like(dbb_ref)
        dba_ref[...] += _colsum(dga)
        dbb_ref[...] += _colsum(dgb)

    rowb = _bs((256, D), lambda i: (i, 0))
    gate_specs = [_bs((256, D), lambda i: (i, G_A // D)), _bs((256, D), lambda i: (i, G_B // D))]
    dta, dtb, dga, dgb, do_a, do_b, dba, dbb = pl.pallas_call(
        merge_out_bwd,
        out_shape=[jax.ShapeDtypeStruct((T, D), BF16)] * 4 + [jax.ShapeDtypeStruct((T, WIDTH_A), BF16), jax.ShapeDtypeStruct((T, D), BF16)]
        + [jax.ShapeDtypeStruct((1, D), F32)] * 2,
        grid=(T // 256,),
        in_specs=[rowb, rowb, rowb] + gate_specs + [_resident(wa.shape, whole3), _resident((D, D), whole2), _resident((D, D), whole2), vec, vec],
        out_specs=[rowb] * 4 + [_bs((256, WIDTH_A), lambda i: (i, 0)), rowb, vec, vec],
        compiler_params=_params("arbitrary"), name="mix_merge_out_bwd")(dmix, t_a, t_b, proj, proj, wa, wb, wo, bg_a, bg_b)
    gs["b_gate"] = jnp.concatenate([dba, dbb], axis=1)

    g_a = _mm_wgrad("mix_bwd_dwa", o_a, dta, a_cols=None, b_cols=D // 4, tm=WIDTH_A, tn=256, J=4)
    g_b = _mm_wgrad("mix_bwd_dwb", o_b, dtb, a_cols=D // 4, b_cols=None, tm=256, tn=512, J=4).reshape(D, D)
    deps = put_g({"w_out": g_out, "w_branch_a": g_a, "w_branch_b": g_b})

    dqn, dkn, dv_b = _gqa_bwd(qn, kn, proj, o_b, lse_b, do_b, deps, k_col)
    dq_b, gs["q_norm"] = _qk_bwd("b_bwd_qnorm", dqn, proj, B_Q, 8, small["q_norm"], cos, sin, in_scale=HEAD_B ** -0.5)
    dk_b, gs["k_norm"] = _qk_bwd("b_bwd_knorm", dkn, proj, B_K, 2, small["k_norm"], cos, sin, in_scale=1.0 / LOG2_E)

    do_groups, c_groups = _combine_bwd(do_a, a_outs, a_lses)
    dqs, dks, dvs, dbs = [], [], [], []
    for grp, d in enumerate(DILATIONS):
        dq, dk, dv, db = _dil_bwd(a_views[grp], bias[grp], do_groups[grp], a_lses[grp], c_groups[grp], d)
        dqs.append(dq), dks.append(dk), dvs.append(dv), dbs.append(db)
    gs["rel_bias"] = _bias_grad(jnp.stack(dbs))

    dproj = _assemble_dproj([dqs, dks, dvs], dq_b, dk_b, dv_b, dga, dgb)
    nq = w_in.shape[2]
    g_in = _mm("mix_bwd_dwin", (4, tpq),
               [(h2, _resident((T, D), lambda j, k: (0, 0)), dproj, _bs((T, WIDTH_A), lambda j, k: (0, j * tpq + k)))],
               jax.ShapeDtypeStruct((4, D, nq), BF16), _bs((None, D, WIDTH_A), lambda j, k: (j, 0, k)), TN,
               rider=take_rider(4, dproj))
    deps = put_g({"w_in": g_in})
    dx1, dx1_half, gs["mix_norm"] = _dh_norm_bwd(
        "mix_bwd_dh", 256,
        [(dproj, _bs((256, nq), lambda i, j=j: (i, j)), w_in, _resident((None, D, nq), lambda i, j=j: (j, 0, 0))) for j in range(4)],
        NT, x1, small["mix_norm"], dx2, deps, rider=take_rider(T // 256, deps))

    dx0, _, gs["ffn1_norm"] = _ffn_bwd("ffn1", x, small["ffn1_norm"], get_w, put_g, ffn1_saved, dx1, dx1_half, last=True,
                                       take_rider=take_rider)
    return loss_cols, dx0, gs


def _position():
    return lax.axis_index("x"), lax.axis_index("y"), lax.axis_index("c")


def _any_specs(n):
    return [pl.BlockSpec(memory_space=pl.ANY)] * n


HBM_SPEC = pl.BlockSpec(memory_space=pltpu.HBM)
SEM_SPEC = pl.BlockSpec(memory_space=pltpu.SEMAPHORE)
DATAFLOW_EFFECT = pltpu.SideEffectType.DATAFLOW_SIDE_EFFECTING
N_PEER_CHIPS = 3
LANES = 128


def _quarter_copies(srcs, lands, send_sems, recv_sems, mode):
    x, y, c = _position()
    me = 2 * x + y
    peers = [(1 - x, y, c), (x, 1 - y, c), (1 - x, 1 - y, c)]
    copies = []
    for src, land, send, recv in zip(srcs, lands, send_sems, recv_sems):
        if mode == "sibling":
            copies.append(pltpu.make_async_remote_copy(src_ref=src, dst_ref=land, send_sem=send.at[0], recv_sem=recv.at[0],
                                                       device_id=(x, y, 1 - c), device_id_type=MESH))
            continue
        if mode == "fill":
            half = land.shape[1] // 2
            for p, (px, py, _) in enumerate(peers):
                part = land.at[2 * px + py, pl.ds(c * half, half)]
                copies.append(pltpu.make_async_remote_copy(src_ref=part, dst_ref=part, send_sem=send.at[p], recv_sem=recv.at[p],
                                                           device_id=(x, y, 1 - c), device_id_type=MESH))
            continue
        scatter = mode == "scatter"
        half = land.shape[1] // 2
        mine = land.at[me, pl.ds(c * half, half)]
        for p, (px, py, pc) in enumerate(peers):
            copies.append(pltpu.make_async_remote_copy(
                src_ref=src.at[2 * px + py] if scatter else mine, dst_ref=land.at[me] if scatter else mine,
                send_sem=send.at[p], recv_sem=recv.at[p], device_id=(px, py, pc), device_id_type=MESH))
    return copies


def _fill_from_sibling(name, stacks):
    n = len(stacks)

    def body(*refs):
        outs = refs[n:2 * n]
        send_sems, recv_sems = refs[2 * n:]
        x, y, c = _position()
        copies = []
        for i, ref in enumerate(outs):
            half = ref.shape[1] // 2
            rows = pl.ds(c * half, half)
            for p, k in enumerate((2 * (1 - x) + y, 2 * x + (1 - y), 2 * (1 - x) + (1 - y))):
                cp = pltpu.make_async_remote_copy(ref.at[k, rows], ref.at[k, rows], send_sems.at[3 * i + p], recv_sems.at[3 * i + p],
                                                  device_id=(x, y, 1 - c), device_id_type=MESH)
                cp.start()
                copies.append(cp)
        for cp in copies:
            cp.wait()

    return pl.pallas_call(
        body, out_shape=[jax.ShapeDtypeStruct(s.shape, s.dtype) for s in stacks],
        in_specs=_any_specs(n), out_specs=_any_specs(n), input_output_aliases={i: i for i in range(n)},
        scratch_shapes=[pltpu.SemaphoreType.DMA((N_PEER_CHIPS * n,)), pltpu.SemaphoreType.DMA((N_PEER_CHIPS * n,))],
        compiler_params=pltpu.CompilerParams(has_side_effects=True), name=name)(*stacks)


def _exchange_start(name, srcs, lands, mode):
    n = len(lands)
    arrays = list(lands) if srcs is None else list(srcs) + list(lands)
    k = len(arrays)

    def body(*refs):
        land_refs = refs[k - n:k]
        send_sems, recv_sems = refs[k:k + n], refs[k + n:k + 2 * n]
        token = refs[2 * k + 2 * n]
        for cp in _quarter_copies(refs[:n], land_refs, send_sems, recv_sems, mode):
            cp.start()
        token[...] = jnp.zeros_like(token)

    sem = pltpu.SemaphoreType.DMA((N_PEER_CHIPS,))
    out_shape = [sem] * (2 * n) + [pltpu.HBM(a.shape, a.dtype) for a in arrays] + [jax.ShapeDtypeStruct((8, LANES), F32)]
    res = pl.pallas_call(
        body, name=name, out_shape=out_shape, in_specs=[HBM_SPEC] * k,
        out_specs=[SEM_SPEC] * (2 * n) + [HBM_SPEC] * k + [pl.BlockSpec(memory_space=pltpu.VMEM)],
        input_output_aliases={i: 2 * n + i for i in range(k)},
        compiler_params=pltpu.CompilerParams(has_side_effects=DATAFLOW_EFFECT),
    )(*[pltpu.with_memory_space_constraint(a, pltpu.HBM) for a in arrays])
    thru = res[2 * n:2 * n + k]
    return res[:n], res[n:2 * n], (None if srcs is None else thru[:n]), thru[k - n:], res[2 * n + k]


def _exchange_wait(name, srcs, lands, send_sems, recv_sems, after, mode):
    n = len(lands)
    arrays = list(lands) if srcs is None else list(srcs) + list(lands)
    k = len(arrays)
    after = list(after) if isinstance(after, (list, tuple)) else [after]

    def body(*refs):
        sends, recvs = refs[k:k + n], refs[k + n:k + 2 * n]
        for cp in _quarter_copies(refs[:n], refs[k - n:k], sends, recvs, mode):
            cp.wait_send()
            cp.wait_recv()

    res = pl.pallas_call(
        body, name=name, out_shape=[pltpu.HBM(a.shape, a.dtype) for a in arrays],
        in_specs=[HBM_SPEC] * k + [SEM_SPEC] * (2 * n) + _any_specs(len(after)),
        out_specs=[HBM_SPEC] * k, input_output_aliases={i: i for i in range(k)},
        compiler_params=pltpu.CompilerParams(has_side_effects=DATAFLOW_EFFECT),
    )(*arrays, *send_sems, *recv_sems, *after)
    return (None if srcs is None else res[:n]), res[k - n:]


def _scatter_and_forward(name, stacks, lands, old_srcs, old_lands, old_sends, old_recvs):
    n1, n0 = len(stacks), len(old_lands)
    sibling_lands = [lax.empty(a.shape, a.dtype) for a in old_lands]
    arrays = list(stacks) + list(lands) + list(old_srcs) + list(old_lands) + sibling_lands
    k, s = len(arrays), 2 * n1 + 2 * n0

    def body(*refs):
        new_srcs, new_lands = refs[:n1], refs[n1:2 * n1]
        was_srcs, landed, to_sibling = refs[2 * n1:2 * n1 + n0], refs[2 * n1 + n0:2 * n1 + 2 * n0], refs[2 * n1 + 2 * n0:k]
        was_sends, was_recvs = refs[k:k + n0], refs[k + n0:k + 2 * n0]
        sems = refs[k + 2 * n0:k + 2 * n0 + s]
        token = refs[k + 2 * n0 + s + k]
        for cp in _quarter_copies(new_srcs, new_lands, sems[:n1], sems[n1:2 * n1], "scatter"):
            cp.start()
        for cp in _quarter_copies(was_srcs, landed, was_sends, was_recvs, "scatter"):
            cp.wait_send()
            cp.wait_recv()
        for cp in _quarter_copies(landed, to_sibling, sems[2 * n1:2 * n1 + n0], sems[2 * n1 + n0:], "sibling"):
            cp.start()
        token[...] = jnp.zeros_like(token)

    sem = pltpu.SemaphoreType.DMA((N_PEER_CHIPS,))
    res = pl.pallas_call(
        body, name=name,
        out_shape=[sem] * s + [pltpu.HBM(a.shape, a.dtype) for a in arrays] + [jax.ShapeDtypeStruct((8, LANES), F32)],
        in_specs=[HBM_SPEC] * k + [SEM_SPEC] * (2 * n0),
        out_specs=[SEM_SPEC] * s + [HBM_SPEC] * k + [pl.BlockSpec(memory_space=pltpu.VMEM)],
        input_output_aliases={i: s + i for i in range(k)},
        compiler_params=pltpu.CompilerParams(has_side_effects=DATAFLOW_EFFECT),
    )(*[pltpu.with_memory_space_constraint(a, pltpu.HBM) for a in arrays], *old_sends, *old_recvs)
    thru = res[s:s + k]
    scatter = (res[:n1], res[n1:2 * n1], thru[:n1], thru[n1:2 * n1])
    sibling = (res[2 * n1:2 * n1 + n0], res[2 * n1 + n0:s], thru[2 * n1 + n0:2 * n1 + 2 * n0], thru[2 * n1 + 2 * n0:])
    return scatter, sibling, res[s + k]


def _own_slots(name, srcs, from_stack=False):
    n = len(srcs)
    me = (2 * lax.axis_index("x") + lax.axis_index("y")).astype(jnp.int32).reshape(1)

    def body(me_ref, *refs):
        for x_ref, o_ref in zip(refs[:n], refs[n:]):
            o_ref[...] = x_ref[...].astype(o_ref.dtype)

    in_specs, out_specs, out_shape = [], [], []
    for src in srcs:
        R, C = src.shape[-2:]
        in_specs.append(pl.BlockSpec((None, R // 2, C), lambda i, me_ref: (me_ref[0], i, 0)) if from_stack
                        else pl.BlockSpec((R // 2, C), lambda i, me_ref: (i, 0)))
        out_specs.append(pl.BlockSpec((None, R // 2, C), lambda i, me_ref: (me_ref[0], i, 0)))
        out_shape.append(jax.ShapeDtypeStruct((4, R, C), BF16))
    grid_spec = pltpu.PrefetchScalarGridSpec(num_scalar_prefetch=1, grid=(2,), in_specs=in_specs, out_specs=out_specs)
    return pl.pallas_call(body, out_shape=out_shape, grid_spec=grid_spec, compiler_params=_params("parallel"), name=name)(me, *srcs)


def _allreduce_small(buf):
    R, C = buf.shape
    flips = [(fx, fy, fc) for fx in (0, 1) for fy in (0, 1) for fc in (0, 1)][1:]

    def body(in_ref, out_ref, land_ref, send_sems, recv_sems):
        x, y, c = _position()
        me = 4 * x + 2 * y + c
        copies = []
        for k, (fx, fy, fc) in enumerate(flips):
            px, py, pc = (1 - x if fx else x), (1 - y if fy else y), (1 - c if fc else c)
            cp = pltpu.make_async_remote_copy(in_ref, land_ref.at[me], send_sems.at[k], recv_sems.at[k],
                                              device_id=(px, py, pc), device_id_type=MESH)
            cp.start()
            copies.append(cp)
        land_ref[me] = in_ref[...]
        for cp in copies:
            cp.wait()
        acc = land_ref[0]
        for k in range(1, 8):
            acc = acc + land_ref[k]
        out_ref[...] = acc

    return pl.pallas_call(
        body, out_shape=jax.ShapeDtypeStruct((R, C), F32),
        in_specs=[pl.BlockSpec(memory_space=pltpu.VMEM)], out_specs=pl.BlockSpec(memory_space=pltpu.VMEM),
        scratch_shapes=[pltpu.VMEM((8, R, C), F32), pltpu.SemaphoreType.DMA((7,)), pltpu.SemaphoreType.DMA((7,))],
        compiler_params=pltpu.CompilerParams(has_side_effects=True), name="allreduce_small")(buf)


def _adamw_math(w, g, m, v):
    m2 = ADAM_B1 * m + (1.0 - ADAM_B1) * g
    v2 = ADAM_B2 * v + (1.0 - ADAM_B2) * (g * g)
    m_hat = m2 / (1.0 - ADAM_B1 ** ADAM_STEP)
    v_hat = v2 / (1.0 - ADAM_B2 ** ADAM_STEP)
    delta = -ADAM_LR * (m_hat / (jnp.sqrt(v_hat) + ADAM_EPS) + ADAM_WD * w)
    return delta, m2, v2


def _adamw_from_partials(wv, mv, vv, *parts):
    def four(a, b, c, d):
        return ((a.astype(F32) + b.astype(F32)) + c.astype(F32)) + d.astype(F32)

    g = four(*parts[:4]) + four(*parts[4:])
    return (g,) + _adamw_math(wv, g, mv, vv)


def _adamw_big(name, w, m, v, mine, theirs):
    R, C = w.shape
    rows = 256 if R % 256 == 0 else R // 2
    nrb = R // rows
    slots = [_tiled(s.reshape(4 * R, C), None, 0, k * nrb) for s in (mine, theirs) for k in range(4)]
    return _ew(name, _adamw_from_partials, [_tiled(w), _tiled(m), _tiled(v)] + slots, [(F32, C)] * 4, n_rows=R, rows=rows)


def _adamw_rider(w, m, v, mine, theirs, steps, deliver):
    R, C = w.shape
    fits = [nb for nb in range(1, steps + 1) if R % nb == 0 and (R // nb) % 16 == 0]
    if not fits:
        return None
    nb = fits[-1]
    rows = R // nb

    def blocks(first):
        return pl.BlockSpec((rows, C), lambda *g: (first + jnp.minimum(g[0], nb - 1), 0))

    flat = [s.reshape(4 * R, C) for s in (mine, theirs)]
    return dict(operands=[w, m, v] + [f for f in flat for _ in range(4)],
                in_specs=[blocks(0)] * 3 + [blocks(k * nb) for _ in flat for k in range(4)],
                out_shape=[jax.ShapeDtypeStruct((R, C), F32)] * 4, out_specs=[blocks(0)] * 4,
                n_blocks=nb, fn=_adamw_from_partials, deliver=lambda outs: deliver(*outs))


BIG = ("ffn1_w1", "ffn1_w3", "ffn1_w2", "w_in", "w_branch_a", "w_branch_b", "w_out", "ffn2_w1", "ffn2_w3", "ffn2_w2")
SMALL = ("ffn1_norm", "mix_norm", "b_gate", "q_norm", "k_norm", "rel_bias", "ffn2_norm", "final_norm")
ORDER = ("ffn1_norm", "ffn1_w1", "ffn1_w3", "ffn1_w2", "mix_norm", "w_in", "b_gate", "q_norm", "k_norm", "rel_bias",
         "w_branch_a", "w_branch_b", "w_out", "ffn2_norm", "ffn2_w1", "ffn2_w3", "ffn2_w2", "final_norm")
TRANSPOSED = ("ffn1_w1", "ffn1_w3", "ffn2_w1", "ffn2_w3")
SIBLING_LAG = 2
EARLY_FORWARDS = 2
LONG_HOST_STEPS = 8
GATHER_GROUPS = (("ffn1_w1", "ffn1_w3"), ("ffn1_w2",), ("w_in",), ("w_branch_a", "w_branch_b", "w_out"),
                 ("ffn2_w1", "ffn2_w3", "ffn2_w2"))


def _pack_small(d):
    rows = []
    for n in SMALL:
        flat = d[n].reshape(-1)
        pad = (-flat.shape[0]) % LANES
        rows.append(jnp.pad(flat, (0, pad)).reshape(-1, LANES))
    buf = jnp.concatenate(rows, axis=0)
    return jnp.pad(buf, ((0, (-buf.shape[0]) % 8), (0, 0)))


def _unpack_small(buf, like):
    out, r = {}, 0
    for n in SMALL:
        size = like[n].size
        nr = -(-size // LANES)
        out[n] = buf[r:r + nr].reshape(-1)[:size].reshape(like[n].shape)
        r += nr
    return out


def kernel(x, ffn1_norm, ffn1_w1, ffn1_w3, ffn1_w2, mix_norm, w_in, b_gate, q_norm, k_norm, rel_bias, w_branch_a, w_branch_b, w_out, ffn2_norm, ffn2_w1, ffn2_w3, ffn2_w2, final_norm, loss_target, m_ffn1_norm, m_ffn1_w1, m_ffn1_w3, m_ffn1_w2, m_mix_norm, m_w_in, m_b_gate, m_q_norm, m_k_norm, m_rel_bias, m_w_branch_a, m_w_branch_b, m_w_out, m_ffn2_norm, m_ffn2_w1, m_ffn2_w3, m_ffn2_w2, m_final_norm, v_ffn1_norm, v_ffn1_w1, v_ffn1_w3, v_ffn1_w2, v_mix_norm, v_w_in, v_b_gate, v_q_norm, v_k_norm, v_rel_bias, v_w_branch_a, v_w_branch_b, v_w_out, v_ffn2_norm, v_ffn2_w1, v_ffn2_w3, v_ffn2_w2, v_final_norm):
    given = dict(locals())
    w = {n: given[n] for n in ORDER}
    m = {n: given["m_" + n] for n in ORDER}
    v = {n: given["v_" + n] for n in ORDER}
    T, D = x.shape[1], x.shape[2]

    def stored(a, n):
        a = a.reshape(a.shape[1:])
        return a.T if n in TRANSPOSED else a

    def returned(a, n):
        return (a.T if n in TRANSPOSED else a).reshape(w[n].shape)

    quarter = {n: stored(w[n], n) for n in BIG}
    send, recv, _, land_thru, token = _exchange_start(
        "gather_start", None, _own_slots("own_weights", [quarter[n] for n in BIG]), "gather")
    index = {n: i for i, n in enumerate(BIG)}
    ready, filling = {}, {}

    def landed_halves(group, after):
        ids = [index[n] for n in group]
        return _exchange_wait("gather_wait_" + group[0], None, [land_thru[i] for i in ids],
                              [send[i] for i in ids], [recv[i] for i in ids], after, "gather")[1]

    def prefetch_w(name, after):
        group = next(g for g in GATHER_GROUPS if name in g)
        started = _exchange_start("fill_start_" + group[0], None, landed_halves(group, after), "fill")
        filling[group] = started
        return [started[4]]

    def get_w(name, after):
        if name not in ready:
            group = next(g for g in GATHER_GROUPS if name in g)
            if group in filling:
                f_send, f_recv, _, thru, _ = filling[group]
                stacks = _exchange_wait("fill_wait_" + group[0], None, thru, f_send, f_recv, after, "fill")[1]
            else:
                stacks = _fill_from_sibling("gather_fill_" + group[0], landed_halves(group, after))
            for n, st in zip(group, stacks):
                ready[n] = st.reshape(D, D) if n in ("w_branch_b", "w_out") else st
        return ready[name]

    scattered, forwarded = [], []

    def forward_oldest(after):
        names, s_sem, r_sem, srcs, lands = scattered.pop(0)
        _, landed = _exchange_wait("scatter_wait_" + names[0], srcs, lands, s_sem, r_sem, after, "scatter")
        started = _exchange_start("sibling_start_" + names[0], landed, [lax.empty(a.shape, a.dtype) for a in landed], "sibling")
        forwarded.append((names,) + tuple(started[:4]))
        return started[4]

    def put_g(grads):
        names = list(grads)
        stacks = [grads[n].reshape((4,) + quarter[n].shape) for n in names]
        lands = _own_slots("own_grad_" + names[0], stacks, from_stack=True)
        if len(scattered) < (1 if len(forwarded) < EARLY_FORWARDS else SIBLING_LAG):
            started = _exchange_start("scatter_start_" + names[0], stacks, lands, "scatter")
            scattered.append((names,) + tuple(started[:4]))
            return [started[4]]
        old_names, s_sem, r_sem, old_srcs, old_lands = scattered.pop(0)
        scatter, sibling, token = _scatter_and_forward("scatter_start_" + names[0], stacks, lands, old_srcs, old_lands, s_sem, r_sem)
        scattered.append((names,) + scatter)
        forwarded.append((old_names,) + sibling)
        return [token]

    grads, deltas, new_m, new_v = {}, {}, {}, {}
    arrived, riding = {}, set()

    def partials(gi, after):
        if gi not in arrived:
            names, s_sem, r_sem, srcs, lands = forwarded[gi]
            arrived[gi] = _exchange_wait("sibling_wait_" + names[0], srcs, lands, s_sem, r_sem, after, "sibling")
        return arrived[gi]

    def deliver_to(n):
        def deliver(*res):
            grads[n], deltas[n], new_m[n], new_v[n] = [returned(r, n) for r in res]
        return deliver

    def take_rider(steps, after):
        cap = None if steps >= LONG_HOST_STEPS else quarter["ffn1_w2"].size
        waiting = [(quarter[n].size, gi, k, n) for gi, entry in enumerate(forwarded) for k, n in enumerate(entry[0])
                   if n not in riding and (cap is None or quarter[n].size <= cap)]
        for _, gi, k, n in sorted(waiting, reverse=True):
            mine, theirs = partials(gi, after)
            rider = _adamw_rider(quarter[n], stored(m[n], n), stored(v[n], n), mine[k], theirs[k], steps, deliver_to(n))
            if rider is not None:
                riding.add(n)
                return rider
        return None

    small = {n: w[n] for n in SMALL}
    packed = [_pack_small({n: d[n] for n in SMALL}) for d in (w, m, v)]
    loss_cols, grad_x, gs = _local_step(x.reshape(T, D), loss_target.reshape(T, D), small, get_w, put_g, deps=[token] + packed,
                                        prefetch_w=prefetch_w, take_rider=take_rider)

    after = grad_x
    while scattered:
        after = forward_oldest(after)
    for gi, entry in enumerate(forwarded):
        mine, theirs = partials(gi, after)
        for n, a, b in zip(entry[0], mine, theirs):
            if n not in riding:
                deliver_to(n)(*_adamw_big(f"adamw_{n}", quarter[n], stored(m[n], n), stored(v[n], n), a, b))

    gs = {n: gs[n].reshape(w[n].shape) for n in SMALL}
    packed_g = _pack_small(gs)
    n_small = packed_g.shape[0]
    summed = _allreduce_small(jnp.concatenate([packed_g, loss_cols.reshape(-1, LANES)], axis=0))
    g_small, loss = summed[:n_small], jnp.sum(summed[n_small:])
    R = g_small.shape[0]
    res = _ew("adamw_small", lambda wv, mv, vv, g: (g,) + _adamw_math(wv, g, mv, vv),
              [_tiled(packed[0]), _tiled(packed[1]), _tiled(packed[2]), _tiled(g_small)], [(F32, LANES)] * 4, n_rows=R, rows=R)
    for d, buf in zip((grads, deltas, new_m, new_v), res):
        d.update(_unpack_small(buf, w))

    return (loss, grad_x.reshape(x.shape), *[grads[n] for n in ORDER], *[deltas[n] for n in ORDER],
            *[new_m[n] for n in ORDER], *[new_v[n] for n in ORDER])
```

```python
import functools
import math

import numpy as np
import jax
import jax.numpy as jnp
from jax import lax
from jax.experimental import pallas as pl
from jax.experimental.pallas import tpu as pltpu

F32 = jnp.float32
BF16 = jnp.bfloat16
MESH = pl.DeviceIdType.MESH

NEG_INF = -1e30
EPS = 1e-6
GRID_W = 64
ROPE_THETA = 10000.0
DILATIONS = (1, 4, 16)
BAND_HALF = 64
HEAD_A = 64
HEADS_A = 8
WIDTH_A = HEADS_A * HEAD_A
HEAD_B = 128
LOG2_E = math.log2(math.e)
QK_SCALE_LOG2 = HEAD_B ** -0.5 * LOG2_E
N_BUCKETS = 32
MAX_DISTANCE = 1024
ADAM_LR, ADAM_B1, ADAM_B2, ADAM_EPS, ADAM_WD, ADAM_STEP = 0.001, 0.9, 0.999, 1e-08, 0.01, 10

B_Q, B_K, B_V = 4608, 5632, 5888
G_A, G_B = 6144, 7168
IN_WIDTH = 8192

VMEM_LIMIT_BYTES = 56 * 1024 * 1024
QB_A = 128
QB_B = 256


def _params(*sem):
    return pltpu.CompilerParams(dimension_semantics=sem, vmem_limit_bytes=VMEM_LIMIT_BYTES)


def _bs(shape, fn):
    return pl.BlockSpec(shape, fn)


def _resident(shape, fn):
    return pl.BlockSpec(shape, fn, pipeline_mode=pl.Buffered(1))


def _mm(name, grid, pairs, out_shape, out_spec, dims, *, extras=(), epilogue=None, deps=(), reds=(), rider=None):
    n_pairs, n_extra, n_deps = len(pairs), len(extras), len(deps)
    operands = [p[0] for p in pairs] + [p[2] for p in pairs] + [e[0] for e in extras] + list(deps)
    in_specs = [p[1] for p in pairs] + [p[3] for p in pairs] + [e[1] for e in extras] + _any_specs(n_deps)
    single = not isinstance(out_shape, (list, tuple))
    out_shapes = [out_shape] if single else list(out_shape)
    out_specs = [out_spec] if single else list(out_spec)
    n_out = len(out_shapes)
    out_shapes += [jax.ShapeDtypeStruct((1, w), F32) for w in reds]
    out_specs += [_bs((1, w), lambda *_: (0, 0)) for w in reds]
    n_rin = 0
    if rider is not None:
        assert rider["n_blocks"] <= grid[0]
        n_rin = len(rider["operands"])
        operands += list(rider["operands"])
        in_specs += list(rider["in_specs"])
        out_shapes += list(rider["out_shape"])
        out_specs += list(rider["out_specs"])

    def body(*refs):
        a_refs, b_refs = refs[:n_pairs], refs[n_pairs:2 * n_pairs]
        e_refs = refs[2 * n_pairs:2 * n_pairs + n_extra]
        o_refs = refs[2 * n_pairs + n_extra + n_deps + n_rin:]
        if rider is not None:
            r_in = refs[2 * n_pairs + n_extra + n_deps:2 * n_pairs + n_extra + n_deps + n_rin]
            r_out = o_refs[n_out + len(reds):]

            @pl.when(pl.program_id(0) < rider["n_blocks"])
            def _():
                for ref, val in zip(r_out, rider["fn"](*[r[...] for r in r_in])):
                    ref[...] = val.astype(ref.dtype)
        acc = None
        for a_ref, b_ref in zip(a_refs, b_refs):
            t = lax.dot_general(a_ref[...], b_ref[...], (dims, ((), ())), preferred_element_type=F32)
            acc = t if acc is None else acc + t
        vals = acc if epilogue is None else epilogue(acc, *[e[...] for e in e_refs])
        if not isinstance(vals, (list, tuple)):
            vals = (vals,)
        for o_ref, v in zip(o_refs[:n_out], vals[:n_out]):
            o_ref[...] = v.astype(o_ref.dtype)
        if reds:
            first = functools.reduce(jnp.logical_and, [pl.program_id(ax) == 0 for ax in range(len(grid))])
            for r_ref, v in zip(o_refs[n_out:], vals[n_out:]):
                @pl.when(first)
                def _(r_ref=r_ref):
                    r_ref[...] = jnp.zeros_like(r_ref)
                r_ref[...] += v

    sem = ["arbitrary" if (reds or rider is not None) else "parallel"] * len(grid)
    res = pl.pallas_call(
        body, out_shape=out_shapes, grid=grid, in_specs=in_specs, out_specs=out_specs,
        compiler_params=_params(*sem), name=name)(*operands)
    if rider is not None:
        rider["deliver"](res[n_out + len(reds):])
        res = res[:n_out + len(reds)]
    return res[0] if (single and not reds) else res


NN = ((1,), (0,))
NT = ((1,), (1,))
TN = ((0,), (0,))


def _mm_wgrad(name, a, b, *, a_cols, b_cols, tm, tn, J, deps=(), rider=None):
    def pick(arr, cols, t):
        if arr.ndim == 3:
            T, c = arr.shape[1], arr.shape[2]
            t = min(t, c)
            return T, c, t, (lambda sel: _bs((None, T, t), lambda j, i, k: (j, 0, sel(i, k))))
        T = arr.shape[0]
        c = arr.shape[1] if cols is None else cols
        t = min(t, c)
        per = c // t
        if cols is None:
            if per == 1:
                return T, c, t, (lambda sel: _resident((T, t), lambda j, i, k: (0, 0)))
            return T, c, t, (lambda sel: _bs((T, t), lambda j, i, k: (0, sel(i, k))))
        return T, c, t, (lambda sel: _bs((T, t), lambda j, i, k: (0, j * per + sel(i, k))))
    _, ca, tm, mk_a = pick(a, a_cols, tm)
    _, cb, tn, mk_b = pick(b, b_cols, tn)
    return _mm(name, (J, ca // tm, cb // tn),
               [(a, mk_a(lambda i, k: i), b, mk_b(lambda i, k: k))],
               jax.ShapeDtypeStruct((J, ca, cb), BF16), _bs((None, tm, tn), lambda j, i, k: (j, i, k)), TN, deps=deps, rider=rider)


def _tiled(arr, width=None, col=0, rowblk=0):
    return ("t", arr, arr.shape[1] if width is None else width, col, rowblk)


def _table(arr):
    return ("f", arr)


def _whole(arr):
    return ("w", arr)


def _ew(name, fn, ins, outs, *, n_rows, rows, reds=(), ncols=1, deps=()):
    nrb = n_rows // rows
    n_deps = len(deps)
    operands, in_specs = [], []
    for spec in ins:
        if spec[0] == "t":
            _, arr, width, col, rowblk = spec
            step = 1 if ncols > 1 else 0
            in_specs.append(_bs((rows, width), lambda c, i, col=col, rowblk=rowblk, step=step: (rowblk + i, col + c * step)))
        elif spec[0] == "f":
            arr = spec[1]
            in_specs.append(_bs((rows, arr.shape[1]), lambda c, i: (i, 0)))
        else:
            arr = spec[1]
            nd = arr.ndim
            if nd == 3:
                in_specs.append(_bs((None,) + arr.shape[1:], lambda c, i: (c, 0, 0)))
            else:
                in_specs.append(_bs(arr.shape, lambda c, i, nd=nd: (0,) * nd))
        operands.append(arr)
    out_shapes = [jax.ShapeDtypeStruct((n_rows, ncols * w), dt) for dt, w in outs]
    out_specs = [_bs((rows, w), lambda c, i: (i, c)) for _, w in outs]
    out_shapes += [jax.ShapeDtypeStruct((ncols, 1, w), F32) for w in reds]
    out_specs += [_bs((None, 1, w), lambda c, i: (c, 0, 0)) for w in reds]
    n_in, n_out, n_red = len(ins), len(outs), len(reds)
    operands += list(deps)
    in_specs += _any_specs(n_deps)

    def body(*refs):
        vals = fn(*[r[...] for r in refs[:n_in]])
        if not isinstance(vals, (tuple, list)):
            vals = (vals,)
        o_refs = refs[n_in + n_deps:]
        for o_ref, v in zip(o_refs[:n_out], vals[:n_out]):
            o_ref[...] = v.astype(o_ref.dtype)
        if n_red:
            i = pl.program_id(1)
            for r_ref, v in zip(o_refs[n_out:], vals[n_out:]):
                @pl.when(i == 0)
                def _(r_ref=r_ref):
                    r_ref[...] = jnp.zeros_like(r_ref)
                r_ref[...] += v

    res = pl.pallas_call(
        body, out_shape=out_shapes, grid=(ncols, nrb), in_specs=in_specs, out_specs=out_specs,
        compiler_params=_params("parallel", "arbitrary" if n_red else "parallel"), name=name)(*operands)
    return res


def _colsum(v):
    return jnp.sum(v, axis=0, keepdims=True)


def _rstd(x):
    return lax.rsqrt(jnp.mean(x * x, axis=-1, keepdims=True) + EPS)


def _sigmoid(x):
    return 0.5 * jnp.tanh(0.5 * x) + 0.5


def _norm_fwd(x, g):
    return x * _rstd(x) * g


def _norm_bwd(x, g, dy):
    r = _rstd(x)
    xh = x * r
    dxh = dy * g
    dx = r * (dxh - xh * jnp.mean(dxh * xh, axis=-1, keepdims=True))
    return dx, dy * xh


def _row_spec(arr, rows):
    if arr.shape[0] == 1:
        return _bs(arr.shape, lambda i: (0, 0))
    return _bs((rows, arr.shape[1]), lambda i: (i, 0))


def _ffn_fwd(tag, x, gain, get_w, deps=(), *, h=None, tail_ins=(), tail_fn=None, tail_outs=(F32,), tail_reds=()):
    T, D = x.shape
    if h is None:
        (h,) = _ew(f"{tag}_norm", lambda xv, g: _norm_fwd(xv, g), [_tiled(x), _whole(gain)], [(BF16, D)], n_rows=T, rows=512,
                   deps=deps)
    w1, w3 = get_w(f"{tag}_w1", h), get_w(f"{tag}_w3", h)
    J, f, _ = w1.shape
    tm = 1024

    def up(h_ref, w1_ref, w3_ref, u_ref, g_ref, a_ref):
        hv = h_ref[...]
        u = lax.dot_general(hv, w1_ref[...], (NT, ((), ())), preferred_element_type=F32)
        g = lax.dot_general(hv, w3_ref[...], (NT, ((), ())), preferred_element_type=F32)
        u_ref[...] = u.astype(BF16)
        g_ref[...] = g.astype(BF16)
        a_ref[...] = (u * _sigmoid(u) * g).astype(BF16)

    slab = _bs((None, tm, f), lambda j, i: (j, i, 0))
    w_spec = _bs((None, f, D), lambda j, i: (j, 0, 0))
    u, g, a = pl.pallas_call(
        up, out_shape=[jax.ShapeDtypeStruct((J, T, f), BF16)] * 3, grid=(J, T // tm),
        in_specs=[_bs((tm, D), lambda j, i: (i, 0)), w_spec, w_spec], out_specs=[slab] * 3,
        compiler_params=_params("parallel", "parallel"), name=f"{tag}_up")(h, w1, w3)
    w2 = get_w(f"{tag}_w2", a)
    def tail(acc, xv, *rest):
        y = xv + 0.5 * acc
        return y if tail_fn is None else tail_fn(y, *rest)

    row = _bs((512, D), lambda i: (i, 0))
    res = _mm(f"{tag}_down", (T // 512,),
              [(a, _bs((None, 512, f), lambda i, j=j: (j, i, 0)), w2, _resident((None, f, D), lambda i, j=j: (j, 0, 0)))
               for j in range(J)],
              [jax.ShapeDtypeStruct((T, D), dt) for dt in tail_outs], [row] * len(tail_outs), NN,
              extras=[(x, row)] + [(t, _row_spec(t, 512)) for t in tail_ins], epilogue=tail, reds=tail_reds)
    return res, (h, u, g, a)


def _dh_norm_bwd(name, rows, pairs, dims, x, gain, dres, deps, also_bf16=False, rider=None):
    T, D = x.shape

    def epilogue(dh, xv, gv, dr):
        dx, dgr = _norm_bwd(xv, gv, dh)
        dx = dx + dr
        return (dx, 0.5 * dx) + ((dx,) if also_bf16 else ()) + (_colsum(dgr),)

    dts = [F32, BF16] + ([BF16] if also_bf16 else [])
    row = _bs((rows, D), lambda i: (i, 0))
    return _mm(name, (T // rows,), pairs, [jax.ShapeDtypeStruct((T, D), dt) for dt in dts], [row] * len(dts), dims,
               extras=[(x, row), (gain, _row_spec(gain, rows)), (dres, row)], epilogue=epilogue, deps=deps, reds=(D,), rider=rider)


def _ffn_bwd(tag, x, gain, get_w, put_g, saved, dy, dy_half, also_bf16=False, last=False, take_rider=lambda steps, after: None):
    h, u, g, a = saved
    T, D = x.shape
    w1, w3, w2 = [get_w(f"{tag}_{n}", dy_half) for n in ("w1", "w3", "w2")]
    J, f, _ = w1.shape
    dw2 = _mm_wgrad(f"{tag}_bwd_dw2", a, dy_half, a_cols=None, b_cols=None, tm=f, tn=D, J=J, rider=take_rider(J, dy_half))
    deps = put_g({f"{tag}_w2": dw2}) if last else []
    tm = 1024

    def up_bwd(dy_ref, w2_ref, u_ref, g_ref, *rest):
        du_ref, dg_ref = rest[-2:]
        da = lax.dot_general(dy_ref[...], w2_ref[...], (NT, ((), ())), preferred_element_type=F32)
        uv, gv = u_ref[...].astype(F32), g_ref[...].astype(F32)
        s = _sigmoid(uv)
        silu = uv * s
        du_ref[...] = (da * gv * (s + silu - silu * s)).astype(BF16)
        dg_ref[...] = (da * silu).astype(BF16)

    slab = _bs((None, tm, f), lambda j, i: (j, i, 0))
    du, dg = pl.pallas_call(
        up_bwd, out_shape=[jax.ShapeDtypeStruct((J, T, f), BF16)] * 2, grid=(J, T // tm),
        in_specs=[_bs((tm, D), lambda j, i: (i, 0)), _bs((None, f, D), lambda j, i: (j, 0, 0)), slab, slab] + _any_specs(len(deps)),
        out_specs=[slab] * 2, compiler_params=_params("parallel", "parallel"), name=f"{tag}_bwd_up")(dy_half, w2, u, g, *deps)
    dw1 = _mm_wgrad(f"{tag}_bwd_dw1", du, h, a_cols=None, b_cols=None, tm=f, tn=D, J=J)
    deps = put_g({f"{tag}_w1": dw1}) if last else []
    dw3 = _mm_wgrad(f"{tag}_bwd_dw3", dg, h, a_cols=None, b_cols=None, tm=f, tn=D, J=J, deps=deps)
    deps = put_g({f"{tag}_w3": dw3} if last else {f"{tag}_w2": dw2, f"{tag}_w1": dw1, f"{tag}_w3": dw3})
    pairs = []
    for j in range(J):
        a_spec = _bs((None, 256, f), lambda i, j=j: (j, i, 0))
        w_spec = _resident((None, f, D), lambda i, j=j: (j, 0, 0))
        pairs += [(du, a_spec, w1, w_spec), (dg, a_spec, w3, w_spec)]
    return _dh_norm_bwd(f"{tag}_bwd_dh", 256, pairs, NN, x, gain, dy, deps, also_bf16, rider=take_rider(T // 256, dw3))


def _t5_bucket(rel):
    n = N_BUCKETS // 2
    max_exact = n // 2
    ret = jnp.where(rel > 0, n, 0)
    a = jnp.abs(rel)
    af = jnp.maximum(a, 1).astype(F32)
    large = max_exact + (jnp.log(af / max_exact) / math.log(MAX_DISTANCE / max_exact) * (n - max_exact)).astype(jnp.int32)
    large = jnp.minimum(large, n - 1)
    return ret + jnp.where(a < max_exact, a, large)


WIN_A = QB_A + 2 * BAND_HALF
WIN_SHIFTS = (0, BAND_HALF, 2 * BAND_HALF)


def _window_variant(n, nblk):
    return jnp.where(n == 0, 0, jnp.where(n == nblk - 1, 2, 1))


def _window_start(n, nblk):
    return pl.multiple_of(jnp.clip(n * QB_A - BAND_HALF, 0, nblk * QB_A - WIN_A), BAND_HALF)


def _band_steps(xp=jnp):
    qi = xp.arange(QB_A, dtype=xp.int32)[None, :, None]
    kj = xp.arange(WIN_A, dtype=xp.int32)[None, None, :]
    return kj - qi - xp.asarray(WIN_SHIFTS, dtype=xp.int32)[:, None, None]


def _bias_tiles(rel_bias):
    wide = QB_A + 2 * WIN_SHIFTS[-1]
    qi = jnp.arange(QB_A, dtype=jnp.int32)[:, None]
    steps = jnp.arange(wide, dtype=jnp.int32)[None, :] - WIN_SHIFTS[-1] - qi
    buckets = jnp.stack([_t5_bucket(steps * d) for d in DILATIONS])
    inband = (jnp.abs(steps) <= BAND_HALF).astype(jnp.int32)
    n_heads = rel_bias.shape[1]

    def body(tab_ref, b_ref, m_ref, o_ref):
        hd = pl.program_id(0)
        bkt = b_ref[...]
        acc = jnp.zeros(bkt.shape, F32)
        for b in range(N_BUCKETS):
            acc = jnp.where(bkt == b, tab_ref[b, hd], acc)
        o_ref[...] = jnp.where(m_ref[...] > 0, acc, NEG_INF)

    base = pl.pallas_call(
        body, out_shape=jax.ShapeDtypeStruct((n_heads, QB_A, wide), F32), grid=(n_heads,),
        in_specs=[pl.BlockSpec(memory_space=pltpu.SMEM),
                  _bs((None, QB_A, wide), lambda hd: (hd // HEADS_A, 0, 0)),
                  _bs((QB_A, wide), lambda hd: (0, 0))],
        out_specs=_bs((None, QB_A, wide), lambda hd: (hd, 0, 0)),
        compiler_params=_params("parallel"), name="a_bias_tiles")(rel_bias, buckets, inband)
    base = base.reshape(len(DILATIONS), HEADS_A, QB_A, wide)
    return jnp.stack([base[..., WIN_SHIFTS[-1] - s:WIN_SHIFTS[-1] - s + WIN_A] for s in WIN_SHIFTS], axis=1)


def _bias_grad(dbias):
    steps = _band_steps(np)
    inband = np.abs(steps) <= BAND_HALF
    present = []
    for d in DILATIONS:
        rel = steps * d
        a = np.abs(rel)
        large = 8 + (np.log(np.maximum(a, 1) / 8.0) / math.log(MAX_DISTANCE / 8.0) * 8).astype(np.int64)
        bk = np.where(rel > 0, 16, 0) + np.where(a < 8, a, np.minimum(large, 15))
        present.append([sorted(set(bk[v][inband[v]].tolist())) for v in range(3)])
    buckets = jnp.stack([_t5_bucket(_band_steps() * d) for d in DILATIONS])
    n_heads = len(DILATIONS) * HEADS_A

    def body(b_ref, d_ref, o_ref):
        row = lax.broadcasted_iota(jnp.int32, (N_BUCKETS, n_heads), 0)
        col = lax.broadcasted_iota(jnp.int32, (N_BUCKETS, n_heads), 1)
        out = jnp.zeros((N_BUCKETS, n_heads), F32)
        for grp in range(len(DILATIONS)):
            for hh in range(HEADS_A):
                hd = grp * HEADS_A + hh
                for b in sorted(set(sum(present[grp], []))):
                    tot = jnp.zeros((), F32)
                    for v in range(3):
                        if b in present[grp][v]:
                            tot = tot + jnp.sum(jnp.where(b_ref[grp, v] == b, d_ref[grp, v, hh], 0.0))
                    out = jnp.where((row == b) & (col == hd), tot, out)
        o_ref[...] = out

    return pl.pallas_call(
        body, out_shape=jax.ShapeDtypeStruct((N_BUCKETS, n_heads), F32),
        compiler_params=pltpu.CompilerParams(vmem_limit_bytes=VMEM_LIMIT_BYTES), name="a_bias_grad")(buckets, dbias)


def _lane_is_second_head(shape):
    return lax.broadcasted_iota(jnp.int32, shape, len(shape) - 1) >= HEAD_A


VIEW_ROWS = 512


def _view_chunks():
    return [pltpu.VMEM((VIEW_ROWS, LANES), F32)] * (WIDTH_A // LANES)


def _rows_to_view(x_ref, col, o_ref, ocol, d, chunks):
    n = VIEW_ROWS // d
    for c, scr in enumerate(chunks):
        scr[...] = x_ref[:, col + c * LANES:col + (c + 1) * LANES].astype(F32)
        for r in range(d):
            at = ocol + r * WIDTH_A + c * LANES
            o_ref[:, at:at + LANES] = scr[pl.ds(r, n, stride=d), :].astype(o_ref.dtype)


def _view_to_rows(v_ref, o_ref, col, d, chunks):
    n = VIEW_ROWS // d
    for c, scr in enumerate(chunks):
        if d == 1:
            o_ref[:, col + c * LANES:col + (c + 1) * LANES] = v_ref[:, c * LANES:(c + 1) * LANES].astype(o_ref.dtype)
            continue
        for r in range(d):
            scr[pl.ds(r, n, stride=d), :] = v_ref[:, r * WIDTH_A + c * LANES:r * WIDTH_A + (c + 1) * LANES].astype(F32)
        o_ref[:, col + c * LANES:col + (c + 1) * LANES] = scr[...].astype(o_ref.dtype)


def _group_view(proj, grp, d):
    T = proj.shape[0]
    if d == 1:
        return proj, (lambda part, r: grp * 3 + part)

    def body(x_ref, o_ref, *chunks):
        for part in range(3):
            _rows_to_view(x_ref, part * WIDTH_A, o_ref, part * d * WIDTH_A, d, chunks)

    view = pl.pallas_call(
        body, out_shape=jax.ShapeDtypeStruct((T // d, 3 * d * WIDTH_A), proj.dtype), grid=(T // VIEW_ROWS,),
        in_specs=[_bs((VIEW_ROWS, 3 * WIDTH_A), lambda i: (i, grp))],
        out_specs=_bs((VIEW_ROWS // d, 3 * d * WIDTH_A), lambda i: (i, 0)),
        scratch_shapes=_view_chunks(), compiler_params=_params("parallel"), name=f"a_view_d{d}")(proj)
    return view, (lambda part, r: part * d + r)


def _stack_heads(v2, second):
    zero = jnp.zeros_like(v2)
    return jnp.concatenate([jnp.where(second, zero, v2), jnp.where(second, v2, zero)], axis=0)


def _unstack_heads(v, second):
    return jnp.where(second, v[QB_A:], v[:QB_A])


def _dil_fwd(view, bias, d):
    pv, colblk = view
    L = pv.shape[0]
    nblk = L // QB_A
    W2 = 2 * HEAD_A
    scale = HEAD_A ** -0.5

    def body(q_ref, k_ref, v_ref, b_ref, o_ref, l_ref):
        win = pl.ds(_window_start(pl.program_id(1), nblk), WIN_A)
        second = _lane_is_second_head((QB_A, W2))
        pairs = range(HEADS_A // 2)
        cols = [slice(hp * W2, (hp + 1) * W2) for hp in pairs]
        s = [lax.dot_general(_stack_heads(q_ref[:, cols[hp]], second), k_ref[win, cols[hp]], (NT, ((), ())),
                             preferred_element_type=F32) * scale + b_ref[2 * hp:2 * hp + 2].reshape(2 * QB_A, WIN_A)
             for hp in pairs]
        m = [jnp.max(x, axis=-1, keepdims=True) for x in s]
        p = [jnp.exp(x - mx) for x, mx in zip(s, m)]
        l = [jnp.sum(x, axis=-1, keepdims=True) for x in p]
        res = [jnp.dot(p[hp].astype(BF16), v_ref[win, cols[hp]], preferred_element_type=F32) / l[hp] for hp in pairs]
        o_ref[...] = jnp.concatenate([_unstack_heads(x, second) for x in res], axis=1).astype(o_ref.dtype)
        l_ref[...] = jnp.concatenate([_unstack_heads(jnp.broadcast_to(mx + jnp.log(lx), (2 * QB_A, W2)), second)
                                      for mx, lx in zip(m, l)], axis=1)

    in_specs = [_bs((QB_A, WIDTH_A), lambda r, n: (n, colblk(0, r))),
                _bs((L, WIDTH_A), lambda r, n: (0, colblk(1, r))), _bs((L, WIDTH_A), lambda r, n: (0, colblk(2, r))),
                _bs((None, HEADS_A, QB_A, WIN_A), lambda r, n: (_window_variant(n, nblk), 0, 0, 0))]
    o, lse = pl.pallas_call(
        body, out_shape=[jax.ShapeDtypeStruct((L, d * WIDTH_A), BF16), jax.ShapeDtypeStruct((L, d * WIDTH_A), F32)],
        grid=(d, nblk), in_specs=in_specs,
        out_specs=[_bs((QB_A, WIDTH_A), lambda r, n: (n, r)), _bs((QB_A, WIDTH_A), lambda r, n: (n, r))],
        compiler_params=_params("parallel", "parallel"), name=f"a_fwd_d{d}")(pv, pv, pv, bias)
    return o, lse


def _dil_bwd(view_qkv, bias, do, lse, cterm, d):
    pv, colblk = view_qkv
    L = pv.shape[0]
    nblk = L // QB_A
    W2 = 2 * HEAD_A
    PPS = 4
    WS = PPS * W2
    ob = WIDTH_A // WS
    scale = HEAD_A ** -0.5

    def body(q_ref, k_ref, v_ref, do_ref, l_ref, c_ref, b_ref, dq_ref, dk_ref, dv_ref, db_ref):
        r, n = pl.program_id(1), pl.program_id(2)

        @pl.when(n == 0)
        def _():
            dk_ref[...] = jnp.zeros_like(dk_ref)
            dv_ref[...] = jnp.zeros_like(dv_ref)

        @pl.when((n == 0) & (r == 0))
        def _():
            db_ref[...] = jnp.zeros_like(db_ref)

        second = _lane_is_second_head((QB_A, W2))
        win = pl.ds(_window_start(n, nblk), WIN_A)
        variant = _window_variant(n, nblk)
        pairs = range(PPS)
        cols = [slice(pp * W2, (pp + 1) * W2) for pp in pairs]

        def head_rows(ref, pp):
            v2 = ref[:, cols[pp]]
            return jnp.concatenate([v2[:, 0:1], v2[:, HEAD_A:HEAD_A + 1]], axis=0)

        kw = [k_ref[win, c] for c in cols]
        vw = [v_ref[win, c] for c in cols]
        qs = [_stack_heads(q_ref[:, c], second) for c in cols]
        dos = [_stack_heads(do_ref[:, c], second) for c in cols]
        s = [lax.dot_general(qs[pp], kw[pp], (NT, ((), ())), preferred_element_type=F32) for pp in pairs]
        dp = [lax.dot_general(dos[pp], vw[pp], (NT, ((), ())), preferred_element_type=F32) for pp in pairs]
        p = [jnp.exp(s[pp] * scale + b_ref[2 * pp:2 * pp + 2].reshape(2 * QB_A, WIN_A) - head_rows(l_ref, pp)) for pp in pairs]
        ds = [p[pp] * (dp[pp] + head_rows(c_ref, pp)) for pp in pairs]
        db_ref[variant] += jnp.concatenate([x.reshape(2, QB_A, WIN_A) for x in ds], axis=0)
        pb = [x.astype(BF16) for x in p]
        dsb = [(x * scale).astype(BF16) for x in ds]
        dq_ref[...] = jnp.concatenate([_unstack_heads(jnp.dot(dsb[pp], kw[pp], preferred_element_type=F32), second)
                                       for pp in pairs], axis=1).astype(dq_ref.dtype)
        dk_ref[win, :] += jnp.concatenate([lax.dot_general(dsb[pp], qs[pp], (TN, ((), ())), preferred_element_type=F32)
                                           for pp in pairs], axis=1)
        dv_ref[win, :] += jnp.concatenate([lax.dot_general(pb[pp], dos[pp], (TN, ((), ())), preferred_element_type=F32)
                                           for pp in pairs], axis=1)

    kv_spec = _resident if d == 1 else _bs
    in_specs = [_bs((QB_A, WS), lambda hp, r, n: (n, colblk(0, r) * ob + hp)),
                kv_spec((L, WS), lambda hp, r, n: (0, colblk(1, r) * ob + hp)),
                kv_spec((L, WS), lambda hp, r, n: (0, colblk(2, r) * ob + hp))]
    in_specs += [_bs((QB_A, WS), lambda hp, r, n: (n, r * ob + hp))] * 3
    in_specs += [_bs((None, 2 * PPS, QB_A, WIN_A), lambda hp, r, n: (_window_variant(n, nblk), hp, 0, 0))]
    out_shape = [jax.ShapeDtypeStruct((L, d * WIDTH_A), BF16), jax.ShapeDtypeStruct((L, d * WIDTH_A), F32),
                 jax.ShapeDtypeStruct((L, d * WIDTH_A), F32), jax.ShapeDtypeStruct((3, HEADS_A, QB_A, WIN_A), F32)]
    out_specs = [_bs((QB_A, WS), lambda hp, r, n: (n, r * ob + hp)),
                 _bs((L, WS), lambda hp, r, n: (0, r * ob + hp)), _bs((L, WS), lambda hp, r, n: (0, r * ob + hp)),
                 _bs((3, 2 * PPS, QB_A, WIN_A), lambda hp, r, n: (0, hp, 0, 0))]
    dq, dk, dv, db = pl.pallas_call(
        body, out_shape=out_shape, grid=(ob, d, nblk), in_specs=in_specs, out_specs=out_specs,
        compiler_params=_params("arbitrary", "arbitrary", "arbitrary"), name=f"a_bwd_d{d}")(
            pv, pv, pv, do, lse, cterm, bias)
    return dq, dk, dv, db


def _assemble_dproj(a_parts, dq_b, dk_b, dv_b, dga, dgb):
    T = dq_b.shape[0]
    flat = [(a_parts[part][g], d) for part in range(3) for g, d in enumerate(DILATIONS)]
    rest = [dq_b, dk_b, dv_b, dga, dgb]

    def body(*refs):
        views, others = refs[:len(flat)], refs[len(flat):len(flat) + len(rest)]
        o_ref, chunks = refs[len(flat) + len(rest)], refs[len(flat) + len(rest) + 1:]
        col = 0
        for v_ref, (_, d) in zip(views, flat):
            _view_to_rows(v_ref, o_ref, col, d, chunks)
            col += WIDTH_A
        for x_ref in others:
            w = x_ref.shape[1]
            o_ref[:, col:col + w] = x_ref[...].astype(o_ref.dtype)
            col += w

    in_specs = [_bs((VIEW_ROWS // d, d * WIDTH_A), lambda i: (i, 0)) for _, d in flat]
    in_specs += [_bs((VIEW_ROWS, x.shape[1]), lambda i: (i, 0)) for x in rest]
    return pl.pallas_call(
        body, out_shape=jax.ShapeDtypeStruct((T, IN_WIDTH), BF16), grid=(T // VIEW_ROWS,), in_specs=in_specs,
        out_specs=_bs((VIEW_ROWS, IN_WIDTH), lambda i: (i, 0)), scratch_shapes=_view_chunks(),
        compiler_params=_params("parallel"), name="mix_bwd_dproj")(*[a for a, _ in flat], *rest)


def _segment_ones():
    i = np.arange(WIDTH_A)
    return jnp.asarray((i[:, None] // HEAD_A == i[None, :] // HEAD_A).astype(np.float32), dtype=BF16)


def _group_weights(l0, l1, l2):
    m = jnp.maximum(jnp.maximum(l0, l1), l2)
    e = [jnp.exp(l - m) for l in (l0, l1, l2)]
    z = e[0] + e[1] + e[2]
    return [ei / z for ei in e]


def _view_specs():
    return [_bs((VIEW_ROWS // d, d * WIDTH_A), lambda i: (i, 0)) for d in DILATIONS]


def _stage_tiles(n):
    return [pltpu.VMEM((VIEW_ROWS, WIDTH_A), F32)] * n


def _token_rows(v_ref, stage, d, chunks):
    if d == 1:
        return v_ref[...].astype(F32)
    _view_to_rows(v_ref, stage, 0, d, chunks)
    return stage[...]


def _combine_fwd(outs, lses):
    T = outs[0].shape[0] * DILATIONS[0]
    n = len(DILATIONS)

    def body(*refs):
        o_refs, l_refs, oa_ref = refs[:n], refs[n:2 * n], refs[2 * n]
        o_st, l_st, chunks = refs[2 * n + 1:3 * n + 1], refs[3 * n + 1:4 * n + 1], refs[4 * n + 1:]
        o = [_token_rows(o_refs[g], o_st[g], d, chunks) for g, d in enumerate(DILATIONS)]
        w = _group_weights(*[_token_rows(l_refs[g], l_st[g], d, chunks) for g, d in enumerate(DILATIONS)])
        oa_ref[...] = (w[0] * o[0] + w[1] * o[1] + w[2] * o[2]).astype(oa_ref.dtype)

    return pl.pallas_call(
        body, out_shape=jax.ShapeDtypeStruct((T, WIDTH_A), BF16), grid=(T // VIEW_ROWS,),
        in_specs=_view_specs() * 2, out_specs=_bs((VIEW_ROWS, WIDTH_A), lambda i: (i, 0)),
        scratch_shapes=_stage_tiles(2 * n) + _view_chunks(), compiler_params=_params("parallel"), name="a_combine")(*outs, *lses)


def _combine_bwd(doa, outs, lses):
    T = doa.shape[0]
    n = len(DILATIONS)

    def body(*refs):
        d_ref, o_refs, l_refs, seg_ref = refs[0], refs[1:n + 1], refs[n + 1:2 * n + 1], refs[2 * n + 1]
        do_refs, c_refs = refs[2 * n + 2:3 * n + 2], refs[3 * n + 2:4 * n + 2]
        o_st, l_st = refs[4 * n + 2:5 * n + 2], refs[5 * n + 2:6 * n + 2]
        tmp, chunks = refs[6 * n + 2], refs[6 * n + 3:]
        o = [_token_rows(o_refs[g], o_st[g], d, chunks) for g, d in enumerate(DILATIONS)]
        w = _group_weights(*[_token_rows(l_refs[g], l_st[g], d, chunks) for g, d in enumerate(DILATIONS)])
        dv = d_ref[...].astype(F32)
        seg = seg_ref[...]
        tot = jnp.zeros(dv.shape, F32)
        for g in range(n):
            prod = w[g] * dv * o[g]
            hi = prod.astype(BF16)
            lo = (prod - hi.astype(F32)).astype(BF16)
            tot = tot + jnp.dot(hi, seg, preferred_element_type=F32) + jnp.dot(lo, seg, preferred_element_type=F32)
        for g, d in enumerate(DILATIONS):
            for ref, val in ((do_refs[g], w[g] * dv), (c_refs[g], -w[g] * tot)):
                if d == 1:
                    ref[...] = val.astype(ref.dtype)
                else:
                    tmp[...] = val
                    _rows_to_view(tmp, 0, ref, 0, d, chunks)

    views = [jax.ShapeDtypeStruct((T // d, d * WIDTH_A), dt) for dt in (BF16, F32) for d in DILATIONS]
    res = pl.pallas_call(
        body, out_shape=views, grid=(T // VIEW_ROWS,),
        in_specs=[_bs((VIEW_ROWS, WIDTH_A), lambda i: (i, 0))] + _view_specs() * 2 + [_bs((WIDTH_A, WIDTH_A), lambda i: (0, 0))],
        out_specs=_view_specs() * 2, scratch_shapes=_stage_tiles(2 * n + 1) + _view_chunks(),
        compiler_params=_params("parallel"), name="a_combine_bwd")(doa, *outs, *lses, _segment_ones())
    return res[:n], res[n:]


def _rope_tables(T):
    rows = T // GRID_W
    row = jnp.repeat(jnp.arange(rows, dtype=F32), GRID_W)
    col = jnp.tile(jnp.arange(GRID_W, dtype=F32), rows)
    n_freq = HEAD_B // 4
    freq = ROPE_THETA ** (-jnp.arange(n_freq, dtype=F32) / n_freq)
    ang = jnp.concatenate([row[:, None] * freq, col[:, None] * freq], axis=-1)
    cos, sin = jnp.repeat(jnp.cos(ang), 2, axis=1), jnp.repeat(jnp.sin(ang), 2, axis=1)
    sign = jnp.where(jnp.arange(HEAD_B) % 2 == 0, -1.0, 1.0).astype(F32)
    return cos, sin * sign


def _swap_pairs(v):
    even = lax.broadcasted_iota(jnp.int32, v.shape, v.ndim - 1) % 2 == 0
    n = v.shape[-1]
    return jnp.where(even, pltpu.roll(v, n - 1, v.ndim - 1), pltpu.roll(v, 1, v.ndim - 1))


def _qk_fwd(name, proj, col0, n_heads, gain, cos, sin, out_scale=1.0, deps=()):
    T = proj.shape[0]

    def fn(xr, g, c, s):
        xn = _norm_fwd(xr.astype(F32), g)
        return (xn * c + _swap_pairs(xn) * s) * out_scale

    (out,) = _ew(name, fn, [_tiled(proj, HEAD_B, col0 // HEAD_B), _whole(gain), _table(cos), _table(sin)],
                 [(BF16, HEAD_B)], n_rows=T, rows=2048, ncols=n_heads, deps=deps)
    return out


def _qk_bwd(name, dout, proj, col0, n_heads, gain, cos, sin, in_scale=1.0):
    T = proj.shape[0]

    def fn(dv, xr, g, c, s):
        dv = dv.astype(F32) * in_scale
        dxn = c * dv + _swap_pairs(s * dv)
        dx, dgr = _norm_bwd(xr.astype(F32), g, dxn)
        return dx, _colsum(dgr)

    dx, dg = _ew(name, fn, [_tiled(dout, HEAD_B, 0), _tiled(proj, HEAD_B, col0 // HEAD_B), _whole(gain),
                            _table(cos), _table(sin)],
                 [(BF16, HEAD_B)], n_rows=T, rows=2048, reds=(HEAD_B,), ncols=n_heads)
    return dx, jnp.sum(dg, axis=0)


def _gqa_fwd(qn, kn, proj, k_col=0):
    T = qn.shape[0]
    GW = 4 * HEAD_B
    QB = QB_B

    def body(q_ref, k_ref, v_ref, o_ref, l_ref):
        k = k_ref[...]
        v_ones = jnp.concatenate([v_ref[...], jnp.ones((T, HEAD_B), BF16)], axis=1)
        lane = lax.broadcasted_iota(jnp.int32, (QB, HEAD_B), 1)
        heads = range(4)
        s = [lax.dot_general(q_ref[:, g * HEAD_B:(g + 1) * HEAD_B], k, (NT, ((), ())), preferred_element_type=F32)
             for g in heads]
        m = [jnp.max(x, axis=-1, keepdims=True) for x in s]
        pv = [jnp.dot(jnp.exp2(x - mx).astype(BF16), v_ones, preferred_element_type=F32) for x, mx in zip(s, m)]
        l = [x[:, HEAD_B:HEAD_B + 1] for x in pv]
        o = [x[:, :HEAD_B] / lx for x, lx in zip(pv, l)]
        o_ref[...] = jnp.concatenate(o, axis=1).astype(o_ref.dtype)
        lse_all = jnp.zeros((QB, HEAD_B), F32)
        for g in heads:
            lse_all = jnp.where(lane == g, m[g] + jnp.log2(l[g]), lse_all)
        l_ref[...] = lse_all

    return pl.pallas_call(
        body, out_shape=[jax.ShapeDtypeStruct((T, 2 * GW), BF16), jax.ShapeDtypeStruct((2, T, HEAD_B), F32)],
        grid=(2, T // QB),
        in_specs=[_bs((QB, GW), lambda kv, i: (i, kv)), _bs((T, HEAD_B), lambda kv, i: (0, k_col + kv)),
                  _bs((T, HEAD_B), lambda kv, i: (0, B_V // HEAD_B + kv))],
        out_specs=[_bs((QB, GW), lambda kv, i: (i, kv)), _bs((None, QB, HEAD_B), lambda kv, i: (kv, i, 0))],
        compiler_params=_params("parallel", "parallel"), name="b_fwd")(qn, kn, proj)


def _gqa_bwd(qn, kn, proj, o, lse, do, deps=(), k_col=0):
    T = qn.shape[0]
    GW = 4 * HEAD_B

    def body(q_ref, k_ref, v_ref, o_ref, l_ref, do_ref, *rest):
        dq_ref, dk_ref, dv_ref = rest[-3:]
        i = pl.program_id(1)

        @pl.when(i == 0)
        def _():
            dk_ref[...] = jnp.zeros_like(dk_ref)
            dv_ref[...] = jnp.zeros_like(dv_ref)

        k, v = k_ref[...], v_ref[...]
        lse_all = l_ref[...]
        for g in range(4):
            cols = slice(g * HEAD_B, (g + 1) * HEAD_B)
            q, dob = q_ref[:, cols], do_ref[:, cols]
            delta = jnp.sum(dob.astype(F32) * o_ref[:, cols].astype(F32), axis=-1, keepdims=True)
            s = lax.dot_general(q, k, (NT, ((), ())), preferred_element_type=F32)
            p = jnp.exp2(s - lse_all[:, g:g + 1])
            dp = lax.dot_general(dob, v, (NT, ((), ())), preferred_element_type=F32)
            ds = (p * (dp - delta)).astype(BF16)
            dq_ref[:, cols] = jnp.dot(ds, k, preferred_element_type=F32).astype(dq_ref.dtype)
            dk_ref[...] += lax.dot_general(ds, q, (TN, ((), ())), preferred_element_type=F32)
            dv_ref[...] += lax.dot_general(p.astype(BF16), dob, (TN, ((), ())), preferred_element_type=F32)

    return pl.pallas_call(
        body, out_shape=[jax.ShapeDtypeStruct((T, 2 * GW), BF16), jax.ShapeDtypeStruct((T, 2 * HEAD_B), F32),
                         jax.ShapeDtypeStruct((T, 2 * HEAD_B), F32)],
        grid=(2, T // QB_B),
        in_specs=[_bs((QB_B, GW), lambda kv, i: (i, kv)), _bs((T, HEAD_B), lambda kv, i: (0, k_col + kv)),
                  _bs((T, HEAD_B), lambda kv, i: (0, B_V // HEAD_B + kv)), _bs((QB_B, GW), lambda kv, i: (i, kv)),
                  _bs((None, QB_B, HEAD_B), lambda kv, i: (kv, i, 0)), _bs((QB_B, GW), lambda kv, i: (i, kv))] + _any_specs(len(deps)),
        out_specs=[_bs((QB_B, GW), lambda kv, i: (i, kv)), _bs((T, HEAD_B), lambda kv, i: (0, kv)),
                   _bs((T, HEAD_B), lambda kv, i: (0, kv))],
        compiler_params=_params("parallel", "arbitrary"), name="b_bwd")(qn, kn, proj, o, lse, do, *deps)


def _local_step(x, target, small, get_w, put_g, deps=(), prefetch_w=lambda name, after: [], take_rider=lambda steps, after: None):
    T, D = x.shape
    gs = {}

    bias = _bias_tiles(small["rel_bias"])
    cos, sin = _rope_tables(T)
    (x1, h2), ffn1_saved = _ffn_fwd("ffn1", x, small["ffn1_norm"], lambda name, after: get_w(name, [after, bias, cos, sin]), deps,
                                    tail_ins=[small["mix_norm"]], tail_fn=lambda y, g: (y, _norm_fwd(y, g)), tail_outs=(F32, BF16))
    w_in = get_w("w_in", h2)
    nq = w_in.shape[2]
    tpq = nq // WIDTH_A

    def proj_tile(j, k):
        c = j * tpq + k
        return jnp.where(c < 3 * len(DILATIONS), (c % 3) * 3 + c // 3, c)

    proj = _mm("mix_in", (4, tpq),
               [(h2, _resident((T, D), lambda j, k: (0, 0)), w_in, _bs((None, D, WIDTH_A), lambda j, k: (j, 0, k)))],
               jax.ShapeDtypeStruct((T, IN_WIDTH), BF16), _bs((T, WIDTH_A), lambda j, k: (0, proj_tile(j, k))), NN)

    a_views = [_group_view(proj, grp, d) for grp, d in enumerate(DILATIONS)]
    a_outs, a_lses = [], []
    for grp, d in enumerate(DILATIONS):
        o, l = _dil_fwd(a_views[grp], bias[grp], d)
        a_outs.append(o)
        a_lses.append(l)
    o_a = _combine_fwd(a_outs, a_lses)

    qk_gain = jnp.concatenate([jnp.tile(small["q_norm"] * QK_SCALE_LOG2, (8, 1)), jnp.tile(small["k_norm"], (2, 1))])[:, None, :]
    qkn = _qk_fwd("b_qknorm", proj, B_Q, 10, qk_gain, cos, sin, deps=prefetch_w("w_branch_a", proj))
    qn, kn, k_col = qkn, qkn, 8
    o_b, lse_b = _gqa_fwd(qn, kn, proj, k_col)
    ahead = prefetch_w("ffn2_w1", o_b)

    wa, wb, wo = get_w("w_branch_a", o_b), get_w("w_branch_b", o_b), get_w("w_out", o_b)
    bg_a, bg_b = small["b_gate"][:, :D], small["b_gate"][:, D:]
    n_a = wa.shape[0]

    def merge_out(oa_ref, ob_ref, ga_ref, gb_ref, x1_ref, wa_ref, wb_ref, wo_ref, ba_ref, bb_ref, g2_ref, *rest):
        ta_ref, tb_ref, mg_ref, x2_ref, hn_ref = rest[-5:]
        oa = oa_ref[...]
        ta = jnp.concatenate([jnp.dot(oa, wa_ref[j], preferred_element_type=F32) for j in range(n_a)], axis=1)
        tb = jnp.dot(ob_ref[...], wb_ref[...], preferred_element_type=F32)
        sa = _sigmoid(ga_ref[...].astype(F32) + ba_ref[...])
        sb = _sigmoid(gb_ref[...].astype(F32) + bb_ref[...])
        merged = (sa * ta + sb * tb).astype(BF16)
        ta_ref[...], tb_ref[...], mg_ref[...] = ta.astype(BF16), tb.astype(BF16), merged
        y = x1_ref[...] + jnp.dot(merged, wo_ref[...], preferred_element_type=F32)
        x2_ref[...] = y
        hn_ref[...] = _norm_fwd(y, g2_ref[...]).astype(BF16)

    row = _bs((512, D), lambda i: (i, 0))
    gate_specs = [_bs((512, D), lambda i: (i, G_A // D)), _bs((512, D), lambda i: (i, G_B // D))]
    whole2, whole3 = (lambda i: (0, 0)), (lambda i: (0, 0, 0))
    vec = _bs((1, D), whole2)
    t_a, t_b, merged, x2, hn2 = pl.pallas_call(
        merge_out, out_shape=[jax.ShapeDtypeStruct((T, D), BF16)] * 3 + [jax.ShapeDtypeStruct((T, D), F32), jax.ShapeDtypeStruct((T, D), BF16)],
        grid=(T // 512,),
        in_specs=[_bs((512, WIDTH_A), lambda i: (i, 0)), row] + gate_specs + [row, _resident(wa.shape, whole3), _resident((D, D), whole2),
                                                                                _resident((D, D), whole2), vec, vec, vec]
        + _any_specs(len(ahead)),
        out_specs=[row] * 5, compiler_params=_params("parallel"), name="mix_merge_out")(
            o_a, o_b, proj, proj, x1, wa, wb, wo, bg_a, bg_b, small["ffn2_norm"], *ahead)

    def head(xv, g, tv):
        r = _rstd(xv)
        xh = xv * r
        e = xh * g - tv
        dy = e * (1.0 / D)
        dxh = dy * g
        dx = r * (dxh - xh * jnp.mean(dxh * xh, axis=-1, keepdims=True))
        return dx, 0.5 * dx, _colsum(e * e) * (0.5 / D), _colsum(dy * xh)

    (dx3, dx3_half, loss_cols, g_final), ffn2_saved = _ffn_fwd(
        "ffn2", x2, small["ffn2_norm"], get_w, h=hn2, tail_ins=[small["final_norm"].reshape(1, D), target], tail_fn=head,
        tail_outs=(F32, BF16), tail_reds=(D, D))
    gs["final_norm"] = g_final.reshape(D)

    dx2, _, dmix, gs["ffn2_norm"] = _ffn_bwd("ffn2", x2, small["ffn2_norm"], get_w, put_g, ffn2_saved, dx3, dx3_half,
                                             also_bf16=True)
    g_out = _mm_wgrad("mix_bwd_dwout", merged, dmix, a_cols=D // 4, b_cols=None, tm=256, tn=512, J=4).reshape(D, D)

    def merge_out_bwd(dx_ref, ta_ref, tb_ref, ga_ref, gb_ref, wa_ref, wb_ref, wo_ref, ba_ref, bb_ref,
                      dta_ref, dtb_ref, dga_ref, dgb_ref, doa_ref, dob_ref, dba_ref, dbb_ref):
        dm = lax.dot_general(dx_ref[...], wo_ref[...], (NT, ((), ())), preferred_element_type=F32)
        ta, tb = ta_ref[...].astype(F32), tb_ref[...].astype(F32)
        sa = _sigmoid(ga_ref[...].astype(F32) + ba_ref[...])
        sb = _sigmoid(gb_ref[...].astype(F32) + bb_ref[...])
        dga, dgb = dm * ta * sa * (1.0 - sa), dm * tb * sb * (1.0 - sb)
        dta, dtb = (dm * sa).astype(BF16), (dm * sb).astype(BF16)
        dta_ref[...], dtb_ref[...] = dta, dtb
        dga_ref[...], dgb_ref[...] = dga.astype(BF16), dgb.astype(BF16)
        w = wa_ref.shape[2]
        doa = sum(lax.dot_general(dta[:, j * w:(j + 1) * w], wa_ref[j], (NT, ((), ())), preferred_element_type=F32) for j in range(n_a))
        doa_ref[...] = doa.astype(BF16)
        dob_ref[...] = lax.dot_general(dtb, wb_ref[...], (NT, ((), ())), preferred_element_type=F32).astype(BF16)

        @pl.when(pl.program_id(0) == 0)
        def _():
            dba_ref[...] = jnp.zeros_like(dba_ref)
            dbb_ref[...] = jnp.zeros_like(dbb_ref)
        dba_ref[...] += _colsum(dga)
        dbb_ref[...] += _colsum(dgb)

    rowb = _bs((256, D), lambda i: (i, 0))
    gate_specs = [_bs((256, D), lambda i: (i, G_A // D)), _bs((256, D), lambda i: (i, G_B // D))]
    dta, dtb, dga, dgb, do_a, do_b, dba, dbb = pl.pallas_call(
        merge_out_bwd,
        out_shape=[jax.ShapeDtypeStruct((T, D), BF16)] * 4 + [jax.ShapeDtypeStruct((T, WIDTH_A), BF16), jax.ShapeDtypeStruct((T, D), BF16)]
        + [jax.ShapeDtypeStruct((1, D), F32)] * 2,
        grid=(T // 256,),
        in_specs=[rowb, rowb, rowb] + gate_specs + [_resident(wa.shape, whole3), _resident((D, D), whole2), _resident((D, D), whole2), vec, vec],
        out_specs=[rowb] * 4 + [_bs((256, WIDTH_A), lambda i: (i, 0)), rowb, vec, vec],
        compiler_params=_params("arbitrary"), name="mix_merge_out_bwd")(dmix, t_a, t_b, proj, proj, wa, wb, wo, bg_a, bg_b)
    gs["b_gate"] = jnp.concatenate([dba, dbb], axis=1)

    g_a = _mm_wgrad("mix_bwd_dwa", o_a, dta, a_cols=None, b_cols=D // 4, tm=WIDTH_A, tn=256, J=4)
    g_b = _mm_wgrad("mix_bwd_dwb", o_b, dtb, a_cols=D // 4, b_cols=None, tm=256, tn=512, J=4).reshape(D, D)
    deps = put_g({"w_out": g_out, "w_branch_a": g_a, "w_branch_b": g_b})

    dqn, dkn, dv_b = _gqa_bwd(qn, kn, proj, o_b, lse_b, do_b, deps, k_col)
    dq_b, gs["q_norm"] = _qk_bwd("b_bwd_qnorm", dqn, proj, B_Q, 8, small["q_norm"], cos, sin, in_scale=HEAD_B ** -0.5)
    dk_b, gs["k_norm"] = _qk_bwd("b_bwd_knorm", dkn, proj, B_K, 2, small["k_norm"], cos, sin, in_scale=1.0 / LOG2_E)

    do_groups, c_groups = _combine_bwd(do_a, a_outs, a_lses)
    dqs, dks, dvs, dbs = [], [], [], []
    for grp, d in enumerate(DILATIONS):
        dq, dk, dv, db = _dil_bwd(a_views[grp], bias[grp], do_groups[grp], a_lses[grp], c_groups[grp], d)
        dqs.append(dq), dks.append(dk), dvs.append(dv), dbs.append(db)
    gs["rel_bias"] = _bias_grad(jnp.stack(dbs))

    dproj = _assemble_dproj([dqs, dks, dvs], dq_b, dk_b, dv_b, dga, dgb)
    nq = w_in.shape[2]
    g_in = _mm("mix_bwd_dwin", (4, tpq),
               [(h2, _resident((T, D), lambda j, k: (0, 0)), dproj, _bs((T, WIDTH_A), lambda j, k: (0, j * tpq + k)))],
               jax.ShapeDtypeStruct((4, D, nq), BF16), _bs((None, D, WIDTH_A), lambda j, k: (j, 0, k)), TN)
    deps = put_g({"w_in": g_in})
    dx1, dx1_half, gs["mix_norm"] = _dh_norm_bwd(
        "mix_bwd_dh", 256,
        [(dproj, _bs((256, nq), lambda i, j=j: (i, j)), w_in, _resident((None, D, nq), lambda i, j=j: (j, 0, 0))) for j in range(4)],
        NT, x1, small["mix_norm"], dx2, deps, rider=take_rider(T // 256, deps))

    dx0, _, gs["ffn1_norm"] = _ffn_bwd("ffn1", x, small["ffn1_norm"], get_w, put_g, ffn1_saved, dx1, dx1_half, last=True,
                                       take_rider=take_rider)
    return loss_cols, dx0, gs


def _position():
    return lax.axis_index("x"), lax.axis_index("y"), lax.axis_index("c")


def _any_specs(n):
    return [pl.BlockSpec(memory_space=pl.ANY)] * n


HBM_SPEC = pl.BlockSpec(memory_space=pltpu.HBM)
SEM_SPEC = pl.BlockSpec(memory_space=pltpu.SEMAPHORE)
DATAFLOW_EFFECT = pltpu.SideEffectType.DATAFLOW_SIDE_EFFECTING
N_PEER_CHIPS = 3
LANES = 128


def _quarter_copies(srcs, lands, send_sems, recv_sems, mode):
    x, y, c = _position()
    me = 2 * x + y
    peers = [(1 - x, y, c), (x, 1 - y, c), (1 - x, 1 - y, c)]
    copies = []
    for src, land, send, recv in zip(srcs, lands, send_sems, recv_sems):
        if mode == "sibling":
            copies.append(pltpu.make_async_remote_copy(src_ref=src, dst_ref=land, send_sem=send.at[0], recv_sem=recv.at[0],
                                                       device_id=(x, y, 1 - c), device_id_type=MESH))
            continue
        if mode == "fill":
            half = land.shape[1] // 2
            for p, (px, py, _) in enumerate(peers):
                part = land.at[2 * px + py, pl.ds(c * half, half)]
                copies.append(pltpu.make_async_remote_copy(src_ref=part, dst_ref=part, send_sem=send.at[p], recv_sem=recv.at[p],
                                                           device_id=(x, y, 1 - c), device_id_type=MESH))
            continue
        scatter = mode == "scatter"
        half = land.shape[1] // 2
        mine = land.at[me, pl.ds(c * half, half)]
        for p, (px, py, pc) in enumerate(peers):
            copies.append(pltpu.make_async_remote_copy(
                src_ref=src.at[2 * px + py] if scatter else mine, dst_ref=land.at[me] if scatter else mine,
                send_sem=send.at[p], recv_sem=recv.at[p], device_id=(px, py, pc), device_id_type=MESH))
    return copies


def _fill_from_sibling(name, stacks):
    n = len(stacks)

    def body(*refs):
        outs = refs[n:2 * n]
        send_sems, recv_sems = refs[2 * n:]
        x, y, c = _position()
        copies = []
        for i, ref in enumerate(outs):
            half = ref.shape[1] // 2
            rows = pl.ds(c * half, half)
            for p, k in enumerate((2 * (1 - x) + y, 2 * x + (1 - y), 2 * (1 - x) + (1 - y))):
                cp = pltpu.make_async_remote_copy(ref.at[k, rows], ref.at[k, rows], send_sems.at[3 * i + p], recv_sems.at[3 * i + p],
                                                  device_id=(x, y, 1 - c), device_id_type=MESH)
                cp.start()
                copies.append(cp)
        for cp in copies:
            cp.wait()

    return pl.pallas_call(
        body, out_shape=[jax.ShapeDtypeStruct(s.shape, s.dtype) for s in stacks],
        in_specs=_any_specs(n), out_specs=_any_specs(n), input_output_aliases={i: i for i in range(n)},
        scratch_shapes=[pltpu.SemaphoreType.DMA((N_PEER_CHIPS * n,)), pltpu.SemaphoreType.DMA((N_PEER_CHIPS * n,))],
        compiler_params=pltpu.CompilerParams(has_side_effects=True), name=name)(*stacks)


def _exchange_start(name, srcs, lands, mode):
    n = len(lands)
    arrays = list(lands) if srcs is None else list(srcs) + list(lands)
    k = len(arrays)

    def body(*refs):
        land_refs = refs[k - n:k]
        send_sems, recv_sems = refs[k:k + n], refs[k + n:k + 2 * n]
        token = refs[2 * k + 2 * n]
        for cp in _quarter_copies(refs[:n], land_refs, send_sems, recv_sems, mode):
            cp.start()
        token[...] = jnp.zeros_like(token)

    sem = pltpu.SemaphoreType.DMA((N_PEER_CHIPS,))
    out_shape = [sem] * (2 * n) + [pltpu.HBM(a.shape, a.dtype) for a in arrays] + [jax.ShapeDtypeStruct((8, LANES), F32)]
    res = pl.pallas_call(
        body, name=name, out_shape=out_shape, in_specs=[HBM_SPEC] * k,
        out_specs=[SEM_SPEC] * (2 * n) + [HBM_SPEC] * k + [pl.BlockSpec(memory_space=pltpu.VMEM)],
        input_output_aliases={i: 2 * n + i for i in range(k)},
        compiler_params=pltpu.CompilerParams(has_side_effects=DATAFLOW_EFFECT),
    )(*[pltpu.with_memory_space_constraint(a, pltpu.HBM) for a in arrays])
    thru = res[2 * n:2 * n + k]
    return res[:n], res[n:2 * n], (None if srcs is None else thru[:n]), thru[k - n:], res[2 * n + k]


def _exchange_wait(name, srcs, lands, send_sems, recv_sems, after, mode):
    n = len(lands)
    arrays = list(lands) if srcs is None else list(srcs) + list(lands)
    k = len(arrays)
    after = list(after) if isinstance(after, (list, tuple)) else [after]

    def body(*refs):
        sends, recvs = refs[k:k + n], refs[k + n:k + 2 * n]
        for cp in _quarter_copies(refs[:n], refs[k - n:k], sends, recvs, mode):
            cp.wait_send()
            cp.wait_recv()

    res = pl.pallas_call(
        body, name=name, out_shape=[pltpu.HBM(a.shape, a.dtype) for a in arrays],
        in_specs=[HBM_SPEC] * k + [SEM_SPEC] * (2 * n) + _any_specs(len(after)),
        out_specs=[HBM_SPEC] * k, input_output_aliases={i: i for i in range(k)},
        compiler_params=pltpu.CompilerParams(has_side_effects=DATAFLOW_EFFECT),
    )(*arrays, *send_sems, *recv_sems, *after)
    return (None if srcs is None else res[:n]), res[k - n:]


def _scatter_and_forward(name, stacks, lands, old_srcs, old_lands, old_sends, old_recvs):
    n1, n0 = len(stacks), len(old_lands)
    sibling_lands = [lax.empty(a.shape, a.dtype) for a in old_lands]
    arrays = list(stacks) + list(lands) + list(old_srcs) + list(old_lands) + sibling_lands
    k, s = len(arrays), 2 * n1 + 2 * n0

    def body(*refs):
        new_srcs, new_lands = refs[:n1], refs[n1:2 * n1]
        was_srcs, landed, to_sibling = refs[2 * n1:2 * n1 + n0], refs[2 * n1 + n0:2 * n1 + 2 * n0], refs[2 * n1 + 2 * n0:k]
        was_sends, was_recvs = refs[k:k + n0], refs[k + n0:k + 2 * n0]
        sems = refs[k + 2 * n0:k + 2 * n0 + s]
        token = refs[k + 2 * n0 + s + k]
        for cp in _quarter_copies(new_srcs, new_lands, sems[:n1], sems[n1:2 * n1], "scatter"):
            cp.start()
        for cp in _quarter_copies(was_srcs, landed, was_sends, was_recvs, "scatter"):
            cp.wait_send()
            cp.wait_recv()
        for cp in _quarter_copies(landed, to_sibling, sems[2 * n1:2 * n1 + n0], sems[2 * n1 + n0:], "sibling"):
            cp.start()
        token[...] = jnp.zeros_like(token)

    sem = pltpu.SemaphoreType.DMA((N_PEER_CHIPS,))
    res = pl.pallas_call(
        body, name=name,
        out_shape=[sem] * s + [pltpu.HBM(a.shape, a.dtype) for a in arrays] + [jax.ShapeDtypeStruct((8, LANES), F32)],
        in_specs=[HBM_SPEC] * k + [SEM_SPEC] * (2 * n0),
        out_specs=[SEM_SPEC] * s + [HBM_SPEC] * k + [pl.BlockSpec(memory_space=pltpu.VMEM)],
        input_output_aliases={i: s + i for i in range(k)},
        compiler_params=pltpu.CompilerParams(has_side_effects=DATAFLOW_EFFECT),
    )(*[pltpu.with_memory_space_constraint(a, pltpu.HBM) for a in arrays], *old_sends, *old_recvs)
    thru = res[s:s + k]
    scatter = (res[:n1], res[n1:2 * n1], thru[:n1], thru[n1:2 * n1])
    sibling = (res[2 * n1:2 * n1 + n0], res[2 * n1 + n0:s], thru[2 * n1 + n0:2 * n1 + 2 * n0], thru[2 * n1 + 2 * n0:])
    return scatter, sibling, res[s + k]


def _own_slots(name, srcs, from_stack=False):
    n = len(srcs)
    me = (2 * lax.axis_index("x") + lax.axis_index("y")).astype(jnp.int32).reshape(1)

    def body(me_ref, *refs):
        for x_ref, o_ref in zip(refs[:n], refs[n:]):
            o_ref[...] = x_ref[...].astype(o_ref.dtype)

    in_specs, out_specs, out_shape = [], [], []
    for src in srcs:
        R, C = src.shape[-2:]
        in_specs.append(pl.BlockSpec((None, R // 2, C), lambda i, me_ref: (me_ref[0], i, 0)) if from_stack
                        else pl.BlockSpec((R // 2, C), lambda i, me_ref: (i, 0)))
        out_specs.append(pl.BlockSpec((None, R // 2, C), lambda i, me_ref: (me_ref[0], i, 0)))
        out_shape.append(jax.ShapeDtypeStruct((4, R, C), BF16))
    grid_spec = pltpu.PrefetchScalarGridSpec(num_scalar_prefetch=1, grid=(2,), in_specs=in_specs, out_specs=out_specs)
    return pl.pallas_call(body, out_shape=out_shape, grid_spec=grid_spec, compiler_params=_params("parallel"), name=name)(me, *srcs)


def _allreduce_small(buf):
    R, C = buf.shape
    flips = [(fx, fy, fc) for fx in (0, 1) for fy in (0, 1) for fc in (0, 1)][1:]

    def body(in_ref, out_ref, land_ref, send_sems, recv_sems):
        x, y, c = _position()
        me = 4 * x + 2 * y + c
        copies = []
        for k, (fx, fy, fc) in enumerate(flips):
            px, py, pc = (1 - x if fx else x), (1 - y if fy else y), (1 - c if fc else c)
            cp = pltpu.make_async_remote_copy(in_ref, land_ref.at[me], send_sems.at[k], recv_sems.at[k],
                                              device_id=(px, py, pc), device_id_type=MESH)
            cp.start()
            copies.append(cp)
        land_ref[me] = in_ref[...]
        for cp in copies:
            cp.wait()
        acc = land_ref[0]
        for k in range(1, 8):
            acc = acc + land_ref[k]
        out_ref[...] = acc

    return pl.pallas_call(
        body, out_shape=jax.ShapeDtypeStruct((R, C), F32),
        in_specs=[pl.BlockSpec(memory_space=pltpu.VMEM)], out_specs=pl.BlockSpec(memory_space=pltpu.VMEM),
        scratch_shapes=[pltpu.VMEM((8, R, C), F32), pltpu.SemaphoreType.DMA((7,)), pltpu.SemaphoreType.DMA((7,))],
        compiler_params=pltpu.CompilerParams(has_side_effects=True), name="allreduce_small")(buf)


def _small_copies(src, land, send_sems, recv_sems):
    x, y, c = _position()
    me = 4 * x + 2 * y + c
    flips = [(fx, fy, fc) for fx in (0, 1) for fy in (0, 1) for fc in (0, 1)][1:]
    return [pltpu.make_async_remote_copy(src, land.at[me], send_sems.at[k], recv_sems.at[k], device_id_type=MESH,
                                         device_id=((1 - x if fx else x), (1 - y if fy else y), (1 - c if fc else c)))
            for k, (fx, fy, fc) in enumerate(flips)]


def _allreduce_small_start(buf):
    R, C = buf.shape
    me = 4 * lax.axis_index("x") + 2 * lax.axis_index("y") + lax.axis_index("c")
    land = lax.dynamic_update_slice(jnp.zeros((8, R, C), F32), buf[None], (me, 0, 0))

    def body(src, land_ref, send_sems, recv_sems, src_out, land_out, token):
        for cp in _small_copies(src, land_ref, send_sems, recv_sems):
            cp.start()
        token[...] = jnp.zeros_like(token)

    sem = pltpu.SemaphoreType.DMA((7,))
    return pl.pallas_call(
        body, name="allreduce_small_start",
        out_shape=[sem, sem, pltpu.HBM(buf.shape, F32), pltpu.HBM(land.shape, F32), jax.ShapeDtypeStruct((8, LANES), F32)],
        in_specs=[HBM_SPEC] * 2, out_specs=[SEM_SPEC] * 2 + [HBM_SPEC] * 2 + [pl.BlockSpec(memory_space=pltpu.VMEM)],
        input_output_aliases={0: 2, 1: 3}, compiler_params=pltpu.CompilerParams(has_side_effects=DATAFLOW_EFFECT),
    )(pltpu.with_memory_space_constraint(buf, pltpu.HBM), pltpu.with_memory_space_constraint(land, pltpu.HBM))


def _allreduce_small_finish(send_sems, recv_sems, buf, land, after):
    def wait(src, land_ref, sends, recvs, *rest):
        for cp in _small_copies(src, land_ref, sends, recvs):
            cp.wait_send()
            cp.wait_recv()

    _, landed = pl.pallas_call(
        wait, name="allreduce_small_wait", out_shape=[pltpu.HBM(buf.shape, F32), pltpu.HBM(land.shape, F32)],
        in_specs=[HBM_SPEC] * 2 + [SEM_SPEC] * 2 + _any_specs(len(after)), out_specs=[HBM_SPEC] * 2,
        input_output_aliases={0: 0, 1: 1}, compiler_params=pltpu.CompilerParams(has_side_effects=DATAFLOW_EFFECT),
    )(buf, land, send_sems, recv_sems, *after)

    def add(land_ref, out_ref):
        acc = land_ref[0]
        for k in range(1, 8):
            acc = acc + land_ref[k]
        out_ref[...] = acc

    return pl.pallas_call(add, name="allreduce_small_sum", out_shape=jax.ShapeDtypeStruct(buf.shape, F32),
                          in_specs=[pl.BlockSpec(memory_space=pltpu.VMEM)], out_specs=pl.BlockSpec(memory_space=pltpu.VMEM))(landed)


def _adamw_math(w, g, m, v):
    m2 = ADAM_B1 * m + (1.0 - ADAM_B1) * g
    v2 = ADAM_B2 * v + (1.0 - ADAM_B2) * (g * g)
    m_hat = m2 / (1.0 - ADAM_B1 ** ADAM_STEP)
    v_hat = v2 / (1.0 - ADAM_B2 ** ADAM_STEP)
    delta = -ADAM_LR * (m_hat / (jnp.sqrt(v_hat) + ADAM_EPS) + ADAM_WD * w)
    return delta, m2, v2


def _adamw_from_partials(wv, mv, vv, *parts):
    def four(a, b, c, d):
        return ((a.astype(F32) + b.astype(F32)) + c.astype(F32)) + d.astype(F32)

    g = four(*parts[:4]) + four(*parts[4:])
    return (g,) + _adamw_math(wv, g, mv, vv)


def _adamw_big(name, w, m, v, mine, theirs):
    R, C = w.shape
    rows = 256 if R % 256 == 0 else R // 2
    nrb = R // rows
    slots = [_tiled(s.reshape(4 * R, C), None, 0, k * nrb) for s in (mine, theirs) for k in range(4)]
    return _ew(name, _adamw_from_partials, [_tiled(w), _tiled(m), _tiled(v)] + slots, [(F32, C)] * 4, n_rows=R, rows=rows)


def _adamw_rider(w, m, v, mine, theirs, steps, deliver):
    R, C = w.shape
    fits = [nb for nb in range(1, steps + 1) if R % nb == 0 and (R // nb) % 16 == 0]
    if not fits:
        return None
    nb = fits[-1]
    rows = R // nb

    def blocks(first):
        return pl.BlockSpec((rows, C), lambda *g: (first + jnp.minimum(g[0], nb - 1), 0))

    flat = [s.reshape(4 * R, C) for s in (mine, theirs)]
    return dict(operands=[w, m, v] + [f for f in flat for _ in range(4)],
                in_specs=[blocks(0)] * 3 + [blocks(k * nb) for _ in flat for k in range(4)],
                out_shape=[jax.ShapeDtypeStruct((R, C), F32)] * 4, out_specs=[blocks(0)] * 4,
                n_blocks=nb, fn=_adamw_from_partials, deliver=lambda outs: deliver(*outs))


BIG = ("ffn1_w1", "ffn1_w3", "ffn1_w2", "w_in", "w_branch_a", "w_branch_b", "w_out", "ffn2_w1", "ffn2_w3", "ffn2_w2")
SMALL = ("ffn1_norm", "mix_norm", "b_gate", "q_norm", "k_norm", "rel_bias", "ffn2_norm", "final_norm")
ORDER = ("ffn1_norm", "ffn1_w1", "ffn1_w3", "ffn1_w2", "mix_norm", "w_in", "b_gate", "q_norm", "k_norm", "rel_bias",
         "w_branch_a", "w_branch_b", "w_out", "ffn2_norm", "ffn2_w1", "ffn2_w3", "ffn2_w2", "final_norm")
TRANSPOSED = ("ffn1_w1", "ffn1_w3", "ffn2_w1", "ffn2_w3")
SIBLING_LAG = 2
EARLY_FORWARDS = 2
LONG_HOST_STEPS = 8
GATHER_GROUPS = (("ffn1_w1", "ffn1_w3"), ("ffn1_w2",), ("w_in",), ("w_branch_a", "w_branch_b", "w_out"),
                 ("ffn2_w1", "ffn2_w3", "ffn2_w2"))


def _pack_small(d):
    rows = []
    for n in SMALL:
        flat = d[n].reshape(-1)
        pad = (-flat.shape[0]) % LANES
        rows.append(jnp.pad(flat, (0, pad)).reshape(-1, LANES))
    buf = jnp.concatenate(rows, axis=0)
    return jnp.pad(buf, ((0, (-buf.shape[0]) % 8), (0, 0)))


def _unpack_small(buf, like):
    out, r = {}, 0
    for n in SMALL:
        size = like[n].size
        nr = -(-size // LANES)
        out[n] = buf[r:r + nr].reshape(-1)[:size].reshape(like[n].shape)
        r += nr
    return out


def kernel(x, ffn1_norm, ffn1_w1, ffn1_w3, ffn1_w2, mix_norm, w_in, b_gate, q_norm, k_norm, rel_bias, w_branch_a, w_branch_b, w_out, ffn2_norm, ffn2_w1, ffn2_w3, ffn2_w2, final_norm, loss_target, m_ffn1_norm, m_ffn1_w1, m_ffn1_w3, m_ffn1_w2, m_mix_norm, m_w_in, m_b_gate, m_q_norm, m_k_norm, m_rel_bias, m_w_branch_a, m_w_branch_b, m_w_out, m_ffn2_norm, m_ffn2_w1, m_ffn2_w3, m_ffn2_w2, m_final_norm, v_ffn1_norm, v_ffn1_w1, v_ffn1_w3, v_ffn1_w2, v_mix_norm, v_w_in, v_b_gate, v_q_norm, v_k_norm, v_rel_bias, v_w_branch_a, v_w_branch_b, v_w_out, v_ffn2_norm, v_ffn2_w1, v_ffn2_w3, v_ffn2_w2, v_final_norm):
    given = dict(locals())
    w = {n: given[n] for n in ORDER}
    m = {n: given["m_" + n] for n in ORDER}
    v = {n: given["v_" + n] for n in ORDER}
    T, D = x.shape[1], x.shape[2]

    def stored(a, n):
        a = a.reshape(a.shape[1:])
        return a.T if n in TRANSPOSED else a

    def returned(a, n):
        return (a.T if n in TRANSPOSED else a).reshape(w[n].shape)

    quarter = {n: stored(w[n], n) for n in BIG}
    send, recv, _, land_thru, token = _exchange_start(
        "gather_start", None, _own_slots("own_weights", [quarter[n] for n in BIG]), "gather")
    index = {n: i for i, n in enumerate(BIG)}
    ready, filling = {}, {}

    def landed_halves(group, after):
        ids = [index[n] for n in group]
        return _exchange_wait("gather_wait_" + group[0], None, [land_thru[i] for i in ids],
                              [send[i] for i in ids], [recv[i] for i in ids], after, "gather")[1]

    def prefetch_w(name, after):
        group = next(g for g in GATHER_GROUPS if name in g)
        started = _exchange_start("fill_start_" + group[0], None, landed_halves(group, after), "fill")
        filling[group] = started
        return [started[4]]

    def get_w(name, after):
        if name not in ready:
            group = next(g for g in GATHER_GROUPS if name in g)
            if group in filling:
                f_send, f_recv, _, thru, _ = filling[group]
                stacks = _exchange_wait("fill_wait_" + group[0], None, thru, f_send, f_recv, after, "fill")[1]
            else:
                stacks = _fill_from_sibling("gather_fill_" + group[0], landed_halves(group, after))
            for n, st in zip(group, stacks):
                ready[n] = st.reshape(D, D) if n in ("w_branch_b", "w_out") else st
        return ready[name]

    scattered, forwarded = [], []

    def forward_oldest(after):
        names, s_sem, r_sem, srcs, lands = scattered.pop(0)
        _, landed = _exchange_wait("scatter_wait_" + names[0], srcs, lands, s_sem, r_sem, after, "scatter")
        started = _exchange_start("sibling_start_" + names[0], landed, [lax.empty(a.shape, a.dtype) for a in landed], "sibling")
        forwarded.append((names,) + tuple(started[:4]))
        return started[4]

    def put_g(grads):
        names = list(grads)
        stacks = [grads[n].reshape((4,) + quarter[n].shape) for n in names]
        lands = _own_slots("own_grad_" + names[0], stacks, from_stack=True)
        if len(scattered) < (1 if len(forwarded) < EARLY_FORWARDS else SIBLING_LAG):
            started = _exchange_start("scatter_start_" + names[0], stacks, lands, "scatter")
            scattered.append((names,) + tuple(started[:4]))
            return [started[4]]
        old_names, s_sem, r_sem, old_srcs, old_lands = scattered.pop(0)
        scatter, sibling, token = _scatter_and_forward("scatter_start_" + names[0], stacks, lands, old_srcs, old_lands, s_sem, r_sem)
        scattered.append((names,) + scatter)
        forwarded.append((old_names,) + sibling)
        return [token]

    grads, deltas, new_m, new_v = {}, {}, {}, {}
    arrived, riding = {}, set()

    def partials(gi, after):
        if gi not in arrived:
            names, s_sem, r_sem, srcs, lands = forwarded[gi]
            arrived[gi] = _exchange_wait("sibling_wait_" + names[0], srcs, lands, s_sem, r_sem, after, "sibling")
        return arrived[gi]

    def deliver_to(n):
        def deliver(*res):
            grads[n], deltas[n], new_m[n], new_v[n] = [returned(r, n) for r in res]
        return deliver

    def take_rider(steps, after):
        cap = None if steps >= LONG_HOST_STEPS else quarter["ffn1_w2"].size
        waiting = [(quarter[n].size, gi, k, n) for gi, entry in enumerate(forwarded) for k, n in enumerate(entry[0])
                   if n not in riding and (cap is None or quarter[n].size <= cap)]
        for _, gi, k, n in sorted(waiting, reverse=True):
            mine, theirs = partials(gi, after)
            rider = _adamw_rider(quarter[n], stored(m[n], n), stored(v[n], n), mine[k], theirs[k], steps, deliver_to(n))
            if rider is not None:
                riding.add(n)
                return rider
        return None

    small = {n: w[n] for n in SMALL}
    packed = [_pack_small({n: d[n] for n in SMALL}) for d in (w, m, v)]
    loss_cols, grad_x, gs = _local_step(x.reshape(T, D), loss_target.reshape(T, D), small, get_w, put_g, deps=[token] + packed,
                                        prefetch_w=prefetch_w, take_rider=take_rider)

    gs = {n: gs[n].reshape(w[n].shape) for n in SMALL}
    packed_g = _pack_small(gs)
    n_small = packed_g.shape[0]
    small_sends, small_recvs, small_src, small_land, small_token = _allreduce_small_start(
        jnp.concatenate([packed_g, loss_cols.reshape(-1, LANES)], axis=0))

    after = [grad_x, small_token]
    while scattered:
        after = forward_oldest(after)
    for gi, entry in enumerate(forwarded):
        mine, theirs = partials(gi, after)
        for n, a, b in zip(entry[0], mine, theirs):
            if n not in riding:
                res = _adamw_big(f"adamw_{n}", quarter[n], stored(m[n], n), stored(v[n], n), a, b)
                deliver_to(n)(*res)
                after = res[0]
    summed = _allreduce_small_finish(small_sends, small_recvs, small_src, small_land, [after])
    g_small, loss = summed[:n_small], jnp.sum(summed[n_small:])
    R = g_small.shape[0]
    res = _ew("adamw_small", lambda wv, mv, vv, g: (g,) + _adamw_math(wv, g, mv, vv),
              [_tiled(packed[0]), _tiled(packed[1]), _tiled(packed[2]), _tiled(g_small)], [(F32, LANES)] * 4, n_rows=R, rows=R)
    for d, buf in zip((grads, deltas, new_m, new_v), res):
        d.update(_unpack_small(buf, w))

    return (loss, grad_x.reshape(x.shape), *[grads[n] for n in ORDER], *[deltas[n] for n in ORDER],
            *[new_m[n] for n in ORDER], *[new_v[n] for n in ORDER])
```

```python
import functools
import math

import numpy as np
import jax
import jax.numpy as jnp
from jax import lax
from jax.experimental import pallas as pl
from jax.experimental.pallas import tpu as pltpu

F32 = jnp.float32
BF16 = jnp.bfloat16
MESH = pl.DeviceIdType.MESH

NEG_INF = -1e30
EPS = 1e-6
GRID_W = 64
ROPE_THETA = 10000.0
DILATIONS = (1, 4, 16)
BAND_HALF = 64
HEAD_A = 64
HEADS_A = 8
WIDTH_A = HEADS_A * HEAD_A
HEAD_B = 128
LOG2_E = math.log2(math.e)
QK_SCALE_LOG2 = HEAD_B ** -0.5 * LOG2_E
N_BUCKETS = 32
MAX_DISTANCE = 1024
ADAM_LR, ADAM_B1, ADAM_B2, ADAM_EPS, ADAM_WD, ADAM_STEP = 0.001, 0.9, 0.999, 1e-08, 0.01, 10

B_Q, B_K, B_V = 4608, 5632, 5888
G_A, G_B = 6144, 7168
IN_WIDTH = 8192

VMEM_LIMIT_BYTES = 56 * 1024 * 1024
QB_A = 128
QB_B = 256


def _params(*sem):
    return pltpu.CompilerParams(dimension_semantics=sem, vmem_limit_bytes=VMEM_LIMIT_BYTES)


def _bs(shape, fn):
    return pl.BlockSpec(shape, fn)


def _resident(shape, fn):
    return pl.BlockSpec(shape, fn, pipeline_mode=pl.Buffered(1))


def _mm(name, grid, pairs, out_shape, out_spec, dims, *, extras=(), epilogue=None, deps=(), reds=(), rider=None):
    n_pairs, n_extra, n_deps = len(pairs), len(extras), len(deps)
    operands = [p[0] for p in pairs] + [p[2] for p in pairs] + [e[0] for e in extras] + list(deps)
    in_specs = [p[1] for p in pairs] + [p[3] for p in pairs] + [e[1] for e in extras] + _any_specs(n_deps)
    single = not isinstance(out_shape, (list, tuple))
    out_shapes = [out_shape] if single else list(out_shape)
    out_specs = [out_spec] if single else list(out_spec)
    n_out = len(out_shapes)
    out_shapes += [jax.ShapeDtypeStruct((1, w), F32) for w in reds]
    out_specs += [_bs((1, w), lambda *_: (0, 0)) for w in reds]
    n_rin = 0
    if rider is not None:
        assert rider["n_blocks"] <= grid[0]
        n_rin = len(rider["operands"])
        operands += list(rider["operands"])
        in_specs += list(rider["in_specs"])
        out_shapes += list(rider["out_shape"])
        out_specs += list(rider["out_specs"])

    def body(*refs):
        a_refs, b_refs = refs[:n_pairs], refs[n_pairs:2 * n_pairs]
        e_refs = refs[2 * n_pairs:2 * n_pairs + n_extra]
        o_refs = refs[2 * n_pairs + n_extra + n_deps + n_rin:]
        if rider is not None:
            r_in = refs[2 * n_pairs + n_extra + n_deps:2 * n_pairs + n_extra + n_deps + n_rin]
            r_out = o_refs[n_out + len(reds):]

            @pl.when(pl.program_id(0) < rider["n_blocks"])
            def _():
                for ref, val in zip(r_out, rider["fn"](*[r[...] for r in r_in])):
                    ref[...] = val.astype(ref.dtype)
        acc = None
        for a_ref, b_ref in zip(a_refs, b_refs):
            t = lax.dot_general(a_ref[...], b_ref[...], (dims, ((), ())), preferred_element_type=F32)
            acc = t if acc is None else acc + t
        vals = acc if epilogue is None else epilogue(acc, *[e[...] for e in e_refs])
        if not isinstance(vals, (list, tuple)):
            vals = (vals,)
        for o_ref, v in zip(o_refs[:n_out], vals[:n_out]):
            o_ref[...] = v.astype(o_ref.dtype)
        if reds:
            first = functools.reduce(jnp.logical_and, [pl.program_id(ax) == 0 for ax in range(len(grid))])
            for r_ref, v in zip(o_refs[n_out:], vals[n_out:]):
                @pl.when(first)
                def _(r_ref=r_ref):
                    r_ref[...] = jnp.zeros_like(r_ref)
                r_ref[...] += v

    sem = ["arbitrary" if (reds or rider is not None) else "parallel"] * len(grid)
    res = pl.pallas_call(
        body, out_shape=out_shapes, grid=grid, in_specs=in_specs, out_specs=out_specs,
        compiler_params=_params(*sem), name=name)(*operands)
    if rider is not None:
        rider["deliver"](res[n_out + len(reds):])
        res = res[:n_out + len(reds)]
    return res[0] if (single and not reds) else res


NN = ((1,), (0,))
NT = ((1,), (1,))
TN = ((0,), (0,))


def _mm_wgrad(name, a, b, *, a_cols, b_cols, tm, tn, J, deps=(), rider=None):
    def pick(arr, cols, t):
        if arr.ndim == 3:
            T, c = arr.shape[1], arr.shape[2]
            t = min(t, c)
            return T, c, t, (lambda sel: _bs((None, T, t), lambda j, i, k: (j, 0, sel(i, k))))
        T = arr.shape[0]
        c = arr.shape[1] if cols is None else cols
        t = min(t, c)
        per = c // t
        if cols is None:
            if per == 1:
                return T, c, t, (lambda sel: _resident((T, t), lambda j, i, k: (0, 0)))
            return T, c, t, (lambda sel: _bs((T, t), lambda j, i, k: (0, sel(i, k))))
        return T, c, t, (lambda sel: _bs((T, t), lambda j, i, k: (0, j * per + sel(i, k))))
    _, ca, tm, mk_a = pick(a, a_cols, tm)
    _, cb, tn, mk_b = pick(b, b_cols, tn)
    return _mm(name, (J, ca // tm, cb // tn),
               [(a, mk_a(lambda i, k: i), b, mk_b(lambda i, k: k))],
               jax.ShapeDtypeStruct((J, ca, cb), BF16), _bs((None, tm, tn), lambda j, i, k: (j, i, k)), TN, deps=deps, rider=rider)


def _tiled(arr, width=None, col=0, rowblk=0):
    return ("t", arr, arr.shape[1] if width is None else width, col, rowblk)


def _table(arr):
    return ("f", arr)


def _whole(arr):
    return ("w", arr)


def _ew(name, fn, ins, outs, *, n_rows, rows, reds=(), ncols=1, deps=()):
    nrb = n_rows // rows
    n_deps = len(deps)
    operands, in_specs = [], []
    for spec in ins:
        if spec[0] == "t":
            _, arr, width, col, rowblk = spec
            step = 1 if ncols > 1 else 0
            in_specs.append(_bs((rows, width), lambda c, i, col=col, rowblk=rowblk, step=step: (rowblk + i, col + c * step)))
        elif spec[0] == "f":
            arr = spec[1]
            in_specs.append(_bs((rows, arr.shape[1]), lambda c, i: (i, 0)))
        else:
            arr = spec[1]
            nd = arr.ndim
            if nd == 3:
                in_specs.append(_bs((None,) + arr.shape[1:], lambda c, i: (c, 0, 0)))
            else:
                in_specs.append(_bs(arr.shape, lambda c, i, nd=nd: (0,) * nd))
        operands.append(arr)
    out_shapes = [jax.ShapeDtypeStruct((n_rows, ncols * w), dt) for dt, w in outs]
    out_specs = [_bs((rows, w), lambda c, i: (i, c)) for _, w in outs]
    out_shapes += [jax.ShapeDtypeStruct((ncols, 1, w), F32) for w in reds]
    out_specs += [_bs((None, 1, w), lambda c, i: (c, 0, 0)) for w in reds]
    n_in, n_out, n_red = len(ins), len(outs), len(reds)
    operands += list(deps)
    in_specs += _any_specs(n_deps)

    def body(*refs):
        vals = fn(*[r[...] for r in refs[:n_in]])
        if not isinstance(vals, (tuple, list)):
            vals = (vals,)
        o_refs = refs[n_in + n_deps:]
        for o_ref, v in zip(o_refs[:n_out], vals[:n_out]):
            o_ref[...] = v.astype(o_ref.dtype)
        if n_red:
            i = pl.program_id(1)
            for r_ref, v in zip(o_refs[n_out:], vals[n_out:]):
                @pl.when(i == 0)
                def _(r_ref=r_ref):
                    r_ref[...] = jnp.zeros_like(r_ref)
                r_ref[...] += v

    res = pl.pallas_call(
        body, out_shape=out_shapes, grid=(ncols, nrb), in_specs=in_specs, out_specs=out_specs,
        compiler_params=_params("parallel", "arbitrary" if n_red else "parallel"), name=name)(*operands)
    return res


def _colsum(v):
    return jnp.sum(v, axis=0, keepdims=True)


def _rstd(x):
    return lax.rsqrt(jnp.mean(x * x, axis=-1, keepdims=True) + EPS)


def _sigmoid(x):
    return 0.5 * jnp.tanh(0.5 * x) + 0.5


def _norm_fwd(x, g):
    return x * _rstd(x) * g


def _norm_bwd(x, g, dy):
    r = _rstd(x)
    xh = x * r
    dxh = dy * g
    dx = r * (dxh - xh * jnp.mean(dxh * xh, axis=-1, keepdims=True))
    return dx, dy * xh


def _row_spec(arr, rows):
    if arr.shape[0] == 1:
        return _bs(arr.shape, lambda i: (0, 0))
    return _bs((rows, arr.shape[1]), lambda i: (i, 0))


def _ffn_fwd(tag, x, gain, get_w, deps=(), *, h=None, tail_ins=(), tail_fn=None, tail_outs=(F32,), tail_reds=()):
    T, D = x.shape
    if h is None:
        (h,) = _ew(f"{tag}_norm", lambda xv, g: _norm_fwd(xv, g), [_tiled(x), _whole(gain)], [(BF16, D)], n_rows=T, rows=512,
                   deps=deps)
    w1, w3 = get_w(f"{tag}_w1", h), get_w(f"{tag}_w3", h)
    J, f, _ = w1.shape
    tm = 1024

    def up(h_ref, w1_ref, w3_ref, u_ref, g_ref, a_ref):
        hv = h_ref[...]
        u = lax.dot_general(hv, w1_ref[...], (NT, ((), ())), preferred_element_type=F32)
        g = lax.dot_general(hv, w3_ref[...], (NT, ((), ())), preferred_element_type=F32)
        u_ref[...] = u.astype(BF16)
        g_ref[...] = g.astype(BF16)
        a_ref[...] = (u * _sigmoid(u) * g).astype(BF16)

    slab = _bs((None, tm, f), lambda j, i: (j, i, 0))
    w_spec = _bs((None, f, D), lambda j, i: (j, 0, 0))
    u, g, a = pl.pallas_call(
        up, out_shape=[jax.ShapeDtypeStruct((J, T, f), BF16)] * 3, grid=(J, T // tm),
        in_specs=[_bs((tm, D), lambda j, i: (i, 0)), w_spec, w_spec], out_specs=[slab] * 3,
        compiler_params=_params("parallel", "parallel"), name=f"{tag}_up")(h, w1, w3)
    w2 = get_w(f"{tag}_w2", a)
    def tail(acc, xv, *rest):
        y = xv + 0.5 * acc
        return y if tail_fn is None else tail_fn(y, *rest)

    row = _bs((512, D), lambda i: (i, 0))
    res = _mm(f"{tag}_down", (T // 512,),
              [(a, _bs((None, 512, f), lambda i, j=j: (j, i, 0)), w2, _resident((None, f, D), lambda i, j=j: (j, 0, 0)))
               for j in range(J)],
              [jax.ShapeDtypeStruct((T, D), dt) for dt in tail_outs], [row] * len(tail_outs), NN,
              extras=[(x, row)] + [(t, _row_spec(t, 512)) for t in tail_ins], epilogue=tail, reds=tail_reds)
    return res, (h, u, g, a)


def _dh_norm_bwd(name, rows, pairs, dims, x, gain, dres, deps, also_bf16=False, rider=None):
    T, D = x.shape

    def epilogue(dh, xv, gv, dr):
        dx, dgr = _norm_bwd(xv, gv, dh)
        dx = dx + dr
        return (dx, 0.5 * dx) + ((dx,) if also_bf16 else ()) + (_colsum(dgr),)

    dts = [F32, BF16] + ([BF16] if also_bf16 else [])
    row = _bs((rows, D), lambda i: (i, 0))
    return _mm(name, (T // rows,), pairs, [jax.ShapeDtypeStruct((T, D), dt) for dt in dts], [row] * len(dts), dims,
               extras=[(x, row), (gain, _row_spec(gain, rows)), (dres, row)], epilogue=epilogue, deps=deps, reds=(D,), rider=rider)


def _ffn_bwd(tag, x, gain, get_w, put_g, saved, dy, dy_half, also_bf16=False, last=False,
             take_rider=lambda steps, after, cap=None: None):
    h, u, g, a = saved
    T, D = x.shape
    w1, w3, w2 = [get_w(f"{tag}_{n}", dy_half) for n in ("w1", "w3", "w2")]
    J, f, _ = w1.shape
    dw2 = _mm_wgrad(f"{tag}_bwd_dw2", a, dy_half, a_cols=None, b_cols=None, tm=f, tn=D, J=J, rider=take_rider(J, dy_half))
    deps = put_g({f"{tag}_w2": dw2}) if last else []
    tm = 1024

    def up_bwd(dy_ref, w2_ref, u_ref, g_ref, *rest):
        du_ref, dg_ref = rest[-2:]
        da = lax.dot_general(dy_ref[...], w2_ref[...], (NT, ((), ())), preferred_element_type=F32)
        uv, gv = u_ref[...].astype(F32), g_ref[...].astype(F32)
        s = _sigmoid(uv)
        silu = uv * s
        du_ref[...] = (da * gv * (s + silu - silu * s)).astype(BF16)
        dg_ref[...] = (da * silu).astype(BF16)

    slab = _bs((None, tm, f), lambda j, i: (j, i, 0))
    du, dg = pl.pallas_call(
        up_bwd, out_shape=[jax.ShapeDtypeStruct((J, T, f), BF16)] * 2, grid=(J, T // tm),
        in_specs=[_bs((tm, D), lambda j, i: (i, 0)), _bs((None, f, D), lambda j, i: (j, 0, 0)), slab, slab] + _any_specs(len(deps)),
        out_specs=[slab] * 2, compiler_params=_params("parallel", "parallel"), name=f"{tag}_bwd_up")(dy_half, w2, u, g, *deps)
    small_update = D * D // 4
    dw1 = _mm_wgrad(f"{tag}_bwd_dw1", du, h, a_cols=None, b_cols=None, tm=f, tn=D, J=J,
                    rider=take_rider(J, du, small_update))
    deps = put_g({f"{tag}_w1": dw1}) if last else []
    dw3 = _mm_wgrad(f"{tag}_bwd_dw3", dg, h, a_cols=None, b_cols=None, tm=f, tn=D, J=J, deps=deps,
                    rider=take_rider(J, dg, small_update))
    deps = put_g({f"{tag}_w3": dw3} if last else {f"{tag}_w2": dw2, f"{tag}_w1": dw1, f"{tag}_w3": dw3})
    pairs = []
    for j in range(J):
        a_spec = _bs((None, 256, f), lambda i, j=j: (j, i, 0))
        w_spec = _resident((None, f, D), lambda i, j=j: (j, 0, 0))
        pairs += [(du, a_spec, w1, w_spec), (dg, a_spec, w3, w_spec)]
    return _dh_norm_bwd(f"{tag}_bwd_dh", 256, pairs, NN, x, gain, dy, deps, also_bf16, rider=take_rider(T // 256, dw3))


def _t5_bucket(rel):
    n = N_BUCKETS // 2
    max_exact = n // 2
    ret = jnp.where(rel > 0, n, 0)
    a = jnp.abs(rel)
    af = jnp.maximum(a, 1).astype(F32)
    large = max_exact + (jnp.log(af / max_exact) / math.log(MAX_DISTANCE / max_exact) * (n - max_exact)).astype(jnp.int32)
    large = jnp.minimum(large, n - 1)
    return ret + jnp.where(a < max_exact, a, large)


WIN_A = QB_A + 2 * BAND_HALF
WIN_SHIFTS = (0, BAND_HALF, 2 * BAND_HALF)


def _window_variant(n, nblk):
    return jnp.where(n == 0, 0, jnp.where(n == nblk - 1, 2, 1))


def _window_start(n, nblk):
    return pl.multiple_of(jnp.clip(n * QB_A - BAND_HALF, 0, nblk * QB_A - WIN_A), BAND_HALF)


def _band_steps(xp=jnp):
    qi = xp.arange(QB_A, dtype=xp.int32)[None, :, None]
    kj = xp.arange(WIN_A, dtype=xp.int32)[None, None, :]
    return kj - qi - xp.asarray(WIN_SHIFTS, dtype=xp.int32)[:, None, None]


def _bias_tiles(rel_bias):
    wide = QB_A + 2 * WIN_SHIFTS[-1]
    qi = jnp.arange(QB_A, dtype=jnp.int32)[:, None]
    steps = jnp.arange(wide, dtype=jnp.int32)[None, :] - WIN_SHIFTS[-1] - qi
    buckets = jnp.stack([_t5_bucket(steps * d) for d in DILATIONS])
    inband = (jnp.abs(steps) <= BAND_HALF).astype(jnp.int32)
    n_heads = rel_bias.shape[1]

    def body(tab_ref, b_ref, m_ref, o_ref):
        hd = pl.program_id(0)
        bkt = b_ref[...]
        acc = jnp.zeros(bkt.shape, F32)
        for b in range(N_BUCKETS):
            acc = jnp.where(bkt == b, tab_ref[b, hd], acc)
        o_ref[...] = jnp.where(m_ref[...] > 0, acc, NEG_INF)

    base = pl.pallas_call(
        body, out_shape=jax.ShapeDtypeStruct((n_heads, QB_A, wide), F32), grid=(n_heads,),
        in_specs=[pl.BlockSpec(memory_space=pltpu.SMEM),
                  _bs((None, QB_A, wide), lambda hd: (hd // HEADS_A, 0, 0)),
                  _bs((QB_A, wide), lambda hd: (0, 0))],
        out_specs=_bs((None, QB_A, wide), lambda hd: (hd, 0, 0)),
        compiler_params=_params("parallel"), name="a_bias_tiles")(rel_bias, buckets, inband)
    base = base.reshape(len(DILATIONS), HEADS_A, QB_A, wide)
    return jnp.stack([base[..., WIN_SHIFTS[-1] - s:WIN_SHIFTS[-1] - s + WIN_A] for s in WIN_SHIFTS], axis=1)


def _bias_grad(dbias):
    steps = _band_steps(np)
    inband = np.abs(steps) <= BAND_HALF
    present = []
    for d in DILATIONS:
        rel = steps * d
        a = np.abs(rel)
        large = 8 + (np.log(np.maximum(a, 1) / 8.0) / math.log(MAX_DISTANCE / 8.0) * 8).astype(np.int64)
        bk = np.where(rel > 0, 16, 0) + np.where(a < 8, a, np.minimum(large, 15))
        present.append([sorted(set(bk[v][inband[v]].tolist())) for v in range(3)])
    buckets = jnp.stack([_t5_bucket(_band_steps() * d) for d in DILATIONS])
    n_heads = len(DILATIONS) * HEADS_A

    def body(b_ref, d_ref, o_ref):
        row = lax.broadcasted_iota(jnp.int32, (N_BUCKETS, n_heads), 0)
        col = lax.broadcasted_iota(jnp.int32, (N_BUCKETS, n_heads), 1)
        out = jnp.zeros((N_BUCKETS, n_heads), F32)
        for grp in range(len(DILATIONS)):
            for hh in range(HEADS_A):
                hd = grp * HEADS_A + hh
                for b in sorted(set(sum(present[grp], []))):
                    tot = jnp.zeros((), F32)
                    for v in range(3):
                        if b in present[grp][v]:
                            tot = tot + jnp.sum(jnp.where(b_ref[grp, v] == b, d_ref[grp, v, hh], 0.0))
                    out = jnp.where((row == b) & (col == hd), tot, out)
        o_ref[...] = out

    return pl.pallas_call(
        body, out_shape=jax.ShapeDtypeStruct((N_BUCKETS, n_heads), F32),
        compiler_params=pltpu.CompilerParams(vmem_limit_bytes=VMEM_LIMIT_BYTES), name="a_bias_grad")(buckets, dbias)


def _lane_is_second_head(shape):
    return lax.broadcasted_iota(jnp.int32, shape, len(shape) - 1) >= HEAD_A


VIEW_ROWS = 512


def _view_chunks():
    return [pltpu.VMEM((VIEW_ROWS, LANES), F32)] * (WIDTH_A // LANES)


def _rows_to_view(x_ref, col, o_ref, ocol, d, chunks):
    n = VIEW_ROWS // d
    for c, scr in enumerate(chunks):
        scr[...] = x_ref[:, col + c * LANES:col + (c + 1) * LANES].astype(F32)
        for r in range(d):
            at = ocol + r * WIDTH_A + c * LANES
            o_ref[:, at:at + LANES] = scr[pl.ds(r, n, stride=d), :].astype(o_ref.dtype)


def _view_to_rows(v_ref, o_ref, col, d, chunks):
    n = VIEW_ROWS // d
    for c, scr in enumerate(chunks):
        if d == 1:
            o_ref[:, col + c * LANES:col + (c + 1) * LANES] = v_ref[:, c * LANES:(c + 1) * LANES].astype(o_ref.dtype)
            continue
        for r in range(d):
            scr[pl.ds(r, n, stride=d), :] = v_ref[:, r * WIDTH_A + c * LANES:r * WIDTH_A + (c + 1) * LANES].astype(F32)
        o_ref[:, col + c * LANES:col + (c + 1) * LANES] = scr[...].astype(o_ref.dtype)


def _group_view(proj, grp, d):
    T = proj.shape[0]
    if d == 1:
        return proj, (lambda part, r: grp * 3 + part)

    def body(x_ref, o_ref, *chunks):
        for part in range(3):
            _rows_to_view(x_ref, part * WIDTH_A, o_ref, part * d * WIDTH_A, d, chunks)

    view = pl.pallas_call(
        body, out_shape=jax.ShapeDtypeStruct((T // d, 3 * d * WIDTH_A), proj.dtype), grid=(T // VIEW_ROWS,),
        in_specs=[_bs((VIEW_ROWS, 3 * WIDTH_A), lambda i: (i, grp))],
        out_specs=_bs((VIEW_ROWS // d, 3 * d * WIDTH_A), lambda i: (i, 0)),
        scratch_shapes=_view_chunks(), compiler_params=_params("parallel"), name=f"a_view_d{d}")(proj)
    return view, (lambda part, r: part * d + r)


def _stack_heads(v2, second):
    zero = jnp.zeros_like(v2)
    return jnp.concatenate([jnp.where(second, zero, v2), jnp.where(second, v2, zero)], axis=0)


def _unstack_heads(v, second):
    return jnp.where(second, v[QB_A:], v[:QB_A])


def _dil_fwd(view, bias, d):
    pv, colblk = view
    L = pv.shape[0]
    nblk = L // QB_A
    W2 = 2 * HEAD_A
    scale = HEAD_A ** -0.5

    def body(q_ref, k_ref, v_ref, b_ref, o_ref, l_ref):
        win = pl.ds(_window_start(pl.program_id(1), nblk), WIN_A)
        second = _lane_is_second_head((QB_A, W2))
        pairs = range(HEADS_A // 2)
        cols = [slice(hp * W2, (hp + 1) * W2) for hp in pairs]
        s = [lax.dot_general(_stack_heads(q_ref[:, cols[hp]], second), k_ref[win, cols[hp]], (NT, ((), ())),
                             preferred_element_type=F32) * scale + b_ref[2 * hp:2 * hp + 2].reshape(2 * QB_A, WIN_A)
             for hp in pairs]
        m = [jnp.max(x, axis=-1, keepdims=True) for x in s]
        p = [jnp.exp(x - mx) for x, mx in zip(s, m)]
        l = [jnp.sum(x, axis=-1, keepdims=True) for x in p]
        res = [jnp.dot(p[hp].astype(BF16), v_ref[win, cols[hp]], preferred_element_type=F32) / l[hp] for hp in pairs]
        o_ref[...] = jnp.concatenate([_unstack_heads(x, second) for x in res], axis=1).astype(o_ref.dtype)
        l_ref[...] = jnp.concatenate([_unstack_heads(jnp.broadcast_to(mx + jnp.log(lx), (2 * QB_A, W2)), second)
                                      for mx, lx in zip(m, l)], axis=1)

    in_specs = [_bs((QB_A, WIDTH_A), lambda r, n: (n, colblk(0, r))),
                _bs((L, WIDTH_A), lambda r, n: (0, colblk(1, r))), _bs((L, WIDTH_A), lambda r, n: (0, colblk(2, r))),
                _bs((None, HEADS_A, QB_A, WIN_A), lambda r, n: (_window_variant(n, nblk), 0, 0, 0))]
    o, lse = pl.pallas_call(
        body, out_shape=[jax.ShapeDtypeStruct((L, d * WIDTH_A), BF16), jax.ShapeDtypeStruct((L, d * WIDTH_A), F32)],
        grid=(d, nblk), in_specs=in_specs,
        out_specs=[_bs((QB_A, WIDTH_A), lambda r, n: (n, r)), _bs((QB_A, WIDTH_A), lambda r, n: (n, r))],
        compiler_params=_params("parallel", "parallel"), name=f"a_fwd_d{d}")(pv, pv, pv, bias)
    return o, lse


def _dil_bwd(view_qkv, bias, do, lse, cterm, d):
    pv, colblk = view_qkv
    L = pv.shape[0]
    nblk = L // QB_A
    W2 = 2 * HEAD_A
    PPS = 4
    WS = PPS * W2
    ob = WIDTH_A // WS
    scale = HEAD_A ** -0.5

    def body(q_ref, k_ref, v_ref, do_ref, l_ref, c_ref, b_ref, dq_ref, dk_ref, dv_ref, db_ref):
        r, n = pl.program_id(1), pl.program_id(2)

        @pl.when(n == 0)
        def _():
            dk_ref[...] = jnp.zeros_like(dk_ref)
            dv_ref[...] = jnp.zeros_like(dv_ref)

        @pl.when((n == 0) & (r == 0))
        def _():
            db_ref[...] = jnp.zeros_like(db_ref)

        second = _lane_is_second_head((QB_A, W2))
        win = pl.ds(_window_start(n, nblk), WIN_A)
        variant = _window_variant(n, nblk)
        pairs = range(PPS)
        cols = [slice(pp * W2, (pp + 1) * W2) for pp in pairs]

        def head_rows(ref, pp):
            v2 = ref[:, cols[pp]]
            return jnp.concatenate([v2[:, 0:1], v2[:, HEAD_A:HEAD_A + 1]], axis=0)

        kw = [k_ref[win, c] for c in cols]
        vw = [v_ref[win, c] for c in cols]
        qs = [_stack_heads(q_ref[:, c], second) for c in cols]
        dos = [_stack_heads(do_ref[:, c], second) for c in cols]
        s = [lax.dot_general(qs[pp], kw[pp], (NT, ((), ())), preferred_element_type=F32) for pp in pairs]
        dp = [lax.dot_general(dos[pp], vw[pp], (NT, ((), ())), preferred_element_type=F32) for pp in pairs]
        p = [jnp.exp(s[pp] * scale + b_ref[2 * pp:2 * pp + 2].reshape(2 * QB_A, WIN_A) - head_rows(l_ref, pp)) for pp in pairs]
        ds = [p[pp] * (dp[pp] + head_rows(c_ref, pp)) for pp in pairs]
        db_ref[variant] += jnp.concatenate([x.reshape(2, QB_A, WIN_A) for x in ds], axis=0)
        pb = [x.astype(BF16) for x in p]
        dsb = [(x * scale).astype(BF16) for x in ds]
        dq_ref[...] = jnp.concatenate([_unstack_heads(jnp.dot(dsb[pp], kw[pp], preferred_element_type=F32), second)
                                       for pp in pairs], axis=1).astype(dq_ref.dtype)
        dk_ref[win, :] += jnp.concatenate([lax.dot_general(dsb[pp], qs[pp], (TN, ((), ())), preferred_element_type=F32)
                                           for pp in pairs], axis=1)
        dv_ref[win, :] += jnp.concatenate([lax.dot_general(pb[pp], dos[pp], (TN, ((), ())), preferred_element_type=F32)
                                           for pp in pairs], axis=1)

    kv_spec = _resident if d == 1 else _bs
    in_specs = [_bs((QB_A, WS), lambda hp, r, n: (n, colblk(0, r) * ob + hp)),
                kv_spec((L, WS), lambda hp, r, n: (0, colblk(1, r) * ob + hp)),
                kv_spec((L, WS), lambda hp, r, n: (0, colblk(2, r) * ob + hp))]
    in_specs += [_bs((QB_A, WS), lambda hp, r, n: (n, r * ob + hp))] * 3
    in_specs += [_bs((None, 2 * PPS, QB_A, WIN_A), lambda hp, r, n: (_window_variant(n, nblk), hp, 0, 0))]
    out_shape = [jax.ShapeDtypeStruct((L, d * WIDTH_A), BF16), jax.ShapeDtypeStruct((L, d * WIDTH_A), F32),
                 jax.ShapeDtypeStruct((L, d * WIDTH_A), F32), jax.ShapeDtypeStruct((3, HEADS_A, QB_A, WIN_A), F32)]
    out_specs = [_bs((QB_A, WS), lambda hp, r, n: (n, r * ob + hp)),
                 _bs((L, WS), lambda hp, r, n: (0, r * ob + hp)), _bs((L, WS), lambda hp, r, n: (0, r * ob + hp)),
                 _bs((3, 2 * PPS, QB_A, WIN_A), lambda hp, r, n: (0, hp, 0, 0))]
    dq, dk, dv, db = pl.pallas_call(
        body, out_shape=out_shape, grid=(ob, d, nblk), in_specs=in_specs, out_specs=out_specs,
        compiler_params=_params("arbitrary", "arbitrary", "arbitrary"), name=f"a_bwd_d{d}")(
            pv, pv, pv, do, lse, cterm, bias)
    return dq, dk, dv, db


def _assemble_dproj(a_parts, dq_b, dk_b, dv_b, dga, dgb):
    T = dq_b.shape[0]
    flat = [(a_parts[part][g], d) for part in range(3) for g, d in enumerate(DILATIONS)]
    rest = [dq_b, dk_b, dv_b, dga, dgb]

    def body(*refs):
        views, others = refs[:len(flat)], refs[len(flat):len(flat) + len(rest)]
        o_ref, chunks = refs[len(flat) + len(rest)], refs[len(flat) + len(rest) + 1:]
        col = 0
        for v_ref, (_, d) in zip(views, flat):
            _view_to_rows(v_ref, o_ref, col, d, chunks)
            col += WIDTH_A
        for x_ref in others:
            w = x_ref.shape[1]
            o_ref[:, col:col + w] = x_ref[...].astype(o_ref.dtype)
            col += w

    in_specs = [_bs((VIEW_ROWS // d, d * WIDTH_A), lambda i: (i, 0)) for _, d in flat]
    in_specs += [_bs((VIEW_ROWS, x.shape[1]), lambda i: (i, 0)) for x in rest]
    return pl.pallas_call(
        body, out_shape=jax.ShapeDtypeStruct((T, IN_WIDTH), BF16), grid=(T // VIEW_ROWS,), in_specs=in_specs,
        out_specs=_bs((VIEW_ROWS, IN_WIDTH), lambda i: (i, 0)), scratch_shapes=_view_chunks(),
        compiler_params=_params("parallel"), name="mix_bwd_dproj")(*[a for a, _ in flat], *rest)


def _segment_ones():
    i = np.arange(WIDTH_A)
    return jnp.asarray((i[:, None] // HEAD_A == i[None, :] // HEAD_A).astype(np.float32), dtype=BF16)


def _group_weights(l0, l1, l2):
    m = jnp.maximum(jnp.maximum(l0, l1), l2)
    e = [jnp.exp(l - m) for l in (l0, l1, l2)]
    z = e[0] + e[1] + e[2]
    return [ei / z for ei in e]


def _view_specs():
    return [_bs((VIEW_ROWS // d, d * WIDTH_A), lambda i: (i, 0)) for d in DILATIONS]


def _stage_tiles(n):
    return [pltpu.VMEM((VIEW_ROWS, WIDTH_A), F32)] * n


def _token_rows(v_ref, stage, d, chunks):
    if d == 1:
        return v_ref[...].astype(F32)
    _view_to_rows(v_ref, stage, 0, d, chunks)
    return stage[...]


def _combine_fwd(outs, lses):
    T = outs[0].shape[0] * DILATIONS[0]
    n = len(DILATIONS)

    def body(*refs):
        o_refs, l_refs, oa_ref = refs[:n], refs[n:2 * n], refs[2 * n]
        o_st, l_st, chunks = refs[2 * n + 1:3 * n + 1], refs[3 * n + 1:4 * n + 1], refs[4 * n + 1:]
        o = [_token_rows(o_refs[g], o_st[g], d, chunks) for g, d in enumerate(DILATIONS)]
        w = _group_weights(*[_token_rows(l_refs[g], l_st[g], d, chunks) for g, d in enumerate(DILATIONS)])
        oa_ref[...] = (w[0] * o[0] + w[1] * o[1] + w[2] * o[2]).astype(oa_ref.dtype)

    return pl.pallas_call(
        body, out_shape=jax.ShapeDtypeStruct((T, WIDTH_A), BF16), grid=(T // VIEW_ROWS,),
        in_specs=_view_specs() * 2, out_specs=_bs((VIEW_ROWS, WIDTH_A), lambda i: (i, 0)),
        scratch_shapes=_stage_tiles(2 * n) + _view_chunks(), compiler_params=_params("parallel"), name="a_combine")(*outs, *lses)


def _combine_bwd(doa, outs, lses):
    T = doa.shape[0]
    n = len(DILATIONS)

    def body(*refs):
        d_ref, o_refs, l_refs, seg_ref = refs[0], refs[1:n + 1], refs[n + 1:2 * n + 1], refs[2 * n + 1]
        do_refs, c_refs = refs[2 * n + 2:3 * n + 2], refs[3 * n + 2:4 * n + 2]
        o_st, l_st = refs[4 * n + 2:5 * n + 2], refs[5 * n + 2:6 * n + 2]
        tmp, chunks = refs[6 * n + 2], refs[6 * n + 3:]
        o = [_token_rows(o_refs[g], o_st[g], d, chunks) for g, d in enumerate(DILATIONS)]
        w = _group_weights(*[_token_rows(l_refs[g], l_st[g], d, chunks) for g, d in enumerate(DILATIONS)])
        dv = d_ref[...].astype(F32)
        seg = seg_ref[...]
        tot = jnp.zeros(dv.shape, F32)
        for g in range(n):
            prod = w[g] * dv * o[g]
            hi = prod.astype(BF16)
            lo = (prod - hi.astype(F32)).astype(BF16)
            tot = tot + jnp.dot(hi, seg, preferred_element_type=F32) + jnp.dot(lo, seg, preferred_element_type=F32)
        for g, d in enumerate(DILATIONS):
            for ref, val in ((do_refs[g], w[g] * dv), (c_refs[g], -w[g] * tot)):
                if d == 1:
                    ref[...] = val.astype(ref.dtype)
                else:
                    tmp[...] = val
                    _rows_to_view(tmp, 0, ref, 0, d, chunks)

    views = [jax.ShapeDtypeStruct((T // d, d * WIDTH_A), dt) for dt in (BF16, F32) for d in DILATIONS]
    res = pl.pallas_call(
        body, out_shape=views, grid=(T // VIEW_ROWS,),
        in_specs=[_bs((VIEW_ROWS, WIDTH_A), lambda i: (i, 0))] + _view_specs() * 2 + [_bs((WIDTH_A, WIDTH_A), lambda i: (0, 0))],
        out_specs=_view_specs() * 2, scratch_shapes=_stage_tiles(2 * n + 1) + _view_chunks(),
        compiler_params=_params("parallel"), name="a_combine_bwd")(doa, *outs, *lses, _segment_ones())
    return res[:n], res[n:]


def _rope_tables(T):
    rows = T // GRID_W
    row = jnp.repeat(jnp.arange(rows, dtype=F32), GRID_W)
    col = jnp.tile(jnp.arange(GRID_W, dtype=F32), rows)
    n_freq = HEAD_B // 4
    freq = ROPE_THETA ** (-jnp.arange(n_freq, dtype=F32) / n_freq)
    ang = jnp.concatenate([row[:, None] * freq, col[:, None] * freq], axis=-1)
    cos, sin = jnp.repeat(jnp.cos(ang), 2, axis=1), jnp.repeat(jnp.sin(ang), 2, axis=1)
    sign = jnp.where(jnp.arange(HEAD_B) % 2 == 0, -1.0, 1.0).astype(F32)
    return cos, sin * sign


def _swap_pairs(v):
    even = lax.broadcasted_iota(jnp.int32, v.shape, v.ndim - 1) % 2 == 0
    n = v.shape[-1]
    return jnp.where(even, pltpu.roll(v, n - 1, v.ndim - 1), pltpu.roll(v, 1, v.ndim - 1))


def _qk_fwd(name, proj, col0, n_heads, gain, cos, sin, out_scale=1.0, deps=()):
    T = proj.shape[0]

    def fn(xr, g, c, s):
        xn = _norm_fwd(xr.astype(F32), g)
        return (xn * c + _swap_pairs(xn) * s) * out_scale

    (out,) = _ew(name, fn, [_tiled(proj, HEAD_B, col0 // HEAD_B), _whole(gain), _table(cos), _table(sin)],
                 [(BF16, HEAD_B)], n_rows=T, rows=2048, ncols=n_heads, deps=deps)
    return out


def _qk_bwd(name, dout, proj, col0, n_heads, gain, cos, sin, in_scale=1.0):
    T = proj.shape[0]

    def fn(dv, xr, g, c, s):
        dv = dv.astype(F32) * in_scale
        dxn = c * dv + _swap_pairs(s * dv)
        dx, dgr = _norm_bwd(xr.astype(F32), g, dxn)
        return dx, _colsum(dgr)

    dx, dg = _ew(name, fn, [_tiled(dout, HEAD_B, 0), _tiled(proj, HEAD_B, col0 // HEAD_B), _whole(gain),
                            _table(cos), _table(sin)],
                 [(BF16, HEAD_B)], n_rows=T, rows=2048, reds=(HEAD_B,), ncols=n_heads)
    return dx, jnp.sum(dg, axis=0)


def _gqa_fwd(qn, kn, proj, k_col=0):
    T = qn.shape[0]
    GW = 4 * HEAD_B
    QB = QB_B

    def body(q_ref, k_ref, v_ref, o_ref, l_ref):
        k = k_ref[...]
        v_ones = jnp.concatenate([v_ref[...], jnp.ones((T, HEAD_B), BF16)], axis=1)
        lane = lax.broadcasted_iota(jnp.int32, (QB, HEAD_B), 1)
        heads = range(4)
        s = [lax.dot_general(q_ref[:, g * HEAD_B:(g + 1) * HEAD_B], k, (NT, ((), ())), preferred_element_type=F32)
             for g in heads]
        m = [jnp.max(x, axis=-1, keepdims=True) for x in s]
        pv = [jnp.dot(jnp.exp2(x - mx).astype(BF16), v_ones, preferred_element_type=F32) for x, mx in zip(s, m)]
        l = [x[:, HEAD_B:HEAD_B + 1] for x in pv]
        o = [x[:, :HEAD_B] / lx for x, lx in zip(pv, l)]
        o_ref[...] = jnp.concatenate(o, axis=1).astype(o_ref.dtype)
        lse_all = jnp.zeros((QB, HEAD_B), F32)
        for g in heads:
            lse_all = jnp.where(lane == g, m[g] + jnp.log2(l[g]), lse_all)
        l_ref[...] = lse_all

    return pl.pallas_call(
        body, out_shape=[jax.ShapeDtypeStruct((T, 2 * GW), BF16), jax.ShapeDtypeStruct((2, T, HEAD_B), F32)],
        grid=(2, T // QB),
        in_specs=[_bs((QB, GW), lambda kv, i: (i, kv)), _bs((T, HEAD_B), lambda kv, i: (0, k_col + kv)),
                  _bs((T, HEAD_B), lambda kv, i: (0, B_V // HEAD_B + kv))],
        out_specs=[_bs((QB, GW), lambda kv, i: (i, kv)), _bs((None, QB, HEAD_B), lambda kv, i: (kv, i, 0))],
        compiler_params=_params("parallel", "parallel"), name="b_fwd")(qn, kn, proj)


def _gqa_bwd(qn, kn, proj, o, lse, do, deps=(), k_col=0):
    T = qn.shape[0]
    GW = 4 * HEAD_B

    def body(q_ref, k_ref, v_ref, o_ref, l_ref, do_ref, *rest):
        dq_ref, dk_ref, dv_ref = rest[-3:]
        i = pl.program_id(1)

        @pl.when(i == 0)
        def _():
            dk_ref[...] = jnp.zeros_like(dk_ref)
            dv_ref[...] = jnp.zeros_like(dv_ref)

        k, v = k_ref[...], v_ref[...]
        lse_all = l_ref[...]
        for g in range(4):
            cols = slice(g * HEAD_B, (g + 1) * HEAD_B)
            q, dob = q_ref[:, cols], do_ref[:, cols]
            delta = jnp.sum(dob.astype(F32) * o_ref[:, cols].astype(F32), axis=-1, keepdims=True)
            s = lax.dot_general(q, k, (NT, ((), ())), preferred_element_type=F32)
            p = jnp.exp2(s - lse_all[:, g:g + 1])
            dp = lax.dot_general(dob, v, (NT, ((), ())), preferred_element_type=F32)
            ds = (p * (dp - delta)).astype(BF16)
            dq_ref[:, cols] = jnp.dot(ds, k, preferred_element_type=F32).astype(dq_ref.dtype)
            dk_ref[...] += lax.dot_general(ds, q, (TN, ((), ())), preferred_element_type=F32)
            dv_ref[...] += lax.dot_general(p.astype(BF16), dob, (TN, ((), ())), preferred_element_type=F32)

    return pl.pallas_call(
        body, out_shape=[jax.ShapeDtypeStruct((T, 2 * GW), BF16), jax.ShapeDtypeStruct((T, 2 * HEAD_B), F32),
                         jax.ShapeDtypeStruct((T, 2 * HEAD_B), F32)],
        grid=(2, T // QB_B),
        in_specs=[_bs((QB_B, GW), lambda kv, i: (i, kv)), _bs((T, HEAD_B), lambda kv, i: (0, k_col + kv)),
                  _bs((T, HEAD_B), lambda kv, i: (0, B_V // HEAD_B + kv)), _bs((QB_B, GW), lambda kv, i: (i, kv)),
                  _bs((None, QB_B, HEAD_B), lambda kv, i: (kv, i, 0)), _bs((QB_B, GW), lambda kv, i: (i, kv))] + _any_specs(len(deps)),
        out_specs=[_bs((QB_B, GW), lambda kv, i: (i, kv)), _bs((T, HEAD_B), lambda kv, i: (0, kv)),
                   _bs((T, HEAD_B), lambda kv, i: (0, kv))],
        compiler_params=_params("parallel", "arbitrary"), name="b_bwd")(qn, kn, proj, o, lse, do, *deps)


def _local_step(x, target, small, get_w, put_g, deps=(), prefetch_w=lambda name, after: [],
                take_rider=lambda steps, after, cap=None: None):
    T, D = x.shape
    gs = {}

    bias = _bias_tiles(small["rel_bias"])
    cos, sin = _rope_tables(T)
    (x1, h2), ffn1_saved = _ffn_fwd("ffn1", x, small["ffn1_norm"], lambda name, after: get_w(name, [after, bias, cos, sin]), deps,
                                    tail_ins=[small["mix_norm"]], tail_fn=lambda y, g: (y, _norm_fwd(y, g)), tail_outs=(F32, BF16))
    w_in = get_w("w_in", h2)
    nq = w_in.shape[2]
    tpq = nq // WIDTH_A

    def proj_tile(j, k):
        c = j * tpq + k
        return jnp.where(c < 3 * len(DILATIONS), (c % 3) * 3 + c // 3, c)

    proj = _mm("mix_in", (4, tpq),
               [(h2, _resident((T, D), lambda j, k: (0, 0)), w_in, _bs((None, D, WIDTH_A), lambda j, k: (j, 0, k)))],
               jax.ShapeDtypeStruct((T, IN_WIDTH), BF16), _bs((T, WIDTH_A), lambda j, k: (0, proj_tile(j, k))), NN)

    a_views = [_group_view(proj, grp, d) for grp, d in enumerate(DILATIONS)]
    a_outs, a_lses = [], []
    for grp, d in enumerate(DILATIONS):
        o, l = _dil_fwd(a_views[grp], bias[grp], d)
        a_outs.append(o)
        a_lses.append(l)
    o_a = _combine_fwd(a_outs, a_lses)

    qk_gain = jnp.concatenate([jnp.tile(small["q_norm"] * QK_SCALE_LOG2, (8, 1)), jnp.tile(small["k_norm"], (2, 1))])[:, None, :]
    qkn = _qk_fwd("b_qknorm", proj, B_Q, 10, qk_gain, cos, sin, deps=prefetch_w("w_branch_a", proj))
    qn, kn, k_col = qkn, qkn, 8
    o_b, lse_b = _gqa_fwd(qn, kn, proj, k_col)
    ahead = prefetch_w("ffn2_w1", o_b)

    wa, wb, wo = get_w("w_branch_a", o_b), get_w("w_branch_b", o_b), get_w("w_out", o_b)
    bg_a, bg_b = small["b_gate"][:, :D], small["b_gate"][:, D:]
    n_a = wa.shape[0]

    def merge_out(oa_ref, ob_ref, ga_ref, gb_ref, x1_ref, wa_ref, wb_ref, wo_ref, ba_ref, bb_ref, g2_ref, *rest):
        ta_ref, tb_ref, mg_ref, x2_ref, hn_ref = rest[-5:]
        oa = oa_ref[...]
        ta = jnp.concatenate([jnp.dot(oa, wa_ref[j], preferred_element_type=F32) for j in range(n_a)], axis=1)
        tb = jnp.dot(ob_ref[...], wb_ref[...], preferred_element_type=F32)
        sa = _sigmoid(ga_ref[...].astype(F32) + ba_ref[...])
        sb = _sigmoid(gb_ref[...].astype(F32) + bb_ref[...])
        merged = (sa * ta + sb * tb).astype(BF16)
        ta_ref[...], tb_ref[...], mg_ref[...] = ta.astype(BF16), tb.astype(BF16), merged
        y = x1_ref[...] + jnp.dot(merged, wo_ref[...], preferred_element_type=F32)
        x2_ref[...] = y
        hn_ref[...] = _norm_fwd(y, g2_ref[...]).astype(BF16)

    row = _bs((512, D), lambda i: (i, 0))
    gate_specs = [_bs((512, D), lambda i: (i, G_A // D)), _bs((512, D), lambda i: (i, G_B // D))]
    whole2, whole3 = (lambda i: (0, 0)), (lambda i: (0, 0, 0))
    vec = _bs((1, D), whole2)
    t_a, t_b, merged, x2, hn2 = pl.pallas_call(
        merge_out, out_shape=[jax.ShapeDtypeStruct((T, D), BF16)] * 3 + [jax.ShapeDtypeStruct((T, D), F32), jax.ShapeDtypeStruct((T, D), BF16)],
        grid=(T // 512,),
        in_specs=[_bs((512, WIDTH_A), lambda i: (i, 0)), row] + gate_specs + [row, _resident(wa.shape, whole3), _resident((D, D), whole2),
                                                                                _resident((D, D), whole2), vec, vec, vec]
        + _any_specs(len(ahead)),
        out_specs=[row] * 5, compiler_params=_params("parallel"), name="mix_merge_out")(
            o_a, o_b, proj, proj, x1, wa, wb, wo, bg_a, bg_b, small["ffn2_norm"], *ahead)

    def head(xv, g, tv):
        r = _rstd(xv)
        xh = xv * r
        e = xh * g - tv
        dy = e * (1.0 / D)
        dxh = dy * g
        dx = r * (dxh - xh * jnp.mean(dxh * xh, axis=-1, keepdims=True))
        return dx, 0.5 * dx, _colsum(e * e) * (0.5 / D), _colsum(dy * xh)

    (dx3, dx3_half, loss_cols, g_final), ffn2_saved = _ffn_fwd(
        "ffn2", x2, small["ffn2_norm"], get_w, h=hn2, tail_ins=[small["final_norm"].reshape(1, D), target], tail_fn=head,
        tail_outs=(F32, BF16), tail_reds=(D, D))
    gs["final_norm"] = g_final.reshape(D)

    dx2, _, dmix, gs["ffn2_norm"] = _ffn_bwd("ffn2", x2, small["ffn2_norm"], get_w, put_g, ffn2_saved, dx3, dx3_half,
                                             also_bf16=True)
    g_out = _mm_wgrad("mix_bwd_dwout", merged, dmix, a_cols=D // 4, b_cols=None, tm=256, tn=512, J=4).reshape(D, D)

    def merge_out_bwd(dx_ref, ta_ref, tb_ref, ga_ref, gb_ref, wa_ref, wb_ref, wo_ref, ba_ref, bb_ref,
                      dta_ref, dtb_ref, dga_ref, dgb_ref, doa_ref, dob_ref, dba_ref, dbb_ref):
        dm = lax.dot_general(dx_ref[...], wo_ref[...], (NT, ((), ())), preferred_element_type=F32)
        ta, tb = ta_ref[...].astype(F32), tb_ref[...].astype(F32)
        sa = _sigmoid(ga_ref[...].astype(F32) + ba_ref[...])
        sb = _sigmoid(gb_ref[...].astype(F32) + bb_ref[...])
        dga, dgb = dm * ta * sa * (1.0 - sa), dm * tb * sb * (1.0 - sb)
        dta, dtb = (dm * sa).astype(BF16), (dm * sb).astype(BF16)
        dta_ref[...], dtb_ref[...] = dta, dtb
        dga_ref[...], dgb_ref[...] = dga.astype(BF16), dgb.astype(BF16)
        w = wa_ref.shape[2]
        doa = sum(lax.dot_general(dta[:, j * w:(j + 1) * w], wa_ref[j], (NT, ((), ())), preferred_element_type=F32) for j in range(n_a))
        doa_ref[...] = doa.astype(BF16)
        dob_ref[...] = lax.dot_general(dtb, wb_ref[...], (NT, ((), ())), preferred_element_type=F32).astype(BF16)

        @pl.when(pl.program_id(0) == 0)
        def _():
            dba_ref[...] = jnp.zeros_like(dba_ref)
            dbb_ref[...] = jnp.zeros_like(dbb_ref)
        dba_ref[...] += _colsum(dga)
        dbb_ref[...] += _colsum(dgb)

    rowb = _bs((256, D), lambda i: (i, 0))
    gate_specs = [_bs((256, D), lambda i: (i, G_A // D)), _bs((256, D), lambda i: (i, G_B // D))]
    dta, dtb, dga, dgb, do_a, do_b, dba, dbb = pl.pallas_call(
        merge_out_bwd,
        out_shape=[jax.ShapeDtypeStruct((T, D), BF16)] * 4 + [jax.ShapeDtypeStruct((T, WIDTH_A), BF16), jax.ShapeDtypeStruct((T, D), BF16)]
        + [jax.ShapeDtypeStruct((1, D), F32)] * 2,
        grid=(T // 256,),
        in_specs=[rowb, rowb, rowb] + gate_specs + [_resident(wa.shape, whole3), _resident((D, D), whole2), _resident((D, D), whole2), vec, vec],
        out_specs=[rowb] * 4 + [_bs((256, WIDTH_A), lambda i: (i, 0)), rowb, vec, vec],
        compiler_params=_params("arbitrary"), name="mix_merge_out_bwd")(dmix, t_a, t_b, proj, proj, wa, wb, wo, bg_a, bg_b)
    gs["b_gate"] = jnp.concatenate([dba, dbb], axis=1)

    g_a = _mm_wgrad("mix_bwd_dwa", o_a, dta, a_cols=None, b_cols=D // 4, tm=WIDTH_A, tn=256, J=4)
    g_b = _mm_wgrad("mix_bwd_dwb", o_b, dtb, a_cols=D // 4, b_cols=None, tm=256, tn=512, J=4).reshape(D, D)
    deps = put_g({"w_out": g_out, "w_branch_a": g_a, "w_branch_b": g_b})

    dqn, dkn, dv_b = _gqa_bwd(qn, kn, proj, o_b, lse_b, do_b, deps, k_col)
    dq_b, gs["q_norm"] = _qk_bwd("b_bwd_qnorm", dqn, proj, B_Q, 8, small["q_norm"], cos, sin, in_scale=HEAD_B ** -0.5)
    dk_b, gs["k_norm"] = _qk_bwd("b_bwd_knorm", dkn, proj, B_K, 2, small["k_norm"], cos, sin, in_scale=1.0 / LOG2_E)

    do_groups, c_groups = _combine_bwd(do_a, a_outs, a_lses)
    dqs, dks, dvs, dbs = [], [], [], []
    for grp, d in enumerate(DILATIONS):
        dq, dk, dv, db = _dil_bwd(a_views[grp], bias[grp], do_groups[grp], a_lses[grp], c_groups[grp], d)
        dqs.append(dq), dks.append(dk), dvs.append(dv), dbs.append(db)
    gs["rel_bias"] = _bias_grad(jnp.stack(dbs))

    dproj = _assemble_dproj([dqs, dks, dvs], dq_b, dk_b, dv_b, dga, dgb)
    nq = w_in.shape[2]
    g_in = _mm("mix_bwd_dwin", (4, tpq),
               [(h2, _resident((T, D), lambda j, k: (0, 0)), dproj, _bs((T, WIDTH_A), lambda j, k: (0, j * tpq + k)))],
               jax.ShapeDtypeStruct((4, D, nq), BF16), _bs((None, D, WIDTH_A), lambda j, k: (j, 0, k)), TN)
    deps = put_g({"w_in": g_in})
    dx1, dx1_half, gs["mix_norm"] = _dh_norm_bwd(
        "mix_bwd_dh", 256,
        [(dproj, _bs((256, nq), lambda i, j=j: (i, j)), w_in, _resident((None, D, nq), lambda i, j=j: (j, 0, 0))) for j in range(4)],
        NT, x1, small["mix_norm"], dx2, deps, rider=take_rider(T // 256, deps))

    dx0, _, gs["ffn1_norm"] = _ffn_bwd("ffn1", x, small["ffn1_norm"], get_w, put_g, ffn1_saved, dx1, dx1_half, last=True,
                                       take_rider=take_rider)
    return loss_cols, dx0, gs


def _position():
    return lax.axis_index("x"), lax.axis_index("y"), lax.axis_index("c")


def _any_specs(n):
    return [pl.BlockSpec(memory_space=pl.ANY)] * n


HBM_SPEC = pl.BlockSpec(memory_space=pltpu.HBM)
SEM_SPEC = pl.BlockSpec(memory_space=pltpu.SEMAPHORE)
DATAFLOW_EFFECT = pltpu.SideEffectType.DATAFLOW_SIDE_EFFECTING
N_PEER_CHIPS = 3
LANES = 128


def _quarter_copies(srcs, lands, send_sems, recv_sems, mode):
    x, y, c = _position()
    me = 2 * x + y
    peers = [(1 - x, y, c), (x, 1 - y, c), (1 - x, 1 - y, c)]
    copies = []
    for src, land, send, recv in zip(srcs, lands, send_sems, recv_sems):
        if mode == "sibling":
            copies.append(pltpu.make_async_remote_copy(src_ref=src, dst_ref=land, send_sem=send.at[0], recv_sem=recv.at[0],
                                                       device_id=(x, y, 1 - c), device_id_type=MESH))
            continue
        if mode == "fill":
            half = land.shape[1] // 2
            for p, (px, py, _) in enumerate(peers):
                part = land.at[2 * px + py, pl.ds(c * half, half)]
                copies.append(pltpu.make_async_remote_copy(src_ref=part, dst_ref=part, send_sem=send.at[p], recv_sem=recv.at[p],
                                                           device_id=(x, y, 1 - c), device_id_type=MESH))
            continue
        scatter = mode == "scatter"
        half = land.shape[1] // 2
        mine = land.at[me, pl.ds(c * half, half)]
        for p, (px, py, pc) in enumerate(peers):
            copies.append(pltpu.make_async_remote_copy(
                src_ref=src.at[2 * px + py] if scatter else mine, dst_ref=land.at[me] if scatter else mine,
                send_sem=send.at[p], recv_sem=recv.at[p], device_id=(px, py, pc), device_id_type=MESH))
    return copies


def _fill_from_sibling(name, stacks):
    n = len(stacks)

    def body(*refs):
        outs = refs[n:2 * n]
        send_sems, recv_sems = refs[2 * n:]
        x, y, c = _position()
        copies = []
        for i, ref in enumerate(outs):
            half = ref.shape[1] // 2
            rows = pl.ds(c * half, half)
            for p, k in enumerate((2 * (1 - x) + y, 2 * x + (1 - y), 2 * (1 - x) + (1 - y))):
                cp = pltpu.make_async_remote_copy(ref.at[k, rows], ref.at[k, rows], send_sems.at[3 * i + p], recv_sems.at[3 * i + p],
                                                  device_id=(x, y, 1 - c), device_id_type=MESH)
                cp.start()
                copies.append(cp)
        for cp in copies:
            cp.wait()

    return pl.pallas_call(
        body, out_shape=[jax.ShapeDtypeStruct(s.shape, s.dtype) for s in stacks],
        in_specs=_any_specs(n), out_specs=_any_specs(n), input_output_aliases={i: i for i in range(n)},
        scratch_shapes=[pltpu.SemaphoreType.DMA((N_PEER_CHIPS * n,)), pltpu.SemaphoreType.DMA((N_PEER_CHIPS * n,))],
        compiler_params=pltpu.CompilerParams(has_side_effects=True), name=name)(*stacks)


def _exchange_start(name, srcs, lands, mode):
    n = len(lands)
    arrays = list(lands) if srcs is None else list(srcs) + list(lands)
    k = len(arrays)

    def body(*refs):
        land_refs = refs[k - n:k]
        send_sems, recv_sems = refs[k:k + n], refs[k + n:k + 2 * n]
        token = refs[2 * k + 2 * n]
        for cp in _quarter_copies(refs[:n], land_refs, send_sems, recv_sems, mode):
            cp.start()
        token[...] = jnp.zeros_like(token)

    sem = pltpu.SemaphoreType.DMA((N_PEER_CHIPS,))
    out_shape = [sem] * (2 * n) + [pltpu.HBM(a.shape, a.dtype) for a in arrays] + [jax.ShapeDtypeStruct((8, LANES), F32)]
    res = pl.pallas_call(
        body, name=name, out_shape=out_shape, in_specs=[HBM_SPEC] * k,
        out_specs=[SEM_SPEC] * (2 * n) + [HBM_SPEC] * k + [pl.BlockSpec(memory_space=pltpu.VMEM)],
        input_output_aliases={i: 2 * n + i for i in range(k)},
        compiler_params=pltpu.CompilerParams(has_side_effects=DATAFLOW_EFFECT),
    )(*[pltpu.with_memory_space_constraint(a, pltpu.HBM) for a in arrays])
    thru = res[2 * n:2 * n + k]
    return res[:n], res[n:2 * n], (None if srcs is None else thru[:n]), thru[k - n:], res[2 * n + k]


def _exchange_wait(name, srcs, lands, send_sems, recv_sems, after, mode):
    n = len(lands)
    arrays = list(lands) if srcs is None else list(srcs) + list(lands)
    k = len(arrays)
    after = list(after) if isinstance(after, (list, tuple)) else [after]

    def body(*refs):
        sends, recvs = refs[k:k + n], refs[k + n:k + 2 * n]
        for cp in _quarter_copies(refs[:n], refs[k - n:k], sends, recvs, mode):
            cp.wait_send()
            cp.wait_recv()

    res = pl.pallas_call(
        body, name=name, out_shape=[pltpu.HBM(a.shape, a.dtype) for a in arrays],
        in_specs=[HBM_SPEC] * k + [SEM_SPEC] * (2 * n) + _any_specs(len(after)),
        out_specs=[HBM_SPEC] * k, input_output_aliases={i: i for i in range(k)},
        compiler_params=pltpu.CompilerParams(has_side_effects=DATAFLOW_EFFECT),
    )(*arrays, *send_sems, *recv_sems, *after)
    return (None if srcs is None else res[:n]), res[k - n:]


def _scatter_and_forward(name, stacks, lands, old_srcs, old_lands, old_sends, old_recvs):
    n1, n0 = len(stacks), len(old_lands)
    sibling_lands = [lax.empty(a.shape, a.dtype) for a in old_lands]
    arrays = list(stacks) + list(lands) + list(old_srcs) + list(old_lands) + sibling_lands
    k, s = len(arrays), 2 * n1 + 2 * n0

    def body(*refs):
        new_srcs, new_lands = refs[:n1], refs[n1:2 * n1]
        was_srcs, landed, to_sibling = refs[2 * n1:2 * n1 + n0], refs[2 * n1 + n0:2 * n1 + 2 * n0], refs[2 * n1 + 2 * n0:k]
        was_sends, was_recvs = refs[k:k + n0], refs[k + n0:k + 2 * n0]
        sems = refs[k + 2 * n0:k + 2 * n0 + s]
        token = refs[k + 2 * n0 + s + k]
        for cp in _quarter_copies(new_srcs, new_lands, sems[:n1], sems[n1:2 * n1], "scatter"):
            cp.start()
        for cp in _quarter_copies(was_srcs, landed, was_sends, was_recvs, "scatter"):
            cp.wait_send()
            cp.wait_recv()
        for cp in _quarter_copies(landed, to_sibling, sems[2 * n1:2 * n1 + n0], sems[2 * n1 + n0:], "sibling"):
            cp.start()
        token[...] = jnp.zeros_like(token)

    sem = pltpu.SemaphoreType.DMA((N_PEER_CHIPS,))
    res = pl.pallas_call(
        body, name=name,
        out_shape=[sem] * s + [pltpu.HBM(a.shape, a.dtype) for a in arrays] + [jax.ShapeDtypeStruct((8, LANES), F32)],
        in_specs=[HBM_SPEC] * k + [SEM_SPEC] * (2 * n0),
        out_specs=[SEM_SPEC] * s + [HBM_SPEC] * k + [pl.BlockSpec(memory_space=pltpu.VMEM)],
        input_output_aliases={i: s + i for i in range(k)},
        compiler_params=pltpu.CompilerParams(has_side_effects=DATAFLOW_EFFECT),
    )(*[pltpu.with_memory_space_constraint(a, pltpu.HBM) for a in arrays], *old_sends, *old_recvs)
    thru = res[s:s + k]
    scatter = (res[:n1], res[n1:2 * n1], thru[:n1], thru[n1:2 * n1])
    sibling = (res[2 * n1:2 * n1 + n0], res[2 * n1 + n0:s], thru[2 * n1 + n0:2 * n1 + 2 * n0], thru[2 * n1 + 2 * n0:])
    return scatter, sibling, res[s + k]


def _own_slots(name, srcs, from_stack=False):
    n = len(srcs)
    me = (2 * lax.axis_index("x") + lax.axis_index("y")).astype(jnp.int32).reshape(1)

    def body(me_ref, *refs):
        for x_ref, o_ref in zip(refs[:n], refs[n:]):
            o_ref[...] = x_ref[...].astype(o_ref.dtype)

    in_specs, out_specs, out_shape = [], [], []
    for src in srcs:
        R, C = src.shape[-2:]
        in_specs.append(pl.BlockSpec((None, R // 2, C), lambda i, me_ref: (me_ref[0], i, 0)) if from_stack
                        else pl.BlockSpec((R // 2, C), lambda i, me_ref: (i, 0)))
        out_specs.append(pl.BlockSpec((None, R // 2, C), lambda i, me_ref: (me_ref[0], i, 0)))
        out_shape.append(jax.ShapeDtypeStruct((4, R, C), BF16))
    grid_spec = pltpu.PrefetchScalarGridSpec(num_scalar_prefetch=1, grid=(2,), in_specs=in_specs, out_specs=out_specs)
    return pl.pallas_call(body, out_shape=out_shape, grid_spec=grid_spec, compiler_params=_params("parallel"), name=name)(me, *srcs)


def _allreduce_small(buf):
    R, C = buf.shape
    flips = [(fx, fy, fc) for fx in (0, 1) for fy in (0, 1) for fc in (0, 1)][1:]

    def body(in_ref, out_ref, land_ref, send_sems, recv_sems):
        x, y, c = _position()
        me = 4 * x + 2 * y + c
        copies = []
        for k, (fx, fy, fc) in enumerate(flips):
            px, py, pc = (1 - x if fx else x), (1 - y if fy else y), (1 - c if fc else c)
            cp = pltpu.make_async_remote_copy(in_ref, land_ref.at[me], send_sems.at[k], recv_sems.at[k],
                                              device_id=(px, py, pc), device_id_type=MESH)
            cp.start()
            copies.append(cp)
        land_ref[me] = in_ref[...]
        for cp in copies:
            cp.wait()
        acc = land_ref[0]
        for k in range(1, 8):
            acc = acc + land_ref[k]
        out_ref[...] = acc

    return pl.pallas_call(
        body, out_shape=jax.ShapeDtypeStruct((R, C), F32),
        in_specs=[pl.BlockSpec(memory_space=pltpu.VMEM)], out_specs=pl.BlockSpec(memory_space=pltpu.VMEM),
        scratch_shapes=[pltpu.VMEM((8, R, C), F32), pltpu.SemaphoreType.DMA((7,)), pltpu.SemaphoreType.DMA((7,))],
        compiler_params=pltpu.CompilerParams(has_side_effects=True), name="allreduce_small")(buf)


def _adamw_math(w, g, m, v):
    m2 = ADAM_B1 * m + (1.0 - ADAM_B1) * g
    v2 = ADAM_B2 * v + (1.0 - ADAM_B2) * (g * g)
    m_hat = m2 / (1.0 - ADAM_B1 ** ADAM_STEP)
    v_hat = v2 / (1.0 - ADAM_B2 ** ADAM_STEP)
    delta = -ADAM_LR * (m_hat / (jnp.sqrt(v_hat) + ADAM_EPS) + ADAM_WD * w)
    return delta, m2, v2


def _adamw_from_partials(wv, mv, vv, *parts):
    def four(a, b, c, d):
        return ((a.astype(F32) + b.astype(F32)) + c.astype(F32)) + d.astype(F32)

    g = four(*parts[:4]) + four(*parts[4:])
    return (g,) + _adamw_math(wv, g, mv, vv)


def _adamw_big(name, w, m, v, mine, theirs):
    R, C = w.shape
    rows = 256 if R % 256 == 0 else R // 2
    nrb = R // rows
    slots = [_tiled(s.reshape(4 * R, C), None, 0, k * nrb) for s in (mine, theirs) for k in range(4)]
    return _ew(name, _adamw_from_partials, [_tiled(w), _tiled(m), _tiled(v)] + slots, [(F32, C)] * 4, n_rows=R, rows=rows)


def _adamw_rider(w, m, v, mine, theirs, steps, deliver):
    R, C = w.shape
    fits = [nb for nb in range(1, steps + 1) if R % nb == 0 and (R // nb) % 16 == 0]
    if not fits:
        return None
    nb = fits[-1]
    rows = R // nb

    def blocks(first):
        return pl.BlockSpec((rows, C), lambda *g: (first + jnp.minimum(g[0], nb - 1), 0))

    flat = [s.reshape(4 * R, C) for s in (mine, theirs)]
    return dict(operands=[w, m, v] + [f for f in flat for _ in range(4)],
                in_specs=[blocks(0)] * 3 + [blocks(k * nb) for _ in flat for k in range(4)],
                out_shape=[jax.ShapeDtypeStruct((R, C), F32)] * 4, out_specs=[blocks(0)] * 4,
                n_blocks=nb, fn=_adamw_from_partials, deliver=lambda outs: deliver(*outs))


BIG = ("ffn1_w1", "ffn1_w3", "ffn1_w2", "w_in", "w_branch_a", "w_branch_b", "w_out", "ffn2_w1", "ffn2_w3", "ffn2_w2")
SMALL = ("ffn1_norm", "mix_norm", "b_gate", "q_norm", "k_norm", "rel_bias", "ffn2_norm", "final_norm")
ORDER = ("ffn1_norm", "ffn1_w1", "ffn1_w3", "ffn1_w2", "mix_norm", "w_in", "b_gate", "q_norm", "k_norm", "rel_bias",
         "w_branch_a", "w_branch_b", "w_out", "ffn2_norm", "ffn2_w1", "ffn2_w3", "ffn2_w2", "final_norm")
TRANSPOSED = ("ffn1_w1", "ffn1_w3", "ffn2_w1", "ffn2_w3")
SIBLING_LAG = 2
EARLY_FORWARDS = 2
LONG_HOST_STEPS = 8
GATHER_GROUPS = (("ffn1_w1", "ffn1_w3"), ("ffn1_w2",), ("w_in",), ("w_branch_a", "w_branch_b", "w_out"),
                 ("ffn2_w1", "ffn2_w3", "ffn2_w2"))


def _pack_small(d):
    rows = []
    for n in SMALL:
        flat = d[n].reshape(-1)
        pad = (-flat.shape[0]) % LANES
        rows.append(jnp.pad(flat, (0, pad)).reshape(-1, LANES))
    buf = jnp.concatenate(rows, axis=0)
    return jnp.pad(buf, ((0, (-buf.shape[0]) % 8), (0, 0)))


def _unpack_small(buf, like):
    out, r = {}, 0
    for n in SMALL:
        size = like[n].size
        nr = -(-size // LANES)
        out[n] = buf[r:r + nr].reshape(-1)[:size].reshape(like[n].shape)
        r += nr
    return out


def kernel(x, ffn1_norm, ffn1_w1, ffn1_w3, ffn1_w2, mix_norm, w_in, b_gate, q_norm, k_norm, rel_bias, w_branch_a, w_branch_b, w_out, ffn2_norm, ffn2_w1, ffn2_w3, ffn2_w2, final_norm, loss_target, m_ffn1_norm, m_ffn1_w1, m_ffn1_w3, m_ffn1_w2, m_mix_norm, m_w_in, m_b_gate, m_q_norm, m_k_norm, m_rel_bias, m_w_branch_a, m_w_branch_b, m_w_out, m_ffn2_norm, m_ffn2_w1, m_ffn2_w3, m_ffn2_w2, m_final_norm, v_ffn1_norm, v_ffn1_w1, v_ffn1_w3, v_ffn1_w2, v_mix_norm, v_w_in, v_b_gate, v_q_norm, v_k_norm, v_rel_bias, v_w_branch_a, v_w_branch_b, v_w_out, v_ffn2_norm, v_ffn2_w1, v_ffn2_w3, v_ffn2_w2, v_final_norm):
    given = dict(locals())
    w = {n: given[n] for n in ORDER}
    m = {n: given["m_" + n] for n in ORDER}
    v = {n: given["v_" + n] for n in ORDER}
    T, D = x.shape[1], x.shape[2]

    def stored(a, n):
        a = a.reshape(a.shape[1:])
        return a.T if n in TRANSPOSED else a

    def returned(a, n):
        return (a.T if n in TRANSPOSED else a).reshape(w[n].shape)

    quarter = {n: stored(w[n], n) for n in BIG}
    send, recv, _, land_thru, token = _exchange_start(
        "gather_start", None, _own_slots("own_weights", [quarter[n] for n in BIG]), "gather")
    index = {n: i for i, n in enumerate(BIG)}
    ready, filling = {}, {}

    def landed_halves(group, after):
        ids = [index[n] for n in group]
        return _exchange_wait("gather_wait_" + group[0], None, [land_thru[i] for i in ids],
                              [send[i] for i in ids], [recv[i] for i in ids], after, "gather")[1]

    def prefetch_w(name, after):
        group = next(g for g in GATHER_GROUPS if name in g)
        started = _exchange_start("fill_start_" + group[0], None, landed_halves(group, after), "fill")
        filling[group] = started
        return [started[4]]

    def get_w(name, after):
        if name not in ready:
            group = next(g for g in GATHER_GROUPS if name in g)
            if group in filling:
                f_send, f_recv, _, thru, _ = filling[group]
                stacks = _exchange_wait("fill_wait_" + group[0], None, thru, f_send, f_recv, after, "fill")[1]
            else:
                stacks = _fill_from_sibling("gather_fill_" + group[0], landed_halves(group, after))
            for n, st in zip(group, stacks):
                ready[n] = st.reshape(D, D) if n in ("w_branch_b", "w_out") else st
        return ready[name]

    scattered, forwarded = [], []

    def forward_oldest(after):
        names, s_sem, r_sem, srcs, lands = scattered.pop(0)
        _, landed = _exchange_wait("scatter_wait_" + names[0], srcs, lands, s_sem, r_sem, after, "scatter")
        started = _exchange_start("sibling_start_" + names[0], landed, [lax.empty(a.shape, a.dtype) for a in landed], "sibling")
        forwarded.append((names,) + tuple(started[:4]))
        return started[4]

    def put_g(grads):
        names = list(grads)
        stacks = [grads[n].reshape((4,) + quarter[n].shape) for n in names]
        lands = _own_slots("own_grad_" + names[0], stacks, from_stack=True)
        if len(scattered) < (1 if len(forwarded) < EARLY_FORWARDS else SIBLING_LAG):
            started = _exchange_start("scatter_start_" + names[0], stacks, lands, "scatter")
            scattered.append((names,) + tuple(started[:4]))
            return [started[4]]
        old_names, s_sem, r_sem, old_srcs, old_lands = scattered.pop(0)
        scatter, sibling, token = _scatter_and_forward("scatter_start_" + names[0], stacks, lands, old_srcs, old_lands, s_sem, r_sem)
        scattered.append((names,) + scatter)
        forwarded.append((old_names,) + sibling)
        return [token]

    grads, deltas, new_m, new_v = {}, {}, {}, {}
    arrived, riding = {}, set()

    def partials(gi, after):
        if gi not in arrived:
            names, s_sem, r_sem, srcs, lands = forwarded[gi]
            arrived[gi] = _exchange_wait("sibling_wait_" + names[0], srcs, lands, s_sem, r_sem, after, "sibling")
        return arrived[gi]

    def deliver_to(n):
        def deliver(*res):
            grads[n], deltas[n], new_m[n], new_v[n] = [returned(r, n) for r in res]
        return deliver

    def take_rider(steps, after, cap=None):
        if cap is None:
            cap = None if steps >= LONG_HOST_STEPS else quarter["ffn1_w2"].size
        waiting = [(quarter[n].size, gi, k, n) for gi, entry in enumerate(forwarded) for k, n in enumerate(entry[0])
                   if n not in riding and (cap is None or quarter[n].size <= cap)]
        for _, gi, k, n in sorted(waiting, reverse=True):
            mine, theirs = partials(gi, after)
            rider = _adamw_rider(quarter[n], stored(m[n], n), stored(v[n], n), mine[k], theirs[k], steps, deliver_to(n))
            if rider is not None:
                riding.add(n)
                return rider
        return None

    small = {n: w[n] for n in SMALL}
    packed = [_pack_small({n: d[n] for n in SMALL}) for d in (w, m, v)]
    loss_cols, grad_x, gs = _local_step(x.reshape(T, D), loss_target.reshape(T, D), small, get_w, put_g, deps=[token] + packed,
                                        prefetch_w=prefetch_w, take_rider=take_rider)

    after = grad_x
    while scattered:
        after = forward_oldest(after)
    for gi, entry in enumerate(forwarded):
        mine, theirs = partials(gi, after)
        for n, a, b in zip(entry[0], mine, theirs):
            if n not in riding:
                deliver_to(n)(*_adamw_big(f"adamw_{n}", quarter[n], stored(m[n], n), stored(v[n], n), a, b))

    gs = {n: gs[n].reshape(w[n].shape) for n in SMALL}
    packed_g = _pack_small(gs)
    n_small = packed_g.shape[0]
    summed = _allreduce_small(jnp.concatenate([packed_g, loss_cols.reshape(-1, LANES)], axis=0))
    g_small, loss = summed[:n_small], jnp.sum(summed[n_small:])
    R = g_small.shape[0]
    res = _ew("adamw_small", lambda wv, mv, vv, g: (g,) + _adamw_math(wv, g, mv, vv),
              [_tiled(packed[0]), _tiled(packed[1]), _tiled(packed[2]), _tiled(g_small)], [(F32, LANES)] * 4, n_rows=R, rows=R)
    for d, buf in zip((grads, deltas, new_m, new_v), res):
        d.update(_unpack_small(buf, w))

    return (loss, grad_x.reshape(x.shape), *[grads[n] for n in ORDER], *[deltas[n] for n in ORDER],
            *[new_m[n] for n in ORDER], *[new_v[n] for n in ORDER])
```
